```python
import jax, jax.numpy as jnp
from jax import lax
import numpy as np

D_MODEL = 2048
BATCH = 8
SEQ = 2048
DEPTH = 1

HEAD_DIM = 128
N_HEADS_TOTAL = D_MODEL // HEAD_DIM
N_FOX_HEADS = N_HEADS_TOTAL // 2
N_SWA_HEADS = N_HEADS_TOTAL - N_FOX_HEADS
N_SWA_KV_HEADS = max(1, N_SWA_HEADS // 4)
SWA_WINDOW = 128
Q_BLOCK = 128
D_FF = 4 * D_MODEL
ROPE_THETA = 10000.0
NORM_EPS = 1e-6
FOX_W = N_FOX_HEADS * HEAD_DIM
SWA_Q_W = N_SWA_HEADS * HEAD_DIM
SWA_KV_W = N_SWA_KV_HEADS * HEAD_DIM
MIX_W = FOX_W + SWA_Q_W
IN_SPLITS = [FOX_W, FOX_W, FOX_W, N_FOX_HEADS, SWA_Q_W, SWA_KV_W, SWA_KV_W]
IN_PROJ_W = sum(IN_SPLITS)
N_MOD = 6

kernel_name = "hymba_fox_swa_sink_hybrid"


def rmsnorm(x, g):
    xf = x.astype(jnp.float32)
    y = xf * lax.rsqrt(jnp.mean(xf * xf, axis=-1, keepdims=True) + NORM_EPS)
    return (y * g.astype(jnp.float32)).astype(x.dtype)


def rope(x, pos):
    d = x.shape[-1]
    half = d // 2
    inv_freq = 1.0 / (ROPE_THETA ** (jnp.arange(half, dtype=jnp.float32) * (2.0 / d)))
    ang = pos.astype(jnp.float32)[:, None] * inv_freq[None, :]
    cos = jnp.cos(ang)[None, :, None, :]
    sin = jnp.sin(ang)[None, :, None, :]
    xf = x.astype(jnp.float32)
    x1, x2 = xf[..., :half], xf[..., half:]
    out = jnp.concatenate([x1 * cos - x2 * sin, x2 * cos + x1 * sin], axis=-1)
    return out.astype(x.dtype)


def forgetting_attention(q, k, v, log_f):
    B, S, H, d = q.shape
    cum = jnp.cumsum(log_f, axis=1).transpose(0, 2, 1)
    scale = d ** -0.5
    tri = jnp.tril(jnp.ones((Q_BLOCK, Q_BLOCK), dtype=bool))
    outs = []
    for i in range(S // Q_BLOCK):
        q0 = i * Q_BLOCK
        end = q0 + Q_BLOCK
        s = jnp.einsum('bqhd,bkhd->bhqk', q[:, q0:end], k[:, :end],
                       preferred_element_type=jnp.float32) * scale
        s = s + cum[:, :, q0:end, None] - cum[:, :, None, :end]
        mask = jnp.concatenate([jnp.ones((Q_BLOCK, q0), dtype=bool), tri], axis=1)
        s = jnp.where(mask[None, None], s, -jnp.inf)
        p = jax.nn.softmax(s, axis=-1)
        outs.append(jnp.einsum('bhqk,bkhd->bqhd', p.astype(v.dtype), v[:, :end]))
    return jnp.concatenate(outs, axis=1)


def sliding_window_sink_attention(q, k, v, sinks):
    B, S, H, d = q.shape
    KVH = k.shape[2]
    G = H // KVH
    nb = S // Q_BLOCK
    scale = d ** -0.5
    pad = ((0, 0), (Q_BLOCK, 0), (0, 0), (0, 0))
    kp = jnp.pad(k, pad).reshape(B, nb + 1, Q_BLOCK, KVH, d)
    vp = jnp.pad(v, pad).reshape(B, nb + 1, Q_BLOCK, KVH, d)
    kb = jnp.concatenate([kp[:, :-1], kp[:, 1:]], axis=2)
    vb = jnp.concatenate([vp[:, :-1], vp[:, 1:]], axis=2)
    qb = q.reshape(B, nb, Q_BLOCK, KVH, G, d)
    s = jnp.einsum('bnqkgd,bnjkd->bnkgqj', qb, kb,
                   preferred_element_type=jnp.float32) * scale
    qi = jnp.arange(Q_BLOCK)[:, None]
    kj = jnp.arange(2 * Q_BLOCK)[None, :]
    diff = qi + Q_BLOCK - kj
    band = (diff >= 0) & (diff < SWA_WINDOW)
    key_idx = jnp.arange(nb)[:, None] * Q_BLOCK + jnp.arange(2 * Q_BLOCK)[None, :] - Q_BLOCK
    valid = key_idx >= 0
    mask = band[None, :, :] & valid[:, None, :]
    s = jnp.where(mask[None, :, None, None], s, -jnp.inf)
    sink = jnp.broadcast_to(sinks.astype(jnp.float32).reshape(KVH, G)[None, None, :, :, None, None],
                            s.shape[:-1] + (1,))
    p = jax.nn.softmax(jnp.concatenate([s, sink], axis=-1), axis=-1)[..., :-1]
    o = jnp.einsum('bnkgqj,bnjkd->bnqkgd', p.astype(v.dtype), vb)
    return o.reshape(B, S, H, d)


def _fwd_setup_inputs(seed: int = 0) -> dict:
    key = jax.random.key(seed)
    ks = jax.random.split(key, 16)
    f32 = jnp.float32
    D = D_MODEL
    def nrm(k, shape, s):
        return jax.random.normal(k, shape, f32) * s
    return {
        "x": nrm(ks[0], (BATCH, SEQ, D), 1.0),
        "c": nrm(ks[1], (BATCH, D), 1.0),
        "w_mod": nrm(ks[2], (DEPTH, D, N_MOD * D), D ** -0.5),
        "b_mod": nrm(ks[3], (DEPTH, N_MOD * D), 0.02),
        "g_pre_mix": 1.0 + nrm(ks[4], (DEPTH, D), 0.02),
        "g_post_mix": 1.0 + nrm(ks[5], (DEPTH, D), 0.02),
        "w_in": nrm(ks[6], (DEPTH, D, IN_PROJ_W), D ** -0.5),
        "b_forget": jax.random.uniform(ks[7], (DEPTH, N_FOX_HEADS), f32, 1.0, 5.0),
        "swa_sinks": nrm(ks[8], (DEPTH, N_SWA_HEADS), 0.5),
        "w_out": nrm(ks[9], (DEPTH, MIX_W, D), MIX_W ** -0.5),
        "g_pre_mlp": 1.0 + nrm(ks[10], (DEPTH, D), 0.02),
        "g_post_mlp": 1.0 + nrm(ks[11], (DEPTH, D), 0.02),
        "w_up": nrm(ks[12], (DEPTH, D, D_FF), D ** -0.5),
        "w_down": nrm(ks[13], (DEPTH, D_FF, D), D_FF ** -0.5),
    }


def _fwd_reference(x, c, w_mod, b_mod, g_pre_mix, g_post_mix, w_in, b_forget, swa_sinks,
              w_out, g_pre_mlp, g_post_mlp, w_up, w_down):
    B, S, D = x.shape
    pos = jnp.arange(S)
    split_idx = np.cumsum(IN_SPLITS)[:-1].tolist()
    cond = jax.nn.silu(c)
    for l in range(DEPTH):
        mod = cond @ w_mod[l] + b_mod[l]
        sh_a, sc_a, gt_a, sh_m, sc_m, gt_m = [m[:, None, :] for m in jnp.split(mod, N_MOD, axis=-1)]

        h = rmsnorm(x, g_pre_mix[l]) * (1.0 + sc_a) + sh_a
        proj = h @ w_in[l]
        fq, fk, fv, fg, sq, sk, sv = jnp.split(proj, split_idx, axis=-1)

        log_f = jax.nn.log_sigmoid(fg.astype(jnp.float32) + b_forget[l].astype(jnp.float32))
        fox = forgetting_attention(fq.reshape(B, S, N_FOX_HEADS, HEAD_DIM),
                                   fk.reshape(B, S, N_FOX_HEADS, HEAD_DIM),
                                   fv.reshape(B, S, N_FOX_HEADS, HEAD_DIM), log_f)

        sq = rope(sq.reshape(B, S, N_SWA_HEADS, HEAD_DIM), pos)
        sk = rope(sk.reshape(B, S, N_SWA_KV_HEADS, HEAD_DIM), pos)
        sv = sv.reshape(B, S, N_SWA_KV_HEADS, HEAD_DIM)
        swa = sliding_window_sink_attention(sq, sk, sv, swa_sinks[l])

        mix = jnp.concatenate([fox.reshape(B, S, FOX_W), swa.reshape(B, S, SWA_Q_W)], axis=-1) @ w_out[l]
        x = x + gt_a * rmsnorm(mix, g_post_mix[l])

        h = rmsnorm(x, g_pre_mlp[l]) * (1.0 + sc_m) + sh_m
        y = jnp.square(jax.nn.relu(h @ w_up[l])) @ w_down[l]
        x = x + gt_m * rmsnorm(y, g_post_mlp[l])
    return x


import jax as _jax
import jax.numpy as _jnp

TWIN_FORMAT = 'train_step'
FWD_PARAMS = ['x', 'c', 'w_mod', 'b_mod', 'g_pre_mix', 'g_post_mix', 'w_in', 'b_forget', 'swa_sinks', 'w_out', 'g_pre_mlp', 'g_post_mlp', 'w_up', 'w_down']
TWIN_WEIGHTS = ['w_mod', 'b_mod', 'g_pre_mix', 'g_post_mix', 'w_in', 'b_forget', 'swa_sinks', 'w_out', 'g_pre_mlp', 'g_post_mlp', 'w_up', 'w_down']
TWIN_DIFF_INPUT = 'x'
TWIN_INPUTS = ['x', 'c', 'w_mod', 'b_mod', 'g_pre_mix', 'g_post_mix', 'w_in', 'b_forget', 'swa_sinks', 'w_out', 'g_pre_mlp', 'g_post_mlp', 'w_up', 'w_down', 'loss_target', 'm_w_mod', 'm_b_mod', 'm_g_pre_mix', 'm_g_post_mix', 'm_w_in', 'm_b_forget', 'm_swa_sinks', 'm_w_out', 'm_g_pre_mlp', 'm_g_post_mlp', 'm_w_up', 'm_w_down', 'v_w_mod', 'v_b_mod', 'v_g_pre_mix', 'v_g_post_mix', 'v_w_in', 'v_b_forget', 'v_swa_sinks', 'v_w_out', 'v_g_pre_mlp', 'v_g_post_mlp', 'v_w_up', 'v_w_down']
TWIN_OUTPUTS = ['loss', 'grad_x', 'grad_w_mod', 'grad_b_mod', 'grad_g_pre_mix', 'grad_g_post_mix', 'grad_w_in', 'grad_b_forget', 'grad_swa_sinks', 'grad_w_out', 'grad_g_pre_mlp', 'grad_g_post_mlp', 'grad_w_up', 'grad_w_down', 'delta_w_mod', 'delta_b_mod', 'delta_g_pre_mix', 'delta_g_post_mix', 'delta_w_in', 'delta_b_forget', 'delta_swa_sinks', 'delta_w_out', 'delta_g_pre_mlp', 'delta_g_post_mlp', 'delta_w_up', 'delta_w_down', 'new_m_w_mod', 'new_m_b_mod', 'new_m_g_pre_mix', 'new_m_g_post_mix', 'new_m_w_in', 'new_m_b_forget', 'new_m_swa_sinks', 'new_m_w_out', 'new_m_g_pre_mlp', 'new_m_g_post_mlp', 'new_m_w_up', 'new_m_w_down', 'new_v_w_mod', 'new_v_b_mod', 'new_v_g_pre_mix', 'new_v_g_post_mix', 'new_v_w_in', 'new_v_b_forget', 'new_v_swa_sinks', 'new_v_w_out', 'new_v_g_pre_mlp', 'new_v_g_post_mlp', 'new_v_w_up', 'new_v_w_down']
TWIN_LEAF_KINDS = {'loss': 'loss', 'grad_x': 'grad_x', 'grad_w_mod': 'grad_w', 'grad_b_mod': 'grad_w', 'grad_g_pre_mix': 'grad_w', 'grad_g_post_mix': 'grad_w', 'grad_w_in': 'grad_w', 'grad_b_forget': 'grad_w', 'grad_swa_sinks': 'grad_w', 'grad_w_out': 'grad_w', 'grad_g_pre_mlp': 'grad_w', 'grad_g_post_mlp': 'grad_w', 'grad_w_up': 'grad_w', 'grad_w_down': 'grad_w', 'delta_w_mod': 'delta_w', 'delta_b_mod': 'delta_w', 'delta_g_pre_mix': 'delta_w', 'delta_g_post_mix': 'delta_w', 'delta_w_in': 'delta_w', 'delta_b_forget': 'delta_w', 'delta_swa_sinks': 'delta_w', 'delta_w_out': 'delta_w', 'delta_g_pre_mlp': 'delta_w', 'delta_g_post_mlp': 'delta_w', 'delta_w_up': 'delta_w', 'delta_w_down': 'delta_w', 'new_m_w_mod': 'new_m', 'new_m_b_mod': 'new_m', 'new_m_g_pre_mix': 'new_m', 'new_m_g_post_mix': 'new_m', 'new_m_w_in': 'new_m', 'new_m_b_forget': 'new_m', 'new_m_swa_sinks': 'new_m', 'new_m_w_out': 'new_m', 'new_m_g_pre_mlp': 'new_m', 'new_m_g_post_mlp': 'new_m', 'new_m_w_up': 'new_m', 'new_m_w_down': 'new_m', 'new_v_w_mod': 'new_v', 'new_v_b_mod': 'new_v', 'new_v_g_pre_mix': 'new_v', 'new_v_g_post_mix': 'new_v', 'new_v_w_in': 'new_v', 'new_v_b_forget': 'new_v', 'new_v_swa_sinks': 'new_v', 'new_v_w_out': 'new_v', 'new_v_g_pre_mlp': 'new_v', 'new_v_g_post_mlp': 'new_v', 'new_v_w_up': 'new_v', 'new_v_w_down': 'new_v'}


def _forward(args):
    return _fwd_reference(*[args[k] for k in FWD_PARAMS])


def _output_shape():
    out = _jax.eval_shape(lambda: _forward(_fwd_setup_inputs(0)))
    return out.shape, out.dtype

N_MICROBATCH = 1
ADAM_LR = 0.001
ADAM_B1 = 0.9
ADAM_B2 = 0.999
ADAM_EPS = 1e-08
ADAM_WD = 0.01
ADAM_STEP = 10
PER_EXAMPLE_BATCH_AXIS = {'x': 0, 'c': 0, 'loss_target': 0}
SHARED_INPUTS = []
_WEIGHT_DTYPES = {'w_mod': _jnp.float32, 'b_mod': _jnp.float32, 'g_pre_mix': _jnp.float32, 'g_post_mix': _jnp.float32, 'w_in': _jnp.float32, 'b_forget': _jnp.float32, 'swa_sinks': _jnp.float32, 'w_out': _jnp.float32, 'g_pre_mlp': _jnp.float32, 'g_post_mlp': _jnp.float32, 'w_up': _jnp.float32, 'w_down': _jnp.float32}
MOMENT_SCALE = {'w_mod': 1.352760e+00, 'b_mod': 2.389395e+00, 'g_pre_mix': 1.219358e-01, 'g_post_mix': 4.373227e+00, 'w_in': 1.040129e+00, 'b_forget': 3.083751e-01, 'swa_sinks': 2.625782e-02, 'w_out': 1.616581e+00, 'g_pre_mlp': 4.786316e-01, 'g_post_mlp': 3.958417e+00, 'w_up': 4.090679e-01, 'w_down': 1.004460e+00}


def _to_microbatches(a, axis):
    t = _jnp.moveaxis(a, axis, 0)
    t = t.reshape((N_MICROBATCH, t.shape[0] // N_MICROBATCH) + t.shape[1:])
    return _jnp.moveaxis(t, 1, axis + 1)


def setup_inputs(seed: int = 0) -> dict:
    inp = _fwd_setup_inputs(seed)
    key = _jax.random.fold_in(_jax.random.key(seed), 7919)
    shape, _ = _output_shape()
    out = dict(inp)
    out["loss_target"] = _jax.random.normal(_jax.random.fold_in(key, 0), shape, _jnp.float32)
    for i, name in enumerate(TWIN_WEIGHTS):
        w = inp[name].astype(_jnp.float32)
        if MOMENT_SCALE is None:
            s = _jnp.sqrt(_jnp.mean(_jnp.square(w)) + 1e-30)
        else:
            s = MOMENT_SCALE[name]
        km, kv = _jax.random.split(_jax.random.fold_in(key, i + 1))
        out[name] = w
        out["m_" + name] = s * _jax.random.normal(km, w.shape, _jnp.float32)
        out["v_" + name] = (s * s) * _jax.random.uniform(kv, w.shape, _jnp.float32, 0.5, 1.5)
    if N_MICROBATCH > 1:
        for name, axis in PER_EXAMPLE_BATCH_AXIS.items():
            out[name] = _to_microbatches(out[name], axis)
    return {'x': out['x'], 'c': out['c'], 'w_mod': out['w_mod'], 'b_mod': out['b_mod'], 'g_pre_mix': out['g_pre_mix'], 'g_post_mix': out['g_post_mix'], 'w_in': out['w_in'], 'b_forget': out['b_forget'], 'swa_sinks': out['swa_sinks'], 'w_out': out['w_out'], 'g_pre_mlp': out['g_pre_mlp'], 'g_post_mlp': out['g_post_mlp'], 'w_up': out['w_up'], 'w_down': out['w_down'], 'loss_target': out['loss_target'], 'm_w_mod': out['m_w_mod'], 'm_b_mod': out['m_b_mod'], 'm_g_pre_mix': out['m_g_pre_mix'], 'm_g_post_mix': out['m_g_post_mix'], 'm_w_in': out['m_w_in'], 'm_b_forget': out['m_b_forget'], 'm_swa_sinks': out['m_swa_sinks'], 'm_w_out': out['m_w_out'], 'm_g_pre_mlp': out['m_g_pre_mlp'], 'm_g_post_mlp': out['m_g_post_mlp'], 'm_w_up': out['m_w_up'], 'm_w_down': out['m_w_down'], 'v_w_mod': out['v_w_mod'], 'v_b_mod': out['v_b_mod'], 'v_g_pre_mix': out['v_g_pre_mix'], 'v_g_post_mix': out['v_g_post_mix'], 'v_w_in': out['v_w_in'], 'v_b_forget': out['v_b_forget'], 'v_swa_sinks': out['v_swa_sinks'], 'v_w_out': out['v_w_out'], 'v_g_pre_mlp': out['v_g_pre_mlp'], 'v_g_post_mlp': out['v_g_post_mlp'], 'v_w_up': out['v_w_up'], 'v_w_down': out['v_w_down']}


def _loss(weights, diff, rest, loss_target):
    with _jax.named_scope("forward"):
        args = {**rest, TWIN_DIFF_INPUT: diff, **{k: w.astype(_WEIGHT_DTYPES[k]) for k, w in weights.items()}}
        y = _forward(args)
    with _jax.named_scope("loss_head"):
        err = _jnp.square(y.astype(_jnp.float32) - loss_target)
        return 0.5 * _jnp.sum(_jnp.mean(err, axis=-1)) if err.ndim else 0.5 * err


def _adamw(w, g, m, v):
    m = ADAM_B1 * m + (1.0 - ADAM_B1) * g
    v = ADAM_B2 * v + (1.0 - ADAM_B2) * _jnp.square(g)
    m_hat = m / (1.0 - ADAM_B1 ** ADAM_STEP)
    v_hat = v / (1.0 - ADAM_B2 ** ADAM_STEP)
    delta = -ADAM_LR * (m_hat / (_jnp.sqrt(v_hat) + ADAM_EPS) + ADAM_WD * w)
    return delta, m, v


def reference(x, c, w_mod, b_mod, g_pre_mix, g_post_mix, w_in, b_forget, swa_sinks, w_out, g_pre_mlp, g_post_mlp, w_up, w_down, loss_target, m_w_mod, m_b_mod, m_g_pre_mix, m_g_post_mix, m_w_in, m_b_forget, m_swa_sinks, m_w_out, m_g_pre_mlp, m_g_post_mlp, m_w_up, m_w_down, v_w_mod, v_b_mod, v_g_pre_mix, v_g_post_mix, v_w_in, v_b_forget, v_swa_sinks, v_w_out, v_g_pre_mlp, v_g_post_mlp, v_w_up, v_w_down):
    given = dict(x=x, c=c, w_mod=w_mod, b_mod=b_mod, g_pre_mix=g_pre_mix, g_post_mix=g_post_mix, w_in=w_in, b_forget=b_forget, swa_sinks=swa_sinks, w_out=w_out, g_pre_mlp=g_pre_mlp, g_post_mlp=g_post_mlp, w_up=w_up, w_down=w_down, loss_target=loss_target, m_w_mod=m_w_mod, m_b_mod=m_b_mod, m_g_pre_mix=m_g_pre_mix, m_g_post_mix=m_g_post_mix, m_w_in=m_w_in, m_b_forget=m_b_forget, m_swa_sinks=m_swa_sinks, m_w_out=m_w_out, m_g_pre_mlp=m_g_pre_mlp, m_g_post_mlp=m_g_post_mlp, m_w_up=m_w_up, m_w_down=m_w_down, v_w_mod=v_w_mod, v_b_mod=v_b_mod, v_g_pre_mix=v_g_pre_mix, v_g_post_mix=v_g_post_mix, v_w_in=v_w_in, v_b_forget=v_b_forget, v_swa_sinks=v_swa_sinks, v_w_out=v_w_out, v_g_pre_mlp=v_g_pre_mlp, v_g_post_mlp=v_g_post_mlp, v_w_up=v_w_up, v_w_down=v_w_down)
    weights = {n: given[n] for n in TWIN_WEIGHTS}
    shared = {n: given[n] for n in SHARED_INPUTS}
    per_example = {n: given[n] for n in ['x', 'c']}
    grad_fn = _jax.value_and_grad(_loss, argnums=(0, 1))

    def one_microbatch(ex, loss_target):
        ex = dict(ex)
        diff = ex.pop(TWIN_DIFF_INPUT)
        return grad_fn(weights, diff, {**shared, **ex}, loss_target)

    if N_MICROBATCH == 1:
        loss, (grad_w, grad_x) = one_microbatch(per_example, given["loss_target"])
    else:
        def body(carry, xs):
            loss_sum, grad_sum = carry
            l_k, (gw_k, gx_k) = one_microbatch(xs[0], xs[1])
            with _jax.named_scope("update"):
                return (loss_sum + l_k, _jax.tree.map(_jnp.add, grad_sum, gw_k)), gx_k

        init = (_jnp.zeros((), _jnp.float32), _jax.tree.map(_jnp.zeros_like, weights))
        (loss, grad_w), grad_x = _jax.lax.scan(body, init, (per_example, given["loss_target"]))
    with _jax.named_scope("update"):
        delta_w, new_m, new_v = {}, {}, {}
        for n in TWIN_WEIGHTS:
            delta_w[n], new_m[n], new_v[n] = _adamw(weights[n], grad_w[n], given["m_" + n], given["v_" + n])
    return (loss, grad_x, *[grad_w[n] for n in TWIN_WEIGHTS], *[delta_w[n] for n in TWIN_WEIGHTS],
            *[new_m[n] for n in TWIN_WEIGHTS], *[new_v[n] for n in TWIN_WEIGHTS])
```

```python
import functools

import jax
import jax.numpy as jnp
from jax import lax
from jax.experimental import pallas as pl
from jax.experimental.pallas import tpu as pltpu

F32 = jnp.float32
BF16 = jnp.bfloat16
MESH = pl.DeviceIdType.MESH

HEAD_DIM = 128
SWA_BLOCK = 128
ROPE_THETA = 10000.0
NORM_EPS = 1e-6
NEG = -1e30
N_MOD = 6
ADAM_LR = 0.001
ADAM_B1 = 0.9
ADAM_B2 = 0.999
ADAM_EPS = 1e-08
ADAM_WD = 0.01
ADAM_STEP = 10
N_CHIPS = 4
N_DEV = 8
LANES = 128
VMEM_CAP = 60 * 1024 * 1024

_NN = (((1,), (0,)), ((), ()))
_NT = (((1,), (1,)), ((), ()))
_TN = (((0,), (0,)), ((), ()))


def _vmem(nbytes):
    return int(min(VMEM_CAP, nbytes * 5 // 4 + (4 << 20)))


def _nbytes(shape, dtype):
    n = 1
    for s in shape:
        n *= s
    return n * jnp.dtype(dtype).itemsize


def _fit(t, n):
    t = min(t, n)
    assert n % t == 0, (t, n)
    return t


def _matmul(name, a, b, mode, out_defs, epilogue, extras=(), tm=1024, tn=1024, tk=512):
    if mode == "nn":
        (M, K), (K2, N) = a.shape, b.shape
    elif mode == "nt":
        (M, K), (N, K2) = a.shape, b.shape
    else:
        (K, M), (K2, N) = a.shape, b.shape
    assert K == K2, (a.shape, b.shape, mode)
    tm, tn, tk = _fit(tm, M), _fit(tn, N), _fit(tk, K)
    nk = K // tk
    dims = {"nn": _NN, "nt": _NT, "tn": _TN}[mode]
    a_spec = (pl.BlockSpec((tk, tm), lambda i, j, k: (k, i)) if mode == "tn"
              else pl.BlockSpec((tm, tk), lambda i, j, k: (i, k)))
    b_spec = (pl.BlockSpec((tn, tk), lambda i, j, k: (j, k)) if mode == "nt"
              else pl.BlockSpec((tk, tn), lambda i, j, k: (k, j)))
    n_ex, n_out = len(extras), len(out_defs)

    def body(*refs):
        a_ref, b_ref = refs[0], refs[1]
        ex = refs[2:2 + n_ex]
        outs = refs[2 + n_ex:2 + n_ex + n_out]
        prod = lax.dot_general(a_ref[...], b_ref[...], dims, preferred_element_type=F32)
        if nk == 1:
            epilogue(prod, ex, outs)
        else:
            acc_ref = refs[-1]
            k = pl.program_id(2)

            @pl.when(k == 0)
            def _():
                acc_ref[...] = prod

            @pl.when(k > 0)
            def _():
                acc_ref[...] += prod

            @pl.when(k == nk - 1)
            def _():
                epilogue(acc_ref[...], ex, outs)

    def wrap(f):
        return lambda i, j, k: f(i, j)

    in_specs = [a_spec, b_spec] + [pl.BlockSpec(blk, wrap(f)) for _, blk, f in extras]
    out_specs = [pl.BlockSpec(blk, wrap(f)) for _, _, blk, f in out_defs]
    out_shape = [jax.ShapeDtypeStruct(s, d) for s, d, _, _ in out_defs]
    need = 2 * (tm * tk + tk * tn) * a.dtype.itemsize + 3 * tm * tn * 4
    need += sum(2 * _nbytes(blk, arr.dtype) for arr, blk, _ in extras)
    need += sum(2 * _nbytes(blk, d) for _, d, blk, _ in out_defs)
    res = pl.pallas_call(
        body, name=name, grid=(M // tm, N // tn, nk),
        in_specs=in_specs, out_specs=out_specs, out_shape=out_shape,
        scratch_shapes=[pltpu.VMEM((tm, tn), F32)] if nk > 1 else [],
        compiler_params=pltpu.CompilerParams(
            dimension_semantics=("parallel", "parallel", "arbitrary"), vmem_limit_bytes=_vmem(need)),
    )(a, b, *[arr for arr, _, _ in extras])
    return res


def _mm_plain(name, a, b, mode, out_dtype, **tiles):
    if mode == "nn":
        M, N = a.shape[0], b.shape[1]
    elif mode == "nt":
        M, N = a.shape[0], b.shape[0]
    else:
        M, N = a.shape[1], b.shape[1]
    tm, tn = _fit(tiles.get("tm", 1024), M), _fit(tiles.get("tn", 1024), N)

    def epi(acc, ex, outs):
        outs[0][...] = acc.astype(out_dtype)

    return _matmul(name, a, b, mode, [((M, N), out_dtype, (tm, tn), lambda i, j: (i, j))], epi, **tiles)[0]


def _rstd(v):
    return lax.rsqrt(jnp.mean(v * v, axis=-1, keepdims=True) + NORM_EPS)


def _row_call(name, body, row_ins, vec_ins, row_outs, acc_outs, S, D, tr):
    tr = _fit(tr, S)
    row_spec = pl.BlockSpec((tr, D), lambda r: (r, 0))
    vec_spec = pl.BlockSpec((1, D), lambda r: (0, 0))
    in_specs = [row_spec] * len(row_ins) + [vec_spec] * len(vec_ins)
    out_specs = [row_spec] * len(row_outs) + [pl.BlockSpec(shp, lambda r: (0, 0)) for shp in acc_outs]
    out_shape = [jax.ShapeDtypeStruct((S, D), d) for d in row_outs] + [jax.ShapeDtypeStruct(shp, F32) for shp in acc_outs]
    need = sum(2 * tr * D * a.dtype.itemsize for a in row_ins) + sum(2 * tr * D * jnp.dtype(d).itemsize for d in row_outs)
    need += 8 * tr * D * 4
    return pl.pallas_call(
        body, name=name, grid=(S // tr,), in_specs=in_specs, out_specs=out_specs, out_shape=out_shape,
        compiler_params=pltpu.CompilerParams(dimension_semantics=("arbitrary",), vmem_limit_bytes=_vmem(need)),
    )(*row_ins, *vec_ins)


def _acc_rows(ref, rows):
    @pl.when(pl.program_id(0) == 0)
    def _():
        ref[...] = jnp.zeros_like(ref)
    for n, r in enumerate(rows):
        ref[n:n + 1, :] += r


def _pre_norm(x, g, sc, sh):
    S, D = x.shape

    def body(x_ref, g_ref, sc_ref, sh_ref, h_ref):
        xv = x_ref[...]
        xn = xv * _rstd(xv)
        h_ref[...] = (xn * g_ref[...] * (1.0 + sc_ref[...]) + sh_ref[...]).astype(BF16)

    return _row_call("pre_norm_mix", body, [x], [g, sc, sh], [BF16], [], S, D, 256)[0]


def _post_mix(x, mix, g_post, gt, g_pre, sc, sh):
    S, D = x.shape

    def body(x_ref, mix_ref, gp_ref, gt_ref, g2_ref, sc_ref, sh_ref, x1_ref, h2_ref):
        mv = mix_ref[...]
        x1 = x_ref[...] + gt_ref[...] * (mv * _rstd(mv) * gp_ref[...])
        x1_ref[...] = x1
        h2_ref[...] = (x1 * _rstd(x1) * g2_ref[...] * (1.0 + sc_ref[...]) + sh_ref[...]).astype(BF16)

    return _row_call("post_mix_pre_mlp", body, [x, mix], [g_post, gt, g_pre, sc, sh], [F32, BF16], [], S, D, 256)


def _loss_and_post_mlp_bwd(x1, y, target, g_post, gt):
    S, D = x1.shape

    def body(x1_ref, y_ref, t_ref, g_ref, gt_ref, dy_ref, dout_ref, loss_ref, acc_ref):
        yv = y_ref[...]
        r = _rstd(yv)
        yh = yv * r
        n = yh * g_ref[...]
        diff = x1_ref[...] + gt_ref[...] * n - t_ref[...]
        dout = diff * (1.0 / D)
        dout_ref[...] = dout
        dn = dout * gt_ref[...]
        dyh = dn * g_ref[...]
        dy_ref[...] = (r * (dyh - yh * jnp.mean(dyh * yh, axis=-1, keepdims=True))).astype(BF16)
        _acc_rows(acc_ref, [jnp.sum(dout * n, axis=0, keepdims=True), jnp.sum(dn * yh, axis=0, keepdims=True)])

        @pl.when(pl.program_id(0) == 0)
        def _():
            loss_ref[...] = jnp.zeros_like(loss_ref)
        loss_ref[...] += jnp.full(loss_ref.shape, (0.5 / D) * jnp.sum(diff * diff), F32)

    return _row_call("loss_post_mlp_bwd", body, [x1, y, target], [g_post, gt], [BF16, F32],
                     [(8, LANES), (8, D)], S, D, 128)


def _pre_mlp_and_post_mix_bwd(dh2, x1, dout, mix, g_pre, sc, g_post, gt):
    S, D = x1.shape

    def body(dh_ref, x1_ref, dout_ref, mix_ref, g_ref, sc_ref, gp_ref, gt_ref, dx1_ref, dmix_ref, acc_ref):
        dh = dh_ref[...]
        x1v = x1_ref[...]
        r3 = _rstd(x1v)
        xn = x1v * r3
        dxn = dh * (1.0 + sc_ref[...]) * g_ref[...]
        dx1 = dout_ref[...] + r3 * (dxn - xn * jnp.mean(dxn * xn, axis=-1, keepdims=True))
        dx1_ref[...] = dx1
        mv = mix_ref[...]
        r2 = _rstd(mv)
        mh = mv * r2
        dn = dx1 * gt_ref[...]
        dmh = dn * gp_ref[...]
        dmix_ref[...] = (r2 * (dmh - mh * jnp.mean(dmh * mh, axis=-1, keepdims=True))).astype(BF16)
        _acc_rows(acc_ref, [
            jnp.sum(dh, axis=0, keepdims=True),
            jnp.sum(dh * xn * g_ref[...], axis=0, keepdims=True),
            jnp.sum(dh * (1.0 + sc_ref[...]) * xn, axis=0, keepdims=True),
            jnp.sum(dx1 * mh * gp_ref[...], axis=0, keepdims=True),
            jnp.sum(dn * mh, axis=0, keepdims=True)])

    return _row_call("pre_mlp_post_mix_bwd", body, [dh2, x1, dout, mix], [g_pre, sc, g_post, gt], [F32, BF16],
                     [(8, D)], S, D, 128)


def _pre_mix_bwd(dh, x, dx1, g_pre, sc):
    S, D = x.shape

    def body(dh_ref, x_ref, dx1_ref, g_ref, sc_ref, gx_ref, acc_ref):
        dhv = dh_ref[...]
        xv = x_ref[...]
        r = _rstd(xv)
        xn = xv * r
        dxn = dhv * (1.0 + sc_ref[...]) * g_ref[...]
        gx_ref[...] = dx1_ref[...] + r * (dxn - xn * jnp.mean(dxn * xn, axis=-1, keepdims=True))
        _acc_rows(acc_ref, [
            jnp.sum(dhv, axis=0, keepdims=True),
            jnp.sum(dhv * xn * g_ref[...], axis=0, keepdims=True),
            jnp.sum(dhv * (1.0 + sc_ref[...]) * xn, axis=0, keepdims=True)])

    return _row_call("pre_mix_bwd", body, [dh, x, dx1], [g_pre, sc], [F32], [(8, D)], S, D, 128)


CUM_BLOCK = 256


def _tri(n, upper):
    r = lax.broadcasted_iota(jnp.int32, (n, n), 0)
    c = lax.broadcasted_iota(jnp.int32, (n, n), 1)
    return ((c >= r) if upper else (c <= r)).astype(F32)


def _fox_gate_fwd(fg, b_pad, n_fox):
    S = fg.shape[0]
    cb = _fit(CUM_BLOCK, S)

    def body(fg_ref, b_ref, cum_ref, cumt_ref, cumb_ref):
        low = _tri(cb, False)
        carry = jnp.zeros((1, LANES), F32)
        for n in range(S // cb):
            z = fg_ref[n * cb:(n + 1) * cb, :] + b_ref[...]
            logf = jnp.minimum(z, 0.0) - jnp.log(1.0 + jnp.exp(-jnp.abs(z)))
            blk = jnp.dot(low, logf, precision=lax.Precision.HIGHEST, preferred_element_type=F32) + carry
            cum_ref[n * cb:(n + 1) * cb, :] = blk
            carry = blk[cb - 1:cb, :]
        cum = cum_ref[...]
        cumt_ref[...] = cum.T
        for h in range(n_fox):
            cumb_ref[h] = jnp.broadcast_to(cum[:, h:h + 1], (S, LANES))

    return pl.pallas_call(
        body, name="fox_gate_fwd",
        out_shape=[jax.ShapeDtypeStruct((S, LANES), F32), jax.ShapeDtypeStruct((LANES, S), F32),
                   jax.ShapeDtypeStruct((n_fox, S, LANES), F32)],
        compiler_params=pltpu.CompilerParams(vmem_limit_bytes=_vmem((4 + 2 * n_fox) * S * LANES * 4)),
    )(fg, b_pad)


def _fox_gate_bwd(dcum, fg, b_pad):
    S = fg.shape[0]
    cb = _fit(CUM_BLOCK, S)

    def body(dc_ref, fg_ref, b_ref, dfg_ref, db_ref):
        up = _tri(cb, True)
        carry = jnp.zeros((1, LANES), F32)
        db = jnp.zeros((1, LANES), F32)
        for n in reversed(range(S // cb)):
            blk = jnp.dot(up, dc_ref[n * cb:(n + 1) * cb, :], precision=lax.Precision.HIGHEST,
                          preferred_element_type=F32) + carry
            carry = blk[0:1, :]
            z = fg_ref[n * cb:(n + 1) * cb, :] + b_ref[...]
            dfg = blk * (1.0 / (1.0 + jnp.exp(z)))
            dfg_ref[n * cb:(n + 1) * cb, :] = dfg.astype(BF16)
            db = db + jnp.sum(dfg, axis=0, keepdims=True)
        db_ref[...] = jnp.broadcast_to(db, db_ref.shape)

    return pl.pallas_call(
        body, name="fox_gate_bwd",
        out_shape=[jax.ShapeDtypeStruct((S, LANES), BF16), jax.ShapeDtypeStruct((8, LANES), F32)],
        compiler_params=pltpu.CompilerParams(vmem_limit_bytes=_vmem(6 * S * LANES * 4)),
    )(dcum, fg, b_pad)


FOX_TILE = 512


def _fox_scores(q, k, cq, ck, q0, k0, tq, tk, scale):
    s = lax.dot_general(q, k, _NT, preferred_element_type=F32) * scale + cq - ck
    row = q0 + lax.broadcasted_iota(jnp.int32, (tq, tk), 0)
    col = k0 + lax.broadcasted_iota(jnp.int32, (tq, tk), 1)
    return jnp.where(col <= row, s, NEG)


def _fox_fwd(proj, cum_b, cum_row, n_fox):
    S = proj.shape[0]
    t = _fit(FOX_TILE, S)
    nq = S // t
    scale = HEAD_DIM ** -0.5

    def body(q_ref, k_ref, v_ref, cq_ref, ck_ref, o_ref, lse_ref):
        def q_block(qi, _):
            q0 = pl.multiple_of(qi * t, t)
            q = q_ref[pl.ds(q0, t), :]
            cq = cq_ref[0, pl.ds(q0, t), :][:, :1]

            def kv_block(j, carry):
                m, l, acc = carry
                k0 = pl.multiple_of(j * t, t)
                s = _fox_scores(q, k_ref[pl.ds(k0, t), :], cq, ck_ref[0, :, pl.ds(k0, t)], q0, k0, t, t, scale)
                m_new = jnp.maximum(m, jnp.max(s, axis=-1, keepdims=True))
                alpha = jnp.exp(m - m_new)
                p = jnp.exp(s - m_new)
                l = alpha * l + jnp.sum(p, axis=-1, keepdims=True)
                acc = alpha * acc + jnp.dot(p.astype(BF16), v_ref[pl.ds(k0, t), :], preferred_element_type=F32)
                return m_new, l, acc

            init = (jnp.full((t, 1), NEG, F32), jnp.zeros((t, 1), F32), jnp.zeros((t, HEAD_DIM), F32))
            m, l, acc = lax.fori_loop(0, qi + 1, kv_block, init)
            o_ref[pl.ds(q0, t), :] = acc / l
            lse_ref[0, pl.ds(q0, t), :] = jnp.broadcast_to(m + jnp.log(l), (t, LANES))
            return 0

        lax.fori_loop(0, nq, q_block, 0)

    col = lambda off: pl.BlockSpec((S, HEAD_DIM), lambda h: (0, off + h))
    per_head = pl.BlockSpec((1, S, LANES), lambda h: (h, 0, 0))
    return pl.pallas_call(
        body, name="fox_fwd", grid=(n_fox,),
        in_specs=[col(0), col(n_fox), col(2 * n_fox), per_head, pl.BlockSpec((1, 1, S), lambda h: (h, 0, 0))],
        out_specs=[pl.BlockSpec((S, HEAD_DIM), lambda h: (0, h)), per_head],
        out_shape=[jax.ShapeDtypeStruct((S, n_fox * HEAD_DIM), F32), jax.ShapeDtypeStruct((n_fox, S, LANES), F32)],
        compiler_params=pltpu.CompilerParams(dimension_semantics=("parallel",),
                                             vmem_limit_bytes=_vmem(16 * S * HEAD_DIM * 4 + 12 * t * t * 4)),
    )(proj, proj, proj, cum_b, cum_row)


def _fox_bwd(proj, o, do, lse_b, cum_b, cum_row, n_fox):
    S = proj.shape[0]
    t = _fit(FOX_TILE, S)
    nq = S // t
    scale = HEAD_DIM ** -0.5

    def body(q_ref, k_ref, v_ref, o_ref, do_ref, lse_ref, cq_ref, ck_ref, dq_ref, dk_ref, dv_ref, dc_ref, dcq_ref,
             dq_acc, delta_ref):
        dq_acc[...] = jnp.zeros_like(dq_acc)
        dcq_ref[...] = jnp.zeros_like(dcq_ref)

        def delta_block(qi, _):
            q0 = pl.multiple_of(qi * t, t)
            d = jnp.sum(do_ref[pl.ds(q0, t), :] * o_ref[pl.ds(q0, t), :], axis=-1, keepdims=True)
            delta_ref[pl.ds(q0, t), :] = jnp.broadcast_to(d, (t, LANES))
            return 0

        lax.fori_loop(0, nq, delta_block, 0)

        def kv_block(j, _):
            k0 = pl.multiple_of(j * t, t)
            k = k_ref[pl.ds(k0, t), :]
            v = v_ref[pl.ds(k0, t), :]
            ck = ck_ref[0, :, pl.ds(k0, t)]

            def q_block(qi, carry):
                dk, dv, dc = carry
                q0 = pl.multiple_of(qi * t, t)
                q = q_ref[pl.ds(q0, t), :]
                dov = do_ref[pl.ds(q0, t), :].astype(BF16)
                s = _fox_scores(q, k, cq_ref[0, pl.ds(q0, t), :][:, :1], ck, q0, k0, t, t, scale)
                p = jnp.exp(s - lse_ref[0, pl.ds(q0, t), :][:, :1])
                dp = lax.dot_general(dov, v, _NT, preferred_element_type=F32)
                ds = p * (dp - delta_ref[pl.ds(q0, t), :][:, :1])
                dsb = ds.astype(BF16)
                dv = dv + lax.dot_general(p.astype(BF16), dov, _TN, preferred_element_type=F32)
                dk = dk + lax.dot_general(dsb, q, _TN, preferred_element_type=F32)
                dq_acc[pl.ds(q0, t), :] += jnp.dot(dsb, k, preferred_element_type=F32)
                dc = dc - jnp.sum(ds, axis=0, keepdims=True)
                dcq_ref[0, pl.ds(q0, t), :] += jnp.broadcast_to(jnp.sum(ds, axis=1, keepdims=True), (t, LANES))
                return dk, dv, dc

            init = (jnp.zeros((t, HEAD_DIM), F32), jnp.zeros((t, HEAD_DIM), F32), jnp.zeros((1, t), F32))
            dk, dv, dc = lax.fori_loop(j, nq, q_block, init)
            dk_ref[pl.ds(k0, t), :] = (dk * scale).astype(BF16)
            dv_ref[pl.ds(k0, t), :] = dv.astype(BF16)
            dc_ref[0, :, pl.ds(k0, t)] = dc
            return 0

        lax.fori_loop(0, nq, kv_block, 0)
        dq_ref[...] = (dq_acc[...] * scale).astype(BF16)

    col = lambda off: pl.BlockSpec((S, HEAD_DIM), lambda h: (0, off + h))
    per_head = pl.BlockSpec((1, S, LANES), lambda h: (h, 0, 0))
    row = pl.BlockSpec((1, 1, S), lambda h: (h, 0, 0))
    grad = jax.ShapeDtypeStruct((S, n_fox * HEAD_DIM), BF16)
    return pl.pallas_call(
        body, name="fox_bwd", grid=(n_fox,),
        in_specs=[col(0), col(n_fox), col(2 * n_fox), col(0), col(0), per_head, per_head, row],
        out_specs=[col(0), col(0), col(0), row, per_head],
        out_shape=[grad, grad, grad, jax.ShapeDtypeStruct((n_fox, 1, S), F32), jax.ShapeDtypeStruct((n_fox, S, LANES), F32)],
        scratch_shapes=[pltpu.VMEM((S, HEAD_DIM), F32), pltpu.VMEM((S, LANES), F32)],
        compiler_params=pltpu.CompilerParams(dimension_semantics=("parallel",),
                                             vmem_limit_bytes=_vmem(24 * S * HEAD_DIM * 4 + 16 * t * t * 4)),
    )(proj, proj, proj, o, do, lse_b, cum_b, cum_row)


def _rope_tables(S):
    half = HEAD_DIM // 2
    inv_freq = 1.0 / (ROPE_THETA ** (jnp.arange(half, dtype=F32) * (2.0 / HEAD_DIM)))
    ang = jnp.arange(S).astype(F32)[:, None] * inv_freq[None, :]
    cos, sin = jnp.cos(ang), jnp.sin(ang)
    return jnp.concatenate([cos, cos], axis=-1), jnp.concatenate([-sin, sin], axis=-1)


def _rope(name, src, first_block, n_blocks, cos, sin_signed):
    S = src.shape[0]

    def body(x_ref, cos_ref, sin_ref, o_ref):
        xv = x_ref[...].astype(F32)
        o_ref[...] = (xv * cos_ref[...] + pltpu.roll(xv, HEAD_DIM // 2, 1) * sin_ref[...]).astype(BF16)

    table = pl.BlockSpec((S, HEAD_DIM), lambda n: (0, 0))
    return pl.pallas_call(
        body, name=name, grid=(n_blocks,),
        in_specs=[pl.BlockSpec((S, HEAD_DIM), lambda n: (0, first_block + n)), table, table],
        out_specs=pl.BlockSpec((S, HEAD_DIM), lambda n: (0, n)),
        out_shape=jax.ShapeDtypeStruct((S, n_blocks * HEAD_DIM), BF16),
        compiler_params=pltpu.CompilerParams(dimension_semantics=("parallel",),
                                             vmem_limit_bytes=_vmem(12 * S * HEAD_DIM * 4)),
    )(src, cos, sin_signed)


def _swa_tile(q_ref, kp_ref, kc_ref, n, group, scale):
    B = SWA_BLOCK
    qs = jnp.concatenate([q_ref[:, g * HEAD_DIM:(g + 1) * HEAD_DIM] for g in range(group)], axis=0)
    kcat = jnp.concatenate([kp_ref[...], kc_ref[...]], axis=0)
    s = lax.dot_general(qs, kcat, _NT, preferred_element_type=F32) * scale
    qi = lax.broadcasted_iota(jnp.int32, (group * B, 2 * B), 0) % B
    kj = lax.broadcasted_iota(jnp.int32, (group * B, 2 * B), 1)
    diff = qi + B - kj
    mask = (diff >= 0) & (diff < B) & ((n * B + kj - B) >= 0)
    return qs, kcat, jnp.where(mask, s, NEG)


def _swa_sink_col(sink_ref, kv, group):
    head = lax.broadcasted_iota(jnp.int32, (group * SWA_BLOCK, 1), 0) // SWA_BLOCK
    col = jnp.zeros((group * SWA_BLOCK, 1), F32)
    for g in range(group):
        col = jnp.where(head == g, sink_ref[kv * group + g], col)
    return col


def _swa_specs(n_kv, group, q_first, k_first, v_first):
    B = SWA_BLOCK
    prev = lambda n: jnp.maximum(n - 1, 0)
    return [
        pl.BlockSpec((B, group * HEAD_DIM), lambda kv, n: (n, q_first + kv)),
        pl.BlockSpec((B, HEAD_DIM), lambda kv, n: (prev(n), k_first + kv)),
        pl.BlockSpec((B, HEAD_DIM), lambda kv, n: (n, k_first + kv)),
        pl.BlockSpec((B, HEAD_DIM), lambda kv, n: (prev(n), v_first + kv)),
        pl.BlockSpec((B, HEAD_DIM), lambda kv, n: (n, v_first + kv)),
    ]


def _swa_fwd(rq, proj, v_first, sinks, n_q, n_kv):
    S = rq.shape[0]
    B = SWA_BLOCK
    group = n_q // n_kv
    scale = HEAD_DIM ** -0.5

    def body(q_ref, kp_ref, kc_ref, vp_ref, vc_ref, sink_ref, o_ref, lse_ref):
        kv, n = pl.program_id(0), pl.program_id(1)
        _, _, s = _swa_tile(q_ref, kp_ref, kc_ref, n, group, scale)
        sink = _swa_sink_col(sink_ref, kv, group)
        m = jnp.maximum(jnp.max(s, axis=-1, keepdims=True), sink)
        p = jnp.exp(s - m)
        denom = jnp.sum(p, axis=-1, keepdims=True) + jnp.exp(sink - m)
        vcat = jnp.concatenate([vp_ref[...], vc_ref[...]], axis=0)
        o = jnp.dot((p / denom).astype(BF16), vcat, preferred_element_type=F32)
        lse = m + jnp.log(denom)
        for g in range(group):
            o_ref[:, g * HEAD_DIM:(g + 1) * HEAD_DIM] = o[g * B:(g + 1) * B, :]
            lse_ref[0, :, g * LANES:(g + 1) * LANES] = jnp.broadcast_to(lse[g * B:(g + 1) * B, :], (B, LANES))

    specs = _swa_specs(n_kv, group, 0, n_q, v_first)
    q_blk = pl.BlockSpec((B, group * HEAD_DIM), lambda kv, n: (n, kv))
    return pl.pallas_call(
        body, name="swa_fwd", grid=(n_kv, S // B),
        in_specs=specs + [pl.BlockSpec(memory_space=pltpu.SMEM)],
        out_specs=[q_blk, pl.BlockSpec((1, B, group * LANES), lambda kv, n: (kv, n, 0))],
        out_shape=[jax.ShapeDtypeStruct((S, n_q * HEAD_DIM), F32), jax.ShapeDtypeStruct((n_kv, S, group * LANES), F32)],
        compiler_params=pltpu.CompilerParams(dimension_semantics=("parallel", "arbitrary")),
    )(rq, rq, rq, proj, proj, sinks)


def _swa_bwd(rq, proj, v_first, sinks, o, do, do_first, lse_b, n_q, n_kv):
    S = rq.shape[0]
    B = SWA_BLOCK
    group = n_q // n_kv
    scale = HEAD_DIM ** -0.5

    def body(q_ref, kp_ref, kc_ref, vp_ref, vc_ref, o_ref, do_ref, lse_ref, sink_ref,
             dq_ref, dk_ref, dv_ref, dsink_ref):
        kv, n = pl.program_id(0), pl.program_id(1)

        @pl.when(n == 0)
        def _():
            dk_ref[...] = jnp.zeros_like(dk_ref)
            dv_ref[...] = jnp.zeros_like(dv_ref)
            dsink_ref[...] = jnp.zeros_like(dsink_ref)

        qs, kcat, s = _swa_tile(q_ref, kp_ref, kc_ref, n, group, scale)
        sink = _swa_sink_col(sink_ref, kv, group)
        stack = lambda ref, w: jnp.concatenate([ref[:, g * w:(g + 1) * w] for g in range(group)], axis=0)
        lse = jnp.concatenate([lse_ref[0, :, g * LANES:g * LANES + 1] for g in range(group)], axis=0)
        do32 = stack(do_ref, HEAD_DIM)
        delta = jnp.sum(do32 * stack(o_ref, HEAD_DIM), axis=-1, keepdims=True)
        dov = do32.astype(BF16)
        p = jnp.exp(s - lse)
        vcat = jnp.concatenate([vp_ref[...], vc_ref[...]], axis=0)
        dp = lax.dot_general(dov, vcat, _NT, preferred_element_type=F32)
        ds = p * (dp - delta)
        dsb = ds.astype(BF16)
        dq = jnp.dot(dsb, kcat, preferred_element_type=F32) * scale
        for g in range(group):
            dq_ref[:, g * HEAD_DIM:(g + 1) * HEAD_DIM] = dq[g * B:(g + 1) * B, :].astype(BF16)
        dkcat = lax.dot_general(dsb, qs, _TN, preferred_element_type=F32) * scale
        dvcat = lax.dot_general(p.astype(BF16), dov, _TN, preferred_element_type=F32)
        prev0 = pl.multiple_of(jnp.maximum(n - 1, 0) * B, B)
        cur0 = pl.multiple_of(n * B, B)
        dk_ref[0, pl.ds(prev0, B), :] += dkcat[:B, :]
        dk_ref[0, pl.ds(cur0, B), :] += dkcat[B:, :]
        dv_ref[0, pl.ds(prev0, B), :] += dvcat[:B, :]
        dv_ref[0, pl.ds(cur0, B), :] += dvcat[B:, :]
        dsk = -jnp.exp(sink - lse) * delta
        lane = lax.broadcasted_iota(jnp.int32, (1, LANES), 1)
        row = jnp.zeros((1, LANES), F32)
        for g in range(group):
            row = row + jnp.where(lane == g, jnp.sum(dsk[g * B:(g + 1) * B, :]), 0.0)
        dsink_ref[0, 0:1, :] += row

    specs = _swa_specs(n_kv, group, 0, n_q, v_first)
    q_blk = pl.BlockSpec((B, group * HEAD_DIM), lambda kv, n: (n, kv))
    acc = pl.BlockSpec((1, S, HEAD_DIM), lambda kv, n: (kv, 0, 0))
    return pl.pallas_call(
        body, name="swa_bwd", grid=(n_kv, S // B),
        in_specs=specs + [q_blk, pl.BlockSpec((B, group * HEAD_DIM), lambda kv, n: (n, do_first + kv)),
                          pl.BlockSpec((1, B, group * LANES), lambda kv, n: (kv, n, 0)),
                          pl.BlockSpec(memory_space=pltpu.SMEM)],
        out_specs=[q_blk, acc, acc, pl.BlockSpec((1, 8, LANES), lambda kv, n: (kv, 0, 0))],
        out_shape=[jax.ShapeDtypeStruct((S, n_q * HEAD_DIM), BF16), jax.ShapeDtypeStruct((n_kv, S, HEAD_DIM), F32),
                   jax.ShapeDtypeStruct((n_kv, S, HEAD_DIM), F32), jax.ShapeDtypeStruct((n_kv, 8, LANES), F32)],
        compiler_params=pltpu.CompilerParams(dimension_semantics=("parallel", "arbitrary")),
    )(rq, rq, rq, proj, proj, o, do, lse_b, sinks)


def _adamw(w, g, m, v):
    m = ADAM_B1 * m + (1.0 - ADAM_B1) * g
    v = ADAM_B2 * v + (1.0 - ADAM_B2) * (g * g)
    m_hat = m / (1.0 - ADAM_B1 ** ADAM_STEP)
    v_hat = v / (1.0 - ADAM_B2 ** ADAM_STEP)
    delta = -ADAM_LR * (m_hat / (jnp.sqrt(v_hat) + ADAM_EPS) + ADAM_WD * w)
    return delta, m, v


def _mod_fwd(cond_in, w_mod, b_shard):
    R, D = cond_in.shape
    cols = w_mod.shape[1]
    tn = _fit(512, cols)

    def body(c_ref, w_ref, b_ref, o_ref):
        cv = c_ref[...]
        cond = (cv / (1.0 + jnp.exp(-cv))).astype(BF16)
        o_ref[...] = jnp.dot(cond, w_ref[...].astype(BF16), preferred_element_type=F32) + b_ref[...]

    return pl.pallas_call(
        body, name="mod_fwd", grid=(cols // tn,),
        in_specs=[pl.BlockSpec((R, D), lambda j: (0, 0)), pl.BlockSpec((D, tn), lambda j: (0, j)),
                  pl.BlockSpec((1, tn), lambda j: (0, j))],
        out_specs=pl.BlockSpec((R, tn), lambda j: (0, j)),
        out_shape=jax.ShapeDtypeStruct((R, cols), F32),
        compiler_params=pltpu.CompilerParams(dimension_semantics=("parallel",), vmem_limit_bytes=_vmem(3 * D * tn * 4)),
    )(cond_in, w_mod, b_shard)


def _mod_update(c_t, dmod, w, m, v):
    D, nb = c_t.shape
    cols = w.shape[1]
    tn = _fit(256, cols)

    def body(c_ref, d_ref, w_ref, m_ref, v_ref, g_ref, dl_ref, nm_ref, nv_ref):
        cv = c_ref[...]
        cond = cv / (1.0 + jnp.exp(-cv))
        g = jnp.zeros((D, tn), F32)
        for b in range(nb):
            g = g + cond[:, b:b + 1] * d_ref[b:b + 1, :]
        g_ref[...] = g
        dl_ref[...], nm_ref[...], nv_ref[...] = _adamw(w_ref[...], g, m_ref[...], v_ref[...])

    blk = pl.BlockSpec((D, tn), lambda j: (0, j))
    out = jax.ShapeDtypeStruct((D, cols), F32)
    return pl.pallas_call(
        body, name="mod_update", grid=(cols // tn,),
        in_specs=[pl.BlockSpec((D, nb), lambda j: (0, 0)), pl.BlockSpec((nb, tn), lambda j: (0, j)), blk, blk, blk],
        out_specs=[blk] * 4, out_shape=[out] * 4,
        compiler_params=pltpu.CompilerParams(dimension_semantics=("parallel",), vmem_limit_bytes=_vmem(18 * D * tn * 4)),
    )(c_t, dmod, w, m, v)


def _adam_update(name, w, g, m, v):
    R, C = w.shape
    tr = _fit(256, R)

    def body(w_ref, g_ref, m_ref, v_ref, dl_ref, nm_ref, nv_ref):
        dl_ref[...], nm_ref[...], nv_ref[...] = _adamw(w_ref[...], g_ref[...], m_ref[...], v_ref[...])

    blk = pl.BlockSpec((tr, C), lambda r: (r, 0))
    out = jax.ShapeDtypeStruct((R, C), F32)
    padded = -(-C // LANES) * LANES
    return pl.pallas_call(
        body, name=name, grid=(R // tr,), in_specs=[blk] * 4, out_specs=[blk] * 3, out_shape=[out] * 3,
        compiler_params=pltpu.CompilerParams(dimension_semantics=("parallel",), vmem_limit_bytes=_vmem(16 * tr * padded * 4)),
    )(w, g, m, v)


def _sum_blocks(name, stacked, n):
    R, C = stacked.shape[0] // n, stacked.shape[1]

    def body(s_ref, o_ref):
        total = s_ref[0:R, :]
        for d in range(1, n):
            total = total + s_ref[d * R:(d + 1) * R, :]
        o_ref[...] = total

    return pl.pallas_call(body, name=name, out_shape=jax.ShapeDtypeStruct((R, C), F32))(stacked)


def _place():
    return lax.axis_index("x"), lax.axis_index("y"), lax.axis_index("c")


def _allgather8(name, block):
    m_per, n = block.shape

    def body(x_ref, out_ref, send_sems, recv_sems, local_sem):
        x, y, c = _place()
        me, sibling = (x, y, c), (x, y, 1 - c)
        chips = [(1 - x, y), (x, 1 - y), (1 - x, 1 - y)]

        def rows(px, py, pc):
            return out_ref.at[pl.ds((4 * px + 2 * py + pc) * m_per, m_per), :]

        def copy(k, blk, to, src=None):
            return pltpu.make_async_remote_copy(
                src_ref=rows(*blk) if src is None else src, dst_ref=rows(*blk),
                send_sem=send_sems.at[k], recv_sem=recv_sems.at[k], device_id=to, device_id_type=MESH)

        mine = pltpu.make_async_copy(x_ref, rows(*me), local_sem)
        mine.start()
        first = [copy(0, me, sibling, src=x_ref)]
        first += [copy(1 + j, me, (*chip, c), src=x_ref) for j, chip in enumerate(chips)]
        for cp in first:
            cp.start()
        passed = [copy(4 + j, (*chip, c), sibling) for j, chip in enumerate(chips)]
        for j, chip in enumerate(chips):
            copy(1 + j, (*chip, c), me).wait_recv()
            passed[j].start()
        copy(0, sibling, me).wait_recv()
        for j, chip in enumerate(chips):
            copy(4 + j, (*chip, 1 - c), me).wait_recv()
        for cp in first + passed:
            cp.wait_send()
        mine.wait()

    return pl.pallas_call(
        body, name=name, out_shape=jax.ShapeDtypeStruct((N_DEV * m_per, n), block.dtype),
        in_specs=[pl.BlockSpec(memory_space=pltpu.VMEM)], out_specs=pl.BlockSpec(memory_space=pltpu.VMEM),
        scratch_shapes=[pltpu.SemaphoreType.DMA((7,)), pltpu.SemaphoreType.DMA((7,)), pltpu.SemaphoreType.DMA],
    )(block)


_ANY = pl.BlockSpec(memory_space=pl.ANY)


def _half(ref, c, rows):
    return ref.at[pl.ds(c * (rows // 2), rows // 2), :]


def _gather_weights(shards):
    nw = len(shards)

    def body(*refs):
        ws, outs = refs[:nw], refs[nw:2 * nw]
        send_sems, recv_sems, local_sems = refs[2 * nw:]
        x, y, c = _place()
        chips = [(1 - x, y), (x, 1 - y), (1 - x, 1 - y)]
        sibling = (x, y, 1 - c)
        pending = []
        for k in range(nw):
            R = ws[k].shape[0]
            mine = pltpu.make_async_copy(ws[k], outs[k].at[2 * x + y], local_sems.at[k])
            mine.start()
            pending.append(mine)

        def ici(k, j):
            R = ws[k].shape[0]
            cx, cy = chips[j]
            return pltpu.make_async_remote_copy(
                src_ref=_half(ws[k], c, R), dst_ref=_half(outs[k].at[2 * x + y], c, R),
                send_sem=send_sems.at[6 * k + j], recv_sem=recv_sems.at[6 * k + j],
                device_id=(cx, cy, c), device_id_type=MESH)

        def landed(k, j, hc):
            R = ws[k].shape[0]
            cx, cy = chips[j]
            return _half(outs[k].at[2 * cx + cy], hc, R)

        def d2d(k, j, hc):
            return pltpu.make_async_remote_copy(
                src_ref=landed(k, j, hc), dst_ref=landed(k, j, hc),
                send_sem=send_sems.at[6 * k + 3 + j], recv_sem=recv_sems.at[6 * k + 3 + j],
                device_id=sibling, device_id_type=MESH)

        sends = []
        for k in range(nw):
            for j in range(3):
                cp = ici(k, j)
                cp.start()
                sends.append(cp)
        for k in range(nw):
            for j in range(3):
                R = ws[k].shape[0]
                pltpu.make_async_remote_copy(
                    src_ref=landed(k, j, c), dst_ref=landed(k, j, c),
                    send_sem=send_sems.at[6 * k + j], recv_sem=recv_sems.at[6 * k + j],
                    device_id=sibling, device_id_type=MESH).wait_recv()
                cp = d2d(k, j, c)
                cp.start()
                sends.append(cp)
        for k in range(nw):
            for j in range(3):
                d2d(k, j, 1 - c).wait_recv()
        for cp in sends:
            cp.wait_send()
        for cp in pending:
            cp.wait()

    return pl.pallas_call(
        body, name="gather_weights",
        out_shape=[jax.ShapeDtypeStruct((N_CHIPS,) + s.shape, s.dtype) for s in shards],
        in_specs=[_ANY] * nw, out_specs=[_ANY] * nw,
        scratch_shapes=[pltpu.SemaphoreType.DMA((6 * nw,)), pltpu.SemaphoreType.DMA((6 * nw,)),
                        pltpu.SemaphoreType.DMA((nw,))],
    )(*shards)


def _pair_exchange(grads):
    nw = len(grads)

    def body(*refs):
        gs, outs = refs[:nw], refs[nw:2 * nw]
        send_sems, recv_sems = refs[2 * nw:]
        x, y, c = _place()
        copies = []
        for k in range(nw):
            half = gs[k].shape[1] // 2
            cp = pltpu.make_async_remote_copy(
                src_ref=gs[k].at[:, pl.ds((1 - c) * half, half), :], dst_ref=outs[k],
                send_sem=send_sems.at[k], recv_sem=recv_sems.at[k], device_id=(x, y, 1 - c), device_id_type=MESH)
            cp.start()
            copies.append(cp)
        for cp in copies:
            cp.wait()

    return pl.pallas_call(
        body, name="grad_pair_exchange",
        out_shape=[jax.ShapeDtypeStruct((N_CHIPS, g.shape[1] // 2, g.shape[2]), g.dtype) for g in grads],
        in_specs=[_ANY] * nw, out_specs=[_ANY] * nw,
        scratch_shapes=[pltpu.SemaphoreType.DMA((nw,)), pltpu.SemaphoreType.DMA((nw,))],
    )(*grads)


def _pair_add(name, core, grad, recv):
    n, R, C = grad.shape
    half = R // 2
    tr = _fit(256, half)
    nblk = half // tr

    def body(core_ref, g_ref, r_ref, o_ref):
        o_ref[...] = (g_ref[...].astype(F32) + r_ref[...].astype(F32)).astype(BF16)

    grid_spec = pltpu.PrefetchScalarGridSpec(
        num_scalar_prefetch=1, grid=(n, nblk),
        in_specs=[pl.BlockSpec((1, tr, C), lambda s, r, core_ref: (s, core_ref[0] * nblk + r, 0)),
                  pl.BlockSpec((1, tr, C), lambda s, r, core_ref: (s, r, 0))],
        out_specs=pl.BlockSpec((1, tr, C), lambda s, r, core_ref: (s, r, 0)))
    return pl.pallas_call(
        body, name=name, grid_spec=grid_spec, out_shape=jax.ShapeDtypeStruct((n, half, C), BF16),
        compiler_params=pltpu.CompilerParams(dimension_semantics=("parallel", "parallel")),
    )(core, grad, recv)


def _chip_exchange(sums):
    nw = len(sums)

    def body(*refs):
        ps, outs = refs[:nw], refs[nw:2 * nw]
        send_sems, recv_sems = refs[2 * nw:]
        x, y, c = _place()
        chips = [(1 - x, y), (x, 1 - y), (1 - x, 1 - y)]
        copies = []
        for k in range(nw):
            for j, (cx, cy) in enumerate(chips):
                cp = pltpu.make_async_remote_copy(
                    src_ref=ps[k].at[2 * cx + cy], dst_ref=outs[k].at[j],
                    send_sem=send_sems.at[3 * k + j], recv_sem=recv_sems.at[3 * k + j],
                    device_id=(cx, cy, c), device_id_type=MESH)
                cp.start()
                copies.append(cp)
        for cp in copies:
            cp.wait()

    return pl.pallas_call(
        body, name="grad_chip_exchange",
        out_shape=[jax.ShapeDtypeStruct((3,) + p.shape[1:], p.dtype) for p in sums],
        in_specs=[_ANY] * nw, out_specs=[_ANY] * nw,
        scratch_shapes=[pltpu.SemaphoreType.DMA((3 * nw,)), pltpu.SemaphoreType.DMA((3 * nw,))],
    )(*sums)


def _chip_add(name, chip, sums, recv):
    _, H, C = sums.shape
    tr = _fit(256, H)

    def body(chip_ref, p_ref, r_ref, o_ref):
        total = p_ref[0].astype(F32)
        for j in range(3):
            total = total + r_ref[j].astype(F32)
        o_ref[...] = total

    grid_spec = pltpu.PrefetchScalarGridSpec(
        num_scalar_prefetch=1, grid=(H // tr,),
        in_specs=[pl.BlockSpec((1, tr, C), lambda r, chip_ref: (chip_ref[0], r, 0)),
                  pl.BlockSpec((3, tr, C), lambda r, chip_ref: (0, r, 0))],
        out_specs=pl.BlockSpec((tr, C), lambda r, chip_ref: (r, 0)))
    return pl.pallas_call(
        body, name=name, grid_spec=grid_spec, out_shape=jax.ShapeDtypeStruct((H, C), F32),
        compiler_params=pltpu.CompilerParams(dimension_semantics=("parallel",)),
    )(chip, sums, recv)


def _pair_share(halves):
    nw = len(halves)

    def body(*refs):
        hs, outs = refs[:nw], refs[nw:2 * nw]
        send_sems, recv_sems, local_sems = refs[2 * nw:]
        x, y, c = _place()
        copies = []
        for k in range(nw):
            mine = pltpu.make_async_copy(hs[k], outs[k].at[c], local_sems.at[k])
            mine.start()
            copies.append(mine)
            cp = pltpu.make_async_remote_copy(
                src_ref=hs[k], dst_ref=outs[k].at[c], send_sem=send_sems.at[k], recv_sem=recv_sems.at[k],
                device_id=(x, y, 1 - c), device_id_type=MESH)
            cp.start()
            copies.append(cp)
        for cp in copies:
            cp.wait()

    return pl.pallas_call(
        body, name="grad_pair_share",
        out_shape=[jax.ShapeDtypeStruct((2,) + h.shape, h.dtype) for h in halves],
        in_specs=[_ANY] * nw, out_specs=[_ANY] * nw,
        scratch_shapes=[pltpu.SemaphoreType.DMA((nw,)), pltpu.SemaphoreType.DMA((nw,)), pltpu.SemaphoreType.DMA((nw,))],
    )(*halves)


def kernel(x, c, w_mod, b_mod, g_pre_mix, g_post_mix, w_in, b_forget, swa_sinks, w_out, g_pre_mlp, g_post_mlp, w_up, w_down, loss_target, m_w_mod, m_b_mod, m_g_pre_mix, m_g_post_mix, m_w_in, m_b_forget, m_swa_sinks, m_w_out, m_g_pre_mlp, m_g_post_mlp, m_w_up, m_w_down, v_w_mod, v_b_mod, v_g_pre_mix, v_g_post_mix, v_w_in, v_b_forget, v_swa_sinks, v_w_out, v_g_pre_mlp, v_g_post_mlp, v_w_up, v_w_down):
    S, D = x.shape[1], x.shape[2]
    n_heads = D // HEAD_DIM
    n_fox = n_heads // 2
    n_swa = n_heads - n_fox
    n_kv = max(1, n_swa // 4)
    fox_w, swa_w, kv_w = n_fox * HEAD_DIM, n_swa * HEAD_DIM, n_kv * HEAD_DIM
    main_w = 3 * fox_w + swa_w + 2 * kv_w
    in_w = main_w + n_fox
    mod_cols = w_mod.shape[2]

    ax, ay, ac = _place()
    chip = 2 * ax + ay
    dev = 2 * chip + ac
    chip_arr = jnp.reshape(chip, (1,)).astype(jnp.int32)
    core_arr = jnp.reshape(ac, (1,)).astype(jnp.int32)

    x2, tgt = x[0], loss_target[0]

    shards = [w_in[0].astype(BF16), w_out[0].astype(BF16), w_up[0].astype(BF16), w_down[0].astype(BF16)]
    f_in, f_out, f_up, f_down = _gather_weights(shards)
    w_in_f = jnp.transpose(f_in, (1, 0, 2)).reshape(D, in_w)
    w_main = jnp.concatenate([w_in_f[:, :3 * fox_w], w_in_f[:, 3 * fox_w + n_fox:]], axis=1)
    w_fg = jnp.pad(w_in_f[:, 3 * fox_w:3 * fox_w + n_fox], ((0, 0), (0, LANES - n_fox)))
    w_out_f = f_out.reshape(D, D)
    w_up_f = jnp.transpose(f_up, (1, 0, 2)).reshape(D, N_CHIPS * w_up.shape[2])
    w_down_f = f_down.reshape(N_CHIPS * w_down.shape[1], D)
    d_ff = w_up_f.shape[1]

    c_all = _allgather8("gather_c", c.reshape(8, D // 8)).reshape(N_DEV, D)
    b_shard = lax.dynamic_slice_in_dim(b_mod, chip * mod_cols, mod_cols, axis=1)
    mod_shard = _mod_fwd(jnp.pad(c_all, ((0, 16 - N_DEV), (0, 0))), w_mod[0], b_shard)[:N_DEV]
    mod_all = _allgather8("gather_mod", mod_shard).reshape(N_CHIPS, 2, N_DEV, mod_cols)[:, 0]
    mod = lax.dynamic_index_in_dim(mod_all, dev, axis=1, keepdims=False).reshape(N_MOD, 1, D)
    sh_a, sc_a, gt_a, sh_m, sc_m, gt_m = [mod[n] for n in range(N_MOD)]

    h = _pre_norm(x2, g_pre_mix, sc_a, sh_a)
    proj = _mm_plain("in_proj", h, w_main, "nn", BF16, tn=_fit(768, main_w))
    fg = _mm_plain("in_proj_gate", h, w_fg, "nn", F32)
    b_pad = jnp.pad(b_forget, ((0, 0), (0, LANES - n_fox)))
    cum, cum_t, cum_b = _fox_gate_fwd(fg, b_pad, n_fox)
    cum_row = cum_t[:n_fox].reshape(n_fox, 1, S)
    fox_o, fox_lse = _fox_fwd(proj, cum_b, cum_row, n_fox)

    cos, sin_signed = _rope_tables(S)
    rq = _rope("rope_fwd", proj, 3 * n_fox, n_swa + n_kv, cos, sin_signed)
    v_first = 3 * n_fox + n_swa + n_kv
    sinks = swa_sinks[0]
    swa_o, swa_lse = _swa_fwd(rq, proj, v_first, sinks, n_swa, n_kv)

    mixcat = jnp.concatenate([fox_o, swa_o], axis=1).astype(BF16)
    mix = _mm_plain("out_proj", mixcat, w_out_f, "nn", F32)
    x1, h2 = _post_mix(x2, mix, g_post_mix, gt_a, g_pre_mlp, sc_m, sh_m)

    tm_u, tn_u = _fit(1024, S), _fit(1024, d_ff)

    def up_epilogue(acc, ex, outs):
        outs[0][...] = acc.astype(BF16)
        r = jnp.maximum(acc, 0.0)
        outs[1][...] = (r * r).astype(BF16)

    ublk = ((S, d_ff), BF16, (tm_u, tn_u), lambda i, j: (i, j))
    u, a = _matmul("mlp_up", h2, w_up_f, "nn", [ublk, ublk], up_epilogue)
    y = _mm_plain("mlp_down", a, w_down_f, "nn", F32)

    dy, dout, loss_part, acc_mlp_post = _loss_and_post_mlp_bwd(x1, y, tgt, g_post_mlp, gt_m)
    loss = lax.psum(loss_part[0, 0], ("x", "y", "c"))

    def du_epilogue(acc, ex, outs):
        outs[0][...] = (acc * (2.0 * jnp.maximum(ex[0][...].astype(F32), 0.0))).astype(BF16)

    du = _matmul("mlp_down_bwd", dy, w_down_f, "nt", [ublk], du_epilogue,
                 extras=[(u, (tm_u, tn_u), lambda i, j: (i, j))])[0]
    g_down = _mm_plain("grad_w_down", a, dy, "tn", BF16)
    tn_s = _fit(1024, w_up.shape[2])
    per = w_up.shape[2] // tn_s

    def shard_epilogue(acc, ex, outs):
        outs[0][0] = acc.astype(BF16)

    g_up = _matmul("grad_w_up", h2, du, "tn",
                   [((N_CHIPS, D, w_up.shape[2]), BF16, (1, _fit(1024, D), tn_s), lambda i, j: (j // per, i, j % per))],
                   shard_epilogue, tn=tn_s)[0]
    dh2 = _mm_plain("mlp_up_bwd", du, w_up_f, "nt", F32)
    dx1, dmix, acc_mid = _pre_mlp_and_post_mix_bwd(dh2, x1, dout, mix, g_pre_mlp, sc_m, g_post_mix, gt_a)

    dmixcat = _mm_plain("out_proj_bwd", dmix, w_out_f, "nt", F32)
    g_out = _mm_plain("grad_w_out", mixcat, dmix, "tn", BF16)

    fdq, fdk, fdv, dcum_row, dcum_q = _fox_bwd(proj, fox_o, dmixcat, fox_lse, cum_b, cum_row, n_fox)
    dcum = jnp.pad(dcum_row.reshape(n_fox, S) + dcum_q[:, :, 0], ((0, LANES - n_fox), (0, 0))).T
    dfg, db_forget = _fox_gate_bwd(dcum, fg, b_pad)

    group_w = (n_swa // n_kv) * HEAD_DIM
    sdq, sdk, sdv, dsink = _swa_bwd(rq, proj, v_first, sinks, swa_o, dmixcat, fox_w // group_w, swa_lse, n_swa, n_kv)
    drq = jnp.concatenate([sdq, jnp.transpose(sdk, (1, 0, 2)).reshape(S, kv_w).astype(BF16)], axis=1)
    d_sq_sk = _rope("rope_bwd", drq, 0, n_swa + n_kv, cos, -sin_signed)
    dsv = jnp.transpose(sdv, (1, 0, 2)).reshape(S, kv_w).astype(BF16)
    dproj = jnp.concatenate([fdq, fdk, fdv, d_sq_sk, dsv], axis=1)

    g_main = _mm_plain("grad_w_in", h, dproj, "tn", BF16, tn=_fit(768, main_w))
    g_fg = _mm_plain("grad_w_in_gate", h, dfg, "tn", BF16)
    dh_gate = _mm_plain("in_proj_gate_bwd", dfg, w_fg, "nt", F32)

    def add_epilogue(acc, ex, outs):
        outs[0][...] = acc + ex[0][...]

    tm_h, tn_h = _fit(1024, S), _fit(1024, D)
    dh = _matmul("in_proj_bwd", dproj, w_main, "nt", [((S, D), F32, (tm_h, tn_h), lambda i, j: (i, j))], add_epilogue,
                 extras=[(dh_gate, (tm_h, tn_h), lambda i, j: (i, j))], tk=_fit(768, main_w))[0]
    grad_x, acc_pre = _pre_mix_bwd(dh, x2, dx1, g_pre_mix, sc_a)

    zero_row = jnp.zeros((1, D), F32)
    tail = jnp.concatenate([db_forget[0:1, :n_fox], dsink[:, 0, :n_swa // n_kv].reshape(1, n_swa),
                            jnp.zeros((1, D - n_fox - n_swa), F32)], axis=1)
    partial = jnp.concatenate([
        acc_pre[0:1], acc_pre[1:2], acc_mid[3:4], acc_mid[0:1], acc_mid[1:2], acc_mlp_post[0:1],
        acc_pre[2:3], acc_mid[4:5], acc_mid[2:3], acc_mlp_post[1:2], tail] + [zero_row] * 5, axis=0)
    gathered = _allgather8("gather_small_grads", partial)
    small = _sum_blocks("sum_small_grads", gathered, N_DEV)
    g_b_mod = small[0:N_MOD].reshape(1, N_MOD * D)
    g_small = {"g_pre_mix": small[6:7], "g_post_mix": small[7:8], "g_pre_mlp": small[8:9], "g_post_mlp": small[9:10],
               "b_forget": small[10:11, :n_fox], "swa_sinks": small[10:11, n_fox:n_fox + n_swa]}

    dmod_all = gathered.reshape(N_DEV, 16, D)[:, :N_MOD].reshape(N_DEV, N_MOD * D)
    dmod_shard = lax.dynamic_slice_in_dim(dmod_all, chip * mod_cols, mod_cols, axis=1)
    g_w_mod, d_w_mod, nm_w_mod, nv_w_mod = _mod_update(c_all.T, dmod_shard, w_mod[0], m_w_mod[0], v_w_mod[0])

    g_in_f = jnp.concatenate([g_main[:, :3 * fox_w], g_fg[:, :n_fox], g_main[:, 3 * fox_w:]], axis=1)
    full = [jnp.transpose(g_in_f.reshape(D, N_CHIPS, in_w // N_CHIPS), (1, 0, 2)),
            g_out.reshape(N_CHIPS, D // N_CHIPS, D), g_up, g_down.reshape(N_CHIPS, d_ff // N_CHIPS, D)]
    names = ["w_in", "w_out", "w_up", "w_down"]
    from_sibling = _pair_exchange(full)
    pair_sums = [_pair_add("pair_add_" + n, core_arr, g, r) for n, g, r in zip(names, full, from_sibling)]
    from_chips = _chip_exchange(pair_sums)
    halves = [_chip_add("chip_add_" + n, chip_arr, p, r) for n, p, r in zip(names, pair_sums, from_chips)]
    g_big = [s.reshape(s.shape[1] * 2, s.shape[2]) for s in _pair_share(halves)]

    grads = {"w_mod": g_w_mod[None], "b_mod": g_b_mod}
    deltas = {"w_mod": d_w_mod[None]}
    new_m = {"w_mod": nm_w_mod[None]}
    new_v = {"w_mod": nv_w_mod[None]}
    weights = {"w_in": (w_in, m_w_in, v_w_in), "w_out": (w_out, m_w_out, v_w_out), "w_up": (w_up, m_w_up, v_w_up),
               "w_down": (w_down, m_w_down, v_w_down)}
    for n, g in zip(names, g_big):
        w, m, v = weights[n]
        d_, m_, v_ = _adam_update("adam_" + n, w[0], g, m[0], v[0])
        grads[n], deltas[n], new_m[n], new_v[n] = g[None], d_[None], m_[None], v_[None]
    small_w = {"b_mod": (b_mod, m_b_mod, v_b_mod), "g_pre_mix": (g_pre_mix, m_g_pre_mix, v_g_pre_mix),
               "g_post_mix": (g_post_mix, m_g_post_mix, v_g_post_mix), "b_forget": (b_forget, m_b_forget, v_b_forget),
               "swa_sinks": (swa_sinks, m_swa_sinks, v_swa_sinks), "g_pre_mlp": (g_pre_mlp, m_g_pre_mlp, v_g_pre_mlp),
               "g_post_mlp": (g_post_mlp, m_g_post_mlp, v_g_post_mlp)}
    g_small["b_mod"] = g_b_mod
    for n, (w, m, v) in small_w.items():
        g = g_small[n]
        grads[n] = g
        deltas[n], new_m[n], new_v[n] = _adam_update("adam_" + n, w, g, m, v)

    order = ["w_mod", "b_mod", "g_pre_mix", "g_post_mix", "w_in", "b_forget", "swa_sinks", "w_out", "g_pre_mlp",
             "g_post_mlp", "w_up", "w_down"]
    return (loss, grad_x[None], *[grads[n] for n in order], *[deltas[n] for n in order],
            *[new_m[n] for n in order], *[new_v[n] for n in order])
```

```python
import functools

import jax
import jax.numpy as jnp
from jax import lax
from jax.experimental import pallas as pl
from jax.experimental.pallas import tpu as pltpu

F32 = jnp.float32
BF16 = jnp.bfloat16
MESH = pl.DeviceIdType.MESH

HEAD_DIM = 128
SWA_BLOCK = 128
ROPE_THETA = 10000.0
NORM_EPS = 1e-6
NEG = -1e30
N_MOD = 6
ADAM_LR = 0.001
ADAM_B1 = 0.9
ADAM_B2 = 0.999
ADAM_EPS = 1e-08
ADAM_WD = 0.01
ADAM_STEP = 10
N_CHIPS = 4
N_DEV = 8
LANES = 128
VMEM_CAP = 60 * 1024 * 1024

_NN = (((1,), (0,)), ((), ()))
_NT = (((1,), (1,)), ((), ()))
_TN = (((0,), (0,)), ((), ()))


def _vmem(nbytes):
    return int(min(VMEM_CAP, nbytes * 5 // 4 + (4 << 20)))


def _nbytes(shape, dtype):
    n = 1
    for s in shape:
        n *= s
    return n * jnp.dtype(dtype).itemsize


def _fit(t, n):
    t = min(t, n)
    assert n % t == 0, (t, n)
    return t


def _matmul(name, a, b, mode, out_defs, epilogue, extras=(), tm=1024, tn=1024, tk=512):
    if mode == "nn":
        (M, K), (K2, N) = a.shape, b.shape
    elif mode == "nt":
        (M, K), (N, K2) = a.shape, b.shape
    else:
        (K, M), (K2, N) = a.shape, b.shape
    assert K == K2, (a.shape, b.shape, mode)
    tm, tn, tk = _fit(tm, M), _fit(tn, N), _fit(tk, K)
    nk = K // tk
    dims = {"nn": _NN, "nt": _NT, "tn": _TN}[mode]
    a_spec = (pl.BlockSpec((tk, tm), lambda i, j, k: (k, i)) if mode == "tn"
              else pl.BlockSpec((tm, tk), lambda i, j, k: (i, k)))
    b_spec = (pl.BlockSpec((tn, tk), lambda i, j, k: (j, k)) if mode == "nt"
              else pl.BlockSpec((tk, tn), lambda i, j, k: (k, j)))
    n_ex, n_out = len(extras), len(out_defs)

    def body(*refs):
        a_ref, b_ref = refs[0], refs[1]
        ex = refs[2:2 + n_ex]
        outs = refs[2 + n_ex:2 + n_ex + n_out]
        prod = lax.dot_general(a_ref[...], b_ref[...], dims, preferred_element_type=F32)
        if nk == 1:
            epilogue(prod, ex, outs)
        else:
            acc_ref = refs[-1]
            k = pl.program_id(2)

            @pl.when(k == 0)
            def _():
                acc_ref[...] = prod

            @pl.when(k > 0)
            def _():
                acc_ref[...] += prod

            @pl.when(k == nk - 1)
            def _():
                epilogue(acc_ref[...], ex, outs)

    def wrap(f):
        return lambda i, j, k: f(i, j)

    in_specs = [a_spec, b_spec] + [pl.BlockSpec(blk, wrap(f)) for _, blk, f in extras]
    out_specs = [pl.BlockSpec(blk, wrap(f)) for _, _, blk, f in out_defs]
    out_shape = [jax.ShapeDtypeStruct(s, d) for s, d, _, _ in out_defs]
    need = 2 * (tm * tk + tk * tn) * a.dtype.itemsize + 3 * tm * tn * 4
    need += sum(2 * _nbytes(blk, arr.dtype) for arr, blk, _ in extras)
    need += sum(2 * _nbytes(blk, d) for _, d, blk, _ in out_defs)
    res = pl.pallas_call(
        body, name=name, grid=(M // tm, N // tn, nk),
        in_specs=in_specs, out_specs=out_specs, out_shape=out_shape,
        scratch_shapes=[pltpu.VMEM((tm, tn), F32)] if nk > 1 else [],
        compiler_params=pltpu.CompilerParams(
            dimension_semantics=("parallel", "parallel", "arbitrary"), vmem_limit_bytes=_vmem(need)),
    )(a, b, *[arr for arr, _, _ in extras])
    return res


def _mm_plain(name, a, b, mode, out_dtype, **tiles):
    if mode == "nn":
        M, N = a.shape[0], b.shape[1]
    elif mode == "nt":
        M, N = a.shape[0], b.shape[0]
    else:
        M, N = a.shape[1], b.shape[1]
    tm, tn = _fit(tiles.get("tm", 1024), M), _fit(tiles.get("tn", 1024), N)

    def epi(acc, ex, outs):
        outs[0][...] = acc.astype(out_dtype)

    return _matmul(name, a, b, mode, [((M, N), out_dtype, (tm, tn), lambda i, j: (i, j))], epi, **tiles)[0]


def _rstd(v):
    return lax.rsqrt(jnp.mean(v * v, axis=-1, keepdims=True) + NORM_EPS)


def _row_call(name, body, row_ins, vec_ins, row_outs, acc_outs, S, D, tr):
    tr = _fit(tr, S)
    row_spec = pl.BlockSpec((tr, D), lambda r: (r, 0))
    vec_spec = pl.BlockSpec((1, D), lambda r: (0, 0))
    in_specs = [row_spec] * len(row_ins) + [vec_spec] * len(vec_ins)
    out_specs = [row_spec] * len(row_outs) + [pl.BlockSpec(shp, lambda r: (0, 0)) for shp in acc_outs]
    out_shape = [jax.ShapeDtypeStruct((S, D), d) for d in row_outs] + [jax.ShapeDtypeStruct(shp, F32) for shp in acc_outs]
    need = sum(2 * tr * D * a.dtype.itemsize for a in row_ins) + sum(2 * tr * D * jnp.dtype(d).itemsize for d in row_outs)
    need += 8 * tr * D * 4
    return pl.pallas_call(
        body, name=name, grid=(S // tr,), in_specs=in_specs, out_specs=out_specs, out_shape=out_shape,
        compiler_params=pltpu.CompilerParams(dimension_semantics=("arbitrary",), vmem_limit_bytes=_vmem(need)),
    )(*row_ins, *vec_ins)


def _acc_rows(ref, rows):
    @pl.when(pl.program_id(0) == 0)
    def _():
        ref[...] = jnp.zeros_like(ref)
    for n, r in enumerate(rows):
        ref[n:n + 1, :] += r


def _pre_norm(x, g, sc, sh):
    S, D = x.shape

    def body(x_ref, g_ref, sc_ref, sh_ref, h_ref):
        xv = x_ref[...]
        xn = xv * _rstd(xv)
        h_ref[...] = (xn * g_ref[...] * (1.0 + sc_ref[...]) + sh_ref[...]).astype(BF16)

    return _row_call("pre_norm_mix", body, [x], [g, sc, sh], [BF16], [], S, D, 256)[0]


def _post_mix(x, mix, g_post, gt, g_pre, sc, sh):
    S, D = x.shape

    def body(x_ref, mix_ref, gp_ref, gt_ref, g2_ref, sc_ref, sh_ref, x1_ref, h2_ref):
        mv = mix_ref[...]
        x1 = x_ref[...] + gt_ref[...] * (mv * _rstd(mv) * gp_ref[...])
        x1_ref[...] = x1
        h2_ref[...] = (x1 * _rstd(x1) * g2_ref[...] * (1.0 + sc_ref[...]) + sh_ref[...]).astype(BF16)

    return _row_call("post_mix_pre_mlp", body, [x, mix], [g_post, gt, g_pre, sc, sh], [F32, BF16], [], S, D, 256)


def _loss_and_post_mlp_bwd(x1, y, target, g_post, gt):
    S, D = x1.shape

    def body(x1_ref, y_ref, t_ref, g_ref, gt_ref, dy_ref, dout_ref, loss_ref, acc_ref):
        yv = y_ref[...]
        r = _rstd(yv)
        yh = yv * r
        n = yh * g_ref[...]
        diff = x1_ref[...] + gt_ref[...] * n - t_ref[...]
        dout = diff * (1.0 / D)
        dout_ref[...] = dout
        dn = dout * gt_ref[...]
        dyh = dn * g_ref[...]
        dy_ref[...] = (r * (dyh - yh * jnp.mean(dyh * yh, axis=-1, keepdims=True))).astype(BF16)
        _acc_rows(acc_ref, [jnp.sum(dout * n, axis=0, keepdims=True), jnp.sum(dn * yh, axis=0, keepdims=True)])

        @pl.when(pl.program_id(0) == 0)
        def _():
            loss_ref[...] = jnp.zeros_like(loss_ref)
        loss_ref[...] += jnp.full(loss_ref.shape, (0.5 / D) * jnp.sum(diff * diff), F32)

    return _row_call("loss_post_mlp_bwd", body, [x1, y, target], [g_post, gt], [BF16, F32],
                     [(8, LANES), (8, D)], S, D, 128)


def _pre_mlp_and_post_mix_bwd(dh2, x1, dout, mix, g_pre, sc, g_post, gt):
    S, D = x1.shape

    def body(dh_ref, x1_ref, dout_ref, mix_ref, g_ref, sc_ref, gp_ref, gt_ref, dx1_ref, dmix_ref, acc_ref):
        dh = dh_ref[...]
        x1v = x1_ref[...]
        r3 = _rstd(x1v)
        xn = x1v * r3
        dxn = dh * (1.0 + sc_ref[...]) * g_ref[...]
        dx1 = dout_ref[...] + r3 * (dxn - xn * jnp.mean(dxn * xn, axis=-1, keepdims=True))
        dx1_ref[...] = dx1
        mv = mix_ref[...]
        r2 = _rstd(mv)
        mh = mv * r2
        dn = dx1 * gt_ref[...]
        dmh = dn * gp_ref[...]
        dmix_ref[...] = (r2 * (dmh - mh * jnp.mean(dmh * mh, axis=-1, keepdims=True))).astype(BF16)
        _acc_rows(acc_ref, [
            jnp.sum(dh, axis=0, keepdims=True),
            jnp.sum(dh * xn * g_ref[...], axis=0, keepdims=True),
            jnp.sum(dh * (1.0 + sc_ref[...]) * xn, axis=0, keepdims=True),
            jnp.sum(dx1 * mh * gp_ref[...], axis=0, keepdims=True),
            jnp.sum(dn * mh, axis=0, keepdims=True)])

    return _row_call("pre_mlp_post_mix_bwd", body, [dh2, x1, dout, mix], [g_pre, sc, g_post, gt], [F32, BF16],
                     [(8, D)], S, D, 128)


def _pre_mix_bwd(dh, x, dx1, g_pre, sc):
    S, D = x.shape

    def body(dh_ref, x_ref, dx1_ref, g_ref, sc_ref, gx_ref, acc_ref):
        dhv = dh_ref[...]
        xv = x_ref[...]
        r = _rstd(xv)
        xn = xv * r
        dxn = dhv * (1.0 + sc_ref[...]) * g_ref[...]
        gx_ref[...] = dx1_ref[...] + r * (dxn - xn * jnp.mean(dxn * xn, axis=-1, keepdims=True))
        _acc_rows(acc_ref, [
            jnp.sum(dhv, axis=0, keepdims=True),
            jnp.sum(dhv * xn * g_ref[...], axis=0, keepdims=True),
            jnp.sum(dhv * (1.0 + sc_ref[...]) * xn, axis=0, keepdims=True)])

    return _row_call("pre_mix_bwd", body, [dh, x, dx1], [g_pre, sc], [F32], [(8, D)], S, D, 128)


CUM_BLOCK = 256


def _tri(n, upper):
    r = lax.broadcasted_iota(jnp.int32, (n, n), 0)
    c = lax.broadcasted_iota(jnp.int32, (n, n), 1)
    return ((c >= r) if upper else (c <= r)).astype(F32)


def _fox_gate_fwd(fg, b_pad, n_fox):
    S = fg.shape[0]
    cb = _fit(CUM_BLOCK, S)

    def body(fg_ref, b_ref, cum_ref, cumt_ref, cumb_ref):
        low = _tri(cb, False)
        carry = jnp.zeros((1, LANES), F32)
        for n in range(S // cb):
            z = fg_ref[n * cb:(n + 1) * cb, :] + b_ref[...]
            logf = jnp.minimum(z, 0.0) - jnp.log(1.0 + jnp.exp(-jnp.abs(z)))
            blk = jnp.dot(low, logf, precision=lax.Precision.HIGHEST, preferred_element_type=F32) + carry
            cum_ref[n * cb:(n + 1) * cb, :] = blk
            carry = blk[cb - 1:cb, :]
        cum = cum_ref[...]
        cumt_ref[...] = cum.T
        for h in range(n_fox):
            cumb_ref[h] = jnp.broadcast_to(cum[:, h:h + 1], (S, LANES))

    return pl.pallas_call(
        body, name="fox_gate_fwd",
        out_shape=[jax.ShapeDtypeStruct((S, LANES), F32), jax.ShapeDtypeStruct((LANES, S), F32),
                   jax.ShapeDtypeStruct((n_fox, S, LANES), F32)],
        compiler_params=pltpu.CompilerParams(vmem_limit_bytes=_vmem((4 + 2 * n_fox) * S * LANES * 4)),
    )(fg, b_pad)


def _fox_gate_bwd(dcum_k, dcum_q, fg, b_pad):
    S = fg.shape[0]
    n_fox = dcum_q.shape[0]
    cb = _fit(CUM_BLOCK, S)

    def body(dk_ref, dq_ref, fg_ref, b_ref, dfg_ref, db_ref, dc_ref):
        lane = lax.broadcasted_iota(jnp.int32, (S, LANES), 1)
        dc = dk_ref[...].T
        for h in range(n_fox):
            dc = dc + jnp.where(lane == h, dq_ref[h], 0.0)
        dc_ref[...] = dc
        up = _tri(cb, True)
        carry = jnp.zeros((1, LANES), F32)
        db = jnp.zeros((1, LANES), F32)
        for n in reversed(range(S // cb)):
            blk = jnp.dot(up, dc_ref[n * cb:(n + 1) * cb, :], precision=lax.Precision.HIGHEST,
                          preferred_element_type=F32) + carry
            carry = blk[0:1, :]
            z = fg_ref[n * cb:(n + 1) * cb, :] + b_ref[...]
            dfg = blk * (1.0 / (1.0 + jnp.exp(z)))
            dfg_ref[n * cb:(n + 1) * cb, :] = dfg.astype(BF16)
            db = db + jnp.sum(dfg, axis=0, keepdims=True)
        db_ref[...] = jnp.broadcast_to(db, db_ref.shape)

    return pl.pallas_call(
        body, name="fox_gate_bwd",
        out_shape=[jax.ShapeDtypeStruct((S, LANES), BF16), jax.ShapeDtypeStruct((8, LANES), F32)],
        scratch_shapes=[pltpu.VMEM((S, LANES), F32)],
        compiler_params=pltpu.CompilerParams(vmem_limit_bytes=_vmem((8 + 2 * n_fox) * S * LANES * 4)),
    )(dcum_k, dcum_q, fg, b_pad)


FOX_TILE = 512


def _fox_scores(q, k, cq, ck, q0, k0, tq, tk, scale):
    s = lax.dot_general(q, k, _NT, preferred_element_type=F32) * scale + cq - ck
    row = q0 + lax.broadcasted_iota(jnp.int32, (tq, tk), 0)
    col = k0 + lax.broadcasted_iota(jnp.int32, (tq, tk), 1)
    return jnp.where(col <= row, s, NEG)


def _fox_fwd(proj, cum_b, cum_row, n_fox):
    S = proj.shape[0]
    t = _fit(FOX_TILE, S)
    nq = S // t
    scale = HEAD_DIM ** -0.5

    def body(q_ref, k_ref, v_ref, cq_ref, ck_ref, o_ref, lse_ref):
        def q_block(qi, _):
            q0 = pl.multiple_of(qi * t, t)
            q = q_ref[pl.ds(q0, t), :]
            cq = cq_ref[0, pl.ds(q0, t), :][:, :1]

            def kv_block(j, carry):
                m, l, acc = carry
                k0 = pl.multiple_of(j * t, t)
                s = _fox_scores(q, k_ref[pl.ds(k0, t), :], cq, ck_ref[0, :, pl.ds(k0, t)], q0, k0, t, t, scale)
                m_new = jnp.maximum(m, jnp.max(s, axis=-1, keepdims=True))
                alpha = jnp.exp(m - m_new)
                p = jnp.exp(s - m_new)
                l = alpha * l + jnp.sum(p, axis=-1, keepdims=True)
                acc = alpha * acc + jnp.dot(p.astype(BF16), v_ref[pl.ds(k0, t), :], preferred_element_type=F32)
                return m_new, l, acc

            init = (jnp.full((t, 1), NEG, F32), jnp.zeros((t, 1), F32), jnp.zeros((t, HEAD_DIM), F32))
            m, l, acc = lax.fori_loop(0, qi + 1, kv_block, init)
            o_ref[pl.ds(q0, t), :] = acc / l
            lse_ref[0, pl.ds(q0, t), :] = jnp.broadcast_to(m + jnp.log(l), (t, LANES))
            return 0

        lax.fori_loop(0, nq, q_block, 0)

    col = lambda off: pl.BlockSpec((S, HEAD_DIM), lambda h: (0, off + h))
    per_head = pl.BlockSpec((1, S, LANES), lambda h: (h, 0, 0))
    return pl.pallas_call(
        body, name="fox_fwd", grid=(n_fox,),
        in_specs=[col(0), col(n_fox), col(2 * n_fox), per_head, pl.BlockSpec((1, 1, S), lambda h: (h, 0, 0))],
        out_specs=[pl.BlockSpec((S, HEAD_DIM), lambda h: (0, h)), per_head],
        out_shape=[jax.ShapeDtypeStruct((S, n_fox * HEAD_DIM), F32), jax.ShapeDtypeStruct((n_fox, S, LANES), F32)],
        compiler_params=pltpu.CompilerParams(dimension_semantics=("parallel",),
                                             vmem_limit_bytes=_vmem(16 * S * HEAD_DIM * 4 + 12 * t * t * 4)),
    )(proj, proj, proj, cum_b, cum_row)


def _fox_bwd(proj, o, do, lse_b, cum_b, cum_row, n_fox):
    S = proj.shape[0]
    t = _fit(FOX_TILE, S)
    nq = S // t
    scale = HEAD_DIM ** -0.5

    def body(q_ref, k_ref, v_ref, o_ref, do_ref, lse_ref, cq_ref, ck_ref, dq_ref, dk_ref, dv_ref, dc_ref, dcq_ref,
             dq_acc, delta_ref):
        dq_acc[...] = jnp.zeros_like(dq_acc)
        dcq_ref[...] = jnp.zeros_like(dcq_ref)

        def delta_block(qi, _):
            q0 = pl.multiple_of(qi * t, t)
            d = jnp.sum(do_ref[pl.ds(q0, t), :] * o_ref[pl.ds(q0, t), :], axis=-1, keepdims=True)
            delta_ref[pl.ds(q0, t), :] = jnp.broadcast_to(d, (t, LANES))
            return 0

        lax.fori_loop(0, nq, delta_block, 0)

        def kv_block(j, _):
            k0 = pl.multiple_of(j * t, t)
            k = k_ref[pl.ds(k0, t), :]
            v = v_ref[pl.ds(k0, t), :]
            ck = ck_ref[0, :, pl.ds(k0, t)]

            def q_block(qi, carry):
                dk, dv, dc = carry
                q0 = pl.multiple_of(qi * t, t)
                q = q_ref[pl.ds(q0, t), :]
                dov = do_ref[pl.ds(q0, t), :].astype(BF16)
                s = _fox_scores(q, k, cq_ref[0, pl.ds(q0, t), :][:, :1], ck, q0, k0, t, t, scale)
                p = jnp.exp(s - lse_ref[0, pl.ds(q0, t), :][:, :1])
                dp = lax.dot_general(dov, v, _NT, preferred_element_type=F32)
                ds = p * (dp - delta_ref[pl.ds(q0, t), :][:, :1])
                dsb = ds.astype(BF16)
                dv = dv + lax.dot_general(p.astype(BF16), dov, _TN, preferred_element_type=F32)
                dk = dk + lax.dot_general(dsb, q, _TN, preferred_element_type=F32)
                dq_acc[pl.ds(q0, t), :] += jnp.dot(dsb, k, preferred_element_type=F32)
                dc = dc - jnp.sum(ds, axis=0, keepdims=True)
                dcq_ref[0, pl.ds(q0, t), :] += jnp.broadcast_to(jnp.sum(ds, axis=1, keepdims=True), (t, LANES))
                return dk, dv, dc

            init = (jnp.zeros((t, HEAD_DIM), F32), jnp.zeros((t, HEAD_DIM), F32), jnp.zeros((1, t), F32))
            dk, dv, dc = lax.fori_loop(j, nq, q_block, init)
            dk_ref[pl.ds(k0, t), :] = (dk * scale).astype(BF16)
            dv_ref[pl.ds(k0, t), :] = dv.astype(BF16)
            dc_ref[0, :, pl.ds(k0, t)] = dc
            return 0

        lax.fori_loop(0, nq, kv_block, 0)
        dq_ref[...] = (dq_acc[...] * scale).astype(BF16)

    col = lambda off: pl.BlockSpec((S, HEAD_DIM), lambda h: (0, off + h))
    per_head = pl.BlockSpec((1, S, LANES), lambda h: (h, 0, 0))
    row = pl.BlockSpec((1, 1, S), lambda h: (h, 0, 0))
    grad = jax.ShapeDtypeStruct((S, n_fox * HEAD_DIM), BF16)
    return pl.pallas_call(
        body, name="fox_bwd", grid=(n_fox,),
        in_specs=[col(0), col(n_fox), col(2 * n_fox), col(0), col(0), per_head, per_head, row],
        out_specs=[col(0), col(0), col(0), row, per_head],
        out_shape=[grad, grad, grad, jax.ShapeDtypeStruct((n_fox, 1, S), F32), jax.ShapeDtypeStruct((n_fox, S, LANES), F32)],
        scratch_shapes=[pltpu.VMEM((S, HEAD_DIM), F32), pltpu.VMEM((S, LANES), F32)],
        compiler_params=pltpu.CompilerParams(dimension_semantics=("parallel",),
                                             vmem_limit_bytes=_vmem(24 * S * HEAD_DIM * 4 + 16 * t * t * 4)),
    )(proj, proj, proj, o, do, lse_b, cum_b, cum_row)


def _rope_tables(S):
    half = HEAD_DIM // 2
    inv_freq = 1.0 / (ROPE_THETA ** (jnp.arange(half, dtype=F32) * (2.0 / HEAD_DIM)))
    ang = jnp.arange(S).astype(F32)[:, None] * inv_freq[None, :]
    cos, sin = jnp.cos(ang), jnp.sin(ang)
    return jnp.concatenate([cos, cos], axis=-1), jnp.concatenate([-sin, sin], axis=-1)


def _rope(name, src, first_block, n_blocks, cos, sin_signed):
    S = src.shape[0]

    def body(x_ref, cos_ref, sin_ref, o_ref):
        xv = x_ref[...].astype(F32)
        o_ref[...] = (xv * cos_ref[...] + pltpu.roll(xv, HEAD_DIM // 2, 1) * sin_ref[...]).astype(BF16)

    table = pl.BlockSpec((S, HEAD_DIM), lambda n: (0, 0))
    return pl.pallas_call(
        body, name=name, grid=(n_blocks,),
        in_specs=[pl.BlockSpec((S, HEAD_DIM), lambda n: (0, first_block + n)), table, table],
        out_specs=pl.BlockSpec((S, HEAD_DIM), lambda n: (0, n)),
        out_shape=jax.ShapeDtypeStruct((S, n_blocks * HEAD_DIM), BF16),
        compiler_params=pltpu.CompilerParams(dimension_semantics=("parallel",),
                                             vmem_limit_bytes=_vmem(12 * S * HEAD_DIM * 4)),
    )(src, cos, sin_signed)


def _swa_tile(q_ref, kp_ref, kc_ref, n, group, scale):
    B = SWA_BLOCK
    qs = jnp.concatenate([q_ref[:, g * HEAD_DIM:(g + 1) * HEAD_DIM] for g in range(group)], axis=0)
    kcat = jnp.concatenate([kp_ref[...], kc_ref[...]], axis=0)
    s = lax.dot_general(qs, kcat, _NT, preferred_element_type=F32) * scale
    qi = lax.broadcasted_iota(jnp.int32, (group * B, 2 * B), 0) % B
    kj = lax.broadcasted_iota(jnp.int32, (group * B, 2 * B), 1)
    diff = qi + B - kj
    mask = (diff >= 0) & (diff < B) & ((n * B + kj - B) >= 0)
    return qs, kcat, jnp.where(mask, s, NEG)


def _swa_sink_col(sink_ref, kv, group):
    head = lax.broadcasted_iota(jnp.int32, (group * SWA_BLOCK, 1), 0) // SWA_BLOCK
    col = jnp.zeros((group * SWA_BLOCK, 1), F32)
    for g in range(group):
        col = jnp.where(head == g, sink_ref[kv * group + g], col)
    return col


def _swa_specs(n_kv, group, q_first, k_first, v_first):
    B = SWA_BLOCK
    prev = lambda n: jnp.maximum(n - 1, 0)
    return [
        pl.BlockSpec((B, group * HEAD_DIM), lambda kv, n: (n, q_first + kv)),
        pl.BlockSpec((B, HEAD_DIM), lambda kv, n: (prev(n), k_first + kv)),
        pl.BlockSpec((B, HEAD_DIM), lambda kv, n: (n, k_first + kv)),
        pl.BlockSpec((B, HEAD_DIM), lambda kv, n: (prev(n), v_first + kv)),
        pl.BlockSpec((B, HEAD_DIM), lambda kv, n: (n, v_first + kv)),
    ]


def _swa_fwd(rq, proj, v_first, sinks, n_q, n_kv):
    S = rq.shape[0]
    B = SWA_BLOCK
    group = n_q // n_kv
    scale = HEAD_DIM ** -0.5

    def body(q_ref, kp_ref, kc_ref, vp_ref, vc_ref, sink_ref, o_ref, lse_ref):
        kv, n = pl.program_id(0), pl.program_id(1)
        _, _, s = _swa_tile(q_ref, kp_ref, kc_ref, n, group, scale)
        sink = _swa_sink_col(sink_ref, kv, group)
        m = jnp.maximum(jnp.max(s, axis=-1, keepdims=True), sink)
        p = jnp.exp(s - m)
        denom = jnp.sum(p, axis=-1, keepdims=True) + jnp.exp(sink - m)
        vcat = jnp.concatenate([vp_ref[...], vc_ref[...]], axis=0)
        o = jnp.dot((p / denom).astype(BF16), vcat, preferred_element_type=F32)
        lse = m + jnp.log(denom)
        for g in range(group):
            o_ref[:, g * HEAD_DIM:(g + 1) * HEAD_DIM] = o[g * B:(g + 1) * B, :]
            lse_ref[0, :, g * LANES:(g + 1) * LANES] = jnp.broadcast_to(lse[g * B:(g + 1) * B, :], (B, LANES))

    specs = _swa_specs(n_kv, group, 0, n_q, v_first)
    q_blk = pl.BlockSpec((B, group * HEAD_DIM), lambda kv, n: (n, kv))
    return pl.pallas_call(
        body, name="swa_fwd", grid=(n_kv, S // B),
        in_specs=specs + [pl.BlockSpec(memory_space=pltpu.SMEM)],
        out_specs=[q_blk, pl.BlockSpec((1, B, group * LANES), lambda kv, n: (kv, n, 0))],
        out_shape=[jax.ShapeDtypeStruct((S, n_q * HEAD_DIM), F32), jax.ShapeDtypeStruct((n_kv, S, group * LANES), F32)],
        compiler_params=pltpu.CompilerParams(dimension_semantics=("parallel", "arbitrary")),
    )(rq, rq, rq, proj, proj, sinks)


def _swa_bwd(rq, proj, v_first, sinks, o, do, do_first, lse_b, n_q, n_kv):
    S = rq.shape[0]
    B = SWA_BLOCK
    group = n_q // n_kv
    scale = HEAD_DIM ** -0.5

    def body(q_ref, kp_ref, kc_ref, vp_ref, vc_ref, o_ref, do_ref, lse_ref, sink_ref,
             dq_ref, dk_ref, dv_ref, dsink_ref):
        kv, n = pl.program_id(0), pl.program_id(1)

        @pl.when(n == 0)
        def _():
            dk_ref[...] = jnp.zeros_like(dk_ref)
            dv_ref[...] = jnp.zeros_like(dv_ref)
            dsink_ref[...] = jnp.zeros_like(dsink_ref)

        qs, kcat, s = _swa_tile(q_ref, kp_ref, kc_ref, n, group, scale)
        sink = _swa_sink_col(sink_ref, kv, group)
        stack = lambda ref, w: jnp.concatenate([ref[:, g * w:(g + 1) * w] for g in range(group)], axis=0)
        lse = jnp.concatenate([lse_ref[0, :, g * LANES:g * LANES + 1] for g in range(group)], axis=0)
        do32 = stack(do_ref, HEAD_DIM)
        delta = jnp.sum(do32 * stack(o_ref, HEAD_DIM), axis=-1, keepdims=True)
        dov = do32.astype(BF16)
        p = jnp.exp(s - lse)
        vcat = jnp.concatenate([vp_ref[...], vc_ref[...]], axis=0)
        dp = lax.dot_general(dov, vcat, _NT, preferred_element_type=F32)
        ds = p * (dp - delta)
        dsb = ds.astype(BF16)
        dq = jnp.dot(dsb, kcat, preferred_element_type=F32) * scale
        for g in range(group):
            dq_ref[:, g * HEAD_DIM:(g + 1) * HEAD_DIM] = dq[g * B:(g + 1) * B, :].astype(BF16)
        dkcat = lax.dot_general(dsb, qs, _TN, preferred_element_type=F32) * scale
        dvcat = lax.dot_general(p.astype(BF16), dov, _TN, preferred_element_type=F32)
        prev0 = pl.multiple_of(jnp.maximum(n - 1, 0) * B, B)
        cur0 = pl.multiple_of(n * B, B)
        dk_ref[0, pl.ds(prev0, B), :] += dkcat[:B, :]
        dk_ref[0, pl.ds(cur0, B), :] += dkcat[B:, :]
        dv_ref[0, pl.ds(prev0, B), :] += dvcat[:B, :]
        dv_ref[0, pl.ds(cur0, B), :] += dvcat[B:, :]
        dsk = -jnp.exp(sink - lse) * delta
        lane = lax.broadcasted_iota(jnp.int32, (1, LANES), 1)
        row = jnp.zeros((1, LANES), F32)
        for g in range(group):
            row = row + jnp.where(lane == g, jnp.sum(dsk[g * B:(g + 1) * B, :]), 0.0)
        dsink_ref[0, 0:1, :] += row

    specs = _swa_specs(n_kv, group, 0, n_q, v_first)
    q_blk = pl.BlockSpec((B, group * HEAD_DIM), lambda kv, n: (n, kv))
    acc = pl.BlockSpec((1, S, HEAD_DIM), lambda kv, n: (kv, 0, 0))
    return pl.pallas_call(
        body, name="swa_bwd", grid=(n_kv, S // B),
        in_specs=specs + [q_blk, pl.BlockSpec((B, group * HEAD_DIM), lambda kv, n: (n, do_first + kv)),
                          pl.BlockSpec((1, B, group * LANES), lambda kv, n: (kv, n, 0)),
                          pl.BlockSpec(memory_space=pltpu.SMEM)],
        out_specs=[q_blk, acc, acc, pl.BlockSpec((1, 8, LANES), lambda kv, n: (kv, 0, 0))],
        out_shape=[jax.ShapeDtypeStruct((S, n_q * HEAD_DIM), BF16), jax.ShapeDtypeStruct((n_kv, S, HEAD_DIM), F32),
                   jax.ShapeDtypeStruct((n_kv, S, HEAD_DIM), F32), jax.ShapeDtypeStruct((n_kv, 8, LANES), F32)],
        compiler_params=pltpu.CompilerParams(dimension_semantics=("parallel", "arbitrary")),
    )(rq, rq, rq, proj, proj, o, do, lse_b, sinks)


def _adamw(w, g, m, v):
    m = ADAM_B1 * m + (1.0 - ADAM_B1) * g
    v = ADAM_B2 * v + (1.0 - ADAM_B2) * (g * g)
    m_hat = m / (1.0 - ADAM_B1 ** ADAM_STEP)
    v_hat = v / (1.0 - ADAM_B2 ** ADAM_STEP)
    delta = -ADAM_LR * (m_hat / (jnp.sqrt(v_hat) + ADAM_EPS) + ADAM_WD * w)
    return delta, m, v


def _mod_fwd(cond_in, w_mod, b_shard):
    R, D = cond_in.shape
    cols = w_mod.shape[1]
    tn = _fit(512, cols)

    def body(c_ref, w_ref, b_ref, o_ref):
        cv = c_ref[...]
        cond = (cv / (1.0 + jnp.exp(-cv))).astype(BF16)
        o_ref[...] = jnp.dot(cond, w_ref[...].astype(BF16), preferred_element_type=F32) + b_ref[...]

    return pl.pallas_call(
        body, name="mod_fwd", grid=(cols // tn,),
        in_specs=[pl.BlockSpec((R, D), lambda j: (0, 0)), pl.BlockSpec((D, tn), lambda j: (0, j)),
                  pl.BlockSpec((1, tn), lambda j: (0, j))],
        out_specs=pl.BlockSpec((R, tn), lambda j: (0, j)),
        out_shape=jax.ShapeDtypeStruct((R, cols), F32),
        compiler_params=pltpu.CompilerParams(dimension_semantics=("parallel",), vmem_limit_bytes=_vmem(3 * D * tn * 4)),
    )(cond_in, w_mod, b_shard)


def _mod_update(c_t, dmod, w, m, v):
    D, nb = c_t.shape
    cols = w.shape[1]
    tn = _fit(256, cols)

    def body(c_ref, d_ref, w_ref, m_ref, v_ref, g_ref, dl_ref, nm_ref, nv_ref):
        cv = c_ref[...]
        cond = cv / (1.0 + jnp.exp(-cv))
        g = jnp.zeros((D, tn), F32)
        for b in range(nb):
            g = g + cond[:, b:b + 1] * d_ref[b:b + 1, :]
        g_ref[...] = g
        dl_ref[...], nm_ref[...], nv_ref[...] = _adamw(w_ref[...], g, m_ref[...], v_ref[...])

    blk = pl.BlockSpec((D, tn), lambda j: (0, j))
    out = jax.ShapeDtypeStruct((D, cols), F32)
    return pl.pallas_call(
        body, name="mod_update", grid=(cols // tn,),
        in_specs=[pl.BlockSpec((D, nb), lambda j: (0, 0)), pl.BlockSpec((nb, tn), lambda j: (0, j)), blk, blk, blk],
        out_specs=[blk] * 4, out_shape=[out] * 4,
        compiler_params=pltpu.CompilerParams(dimension_semantics=("parallel",), vmem_limit_bytes=_vmem(18 * D * tn * 4)),
    )(c_t, dmod, w, m, v)


def _adam_update(name, w, g, m, v):
    R, C = w.shape
    tr = _fit(256, R)

    def body(w_ref, g_ref, m_ref, v_ref, dl_ref, nm_ref, nv_ref):
        dl_ref[...], nm_ref[...], nv_ref[...] = _adamw(w_ref[...], g_ref[...], m_ref[...], v_ref[...])

    blk = pl.BlockSpec((tr, C), lambda r: (r, 0))
    out = jax.ShapeDtypeStruct((R, C), F32)
    padded = -(-C // LANES) * LANES
    return pl.pallas_call(
        body, name=name, grid=(R // tr,), in_specs=[blk] * 4, out_specs=[blk] * 3, out_shape=[out] * 3,
        compiler_params=pltpu.CompilerParams(dimension_semantics=("parallel",), vmem_limit_bytes=_vmem(16 * tr * padded * 4)),
    )(w, g, m, v)


def _sum_blocks(name, stacked, n):
    R, C = stacked.shape[0] // n, stacked.shape[1]

    def body(s_ref, o_ref):
        total = s_ref[0:R, :]
        for d in range(1, n):
            total = total + s_ref[d * R:(d + 1) * R, :]
        o_ref[...] = total

    return pl.pallas_call(body, name=name, out_shape=jax.ShapeDtypeStruct((R, C), F32))(stacked)


def _place():
    return lax.axis_index("x"), lax.axis_index("y"), lax.axis_index("c")


def _allgather8(name, block):
    m_per, n = block.shape

    def body(x_ref, out_ref, send_sems, recv_sems, local_sem):
        x, y, c = _place()
        me, sibling = (x, y, c), (x, y, 1 - c)
        chips = [(1 - x, y), (x, 1 - y), (1 - x, 1 - y)]

        def rows(px, py, pc):
            return out_ref.at[pl.ds((4 * px + 2 * py + pc) * m_per, m_per), :]

        def copy(k, blk, to, src=None):
            return pltpu.make_async_remote_copy(
                src_ref=rows(*blk) if src is None else src, dst_ref=rows(*blk),
                send_sem=send_sems.at[k], recv_sem=recv_sems.at[k], device_id=to, device_id_type=MESH)

        mine = pltpu.make_async_copy(x_ref, rows(*me), local_sem)
        mine.start()
        first = [copy(0, me, sibling, src=x_ref)]
        first += [copy(1 + j, me, (*chip, c), src=x_ref) for j, chip in enumerate(chips)]
        for cp in first:
            cp.start()
        passed = [copy(4 + j, (*chip, c), sibling) for j, chip in enumerate(chips)]
        for j, chip in enumerate(chips):
            copy(1 + j, (*chip, c), me).wait_recv()
            passed[j].start()
        copy(0, sibling, me).wait_recv()
        for j, chip in enumerate(chips):
            copy(4 + j, (*chip, 1 - c), me).wait_recv()
        for cp in first + passed:
            cp.wait_send()
        mine.wait()

    return pl.pallas_call(
        body, name=name, out_shape=jax.ShapeDtypeStruct((N_DEV * m_per, n), block.dtype),
        in_specs=[pl.BlockSpec(memory_space=pltpu.VMEM)], out_specs=pl.BlockSpec(memory_space=pltpu.VMEM),
        scratch_shapes=[pltpu.SemaphoreType.DMA((7,)), pltpu.SemaphoreType.DMA((7,)), pltpu.SemaphoreType.DMA],
    )(block)


_ANY = pl.BlockSpec(memory_space=pl.ANY)


def _half(ref, c, rows):
    return ref.at[pl.ds(c * (rows // 2), rows // 2), :]


def _gather_weights(shards):
    nw = len(shards)

    def body(*refs):
        ws, outs = refs[:nw], refs[nw:2 * nw]
        send_sems, recv_sems = refs[2 * nw:]
        x, y, c = _place()
        chips = [(1 - x, y), (x, 1 - y), (1 - x, 1 - y)]
        sibling = (x, y, 1 - c)

        def ici(k, j):
            R = ws[k].shape[0]
            cx, cy = chips[j]
            return pltpu.make_async_remote_copy(
                src_ref=_half(ws[k], c, R), dst_ref=_half(outs[k].at[2 * x + y], c, R),
                send_sem=send_sems.at[6 * k + j], recv_sem=recv_sems.at[6 * k + j],
                device_id=(cx, cy, c), device_id_type=MESH)

        def landed(k, j, hc):
            R = ws[k].shape[0]
            cx, cy = chips[j]
            return _half(outs[k].at[2 * cx + cy], hc, R)

        def d2d(k, j, hc):
            return pltpu.make_async_remote_copy(
                src_ref=landed(k, j, hc), dst_ref=landed(k, j, hc),
                send_sem=send_sems.at[6 * k + 3 + j], recv_sem=recv_sems.at[6 * k + 3 + j],
                device_id=sibling, device_id_type=MESH)

        sends = []
        for k in range(nw):
            for j in range(3):
                cp = ici(k, j)
                cp.start()
                sends.append(cp)
        for k in range(nw):
            for j in range(3):
                R = ws[k].shape[0]
                pltpu.make_async_remote_copy(
                    src_ref=landed(k, j, c), dst_ref=landed(k, j, c),
                    send_sem=send_sems.at[6 * k + j], recv_sem=recv_sems.at[6 * k + j],
                    device_id=sibling, device_id_type=MESH).wait_recv()
                cp = d2d(k, j, c)
                cp.start()
                sends.append(cp)
        for k in range(nw):
            for j in range(3):
                d2d(k, j, 1 - c).wait_recv()
        for cp in sends:
            cp.wait_send()

    return pl.pallas_call(
        body, name="gather_weights",
        out_shape=[jax.ShapeDtypeStruct((N_CHIPS,) + s.shape, s.dtype) for s in shards],
        in_specs=[_ANY] * nw, out_specs=[_ANY] * nw,
        scratch_shapes=[pltpu.SemaphoreType.DMA((6 * nw,)), pltpu.SemaphoreType.DMA((6 * nw,))],
    )(*shards)


def _pair_exchange(grads):
    nw = len(grads)

    def body(*refs):
        gs, outs = refs[:nw], refs[nw:2 * nw]
        send_sems, recv_sems = refs[2 * nw:]
        x, y, c = _place()
        copies = []
        for k in range(nw):
            half = gs[k].shape[1] // 2
            cp = pltpu.make_async_remote_copy(
                src_ref=gs[k].at[:, pl.ds((1 - c) * half, half), :], dst_ref=outs[k],
                send_sem=send_sems.at[k], recv_sem=recv_sems.at[k], device_id=(x, y, 1 - c), device_id_type=MESH)
            cp.start()
            copies.append(cp)
        for cp in copies:
            cp.wait()

    return pl.pallas_call(
        body, name="grad_pair_exchange",
        out_shape=[jax.ShapeDtypeStruct((N_CHIPS, g.shape[1] // 2, g.shape[2]), g.dtype) for g in grads],
        in_specs=[_ANY] * nw, out_specs=[_ANY] * nw,
        scratch_shapes=[pltpu.SemaphoreType.DMA((nw,)), pltpu.SemaphoreType.DMA((nw,))],
    )(*grads)


def _pair_add(name, core, grad, recv):
    n, R, C = grad.shape
    half = R // 2
    tr = _fit(256, half)
    nblk = half // tr

    def body(core_ref, g_ref, r_ref, o_ref):
        o_ref[...] = (g_ref[...].astype(F32) + r_ref[...].astype(F32)).astype(BF16)

    grid_spec = pltpu.PrefetchScalarGridSpec(
        num_scalar_prefetch=1, grid=(n, nblk),
        in_specs=[pl.BlockSpec((1, tr, C), lambda s, r, core_ref: (s, core_ref[0] * nblk + r, 0)),
                  pl.BlockSpec((1, tr, C), lambda s, r, core_ref: (s, r, 0))],
        out_specs=pl.BlockSpec((1, tr, C), lambda s, r, core_ref: (s, r, 0)))
    return pl.pallas_call(
        body, name=name, grid_spec=grid_spec, out_shape=jax.ShapeDtypeStruct((n, half, C), BF16),
        compiler_params=pltpu.CompilerParams(dimension_semantics=("parallel", "parallel")),
    )(core, grad, recv)


def _chip_exchange(sums):
    nw = len(sums)

    def body(*refs):
        ps, outs = refs[:nw], refs[nw:2 * nw]
        send_sems, recv_sems = refs[2 * nw:]
        x, y, c = _place()
        chips = [(1 - x, y), (x, 1 - y), (1 - x, 1 - y)]
        copies = []
        for k in range(nw):
            for j, (cx, cy) in enumerate(chips):
                cp = pltpu.make_async_remote_copy(
                    src_ref=ps[k].at[2 * cx + cy], dst_ref=outs[k].at[j],
                    send_sem=send_sems.at[3 * k + j], recv_sem=recv_sems.at[3 * k + j],
                    device_id=(cx, cy, c), device_id_type=MESH)
                cp.start()
                copies.append(cp)
        for cp in copies:
            cp.wait()

    return pl.pallas_call(
        body, name="grad_chip_exchange",
        out_shape=[jax.ShapeDtypeStruct((3,) + p.shape[1:], p.dtype) for p in sums],
        in_specs=[_ANY] * nw, out_specs=[_ANY] * nw,
        scratch_shapes=[pltpu.SemaphoreType.DMA((3 * nw,)), pltpu.SemaphoreType.DMA((3 * nw,))],
    )(*sums)


def _chip_add(name, chip, sums, recv):
    _, H, C = sums.shape
    tr = _fit(256, H)

    def body(chip_ref, p_ref, r_ref, o_ref):
        total = p_ref[0].astype(F32)
        for j in range(3):
            total = total + r_ref[j].astype(F32)
        o_ref[...] = total

    grid_spec = pltpu.PrefetchScalarGridSpec(
        num_scalar_prefetch=1, grid=(H // tr,),
        in_specs=[pl.BlockSpec((1, tr, C), lambda r, chip_ref: (chip_ref[0], r, 0)),
                  pl.BlockSpec((3, tr, C), lambda r, chip_ref: (0, r, 0))],
        out_specs=pl.BlockSpec((tr, C), lambda r, chip_ref: (r, 0)))
    return pl.pallas_call(
        body, name=name, grid_spec=grid_spec, out_shape=jax.ShapeDtypeStruct((H, C), F32),
        compiler_params=pltpu.CompilerParams(dimension_semantics=("parallel",)),
    )(chip, sums, recv)


def _pair_share(halves):
    nw = len(halves)

    def body(*refs):
        hs, outs = refs[:nw], refs[nw:2 * nw]
        send_sems, recv_sems = refs[2 * nw:]
        x, y, c = _place()
        copies = []
        for k in range(nw):
            cp = pltpu.make_async_remote_copy(
                src_ref=hs[k], dst_ref=outs[k], send_sem=send_sems.at[k], recv_sem=recv_sems.at[k],
                device_id=(x, y, 1 - c), device_id_type=MESH)
            cp.start()
            copies.append(cp)
        for cp in copies:
            cp.wait()

    return pl.pallas_call(
        body, name="grad_pair_share",
        out_shape=[jax.ShapeDtypeStruct(h.shape, h.dtype) for h in halves],
        in_specs=[_ANY] * nw, out_specs=[_ANY] * nw,
        scratch_shapes=[pltpu.SemaphoreType.DMA((nw,)), pltpu.SemaphoreType.DMA((nw,))],
    )(*halves)


def _adam_halves(name, core, w, g_own, g_other, m, v):
    R, C = w.shape
    H = R // 2
    tr = _fit(256, H)
    nblk = H // tr

    def body(core_ref, w_ref, go_ref, gr_ref, m_ref, v_ref, g_ref, dl_ref, nm_ref, nv_ref):
        own = (pl.program_id(0) // nblk) == core_ref[0]
        g = jnp.where(own, go_ref[...], gr_ref[...])
        g_ref[...] = g
        dl_ref[...], nm_ref[...], nv_ref[...] = _adamw(w_ref[...], g, m_ref[...], v_ref[...])

    blk = pl.BlockSpec((tr, C), lambda r, core_ref: (r, 0))
    half = pl.BlockSpec((tr, C), lambda r, core_ref: (r % nblk, 0))
    out = jax.ShapeDtypeStruct((R, C), F32)
    padded = -(-C // LANES) * LANES
    grid_spec = pltpu.PrefetchScalarGridSpec(
        num_scalar_prefetch=1, grid=(R // tr,), in_specs=[blk, half, half, blk, blk], out_specs=[blk] * 4)
    return pl.pallas_call(
        body, name=name, grid_spec=grid_spec, out_shape=[out] * 4,
        compiler_params=pltpu.CompilerParams(dimension_semantics=("parallel",), vmem_limit_bytes=_vmem(20 * tr * padded * 4)),
    )(core, w, g_own, g_other, m, v)


def kernel(x, c, w_mod, b_mod, g_pre_mix, g_post_mix, w_in, b_forget, swa_sinks, w_out, g_pre_mlp, g_post_mlp, w_up, w_down, loss_target, m_w_mod, m_b_mod, m_g_pre_mix, m_g_post_mix, m_w_in, m_b_forget, m_swa_sinks, m_w_out, m_g_pre_mlp, m_g_post_mlp, m_w_up, m_w_down, v_w_mod, v_b_mod, v_g_pre_mix, v_g_post_mix, v_w_in, v_b_forget, v_swa_sinks, v_w_out, v_g_pre_mlp, v_g_post_mlp, v_w_up, v_w_down):
    S, D = x.shape[1], x.shape[2]
    n_heads = D // HEAD_DIM
    n_fox = n_heads // 2
    n_swa = n_heads - n_fox
    n_kv = max(1, n_swa // 4)
    fox_w, swa_w, kv_w = n_fox * HEAD_DIM, n_swa * HEAD_DIM, n_kv * HEAD_DIM
    main_w = 3 * fox_w + swa_w + 2 * kv_w
    in_w = main_w + n_fox
    mod_cols = w_mod.shape[2]

    ax, ay, ac = _place()
    chip = 2 * ax + ay
    dev = 2 * chip + ac
    chip_arr = jnp.reshape(chip, (1,)).astype(jnp.int32)
    core_arr = jnp.reshape(ac, (1,)).astype(jnp.int32)

    x2, tgt = x[0], loss_target[0]

    shards = [w_in[0].astype(BF16), w_out[0].astype(BF16), w_up[0].astype(BF16), w_down[0].astype(BF16)]
    f_in, f_out, f_up, f_down = [lax.dynamic_update_index_in_dim(f, s, chip, 0)
                                 for f, s in zip(_gather_weights(shards), shards)]
    w_in_f = jnp.transpose(f_in, (1, 0, 2)).reshape(D, in_w)
    w_main = jnp.concatenate([w_in_f[:, :3 * fox_w], w_in_f[:, 3 * fox_w + n_fox:]], axis=1)
    w_fg = jnp.pad(w_in_f[:, 3 * fox_w:3 * fox_w + n_fox], ((0, 0), (0, LANES - n_fox)))
    w_out_f = f_out.reshape(D, D)
    w_up_f = jnp.transpose(f_up, (1, 0, 2)).reshape(D, N_CHIPS * w_up.shape[2])
    w_down_f = f_down.reshape(N_CHIPS * w_down.shape[1], D)
    d_ff = w_up_f.shape[1]

    c_all = _allgather8("gather_c", c.reshape(8, D // 8)).reshape(N_DEV, D)
    b_shard = lax.dynamic_slice_in_dim(b_mod, chip * mod_cols, mod_cols, axis=1)
    mod_shard = _mod_fwd(jnp.pad(c_all, ((0, 16 - N_DEV), (0, 0))), w_mod[0], b_shard)[:N_DEV]
    mod_all = _allgather8("gather_mod", mod_shard).reshape(N_CHIPS, 2, N_DEV, mod_cols)[:, 0]
    mod = lax.dynamic_index_in_dim(mod_all, dev, axis=1, keepdims=False).reshape(N_MOD, 1, D)
    sh_a, sc_a, gt_a, sh_m, sc_m, gt_m = [mod[n] for n in range(N_MOD)]

    h = _pre_norm(x2, g_pre_mix, sc_a, sh_a)
    proj = _mm_plain("in_proj", h, w_main, "nn", BF16, tn=_fit(768, main_w))
    fg = _mm_plain("in_proj_gate", h, w_fg, "nn", F32)
    b_pad = jnp.pad(b_forget, ((0, 0), (0, LANES - n_fox)))
    cum, cum_t, cum_b = _fox_gate_fwd(fg, b_pad, n_fox)
    cum_row = cum_t[:n_fox].reshape(n_fox, 1, S)
    fox_o, fox_lse = _fox_fwd(proj, cum_b, cum_row, n_fox)

    cos, sin_signed = _rope_tables(S)
    rq = _rope("rope_fwd", proj, 3 * n_fox, n_swa + n_kv, cos, sin_signed)
    v_first = 3 * n_fox + n_swa + n_kv
    sinks = swa_sinks[0]
    swa_o, swa_lse = _swa_fwd(rq, proj, v_first, sinks, n_swa, n_kv)

    mixcat = jnp.concatenate([fox_o, swa_o], axis=1).astype(BF16)
    mix = _mm_plain("out_proj", mixcat, w_out_f, "nn", F32)
    x1, h2 = _post_mix(x2, mix, g_post_mix, gt_a, g_pre_mlp, sc_m, sh_m)

    tm_u, tn_u = _fit(1024, S), _fit(1024, d_ff)

    def up_epilogue(acc, ex, outs):
        outs[0][...] = acc.astype(BF16)
        r = jnp.maximum(acc, 0.0)
        outs[1][...] = (r * r).astype(BF16)

    ublk = ((S, d_ff), BF16, (tm_u, tn_u), lambda i, j: (i, j))
    u, a = _matmul("mlp_up", h2, w_up_f, "nn", [ublk, ublk], up_epilogue)
    y = _mm_plain("mlp_down", a, w_down_f, "nn", F32)

    dy, dout, loss_part, acc_mlp_post = _loss_and_post_mlp_bwd(x1, y, tgt, g_post_mlp, gt_m)
    loss = lax.psum(loss_part[0, 0], ("x", "y", "c"))

    def du_epilogue(acc, ex, outs):
        outs[0][...] = (acc * (2.0 * jnp.maximum(ex[0][...].astype(F32), 0.0))).astype(BF16)

    du = _matmul("mlp_down_bwd", dy, w_down_f, "nt", [ublk], du_epilogue,
                 extras=[(u, (tm_u, tn_u), lambda i, j: (i, j))])[0]
    g_down = _mm_plain("grad_w_down", a, dy, "tn", BF16)
    tn_s = _fit(1024, w_up.shape[2])
    per = w_up.shape[2] // tn_s

    def shard_epilogue(acc, ex, outs):
        outs[0][0] = acc.astype(BF16)

    g_up = _matmul("grad_w_up", h2, du, "tn",
                   [((N_CHIPS, D, w_up.shape[2]), BF16, (1, _fit(1024, D), tn_s), lambda i, j: (j // per, i, j % per))],
                   shard_epilogue, tn=tn_s)[0]
    dh2 = _mm_plain("mlp_up_bwd", du, w_up_f, "nt", F32)
    dx1, dmix, acc_mid = _pre_mlp_and_post_mix_bwd(dh2, x1, dout, mix, g_pre_mlp, sc_m, g_post_mix, gt_a)

    dmixcat = _mm_plain("out_proj_bwd", dmix, w_out_f, "nt", F32)
    g_out = _mm_plain("grad_w_out", mixcat, dmix, "tn", BF16)

    fdq, fdk, fdv, dcum_row, dcum_q = _fox_bwd(proj, fox_o, dmixcat, fox_lse, cum_b, cum_row, n_fox)
    dcum_k = jnp.pad(dcum_row.reshape(n_fox, S), ((0, LANES - n_fox), (0, 0)))
    dfg, db_forget = _fox_gate_bwd(dcum_k, dcum_q, fg, b_pad)

    group_w = (n_swa // n_kv) * HEAD_DIM
    sdq, sdk, sdv, dsink = _swa_bwd(rq, proj, v_first, sinks, swa_o, dmixcat, fox_w // group_w, swa_lse, n_swa, n_kv)
    drq = jnp.concatenate([sdq, jnp.transpose(sdk, (1, 0, 2)).reshape(S, kv_w).astype(BF16)], axis=1)
    d_sq_sk = _rope("rope_bwd", drq, 0, n_swa + n_kv, cos, -sin_signed)
    dsv = jnp.transpose(sdv, (1, 0, 2)).reshape(S, kv_w).astype(BF16)
    dproj = jnp.concatenate([fdq, fdk, fdv, d_sq_sk, dsv], axis=1)

    g_main = _mm_plain("grad_w_in", h, dproj, "tn", BF16, tn=_fit(768, main_w))
    g_fg = _mm_plain("grad_w_in_gate", h, dfg, "tn", BF16)
    dh_gate = _mm_plain("in_proj_gate_bwd", dfg, w_fg, "nt", F32)

    def add_epilogue(acc, ex, outs):
        outs[0][...] = acc + ex[0][...]

    tm_h, tn_h = _fit(1024, S), _fit(1024, D)
    dh = _matmul("in_proj_bwd", dproj, w_main, "nt", [((S, D), F32, (tm_h, tn_h), lambda i, j: (i, j))], add_epilogue,
                 extras=[(dh_gate, (tm_h, tn_h), lambda i, j: (i, j))], tk=_fit(768, main_w))[0]
    grad_x, acc_pre = _pre_mix_bwd(dh, x2, dx1, g_pre_mix, sc_a)

    zero_row = jnp.zeros((1, D), F32)
    tail = jnp.concatenate([db_forget[0:1, :n_fox], dsink[:, 0, :n_swa // n_kv].reshape(1, n_swa),
                            jnp.zeros((1, D - n_fox - n_swa), F32)], axis=1)
    partial = jnp.concatenate([
        acc_pre[0:1], acc_pre[1:2], acc_mid[3:4], acc_mid[0:1], acc_mid[1:2], acc_mlp_post[0:1],
        acc_pre[2:3], acc_mid[4:5], acc_mid[2:3], acc_mlp_post[1:2], tail] + [zero_row] * 5, axis=0)
    gathered = _allgather8("gather_small_grads", partial)
    small = _sum_blocks("sum_small_grads", gathered, N_DEV)
    g_b_mod = small[0:N_MOD].reshape(1, N_MOD * D)
    g_small = {"g_pre_mix": small[6:7], "g_post_mix": small[7:8], "g_pre_mlp": small[8:9], "g_post_mlp": small[9:10],
               "b_forget": small[10:11, :n_fox], "swa_sinks": small[10:11, n_fox:n_fox + n_swa]}

    dmod_all = gathered.reshape(N_DEV, 16, D)[:, :N_MOD].reshape(N_DEV, N_MOD * D)
    dmod_shard = lax.dynamic_slice_in_dim(dmod_all, chip * mod_cols, mod_cols, axis=1)
    g_w_mod, d_w_mod, nm_w_mod, nv_w_mod = _mod_update(c_all.T, dmod_shard, w_mod[0], m_w_mod[0], v_w_mod[0])

    g_in_f = jnp.concatenate([g_main[:, :3 * fox_w], g_fg[:, :n_fox], g_main[:, 3 * fox_w:]], axis=1)
    full = [jnp.transpose(g_in_f.reshape(D, N_CHIPS, in_w // N_CHIPS), (1, 0, 2)),
            g_out.reshape(N_CHIPS, D // N_CHIPS, D), g_up, g_down.reshape(N_CHIPS, d_ff // N_CHIPS, D)]
    names = ["w_in", "w_out", "w_up", "w_down"]
    from_sibling = _pair_exchange(full)
    pair_sums = [_pair_add("pair_add_" + n, core_arr, g, r) for n, g, r in zip(names, full, from_sibling)]
    from_chips = _chip_exchange(pair_sums)
    halves = [_chip_add("chip_add_" + n, chip_arr, p, r) for n, p, r in zip(names, pair_sums, from_chips)]
    other_halves = _pair_share(halves)

    grads = {"w_mod": g_w_mod[None], "b_mod": g_b_mod}
    deltas = {"w_mod": d_w_mod[None]}
    new_m = {"w_mod": nm_w_mod[None]}
    new_v = {"w_mod": nv_w_mod[None]}
    weights = {"w_in": (w_in, m_w_in, v_w_in), "w_out": (w_out, m_w_out, v_w_out), "w_up": (w_up, m_w_up, v_w_up),
               "w_down": (w_down, m_w_down, v_w_down)}
    for n, own, other in zip(names, halves, other_halves):
        w, m, v = weights[n]
        g, d_, m_, v_ = _adam_halves("adam_" + n, core_arr, w[0], own, other, m[0], v[0])
        grads[n], deltas[n], new_m[n], new_v[n] = g[None], d_[None], m_[None], v_[None]
    small_w = {"b_mod": (b_mod, m_b_mod, v_b_mod), "g_pre_mix": (g_pre_mix, m_g_pre_mix, v_g_pre_mix),
               "g_post_mix": (g_post_mix, m_g_post_mix, v_g_post_mix), "b_forget": (b_forget, m_b_forget, v_b_forget),
               "swa_sinks": (swa_sinks, m_swa_sinks, v_swa_sinks), "g_pre_mlp": (g_pre_mlp, m_g_pre_mlp, v_g_pre_mlp),
               "g_post_mlp": (g_post_mlp, m_g_post_mlp, v_g_post_mlp)}
    g_small["b_mod"] = g_b_mod
    for n, (w, m, v) in small_w.items():
        g = g_small[n]
        grads[n] = g
        deltas[n], new_m[n], new_v[n] = _adam_update("adam_" + n, w, g, m, v)

    order = ["w_mod", "b_mod", "g_pre_mix", "g_post_mix", "w_in", "b_forget", "swa_sinks", "w_out", "g_pre_mlp",
             "g_post_mlp", "w_up", "w_down"]
    return (loss, grad_x[None], *[grads[n] for n in order], *[deltas[n] for n in order],
            *[new_m[n] for n in order], *[new_v[n] for n in order])
```

```python
import jax
import jax.numpy as jnp
from jax import lax
from jax.experimental import pallas as pl
from jax.experimental.pallas import tpu as pltpu

F32 = jnp.float32
BF16 = jnp.bfloat16
MESH = pl.DeviceIdType.MESH

HEAD_DIM = 128
SWA_BLOCK = 128
ROPE_THETA = 10000.0
NORM_EPS = 1e-6
NEG = -1e30
N_MOD = 6
ADAM_LR = 0.001
ADAM_B1 = 0.9
ADAM_B2 = 0.999
ADAM_EPS = 1e-08
ADAM_WD = 0.01
ADAM_STEP = 10
N_CHIPS = 4
N_DEV = 8
LANES = 128
VMEM_CAP = 60 * 1024 * 1024

_NN = (((1,), (0,)), ((), ()))
_NT = (((1,), (1,)), ((), ()))
_TN = (((0,), (0,)), ((), ()))


def _vmem(nbytes):
    return int(min(VMEM_CAP, nbytes * 5 // 4 + (4 << 20)))


def _nbytes(shape, dtype):
    n = 1
    for s in shape:
        n *= s
    return n * jnp.dtype(dtype).itemsize


def _fit(t, n):
    t = min(t, n)
    assert n % t == 0, (t, n)
    return t


def _matmul(name, a, b, mode, out_defs, epilogue, extras=(), tm=1024, tn=1024, tk=512):
    if mode == "nn":
        (M, K), (K2, N) = a.shape, b.shape
    elif mode == "nt":
        (M, K), (N, K2) = a.shape, b.shape
    else:
        (K, M), (K2, N) = a.shape, b.shape
    assert K == K2, (a.shape, b.shape, mode)
    tm, tn, tk = _fit(tm, M), _fit(tn, N), _fit(tk, K)
    nk = K // tk
    dims = {"nn": _NN, "nt": _NT, "tn": _TN}[mode]
    a_spec = (pl.BlockSpec((tk, tm), lambda i, j, k: (k, i)) if mode == "tn"
              else pl.BlockSpec((tm, tk), lambda i, j, k: (i, k)))
    b_spec = (pl.BlockSpec((tn, tk), lambda i, j, k: (j, k)) if mode == "nt"
              else pl.BlockSpec((tk, tn), lambda i, j, k: (k, j)))
    n_ex, n_out = len(extras), len(out_defs)

    def body(*refs):
        a_ref, b_ref = refs[0], refs[1]
        ex = refs[2:2 + n_ex]
        outs = refs[2 + n_ex:2 + n_ex + n_out]
        prod = lax.dot_general(a_ref[...], b_ref[...], dims, preferred_element_type=F32)
        if nk == 1:
            epilogue(prod, ex, outs)
        else:
            acc_ref = refs[-1]
            k = pl.program_id(2)

            @pl.when(k == 0)
            def _():
                acc_ref[...] = prod

            @pl.when(k > 0)
            def _():
                acc_ref[...] += prod

            @pl.when(k == nk - 1)
            def _():
                epilogue(acc_ref[...], ex, outs)

    def wrap(f):
        return lambda i, j, k: f(i, j)

    in_specs = [a_spec, b_spec] + [pl.BlockSpec(blk, wrap(f)) for _, blk, f in extras]
    out_specs = [pl.BlockSpec(blk, wrap(f)) for _, _, blk, f in out_defs]
    out_shape = [jax.ShapeDtypeStruct(s, d) for s, d, _, _ in out_defs]
    need = 2 * (tm * tk + tk * tn) * a.dtype.itemsize + 3 * tm * tn * 4
    need += sum(2 * _nbytes(blk, arr.dtype) for arr, blk, _ in extras)
    need += sum(2 * _nbytes(blk, d) for _, d, blk, _ in out_defs)
    res = pl.pallas_call(
        body, name=name, grid=(M // tm, N // tn, nk),
        in_specs=in_specs, out_specs=out_specs, out_shape=out_shape,
        scratch_shapes=[pltpu.VMEM((tm, tn), F32)] if nk > 1 else [],
        compiler_params=pltpu.CompilerParams(
            dimension_semantics=("parallel", "parallel", "arbitrary"), vmem_limit_bytes=_vmem(need)),
    )(a, b, *[arr for arr, _, _ in extras])
    return res


def _mm_plain(name, a, b, mode, out_dtype, **tiles):
    if mode == "nn":
        M, N = a.shape[0], b.shape[1]
    elif mode == "nt":
        M, N = a.shape[0], b.shape[0]
    else:
        M, N = a.shape[1], b.shape[1]
    tm, tn = _fit(tiles.get("tm", 1024), M), _fit(tiles.get("tn", 1024), N)

    def epi(acc, ex, outs):
        outs[0][...] = acc.astype(out_dtype)

    return _matmul(name, a, b, mode, [((M, N), out_dtype, (tm, tn), lambda i, j: (i, j))], epi, **tiles)[0]


def _rstd(v):
    return lax.rsqrt(jnp.mean(v * v, axis=-1, keepdims=True) + NORM_EPS)


def _row_call(name, body, row_ins, vec_ins, row_outs, acc_outs, S, D, tr):
    tr = _fit(tr, S)
    row_spec = pl.BlockSpec((tr, D), lambda r: (r, 0))
    vec_spec = pl.BlockSpec((1, D), lambda r: (0, 0))
    in_specs = [row_spec] * len(row_ins) + [vec_spec] * len(vec_ins)
    out_specs = [row_spec] * len(row_outs) + [pl.BlockSpec(shp, lambda r: (0, 0)) for shp in acc_outs]
    out_shape = [jax.ShapeDtypeStruct((S, D), d) for d in row_outs] + [jax.ShapeDtypeStruct(shp, F32) for shp in acc_outs]
    need = sum(2 * tr * D * a.dtype.itemsize for a in row_ins) + sum(2 * tr * D * jnp.dtype(d).itemsize for d in row_outs)
    need += 8 * tr * D * 4
    return pl.pallas_call(
        body, name=name, grid=(S // tr,), in_specs=in_specs, out_specs=out_specs, out_shape=out_shape,
        compiler_params=pltpu.CompilerParams(dimension_semantics=("arbitrary",), vmem_limit_bytes=_vmem(need)),
    )(*row_ins, *vec_ins)


def _acc_rows(ref, rows):
    @pl.when(pl.program_id(0) == 0)
    def _():
        ref[...] = jnp.zeros_like(ref)
    for n, r in enumerate(rows):
        ref[n:n + 1, :] += r


def _pre_norm(x, g, sc, sh):
    S, D = x.shape

    def body(x_ref, g_ref, sc_ref, sh_ref, h_ref):
        xv = x_ref[...]
        xn = xv * _rstd(xv)
        h_ref[...] = (xn * g_ref[...] * (1.0 + sc_ref[...]) + sh_ref[...]).astype(BF16)

    return _row_call("pre_norm_mix", body, [x], [g, sc, sh], [BF16], [], S, D, 256)[0]


def _post_mix(x, mix, g_post, gt, g_pre, sc, sh):
    S, D = x.shape

    def body(x_ref, mix_ref, gp_ref, gt_ref, g2_ref, sc_ref, sh_ref, x1_ref, h2_ref):
        mv = mix_ref[...]
        x1 = x_ref[...] + gt_ref[...] * (mv * _rstd(mv) * gp_ref[...])
        x1_ref[...] = x1
        h2_ref[...] = (x1 * _rstd(x1) * g2_ref[...] * (1.0 + sc_ref[...]) + sh_ref[...]).astype(BF16)

    return _row_call("post_mix_pre_mlp", body, [x, mix], [g_post, gt, g_pre, sc, sh], [F32, BF16], [], S, D, 256)


def _loss_and_post_mlp_bwd(x1, y, target, g_post, gt):
    S, D = x1.shape

    def body(x1_ref, y_ref, t_ref, g_ref, gt_ref, dy_ref, dout_ref, loss_ref, acc_ref):
        yv = y_ref[...]
        r = _rstd(yv)
        yh = yv * r
        n = yh * g_ref[...]
        diff = x1_ref[...] + gt_ref[...] * n - t_ref[...]
        dout = diff * (1.0 / D)
        dout_ref[...] = dout
        dn = dout * gt_ref[...]
        dyh = dn * g_ref[...]
        dy_ref[...] = (r * (dyh - yh * jnp.mean(dyh * yh, axis=-1, keepdims=True))).astype(BF16)
        _acc_rows(acc_ref, [jnp.sum(dout * n, axis=0, keepdims=True), jnp.sum(dn * yh, axis=0, keepdims=True)])

        @pl.when(pl.program_id(0) == 0)
        def _():
            loss_ref[...] = jnp.zeros_like(loss_ref)
        loss_ref[...] += jnp.full(loss_ref.shape, (0.5 / D) * jnp.sum(diff * diff), F32)

    return _row_call("loss_post_mlp_bwd", body, [x1, y, target], [g_post, gt], [BF16, F32],
                     [(8, LANES), (8, D)], S, D, 128)


def _pre_mlp_and_post_mix_bwd(dh2, x1, dout, mix, g_pre, sc, g_post, gt):
    S, D = x1.shape

    def body(dh_ref, x1_ref, dout_ref, mix_ref, g_ref, sc_ref, gp_ref, gt_ref, dx1_ref, dmix_ref, acc_ref):
        dh = dh_ref[...]
        x1v = x1_ref[...]
        r3 = _rstd(x1v)
        xn = x1v * r3
        dxn = dh * (1.0 + sc_ref[...]) * g_ref[...]
        dx1 = dout_ref[...] + r3 * (dxn - xn * jnp.mean(dxn * xn, axis=-1, keepdims=True))
        dx1_ref[...] = dx1
        mv = mix_ref[...]
        r2 = _rstd(mv)
        mh = mv * r2
        dn = dx1 * gt_ref[...]
        dmh = dn * gp_ref[...]
        dmix_ref[...] = (r2 * (dmh - mh * jnp.mean(dmh * mh, axis=-1, keepdims=True))).astype(BF16)
        _acc_rows(acc_ref, [
            jnp.sum(dh, axis=0, keepdims=True),
            jnp.sum(dh * xn * g_ref[...], axis=0, keepdims=True),
            jnp.sum(dh * (1.0 + sc_ref[...]) * xn, axis=0, keepdims=True),
            jnp.sum(dx1 * mh * gp_ref[...], axis=0, keepdims=True),
            jnp.sum(dn * mh, axis=0, keepdims=True)])

    return _row_call("pre_mlp_post_mix_bwd", body, [dh2, x1, dout, mix], [g_pre, sc, g_post, gt], [F32, BF16],
                     [(8, D)], S, D, 128)


def _pre_mix_bwd(dh, x, dx1, g_pre, sc):
    S, D = x.shape

    def body(dh_ref, x_ref, dx1_ref, g_ref, sc_ref, gx_ref, acc_ref):
        dhv = dh_ref[...]
        xv = x_ref[...]
        r = _rstd(xv)
        xn = xv * r
        dxn = dhv * (1.0 + sc_ref[...]) * g_ref[...]
        gx_ref[...] = dx1_ref[...] + r * (dxn - xn * jnp.mean(dxn * xn, axis=-1, keepdims=True))
        _acc_rows(acc_ref, [
            jnp.sum(dhv, axis=0, keepdims=True),
            jnp.sum(dhv * xn * g_ref[...], axis=0, keepdims=True),
            jnp.sum(dhv * (1.0 + sc_ref[...]) * xn, axis=0, keepdims=True)])

    return _row_call("pre_mix_bwd", body, [dh, x, dx1], [g_pre, sc], [F32], [(8, D)], S, D, 128)


CUM_BLOCK = 256


def _tri(n, upper):
    r = lax.broadcasted_iota(jnp.int32, (n, n), 0)
    c = lax.broadcasted_iota(jnp.int32, (n, n), 1)
    return ((c >= r) if upper else (c <= r)).astype(F32)


def _fox_gate_fwd(fg, b_pad, n_fox):
    S = fg.shape[0]
    cb = _fit(CUM_BLOCK, S)

    def body(fg_ref, b_ref, cum_ref, cumt_ref, cumb_ref):
        low = _tri(cb, False)
        carry = jnp.zeros((1, LANES), F32)
        for n in range(S // cb):
            z = fg_ref[n * cb:(n + 1) * cb, :] + b_ref[...]
            logf = jnp.minimum(z, 0.0) - jnp.log(1.0 + jnp.exp(-jnp.abs(z)))
            blk = jnp.dot(low, logf, precision=lax.Precision.HIGHEST, preferred_element_type=F32) + carry
            cum_ref[n * cb:(n + 1) * cb, :] = blk
            carry = blk[cb - 1:cb, :]
        cum = cum_ref[...]
        cumt_ref[...] = cum.T
        for h in range(n_fox):
            cumb_ref[h] = jnp.broadcast_to(cum[:, h:h + 1], (S, LANES))

    return pl.pallas_call(
        body, name="fox_gate_fwd",
        out_shape=[jax.ShapeDtypeStruct((S, LANES), F32), jax.ShapeDtypeStruct((LANES, S), F32),
                   jax.ShapeDtypeStruct((n_fox, S, LANES), F32)],
        compiler_params=pltpu.CompilerParams(vmem_limit_bytes=_vmem((4 + 2 * n_fox) * S * LANES * 4)),
    )(fg, b_pad)


def _fox_gate_bwd(dcum_k, dcum_q, fg, b_pad):
    S = fg.shape[0]
    n_fox = dcum_q.shape[0]
    cb = _fit(CUM_BLOCK, S)

    def body(dk_ref, dq_ref, fg_ref, b_ref, dfg_ref, db_ref, dc_ref):
        lane = lax.broadcasted_iota(jnp.int32, (S, LANES), 1)
        dc = dk_ref[...].T
        for h in range(n_fox):
            dc = dc + jnp.where(lane == h, dq_ref[h], 0.0)
        dc_ref[...] = dc
        up = _tri(cb, True)
        carry = jnp.zeros((1, LANES), F32)
        db = jnp.zeros((1, LANES), F32)
        for n in reversed(range(S // cb)):
            blk = jnp.dot(up, dc_ref[n * cb:(n + 1) * cb, :], precision=lax.Precision.HIGHEST,
                          preferred_element_type=F32) + carry
            carry = blk[0:1, :]
            z = fg_ref[n * cb:(n + 1) * cb, :] + b_ref[...]
            dfg = blk * (1.0 / (1.0 + jnp.exp(z)))
            dfg_ref[n * cb:(n + 1) * cb, :] = dfg.astype(BF16)
            db = db + jnp.sum(dfg, axis=0, keepdims=True)
        db_ref[...] = jnp.broadcast_to(db, db_ref.shape)

    return pl.pallas_call(
        body, name="fox_gate_bwd",
        out_shape=[jax.ShapeDtypeStruct((S, LANES), BF16), jax.ShapeDtypeStruct((8, LANES), F32)],
        scratch_shapes=[pltpu.VMEM((S, LANES), F32)],
        compiler_params=pltpu.CompilerParams(vmem_limit_bytes=_vmem((8 + 2 * n_fox) * S * LANES * 4)),
    )(dcum_k, dcum_q, fg, b_pad)


FOX_TILE = 512


def _fox_scores(q, k, cq, ck, q0, k0, tq, tk, scale):
    s = lax.dot_general(q, k, _NT, preferred_element_type=F32) * scale + cq - ck
    row = q0 + lax.broadcasted_iota(jnp.int32, (tq, tk), 0)
    col = k0 + lax.broadcasted_iota(jnp.int32, (tq, tk), 1)
    return jnp.where(col <= row, s, NEG)


def _fox_fwd(proj, cum_b, cum_row, n_fox):
    S = proj.shape[0]
    t = _fit(FOX_TILE, S)
    nq = S // t
    scale = HEAD_DIM ** -0.5

    def body(q_ref, k_ref, v_ref, cq_ref, ck_ref, o_ref, lse_ref):
        def q_block(qi, _):
            q0 = pl.multiple_of(qi * t, t)
            q = q_ref[pl.ds(q0, t), :]
            cq = cq_ref[0, pl.ds(q0, t), :][:, :1]

            def kv_block(j, carry):
                m, l, acc = carry
                k0 = pl.multiple_of(j * t, t)
                s = _fox_scores(q, k_ref[pl.ds(k0, t), :], cq, ck_ref[0, :, pl.ds(k0, t)], q0, k0, t, t, scale)
                m_new = jnp.maximum(m, jnp.max(s, axis=-1, keepdims=True))
                alpha = jnp.exp(m - m_new)
                p = jnp.exp(s - m_new)
                l = alpha * l + jnp.sum(p, axis=-1, keepdims=True)
                acc = alpha * acc + jnp.dot(p.astype(BF16), v_ref[pl.ds(k0, t), :], preferred_element_type=F32)
                return m_new, l, acc

            init = (jnp.full((t, 1), NEG, F32), jnp.zeros((t, 1), F32), jnp.zeros((t, HEAD_DIM), F32))
            m, l, acc = lax.fori_loop(0, qi + 1, kv_block, init)
            o_ref[pl.ds(q0, t), :] = acc / l
            lse_ref[0, pl.ds(q0, t), :] = jnp.broadcast_to(m + jnp.log(l), (t, LANES))
            return 0

        lax.fori_loop(0, nq, q_block, 0)

    col = lambda off: pl.BlockSpec((S, HEAD_DIM), lambda h: (0, off + h))
    per_head = pl.BlockSpec((1, S, LANES), lambda h: (h, 0, 0))
    return pl.pallas_call(
        body, name="fox_fwd", grid=(n_fox,),
        in_specs=[col(0), col(n_fox), col(2 * n_fox), per_head, pl.BlockSpec((1, 1, S), lambda h: (h, 0, 0))],
        out_specs=[pl.BlockSpec((S, HEAD_DIM), lambda h: (0, h)), per_head],
        out_shape=[jax.ShapeDtypeStruct((S, n_fox * HEAD_DIM), F32), jax.ShapeDtypeStruct((n_fox, S, LANES), F32)],
        compiler_params=pltpu.CompilerParams(dimension_semantics=("parallel",),
                                             vmem_limit_bytes=_vmem(16 * S * HEAD_DIM * 4 + 12 * t * t * 4)),
    )(proj, proj, proj, cum_b, cum_row)


def _fox_bwd(proj, o, do, lse_b, cum_b, cum_row, n_fox):
    S = proj.shape[0]
    t = _fit(FOX_TILE, S)
    nq = S // t
    scale = HEAD_DIM ** -0.5

    def body(q_ref, k_ref, v_ref, o_ref, do_ref, lse_ref, cq_ref, ck_ref, dq_ref, dk_ref, dv_ref, dc_ref, dcq_ref,
             dq_acc, delta_ref):
        dq_acc[...] = jnp.zeros_like(dq_acc)
        dcq_ref[...] = jnp.zeros_like(dcq_ref)

        def delta_block(qi, _):
            q0 = pl.multiple_of(qi * t, t)
            d = jnp.sum(do_ref[pl.ds(q0, t), :] * o_ref[pl.ds(q0, t), :], axis=-1, keepdims=True)
            delta_ref[pl.ds(q0, t), :] = jnp.broadcast_to(d, (t, LANES))
            return 0

        lax.fori_loop(0, nq, delta_block, 0)

        def kv_block(j, _):
            k0 = pl.multiple_of(j * t, t)
            k = k_ref[pl.ds(k0, t), :]
            v = v_ref[pl.ds(k0, t), :]
            ck = ck_ref[0, :, pl.ds(k0, t)]

            def q_block(qi, carry):
                dk, dv, dc = carry
                q0 = pl.multiple_of(qi * t, t)
                q = q_ref[pl.ds(q0, t), :]
                dov = do_ref[pl.ds(q0, t), :].astype(BF16)
                s = _fox_scores(q, k, cq_ref[0, pl.ds(q0, t), :][:, :1], ck, q0, k0, t, t, scale)
                p = jnp.exp(s - lse_ref[0, pl.ds(q0, t), :][:, :1])
                dp = lax.dot_general(dov, v, _NT, preferred_element_type=F32)
                ds = p * (dp - delta_ref[pl.ds(q0, t), :][:, :1])
                dsb = ds.astype(BF16)
                dv = dv + lax.dot_general(p.astype(BF16), dov, _TN, preferred_element_type=F32)
                dk = dk + lax.dot_general(dsb, q, _TN, preferred_element_type=F32)
                dq_acc[pl.ds(q0, t), :] += jnp.dot(dsb, k, preferred_element_type=F32)
                dc = dc - jnp.sum(ds, axis=0, keepdims=True)
                dcq_ref[0, pl.ds(q0, t), :] += jnp.broadcast_to(jnp.sum(ds, axis=1, keepdims=True), (t, LANES))
                return dk, dv, dc

            init = (jnp.zeros((t, HEAD_DIM), F32), jnp.zeros((t, HEAD_DIM), F32), jnp.zeros((1, t), F32))
            dk, dv, dc = lax.fori_loop(j, nq, q_block, init)
            dk_ref[pl.ds(k0, t), :] = (dk * scale).astype(BF16)
            dv_ref[pl.ds(k0, t), :] = dv.astype(BF16)
            dc_ref[0, :, pl.ds(k0, t)] = dc
            return 0

        lax.fori_loop(0, nq, kv_block, 0)
        dq_ref[...] = (dq_acc[...] * scale).astype(BF16)

    col = lambda off: pl.BlockSpec((S, HEAD_DIM), lambda h: (0, off + h))
    per_head = pl.BlockSpec((1, S, LANES), lambda h: (h, 0, 0))
    row = pl.BlockSpec((1, 1, S), lambda h: (h, 0, 0))
    grad = jax.ShapeDtypeStruct((S, n_fox * HEAD_DIM), BF16)
    return pl.pallas_call(
        body, name="fox_bwd", grid=(n_fox,),
        in_specs=[col(0), col(n_fox), col(2 * n_fox), col(0), col(0), per_head, per_head, row],
        out_specs=[col(0), col(0), col(0), row, per_head],
        out_shape=[grad, grad, grad, jax.ShapeDtypeStruct((n_fox, 1, S), F32), jax.ShapeDtypeStruct((n_fox, S, LANES), F32)],
        scratch_shapes=[pltpu.VMEM((S, HEAD_DIM), F32), pltpu.VMEM((S, LANES), F32)],
        compiler_params=pltpu.CompilerParams(dimension_semantics=("parallel",),
                                             vmem_limit_bytes=_vmem(24 * S * HEAD_DIM * 4 + 16 * t * t * 4)),
    )(proj, proj, proj, o, do, lse_b, cum_b, cum_row)


def _rope_tables(S):
    half = HEAD_DIM // 2
    inv_freq = 1.0 / (ROPE_THETA ** (jnp.arange(half, dtype=F32) * (2.0 / HEAD_DIM)))
    ang = jnp.arange(S).astype(F32)[:, None] * inv_freq[None, :]
    cos, sin = jnp.cos(ang), jnp.sin(ang)
    return jnp.concatenate([cos, cos], axis=-1), jnp.concatenate([-sin, sin], axis=-1)


def _rope(name, src, first_block, n_blocks, cos, sin_signed):
    S = src.shape[0]

    def body(x_ref, cos_ref, sin_ref, o_ref):
        xv = x_ref[...].astype(F32)
        o_ref[...] = (xv * cos_ref[...] + pltpu.roll(xv, HEAD_DIM // 2, 1) * sin_ref[...]).astype(BF16)

    table = pl.BlockSpec((S, HEAD_DIM), lambda n: (0, 0))
    return pl.pallas_call(
        body, name=name, grid=(n_blocks,),
        in_specs=[pl.BlockSpec((S, HEAD_DIM), lambda n: (0, first_block + n)), table, table],
        out_specs=pl.BlockSpec((S, HEAD_DIM), lambda n: (0, n)),
        out_shape=jax.ShapeDtypeStruct((S, n_blocks * HEAD_DIM), BF16),
        compiler_params=pltpu.CompilerParams(dimension_semantics=("parallel",),
                                             vmem_limit_bytes=_vmem(12 * S * HEAD_DIM * 4)),
    )(src, cos, sin_signed)


def _swa_tile(q_ref, kp_ref, kc_ref, n, group, scale):
    B = SWA_BLOCK
    qs = jnp.concatenate([q_ref[:, g * HEAD_DIM:(g + 1) * HEAD_DIM] for g in range(group)], axis=0)
    kcat = jnp.concatenate([kp_ref[...], kc_ref[...]], axis=0)
    s = lax.dot_general(qs, kcat, _NT, preferred_element_type=F32) * scale
    qi = lax.broadcasted_iota(jnp.int32, (group * B, 2 * B), 0) % B
    kj = lax.broadcasted_iota(jnp.int32, (group * B, 2 * B), 1)
    diff = qi + B - kj
    mask = (diff >= 0) & (diff < B) & ((n * B + kj - B) >= 0)
    return qs, kcat, jnp.where(mask, s, NEG)


def _swa_sink_col(sink_ref, kv, group):
    head = lax.broadcasted_iota(jnp.int32, (group * SWA_BLOCK, 1), 0) // SWA_BLOCK
    col = jnp.zeros((group * SWA_BLOCK, 1), F32)
    for g in range(group):
        col = jnp.where(head == g, sink_ref[kv * group + g], col)
    return col


def _swa_specs(n_kv, group, q_first, k_first, v_first):
    B = SWA_BLOCK
    prev = lambda n: jnp.maximum(n - 1, 0)
    return [
        pl.BlockSpec((B, group * HEAD_DIM), lambda kv, n: (n, q_first + kv)),
        pl.BlockSpec((B, HEAD_DIM), lambda kv, n: (prev(n), k_first + kv)),
        pl.BlockSpec((B, HEAD_DIM), lambda kv, n: (n, k_first + kv)),
        pl.BlockSpec((B, HEAD_DIM), lambda kv, n: (prev(n), v_first + kv)),
        pl.BlockSpec((B, HEAD_DIM), lambda kv, n: (n, v_first + kv)),
    ]


def _swa_fwd(rq, proj, v_first, sinks, n_q, n_kv):
    S = rq.shape[0]
    B = SWA_BLOCK
    group = n_q // n_kv
    scale = HEAD_DIM ** -0.5

    def body(q_ref, kp_ref, kc_ref, vp_ref, vc_ref, sink_ref, o_ref, lse_ref):
        kv, n = pl.program_id(0), pl.program_id(1)
        _, _, s = _swa_tile(q_ref, kp_ref, kc_ref, n, group, scale)
        sink = _swa_sink_col(sink_ref, kv, group)
        m = jnp.maximum(jnp.max(s, axis=-1, keepdims=True), sink)
        p = jnp.exp(s - m)
        denom = jnp.sum(p, axis=-1, keepdims=True) + jnp.exp(sink - m)
        vcat = jnp.concatenate([vp_ref[...], vc_ref[...]], axis=0)
        o = jnp.dot((p / denom).astype(BF16), vcat, preferred_element_type=F32)
        lse = m + jnp.log(denom)
        for g in range(group):
            o_ref[:, g * HEAD_DIM:(g + 1) * HEAD_DIM] = o[g * B:(g + 1) * B, :]
            lse_ref[0, :, g * LANES:(g + 1) * LANES] = jnp.broadcast_to(lse[g * B:(g + 1) * B, :], (B, LANES))

    specs = _swa_specs(n_kv, group, 0, n_q, v_first)
    q_blk = pl.BlockSpec((B, group * HEAD_DIM), lambda kv, n: (n, kv))
    return pl.pallas_call(
        body, name="swa_fwd", grid=(n_kv, S // B),
        in_specs=specs + [pl.BlockSpec(memory_space=pltpu.SMEM)],
        out_specs=[q_blk, pl.BlockSpec((1, B, group * LANES), lambda kv, n: (kv, n, 0))],
        out_shape=[jax.ShapeDtypeStruct((S, n_q * HEAD_DIM), F32), jax.ShapeDtypeStruct((n_kv, S, group * LANES), F32)],
        compiler_params=pltpu.CompilerParams(dimension_semantics=("parallel", "arbitrary")),
    )(rq, rq, rq, proj, proj, sinks)


def _swa_bwd(rq, proj, v_first, sinks, o, do, do_first, lse_b, n_q, n_kv):
    S = rq.shape[0]
    B = SWA_BLOCK
    group = n_q // n_kv
    scale = HEAD_DIM ** -0.5

    def body(q_ref, kp_ref, kc_ref, vp_ref, vc_ref, o_ref, do_ref, lse_ref, sink_ref,
             dq_ref, dk_ref, dv_ref, dsink_ref):
        kv, n = pl.program_id(0), pl.program_id(1)

        @pl.when(n == 0)
        def _():
            dk_ref[...] = jnp.zeros_like(dk_ref)
            dv_ref[...] = jnp.zeros_like(dv_ref)
            dsink_ref[...] = jnp.zeros_like(dsink_ref)

        qs, kcat, s = _swa_tile(q_ref, kp_ref, kc_ref, n, group, scale)
        sink = _swa_sink_col(sink_ref, kv, group)
        stack = lambda ref, w: jnp.concatenate([ref[:, g * w:(g + 1) * w] for g in range(group)], axis=0)
        lse = jnp.concatenate([lse_ref[0, :, g * LANES:g * LANES + 1] for g in range(group)], axis=0)
        do32 = stack(do_ref, HEAD_DIM)
        delta = jnp.sum(do32 * stack(o_ref, HEAD_DIM), axis=-1, keepdims=True)
        dov = do32.astype(BF16)
        p = jnp.exp(s - lse)
        vcat = jnp.concatenate([vp_ref[...], vc_ref[...]], axis=0)
        dp = lax.dot_general(dov, vcat, _NT, preferred_element_type=F32)
        ds = p * (dp - delta)
        dsb = ds.astype(BF16)
        dq = jnp.dot(dsb, kcat, preferred_element_type=F32) * scale
        for g in range(group):
            dq_ref[:, g * HEAD_DIM:(g + 1) * HEAD_DIM] = dq[g * B:(g + 1) * B, :].astype(BF16)
        dkcat = lax.dot_general(dsb, qs, _TN, preferred_element_type=F32) * scale
        dvcat = lax.dot_general(p.astype(BF16), dov, _TN, preferred_element_type=F32)
        prev0 = pl.multiple_of(jnp.maximum(n - 1, 0) * B, B)
        cur0 = pl.multiple_of(n * B, B)
        dk_ref[0, pl.ds(prev0, B), :] += dkcat[:B, :]
        dk_ref[0, pl.ds(cur0, B), :] += dkcat[B:, :]
        dv_ref[0, pl.ds(prev0, B), :] += dvcat[:B, :]
        dv_ref[0, pl.ds(cur0, B), :] += dvcat[B:, :]
        dsk = -jnp.exp(sink - lse) * delta
        lane = lax.broadcasted_iota(jnp.int32, (1, LANES), 1)
        row = jnp.zeros((1, LANES), F32)
        for g in range(group):
            row = row + jnp.where(lane == g, jnp.sum(dsk[g * B:(g + 1) * B, :]), 0.0)
        dsink_ref[0, 0:1, :] += row

    specs = _swa_specs(n_kv, group, 0, n_q, v_first)
    q_blk = pl.BlockSpec((B, group * HEAD_DIM), lambda kv, n: (n, kv))
    acc = pl.BlockSpec((1, S, HEAD_DIM), lambda kv, n: (kv, 0, 0))
    return pl.pallas_call(
        body, name="swa_bwd", grid=(n_kv, S // B),
        in_specs=specs + [q_blk, pl.BlockSpec((B, group * HEAD_DIM), lambda kv, n: (n, do_first + kv)),
                          pl.BlockSpec((1, B, group * LANES), lambda kv, n: (kv, n, 0)),
                          pl.BlockSpec(memory_space=pltpu.SMEM)],
        out_specs=[q_blk, acc, acc, pl.BlockSpec((1, 8, LANES), lambda kv, n: (kv, 0, 0))],
        out_shape=[jax.ShapeDtypeStruct((S, n_q * HEAD_DIM), BF16), jax.ShapeDtypeStruct((n_kv, S, HEAD_DIM), F32),
                   jax.ShapeDtypeStruct((n_kv, S, HEAD_DIM), F32), jax.ShapeDtypeStruct((n_kv, 8, LANES), F32)],
        compiler_params=pltpu.CompilerParams(dimension_semantics=("parallel", "arbitrary")),
    )(rq, rq, rq, proj, proj, o, do, lse_b, sinks)


def _adamw(w, g, m, v):
    m = ADAM_B1 * m + (1.0 - ADAM_B1) * g
    v = ADAM_B2 * v + (1.0 - ADAM_B2) * (g * g)
    m_hat = m / (1.0 - ADAM_B1 ** ADAM_STEP)
    v_hat = v / (1.0 - ADAM_B2 ** ADAM_STEP)
    delta = -ADAM_LR * (m_hat / (jnp.sqrt(v_hat) + ADAM_EPS) + ADAM_WD * w)
    return delta, m, v


def _mod_fwd(cond_in, w_mod, b_shard):
    R, D = cond_in.shape
    cols = w_mod.shape[1]
    tn = _fit(512, cols)

    def body(c_ref, w_ref, b_ref, o_ref):
        cv = c_ref[...]
        cond = (cv / (1.0 + jnp.exp(-cv))).astype(BF16)
        o_ref[...] = jnp.dot(cond, w_ref[...].astype(BF16), preferred_element_type=F32) + b_ref[...]

    return pl.pallas_call(
        body, name="mod_fwd", grid=(cols // tn,),
        in_specs=[pl.BlockSpec((R, D), lambda j: (0, 0)), pl.BlockSpec((D, tn), lambda j: (0, j)),
                  pl.BlockSpec((1, tn), lambda j: (0, j))],
        out_specs=pl.BlockSpec((R, tn), lambda j: (0, j)),
        out_shape=jax.ShapeDtypeStruct((R, cols), F32),
        compiler_params=pltpu.CompilerParams(dimension_semantics=("parallel",), vmem_limit_bytes=_vmem(3 * D * tn * 4)),
    )(cond_in, w_mod, b_shard)


def _mod_update(c_t, dmod, w, m, v):
    D, nb = c_t.shape
    cols = w.shape[1]
    tn = _fit(256, cols)

    def body(c_ref, d_ref, w_ref, m_ref, v_ref, g_ref, dl_ref, nm_ref, nv_ref):
        cv = c_ref[...]
        cond = cv / (1.0 + jnp.exp(-cv))
        g = jnp.zeros((D, tn), F32)
        for b in range(nb):
            g = g + cond[:, b:b + 1] * d_ref[b:b + 1, :]
        g_ref[...] = g
        dl_ref[...], nm_ref[...], nv_ref[...] = _adamw(w_ref[...], g, m_ref[...], v_ref[...])

    blk = pl.BlockSpec((D, tn), lambda j: (0, j))
    out = jax.ShapeDtypeStruct((D, cols), F32)
    return pl.pallas_call(
        body, name="mod_update", grid=(cols // tn,),
        in_specs=[pl.BlockSpec((D, nb), lambda j: (0, 0)), pl.BlockSpec((nb, tn), lambda j: (0, j)), blk, blk, blk],
        out_specs=[blk] * 4, out_shape=[out] * 4,
        compiler_params=pltpu.CompilerParams(dimension_semantics=("parallel",), vmem_limit_bytes=_vmem(18 * D * tn * 4)),
    )(c_t, dmod, w, m, v)


def _adam_update(name, w, g, m, v):
    R, C = w.shape
    tr = _fit(256, R)

    def body(w_ref, g_ref, m_ref, v_ref, dl_ref, nm_ref, nv_ref):
        dl_ref[...], nm_ref[...], nv_ref[...] = _adamw(w_ref[...], g_ref[...], m_ref[...], v_ref[...])

    blk = pl.BlockSpec((tr, C), lambda r: (r, 0))
    out = jax.ShapeDtypeStruct((R, C), F32)
    padded = -(-C // LANES) * LANES
    return pl.pallas_call(
        body, name=name, grid=(R // tr,), in_specs=[blk] * 4, out_specs=[blk] * 3, out_shape=[out] * 3,
        compiler_params=pltpu.CompilerParams(dimension_semantics=("parallel",), vmem_limit_bytes=_vmem(16 * tr * padded * 4)),
    )(w, g, m, v)


def _sum_blocks(name, stacked, n):
    R, C = stacked.shape[0] // n, stacked.shape[1]

    def body(s_ref, o_ref):
        total = s_ref[0:R, :]
        for d in range(1, n):
            total = total + s_ref[d * R:(d + 1) * R, :]
        o_ref[...] = total

    return pl.pallas_call(body, name=name, out_shape=jax.ShapeDtypeStruct((R, C), F32))(stacked)


def _place():
    return lax.axis_index("x"), lax.axis_index("y"), lax.axis_index("c")


def _allgather8(name, block):
    m_per, n = block.shape

    def body(x_ref, out_ref, send_sems, recv_sems, local_sem):
        x, y, c = _place()
        me, sibling = (x, y, c), (x, y, 1 - c)
        chips = [(1 - x, y), (x, 1 - y), (1 - x, 1 - y)]

        def rows(px, py, pc):
            return out_ref.at[pl.ds((4 * px + 2 * py + pc) * m_per, m_per), :]

        def copy(k, blk, to, src=None):
            return pltpu.make_async_remote_copy(
                src_ref=rows(*blk) if src is None else src, dst_ref=rows(*blk),
                send_sem=send_sems.at[k], recv_sem=recv_sems.at[k], device_id=to, device_id_type=MESH)

        mine = pltpu.make_async_copy(x_ref, rows(*me), local_sem)
        mine.start()
        first = [copy(0, me, sibling, src=x_ref)]
        first += [copy(1 + j, me, (*chip, c), src=x_ref) for j, chip in enumerate(chips)]
        for cp in first:
            cp.start()
        passed = [copy(4 + j, (*chip, c), sibling) for j, chip in enumerate(chips)]
        for j, chip in enumerate(chips):
            copy(1 + j, (*chip, c), me).wait_recv()
            passed[j].start()
        copy(0, sibling, me).wait_recv()
        for j, chip in enumerate(chips):
            copy(4 + j, (*chip, 1 - c), me).wait_recv()
        for cp in first + passed:
            cp.wait_send()
        mine.wait()

    return pl.pallas_call(
        body, name=name, out_shape=jax.ShapeDtypeStruct((N_DEV * m_per, n), block.dtype),
        in_specs=[pl.BlockSpec(memory_space=pltpu.VMEM)], out_specs=pl.BlockSpec(memory_space=pltpu.VMEM),
        scratch_shapes=[pltpu.SemaphoreType.DMA((7,)), pltpu.SemaphoreType.DMA((7,)), pltpu.SemaphoreType.DMA],
    )(block)


_ANY = pl.BlockSpec(memory_space=pl.ANY)


def _half(ref, c, rows):
    return ref.at[pl.ds(c * (rows // 2), rows // 2), :]


_HBM = pl.BlockSpec(memory_space=pltpu.HBM)
_SEM = pl.BlockSpec(memory_space=pltpu.SEMAPHORE)
_EFFECT = pltpu.SideEffectType.DATAFLOW_SIDE_EFFECTING


def _ici_start(name, srcs, land_shapes, plan):
    ns, nl = len(srcs), len(land_shapes)
    n_copies = 3 * ns

    def body(*refs):
        src_refs, land_refs = refs[:ns], refs[ns:ns + nl]
        send_sems, recv_sems = refs[ns + nl], refs[ns + nl + 1]
        token = refs[-1]
        for n, (src, dst, peer, _) in enumerate(plan(src_refs, land_refs)):
            pltpu.make_async_remote_copy(src_ref=src, dst_ref=dst, send_sem=send_sems.at[n], recv_sem=recv_sems.at[n],
                                         device_id=peer, device_id_type=MESH).start()
        token[...] = jnp.zeros_like(token)

    lands = [lax.empty(s.shape, s.dtype) for s in land_shapes]
    out = pl.pallas_call(
        body, name=name,
        out_shape=(pltpu.SemaphoreType.DMA((n_copies,)), pltpu.SemaphoreType.DMA((n_copies,)),
                   *[pltpu.HBM(a.shape, a.dtype) for a in list(srcs) + lands], jax.ShapeDtypeStruct((8, LANES), F32)),
        in_specs=[_HBM] * (ns + nl),
        out_specs=(_SEM, _SEM, *[_HBM] * (ns + nl), pl.BlockSpec(memory_space=pltpu.VMEM)),
        input_output_aliases={n: 2 + n for n in range(ns + nl)},
        compiler_params=pltpu.CompilerParams(has_side_effects=_EFFECT),
    )(*[pltpu.with_memory_space_constraint(a, pltpu.HBM) for a in list(srcs) + lands])
    return out[0], out[1], list(out[2:2 + ns]), list(out[2 + ns:2 + ns + nl]), out[-1]


def _ici_wait(name, send_sems, recv_sems, srcs, lands, plan, after):
    ns, nl = len(srcs), len(lands)

    def body(*refs):
        src_refs, land_refs = refs[:ns], refs[ns:ns + nl]
        send_sems, recv_sems = refs[ns + nl], refs[ns + nl + 1]
        for n, (src, _, peer, mine) in enumerate(plan(src_refs, land_refs)):
            cp = pltpu.make_async_remote_copy(src_ref=src, dst_ref=mine, send_sem=send_sems.at[n],
                                              recv_sem=recv_sems.at[n], device_id=peer, device_id_type=MESH)
            cp.wait_send()
            cp.wait_recv()

    out = pl.pallas_call(
        body, name=name, out_shape=[pltpu.HBM(a.shape, a.dtype) for a in list(srcs) + list(lands)],
        in_specs=[_HBM] * (ns + nl) + [_SEM, _SEM, _ANY], out_specs=[_HBM] * (ns + nl),
        input_output_aliases={n: n for n in range(ns + nl)},
        compiler_params=pltpu.CompilerParams(has_side_effects=_EFFECT),
    )(*srcs, *lands, send_sems, recv_sems, after)
    return list(out[:ns]), list(out[ns:])


def _gather_plan(src_refs, land_refs):
    x, y, c = _place()
    copies = []
    for w, land in zip(src_refs, land_refs):
        R = w.shape[0]
        for cx, cy in [(1 - x, y), (x, 1 - y), (1 - x, 1 - y)]:
            copies.append((_half(w, c, R), _half(land.at[2 * x + y], c, R), (cx, cy, c),
                           _half(land.at[2 * cx + cy], c, R)))
    return copies


def _pass_to_sibling(name, lands):
    nw = len(lands)

    def body(*refs):
        ins, outs = refs[:nw], refs[nw:2 * nw]
        send_sems, recv_sems = refs[2 * nw:]
        x, y, c = _place()
        chips = [(1 - x, y), (x, 1 - y), (1 - x, 1 - y)]
        copies = []
        for k in range(nw):
            R = ins[k].shape[1]
            for j, (cx, cy) in enumerate(chips):
                cp = pltpu.make_async_remote_copy(
                    src_ref=_half(ins[k].at[2 * cx + cy], c, R), dst_ref=_half(outs[k].at[2 * cx + cy], c, R),
                    send_sem=send_sems.at[3 * k + j], recv_sem=recv_sems.at[3 * k + j],
                    device_id=(x, y, 1 - c), device_id_type=MESH)
                cp.start()
                copies.append(cp)
        for k in range(nw):
            R = ins[k].shape[1]
            for j, (cx, cy) in enumerate(chips):
                pltpu.make_async_remote_copy(
                    src_ref=_half(ins[k].at[2 * cx + cy], c, R), dst_ref=_half(outs[k].at[2 * cx + cy], 1 - c, R),
                    send_sem=send_sems.at[3 * k + j], recv_sem=recv_sems.at[3 * k + j],
                    device_id=(x, y, 1 - c), device_id_type=MESH).wait_recv()
        for cp in copies:
            cp.wait_send()

    return pl.pallas_call(
        body, name=name, out_shape=[jax.ShapeDtypeStruct(a.shape, a.dtype) for a in lands],
        in_specs=[_ANY] * nw, out_specs=[_ANY] * nw, input_output_aliases={k: k for k in range(nw)},
        scratch_shapes=[pltpu.SemaphoreType.DMA((3 * nw,)), pltpu.SemaphoreType.DMA((3 * nw,))],
    )(*lands)


def _tie(vec, token):
    return vec + token[0:1, 0:1]


def _pair_exchange(name, grads):
    nw = len(grads)

    def body(*refs):
        gs, outs = refs[:nw], refs[nw:2 * nw]
        send_sems, recv_sems = refs[2 * nw:]
        x, y, c = _place()
        copies = []
        for k in range(nw):
            half = gs[k].shape[1] // 2
            cp = pltpu.make_async_remote_copy(
                src_ref=gs[k].at[:, pl.ds((1 - c) * half, half), :], dst_ref=outs[k],
                send_sem=send_sems.at[k], recv_sem=recv_sems.at[k], device_id=(x, y, 1 - c), device_id_type=MESH)
            cp.start()
            copies.append(cp)
        for cp in copies:
            cp.wait()

    return pl.pallas_call(
        body, name=name,
        out_shape=[jax.ShapeDtypeStruct((N_CHIPS, g.shape[1] // 2, g.shape[2]), g.dtype) for g in grads],
        in_specs=[_ANY] * nw, out_specs=[_ANY] * nw,
        scratch_shapes=[pltpu.SemaphoreType.DMA((nw,)), pltpu.SemaphoreType.DMA((nw,))],
    )(*grads)


def _pair_add(name, core, grad, recv):
    n, R, C = grad.shape
    half = R // 2
    tr = _fit(256, half)
    nblk = half // tr

    def body(core_ref, g_ref, r_ref, o_ref):
        o_ref[...] = (g_ref[...].astype(F32) + r_ref[...].astype(F32)).astype(BF16)

    grid_spec = pltpu.PrefetchScalarGridSpec(
        num_scalar_prefetch=1, grid=(n, nblk),
        in_specs=[pl.BlockSpec((1, tr, C), lambda s, r, core_ref: (s, core_ref[0] * nblk + r, 0)),
                  pl.BlockSpec((1, tr, C), lambda s, r, core_ref: (s, r, 0))],
        out_specs=pl.BlockSpec((1, tr, C), lambda s, r, core_ref: (s, r, 0)))
    return pl.pallas_call(
        body, name=name, grid_spec=grid_spec, out_shape=jax.ShapeDtypeStruct((n, half, C), BF16),
        compiler_params=pltpu.CompilerParams(dimension_semantics=("parallel", "parallel")),
    )(core, grad, recv)


def _scatter_plan(src_refs, land_refs):
    x, y, c = _place()
    copies = []
    for p, land in zip(src_refs, land_refs):
        for j, (cx, cy) in enumerate([(1 - x, y), (x, 1 - y), (1 - x, 1 - y)]):
            copies.append((p.at[2 * cx + cy], land.at[j], (cx, cy, c), land.at[j]))
    return copies


def _chip_add(name, chip, sums, recv):
    _, H, C = sums.shape
    tr = _fit(256, H)

    def body(chip_ref, p_ref, r_ref, o_ref):
        total = p_ref[0].astype(F32)
        for j in range(3):
            total = total + r_ref[j].astype(F32)
        o_ref[...] = total

    grid_spec = pltpu.PrefetchScalarGridSpec(
        num_scalar_prefetch=1, grid=(H // tr,),
        in_specs=[pl.BlockSpec((1, tr, C), lambda r, chip_ref: (chip_ref[0], r, 0)),
                  pl.BlockSpec((3, tr, C), lambda r, chip_ref: (0, r, 0))],
        out_specs=pl.BlockSpec((tr, C), lambda r, chip_ref: (r, 0)))
    return pl.pallas_call(
        body, name=name, grid_spec=grid_spec, out_shape=jax.ShapeDtypeStruct((H, C), F32),
        compiler_params=pltpu.CompilerParams(dimension_semantics=("parallel",)),
    )(chip, sums, recv)


def _pair_share(name, halves):
    nw = len(halves)

    def body(*refs):
        hs, outs = refs[:nw], refs[nw:2 * nw]
        send_sems, recv_sems = refs[2 * nw:]
        x, y, c = _place()
        copies = []
        for k in range(nw):
            cp = pltpu.make_async_remote_copy(
                src_ref=hs[k], dst_ref=outs[k], send_sem=send_sems.at[k], recv_sem=recv_sems.at[k],
                device_id=(x, y, 1 - c), device_id_type=MESH)
            cp.start()
            copies.append(cp)
        for cp in copies:
            cp.wait()

    return pl.pallas_call(
        body, name=name,
        out_shape=[jax.ShapeDtypeStruct(h.shape, h.dtype) for h in halves],
        in_specs=[_ANY] * nw, out_specs=[_ANY] * nw,
        scratch_shapes=[pltpu.SemaphoreType.DMA((nw,)), pltpu.SemaphoreType.DMA((nw,))],
    )(*halves)


def _adam_halves(name, core, w, g_own, g_other, m, v):
    R, C = w.shape
    H = R // 2
    tr = _fit(256, H)
    nblk = H // tr

    def body(core_ref, w_ref, go_ref, gr_ref, m_ref, v_ref, g_ref, dl_ref, nm_ref, nv_ref):
        own = (pl.program_id(0) // nblk) == core_ref[0]
        g = jnp.where(own, go_ref[...], gr_ref[...])
        g_ref[...] = g
        dl_ref[...], nm_ref[...], nv_ref[...] = _adamw(w_ref[...], g, m_ref[...], v_ref[...])

    blk = pl.BlockSpec((tr, C), lambda r, core_ref: (r, 0))
    half = pl.BlockSpec((tr, C), lambda r, core_ref: (r % nblk, 0))
    out = jax.ShapeDtypeStruct((R, C), F32)
    padded = -(-C // LANES) * LANES
    grid_spec = pltpu.PrefetchScalarGridSpec(
        num_scalar_prefetch=1, grid=(R // tr,), in_specs=[blk, half, half, blk, blk], out_specs=[blk] * 4)
    return pl.pallas_call(
        body, name=name, grid_spec=grid_spec, out_shape=[out] * 4,
        compiler_params=pltpu.CompilerParams(dimension_semantics=("parallel",), vmem_limit_bytes=_vmem(20 * tr * padded * 4)),
    )(core, w, g_own, g_other, m, v)


def kernel(x, c, w_mod, b_mod, g_pre_mix, g_post_mix, w_in, b_forget, swa_sinks, w_out, g_pre_mlp, g_post_mlp, w_up, w_down, loss_target, m_w_mod, m_b_mod, m_g_pre_mix, m_g_post_mix, m_w_in, m_b_forget, m_swa_sinks, m_w_out, m_g_pre_mlp, m_g_post_mlp, m_w_up, m_w_down, v_w_mod, v_b_mod, v_g_pre_mix, v_g_post_mix, v_w_in, v_b_forget, v_swa_sinks, v_w_out, v_g_pre_mlp, v_g_post_mlp, v_w_up, v_w_down):
    S, D = x.shape[1], x.shape[2]
    n_heads = D // HEAD_DIM
    n_fox = n_heads // 2
    n_swa = n_heads - n_fox
    n_kv = max(1, n_swa // 4)
    fox_w, swa_w, kv_w = n_fox * HEAD_DIM, n_swa * HEAD_DIM, n_kv * HEAD_DIM
    main_w = 3 * fox_w + swa_w + 2 * kv_w
    in_w = main_w + n_fox
    mod_cols = w_mod.shape[2]

    ax, ay, ac = _place()
    chip = 2 * ax + ay
    dev = 2 * chip + ac
    chip_arr = jnp.reshape(chip, (1,)).astype(jnp.int32)
    core_arr = jnp.reshape(ac, (1,)).astype(jnp.int32)

    x2, tgt = x[0], loss_target[0]

    names = ["w_in", "w_out", "w_up", "w_down"]
    flights = {}
    token = jnp.zeros((8, LANES), F32)
    for n, w in zip(names, [w_in, w_out, w_up, w_down]):
        shard = _tie(w[0], token).astype(BF16)
        flights[n] = _ici_start("gather_start_" + n, [shard], [jax.ShapeDtypeStruct((N_CHIPS,) + shard.shape, BF16)],
                                _gather_plan)
        token = flights[n][4]
    c = _tie(c, token)

    def gathered(n, after):
        send, recv, srcs, lands, _ = flights[n]
        srcs, lands = _ici_wait("gather_wait_" + n, send, recv, srcs, lands, _gather_plan, after)
        lands = _pass_to_sibling("gather_pass_" + n, lands)
        return lax.dynamic_update_index_in_dim(lands[0], srcs[0], chip, 0)

    d_ff = N_CHIPS * w_up.shape[2]

    c_all = _allgather8("gather_c", c.reshape(8, D // 8)).reshape(N_DEV, D)
    b_shard = lax.dynamic_slice_in_dim(b_mod, chip * mod_cols, mod_cols, axis=1)
    mod_shard = _mod_fwd(jnp.pad(c_all, ((0, 16 - N_DEV), (0, 0))), w_mod[0], b_shard)[:N_DEV]
    mod_all = _allgather8("gather_mod", mod_shard).reshape(N_CHIPS, 2, N_DEV, mod_cols)[:, 0]
    mod = lax.dynamic_index_in_dim(mod_all, dev, axis=1, keepdims=False).reshape(N_MOD, 1, D)
    sh_a, sc_a, gt_a, sh_m, sc_m, gt_m = [mod[n] for n in range(N_MOD)]

    h = _pre_norm(x2, g_pre_mix, sc_a, sh_a)
    w_in_f = jnp.transpose(gathered("w_in", h), (1, 0, 2)).reshape(D, in_w)
    w_main = jnp.concatenate([w_in_f[:, :3 * fox_w], w_in_f[:, 3 * fox_w + n_fox:]], axis=1)
    w_fg = jnp.pad(w_in_f[:, 3 * fox_w:3 * fox_w + n_fox], ((0, 0), (0, LANES - n_fox)))
    proj = _mm_plain("in_proj", h, w_main, "nn", BF16, tn=_fit(768, main_w))
    fg = _mm_plain("in_proj_gate", h, w_fg, "nn", F32)
    b_pad = jnp.pad(b_forget, ((0, 0), (0, LANES - n_fox)))
    cum, cum_t, cum_b = _fox_gate_fwd(fg, b_pad, n_fox)
    cum_row = cum_t[:n_fox].reshape(n_fox, 1, S)
    fox_o, fox_lse = _fox_fwd(proj, cum_b, cum_row, n_fox)

    cos, sin_signed = _rope_tables(S)
    rq = _rope("rope_fwd", proj, 3 * n_fox, n_swa + n_kv, cos, sin_signed)
    v_first = 3 * n_fox + n_swa + n_kv
    sinks = swa_sinks[0]
    swa_o, swa_lse = _swa_fwd(rq, proj, v_first, sinks, n_swa, n_kv)

    mixcat = jnp.concatenate([fox_o, swa_o], axis=1).astype(BF16)
    w_out_f = gathered("w_out", mixcat).reshape(D, D)
    mix = _mm_plain("out_proj", mixcat, w_out_f, "nn", F32)
    x1, h2 = _post_mix(x2, mix, g_post_mix, gt_a, g_pre_mlp, sc_m, sh_m)
    w_up_f = jnp.transpose(gathered("w_up", h2), (1, 0, 2)).reshape(D, d_ff)

    tm_u, tn_u = _fit(1024, S), _fit(1024, d_ff)

    def up_epilogue(acc, ex, outs):
        outs[0][...] = acc.astype(BF16)
        r = jnp.maximum(acc, 0.0)
        outs[1][...] = (r * r).astype(BF16)

    ublk = ((S, d_ff), BF16, (tm_u, tn_u), lambda i, j: (i, j))
    u, a = _matmul("mlp_up", h2, w_up_f, "nn", [ublk, ublk], up_epilogue)
    w_down_f = gathered("w_down", a).reshape(d_ff, D)
    y = _mm_plain("mlp_down", a, w_down_f, "nn", F32)

    dy, dout, loss_part, acc_mlp_post = _loss_and_post_mlp_bwd(x1, y, tgt, g_post_mlp, gt_m)
    loss = lax.psum(loss_part[0, 0], ("x", "y", "c"))

    def du_epilogue(acc, ex, outs):
        outs[0][...] = (acc * (2.0 * jnp.maximum(ex[0][...].astype(F32), 0.0))).astype(BF16)

    du = _matmul("mlp_down_bwd", dy, w_down_f, "nt", [ublk], du_epilogue,
                 extras=[(u, (tm_u, tn_u), lambda i, j: (i, j))])[0]
    g_down = _mm_plain("grad_w_down", a, dy, "tn", BF16)
    tn_s = _fit(1024, w_up.shape[2])
    per = w_up.shape[2] // tn_s

    def shard_epilogue(acc, ex, outs):
        outs[0][0] = acc.astype(BF16)

    g_up = _matmul("grad_w_up", h2, du, "tn",
                   [((N_CHIPS, D, w_up.shape[2]), BF16, (1, _fit(1024, D), tn_s), lambda i, j: (j // per, i, j % per))],
                   shard_epilogue, tn=tn_s)[0]

    def reduce_start(tag, fulls):
        from_sibling = _pair_exchange("grad_pair_exchange_" + tag, fulls)
        sums = [_pair_add("pair_add_%s_%d" % (tag, k), core_arr, g, r) for k, (g, r) in enumerate(zip(fulls, from_sibling))]
        return _ici_start("grad_scatter_start_" + tag, sums,
                          [jax.ShapeDtypeStruct((3,) + p.shape[1:], BF16) for p in sums], _scatter_plan)

    def reduce_finish(tag, flight, after):
        send, recv, srcs, lands, _ = flight
        sums, received = _ici_wait("grad_scatter_wait_" + tag, send, recv, srcs, lands, _scatter_plan, after)
        halves = [_chip_add("chip_add_%s_%d" % (tag, k), chip_arr, p, r) for k, (p, r) in enumerate(zip(sums, received))]
        return halves, _pair_share("grad_pair_share_" + tag, halves)

    flight_mlp = reduce_start("mlp", [g_up, g_down.reshape(N_CHIPS, d_ff // N_CHIPS, D)])
    dh2 = _mm_plain("mlp_up_bwd", du, w_up_f, "nt", F32)
    dx1, dmix, acc_mid = _pre_mlp_and_post_mix_bwd(dh2, x1, dout, mix, _tie(g_pre_mlp, flight_mlp[4]), sc_m,
                                                   g_post_mix, gt_a)

    dmixcat = _mm_plain("out_proj_bwd", dmix, w_out_f, "nt", F32)
    g_out = _mm_plain("grad_w_out", mixcat, dmix, "tn", BF16)

    fdq, fdk, fdv, dcum_row, dcum_q = _fox_bwd(proj, fox_o, dmixcat, fox_lse, cum_b, cum_row, n_fox)
    dcum_k = jnp.pad(dcum_row.reshape(n_fox, S), ((0, LANES - n_fox), (0, 0)))
    dfg, db_forget = _fox_gate_bwd(dcum_k, dcum_q, fg, b_pad)

    group_w = (n_swa // n_kv) * HEAD_DIM
    sdq, sdk, sdv, dsink = _swa_bwd(rq, proj, v_first, sinks, swa_o, dmixcat, fox_w // group_w, swa_lse, n_swa, n_kv)
    drq = jnp.concatenate([sdq, jnp.transpose(sdk, (1, 0, 2)).reshape(S, kv_w).astype(BF16)], axis=1)
    d_sq_sk = _rope("rope_bwd", drq, 0, n_swa + n_kv, cos, -sin_signed)
    dsv = jnp.transpose(sdv, (1, 0, 2)).reshape(S, kv_w).astype(BF16)
    dproj = jnp.concatenate([fdq, fdk, fdv, d_sq_sk, dsv], axis=1)

    g_main = _mm_plain("grad_w_in", h, dproj, "tn", BF16, tn=_fit(768, main_w))
    g_fg = _mm_plain("grad_w_in_gate", h, dfg, "tn", BF16)
    g_in_f = jnp.concatenate([g_main[:, :3 * fox_w], g_fg[:, :n_fox], g_main[:, 3 * fox_w:]], axis=1)
    flight_mix = reduce_start("mix", [jnp.transpose(g_in_f.reshape(D, N_CHIPS, in_w // N_CHIPS), (1, 0, 2)),
                                      g_out.reshape(N_CHIPS, D // N_CHIPS, D)])
    dh_gate = _mm_plain("in_proj_gate_bwd", dfg, w_fg, "nt", F32)

    def add_epilogue(acc, ex, outs):
        outs[0][...] = acc + ex[0][...]

    tm_h, tn_h = _fit(1024, S), _fit(1024, D)
    dh = _matmul("in_proj_bwd", dproj, w_main, "nt", [((S, D), F32, (tm_h, tn_h), lambda i, j: (i, j))], add_epilogue,
                 extras=[(dh_gate, (tm_h, tn_h), lambda i, j: (i, j))], tk=_fit(768, main_w))[0]
    grad_x, acc_pre = _pre_mix_bwd(dh, x2, dx1, _tie(g_pre_mix, flight_mix[4]), sc_a)
    halves_mlp, others_mlp = reduce_finish("mlp", flight_mlp, grad_x)

    zero_row = jnp.zeros((1, D), F32)
    tail = jnp.concatenate([db_forget[0:1, :n_fox], dsink[:, 0, :n_swa // n_kv].reshape(1, n_swa),
                            jnp.zeros((1, D - n_fox - n_swa), F32)], axis=1)
    partial = jnp.concatenate([
        acc_pre[0:1], acc_pre[1:2], acc_mid[3:4], acc_mid[0:1], acc_mid[1:2], acc_mlp_post[0:1],
        acc_pre[2:3], acc_mid[4:5], acc_mid[2:3], acc_mlp_post[1:2], tail] + [zero_row] * 5, axis=0)
    gathered = _allgather8("gather_small_grads", partial)
    small = _sum_blocks("sum_small_grads", gathered, N_DEV)
    g_b_mod = small[0:N_MOD].reshape(1, N_MOD * D)
    g_small = {"g_pre_mix": small[6:7], "g_post_mix": small[7:8], "g_pre_mlp": small[8:9], "g_post_mlp": small[9:10],
               "b_forget": small[10:11, :n_fox], "swa_sinks": small[10:11, n_fox:n_fox + n_swa]}

    dmod_all = gathered.reshape(N_DEV, 16, D)[:, :N_MOD].reshape(N_DEV, N_MOD * D)
    dmod_shard = lax.dynamic_slice_in_dim(dmod_all, chip * mod_cols, mod_cols, axis=1)
    g_w_mod, d_w_mod, nm_w_mod, nv_w_mod = _mod_update(c_all.T, dmod_shard, w_mod[0], m_w_mod[0], v_w_mod[0])

    grads = {"w_mod": g_w_mod[None], "b_mod": g_b_mod}
    deltas = {"w_mod": d_w_mod[None]}
    new_m = {"w_mod": nm_w_mod[None]}
    new_v = {"w_mod": nv_w_mod[None]}
    weights = {"w_in": (w_in, m_w_in, v_w_in), "w_out": (w_out, m_w_out, v_w_out), "w_up": (w_up, m_w_up, v_w_up),
               "w_down": (w_down, m_w_down, v_w_down)}

    def big_update(n, own, other):
        w, m, v = weights[n]
        g, d_, m_, v_ = _adam_halves("adam_" + n, core_arr, w[0], own, other, m[0], v[0])
        grads[n], deltas[n], new_m[n], new_v[n] = g[None], d_[None], m_[None], v_[None]

    big_update("w_up", halves_mlp[0], others_mlp[0])
    big_update("w_down", halves_mlp[1], others_mlp[1])
    ran = deltas["w_down"][0, :8, :LANES] + deltas["w_up"][0, :8, :LANES] + d_w_mod[:8, :LANES]
    halves_mix, others_mix = reduce_finish("mix", flight_mix, ran)
    big_update("w_in", halves_mix[0], others_mix[0])
    big_update("w_out", halves_mix[1], others_mix[1])
    small_w = {"b_mod": (b_mod, m_b_mod, v_b_mod), "g_pre_mix": (g_pre_mix, m_g_pre_mix, v_g_pre_mix),
               "g_post_mix": (g_post_mix, m_g_post_mix, v_g_post_mix), "b_forget": (b_forget, m_b_forget, v_b_forget),
               "swa_sinks": (swa_sinks, m_swa_sinks, v_swa_sinks), "g_pre_mlp": (g_pre_mlp, m_g_pre_mlp, v_g_pre_mlp),
               "g_post_mlp": (g_post_mlp, m_g_post_mlp, v_g_post_mlp)}
    g_small["b_mod"] = g_b_mod
    for n, (w, m, v) in small_w.items():
        g = g_small[n]
        grads[n] = g
        deltas[n], new_m[n], new_v[n] = _adam_update("adam_" + n, w, g, m, v)

    order = ["w_mod", "b_mod", "g_pre_mix", "g_post_mix", "w_in", "b_forget", "swa_sinks", "w_out", "g_pre_mlp",
             "g_post_mlp", "w_up", "w_down"]
    return (loss, grad_x[None], *[grads[n] for n in order], *[deltas[n] for n in order],
            *[new_m[n] for n in order], *[new_v[n] for n in order])
```

```python
import jax
import jax.numpy as jnp
from jax import lax
from jax.experimental import pallas as pl
from jax.experimental.pallas import tpu as pltpu

F32 = jnp.float32
BF16 = jnp.bfloat16
MESH = pl.DeviceIdType.MESH

HEAD_DIM = 128
SWA_BLOCK = 128
ROPE_THETA = 10000.0
NORM_EPS = 1e-6
NEG = -1e30
N_MOD = 6
ADAM_LR = 0.001
ADAM_B1 = 0.9
ADAM_B2 = 0.999
ADAM_EPS = 1e-08
ADAM_WD = 0.01
ADAM_STEP = 10
N_CHIPS = 4
N_DEV = 8
LANES = 128
VMEM_CAP = 60 * 1024 * 1024

_NN = (((1,), (0,)), ((), ()))
_NT = (((1,), (1,)), ((), ()))
_TN = (((0,), (0,)), ((), ()))


def _vmem(nbytes):
    return int(min(VMEM_CAP, nbytes * 5 // 4 + (4 << 20)))


def _nbytes(shape, dtype):
    n = 1
    for s in shape:
        n *= s
    return n * jnp.dtype(dtype).itemsize


def _fit(t, n):
    t = min(t, n)
    assert n % t == 0, (t, n)
    return t


MM_TM, MM_TN, MM_TK = 512, 1024, 2048


def _matmul(name, a, b, mode, out_defs, epilogue, extras=(), tm=MM_TM, tn=MM_TN, tk=MM_TK):
    if mode == "nn":
        (M, K), (K2, N) = a.shape, b.shape
    elif mode == "nt":
        (M, K), (N, K2) = a.shape, b.shape
    else:
        (K, M), (K2, N) = a.shape, b.shape
    assert K == K2, (a.shape, b.shape, mode)
    tm, tn, tk = _fit(tm, M), _fit(tn, N), _fit(tk, K)
    nk = K // tk
    dims = {"nn": _NN, "nt": _NT, "tn": _TN}[mode]
    a_spec = (pl.BlockSpec((tk, tm), lambda i, j, k: (k, i)) if mode == "tn"
              else pl.BlockSpec((tm, tk), lambda i, j, k: (i, k)))
    b_spec = (pl.BlockSpec((tn, tk), lambda i, j, k: (j, k)) if mode == "nt"
              else pl.BlockSpec((tk, tn), lambda i, j, k: (k, j)))
    n_ex, n_out = len(extras), len(out_defs)

    def body(*refs):
        a_ref, b_ref = refs[0], refs[1]
        ex = refs[2:2 + n_ex]
        outs = refs[2 + n_ex:2 + n_ex + n_out]
        prod = lax.dot_general(a_ref[...], b_ref[...], dims, preferred_element_type=F32)
        if nk == 1:
            epilogue(prod, ex, outs)
        else:
            acc_ref = refs[-1]
            k = pl.program_id(2)

            @pl.when(k == 0)
            def _():
                acc_ref[...] = prod

            @pl.when(k > 0)
            def _():
                acc_ref[...] += prod

            @pl.when(k == nk - 1)
            def _():
                epilogue(acc_ref[...], ex, outs)

    def wrap(f):
        return lambda i, j, k: f(i, j)

    in_specs = [a_spec, b_spec] + [pl.BlockSpec(blk, wrap(f)) for _, blk, f in extras]
    out_specs = [pl.BlockSpec(blk, wrap(f)) for _, _, blk, f in out_defs]
    out_shape = [jax.ShapeDtypeStruct(s, d) for s, d, _, _ in out_defs]
    need = 2 * (tm * tk + tk * tn) * a.dtype.itemsize + 3 * tm * tn * 4
    need += sum(2 * _nbytes(blk, arr.dtype) for arr, blk, _ in extras)
    need += sum(2 * _nbytes(blk, d) for _, d, blk, _ in out_defs)
    res = pl.pallas_call(
        body, name=name, grid=(M // tm, N // tn, nk),
        in_specs=in_specs, out_specs=out_specs, out_shape=out_shape,
        scratch_shapes=[pltpu.VMEM((tm, tn), F32)] if nk > 1 else [],
        compiler_params=pltpu.CompilerParams(
            dimension_semantics=("parallel", "parallel", "arbitrary"), vmem_limit_bytes=_vmem(need)),
    )(a, b, *[arr for arr, _, _ in extras])
    return res


def _mm_plain(name, a, b, mode, out_dtype, **tiles):
    if mode == "nn":
        M, N = a.shape[0], b.shape[1]
    elif mode == "nt":
        M, N = a.shape[0], b.shape[0]
    else:
        M, N = a.shape[1], b.shape[1]
    tm, tn = _fit(tiles.get("tm", MM_TM), M), _fit(tiles.get("tn", MM_TN), N)

    def epi(acc, ex, outs):
        outs[0][...] = acc.astype(out_dtype)

    return _matmul(name, a, b, mode, [((M, N), out_dtype, (tm, tn), lambda i, j: (i, j))], epi, **tiles)[0]


def _rstd(v):
    return lax.rsqrt(jnp.mean(v * v, axis=-1, keepdims=True) + NORM_EPS)


def _row_call(name, body, row_ins, vec_ins, row_outs, acc_outs, S, D, tr):
    tr = _fit(tr, S)
    row_spec = pl.BlockSpec((tr, D), lambda r: (r, 0))
    vec_spec = pl.BlockSpec((1, D), lambda r: (0, 0))
    in_specs = [row_spec] * len(row_ins) + [vec_spec] * len(vec_ins)
    out_specs = [row_spec] * len(row_outs) + [pl.BlockSpec(shp, lambda r: (0, 0)) for shp in acc_outs]
    out_shape = [jax.ShapeDtypeStruct((S, D), d) for d in row_outs] + [jax.ShapeDtypeStruct(shp, F32) for shp in acc_outs]
    need = sum(2 * tr * D * a.dtype.itemsize for a in row_ins) + sum(2 * tr * D * jnp.dtype(d).itemsize for d in row_outs)
    need += 8 * tr * D * 4
    return pl.pallas_call(
        body, name=name, grid=(S // tr,), in_specs=in_specs, out_specs=out_specs, out_shape=out_shape,
        compiler_params=pltpu.CompilerParams(dimension_semantics=("arbitrary",), vmem_limit_bytes=_vmem(need)),
    )(*row_ins, *vec_ins)


def _acc_rows(ref, rows):
    @pl.when(pl.program_id(0) == 0)
    def _():
        ref[...] = jnp.zeros_like(ref)
    for n, r in enumerate(rows):
        ref[n:n + 1, :] += r


def _pre_norm(x, g, sc, sh):
    S, D = x.shape

    def body(x_ref, g_ref, sc_ref, sh_ref, h_ref):
        xv = x_ref[...]
        xn = xv * _rstd(xv)
        h_ref[...] = (xn * g_ref[...] * (1.0 + sc_ref[...]) + sh_ref[...]).astype(BF16)

    return _row_call("pre_norm_mix", body, [x], [g, sc, sh], [BF16], [], S, D, 256)[0]


def _post_mix(x, mix, g_post, gt, g_pre, sc, sh):
    S, D = x.shape

    def body(x_ref, mix_ref, gp_ref, gt_ref, g2_ref, sc_ref, sh_ref, x1_ref, h2_ref):
        mv = mix_ref[...]
        x1 = x_ref[...] + gt_ref[...] * (mv * _rstd(mv) * gp_ref[...])
        x1_ref[...] = x1
        h2_ref[...] = (x1 * _rstd(x1) * g2_ref[...] * (1.0 + sc_ref[...]) + sh_ref[...]).astype(BF16)

    return _row_call("post_mix_pre_mlp", body, [x, mix], [g_post, gt, g_pre, sc, sh], [F32, BF16], [], S, D, 256)


def _loss_and_post_mlp_bwd(x1, y, target, g_post, gt):
    S, D = x1.shape

    def body(x1_ref, y_ref, t_ref, g_ref, gt_ref, dy_ref, dout_ref, loss_ref, acc_ref):
        yv = y_ref[...]
        r = _rstd(yv)
        yh = yv * r
        n = yh * g_ref[...]
        diff = x1_ref[...] + gt_ref[...] * n - t_ref[...]
        dout = diff * (1.0 / D)
        dout_ref[...] = dout
        dn = dout * gt_ref[...]
        dyh = dn * g_ref[...]
        dy_ref[...] = (r * (dyh - yh * jnp.mean(dyh * yh, axis=-1, keepdims=True))).astype(BF16)
        _acc_rows(acc_ref, [jnp.sum(dout * n, axis=0, keepdims=True), jnp.sum(dn * yh, axis=0, keepdims=True)])

        @pl.when(pl.program_id(0) == 0)
        def _():
            loss_ref[...] = jnp.zeros_like(loss_ref)
        loss_ref[...] += jnp.full(loss_ref.shape, (0.5 / D) * jnp.sum(diff * diff), F32)

    return _row_call("loss_post_mlp_bwd", body, [x1, y, target], [g_post, gt], [BF16, F32],
                     [(8, LANES), (8, D)], S, D, 128)


def _pre_mlp_and_post_mix_bwd(dh2, x1, dout, mix, g_pre, sc, g_post, gt):
    S, D = x1.shape

    def body(dh_ref, x1_ref, dout_ref, mix_ref, g_ref, sc_ref, gp_ref, gt_ref, dx1_ref, dmix_ref, acc_ref):
        dh = dh_ref[...]
        x1v = x1_ref[...]
        r3 = _rstd(x1v)
        xn = x1v * r3
        dxn = dh * (1.0 + sc_ref[...]) * g_ref[...]
        dx1 = dout_ref[...] + r3 * (dxn - xn * jnp.mean(dxn * xn, axis=-1, keepdims=True))
        dx1_ref[...] = dx1
        mv = mix_ref[...]
        r2 = _rstd(mv)
        mh = mv * r2
        dn = dx1 * gt_ref[...]
        dmh = dn * gp_ref[...]
        dmix_ref[...] = (r2 * (dmh - mh * jnp.mean(dmh * mh, axis=-1, keepdims=True))).astype(BF16)
        _acc_rows(acc_ref, [
            jnp.sum(dh, axis=0, keepdims=True),
            jnp.sum(dh * xn * g_ref[...], axis=0, keepdims=True),
            jnp.sum(dh * (1.0 + sc_ref[...]) * xn, axis=0, keepdims=True),
            jnp.sum(dx1 * mh * gp_ref[...], axis=0, keepdims=True),
            jnp.sum(dn * mh, axis=0, keepdims=True)])

    return _row_call("pre_mlp_post_mix_bwd", body, [dh2, x1, dout, mix], [g_pre, sc, g_post, gt], [F32, BF16],
                     [(8, D)], S, D, 128)


def _pre_mix_bwd(dh, x, dx1, g_pre, sc):
    S, D = x.shape

    def body(dh_ref, x_ref, dx1_ref, g_ref, sc_ref, gx_ref, acc_ref):
        dhv = dh_ref[...]
        xv = x_ref[...]
        r = _rstd(xv)
        xn = xv * r
        dxn = dhv * (1.0 + sc_ref[...]) * g_ref[...]
        gx_ref[...] = dx1_ref[...] + r * (dxn - xn * jnp.mean(dxn * xn, axis=-1, keepdims=True))
        _acc_rows(acc_ref, [
            jnp.sum(dhv, axis=0, keepdims=True),
            jnp.sum(dhv * xn * g_ref[...], axis=0, keepdims=True),
            jnp.sum(dhv * (1.0 + sc_ref[...]) * xn, axis=0, keepdims=True)])

    return _row_call("pre_mix_bwd", body, [dh, x, dx1], [g_pre, sc], [F32], [(8, D)], S, D, 128)


CUM_BLOCK = 256


def _tri(n, upper):
    r = lax.broadcasted_iota(jnp.int32, (n, n), 0)
    c = lax.broadcasted_iota(jnp.int32, (n, n), 1)
    return ((c >= r) if upper else (c <= r)).astype(F32)


def _fox_gate_fwd(fg, b_pad, n_fox):
    S = fg.shape[0]
    cb = _fit(CUM_BLOCK, S)

    def body(fg_ref, b_ref, cum_ref, cumt_ref, cumb_ref):
        low = _tri(cb, False)
        carry = jnp.zeros((1, LANES), F32)
        for n in range(S // cb):
            z = fg_ref[n * cb:(n + 1) * cb, :] + b_ref[...]
            logf = jnp.minimum(z, 0.0) - jnp.log(1.0 + jnp.exp(-jnp.abs(z)))
            blk = jnp.dot(low, logf, precision=lax.Precision.HIGHEST, preferred_element_type=F32) + carry
            cum_ref[n * cb:(n + 1) * cb, :] = blk
            carry = blk[cb - 1:cb, :]
        cum = cum_ref[...]
        cumt_ref[...] = cum.T
        for h in range(n_fox):
            cumb_ref[h] = jnp.broadcast_to(cum[:, h:h + 1], (S, LANES))

    return pl.pallas_call(
        body, name="fox_gate_fwd",
        out_shape=[jax.ShapeDtypeStruct((S, LANES), F32), jax.ShapeDtypeStruct((LANES, S), F32),
                   jax.ShapeDtypeStruct((n_fox, S, LANES), F32)],
        compiler_params=pltpu.CompilerParams(vmem_limit_bytes=_vmem((4 + 2 * n_fox) * S * LANES * 4)),
    )(fg, b_pad)


def _fox_gate_bwd(dcum_k, dcum_q, fg, b_pad):
    S = fg.shape[0]
    n_fox = dcum_q.shape[0]
    cb = _fit(CUM_BLOCK, S)

    def body(dk_ref, dq_ref, fg_ref, b_ref, dfg_ref, db_ref, dc_ref):
        lane = lax.broadcasted_iota(jnp.int32, (S, LANES), 1)
        dc = dk_ref[...].T
        for h in range(n_fox):
            dc = dc + jnp.where(lane == h, dq_ref[h], 0.0)
        dc_ref[...] = dc
        up = _tri(cb, True)
        carry = jnp.zeros((1, LANES), F32)
        db = jnp.zeros((1, LANES), F32)
        for n in reversed(range(S // cb)):
            blk = jnp.dot(up, dc_ref[n * cb:(n + 1) * cb, :], precision=lax.Precision.HIGHEST,
                          preferred_element_type=F32) + carry
            carry = blk[0:1, :]
            z = fg_ref[n * cb:(n + 1) * cb, :] + b_ref[...]
            dfg = blk * (1.0 / (1.0 + jnp.exp(z)))
            dfg_ref[n * cb:(n + 1) * cb, :] = dfg.astype(BF16)
            db = db + jnp.sum(dfg, axis=0, keepdims=True)
        db_ref[...] = jnp.broadcast_to(db, db_ref.shape)

    return pl.pallas_call(
        body, name="fox_gate_bwd",
        out_shape=[jax.ShapeDtypeStruct((S, LANES), BF16), jax.ShapeDtypeStruct((8, LANES), F32)],
        scratch_shapes=[pltpu.VMEM((S, LANES), F32)],
        compiler_params=pltpu.CompilerParams(vmem_limit_bytes=_vmem((8 + 2 * n_fox) * S * LANES * 4)),
    )(dcum_k, dcum_q, fg, b_pad)


FOX_TILE = 512


def _fox_scores(q, k, cq, ck, q0, k0, tq, tk, scale):
    s = lax.dot_general(q, k, _NT, preferred_element_type=F32) * scale + cq - ck
    row = q0 + lax.broadcasted_iota(jnp.int32, (tq, tk), 0)
    col = k0 + lax.broadcasted_iota(jnp.int32, (tq, tk), 1)
    return jnp.where(col <= row, s, NEG)


def _fox_fwd(proj, cum_b, cum_row, n_fox):
    S = proj.shape[0]
    t = _fit(FOX_TILE, S)
    nq = S // t
    scale = HEAD_DIM ** -0.5

    def body(q_ref, k_ref, v_ref, cq_ref, ck_ref, o_ref, lse_ref):
        def q_block(qi, _):
            q0 = pl.multiple_of(qi * t, t)
            q = q_ref[pl.ds(q0, t), :]
            cq = cq_ref[0, pl.ds(q0, t), :][:, :1]

            def kv_block(j, carry):
                m, l, acc = carry
                k0 = pl.multiple_of(j * t, t)
                s = _fox_scores(q, k_ref[pl.ds(k0, t), :], cq, ck_ref[0, :, pl.ds(k0, t)], q0, k0, t, t, scale)
                m_new = jnp.maximum(m, jnp.max(s, axis=-1, keepdims=True))
                alpha = jnp.exp(m - m_new)
                p = jnp.exp(s - m_new)
                l = alpha * l + jnp.sum(p, axis=-1, keepdims=True)
                acc = alpha * acc + jnp.dot(p.astype(BF16), v_ref[pl.ds(k0, t), :], preferred_element_type=F32)
                return m_new, l, acc

            init = (jnp.full((t, 1), NEG, F32), jnp.zeros((t, 1), F32), jnp.zeros((t, HEAD_DIM), F32))
            m, l, acc = lax.fori_loop(0, qi + 1, kv_block, init)
            o_ref[pl.ds(q0, t), :] = acc / l
            lse_ref[0, pl.ds(q0, t), :] = jnp.broadcast_to(m + jnp.log(l), (t, LANES))
            return 0

        lax.fori_loop(0, nq, q_block, 0)

    col = lambda off: pl.BlockSpec((S, HEAD_DIM), lambda h: (0, off + h))
    per_head = pl.BlockSpec((1, S, LANES), lambda h: (h, 0, 0))
    return pl.pallas_call(
        body, name="fox_fwd", grid=(n_fox,),
        in_specs=[col(0), col(n_fox), col(2 * n_fox), per_head, pl.BlockSpec((1, 1, S), lambda h: (h, 0, 0))],
        out_specs=[pl.BlockSpec((S, HEAD_DIM), lambda h: (0, h)), per_head],
        out_shape=[jax.ShapeDtypeStruct((S, n_fox * HEAD_DIM), F32), jax.ShapeDtypeStruct((n_fox, S, LANES), F32)],
        compiler_params=pltpu.CompilerParams(dimension_semantics=("parallel",),
                                             vmem_limit_bytes=_vmem(16 * S * HEAD_DIM * 4 + 12 * t * t * 4)),
    )(proj, proj, proj, cum_b, cum_row)


def _fox_bwd(proj, o, do, lse_b, cum_b, cum_row, n_fox):
    S = proj.shape[0]
    t = _fit(FOX_TILE, S)
    nq = S // t
    scale = HEAD_DIM ** -0.5

    def body(q_ref, k_ref, v_ref, o_ref, do_ref, lse_ref, cq_ref, ck_ref, dq_ref, dk_ref, dv_ref, dc_ref, dcq_ref,
             dq_acc, delta_ref):
        dq_acc[...] = jnp.zeros_like(dq_acc)
        dcq_ref[...] = jnp.zeros_like(dcq_ref)

        def delta_block(qi, _):
            q0 = pl.multiple_of(qi * t, t)
            d = jnp.sum(do_ref[pl.ds(q0, t), :] * o_ref[pl.ds(q0, t), :], axis=-1, keepdims=True)
            delta_ref[pl.ds(q0, t), :] = jnp.broadcast_to(d, (t, LANES))
            return 0

        lax.fori_loop(0, nq, delta_block, 0)

        def kv_block(j, _):
            k0 = pl.multiple_of(j * t, t)
            k = k_ref[pl.ds(k0, t), :]
            v = v_ref[pl.ds(k0, t), :]
            ck = ck_ref[0, :, pl.ds(k0, t)]

            def q_block(qi, carry):
                dk, dv, dc = carry
                q0 = pl.multiple_of(qi * t, t)
                q = q_ref[pl.ds(q0, t), :]
                dov = do_ref[pl.ds(q0, t), :].astype(BF16)
                s = _fox_scores(q, k, cq_ref[0, pl.ds(q0, t), :][:, :1], ck, q0, k0, t, t, scale)
                p = jnp.exp(s - lse_ref[0, pl.ds(q0, t), :][:, :1])
                dp = lax.dot_general(dov, v, _NT, preferred_element_type=F32)
                ds = p * (dp - delta_ref[pl.ds(q0, t), :][:, :1])
                dsb = ds.astype(BF16)
                dv = dv + lax.dot_general(p.astype(BF16), dov, _TN, preferred_element_type=F32)
                dk = dk + lax.dot_general(dsb, q, _TN, preferred_element_type=F32)
                dq_acc[pl.ds(q0, t), :] += jnp.dot(dsb, k, preferred_element_type=F32)
                dc = dc - jnp.sum(ds, axis=0, keepdims=True)
                dcq_ref[0, pl.ds(q0, t), :] += jnp.broadcast_to(jnp.sum(ds, axis=1, keepdims=True), (t, LANES))
                return dk, dv, dc

            init = (jnp.zeros((t, HEAD_DIM), F32), jnp.zeros((t, HEAD_DIM), F32), jnp.zeros((1, t), F32))
            dk, dv, dc = lax.fori_loop(j, nq, q_block, init)
            dk_ref[pl.ds(k0, t), :] = (dk * scale).astype(BF16)
            dv_ref[pl.ds(k0, t), :] = dv.astype(BF16)
            dc_ref[0, :, pl.ds(k0, t)] = dc
            return 0

        lax.fori_loop(0, nq, kv_block, 0)
        dq_ref[...] = (dq_acc[...] * scale).astype(BF16)

    col = lambda off: pl.BlockSpec((S, HEAD_DIM), lambda h: (0, off + h))
    per_head = pl.BlockSpec((1, S, LANES), lambda h: (h, 0, 0))
    row = pl.BlockSpec((1, 1, S), lambda h: (h, 0, 0))
    grad = jax.ShapeDtypeStruct((S, n_fox * HEAD_DIM), BF16)
    return pl.pallas_call(
        body, name="fox_bwd", grid=(n_fox,),
        in_specs=[col(0), col(n_fox), col(2 * n_fox), col(0), col(0), per_head, per_head, row],
        out_specs=[col(0), col(0), col(0), row, per_head],
        out_shape=[grad, grad, grad, jax.ShapeDtypeStruct((n_fox, 1, S), F32), jax.ShapeDtypeStruct((n_fox, S, LANES), F32)],
        scratch_shapes=[pltpu.VMEM((S, HEAD_DIM), F32), pltpu.VMEM((S, LANES), F32)],
        compiler_params=pltpu.CompilerParams(dimension_semantics=("parallel",),
                                             vmem_limit_bytes=_vmem(24 * S * HEAD_DIM * 4 + 16 * t * t * 4)),
    )(proj, proj, proj, o, do, lse_b, cum_b, cum_row)


def _rope_tables(S):
    half = HEAD_DIM // 2
    inv_freq = 1.0 / (ROPE_THETA ** (jnp.arange(half, dtype=F32) * (2.0 / HEAD_DIM)))
    ang = jnp.arange(S).astype(F32)[:, None] * inv_freq[None, :]
    cos, sin = jnp.cos(ang), jnp.sin(ang)
    return jnp.concatenate([cos, cos], axis=-1), jnp.concatenate([-sin, sin], axis=-1)


def _rope(name, src, first_block, n_blocks, cos, sin_signed):
    S = src.shape[0]

    def body(x_ref, cos_ref, sin_ref, o_ref):
        xv = x_ref[...].astype(F32)
        o_ref[...] = (xv * cos_ref[...] + pltpu.roll(xv, HEAD_DIM // 2, 1) * sin_ref[...]).astype(BF16)

    table = pl.BlockSpec((S, HEAD_DIM), lambda n: (0, 0))
    return pl.pallas_call(
        body, name=name, grid=(n_blocks,),
        in_specs=[pl.BlockSpec((S, HEAD_DIM), lambda n: (0, first_block + n)), table, table],
        out_specs=pl.BlockSpec((S, HEAD_DIM), lambda n: (0, n)),
        out_shape=jax.ShapeDtypeStruct((S, n_blocks * HEAD_DIM), BF16),
        compiler_params=pltpu.CompilerParams(dimension_semantics=("parallel",),
                                             vmem_limit_bytes=_vmem(12 * S * HEAD_DIM * 4)),
    )(src, cos, sin_signed)


def _swa_tile(q_ref, kp_ref, kc_ref, n, group, scale):
    B = SWA_BLOCK
    qs = jnp.concatenate([q_ref[:, g * HEAD_DIM:(g + 1) * HEAD_DIM] for g in range(group)], axis=0)
    kcat = jnp.concatenate([kp_ref[...], kc_ref[...]], axis=0)
    s = lax.dot_general(qs, kcat, _NT, preferred_element_type=F32) * scale
    qi = lax.broadcasted_iota(jnp.int32, (group * B, 2 * B), 0) % B
    kj = lax.broadcasted_iota(jnp.int32, (group * B, 2 * B), 1)
    diff = qi + B - kj
    mask = (diff >= 0) & (diff < B) & ((n * B + kj - B) >= 0)
    return qs, kcat, jnp.where(mask, s, NEG)


def _swa_sink_col(sink_ref, kv, group):
    head = lax.broadcasted_iota(jnp.int32, (group * SWA_BLOCK, 1), 0) // SWA_BLOCK
    col = jnp.zeros((group * SWA_BLOCK, 1), F32)
    for g in range(group):
        col = jnp.where(head == g, sink_ref[kv * group + g], col)
    return col


def _swa_specs(n_kv, group, q_first, k_first, v_first):
    B = SWA_BLOCK
    prev = lambda n: jnp.maximum(n - 1, 0)
    return [
        pl.BlockSpec((B, group * HEAD_DIM), lambda kv, n: (n, q_first + kv)),
        pl.BlockSpec((B, HEAD_DIM), lambda kv, n: (prev(n), k_first + kv)),
        pl.BlockSpec((B, HEAD_DIM), lambda kv, n: (n, k_first + kv)),
        pl.BlockSpec((B, HEAD_DIM), lambda kv, n: (prev(n), v_first + kv)),
        pl.BlockSpec((B, HEAD_DIM), lambda kv, n: (n, v_first + kv)),
    ]


def _swa_fwd(rq, proj, v_first, sinks, n_q, n_kv):
    S = rq.shape[0]
    B = SWA_BLOCK
    group = n_q // n_kv
    scale = HEAD_DIM ** -0.5

    def body(q_ref, kp_ref, kc_ref, vp_ref, vc_ref, sink_ref, o_ref, lse_ref):
        kv, n = pl.program_id(0), pl.program_id(1)
        _, _, s = _swa_tile(q_ref, kp_ref, kc_ref, n, group, scale)
        sink = _swa_sink_col(sink_ref, kv, group)
        m = jnp.maximum(jnp.max(s, axis=-1, keepdims=True), sink)
        p = jnp.exp(s - m)
        denom = jnp.sum(p, axis=-1, keepdims=True) + jnp.exp(sink - m)
        vcat = jnp.concatenate([vp_ref[...], vc_ref[...]], axis=0)
        o = jnp.dot((p / denom).astype(BF16), vcat, preferred_element_type=F32)
        lse = m + jnp.log(denom)
        for g in range(group):
            o_ref[:, g * HEAD_DIM:(g + 1) * HEAD_DIM] = o[g * B:(g + 1) * B, :]
            lse_ref[0, :, g * LANES:(g + 1) * LANES] = jnp.broadcast_to(lse[g * B:(g + 1) * B, :], (B, LANES))

    specs = _swa_specs(n_kv, group, 0, n_q, v_first)
    q_blk = pl.BlockSpec((B, group * HEAD_DIM), lambda kv, n: (n, kv))
    return pl.pallas_call(
        body, name="swa_fwd", grid=(n_kv, S // B),
        in_specs=specs + [pl.BlockSpec(memory_space=pltpu.SMEM)],
        out_specs=[q_blk, pl.BlockSpec((1, B, group * LANES), lambda kv, n: (kv, n, 0))],
        out_shape=[jax.ShapeDtypeStruct((S, n_q * HEAD_DIM), F32), jax.ShapeDtypeStruct((n_kv, S, group * LANES), F32)],
        compiler_params=pltpu.CompilerParams(dimension_semantics=("parallel", "arbitrary")),
    )(rq, rq, rq, proj, proj, sinks)


def _swa_bwd(rq, proj, v_first, sinks, o, do, do_first, lse_b, n_q, n_kv):
    S = rq.shape[0]
    B = SWA_BLOCK
    group = n_q // n_kv
    scale = HEAD_DIM ** -0.5

    def body(q_ref, kp_ref, kc_ref, vp_ref, vc_ref, o_ref, do_ref, lse_ref, sink_ref,
             dq_ref, dk_ref, dv_ref, dsink_ref):
        kv, n = pl.program_id(0), pl.program_id(1)

        @pl.when(n == 0)
        def _():
            dk_ref[...] = jnp.zeros_like(dk_ref)
            dv_ref[...] = jnp.zeros_like(dv_ref)
            dsink_ref[...] = jnp.zeros_like(dsink_ref)

        qs, kcat, s = _swa_tile(q_ref, kp_ref, kc_ref, n, group, scale)
        sink = _swa_sink_col(sink_ref, kv, group)
        stack = lambda ref, w: jnp.concatenate([ref[:, g * w:(g + 1) * w] for g in range(group)], axis=0)
        lse = jnp.concatenate([lse_ref[0, :, g * LANES:g * LANES + 1] for g in range(group)], axis=0)
        do32 = stack(do_ref, HEAD_DIM)
        delta = jnp.sum(do32 * stack(o_ref, HEAD_DIM), axis=-1, keepdims=True)
        dov = do32.astype(BF16)
        p = jnp.exp(s - lse)
        vcat = jnp.concatenate([vp_ref[...], vc_ref[...]], axis=0)
        dp = lax.dot_general(dov, vcat, _NT, preferred_element_type=F32)
        ds = p * (dp - delta)
        dsb = ds.astype(BF16)
        dq = jnp.dot(dsb, kcat, preferred_element_type=F32) * scale
        for g in range(group):
            dq_ref[:, g * HEAD_DIM:(g + 1) * HEAD_DIM] = dq[g * B:(g + 1) * B, :].astype(BF16)
        dkcat = lax.dot_general(dsb, qs, _TN, preferred_element_type=F32) * scale
        dvcat = lax.dot_general(p.astype(BF16), dov, _TN, preferred_element_type=F32)
        prev0 = pl.multiple_of(jnp.maximum(n - 1, 0) * B, B)
        cur0 = pl.multiple_of(n * B, B)
        dk_ref[0, pl.ds(prev0, B), :] += dkcat[:B, :]
        dk_ref[0, pl.ds(cur0, B), :] += dkcat[B:, :]
        dv_ref[0, pl.ds(prev0, B), :] += dvcat[:B, :]
        dv_ref[0, pl.ds(cur0, B), :] += dvcat[B:, :]
        dsk = -jnp.exp(sink - lse) * delta
        lane = lax.broadcasted_iota(jnp.int32, (1, LANES), 1)
        row = jnp.zeros((1, LANES), F32)
        for g in range(group):
            row = row + jnp.where(lane == g, jnp.sum(dsk[g * B:(g + 1) * B, :]), 0.0)
        dsink_ref[0, 0:1, :] += row

    specs = _swa_specs(n_kv, group, 0, n_q, v_first)
    q_blk = pl.BlockSpec((B, group * HEAD_DIM), lambda kv, n: (n, kv))
    acc = pl.BlockSpec((1, S, HEAD_DIM), lambda kv, n: (kv, 0, 0))
    return pl.pallas_call(
        body, name="swa_bwd", grid=(n_kv, S // B),
        in_specs=specs + [q_blk, pl.BlockSpec((B, group * HEAD_DIM), lambda kv, n: (n, do_first + kv)),
                          pl.BlockSpec((1, B, group * LANES), lambda kv, n: (kv, n, 0)),
                          pl.BlockSpec(memory_space=pltpu.SMEM)],
        out_specs=[q_blk, acc, acc, pl.BlockSpec((1, 8, LANES), lambda kv, n: (kv, 0, 0))],
        out_shape=[jax.ShapeDtypeStruct((S, n_q * HEAD_DIM), BF16), jax.ShapeDtypeStruct((n_kv, S, HEAD_DIM), F32),
                   jax.ShapeDtypeStruct((n_kv, S, HEAD_DIM), F32), jax.ShapeDtypeStruct((n_kv, 8, LANES), F32)],
        compiler_params=pltpu.CompilerParams(dimension_semantics=("parallel", "arbitrary")),
    )(rq, rq, rq, proj, proj, o, do, lse_b, sinks)


def _adamw(w, g, m, v):
    m = ADAM_B1 * m + (1.0 - ADAM_B1) * g
    v = ADAM_B2 * v + (1.0 - ADAM_B2) * (g * g)
    m_hat = m / (1.0 - ADAM_B1 ** ADAM_STEP)
    v_hat = v / (1.0 - ADAM_B2 ** ADAM_STEP)
    delta = -ADAM_LR * (m_hat / (jnp.sqrt(v_hat) + ADAM_EPS) + ADAM_WD * w)
    return delta, m, v


def _mod_fwd(cond_in, w_mod, b_shard):
    R, D = cond_in.shape
    cols = w_mod.shape[1]
    tn = _fit(512, cols)

    def body(c_ref, w_ref, b_ref, o_ref):
        cv = c_ref[...]
        cond = (cv / (1.0 + jnp.exp(-cv))).astype(BF16)
        o_ref[...] = jnp.dot(cond, w_ref[...].astype(BF16), preferred_element_type=F32) + b_ref[...]

    return pl.pallas_call(
        body, name="mod_fwd", grid=(cols // tn,),
        in_specs=[pl.BlockSpec((R, D), lambda j: (0, 0)), pl.BlockSpec((D, tn), lambda j: (0, j)),
                  pl.BlockSpec((1, tn), lambda j: (0, j))],
        out_specs=pl.BlockSpec((R, tn), lambda j: (0, j)),
        out_shape=jax.ShapeDtypeStruct((R, cols), F32),
        compiler_params=pltpu.CompilerParams(dimension_semantics=("parallel",), vmem_limit_bytes=_vmem(3 * D * tn * 4)),
    )(cond_in, w_mod, b_shard)


def _mod_update(c_t, dmod, w, m, v):
    D, nb = c_t.shape
    cols = w.shape[1]
    tn = _fit(256, cols)

    def body(c_ref, d_ref, w_ref, m_ref, v_ref, g_ref, dl_ref, nm_ref, nv_ref):
        cv = c_ref[...]
        cond = cv / (1.0 + jnp.exp(-cv))
        g = jnp.zeros((D, tn), F32)
        for b in range(nb):
            g = g + cond[:, b:b + 1] * d_ref[b:b + 1, :]
        g_ref[...] = g
        dl_ref[...], nm_ref[...], nv_ref[...] = _adamw(w_ref[...], g, m_ref[...], v_ref[...])

    blk = pl.BlockSpec((D, tn), lambda j: (0, j))
    out = jax.ShapeDtypeStruct((D, cols), F32)
    return pl.pallas_call(
        body, name="mod_update", grid=(cols // tn,),
        in_specs=[pl.BlockSpec((D, nb), lambda j: (0, 0)), pl.BlockSpec((nb, tn), lambda j: (0, j)), blk, blk, blk],
        out_specs=[blk] * 4, out_shape=[out] * 4,
        compiler_params=pltpu.CompilerParams(dimension_semantics=("parallel",), vmem_limit_bytes=_vmem(18 * D * tn * 4)),
    )(c_t, dmod, w, m, v)


def _adam_update(name, w, g, m, v):
    R, C = w.shape
    tr = _fit(256, R)

    def body(w_ref, g_ref, m_ref, v_ref, dl_ref, nm_ref, nv_ref):
        dl_ref[...], nm_ref[...], nv_ref[...] = _adamw(w_ref[...], g_ref[...], m_ref[...], v_ref[...])

    blk = pl.BlockSpec((tr, C), lambda r: (r, 0))
    out = jax.ShapeDtypeStruct((R, C), F32)
    padded = -(-C // LANES) * LANES
    return pl.pallas_call(
        body, name=name, grid=(R // tr,), in_specs=[blk] * 4, out_specs=[blk] * 3, out_shape=[out] * 3,
        compiler_params=pltpu.CompilerParams(dimension_semantics=("parallel",), vmem_limit_bytes=_vmem(16 * tr * padded * 4)),
    )(w, g, m, v)


def _sum_blocks(name, stacked, n):
    R, C = stacked.shape[0] // n, stacked.shape[1]

    def body(s_ref, o_ref):
        total = s_ref[0:R, :]
        for d in range(1, n):
            total = total + s_ref[d * R:(d + 1) * R, :]
        o_ref[...] = total

    return pl.pallas_call(body, name=name, out_shape=jax.ShapeDtypeStruct((R, C), F32))(stacked)


def _place():
    return lax.axis_index("x"), lax.axis_index("y"), lax.axis_index("c")


def _allgather8(name, block):
    m_per, n = block.shape

    def body(x_ref, out_ref, token_ref, send_sems, recv_sems, local_sem):
        token_ref[...] = jnp.zeros_like(token_ref)
        x, y, c = _place()
        me, sibling = (x, y, c), (x, y, 1 - c)
        chips = [(1 - x, y), (x, 1 - y), (1 - x, 1 - y)]

        def rows(px, py, pc):
            return out_ref.at[pl.ds((4 * px + 2 * py + pc) * m_per, m_per), :]

        def copy(k, blk, to, src=None):
            return pltpu.make_async_remote_copy(
                src_ref=rows(*blk) if src is None else src, dst_ref=rows(*blk),
                send_sem=send_sems.at[k], recv_sem=recv_sems.at[k], device_id=to, device_id_type=MESH)

        mine = pltpu.make_async_copy(x_ref, rows(*me), local_sem)
        mine.start()
        first = [copy(0, me, sibling, src=x_ref)]
        first += [copy(1 + j, me, (*chip, c), src=x_ref) for j, chip in enumerate(chips)]
        for cp in first:
            cp.start()
        passed = [copy(4 + j, (*chip, c), sibling) for j, chip in enumerate(chips)]
        for j, chip in enumerate(chips):
            copy(1 + j, (*chip, c), me).wait_recv()
            passed[j].start()
        copy(0, sibling, me).wait_recv()
        for j, chip in enumerate(chips):
            copy(4 + j, (*chip, 1 - c), me).wait_recv()
        for cp in first + passed:
            cp.wait_send()
        mine.wait()

    vmem = pl.BlockSpec(memory_space=pltpu.VMEM)
    return pl.pallas_call(
        body, name=name,
        out_shape=[jax.ShapeDtypeStruct((N_DEV * m_per, n), block.dtype), jax.ShapeDtypeStruct((8, LANES), F32)],
        in_specs=[vmem], out_specs=[vmem, vmem],
        scratch_shapes=[pltpu.SemaphoreType.DMA((7,)), pltpu.SemaphoreType.DMA((7,)), pltpu.SemaphoreType.DMA],
    )(block)


_ANY = pl.BlockSpec(memory_space=pl.ANY)


def _half(ref, c, rows):
    return ref.at[pl.ds(c * (rows // 2), rows // 2), :]


_HBM = pl.BlockSpec(memory_space=pltpu.HBM)
_SEM = pl.BlockSpec(memory_space=pltpu.SEMAPHORE)
_EFFECT = pltpu.SideEffectType.DATAFLOW_SIDE_EFFECTING


def _ici_start(name, srcs, land_shapes, plan):
    ns, nl = len(srcs), len(land_shapes)
    n_copies = 3 * ns

    def body(*refs):
        src_refs, land_refs = refs[:ns], refs[ns:ns + nl]
        send_sems, recv_sems = refs[ns + nl], refs[ns + nl + 1]
        token = refs[-1]
        for n, (src, dst, peer, _) in enumerate(plan(src_refs, land_refs)):
            pltpu.make_async_remote_copy(src_ref=src, dst_ref=dst, send_sem=send_sems.at[n], recv_sem=recv_sems.at[n],
                                         device_id=peer, device_id_type=MESH).start()
        token[...] = jnp.zeros_like(token)

    lands = [lax.empty(s.shape, s.dtype) for s in land_shapes]
    out = pl.pallas_call(
        body, name=name,
        out_shape=(pltpu.SemaphoreType.DMA((n_copies,)), pltpu.SemaphoreType.DMA((n_copies,)),
                   *[pltpu.HBM(a.shape, a.dtype) for a in list(srcs) + lands], jax.ShapeDtypeStruct((8, LANES), F32)),
        in_specs=[_HBM] * (ns + nl),
        out_specs=(_SEM, _SEM, *[_HBM] * (ns + nl), pl.BlockSpec(memory_space=pltpu.VMEM)),
        input_output_aliases={n: 2 + n for n in range(ns + nl)},
        compiler_params=pltpu.CompilerParams(has_side_effects=_EFFECT),
    )(*[pltpu.with_memory_space_constraint(a, pltpu.HBM) for a in list(srcs) + lands])
    return out[0], out[1], list(out[2:2 + ns]), list(out[2 + ns:2 + ns + nl]), out[-1]


def _ici_wait(name, send_sems, recv_sems, srcs, lands, plan, after):
    ns, nl = len(srcs), len(lands)

    def body(*refs):
        src_refs, land_refs = refs[:ns], refs[ns:ns + nl]
        send_sems, recv_sems = refs[ns + nl], refs[ns + nl + 1]
        for n, (src, _, peer, mine) in enumerate(plan(src_refs, land_refs)):
            cp = pltpu.make_async_remote_copy(src_ref=src, dst_ref=mine, send_sem=send_sems.at[n],
                                              recv_sem=recv_sems.at[n], device_id=peer, device_id_type=MESH)
            cp.wait_send()
            cp.wait_recv()

    out = pl.pallas_call(
        body, name=name, out_shape=[pltpu.HBM(a.shape, a.dtype) for a in list(srcs) + list(lands)],
        in_specs=[_HBM] * (ns + nl) + [_SEM, _SEM, _ANY], out_specs=[_HBM] * (ns + nl),
        input_output_aliases={n: n for n in range(ns + nl)},
        compiler_params=pltpu.CompilerParams(has_side_effects=_EFFECT),
    )(*srcs, *lands, send_sems, recv_sems, after)
    return list(out[:ns]), list(out[ns:])


def _gather_plan(src_refs, land_refs):
    x, y, c = _place()
    copies = []
    for w, land in zip(src_refs, land_refs):
        R = w.shape[0]
        for cx, cy in [(1 - x, y), (x, 1 - y), (1 - x, 1 - y)]:
            copies.append((_half(w, c, R), _half(land.at[2 * x + y], c, R), (cx, cy, c),
                           _half(land.at[2 * cx + cy], c, R)))
    return copies


def _pass_to_sibling(name, lands):
    nw = len(lands)

    def body(*refs):
        ins, outs = refs[:nw], refs[nw:2 * nw]
        send_sems, recv_sems = refs[2 * nw:]
        x, y, c = _place()
        chips = [(1 - x, y), (x, 1 - y), (1 - x, 1 - y)]
        copies = []
        for k in range(nw):
            R = ins[k].shape[1]
            for j, (cx, cy) in enumerate(chips):
                cp = pltpu.make_async_remote_copy(
                    src_ref=_half(ins[k].at[2 * cx + cy], c, R), dst_ref=_half(outs[k].at[2 * cx + cy], c, R),
                    send_sem=send_sems.at[3 * k + j], recv_sem=recv_sems.at[3 * k + j],
                    device_id=(x, y, 1 - c), device_id_type=MESH)
                cp.start()
                copies.append(cp)
        for k in range(nw):
            R = ins[k].shape[1]
            for j, (cx, cy) in enumerate(chips):
                pltpu.make_async_remote_copy(
                    src_ref=_half(ins[k].at[2 * cx + cy], c, R), dst_ref=_half(outs[k].at[2 * cx + cy], 1 - c, R),
                    send_sem=send_sems.at[3 * k + j], recv_sem=recv_sems.at[3 * k + j],
                    device_id=(x, y, 1 - c), device_id_type=MESH).wait_recv()
        for cp in copies:
            cp.wait_send()

    return pl.pallas_call(
        body, name=name, out_shape=[jax.ShapeDtypeStruct(a.shape, a.dtype) for a in lands],
        in_specs=[_ANY] * nw, out_specs=[_ANY] * nw, input_output_aliases={k: k for k in range(nw)},
        scratch_shapes=[pltpu.SemaphoreType.DMA((3 * nw,)), pltpu.SemaphoreType.DMA((3 * nw,))],
    )(*lands)


def _tie(vec, token):
    return vec + token[0:1, 0:1]


def _pair_exchange(name, grads):
    nw = len(grads)

    def body(*refs):
        gs, outs = refs[:nw], refs[nw:2 * nw]
        send_sems, recv_sems = refs[2 * nw:]
        x, y, c = _place()
        copies = []
        for k in range(nw):
            half = gs[k].shape[1] // 2
            cp = pltpu.make_async_remote_copy(
                src_ref=gs[k].at[:, pl.ds((1 - c) * half, half), :], dst_ref=outs[k],
                send_sem=send_sems.at[k], recv_sem=recv_sems.at[k], device_id=(x, y, 1 - c), device_id_type=MESH)
            cp.start()
            copies.append(cp)
        for cp in copies:
            cp.wait()

    return pl.pallas_call(
        body, name=name,
        out_shape=[jax.ShapeDtypeStruct((N_CHIPS, g.shape[1] // 2, g.shape[2]), g.dtype) for g in grads],
        in_specs=[_ANY] * nw, out_specs=[_ANY] * nw,
        scratch_shapes=[pltpu.SemaphoreType.DMA((nw,)), pltpu.SemaphoreType.DMA((nw,))],
    )(*grads)


def _pair_add(name, core, grad, recv):
    n, R, C = grad.shape
    half = R // 2
    tr = _fit(256, half)
    nblk = half // tr

    def body(core_ref, g_ref, r_ref, o_ref):
        o_ref[...] = (g_ref[...].astype(F32) + r_ref[...].astype(F32)).astype(BF16)

    grid_spec = pltpu.PrefetchScalarGridSpec(
        num_scalar_prefetch=1, grid=(n, nblk),
        in_specs=[pl.BlockSpec((1, tr, C), lambda s, r, core_ref: (s, core_ref[0] * nblk + r, 0)),
                  pl.BlockSpec((1, tr, C), lambda s, r, core_ref: (s, r, 0))],
        out_specs=pl.BlockSpec((1, tr, C), lambda s, r, core_ref: (s, r, 0)))
    return pl.pallas_call(
        body, name=name, grid_spec=grid_spec, out_shape=jax.ShapeDtypeStruct((n, half, C), BF16),
        compiler_params=pltpu.CompilerParams(dimension_semantics=("parallel", "parallel")),
    )(core, grad, recv)


def _scatter_plan(src_refs, land_refs):
    x, y, c = _place()
    copies = []
    for p, land in zip(src_refs, land_refs):
        for j, (cx, cy) in enumerate([(1 - x, y), (x, 1 - y), (1 - x, 1 - y)]):
            copies.append((p.at[2 * cx + cy], land.at[j], (cx, cy, c), land.at[j]))
    return copies


def _chip_add(name, chip, sums, recv):
    _, H, C = sums.shape
    tr = _fit(256, H)

    def body(chip_ref, p_ref, r_ref, o_ref):
        total = p_ref[0].astype(F32)
        for j in range(3):
            total = total + r_ref[j].astype(F32)
        o_ref[...] = total

    grid_spec = pltpu.PrefetchScalarGridSpec(
        num_scalar_prefetch=1, grid=(H // tr,),
        in_specs=[pl.BlockSpec((1, tr, C), lambda r, chip_ref: (chip_ref[0], r, 0)),
                  pl.BlockSpec((3, tr, C), lambda r, chip_ref: (0, r, 0))],
        out_specs=pl.BlockSpec((tr, C), lambda r, chip_ref: (r, 0)))
    return pl.pallas_call(
        body, name=name, grid_spec=grid_spec, out_shape=jax.ShapeDtypeStruct((H, C), F32),
        compiler_params=pltpu.CompilerParams(dimension_semantics=("parallel",)),
    )(chip, sums, recv)


def _pair_share(name, halves):
    nw = len(halves)

    def body(*refs):
        hs, outs = refs[:nw], refs[nw:2 * nw]
        send_sems, recv_sems = refs[2 * nw:]
        x, y, c = _place()
        copies = []
        for k in range(nw):
            cp = pltpu.make_async_remote_copy(
                src_ref=hs[k], dst_ref=outs[k], send_sem=send_sems.at[k], recv_sem=recv_sems.at[k],
                device_id=(x, y, 1 - c), device_id_type=MESH)
            cp.start()
            copies.append(cp)
        for cp in copies:
            cp.wait()

    return pl.pallas_call(
        body, name=name,
        out_shape=[jax.ShapeDtypeStruct(h.shape, h.dtype) for h in halves],
        in_specs=[_ANY] * nw, out_specs=[_ANY] * nw,
        scratch_shapes=[pltpu.SemaphoreType.DMA((nw,)), pltpu.SemaphoreType.DMA((nw,))],
    )(*halves)


def _adam_halves(name, core, w, g_own, g_other, m, v):
    R, C = w.shape
    H = R // 2
    tr = _fit(256, H)
    nblk = H // tr

    def body(core_ref, w_ref, go_ref, gr_ref, m_ref, v_ref, g_ref, dl_ref, nm_ref, nv_ref):
        own = (pl.program_id(0) // nblk) == core_ref[0]
        g = jnp.where(own, go_ref[...], gr_ref[...])
        g_ref[...] = g
        dl_ref[...], nm_ref[...], nv_ref[...] = _adamw(w_ref[...], g, m_ref[...], v_ref[...])

    blk = pl.BlockSpec((tr, C), lambda r, core_ref: (r, 0))
    half = pl.BlockSpec((tr, C), lambda r, core_ref: (r % nblk, 0))
    out = jax.ShapeDtypeStruct((R, C), F32)
    padded = -(-C // LANES) * LANES
    grid_spec = pltpu.PrefetchScalarGridSpec(
        num_scalar_prefetch=1, grid=(R // tr,), in_specs=[blk, half, half, blk, blk], out_specs=[blk] * 4)
    return pl.pallas_call(
        body, name=name, grid_spec=grid_spec, out_shape=[out] * 4,
        compiler_params=pltpu.CompilerParams(dimension_semantics=("parallel",), vmem_limit_bytes=_vmem(20 * tr * padded * 4)),
    )(core, w, g_own, g_other, m, v)


def kernel(x, c, w_mod, b_mod, g_pre_mix, g_post_mix, w_in, b_forget, swa_sinks, w_out, g_pre_mlp, g_post_mlp, w_up, w_down, loss_target, m_w_mod, m_b_mod, m_g_pre_mix, m_g_post_mix, m_w_in, m_b_forget, m_swa_sinks, m_w_out, m_g_pre_mlp, m_g_post_mlp, m_w_up, m_w_down, v_w_mod, v_b_mod, v_g_pre_mix, v_g_post_mix, v_w_in, v_b_forget, v_swa_sinks, v_w_out, v_g_pre_mlp, v_g_post_mlp, v_w_up, v_w_down):
    S, D = x.shape[1], x.shape[2]
    n_heads = D // HEAD_DIM
    n_fox = n_heads // 2
    n_swa = n_heads - n_fox
    n_kv = max(1, n_swa // 4)
    fox_w, swa_w, kv_w = n_fox * HEAD_DIM, n_swa * HEAD_DIM, n_kv * HEAD_DIM
    main_w = 3 * fox_w + swa_w + 2 * kv_w
    in_w = main_w + n_fox
    mod_cols = w_mod.shape[2]

    ax, ay, ac = _place()
    chip = 2 * ax + ay
    dev = 2 * chip + ac
    chip_arr = jnp.reshape(chip, (1,)).astype(jnp.int32)
    core_arr = jnp.reshape(ac, (1,)).astype(jnp.int32)

    x2, tgt = x[0], loss_target[0]

    c_all, _ = _allgather8("gather_c", c.reshape(8, D // 8))
    c_all = c_all.reshape(N_DEV, D)
    b_shard = lax.dynamic_slice_in_dim(b_mod, chip * mod_cols, mod_cols, axis=1)
    mod_shard = _mod_fwd(jnp.pad(c_all, ((0, 16 - N_DEV), (0, 0))), w_mod[0], b_shard)[:N_DEV]
    mod_all, token = _allgather8("gather_mod", mod_shard)
    mod_all = mod_all.reshape(N_CHIPS, 2, N_DEV, mod_cols)[:, 0]
    mod = lax.dynamic_index_in_dim(mod_all, dev, axis=1, keepdims=False).reshape(N_MOD, 1, D)
    sh_a, sc_a, gt_a, sh_m, sc_m, gt_m = [mod[n] for n in range(N_MOD)]

    names = ["w_in", "w_out", "w_up", "w_down"]
    flights = {}
    for n, w in zip(names, [w_in, w_out, w_up, w_down]):
        shard = _tie(w[0], token).astype(BF16)
        flights[n] = _ici_start("gather_start_" + n, [shard], [jax.ShapeDtypeStruct((N_CHIPS,) + shard.shape, BF16)],
                                _gather_plan)
        token = flights[n][4]
    sc_a = _tie(sc_a, token)

    def gathered(n, after):
        send, recv, srcs, lands, _ = flights[n]
        srcs, lands = _ici_wait("gather_wait_" + n, send, recv, srcs, lands, _gather_plan, after)
        lands = _pass_to_sibling("gather_pass_" + n, lands)
        return lax.dynamic_update_index_in_dim(lands[0], srcs[0], chip, 0)

    d_ff = N_CHIPS * w_up.shape[2]

    h = _pre_norm(x2, g_pre_mix, sc_a, sh_a)
    w_in_f = jnp.transpose(gathered("w_in", h), (1, 0, 2)).reshape(D, in_w)
    w_main = jnp.concatenate([w_in_f[:, :3 * fox_w], w_in_f[:, 3 * fox_w + n_fox:]], axis=1)
    w_fg = jnp.pad(w_in_f[:, 3 * fox_w:3 * fox_w + n_fox], ((0, 0), (0, LANES - n_fox)))
    proj = _mm_plain("in_proj", h, w_main, "nn", BF16, tn=_fit(768, main_w))
    fg = _mm_plain("in_proj_gate", h, w_fg, "nn", F32)
    b_pad = jnp.pad(b_forget, ((0, 0), (0, LANES - n_fox)))
    cum, cum_t, cum_b = _fox_gate_fwd(fg, b_pad, n_fox)
    cum_row = cum_t[:n_fox].reshape(n_fox, 1, S)
    fox_o, fox_lse = _fox_fwd(proj, cum_b, cum_row, n_fox)

    cos, sin_signed = _rope_tables(S)
    rq = _rope("rope_fwd", proj, 3 * n_fox, n_swa + n_kv, cos, sin_signed)
    v_first = 3 * n_fox + n_swa + n_kv
    sinks = swa_sinks[0]
    swa_o, swa_lse = _swa_fwd(rq, proj, v_first, sinks, n_swa, n_kv)

    mixcat = jnp.concatenate([fox_o, swa_o], axis=1).astype(BF16)
    w_out_f = gathered("w_out", mixcat).reshape(D, D)
    mix = _mm_plain("out_proj", mixcat, w_out_f, "nn", F32)
    x1, h2 = _post_mix(x2, mix, g_post_mix, gt_a, g_pre_mlp, sc_m, sh_m)
    w_up_f = jnp.transpose(gathered("w_up", h2), (1, 0, 2)).reshape(D, d_ff)

    tm_u, tn_u = _fit(MM_TM, S), _fit(MM_TN, d_ff)

    def up_epilogue(acc, ex, outs):
        outs[0][...] = acc.astype(BF16)
        r = jnp.maximum(acc, 0.0)
        outs[1][...] = (r * r).astype(BF16)

    ublk = ((S, d_ff), BF16, (tm_u, tn_u), lambda i, j: (i, j))
    u, a = _matmul("mlp_up", h2, w_up_f, "nn", [ublk, ublk], up_epilogue)
    w_down_f = gathered("w_down", a).reshape(d_ff, D)
    y = _mm_plain("mlp_down", a, w_down_f, "nn", F32)

    dy, dout, loss_part, acc_mlp_post = _loss_and_post_mlp_bwd(x1, y, tgt, g_post_mlp, gt_m)
    loss = lax.psum(loss_part[0, 0], ("x", "y", "c"))

    def du_epilogue(acc, ex, outs):
        outs[0][...] = (acc * (2.0 * jnp.maximum(ex[0][...].astype(F32), 0.0))).astype(BF16)

    du = _matmul("mlp_down_bwd", dy, w_down_f, "nt", [ublk], du_epilogue,
                 extras=[(u, (tm_u, tn_u), lambda i, j: (i, j))])[0]
    g_down = _mm_plain("grad_w_down", a, dy, "tn", BF16)
    tn_s = _fit(MM_TN, w_up.shape[2])
    per = w_up.shape[2] // tn_s

    def shard_epilogue(acc, ex, outs):
        outs[0][0] = acc.astype(BF16)

    g_up = _matmul("grad_w_up", h2, du, "tn",
                   [((N_CHIPS, D, w_up.shape[2]), BF16, (1, _fit(MM_TM, D), tn_s), lambda i, j: (j // per, i, j % per))],
                   shard_epilogue, tn=tn_s)[0]

    def reduce_start(tag, fulls):
        from_sibling = _pair_exchange("grad_pair_exchange_" + tag, fulls)
        sums = [_pair_add("pair_add_%s_%d" % (tag, k), core_arr, g, r) for k, (g, r) in enumerate(zip(fulls, from_sibling))]
        return _ici_start("grad_scatter_start_" + tag, sums,
                          [jax.ShapeDtypeStruct((3,) + p.shape[1:], BF16) for p in sums], _scatter_plan)

    def reduce_finish(tag, flight, after):
        send, recv, srcs, lands, _ = flight
        sums, received = _ici_wait("grad_scatter_wait_" + tag, send, recv, srcs, lands, _scatter_plan, after)
        halves = [_chip_add("chip_add_%s_%d" % (tag, k), chip_arr, p, r) for k, (p, r) in enumerate(zip(sums, received))]
        return halves, _pair_share("grad_pair_share_" + tag, halves)

    flight_mlp = reduce_start("mlp", [g_up, g_down.reshape(N_CHIPS, d_ff // N_CHIPS, D)])
    dh2 = _mm_plain("mlp_up_bwd", du, w_up_f, "nt", F32)
    dx1, dmix, acc_mid = _pre_mlp_and_post_mix_bwd(dh2, x1, dout, mix, _tie(g_pre_mlp, flight_mlp[4]), sc_m,
                                                   g_post_mix, gt_a)

    dmixcat = _mm_plain("out_proj_bwd", dmix, w_out_f, "nt", F32)
    g_out = _mm_plain("grad_w_out", mixcat, dmix, "tn", BF16)

    fdq, fdk, fdv, dcum_row, dcum_q = _fox_bwd(proj, fox_o, dmixcat, fox_lse, cum_b, cum_row, n_fox)
    dcum_k = jnp.pad(dcum_row.reshape(n_fox, S), ((0, LANES - n_fox), (0, 0)))
    dfg, db_forget = _fox_gate_bwd(dcum_k, dcum_q, fg, b_pad)

    group_w = (n_swa // n_kv) * HEAD_DIM
    sdq, sdk, sdv, dsink = _swa_bwd(rq, proj, v_first, sinks, swa_o, dmixcat, fox_w // group_w, swa_lse, n_swa, n_kv)
    drq = jnp.concatenate([sdq, jnp.transpose(sdk, (1, 0, 2)).reshape(S, kv_w).astype(BF16)], axis=1)
    d_sq_sk = _rope("rope_bwd", drq, 0, n_swa + n_kv, cos, -sin_signed)
    dsv = jnp.transpose(sdv, (1, 0, 2)).reshape(S, kv_w).astype(BF16)
    dproj = jnp.concatenate([fdq, fdk, fdv, d_sq_sk, dsv], axis=1)

    g_main = _mm_plain("grad_w_in", h, dproj, "tn", BF16, tn=_fit(768, main_w))
    g_fg = _mm_plain("grad_w_in_gate", h, dfg, "tn", BF16)
    dh_gate = _mm_plain("in_proj_gate_bwd", dfg, w_fg, "nt", F32)

    def add_epilogue(acc, ex, outs):
        outs[0][...] = acc + ex[0][...]

    tm_h, tn_h = _fit(MM_TM, S), _fit(MM_TN, D)
    dh = _matmul("in_proj_bwd", dproj, w_main, "nt", [((S, D), F32, (tm_h, tn_h), lambda i, j: (i, j))], add_epilogue,
                 extras=[(dh_gate, (tm_h, tn_h), lambda i, j: (i, j))], tk=_fit(2304, main_w))[0]
    grad_x, acc_pre = _pre_mix_bwd(dh, x2, dx1, g_pre_mix, sc_a)

    zero_row = jnp.zeros((1, D), F32)
    tail = jnp.concatenate([db_forget[0:1, :n_fox], dsink[:, 0, :n_swa // n_kv].reshape(1, n_swa),
                            jnp.zeros((1, D - n_fox - n_swa), F32)], axis=1)
    partial = jnp.concatenate([
        acc_pre[0:1], acc_pre[1:2], acc_mid[3:4], acc_mid[0:1], acc_mid[1:2], acc_mlp_post[0:1],
        acc_pre[2:3], acc_mid[4:5], acc_mid[2:3], acc_mlp_post[1:2], tail] + [zero_row] * 5, axis=0)
    gathered_small, token = _allgather8("gather_small_grads", partial)

    g_fg_tied = _tie(g_fg[:, :n_fox].astype(F32), token).astype(BF16)
    g_in_f = jnp.concatenate([g_main[:, :3 * fox_w], g_fg_tied, g_main[:, 3 * fox_w:]], axis=1)
    flight_mix = reduce_start("mix", [jnp.transpose(g_in_f.reshape(D, N_CHIPS, in_w // N_CHIPS), (1, 0, 2)),
                                      g_out.reshape(N_CHIPS, D // N_CHIPS, D)])
    halves_mlp, others_mlp = reduce_finish("mlp", flight_mlp, flight_mix[4])
    small = _sum_blocks("sum_small_grads", gathered_small, N_DEV)
    g_b_mod = small[0:N_MOD].reshape(1, N_MOD * D)
    g_small = {"g_pre_mix": small[6:7], "g_post_mix": small[7:8], "g_pre_mlp": small[8:9], "g_post_mlp": small[9:10],
               "b_forget": small[10:11, :n_fox], "swa_sinks": small[10:11, n_fox:n_fox + n_swa]}

    dmod_all = gathered_small.reshape(N_DEV, 16, D)[:, :N_MOD].reshape(N_DEV, N_MOD * D)
    dmod_shard = lax.dynamic_slice_in_dim(dmod_all, chip * mod_cols, mod_cols, axis=1)
    g_w_mod, d_w_mod, nm_w_mod, nv_w_mod = _mod_update(c_all.T, dmod_shard, w_mod[0], m_w_mod[0], v_w_mod[0])

    grads = {"w_mod": g_w_mod[None], "b_mod": g_b_mod}
    deltas = {"w_mod": d_w_mod[None]}
    new_m = {"w_mod": nm_w_mod[None]}
    new_v = {"w_mod": nv_w_mod[None]}
    weights = {"w_in": (w_in, m_w_in, v_w_in), "w_out": (w_out, m_w_out, v_w_out), "w_up": (w_up, m_w_up, v_w_up),
               "w_down": (w_down, m_w_down, v_w_down)}

    def big_update(n, own, other):
        w, m, v = weights[n]
        g, d_, m_, v_ = _adam_halves("adam_" + n, core_arr, w[0], own, other, m[0], v[0])
        grads[n], deltas[n], new_m[n], new_v[n] = g[None], d_[None], m_[None], v_[None]

    big_update("w_up", halves_mlp[0], others_mlp[0])
    big_update("w_down", halves_mlp[1], others_mlp[1])
    ran = deltas["w_down"][0, :8, :LANES] + deltas["w_up"][0, :8, :LANES] + d_w_mod[:8, :LANES]
    halves_mix, others_mix = reduce_finish("mix", flight_mix, ran)
    big_update("w_in", halves_mix[0], others_mix[0])
    big_update("w_out", halves_mix[1], others_mix[1])
    small_w = {"b_mod": (b_mod, m_b_mod, v_b_mod), "g_pre_mix": (g_pre_mix, m_g_pre_mix, v_g_pre_mix),
               "g_post_mix": (g_post_mix, m_g_post_mix, v_g_post_mix), "b_forget": (b_forget, m_b_forget, v_b_forget),
               "swa_sinks": (swa_sinks, m_swa_sinks, v_swa_sinks), "g_pre_mlp": (g_pre_mlp, m_g_pre_mlp, v_g_pre_mlp),
               "g_post_mlp": (g_post_mlp, m_g_post_mlp, v_g_post_mlp)}
    g_small["b_mod"] = g_b_mod
    for n, (w, m, v) in small_w.items():
        g = g_small[n]
        grads[n] = g
        deltas[n], new_m[n], new_v[n] = _adam_update("adam_" + n, w, g, m, v)

    order = ["w_mod", "b_mod", "g_pre_mix", "g_post_mix", "w_in", "b_forget", "swa_sinks", "w_out", "g_pre_mlp",
             "g_post_mlp", "w_up", "w_down"]
    return (loss, grad_x[None], *[grads[n] for n in order], *[deltas[n] for n in order],
            *[new_m[n] for n in order], *[new_v[n] for n in order])
```

```python
import jax
import jax.numpy as jnp
from jax import lax
from jax.experimental import pallas as pl
from jax.experimental.pallas import tpu as pltpu

F32 = jnp.float32
BF16 = jnp.bfloat16
MESH = pl.DeviceIdType.MESH

HEAD_DIM = 128
SWA_BLOCK = 128
ROPE_THETA = 10000.0
NORM_EPS = 1e-6
NEG = -1e30
N_MOD = 6
ADAM_LR = 0.001
ADAM_B1 = 0.9
ADAM_B2 = 0.999
ADAM_EPS = 1e-08
ADAM_WD = 0.01
ADAM_STEP = 10
N_CHIPS = 4
N_DEV = 8
LANES = 128
VMEM_CAP = 60 * 1024 * 1024

_NN = (((1,), (0,)), ((), ()))
_NT = (((1,), (1,)), ((), ()))
_TN = (((0,), (0,)), ((), ()))


def _vmem(nbytes):
    return int(min(VMEM_CAP, nbytes * 5 // 4 + (4 << 20)))


def _nbytes(shape, dtype):
    n = 1
    for s in shape:
        n *= s
    return n * jnp.dtype(dtype).itemsize


def _hbm(x):
    return pltpu.with_memory_space_constraint(x, pltpu.HBM)


def _fit(t, n):
    t = min(t, n)
    assert n % t == 0, (t, n)
    return t


MM_TM, MM_TN, MM_TK = 512, 1024, 2048


def _matmul(name, a, b, mode, out_defs, epilogue, extras=(), tm=MM_TM, tn=MM_TN, tk=MM_TK):
    if mode == "nn":
        (M, K), (K2, N) = a.shape, b.shape
    elif mode == "nt":
        (M, K), (N, K2) = a.shape, b.shape
    else:
        (K, M), (K2, N) = a.shape, b.shape
    assert K == K2, (a.shape, b.shape, mode)
    tm, tn, tk = _fit(tm, M), _fit(tn, N), _fit(tk, K)
    nk = K // tk
    dims = {"nn": _NN, "nt": _NT, "tn": _TN}[mode]
    a_spec = (pl.BlockSpec((tk, tm), lambda i, j, k: (k, i)) if mode == "tn"
              else pl.BlockSpec((tm, tk), lambda i, j, k: (i, k)))
    b_spec = (pl.BlockSpec((tn, tk), lambda i, j, k: (j, k)) if mode == "nt"
              else pl.BlockSpec((tk, tn), lambda i, j, k: (k, j)))
    n_ex, n_out = len(extras), len(out_defs)

    def body(*refs):
        a_ref, b_ref = refs[0], refs[1]
        ex = refs[2:2 + n_ex]
        outs = refs[2 + n_ex:2 + n_ex + n_out]
        prod = lax.dot_general(a_ref[...], b_ref[...], dims, preferred_element_type=F32)
        if nk == 1:
            epilogue(prod, ex, outs)
        else:
            acc_ref = refs[-1]
            k = pl.program_id(2)

            @pl.when(k == 0)
            def _():
                acc_ref[...] = prod

            @pl.when(k > 0)
            def _():
                acc_ref[...] += prod

            @pl.when(k == nk - 1)
            def _():
                epilogue(acc_ref[...], ex, outs)

    def wrap(f):
        return lambda i, j, k: f(i, j)

    in_specs = [a_spec, b_spec] + [pl.BlockSpec(blk, wrap(f)) for _, blk, f in extras]
    out_specs = [pl.BlockSpec(blk, wrap(f)) for _, _, blk, f in out_defs]
    out_shape = [jax.ShapeDtypeStruct(s, d) for s, d, _, _ in out_defs]
    need = 2 * (tm * tk + tk * tn) * a.dtype.itemsize + 3 * tm * tn * 4
    need += sum(2 * _nbytes(blk, arr.dtype) for arr, blk, _ in extras)
    need += sum(2 * _nbytes(blk, d) for _, d, blk, _ in out_defs)
    res = pl.pallas_call(
        body, name=name, grid=(M // tm, N // tn, nk),
        in_specs=in_specs, out_specs=out_specs, out_shape=out_shape,
        scratch_shapes=[pltpu.VMEM((tm, tn), F32)] if nk > 1 else [],
        compiler_params=pltpu.CompilerParams(
            dimension_semantics=("parallel", "parallel", "arbitrary"), vmem_limit_bytes=_vmem(need)),
    )(_hbm(a), _hbm(b), *[_hbm(arr) for arr, _, _ in extras])
    return res


def _mm_plain(name, a, b, mode, out_dtype, **tiles):
    if mode == "nn":
        M, N = a.shape[0], b.shape[1]
    elif mode == "nt":
        M, N = a.shape[0], b.shape[0]
    else:
        M, N = a.shape[1], b.shape[1]
    tm, tn = _fit(tiles.get("tm", MM_TM), M), _fit(tiles.get("tn", MM_TN), N)

    def epi(acc, ex, outs):
        outs[0][...] = acc.astype(out_dtype)

    return _matmul(name, a, b, mode, [((M, N), out_dtype, (tm, tn), lambda i, j: (i, j))], epi, **tiles)[0]


def _rstd(v):
    return lax.rsqrt(jnp.mean(v * v, axis=-1, keepdims=True) + NORM_EPS)


def _row_call(name, body, row_ins, vec_ins, row_outs, acc_outs, S, D, tr):
    tr = _fit(tr, S)
    row_spec = pl.BlockSpec((tr, D), lambda r: (r, 0))
    vec_spec = pl.BlockSpec((1, D), lambda r: (0, 0))
    in_specs = [row_spec] * len(row_ins) + [vec_spec] * len(vec_ins)
    out_specs = [row_spec] * len(row_outs) + [pl.BlockSpec(shp, lambda r: (0, 0)) for shp in acc_outs]
    out_shape = [jax.ShapeDtypeStruct((S, D), d) for d in row_outs] + [jax.ShapeDtypeStruct(shp, F32) for shp in acc_outs]
    need = sum(2 * tr * D * a.dtype.itemsize for a in row_ins) + sum(2 * tr * D * jnp.dtype(d).itemsize for d in row_outs)
    need += 8 * tr * D * 4
    return pl.pallas_call(
        body, name=name, grid=(S // tr,), in_specs=in_specs, out_specs=out_specs, out_shape=out_shape,
        compiler_params=pltpu.CompilerParams(dimension_semantics=("arbitrary",), vmem_limit_bytes=_vmem(need)),
    )(*[_hbm(r) for r in row_ins], *vec_ins)


def _acc_rows(ref, rows):
    @pl.when(pl.program_id(0) == 0)
    def _():
        ref[...] = jnp.zeros_like(ref)
    for n, r in enumerate(rows):
        ref[n:n + 1, :] += r


def _pre_norm(x, g, sc, sh):
    S, D = x.shape

    def body(x_ref, g_ref, sc_ref, sh_ref, h_ref):
        xv = x_ref[...]
        xn = xv * _rstd(xv)
        h_ref[...] = (xn * g_ref[...] * (1.0 + sc_ref[...]) + sh_ref[...]).astype(BF16)

    return _row_call("pre_norm_mix", body, [x], [g, sc, sh], [BF16], [], S, D, 256)[0]


def _post_mix(x, mix, g_post, gt, g_pre, sc, sh):
    S, D = x.shape

    def body(x_ref, mix_ref, gp_ref, gt_ref, g2_ref, sc_ref, sh_ref, x1_ref, h2_ref):
        mv = mix_ref[...]
        x1 = x_ref[...] + gt_ref[...] * (mv * _rstd(mv) * gp_ref[...])
        x1_ref[...] = x1
        h2_ref[...] = (x1 * _rstd(x1) * g2_ref[...] * (1.0 + sc_ref[...]) + sh_ref[...]).astype(BF16)

    return _row_call("post_mix_pre_mlp", body, [x, mix], [g_post, gt, g_pre, sc, sh], [F32, BF16], [], S, D, 256)


def _loss_and_post_mlp_bwd(x1, y, target, g_post, gt):
    S, D = x1.shape

    def body(x1_ref, y_ref, t_ref, g_ref, gt_ref, dy_ref, dout_ref, loss_ref, acc_ref):
        yv = y_ref[...]
        r = _rstd(yv)
        yh = yv * r
        n = yh * g_ref[...]
        diff = x1_ref[...] + gt_ref[...] * n - t_ref[...]
        dout = diff * (1.0 / D)
        dout_ref[...] = dout
        dn = dout * gt_ref[...]
        dyh = dn * g_ref[...]
        dy_ref[...] = (r * (dyh - yh * jnp.mean(dyh * yh, axis=-1, keepdims=True))).astype(BF16)
        _acc_rows(acc_ref, [jnp.sum(dout * n, axis=0, keepdims=True), jnp.sum(dn * yh, axis=0, keepdims=True)])

        @pl.when(pl.program_id(0) == 0)
        def _():
            loss_ref[...] = jnp.zeros_like(loss_ref)
        loss_ref[...] += jnp.full(loss_ref.shape, (0.5 / D) * jnp.sum(diff * diff), F32)

    return _row_call("loss_post_mlp_bwd", body, [x1, y, target], [g_post, gt], [BF16, F32],
                     [(8, LANES), (8, D)], S, D, 128)


def _pre_mlp_and_post_mix_bwd(dh2, x1, dout, mix, g_pre, sc, g_post, gt):
    S, D = x1.shape

    def body(dh_ref, x1_ref, dout_ref, mix_ref, g_ref, sc_ref, gp_ref, gt_ref, dx1_ref, dmix_ref, acc_ref):
        dh = dh_ref[...]
        x1v = x1_ref[...]
        r3 = _rstd(x1v)
        xn = x1v * r3
        dxn = dh * (1.0 + sc_ref[...]) * g_ref[...]
        dx1 = dout_ref[...] + r3 * (dxn - xn * jnp.mean(dxn * xn, axis=-1, keepdims=True))
        dx1_ref[...] = dx1
        mv = mix_ref[...]
        r2 = _rstd(mv)
        mh = mv * r2
        dn = dx1 * gt_ref[...]
        dmh = dn * gp_ref[...]
        dmix_ref[...] = (r2 * (dmh - mh * jnp.mean(dmh * mh, axis=-1, keepdims=True))).astype(BF16)
        _acc_rows(acc_ref, [
            jnp.sum(dh, axis=0, keepdims=True),
            jnp.sum(dh * xn * g_ref[...], axis=0, keepdims=True),
            jnp.sum(dh * (1.0 + sc_ref[...]) * xn, axis=0, keepdims=True),
            jnp.sum(dx1 * mh * gp_ref[...], axis=0, keepdims=True),
            jnp.sum(dn * mh, axis=0, keepdims=True)])

    return _row_call("pre_mlp_post_mix_bwd", body, [dh2, x1, dout, mix], [g_pre, sc, g_post, gt], [F32, BF16],
                     [(8, D)], S, D, 128)


def _pre_mix_bwd(dh, x, dx1, g_pre, sc):
    S, D = x.shape

    def body(dh_ref, x_ref, dx1_ref, g_ref, sc_ref, gx_ref, acc_ref):
        dhv = dh_ref[...]
        xv = x_ref[...]
        r = _rstd(xv)
        xn = xv * r
        dxn = dhv * (1.0 + sc_ref[...]) * g_ref[...]
        gx_ref[...] = dx1_ref[...] + r * (dxn - xn * jnp.mean(dxn * xn, axis=-1, keepdims=True))
        _acc_rows(acc_ref, [
            jnp.sum(dhv, axis=0, keepdims=True),
            jnp.sum(dhv * xn * g_ref[...], axis=0, keepdims=True),
            jnp.sum(dhv * (1.0 + sc_ref[...]) * xn, axis=0, keepdims=True)])

    return _row_call("pre_mix_bwd", body, [dh, x, dx1], [g_pre, sc], [F32], [(8, D)], S, D, 128)


CUM_BLOCK = 256


def _tri(n, upper):
    r = lax.broadcasted_iota(jnp.int32, (n, n), 0)
    c = lax.broadcasted_iota(jnp.int32, (n, n), 1)
    return ((c >= r) if upper else (c <= r)).astype(F32)


def _fox_gate_fwd(fg, b_pad, n_fox):
    S = fg.shape[0]
    cb = _fit(CUM_BLOCK, S)

    def body(fg_ref, b_ref, cum_ref, cumt_ref, cumb_ref):
        low = _tri(cb, False)
        carry = jnp.zeros((1, LANES), F32)
        for n in range(S // cb):
            z = fg_ref[n * cb:(n + 1) * cb, :] + b_ref[...]
            logf = jnp.minimum(z, 0.0) - jnp.log(1.0 + jnp.exp(-jnp.abs(z)))
            blk = jnp.dot(low, logf, precision=lax.Precision.HIGHEST, preferred_element_type=F32) + carry
            cum_ref[n * cb:(n + 1) * cb, :] = blk
            carry = blk[cb - 1:cb, :]
        cum = cum_ref[...]
        cumt_ref[...] = cum.T
        for h in range(n_fox):
            cumb_ref[h] = jnp.broadcast_to(cum[:, h:h + 1], (S, LANES))

    return pl.pallas_call(
        body, name="fox_gate_fwd",
        out_shape=[jax.ShapeDtypeStruct((S, LANES), F32), jax.ShapeDtypeStruct((LANES, S), F32),
                   jax.ShapeDtypeStruct((n_fox, S, LANES), F32)],
        compiler_params=pltpu.CompilerParams(vmem_limit_bytes=_vmem((4 + 2 * n_fox) * S * LANES * 4)),
    )(fg, b_pad)


def _fox_gate_bwd(dcum_k, dcum_q, fg, b_pad):
    S = fg.shape[0]
    n_fox = dcum_q.shape[0]
    cb = _fit(CUM_BLOCK, S)

    def body(dk_ref, dq_ref, fg_ref, b_ref, dfg_ref, db_ref, dc_ref):
        lane = lax.broadcasted_iota(jnp.int32, (S, LANES), 1)
        dc = dk_ref[...].T
        for h in range(n_fox):
            dc = dc + jnp.where(lane == h, dq_ref[h], 0.0)
        dc_ref[...] = dc
        up = _tri(cb, True)
        carry = jnp.zeros((1, LANES), F32)
        db = jnp.zeros((1, LANES), F32)
        for n in reversed(range(S // cb)):
            blk = jnp.dot(up, dc_ref[n * cb:(n + 1) * cb, :], precision=lax.Precision.HIGHEST,
                          preferred_element_type=F32) + carry
            carry = blk[0:1, :]
            z = fg_ref[n * cb:(n + 1) * cb, :] + b_ref[...]
            dfg = blk * (1.0 / (1.0 + jnp.exp(z)))
            dfg_ref[n * cb:(n + 1) * cb, :] = dfg.astype(BF16)
            db = db + jnp.sum(dfg, axis=0, keepdims=True)
        db_ref[...] = jnp.broadcast_to(db, db_ref.shape)

    return pl.pallas_call(
        body, name="fox_gate_bwd",
        out_shape=[jax.ShapeDtypeStruct((S, LANES), BF16), jax.ShapeDtypeStruct((8, LANES), F32)],
        scratch_shapes=[pltpu.VMEM((S, LANES), F32)],
        compiler_params=pltpu.CompilerParams(vmem_limit_bytes=_vmem((8 + 2 * n_fox) * S * LANES * 4)),
    )(dcum_k, dcum_q, fg, b_pad)


FOX_TILE = 512


def _fox_scores(q, k, cq, ck, q0, k0, tq, tk, scale):
    s = lax.dot_general(q, k, _NT, preferred_element_type=F32) * scale + cq - ck
    row = q0 + lax.broadcasted_iota(jnp.int32, (tq, tk), 0)
    col = k0 + lax.broadcasted_iota(jnp.int32, (tq, tk), 1)
    return jnp.where(col <= row, s, NEG)


def _fox_fwd(proj, cum_b, cum_row, n_fox):
    S = proj.shape[0]
    t = _fit(FOX_TILE, S)
    nq = S // t
    scale = HEAD_DIM ** -0.5

    def body(q_ref, k_ref, v_ref, cq_ref, ck_ref, o_ref, lse_ref):
        def q_block(qi, _):
            q0 = pl.multiple_of(qi * t, t)
            q = q_ref[pl.ds(q0, t), :]
            cq = cq_ref[0, pl.ds(q0, t), :][:, :1]

            def kv_block(j, carry):
                m, l, acc = carry
                k0 = pl.multiple_of(j * t, t)
                s = _fox_scores(q, k_ref[pl.ds(k0, t), :], cq, ck_ref[0, :, pl.ds(k0, t)], q0, k0, t, t, scale)
                m_new = jnp.maximum(m, jnp.max(s, axis=-1, keepdims=True))
                alpha = jnp.exp(m - m_new)
                p = jnp.exp(s - m_new)
                l = alpha * l + jnp.sum(p, axis=-1, keepdims=True)
                acc = alpha * acc + jnp.dot(p.astype(BF16), v_ref[pl.ds(k0, t), :], preferred_element_type=F32)
                return m_new, l, acc

            init = (jnp.full((t, 1), NEG, F32), jnp.zeros((t, 1), F32), jnp.zeros((t, HEAD_DIM), F32))
            m, l, acc = lax.fori_loop(0, qi + 1, kv_block, init)
            o_ref[pl.ds(q0, t), :] = acc / l
            lse_ref[0, pl.ds(q0, t), :] = jnp.broadcast_to(m + jnp.log(l), (t, LANES))
            return 0

        lax.fori_loop(0, nq, q_block, 0)

    col = lambda off: pl.BlockSpec((S, HEAD_DIM), lambda h: (0, off + h))
    per_head = pl.BlockSpec((1, S, LANES), lambda h: (h, 0, 0))
    return pl.pallas_call(
        body, name="fox_fwd", grid=(n_fox,),
        in_specs=[col(0), col(n_fox), col(2 * n_fox), per_head, pl.BlockSpec((1, 1, S), lambda h: (h, 0, 0))],
        out_specs=[pl.BlockSpec((S, HEAD_DIM), lambda h: (0, h)), per_head],
        out_shape=[jax.ShapeDtypeStruct((S, n_fox * HEAD_DIM), F32), jax.ShapeDtypeStruct((n_fox, S, LANES), F32)],
        compiler_params=pltpu.CompilerParams(dimension_semantics=("parallel",),
                                             vmem_limit_bytes=_vmem(16 * S * HEAD_DIM * 4 + 12 * t * t * 4)),
    )(_hbm(proj), _hbm(proj), _hbm(proj), _hbm(cum_b), cum_row)


def _fox_bwd(proj, o, do, lse_b, cum_b, cum_row, n_fox):
    S = proj.shape[0]
    t = _fit(FOX_TILE, S)
    nq = S // t
    scale = HEAD_DIM ** -0.5

    def body(q_ref, k_ref, v_ref, o_ref, do_ref, lse_ref, cq_ref, ck_ref, dq_ref, dk_ref, dv_ref, dc_ref, dcq_ref,
             dq_acc, delta_ref):
        dq_acc[...] = jnp.zeros_like(dq_acc)
        dcq_ref[...] = jnp.zeros_like(dcq_ref)

        def delta_block(qi, _):
            q0 = pl.multiple_of(qi * t, t)
            d = jnp.sum(do_ref[pl.ds(q0, t), :] * o_ref[pl.ds(q0, t), :], axis=-1, keepdims=True)
            delta_ref[pl.ds(q0, t), :] = jnp.broadcast_to(d, (t, LANES))
            return 0

        lax.fori_loop(0, nq, delta_block, 0)

        def kv_block(j, _):
            k0 = pl.multiple_of(j * t, t)
            k = k_ref[pl.ds(k0, t), :]
            v = v_ref[pl.ds(k0, t), :]
            ck = ck_ref[0, :, pl.ds(k0, t)]

            def q_block(qi, carry):
                dk, dv, dc = carry
                q0 = pl.multiple_of(qi * t, t)
                q = q_ref[pl.ds(q0, t), :]
                dov = do_ref[pl.ds(q0, t), :].astype(BF16)
                s = _fox_scores(q, k, cq_ref[0, pl.ds(q0, t), :][:, :1], ck, q0, k0, t, t, scale)
                p = jnp.exp(s - lse_ref[0, pl.ds(q0, t), :][:, :1])
                dp = lax.dot_general(dov, v, _NT, preferred_element_type=F32)
                ds = p * (dp - delta_ref[pl.ds(q0, t), :][:, :1])
                dsb = ds.astype(BF16)
                dv = dv + lax.dot_general(p.astype(BF16), dov, _TN, preferred_element_type=F32)
                dk = dk + lax.dot_general(dsb, q, _TN, preferred_element_type=F32)
                dq_acc[pl.ds(q0, t), :] += jnp.dot(dsb, k, preferred_element_type=F32)
                dc = dc - jnp.sum(ds, axis=0, keepdims=True)
                dcq_ref[0, pl.ds(q0, t), :] += jnp.broadcast_to(jnp.sum(ds, axis=1, keepdims=True), (t, LANES))
                return dk, dv, dc

            init = (jnp.zeros((t, HEAD_DIM), F32), jnp.zeros((t, HEAD_DIM), F32), jnp.zeros((1, t), F32))
            dk, dv, dc = lax.fori_loop(j, nq, q_block, init)
            dk_ref[pl.ds(k0, t), :] = (dk * scale).astype(BF16)
            dv_ref[pl.ds(k0, t), :] = dv.astype(BF16)
            dc_ref[0, :, pl.ds(k0, t)] = dc
            return 0

        lax.fori_loop(0, nq, kv_block, 0)
        dq_ref[...] = (dq_acc[...] * scale).astype(BF16)

    col = lambda off: pl.BlockSpec((S, HEAD_DIM), lambda h: (0, off + h))
    per_head = pl.BlockSpec((1, S, LANES), lambda h: (h, 0, 0))
    row = pl.BlockSpec((1, 1, S), lambda h: (h, 0, 0))
    grad = jax.ShapeDtypeStruct((S, n_fox * HEAD_DIM), BF16)
    return pl.pallas_call(
        body, name="fox_bwd", grid=(n_fox,),
        in_specs=[col(0), col(n_fox), col(2 * n_fox), col(0), col(0), per_head, per_head, row],
        out_specs=[col(0), col(0), col(0), row, per_head],
        out_shape=[grad, grad, grad, jax.ShapeDtypeStruct((n_fox, 1, S), F32), jax.ShapeDtypeStruct((n_fox, S, LANES), F32)],
        scratch_shapes=[pltpu.VMEM((S, HEAD_DIM), F32), pltpu.VMEM((S, LANES), F32)],
        compiler_params=pltpu.CompilerParams(dimension_semantics=("parallel",),
                                             vmem_limit_bytes=_vmem(24 * S * HEAD_DIM * 4 + 16 * t * t * 4)),
    )(_hbm(proj), _hbm(proj), _hbm(proj), _hbm(o), _hbm(do), _hbm(lse_b), _hbm(cum_b), cum_row)


def _rope_tables(S):
    half = HEAD_DIM // 2
    inv_freq = 1.0 / (ROPE_THETA ** (jnp.arange(half, dtype=F32) * (2.0 / HEAD_DIM)))
    ang = jnp.arange(S).astype(F32)[:, None] * inv_freq[None, :]
    cos, sin = jnp.cos(ang), jnp.sin(ang)
    return jnp.concatenate([cos, cos], axis=-1), jnp.concatenate([-sin, sin], axis=-1)


def _rope(name, src, first_block, n_blocks, cos, sin_signed):
    S = src.shape[0]

    def body(x_ref, cos_ref, sin_ref, o_ref):
        xv = x_ref[...].astype(F32)
        o_ref[...] = (xv * cos_ref[...] + pltpu.roll(xv, HEAD_DIM // 2, 1) * sin_ref[...]).astype(BF16)

    table = pl.BlockSpec((S, HEAD_DIM), lambda n: (0, 0))
    return pl.pallas_call(
        body, name=name, grid=(n_blocks,),
        in_specs=[pl.BlockSpec((S, HEAD_DIM), lambda n: (0, first_block + n)), table, table],
        out_specs=pl.BlockSpec((S, HEAD_DIM), lambda n: (0, n)),
        out_shape=jax.ShapeDtypeStruct((S, n_blocks * HEAD_DIM), BF16),
        compiler_params=pltpu.CompilerParams(dimension_semantics=("parallel",),
                                             vmem_limit_bytes=_vmem(12 * S * HEAD_DIM * 4)),
    )(_hbm(src), cos, sin_signed)


def _swa_tile(q_ref, kp_ref, kc_ref, n, group, scale):
    B = SWA_BLOCK
    qs = jnp.concatenate([q_ref[:, g * HEAD_DIM:(g + 1) * HEAD_DIM] for g in range(group)], axis=0)
    kcat = jnp.concatenate([kp_ref[...], kc_ref[...]], axis=0)
    s = lax.dot_general(qs, kcat, _NT, preferred_element_type=F32) * scale
    qi = lax.broadcasted_iota(jnp.int32, (group * B, 2 * B), 0) % B
    kj = lax.broadcasted_iota(jnp.int32, (group * B, 2 * B), 1)
    diff = qi + B - kj
    mask = (diff >= 0) & (diff < B) & ((n * B + kj - B) >= 0)
    return qs, kcat, jnp.where(mask, s, NEG)


def _swa_sink_col(sink_ref, kv, group):
    head = lax.broadcasted_iota(jnp.int32, (group * SWA_BLOCK, 1), 0) // SWA_BLOCK
    col = jnp.zeros((group * SWA_BLOCK, 1), F32)
    for g in range(group):
        col = jnp.where(head == g, sink_ref[kv * group + g], col)
    return col


def _swa_specs(n_kv, group, q_first, k_first, v_first):
    B = SWA_BLOCK
    prev = lambda n: jnp.maximum(n - 1, 0)
    return [
        pl.BlockSpec((B, group * HEAD_DIM), lambda kv, n: (n, q_first + kv)),
        pl.BlockSpec((B, HEAD_DIM), lambda kv, n: (prev(n), k_first + kv)),
        pl.BlockSpec((B, HEAD_DIM), lambda kv, n: (n, k_first + kv)),
        pl.BlockSpec((B, HEAD_DIM), lambda kv, n: (prev(n), v_first + kv)),
        pl.BlockSpec((B, HEAD_DIM), lambda kv, n: (n, v_first + kv)),
    ]


def _swa_fwd(rq, proj, v_first, sinks, n_q, n_kv):
    S = rq.shape[0]
    B = SWA_BLOCK
    group = n_q // n_kv
    scale = HEAD_DIM ** -0.5

    def body(q_ref, kp_ref, kc_ref, vp_ref, vc_ref, sink_ref, o_ref, lse_ref):
        kv, n = pl.program_id(0), pl.program_id(1)
        _, _, s = _swa_tile(q_ref, kp_ref, kc_ref, n, group, scale)
        sink = _swa_sink_col(sink_ref, kv, group)
        m = jnp.maximum(jnp.max(s, axis=-1, keepdims=True), sink)
        p = jnp.exp(s - m)
        denom = jnp.sum(p, axis=-1, keepdims=True) + jnp.exp(sink - m)
        vcat = jnp.concatenate([vp_ref[...], vc_ref[...]], axis=0)
        o = jnp.dot((p / denom).astype(BF16), vcat, preferred_element_type=F32)
        lse = m + jnp.log(denom)
        for g in range(group):
            o_ref[:, g * HEAD_DIM:(g + 1) * HEAD_DIM] = o[g * B:(g + 1) * B, :]
            lse_ref[0, :, g * LANES:(g + 1) * LANES] = jnp.broadcast_to(lse[g * B:(g + 1) * B, :], (B, LANES))

    specs = _swa_specs(n_kv, group, 0, n_q, v_first)
    q_blk = pl.BlockSpec((B, group * HEAD_DIM), lambda kv, n: (n, kv))
    return pl.pallas_call(
        body, name="swa_fwd", grid=(n_kv, S // B),
        in_specs=specs + [pl.BlockSpec(memory_space=pltpu.SMEM)],
        out_specs=[q_blk, pl.BlockSpec((1, B, group * LANES), lambda kv, n: (kv, n, 0))],
        out_shape=[jax.ShapeDtypeStruct((S, n_q * HEAD_DIM), F32), jax.ShapeDtypeStruct((n_kv, S, group * LANES), F32)],
        compiler_params=pltpu.CompilerParams(dimension_semantics=("parallel", "arbitrary")),
    )(_hbm(rq), _hbm(rq), _hbm(rq), _hbm(proj), _hbm(proj), sinks)


def _swa_bwd(rq, proj, v_first, sinks, o, do, do_first, lse_b, n_q, n_kv):
    S = rq.shape[0]
    B = SWA_BLOCK
    group = n_q // n_kv
    scale = HEAD_DIM ** -0.5

    def body(q_ref, kp_ref, kc_ref, vp_ref, vc_ref, o_ref, do_ref, lse_ref, sink_ref,
             dq_ref, dk_ref, dv_ref, dsink_ref):
        kv, n = pl.program_id(0), pl.program_id(1)

        @pl.when(n == 0)
        def _():
            dk_ref[...] = jnp.zeros_like(dk_ref)
            dv_ref[...] = jnp.zeros_like(dv_ref)
            dsink_ref[...] = jnp.zeros_like(dsink_ref)

        qs, kcat, s = _swa_tile(q_ref, kp_ref, kc_ref, n, group, scale)
        sink = _swa_sink_col(sink_ref, kv, group)
        stack = lambda ref, w: jnp.concatenate([ref[:, g * w:(g + 1) * w] for g in range(group)], axis=0)
        lse = jnp.concatenate([lse_ref[0, :, g * LANES:g * LANES + 1] for g in range(group)], axis=0)
        do32 = stack(do_ref, HEAD_DIM)
        delta = jnp.sum(do32 * stack(o_ref, HEAD_DIM), axis=-1, keepdims=True)
        dov = do32.astype(BF16)
        p = jnp.exp(s - lse)
        vcat = jnp.concatenate([vp_ref[...], vc_ref[...]], axis=0)
        dp = lax.dot_general(dov, vcat, _NT, preferred_element_type=F32)
        ds = p * (dp - delta)
        dsb = ds.astype(BF16)
        dq = jnp.dot(dsb, kcat, preferred_element_type=F32) * scale
        for g in range(group):
            dq_ref[:, g * HEAD_DIM:(g + 1) * HEAD_DIM] = dq[g * B:(g + 1) * B, :].astype(BF16)
        dkcat = lax.dot_general(dsb, qs, _TN, preferred_element_type=F32) * scale
        dvcat = lax.dot_general(p.astype(BF16), dov, _TN, preferred_element_type=F32)
        prev0 = pl.multiple_of(jnp.maximum(n - 1, 0) * B, B)
        cur0 = pl.multiple_of(n * B, B)
        dk_ref[0, pl.ds(prev0, B), :] += dkcat[:B, :]
        dk_ref[0, pl.ds(cur0, B), :] += dkcat[B:, :]
        dv_ref[0, pl.ds(prev0, B), :] += dvcat[:B, :]
        dv_ref[0, pl.ds(cur0, B), :] += dvcat[B:, :]
        dsk = -jnp.exp(sink - lse) * delta
        lane = lax.broadcasted_iota(jnp.int32, (1, LANES), 1)
        row = jnp.zeros((1, LANES), F32)
        for g in range(group):
            row = row + jnp.where(lane == g, jnp.sum(dsk[g * B:(g + 1) * B, :]), 0.0)
        dsink_ref[0, 0:1, :] += row

    specs = _swa_specs(n_kv, group, 0, n_q, v_first)
    q_blk = pl.BlockSpec((B, group * HEAD_DIM), lambda kv, n: (n, kv))
    acc = pl.BlockSpec((1, S, HEAD_DIM), lambda kv, n: (kv, 0, 0))
    return pl.pallas_call(
        body, name="swa_bwd", grid=(n_kv, S // B),
        in_specs=specs + [q_blk, pl.BlockSpec((B, group * HEAD_DIM), lambda kv, n: (n, do_first + kv)),
                          pl.BlockSpec((1, B, group * LANES), lambda kv, n: (kv, n, 0)),
                          pl.BlockSpec(memory_space=pltpu.SMEM)],
        out_specs=[q_blk, acc, acc, pl.BlockSpec((1, 8, LANES), lambda kv, n: (kv, 0, 0))],
        out_shape=[jax.ShapeDtypeStruct((S, n_q * HEAD_DIM), BF16), jax.ShapeDtypeStruct((n_kv, S, HEAD_DIM), F32),
                   jax.ShapeDtypeStruct((n_kv, S, HEAD_DIM), F32), jax.ShapeDtypeStruct((n_kv, 8, LANES), F32)],
        compiler_params=pltpu.CompilerParams(dimension_semantics=("parallel", "arbitrary")),
    )(_hbm(rq), _hbm(rq), _hbm(rq), _hbm(proj), _hbm(proj), _hbm(o), _hbm(do), _hbm(lse_b), sinks)


def _adamw(w, g, m, v):
    m = ADAM_B1 * m + (1.0 - ADAM_B1) * g
    v = ADAM_B2 * v + (1.0 - ADAM_B2) * (g * g)
    m_hat = m / (1.0 - ADAM_B1 ** ADAM_STEP)
    v_hat = v / (1.0 - ADAM_B2 ** ADAM_STEP)
    delta = -ADAM_LR * (m_hat / (jnp.sqrt(v_hat) + ADAM_EPS) + ADAM_WD * w)
    return delta, m, v


def _mod_fwd(cond_in, w_mod, b_shard):
    R, D = cond_in.shape
    cols = w_mod.shape[1]
    tn = _fit(512, cols)

    def body(c_ref, w_ref, b_ref, o_ref):
        cv = c_ref[...]
        cond = (cv / (1.0 + jnp.exp(-cv))).astype(BF16)
        o_ref[...] = jnp.dot(cond, w_ref[...].astype(BF16), preferred_element_type=F32) + b_ref[...]

    return pl.pallas_call(
        body, name="mod_fwd", grid=(cols // tn,),
        in_specs=[pl.BlockSpec((R, D), lambda j: (0, 0)), pl.BlockSpec((D, tn), lambda j: (0, j)),
                  pl.BlockSpec((1, tn), lambda j: (0, j))],
        out_specs=pl.BlockSpec((R, tn), lambda j: (0, j)),
        out_shape=jax.ShapeDtypeStruct((R, cols), F32),
        compiler_params=pltpu.CompilerParams(dimension_semantics=("parallel",), vmem_limit_bytes=_vmem(3 * D * tn * 4)),
    )(cond_in, _hbm(w_mod), b_shard)


def _mod_update(c_t, dmod, w, m, v):
    D, nb = c_t.shape
    cols = w.shape[1]
    tn = _fit(256, cols)

    def body(c_ref, d_ref, w_ref, m_ref, v_ref, g_ref, dl_ref, nm_ref, nv_ref):
        cv = c_ref[...]
        cond = cv / (1.0 + jnp.exp(-cv))
        g = jnp.zeros((D, tn), F32)
        for b in range(nb):
            g = g + cond[:, b:b + 1] * d_ref[b:b + 1, :]
        g_ref[...] = g
        dl_ref[...], nm_ref[...], nv_ref[...] = _adamw(w_ref[...], g, m_ref[...], v_ref[...])

    blk = pl.BlockSpec((D, tn), lambda j: (0, j))
    out = jax.ShapeDtypeStruct((D, cols), F32)
    return pl.pallas_call(
        body, name="mod_update", grid=(cols // tn,),
        in_specs=[pl.BlockSpec((D, nb), lambda j: (0, 0)), pl.BlockSpec((nb, tn), lambda j: (0, j)), blk, blk, blk],
        out_specs=[blk] * 4, out_shape=[out] * 4,
        compiler_params=pltpu.CompilerParams(dimension_semantics=("parallel",), vmem_limit_bytes=_vmem(18 * D * tn * 4)),
    )(c_t, dmod, _hbm(w), _hbm(m), _hbm(v))


def _small_update(stacked, w, m, v):
    R, C = w.shape

    def body(s_ref, w_ref, m_ref, v_ref, g_ref, dl_ref, nm_ref, nv_ref):
        g = s_ref[0:R, :]
        for d in range(1, N_DEV):
            g = g + s_ref[d * R:(d + 1) * R, :]
        g_ref[...] = g
        dl_ref[...], nm_ref[...], nv_ref[...] = _adamw(w_ref[...], g, m_ref[...], v_ref[...])

    return pl.pallas_call(body, name="small_update", out_shape=[jax.ShapeDtypeStruct((R, C), F32)] * 4)(stacked, w, m, v)


def _place():
    return lax.axis_index("x"), lax.axis_index("y"), lax.axis_index("c")


def _allgather8(name, block):
    m_per, n = block.shape

    def body(x_ref, out_ref, token_ref, send_sems, recv_sems, local_sem):
        token_ref[...] = jnp.zeros_like(token_ref)
        x, y, c = _place()
        me, sibling = (x, y, c), (x, y, 1 - c)
        chips = [(1 - x, y), (x, 1 - y), (1 - x, 1 - y)]

        def rows(px, py, pc):
            return out_ref.at[pl.ds((4 * px + 2 * py + pc) * m_per, m_per), :]

        def copy(k, blk, to, src=None):
            return pltpu.make_async_remote_copy(
                src_ref=rows(*blk) if src is None else src, dst_ref=rows(*blk),
                send_sem=send_sems.at[k], recv_sem=recv_sems.at[k], device_id=to, device_id_type=MESH)

        mine = pltpu.make_async_copy(x_ref, rows(*me), local_sem)
        mine.start()
        first = [copy(0, me, sibling, src=x_ref)]
        first += [copy(1 + j, me, (*chip, c), src=x_ref) for j, chip in enumerate(chips)]
        for cp in first:
            cp.start()
        passed = [copy(4 + j, (*chip, c), sibling) for j, chip in enumerate(chips)]
        for j, chip in enumerate(chips):
            copy(1 + j, (*chip, c), me).wait_recv()
            passed[j].start()
        copy(0, sibling, me).wait_recv()
        for j, chip in enumerate(chips):
            copy(4 + j, (*chip, 1 - c), me).wait_recv()
        for cp in first + passed:
            cp.wait_send()
        mine.wait()

    vmem = pl.BlockSpec(memory_space=pltpu.VMEM)
    return pl.pallas_call(
        body, name=name,
        out_shape=[jax.ShapeDtypeStruct((N_DEV * m_per, n), block.dtype), jax.ShapeDtypeStruct((8, LANES), F32)],
        in_specs=[vmem], out_specs=[vmem, vmem],
        scratch_shapes=[pltpu.SemaphoreType.DMA((7,)), pltpu.SemaphoreType.DMA((7,)), pltpu.SemaphoreType.DMA],
    )(block)


_ANY = pl.BlockSpec(memory_space=pl.ANY)


def _half(ref, c, rows):
    return ref.at[pl.ds(c * (rows // 2), rows // 2), :]


_HBM = pl.BlockSpec(memory_space=pltpu.HBM)
_SEM = pl.BlockSpec(memory_space=pltpu.SEMAPHORE)
_EFFECT = pltpu.SideEffectType.DATAFLOW_SIDE_EFFECTING


def _ici_start(name, srcs, land_shapes, plan):
    ns, nl = len(srcs), len(land_shapes)
    n_copies = 3 * ns

    def body(*refs):
        src_refs, land_refs = refs[:ns], refs[ns:ns + nl]
        send_sems, recv_sems = refs[ns + nl], refs[ns + nl + 1]
        token = refs[-1]
        for n, (src, dst, peer, _) in enumerate(plan(src_refs, land_refs)):
            pltpu.make_async_remote_copy(src_ref=src, dst_ref=dst, send_sem=send_sems.at[n], recv_sem=recv_sems.at[n],
                                         device_id=peer, device_id_type=MESH).start()
        token[...] = jnp.zeros_like(token)

    lands = [lax.empty(s.shape, s.dtype) for s in land_shapes]
    out = pl.pallas_call(
        body, name=name,
        out_shape=(pltpu.SemaphoreType.DMA((n_copies,)), pltpu.SemaphoreType.DMA((n_copies,)),
                   *[pltpu.HBM(a.shape, a.dtype) for a in list(srcs) + lands], jax.ShapeDtypeStruct((8, LANES), F32)),
        in_specs=[_HBM] * (ns + nl),
        out_specs=(_SEM, _SEM, *[_HBM] * (ns + nl), pl.BlockSpec(memory_space=pltpu.VMEM)),
        input_output_aliases={n: 2 + n for n in range(ns + nl)},
        compiler_params=pltpu.CompilerParams(has_side_effects=_EFFECT),
    )(*[pltpu.with_memory_space_constraint(a, pltpu.HBM) for a in list(srcs) + lands])
    return out[0], out[1], list(out[2:2 + ns]), list(out[2 + ns:2 + ns + nl]), out[-1]


def _ici_wait(name, send_sems, recv_sems, srcs, lands, plan, after):
    ns, nl = len(srcs), len(lands)

    def body(*refs):
        src_refs, land_refs = refs[:ns], refs[ns:ns + nl]
        send_sems, recv_sems = refs[ns + nl], refs[ns + nl + 1]
        for n, (src, _, peer, mine) in enumerate(plan(src_refs, land_refs)):
            cp = pltpu.make_async_remote_copy(src_ref=src, dst_ref=mine, send_sem=send_sems.at[n],
                                              recv_sem=recv_sems.at[n], device_id=peer, device_id_type=MESH)
            cp.wait_send()
            cp.wait_recv()

    out = pl.pallas_call(
        body, name=name, out_shape=[pltpu.HBM(a.shape, a.dtype) for a in list(srcs) + list(lands)],
        in_specs=[_HBM] * (ns + nl) + [_SEM, _SEM, _ANY], out_specs=[_HBM] * (ns + nl),
        input_output_aliases={n: n for n in range(ns + nl)},
        compiler_params=pltpu.CompilerParams(has_side_effects=_EFFECT),
    )(*srcs, *lands, send_sems, recv_sems, after)
    return list(out[:ns]), list(out[ns:])


def _gather_plan(src_refs, land_refs):
    x, y, c = _place()
    copies = []
    for w, land in zip(src_refs, land_refs):
        R = w.shape[0]
        for cx, cy in [(1 - x, y), (x, 1 - y), (1 - x, 1 - y)]:
            copies.append((_half(w, c, R), _half(land.at[2 * x + y], c, R), (cx, cy, c),
                           _half(land.at[2 * cx + cy], c, R)))
    return copies


def _pass_to_sibling(name, lands):
    nw = len(lands)

    def body(*refs):
        ins, outs = refs[:nw], refs[nw:2 * nw]
        send_sems, recv_sems = refs[2 * nw:]
        x, y, c = _place()
        chips = [(1 - x, y), (x, 1 - y), (1 - x, 1 - y)]
        copies = []
        for k in range(nw):
            R = ins[k].shape[1]
            for j, (cx, cy) in enumerate(chips):
                cp = pltpu.make_async_remote_copy(
                    src_ref=_half(ins[k].at[2 * cx + cy], c, R), dst_ref=_half(outs[k].at[2 * cx + cy], c, R),
                    send_sem=send_sems.at[3 * k + j], recv_sem=recv_sems.at[3 * k + j],
                    device_id=(x, y, 1 - c), device_id_type=MESH)
                cp.start()
                copies.append(cp)
        for k in range(nw):
            R = ins[k].shape[1]
            for j, (cx, cy) in enumerate(chips):
                pltpu.make_async_remote_copy(
                    src_ref=_half(ins[k].at[2 * cx + cy], c, R), dst_ref=_half(outs[k].at[2 * cx + cy], 1 - c, R),
                    send_sem=send_sems.at[3 * k + j], recv_sem=recv_sems.at[3 * k + j],
                    device_id=(x, y, 1 - c), device_id_type=MESH).wait_recv()
        for cp in copies:
            cp.wait_send()

    return pl.pallas_call(
        body, name=name, out_shape=[jax.ShapeDtypeStruct(a.shape, a.dtype) for a in lands],
        in_specs=[_ANY] * nw, out_specs=[_ANY] * nw, input_output_aliases={k: k for k in range(nw)},
        scratch_shapes=[pltpu.SemaphoreType.DMA((3 * nw,)), pltpu.SemaphoreType.DMA((3 * nw,))],
    )(*lands)


def _tie(vec, token):
    return vec + token[0:1, 0:1]


def _pair_exchange(name, grads):
    nw = len(grads)

    def body(*refs):
        gs, outs = refs[:nw], refs[nw:2 * nw]
        send_sems, recv_sems = refs[2 * nw:]
        x, y, c = _place()
        copies = []
        for k in range(nw):
            half = gs[k].shape[1] // 2
            cp = pltpu.make_async_remote_copy(
                src_ref=gs[k].at[:, pl.ds((1 - c) * half, half), :], dst_ref=outs[k],
                send_sem=send_sems.at[k], recv_sem=recv_sems.at[k], device_id=(x, y, 1 - c), device_id_type=MESH)
            cp.start()
            copies.append(cp)
        for cp in copies:
            cp.wait()

    return pl.pallas_call(
        body, name=name,
        out_shape=[jax.ShapeDtypeStruct((N_CHIPS, g.shape[1] // 2, g.shape[2]), g.dtype) for g in grads],
        in_specs=[_ANY] * nw, out_specs=[_ANY] * nw,
        scratch_shapes=[pltpu.SemaphoreType.DMA((nw,)), pltpu.SemaphoreType.DMA((nw,))],
    )(*grads)


def _pair_add(name, core, grad, recv):
    n, R, C = grad.shape
    half = R // 2
    tr = _fit(256, half)
    nblk = half // tr

    def body(core_ref, g_ref, r_ref, o_ref):
        o_ref[...] = (g_ref[...].astype(F32) + r_ref[...].astype(F32)).astype(BF16)

    grid_spec = pltpu.PrefetchScalarGridSpec(
        num_scalar_prefetch=1, grid=(n, nblk),
        in_specs=[pl.BlockSpec((1, tr, C), lambda s, r, core_ref: (s, core_ref[0] * nblk + r, 0)),
                  pl.BlockSpec((1, tr, C), lambda s, r, core_ref: (s, r, 0))],
        out_specs=pl.BlockSpec((1, tr, C), lambda s, r, core_ref: (s, r, 0)))
    return pl.pallas_call(
        body, name=name, grid_spec=grid_spec, out_shape=jax.ShapeDtypeStruct((n, half, C), BF16),
        compiler_params=pltpu.CompilerParams(dimension_semantics=("parallel", "parallel")),
    )(core, _hbm(grad), _hbm(recv))


def _scatter_plan(src_refs, land_refs):
    x, y, c = _place()
    copies = []
    for p, land in zip(src_refs, land_refs):
        for j, (cx, cy) in enumerate([(1 - x, y), (x, 1 - y), (1 - x, 1 - y)]):
            copies.append((p.at[2 * cx + cy], land.at[j], (cx, cy, c), land.at[j]))
    return copies


def _chip_add(name, chip, sums, recv):
    _, H, C = sums.shape
    tr = _fit(256, H)

    def body(chip_ref, p_ref, r_ref, o_ref):
        total = p_ref[0].astype(F32)
        for j in range(3):
            total = total + r_ref[j].astype(F32)
        o_ref[...] = total

    grid_spec = pltpu.PrefetchScalarGridSpec(
        num_scalar_prefetch=1, grid=(H // tr,),
        in_specs=[pl.BlockSpec((1, tr, C), lambda r, chip_ref: (chip_ref[0], r, 0)),
                  pl.BlockSpec((3, tr, C), lambda r, chip_ref: (0, r, 0))],
        out_specs=pl.BlockSpec((tr, C), lambda r, chip_ref: (r, 0)))
    return pl.pallas_call(
        body, name=name, grid_spec=grid_spec, out_shape=jax.ShapeDtypeStruct((H, C), F32),
        compiler_params=pltpu.CompilerParams(dimension_semantics=("parallel",)),
    )(chip, _hbm(sums), _hbm(recv))


def _pair_share(name, halves):
    nw = len(halves)

    def body(*refs):
        hs, outs = refs[:nw], refs[nw:2 * nw]
        send_sems, recv_sems = refs[2 * nw:]
        x, y, c = _place()
        copies = []
        for k in range(nw):
            cp = pltpu.make_async_remote_copy(
                src_ref=hs[k], dst_ref=outs[k], send_sem=send_sems.at[k], recv_sem=recv_sems.at[k],
                device_id=(x, y, 1 - c), device_id_type=MESH)
            cp.start()
            copies.append(cp)
        for cp in copies:
            cp.wait()

    return pl.pallas_call(
        body, name=name,
        out_shape=[jax.ShapeDtypeStruct(h.shape, h.dtype) for h in halves],
        in_specs=[_ANY] * nw, out_specs=[_ANY] * nw,
        scratch_shapes=[pltpu.SemaphoreType.DMA((nw,)), pltpu.SemaphoreType.DMA((nw,))],
    )(*halves)


def _adam_halves(name, core, w, g_own, g_other, m, v):
    R, C = w.shape
    H = R // 2
    tr = _fit(256, H)
    nblk = H // tr

    def body(core_ref, w_ref, go_ref, gr_ref, m_ref, v_ref, g_ref, dl_ref, nm_ref, nv_ref):
        own = (pl.program_id(0) // nblk) == core_ref[0]
        g = jnp.where(own, go_ref[...], gr_ref[...])
        g_ref[...] = g
        dl_ref[...], nm_ref[...], nv_ref[...] = _adamw(w_ref[...], g, m_ref[...], v_ref[...])

    blk = pl.BlockSpec((tr, C), lambda r, core_ref: (r, 0))

    def half_spec(is_own):
        def index(r, core_ref):
            mine = ((r // nblk) == core_ref[0]) == is_own
            return (jnp.where(mine, r % nblk, jnp.where(is_own == (core_ref[0] == 0), nblk - 1, 0)), 0)
        return pl.BlockSpec((tr, C), index)
    out = jax.ShapeDtypeStruct((R, C), F32)
    padded = -(-C // LANES) * LANES
    grid_spec = pltpu.PrefetchScalarGridSpec(
        num_scalar_prefetch=1, grid=(R // tr,), in_specs=[blk, half_spec(True), half_spec(False), blk, blk],
        out_specs=[blk] * 4)
    return pl.pallas_call(
        body, name=name, grid_spec=grid_spec, out_shape=[out] * 4,
        compiler_params=pltpu.CompilerParams(dimension_semantics=("parallel",), vmem_limit_bytes=_vmem(20 * tr * padded * 4)),
    )(core, _hbm(w), _hbm(g_own), _hbm(g_other), _hbm(m), _hbm(v))


def kernel(x, c, w_mod, b_mod, g_pre_mix, g_post_mix, w_in, b_forget, swa_sinks, w_out, g_pre_mlp, g_post_mlp, w_up, w_down, loss_target, m_w_mod, m_b_mod, m_g_pre_mix, m_g_post_mix, m_w_in, m_b_forget, m_swa_sinks, m_w_out, m_g_pre_mlp, m_g_post_mlp, m_w_up, m_w_down, v_w_mod, v_b_mod, v_g_pre_mix, v_g_post_mix, v_w_in, v_b_forget, v_swa_sinks, v_w_out, v_g_pre_mlp, v_g_post_mlp, v_w_up, v_w_down):
    S, D = x.shape[1], x.shape[2]
    n_heads = D // HEAD_DIM
    n_fox = n_heads // 2
    n_swa = n_heads - n_fox
    n_kv = max(1, n_swa // 4)
    fox_w, swa_w, kv_w = n_fox * HEAD_DIM, n_swa * HEAD_DIM, n_kv * HEAD_DIM
    main_w = 3 * fox_w + swa_w + 2 * kv_w
    in_w = main_w + n_fox
    mod_cols = w_mod.shape[2]

    ax, ay, ac = _place()
    chip = 2 * ax + ay
    dev = 2 * chip + ac
    chip_arr = jnp.reshape(chip, (1,)).astype(jnp.int32)
    core_arr = jnp.reshape(ac, (1,)).astype(jnp.int32)

    x2, tgt = x[0], loss_target[0]

    names = ["w_in", "w_out", "w_up", "w_down"]
    flights = {}

    def start_gather(n, w, token):
        shard = (w[0] if token is None else _tie(w[0], token)).astype(BF16)
        flights[n] = _ici_start("gather_start_" + n, [shard], [jax.ShapeDtypeStruct((N_CHIPS,) + shard.shape, BF16)],
                                _gather_plan)
        return flights[n][4]

    token = start_gather("w_in", w_in, None)
    c_all, _ = _allgather8("gather_c", _tie(c, token).reshape(8, D // 8))
    c_all = c_all.reshape(N_DEV, D)
    b_shard = lax.dynamic_slice_in_dim(b_mod, chip * mod_cols, mod_cols, axis=1)
    mod_shard = _mod_fwd(jnp.pad(c_all, ((0, 16 - N_DEV), (0, 0))), w_mod[0], b_shard)[:N_DEV]
    mod_all, token = _allgather8("gather_mod", mod_shard)
    mod_all = mod_all.reshape(N_CHIPS, 2, N_DEV, mod_cols)[:, 0]
    mod = lax.dynamic_index_in_dim(mod_all, dev, axis=1, keepdims=False).reshape(N_MOD, 1, D)
    sh_a, sc_a, gt_a, sh_m, sc_m, gt_m = [mod[n] for n in range(N_MOD)]
    for n, w in zip(names[1:], [w_out, w_up, w_down]):
        token = start_gather(n, w, token)
    sc_a = _tie(sc_a, token)

    def gathered(n, after):
        send, recv, srcs, lands, _ = flights[n]
        srcs, lands = _ici_wait("gather_wait_" + n, send, recv, srcs, lands, _gather_plan, after)
        lands = _pass_to_sibling("gather_pass_" + n, lands)
        return lax.dynamic_update_index_in_dim(lands[0], srcs[0], chip, 0)

    d_ff = N_CHIPS * w_up.shape[2]

    h = _pre_norm(x2, g_pre_mix, sc_a, sh_a)
    w_in_f = jnp.transpose(gathered("w_in", h), (1, 0, 2)).reshape(D, in_w)
    w_main = jnp.concatenate([w_in_f[:, :3 * fox_w], w_in_f[:, 3 * fox_w + n_fox:]], axis=1)
    w_fg = jnp.pad(w_in_f[:, 3 * fox_w:3 * fox_w + n_fox], ((0, 0), (0, LANES - n_fox)))
    proj = _mm_plain("in_proj", h, w_main, "nn", BF16, tn=_fit(768, main_w))
    fg = _mm_plain("in_proj_gate", h, w_fg, "nn", F32)
    b_pad = jnp.pad(b_forget, ((0, 0), (0, LANES - n_fox)))
    cum, cum_t, cum_b = _fox_gate_fwd(fg, b_pad, n_fox)
    cum_row = cum_t[:n_fox].reshape(n_fox, 1, S)
    fox_o, fox_lse = _fox_fwd(proj, cum_b, cum_row, n_fox)

    cos, sin_signed = _rope_tables(S)
    rq = _rope("rope_fwd", proj, 3 * n_fox, n_swa + n_kv, cos, sin_signed)
    v_first = 3 * n_fox + n_swa + n_kv
    sinks = swa_sinks[0]
    swa_o, swa_lse = _swa_fwd(rq, proj, v_first, sinks, n_swa, n_kv)

    mixcat = jnp.concatenate([fox_o, swa_o], axis=1).astype(BF16)
    w_out_f = gathered("w_out", mixcat).reshape(D, D)
    mix = _mm_plain("out_proj", mixcat, w_out_f, "nn", F32)
    x1, h2 = _post_mix(x2, mix, g_post_mix, gt_a, g_pre_mlp, sc_m, sh_m)
    w_up_f = jnp.transpose(gathered("w_up", h2), (1, 0, 2)).reshape(D, d_ff)

    tm_u, tn_u = _fit(MM_TM, S), _fit(MM_TN, d_ff)

    def up_epilogue(acc, ex, outs):
        outs[0][...] = acc.astype(BF16)
        r = jnp.maximum(acc, 0.0)
        outs[1][...] = (r * r).astype(BF16)

    ublk = ((S, d_ff), BF16, (tm_u, tn_u), lambda i, j: (i, j))
    u, a = _matmul("mlp_up", h2, w_up_f, "nn", [ublk, ublk], up_epilogue)
    w_down_f = gathered("w_down", a).reshape(d_ff, D)
    y = _mm_plain("mlp_down", a, w_down_f, "nn", F32)

    dy, dout, loss_part, acc_mlp_post = _loss_and_post_mlp_bwd(x1, y, tgt, g_post_mlp, gt_m)
    loss = lax.psum(loss_part[0, 0], ("x", "y", "c"))

    def du_epilogue(acc, ex, outs):
        outs[0][...] = (acc * (2.0 * jnp.maximum(ex[0][...].astype(F32), 0.0))).astype(BF16)

    du = _matmul("mlp_down_bwd", dy, w_down_f, "nt", [ublk], du_epilogue,
                 extras=[(u, (tm_u, tn_u), lambda i, j: (i, j))])[0]
    g_down = _mm_plain("grad_w_down", a, dy, "tn", BF16)
    tn_s = _fit(MM_TN, w_up.shape[2])
    per = w_up.shape[2] // tn_s

    def shard_epilogue(acc, ex, outs):
        outs[0][0] = acc.astype(BF16)

    g_up = _matmul("grad_w_up", h2, du, "tn",
                   [((N_CHIPS, D, w_up.shape[2]), BF16, (1, _fit(MM_TM, D), tn_s), lambda i, j: (j // per, i, j % per))],
                   shard_epilogue, tn=tn_s)[0]

    def reduce_start(tag, fulls):
        from_sibling = _pair_exchange("grad_pair_exchange_" + tag, fulls)
        sums = [_pair_add("pair_add_%s_%d" % (tag, k), core_arr, g, r) for k, (g, r) in enumerate(zip(fulls, from_sibling))]
        return _ici_start("grad_scatter_start_" + tag, sums,
                          [jax.ShapeDtypeStruct((3,) + p.shape[1:], BF16) for p in sums], _scatter_plan)

    def reduce_finish(tag, flight, after):
        send, recv, srcs, lands, _ = flight
        sums, received = _ici_wait("grad_scatter_wait_" + tag, send, recv, srcs, lands, _scatter_plan, after)
        halves = [_chip_add("chip_add_%s_%d" % (tag, k), chip_arr, p, r) for k, (p, r) in enumerate(zip(sums, received))]
        return halves, _pair_share("grad_pair_share_" + tag, halves)

    flight_mlp = reduce_start("mlp", [g_up, g_down.reshape(N_CHIPS, d_ff // N_CHIPS, D)])
    dh2 = _mm_plain("mlp_up_bwd", du, w_up_f, "nt", F32)
    dx1, dmix, acc_mid = _pre_mlp_and_post_mix_bwd(dh2, x1, dout, mix, _tie(g_pre_mlp, flight_mlp[4]), sc_m,
                                                   g_post_mix, gt_a)

    dmixcat = _mm_plain("out_proj_bwd", dmix, w_out_f, "nt", F32)
    g_out = _mm_plain("grad_w_out", mixcat, dmix, "tn", BF16)

    fdq, fdk, fdv, dcum_row, dcum_q = _fox_bwd(proj, fox_o, dmixcat, fox_lse, cum_b, cum_row, n_fox)
    dcum_k = jnp.pad(dcum_row.reshape(n_fox, S), ((0, LANES - n_fox), (0, 0)))
    dfg, db_forget = _fox_gate_bwd(dcum_k, dcum_q, fg, b_pad)

    group_w = (n_swa // n_kv) * HEAD_DIM
    sdq, sdk, sdv, dsink = _swa_bwd(rq, proj, v_first, sinks, swa_o, dmixcat, fox_w // group_w, swa_lse, n_swa, n_kv)
    drq = jnp.concatenate([sdq, jnp.transpose(sdk, (1, 0, 2)).reshape(S, kv_w).astype(BF16)], axis=1)
    d_sq_sk = _rope("rope_bwd", drq, 0, n_swa + n_kv, cos, -sin_signed)
    dsv = jnp.transpose(sdv, (1, 0, 2)).reshape(S, kv_w).astype(BF16)
    dproj = jnp.concatenate([fdq, fdk, fdv, d_sq_sk, dsv], axis=1)

    g_main = _mm_plain("grad_w_in", h, dproj, "tn", BF16, tn=_fit(768, main_w))
    g_fg = _mm_plain("grad_w_in_gate", h, dfg, "tn", BF16)
    dh_gate = _mm_plain("in_proj_gate_bwd", dfg, w_fg, "nt", F32)

    def add_epilogue(acc, ex, outs):
        outs[0][...] = acc + ex[0][...]

    tm_h, tn_h = _fit(MM_TM, S), _fit(MM_TN, D)
    dh = _matmul("in_proj_bwd", dproj, w_main, "nt", [((S, D), F32, (tm_h, tn_h), lambda i, j: (i, j))], add_epilogue,
                 extras=[(dh_gate, (tm_h, tn_h), lambda i, j: (i, j))], tk=_fit(2304, main_w))[0]
    grad_x, acc_pre = _pre_mix_bwd(dh, x2, dx1, g_pre_mix, sc_a)

    zero_row = jnp.zeros((1, D), F32)
    tail = jnp.concatenate([db_forget[0:1, :n_fox], dsink[:, 0, :n_swa // n_kv].reshape(1, n_swa),
                            jnp.zeros((1, D - n_fox - n_swa), F32)], axis=1)
    partial = jnp.concatenate([
        acc_pre[0:1], acc_pre[1:2], acc_mid[3:4], acc_mid[0:1], acc_mid[1:2], acc_mlp_post[0:1],
        acc_pre[2:3], acc_mid[4:5], acc_mid[2:3], acc_mlp_post[1:2], tail] + [zero_row] * 5, axis=0)
    gathered_small, token = _allgather8("gather_small_grads", partial)

    g_fg_tied = _tie(g_fg[:, :n_fox].astype(F32), token).astype(BF16)
    g_in_f = jnp.concatenate([g_main[:, :3 * fox_w], g_fg_tied, g_main[:, 3 * fox_w:]], axis=1)
    flight_mix = reduce_start("mix", [jnp.transpose(g_in_f.reshape(D, N_CHIPS, in_w // N_CHIPS), (1, 0, 2)),
                                      g_out.reshape(N_CHIPS, D // N_CHIPS, D)])
    halves_mlp, others_mlp = reduce_finish("mlp", flight_mlp, flight_mix[4])

    def pack(bm, gpm, gqm, gpl, gql, bf, sk):
        last = jnp.concatenate([bf, sk, jnp.zeros((1, D - n_fox - n_swa), F32)], axis=1)
        return jnp.concatenate([bm.reshape(N_MOD, D), gpm, gqm, gpl, gql, last, jnp.zeros((5, D), F32)], axis=0)

    def unpack(p):
        return {"b_mod": p[0:N_MOD].reshape(1, N_MOD * D), "g_pre_mix": p[6:7], "g_post_mix": p[7:8],
                "g_pre_mlp": p[8:9], "g_post_mlp": p[9:10], "b_forget": p[10:11, :n_fox],
                "swa_sinks": p[10:11, n_fox:n_fox + n_swa]}

    small_out = _small_update(
        gathered_small, pack(b_mod, g_pre_mix, g_post_mix, g_pre_mlp, g_post_mlp, b_forget, swa_sinks),
        pack(m_b_mod, m_g_pre_mix, m_g_post_mix, m_g_pre_mlp, m_g_post_mlp, m_b_forget, m_swa_sinks),
        pack(v_b_mod, v_g_pre_mix, v_g_post_mix, v_g_pre_mlp, v_g_post_mlp, v_b_forget, v_swa_sinks))
    g_small, d_small, m_small, v_small = [unpack(p) for p in small_out]

    dmod_all = gathered_small.reshape(N_DEV, 16, D)[:, :N_MOD].reshape(N_DEV, N_MOD * D)
    dmod_shard = lax.dynamic_slice_in_dim(dmod_all, chip * mod_cols, mod_cols, axis=1)
    g_w_mod, d_w_mod, nm_w_mod, nv_w_mod = _mod_update(c_all.T, dmod_shard, w_mod[0], m_w_mod[0], v_w_mod[0])

    grads = dict(g_small, w_mod=g_w_mod[None])
    deltas = dict(d_small, w_mod=d_w_mod[None])
    new_m = dict(m_small, w_mod=nm_w_mod[None])
    new_v = dict(v_small, w_mod=nv_w_mod[None])
    weights = {"w_in": (w_in, m_w_in, v_w_in), "w_out": (w_out, m_w_out, v_w_out), "w_up": (w_up, m_w_up, v_w_up),
               "w_down": (w_down, m_w_down, v_w_down)}

    def big_update(n, own, other):
        w, m, v = weights[n]
        g, d_, m_, v_ = _adam_halves("adam_" + n, core_arr, w[0], own, other, m[0], v[0])
        grads[n], deltas[n], new_m[n], new_v[n] = g[None], d_[None], m_[None], v_[None]

    big_update("w_up", halves_mlp[0], others_mlp[0])
    big_update("w_down", halves_mlp[1], others_mlp[1])
    ran = deltas["w_down"][0, :8, :LANES] + deltas["w_up"][0, :8, :LANES] + d_w_mod[:8, :LANES]
    halves_mix, others_mix = reduce_finish("mix", flight_mix, ran)
    big_update("w_in", halves_mix[0], others_mix[0])
    big_update("w_out", halves_mix[1], others_mix[1])

    order = ["w_mod", "b_mod", "g_pre_mix", "g_post_mix", "w_in", "b_forget", "swa_sinks", "w_out", "g_pre_mlp",
             "g_post_mlp", "w_up", "w_down"]
    return (loss, grad_x[None], *[grads[n] for n in order], *[deltas[n] for n in order],
            *[new_m[n] for n in order], *[new_v[n] for n in order])
```

```python
import jax
import jax.numpy as jnp
from jax import lax
from jax.experimental import pallas as pl
from jax.experimental.pallas import tpu as pltpu

F32 = jnp.float32
BF16 = jnp.bfloat16
MESH = pl.DeviceIdType.MESH

HEAD_DIM = 128
SWA_BLOCK = 128
ROPE_THETA = 10000.0
NORM_EPS = 1e-6
NEG = -1e30
N_MOD = 6
ADAM_LR = 0.001
ADAM_B1 = 0.9
ADAM_B2 = 0.999
ADAM_EPS = 1e-08
ADAM_WD = 0.01
ADAM_STEP = 10
N_CHIPS = 4
N_DEV = 8
LANES = 128
VMEM_CAP = 60 * 1024 * 1024

_NN = (((1,), (0,)), ((), ()))
_NT = (((1,), (1,)), ((), ()))
_TN = (((0,), (0,)), ((), ()))


def _vmem(nbytes):
    return int(min(VMEM_CAP, nbytes * 5 // 4 + (4 << 20)))


def _nbytes(shape, dtype):
    n = 1
    for s in shape:
        n *= s
    return n * jnp.dtype(dtype).itemsize


def _fit(t, n):
    t = min(t, n)
    assert n % t == 0, (t, n)
    return t


MM_TM, MM_TN, MM_TK = 512, 1024, 2048


def _matmul(name, a, b, mode, out_defs, epilogue, extras=(), tm=MM_TM, tn=MM_TN, tk=MM_TK):
    if mode == "nn":
        (M, K), (K2, N) = a.shape, b.shape
    elif mode == "nt":
        (M, K), (N, K2) = a.shape, b.shape
    else:
        (K, M), (K2, N) = a.shape, b.shape
    assert K == K2, (a.shape, b.shape, mode)
    tm, tn, tk = _fit(tm, M), _fit(tn, N), _fit(tk, K)
    nk = K // tk
    dims = {"nn": _NN, "nt": _NT, "tn": _TN}[mode]
    a_spec = (pl.BlockSpec((tk, tm), lambda i, j, k: (k, i)) if mode == "tn"
              else pl.BlockSpec((tm, tk), lambda i, j, k: (i, k)))
    b_spec = (pl.BlockSpec((tn, tk), lambda i, j, k: (j, k)) if mode == "nt"
              else pl.BlockSpec((tk, tn), lambda i, j, k: (k, j)))
    n_ex, n_out = len(extras), len(out_defs)

    def body(*refs):
        a_ref, b_ref = refs[0], refs[1]
        ex = refs[2:2 + n_ex]
        outs = refs[2 + n_ex:2 + n_ex + n_out]
        prod = lax.dot_general(a_ref[...], b_ref[...], dims, preferred_element_type=F32)
        if nk == 1:
            epilogue(prod, ex, outs)
        else:
            acc_ref = refs[-1]
            k = pl.program_id(2)

            @pl.when(k == 0)
            def _():
                acc_ref[...] = prod

            @pl.when(k > 0)
            def _():
                acc_ref[...] += prod

            @pl.when(k == nk - 1)
            def _():
                epilogue(acc_ref[...], ex, outs)

    def wrap(f):
        return lambda i, j, k: f(i, j)

    in_specs = [a_spec, b_spec] + [pl.BlockSpec(blk, wrap(f)) for _, blk, f in extras]
    out_specs = [pl.BlockSpec(blk, wrap(f)) for _, _, blk, f in out_defs]
    out_shape = [jax.ShapeDtypeStruct(s, d) for s, d, _, _ in out_defs]
    need = 2 * (tm * tk + tk * tn) * a.dtype.itemsize + 3 * tm * tn * 4
    need += sum(2 * _nbytes(blk, arr.dtype) for arr, blk, _ in extras)
    need += sum(2 * _nbytes(blk, d) for _, d, blk, _ in out_defs)
    res = pl.pallas_call(
        body, name=name, grid=(M // tm, N // tn, nk),
        in_specs=in_specs, out_specs=out_specs, out_shape=out_shape,
        scratch_shapes=[pltpu.VMEM((tm, tn), F32)] if nk > 1 else [],
        compiler_params=pltpu.CompilerParams(
            dimension_semantics=("parallel", "parallel", "arbitrary"), vmem_limit_bytes=_vmem(need)),
    )(a, b, *[arr for arr, _, _ in extras])
    return res


def _mm_plain(name, a, b, mode, out_dtype, **tiles):
    if mode == "nn":
        M, N = a.shape[0], b.shape[1]
    elif mode == "nt":
        M, N = a.shape[0], b.shape[0]
    else:
        M, N = a.shape[1], b.shape[1]
    tm, tn = _fit(tiles.get("tm", MM_TM), M), _fit(tiles.get("tn", MM_TN), N)

    def epi(acc, ex, outs):
        outs[0][...] = acc.astype(out_dtype)

    return _matmul(name, a, b, mode, [((M, N), out_dtype, (tm, tn), lambda i, j: (i, j))], epi, **tiles)[0]


def _rstd(v):
    return lax.rsqrt(jnp.mean(v * v, axis=-1, keepdims=True) + NORM_EPS)


def _row_call(name, body, row_ins, vec_ins, row_outs, acc_outs, S, D, tr):
    tr = _fit(tr, S)
    row_spec = pl.BlockSpec((tr, D), lambda r: (r, 0))
    vec_spec = pl.BlockSpec((1, D), lambda r: (0, 0))
    in_specs = [row_spec] * len(row_ins) + [vec_spec] * len(vec_ins)
    out_specs = [row_spec] * len(row_outs) + [pl.BlockSpec(shp, lambda r: (0, 0)) for shp in acc_outs]
    out_shape = [jax.ShapeDtypeStruct((S, D), d) for d in row_outs] + [jax.ShapeDtypeStruct(shp, F32) for shp in acc_outs]
    need = sum(2 * tr * D * a.dtype.itemsize for a in row_ins) + sum(2 * tr * D * jnp.dtype(d).itemsize for d in row_outs)
    need += 8 * tr * D * 4
    return pl.pallas_call(
        body, name=name, grid=(S // tr,), in_specs=in_specs, out_specs=out_specs, out_shape=out_shape,
        compiler_params=pltpu.CompilerParams(dimension_semantics=("arbitrary",), vmem_limit_bytes=_vmem(need)),
    )(*row_ins, *vec_ins)


def _acc_rows(ref, rows):
    @pl.when(pl.program_id(0) == 0)
    def _():
        ref[...] = jnp.zeros_like(ref)
    for n, r in enumerate(rows):
        ref[n:n + 1, :] += r


def _pre_norm(x, g, sc, sh):
    S, D = x.shape

    def body(x_ref, g_ref, sc_ref, sh_ref, h_ref):
        xv = x_ref[...]
        xn = xv * _rstd(xv)
        h_ref[...] = (xn * g_ref[...] * (1.0 + sc_ref[...]) + sh_ref[...]).astype(BF16)

    return _row_call("pre_norm_mix", body, [x], [g, sc, sh], [BF16], [], S, D, 256)[0]


def _post_mix(x, mix, g_post, gt, g_pre, sc, sh):
    S, D = x.shape

    def body(x_ref, mix_ref, gp_ref, gt_ref, g2_ref, sc_ref, sh_ref, x1_ref, h2_ref):
        mv = mix_ref[...]
        x1 = x_ref[...] + gt_ref[...] * (mv * _rstd(mv) * gp_ref[...])
        x1_ref[...] = x1
        h2_ref[...] = (x1 * _rstd(x1) * g2_ref[...] * (1.0 + sc_ref[...]) + sh_ref[...]).astype(BF16)

    return _row_call("post_mix_pre_mlp", body, [x, mix], [g_post, gt, g_pre, sc, sh], [F32, BF16], [], S, D, 256)


def _loss_and_post_mlp_bwd(x1, y, target, g_post, gt):
    S, D = x1.shape

    def body(x1_ref, y_ref, t_ref, g_ref, gt_ref, dy_ref, dout_ref, loss_ref, acc_ref):
        yv = y_ref[...]
        r = _rstd(yv)
        yh = yv * r
        n = yh * g_ref[...]
        diff = x1_ref[...] + gt_ref[...] * n - t_ref[...]
        dout = diff * (1.0 / D)
        dout_ref[...] = dout
        dn = dout * gt_ref[...]
        dyh = dn * g_ref[...]
        dy_ref[...] = (r * (dyh - yh * jnp.mean(dyh * yh, axis=-1, keepdims=True))).astype(BF16)
        _acc_rows(acc_ref, [jnp.sum(dout * n, axis=0, keepdims=True), jnp.sum(dn * yh, axis=0, keepdims=True)])

        @pl.when(pl.program_id(0) == 0)
        def _():
            loss_ref[...] = jnp.zeros_like(loss_ref)
        loss_ref[...] += jnp.full(loss_ref.shape, (0.5 / D) * jnp.sum(diff * diff), F32)

    return _row_call("loss_post_mlp_bwd", body, [x1, y, target], [g_post, gt], [BF16, F32],
                     [(8, LANES), (8, D)], S, D, 128)


def _pre_mlp_and_post_mix_bwd(dh2, x1, dout, mix, g_pre, sc, g_post, gt):
    S, D = x1.shape

    def body(dh_ref, x1_ref, dout_ref, mix_ref, g_ref, sc_ref, gp_ref, gt_ref, dx1_ref, dmix_ref, acc_ref):
        dh = dh_ref[...]
        x1v = x1_ref[...]
        r3 = _rstd(x1v)
        xn = x1v * r3
        dxn = dh * (1.0 + sc_ref[...]) * g_ref[...]
        dx1 = dout_ref[...] + r3 * (dxn - xn * jnp.mean(dxn * xn, axis=-1, keepdims=True))
        dx1_ref[...] = dx1
        mv = mix_ref[...]
        r2 = _rstd(mv)
        mh = mv * r2
        dn = dx1 * gt_ref[...]
        dmh = dn * gp_ref[...]
        dmix_ref[...] = (r2 * (dmh - mh * jnp.mean(dmh * mh, axis=-1, keepdims=True))).astype(BF16)
        _acc_rows(acc_ref, [
            jnp.sum(dh, axis=0, keepdims=True),
            jnp.sum(dh * xn * g_ref[...], axis=0, keepdims=True),
            jnp.sum(dh * (1.0 + sc_ref[...]) * xn, axis=0, keepdims=True),
            jnp.sum(dx1 * mh * gp_ref[...], axis=0, keepdims=True),
            jnp.sum(dn * mh, axis=0, keepdims=True)])

    return _row_call("pre_mlp_post_mix_bwd", body, [dh2, x1, dout, mix], [g_pre, sc, g_post, gt], [F32, BF16],
                     [(8, D)], S, D, 128)


def _pre_mix_bwd(dh, x, dx1, g_pre, sc):
    S, D = x.shape

    def body(dh_ref, x_ref, dx1_ref, g_ref, sc_ref, gx_ref, acc_ref):
        dhv = dh_ref[...]
        xv = x_ref[...]
        r = _rstd(xv)
        xn = xv * r
        dxn = dhv * (1.0 + sc_ref[...]) * g_ref[...]
        gx_ref[...] = dx1_ref[...] + r * (dxn - xn * jnp.mean(dxn * xn, axis=-1, keepdims=True))
        _acc_rows(acc_ref, [
            jnp.sum(dhv, axis=0, keepdims=True),
            jnp.sum(dhv * xn * g_ref[...], axis=0, keepdims=True),
            jnp.sum(dhv * (1.0 + sc_ref[...]) * xn, axis=0, keepdims=True)])

    return _row_call("pre_mix_bwd", body, [dh, x, dx1], [g_pre, sc], [F32], [(8, D)], S, D, 128)


CUM_BLOCK = 256


def _tri(n, upper):
    r = lax.broadcasted_iota(jnp.int32, (n, n), 0)
    c = lax.broadcasted_iota(jnp.int32, (n, n), 1)
    return ((c >= r) if upper else (c <= r)).astype(F32)


def _fox_gate_fwd(fg, b_pad):
    S = fg.shape[0]
    cb = _fit(CUM_BLOCK, S)

    def body(fg_ref, b_ref, cumt_ref, cum_ref):
        low = _tri(cb, False)
        carry = jnp.zeros((1, LANES), F32)
        for n in range(S // cb):
            z = fg_ref[n * cb:(n + 1) * cb, :] + b_ref[...]
            logf = jnp.minimum(z, 0.0) - jnp.log(1.0 + jnp.exp(-jnp.abs(z)))
            blk = jnp.dot(low, logf, precision=lax.Precision.HIGHEST, preferred_element_type=F32) + carry
            cum_ref[n * cb:(n + 1) * cb, :] = blk
            carry = blk[cb - 1:cb, :]
        cumt_ref[...] = cum_ref[...].T

    return pl.pallas_call(
        body, name="fox_gate_fwd", out_shape=jax.ShapeDtypeStruct((LANES, S), F32),
        scratch_shapes=[pltpu.VMEM((S, LANES), F32)],
        compiler_params=pltpu.CompilerParams(vmem_limit_bytes=_vmem(6 * S * LANES * 4)),
    )(fg, b_pad)


def _fox_gate_bwd(dcum_k, dcum_q, fg, b_pad):
    S = fg.shape[0]
    n_fox = dcum_q.shape[0]
    cb = _fit(CUM_BLOCK, S)

    def body(dk_ref, dq_ref, fg_ref, b_ref, dfg_ref, db_ref, dc_ref):
        lane = lax.broadcasted_iota(jnp.int32, (S, LANES), 1)
        dc = dk_ref[...].T
        for h in range(n_fox):
            dc = dc + jnp.where(lane == h, dq_ref[h], 0.0)
        dc_ref[...] = dc
        up = _tri(cb, True)
        carry = jnp.zeros((1, LANES), F32)
        db = jnp.zeros((1, LANES), F32)
        for n in reversed(range(S // cb)):
            blk = jnp.dot(up, dc_ref[n * cb:(n + 1) * cb, :], precision=lax.Precision.HIGHEST,
                          preferred_element_type=F32) + carry
            carry = blk[0:1, :]
            z = fg_ref[n * cb:(n + 1) * cb, :] + b_ref[...]
            dfg = blk * (1.0 / (1.0 + jnp.exp(z)))
            dfg_ref[n * cb:(n + 1) * cb, :] = dfg.astype(BF16)
            db = db + jnp.sum(dfg, axis=0, keepdims=True)
        db_ref[...] = jnp.broadcast_to(db, db_ref.shape)

    return pl.pallas_call(
        body, name="fox_gate_bwd",
        out_shape=[jax.ShapeDtypeStruct((S, LANES), BF16), jax.ShapeDtypeStruct((8, LANES), F32)],
        scratch_shapes=[pltpu.VMEM((S, LANES), F32)],
        compiler_params=pltpu.CompilerParams(vmem_limit_bytes=_vmem((8 + 2 * n_fox) * S * LANES * 4)),
    )(dcum_k, dcum_q, fg, b_pad)


FOX_TILE = 512


LOG2E = 1.4426950408889634


def _fox_scores(q, k, ck2, masked, t):
    s = lax.dot_general(q, k, _NT, preferred_element_type=F32) * (HEAD_DIM ** -0.5 * LOG2E) - ck2
    if masked:
        row = lax.broadcasted_iota(jnp.int32, (t, t), 0)
        col = lax.broadcasted_iota(jnp.int32, (t, t), 1)
        s = jnp.where(col <= row, s, NEG)
    return s


def _fox_fwd(proj, cum_row, n_fox):
    S = proj.shape[0]
    t = _fit(FOX_TILE, S)
    nq = S // t

    def body(q_ref, k_ref, v_ref, ck_ref, o_ref, lse_ref):
        def q_block(qi, _):
            q0 = pl.multiple_of(qi * t, t)
            q = q_ref[pl.ds(q0, t), :]

            def kv_block(j, carry, masked):
                m, l, acc = carry
                k0 = pl.multiple_of(j * t, t)
                s = _fox_scores(q, k_ref[pl.ds(k0, t), :], ck_ref[0, :, pl.ds(k0, t)] * LOG2E, masked, t)
                m_new = jnp.maximum(m, jnp.max(s, axis=-1, keepdims=True))
                alpha = jnp.exp2(m - m_new)
                p = jnp.exp2(s - m_new)
                l = alpha * l + jnp.sum(p, axis=-1, keepdims=True)
                acc = alpha * acc + jnp.dot(p.astype(BF16), v_ref[pl.ds(k0, t), :], preferred_element_type=F32)
                return m_new, l, acc

            init = (jnp.full((t, 1), NEG, F32), jnp.zeros((t, 1), F32), jnp.zeros((t, HEAD_DIM), F32))
            carry = lax.fori_loop(0, qi, lambda j, cr: kv_block(j, cr, False), init)
            m, l, acc = kv_block(qi, carry, True)
            o_ref[pl.ds(q0, t), :] = acc / l
            lse_ref[0, pl.ds(q0, t), :] = jnp.broadcast_to(m + jnp.log(l) * LOG2E, (t, LANES))
            return 0

        lax.fori_loop(0, nq, q_block, 0)

    col = lambda off: pl.BlockSpec((S, HEAD_DIM), lambda h: (0, off + h))
    per_head = pl.BlockSpec((1, S, LANES), lambda h: (h, 0, 0))
    return pl.pallas_call(
        body, name="fox_fwd", grid=(n_fox,),
        in_specs=[col(0), col(n_fox), col(2 * n_fox), pl.BlockSpec((1, 1, S), lambda h: (h, 0, 0))],
        out_specs=[pl.BlockSpec((S, HEAD_DIM), lambda h: (0, h)), per_head],
        out_shape=[jax.ShapeDtypeStruct((S, n_fox * HEAD_DIM), F32), jax.ShapeDtypeStruct((n_fox, S, LANES), F32)],
        compiler_params=pltpu.CompilerParams(dimension_semantics=("parallel",),
                                             vmem_limit_bytes=_vmem(16 * S * HEAD_DIM * 4 + 12 * t * t * 4)),
    )(proj, proj, proj, cum_row)


def _fox_bwd(proj, o, do, lse_b, cum_row, n_fox):
    S = proj.shape[0]
    t = _fit(FOX_TILE, S)
    nq = S // t
    scale = HEAD_DIM ** -0.5

    def body(q_ref, k_ref, v_ref, o_ref, do_ref, lse_ref, ck_ref, dq_ref, dk_ref, dv_ref, dc_ref, dcq_ref,
             dq_acc, delta_ref):
        dq_acc[...] = jnp.zeros_like(dq_acc)
        dcq_ref[...] = jnp.zeros_like(dcq_ref)

        def delta_block(qi, _):
            q0 = pl.multiple_of(qi * t, t)
            d = jnp.sum(do_ref[pl.ds(q0, t), :] * o_ref[pl.ds(q0, t), :], axis=-1, keepdims=True)
            delta_ref[pl.ds(q0, t), :] = jnp.broadcast_to(d, (t, LANES))
            return 0

        lax.fori_loop(0, nq, delta_block, 0)

        def kv_block(j, _):
            k0 = pl.multiple_of(j * t, t)
            k = k_ref[pl.ds(k0, t), :]
            v = v_ref[pl.ds(k0, t), :]
            ck2 = ck_ref[0, :, pl.ds(k0, t)] * LOG2E

            def q_block(qi, carry, masked):
                dk, dv, dc = carry
                q0 = pl.multiple_of(qi * t, t)
                q = q_ref[pl.ds(q0, t), :]
                dov = do_ref[pl.ds(q0, t), :].astype(BF16)
                p = jnp.exp2(_fox_scores(q, k, ck2, masked, t) - lse_ref[0, pl.ds(q0, t), :][:, :1])
                dp = lax.dot_general(dov, v, _NT, preferred_element_type=F32)
                ds = p * (dp - delta_ref[pl.ds(q0, t), :][:, :1])
                dsb = ds.astype(BF16)
                dv = dv + lax.dot_general(p.astype(BF16), dov, _TN, preferred_element_type=F32)
                dk = dk + lax.dot_general(dsb, q, _TN, preferred_element_type=F32)
                dq_acc[pl.ds(q0, t), :] += jnp.dot(dsb, k, preferred_element_type=F32)
                dc = dc - jnp.sum(ds, axis=0, keepdims=True)
                dcq_ref[0, pl.ds(q0, t), :] += jnp.broadcast_to(jnp.sum(ds, axis=1, keepdims=True), (t, LANES))
                return dk, dv, dc

            init = (jnp.zeros((t, HEAD_DIM), F32), jnp.zeros((t, HEAD_DIM), F32), jnp.zeros((1, t), F32))
            carry = q_block(j, init, True)
            dk, dv, dc = lax.fori_loop(j + 1, nq, lambda qi, cr: q_block(qi, cr, False), carry)
            dk_ref[pl.ds(k0, t), :] = (dk * scale).astype(BF16)
            dv_ref[pl.ds(k0, t), :] = dv.astype(BF16)
            dc_ref[0, :, pl.ds(k0, t)] = dc
            return 0

        lax.fori_loop(0, nq, kv_block, 0)
        dq_ref[...] = (dq_acc[...] * scale).astype(BF16)

    col = lambda off: pl.BlockSpec((S, HEAD_DIM), lambda h: (0, off + h))
    per_head = pl.BlockSpec((1, S, LANES), lambda h: (h, 0, 0))
    row = pl.BlockSpec((1, 1, S), lambda h: (h, 0, 0))
    grad = jax.ShapeDtypeStruct((S, n_fox * HEAD_DIM), BF16)
    return pl.pallas_call(
        body, name="fox_bwd", grid=(n_fox,),
        in_specs=[col(0), col(n_fox), col(2 * n_fox), col(0), col(0), per_head, row],
        out_specs=[col(0), col(0), col(0), row, per_head],
        out_shape=[grad, grad, grad, jax.ShapeDtypeStruct((n_fox, 1, S), F32), jax.ShapeDtypeStruct((n_fox, S, LANES), F32)],
        scratch_shapes=[pltpu.VMEM((S, HEAD_DIM), F32), pltpu.VMEM((S, LANES), F32)],
        compiler_params=pltpu.CompilerParams(dimension_semantics=("parallel",),
                                             vmem_limit_bytes=_vmem(24 * S * HEAD_DIM * 4 + 16 * t * t * 4)),
    )(proj, proj, proj, o, do, lse_b, cum_row)


def _rope_tables(S):
    half = HEAD_DIM // 2
    inv_freq = 1.0 / (ROPE_THETA ** (jnp.arange(half, dtype=F32) * (2.0 / HEAD_DIM)))
    ang = jnp.arange(S).astype(F32)[:, None] * inv_freq[None, :]
    cos, sin = jnp.cos(ang), jnp.sin(ang)
    return jnp.concatenate([cos, cos], axis=-1), jnp.concatenate([-sin, sin], axis=-1)


def _rope(name, src, first_block, n_blocks, cos, sin_signed):
    S = src.shape[0]

    def body(x_ref, cos_ref, sin_ref, o_ref):
        xv = x_ref[...].astype(F32)
        o_ref[...] = (xv * cos_ref[...] + pltpu.roll(xv, HEAD_DIM // 2, 1) * sin_ref[...]).astype(BF16)

    table = pl.BlockSpec((S, HEAD_DIM), lambda n: (0, 0))
    return pl.pallas_call(
        body, name=name, grid=(n_blocks,),
        in_specs=[pl.BlockSpec((S, HEAD_DIM), lambda n: (0, first_block + n)), table, table],
        out_specs=pl.BlockSpec((S, HEAD_DIM), lambda n: (0, n)),
        out_shape=jax.ShapeDtypeStruct((S, n_blocks * HEAD_DIM), BF16),
        compiler_params=pltpu.CompilerParams(dimension_semantics=("parallel",),
                                             vmem_limit_bytes=_vmem(12 * S * HEAD_DIM * 4)),
    )(src, cos, sin_signed)


def _swa_tile(q_ref, kp_ref, kc_ref, n, group, scale):
    B = SWA_BLOCK
    qs = jnp.concatenate([q_ref[:, g * HEAD_DIM:(g + 1) * HEAD_DIM] for g in range(group)], axis=0)
    kcat = jnp.concatenate([kp_ref[...], kc_ref[...]], axis=0)
    s = lax.dot_general(qs, kcat, _NT, preferred_element_type=F32) * scale
    qi = lax.broadcasted_iota(jnp.int32, (group * B, 2 * B), 0) % B
    kj = lax.broadcasted_iota(jnp.int32, (group * B, 2 * B), 1)
    diff = qi + B - kj
    mask = (diff >= 0) & (diff < B) & ((n * B + kj - B) >= 0)
    return qs, kcat, jnp.where(mask, s, NEG)


def _swa_sink_col(sink_ref, kv, group):
    head = lax.broadcasted_iota(jnp.int32, (group * SWA_BLOCK, 1), 0) // SWA_BLOCK
    col = jnp.zeros((group * SWA_BLOCK, 1), F32)
    for g in range(group):
        col = jnp.where(head == g, sink_ref[kv * group + g], col)
    return col


def _swa_specs(n_kv, group, q_first, k_first, v_first):
    B = SWA_BLOCK
    prev = lambda n: jnp.maximum(n - 1, 0)
    return [
        pl.BlockSpec((B, group * HEAD_DIM), lambda kv, n: (n, q_first + kv)),
        pl.BlockSpec((B, HEAD_DIM), lambda kv, n: (prev(n), k_first + kv)),
        pl.BlockSpec((B, HEAD_DIM), lambda kv, n: (n, k_first + kv)),
        pl.BlockSpec((B, HEAD_DIM), lambda kv, n: (prev(n), v_first + kv)),
        pl.BlockSpec((B, HEAD_DIM), lambda kv, n: (n, v_first + kv)),
    ]


def _swa_fwd(rq, proj, v_first, sinks, n_q, n_kv):
    S = rq.shape[0]
    B = SWA_BLOCK
    group = n_q // n_kv
    scale = HEAD_DIM ** -0.5

    def body(q_ref, kp_ref, kc_ref, vp_ref, vc_ref, sink_ref, o_ref, lse_ref):
        kv, n = pl.program_id(0), pl.program_id(1)
        _, _, s = _swa_tile(q_ref, kp_ref, kc_ref, n, group, scale)
        sink = _swa_sink_col(sink_ref, kv, group)
        m = jnp.maximum(jnp.max(s, axis=-1, keepdims=True), sink)
        p = jnp.exp(s - m)
        denom = jnp.sum(p, axis=-1, keepdims=True) + jnp.exp(sink - m)
        vcat = jnp.concatenate([vp_ref[...], vc_ref[...]], axis=0)
        o = jnp.dot((p / denom).astype(BF16), vcat, preferred_element_type=F32)
        lse = m + jnp.log(denom)
        for g in range(group):
            o_ref[:, g * HEAD_DIM:(g + 1) * HEAD_DIM] = o[g * B:(g + 1) * B, :]
            lse_ref[0, :, g * LANES:(g + 1) * LANES] = jnp.broadcast_to(lse[g * B:(g + 1) * B, :], (B, LANES))

    specs = _swa_specs(n_kv, group, 0, n_q, v_first)
    q_blk = pl.BlockSpec((B, group * HEAD_DIM), lambda kv, n: (n, kv))
    return pl.pallas_call(
        body, name="swa_fwd", grid=(n_kv, S // B),
        in_specs=specs + [pl.BlockSpec(memory_space=pltpu.SMEM)],
        out_specs=[q_blk, pl.BlockSpec((1, B, group * LANES), lambda kv, n: (kv, n, 0))],
        out_shape=[jax.ShapeDtypeStruct((S, n_q * HEAD_DIM), F32), jax.ShapeDtypeStruct((n_kv, S, group * LANES), F32)],
        compiler_params=pltpu.CompilerParams(dimension_semantics=("parallel", "arbitrary")),
    )(rq, rq, rq, proj, proj, sinks)


def _swa_bwd(rq, proj, v_first, sinks, o, do, do_first, lse_b, n_q, n_kv):
    S = rq.shape[0]
    B = SWA_BLOCK
    group = n_q // n_kv
    scale = HEAD_DIM ** -0.5

    def body(q_ref, kp_ref, kc_ref, vp_ref, vc_ref, o_ref, do_ref, lse_ref, sink_ref,
             dq_ref, dk_ref, dv_ref, dsink_ref):
        kv, n = pl.program_id(0), pl.program_id(1)

        @pl.when(n == 0)
        def _():
            dk_ref[...] = jnp.zeros_like(dk_ref)
            dv_ref[...] = jnp.zeros_like(dv_ref)
            dsink_ref[...] = jnp.zeros_like(dsink_ref)

        qs, kcat, s = _swa_tile(q_ref, kp_ref, kc_ref, n, group, scale)
        sink = _swa_sink_col(sink_ref, kv, group)
        stack = lambda ref, w: jnp.concatenate([ref[:, g * w:(g + 1) * w] for g in range(group)], axis=0)
        lse = jnp.concatenate([lse_ref[0, :, g * LANES:g * LANES + 1] for g in range(group)], axis=0)
        do32 = stack(do_ref, HEAD_DIM)
        delta = jnp.sum(do32 * stack(o_ref, HEAD_DIM), axis=-1, keepdims=True)
        dov = do32.astype(BF16)
        p = jnp.exp(s - lse)
        vcat = jnp.concatenate([vp_ref[...], vc_ref[...]], axis=0)
        dp = lax.dot_general(dov, vcat, _NT, preferred_element_type=F32)
        ds = p * (dp - delta)
        dsb = ds.astype(BF16)
        dq = jnp.dot(dsb, kcat, preferred_element_type=F32) * scale
        for g in range(group):
            dq_ref[:, g * HEAD_DIM:(g + 1) * HEAD_DIM] = dq[g * B:(g + 1) * B, :].astype(BF16)
        dkcat = lax.dot_general(dsb, qs, _TN, preferred_element_type=F32) * scale
        dvcat = lax.dot_general(p.astype(BF16), dov, _TN, preferred_element_type=F32)
        prev0 = pl.multiple_of(jnp.maximum(n - 1, 0) * B, B)
        cur0 = pl.multiple_of(n * B, B)
        dk_ref[0, pl.ds(prev0, B), :] += dkcat[:B, :]
        dk_ref[0, pl.ds(cur0, B), :] += dkcat[B:, :]
        dv_ref[0, pl.ds(prev0, B), :] += dvcat[:B, :]
        dv_ref[0, pl.ds(cur0, B), :] += dvcat[B:, :]
        dsk = -jnp.exp(sink - lse) * delta
        lane = lax.broadcasted_iota(jnp.int32, (1, LANES), 1)
        row = jnp.zeros((1, LANES), F32)
        for g in range(group):
            row = row + jnp.where(lane == g, jnp.sum(dsk[g * B:(g + 1) * B, :]), 0.0)
        dsink_ref[0, 0:1, :] += row

    specs = _swa_specs(n_kv, group, 0, n_q, v_first)
    q_blk = pl.BlockSpec((B, group * HEAD_DIM), lambda kv, n: (n, kv))
    acc = pl.BlockSpec((1, S, HEAD_DIM), lambda kv, n: (kv, 0, 0))
    return pl.pallas_call(
        body, name="swa_bwd", grid=(n_kv, S // B),
        in_specs=specs + [q_blk, pl.BlockSpec((B, group * HEAD_DIM), lambda kv, n: (n, do_first + kv)),
                          pl.BlockSpec((1, B, group * LANES), lambda kv, n: (kv, n, 0)),
                          pl.BlockSpec(memory_space=pltpu.SMEM)],
        out_specs=[q_blk, acc, acc, pl.BlockSpec((1, 8, LANES), lambda kv, n: (kv, 0, 0))],
        out_shape=[jax.ShapeDtypeStruct((S, n_q * HEAD_DIM), BF16), jax.ShapeDtypeStruct((n_kv, S, HEAD_DIM), F32),
                   jax.ShapeDtypeStruct((n_kv, S, HEAD_DIM), F32), jax.ShapeDtypeStruct((n_kv, 8, LANES), F32)],
        compiler_params=pltpu.CompilerParams(dimension_semantics=("parallel", "arbitrary")),
    )(rq, rq, rq, proj, proj, o, do, lse_b, sinks)


def _adamw(w, g, m, v):
    m = ADAM_B1 * m + (1.0 - ADAM_B1) * g
    v = ADAM_B2 * v + (1.0 - ADAM_B2) * (g * g)
    m_hat = m / (1.0 - ADAM_B1 ** ADAM_STEP)
    v_hat = v / (1.0 - ADAM_B2 ** ADAM_STEP)
    delta = -ADAM_LR * (m_hat / (jnp.sqrt(v_hat) + ADAM_EPS) + ADAM_WD * w)
    return delta, m, v


def _mod_fwd(cond_in, w_mod, b_shard):
    R, D = cond_in.shape
    cols = w_mod.shape[1]
    tn = _fit(512, cols)

    def body(c_ref, w_ref, b_ref, o_ref):
        cv = c_ref[...]
        cond = (cv / (1.0 + jnp.exp(-cv))).astype(BF16)
        o_ref[...] = jnp.dot(cond, w_ref[...].astype(BF16), preferred_element_type=F32) + b_ref[...]

    return pl.pallas_call(
        body, name="mod_fwd", grid=(cols // tn,),
        in_specs=[pl.BlockSpec((R, D), lambda j: (0, 0)), pl.BlockSpec((D, tn), lambda j: (0, j)),
                  pl.BlockSpec((1, tn), lambda j: (0, j))],
        out_specs=pl.BlockSpec((R, tn), lambda j: (0, j)),
        out_shape=jax.ShapeDtypeStruct((R, cols), F32),
        compiler_params=pltpu.CompilerParams(dimension_semantics=("parallel",), vmem_limit_bytes=_vmem(3 * D * tn * 4)),
    )(cond_in, w_mod, b_shard)


def _mod_update(c_t, dmod, w, m, v):
    D, nb = c_t.shape
    cols = w.shape[1]
    tn = _fit(256, cols)

    def body(c_ref, d_ref, w_ref, m_ref, v_ref, g_ref, dl_ref, nm_ref, nv_ref):
        cv = c_ref[...]
        cond = cv / (1.0 + jnp.exp(-cv))
        g = jnp.zeros((D, tn), F32)
        for b in range(nb):
            g = g + cond[:, b:b + 1] * d_ref[b:b + 1, :]
        g_ref[...] = g
        dl_ref[...], nm_ref[...], nv_ref[...] = _adamw(w_ref[...], g, m_ref[...], v_ref[...])

    blk = pl.BlockSpec((D, tn), lambda j: (0, j))
    out = jax.ShapeDtypeStruct((D, cols), F32)
    return pl.pallas_call(
        body, name="mod_update", grid=(cols // tn,),
        in_specs=[pl.BlockSpec((D, nb), lambda j: (0, 0)), pl.BlockSpec((nb, tn), lambda j: (0, j)), blk, blk, blk],
        out_specs=[blk] * 4, out_shape=[out] * 4,
        compiler_params=pltpu.CompilerParams(dimension_semantics=("parallel",), vmem_limit_bytes=_vmem(18 * D * tn * 4)),
    )(c_t, dmod, w, m, v)


def _small_update(stacked, w, m, v):
    R, C = w.shape

    def body(s_ref, w_ref, m_ref, v_ref, g_ref, dl_ref, nm_ref, nv_ref):
        g = s_ref[0:R, :]
        for d in range(1, N_DEV):
            g = g + s_ref[d * R:(d + 1) * R, :]
        g_ref[...] = g
        dl_ref[...], nm_ref[...], nv_ref[...] = _adamw(w_ref[...], g, m_ref[...], v_ref[...])

    return pl.pallas_call(body, name="small_update", out_shape=[jax.ShapeDtypeStruct((R, C), F32)] * 4)(stacked, w, m, v)


def _place():
    return lax.axis_index("x"), lax.axis_index("y"), lax.axis_index("c")


def _allgather8(name, block):
    m_per, n = block.shape

    def body(x_ref, out_ref, token_ref, send_sems, recv_sems, local_sem):
        token_ref[...] = jnp.zeros_like(token_ref)
        x, y, c = _place()
        me, sibling = (x, y, c), (x, y, 1 - c)
        chips = [(1 - x, y), (x, 1 - y), (1 - x, 1 - y)]

        def rows(px, py, pc):
            return out_ref.at[pl.ds((4 * px + 2 * py + pc) * m_per, m_per), :]

        def copy(k, blk, to, src=None):
            return pltpu.make_async_remote_copy(
                src_ref=rows(*blk) if src is None else src, dst_ref=rows(*blk),
                send_sem=send_sems.at[k], recv_sem=recv_sems.at[k], device_id=to, device_id_type=MESH)

        mine = pltpu.make_async_copy(x_ref, rows(*me), local_sem)
        mine.start()
        first = [copy(0, me, sibling, src=x_ref)]
        first += [copy(1 + j, me, (*chip, c), src=x_ref) for j, chip in enumerate(chips)]
        for cp in first:
            cp.start()
        passed = [copy(4 + j, (*chip, c), sibling) for j, chip in enumerate(chips)]
        for j, chip in enumerate(chips):
            copy(1 + j, (*chip, c), me).wait_recv()
            passed[j].start()
        copy(0, sibling, me).wait_recv()
        for j, chip in enumerate(chips):
            copy(4 + j, (*chip, 1 - c), me).wait_recv()
        for cp in first + passed:
            cp.wait_send()
        mine.wait()

    vmem = pl.BlockSpec(memory_space=pltpu.VMEM)
    return pl.pallas_call(
        body, name=name,
        out_shape=[jax.ShapeDtypeStruct((N_DEV * m_per, n), block.dtype), jax.ShapeDtypeStruct((8, LANES), F32)],
        in_specs=[vmem], out_specs=[vmem, vmem],
        scratch_shapes=[pltpu.SemaphoreType.DMA((7,)), pltpu.SemaphoreType.DMA((7,)), pltpu.SemaphoreType.DMA],
    )(block)


_ANY = pl.BlockSpec(memory_space=pl.ANY)


def _half(ref, c, rows):
    return ref.at[pl.ds(c * (rows // 2), rows // 2), :]


_HBM = pl.BlockSpec(memory_space=pltpu.HBM)
_SEM = pl.BlockSpec(memory_space=pltpu.SEMAPHORE)
_EFFECT = pltpu.SideEffectType.DATAFLOW_SIDE_EFFECTING


def _ici_start(name, srcs, land_shapes, plan):
    ns, nl = len(srcs), len(land_shapes)
    n_copies = 3 * ns

    def body(*refs):
        src_refs, land_refs = refs[:ns], refs[ns:ns + nl]
        send_sems, recv_sems = refs[ns + nl], refs[ns + nl + 1]
        token = refs[-1]
        for n, (src, dst, peer, _) in enumerate(plan(src_refs, land_refs)):
            pltpu.make_async_remote_copy(src_ref=src, dst_ref=dst, send_sem=send_sems.at[n], recv_sem=recv_sems.at[n],
                                         device_id=peer, device_id_type=MESH).start()
        token[...] = jnp.zeros_like(token)

    lands = [lax.empty(s.shape, s.dtype) for s in land_shapes]
    out = pl.pallas_call(
        body, name=name,
        out_shape=(pltpu.SemaphoreType.DMA((n_copies,)), pltpu.SemaphoreType.DMA((n_copies,)),
                   *[pltpu.HBM(a.shape, a.dtype) for a in list(srcs) + lands], jax.ShapeDtypeStruct((8, LANES), F32)),
        in_specs=[_HBM] * (ns + nl),
        out_specs=(_SEM, _SEM, *[_HBM] * (ns + nl), pl.BlockSpec(memory_space=pltpu.VMEM)),
        input_output_aliases={n: 2 + n for n in range(ns + nl)},
        compiler_params=pltpu.CompilerParams(has_side_effects=_EFFECT),
    )(*[pltpu.with_memory_space_constraint(a, pltpu.HBM) for a in list(srcs) + lands])
    return out[0], out[1], list(out[2:2 + ns]), list(out[2 + ns:2 + ns + nl]), out[-1]


def _ici_wait(name, send_sems, recv_sems, srcs, lands, plan, after):
    ns, nl = len(srcs), len(lands)

    def body(*refs):
        src_refs, land_refs = refs[:ns], refs[ns:ns + nl]
        send_sems, recv_sems = refs[ns + nl], refs[ns + nl + 1]
        for n, (src, _, peer, mine) in enumerate(plan(src_refs, land_refs)):
            cp = pltpu.make_async_remote_copy(src_ref=src, dst_ref=mine, send_sem=send_sems.at[n],
                                              recv_sem=recv_sems.at[n], device_id=peer, device_id_type=MESH)
            cp.wait_send()
            cp.wait_recv()

    out = pl.pallas_call(
        body, name=name, out_shape=[pltpu.HBM(a.shape, a.dtype) for a in list(srcs) + list(lands)],
        in_specs=[_HBM] * (ns + nl) + [_SEM, _SEM, _ANY], out_specs=[_HBM] * (ns + nl),
        input_output_aliases={n: n for n in range(ns + nl)},
        compiler_params=pltpu.CompilerParams(has_side_effects=_EFFECT),
    )(*srcs, *lands, send_sems, recv_sems, after)
    return list(out[:ns]), list(out[ns:])


def _gather_plan(src_refs, land_refs):
    x, y, c = _place()
    copies = []
    for w, land in zip(src_refs, land_refs):
        R = w.shape[0]
        for cx, cy in [(1 - x, y), (x, 1 - y), (1 - x, 1 - y)]:
            copies.append((_half(w, c, R), _half(land.at[2 * x + y], c, R), (cx, cy, c),
                           _half(land.at[2 * cx + cy], c, R)))
    return copies


def _pass_to_sibling(name, lands):
    nw = len(lands)

    def body(*refs):
        ins, outs = refs[:nw], refs[nw:2 * nw]
        send_sems, recv_sems = refs[2 * nw:]
        x, y, c = _place()
        chips = [(1 - x, y), (x, 1 - y), (1 - x, 1 - y)]
        copies = []
        for k in range(nw):
            R = ins[k].shape[1]
            for j, (cx, cy) in enumerate(chips):
                cp = pltpu.make_async_remote_copy(
                    src_ref=_half(ins[k].at[2 * cx + cy], c, R), dst_ref=_half(outs[k].at[2 * cx + cy], c, R),
                    send_sem=send_sems.at[3 * k + j], recv_sem=recv_sems.at[3 * k + j],
                    device_id=(x, y, 1 - c), device_id_type=MESH)
                cp.start()
                copies.append(cp)
        for k in range(nw):
            R = ins[k].shape[1]
            for j, (cx, cy) in enumerate(chips):
                pltpu.make_async_remote_copy(
                    src_ref=_half(ins[k].at[2 * cx + cy], c, R), dst_ref=_half(outs[k].at[2 * cx + cy], 1 - c, R),
                    send_sem=send_sems.at[3 * k + j], recv_sem=recv_sems.at[3 * k + j],
                    device_id=(x, y, 1 - c), device_id_type=MESH).wait_recv()
        for cp in copies:
            cp.wait_send()

    return pl.pallas_call(
        body, name=name, out_shape=[jax.ShapeDtypeStruct(a.shape, a.dtype) for a in lands],
        in_specs=[_ANY] * nw, out_specs=[_ANY] * nw, input_output_aliases={k: k for k in range(nw)},
        scratch_shapes=[pltpu.SemaphoreType.DMA((3 * nw,)), pltpu.SemaphoreType.DMA((3 * nw,))],
    )(*lands)


def _tie(vec, token):
    return vec + token[0:1, 0:1]


def _pair_exchange(name, grads):
    nw = len(grads)

    def body(*refs):
        gs, outs = refs[:nw], refs[nw:2 * nw]
        send_sems, recv_sems = refs[2 * nw:]
        x, y, c = _place()
        copies = []
        for k in range(nw):
            half = gs[k].shape[1] // 2
            cp = pltpu.make_async_remote_copy(
                src_ref=gs[k].at[:, pl.ds((1 - c) * half, half), :], dst_ref=outs[k],
                send_sem=send_sems.at[k], recv_sem=recv_sems.at[k], device_id=(x, y, 1 - c), device_id_type=MESH)
            cp.start()
            copies.append(cp)
        for cp in copies:
            cp.wait()

    return pl.pallas_call(
        body, name=name,
        out_shape=[jax.ShapeDtypeStruct((N_CHIPS, g.shape[1] // 2, g.shape[2]), g.dtype) for g in grads],
        in_specs=[_ANY] * nw, out_specs=[_ANY] * nw,
        scratch_shapes=[pltpu.SemaphoreType.DMA((nw,)), pltpu.SemaphoreType.DMA((nw,))],
    )(*grads)


def _pair_add(name, core, grad, recv):
    n, R, C = grad.shape
    half = R // 2
    tr = _fit(256, half)
    nblk = half // tr

    def body(core_ref, g_ref, r_ref, o_ref):
        o_ref[...] = (g_ref[...].astype(F32) + r_ref[...].astype(F32)).astype(BF16)

    grid_spec = pltpu.PrefetchScalarGridSpec(
        num_scalar_prefetch=1, grid=(n, nblk),
        in_specs=[pl.BlockSpec((1, tr, C), lambda s, r, core_ref: (s, core_ref[0] * nblk + r, 0)),
                  pl.BlockSpec((1, tr, C), lambda s, r, core_ref: (s, r, 0))],
        out_specs=pl.BlockSpec((1, tr, C), lambda s, r, core_ref: (s, r, 0)))
    return pl.pallas_call(
        body, name=name, grid_spec=grid_spec, out_shape=jax.ShapeDtypeStruct((n, half, C), BF16),
        compiler_params=pltpu.CompilerParams(dimension_semantics=("parallel", "parallel")),
    )(core, grad, recv)


def _scatter_plan(src_refs, land_refs):
    x, y, c = _place()
    copies = []
    for p, land in zip(src_refs, land_refs):
        for j, (cx, cy) in enumerate([(1 - x, y), (x, 1 - y), (1 - x, 1 - y)]):
            copies.append((p.at[2 * cx + cy], land.at[j], (cx, cy, c), land.at[j]))
    return copies


def _chip_add(name, chip, sums, recv):
    _, H, C = sums.shape
    tr = _fit(256, H)

    def body(chip_ref, p_ref, r_ref, o_ref):
        total = p_ref[0].astype(F32)
        for j in range(3):
            total = total + r_ref[j].astype(F32)
        o_ref[...] = total

    grid_spec = pltpu.PrefetchScalarGridSpec(
        num_scalar_prefetch=1, grid=(H // tr,),
        in_specs=[pl.BlockSpec((1, tr, C), lambda r, chip_ref: (chip_ref[0], r, 0)),
                  pl.BlockSpec((3, tr, C), lambda r, chip_ref: (0, r, 0))],
        out_specs=pl.BlockSpec((tr, C), lambda r, chip_ref: (r, 0)))
    return pl.pallas_call(
        body, name=name, grid_spec=grid_spec, out_shape=jax.ShapeDtypeStruct((H, C), F32),
        compiler_params=pltpu.CompilerParams(dimension_semantics=("parallel",)),
    )(chip, sums, recv)


def _pair_share(name, halves):
    nw = len(halves)

    def body(*refs):
        hs, outs = refs[:nw], refs[nw:2 * nw]
        send_sems, recv_sems = refs[2 * nw:]
        x, y, c = _place()
        copies = []
        for k in range(nw):
            cp = pltpu.make_async_remote_copy(
                src_ref=hs[k], dst_ref=outs[k], send_sem=send_sems.at[k], recv_sem=recv_sems.at[k],
                device_id=(x, y, 1 - c), device_id_type=MESH)
            cp.start()
            copies.append(cp)
        for cp in copies:
            cp.wait()

    return pl.pallas_call(
        body, name=name,
        out_shape=[jax.ShapeDtypeStruct(h.shape, h.dtype) for h in halves],
        in_specs=[_ANY] * nw, out_specs=[_ANY] * nw,
        scratch_shapes=[pltpu.SemaphoreType.DMA((nw,)), pltpu.SemaphoreType.DMA((nw,))],
    )(*halves)


def _adam_halves(name, core, w, g_own, g_other, m, v):
    R, C = w.shape
    H = R // 2
    tr = _fit(256, H)
    nblk = H // tr

    def body(core_ref, w_ref, go_ref, gr_ref, m_ref, v_ref, g_ref, dl_ref, nm_ref, nv_ref):
        own = (pl.program_id(0) // nblk) == core_ref[0]
        g = jnp.where(own, go_ref[...], gr_ref[...])
        g_ref[...] = g
        dl_ref[...], nm_ref[...], nv_ref[...] = _adamw(w_ref[...], g, m_ref[...], v_ref[...])

    blk = pl.BlockSpec((tr, C), lambda r, core_ref: (r, 0))

    def half_spec(is_own):
        def index(r, core_ref):
            mine = ((r // nblk) == core_ref[0]) == is_own
            return (jnp.where(mine, r % nblk, jnp.where(is_own == (core_ref[0] == 0), nblk - 1, 0)), 0)
        return pl.BlockSpec((tr, C), index)
    out = jax.ShapeDtypeStruct((R, C), F32)
    padded = -(-C // LANES) * LANES
    grid_spec = pltpu.PrefetchScalarGridSpec(
        num_scalar_prefetch=1, grid=(R // tr,), in_specs=[blk, half_spec(True), half_spec(False), blk, blk],
        out_specs=[blk] * 4)
    return pl.pallas_call(
        body, name=name, grid_spec=grid_spec, out_shape=[out] * 4,
        compiler_params=pltpu.CompilerParams(dimension_semantics=("parallel",), vmem_limit_bytes=_vmem(20 * tr * padded * 4)),
    )(core, w, g_own, g_other, m, v)


def kernel(x, c, w_mod, b_mod, g_pre_mix, g_post_mix, w_in, b_forget, swa_sinks, w_out, g_pre_mlp, g_post_mlp, w_up, w_down, loss_target, m_w_mod, m_b_mod, m_g_pre_mix, m_g_post_mix, m_w_in, m_b_forget, m_swa_sinks, m_w_out, m_g_pre_mlp, m_g_post_mlp, m_w_up, m_w_down, v_w_mod, v_b_mod, v_g_pre_mix, v_g_post_mix, v_w_in, v_b_forget, v_swa_sinks, v_w_out, v_g_pre_mlp, v_g_post_mlp, v_w_up, v_w_down):
    S, D = x.shape[1], x.shape[2]
    n_heads = D // HEAD_DIM
    n_fox = n_heads // 2
    n_swa = n_heads - n_fox
    n_kv = max(1, n_swa // 4)
    fox_w, swa_w, kv_w = n_fox * HEAD_DIM, n_swa * HEAD_DIM, n_kv * HEAD_DIM
    main_w = 3 * fox_w + swa_w + 2 * kv_w
    in_w = main_w + n_fox
    mod_cols = w_mod.shape[2]

    ax, ay, ac = _place()
    chip = 2 * ax + ay
    dev = 2 * chip + ac
    chip_arr = jnp.reshape(chip, (1,)).astype(jnp.int32)
    core_arr = jnp.reshape(ac, (1,)).astype(jnp.int32)

    x2, tgt = x[0], loss_target[0]

    c_all, _ = _allgather8("gather_c", c.reshape(8, D // 8))
    c_all = c_all.reshape(N_DEV, D)
    b_shard = lax.dynamic_slice_in_dim(b_mod, chip * mod_cols, mod_cols, axis=1)
    mod_shard = _mod_fwd(jnp.pad(c_all, ((0, 16 - N_DEV), (0, 0))), w_mod[0], b_shard)[:N_DEV]
    mod_all, token = _allgather8("gather_mod", mod_shard)
    mod_all = mod_all.reshape(N_CHIPS, 2, N_DEV, mod_cols)[:, 0]
    mod = lax.dynamic_index_in_dim(mod_all, dev, axis=1, keepdims=False).reshape(N_MOD, 1, D)
    sh_a, sc_a, gt_a, sh_m, sc_m, gt_m = [mod[n] for n in range(N_MOD)]

    names = ["w_in", "w_out", "w_up", "w_down"]
    flights = {}
    for n, w in zip(names, [w_in, w_out, w_up, w_down]):
        shard = _tie(w[0], token).astype(BF16)
        flights[n] = _ici_start("gather_start_" + n, [shard], [jax.ShapeDtypeStruct((N_CHIPS,) + shard.shape, BF16)],
                                _gather_plan)
        token = flights[n][4]
    sc_a = _tie(sc_a, token)

    def gathered(n, after):
        send, recv, srcs, lands, _ = flights[n]
        srcs, lands = _ici_wait("gather_wait_" + n, send, recv, srcs, lands, _gather_plan, after)
        lands = _pass_to_sibling("gather_pass_" + n, lands)
        return lax.dynamic_update_index_in_dim(lands[0], srcs[0], chip, 0)

    d_ff = N_CHIPS * w_up.shape[2]

    h = _pre_norm(x2, g_pre_mix, sc_a, sh_a)
    w_in_f = jnp.transpose(gathered("w_in", h), (1, 0, 2)).reshape(D, in_w)
    w_main = jnp.concatenate([w_in_f[:, :3 * fox_w], w_in_f[:, 3 * fox_w + n_fox:]], axis=1)
    w_fg = jnp.pad(w_in_f[:, 3 * fox_w:3 * fox_w + n_fox], ((0, 0), (0, LANES - n_fox)))
    proj = _mm_plain("in_proj", h, w_main, "nn", BF16, tn=_fit(768, main_w))
    fg = _mm_plain("in_proj_gate", h, w_fg, "nn", F32)
    b_pad = jnp.pad(b_forget, ((0, 0), (0, LANES - n_fox)))
    cum_row = _fox_gate_fwd(fg, b_pad)[:n_fox].reshape(n_fox, 1, S)
    fox_o, fox_lse = _fox_fwd(proj, cum_row, n_fox)

    cos, sin_signed = _rope_tables(S)
    rq = _rope("rope_fwd", proj, 3 * n_fox, n_swa + n_kv, cos, sin_signed)
    v_first = 3 * n_fox + n_swa + n_kv
    sinks = swa_sinks[0]
    swa_o, swa_lse = _swa_fwd(rq, proj, v_first, sinks, n_swa, n_kv)

    mixcat = jnp.concatenate([fox_o, swa_o], axis=1).astype(BF16)
    w_out_f = gathered("w_out", mixcat).reshape(D, D)
    mix = _mm_plain("out_proj", mixcat, w_out_f, "nn", F32)
    x1, h2 = _post_mix(x2, mix, g_post_mix, gt_a, g_pre_mlp, sc_m, sh_m)
    w_up_f = jnp.transpose(gathered("w_up", h2), (1, 0, 2)).reshape(D, d_ff)

    tm_u, tn_u = _fit(MM_TM, S), _fit(MM_TN, d_ff)

    def up_epilogue(acc, ex, outs):
        outs[0][...] = acc.astype(BF16)
        r = jnp.maximum(acc, 0.0)
        outs[1][...] = (r * r).astype(BF16)

    ublk = ((S, d_ff), BF16, (tm_u, tn_u), lambda i, j: (i, j))
    u, a = _matmul("mlp_up", h2, w_up_f, "nn", [ublk, ublk], up_epilogue)
    w_down_f = gathered("w_down", a).reshape(d_ff, D)
    y = _mm_plain("mlp_down", a, w_down_f, "nn", F32)

    dy, dout, loss_part, acc_mlp_post = _loss_and_post_mlp_bwd(x1, y, tgt, g_post_mlp, gt_m)
    loss = lax.psum(loss_part[0, 0], ("x", "y", "c"))

    def du_epilogue(acc, ex, outs):
        outs[0][...] = (acc * (2.0 * jnp.maximum(ex[0][...].astype(F32), 0.0))).astype(BF16)

    du = _matmul("mlp_down_bwd", dy, w_down_f, "nt", [ublk], du_epilogue,
                 extras=[(u, (tm_u, tn_u), lambda i, j: (i, j))])[0]
    g_down = _mm_plain("grad_w_down", a, dy, "tn", BF16)
    tn_s = _fit(MM_TN, w_up.shape[2])
    per = w_up.shape[2] // tn_s

    def shard_epilogue(acc, ex, outs):
        outs[0][0] = acc.astype(BF16)

    g_up = _matmul("grad_w_up", h2, du, "tn",
                   [((N_CHIPS, D, w_up.shape[2]), BF16, (1, _fit(MM_TM, D), tn_s), lambda i, j: (j // per, i, j % per))],
                   shard_epilogue, tn=tn_s)[0]

    def reduce_start(tag, fulls):
        from_sibling = _pair_exchange("grad_pair_exchange_" + tag, fulls)
        sums = [_pair_add("pair_add_%s_%d" % (tag, k), core_arr, g, r) for k, (g, r) in enumerate(zip(fulls, from_sibling))]
        return _ici_start("grad_scatter_start_" + tag, sums,
                          [jax.ShapeDtypeStruct((3,) + p.shape[1:], BF16) for p in sums], _scatter_plan)

    def reduce_finish(tag, flight, after):
        send, recv, srcs, lands, _ = flight
        sums, received = _ici_wait("grad_scatter_wait_" + tag, send, recv, srcs, lands, _scatter_plan, after)
        halves = [_chip_add("chip_add_%s_%d" % (tag, k), chip_arr, p, r) for k, (p, r) in enumerate(zip(sums, received))]
        return halves, _pair_share("grad_pair_share_" + tag, halves)

    flight_mlp = reduce_start("mlp", [g_up, g_down.reshape(N_CHIPS, d_ff // N_CHIPS, D)])
    dh2 = _mm_plain("mlp_up_bwd", du, w_up_f, "nt", F32)
    dx1, dmix, acc_mid = _pre_mlp_and_post_mix_bwd(dh2, x1, dout, mix, _tie(g_pre_mlp, flight_mlp[4]), sc_m,
                                                   g_post_mix, gt_a)

    dmixcat = _mm_plain("out_proj_bwd", dmix, w_out_f, "nt", F32)
    g_out = _mm_plain("grad_w_out", mixcat, dmix, "tn", BF16)

    fdq, fdk, fdv, dcum_row, dcum_q = _fox_bwd(proj, fox_o, dmixcat, fox_lse, cum_row, n_fox)
    dcum_k = jnp.pad(dcum_row.reshape(n_fox, S), ((0, LANES - n_fox), (0, 0)))
    dfg, db_forget = _fox_gate_bwd(dcum_k, dcum_q, fg, b_pad)

    group_w = (n_swa // n_kv) * HEAD_DIM
    sdq, sdk, sdv, dsink = _swa_bwd(rq, proj, v_first, sinks, swa_o, dmixcat, fox_w // group_w, swa_lse, n_swa, n_kv)
    drq = jnp.concatenate([sdq, jnp.transpose(sdk, (1, 0, 2)).reshape(S, kv_w).astype(BF16)], axis=1)
    d_sq_sk = _rope("rope_bwd", drq, 0, n_swa + n_kv, cos, -sin_signed)
    dsv = jnp.transpose(sdv, (1, 0, 2)).reshape(S, kv_w).astype(BF16)
    dproj = jnp.concatenate([fdq, fdk, fdv, d_sq_sk, dsv], axis=1)

    g_main = _mm_plain("grad_w_in", h, dproj, "tn", BF16, tn=_fit(768, main_w))
    g_fg = _mm_plain("grad_w_in_gate", h, dfg, "tn", BF16)
    dh_gate = _mm_plain("in_proj_gate_bwd", dfg, w_fg, "nt", F32)

    def add_epilogue(acc, ex, outs):
        outs[0][...] = acc + ex[0][...]

    tm_h, tn_h = _fit(MM_TM, S), _fit(MM_TN, D)
    dh = _matmul("in_proj_bwd", dproj, w_main, "nt", [((S, D), F32, (tm_h, tn_h), lambda i, j: (i, j))], add_epilogue,
                 extras=[(dh_gate, (tm_h, tn_h), lambda i, j: (i, j))], tk=_fit(2304, main_w))[0]
    grad_x, acc_pre = _pre_mix_bwd(dh, x2, dx1, g_pre_mix, sc_a)

    zero_row = jnp.zeros((1, D), F32)
    tail = jnp.concatenate([db_forget[0:1, :n_fox], dsink[:, 0, :n_swa // n_kv].reshape(1, n_swa),
                            jnp.zeros((1, D - n_fox - n_swa), F32)], axis=1)
    partial = jnp.concatenate([
        acc_pre[0:1], acc_pre[1:2], acc_mid[3:4], acc_mid[0:1], acc_mid[1:2], acc_mlp_post[0:1],
        acc_pre[2:3], acc_mid[4:5], acc_mid[2:3], acc_mlp_post[1:2], tail] + [zero_row] * 5, axis=0)
    gathered_small, token = _allgather8("gather_small_grads", partial)

    g_fg_tied = _tie(g_fg[:, :n_fox].astype(F32), token).astype(BF16)
    g_in_f = jnp.concatenate([g_main[:, :3 * fox_w], g_fg_tied, g_main[:, 3 * fox_w:]], axis=1)
    flight_mix = reduce_start("mix", [jnp.transpose(g_in_f.reshape(D, N_CHIPS, in_w // N_CHIPS), (1, 0, 2)),
                                      g_out.reshape(N_CHIPS, D // N_CHIPS, D)])
    halves_mlp, others_mlp = reduce_finish("mlp", flight_mlp, flight_mix[4])

    def pack(bm, gpm, gqm, gpl, gql, bf, sk):
        last = jnp.concatenate([bf, sk, jnp.zeros((1, D - n_fox - n_swa), F32)], axis=1)
        return jnp.concatenate([bm.reshape(N_MOD, D), gpm, gqm, gpl, gql, last, jnp.zeros((5, D), F32)], axis=0)

    def unpack(p):
        return {"b_mod": p[0:N_MOD].reshape(1, N_MOD * D), "g_pre_mix": p[6:7], "g_post_mix": p[7:8],
                "g_pre_mlp": p[8:9], "g_post_mlp": p[9:10], "b_forget": p[10:11, :n_fox],
                "swa_sinks": p[10:11, n_fox:n_fox + n_swa]}

    small_out = _small_update(
        gathered_small, pack(b_mod, g_pre_mix, g_post_mix, g_pre_mlp, g_post_mlp, b_forget, swa_sinks),
        pack(m_b_mod, m_g_pre_mix, m_g_post_mix, m_g_pre_mlp, m_g_post_mlp, m_b_forget, m_swa_sinks),
        pack(v_b_mod, v_g_pre_mix, v_g_post_mix, v_g_pre_mlp, v_g_post_mlp, v_b_forget, v_swa_sinks))
    g_small, d_small, m_small, v_small = [unpack(p) for p in small_out]

    dmod_all = gathered_small.reshape(N_DEV, 16, D)[:, :N_MOD].reshape(N_DEV, N_MOD * D)
    dmod_shard = lax.dynamic_slice_in_dim(dmod_all, chip * mod_cols, mod_cols, axis=1)
    g_w_mod, d_w_mod, nm_w_mod, nv_w_mod = _mod_update(c_all.T, dmod_shard, w_mod[0], m_w_mod[0], v_w_mod[0])

    grads = dict(g_small, w_mod=g_w_mod[None])
    deltas = dict(d_small, w_mod=d_w_mod[None])
    new_m = dict(m_small, w_mod=nm_w_mod[None])
    new_v = dict(v_small, w_mod=nv_w_mod[None])
    weights = {"w_in": (w_in, m_w_in, v_w_in), "w_out": (w_out, m_w_out, v_w_out), "w_up": (w_up, m_w_up, v_w_up),
               "w_down": (w_down, m_w_down, v_w_down)}

    def big_update(n, own, other):
        w, m, v = weights[n]
        g, d_, m_, v_ = _adam_halves("adam_" + n, core_arr, w[0], own, other, m[0], v[0])
        grads[n], deltas[n], new_m[n], new_v[n] = g[None], d_[None], m_[None], v_[None]

    big_update("w_up", halves_mlp[0], others_mlp[0])
    big_update("w_down", halves_mlp[1], others_mlp[1])
    ran = deltas["w_down"][0, :8, :LANES] + deltas["w_up"][0, :8, :LANES] + d_w_mod[:8, :LANES]
    halves_mix, others_mix = reduce_finish("mix", flight_mix, ran)
    big_update("w_in", halves_mix[0], others_mix[0])
    big_update("w_out", halves_mix[1], others_mix[1])

    order = ["w_mod", "b_mod", "g_pre_mix", "g_post_mix", "w_in", "b_forget", "swa_sinks", "w_out", "g_pre_mlp",
             "g_post_mlp", "w_up", "w_down"]
    return (loss, grad_x[None], *[grads[n] for n in order], *[deltas[n] for n in order],
            *[new_m[n] for n in order], *[new_v[n] for n in order])
```

```python
import jax
import jax.numpy as jnp
from jax import lax
from jax.experimental import pallas as pl
from jax.experimental.pallas import tpu as pltpu

F32 = jnp.float32
BF16 = jnp.bfloat16
MESH = pl.DeviceIdType.MESH

HEAD_DIM = 128
SWA_BLOCK = 128
ROPE_THETA = 10000.0
NORM_EPS = 1e-6
NEG = -1e30
N_MOD = 6
ADAM_LR = 0.001
ADAM_B1 = 0.9
ADAM_B2 = 0.999
ADAM_EPS = 1e-08
ADAM_WD = 0.01
ADAM_STEP = 10
N_CHIPS = 4
N_DEV = 8
LANES = 128
VMEM_CAP = 60 * 1024 * 1024

_NN = (((1,), (0,)), ((), ()))
_NT = (((1,), (1,)), ((), ()))
_TN = (((0,), (0,)), ((), ()))


def _vmem(nbytes):
    return int(min(VMEM_CAP, nbytes * 5 // 4 + (4 << 20)))


def _nbytes(shape, dtype):
    n = 1
    for s in shape:
        n *= s
    return n * jnp.dtype(dtype).itemsize


def _fit(t, n):
    t = min(t, n)
    assert n % t == 0, (t, n)
    return t


MM_TM, MM_TN, MM_TK = 512, 1024, 2048


def _matmul(name, a, b, mode, out_defs, epilogue, extras=(), tm=MM_TM, tn=MM_TN, tk=MM_TK):
    if mode == "nn":
        (M, K), (K2, N) = a.shape, b.shape
    elif mode == "nt":
        (M, K), (N, K2) = a.shape, b.shape
    else:
        (K, M), (K2, N) = a.shape, b.shape
    assert K == K2, (a.shape, b.shape, mode)
    tm, tn, tk = _fit(tm, M), _fit(tn, N), _fit(tk, K)
    nk = K // tk
    dims = {"nn": _NN, "nt": _NT, "tn": _TN}[mode]
    a_spec = (pl.BlockSpec((tk, tm), lambda i, j, k: (k, i)) if mode == "tn"
              else pl.BlockSpec((tm, tk), lambda i, j, k: (i, k)))
    b_spec = (pl.BlockSpec((tn, tk), lambda i, j, k: (j, k)) if mode == "nt"
              else pl.BlockSpec((tk, tn), lambda i, j, k: (k, j)))
    n_ex, n_out = len(extras), len(out_defs)

    def body(*refs):
        a_ref, b_ref = refs[0], refs[1]
        ex = refs[2:2 + n_ex]
        outs = refs[2 + n_ex:2 + n_ex + n_out]
        prod = lax.dot_general(a_ref[...], b_ref[...], dims, preferred_element_type=F32)
        if nk == 1:
            epilogue(prod, ex, outs)
        else:
            acc_ref = refs[-1]
            k = pl.program_id(2)

            @pl.when(k == 0)
            def _():
                acc_ref[...] = prod

            @pl.when(k > 0)
            def _():
                acc_ref[...] += prod

            @pl.when(k == nk - 1)
            def _():
                epilogue(acc_ref[...], ex, outs)

    def wrap(f):
        return lambda i, j, k: f(i, j)

    in_specs = [a_spec, b_spec] + [pl.BlockSpec(blk, wrap(f)) for _, blk, f in extras]
    out_specs = [pl.BlockSpec(blk, wrap(f)) for _, _, blk, f in out_defs]
    out_shape = [jax.ShapeDtypeStruct(s, d) for s, d, _, _ in out_defs]
    need = 2 * (tm * tk + tk * tn) * a.dtype.itemsize + 3 * tm * tn * 4
    need += sum(2 * _nbytes(blk, arr.dtype) for arr, blk, _ in extras)
    need += sum(2 * _nbytes(blk, d) for _, d, blk, _ in out_defs)
    res = pl.pallas_call(
        body, name=name, grid=(M // tm, N // tn, nk),
        in_specs=in_specs, out_specs=out_specs, out_shape=out_shape,
        scratch_shapes=[pltpu.VMEM((tm, tn), F32)] if nk > 1 else [],
        compiler_params=pltpu.CompilerParams(
            dimension_semantics=("parallel", "parallel", "arbitrary"), vmem_limit_bytes=_vmem(need)),
    )(a, b, *[arr for arr, _, _ in extras])
    return res


def _mm_plain(name, a, b, mode, out_dtype, **tiles):
    if mode == "nn":
        M, N = a.shape[0], b.shape[1]
    elif mode == "nt":
        M, N = a.shape[0], b.shape[0]
    else:
        M, N = a.shape[1], b.shape[1]
    tm, tn = _fit(tiles.get("tm", MM_TM), M), _fit(tiles.get("tn", MM_TN), N)

    def epi(acc, ex, outs):
        outs[0][...] = acc.astype(out_dtype)

    return _matmul(name, a, b, mode, [((M, N), out_dtype, (tm, tn), lambda i, j: (i, j))], epi, **tiles)[0]


def _rstd(v):
    return lax.rsqrt(jnp.mean(v * v, axis=-1, keepdims=True) + NORM_EPS)


def _row_call(name, body, row_ins, vec_ins, row_outs, acc_outs, S, D, tr):
    tr = _fit(tr, S)
    row_spec = pl.BlockSpec((tr, D), lambda r: (r, 0))
    vec_spec = pl.BlockSpec((1, D), lambda r: (0, 0))
    in_specs = [row_spec] * len(row_ins) + [vec_spec] * len(vec_ins)
    out_specs = [row_spec] * len(row_outs) + [pl.BlockSpec(shp, lambda r: (0, 0)) for shp in acc_outs]
    out_shape = [jax.ShapeDtypeStruct((S, D), d) for d in row_outs] + [jax.ShapeDtypeStruct(shp, F32) for shp in acc_outs]
    need = sum(2 * tr * D * a.dtype.itemsize for a in row_ins) + sum(2 * tr * D * jnp.dtype(d).itemsize for d in row_outs)
    need += 8 * tr * D * 4
    return pl.pallas_call(
        body, name=name, grid=(S // tr,), in_specs=in_specs, out_specs=out_specs, out_shape=out_shape,
        compiler_params=pltpu.CompilerParams(dimension_semantics=("arbitrary",), vmem_limit_bytes=_vmem(need)),
    )(*row_ins, *vec_ins)


def _acc_rows(ref, rows):
    @pl.when(pl.program_id(0) == 0)
    def _():
        ref[...] = jnp.zeros_like(ref)
    for n, r in enumerate(rows):
        ref[n:n + 1, :] += r


def _pre_norm(x, g, sc, sh):
    S, D = x.shape

    def body(x_ref, g_ref, sc_ref, sh_ref, h_ref):
        xv = x_ref[...]
        xn = xv * _rstd(xv)
        h_ref[...] = (xn * g_ref[...] * (1.0 + sc_ref[...]) + sh_ref[...]).astype(BF16)

    return _row_call("pre_norm_mix", body, [x], [g, sc, sh], [BF16], [], S, D, 256)[0]


def _post_mix(x, mix, g_post, gt, g_pre, sc, sh):
    S, D = x.shape

    def body(x_ref, mix_ref, gp_ref, gt_ref, g2_ref, sc_ref, sh_ref, x1_ref, h2_ref):
        mv = mix_ref[...]
        x1 = x_ref[...] + gt_ref[...] * (mv * _rstd(mv) * gp_ref[...])
        x1_ref[...] = x1
        h2_ref[...] = (x1 * _rstd(x1) * g2_ref[...] * (1.0 + sc_ref[...]) + sh_ref[...]).astype(BF16)

    return _row_call("post_mix_pre_mlp", body, [x, mix], [g_post, gt, g_pre, sc, sh], [F32, BF16], [], S, D, 256)


def _loss_and_post_mlp_bwd(x1, y, target, g_post, gt):
    S, D = x1.shape

    def body(x1_ref, y_ref, t_ref, g_ref, gt_ref, dy_ref, dout_ref, loss_ref, acc_ref):
        yv = y_ref[...]
        r = _rstd(yv)
        yh = yv * r
        n = yh * g_ref[...]
        diff = x1_ref[...] + gt_ref[...] * n - t_ref[...]
        dout = diff * (1.0 / D)
        dout_ref[...] = dout
        dn = dout * gt_ref[...]
        dyh = dn * g_ref[...]
        dy_ref[...] = (r * (dyh - yh * jnp.mean(dyh * yh, axis=-1, keepdims=True))).astype(BF16)
        _acc_rows(acc_ref, [jnp.sum(dout * n, axis=0, keepdims=True), jnp.sum(dn * yh, axis=0, keepdims=True)])

        @pl.when(pl.program_id(0) == 0)
        def _():
            loss_ref[...] = jnp.zeros_like(loss_ref)
        loss_ref[...] += jnp.full(loss_ref.shape, (0.5 / D) * jnp.sum(diff * diff), F32)

    return _row_call("loss_post_mlp_bwd", body, [x1, y, target], [g_post, gt], [BF16, F32],
                     [(8, LANES), (8, D)], S, D, 128)


def _pre_mlp_and_post_mix_bwd(dh2, x1, dout, mix, g_pre, sc, g_post, gt):
    S, D = x1.shape

    def body(dh_ref, x1_ref, dout_ref, mix_ref, g_ref, sc_ref, gp_ref, gt_ref, dx1_ref, dmix_ref, acc_ref):
        dh = dh_ref[...]
        x1v = x1_ref[...]
        r3 = _rstd(x1v)
        xn = x1v * r3
        dxn = dh * (1.0 + sc_ref[...]) * g_ref[...]
        dx1 = dout_ref[...] + r3 * (dxn - xn * jnp.mean(dxn * xn, axis=-1, keepdims=True))
        dx1_ref[...] = dx1
        mv = mix_ref[...]
        r2 = _rstd(mv)
        mh = mv * r2
        dn = dx1 * gt_ref[...]
        dmh = dn * gp_ref[...]
        dmix_ref[...] = (r2 * (dmh - mh * jnp.mean(dmh * mh, axis=-1, keepdims=True))).astype(BF16)
        _acc_rows(acc_ref, [
            jnp.sum(dh, axis=0, keepdims=True),
            jnp.sum(dh * xn * g_ref[...], axis=0, keepdims=True),
            jnp.sum(dh * (1.0 + sc_ref[...]) * xn, axis=0, keepdims=True),
            jnp.sum(dx1 * mh * gp_ref[...], axis=0, keepdims=True),
            jnp.sum(dn * mh, axis=0, keepdims=True)])

    return _row_call("pre_mlp_post_mix_bwd", body, [dh2, x1, dout, mix], [g_pre, sc, g_post, gt], [F32, BF16],
                     [(8, D)], S, D, 128)


def _pre_mix_bwd(dh, x, dx1, g_pre, sc):
    S, D = x.shape

    def body(dh_ref, x_ref, dx1_ref, g_ref, sc_ref, gx_ref, acc_ref):
        dhv = dh_ref[...]
        xv = x_ref[...]
        r = _rstd(xv)
        xn = xv * r
        dxn = dhv * (1.0 + sc_ref[...]) * g_ref[...]
        gx_ref[...] = dx1_ref[...] + r * (dxn - xn * jnp.mean(dxn * xn, axis=-1, keepdims=True))
        _acc_rows(acc_ref, [
            jnp.sum(dhv, axis=0, keepdims=True),
            jnp.sum(dhv * xn * g_ref[...], axis=0, keepdims=True),
            jnp.sum(dhv * (1.0 + sc_ref[...]) * xn, axis=0, keepdims=True)])

    return _row_call("pre_mix_bwd", body, [dh, x, dx1], [g_pre, sc], [F32], [(8, D)], S, D, 128)


CUM_BLOCK = 256


def _tri(n, upper):
    r = lax.broadcasted_iota(jnp.int32, (n, n), 0)
    c = lax.broadcasted_iota(jnp.int32, (n, n), 1)
    return ((c >= r) if upper else (c <= r)).astype(F32)


def _fox_gate_fwd(fg, b_pad):
    S = fg.shape[0]
    cb = _fit(CUM_BLOCK, S)

    def body(fg_ref, b_ref, cumt_ref, cum_ref):
        low = _tri(cb, False)
        carry = jnp.zeros((1, LANES), F32)
        for n in range(S // cb):
            z = fg_ref[n * cb:(n + 1) * cb, :] + b_ref[...]
            logf = jnp.minimum(z, 0.0) - jnp.log(1.0 + jnp.exp(-jnp.abs(z)))
            blk = jnp.dot(low, logf, precision=lax.Precision.HIGHEST, preferred_element_type=F32) + carry
            cum_ref[n * cb:(n + 1) * cb, :] = blk
            carry = blk[cb - 1:cb, :]
        cumt_ref[...] = cum_ref[...].T

    return pl.pallas_call(
        body, name="fox_gate_fwd", out_shape=jax.ShapeDtypeStruct((LANES, S), F32),
        scratch_shapes=[pltpu.VMEM((S, LANES), F32)],
        compiler_params=pltpu.CompilerParams(vmem_limit_bytes=_vmem(6 * S * LANES * 4)),
    )(fg, b_pad)


def _fox_gate_bwd(dcum_k, dcum_q, fg, b_pad):
    S = fg.shape[0]
    n_fox = dcum_q.shape[0]
    cb = _fit(CUM_BLOCK, S)

    def body(dk_ref, dq_ref, fg_ref, b_ref, dfg_ref, db_ref, dc_ref):
        lane = lax.broadcasted_iota(jnp.int32, (S, LANES), 1)
        dc = dk_ref[...].T
        for h in range(n_fox):
            dc = dc + jnp.where(lane == h, dq_ref[h], 0.0)
        dc_ref[...] = dc
        up = _tri(cb, True)
        carry = jnp.zeros((1, LANES), F32)
        db = jnp.zeros((1, LANES), F32)
        for n in reversed(range(S // cb)):
            blk = jnp.dot(up, dc_ref[n * cb:(n + 1) * cb, :], precision=lax.Precision.HIGHEST,
                          preferred_element_type=F32) + carry
            carry = blk[0:1, :]
            z = fg_ref[n * cb:(n + 1) * cb, :] + b_ref[...]
            dfg = blk * (1.0 / (1.0 + jnp.exp(z)))
            dfg_ref[n * cb:(n + 1) * cb, :] = dfg.astype(BF16)
            db = db + jnp.sum(dfg, axis=0, keepdims=True)
        db_ref[...] = jnp.broadcast_to(db, db_ref.shape)

    return pl.pallas_call(
        body, name="fox_gate_bwd",
        out_shape=[jax.ShapeDtypeStruct((S, LANES), BF16), jax.ShapeDtypeStruct((8, LANES), F32)],
        scratch_shapes=[pltpu.VMEM((S, LANES), F32)],
        compiler_params=pltpu.CompilerParams(vmem_limit_bytes=_vmem((8 + 2 * n_fox) * S * LANES * 4)),
    )(dcum_k, dcum_q, fg, b_pad)


FOX_TILE = 512


LOG2E = 1.4426950408889634


def _fox_scores(q, k, ck2, masked, t):
    s = lax.dot_general(q, k, _NT, preferred_element_type=F32) * (HEAD_DIM ** -0.5 * LOG2E) - ck2
    if masked:
        row = lax.broadcasted_iota(jnp.int32, (t, t), 0)
        col = lax.broadcasted_iota(jnp.int32, (t, t), 1)
        s = jnp.where(col <= row, s, NEG)
    return s


def _fox_fwd(proj, cum_row, n_fox):
    S = proj.shape[0]
    t = _fit(FOX_TILE, S)
    nq = S // t

    def body(q_ref, k_ref, v_ref, ck_ref, o_ref, lse_ref):
        def q_block(qi, _):
            q0 = pl.multiple_of(qi * t, t)
            q = q_ref[pl.ds(q0, t), :]

            def kv_block(j, carry, masked):
                m, l, acc = carry
                k0 = pl.multiple_of(j * t, t)
                s = _fox_scores(q, k_ref[pl.ds(k0, t), :], ck_ref[0, :, pl.ds(k0, t)] * LOG2E, masked, t)
                m_new = jnp.maximum(m, jnp.max(s, axis=-1, keepdims=True))
                alpha = jnp.exp2(m - m_new)
                p = jnp.exp2(s - m_new)
                l = alpha * l + jnp.sum(p, axis=-1, keepdims=True)
                acc = alpha * acc + jnp.dot(p.astype(BF16), v_ref[pl.ds(k0, t), :], preferred_element_type=F32)
                return m_new, l, acc

            init = (jnp.full((t, 1), NEG, F32), jnp.zeros((t, 1), F32), jnp.zeros((t, HEAD_DIM), F32))
            carry = lax.fori_loop(0, qi, lambda j, cr: kv_block(j, cr, False), init)
            m, l, acc = kv_block(qi, carry, True)
            o_ref[pl.ds(q0, t), :] = acc / l
            lse_ref[0, pl.ds(q0, t), :] = jnp.broadcast_to(m + jnp.log(l) * LOG2E, (t, LANES))
            return 0

        lax.fori_loop(0, nq, q_block, 0)

    col = lambda off: pl.BlockSpec((S, HEAD_DIM), lambda h: (0, off + h))
    per_head = pl.BlockSpec((1, S, LANES), lambda h: (h, 0, 0))
    return pl.pallas_call(
        body, name="fox_fwd", grid=(n_fox,),
        in_specs=[col(0), col(n_fox), col(2 * n_fox), pl.BlockSpec((1, 1, S), lambda h: (h, 0, 0))],
        out_specs=[pl.BlockSpec((S, HEAD_DIM), lambda h: (0, h)), per_head],
        out_shape=[jax.ShapeDtypeStruct((S, n_fox * HEAD_DIM), F32), jax.ShapeDtypeStruct((n_fox, S, LANES), F32)],
        compiler_params=pltpu.CompilerParams(dimension_semantics=("parallel",),
                                             vmem_limit_bytes=_vmem(16 * S * HEAD_DIM * 4 + 12 * t * t * 4)),
    )(proj, proj, proj, cum_row)


def _fox_bwd(proj, o, do, lse_b, cum_row, n_fox):
    S = proj.shape[0]
    t = _fit(FOX_TILE, S)
    nq = S // t
    scale = HEAD_DIM ** -0.5

    def body(q_ref, k_ref, v_ref, o_ref, do_ref, lse_ref, ck_ref, dq_ref, dk_ref, dv_ref, dc_ref, dcq_ref,
             dq_acc, delta_ref):
        dq_acc[...] = jnp.zeros_like(dq_acc)
        dcq_ref[...] = jnp.zeros_like(dcq_ref)

        def delta_block(qi, _):
            q0 = pl.multiple_of(qi * t, t)
            d = jnp.sum(do_ref[pl.ds(q0, t), :] * o_ref[pl.ds(q0, t), :], axis=-1, keepdims=True)
            delta_ref[pl.ds(q0, t), :] = jnp.broadcast_to(d, (t, LANES))
            return 0

        lax.fori_loop(0, nq, delta_block, 0)

        def kv_block(j, _):
            k0 = pl.multiple_of(j * t, t)
            k = k_ref[pl.ds(k0, t), :]
            v = v_ref[pl.ds(k0, t), :]
            ck2 = ck_ref[0, :, pl.ds(k0, t)] * LOG2E

            def q_block(qi, carry, masked):
                dk, dv, dc = carry
                q0 = pl.multiple_of(qi * t, t)
                q = q_ref[pl.ds(q0, t), :]
                dov = do_ref[pl.ds(q0, t), :].astype(BF16)
                p = jnp.exp2(_fox_scores(q, k, ck2, masked, t) - lse_ref[0, pl.ds(q0, t), :][:, :1])
                dp = lax.dot_general(dov, v, _NT, preferred_element_type=F32)
                ds = p * (dp - delta_ref[pl.ds(q0, t), :][:, :1])
                dsb = ds.astype(BF16)
                dv = dv + lax.dot_general(p.astype(BF16), dov, _TN, preferred_element_type=F32)
                dk = dk + lax.dot_general(dsb, q, _TN, preferred_element_type=F32)
                dq_acc[pl.ds(q0, t), :] += jnp.dot(dsb, k, preferred_element_type=F32)
                dc = dc - jnp.sum(ds, axis=0, keepdims=True)
                dcq_ref[0, pl.ds(q0, t), :] += jnp.broadcast_to(jnp.sum(ds, axis=1, keepdims=True), (t, LANES))
                return dk, dv, dc

            init = (jnp.zeros((t, HEAD_DIM), F32), jnp.zeros((t, HEAD_DIM), F32), jnp.zeros((1, t), F32))
            carry = q_block(j, init, True)
            dk, dv, dc = lax.fori_loop(j + 1, nq, lambda qi, cr: q_block(qi, cr, False), carry)
            dk_ref[pl.ds(k0, t), :] = (dk * scale).astype(BF16)
            dv_ref[pl.ds(k0, t), :] = dv.astype(BF16)
            dc_ref[0, :, pl.ds(k0, t)] = dc
            return 0

        lax.fori_loop(0, nq, kv_block, 0)
        dq_ref[...] = (dq_acc[...] * scale).astype(BF16)

    col = lambda off: pl.BlockSpec((S, HEAD_DIM), lambda h: (0, off + h))
    per_head = pl.BlockSpec((1, S, LANES), lambda h: (h, 0, 0))
    row = pl.BlockSpec((1, 1, S), lambda h: (h, 0, 0))
    grad = jax.ShapeDtypeStruct((S, n_fox * HEAD_DIM), BF16)
    return pl.pallas_call(
        body, name="fox_bwd", grid=(n_fox,),
        in_specs=[col(0), col(n_fox), col(2 * n_fox), col(0), col(0), per_head, row],
        out_specs=[col(0), col(0), col(0), row, per_head],
        out_shape=[grad, grad, grad, jax.ShapeDtypeStruct((n_fox, 1, S), F32), jax.ShapeDtypeStruct((n_fox, S, LANES), F32)],
        scratch_shapes=[pltpu.VMEM((S, HEAD_DIM), F32), pltpu.VMEM((S, LANES), F32)],
        compiler_params=pltpu.CompilerParams(dimension_semantics=("parallel",),
                                             vmem_limit_bytes=_vmem(24 * S * HEAD_DIM * 4 + 16 * t * t * 4)),
    )(proj, proj, proj, o, do, lse_b, cum_row)


def _rope_tables(S):
    half = HEAD_DIM // 2
    inv_freq = 1.0 / (ROPE_THETA ** (jnp.arange(half, dtype=F32) * (2.0 / HEAD_DIM)))
    ang = jnp.arange(S).astype(F32)[:, None] * inv_freq[None, :]
    cos, sin = jnp.cos(ang), jnp.sin(ang)
    return jnp.concatenate([cos, cos], axis=-1), jnp.concatenate([-sin, sin], axis=-1)


def _rope(name, src, first_block, n_blocks, cos, sin_signed):
    S = src.shape[0]

    def body(x_ref, cos_ref, sin_ref, o_ref):
        xv = x_ref[...].astype(F32)
        o_ref[...] = (xv * cos_ref[...] + pltpu.roll(xv, HEAD_DIM // 2, 1) * sin_ref[...]).astype(BF16)

    table = pl.BlockSpec((S, HEAD_DIM), lambda n: (0, 0))
    return pl.pallas_call(
        body, name=name, grid=(n_blocks,),
        in_specs=[pl.BlockSpec((S, HEAD_DIM), lambda n: (0, first_block + n)), table, table],
        out_specs=pl.BlockSpec((S, HEAD_DIM), lambda n: (0, n)),
        out_shape=jax.ShapeDtypeStruct((S, n_blocks * HEAD_DIM), BF16),
        compiler_params=pltpu.CompilerParams(dimension_semantics=("parallel",),
                                             vmem_limit_bytes=_vmem(12 * S * HEAD_DIM * 4)),
    )(src, cos, sin_signed)


def _swa_tile(q_ref, kp_ref, kc_ref, n, group, scale):
    B = SWA_BLOCK
    qs = jnp.concatenate([q_ref[:, g * HEAD_DIM:(g + 1) * HEAD_DIM] for g in range(group)], axis=0)
    kcat = jnp.concatenate([kp_ref[...], kc_ref[...]], axis=0)
    s = lax.dot_general(qs, kcat, _NT, preferred_element_type=F32) * scale
    qi = lax.broadcasted_iota(jnp.int32, (group * B, 2 * B), 0) % B
    kj = lax.broadcasted_iota(jnp.int32, (group * B, 2 * B), 1)
    diff = qi + B - kj
    mask = (diff >= 0) & (diff < B) & ((n * B + kj - B) >= 0)
    return qs, kcat, jnp.where(mask, s, NEG)


def _swa_sink_col(sink_ref, kv, group):
    head = lax.broadcasted_iota(jnp.int32, (group * SWA_BLOCK, 1), 0) // SWA_BLOCK
    col = jnp.zeros((group * SWA_BLOCK, 1), F32)
    for g in range(group):
        col = jnp.where(head == g, sink_ref[kv * group + g], col)
    return col


def _swa_specs(n_kv, group, q_first, k_first, v_first):
    B = SWA_BLOCK
    prev = lambda n: jnp.maximum(n - 1, 0)
    return [
        pl.BlockSpec((B, group * HEAD_DIM), lambda kv, n: (n, q_first + kv)),
        pl.BlockSpec((B, HEAD_DIM), lambda kv, n: (prev(n), k_first + kv)),
        pl.BlockSpec((B, HEAD_DIM), lambda kv, n: (n, k_first + kv)),
        pl.BlockSpec((B, HEAD_DIM), lambda kv, n: (prev(n), v_first + kv)),
        pl.BlockSpec((B, HEAD_DIM), lambda kv, n: (n, v_first + kv)),
    ]


def _swa_fwd(rq, proj, v_first, sinks, n_q, n_kv):
    S = rq.shape[0]
    B = SWA_BLOCK
    group = n_q // n_kv
    scale = HEAD_DIM ** -0.5

    def body(q_ref, kp_ref, kc_ref, vp_ref, vc_ref, sink_ref, o_ref, lse_ref):
        kv, n = pl.program_id(0), pl.program_id(1)
        _, _, s = _swa_tile(q_ref, kp_ref, kc_ref, n, group, scale)
        sink = _swa_sink_col(sink_ref, kv, group)
        m = jnp.maximum(jnp.max(s, axis=-1, keepdims=True), sink)
        p = jnp.exp(s - m)
        denom = jnp.sum(p, axis=-1, keepdims=True) + jnp.exp(sink - m)
        vcat = jnp.concatenate([vp_ref[...], vc_ref[...]], axis=0)
        o = jnp.dot((p / denom).astype(BF16), vcat, preferred_element_type=F32)
        lse = m + jnp.log(denom)
        for g in range(group):
            o_ref[:, g * HEAD_DIM:(g + 1) * HEAD_DIM] = o[g * B:(g + 1) * B, :]
            lse_ref[0, :, g * LANES:(g + 1) * LANES] = jnp.broadcast_to(lse[g * B:(g + 1) * B, :], (B, LANES))

    specs = _swa_specs(n_kv, group, 0, n_q, v_first)
    q_blk = pl.BlockSpec((B, group * HEAD_DIM), lambda kv, n: (n, kv))
    return pl.pallas_call(
        body, name="swa_fwd", grid=(n_kv, S // B),
        in_specs=specs + [pl.BlockSpec(memory_space=pltpu.SMEM)],
        out_specs=[q_blk, pl.BlockSpec((1, B, group * LANES), lambda kv, n: (kv, n, 0))],
        out_shape=[jax.ShapeDtypeStruct((S, n_q * HEAD_DIM), F32), jax.ShapeDtypeStruct((n_kv, S, group * LANES), F32)],
        compiler_params=pltpu.CompilerParams(dimension_semantics=("parallel", "arbitrary")),
    )(rq, rq, rq, proj, proj, sinks)


def _swa_bwd(rq, proj, v_first, sinks, o, do, do_first, lse_b, n_q, n_kv):
    S = rq.shape[0]
    B = SWA_BLOCK
    group = n_q // n_kv
    scale = HEAD_DIM ** -0.5

    def body(q_ref, kp_ref, kc_ref, vp_ref, vc_ref, o_ref, do_ref, lse_ref, sink_ref,
             dq_ref, dk_ref, dv_ref, dsink_ref):
        kv, n = pl.program_id(0), pl.program_id(1)

        @pl.when(n == 0)
        def _():
            dk_ref[...] = jnp.zeros_like(dk_ref)
            dv_ref[...] = jnp.zeros_like(dv_ref)
            dsink_ref[...] = jnp.zeros_like(dsink_ref)

        qs, kcat, s = _swa_tile(q_ref, kp_ref, kc_ref, n, group, scale)
        sink = _swa_sink_col(sink_ref, kv, group)
        stack = lambda ref, w: jnp.concatenate([ref[:, g * w:(g + 1) * w] for g in range(group)], axis=0)
        lse = jnp.concatenate([lse_ref[0, :, g * LANES:g * LANES + 1] for g in range(group)], axis=0)
        do32 = stack(do_ref, HEAD_DIM)
        delta = jnp.sum(do32 * stack(o_ref, HEAD_DIM), axis=-1, keepdims=True)
        dov = do32.astype(BF16)
        p = jnp.exp(s - lse)
        vcat = jnp.concatenate([vp_ref[...], vc_ref[...]], axis=0)
        dp = lax.dot_general(dov, vcat, _NT, preferred_element_type=F32)
        ds = p * (dp - delta)
        dsb = ds.astype(BF16)
        dq = jnp.dot(dsb, kcat, preferred_element_type=F32) * scale
        for g in range(group):
            dq_ref[:, g * HEAD_DIM:(g + 1) * HEAD_DIM] = dq[g * B:(g + 1) * B, :].astype(BF16)
        dkcat = lax.dot_general(dsb, qs, _TN, preferred_element_type=F32) * scale
        dvcat = lax.dot_general(p.astype(BF16), dov, _TN, preferred_element_type=F32)
        prev0 = pl.multiple_of(jnp.maximum(n - 1, 0) * B, B)
        cur0 = pl.multiple_of(n * B, B)
        dk_ref[0, pl.ds(prev0, B), :] += dkcat[:B, :]
        dk_ref[0, pl.ds(cur0, B), :] += dkcat[B:, :]
        dv_ref[0, pl.ds(prev0, B), :] += dvcat[:B, :]
        dv_ref[0, pl.ds(cur0, B), :] += dvcat[B:, :]
        dsk = -jnp.exp(sink - lse) * delta
        lane = lax.broadcasted_iota(jnp.int32, (1, LANES), 1)
        row = jnp.zeros((1, LANES), F32)
        for g in range(group):
            row = row + jnp.where(lane == g, jnp.sum(dsk[g * B:(g + 1) * B, :]), 0.0)
        dsink_ref[0, 0:1, :] += row

    specs = _swa_specs(n_kv, group, 0, n_q, v_first)
    q_blk = pl.BlockSpec((B, group * HEAD_DIM), lambda kv, n: (n, kv))
    acc = pl.BlockSpec((1, S, HEAD_DIM), lambda kv, n: (kv, 0, 0))
    return pl.pallas_call(
        body, name="swa_bwd", grid=(n_kv, S // B),
        in_specs=specs + [q_blk, pl.BlockSpec((B, group * HEAD_DIM), lambda kv, n: (n, do_first + kv)),
                          pl.BlockSpec((1, B, group * LANES), lambda kv, n: (kv, n, 0)),
                          pl.BlockSpec(memory_space=pltpu.SMEM)],
        out_specs=[q_blk, acc, acc, pl.BlockSpec((1, 8, LANES), lambda kv, n: (kv, 0, 0))],
        out_shape=[jax.ShapeDtypeStruct((S, n_q * HEAD_DIM), BF16), jax.ShapeDtypeStruct((n_kv, S, HEAD_DIM), F32),
                   jax.ShapeDtypeStruct((n_kv, S, HEAD_DIM), F32), jax.ShapeDtypeStruct((n_kv, 8, LANES), F32)],
        compiler_params=pltpu.CompilerParams(dimension_semantics=("parallel", "arbitrary")),
    )(rq, rq, rq, proj, proj, o, do, lse_b, sinks)


def _adamw(w, g, m, v):
    m = ADAM_B1 * m + (1.0 - ADAM_B1) * g
    v = ADAM_B2 * v + (1.0 - ADAM_B2) * (g * g)
    m_hat = m / (1.0 - ADAM_B1 ** ADAM_STEP)
    v_hat = v / (1.0 - ADAM_B2 ** ADAM_STEP)
    delta = -ADAM_LR * (m_hat / (jnp.sqrt(v_hat) + ADAM_EPS) + ADAM_WD * w)
    return delta, m, v


def _mod_fwd(cond_in, w_mod, b_shard):
    R, D = cond_in.shape
    cols = w_mod.shape[1]
    tn = _fit(512, cols)

    def body(c_ref, w_ref, b_ref, o_ref):
        cv = c_ref[...]
        cond = (cv / (1.0 + jnp.exp(-cv))).astype(BF16)
        o_ref[...] = jnp.dot(cond, w_ref[...].astype(BF16), preferred_element_type=F32) + b_ref[...]

    return pl.pallas_call(
        body, name="mod_fwd", grid=(cols // tn,),
        in_specs=[pl.BlockSpec((R, D), lambda j: (0, 0)), pl.BlockSpec((D, tn), lambda j: (0, j)),
                  pl.BlockSpec((1, tn), lambda j: (0, j))],
        out_specs=pl.BlockSpec((R, tn), lambda j: (0, j)),
        out_shape=jax.ShapeDtypeStruct((R, cols), F32),
        compiler_params=pltpu.CompilerParams(dimension_semantics=("parallel",), vmem_limit_bytes=_vmem(3 * D * tn * 4)),
    )(cond_in, w_mod, b_shard)


def _mod_update(c_t, dmod, w, m, v):
    D, nb = c_t.shape
    cols = w.shape[1]
    tn = _fit(256, cols)

    def body(c_ref, d_ref, w_ref, m_ref, v_ref, g_ref, dl_ref, nm_ref, nv_ref):
        cv = c_ref[...]
        cond = cv / (1.0 + jnp.exp(-cv))
        g = jnp.zeros((D, tn), F32)
        for b in range(nb):
            g = g + cond[:, b:b + 1] * d_ref[b:b + 1, :]
        g_ref[...] = g
        dl_ref[...], nm_ref[...], nv_ref[...] = _adamw(w_ref[...], g, m_ref[...], v_ref[...])

    blk = pl.BlockSpec((D, tn), lambda j: (0, j))
    out = jax.ShapeDtypeStruct((D, cols), F32)
    return pl.pallas_call(
        body, name="mod_update", grid=(cols // tn,),
        in_specs=[pl.BlockSpec((D, nb), lambda j: (0, 0)), pl.BlockSpec((nb, tn), lambda j: (0, j)), blk, blk, blk],
        out_specs=[blk] * 4, out_shape=[out] * 4,
        compiler_params=pltpu.CompilerParams(dimension_semantics=("parallel",), vmem_limit_bytes=_vmem(18 * D * tn * 4)),
    )(c_t, dmod, w, m, v)


def _small_update(stacked, w, m, v):
    R, C = w.shape

    def body(s_ref, w_ref, m_ref, v_ref, g_ref, dl_ref, nm_ref, nv_ref):
        g = s_ref[0:R, :]
        for d in range(1, N_DEV):
            g = g + s_ref[d * R:(d + 1) * R, :]
        g_ref[...] = g
        dl_ref[...], nm_ref[...], nv_ref[...] = _adamw(w_ref[...], g, m_ref[...], v_ref[...])

    return pl.pallas_call(body, name="small_update", out_shape=[jax.ShapeDtypeStruct((R, C), F32)] * 4)(stacked, w, m, v)


def _place():
    return lax.axis_index("x"), lax.axis_index("y"), lax.axis_index("c")


def _allgather8(name, block):
    m_per, n = block.shape

    def body(x_ref, out_ref, token_ref, send_sems, recv_sems, local_sem):
        token_ref[...] = jnp.zeros_like(token_ref)
        x, y, c = _place()
        me, sibling = (x, y, c), (x, y, 1 - c)
        chips = [(1 - x, y), (x, 1 - y), (1 - x, 1 - y)]

        def rows(px, py, pc):
            return out_ref.at[pl.ds((4 * px + 2 * py + pc) * m_per, m_per), :]

        def copy(k, blk, to, src=None):
            return pltpu.make_async_remote_copy(
                src_ref=rows(*blk) if src is None else src, dst_ref=rows(*blk),
                send_sem=send_sems.at[k], recv_sem=recv_sems.at[k], device_id=to, device_id_type=MESH)

        mine = pltpu.make_async_copy(x_ref, rows(*me), local_sem)
        mine.start()
        first = [copy(0, me, sibling, src=x_ref)]
        first += [copy(1 + j, me, (*chip, c), src=x_ref) for j, chip in enumerate(chips)]
        for cp in first:
            cp.start()
        passed = [copy(4 + j, (*chip, c), sibling) for j, chip in enumerate(chips)]
        for j, chip in enumerate(chips):
            copy(1 + j, (*chip, c), me).wait_recv()
            passed[j].start()
        copy(0, sibling, me).wait_recv()
        for j, chip in enumerate(chips):
            copy(4 + j, (*chip, 1 - c), me).wait_recv()
        for cp in first + passed:
            cp.wait_send()
        mine.wait()

    vmem = pl.BlockSpec(memory_space=pltpu.VMEM)
    return pl.pallas_call(
        body, name=name,
        out_shape=[jax.ShapeDtypeStruct((N_DEV * m_per, n), block.dtype), jax.ShapeDtypeStruct((8, LANES), F32)],
        in_specs=[vmem], out_specs=[vmem, vmem],
        scratch_shapes=[pltpu.SemaphoreType.DMA((7,)), pltpu.SemaphoreType.DMA((7,)), pltpu.SemaphoreType.DMA],
    )(block)


_ANY = pl.BlockSpec(memory_space=pl.ANY)


def _half(ref, c, rows):
    return ref.at[pl.ds(c * (rows // 2), rows // 2), :]


_HBM = pl.BlockSpec(memory_space=pltpu.HBM)
_SEM = pl.BlockSpec(memory_space=pltpu.SEMAPHORE)
_EFFECT = pltpu.SideEffectType.DATAFLOW_SIDE_EFFECTING


def _ici_start(name, srcs, land_shapes, plan):
    ns, nl = len(srcs), len(land_shapes)
    n_copies = 3 * ns

    def body(*refs):
        src_refs, land_refs = refs[:ns], refs[ns:ns + nl]
        send_sems, recv_sems = refs[ns + nl], refs[ns + nl + 1]
        token = refs[-1]
        for n, (src, dst, peer, _) in enumerate(plan(src_refs, land_refs)):
            pltpu.make_async_remote_copy(src_ref=src, dst_ref=dst, send_sem=send_sems.at[n], recv_sem=recv_sems.at[n],
                                         device_id=peer, device_id_type=MESH).start()
        token[...] = jnp.zeros_like(token)

    lands = [lax.empty(s.shape, s.dtype) for s in land_shapes]
    out = pl.pallas_call(
        body, name=name,
        out_shape=(pltpu.SemaphoreType.DMA((n_copies,)), pltpu.SemaphoreType.DMA((n_copies,)),
                   *[pltpu.HBM(a.shape, a.dtype) for a in list(srcs) + lands], jax.ShapeDtypeStruct((8, LANES), F32)),
        in_specs=[_HBM] * (ns + nl),
        out_specs=(_SEM, _SEM, *[_HBM] * (ns + nl), pl.BlockSpec(memory_space=pltpu.VMEM)),
        input_output_aliases={n: 2 + n for n in range(ns + nl)},
        compiler_params=pltpu.CompilerParams(has_side_effects=_EFFECT),
    )(*[pltpu.with_memory_space_constraint(a, pltpu.HBM) for a in list(srcs) + lands])
    return out[0], out[1], list(out[2:2 + ns]), list(out[2 + ns:2 + ns + nl]), out[-1]


def _ici_wait(name, send_sems, recv_sems, srcs, lands, plan, after):
    ns, nl = len(srcs), len(lands)

    def body(*refs):
        src_refs, land_refs = refs[:ns], refs[ns:ns + nl]
        send_sems, recv_sems = refs[ns + nl], refs[ns + nl + 1]
        for n, (src, _, peer, mine) in enumerate(plan(src_refs, land_refs)):
            cp = pltpu.make_async_remote_copy(src_ref=src, dst_ref=mine, send_sem=send_sems.at[n],
                                              recv_sem=recv_sems.at[n], device_id=peer, device_id_type=MESH)
            cp.wait_send()
            cp.wait_recv()

    out = pl.pallas_call(
        body, name=name, out_shape=[pltpu.HBM(a.shape, a.dtype) for a in list(srcs) + list(lands)],
        in_specs=[_HBM] * (ns + nl) + [_SEM, _SEM, _ANY], out_specs=[_HBM] * (ns + nl),
        input_output_aliases={n: n for n in range(ns + nl)},
        compiler_params=pltpu.CompilerParams(has_side_effects=_EFFECT),
    )(*srcs, *lands, send_sems, recv_sems, after)
    return list(out[:ns]), list(out[ns:])


def _gather_plan(src_refs, land_refs):
    x, y, c = _place()
    copies = []
    for w, land in zip(src_refs, land_refs):
        R = w.shape[0]
        for cx, cy in [(1 - x, y), (x, 1 - y), (1 - x, 1 - y)]:
            copies.append((_half(w, c, R), _half(land.at[2 * x + y], c, R), (cx, cy, c),
                           _half(land.at[2 * cx + cy], c, R)))
    return copies


def _pass_to_sibling(name, lands):
    nw = len(lands)

    def body(*refs):
        ins, outs = refs[:nw], refs[nw:2 * nw]
        send_sems, recv_sems = refs[2 * nw:]
        x, y, c = _place()
        chips = [(1 - x, y), (x, 1 - y), (1 - x, 1 - y)]
        copies = []
        for k in range(nw):
            R = ins[k].shape[1]
            for j, (cx, cy) in enumerate(chips):
                cp = pltpu.make_async_remote_copy(
                    src_ref=_half(ins[k].at[2 * cx + cy], c, R), dst_ref=_half(outs[k].at[2 * cx + cy], c, R),
                    send_sem=send_sems.at[3 * k + j], recv_sem=recv_sems.at[3 * k + j],
                    device_id=(x, y, 1 - c), device_id_type=MESH)
                cp.start()
                copies.append(cp)
        for k in range(nw):
            R = ins[k].shape[1]
            for j, (cx, cy) in enumerate(chips):
                pltpu.make_async_remote_copy(
                    src_ref=_half(ins[k].at[2 * cx + cy], c, R), dst_ref=_half(outs[k].at[2 * cx + cy], 1 - c, R),
                    send_sem=send_sems.at[3 * k + j], recv_sem=recv_sems.at[3 * k + j],
                    device_id=(x, y, 1 - c), device_id_type=MESH).wait_recv()
        for cp in copies:
            cp.wait_send()

    return pl.pallas_call(
        body, name=name, out_shape=[jax.ShapeDtypeStruct(a.shape, a.dtype) for a in lands],
        in_specs=[_ANY] * nw, out_specs=[_ANY] * nw, input_output_aliases={k: k for k in range(nw)},
        scratch_shapes=[pltpu.SemaphoreType.DMA((3 * nw,)), pltpu.SemaphoreType.DMA((3 * nw,))],
    )(*lands)


def _tie(vec, token):
    return vec + token[0:1, 0:1]


def _pair_exchange(name, grads):
    nw = len(grads)

    def body(*refs):
        gs, outs = refs[:nw], refs[nw:2 * nw]
        send_sems, recv_sems = refs[2 * nw:]
        x, y, c = _place()
        copies = []
        for k in range(nw):
            half = gs[k].shape[1] // 2
            cp = pltpu.make_async_remote_copy(
                src_ref=gs[k].at[:, pl.ds((1 - c) * half, half), :], dst_ref=outs[k],
                send_sem=send_sems.at[k], recv_sem=recv_sems.at[k], device_id=(x, y, 1 - c), device_id_type=MESH)
            cp.start()
            copies.append(cp)
        for cp in copies:
            cp.wait()

    return pl.pallas_call(
        body, name=name,
        out_shape=[jax.ShapeDtypeStruct((N_CHIPS, g.shape[1] // 2, g.shape[2]), g.dtype) for g in grads],
        in_specs=[_ANY] * nw, out_specs=[_ANY] * nw,
        scratch_shapes=[pltpu.SemaphoreType.DMA((nw,)), pltpu.SemaphoreType.DMA((nw,))],
    )(*grads)


ROW_ALIGN = 16
TILE_ELEMS = 512 * 1024


def _tiles(rows, cols):
    fits = [t for t in range(ROW_ALIGN, min(rows, 256) + 1, ROW_ALIGN) if rows % t == 0]
    tr = fits[-1] if fits and fits[-1] >= 64 else rows
    tc = cols
    while tr * tc > TILE_ELEMS and tc % (2 * LANES) == 0:
        tc //= 2
    return tr, tc


def _pair_add(name, core, grad, recv):
    n, R, C = grad.shape
    half = R // 2
    tr, tc = _tiles(half, C)
    nr = half // tr

    def body(core_ref, g_ref, r_ref, o_ref):
        o_ref[...] = (g_ref[...].astype(F32) + r_ref[...].astype(F32)).astype(BF16)

    grid_spec = pltpu.PrefetchScalarGridSpec(
        num_scalar_prefetch=1, grid=(n, nr, C // tc),
        in_specs=[pl.BlockSpec((1, tr, tc), lambda s, r, q, core_ref: (s, core_ref[0] * nr + r, q)),
                  pl.BlockSpec((1, tr, tc), lambda s, r, q, core_ref: (s, r, q))],
        out_specs=pl.BlockSpec((1, tr, tc), lambda s, r, q, core_ref: (s, r, q)))
    return pl.pallas_call(
        body, name=name, grid_spec=grid_spec, out_shape=jax.ShapeDtypeStruct((n, half, C), BF16),
        compiler_params=pltpu.CompilerParams(dimension_semantics=("parallel", "parallel", "parallel")),
    )(core, grad, recv)


def _scatter_plan(src_refs, land_refs):
    x, y, c = _place()
    copies = []
    for p, land in zip(src_refs, land_refs):
        for j, (cx, cy) in enumerate([(1 - x, y), (x, 1 - y), (1 - x, 1 - y)]):
            copies.append((p.at[2 * cx + cy], land.at[j], (cx, cy, c), land.at[j]))
    return copies


def _chip_add(name, chip, sums, recv):
    _, H, C = sums.shape
    tr, tc = _tiles(H, C)

    def body(chip_ref, p_ref, r_ref, o_ref):
        total = p_ref[0].astype(F32)
        for j in range(3):
            total = total + r_ref[j].astype(F32)
        o_ref[...] = total

    grid_spec = pltpu.PrefetchScalarGridSpec(
        num_scalar_prefetch=1, grid=(H // tr, C // tc),
        in_specs=[pl.BlockSpec((1, tr, tc), lambda r, q, chip_ref: (chip_ref[0], r, q)),
                  pl.BlockSpec((3, tr, tc), lambda r, q, chip_ref: (0, r, q))],
        out_specs=pl.BlockSpec((tr, tc), lambda r, q, chip_ref: (r, q)))
    return pl.pallas_call(
        body, name=name, grid_spec=grid_spec, out_shape=jax.ShapeDtypeStruct((H, C), F32),
        compiler_params=pltpu.CompilerParams(dimension_semantics=("parallel", "parallel")),
    )(chip, sums, recv)


def _pair_share(name, halves):
    nw = len(halves)

    def body(*refs):
        hs, outs = refs[:nw], refs[nw:2 * nw]
        send_sems, recv_sems = refs[2 * nw:]
        x, y, c = _place()
        copies = []
        for k in range(nw):
            cp = pltpu.make_async_remote_copy(
                src_ref=hs[k], dst_ref=outs[k], send_sem=send_sems.at[k], recv_sem=recv_sems.at[k],
                device_id=(x, y, 1 - c), device_id_type=MESH)
            cp.start()
            copies.append(cp)
        for cp in copies:
            cp.wait()

    return pl.pallas_call(
        body, name=name,
        out_shape=[jax.ShapeDtypeStruct(h.shape, h.dtype) for h in halves],
        in_specs=[_ANY] * nw, out_specs=[_ANY] * nw,
        scratch_shapes=[pltpu.SemaphoreType.DMA((nw,)), pltpu.SemaphoreType.DMA((nw,))],
    )(*halves)


def _adam_halves(name, core, w, g_own, g_other, m, v):
    R, C = w.shape
    H = R // 2
    tr, tc = _tiles(H, C)
    nr, nc = H // tr, C // tc

    def body(core_ref, w_ref, go_ref, gr_ref, m_ref, v_ref, g_ref, dl_ref, nm_ref, nv_ref):
        own = (pl.program_id(0) // nr) == core_ref[0]
        g = jnp.where(own, go_ref[...], gr_ref[...])
        g_ref[...] = g
        dl_ref[...], nm_ref[...], nv_ref[...] = _adamw(w_ref[...], g, m_ref[...], v_ref[...])

    blk = pl.BlockSpec((tr, tc), lambda r, q, core_ref: (r, q))

    def half_spec(is_own):
        def index(r, q, core_ref):
            mine = ((r // nr) == core_ref[0]) == is_own
            done = is_own == (core_ref[0] == 0)
            return (jnp.where(mine, r % nr, jnp.where(done, nr - 1, 0)), jnp.where(mine, q, jnp.where(done, nc - 1, 0)))
        return pl.BlockSpec((tr, tc), index)
    out = jax.ShapeDtypeStruct((R, C), F32)
    grid_spec = pltpu.PrefetchScalarGridSpec(
        num_scalar_prefetch=1, grid=(R // tr, nc), in_specs=[blk, half_spec(True), half_spec(False), blk, blk],
        out_specs=[blk] * 4)
    return pl.pallas_call(
        body, name=name, grid_spec=grid_spec, out_shape=[out] * 4,
        compiler_params=pltpu.CompilerParams(dimension_semantics=("parallel", "parallel"),
                                             vmem_limit_bytes=_vmem(20 * tr * tc * 4)),
    )(core, w, g_own, g_other, m, v)


def kernel(x, c, w_mod, b_mod, g_pre_mix, g_post_mix, w_in, b_forget, swa_sinks, w_out, g_pre_mlp, g_post_mlp, w_up, w_down, loss_target, m_w_mod, m_b_mod, m_g_pre_mix, m_g_post_mix, m_w_in, m_b_forget, m_swa_sinks, m_w_out, m_g_pre_mlp, m_g_post_mlp, m_w_up, m_w_down, v_w_mod, v_b_mod, v_g_pre_mix, v_g_post_mix, v_w_in, v_b_forget, v_swa_sinks, v_w_out, v_g_pre_mlp, v_g_post_mlp, v_w_up, v_w_down):
    S, D = x.shape[1], x.shape[2]
    n_heads = D // HEAD_DIM
    n_fox = n_heads // 2
    n_swa = n_heads - n_fox
    n_kv = max(1, n_swa // 4)
    fox_w, swa_w, kv_w = n_fox * HEAD_DIM, n_swa * HEAD_DIM, n_kv * HEAD_DIM
    main_w = 3 * fox_w + swa_w + 2 * kv_w
    in_w = main_w + n_fox
    mod_cols = w_mod.shape[2]

    ax, ay, ac = _place()
    chip = 2 * ax + ay
    dev = 2 * chip + ac
    chip_arr = jnp.reshape(chip, (1,)).astype(jnp.int32)
    core_arr = jnp.reshape(ac, (1,)).astype(jnp.int32)

    x2, tgt = x[0], loss_target[0]

    c_all, _ = _allgather8("gather_c", c.reshape(8, D // 8))
    c_all = c_all.reshape(N_DEV, D)
    b_shard = lax.dynamic_slice_in_dim(b_mod, chip * mod_cols, mod_cols, axis=1)
    mod_shard = _mod_fwd(jnp.pad(c_all, ((0, 16 - N_DEV), (0, 0))), w_mod[0], b_shard)[:N_DEV]
    mod_all, token = _allgather8("gather_mod", mod_shard)
    mod_all = mod_all.reshape(N_CHIPS, 2, N_DEV, mod_cols)[:, 0]
    mod = lax.dynamic_index_in_dim(mod_all, dev, axis=1, keepdims=False).reshape(N_MOD, 1, D)
    sh_a, sc_a, gt_a, sh_m, sc_m, gt_m = [mod[n] for n in range(N_MOD)]

    in_rows = in_w // N_CHIPS
    in_rows_pad = -(-in_rows // (2 * ROW_ALIGN)) * (2 * ROW_ALIGN)

    def rows_of(a):
        return jnp.pad(a[0].T, ((0, in_rows_pad - in_rows), (0, 0)))

    names = ["w_in", "w_out", "w_up", "w_down"]
    flights = {}
    for n, w in zip(names, [rows_of(w_in), w_out[0], w_up[0], w_down[0]]):
        shard = _tie(w, token).astype(BF16)
        flights[n] = _ici_start("gather_start_" + n, [shard], [jax.ShapeDtypeStruct((N_CHIPS,) + shard.shape, BF16)],
                                _gather_plan)
        token = flights[n][4]
    sc_a = _tie(sc_a, token)

    def gathered(n, after):
        send, recv, srcs, lands, _ = flights[n]
        srcs, lands = _ici_wait("gather_wait_" + n, send, recv, srcs, lands, _gather_plan, after)
        lands = _pass_to_sibling("gather_pass_" + n, lands)
        return lax.dynamic_update_index_in_dim(lands[0], srcs[0], chip, 0)

    d_ff = N_CHIPS * w_up.shape[2]

    h = _pre_norm(x2, g_pre_mix, sc_a, sh_a)
    w_in_t = gathered("w_in", h)[:, :in_rows].reshape(in_w, D)
    w_main_t = jnp.concatenate([w_in_t[:3 * fox_w], w_in_t[3 * fox_w + n_fox:]], axis=0)
    w_fg_t = jnp.pad(w_in_t[3 * fox_w:3 * fox_w + n_fox], ((0, LANES - n_fox), (0, 0)))
    proj = _mm_plain("in_proj", h, w_main_t, "nt", BF16, tn=_fit(768, main_w))
    fg = _mm_plain("in_proj_gate", h, w_fg_t, "nt", F32)
    b_pad = jnp.pad(b_forget, ((0, 0), (0, LANES - n_fox)))
    cum_row = _fox_gate_fwd(fg, b_pad)[:n_fox].reshape(n_fox, 1, S)
    fox_o, fox_lse = _fox_fwd(proj, cum_row, n_fox)

    cos, sin_signed = _rope_tables(S)
    rq = _rope("rope_fwd", proj, 3 * n_fox, n_swa + n_kv, cos, sin_signed)
    v_first = 3 * n_fox + n_swa + n_kv
    sinks = swa_sinks[0]
    swa_o, swa_lse = _swa_fwd(rq, proj, v_first, sinks, n_swa, n_kv)

    mixcat = jnp.concatenate([fox_o, swa_o], axis=1).astype(BF16)
    w_out_f = gathered("w_out", mixcat).reshape(D, D)
    mix = _mm_plain("out_proj", mixcat, w_out_f, "nn", F32)
    x1, h2 = _post_mix(x2, mix, g_post_mix, gt_a, g_pre_mlp, sc_m, sh_m)
    w_up_f = jnp.transpose(gathered("w_up", h2), (1, 0, 2)).reshape(D, d_ff)

    tm_u, tn_u = _fit(MM_TM, S), _fit(MM_TN, d_ff)

    def up_epilogue(acc, ex, outs):
        outs[0][...] = acc.astype(BF16)
        r = jnp.maximum(acc, 0.0)
        outs[1][...] = (r * r).astype(BF16)

    ublk = ((S, d_ff), BF16, (tm_u, tn_u), lambda i, j: (i, j))
    u, a = _matmul("mlp_up", h2, w_up_f, "nn", [ublk, ublk], up_epilogue)
    w_down_f = gathered("w_down", a).reshape(d_ff, D)
    y = _mm_plain("mlp_down", a, w_down_f, "nn", F32)

    dy, dout, loss_part, acc_mlp_post = _loss_and_post_mlp_bwd(x1, y, tgt, g_post_mlp, gt_m)
    loss = lax.psum(loss_part[0, 0], ("x", "y", "c"))

    def du_epilogue(acc, ex, outs):
        outs[0][...] = (acc * (2.0 * jnp.maximum(ex[0][...].astype(F32), 0.0))).astype(BF16)

    du = _matmul("mlp_down_bwd", dy, w_down_f, "nt", [ublk], du_epilogue,
                 extras=[(u, (tm_u, tn_u), lambda i, j: (i, j))])[0]
    g_down = _mm_plain("grad_w_down", a, dy, "tn", BF16)
    tn_s = _fit(MM_TN, w_up.shape[2])
    per = w_up.shape[2] // tn_s

    def shard_epilogue(acc, ex, outs):
        outs[0][0] = acc.astype(BF16)

    g_up = _matmul("grad_w_up", h2, du, "tn",
                   [((N_CHIPS, D, w_up.shape[2]), BF16, (1, _fit(MM_TM, D), tn_s), lambda i, j: (j // per, i, j % per))],
                   shard_epilogue, tn=tn_s)[0]

    def reduce_start(tag, fulls):
        from_sibling = _pair_exchange("grad_pair_exchange_" + tag, fulls)
        sums = [_pair_add("pair_add_%s_%d" % (tag, k), core_arr, g, r) for k, (g, r) in enumerate(zip(fulls, from_sibling))]
        return _ici_start("grad_scatter_start_" + tag, sums,
                          [jax.ShapeDtypeStruct((3,) + p.shape[1:], BF16) for p in sums], _scatter_plan)

    def reduce_finish(tag, flight, after):
        send, recv, srcs, lands, _ = flight
        sums, received = _ici_wait("grad_scatter_wait_" + tag, send, recv, srcs, lands, _scatter_plan, after)
        halves = [_chip_add("chip_add_%s_%d" % (tag, k), chip_arr, p, r) for k, (p, r) in enumerate(zip(sums, received))]
        return halves, _pair_share("grad_pair_share_" + tag, halves)

    flight_mlp = reduce_start("mlp", [g_up, g_down.reshape(N_CHIPS, d_ff // N_CHIPS, D)])
    dh2 = _mm_plain("mlp_up_bwd", du, w_up_f, "nt", F32)
    dx1, dmix, acc_mid = _pre_mlp_and_post_mix_bwd(dh2, x1, dout, mix, _tie(g_pre_mlp, flight_mlp[4]), sc_m,
                                                   g_post_mix, gt_a)

    dmixcat = _mm_plain("out_proj_bwd", dmix, w_out_f, "nt", F32)
    g_out = _mm_plain("grad_w_out", mixcat, dmix, "tn", BF16)

    fdq, fdk, fdv, dcum_row, dcum_q = _fox_bwd(proj, fox_o, dmixcat, fox_lse, cum_row, n_fox)
    dcum_k = jnp.pad(dcum_row.reshape(n_fox, S), ((0, LANES - n_fox), (0, 0)))
    dfg, db_forget = _fox_gate_bwd(dcum_k, dcum_q, fg, b_pad)

    group_w = (n_swa // n_kv) * HEAD_DIM
    sdq, sdk, sdv, dsink = _swa_bwd(rq, proj, v_first, sinks, swa_o, dmixcat, fox_w // group_w, swa_lse, n_swa, n_kv)
    drq = jnp.concatenate([sdq, jnp.transpose(sdk, (1, 0, 2)).reshape(S, kv_w).astype(BF16)], axis=1)
    d_sq_sk = _rope("rope_bwd", drq, 0, n_swa + n_kv, cos, -sin_signed)
    dsv = jnp.transpose(sdv, (1, 0, 2)).reshape(S, kv_w).astype(BF16)
    dproj = jnp.concatenate([fdq, fdk, fdv, d_sq_sk, dsv], axis=1)

    g_main_t = _mm_plain("grad_w_in", dproj, h, "tn", BF16, tm=_fit(768, main_w))
    g_fg_t = _mm_plain("grad_w_in_gate", dfg, h, "tn", BF16)
    dh_gate = _mm_plain("in_proj_gate_bwd", dfg, w_fg_t, "nn", F32)

    def add_epilogue(acc, ex, outs):
        outs[0][...] = acc + ex[0][...]

    tm_h, tn_h = _fit(MM_TM, S), _fit(MM_TN, D)
    dh = _matmul("in_proj_bwd", dproj, w_main_t, "nn", [((S, D), F32, (tm_h, tn_h), lambda i, j: (i, j))], add_epilogue,
                 extras=[(dh_gate, (tm_h, tn_h), lambda i, j: (i, j))], tk=_fit(2304, main_w))[0]
    grad_x, acc_pre = _pre_mix_bwd(dh, x2, dx1, g_pre_mix, sc_a)

    zero_row = jnp.zeros((1, D), F32)
    tail = jnp.concatenate([db_forget[0:1, :n_fox], dsink[:, 0, :n_swa // n_kv].reshape(1, n_swa),
                            jnp.zeros((1, D - n_fox - n_swa), F32)], axis=1)
    partial = jnp.concatenate([
        acc_pre[0:1], acc_pre[1:2], acc_mid[3:4], acc_mid[0:1], acc_mid[1:2], acc_mlp_post[0:1],
        acc_pre[2:3], acc_mid[4:5], acc_mid[2:3], acc_mlp_post[1:2], tail] + [zero_row] * 5, axis=0)
    gathered_small, token = _allgather8("gather_small_grads", partial)

    g_fg_tied = _tie(g_fg_t[:n_fox].astype(F32), token).astype(BF16)
    g_in_t = jnp.concatenate([g_main_t[:3 * fox_w], g_fg_tied, g_main_t[3 * fox_w:]], axis=0)
    g_in_t = jnp.pad(g_in_t.reshape(N_CHIPS, in_rows, D), ((0, 0), (0, in_rows_pad - in_rows), (0, 0)))
    flight_mix = reduce_start("mix", [g_in_t, g_out.reshape(N_CHIPS, D // N_CHIPS, D)])
    halves_mlp, others_mlp = reduce_finish("mlp", flight_mlp, flight_mix[4])

    def pack(bm, gpm, gqm, gpl, gql, bf, sk):
        last = jnp.concatenate([bf, sk, jnp.zeros((1, D - n_fox - n_swa), F32)], axis=1)
        return jnp.concatenate([bm.reshape(N_MOD, D), gpm, gqm, gpl, gql, last, jnp.zeros((5, D), F32)], axis=0)

    def unpack(p):
        return {"b_mod": p[0:N_MOD].reshape(1, N_MOD * D), "g_pre_mix": p[6:7], "g_post_mix": p[7:8],
                "g_pre_mlp": p[8:9], "g_post_mlp": p[9:10], "b_forget": p[10:11, :n_fox],
                "swa_sinks": p[10:11, n_fox:n_fox + n_swa]}

    small_out = _small_update(
        gathered_small, pack(b_mod, g_pre_mix, g_post_mix, g_pre_mlp, g_post_mlp, b_forget, swa_sinks),
        pack(m_b_mod, m_g_pre_mix, m_g_post_mix, m_g_pre_mlp, m_g_post_mlp, m_b_forget, m_swa_sinks),
        pack(v_b_mod, v_g_pre_mix, v_g_post_mix, v_g_pre_mlp, v_g_post_mlp, v_b_forget, v_swa_sinks))
    g_small, d_small, m_small, v_small = [unpack(p) for p in small_out]

    dmod_all = gathered_small.reshape(N_DEV, 16, D)[:, :N_MOD].reshape(N_DEV, N_MOD * D)
    dmod_shard = lax.dynamic_slice_in_dim(dmod_all, chip * mod_cols, mod_cols, axis=1)
    g_w_mod, d_w_mod, nm_w_mod, nv_w_mod = _mod_update(c_all.T, dmod_shard, w_mod[0], m_w_mod[0], v_w_mod[0])

    grads = dict(g_small, w_mod=g_w_mod[None])
    deltas = dict(d_small, w_mod=d_w_mod[None])
    new_m = dict(m_small, w_mod=nm_w_mod[None])
    new_v = dict(v_small, w_mod=nv_w_mod[None])
    weights = {"w_in": (w_in, m_w_in, v_w_in), "w_out": (w_out, m_w_out, v_w_out), "w_up": (w_up, m_w_up, v_w_up),
               "w_down": (w_down, m_w_down, v_w_down)}

    def big_update(n, own, other):
        transposed = n == "w_in"
        w, m, v = [rows_of(a) if transposed else a[0] for a in weights[n]]
        outs = _adam_halves("adam_" + n, core_arr, w, own, other, m, v)
        if transposed:
            outs = [o[:in_rows].T for o in outs]
        grads[n], deltas[n], new_m[n], new_v[n] = [o[None] for o in outs]

    big_update("w_up", halves_mlp[0], others_mlp[0])
    big_update("w_down", halves_mlp[1], others_mlp[1])
    ran = deltas["w_down"][0, :8, :LANES] + deltas["w_up"][0, :8, :LANES] + d_w_mod[:8, :LANES]
    halves_mix, others_mix = reduce_finish("mix", flight_mix, ran)
    big_update("w_in", halves_mix[0], others_mix[0])
    big_update("w_out", halves_mix[1], others_mix[1])

    order = ["w_mod", "b_mod", "g_pre_mix", "g_post_mix", "w_in", "b_forget", "swa_sinks", "w_out", "g_pre_mlp",
             "g_post_mlp", "w_up", "w_down"]
    return (loss, grad_x[None], *[grads[n] for n in order], *[deltas[n] for n in order],
            *[new_m[n] for n in order], *[new_v[n] for n in order])
```

```python
import jax
import jax.numpy as jnp
from jax import lax
from jax.experimental import pallas as pl
from jax.experimental.pallas import tpu as pltpu

F32 = jnp.float32
BF16 = jnp.bfloat16
MESH = pl.DeviceIdType.MESH

HEAD_DIM = 128
SWA_BLOCK = 128
ROPE_THETA = 10000.0
NORM_EPS = 1e-6
NEG = -1e30
N_MOD = 6
ADAM_LR = 0.001
ADAM_B1 = 0.9
ADAM_B2 = 0.999
ADAM_EPS = 1e-08
ADAM_WD = 0.01
ADAM_STEP = 10
N_CHIPS = 4
N_DEV = 8
LANES = 128
VMEM_CAP = 60 * 1024 * 1024

_NN = (((1,), (0,)), ((), ()))
_NT = (((1,), (1,)), ((), ()))
_TN = (((0,), (0,)), ((), ()))


def _vmem(nbytes):
    return int(min(VMEM_CAP, nbytes * 5 // 4 + (4 << 20)))


def _nbytes(shape, dtype):
    n = 1
    for s in shape:
        n *= s
    return n * jnp.dtype(dtype).itemsize


def _fit(t, n):
    t = min(t, n)
    assert n % t == 0, (t, n)
    return t


MM_TM, MM_TN, MM_TK = 512, 1024, 2048


def _matmul(name, a, b, mode, out_defs, epilogue, extras=(), tm=MM_TM, tn=MM_TN, tk=MM_TK):
    if mode == "nn":
        (M, K), (K2, N) = a.shape, b.shape
    elif mode == "nt":
        (M, K), (N, K2) = a.shape, b.shape
    else:
        (K, M), (K2, N) = a.shape, b.shape
    assert K == K2, (a.shape, b.shape, mode)
    tm, tn, tk = _fit(tm, M), _fit(tn, N), _fit(tk, K)
    nk = K // tk
    dims = {"nn": _NN, "nt": _NT, "tn": _TN}[mode]
    a_spec = (pl.BlockSpec((tk, tm), lambda i, j, k: (k, i)) if mode == "tn"
              else pl.BlockSpec((tm, tk), lambda i, j, k: (i, k)))
    b_spec = (pl.BlockSpec((tn, tk), lambda i, j, k: (j, k)) if mode == "nt"
              else pl.BlockSpec((tk, tn), lambda i, j, k: (k, j)))
    n_ex, n_out = len(extras), len(out_defs)

    def body(*refs):
        a_ref, b_ref = refs[0], refs[1]
        ex = refs[2:2 + n_ex]
        outs = refs[2 + n_ex:2 + n_ex + n_out]
        prod = lax.dot_general(a_ref[...], b_ref[...], dims, preferred_element_type=F32)
        if nk == 1:
            epilogue(prod, ex, outs)
        else:
            acc_ref = refs[-1]
            k = pl.program_id(2)

            @pl.when(k == 0)
            def _():
                acc_ref[...] = prod

            @pl.when(k > 0)
            def _():
                acc_ref[...] += prod

            @pl.when(k == nk - 1)
            def _():
                epilogue(acc_ref[...], ex, outs)

    def wrap(f):
        return lambda i, j, k: f(i, j)

    in_specs = [a_spec, b_spec] + [pl.BlockSpec(blk, wrap(f)) for _, blk, f in extras]
    out_specs = [pl.BlockSpec(blk, wrap(f)) for _, _, blk, f in out_defs]
    out_shape = [jax.ShapeDtypeStruct(s, d) for s, d, _, _ in out_defs]
    need = 2 * (tm * tk + tk * tn) * a.dtype.itemsize + 3 * tm * tn * 4
    need += sum(2 * _nbytes(blk, arr.dtype) for arr, blk, _ in extras)
    need += sum(2 * _nbytes(blk, d) for _, d, blk, _ in out_defs)
    res = pl.pallas_call(
        body, name=name, grid=(M // tm, N // tn, nk),
        in_specs=in_specs, out_specs=out_specs, out_shape=out_shape,
        scratch_shapes=[pltpu.VMEM((tm, tn), F32)] if nk > 1 else [],
        compiler_params=pltpu.CompilerParams(
            dimension_semantics=("parallel", "parallel", "arbitrary"), vmem_limit_bytes=_vmem(need)),
    )(a, b, *[arr for arr, _, _ in extras])
    return res


def _behind(token):
    return (token, (8, LANES), lambda i, j: (0, 0))


def _mm_plain(name, a, b, mode, out_dtype, after=None, **tiles):
    if mode == "nn":
        M, N = a.shape[0], b.shape[1]
    elif mode == "nt":
        M, N = a.shape[0], b.shape[0]
    else:
        M, N = a.shape[1], b.shape[1]
    tm, tn = _fit(tiles.get("tm", MM_TM), M), _fit(tiles.get("tn", MM_TN), N)

    def epi(acc, ex, outs):
        outs[0][...] = acc.astype(out_dtype)

    return _matmul(name, a, b, mode, [((M, N), out_dtype, (tm, tn), lambda i, j: (i, j))], epi,
                   extras=[] if after is None else [_behind(after)], **tiles)[0]


def _rstd(v):
    return lax.rsqrt(jnp.mean(v * v, axis=-1, keepdims=True) + NORM_EPS)


def _row_call(name, body, row_ins, vec_ins, row_outs, acc_outs, S, D, tr):
    tr = _fit(tr, S)
    row_spec = pl.BlockSpec((tr, D), lambda r: (r, 0))
    vec_spec = pl.BlockSpec((1, D), lambda r: (0, 0))
    in_specs = [row_spec] * len(row_ins) + [vec_spec] * len(vec_ins)
    out_specs = [row_spec] * len(row_outs) + [pl.BlockSpec(shp, lambda r: (0, 0)) for shp in acc_outs]
    out_shape = [jax.ShapeDtypeStruct((S, D), d) for d in row_outs] + [jax.ShapeDtypeStruct(shp, F32) for shp in acc_outs]
    need = sum(2 * tr * D * a.dtype.itemsize for a in row_ins) + sum(2 * tr * D * jnp.dtype(d).itemsize for d in row_outs)
    need += 8 * tr * D * 4
    return pl.pallas_call(
        body, name=name, grid=(S // tr,), in_specs=in_specs, out_specs=out_specs, out_shape=out_shape,
        compiler_params=pltpu.CompilerParams(dimension_semantics=("arbitrary",), vmem_limit_bytes=_vmem(need)),
    )(*row_ins, *vec_ins)


def _acc_rows(ref, rows):
    @pl.when(pl.program_id(0) == 0)
    def _():
        ref[...] = jnp.zeros_like(ref)
    for n, r in enumerate(rows):
        ref[n:n + 1, :] += r


def _pre_norm(x, g, sc, sh):
    S, D = x.shape

    def body(x_ref, g_ref, sc_ref, sh_ref, h_ref):
        xv = x_ref[...]
        xn = xv * _rstd(xv)
        h_ref[...] = (xn * g_ref[...] * (1.0 + sc_ref[...]) + sh_ref[...]).astype(BF16)

    return _row_call("pre_norm_mix", body, [x], [g, sc, sh], [BF16], [], S, D, 256)[0]


def _post_mix(x, mix, g_post, gt, g_pre, sc, sh):
    S, D = x.shape

    def body(x_ref, mix_ref, gp_ref, gt_ref, g2_ref, sc_ref, sh_ref, x1_ref, h2_ref):
        mv = mix_ref[...]
        x1 = x_ref[...] + gt_ref[...] * (mv * _rstd(mv) * gp_ref[...])
        x1_ref[...] = x1
        h2_ref[...] = (x1 * _rstd(x1) * g2_ref[...] * (1.0 + sc_ref[...]) + sh_ref[...]).astype(BF16)

    return _row_call("post_mix_pre_mlp", body, [x, mix], [g_post, gt, g_pre, sc, sh], [F32, BF16], [], S, D, 256)


def _loss_and_post_mlp_bwd(x1, y, target, g_post, gt):
    S, D = x1.shape

    def body(x1_ref, y_ref, t_ref, g_ref, gt_ref, dy_ref, dout_ref, loss_ref, acc_ref):
        yv = y_ref[...]
        r = _rstd(yv)
        yh = yv * r
        n = yh * g_ref[...]
        diff = x1_ref[...] + gt_ref[...] * n - t_ref[...]
        dout = diff * (1.0 / D)
        dout_ref[...] = dout
        dn = dout * gt_ref[...]
        dyh = dn * g_ref[...]
        dy_ref[...] = (r * (dyh - yh * jnp.mean(dyh * yh, axis=-1, keepdims=True))).astype(BF16)
        _acc_rows(acc_ref, [jnp.sum(dout * n, axis=0, keepdims=True), jnp.sum(dn * yh, axis=0, keepdims=True)])

        @pl.when(pl.program_id(0) == 0)
        def _():
            loss_ref[...] = jnp.zeros_like(loss_ref)
        loss_ref[...] += jnp.full(loss_ref.shape, (0.5 / D) * jnp.sum(diff * diff), F32)

    return _row_call("loss_post_mlp_bwd", body, [x1, y, target], [g_post, gt], [BF16, F32],
                     [(8, LANES), (8, D)], S, D, 128)


def _pre_mlp_and_post_mix_bwd(dh2, x1, dout, mix, g_pre, sc, g_post, gt):
    S, D = x1.shape

    def body(dh_ref, x1_ref, dout_ref, mix_ref, g_ref, sc_ref, gp_ref, gt_ref, dx1_ref, dmix_ref, acc_ref):
        dh = dh_ref[...]
        x1v = x1_ref[...]
        r3 = _rstd(x1v)
        xn = x1v * r3
        dxn = dh * (1.0 + sc_ref[...]) * g_ref[...]
        dx1 = dout_ref[...] + r3 * (dxn - xn * jnp.mean(dxn * xn, axis=-1, keepdims=True))
        dx1_ref[...] = dx1
        mv = mix_ref[...]
        r2 = _rstd(mv)
        mh = mv * r2
        dn = dx1 * gt_ref[...]
        dmh = dn * gp_ref[...]
        dmix_ref[...] = (r2 * (dmh - mh * jnp.mean(dmh * mh, axis=-1, keepdims=True))).astype(BF16)
        _acc_rows(acc_ref, [
            jnp.sum(dh, axis=0, keepdims=True),
            jnp.sum(dh * xn * g_ref[...], axis=0, keepdims=True),
            jnp.sum(dh * (1.0 + sc_ref[...]) * xn, axis=0, keepdims=True),
            jnp.sum(dx1 * mh * gp_ref[...], axis=0, keepdims=True),
            jnp.sum(dn * mh, axis=0, keepdims=True)])

    return _row_call("pre_mlp_post_mix_bwd", body, [dh2, x1, dout, mix], [g_pre, sc, g_post, gt], [F32, BF16],
                     [(8, D)], S, D, 128)


def _pre_mix_bwd(dh, x, dx1, g_pre, sc):
    S, D = x.shape

    def body(dh_ref, x_ref, dx1_ref, g_ref, sc_ref, gx_ref, acc_ref):
        dhv = dh_ref[...]
        xv = x_ref[...]
        r = _rstd(xv)
        xn = xv * r
        dxn = dhv * (1.0 + sc_ref[...]) * g_ref[...]
        gx_ref[...] = dx1_ref[...] + r * (dxn - xn * jnp.mean(dxn * xn, axis=-1, keepdims=True))
        _acc_rows(acc_ref, [
            jnp.sum(dhv, axis=0, keepdims=True),
            jnp.sum(dhv * xn * g_ref[...], axis=0, keepdims=True),
            jnp.sum(dhv * (1.0 + sc_ref[...]) * xn, axis=0, keepdims=True)])

    return _row_call("pre_mix_bwd", body, [dh, x, dx1], [g_pre, sc], [F32], [(8, D)], S, D, 128)


CUM_BLOCK = 256


def _tri(n, upper):
    r = lax.broadcasted_iota(jnp.int32, (n, n), 0)
    c = lax.broadcasted_iota(jnp.int32, (n, n), 1)
    return ((c >= r) if upper else (c <= r)).astype(F32)


def _fox_gate_fwd(fg, b_pad):
    S = fg.shape[0]
    cb = _fit(CUM_BLOCK, S)

    def body(fg_ref, b_ref, cumt_ref, cum_ref):
        low = _tri(cb, False)
        carry = jnp.zeros((1, LANES), F32)
        for n in range(S // cb):
            z = fg_ref[n * cb:(n + 1) * cb, :] + b_ref[...]
            logf = jnp.minimum(z, 0.0) - jnp.log(1.0 + jnp.exp(-jnp.abs(z)))
            blk = jnp.dot(low, logf, precision=lax.Precision.HIGHEST, preferred_element_type=F32) + carry
            cum_ref[n * cb:(n + 1) * cb, :] = blk
            carry = blk[cb - 1:cb, :]
        cumt_ref[...] = cum_ref[...].T

    return pl.pallas_call(
        body, name="fox_gate_fwd", out_shape=jax.ShapeDtypeStruct((LANES, S), F32),
        scratch_shapes=[pltpu.VMEM((S, LANES), F32)],
        compiler_params=pltpu.CompilerParams(vmem_limit_bytes=_vmem(6 * S * LANES * 4)),
    )(fg, b_pad)


def _fox_gate_bwd(dcum_k, dcum_q, fg, b_pad):
    S = fg.shape[0]
    n_fox = dcum_q.shape[0]
    cb = _fit(CUM_BLOCK, S)

    def body(dk_ref, dq_ref, fg_ref, b_ref, dfg_ref, db_ref, dc_ref):
        lane = lax.broadcasted_iota(jnp.int32, (S, LANES), 1)
        dc = dk_ref[...].T
        for h in range(n_fox):
            dc = dc + jnp.where(lane == h, dq_ref[h], 0.0)
        dc_ref[...] = dc
        up = _tri(cb, True)
        carry = jnp.zeros((1, LANES), F32)
        db = jnp.zeros((1, LANES), F32)
        for n in reversed(range(S // cb)):
            blk = jnp.dot(up, dc_ref[n * cb:(n + 1) * cb, :], precision=lax.Precision.HIGHEST,
                          preferred_element_type=F32) + carry
            carry = blk[0:1, :]
            z = fg_ref[n * cb:(n + 1) * cb, :] + b_ref[...]
            dfg = blk * (1.0 / (1.0 + jnp.exp(z)))
            dfg_ref[n * cb:(n + 1) * cb, :] = dfg.astype(BF16)
            db = db + jnp.sum(dfg, axis=0, keepdims=True)
        db_ref[...] = jnp.broadcast_to(db, db_ref.shape)

    return pl.pallas_call(
        body, name="fox_gate_bwd",
        out_shape=[jax.ShapeDtypeStruct((S, LANES), BF16), jax.ShapeDtypeStruct((8, LANES), F32)],
        scratch_shapes=[pltpu.VMEM((S, LANES), F32)],
        compiler_params=pltpu.CompilerParams(vmem_limit_bytes=_vmem((8 + 2 * n_fox) * S * LANES * 4)),
    )(dcum_k, dcum_q, fg, b_pad)


FOX_TILE = 512


LOG2E = 1.4426950408889634


def _fox_scores(q, k, ck2, masked, t):
    s = lax.dot_general(q, k, _NT, preferred_element_type=F32) * (HEAD_DIM ** -0.5 * LOG2E) - ck2
    if masked:
        row = lax.broadcasted_iota(jnp.int32, (t, t), 0)
        col = lax.broadcasted_iota(jnp.int32, (t, t), 1)
        s = jnp.where(col <= row, s, NEG)
    return s


def _fox_fwd(proj, cum_row, n_fox):
    S = proj.shape[0]
    t = _fit(FOX_TILE, S)
    nq = S // t

    def body(q_ref, k_ref, v_ref, ck_ref, o_ref, lse_ref):
        def q_block(qi, _):
            q0 = pl.multiple_of(qi * t, t)
            q = q_ref[pl.ds(q0, t), :]

            def kv_block(j, carry, masked):
                m, l, acc = carry
                k0 = pl.multiple_of(j * t, t)
                s = _fox_scores(q, k_ref[pl.ds(k0, t), :], ck_ref[0, :, pl.ds(k0, t)] * LOG2E, masked, t)
                m_new = jnp.maximum(m, jnp.max(s, axis=-1, keepdims=True))
                alpha = jnp.exp2(m - m_new)
                p = jnp.exp2(s - m_new)
                l = alpha * l + jnp.sum(p, axis=-1, keepdims=True)
                acc = alpha * acc + jnp.dot(p.astype(BF16), v_ref[pl.ds(k0, t), :], preferred_element_type=F32)
                return m_new, l, acc

            init = (jnp.full((t, 1), NEG, F32), jnp.zeros((t, 1), F32), jnp.zeros((t, HEAD_DIM), F32))
            carry = lax.fori_loop(0, qi, lambda j, cr: kv_block(j, cr, False), init)
            m, l, acc = kv_block(qi, carry, True)
            o_ref[pl.ds(q0, t), :] = acc / l
            lse_ref[0, pl.ds(q0, t), :] = jnp.broadcast_to(m + jnp.log(l) * LOG2E, (t, LANES))
            return 0

        lax.fori_loop(0, nq, q_block, 0)

    col = lambda off: pl.BlockSpec((S, HEAD_DIM), lambda h: (0, off + h))
    per_head = pl.BlockSpec((1, S, LANES), lambda h: (h, 0, 0))
    return pl.pallas_call(
        body, name="fox_fwd", grid=(n_fox,),
        in_specs=[col(0), col(n_fox), col(2 * n_fox), pl.BlockSpec((1, 1, S), lambda h: (h, 0, 0))],
        out_specs=[pl.BlockSpec((S, HEAD_DIM), lambda h: (0, h)), per_head],
        out_shape=[jax.ShapeDtypeStruct((S, n_fox * HEAD_DIM), F32), jax.ShapeDtypeStruct((n_fox, S, LANES), F32)],
        compiler_params=pltpu.CompilerParams(dimension_semantics=("parallel",),
                                             vmem_limit_bytes=_vmem(16 * S * HEAD_DIM * 4 + 12 * t * t * 4)),
    )(proj, proj, proj, cum_row)


def _fox_bwd(proj, o, do, lse_b, cum_row, n_fox):
    S = proj.shape[0]
    t = _fit(FOX_TILE, S)
    nq = S // t
    scale = HEAD_DIM ** -0.5

    def body(q_ref, k_ref, v_ref, o_ref, do_ref, lse_ref, ck_ref, dq_ref, dk_ref, dv_ref, dc_ref, dcq_ref,
             dq_acc, delta_ref):
        dq_acc[...] = jnp.zeros_like(dq_acc)
        dcq_ref[...] = jnp.zeros_like(dcq_ref)

        def delta_block(qi, _):
            q0 = pl.multiple_of(qi * t, t)
            d = jnp.sum(do_ref[pl.ds(q0, t), :] * o_ref[pl.ds(q0, t), :], axis=-1, keepdims=True)
            delta_ref[pl.ds(q0, t), :] = jnp.broadcast_to(d, (t, LANES))
            return 0

        lax.fori_loop(0, nq, delta_block, 0)

        def kv_block(j, _):
            k0 = pl.multiple_of(j * t, t)
            k = k_ref[pl.ds(k0, t), :]
            v = v_ref[pl.ds(k0, t), :]
            ck2 = ck_ref[0, :, pl.ds(k0, t)] * LOG2E

            def q_block(qi, carry, masked):
                dk, dv, dc = carry
                q0 = pl.multiple_of(qi * t, t)
                q = q_ref[pl.ds(q0, t), :]
                dov = do_ref[pl.ds(q0, t), :].astype(BF16)
                p = jnp.exp2(_fox_scores(q, k, ck2, masked, t) - lse_ref[0, pl.ds(q0, t), :][:, :1])
                dp = lax.dot_general(dov, v, _NT, preferred_element_type=F32)
                ds = p * (dp - delta_ref[pl.ds(q0, t), :][:, :1])
                dsb = ds.astype(BF16)
                dv = dv + lax.dot_general(p.astype(BF16), dov, _TN, preferred_element_type=F32)
                dk = dk + lax.dot_general(dsb, q, _TN, preferred_element_type=F32)
                dq_acc[pl.ds(q0, t), :] += jnp.dot(dsb, k, preferred_element_type=F32)
                dc = dc - jnp.sum(ds, axis=0, keepdims=True)
                dcq_ref[0, pl.ds(q0, t), :] += jnp.broadcast_to(jnp.sum(ds, axis=1, keepdims=True), (t, LANES))
                return dk, dv, dc

            init = (jnp.zeros((t, HEAD_DIM), F32), jnp.zeros((t, HEAD_DIM), F32), jnp.zeros((1, t), F32))
            carry = q_block(j, init, True)
            dk, dv, dc = lax.fori_loop(j + 1, nq, lambda qi, cr: q_block(qi, cr, False), carry)
            dk_ref[pl.ds(k0, t), :] = (dk * scale).astype(BF16)
            dv_ref[pl.ds(k0, t), :] = dv.astype(BF16)
            dc_ref[0, :, pl.ds(k0, t)] = dc
            return 0

        lax.fori_loop(0, nq, kv_block, 0)
        dq_ref[...] = (dq_acc[...] * scale).astype(BF16)

    col = lambda off: pl.BlockSpec((S, HEAD_DIM), lambda h: (0, off + h))
    per_head = pl.BlockSpec((1, S, LANES), lambda h: (h, 0, 0))
    row = pl.BlockSpec((1, 1, S), lambda h: (h, 0, 0))
    grad = jax.ShapeDtypeStruct((S, n_fox * HEAD_DIM), BF16)
    return pl.pallas_call(
        body, name="fox_bwd", grid=(n_fox,),
        in_specs=[col(0), col(n_fox), col(2 * n_fox), col(0), col(0), per_head, row],
        out_specs=[col(0), col(0), col(0), row, per_head],
        out_shape=[grad, grad, grad, jax.ShapeDtypeStruct((n_fox, 1, S), F32), jax.ShapeDtypeStruct((n_fox, S, LANES), F32)],
        scratch_shapes=[pltpu.VMEM((S, HEAD_DIM), F32), pltpu.VMEM((S, LANES), F32)],
        compiler_params=pltpu.CompilerParams(dimension_semantics=("parallel",),
                                             vmem_limit_bytes=_vmem(24 * S * HEAD_DIM * 4 + 16 * t * t * 4)),
    )(proj, proj, proj, o, do, lse_b, cum_row)


def _rope_tables(S):
    half = HEAD_DIM // 2
    inv_freq = 1.0 / (ROPE_THETA ** (jnp.arange(half, dtype=F32) * (2.0 / HEAD_DIM)))
    ang = jnp.arange(S).astype(F32)[:, None] * inv_freq[None, :]
    cos, sin = jnp.cos(ang), jnp.sin(ang)
    return jnp.concatenate([cos, cos], axis=-1), jnp.concatenate([-sin, sin], axis=-1)


def _rope(name, src, first_block, n_blocks, cos, sin_signed):
    S = src.shape[0]

    def body(x_ref, cos_ref, sin_ref, o_ref):
        xv = x_ref[...].astype(F32)
        o_ref[...] = (xv * cos_ref[...] + pltpu.roll(xv, HEAD_DIM // 2, 1) * sin_ref[...]).astype(BF16)

    table = pl.BlockSpec((S, HEAD_DIM), lambda n: (0, 0))
    return pl.pallas_call(
        body, name=name, grid=(n_blocks,),
        in_specs=[pl.BlockSpec((S, HEAD_DIM), lambda n: (0, first_block + n)), table, table],
        out_specs=pl.BlockSpec((S, HEAD_DIM), lambda n: (0, n)),
        out_shape=jax.ShapeDtypeStruct((S, n_blocks * HEAD_DIM), BF16),
        compiler_params=pltpu.CompilerParams(dimension_semantics=("parallel",),
                                             vmem_limit_bytes=_vmem(12 * S * HEAD_DIM * 4)),
    )(src, cos, sin_signed)


def _swa_tile(q_ref, kp_ref, kc_ref, n, group, scale):
    B = SWA_BLOCK
    qs = jnp.concatenate([q_ref[:, g * HEAD_DIM:(g + 1) * HEAD_DIM] for g in range(group)], axis=0)
    kcat = jnp.concatenate([kp_ref[...], kc_ref[...]], axis=0)
    s = lax.dot_general(qs, kcat, _NT, preferred_element_type=F32) * scale
    qi = lax.broadcasted_iota(jnp.int32, (group * B, 2 * B), 0) % B
    kj = lax.broadcasted_iota(jnp.int32, (group * B, 2 * B), 1)
    diff = qi + B - kj
    mask = (diff >= 0) & (diff < B) & ((n * B + kj - B) >= 0)
    return qs, kcat, jnp.where(mask, s, NEG)


def _swa_sink_col(sink_ref, kv, group):
    head = lax.broadcasted_iota(jnp.int32, (group * SWA_BLOCK, 1), 0) // SWA_BLOCK
    col = jnp.zeros((group * SWA_BLOCK, 1), F32)
    for g in range(group):
        col = jnp.where(head == g, sink_ref[kv * group + g], col)
    return col


def _swa_specs(n_kv, group, q_first, k_first, v_first):
    B = SWA_BLOCK
    prev = lambda n: jnp.maximum(n - 1, 0)
    return [
        pl.BlockSpec((B, group * HEAD_DIM), lambda kv, n: (n, q_first + kv)),
        pl.BlockSpec((B, HEAD_DIM), lambda kv, n: (prev(n), k_first + kv)),
        pl.BlockSpec((B, HEAD_DIM), lambda kv, n: (n, k_first + kv)),
        pl.BlockSpec((B, HEAD_DIM), lambda kv, n: (prev(n), v_first + kv)),
        pl.BlockSpec((B, HEAD_DIM), lambda kv, n: (n, v_first + kv)),
    ]


def _swa_fwd(rq, proj, v_first, sinks, n_q, n_kv):
    S = rq.shape[0]
    B = SWA_BLOCK
    group = n_q // n_kv
    scale = HEAD_DIM ** -0.5

    def body(q_ref, kp_ref, kc_ref, vp_ref, vc_ref, sink_ref, o_ref, lse_ref):
        kv, n = pl.program_id(0), pl.program_id(1)
        _, _, s = _swa_tile(q_ref, kp_ref, kc_ref, n, group, scale)
        sink = _swa_sink_col(sink_ref, kv, group)
        m = jnp.maximum(jnp.max(s, axis=-1, keepdims=True), sink)
        p = jnp.exp(s - m)
        denom = jnp.sum(p, axis=-1, keepdims=True) + jnp.exp(sink - m)
        vcat = jnp.concatenate([vp_ref[...], vc_ref[...]], axis=0)
        o = jnp.dot((p / denom).astype(BF16), vcat, preferred_element_type=F32)
        lse = m + jnp.log(denom)
        for g in range(group):
            o_ref[:, g * HEAD_DIM:(g + 1) * HEAD_DIM] = o[g * B:(g + 1) * B, :]
            lse_ref[0, :, g * LANES:(g + 1) * LANES] = jnp.broadcast_to(lse[g * B:(g + 1) * B, :], (B, LANES))

    specs = _swa_specs(n_kv, group, 0, n_q, v_first)
    q_blk = pl.BlockSpec((B, group * HEAD_DIM), lambda kv, n: (n, kv))
    return pl.pallas_call(
        body, name="swa_fwd", grid=(n_kv, S // B),
        in_specs=specs + [pl.BlockSpec(memory_space=pltpu.SMEM)],
        out_specs=[q_blk, pl.BlockSpec((1, B, group * LANES), lambda kv, n: (kv, n, 0))],
        out_shape=[jax.ShapeDtypeStruct((S, n_q * HEAD_DIM), F32), jax.ShapeDtypeStruct((n_kv, S, group * LANES), F32)],
        compiler_params=pltpu.CompilerParams(dimension_semantics=("parallel", "arbitrary")),
    )(rq, rq, rq, proj, proj, sinks)


def _swa_bwd(rq, proj, v_first, sinks, o, do, do_first, lse_b, n_q, n_kv):
    S = rq.shape[0]
    B = SWA_BLOCK
    group = n_q // n_kv
    scale = HEAD_DIM ** -0.5

    def body(q_ref, kp_ref, kc_ref, vp_ref, vc_ref, o_ref, do_ref, lse_ref, sink_ref,
             dq_ref, dk_ref, dv_ref, dsink_ref):
        kv, n = pl.program_id(0), pl.program_id(1)

        @pl.when(n == 0)
        def _():
            dk_ref[...] = jnp.zeros_like(dk_ref)
            dv_ref[...] = jnp.zeros_like(dv_ref)
            dsink_ref[...] = jnp.zeros_like(dsink_ref)

        qs, kcat, s = _swa_tile(q_ref, kp_ref, kc_ref, n, group, scale)
        sink = _swa_sink_col(sink_ref, kv, group)
        stack = lambda ref, w: jnp.concatenate([ref[:, g * w:(g + 1) * w] for g in range(group)], axis=0)
        lse = jnp.concatenate([lse_ref[0, :, g * LANES:g * LANES + 1] for g in range(group)], axis=0)
        do32 = stack(do_ref, HEAD_DIM)
        delta = jnp.sum(do32 * stack(o_ref, HEAD_DIM), axis=-1, keepdims=True)
        dov = do32.astype(BF16)
        p = jnp.exp(s - lse)
        vcat = jnp.concatenate([vp_ref[...], vc_ref[...]], axis=0)
        dp = lax.dot_general(dov, vcat, _NT, preferred_element_type=F32)
        ds = p * (dp - delta)
        dsb = ds.astype(BF16)
        dq = jnp.dot(dsb, kcat, preferred_element_type=F32) * scale
        for g in range(group):
            dq_ref[:, g * HEAD_DIM:(g + 1) * HEAD_DIM] = dq[g * B:(g + 1) * B, :].astype(BF16)
        dkcat = lax.dot_general(dsb, qs, _TN, preferred_element_type=F32) * scale
        dvcat = lax.dot_general(p.astype(BF16), dov, _TN, preferred_element_type=F32)
        prev0 = pl.multiple_of(jnp.maximum(n - 1, 0) * B, B)
        cur0 = pl.multiple_of(n * B, B)
        dk_ref[0, pl.ds(prev0, B), :] += dkcat[:B, :]
        dk_ref[0, pl.ds(cur0, B), :] += dkcat[B:, :]
        dv_ref[0, pl.ds(prev0, B), :] += dvcat[:B, :]
        dv_ref[0, pl.ds(cur0, B), :] += dvcat[B:, :]
        dsk = -jnp.exp(sink - lse) * delta
        lane = lax.broadcasted_iota(jnp.int32, (1, LANES), 1)
        row = jnp.zeros((1, LANES), F32)
        for g in range(group):
            row = row + jnp.where(lane == g, jnp.sum(dsk[g * B:(g + 1) * B, :]), 0.0)
        dsink_ref[0, 0:1, :] += row

    specs = _swa_specs(n_kv, group, 0, n_q, v_first)
    q_blk = pl.BlockSpec((B, group * HEAD_DIM), lambda kv, n: (n, kv))
    acc = pl.BlockSpec((1, S, HEAD_DIM), lambda kv, n: (kv, 0, 0))
    return pl.pallas_call(
        body, name="swa_bwd", grid=(n_kv, S // B),
        in_specs=specs + [q_blk, pl.BlockSpec((B, group * HEAD_DIM), lambda kv, n: (n, do_first + kv)),
                          pl.BlockSpec((1, B, group * LANES), lambda kv, n: (kv, n, 0)),
                          pl.BlockSpec(memory_space=pltpu.SMEM)],
        out_specs=[q_blk, acc, acc, pl.BlockSpec((1, 8, LANES), lambda kv, n: (kv, 0, 0))],
        out_shape=[jax.ShapeDtypeStruct((S, n_q * HEAD_DIM), BF16), jax.ShapeDtypeStruct((n_kv, S, HEAD_DIM), F32),
                   jax.ShapeDtypeStruct((n_kv, S, HEAD_DIM), F32), jax.ShapeDtypeStruct((n_kv, 8, LANES), F32)],
        compiler_params=pltpu.CompilerParams(dimension_semantics=("parallel", "arbitrary")),
    )(rq, rq, rq, proj, proj, o, do, lse_b, sinks)


def _adamw(w, g, m, v):
    m = ADAM_B1 * m + (1.0 - ADAM_B1) * g
    v = ADAM_B2 * v + (1.0 - ADAM_B2) * (g * g)
    m_hat = m / (1.0 - ADAM_B1 ** ADAM_STEP)
    v_hat = v / (1.0 - ADAM_B2 ** ADAM_STEP)
    delta = -ADAM_LR * (m_hat / (jnp.sqrt(v_hat) + ADAM_EPS) + ADAM_WD * w)
    return delta, m, v


def _mod_fwd(cond_in, w_mod, b_shard):
    R, D = cond_in.shape
    cols = w_mod.shape[1]
    tn = _fit(512, cols)

    def body(c_ref, w_ref, b_ref, o_ref):
        cv = c_ref[...]
        cond = (cv / (1.0 + jnp.exp(-cv))).astype(BF16)
        o_ref[...] = jnp.dot(cond, w_ref[...].astype(BF16), preferred_element_type=F32) + b_ref[...]

    return pl.pallas_call(
        body, name="mod_fwd", grid=(cols // tn,),
        in_specs=[pl.BlockSpec((R, D), lambda j: (0, 0)), pl.BlockSpec((D, tn), lambda j: (0, j)),
                  pl.BlockSpec((1, tn), lambda j: (0, j))],
        out_specs=pl.BlockSpec((R, tn), lambda j: (0, j)),
        out_shape=jax.ShapeDtypeStruct((R, cols), F32),
        compiler_params=pltpu.CompilerParams(dimension_semantics=("parallel",), vmem_limit_bytes=_vmem(3 * D * tn * 4)),
    )(cond_in, w_mod, b_shard)


def _mod_update(c_t, dmod, w, m, v):
    D, nb = c_t.shape
    cols = w.shape[1]
    tn = _fit(256, cols)

    def body(c_ref, d_ref, w_ref, m_ref, v_ref, g_ref, dl_ref, nm_ref, nv_ref):
        cv = c_ref[...]
        cond = cv / (1.0 + jnp.exp(-cv))
        g = jnp.zeros((D, tn), F32)
        for b in range(nb):
            g = g + cond[:, b:b + 1] * d_ref[b:b + 1, :]
        g_ref[...] = g
        dl_ref[...], nm_ref[...], nv_ref[...] = _adamw(w_ref[...], g, m_ref[...], v_ref[...])

    blk = pl.BlockSpec((D, tn), lambda j: (0, j))
    out = jax.ShapeDtypeStruct((D, cols), F32)
    return pl.pallas_call(
        body, name="mod_update", grid=(cols // tn,),
        in_specs=[pl.BlockSpec((D, nb), lambda j: (0, 0)), pl.BlockSpec((nb, tn), lambda j: (0, j)), blk, blk, blk],
        out_specs=[blk] * 4, out_shape=[out] * 4,
        compiler_params=pltpu.CompilerParams(dimension_semantics=("parallel",), vmem_limit_bytes=_vmem(18 * D * tn * 4)),
    )(c_t, dmod, w, m, v)


def _small_update(stacked, w, m, v):
    R, C = w.shape

    def body(s_ref, w_ref, m_ref, v_ref, g_ref, dl_ref, nm_ref, nv_ref):
        g = s_ref[0:R, :]
        for d in range(1, N_DEV):
            g = g + s_ref[d * R:(d + 1) * R, :]
        g_ref[...] = g
        dl_ref[...], nm_ref[...], nv_ref[...] = _adamw(w_ref[...], g, m_ref[...], v_ref[...])

    return pl.pallas_call(body, name="small_update", out_shape=[jax.ShapeDtypeStruct((R, C), F32)] * 4)(stacked, w, m, v)


def _place():
    return lax.axis_index("x"), lax.axis_index("y"), lax.axis_index("c")


def _allgather8(name, block):
    m_per, n = block.shape

    def body(x_ref, out_ref, token_ref, send_sems, recv_sems, local_sem):
        token_ref[...] = jnp.zeros_like(token_ref)
        x, y, c = _place()
        me, sibling = (x, y, c), (x, y, 1 - c)
        chips = [(1 - x, y), (x, 1 - y), (1 - x, 1 - y)]

        def rows(px, py, pc):
            return out_ref.at[pl.ds((4 * px + 2 * py + pc) * m_per, m_per), :]

        def copy(k, blk, to, src=None):
            return pltpu.make_async_remote_copy(
                src_ref=rows(*blk) if src is None else src, dst_ref=rows(*blk),
                send_sem=send_sems.at[k], recv_sem=recv_sems.at[k], device_id=to, device_id_type=MESH)

        mine = pltpu.make_async_copy(x_ref, rows(*me), local_sem)
        mine.start()
        first = [copy(0, me, sibling, src=x_ref)]
        first += [copy(1 + j, me, (*chip, c), src=x_ref) for j, chip in enumerate(chips)]
        for cp in first:
            cp.start()
        passed = [copy(4 + j, (*chip, c), sibling) for j, chip in enumerate(chips)]
        for j, chip in enumerate(chips):
            copy(1 + j, (*chip, c), me).wait_recv()
            passed[j].start()
        copy(0, sibling, me).wait_recv()
        for j, chip in enumerate(chips):
            copy(4 + j, (*chip, 1 - c), me).wait_recv()
        for cp in first + passed:
            cp.wait_send()
        mine.wait()

    vmem = pl.BlockSpec(memory_space=pltpu.VMEM)
    return pl.pallas_call(
        body, name=name,
        out_shape=[jax.ShapeDtypeStruct((N_DEV * m_per, n), block.dtype), jax.ShapeDtypeStruct((8, LANES), F32)],
        in_specs=[vmem], out_specs=[vmem, vmem],
        scratch_shapes=[pltpu.SemaphoreType.DMA((7,)), pltpu.SemaphoreType.DMA((7,)), pltpu.SemaphoreType.DMA],
    )(block)


_ANY = pl.BlockSpec(memory_space=pl.ANY)


def _half(ref, c, rows):
    return ref.at[pl.ds(c * (rows // 2), rows // 2), :]


_HBM = pl.BlockSpec(memory_space=pltpu.HBM)
_SEM = pl.BlockSpec(memory_space=pltpu.SEMAPHORE)
_EFFECT = pltpu.SideEffectType.DATAFLOW_SIDE_EFFECTING


def _ici_start(name, srcs, land_shapes, plan, per_source=3):
    ns, nl = len(srcs), len(land_shapes)
    n_copies = per_source * ns

    def body(*refs):
        src_refs, land_refs = refs[:ns], refs[ns:ns + nl]
        send_sems, recv_sems = refs[ns + nl], refs[ns + nl + 1]
        token = refs[-1]
        for n, (src, dst, peer, _) in enumerate(plan(src_refs, land_refs)):
            pltpu.make_async_remote_copy(src_ref=src, dst_ref=dst, send_sem=send_sems.at[n], recv_sem=recv_sems.at[n],
                                         device_id=peer, device_id_type=MESH).start()
        token[...] = jnp.zeros_like(token)

    lands = [lax.empty(s.shape, s.dtype) for s in land_shapes]
    out = pl.pallas_call(
        body, name=name,
        out_shape=(pltpu.SemaphoreType.DMA((n_copies,)), pltpu.SemaphoreType.DMA((n_copies,)),
                   *[pltpu.HBM(a.shape, a.dtype) for a in list(srcs) + lands], jax.ShapeDtypeStruct((8, LANES), F32)),
        in_specs=[_HBM] * (ns + nl),
        out_specs=(_SEM, _SEM, *[_HBM] * (ns + nl), pl.BlockSpec(memory_space=pltpu.VMEM)),
        input_output_aliases={n: 2 + n for n in range(ns + nl)},
        compiler_params=pltpu.CompilerParams(has_side_effects=_EFFECT),
    )(*[pltpu.with_memory_space_constraint(a, pltpu.HBM) for a in list(srcs) + lands])
    return out[0], out[1], list(out[2:2 + ns]), list(out[2 + ns:2 + ns + nl]), out[-1]


def _ici_wait(name, send_sems, recv_sems, srcs, lands, plan, after):
    ns, nl = len(srcs), len(lands)

    def body(*refs):
        src_refs, land_refs = refs[:ns], refs[ns:ns + nl]
        send_sems, recv_sems = refs[ns + nl], refs[ns + nl + 1]
        for n, (src, _, peer, mine) in enumerate(plan(src_refs, land_refs)):
            cp = pltpu.make_async_remote_copy(src_ref=src, dst_ref=mine, send_sem=send_sems.at[n],
                                              recv_sem=recv_sems.at[n], device_id=peer, device_id_type=MESH)
            cp.wait_send()
            cp.wait_recv()

    out = pl.pallas_call(
        body, name=name, out_shape=[pltpu.HBM(a.shape, a.dtype) for a in list(srcs) + list(lands)],
        in_specs=[_HBM] * (ns + nl) + [_SEM, _SEM, _ANY], out_specs=[_HBM] * (ns + nl),
        input_output_aliases={n: n for n in range(ns + nl)},
        compiler_params=pltpu.CompilerParams(has_side_effects=_EFFECT),
    )(*srcs, *lands, send_sems, recv_sems, after)
    return list(out[:ns]), list(out[ns:])


def _gather_plan(src_refs, land_refs):
    x, y, c = _place()
    copies = []
    for w, land in zip(src_refs, land_refs):
        R = w.shape[0]
        for cx, cy in [(1 - x, y), (x, 1 - y), (1 - x, 1 - y)]:
            copies.append((_half(w, c, R), _half(land.at[2 * x + y], c, R), (cx, cy, c),
                           _half(land.at[2 * cx + cy], c, R)))
    return copies


def _pair_plan(src_refs, land_refs):
    x, y, c = _place()
    copies = []
    for g, land in zip(src_refs, land_refs):
        half = g.shape[1] // 2
        copies.append((g.at[:, pl.ds((1 - c) * half, half), :], land, (x, y, 1 - c), land))
    return copies


def _share_plan(src_refs, land_refs):
    x, y, c = _place()
    return [(h, land, (x, y, 1 - c), land) for h, land in zip(src_refs, land_refs)]


def _pass_to_sibling(name, lands):
    nw = len(lands)

    def body(*refs):
        ins, outs = refs[:nw], refs[nw:2 * nw]
        send_sems, recv_sems = refs[2 * nw:]
        x, y, c = _place()
        chips = [(1 - x, y), (x, 1 - y), (1 - x, 1 - y)]
        copies = []
        for k in range(nw):
            R = ins[k].shape[1]
            for j, (cx, cy) in enumerate(chips):
                cp = pltpu.make_async_remote_copy(
                    src_ref=_half(ins[k].at[2 * cx + cy], c, R), dst_ref=_half(outs[k].at[2 * cx + cy], c, R),
                    send_sem=send_sems.at[3 * k + j], recv_sem=recv_sems.at[3 * k + j],
                    device_id=(x, y, 1 - c), device_id_type=MESH)
                cp.start()
                copies.append(cp)
        for k in range(nw):
            R = ins[k].shape[1]
            for j, (cx, cy) in enumerate(chips):
                pltpu.make_async_remote_copy(
                    src_ref=_half(ins[k].at[2 * cx + cy], c, R), dst_ref=_half(outs[k].at[2 * cx + cy], 1 - c, R),
                    send_sem=send_sems.at[3 * k + j], recv_sem=recv_sems.at[3 * k + j],
                    device_id=(x, y, 1 - c), device_id_type=MESH).wait_recv()
        for cp in copies:
            cp.wait_send()

    return pl.pallas_call(
        body, name=name, out_shape=[jax.ShapeDtypeStruct(a.shape, a.dtype) for a in lands],
        in_specs=[_ANY] * nw, out_specs=[_ANY] * nw, input_output_aliases={k: k for k in range(nw)},
        scratch_shapes=[pltpu.SemaphoreType.DMA((3 * nw,)), pltpu.SemaphoreType.DMA((3 * nw,))],
    )(*lands)


def _tie(vec, token):
    return vec + token[0:1, 0:1]


ROW_ALIGN = 16
TILE_ELEMS = 512 * 1024


def _tiles(rows, cols):
    fits = [t for t in range(ROW_ALIGN, min(rows, 256) + 1, ROW_ALIGN) if rows % t == 0]
    tr = fits[-1] if fits and fits[-1] >= 64 else rows
    tc = cols
    while tr * tc > TILE_ELEMS and tc % (2 * LANES) == 0:
        tc //= 2
    return tr, tc


def _pair_add(name, core, grad, recv):
    n, R, C = grad.shape
    half = R // 2
    tr, tc = _tiles(half, C)
    nr = half // tr

    def body(core_ref, g_ref, r_ref, o_ref):
        o_ref[...] = (g_ref[...].astype(F32) + r_ref[...].astype(F32)).astype(BF16)

    grid_spec = pltpu.PrefetchScalarGridSpec(
        num_scalar_prefetch=1, grid=(n, nr, C // tc),
        in_specs=[pl.BlockSpec((1, tr, tc), lambda s, r, q, core_ref: (s, core_ref[0] * nr + r, q)),
                  pl.BlockSpec((1, tr, tc), lambda s, r, q, core_ref: (s, r, q))],
        out_specs=pl.BlockSpec((1, tr, tc), lambda s, r, q, core_ref: (s, r, q)))
    return pl.pallas_call(
        body, name=name, grid_spec=grid_spec, out_shape=jax.ShapeDtypeStruct((n, half, C), BF16),
        compiler_params=pltpu.CompilerParams(dimension_semantics=("parallel", "parallel", "parallel")),
    )(core, grad, recv)


def _scatter_plan(src_refs, land_refs):
    x, y, c = _place()
    copies = []
    for p, land in zip(src_refs, land_refs):
        for j, (cx, cy) in enumerate([(1 - x, y), (x, 1 - y), (1 - x, 1 - y)]):
            copies.append((p.at[2 * cx + cy], land.at[j], (cx, cy, c), land.at[j]))
    return copies


def _chip_add(name, chip, sums, recv):
    _, H, C = sums.shape
    tr, tc = _tiles(H, C)

    def body(chip_ref, p_ref, r_ref, o_ref):
        total = p_ref[0].astype(F32)
        for j in range(3):
            total = total + r_ref[j].astype(F32)
        o_ref[...] = total

    grid_spec = pltpu.PrefetchScalarGridSpec(
        num_scalar_prefetch=1, grid=(H // tr, C // tc),
        in_specs=[pl.BlockSpec((1, tr, tc), lambda r, q, chip_ref: (chip_ref[0], r, q)),
                  pl.BlockSpec((3, tr, tc), lambda r, q, chip_ref: (0, r, q))],
        out_specs=pl.BlockSpec((tr, tc), lambda r, q, chip_ref: (r, q)))
    return pl.pallas_call(
        body, name=name, grid_spec=grid_spec, out_shape=jax.ShapeDtypeStruct((H, C), F32),
        compiler_params=pltpu.CompilerParams(dimension_semantics=("parallel", "parallel")),
    )(chip, sums, recv)


def _pair_share(name, halves):
    nw = len(halves)

    def body(*refs):
        hs, outs = refs[:nw], refs[nw:2 * nw]
        send_sems, recv_sems = refs[2 * nw:]
        x, y, c = _place()
        copies = []
        for k in range(nw):
            cp = pltpu.make_async_remote_copy(
                src_ref=hs[k], dst_ref=outs[k], send_sem=send_sems.at[k], recv_sem=recv_sems.at[k],
                device_id=(x, y, 1 - c), device_id_type=MESH)
            cp.start()
            copies.append(cp)
        for cp in copies:
            cp.wait()

    return pl.pallas_call(
        body, name=name,
        out_shape=[jax.ShapeDtypeStruct(h.shape, h.dtype) for h in halves],
        in_specs=[_ANY] * nw, out_specs=[_ANY] * nw,
        scratch_shapes=[pltpu.SemaphoreType.DMA((nw,)), pltpu.SemaphoreType.DMA((nw,))],
    )(*halves)


def _adam_halves(name, core, w, g_own, g_other, m, v):
    R, C = w.shape
    H = R // 2
    tr, tc = _tiles(H, C)
    nr, nc = H // tr, C // tc

    def body(core_ref, w_ref, go_ref, gr_ref, m_ref, v_ref, g_ref, dl_ref, nm_ref, nv_ref):
        own = (pl.program_id(0) // nr) == core_ref[0]
        g = jnp.where(own, go_ref[...], gr_ref[...])
        g_ref[...] = g
        dl_ref[...], nm_ref[...], nv_ref[...] = _adamw(w_ref[...], g, m_ref[...], v_ref[...])

    blk = pl.BlockSpec((tr, tc), lambda r, q, core_ref: (r, q))

    def half_spec(is_own):
        def index(r, q, core_ref):
            mine = ((r // nr) == core_ref[0]) == is_own
            done = is_own == (core_ref[0] == 0)
            return (jnp.where(mine, r % nr, jnp.where(done, nr - 1, 0)), jnp.where(mine, q, jnp.where(done, nc - 1, 0)))
        return pl.BlockSpec((tr, tc), index)
    out = jax.ShapeDtypeStruct((R, C), F32)
    grid_spec = pltpu.PrefetchScalarGridSpec(
        num_scalar_prefetch=1, grid=(R // tr, nc), in_specs=[blk, half_spec(True), half_spec(False), blk, blk],
        out_specs=[blk] * 4)
    return pl.pallas_call(
        body, name=name, grid_spec=grid_spec, out_shape=[out] * 4,
        compiler_params=pltpu.CompilerParams(dimension_semantics=("parallel", "parallel"),
                                             vmem_limit_bytes=_vmem(20 * tr * tc * 4)),
    )(core, w, g_own, g_other, m, v)


def kernel(x, c, w_mod, b_mod, g_pre_mix, g_post_mix, w_in, b_forget, swa_sinks, w_out, g_pre_mlp, g_post_mlp, w_up, w_down, loss_target, m_w_mod, m_b_mod, m_g_pre_mix, m_g_post_mix, m_w_in, m_b_forget, m_swa_sinks, m_w_out, m_g_pre_mlp, m_g_post_mlp, m_w_up, m_w_down, v_w_mod, v_b_mod, v_g_pre_mix, v_g_post_mix, v_w_in, v_b_forget, v_swa_sinks, v_w_out, v_g_pre_mlp, v_g_post_mlp, v_w_up, v_w_down):
    S, D = x.shape[1], x.shape[2]
    n_heads = D // HEAD_DIM
    n_fox = n_heads // 2
    n_swa = n_heads - n_fox
    n_kv = max(1, n_swa // 4)
    fox_w, swa_w, kv_w = n_fox * HEAD_DIM, n_swa * HEAD_DIM, n_kv * HEAD_DIM
    main_w = 3 * fox_w + swa_w + 2 * kv_w
    in_w = main_w + n_fox
    mod_cols = w_mod.shape[2]

    ax, ay, ac = _place()
    chip = 2 * ax + ay
    dev = 2 * chip + ac
    chip_arr = jnp.reshape(chip, (1,)).astype(jnp.int32)
    core_arr = jnp.reshape(ac, (1,)).astype(jnp.int32)

    x2, tgt = x[0], loss_target[0]

    c_all, _ = _allgather8("gather_c", c.reshape(8, D // 8))
    c_all = c_all.reshape(N_DEV, D)
    b_shard = lax.dynamic_slice_in_dim(b_mod, chip * mod_cols, mod_cols, axis=1)
    mod_shard = _mod_fwd(jnp.pad(c_all, ((0, 16 - N_DEV), (0, 0))), w_mod[0], b_shard)[:N_DEV]
    mod_all, token = _allgather8("gather_mod", mod_shard)
    mod_all = mod_all.reshape(N_CHIPS, 2, N_DEV, mod_cols)[:, 0]
    mod = lax.dynamic_index_in_dim(mod_all, dev, axis=1, keepdims=False).reshape(N_MOD, 1, D)
    sh_a, sc_a, gt_a, sh_m, sc_m, gt_m = [mod[n] for n in range(N_MOD)]

    in_rows = in_w // N_CHIPS
    in_rows_pad = -(-in_rows // (2 * ROW_ALIGN)) * (2 * ROW_ALIGN)

    def rows_of(a):
        return jnp.pad(a[0].T, ((0, in_rows_pad - in_rows), (0, 0)))

    names = ["w_in", "w_out", "w_up", "w_down"]
    flights = {}
    for n, w in zip(names, [rows_of(w_in), w_out[0], w_up[0], w_down[0]]):
        shard = _tie(w, token).astype(BF16)
        flights[n] = _ici_start("gather_start_" + n, [shard], [jax.ShapeDtypeStruct((N_CHIPS,) + shard.shape, BF16)],
                                _gather_plan)
        token = flights[n][4]
    sc_a = _tie(sc_a, token)

    def gathered(n, after):
        send, recv, srcs, lands, _ = flights[n]
        srcs, lands = _ici_wait("gather_wait_" + n, send, recv, srcs, lands, _gather_plan, after)
        lands = _pass_to_sibling("gather_pass_" + n, lands)
        return lax.dynamic_update_index_in_dim(lands[0], srcs[0], chip, 0)

    d_ff = N_CHIPS * w_up.shape[2]

    h = _pre_norm(x2, g_pre_mix, sc_a, sh_a)
    def stack_rows(stack, lo, hi):
        pieces = []
        while lo < hi:
            slab, r = divmod(lo, in_rows)
            n = min(hi - lo, in_rows - r)
            pieces.append(stack[slab, r:r + n])
            lo += n
        return pieces

    w_in_stack = gathered("w_in", h)
    w_main_t = jnp.concatenate(stack_rows(w_in_stack, 0, 3 * fox_w) + stack_rows(w_in_stack, 3 * fox_w + n_fox, in_w),
                               axis=0)
    w_fg_t = jnp.pad(jnp.concatenate(stack_rows(w_in_stack, 3 * fox_w, 3 * fox_w + n_fox), axis=0),
                     ((0, LANES - n_fox), (0, 0)))
    proj = _mm_plain("in_proj", h, w_main_t, "nt", BF16, tn=_fit(768, main_w))
    fg = _mm_plain("in_proj_gate", h, w_fg_t, "nt", F32)
    b_pad = jnp.pad(b_forget, ((0, 0), (0, LANES - n_fox)))
    cum_row = _fox_gate_fwd(fg, b_pad)[:n_fox].reshape(n_fox, 1, S)
    fox_o, fox_lse = _fox_fwd(proj, cum_row, n_fox)

    cos, sin_signed = _rope_tables(S)
    rq = _rope("rope_fwd", proj, 3 * n_fox, n_swa + n_kv, cos, sin_signed)
    v_first = 3 * n_fox + n_swa + n_kv
    sinks = swa_sinks[0]
    swa_o, swa_lse = _swa_fwd(rq, proj, v_first, sinks, n_swa, n_kv)

    mixcat = jnp.concatenate([fox_o, swa_o], axis=1).astype(BF16)
    w_out_f = gathered("w_out", mixcat).reshape(D, D)
    mix = _mm_plain("out_proj", mixcat, w_out_f, "nn", F32)
    x1, h2 = _post_mix(x2, mix, g_post_mix, gt_a, g_pre_mlp, sc_m, sh_m)
    w_up_f = jnp.transpose(gathered("w_up", h2), (1, 0, 2)).reshape(D, d_ff)

    tm_u, tn_u = _fit(MM_TM, S), _fit(MM_TN, d_ff)

    def up_epilogue(acc, ex, outs):
        outs[0][...] = acc.astype(BF16)
        r = jnp.maximum(acc, 0.0)
        outs[1][...] = (r * r).astype(BF16)

    ublk = ((S, d_ff), BF16, (tm_u, tn_u), lambda i, j: (i, j))
    u, a = _matmul("mlp_up", h2, w_up_f, "nn", [ublk, ublk], up_epilogue)
    w_down_f = gathered("w_down", a).reshape(d_ff, D)
    y = _mm_plain("mlp_down", a, w_down_f, "nn", F32)

    dy, dout, loss_part, acc_mlp_post = _loss_and_post_mlp_bwd(x1, y, tgt, g_post_mlp, gt_m)
    loss = lax.psum(loss_part[0, 0], ("x", "y", "c"))

    def du_epilogue(acc, ex, outs):
        outs[0][...] = (acc * (2.0 * jnp.maximum(ex[0][...].astype(F32), 0.0))).astype(BF16)

    du = _matmul("mlp_down_bwd", dy, w_down_f, "nt", [ublk], du_epilogue,
                 extras=[(u, (tm_u, tn_u), lambda i, j: (i, j))])[0]
    def pair_start(tag, fulls):
        return _ici_start("grad_pair_start_" + tag, fulls,
                          [jax.ShapeDtypeStruct((N_CHIPS, g.shape[1] // 2, g.shape[2]), BF16) for g in fulls],
                          _pair_plan, per_source=1)

    def scatter_start(tag, pair_flights, after):
        sums = []
        for k, (send, recv, fulls, lands, _) in enumerate(pair_flights):
            fulls, from_sibling = _ici_wait("grad_pair_wait_%s_%d" % (tag, k), send, recv, fulls, lands, _pair_plan, after)
            sums += [_pair_add("pair_add_%s_%d_%d" % (tag, k, n), core_arr, g, r)
                     for n, (g, r) in enumerate(zip(fulls, from_sibling))]
        return _ici_start("grad_scatter_start_" + tag, sums,
                          [jax.ShapeDtypeStruct((3,) + p.shape[1:], BF16) for p in sums], _scatter_plan)

    def scatter_finish(tag, flight, after):
        send, recv, srcs, lands, _ = flight
        sums, received = _ici_wait("grad_scatter_wait_" + tag, send, recv, srcs, lands, _scatter_plan, after)
        return [_chip_add("chip_add_%s_%d" % (tag, k), chip_arr, p, r) for k, (p, r) in enumerate(zip(sums, received))]

    g_down = _mm_plain("grad_w_down", a, dy, "tn", BF16)
    pair_down = pair_start("down", [g_down.reshape(N_CHIPS, d_ff // N_CHIPS, D)])
    tn_s = _fit(MM_TN, w_up.shape[2])
    per = w_up.shape[2] // tn_s

    def shard_epilogue(acc, ex, outs):
        outs[0][0] = acc.astype(BF16)

    g_up = _matmul("grad_w_up", h2, du, "tn",
                   [((N_CHIPS, D, w_up.shape[2]), BF16, (1, _fit(MM_TM, D), tn_s), lambda i, j: (j // per, i, j % per))],
                   shard_epilogue, extras=[_behind(pair_down[4])], tn=tn_s)[0]
    pair_up = pair_start("up", [g_up])
    dh2 = _mm_plain("mlp_up_bwd", du, w_up_f, "nt", F32, after=pair_up[4])
    flight_mlp = scatter_start("mlp", [pair_up, pair_down], dh2)
    dx1, dmix, acc_mid = _pre_mlp_and_post_mix_bwd(dh2, x1, dout, mix, _tie(g_pre_mlp, flight_mlp[4]), sc_m,
                                                   g_post_mix, gt_a)

    dmixcat = _mm_plain("out_proj_bwd", dmix, w_out_f, "nt", F32)
    g_out = _mm_plain("grad_w_out", mixcat, dmix, "tn", BF16)

    fdq, fdk, fdv, dcum_row, dcum_q = _fox_bwd(proj, fox_o, dmixcat, fox_lse, cum_row, n_fox)
    dcum_k = jnp.pad(dcum_row.reshape(n_fox, S), ((0, LANES - n_fox), (0, 0)))
    dfg, db_forget = _fox_gate_bwd(dcum_k, dcum_q, fg, b_pad)

    group_w = (n_swa // n_kv) * HEAD_DIM
    sdq, sdk, sdv, dsink = _swa_bwd(rq, proj, v_first, sinks, swa_o, dmixcat, fox_w // group_w, swa_lse, n_swa, n_kv)
    drq = jnp.concatenate([sdq, jnp.transpose(sdk, (1, 0, 2)).reshape(S, kv_w).astype(BF16)], axis=1)
    d_sq_sk = _rope("rope_bwd", drq, 0, n_swa + n_kv, cos, -sin_signed)
    dsv = jnp.transpose(sdv, (1, 0, 2)).reshape(S, kv_w).astype(BF16)
    dproj = jnp.concatenate([fdq, fdk, fdv, d_sq_sk, dsv], axis=1)

    g_main_t = _mm_plain("grad_w_in", dproj, h, "tn", BF16, tm=_fit(768, main_w))
    g_fg_t = _mm_plain("grad_w_in_gate", dfg, h, "tn", BF16)

    g_in_t = jnp.concatenate([g_main_t[:3 * fox_w], g_fg_t[:n_fox], g_main_t[3 * fox_w:]], axis=0)
    g_in_t = jnp.pad(g_in_t.reshape(N_CHIPS, in_rows, D), ((0, 0), (0, in_rows_pad - in_rows), (0, 0)))
    pair_mix = pair_start("mix", [g_in_t, g_out.reshape(N_CHIPS, D // N_CHIPS, D)])
    dh_gate = _mm_plain("in_proj_gate_bwd", dfg, w_fg_t, "nn", F32, after=pair_mix[4])

    def add_epilogue(acc, ex, outs):
        outs[0][...] = acc + ex[0][...]

    tm_h, tn_h = _fit(MM_TM, S), _fit(MM_TN, D)
    dh = _matmul("in_proj_bwd", dproj, w_main_t, "nn", [((S, D), F32, (tm_h, tn_h), lambda i, j: (i, j))], add_epilogue,
                 extras=[(dh_gate, (tm_h, tn_h), lambda i, j: (i, j))], tk=_fit(2304, main_w))[0]
    grad_x, acc_pre = _pre_mix_bwd(dh, x2, dx1, g_pre_mix, sc_a)

    zero_row = jnp.zeros((1, D), F32)
    tail = jnp.concatenate([db_forget[0:1, :n_fox], dsink[:, 0, :n_swa // n_kv].reshape(1, n_swa),
                            jnp.zeros((1, D - n_fox - n_swa), F32)], axis=1)
    partial = jnp.concatenate([
        acc_pre[0:1], acc_pre[1:2], acc_mid[3:4], acc_mid[0:1], acc_mid[1:2], acc_mlp_post[0:1],
        acc_pre[2:3], acc_mid[4:5], acc_mid[2:3], acc_mlp_post[1:2], tail] + [zero_row] * 5, axis=0)
    gathered_small, token = _allgather8("gather_small_grads", partial)

    flight_mix = scatter_start("mix", [pair_mix], token)
    halves_mlp = scatter_finish("mlp", flight_mlp, flight_mix[4])
    share_mlp = _ici_start("grad_share_start_mlp", halves_mlp,
                           [jax.ShapeDtypeStruct(hv.shape, F32) for hv in halves_mlp], _share_plan, per_source=1)

    def pack(bm, gpm, gqm, gpl, gql, bf, sk):
        last = jnp.concatenate([bf, sk, jnp.zeros((1, D - n_fox - n_swa), F32)], axis=1)
        return jnp.concatenate([bm.reshape(N_MOD, D), gpm, gqm, gpl, gql, last, jnp.zeros((5, D), F32)], axis=0)

    def unpack(p):
        return {"b_mod": p[0:N_MOD].reshape(1, N_MOD * D), "g_pre_mix": p[6:7], "g_post_mix": p[7:8],
                "g_pre_mlp": p[8:9], "g_post_mlp": p[9:10], "b_forget": p[10:11, :n_fox],
                "swa_sinks": p[10:11, n_fox:n_fox + n_swa]}

    small_out = _small_update(
        gathered_small, _tie(pack(b_mod, g_pre_mix, g_post_mix, g_pre_mlp, g_post_mlp, b_forget, swa_sinks), share_mlp[4]),
        pack(m_b_mod, m_g_pre_mix, m_g_post_mix, m_g_pre_mlp, m_g_post_mlp, m_b_forget, m_swa_sinks),
        pack(v_b_mod, v_g_pre_mix, v_g_post_mix, v_g_pre_mlp, v_g_post_mlp, v_b_forget, v_swa_sinks))
    g_small, d_small, m_small, v_small = [unpack(p) for p in small_out]

    dmod_all = gathered_small.reshape(N_DEV, 16, D)[:, :N_MOD].reshape(N_DEV, N_MOD * D)
    dmod_shard = _tie(lax.dynamic_slice_in_dim(dmod_all, chip * mod_cols, mod_cols, axis=1), share_mlp[4])
    g_w_mod, d_w_mod, nm_w_mod, nv_w_mod = _mod_update(c_all.T, dmod_shard, w_mod[0], m_w_mod[0], v_w_mod[0])
    send, recv, halves_mlp, lands, _ = share_mlp
    halves_mlp, others_mlp = _ici_wait("grad_share_wait_mlp", send, recv, halves_mlp, lands, _share_plan,
                                       d_w_mod[:8, :LANES] + small_out[1][:8, :LANES])

    grads = dict(g_small, w_mod=g_w_mod[None])
    deltas = dict(d_small, w_mod=d_w_mod[None])
    new_m = dict(m_small, w_mod=nm_w_mod[None])
    new_v = dict(v_small, w_mod=nv_w_mod[None])
    weights = {"w_in": (w_in, m_w_in, v_w_in), "w_out": (w_out, m_w_out, v_w_out), "w_up": (w_up, m_w_up, v_w_up),
               "w_down": (w_down, m_w_down, v_w_down)}

    def big_update(n, own, other):
        transposed = n == "w_in"
        w, m, v = [rows_of(a) if transposed else a[0] for a in weights[n]]
        outs = _adam_halves("adam_" + n, core_arr, w, own, other, m, v)
        if transposed:
            outs = [o[:in_rows].T for o in outs]
        grads[n], deltas[n], new_m[n], new_v[n] = [o[None] for o in outs]

    big_update("w_up", halves_mlp[0], others_mlp[0])
    big_update("w_down", halves_mlp[1], others_mlp[1])
    ran = deltas["w_down"][0, :8, :LANES] + deltas["w_up"][0, :8, :LANES] + d_w_mod[:8, :LANES]
    halves_mix = scatter_finish("mix", flight_mix, ran)
    others_mix = _pair_share("grad_pair_share_mix", halves_mix)
    big_update("w_in", halves_mix[0], others_mix[0])
    big_update("w_out", halves_mix[1], others_mix[1])

    order = ["w_mod", "b_mod", "g_pre_mix", "g_post_mix", "w_in", "b_forget", "swa_sinks", "w_out", "g_pre_mlp",
             "g_post_mlp", "w_up", "w_down"]
    return (loss, grad_x[None], *[grads[n] for n in order], *[deltas[n] for n in order],
            *[new_m[n] for n in order], *[new_v[n] for n in order])
```

```python
import jax
import jax.numpy as jnp
from jax import lax
from jax.experimental import pallas as pl
from jax.experimental.pallas import tpu as pltpu

F32 = jnp.float32
BF16 = jnp.bfloat16
MESH = pl.DeviceIdType.MESH

HEAD_DIM = 128
SWA_BLOCK = 128
ROPE_THETA = 10000.0
NORM_EPS = 1e-6
NEG = -1e30
N_MOD = 6
ADAM_LR = 0.001
ADAM_B1 = 0.9
ADAM_B2 = 0.999
ADAM_EPS = 1e-08
ADAM_WD = 0.01
ADAM_STEP = 10
N_CHIPS = 4
N_DEV = 8
LANES = 128
VMEM_CAP = 60 * 1024 * 1024

_NN = (((1,), (0,)), ((), ()))
_NT = (((1,), (1,)), ((), ()))
_TN = (((0,), (0,)), ((), ()))


def _vmem(nbytes):
    return int(min(VMEM_CAP, nbytes * 5 // 4 + (4 << 20)))


def _nbytes(shape, dtype):
    n = 1
    for s in shape:
        n *= s
    return n * jnp.dtype(dtype).itemsize


def _fit(t, n):
    t = min(t, n)
    assert n % t == 0, (t, n)
    return t


MM_TM, MM_TN, MM_TK = 512, 1024, 2048


def _matmul(name, a, b, mode, out_defs, epilogue, extras=(), tm=MM_TM, tn=MM_TN, tk=MM_TK):
    if mode == "nn":
        (M, K), (K2, N) = a.shape, b.shape
    elif mode == "nt":
        (M, K), (N, K2) = a.shape, b.shape
    else:
        (K, M), (K2, N) = a.shape, b.shape
    assert K == K2, (a.shape, b.shape, mode)
    tm, tn, tk = _fit(tm, M), _fit(tn, N), _fit(tk, K)
    nk = K // tk
    dims = {"nn": _NN, "nt": _NT, "tn": _TN}[mode]
    a_spec = (pl.BlockSpec((tk, tm), lambda i, j, k: (k, i)) if mode == "tn"
              else pl.BlockSpec((tm, tk), lambda i, j, k: (i, k)))
    b_spec = (pl.BlockSpec((tn, tk), lambda i, j, k: (j, k)) if mode == "nt"
              else pl.BlockSpec((tk, tn), lambda i, j, k: (k, j)))
    n_ex, n_out = len(extras), len(out_defs)

    def body(*refs):
        a_ref, b_ref = refs[0], refs[1]
        ex = refs[2:2 + n_ex]
        outs = refs[2 + n_ex:2 + n_ex + n_out]
        prod = lax.dot_general(a_ref[...], b_ref[...], dims, preferred_element_type=F32)
        if nk == 1:
            epilogue(prod, ex, outs)
        else:
            acc_ref = refs[-1]
            k = pl.program_id(2)

            @pl.when(k == 0)
            def _():
                acc_ref[...] = prod

            @pl.when(k > 0)
            def _():
                acc_ref[...] += prod

            @pl.when(k == nk - 1)
            def _():
                epilogue(acc_ref[...], ex, outs)

    def wrap(f):
        return lambda i, j, k: f(i, j)

    in_specs = [a_spec, b_spec] + [pl.BlockSpec(blk, wrap(f)) for _, blk, f in extras]
    out_specs = [pl.BlockSpec(blk, wrap(f)) for _, _, blk, f in out_defs]
    out_shape = [jax.ShapeDtypeStruct(s, d) for s, d, _, _ in out_defs]
    need = 2 * (tm * tk + tk * tn) * a.dtype.itemsize + 3 * tm * tn * 4
    need += sum(2 * _nbytes(blk, arr.dtype) for arr, blk, _ in extras)
    need += sum(2 * _nbytes(blk, d) for _, d, blk, _ in out_defs)
    res = pl.pallas_call(
        body, name=name, grid=(M // tm, N // tn, nk),
        in_specs=in_specs, out_specs=out_specs, out_shape=out_shape,
        scratch_shapes=[pltpu.VMEM((tm, tn), F32)] if nk > 1 else [],
        compiler_params=pltpu.CompilerParams(
            dimension_semantics=("parallel", "parallel", "arbitrary"), vmem_limit_bytes=_vmem(need)),
    )(a, b, *[arr for arr, _, _ in extras])
    return res


def _behind(token):
    return (token, (8, LANES), lambda i, j: (0, 0))


def _mm_plain(name, a, b, mode, out_dtype, after=None, **tiles):
    if mode == "nn":
        M, N = a.shape[0], b.shape[1]
    elif mode == "nt":
        M, N = a.shape[0], b.shape[0]
    else:
        M, N = a.shape[1], b.shape[1]
    tm, tn = _fit(tiles.get("tm", MM_TM), M), _fit(tiles.get("tn", MM_TN), N)

    def epi(acc, ex, outs):
        outs[0][...] = acc.astype(out_dtype)

    return _matmul(name, a, b, mode, [((M, N), out_dtype, (tm, tn), lambda i, j: (i, j))], epi,
                   extras=[] if after is None else [_behind(after)], **tiles)[0]


def _rstd(v):
    return lax.rsqrt(jnp.mean(v * v, axis=-1, keepdims=True) + NORM_EPS)


def _row_call(name, body, row_ins, vec_ins, row_outs, acc_outs, S, D, tr):
    tr = _fit(tr, S)
    row_spec = pl.BlockSpec((tr, D), lambda r: (r, 0))
    vec_spec = pl.BlockSpec((1, D), lambda r: (0, 0))
    in_specs = [row_spec] * len(row_ins) + [vec_spec] * len(vec_ins)
    out_specs = [row_spec] * len(row_outs) + [pl.BlockSpec(shp, lambda r: (0, 0)) for shp in acc_outs]
    out_shape = [jax.ShapeDtypeStruct((S, D), d) for d in row_outs] + [jax.ShapeDtypeStruct(shp, F32) for shp in acc_outs]
    need = sum(2 * tr * D * a.dtype.itemsize for a in row_ins) + sum(2 * tr * D * jnp.dtype(d).itemsize for d in row_outs)
    need += 8 * tr * D * 4
    return pl.pallas_call(
        body, name=name, grid=(S // tr,), in_specs=in_specs, out_specs=out_specs, out_shape=out_shape,
        compiler_params=pltpu.CompilerParams(dimension_semantics=("arbitrary",), vmem_limit_bytes=_vmem(need)),
    )(*row_ins, *vec_ins)


def _acc_rows(ref, rows):
    @pl.when(pl.program_id(0) == 0)
    def _():
        ref[...] = jnp.zeros_like(ref)
    for n, r in enumerate(rows):
        ref[n:n + 1, :] += r


def _pre_norm(x, g, sc, sh):
    S, D = x.shape

    def body(x_ref, g_ref, sc_ref, sh_ref, h_ref):
        xv = x_ref[...]
        xn = xv * _rstd(xv)
        h_ref[...] = (xn * g_ref[...] * (1.0 + sc_ref[...]) + sh_ref[...]).astype(BF16)

    return _row_call("pre_norm_mix", body, [x], [g, sc, sh], [BF16], [], S, D, 256)[0]


def _post_mix(x, mix, g_post, gt, g_pre, sc, sh):
    S, D = x.shape

    def body(x_ref, mix_ref, gp_ref, gt_ref, g2_ref, sc_ref, sh_ref, x1_ref, h2_ref):
        mv = mix_ref[...]
        x1 = x_ref[...] + gt_ref[...] * (mv * _rstd(mv) * gp_ref[...])
        x1_ref[...] = x1
        h2_ref[...] = (x1 * _rstd(x1) * g2_ref[...] * (1.0 + sc_ref[...]) + sh_ref[...]).astype(BF16)

    return _row_call("post_mix_pre_mlp", body, [x, mix], [g_post, gt, g_pre, sc, sh], [F32, BF16], [], S, D, 256)


def _loss_and_post_mlp_bwd(x1, y, target, g_post, gt):
    S, D = x1.shape

    def body(x1_ref, y_ref, t_ref, g_ref, gt_ref, dy_ref, dout_ref, loss_ref, acc_ref):
        yv = y_ref[...]
        r = _rstd(yv)
        yh = yv * r
        n = yh * g_ref[...]
        diff = x1_ref[...] + gt_ref[...] * n - t_ref[...]
        dout = diff * (1.0 / D)
        dout_ref[...] = dout
        dn = dout * gt_ref[...]
        dyh = dn * g_ref[...]
        dy_ref[...] = (r * (dyh - yh * jnp.mean(dyh * yh, axis=-1, keepdims=True))).astype(BF16)
        _acc_rows(acc_ref, [jnp.sum(dout * n, axis=0, keepdims=True), jnp.sum(dn * yh, axis=0, keepdims=True)])

        @pl.when(pl.program_id(0) == 0)
        def _():
            loss_ref[...] = jnp.zeros_like(loss_ref)
        loss_ref[...] += jnp.full(loss_ref.shape, (0.5 / D) * jnp.sum(diff * diff), F32)

    return _row_call("loss_post_mlp_bwd", body, [x1, y, target], [g_post, gt], [BF16, F32],
                     [(8, LANES), (8, D)], S, D, 128)


def _pre_mlp_and_post_mix_bwd(dh2, x1, dout, mix, g_pre, sc, g_post, gt):
    S, D = x1.shape

    def body(dh_ref, x1_ref, dout_ref, mix_ref, g_ref, sc_ref, gp_ref, gt_ref, dx1_ref, dmix_ref, acc_ref):
        dh = dh_ref[...]
        x1v = x1_ref[...]
        r3 = _rstd(x1v)
        xn = x1v * r3
        dxn = dh * (1.0 + sc_ref[...]) * g_ref[...]
        dx1 = dout_ref[...] + r3 * (dxn - xn * jnp.mean(dxn * xn, axis=-1, keepdims=True))
        dx1_ref[...] = dx1
        mv = mix_ref[...]
        r2 = _rstd(mv)
        mh = mv * r2
        dn = dx1 * gt_ref[...]
        dmh = dn * gp_ref[...]
        dmix_ref[...] = (r2 * (dmh - mh * jnp.mean(dmh * mh, axis=-1, keepdims=True))).astype(BF16)
        _acc_rows(acc_ref, [
            jnp.sum(dh, axis=0, keepdims=True),
            jnp.sum(dh * xn * g_ref[...], axis=0, keepdims=True),
            jnp.sum(dh * (1.0 + sc_ref[...]) * xn, axis=0, keepdims=True),
            jnp.sum(dx1 * mh * gp_ref[...], axis=0, keepdims=True),
            jnp.sum(dn * mh, axis=0, keepdims=True)])

    return _row_call("pre_mlp_post_mix_bwd", body, [dh2, x1, dout, mix], [g_pre, sc, g_post, gt], [F32, BF16],
                     [(8, D)], S, D, 128)


def _pre_mix_bwd(dh, x, dx1, g_pre, sc):
    S, D = x.shape

    def body(dh_ref, x_ref, dx1_ref, g_ref, sc_ref, gx_ref, acc_ref):
        dhv = dh_ref[...]
        xv = x_ref[...]
        r = _rstd(xv)
        xn = xv * r
        dxn = dhv * (1.0 + sc_ref[...]) * g_ref[...]
        gx_ref[...] = dx1_ref[...] + r * (dxn - xn * jnp.mean(dxn * xn, axis=-1, keepdims=True))
        _acc_rows(acc_ref, [
            jnp.sum(dhv, axis=0, keepdims=True),
            jnp.sum(dhv * xn * g_ref[...], axis=0, keepdims=True),
            jnp.sum(dhv * (1.0 + sc_ref[...]) * xn, axis=0, keepdims=True)])

    return _row_call("pre_mix_bwd", body, [dh, x, dx1], [g_pre, sc], [F32], [(8, D)], S, D, 128)


CUM_BLOCK = 256


def _tri(n, upper):
    r = lax.broadcasted_iota(jnp.int32, (n, n), 0)
    c = lax.broadcasted_iota(jnp.int32, (n, n), 1)
    return ((c >= r) if upper else (c <= r)).astype(F32)


def _fox_gate_fwd(fg, b_pad):
    S = fg.shape[0]
    cb = _fit(CUM_BLOCK, S)

    def body(fg_ref, b_ref, cumt_ref, cum_ref):
        low = _tri(cb, False)
        carry = jnp.zeros((1, LANES), F32)
        for n in range(S // cb):
            z = fg_ref[n * cb:(n + 1) * cb, :] + b_ref[...]
            logf = jnp.minimum(z, 0.0) - jnp.log(1.0 + jnp.exp(-jnp.abs(z)))
            blk = jnp.dot(low, logf, precision=lax.Precision.HIGHEST, preferred_element_type=F32) + carry
            cum_ref[n * cb:(n + 1) * cb, :] = blk
            carry = blk[cb - 1:cb, :]
        cumt_ref[...] = cum_ref[...].T

    return pl.pallas_call(
        body, name="fox_gate_fwd", out_shape=jax.ShapeDtypeStruct((LANES, S), F32),
        scratch_shapes=[pltpu.VMEM((S, LANES), F32)],
        compiler_params=pltpu.CompilerParams(vmem_limit_bytes=_vmem(6 * S * LANES * 4)),
    )(fg, b_pad)


def _fox_gate_bwd(dcum_k, dcum_q, fg, b_pad):
    S = fg.shape[0]
    n_fox = dcum_q.shape[0]
    cb = _fit(CUM_BLOCK, S)

    def body(dk_ref, dq_ref, fg_ref, b_ref, dfg_ref, db_ref, dc_ref):
        lane = lax.broadcasted_iota(jnp.int32, (S, LANES), 1)
        dc = dk_ref[...].T
        for h in range(n_fox):
            dc = dc + jnp.where(lane == h, dq_ref[h], 0.0)
        dc_ref[...] = dc
        up = _tri(cb, True)
        carry = jnp.zeros((1, LANES), F32)
        db = jnp.zeros((1, LANES), F32)
        for n in reversed(range(S // cb)):
            blk = jnp.dot(up, dc_ref[n * cb:(n + 1) * cb, :], precision=lax.Precision.HIGHEST,
                          preferred_element_type=F32) + carry
            carry = blk[0:1, :]
            z = fg_ref[n * cb:(n + 1) * cb, :] + b_ref[...]
            dfg = blk * (1.0 / (1.0 + jnp.exp(z)))
            dfg_ref[n * cb:(n + 1) * cb, :] = dfg.astype(BF16)
            db = db + jnp.sum(dfg, axis=0, keepdims=True)
        db_ref[...] = jnp.broadcast_to(db, db_ref.shape)

    return pl.pallas_call(
        body, name="fox_gate_bwd",
        out_shape=[jax.ShapeDtypeStruct((S, LANES), BF16), jax.ShapeDtypeStruct((8, LANES), F32)],
        scratch_shapes=[pltpu.VMEM((S, LANES), F32)],
        compiler_params=pltpu.CompilerParams(vmem_limit_bytes=_vmem((8 + 2 * n_fox) * S * LANES * 4)),
    )(dcum_k, dcum_q, fg, b_pad)


FOX_TILE = 512


LOG2E = 1.4426950408889634


def _fox_scores(q, k, ck2, masked, t):
    s = lax.dot_general(q, k, _NT, preferred_element_type=F32) * (HEAD_DIM ** -0.5 * LOG2E) - ck2
    if masked:
        row = lax.broadcasted_iota(jnp.int32, (t, t), 0)
        col = lax.broadcasted_iota(jnp.int32, (t, t), 1)
        s = jnp.where(col <= row, s, NEG)
    return s


def _fox_fwd(proj, cum_row, n_fox):
    S = proj.shape[0]
    t = _fit(FOX_TILE, S)
    nq = S // t

    def body(q_ref, k_ref, v_ref, ck_ref, o_ref, lse_ref):
        def q_block(qi, _):
            q0 = pl.multiple_of(qi * t, t)
            q = q_ref[pl.ds(q0, t), :]

            def kv_block(j, carry, masked):
                m, l, acc = carry
                k0 = pl.multiple_of(j * t, t)
                s = _fox_scores(q, k_ref[pl.ds(k0, t), :], ck_ref[0, :, pl.ds(k0, t)] * LOG2E, masked, t)
                m_new = jnp.maximum(m, jnp.max(s, axis=-1, keepdims=True))
                alpha = jnp.exp2(m - m_new)
                p = jnp.exp2(s - m_new)
                l = alpha * l + jnp.sum(p, axis=-1, keepdims=True)
                acc = alpha * acc + jnp.dot(p.astype(BF16), v_ref[pl.ds(k0, t), :], preferred_element_type=F32)
                return m_new, l, acc

            init = (jnp.full((t, 1), NEG, F32), jnp.zeros((t, 1), F32), jnp.zeros((t, HEAD_DIM), F32))
            carry = lax.fori_loop(0, qi, lambda j, cr: kv_block(j, cr, False), init)
            m, l, acc = kv_block(qi, carry, True)
            o_ref[pl.ds(q0, t), :] = acc / l
            lse_ref[0, pl.ds(q0, t), :] = jnp.broadcast_to(m + jnp.log(l) * LOG2E, (t, LANES))
            return 0

        lax.fori_loop(0, nq, q_block, 0)

    col = lambda off: pl.BlockSpec((S, HEAD_DIM), lambda h: (0, off + h))
    per_head = pl.BlockSpec((1, S, LANES), lambda h: (h, 0, 0))
    return pl.pallas_call(
        body, name="fox_fwd", grid=(n_fox,),
        in_specs=[col(0), col(n_fox), col(2 * n_fox), pl.BlockSpec((1, 1, S), lambda h: (h, 0, 0))],
        out_specs=[pl.BlockSpec((S, HEAD_DIM), lambda h: (0, h)), per_head],
        out_shape=[jax.ShapeDtypeStruct((S, n_fox * HEAD_DIM), F32), jax.ShapeDtypeStruct((n_fox, S, LANES), F32)],
        compiler_params=pltpu.CompilerParams(dimension_semantics=("parallel",),
                                             vmem_limit_bytes=_vmem(16 * S * HEAD_DIM * 4 + 12 * t * t * 4)),
    )(proj, proj, proj, cum_row)


def _fox_bwd(proj, o, do, lse_b, cum_row, n_fox):
    S = proj.shape[0]
    t = _fit(FOX_TILE, S)
    nq = S // t
    scale = HEAD_DIM ** -0.5

    def body(q_ref, k_ref, v_ref, o_ref, do_ref, lse_ref, ck_ref, dq_ref, dk_ref, dv_ref, dc_ref, dcq_ref,
             dq_acc, delta_ref):
        dq_acc[...] = jnp.zeros_like(dq_acc)
        dcq_ref[...] = jnp.zeros_like(dcq_ref)

        def delta_block(qi, _):
            q0 = pl.multiple_of(qi * t, t)
            d = jnp.sum(do_ref[pl.ds(q0, t), :] * o_ref[pl.ds(q0, t), :], axis=-1, keepdims=True)
            delta_ref[pl.ds(q0, t), :] = jnp.broadcast_to(d, (t, LANES))
            return 0

        lax.fori_loop(0, nq, delta_block, 0)

        def kv_block(j, _):
            k0 = pl.multiple_of(j * t, t)
            k = k_ref[pl.ds(k0, t), :]
            v = v_ref[pl.ds(k0, t), :]
            ck2 = ck_ref[0, :, pl.ds(k0, t)] * LOG2E

            def q_block(qi, carry, masked):
                dk, dv, dc = carry
                q0 = pl.multiple_of(qi * t, t)
                q = q_ref[pl.ds(q0, t), :]
                dov = do_ref[pl.ds(q0, t), :].astype(BF16)
                p = jnp.exp2(_fox_scores(q, k, ck2, masked, t) - lse_ref[0, pl.ds(q0, t), :][:, :1])
                dp = lax.dot_general(dov, v, _NT, preferred_element_type=F32)
                ds = p * (dp - delta_ref[pl.ds(q0, t), :][:, :1])
                dsb = ds.astype(BF16)
                dv = dv + lax.dot_general(p.astype(BF16), dov, _TN, preferred_element_type=F32)
                dk = dk + lax.dot_general(dsb, q, _TN, preferred_element_type=F32)
                dq_acc[pl.ds(q0, t), :] += jnp.dot(dsb, k, preferred_element_type=F32)
                dc = dc - jnp.sum(ds, axis=0, keepdims=True)
                dcq_ref[0, pl.ds(q0, t), :] += jnp.broadcast_to(jnp.sum(ds, axis=1, keepdims=True), (t, LANES))
                return dk, dv, dc

            init = (jnp.zeros((t, HEAD_DIM), F32), jnp.zeros((t, HEAD_DIM), F32), jnp.zeros((1, t), F32))
            carry = q_block(j, init, True)
            dk, dv, dc = lax.fori_loop(j + 1, nq, lambda qi, cr: q_block(qi, cr, False), carry)
            dk_ref[pl.ds(k0, t), :] = (dk * scale).astype(BF16)
            dv_ref[pl.ds(k0, t), :] = dv.astype(BF16)
            dc_ref[0, :, pl.ds(k0, t)] = dc
            return 0

        lax.fori_loop(0, nq, kv_block, 0)
        dq_ref[...] = (dq_acc[...] * scale).astype(BF16)

    col = lambda off: pl.BlockSpec((S, HEAD_DIM), lambda h: (0, off + h))
    per_head = pl.BlockSpec((1, S, LANES), lambda h: (h, 0, 0))
    row = pl.BlockSpec((1, 1, S), lambda h: (h, 0, 0))
    grad = jax.ShapeDtypeStruct((S, n_fox * HEAD_DIM), BF16)
    return pl.pallas_call(
        body, name="fox_bwd", grid=(n_fox,),
        in_specs=[col(0), col(n_fox), col(2 * n_fox), col(0), col(0), per_head, row],
        out_specs=[col(0), col(0), col(0), row, per_head],
        out_shape=[grad, grad, grad, jax.ShapeDtypeStruct((n_fox, 1, S), F32), jax.ShapeDtypeStruct((n_fox, S, LANES), F32)],
        scratch_shapes=[pltpu.VMEM((S, HEAD_DIM), F32), pltpu.VMEM((S, LANES), F32)],
        compiler_params=pltpu.CompilerParams(dimension_semantics=("parallel",),
                                             vmem_limit_bytes=_vmem(24 * S * HEAD_DIM * 4 + 16 * t * t * 4)),
    )(proj, proj, proj, o, do, lse_b, cum_row)


def _rope_tables(S):
    half = HEAD_DIM // 2
    inv_freq = 1.0 / (ROPE_THETA ** (jnp.arange(half, dtype=F32) * (2.0 / HEAD_DIM)))
    ang = jnp.arange(S).astype(F32)[:, None] * inv_freq[None, :]
    cos, sin = jnp.cos(ang), jnp.sin(ang)
    return jnp.concatenate([cos, cos], axis=-1), jnp.concatenate([-sin, sin], axis=-1)


def _rope(name, src, first_block, n_blocks, cos, sin_signed):
    S = src.shape[0]

    def body(x_ref, cos_ref, sin_ref, o_ref):
        xv = x_ref[...].astype(F32)
        o_ref[...] = (xv * cos_ref[...] + pltpu.roll(xv, HEAD_DIM // 2, 1) * sin_ref[...]).astype(BF16)

    table = pl.BlockSpec((S, HEAD_DIM), lambda n: (0, 0))
    return pl.pallas_call(
        body, name=name, grid=(n_blocks,),
        in_specs=[pl.BlockSpec((S, HEAD_DIM), lambda n: (0, first_block + n)), table, table],
        out_specs=pl.BlockSpec((S, HEAD_DIM), lambda n: (0, n)),
        out_shape=jax.ShapeDtypeStruct((S, n_blocks * HEAD_DIM), BF16),
        compiler_params=pltpu.CompilerParams(dimension_semantics=("parallel",),
                                             vmem_limit_bytes=_vmem(12 * S * HEAD_DIM * 4)),
    )(src, cos, sin_signed)


def _swa_tile(q_ref, kp_ref, kc_ref, n, group, scale):
    B = SWA_BLOCK
    qs = jnp.concatenate([q_ref[:, g * HEAD_DIM:(g + 1) * HEAD_DIM] for g in range(group)], axis=0)
    kcat = jnp.concatenate([kp_ref[...], kc_ref[...]], axis=0)
    s = lax.dot_general(qs, kcat, _NT, preferred_element_type=F32) * scale
    qi = lax.broadcasted_iota(jnp.int32, (group * B, 2 * B), 0) % B
    kj = lax.broadcasted_iota(jnp.int32, (group * B, 2 * B), 1)
    diff = qi + B - kj
    mask = (diff >= 0) & (diff < B) & ((n * B + kj - B) >= 0)
    return qs, kcat, jnp.where(mask, s, NEG)


def _swa_sink_col(sink_ref, kv, group):
    head = lax.broadcasted_iota(jnp.int32, (group * SWA_BLOCK, 1), 0) // SWA_BLOCK
    col = jnp.zeros((group * SWA_BLOCK, 1), F32)
    for g in range(group):
        col = jnp.where(head == g, sink_ref[kv * group + g], col)
    return col


def _swa_specs(n_kv, group, q_first, k_first, v_first):
    B = SWA_BLOCK
    prev = lambda n: jnp.maximum(n - 1, 0)
    return [
        pl.BlockSpec((B, group * HEAD_DIM), lambda kv, n: (n, q_first + kv)),
        pl.BlockSpec((B, HEAD_DIM), lambda kv, n: (prev(n), k_first + kv)),
        pl.BlockSpec((B, HEAD_DIM), lambda kv, n: (n, k_first + kv)),
        pl.BlockSpec((B, HEAD_DIM), lambda kv, n: (prev(n), v_first + kv)),
        pl.BlockSpec((B, HEAD_DIM), lambda kv, n: (n, v_first + kv)),
    ]


def _swa_fwd(rq, proj, v_first, sinks, n_q, n_kv):
    S = rq.shape[0]
    B = SWA_BLOCK
    group = n_q // n_kv
    scale = HEAD_DIM ** -0.5

    def body(q_ref, kp_ref, kc_ref, vp_ref, vc_ref, sink_ref, o_ref, lse_ref):
        kv, n = pl.program_id(0), pl.program_id(1)
        _, _, s = _swa_tile(q_ref, kp_ref, kc_ref, n, group, scale)
        sink = _swa_sink_col(sink_ref, kv, group)
        m = jnp.maximum(jnp.max(s, axis=-1, keepdims=True), sink)
        p = jnp.exp(s - m)
        denom = jnp.sum(p, axis=-1, keepdims=True) + jnp.exp(sink - m)
        vcat = jnp.concatenate([vp_ref[...], vc_ref[...]], axis=0)
        o = jnp.dot((p / denom).astype(BF16), vcat, preferred_element_type=F32)
        lse = m + jnp.log(denom)
        for g in range(group):
            o_ref[:, g * HEAD_DIM:(g + 1) * HEAD_DIM] = o[g * B:(g + 1) * B, :]
            lse_ref[0, :, g * LANES:(g + 1) * LANES] = jnp.broadcast_to(lse[g * B:(g + 1) * B, :], (B, LANES))

    specs = _swa_specs(n_kv, group, 0, n_q, v_first)
    q_blk = pl.BlockSpec((B, group * HEAD_DIM), lambda kv, n: (n, kv))
    return pl.pallas_call(
        body, name="swa_fwd", grid=(n_kv, S // B),
        in_specs=specs + [pl.BlockSpec(memory_space=pltpu.SMEM)],
        out_specs=[q_blk, pl.BlockSpec((1, B, group * LANES), lambda kv, n: (kv, n, 0))],
        out_shape=[jax.ShapeDtypeStruct((S, n_q * HEAD_DIM), F32), jax.ShapeDtypeStruct((n_kv, S, group * LANES), F32)],
        compiler_params=pltpu.CompilerParams(dimension_semantics=("parallel", "arbitrary")),
    )(rq, rq, rq, proj, proj, sinks)


def _swa_bwd(rq, proj, v_first, sinks, o, do, do_first, lse_b, n_q, n_kv):
    S = rq.shape[0]
    B = SWA_BLOCK
    group = n_q // n_kv
    scale = HEAD_DIM ** -0.5

    def body(q_ref, kp_ref, kc_ref, vp_ref, vc_ref, o_ref, do_ref, lse_ref, sink_ref,
             dq_ref, dk_ref, dv_ref, dsink_ref):
        kv, n = pl.program_id(0), pl.program_id(1)

        @pl.when(n == 0)
        def _():
            dk_ref[...] = jnp.zeros_like(dk_ref)
            dv_ref[...] = jnp.zeros_like(dv_ref)
            dsink_ref[...] = jnp.zeros_like(dsink_ref)

        qs, kcat, s = _swa_tile(q_ref, kp_ref, kc_ref, n, group, scale)
        sink = _swa_sink_col(sink_ref, kv, group)
        stack = lambda ref, w: jnp.concatenate([ref[:, g * w:(g + 1) * w] for g in range(group)], axis=0)
        lse = jnp.concatenate([lse_ref[0, :, g * LANES:g * LANES + 1] for g in range(group)], axis=0)
        do32 = stack(do_ref, HEAD_DIM)
        delta = jnp.sum(do32 * stack(o_ref, HEAD_DIM), axis=-1, keepdims=True)
        dov = do32.astype(BF16)
        p = jnp.exp(s - lse)
        vcat = jnp.concatenate([vp_ref[...], vc_ref[...]], axis=0)
        dp = lax.dot_general(dov, vcat, _NT, preferred_element_type=F32)
        ds = p * (dp - delta)
        dsb = ds.astype(BF16)
        dq = jnp.dot(dsb, kcat, preferred_element_type=F32) * scale
        for g in range(group):
            dq_ref[:, g * HEAD_DIM:(g + 1) * HEAD_DIM] = dq[g * B:(g + 1) * B, :].astype(BF16)
        dkcat = lax.dot_general(dsb, qs, _TN, preferred_element_type=F32) * scale
        dvcat = lax.dot_general(p.astype(BF16), dov, _TN, preferred_element_type=F32)
        prev0 = pl.multiple_of(jnp.maximum(n - 1, 0) * B, B)
        cur0 = pl.multiple_of(n * B, B)
        dk_ref[0, pl.ds(prev0, B), :] += dkcat[:B, :]
        dk_ref[0, pl.ds(cur0, B), :] += dkcat[B:, :]
        dv_ref[0, pl.ds(prev0, B), :] += dvcat[:B, :]
        dv_ref[0, pl.ds(cur0, B), :] += dvcat[B:, :]
        dsk = -jnp.exp(sink - lse) * delta
        lane = lax.broadcasted_iota(jnp.int32, (1, LANES), 1)
        row = jnp.zeros((1, LANES), F32)
        for g in range(group):
            row = row + jnp.where(lane == g, jnp.sum(dsk[g * B:(g + 1) * B, :]), 0.0)
        dsink_ref[0, 0:1, :] += row

    specs = _swa_specs(n_kv, group, 0, n_q, v_first)
    q_blk = pl.BlockSpec((B, group * HEAD_DIM), lambda kv, n: (n, kv))
    acc = pl.BlockSpec((1, S, HEAD_DIM), lambda kv, n: (kv, 0, 0))
    return pl.pallas_call(
        body, name="swa_bwd", grid=(n_kv, S // B),
        in_specs=specs + [q_blk, pl.BlockSpec((B, group * HEAD_DIM), lambda kv, n: (n, do_first + kv)),
                          pl.BlockSpec((1, B, group * LANES), lambda kv, n: (kv, n, 0)),
                          pl.BlockSpec(memory_space=pltpu.SMEM)],
        out_specs=[q_blk, acc, acc, pl.BlockSpec((1, 8, LANES), lambda kv, n: (kv, 0, 0))],
        out_shape=[jax.ShapeDtypeStruct((S, n_q * HEAD_DIM), BF16), jax.ShapeDtypeStruct((n_kv, S, HEAD_DIM), F32),
                   jax.ShapeDtypeStruct((n_kv, S, HEAD_DIM), F32), jax.ShapeDtypeStruct((n_kv, 8, LANES), F32)],
        compiler_params=pltpu.CompilerParams(dimension_semantics=("parallel", "arbitrary")),
    )(rq, rq, rq, proj, proj, o, do, lse_b, sinks)


def _adamw(w, g, m, v):
    m = ADAM_B1 * m + (1.0 - ADAM_B1) * g
    v = ADAM_B2 * v + (1.0 - ADAM_B2) * (g * g)
    m_hat = m / (1.0 - ADAM_B1 ** ADAM_STEP)
    v_hat = v / (1.0 - ADAM_B2 ** ADAM_STEP)
    delta = -ADAM_LR * (m_hat / (jnp.sqrt(v_hat) + ADAM_EPS) + ADAM_WD * w)
    return delta, m, v


def _mod_fwd(cond_in, w_mod, b_shard):
    R, D = cond_in.shape
    cols = w_mod.shape[1]
    tn = _fit(512, cols)

    def body(c_ref, w_ref, b_ref, o_ref):
        cv = c_ref[...]
        cond = (cv / (1.0 + jnp.exp(-cv))).astype(BF16)
        o_ref[...] = jnp.dot(cond, w_ref[...].astype(BF16), preferred_element_type=F32) + b_ref[...]

    return pl.pallas_call(
        body, name="mod_fwd", grid=(cols // tn,),
        in_specs=[pl.BlockSpec((R, D), lambda j: (0, 0)), pl.BlockSpec((D, tn), lambda j: (0, j)),
                  pl.BlockSpec((1, tn), lambda j: (0, j))],
        out_specs=pl.BlockSpec((R, tn), lambda j: (0, j)),
        out_shape=jax.ShapeDtypeStruct((R, cols), F32),
        compiler_params=pltpu.CompilerParams(dimension_semantics=("parallel",), vmem_limit_bytes=_vmem(3 * D * tn * 4)),
    )(cond_in, w_mod, b_shard)


def _mod_update(c_t, dmod, w, m, v):
    D, nb = c_t.shape
    cols = w.shape[1]
    tn = _fit(256, cols)

    def body(c_ref, d_ref, w_ref, m_ref, v_ref, g_ref, dl_ref, nm_ref, nv_ref):
        cv = c_ref[...]
        cond = cv / (1.0 + jnp.exp(-cv))
        g = jnp.zeros((D, tn), F32)
        for b in range(nb):
            g = g + cond[:, b:b + 1] * d_ref[b:b + 1, :]
        g_ref[...] = g
        dl_ref[...], nm_ref[...], nv_ref[...] = _adamw(w_ref[...], g, m_ref[...], v_ref[...])

    blk = pl.BlockSpec((D, tn), lambda j: (0, j))
    out = jax.ShapeDtypeStruct((D, cols), F32)
    return pl.pallas_call(
        body, name="mod_update", grid=(cols // tn,),
        in_specs=[pl.BlockSpec((D, nb), lambda j: (0, 0)), pl.BlockSpec((nb, tn), lambda j: (0, j)), blk, blk, blk],
        out_specs=[blk] * 4, out_shape=[out] * 4,
        compiler_params=pltpu.CompilerParams(dimension_semantics=("parallel",), vmem_limit_bytes=_vmem(18 * D * tn * 4)),
    )(c_t, dmod, w, m, v)


def _small_update(stacked, w, m, v):
    R, C = w.shape

    def body(s_ref, w_ref, m_ref, v_ref, g_ref, dl_ref, nm_ref, nv_ref):
        g = s_ref[0:R, :]
        for d in range(1, N_DEV):
            g = g + s_ref[d * R:(d + 1) * R, :]
        g_ref[...] = g
        dl_ref[...], nm_ref[...], nv_ref[...] = _adamw(w_ref[...], g, m_ref[...], v_ref[...])

    return pl.pallas_call(body, name="small_update", out_shape=[jax.ShapeDtypeStruct((R, C), F32)] * 4)(stacked, w, m, v)


def _place():
    return lax.axis_index("x"), lax.axis_index("y"), lax.axis_index("c")


def _allgather8(name, block):
    m_per, n = block.shape

    def body(x_ref, out_ref, token_ref, send_sems, recv_sems, local_sem):
        token_ref[...] = jnp.zeros_like(token_ref)
        x, y, c = _place()
        me, sibling = (x, y, c), (x, y, 1 - c)
        chips = [(1 - x, y), (x, 1 - y), (1 - x, 1 - y)]

        def rows(px, py, pc):
            return out_ref.at[pl.ds((4 * px + 2 * py + pc) * m_per, m_per), :]

        def copy(k, blk, to, src=None):
            return pltpu.make_async_remote_copy(
                src_ref=rows(*blk) if src is None else src, dst_ref=rows(*blk),
                send_sem=send_sems.at[k], recv_sem=recv_sems.at[k], device_id=to, device_id_type=MESH)

        mine = pltpu.make_async_copy(x_ref, rows(*me), local_sem)
        mine.start()
        first = [copy(0, me, sibling, src=x_ref)]
        first += [copy(1 + j, me, (*chip, c), src=x_ref) for j, chip in enumerate(chips)]
        for cp in first:
            cp.start()
        passed = [copy(4 + j, (*chip, c), sibling) for j, chip in enumerate(chips)]
        for j, chip in enumerate(chips):
            copy(1 + j, (*chip, c), me).wait_recv()
            passed[j].start()
        copy(0, sibling, me).wait_recv()
        for j, chip in enumerate(chips):
            copy(4 + j, (*chip, 1 - c), me).wait_recv()
        for cp in first + passed:
            cp.wait_send()
        mine.wait()

    vmem = pl.BlockSpec(memory_space=pltpu.VMEM)
    return pl.pallas_call(
        body, name=name,
        out_shape=[jax.ShapeDtypeStruct((N_DEV * m_per, n), block.dtype), jax.ShapeDtypeStruct((8, LANES), F32)],
        in_specs=[vmem], out_specs=[vmem, vmem],
        scratch_shapes=[pltpu.SemaphoreType.DMA((7,)), pltpu.SemaphoreType.DMA((7,)), pltpu.SemaphoreType.DMA],
    )(block)


_ANY = pl.BlockSpec(memory_space=pl.ANY)


def _half(ref, c, rows):
    return ref.at[pl.ds(c * (rows // 2), rows // 2), :]


_HBM = pl.BlockSpec(memory_space=pltpu.HBM)
_SEM = pl.BlockSpec(memory_space=pltpu.SEMAPHORE)
_EFFECT = pltpu.SideEffectType.DATAFLOW_SIDE_EFFECTING


def _ici_start(name, srcs, land_shapes, plan, per_source=3):
    ns, nl = len(srcs), len(land_shapes)
    n_copies = per_source * ns

    def body(*refs):
        src_refs, land_refs = refs[:ns], refs[ns:ns + nl]
        send_sems, recv_sems = refs[ns + nl], refs[ns + nl + 1]
        token = refs[-1]
        for n, (src, dst, peer, _) in enumerate(plan(src_refs, land_refs)):
            pltpu.make_async_remote_copy(src_ref=src, dst_ref=dst, send_sem=send_sems.at[n], recv_sem=recv_sems.at[n],
                                         device_id=peer, device_id_type=MESH).start()
        token[...] = jnp.zeros_like(token)

    lands = [lax.empty(s.shape, s.dtype) for s in land_shapes]
    out = pl.pallas_call(
        body, name=name,
        out_shape=(pltpu.SemaphoreType.DMA((n_copies,)), pltpu.SemaphoreType.DMA((n_copies,)),
                   *[pltpu.HBM(a.shape, a.dtype) for a in list(srcs) + lands], jax.ShapeDtypeStruct((8, LANES), F32)),
        in_specs=[_HBM] * (ns + nl),
        out_specs=(_SEM, _SEM, *[_HBM] * (ns + nl), pl.BlockSpec(memory_space=pltpu.VMEM)),
        input_output_aliases={n: 2 + n for n in range(ns + nl)},
        compiler_params=pltpu.CompilerParams(has_side_effects=_EFFECT),
    )(*[pltpu.with_memory_space_constraint(a, pltpu.HBM) for a in list(srcs) + lands])
    return out[0], out[1], list(out[2:2 + ns]), list(out[2 + ns:2 + ns + nl]), out[-1]


def _ici_wait(name, send_sems, recv_sems, srcs, lands, plan, after):
    ns, nl = len(srcs), len(lands)

    def body(*refs):
        src_refs, land_refs = refs[:ns], refs[ns:ns + nl]
        send_sems, recv_sems = refs[ns + nl], refs[ns + nl + 1]
        for n, (src, _, peer, mine) in enumerate(plan(src_refs, land_refs)):
            cp = pltpu.make_async_remote_copy(src_ref=src, dst_ref=mine, send_sem=send_sems.at[n],
                                              recv_sem=recv_sems.at[n], device_id=peer, device_id_type=MESH)
            cp.wait_send()
            cp.wait_recv()

    out = pl.pallas_call(
        body, name=name, out_shape=[pltpu.HBM(a.shape, a.dtype) for a in list(srcs) + list(lands)],
        in_specs=[_HBM] * (ns + nl) + [_SEM, _SEM, _ANY], out_specs=[_HBM] * (ns + nl),
        input_output_aliases={n: n for n in range(ns + nl)},
        compiler_params=pltpu.CompilerParams(has_side_effects=_EFFECT),
    )(*srcs, *lands, send_sems, recv_sems, after)
    return list(out[:ns]), list(out[ns:])


def _gather_plan(src_refs, land_refs):
    x, y, c = _place()
    copies = []
    for w, land in zip(src_refs, land_refs):
        R = w.shape[0]
        for cx, cy in [(1 - x, y), (x, 1 - y), (1 - x, 1 - y)]:
            copies.append((_half(w, c, R), _half(land.at[2 * x + y], c, R), (cx, cy, c),
                           _half(land.at[2 * cx + cy], c, R)))
    return copies


def _pair_plan(src_refs, land_refs):
    x, y, c = _place()
    copies = []
    for g, land in zip(src_refs, land_refs):
        half = g.shape[1] // 2
        copies.append((g.at[:, pl.ds((1 - c) * half, half), :], land, (x, y, 1 - c), land))
    return copies


def _share_plan(src_refs, land_refs):
    x, y, c = _place()
    return [(h, land, (x, y, 1 - c), land) for h, land in zip(src_refs, land_refs)]


def _pass_to_sibling(name, lands):
    nw = len(lands)

    def body(*refs):
        ins, outs = refs[:nw], refs[nw:2 * nw]
        send_sems, recv_sems = refs[2 * nw:]
        x, y, c = _place()
        chips = [(1 - x, y), (x, 1 - y), (1 - x, 1 - y)]
        copies = []
        for k in range(nw):
            R = ins[k].shape[1]
            for j, (cx, cy) in enumerate(chips):
                cp = pltpu.make_async_remote_copy(
                    src_ref=_half(ins[k].at[2 * cx + cy], c, R), dst_ref=_half(outs[k].at[2 * cx + cy], c, R),
                    send_sem=send_sems.at[3 * k + j], recv_sem=recv_sems.at[3 * k + j],
                    device_id=(x, y, 1 - c), device_id_type=MESH)
                cp.start()
                copies.append(cp)
        for k in range(nw):
            R = ins[k].shape[1]
            for j, (cx, cy) in enumerate(chips):
                pltpu.make_async_remote_copy(
                    src_ref=_half(ins[k].at[2 * cx + cy], c, R), dst_ref=_half(outs[k].at[2 * cx + cy], 1 - c, R),
                    send_sem=send_sems.at[3 * k + j], recv_sem=recv_sems.at[3 * k + j],
                    device_id=(x, y, 1 - c), device_id_type=MESH).wait_recv()
        for cp in copies:
            cp.wait_send()

    return pl.pallas_call(
        body, name=name, out_shape=[jax.ShapeDtypeStruct(a.shape, a.dtype) for a in lands],
        in_specs=[_ANY] * nw, out_specs=[_ANY] * nw, input_output_aliases={k: k for k in range(nw)},
        scratch_shapes=[pltpu.SemaphoreType.DMA((3 * nw,)), pltpu.SemaphoreType.DMA((3 * nw,))],
    )(*lands)


def _tie(vec, token):
    return vec + token[0:1, 0:1]


ROW_ALIGN = 16
TILE_ELEMS = 512 * 1024


def _tiles(rows, cols):
    fits = [t for t in range(ROW_ALIGN, min(rows, 256) + 1, ROW_ALIGN) if rows % t == 0]
    tr = fits[-1] if fits and fits[-1] >= 64 else rows
    tc = cols
    while tr * tc > TILE_ELEMS and tc % (2 * LANES) == 0:
        tc //= 2
    return tr, tc


def _pair_add(name, core, grad, recv):
    n, R, C = grad.shape
    half = R // 2
    tr, tc = _tiles(half, C)
    nr = half // tr

    def body(core_ref, g_ref, r_ref, o_ref):
        o_ref[...] = (g_ref[...].astype(F32) + r_ref[...].astype(F32)).astype(BF16)

    grid_spec = pltpu.PrefetchScalarGridSpec(
        num_scalar_prefetch=1, grid=(n, nr, C // tc),
        in_specs=[pl.BlockSpec((1, tr, tc), lambda s, r, q, core_ref: (s, core_ref[0] * nr + r, q)),
                  pl.BlockSpec((1, tr, tc), lambda s, r, q, core_ref: (s, r, q))],
        out_specs=pl.BlockSpec((1, tr, tc), lambda s, r, q, core_ref: (s, r, q)))
    return pl.pallas_call(
        body, name=name, grid_spec=grid_spec, out_shape=jax.ShapeDtypeStruct((n, half, C), BF16),
        compiler_params=pltpu.CompilerParams(dimension_semantics=("parallel", "parallel", "parallel")),
    )(core, grad, recv)


def _scatter_plan(src_refs, land_refs):
    x, y, c = _place()
    copies = []
    for p, land in zip(src_refs, land_refs):
        for j, (cx, cy) in enumerate([(1 - x, y), (x, 1 - y), (1 - x, 1 - y)]):
            copies.append((p.at[2 * cx + cy], land.at[j], (cx, cy, c), land.at[j]))
    return copies


def _chip_add(name, chip, sums, recv):
    _, H, C = sums.shape
    tr, tc = _tiles(H, C)

    def body(chip_ref, p_ref, r_ref, o_ref):
        total = p_ref[0].astype(F32)
        for j in range(3):
            total = total + r_ref[j].astype(F32)
        o_ref[...] = total

    grid_spec = pltpu.PrefetchScalarGridSpec(
        num_scalar_prefetch=1, grid=(H // tr, C // tc),
        in_specs=[pl.BlockSpec((1, tr, tc), lambda r, q, chip_ref: (chip_ref[0], r, q)),
                  pl.BlockSpec((3, tr, tc), lambda r, q, chip_ref: (0, r, q))],
        out_specs=pl.BlockSpec((tr, tc), lambda r, q, chip_ref: (r, q)))
    return pl.pallas_call(
        body, name=name, grid_spec=grid_spec, out_shape=jax.ShapeDtypeStruct((H, C), F32),
        compiler_params=pltpu.CompilerParams(dimension_semantics=("parallel", "parallel")),
    )(chip, sums, recv)


def _pair_share(name, halves):
    nw = len(halves)

    def body(*refs):
        hs, outs = refs[:nw], refs[nw:2 * nw]
        send_sems, recv_sems = refs[2 * nw:]
        x, y, c = _place()
        copies = []
        for k in range(nw):
            cp = pltpu.make_async_remote_copy(
                src_ref=hs[k], dst_ref=outs[k], send_sem=send_sems.at[k], recv_sem=recv_sems.at[k],
                device_id=(x, y, 1 - c), device_id_type=MESH)
            cp.start()
            copies.append(cp)
        for cp in copies:
            cp.wait()

    return pl.pallas_call(
        body, name=name,
        out_shape=[jax.ShapeDtypeStruct(h.shape, h.dtype) for h in halves],
        in_specs=[_ANY] * nw, out_specs=[_ANY] * nw,
        scratch_shapes=[pltpu.SemaphoreType.DMA((nw,)), pltpu.SemaphoreType.DMA((nw,))],
    )(*halves)


def _adam_halves(name, core, w, g_own, g_other, m, v):
    R, C = w.shape
    H = R // 2
    tr, tc = _tiles(H, C)
    nr, nc = H // tr, C // tc

    def body(core_ref, w_ref, go_ref, gr_ref, m_ref, v_ref, g_ref, dl_ref, nm_ref, nv_ref):
        own = (pl.program_id(0) // nr) == core_ref[0]
        g = jnp.where(own, go_ref[...], gr_ref[...])
        g_ref[...] = g
        dl_ref[...], nm_ref[...], nv_ref[...] = _adamw(w_ref[...], g, m_ref[...], v_ref[...])

    blk = pl.BlockSpec((tr, tc), lambda r, q, core_ref: (r, q))

    def half_spec(is_own):
        def index(r, q, core_ref):
            mine = ((r // nr) == core_ref[0]) == is_own
            done = is_own == (core_ref[0] == 0)
            return (jnp.where(mine, r % nr, jnp.where(done, nr - 1, 0)), jnp.where(mine, q, jnp.where(done, nc - 1, 0)))
        return pl.BlockSpec((tr, tc), index)
    out = jax.ShapeDtypeStruct((R, C), F32)
    grid_spec = pltpu.PrefetchScalarGridSpec(
        num_scalar_prefetch=1, grid=(R // tr, nc), in_specs=[blk, half_spec(True), half_spec(False), blk, blk],
        out_specs=[blk] * 4)
    return pl.pallas_call(
        body, name=name, grid_spec=grid_spec, out_shape=[out] * 4,
        compiler_params=pltpu.CompilerParams(dimension_semantics=("parallel", "parallel"),
                                             vmem_limit_bytes=_vmem(20 * tr * tc * 4)),
    )(core, w, g_own, g_other, m, v)


def kernel(x, c, w_mod, b_mod, g_pre_mix, g_post_mix, w_in, b_forget, swa_sinks, w_out, g_pre_mlp, g_post_mlp, w_up, w_down, loss_target, m_w_mod, m_b_mod, m_g_pre_mix, m_g_post_mix, m_w_in, m_b_forget, m_swa_sinks, m_w_out, m_g_pre_mlp, m_g_post_mlp, m_w_up, m_w_down, v_w_mod, v_b_mod, v_g_pre_mix, v_g_post_mix, v_w_in, v_b_forget, v_swa_sinks, v_w_out, v_g_pre_mlp, v_g_post_mlp, v_w_up, v_w_down):
    S, D = x.shape[1], x.shape[2]
    n_heads = D // HEAD_DIM
    n_fox = n_heads // 2
    n_swa = n_heads - n_fox
    n_kv = max(1, n_swa // 4)
    fox_w, swa_w, kv_w = n_fox * HEAD_DIM, n_swa * HEAD_DIM, n_kv * HEAD_DIM
    main_w = 3 * fox_w + swa_w + 2 * kv_w
    in_w = main_w + n_fox
    mod_cols = w_mod.shape[2]

    ax, ay, ac = _place()
    chip = 2 * ax + ay
    dev = 2 * chip + ac
    chip_arr = jnp.reshape(chip, (1,)).astype(jnp.int32)
    core_arr = jnp.reshape(ac, (1,)).astype(jnp.int32)

    x2, tgt = x[0], loss_target[0]

    c_all, _ = _allgather8("gather_c", c.reshape(8, D // 8))
    c_all = c_all.reshape(N_DEV, D)
    b_shard = lax.dynamic_slice_in_dim(b_mod, chip * mod_cols, mod_cols, axis=1)
    mod_shard = _mod_fwd(jnp.pad(c_all, ((0, 16 - N_DEV), (0, 0))), w_mod[0], b_shard)[:N_DEV]
    mod_all, token = _allgather8("gather_mod", mod_shard)
    mod_all = mod_all.reshape(N_CHIPS, 2, N_DEV, mod_cols)[:, 0]
    mod = lax.dynamic_index_in_dim(mod_all, dev, axis=1, keepdims=False).reshape(N_MOD, 1, D)
    sh_a, sc_a, gt_a, sh_m, sc_m, gt_m = [mod[n] for n in range(N_MOD)]

    in_rows = in_w // N_CHIPS
    in_rows_pad = -(-in_rows // (2 * ROW_ALIGN)) * (2 * ROW_ALIGN)

    def rows_of(a):
        return jnp.pad(a[0].T, ((0, in_rows_pad - in_rows), (0, 0)))

    names = ["w_in", "w_out", "w_up", "w_down"]
    flights = {}
    for n, w in zip(names, [rows_of(w_in), w_out[0], w_up[0], w_down[0]]):
        shard = _tie(w, token).astype(BF16)
        flights[n] = _ici_start("gather_start_" + n, [shard], [jax.ShapeDtypeStruct((N_CHIPS,) + shard.shape, BF16)],
                                _gather_plan)
        token = flights[n][4]
    sc_a = _tie(sc_a, token)

    def gathered(n, after):
        send, recv, srcs, lands, _ = flights[n]
        srcs, lands = _ici_wait("gather_wait_" + n, send, recv, srcs, lands, _gather_plan, after)
        lands = _pass_to_sibling("gather_pass_" + n, lands)
        return lax.dynamic_update_index_in_dim(lands[0], srcs[0], chip, 0)

    d_ff = N_CHIPS * w_up.shape[2]

    h = _pre_norm(x2, g_pre_mix, sc_a, sh_a)
    w_in_t = gathered("w_in", h)[:, :in_rows].reshape(in_w, D)
    w_main_t = jnp.concatenate([w_in_t[:3 * fox_w], w_in_t[3 * fox_w + n_fox:]], axis=0)
    w_fg_t = jnp.pad(w_in_t[3 * fox_w:3 * fox_w + n_fox], ((0, LANES - n_fox), (0, 0)))
    proj = _mm_plain("in_proj", h, w_main_t, "nt", BF16, tn=_fit(768, main_w))
    fg = _mm_plain("in_proj_gate", h, w_fg_t, "nt", F32)
    b_pad = jnp.pad(b_forget, ((0, 0), (0, LANES - n_fox)))
    cum_row = _fox_gate_fwd(fg, b_pad)[:n_fox].reshape(n_fox, 1, S)
    fox_o, fox_lse = _fox_fwd(proj, cum_row, n_fox)

    cos, sin_signed = _rope_tables(S)
    rq = _rope("rope_fwd", proj, 3 * n_fox, n_swa + n_kv, cos, sin_signed)
    v_first = 3 * n_fox + n_swa + n_kv
    sinks = swa_sinks[0]
    swa_o, swa_lse = _swa_fwd(rq, proj, v_first, sinks, n_swa, n_kv)

    mixcat = jnp.concatenate([fox_o, swa_o], axis=1).astype(BF16)
    w_out_f = gathered("w_out", mixcat).reshape(D, D)
    mix = _mm_plain("out_proj", mixcat, w_out_f, "nn", F32)
    x1, h2 = _post_mix(x2, mix, g_post_mix, gt_a, g_pre_mlp, sc_m, sh_m)
    w_up_f = jnp.transpose(gathered("w_up", h2), (1, 0, 2)).reshape(D, d_ff)

    tm_u, tn_u = _fit(MM_TM, S), _fit(MM_TN, d_ff)

    def up_epilogue(acc, ex, outs):
        outs[0][...] = acc.astype(BF16)
        r = jnp.maximum(acc, 0.0)
        outs[1][...] = (r * r).astype(BF16)

    ublk = ((S, d_ff), BF16, (tm_u, tn_u), lambda i, j: (i, j))
    u, a = _matmul("mlp_up", h2, w_up_f, "nn", [ublk, ublk], up_epilogue)
    w_down_f = gathered("w_down", a).reshape(d_ff, D)
    y = _mm_plain("mlp_down", a, w_down_f, "nn", F32)

    dy, dout, loss_part, acc_mlp_post = _loss_and_post_mlp_bwd(x1, y, tgt, g_post_mlp, gt_m)

    def du_epilogue(acc, ex, outs):
        outs[0][...] = (acc * (2.0 * jnp.maximum(ex[0][...].astype(F32), 0.0))).astype(BF16)

    du = _matmul("mlp_down_bwd", dy, w_down_f, "nt", [ublk], du_epilogue,
                 extras=[(u, (tm_u, tn_u), lambda i, j: (i, j))])[0]
    def pair_start(tag, fulls):
        return _ici_start("grad_pair_start_" + tag, fulls,
                          [jax.ShapeDtypeStruct((N_CHIPS, g.shape[1] // 2, g.shape[2]), BF16) for g in fulls],
                          _pair_plan, per_source=1)

    def scatter_start(tag, pair_flights, after):
        sums = []
        for k, (send, recv, fulls, lands, _) in enumerate(pair_flights):
            fulls, from_sibling = _ici_wait("grad_pair_wait_%s_%d" % (tag, k), send, recv, fulls, lands, _pair_plan, after)
            sums += [_pair_add("pair_add_%s_%d_%d" % (tag, k, n), core_arr, g, r)
                     for n, (g, r) in enumerate(zip(fulls, from_sibling))]
        return _ici_start("grad_scatter_start_" + tag, sums,
                          [jax.ShapeDtypeStruct((3,) + p.shape[1:], BF16) for p in sums], _scatter_plan)

    def scatter_finish(tag, flight, after):
        send, recv, srcs, lands, _ = flight
        sums, received = _ici_wait("grad_scatter_wait_" + tag, send, recv, srcs, lands, _scatter_plan, after)
        return [_chip_add("chip_add_%s_%d" % (tag, k), chip_arr, p, r) for k, (p, r) in enumerate(zip(sums, received))]

    g_down = _mm_plain("grad_w_down", a, dy, "tn", BF16)
    pair_down = pair_start("down", [g_down.reshape(N_CHIPS, d_ff // N_CHIPS, D)])
    tn_s = _fit(MM_TN, w_up.shape[2])
    per = w_up.shape[2] // tn_s

    def shard_epilogue(acc, ex, outs):
        outs[0][0] = acc.astype(BF16)

    g_up = _matmul("grad_w_up", h2, du, "tn",
                   [((N_CHIPS, D, w_up.shape[2]), BF16, (1, _fit(MM_TM, D), tn_s), lambda i, j: (j // per, i, j % per))],
                   shard_epilogue, extras=[_behind(pair_down[4])], tn=tn_s)[0]
    pair_up = pair_start("up", [g_up])
    dh2 = _mm_plain("mlp_up_bwd", du, w_up_f, "nt", F32, after=pair_up[4])
    flight_mlp = scatter_start("mlp", [pair_up, pair_down], dh2)
    dx1, dmix, acc_mid = _pre_mlp_and_post_mix_bwd(dh2, x1, dout, mix, _tie(g_pre_mlp, flight_mlp[4]), sc_m,
                                                   g_post_mix, gt_a)

    dmixcat = _mm_plain("out_proj_bwd", dmix, w_out_f, "nt", F32)
    g_out = _mm_plain("grad_w_out", mixcat, dmix, "tn", BF16)

    fdq, fdk, fdv, dcum_row, dcum_q = _fox_bwd(proj, fox_o, dmixcat, fox_lse, cum_row, n_fox)
    dcum_k = jnp.pad(dcum_row.reshape(n_fox, S), ((0, LANES - n_fox), (0, 0)))
    dfg, db_forget = _fox_gate_bwd(dcum_k, dcum_q, fg, b_pad)

    group_w = (n_swa // n_kv) * HEAD_DIM
    sdq, sdk, sdv, dsink = _swa_bwd(rq, proj, v_first, sinks, swa_o, dmixcat, fox_w // group_w, swa_lse, n_swa, n_kv)
    drq = jnp.concatenate([sdq, jnp.transpose(sdk, (1, 0, 2)).reshape(S, kv_w).astype(BF16)], axis=1)
    d_sq_sk = _rope("rope_bwd", drq, 0, n_swa + n_kv, cos, -sin_signed)
    dsv = jnp.transpose(sdv, (1, 0, 2)).reshape(S, kv_w).astype(BF16)
    dproj = jnp.concatenate([fdq, fdk, fdv, d_sq_sk, dsv], axis=1)

    g_main_t = _mm_plain("grad_w_in", dproj, h, "tn", BF16, tm=_fit(768, main_w))
    g_fg_t = _mm_plain("grad_w_in_gate", dfg, h, "tn", BF16)

    g_in_t = jnp.concatenate([g_main_t[:3 * fox_w], g_fg_t[:n_fox], g_main_t[3 * fox_w:]], axis=0)
    g_in_t = jnp.pad(g_in_t.reshape(N_CHIPS, in_rows, D), ((0, 0), (0, in_rows_pad - in_rows), (0, 0)))
    pair_mix = pair_start("mix", [g_in_t, g_out.reshape(N_CHIPS, D // N_CHIPS, D)])
    dh_gate = _mm_plain("in_proj_gate_bwd", dfg, w_fg_t, "nn", F32, after=pair_mix[4])

    def add_epilogue(acc, ex, outs):
        outs[0][...] = acc + ex[0][...]

    tm_h, tn_h = _fit(MM_TM, S), _fit(MM_TN, D)
    dh = _matmul("in_proj_bwd", dproj, w_main_t, "nn", [((S, D), F32, (tm_h, tn_h), lambda i, j: (i, j))], add_epilogue,
                 extras=[(dh_gate, (tm_h, tn_h), lambda i, j: (i, j))], tk=_fit(2304, main_w))[0]
    grad_x, acc_pre = _pre_mix_bwd(dh, x2, dx1, g_pre_mix, sc_a)

    zero_row = jnp.zeros((1, D), F32)
    tail = jnp.concatenate([db_forget[0:1, :n_fox], dsink[:, 0, :n_swa // n_kv].reshape(1, n_swa),
                            loss_part[0:1, 0:1], jnp.zeros((1, D - n_fox - n_swa - 1), F32)], axis=1)
    partial = jnp.concatenate([
        acc_pre[0:1], acc_pre[1:2], acc_mid[3:4], acc_mid[0:1], acc_mid[1:2], acc_mlp_post[0:1],
        acc_pre[2:3], acc_mid[4:5], acc_mid[2:3], acc_mlp_post[1:2], tail] + [zero_row] * 5, axis=0)
    gathered_small, token = _allgather8("gather_small_grads", partial)

    flight_mix = scatter_start("mix", [pair_mix], token)
    halves_mlp = scatter_finish("mlp", flight_mlp, flight_mix[4])
    share_mlp = _ici_start("grad_share_start_mlp", halves_mlp,
                           [jax.ShapeDtypeStruct(hv.shape, F32) for hv in halves_mlp], _share_plan, per_source=1)

    def pack(bm, gpm, gqm, gpl, gql, bf, sk):
        last = jnp.concatenate([bf, sk, jnp.zeros((1, D - n_fox - n_swa), F32)], axis=1)
        return jnp.concatenate([bm.reshape(N_MOD, D), gpm, gqm, gpl, gql, last, jnp.zeros((5, D), F32)], axis=0)

    def unpack(p):
        return {"b_mod": p[0:N_MOD].reshape(1, N_MOD * D), "g_pre_mix": p[6:7], "g_post_mix": p[7:8],
                "g_pre_mlp": p[8:9], "g_post_mlp": p[9:10], "b_forget": p[10:11, :n_fox],
                "swa_sinks": p[10:11, n_fox:n_fox + n_swa]}

    small_out = _small_update(
        gathered_small, _tie(pack(b_mod, g_pre_mix, g_post_mix, g_pre_mlp, g_post_mlp, b_forget, swa_sinks), share_mlp[4]),
        pack(m_b_mod, m_g_pre_mix, m_g_post_mix, m_g_pre_mlp, m_g_post_mlp, m_b_forget, m_swa_sinks),
        pack(v_b_mod, v_g_pre_mix, v_g_post_mix, v_g_pre_mlp, v_g_post_mlp, v_b_forget, v_swa_sinks))
    g_small, d_small, m_small, v_small = [unpack(p) for p in small_out]
    loss = small_out[0][N_MOD + 4, n_fox + n_swa]

    dmod_all = gathered_small.reshape(N_DEV, 16, D)[:, :N_MOD].reshape(N_DEV, N_MOD * D)
    dmod_shard = _tie(lax.dynamic_slice_in_dim(dmod_all, chip * mod_cols, mod_cols, axis=1), share_mlp[4])
    g_w_mod, d_w_mod, nm_w_mod, nv_w_mod = _mod_update(c_all.T, dmod_shard, w_mod[0], m_w_mod[0], v_w_mod[0])
    send, recv, halves_mlp, lands, _ = share_mlp
    halves_mlp, others_mlp = _ici_wait("grad_share_wait_mlp", send, recv, halves_mlp, lands, _share_plan,
                                       d_w_mod[:8, :LANES] + small_out[1][:8, :LANES])

    grads = dict(g_small, w_mod=g_w_mod[None])
    deltas = dict(d_small, w_mod=d_w_mod[None])
    new_m = dict(m_small, w_mod=nm_w_mod[None])
    new_v = dict(v_small, w_mod=nv_w_mod[None])
    weights = {"w_in": (w_in, m_w_in, v_w_in), "w_out": (w_out, m_w_out, v_w_out), "w_up": (w_up, m_w_up, v_w_up),
               "w_down": (w_down, m_w_down, v_w_down)}

    def big_update(n, own, other):
        transposed = n == "w_in"
        w, m, v = [rows_of(a) if transposed else a[0] for a in weights[n]]
        outs = _adam_halves("adam_" + n, core_arr, w, own, other, m, v)
        if transposed:
            outs = [o[:in_rows].T for o in outs]
        grads[n], deltas[n], new_m[n], new_v[n] = [o[None] for o in outs]

    big_update("w_up", halves_mlp[0], others_mlp[0])
    big_update("w_down", halves_mlp[1], others_mlp[1])
    ran = deltas["w_down"][0, :8, :LANES] + deltas["w_up"][0, :8, :LANES] + d_w_mod[:8, :LANES]
    halves_mix = scatter_finish("mix", flight_mix, ran)
    others_mix = _pair_share("grad_pair_share_mix", halves_mix)
    big_update("w_in", halves_mix[0], others_mix[0])
    big_update("w_out", halves_mix[1], others_mix[1])

    order = ["w_mod", "b_mod", "g_pre_mix", "g_post_mix", "w_in", "b_forget", "swa_sinks", "w_out", "g_pre_mlp",
             "g_post_mlp", "w_up", "w_down"]
    return (loss, grad_x[None], *[grads[n] for n in order], *[deltas[n] for n in order],
            *[new_m[n] for n in order], *[new_v[n] for n in order])
```

```python
import jax
import jax.numpy as jnp
from jax import lax
from jax.experimental import pallas as pl
from jax.experimental.pallas import tpu as pltpu

F32 = jnp.float32
BF16 = jnp.bfloat16
MESH = pl.DeviceIdType.MESH

HEAD_DIM = 128
SWA_BLOCK = 128
ROPE_THETA = 10000.0
NORM_EPS = 1e-6
NEG = -1e30
N_MOD = 6
ADAM_LR = 0.001
ADAM_B1 = 0.9
ADAM_B2 = 0.999
ADAM_EPS = 1e-08
ADAM_WD = 0.01
ADAM_STEP = 10
N_CHIPS = 4
N_DEV = 8
LANES = 128
VMEM_CAP = 60 * 1024 * 1024

_NN = (((1,), (0,)), ((), ()))
_NT = (((1,), (1,)), ((), ()))
_TN = (((0,), (0,)), ((), ()))


def _vmem(nbytes):
    return int(min(VMEM_CAP, nbytes * 5 // 4 + (4 << 20)))


def _nbytes(shape, dtype):
    n = 1
    for s in shape:
        n *= s
    return n * jnp.dtype(dtype).itemsize


def _fit(t, n):
    t = min(t, n)
    assert n % t == 0, (t, n)
    return t


MM_TM, MM_TN, MM_TK = 512, 1024, 2048


def _matmul(name, a, b, mode, out_defs, epilogue, extras=(), tm=MM_TM, tn=MM_TN, tk=MM_TK, revisits=False):
    if mode == "nn":
        (M, K), (K2, N) = a.shape, b.shape
    elif mode == "nt":
        (M, K), (N, K2) = a.shape, b.shape
    else:
        (K, M), (K2, N) = a.shape, b.shape
    assert K == K2, (a.shape, b.shape, mode)
    tm, tn, tk = _fit(tm, M), _fit(tn, N), _fit(tk, K)
    nk = K // tk
    dims = {"nn": _NN, "nt": _NT, "tn": _TN}[mode]
    a_spec = (pl.BlockSpec((tk, tm), lambda i, j, k: (k, i)) if mode == "tn"
              else pl.BlockSpec((tm, tk), lambda i, j, k: (i, k)))
    b_spec = (pl.BlockSpec((tn, tk), lambda i, j, k: (j, k)) if mode == "nt"
              else pl.BlockSpec((tk, tn), lambda i, j, k: (k, j)))
    n_ex, n_out = len(extras), len(out_defs)

    def body(*refs):
        a_ref, b_ref = refs[0], refs[1]
        ex = refs[2:2 + n_ex]
        outs = refs[2 + n_ex:2 + n_ex + n_out]
        prod = lax.dot_general(a_ref[...], b_ref[...], dims, preferred_element_type=F32)
        if nk == 1:
            epilogue(prod, ex, outs)
        else:
            acc_ref = refs[-1]
            k = pl.program_id(2)

            @pl.when(k == 0)
            def _():
                acc_ref[...] = prod

            @pl.when(k > 0)
            def _():
                acc_ref[...] += prod

            @pl.when(k == nk - 1)
            def _():
                epilogue(acc_ref[...], ex, outs)

    def wrap(f):
        return lambda i, j, k: f(i, j)

    in_specs = [a_spec, b_spec] + [pl.BlockSpec(blk, wrap(f)) for _, blk, f in extras]
    out_specs = [pl.BlockSpec(blk, wrap(f)) for _, _, blk, f in out_defs]
    out_shape = [jax.ShapeDtypeStruct(s, d) for s, d, _, _ in out_defs]
    need = 2 * (tm * tk + tk * tn) * a.dtype.itemsize + 3 * tm * tn * 4
    need += sum(2 * _nbytes(blk, arr.dtype) for arr, blk, _ in extras)
    need += sum(2 * _nbytes(blk, d) for _, d, blk, _ in out_defs)
    res = pl.pallas_call(
        body, name=name, grid=(M // tm, N // tn, nk),
        in_specs=in_specs, out_specs=out_specs, out_shape=out_shape,
        scratch_shapes=[pltpu.VMEM((tm, tn), F32)] if nk > 1 else [],
        compiler_params=pltpu.CompilerParams(
            dimension_semantics=("parallel", "arbitrary" if revisits else "parallel", "arbitrary"),
            vmem_limit_bytes=_vmem(need)),
    )(a, b, *[arr for arr, _, _ in extras])
    return res


def _behind(token):
    return (token, (8, LANES), lambda i, j: (0, 0))


def _mm_plain(name, a, b, mode, out_dtype, after=None, **tiles):
    if mode == "nn":
        M, N = a.shape[0], b.shape[1]
    elif mode == "nt":
        M, N = a.shape[0], b.shape[0]
    else:
        M, N = a.shape[1], b.shape[1]
    tm, tn = _fit(tiles.get("tm", MM_TM), M), _fit(tiles.get("tn", MM_TN), N)

    def epi(acc, ex, outs):
        outs[0][...] = acc.astype(out_dtype)

    return _matmul(name, a, b, mode, [((M, N), out_dtype, (tm, tn), lambda i, j: (i, j))], epi,
                   extras=[] if after is None else [_behind(after)], **tiles)[0]


def _rstd(v):
    return lax.rsqrt(jnp.mean(v * v, axis=-1, keepdims=True) + NORM_EPS)


def _row_call(name, body, row_ins, vec_ins, row_outs, acc_outs, S, D, tr):
    tr = _fit(tr, S)
    row_spec = pl.BlockSpec((tr, D), lambda r: (r, 0))
    vec_spec = pl.BlockSpec((1, D), lambda r: (0, 0))
    in_specs = [row_spec] * len(row_ins) + [vec_spec] * len(vec_ins)
    out_specs = [row_spec] * len(row_outs) + [pl.BlockSpec(shp, lambda r: (0, 0)) for shp in acc_outs]
    out_shape = [jax.ShapeDtypeStruct((S, D), d) for d in row_outs] + [jax.ShapeDtypeStruct(shp, F32) for shp in acc_outs]
    need = sum(2 * tr * D * a.dtype.itemsize for a in row_ins) + sum(2 * tr * D * jnp.dtype(d).itemsize for d in row_outs)
    need += 8 * tr * D * 4
    return pl.pallas_call(
        body, name=name, grid=(S // tr,), in_specs=in_specs, out_specs=out_specs, out_shape=out_shape,
        compiler_params=pltpu.CompilerParams(dimension_semantics=("arbitrary",), vmem_limit_bytes=_vmem(need)),
    )(*row_ins, *vec_ins)


def _acc_rows(ref, rows):
    @pl.when(pl.program_id(0) == 0)
    def _():
        ref[...] = jnp.zeros_like(ref)
    for n, r in enumerate(rows):
        ref[n:n + 1, :] += r


def _pre_norm(x, g, sc, sh):
    S, D = x.shape

    def body(x_ref, g_ref, sc_ref, sh_ref, h_ref):
        xv = x_ref[...]
        xn = xv * _rstd(xv)
        h_ref[...] = (xn * g_ref[...] * (1.0 + sc_ref[...]) + sh_ref[...]).astype(BF16)

    return _row_call("pre_norm_mix", body, [x], [g, sc, sh], [BF16], [], S, D, 256)[0]


def _post_mix(x, mix, g_post, gt, g_pre, sc, sh):
    S, D = x.shape

    def body(x_ref, mix_ref, gp_ref, gt_ref, g2_ref, sc_ref, sh_ref, x1_ref, h2_ref):
        mv = mix_ref[...]
        x1 = x_ref[...] + gt_ref[...] * (mv * _rstd(mv) * gp_ref[...])
        x1_ref[...] = x1
        h2_ref[...] = (x1 * _rstd(x1) * g2_ref[...] * (1.0 + sc_ref[...]) + sh_ref[...]).astype(BF16)

    return _row_call("post_mix_pre_mlp", body, [x, mix], [g_post, gt, g_pre, sc, sh], [F32, BF16], [], S, D, 256)


def _loss_and_post_mlp_bwd(x1, y, target, g_post, gt):
    S, D = x1.shape

    def body(x1_ref, y_ref, t_ref, g_ref, gt_ref, dy_ref, dout_ref, loss_ref, acc_ref):
        yv = y_ref[...]
        r = _rstd(yv)
        yh = yv * r
        n = yh * g_ref[...]
        diff = x1_ref[...] + gt_ref[...] * n - t_ref[...]
        dout = diff * (1.0 / D)
        dout_ref[...] = dout
        dn = dout * gt_ref[...]
        dyh = dn * g_ref[...]
        dy_ref[...] = (r * (dyh - yh * jnp.mean(dyh * yh, axis=-1, keepdims=True))).astype(BF16)
        _acc_rows(acc_ref, [jnp.sum(dout * n, axis=0, keepdims=True), jnp.sum(dn * yh, axis=0, keepdims=True)])

        @pl.when(pl.program_id(0) == 0)
        def _():
            loss_ref[...] = jnp.zeros_like(loss_ref)
        loss_ref[...] += jnp.full(loss_ref.shape, (0.5 / D) * jnp.sum(diff * diff), F32)

    return _row_call("loss_post_mlp_bwd", body, [x1, y, target], [g_post, gt], [BF16, F32],
                     [(8, LANES), (8, D)], S, D, 128)


def _pre_mlp_and_post_mix_bwd(dh2, x1, dout, mix, g_pre, sc, g_post, gt):
    S, D = x1.shape

    def body(dh_ref, x1_ref, dout_ref, mix_ref, g_ref, sc_ref, gp_ref, gt_ref, dx1_ref, dmix_ref, acc_ref):
        dh = dh_ref[...]
        x1v = x1_ref[...]
        r3 = _rstd(x1v)
        xn = x1v * r3
        dxn = dh * (1.0 + sc_ref[...]) * g_ref[...]
        dx1 = dout_ref[...] + r3 * (dxn - xn * jnp.mean(dxn * xn, axis=-1, keepdims=True))
        dx1_ref[...] = dx1
        mv = mix_ref[...]
        r2 = _rstd(mv)
        mh = mv * r2
        dn = dx1 * gt_ref[...]
        dmh = dn * gp_ref[...]
        dmix_ref[...] = (r2 * (dmh - mh * jnp.mean(dmh * mh, axis=-1, keepdims=True))).astype(BF16)
        _acc_rows(acc_ref, [
            jnp.sum(dh, axis=0, keepdims=True),
            jnp.sum(dh * xn * g_ref[...], axis=0, keepdims=True),
            jnp.sum(dh * (1.0 + sc_ref[...]) * xn, axis=0, keepdims=True),
            jnp.sum(dx1 * mh * gp_ref[...], axis=0, keepdims=True),
            jnp.sum(dn * mh, axis=0, keepdims=True)])

    return _row_call("pre_mlp_post_mix_bwd", body, [dh2, x1, dout, mix], [g_pre, sc, g_post, gt], [F32, BF16],
                     [(8, D)], S, D, 128)


def _pre_mix_bwd(dh, x, dx1, g_pre, sc):
    S, D = x.shape

    def body(dh_ref, x_ref, dx1_ref, g_ref, sc_ref, gx_ref, acc_ref):
        dhv = dh_ref[...]
        xv = x_ref[...]
        r = _rstd(xv)
        xn = xv * r
        dxn = dhv * (1.0 + sc_ref[...]) * g_ref[...]
        gx_ref[...] = dx1_ref[...] + r * (dxn - xn * jnp.mean(dxn * xn, axis=-1, keepdims=True))
        _acc_rows(acc_ref, [
            jnp.sum(dhv, axis=0, keepdims=True),
            jnp.sum(dhv * xn * g_ref[...], axis=0, keepdims=True),
            jnp.sum(dhv * (1.0 + sc_ref[...]) * xn, axis=0, keepdims=True)])

    return _row_call("pre_mix_bwd", body, [dh, x, dx1], [g_pre, sc], [F32], [(8, D)], S, D, 128)


CUM_BLOCK = 256


def _tri(n, upper):
    r = lax.broadcasted_iota(jnp.int32, (n, n), 0)
    c = lax.broadcasted_iota(jnp.int32, (n, n), 1)
    return ((c >= r) if upper else (c <= r)).astype(F32)


def _fox_gate_fwd(fg, b_pad):
    S = fg.shape[0]
    cb = _fit(CUM_BLOCK, S)

    def body(fg_ref, b_ref, cumt_ref, cum_ref):
        low = _tri(cb, False)
        carry = jnp.zeros((1, LANES), F32)
        for n in range(S // cb):
            z = fg_ref[n * cb:(n + 1) * cb, :] + b_ref[...]
            logf = jnp.minimum(z, 0.0) - jnp.log(1.0 + jnp.exp(-jnp.abs(z)))
            blk = jnp.dot(low, logf, precision=lax.Precision.HIGHEST, preferred_element_type=F32) + carry
            cum_ref[n * cb:(n + 1) * cb, :] = blk
            carry = blk[cb - 1:cb, :]
        cumt_ref[...] = cum_ref[...].T

    return pl.pallas_call(
        body, name="fox_gate_fwd", out_shape=jax.ShapeDtypeStruct((LANES, S), F32),
        scratch_shapes=[pltpu.VMEM((S, LANES), F32)],
        compiler_params=pltpu.CompilerParams(vmem_limit_bytes=_vmem(6 * S * LANES * 4)),
    )(fg, b_pad)


def _fox_gate_bwd(dcum_k, dcum_q, fg, b_pad):
    S = fg.shape[0]
    n_fox = dcum_q.shape[0]
    cb = _fit(CUM_BLOCK, S)

    def body(dk_ref, dq_ref, fg_ref, b_ref, dfg_ref, db_ref, dc_ref):
        lane = lax.broadcasted_iota(jnp.int32, (S, LANES), 1)
        dc = dk_ref[...].T
        for h in range(n_fox):
            dc = dc + jnp.where(lane == h, dq_ref[h], 0.0)
        dc_ref[...] = dc
        up = _tri(cb, True)
        carry = jnp.zeros((1, LANES), F32)
        db = jnp.zeros((1, LANES), F32)
        for n in reversed(range(S // cb)):
            blk = jnp.dot(up, dc_ref[n * cb:(n + 1) * cb, :], precision=lax.Precision.HIGHEST,
                          preferred_element_type=F32) + carry
            carry = blk[0:1, :]
            z = fg_ref[n * cb:(n + 1) * cb, :] + b_ref[...]
            dfg = blk * (1.0 / (1.0 + jnp.exp(z)))
            dfg_ref[n * cb:(n + 1) * cb, :] = dfg.astype(BF16)
            db = db + jnp.sum(dfg, axis=0, keepdims=True)
        db_ref[...] = jnp.broadcast_to(db, db_ref.shape)

    return pl.pallas_call(
        body, name="fox_gate_bwd",
        out_shape=[jax.ShapeDtypeStruct((S, LANES), BF16), jax.ShapeDtypeStruct((8, LANES), F32)],
        scratch_shapes=[pltpu.VMEM((S, LANES), F32)],
        compiler_params=pltpu.CompilerParams(vmem_limit_bytes=_vmem((8 + 2 * n_fox) * S * LANES * 4)),
    )(dcum_k, dcum_q, fg, b_pad)


FOX_TILE = 512


LOG2E = 1.4426950408889634


def _fox_scores(q, k, ck2, masked, t):
    s = lax.dot_general(q, k, _NT, preferred_element_type=F32) * (HEAD_DIM ** -0.5 * LOG2E) - ck2
    if masked:
        row = lax.broadcasted_iota(jnp.int32, (t, t), 0)
        col = lax.broadcasted_iota(jnp.int32, (t, t), 1)
        s = jnp.where(col <= row, s, NEG)
    return s


def _fox_fwd(proj, cum_row, n_fox):
    S = proj.shape[0]
    t = _fit(FOX_TILE, S)
    nq = S // t

    def body(q_ref, k_ref, v_ref, ck_ref, o_ref, lse_ref):
        def q_block(qi, _):
            q0 = pl.multiple_of(qi * t, t)
            q = q_ref[pl.ds(q0, t), :]

            def kv_block(j, carry, masked):
                m, l, acc = carry
                k0 = pl.multiple_of(j * t, t)
                s = _fox_scores(q, k_ref[pl.ds(k0, t), :], ck_ref[0, :, pl.ds(k0, t)] * LOG2E, masked, t)
                m_new = jnp.maximum(m, jnp.max(s, axis=-1, keepdims=True))
                alpha = jnp.exp2(m - m_new)
                p = jnp.exp2(s - m_new)
                l = alpha * l + jnp.sum(p, axis=-1, keepdims=True)
                acc = alpha * acc + jnp.dot(p.astype(BF16), v_ref[pl.ds(k0, t), :], preferred_element_type=F32)
                return m_new, l, acc

            init = (jnp.full((t, 1), NEG, F32), jnp.zeros((t, 1), F32), jnp.zeros((t, HEAD_DIM), F32))
            carry = lax.fori_loop(0, qi, lambda j, cr: kv_block(j, cr, False), init)
            m, l, acc = kv_block(qi, carry, True)
            o_ref[pl.ds(q0, t), :] = acc / l
            lse_ref[0, pl.ds(q0, t), :] = jnp.broadcast_to(m + jnp.log(l) * LOG2E, (t, LANES))
            return 0

        lax.fori_loop(0, nq, q_block, 0)

    col = lambda off: pl.BlockSpec((S, HEAD_DIM), lambda h: (0, off + h))
    per_head = pl.BlockSpec((1, S, LANES), lambda h: (h, 0, 0))
    return pl.pallas_call(
        body, name="fox_fwd", grid=(n_fox,),
        in_specs=[col(0), col(n_fox), col(2 * n_fox), pl.BlockSpec((1, 1, S), lambda h: (h, 0, 0))],
        out_specs=[pl.BlockSpec((S, HEAD_DIM), lambda h: (0, h)), per_head],
        out_shape=[jax.ShapeDtypeStruct((S, n_fox * HEAD_DIM), F32), jax.ShapeDtypeStruct((n_fox, S, LANES), F32)],
        compiler_params=pltpu.CompilerParams(dimension_semantics=("parallel",),
                                             vmem_limit_bytes=_vmem(16 * S * HEAD_DIM * 4 + 12 * t * t * 4)),
    )(proj, proj, proj, cum_row)


def _fox_bwd(proj, o, do, lse_b, cum_row, n_fox):
    S = proj.shape[0]
    t = _fit(FOX_TILE, S)
    nq = S // t
    scale = HEAD_DIM ** -0.5

    def body(q_ref, k_ref, v_ref, o_ref, do_ref, lse_ref, ck_ref, dq_ref, dk_ref, dv_ref, dc_ref, dcq_ref,
             dq_acc, delta_ref):
        dq_acc[...] = jnp.zeros_like(dq_acc)
        dcq_ref[...] = jnp.zeros_like(dcq_ref)

        def delta_block(qi, _):
            q0 = pl.multiple_of(qi * t, t)
            d = jnp.sum(do_ref[pl.ds(q0, t), :] * o_ref[pl.ds(q0, t), :], axis=-1, keepdims=True)
            delta_ref[pl.ds(q0, t), :] = jnp.broadcast_to(d, (t, LANES))
            return 0

        lax.fori_loop(0, nq, delta_block, 0)

        def kv_block(j, _):
            k0 = pl.multiple_of(j * t, t)
            k = k_ref[pl.ds(k0, t), :]
            v = v_ref[pl.ds(k0, t), :]
            ck2 = ck_ref[0, :, pl.ds(k0, t)] * LOG2E

            def q_block(qi, carry, masked):
                dk, dv, dc = carry
                q0 = pl.multiple_of(qi * t, t)
                q = q_ref[pl.ds(q0, t), :]
                dov = do_ref[pl.ds(q0, t), :].astype(BF16)
                p = jnp.exp2(_fox_scores(q, k, ck2, masked, t) - lse_ref[0, pl.ds(q0, t), :][:, :1])
                dp = lax.dot_general(dov, v, _NT, preferred_element_type=F32)
                ds = p * (dp - delta_ref[pl.ds(q0, t), :][:, :1])
                dsb = ds.astype(BF16)
                dv = dv + lax.dot_general(p.astype(BF16), dov, _TN, preferred_element_type=F32)
                dk = dk + lax.dot_general(dsb, q, _TN, preferred_element_type=F32)
                dq_acc[pl.ds(q0, t), :] += jnp.dot(dsb, k, preferred_element_type=F32)
                dc = dc - jnp.sum(ds, axis=0, keepdims=True)
                dcq_ref[0, pl.ds(q0, t), :] += jnp.broadcast_to(jnp.sum(ds, axis=1, keepdims=True), (t, LANES))
                return dk, dv, dc

            init = (jnp.zeros((t, HEAD_DIM), F32), jnp.zeros((t, HEAD_DIM), F32), jnp.zeros((1, t), F32))
            carry = q_block(j, init, True)
            dk, dv, dc = lax.fori_loop(j + 1, nq, lambda qi, cr: q_block(qi, cr, False), carry)
            dk_ref[pl.ds(k0, t), :] = (dk * scale).astype(BF16)
            dv_ref[pl.ds(k0, t), :] = dv.astype(BF16)
            dc_ref[0, :, pl.ds(k0, t)] = dc
            return 0

        lax.fori_loop(0, nq, kv_block, 0)
        dq_ref[...] = (dq_acc[...] * scale).astype(BF16)

    col = lambda off: pl.BlockSpec((S, HEAD_DIM), lambda h: (0, off + h))
    per_head = pl.BlockSpec((1, S, LANES), lambda h: (h, 0, 0))
    row = pl.BlockSpec((1, 1, S), lambda h: (h, 0, 0))
    grad = jax.ShapeDtypeStruct((S, n_fox * HEAD_DIM), BF16)
    return pl.pallas_call(
        body, name="fox_bwd", grid=(n_fox,),
        in_specs=[col(0), col(n_fox), col(2 * n_fox), col(0), col(0), per_head, row],
        out_specs=[col(0), col(0), col(0), row, per_head],
        out_shape=[grad, grad, grad, jax.ShapeDtypeStruct((n_fox, 1, S), F32), jax.ShapeDtypeStruct((n_fox, S, LANES), F32)],
        scratch_shapes=[pltpu.VMEM((S, HEAD_DIM), F32), pltpu.VMEM((S, LANES), F32)],
        compiler_params=pltpu.CompilerParams(dimension_semantics=("parallel",),
                                             vmem_limit_bytes=_vmem(24 * S * HEAD_DIM * 4 + 16 * t * t * 4)),
    )(proj, proj, proj, o, do, lse_b, cum_row)


def _rope_tables(S):
    half = HEAD_DIM // 2
    inv_freq = 1.0 / (ROPE_THETA ** (jnp.arange(half, dtype=F32) * (2.0 / HEAD_DIM)))
    ang = jnp.arange(S).astype(F32)[:, None] * inv_freq[None, :]
    cos, sin = jnp.cos(ang), jnp.sin(ang)
    return jnp.concatenate([cos, cos], axis=-1), jnp.concatenate([-sin, sin], axis=-1)


def _rope(name, src, first_block, n_blocks, cos, sin_signed):
    S = src.shape[0]

    def body(x_ref, cos_ref, sin_ref, o_ref):
        xv = x_ref[...].astype(F32)
        o_ref[...] = (xv * cos_ref[...] + pltpu.roll(xv, HEAD_DIM // 2, 1) * sin_ref[...]).astype(BF16)

    table = pl.BlockSpec((S, HEAD_DIM), lambda n: (0, 0))
    return pl.pallas_call(
        body, name=name, grid=(n_blocks,),
        in_specs=[pl.BlockSpec((S, HEAD_DIM), lambda n: (0, first_block + n)), table, table],
        out_specs=pl.BlockSpec((S, HEAD_DIM), lambda n: (0, n)),
        out_shape=jax.ShapeDtypeStruct((S, n_blocks * HEAD_DIM), BF16),
        compiler_params=pltpu.CompilerParams(dimension_semantics=("parallel",),
                                             vmem_limit_bytes=_vmem(12 * S * HEAD_DIM * 4)),
    )(src, cos, sin_signed)


def _swa_tile(q_ref, kp_ref, kc_ref, n, group, scale):
    B = SWA_BLOCK
    qs = jnp.concatenate([q_ref[:, g * HEAD_DIM:(g + 1) * HEAD_DIM] for g in range(group)], axis=0)
    kcat = jnp.concatenate([kp_ref[...], kc_ref[...]], axis=0)
    s = lax.dot_general(qs, kcat, _NT, preferred_element_type=F32) * scale
    qi = lax.broadcasted_iota(jnp.int32, (group * B, 2 * B), 0) % B
    kj = lax.broadcasted_iota(jnp.int32, (group * B, 2 * B), 1)
    diff = qi + B - kj
    mask = (diff >= 0) & (diff < B) & ((n * B + kj - B) >= 0)
    return qs, kcat, jnp.where(mask, s, NEG)


def _swa_sink_col(sink_ref, kv, group):
    head = lax.broadcasted_iota(jnp.int32, (group * SWA_BLOCK, 1), 0) // SWA_BLOCK
    col = jnp.zeros((group * SWA_BLOCK, 1), F32)
    for g in range(group):
        col = jnp.where(head == g, sink_ref[kv * group + g], col)
    return col


def _swa_specs(n_kv, group, q_first, k_first, v_first):
    B = SWA_BLOCK
    prev = lambda n: jnp.maximum(n - 1, 0)
    return [
        pl.BlockSpec((B, group * HEAD_DIM), lambda kv, n: (n, q_first + kv)),
        pl.BlockSpec((B, HEAD_DIM), lambda kv, n: (prev(n), k_first + kv)),
        pl.BlockSpec((B, HEAD_DIM), lambda kv, n: (n, k_first + kv)),
        pl.BlockSpec((B, HEAD_DIM), lambda kv, n: (prev(n), v_first + kv)),
        pl.BlockSpec((B, HEAD_DIM), lambda kv, n: (n, v_first + kv)),
    ]


def _swa_fwd(rq, proj, v_first, sinks, n_q, n_kv):
    S = rq.shape[0]
    B = SWA_BLOCK
    group = n_q // n_kv
    scale = HEAD_DIM ** -0.5

    def body(q_ref, kp_ref, kc_ref, vp_ref, vc_ref, sink_ref, o_ref, lse_ref):
        kv, n = pl.program_id(0), pl.program_id(1)
        _, _, s = _swa_tile(q_ref, kp_ref, kc_ref, n, group, scale)
        sink = _swa_sink_col(sink_ref, kv, group)
        m = jnp.maximum(jnp.max(s, axis=-1, keepdims=True), sink)
        p = jnp.exp(s - m)
        denom = jnp.sum(p, axis=-1, keepdims=True) + jnp.exp(sink - m)
        vcat = jnp.concatenate([vp_ref[...], vc_ref[...]], axis=0)
        o = jnp.dot((p / denom).astype(BF16), vcat, preferred_element_type=F32)
        lse = m + jnp.log(denom)
        for g in range(group):
            o_ref[:, g * HEAD_DIM:(g + 1) * HEAD_DIM] = o[g * B:(g + 1) * B, :]
            lse_ref[0, :, g * LANES:(g + 1) * LANES] = jnp.broadcast_to(lse[g * B:(g + 1) * B, :], (B, LANES))

    specs = _swa_specs(n_kv, group, 0, n_q, v_first)
    q_blk = pl.BlockSpec((B, group * HEAD_DIM), lambda kv, n: (n, kv))
    return pl.pallas_call(
        body, name="swa_fwd", grid=(n_kv, S // B),
        in_specs=specs + [pl.BlockSpec(memory_space=pltpu.SMEM)],
        out_specs=[q_blk, pl.BlockSpec((1, B, group * LANES), lambda kv, n: (kv, n, 0))],
        out_shape=[jax.ShapeDtypeStruct((S, n_q * HEAD_DIM), F32), jax.ShapeDtypeStruct((n_kv, S, group * LANES), F32)],
        compiler_params=pltpu.CompilerParams(dimension_semantics=("parallel", "arbitrary")),
    )(rq, rq, rq, proj, proj, sinks)


def _swa_bwd(rq, proj, v_first, sinks, o, do, do_first, lse_b, n_q, n_kv):
    S = rq.shape[0]
    B = SWA_BLOCK
    group = n_q // n_kv
    scale = HEAD_DIM ** -0.5

    def body(q_ref, kp_ref, kc_ref, vp_ref, vc_ref, o_ref, do_ref, lse_ref, sink_ref,
             dq_ref, dk_ref, dv_ref, dsink_ref):
        kv, n = pl.program_id(0), pl.program_id(1)

        @pl.when(n == 0)
        def _():
            dk_ref[...] = jnp.zeros_like(dk_ref)
            dv_ref[...] = jnp.zeros_like(dv_ref)
            dsink_ref[...] = jnp.zeros_like(dsink_ref)

        qs, kcat, s = _swa_tile(q_ref, kp_ref, kc_ref, n, group, scale)
        sink = _swa_sink_col(sink_ref, kv, group)
        stack = lambda ref, w: jnp.concatenate([ref[:, g * w:(g + 1) * w] for g in range(group)], axis=0)
        lse = jnp.concatenate([lse_ref[0, :, g * LANES:g * LANES + 1] for g in range(group)], axis=0)
        do32 = stack(do_ref, HEAD_DIM)
        delta = jnp.sum(do32 * stack(o_ref, HEAD_DIM), axis=-1, keepdims=True)
        dov = do32.astype(BF16)
        p = jnp.exp(s - lse)
        vcat = jnp.concatenate([vp_ref[...], vc_ref[...]], axis=0)
        dp = lax.dot_general(dov, vcat, _NT, preferred_element_type=F32)
        ds = p * (dp - delta)
        dsb = ds.astype(BF16)
        dq = jnp.dot(dsb, kcat, preferred_element_type=F32) * scale
        for g in range(group):
            dq_ref[:, g * HEAD_DIM:(g + 1) * HEAD_DIM] = dq[g * B:(g + 1) * B, :].astype(BF16)
        dkcat = lax.dot_general(dsb, qs, _TN, preferred_element_type=F32) * scale
        dvcat = lax.dot_general(p.astype(BF16), dov, _TN, preferred_element_type=F32)
        prev0 = pl.multiple_of(jnp.maximum(n - 1, 0) * B, B)
        cur0 = pl.multiple_of(n * B, B)
        dk_ref[0, pl.ds(prev0, B), :] += dkcat[:B, :]
        dk_ref[0, pl.ds(cur0, B), :] += dkcat[B:, :]
        dv_ref[0, pl.ds(prev0, B), :] += dvcat[:B, :]
        dv_ref[0, pl.ds(cur0, B), :] += dvcat[B:, :]
        dsk = -jnp.exp(sink - lse) * delta
        lane = lax.broadcasted_iota(jnp.int32, (1, LANES), 1)
        row = jnp.zeros((1, LANES), F32)
        for g in range(group):
            row = row + jnp.where(lane == g, jnp.sum(dsk[g * B:(g + 1) * B, :]), 0.0)
        dsink_ref[0, 0:1, :] += row

    specs = _swa_specs(n_kv, group, 0, n_q, v_first)
    q_blk = pl.BlockSpec((B, group * HEAD_DIM), lambda kv, n: (n, kv))
    acc = pl.BlockSpec((1, S, HEAD_DIM), lambda kv, n: (kv, 0, 0))
    return pl.pallas_call(
        body, name="swa_bwd", grid=(n_kv, S // B),
        in_specs=specs + [q_blk, pl.BlockSpec((B, group * HEAD_DIM), lambda kv, n: (n, do_first + kv)),
                          pl.BlockSpec((1, B, group * LANES), lambda kv, n: (kv, n, 0)),
                          pl.BlockSpec(memory_space=pltpu.SMEM)],
        out_specs=[q_blk, acc, acc, pl.BlockSpec((1, 8, LANES), lambda kv, n: (kv, 0, 0))],
        out_shape=[jax.ShapeDtypeStruct((S, n_q * HEAD_DIM), BF16), jax.ShapeDtypeStruct((n_kv, S, HEAD_DIM), F32),
                   jax.ShapeDtypeStruct((n_kv, S, HEAD_DIM), F32), jax.ShapeDtypeStruct((n_kv, 8, LANES), F32)],
        compiler_params=pltpu.CompilerParams(dimension_semantics=("parallel", "arbitrary")),
    )(rq, rq, rq, proj, proj, o, do, lse_b, sinks)


def _adamw(w, g, m, v):
    m = ADAM_B1 * m + (1.0 - ADAM_B1) * g
    v = ADAM_B2 * v + (1.0 - ADAM_B2) * (g * g)
    m_hat = m / (1.0 - ADAM_B1 ** ADAM_STEP)
    v_hat = v / (1.0 - ADAM_B2 ** ADAM_STEP)
    delta = -ADAM_LR * (m_hat / (jnp.sqrt(v_hat) + ADAM_EPS) + ADAM_WD * w)
    return delta, m, v


def _mod_fwd(cond_in, w_mod, b_shard):
    R, D = cond_in.shape
    cols = w_mod.shape[1]
    tn = _fit(512, cols)

    def body(c_ref, w_ref, b_ref, o_ref):
        cv = c_ref[...]
        cond = (cv / (1.0 + jnp.exp(-cv))).astype(BF16)
        o_ref[...] = jnp.dot(cond, w_ref[...].astype(BF16), preferred_element_type=F32) + b_ref[...]

    return pl.pallas_call(
        body, name="mod_fwd", grid=(cols // tn,),
        in_specs=[pl.BlockSpec((R, D), lambda j: (0, 0)), pl.BlockSpec((D, tn), lambda j: (0, j)),
                  pl.BlockSpec((1, tn), lambda j: (0, j))],
        out_specs=pl.BlockSpec((R, tn), lambda j: (0, j)),
        out_shape=jax.ShapeDtypeStruct((R, cols), F32),
        compiler_params=pltpu.CompilerParams(dimension_semantics=("parallel",), vmem_limit_bytes=_vmem(3 * D * tn * 4)),
    )(cond_in, w_mod, b_shard)


def _mod_update(c_t, dmod, w, m, v):
    D, nb = c_t.shape
    cols = w.shape[1]
    tn = _fit(256, cols)

    def body(c_ref, d_ref, w_ref, m_ref, v_ref, g_ref, dl_ref, nm_ref, nv_ref):
        cv = c_ref[...]
        cond = cv / (1.0 + jnp.exp(-cv))
        g = jnp.zeros((D, tn), F32)
        for b in range(nb):
            g = g + cond[:, b:b + 1] * d_ref[b:b + 1, :]
        g_ref[...] = g
        dl_ref[...], nm_ref[...], nv_ref[...] = _adamw(w_ref[...], g, m_ref[...], v_ref[...])

    blk = pl.BlockSpec((D, tn), lambda j: (0, j))
    out = jax.ShapeDtypeStruct((D, cols), F32)
    return pl.pallas_call(
        body, name="mod_update", grid=(cols // tn,),
        in_specs=[pl.BlockSpec((D, nb), lambda j: (0, 0)), pl.BlockSpec((nb, tn), lambda j: (0, j)), blk, blk, blk],
        out_specs=[blk] * 4, out_shape=[out] * 4,
        compiler_params=pltpu.CompilerParams(dimension_semantics=("parallel",), vmem_limit_bytes=_vmem(18 * D * tn * 4)),
    )(c_t, dmod, w, m, v)


def _small_update(stacked, w, m, v):
    R, C = w.shape

    def body(s_ref, w_ref, m_ref, v_ref, g_ref, dl_ref, nm_ref, nv_ref):
        g = s_ref[0:R, :]
        for d in range(1, N_DEV):
            g = g + s_ref[d * R:(d + 1) * R, :]
        g_ref[...] = g
        dl_ref[...], nm_ref[...], nv_ref[...] = _adamw(w_ref[...], g, m_ref[...], v_ref[...])

    return pl.pallas_call(body, name="small_update", out_shape=[jax.ShapeDtypeStruct((R, C), F32)] * 4)(stacked, w, m, v)


def _place():
    return lax.axis_index("x"), lax.axis_index("y"), lax.axis_index("c")


def _allgather8(name, block):
    m_per, n = block.shape

    def body(x_ref, out_ref, token_ref, send_sems, recv_sems, local_sem):
        token_ref[...] = jnp.zeros_like(token_ref)
        x, y, c = _place()
        me, sibling = (x, y, c), (x, y, 1 - c)
        chips = [(1 - x, y), (x, 1 - y), (1 - x, 1 - y)]

        def rows(px, py, pc):
            return out_ref.at[pl.ds((4 * px + 2 * py + pc) * m_per, m_per), :]

        def copy(k, blk, to, src=None):
            return pltpu.make_async_remote_copy(
                src_ref=rows(*blk) if src is None else src, dst_ref=rows(*blk),
                send_sem=send_sems.at[k], recv_sem=recv_sems.at[k], device_id=to, device_id_type=MESH)

        mine = pltpu.make_async_copy(x_ref, rows(*me), local_sem)
        mine.start()
        first = [copy(0, me, sibling, src=x_ref)]
        first += [copy(1 + j, me, (*chip, c), src=x_ref) for j, chip in enumerate(chips)]
        for cp in first:
            cp.start()
        passed = [copy(4 + j, (*chip, c), sibling) for j, chip in enumerate(chips)]
        for j, chip in enumerate(chips):
            copy(1 + j, (*chip, c), me).wait_recv()
            passed[j].start()
        copy(0, sibling, me).wait_recv()
        for j, chip in enumerate(chips):
            copy(4 + j, (*chip, 1 - c), me).wait_recv()
        for cp in first + passed:
            cp.wait_send()
        mine.wait()

    vmem = pl.BlockSpec(memory_space=pltpu.VMEM)
    return pl.pallas_call(
        body, name=name,
        out_shape=[jax.ShapeDtypeStruct((N_DEV * m_per, n), block.dtype), jax.ShapeDtypeStruct((8, LANES), F32)],
        in_specs=[vmem], out_specs=[vmem, vmem],
        scratch_shapes=[pltpu.SemaphoreType.DMA((7,)), pltpu.SemaphoreType.DMA((7,)), pltpu.SemaphoreType.DMA],
    )(block)


_ANY = pl.BlockSpec(memory_space=pl.ANY)


def _half(ref, c, rows):
    return ref.at[pl.ds(c * (rows // 2), rows // 2), :]


_HBM = pl.BlockSpec(memory_space=pltpu.HBM)
_SEM = pl.BlockSpec(memory_space=pltpu.SEMAPHORE)
_EFFECT = pltpu.SideEffectType.DATAFLOW_SIDE_EFFECTING


def _ici_start(name, srcs, land_shapes, plan, per_source=3):
    ns, nl = len(srcs), len(land_shapes)
    n_copies = per_source * ns

    def body(*refs):
        src_refs, land_refs = refs[:ns], refs[ns:ns + nl]
        send_sems, recv_sems = refs[ns + nl], refs[ns + nl + 1]
        token = refs[-1]
        for n, (src, dst, peer, _) in enumerate(plan(src_refs, land_refs)):
            pltpu.make_async_remote_copy(src_ref=src, dst_ref=dst, send_sem=send_sems.at[n], recv_sem=recv_sems.at[n],
                                         device_id=peer, device_id_type=MESH).start()
        token[...] = jnp.zeros_like(token)

    lands = [lax.empty(s.shape, s.dtype) for s in land_shapes]
    out = pl.pallas_call(
        body, name=name,
        out_shape=(pltpu.SemaphoreType.DMA((n_copies,)), pltpu.SemaphoreType.DMA((n_copies,)),
                   *[pltpu.HBM(a.shape, a.dtype) for a in list(srcs) + lands], jax.ShapeDtypeStruct((8, LANES), F32)),
        in_specs=[_HBM] * (ns + nl),
        out_specs=(_SEM, _SEM, *[_HBM] * (ns + nl), pl.BlockSpec(memory_space=pltpu.VMEM)),
        input_output_aliases={n: 2 + n for n in range(ns + nl)},
        compiler_params=pltpu.CompilerParams(has_side_effects=_EFFECT),
    )(*[pltpu.with_memory_space_constraint(a, pltpu.HBM) for a in list(srcs) + lands])
    return out[0], out[1], list(out[2:2 + ns]), list(out[2 + ns:2 + ns + nl]), out[-1]


def _ici_wait(name, send_sems, recv_sems, srcs, lands, plan, after):
    ns, nl = len(srcs), len(lands)

    def body(*refs):
        src_refs, land_refs = refs[:ns], refs[ns:ns + nl]
        send_sems, recv_sems = refs[ns + nl], refs[ns + nl + 1]
        for n, (src, _, peer, mine) in enumerate(plan(src_refs, land_refs)):
            cp = pltpu.make_async_remote_copy(src_ref=src, dst_ref=mine, send_sem=send_sems.at[n],
                                              recv_sem=recv_sems.at[n], device_id=peer, device_id_type=MESH)
            cp.wait_send()
            cp.wait_recv()

    out = pl.pallas_call(
        body, name=name, out_shape=[pltpu.HBM(a.shape, a.dtype) for a in list(srcs) + list(lands)],
        in_specs=[_HBM] * (ns + nl) + [_SEM, _SEM, _ANY], out_specs=[_HBM] * (ns + nl),
        input_output_aliases={n: n for n in range(ns + nl)},
        compiler_params=pltpu.CompilerParams(has_side_effects=_EFFECT),
    )(*srcs, *lands, send_sems, recv_sems, after)
    return list(out[:ns]), list(out[ns:])


def _gather_plan(src_refs, land_refs):
    x, y, c = _place()
    copies = []
    for w, land in zip(src_refs, land_refs):
        R = w.shape[0]
        for cx, cy in [(1 - x, y), (x, 1 - y), (1 - x, 1 - y)]:
            copies.append((_half(w, c, R), _half(land.at[2 * x + y], c, R), (cx, cy, c),
                           _half(land.at[2 * cx + cy], c, R)))
    return copies


def _pair_plan(src_refs, land_refs):
    x, y, c = _place()
    copies = []
    for g, land in zip(src_refs, land_refs):
        half = g.shape[1] // 2
        copies.append((g.at[:, pl.ds((1 - c) * half, half), :], land, (x, y, 1 - c), land))
    return copies


def _share_plan(src_refs, land_refs):
    x, y, c = _place()
    return [(h, land, (x, y, 1 - c), land) for h, land in zip(src_refs, land_refs)]


def _pass_to_sibling(name, lands):
    nw = len(lands)

    def body(*refs):
        ins, outs = refs[:nw], refs[nw:2 * nw]
        send_sems, recv_sems = refs[2 * nw:]
        x, y, c = _place()
        chips = [(1 - x, y), (x, 1 - y), (1 - x, 1 - y)]
        copies = []
        for k in range(nw):
            R = ins[k].shape[1]
            for j, (cx, cy) in enumerate(chips):
                cp = pltpu.make_async_remote_copy(
                    src_ref=_half(ins[k].at[2 * cx + cy], c, R), dst_ref=_half(outs[k].at[2 * cx + cy], c, R),
                    send_sem=send_sems.at[3 * k + j], recv_sem=recv_sems.at[3 * k + j],
                    device_id=(x, y, 1 - c), device_id_type=MESH)
                cp.start()
                copies.append(cp)
        for k in range(nw):
            R = ins[k].shape[1]
            for j, (cx, cy) in enumerate(chips):
                pltpu.make_async_remote_copy(
                    src_ref=_half(ins[k].at[2 * cx + cy], c, R), dst_ref=_half(outs[k].at[2 * cx + cy], 1 - c, R),
                    send_sem=send_sems.at[3 * k + j], recv_sem=recv_sems.at[3 * k + j],
                    device_id=(x, y, 1 - c), device_id_type=MESH).wait_recv()
        for cp in copies:
            cp.wait_send()

    return pl.pallas_call(
        body, name=name, out_shape=[jax.ShapeDtypeStruct(a.shape, a.dtype) for a in lands],
        in_specs=[_ANY] * nw, out_specs=[_ANY] * nw, input_output_aliases={k: k for k in range(nw)},
        scratch_shapes=[pltpu.SemaphoreType.DMA((3 * nw,)), pltpu.SemaphoreType.DMA((3 * nw,))],
    )(*lands)


def _tie(vec, token):
    return vec + token[0:1, 0:1]


ROW_ALIGN = 16
TILE_ELEMS = 512 * 1024


def _tiles(rows, cols):
    fits = [t for t in range(ROW_ALIGN, min(rows, 256) + 1, ROW_ALIGN) if rows % t == 0]
    tr = fits[-1] if fits and fits[-1] >= 64 else rows
    tc = cols
    while tr * tc > TILE_ELEMS and tc % (2 * LANES) == 0:
        tc //= 2
    return tr, tc


def _pair_add(name, core, grad, recv):
    n, R, C = grad.shape
    half = R // 2
    tr, tc = _tiles(half, C)
    nr = half // tr

    def body(core_ref, g_ref, r_ref, o_ref):
        o_ref[...] = (g_ref[...].astype(F32) + r_ref[...].astype(F32)).astype(BF16)

    grid_spec = pltpu.PrefetchScalarGridSpec(
        num_scalar_prefetch=1, grid=(n, nr, C // tc),
        in_specs=[pl.BlockSpec((1, tr, tc), lambda s, r, q, core_ref: (s, core_ref[0] * nr + r, q)),
                  pl.BlockSpec((1, tr, tc), lambda s, r, q, core_ref: (s, r, q))],
        out_specs=pl.BlockSpec((1, tr, tc), lambda s, r, q, core_ref: (s, r, q)))
    return pl.pallas_call(
        body, name=name, grid_spec=grid_spec, out_shape=jax.ShapeDtypeStruct((n, half, C), BF16),
        compiler_params=pltpu.CompilerParams(dimension_semantics=("parallel", "parallel", "parallel")),
    )(core, grad, recv)


def _scatter_plan(src_refs, land_refs):
    x, y, c = _place()
    copies = []
    for p, land in zip(src_refs, land_refs):
        for j, (cx, cy) in enumerate([(1 - x, y), (x, 1 - y), (1 - x, 1 - y)]):
            copies.append((p.at[2 * cx + cy], land.at[j], (cx, cy, c), land.at[j]))
    return copies


def _chip_add(name, chip, sums, recv):
    _, H, C = sums.shape
    tr, tc = _tiles(H, C)

    def body(chip_ref, p_ref, r_ref, o_ref):
        total = p_ref[0].astype(F32)
        for j in range(3):
            total = total + r_ref[j].astype(F32)
        o_ref[...] = total

    grid_spec = pltpu.PrefetchScalarGridSpec(
        num_scalar_prefetch=1, grid=(H // tr, C // tc),
        in_specs=[pl.BlockSpec((1, tr, tc), lambda r, q, chip_ref: (chip_ref[0], r, q)),
                  pl.BlockSpec((3, tr, tc), lambda r, q, chip_ref: (0, r, q))],
        out_specs=pl.BlockSpec((tr, tc), lambda r, q, chip_ref: (r, q)))
    return pl.pallas_call(
        body, name=name, grid_spec=grid_spec, out_shape=jax.ShapeDtypeStruct((H, C), F32),
        compiler_params=pltpu.CompilerParams(dimension_semantics=("parallel", "parallel")),
    )(chip, sums, recv)


def _pair_share(name, halves):
    nw = len(halves)

    def body(*refs):
        hs, outs = refs[:nw], refs[nw:2 * nw]
        send_sems, recv_sems = refs[2 * nw:]
        x, y, c = _place()
        copies = []
        for k in range(nw):
            cp = pltpu.make_async_remote_copy(
                src_ref=hs[k], dst_ref=outs[k], send_sem=send_sems.at[k], recv_sem=recv_sems.at[k],
                device_id=(x, y, 1 - c), device_id_type=MESH)
            cp.start()
            copies.append(cp)
        for cp in copies:
            cp.wait()

    return pl.pallas_call(
        body, name=name,
        out_shape=[jax.ShapeDtypeStruct(h.shape, h.dtype) for h in halves],
        in_specs=[_ANY] * nw, out_specs=[_ANY] * nw,
        scratch_shapes=[pltpu.SemaphoreType.DMA((nw,)), pltpu.SemaphoreType.DMA((nw,))],
    )(*halves)


def _adam_halves(name, core, w, g_own, g_other, m, v):
    R, C = w.shape
    H = R // 2
    tr, tc = _tiles(H, C)
    nr, nc = H // tr, C // tc

    def body(core_ref, w_ref, go_ref, gr_ref, m_ref, v_ref, g_ref, dl_ref, nm_ref, nv_ref):
        own = (pl.program_id(0) // nr) == core_ref[0]
        g = jnp.where(own, go_ref[...], gr_ref[...])
        g_ref[...] = g
        dl_ref[...], nm_ref[...], nv_ref[...] = _adamw(w_ref[...], g, m_ref[...], v_ref[...])

    blk = pl.BlockSpec((tr, tc), lambda r, q, core_ref: (r, q))

    def half_spec(is_own):
        def index(r, q, core_ref):
            mine = ((r // nr) == core_ref[0]) == is_own
            done = is_own == (core_ref[0] == 0)
            return (jnp.where(mine, r % nr, jnp.where(done, nr - 1, 0)), jnp.where(mine, q, jnp.where(done, nc - 1, 0)))
        return pl.BlockSpec((tr, tc), index)
    out = jax.ShapeDtypeStruct((R, C), F32)
    grid_spec = pltpu.PrefetchScalarGridSpec(
        num_scalar_prefetch=1, grid=(R // tr, nc), in_specs=[blk, half_spec(True), half_spec(False), blk, blk],
        out_specs=[blk] * 4)
    return pl.pallas_call(
        body, name=name, grid_spec=grid_spec, out_shape=[out] * 4,
        compiler_params=pltpu.CompilerParams(dimension_semantics=("parallel", "parallel"),
                                             vmem_limit_bytes=_vmem(20 * tr * tc * 4)),
    )(core, w, g_own, g_other, m, v)


def kernel(x, c, w_mod, b_mod, g_pre_mix, g_post_mix, w_in, b_forget, swa_sinks, w_out, g_pre_mlp, g_post_mlp, w_up, w_down, loss_target, m_w_mod, m_b_mod, m_g_pre_mix, m_g_post_mix, m_w_in, m_b_forget, m_swa_sinks, m_w_out, m_g_pre_mlp, m_g_post_mlp, m_w_up, m_w_down, v_w_mod, v_b_mod, v_g_pre_mix, v_g_post_mix, v_w_in, v_b_forget, v_swa_sinks, v_w_out, v_g_pre_mlp, v_g_post_mlp, v_w_up, v_w_down):
    S, D = x.shape[1], x.shape[2]
    n_heads = D // HEAD_DIM
    n_fox = n_heads // 2
    n_swa = n_heads - n_fox
    n_kv = max(1, n_swa // 4)
    fox_w, swa_w, kv_w = n_fox * HEAD_DIM, n_swa * HEAD_DIM, n_kv * HEAD_DIM
    main_w = 3 * fox_w + swa_w + 2 * kv_w
    in_w = main_w + n_fox
    mod_cols = w_mod.shape[2]

    ax, ay, ac = _place()
    chip = 2 * ax + ay
    dev = 2 * chip + ac
    chip_arr = jnp.reshape(chip, (1,)).astype(jnp.int32)
    core_arr = jnp.reshape(ac, (1,)).astype(jnp.int32)

    x2, tgt = x[0], loss_target[0]

    c_all, _ = _allgather8("gather_c", c.reshape(8, D // 8))
    c_all = c_all.reshape(N_DEV, D)
    b_shard = lax.dynamic_slice_in_dim(b_mod, chip * mod_cols, mod_cols, axis=1)
    mod_shard = _mod_fwd(jnp.pad(c_all, ((0, 16 - N_DEV), (0, 0))), w_mod[0], b_shard)[:N_DEV]
    mod_all, token = _allgather8("gather_mod", mod_shard)
    mod_all = mod_all.reshape(N_CHIPS, 2, N_DEV, mod_cols)[:, 0]
    mod = lax.dynamic_index_in_dim(mod_all, dev, axis=1, keepdims=False).reshape(N_MOD, 1, D)
    sh_a, sc_a, gt_a, sh_m, sc_m, gt_m = [mod[n] for n in range(N_MOD)]

    in_rows = in_w // N_CHIPS
    in_rows_pad = -(-in_rows // LANES) * LANES
    slab_w = N_CHIPS * in_rows_pad

    def rows_of(a):
        return jnp.pad(a[0].T, ((0, in_rows_pad - in_rows), (0, 0)))

    def slab_cols(lo, hi):
        spans = []
        while lo < hi:
            s, r = divmod(lo, in_rows)
            n = min(hi - lo, in_rows - r)
            spans.append((s * in_rows_pad + r, s * in_rows_pad + r + n))
            lo += n
        return spans

    gate_lo = 3 * fox_w
    main_spans = slab_cols(0, gate_lo) + slab_cols(gate_lo + n_fox, in_w)
    (gate_first, gate_last), = slab_cols(gate_lo, gate_lo + n_fox)

    names = ["w_in", "w_out", "w_up", "w_down"]
    flights = {}
    for n, w in zip(names, [rows_of(w_in), w_out[0], w_up[0], w_down[0]]):
        shard = _tie(w, token).astype(BF16)
        flights[n] = _ici_start("gather_start_" + n, [shard], [jax.ShapeDtypeStruct((N_CHIPS,) + shard.shape, BF16)],
                                _gather_plan)
        token = flights[n][4]
    sc_a = _tie(sc_a, token)

    def gathered(n, after):
        send, recv, srcs, lands, _ = flights[n]
        srcs, lands = _ici_wait("gather_wait_" + n, send, recv, srcs, lands, _gather_plan, after)
        lands = _pass_to_sibling("gather_pass_" + n, lands)
        return lax.dynamic_update_index_in_dim(lands[0], srcs[0], chip, 0)

    d_ff = N_CHIPS * w_up.shape[2]

    h = _pre_norm(x2, g_pre_mix, sc_a, sh_a)
    w_slab_t = gathered("w_in", h).reshape(slab_w, D)
    tm_p, tn_p = _fit(MM_TM, S), _fit(512, slab_w)
    win0 = gate_first // LANES * LANES
    win_j, win_off = divmod(win0, tn_p)
    assert win_off + 2 * LANES <= tn_p and gate_last - win0 <= 2 * LANES

    def proj_epilogue(acc, ex, outs):
        outs[0][...] = acc.astype(BF16)

        @pl.when(pl.program_id(1) == win_j)
        def _():
            outs[1][...] = acc[:, win_off:win_off + 2 * LANES]

    proj_slab, gate_win = _matmul(
        "in_proj", h, w_slab_t, "nt",
        [((S, slab_w), BF16, (tm_p, tn_p), lambda i, j: (i, j)), ((S, 2 * LANES), F32, (tm_p, 2 * LANES), lambda i, j: (i, 0))],
        proj_epilogue, tn=tn_p, revisits=True)
    proj = jnp.concatenate([proj_slab[:, lo:hi] for lo, hi in main_spans], axis=1)
    fg = jnp.pad(gate_win[:, gate_first - win0:gate_last - win0], ((0, 0), (0, LANES - n_fox)))
    b_pad = jnp.pad(b_forget, ((0, 0), (0, LANES - n_fox)))
    cum_row = _fox_gate_fwd(fg, b_pad)[:n_fox].reshape(n_fox, 1, S)
    fox_o, fox_lse = _fox_fwd(proj, cum_row, n_fox)

    cos, sin_signed = _rope_tables(S)
    rq = _rope("rope_fwd", proj, 3 * n_fox, n_swa + n_kv, cos, sin_signed)
    v_first = 3 * n_fox + n_swa + n_kv
    sinks = swa_sinks[0]
    swa_o, swa_lse = _swa_fwd(rq, proj, v_first, sinks, n_swa, n_kv)

    mixcat = jnp.concatenate([fox_o, swa_o], axis=1).astype(BF16)
    w_out_f = gathered("w_out", mixcat).reshape(D, D)
    mix = _mm_plain("out_proj", mixcat, w_out_f, "nn", F32)
    x1, h2 = _post_mix(x2, mix, g_post_mix, gt_a, g_pre_mlp, sc_m, sh_m)
    w_up_f = jnp.transpose(gathered("w_up", h2), (1, 0, 2)).reshape(D, d_ff)

    tm_u, tn_u = _fit(MM_TM, S), _fit(MM_TN, d_ff)

    def up_epilogue(acc, ex, outs):
        outs[0][...] = acc.astype(BF16)
        r = jnp.maximum(acc, 0.0)
        outs[1][...] = (r * r).astype(BF16)

    ublk = ((S, d_ff), BF16, (tm_u, tn_u), lambda i, j: (i, j))
    u, a = _matmul("mlp_up", h2, w_up_f, "nn", [ublk, ublk], up_epilogue)
    w_down_f = gathered("w_down", a).reshape(d_ff, D)
    y = _mm_plain("mlp_down", a, w_down_f, "nn", F32)

    dy, dout, loss_part, acc_mlp_post = _loss_and_post_mlp_bwd(x1, y, tgt, g_post_mlp, gt_m)

    def du_epilogue(acc, ex, outs):
        outs[0][...] = (acc * (2.0 * jnp.maximum(ex[0][...].astype(F32), 0.0))).astype(BF16)

    du = _matmul("mlp_down_bwd", dy, w_down_f, "nt", [ublk], du_epilogue,
                 extras=[(u, (tm_u, tn_u), lambda i, j: (i, j))])[0]
    def pair_start(tag, fulls):
        return _ici_start("grad_pair_start_" + tag, fulls,
                          [jax.ShapeDtypeStruct((N_CHIPS, g.shape[1] // 2, g.shape[2]), BF16) for g in fulls],
                          _pair_plan, per_source=1)

    def scatter_start(tag, pair_flights, after):
        sums = []
        for k, (send, recv, fulls, lands, _) in enumerate(pair_flights):
            fulls, from_sibling = _ici_wait("grad_pair_wait_%s_%d" % (tag, k), send, recv, fulls, lands, _pair_plan, after)
            sums += [_pair_add("pair_add_%s_%d_%d" % (tag, k, n), core_arr, g, r)
                     for n, (g, r) in enumerate(zip(fulls, from_sibling))]
        return _ici_start("grad_scatter_start_" + tag, sums,
                          [jax.ShapeDtypeStruct((3,) + p.shape[1:], BF16) for p in sums], _scatter_plan)

    def scatter_finish(tag, flight, after):
        send, recv, srcs, lands, _ = flight
        sums, received = _ici_wait("grad_scatter_wait_" + tag, send, recv, srcs, lands, _scatter_plan, after)
        return [_chip_add("chip_add_%s_%d" % (tag, k), chip_arr, p, r) for k, (p, r) in enumerate(zip(sums, received))]

    g_down = _mm_plain("grad_w_down", a, dy, "tn", BF16)
    pair_down = pair_start("down", [g_down.reshape(N_CHIPS, d_ff // N_CHIPS, D)])
    tn_s = _fit(MM_TN, w_up.shape[2])
    per = w_up.shape[2] // tn_s

    def shard_epilogue(acc, ex, outs):
        outs[0][0] = acc.astype(BF16)

    g_up = _matmul("grad_w_up", h2, du, "tn",
                   [((N_CHIPS, D, w_up.shape[2]), BF16, (1, _fit(MM_TM, D), tn_s), lambda i, j: (j // per, i, j % per))],
                   shard_epilogue, extras=[_behind(pair_down[4])], tn=tn_s)[0]
    pair_up = pair_start("up", [g_up])
    dh2 = _mm_plain("mlp_up_bwd", du, w_up_f, "nt", F32, after=pair_up[4])
    flight_mlp = scatter_start("mlp", [pair_up, pair_down], dh2)
    dx1, dmix, acc_mid = _pre_mlp_and_post_mix_bwd(dh2, x1, dout, mix, _tie(g_pre_mlp, flight_mlp[4]), sc_m,
                                                   g_post_mix, gt_a)

    dmixcat = _mm_plain("out_proj_bwd", dmix, w_out_f, "nt", F32)
    g_out = _mm_plain("grad_w_out", mixcat, dmix, "tn", BF16)

    fdq, fdk, fdv, dcum_row, dcum_q = _fox_bwd(proj, fox_o, dmixcat, fox_lse, cum_row, n_fox)
    dcum_k = jnp.pad(dcum_row.reshape(n_fox, S), ((0, LANES - n_fox), (0, 0)))
    dfg, db_forget = _fox_gate_bwd(dcum_k, dcum_q, fg, b_pad)

    group_w = (n_swa // n_kv) * HEAD_DIM
    sdq, sdk, sdv, dsink = _swa_bwd(rq, proj, v_first, sinks, swa_o, dmixcat, fox_w // group_w, swa_lse, n_swa, n_kv)
    drq = jnp.concatenate([sdq, jnp.transpose(sdk, (1, 0, 2)).reshape(S, kv_w).astype(BF16)], axis=1)
    d_sq_sk = _rope("rope_bwd", drq, 0, n_swa + n_kv, cos, -sin_signed)
    dsv = jnp.transpose(sdv, (1, 0, 2)).reshape(S, kv_w).astype(BF16)
    dproj = jnp.concatenate([fdq, fdk, fdv, d_sq_sk, dsv], axis=1)

    pieces = []
    for s in range(N_CHIPS):
        lo, hi = s * in_rows, (s + 1) * in_rows
        for src, first, last, shift in [(dproj, 0, gate_lo, 0), (dfg, gate_lo, gate_lo + n_fox, gate_lo),
                                        (dproj, gate_lo + n_fox, in_w, n_fox)]:
            if max(lo, first) < min(hi, last):
                pieces.append(src[:, max(lo, first) - shift:min(hi, last) - shift])
        pieces.append(jnp.zeros((S, in_rows_pad - in_rows), BF16))
    dproj_slab = jnp.concatenate(pieces, axis=1)

    g_in_t = _mm_plain("grad_w_in", dproj_slab, h, "tn", BF16, tm=_fit(512, slab_w))
    pair_mix = pair_start("mix", [g_in_t.reshape(N_CHIPS, in_rows_pad, D), g_out.reshape(N_CHIPS, D // N_CHIPS, D)])
    dh = _mm_plain("in_proj_bwd", dproj_slab, w_slab_t, "nn", F32, after=pair_mix[4], tk=_fit(2560, slab_w))
    grad_x, acc_pre = _pre_mix_bwd(dh, x2, dx1, g_pre_mix, sc_a)

    zero_row = jnp.zeros((1, D), F32)
    tail = jnp.concatenate([db_forget[0:1, :n_fox], dsink[:, 0, :n_swa // n_kv].reshape(1, n_swa),
                            loss_part[0:1, 0:1], jnp.zeros((1, D - n_fox - n_swa - 1), F32)], axis=1)
    partial = jnp.concatenate([
        acc_pre[0:1], acc_pre[1:2], acc_mid[3:4], acc_mid[0:1], acc_mid[1:2], acc_mlp_post[0:1],
        acc_pre[2:3], acc_mid[4:5], acc_mid[2:3], acc_mlp_post[1:2], tail] + [zero_row] * 5, axis=0)
    gathered_small, token = _allgather8("gather_small_grads", partial)

    flight_mix = scatter_start("mix", [pair_mix], token)
    halves_mlp = scatter_finish("mlp", flight_mlp, flight_mix[4])
    share_mlp = _ici_start("grad_share_start_mlp", halves_mlp,
                           [jax.ShapeDtypeStruct(hv.shape, F32) for hv in halves_mlp], _share_plan, per_source=1)

    def pack(bm, gpm, gqm, gpl, gql, bf, sk):
        last = jnp.concatenate([bf, sk, jnp.zeros((1, D - n_fox - n_swa), F32)], axis=1)
        return jnp.concatenate([bm.reshape(N_MOD, D), gpm, gqm, gpl, gql, last, jnp.zeros((5, D), F32)], axis=0)

    def unpack(p):
        return {"b_mod": p[0:N_MOD].reshape(1, N_MOD * D), "g_pre_mix": p[6:7], "g_post_mix": p[7:8],
                "g_pre_mlp": p[8:9], "g_post_mlp": p[9:10], "b_forget": p[10:11, :n_fox],
                "swa_sinks": p[10:11, n_fox:n_fox + n_swa]}

    small_out = _small_update(
        gathered_small, _tie(pack(b_mod, g_pre_mix, g_post_mix, g_pre_mlp, g_post_mlp, b_forget, swa_sinks), share_mlp[4]),
        pack(m_b_mod, m_g_pre_mix, m_g_post_mix, m_g_pre_mlp, m_g_post_mlp, m_b_forget, m_swa_sinks),
        pack(v_b_mod, v_g_pre_mix, v_g_post_mix, v_g_pre_mlp, v_g_post_mlp, v_b_forget, v_swa_sinks))
    g_small, d_small, m_small, v_small = [unpack(p) for p in small_out]
    loss = small_out[0][N_MOD + 4, n_fox + n_swa]

    dmod_all = gathered_small.reshape(N_DEV, 16, D)[:, :N_MOD].reshape(N_DEV, N_MOD * D)
    dmod_shard = _tie(lax.dynamic_slice_in_dim(dmod_all, chip * mod_cols, mod_cols, axis=1), share_mlp[4])
    g_w_mod, d_w_mod, nm_w_mod, nv_w_mod = _mod_update(c_all.T, dmod_shard, w_mod[0], m_w_mod[0], v_w_mod[0])
    send, recv, halves_mlp, lands, _ = share_mlp
    halves_mlp, others_mlp = _ici_wait("grad_share_wait_mlp", send, recv, halves_mlp, lands, _share_plan,
                                       d_w_mod[:8, :LANES] + small_out[1][:8, :LANES])

    grads = dict(g_small, w_mod=g_w_mod[None])
    deltas = dict(d_small, w_mod=d_w_mod[None])
    new_m = dict(m_small, w_mod=nm_w_mod[None])
    new_v = dict(v_small, w_mod=nv_w_mod[None])
    weights = {"w_in": (w_in, m_w_in, v_w_in), "w_out": (w_out, m_w_out, v_w_out), "w_up": (w_up, m_w_up, v_w_up),
               "w_down": (w_down, m_w_down, v_w_down)}

    def big_update(n, own, other):
        transposed = n == "w_in"
        w, m, v = [rows_of(a) if transposed else a[0] for a in weights[n]]
        outs = _adam_halves("adam_" + n, core_arr, w, own, other, m, v)
        if transposed:
            outs = [o[:in_rows].T for o in outs]
        grads[n], deltas[n], new_m[n], new_v[n] = [o[None] for o in outs]

    big_update("w_up", halves_mlp[0], others_mlp[0])
    big_update("w_down", halves_mlp[1], others_mlp[1])
    ran = deltas["w_down"][0, :8, :LANES] + deltas["w_up"][0, :8, :LANES] + d_w_mod[:8, :LANES]
    halves_mix = scatter_finish("mix", flight_mix, ran)
    others_mix = _pair_share("grad_pair_share_mix", halves_mix)
    big_update("w_in", halves_mix[0], others_mix[0])
    big_update("w_out", halves_mix[1], others_mix[1])

    order = ["w_mod", "b_mod", "g_pre_mix", "g_post_mix", "w_in", "b_forget", "swa_sinks", "w_out", "g_pre_mlp",
             "g_post_mlp", "w_up", "w_down"]
    return (loss, grad_x[None], *[grads[n] for n in order], *[deltas[n] for n in order],
            *[new_m[n] for n in order], *[new_v[n] for n in order])
```

```python
import jax
import jax.numpy as jnp
from jax import lax
from jax.experimental import pallas as pl
from jax.experimental.pallas import tpu as pltpu

F32 = jnp.float32
BF16 = jnp.bfloat16
MESH = pl.DeviceIdType.MESH

HEAD_DIM = 128
SWA_BLOCK = 128
ROPE_THETA = 10000.0
NORM_EPS = 1e-6
NEG = -1e30
N_MOD = 6
ADAM_LR = 0.001
ADAM_B1 = 0.9
ADAM_B2 = 0.999
ADAM_EPS = 1e-08
ADAM_WD = 0.01
ADAM_STEP = 10
N_CHIPS = 4
N_DEV = 8
LANES = 128
VMEM_CAP = 60 * 1024 * 1024

_NN = (((1,), (0,)), ((), ()))
_NT = (((1,), (1,)), ((), ()))
_TN = (((0,), (0,)), ((), ()))


def _vmem(nbytes):
    return int(min(VMEM_CAP, nbytes * 5 // 4 + (4 << 20)))


def _nbytes(shape, dtype):
    n = 1
    for s in shape:
        n *= s
    return n * jnp.dtype(dtype).itemsize


def _fit(t, n):
    t = min(t, n)
    assert n % t == 0, (t, n)
    return t


MM_TM, MM_TN, MM_TK = 512, 1024, 2048


def _matmul(name, a, b, mode, out_defs, epilogue, extras=(), tm=MM_TM, tn=MM_TN, tk=MM_TK, revisits=False):
    if mode == "nn":
        (M, K), (K2, N) = a.shape, b.shape
    elif mode == "nt":
        (M, K), (N, K2) = a.shape, b.shape
    else:
        (K, M), (K2, N) = a.shape, b.shape
    assert K == K2, (a.shape, b.shape, mode)
    tm, tn, tk = _fit(tm, M), _fit(tn, N), _fit(tk, K)
    nk = K // tk
    dims = {"nn": _NN, "nt": _NT, "tn": _TN}[mode]
    a_spec = (pl.BlockSpec((tk, tm), lambda i, j, k: (k, i)) if mode == "tn"
              else pl.BlockSpec((tm, tk), lambda i, j, k: (i, k)))
    b_spec = (pl.BlockSpec((tn, tk), lambda i, j, k: (j, k)) if mode == "nt"
              else pl.BlockSpec((tk, tn), lambda i, j, k: (k, j)))
    n_ex, n_out = len(extras), len(out_defs)

    def body(*refs):
        a_ref, b_ref = refs[0], refs[1]
        ex = refs[2:2 + n_ex]
        outs = refs[2 + n_ex:2 + n_ex + n_out]
        prod = lax.dot_general(a_ref[...], b_ref[...], dims, preferred_element_type=F32)
        if nk == 1:
            epilogue(prod, ex, outs)
        else:
            acc_ref = refs[-1]
            k = pl.program_id(2)

            @pl.when(k == 0)
            def _():
                acc_ref[...] = prod

            @pl.when(k > 0)
            def _():
                acc_ref[...] += prod

            @pl.when(k == nk - 1)
            def _():
                epilogue(acc_ref[...], ex, outs)

    def wrap(f):
        return lambda i, j, k: f(i, j)

    in_specs = [a_spec, b_spec] + [pl.BlockSpec(blk, wrap(f)) for _, blk, f in extras]
    out_specs = [pl.BlockSpec(blk, wrap(f)) for _, _, blk, f in out_defs]
    out_shape = [jax.ShapeDtypeStruct(s, d) for s, d, _, _ in out_defs]
    need = 2 * (tm * tk + tk * tn) * a.dtype.itemsize + 3 * tm * tn * 4
    need += sum(2 * _nbytes(blk, arr.dtype) for arr, blk, _ in extras)
    need += sum(2 * _nbytes(blk, d) for _, d, blk, _ in out_defs)
    res = pl.pallas_call(
        body, name=name, grid=(M // tm, N // tn, nk),
        in_specs=in_specs, out_specs=out_specs, out_shape=out_shape,
        scratch_shapes=[pltpu.VMEM((tm, tn), F32)] if nk > 1 else [],
        compiler_params=pltpu.CompilerParams(
            dimension_semantics=("parallel", "arbitrary" if revisits else "parallel", "arbitrary"),
            vmem_limit_bytes=_vmem(need)),
    )(a, b, *[arr for arr, _, _ in extras])
    return res


def _behind(token):
    return (token, (8, LANES), lambda i, j: (0, 0))


def _mm_plain(name, a, b, mode, out_dtype, after=None, **tiles):
    if mode == "nn":
        M, N = a.shape[0], b.shape[1]
    elif mode == "nt":
        M, N = a.shape[0], b.shape[0]
    else:
        M, N = a.shape[1], b.shape[1]
    tm, tn = _fit(tiles.get("tm", MM_TM), M), _fit(tiles.get("tn", MM_TN), N)

    def epi(acc, ex, outs):
        outs[0][...] = acc.astype(out_dtype)

    return _matmul(name, a, b, mode, [((M, N), out_dtype, (tm, tn), lambda i, j: (i, j))], epi,
                   extras=[] if after is None else [_behind(after)], **tiles)[0]


def _rstd(v):
    return lax.rsqrt(jnp.mean(v * v, axis=-1, keepdims=True) + NORM_EPS)


def _row_call(name, body, row_ins, vec_ins, row_outs, acc_outs, S, D, tr):
    tr = _fit(tr, S)
    row_spec = pl.BlockSpec((tr, D), lambda r: (r, 0))
    vec_spec = pl.BlockSpec((1, D), lambda r: (0, 0))
    in_specs = [row_spec] * len(row_ins) + [vec_spec] * len(vec_ins)
    out_specs = [row_spec] * len(row_outs) + [pl.BlockSpec(shp, lambda r: (0, 0)) for shp in acc_outs]
    out_shape = [jax.ShapeDtypeStruct((S, D), d) for d in row_outs] + [jax.ShapeDtypeStruct(shp, F32) for shp in acc_outs]
    need = sum(2 * tr * D * a.dtype.itemsize for a in row_ins) + sum(2 * tr * D * jnp.dtype(d).itemsize for d in row_outs)
    need += 8 * tr * D * 4
    return pl.pallas_call(
        body, name=name, grid=(S // tr,), in_specs=in_specs, out_specs=out_specs, out_shape=out_shape,
        compiler_params=pltpu.CompilerParams(dimension_semantics=("arbitrary",), vmem_limit_bytes=_vmem(need)),
    )(*row_ins, *vec_ins)


def _acc_rows(ref, rows):
    @pl.when(pl.program_id(0) == 0)
    def _():
        ref[...] = jnp.zeros_like(ref)
    for n, r in enumerate(rows):
        ref[n:n + 1, :] += r


def _pre_norm(x, g, sc, sh):
    S, D = x.shape

    def body(x_ref, g_ref, sc_ref, sh_ref, h_ref):
        xv = x_ref[...]
        xn = xv * _rstd(xv)
        h_ref[...] = (xn * g_ref[...] * (1.0 + sc_ref[...]) + sh_ref[...]).astype(BF16)

    return _row_call("pre_norm_mix", body, [x], [g, sc, sh], [BF16], [], S, D, 256)[0]


def _post_mix(x, mix, g_post, gt, g_pre, sc, sh):
    S, D = x.shape

    def body(x_ref, mix_ref, gp_ref, gt_ref, g2_ref, sc_ref, sh_ref, x1_ref, h2_ref):
        mv = mix_ref[...]
        x1 = x_ref[...] + gt_ref[...] * (mv * _rstd(mv) * gp_ref[...])
        x1_ref[...] = x1
        h2_ref[...] = (x1 * _rstd(x1) * g2_ref[...] * (1.0 + sc_ref[...]) + sh_ref[...]).astype(BF16)

    return _row_call("post_mix_pre_mlp", body, [x, mix], [g_post, gt, g_pre, sc, sh], [F32, BF16], [], S, D, 256)


def _loss_and_post_mlp_bwd(x1, y, target, g_post, gt):
    S, D = x1.shape

    def body(x1_ref, y_ref, t_ref, g_ref, gt_ref, dy_ref, dout_ref, loss_ref, acc_ref):
        yv = y_ref[...]
        r = _rstd(yv)
        yh = yv * r
        n = yh * g_ref[...]
        diff = x1_ref[...] + gt_ref[...] * n - t_ref[...]
        dout = diff * (1.0 / D)
        dout_ref[...] = dout
        dn = dout * gt_ref[...]
        dyh = dn * g_ref[...]
        dy_ref[...] = (r * (dyh - yh * jnp.mean(dyh * yh, axis=-1, keepdims=True))).astype(BF16)
        _acc_rows(acc_ref, [jnp.sum(dout * n, axis=0, keepdims=True), jnp.sum(dn * yh, axis=0, keepdims=True)])

        @pl.when(pl.program_id(0) == 0)
        def _():
            loss_ref[...] = jnp.zeros_like(loss_ref)
        loss_ref[...] += jnp.full(loss_ref.shape, (0.5 / D) * jnp.sum(diff * diff), F32)

    return _row_call("loss_post_mlp_bwd", body, [x1, y, target], [g_post, gt], [BF16, F32],
                     [(8, LANES), (8, D)], S, D, 128)


def _pre_mlp_and_post_mix_bwd(dh2, x1, dout, mix, g_pre, sc, g_post, gt):
    S, D = x1.shape

    def body(dh_ref, x1_ref, dout_ref, mix_ref, g_ref, sc_ref, gp_ref, gt_ref, dx1_ref, dmix_ref, acc_ref):
        dh = dh_ref[...]
        x1v = x1_ref[...]
        r3 = _rstd(x1v)
        xn = x1v * r3
        dxn = dh * (1.0 + sc_ref[...]) * g_ref[...]
        dx1 = dout_ref[...] + r3 * (dxn - xn * jnp.mean(dxn * xn, axis=-1, keepdims=True))
        dx1_ref[...] = dx1
        mv = mix_ref[...]
        r2 = _rstd(mv)
        mh = mv * r2
        dn = dx1 * gt_ref[...]
        dmh = dn * gp_ref[...]
        dmix_ref[...] = (r2 * (dmh - mh * jnp.mean(dmh * mh, axis=-1, keepdims=True))).astype(BF16)
        _acc_rows(acc_ref, [
            jnp.sum(dh, axis=0, keepdims=True),
            jnp.sum(dh * xn * g_ref[...], axis=0, keepdims=True),
            jnp.sum(dh * (1.0 + sc_ref[...]) * xn, axis=0, keepdims=True),
            jnp.sum(dx1 * mh * gp_ref[...], axis=0, keepdims=True),
            jnp.sum(dn * mh, axis=0, keepdims=True)])

    return _row_call("pre_mlp_post_mix_bwd", body, [dh2, x1, dout, mix], [g_pre, sc, g_post, gt], [F32, BF16],
                     [(8, D)], S, D, 128)


def _pre_mix_bwd(dh, x, dx1, g_pre, sc):
    S, D = x.shape

    def body(dh_ref, x_ref, dx1_ref, g_ref, sc_ref, gx_ref, acc_ref):
        dhv = dh_ref[...]
        xv = x_ref[...]
        r = _rstd(xv)
        xn = xv * r
        dxn = dhv * (1.0 + sc_ref[...]) * g_ref[...]
        gx_ref[...] = dx1_ref[...] + r * (dxn - xn * jnp.mean(dxn * xn, axis=-1, keepdims=True))
        _acc_rows(acc_ref, [
            jnp.sum(dhv, axis=0, keepdims=True),
            jnp.sum(dhv * xn * g_ref[...], axis=0, keepdims=True),
            jnp.sum(dhv * (1.0 + sc_ref[...]) * xn, axis=0, keepdims=True)])

    return _row_call("pre_mix_bwd", body, [dh, x, dx1], [g_pre, sc], [F32], [(8, D)], S, D, 128)


CUM_BLOCK = 256


def _tri(n, upper):
    r = lax.broadcasted_iota(jnp.int32, (n, n), 0)
    c = lax.broadcasted_iota(jnp.int32, (n, n), 1)
    return ((c >= r) if upper else (c <= r)).astype(F32)


def _fox_gate_fwd(fg, b_pad):
    S = fg.shape[0]
    cb = _fit(CUM_BLOCK, S)

    def body(fg_ref, b_ref, cumt_ref, cum_ref):
        low = _tri(cb, False)
        carry = jnp.zeros((1, LANES), F32)
        for n in range(S // cb):
            z = fg_ref[n * cb:(n + 1) * cb, :] + b_ref[...]
            logf = jnp.minimum(z, 0.0) - jnp.log(1.0 + jnp.exp(-jnp.abs(z)))
            blk = jnp.dot(low, logf, precision=lax.Precision.HIGHEST, preferred_element_type=F32) + carry
            cum_ref[n * cb:(n + 1) * cb, :] = blk
            carry = blk[cb - 1:cb, :]
        cumt_ref[...] = cum_ref[...].T

    return pl.pallas_call(
        body, name="fox_gate_fwd", out_shape=jax.ShapeDtypeStruct((LANES, S), F32),
        scratch_shapes=[pltpu.VMEM((S, LANES), F32)],
        compiler_params=pltpu.CompilerParams(vmem_limit_bytes=_vmem(6 * S * LANES * 4)),
    )(fg, b_pad)


def _fox_gate_bwd(dcum_k, dcum_q, fg, b_pad):
    S = fg.shape[0]
    n_fox = dcum_q.shape[0]
    cb = _fit(CUM_BLOCK, S)

    def body(dk_ref, dq_ref, fg_ref, b_ref, dfg_ref, db_ref, dc_ref):
        lane = lax.broadcasted_iota(jnp.int32, (S, LANES), 1)
        dc = dk_ref[...].T
        for h in range(n_fox):
            dc = dc + jnp.where(lane == h, dq_ref[h], 0.0)
        dc_ref[...] = dc
        up = _tri(cb, True)
        carry = jnp.zeros((1, LANES), F32)
        db = jnp.zeros((1, LANES), F32)
        for n in reversed(range(S // cb)):
            blk = jnp.dot(up, dc_ref[n * cb:(n + 1) * cb, :], precision=lax.Precision.HIGHEST,
                          preferred_element_type=F32) + carry
            carry = blk[0:1, :]
            z = fg_ref[n * cb:(n + 1) * cb, :] + b_ref[...]
            dfg = blk * (1.0 / (1.0 + jnp.exp(z)))
            dfg_ref[n * cb:(n + 1) * cb, :] = dfg.astype(BF16)
            db = db + jnp.sum(dfg, axis=0, keepdims=True)
        db_ref[...] = jnp.broadcast_to(db, db_ref.shape)

    return pl.pallas_call(
        body, name="fox_gate_bwd",
        out_shape=[jax.ShapeDtypeStruct((S, LANES), BF16), jax.ShapeDtypeStruct((8, LANES), F32)],
        scratch_shapes=[pltpu.VMEM((S, LANES), F32)],
        compiler_params=pltpu.CompilerParams(vmem_limit_bytes=_vmem((8 + 2 * n_fox) * S * LANES * 4)),
    )(dcum_k, dcum_q, fg, b_pad)


FOX_TILE = 512


LOG2E = 1.4426950408889634


def _fox_scores(q, k, ck2, masked, t):
    s = lax.dot_general(q, k, _NT, preferred_element_type=F32) * (HEAD_DIM ** -0.5 * LOG2E) - ck2
    if masked:
        row = lax.broadcasted_iota(jnp.int32, (t, t), 0)
        col = lax.broadcasted_iota(jnp.int32, (t, t), 1)
        s = jnp.where(col <= row, s, NEG)
    return s


def _fox_fwd(proj, cum_row, n_fox):
    S = proj.shape[0]
    t = _fit(FOX_TILE, S)
    nq = S // t

    def body(q_ref, k_ref, v_ref, ck_ref, o_ref, lse_ref):
        def q_block(qi, _):
            q0 = pl.multiple_of(qi * t, t)
            q = q_ref[pl.ds(q0, t), :]

            def kv_block(j, carry, masked):
                m, l, acc = carry
                k0 = pl.multiple_of(j * t, t)
                s = _fox_scores(q, k_ref[pl.ds(k0, t), :], ck_ref[0, :, pl.ds(k0, t)] * LOG2E, masked, t)
                m_new = jnp.maximum(m, jnp.max(s, axis=-1, keepdims=True))
                alpha = jnp.exp2(m - m_new)
                p = jnp.exp2(s - m_new)
                l = alpha * l + jnp.sum(p, axis=-1, keepdims=True)
                acc = alpha * acc + jnp.dot(p.astype(BF16), v_ref[pl.ds(k0, t), :], preferred_element_type=F32)
                return m_new, l, acc

            init = (jnp.full((t, 1), NEG, F32), jnp.zeros((t, 1), F32), jnp.zeros((t, HEAD_DIM), F32))
            carry = lax.fori_loop(0, qi, lambda j, cr: kv_block(j, cr, False), init)
            m, l, acc = kv_block(qi, carry, True)
            o_ref[pl.ds(q0, t), :] = acc / l
            lse_ref[0, pl.ds(q0, t), :] = jnp.broadcast_to(m + jnp.log(l) * LOG2E, (t, LANES))
            return 0

        lax.fori_loop(0, nq, q_block, 0)

    col = lambda off: pl.BlockSpec((S, HEAD_DIM), lambda h: (0, off + h))
    per_head = pl.BlockSpec((1, S, LANES), lambda h: (h, 0, 0))
    return pl.pallas_call(
        body, name="fox_fwd", grid=(n_fox,),
        in_specs=[col(0), col(n_fox), col(2 * n_fox), pl.BlockSpec((1, 1, S), lambda h: (h, 0, 0))],
        out_specs=[pl.BlockSpec((S, HEAD_DIM), lambda h: (0, h)), per_head],
        out_shape=[jax.ShapeDtypeStruct((S, n_fox * HEAD_DIM), F32), jax.ShapeDtypeStruct((n_fox, S, LANES), F32)],
        compiler_params=pltpu.CompilerParams(dimension_semantics=("parallel",),
                                             vmem_limit_bytes=_vmem(16 * S * HEAD_DIM * 4 + 12 * t * t * 4)),
    )(proj, proj, proj, cum_row)


def _fox_bwd(proj, o, do, lse_b, cum_row, n_fox):
    S = proj.shape[0]
    t = _fit(FOX_TILE, S)
    nq = S // t
    scale = HEAD_DIM ** -0.5

    def body(q_ref, k_ref, v_ref, o_ref, do_ref, lse_ref, ck_ref, dq_ref, dk_ref, dv_ref, dc_ref, dcq_ref,
             dq_acc, delta_ref):
        dq_acc[...] = jnp.zeros_like(dq_acc)
        dcq_ref[...] = jnp.zeros_like(dcq_ref)

        def delta_block(qi, _):
            q0 = pl.multiple_of(qi * t, t)
            d = jnp.sum(do_ref[pl.ds(q0, t), :] * o_ref[pl.ds(q0, t), :], axis=-1, keepdims=True)
            delta_ref[pl.ds(q0, t), :] = jnp.broadcast_to(d, (t, LANES))
            return 0

        lax.fori_loop(0, nq, delta_block, 0)

        def kv_block(j, _):
            k0 = pl.multiple_of(j * t, t)
            k = k_ref[pl.ds(k0, t), :]
            v = v_ref[pl.ds(k0, t), :]
            ck2 = ck_ref[0, :, pl.ds(k0, t)] * LOG2E

            def q_block(qi, carry, masked):
                dk, dv, dc = carry
                q0 = pl.multiple_of(qi * t, t)
                q = q_ref[pl.ds(q0, t), :]
                dov = do_ref[pl.ds(q0, t), :].astype(BF16)
                p = jnp.exp2(_fox_scores(q, k, ck2, masked, t) - lse_ref[0, pl.ds(q0, t), :][:, :1])
                dp = lax.dot_general(dov, v, _NT, preferred_element_type=F32)
                ds = p * (dp - delta_ref[pl.ds(q0, t), :][:, :1])
                dsb = ds.astype(BF16)
                dv = dv + lax.dot_general(p.astype(BF16), dov, _TN, preferred_element_type=F32)
                dk = dk + lax.dot_general(dsb, q, _TN, preferred_element_type=F32)
                dq_acc[pl.ds(q0, t), :] += jnp.dot(dsb, k, preferred_element_type=F32)
                dc = dc - jnp.sum(ds, axis=0, keepdims=True)
                dcq_ref[0, pl.ds(q0, t), :] += jnp.broadcast_to(jnp.sum(ds, axis=1, keepdims=True), (t, LANES))
                return dk, dv, dc

            init = (jnp.zeros((t, HEAD_DIM), F32), jnp.zeros((t, HEAD_DIM), F32), jnp.zeros((1, t), F32))
            carry = q_block(j, init, True)
            dk, dv, dc = lax.fori_loop(j + 1, nq, lambda qi, cr: q_block(qi, cr, False), carry)
            dk_ref[pl.ds(k0, t), :] = (dk * scale).astype(BF16)
            dv_ref[pl.ds(k0, t), :] = dv.astype(BF16)
            dc_ref[0, :, pl.ds(k0, t)] = dc
            return 0

        lax.fori_loop(0, nq, kv_block, 0)
        dq_ref[...] = (dq_acc[...] * scale).astype(BF16)

    col = lambda off: pl.BlockSpec((S, HEAD_DIM), lambda h: (0, off + h))
    per_head = pl.BlockSpec((1, S, LANES), lambda h: (h, 0, 0))
    row = pl.BlockSpec((1, 1, S), lambda h: (h, 0, 0))
    grad = jax.ShapeDtypeStruct((S, n_fox * HEAD_DIM), BF16)
    return pl.pallas_call(
        body, name="fox_bwd", grid=(n_fox,),
        in_specs=[col(0), col(n_fox), col(2 * n_fox), col(0), col(0), per_head, row],
        out_specs=[col(0), col(0), col(0), row, per_head],
        out_shape=[grad, grad, grad, jax.ShapeDtypeStruct((n_fox, 1, S), F32), jax.ShapeDtypeStruct((n_fox, S, LANES), F32)],
        scratch_shapes=[pltpu.VMEM((S, HEAD_DIM), F32), pltpu.VMEM((S, LANES), F32)],
        compiler_params=pltpu.CompilerParams(dimension_semantics=("parallel",),
                                             vmem_limit_bytes=_vmem(24 * S * HEAD_DIM * 4 + 16 * t * t * 4)),
    )(proj, proj, proj, o, do, lse_b, cum_row)


def _rope_tables(S):
    half = HEAD_DIM // 2
    inv_freq = 1.0 / (ROPE_THETA ** (jnp.arange(half, dtype=F32) * (2.0 / HEAD_DIM)))
    ang = jnp.arange(S).astype(F32)[:, None] * inv_freq[None, :]
    cos, sin = jnp.cos(ang), jnp.sin(ang)
    return jnp.concatenate([cos, cos], axis=-1), jnp.concatenate([-sin, sin], axis=-1)


def _rope(name, src, first_block, n_blocks, cos, sin_signed):
    S = src.shape[0]

    def body(x_ref, cos_ref, sin_ref, o_ref):
        xv = x_ref[...].astype(F32)
        o_ref[...] = (xv * cos_ref[...] + pltpu.roll(xv, HEAD_DIM // 2, 1) * sin_ref[...]).astype(BF16)

    table = pl.BlockSpec((S, HEAD_DIM), lambda n: (0, 0))
    return pl.pallas_call(
        body, name=name, grid=(n_blocks,),
        in_specs=[pl.BlockSpec((S, HEAD_DIM), lambda n: (0, first_block + n)), table, table],
        out_specs=pl.BlockSpec((S, HEAD_DIM), lambda n: (0, n)),
        out_shape=jax.ShapeDtypeStruct((S, n_blocks * HEAD_DIM), BF16),
        compiler_params=pltpu.CompilerParams(dimension_semantics=("parallel",),
                                             vmem_limit_bytes=_vmem(12 * S * HEAD_DIM * 4)),
    )(src, cos, sin_signed)


def _swa_tile(q_ref, kp_ref, kc_ref, n, group, scale):
    B = SWA_BLOCK
    qs = jnp.concatenate([q_ref[:, g * HEAD_DIM:(g + 1) * HEAD_DIM] for g in range(group)], axis=0)
    kcat = jnp.concatenate([kp_ref[...], kc_ref[...]], axis=0)
    s = lax.dot_general(qs, kcat, _NT, preferred_element_type=F32) * scale
    qi = lax.broadcasted_iota(jnp.int32, (group * B, 2 * B), 0) % B
    kj = lax.broadcasted_iota(jnp.int32, (group * B, 2 * B), 1)
    diff = qi + B - kj
    mask = (diff >= 0) & (diff < B) & ((n * B + kj - B) >= 0)
    return qs, kcat, jnp.where(mask, s, NEG)


def _swa_sink_col(sink_ref, kv, group):
    head = lax.broadcasted_iota(jnp.int32, (group * SWA_BLOCK, 1), 0) // SWA_BLOCK
    col = jnp.zeros((group * SWA_BLOCK, 1), F32)
    for g in range(group):
        col = jnp.where(head == g, sink_ref[kv * group + g], col)
    return col


def _swa_specs(n_kv, group, q_first, k_first, v_first):
    B = SWA_BLOCK
    prev = lambda n: jnp.maximum(n - 1, 0)
    return [
        pl.BlockSpec((B, group * HEAD_DIM), lambda kv, n: (n, q_first + kv)),
        pl.BlockSpec((B, HEAD_DIM), lambda kv, n: (prev(n), k_first + kv)),
        pl.BlockSpec((B, HEAD_DIM), lambda kv, n: (n, k_first + kv)),
        pl.BlockSpec((B, HEAD_DIM), lambda kv, n: (prev(n), v_first + kv)),
        pl.BlockSpec((B, HEAD_DIM), lambda kv, n: (n, v_first + kv)),
    ]


def _swa_fwd(rq, proj, v_first, sinks, n_q, n_kv):
    S = rq.shape[0]
    B = SWA_BLOCK
    group = n_q // n_kv
    scale = HEAD_DIM ** -0.5

    def body(q_ref, kp_ref, kc_ref, vp_ref, vc_ref, sink_ref, o_ref, lse_ref):
        kv, n = pl.program_id(0), pl.program_id(1)
        _, _, s = _swa_tile(q_ref, kp_ref, kc_ref, n, group, scale)
        sink = _swa_sink_col(sink_ref, kv, group)
        m = jnp.maximum(jnp.max(s, axis=-1, keepdims=True), sink)
        p = jnp.exp(s - m)
        denom = jnp.sum(p, axis=-1, keepdims=True) + jnp.exp(sink - m)
        vcat = jnp.concatenate([vp_ref[...], vc_ref[...]], axis=0)
        o = jnp.dot((p / denom).astype(BF16), vcat, preferred_element_type=F32)
        lse = m + jnp.log(denom)
        for g in range(group):
            o_ref[:, g * HEAD_DIM:(g + 1) * HEAD_DIM] = o[g * B:(g + 1) * B, :]
            lse_ref[0, :, g * LANES:(g + 1) * LANES] = jnp.broadcast_to(lse[g * B:(g + 1) * B, :], (B, LANES))

    specs = _swa_specs(n_kv, group, 0, n_q, v_first)
    q_blk = pl.BlockSpec((B, group * HEAD_DIM), lambda kv, n: (n, kv))
    return pl.pallas_call(
        body, name="swa_fwd", grid=(n_kv, S // B),
        in_specs=specs + [pl.BlockSpec(memory_space=pltpu.SMEM)],
        out_specs=[q_blk, pl.BlockSpec((1, B, group * LANES), lambda kv, n: (kv, n, 0))],
        out_shape=[jax.ShapeDtypeStruct((S, n_q * HEAD_DIM), F32), jax.ShapeDtypeStruct((n_kv, S, group * LANES), F32)],
        compiler_params=pltpu.CompilerParams(dimension_semantics=("parallel", "arbitrary")),
    )(rq, rq, rq, proj, proj, sinks)


def _swa_bwd(rq, proj, v_first, sinks, o, do, do_first, lse_b, n_q, n_kv):
    S = rq.shape[0]
    B = SWA_BLOCK
    group = n_q // n_kv
    scale = HEAD_DIM ** -0.5

    def body(q_ref, kp_ref, kc_ref, vp_ref, vc_ref, o_ref, do_ref, lse_ref, sink_ref,
             dq_ref, dk_ref, dv_ref, dsink_ref):
        kv, n = pl.program_id(0), pl.program_id(1)

        @pl.when(n == 0)
        def _():
            dk_ref[...] = jnp.zeros_like(dk_ref)
            dv_ref[...] = jnp.zeros_like(dv_ref)
            dsink_ref[...] = jnp.zeros_like(dsink_ref)

        qs, kcat, s = _swa_tile(q_ref, kp_ref, kc_ref, n, group, scale)
        sink = _swa_sink_col(sink_ref, kv, group)
        stack = lambda ref, w: jnp.concatenate([ref[:, g * w:(g + 1) * w] for g in range(group)], axis=0)
        lse = jnp.concatenate([lse_ref[0, :, g * LANES:g * LANES + 1] for g in range(group)], axis=0)
        do32 = stack(do_ref, HEAD_DIM)
        delta = jnp.sum(do32 * stack(o_ref, HEAD_DIM), axis=-1, keepdims=True)
        dov = do32.astype(BF16)
        p = jnp.exp(s - lse)
        vcat = jnp.concatenate([vp_ref[...], vc_ref[...]], axis=0)
        dp = lax.dot_general(dov, vcat, _NT, preferred_element_type=F32)
        ds = p * (dp - delta)
        dsb = ds.astype(BF16)
        dq = jnp.dot(dsb, kcat, preferred_element_type=F32) * scale
        for g in range(group):
            dq_ref[:, g * HEAD_DIM:(g + 1) * HEAD_DIM] = dq[g * B:(g + 1) * B, :].astype(BF16)
        dkcat = lax.dot_general(dsb, qs, _TN, preferred_element_type=F32) * scale
        dvcat = lax.dot_general(p.astype(BF16), dov, _TN, preferred_element_type=F32)
        prev0 = pl.multiple_of(jnp.maximum(n - 1, 0) * B, B)
        cur0 = pl.multiple_of(n * B, B)
        dk_ref[0, pl.ds(prev0, B), :] += dkcat[:B, :]
        dk_ref[0, pl.ds(cur0, B), :] += dkcat[B:, :]
        dv_ref[0, pl.ds(prev0, B), :] += dvcat[:B, :]
        dv_ref[0, pl.ds(cur0, B), :] += dvcat[B:, :]
        dsk = -jnp.exp(sink - lse) * delta
        lane = lax.broadcasted_iota(jnp.int32, (1, LANES), 1)
        row = jnp.zeros((1, LANES), F32)
        for g in range(group):
            row = row + jnp.where(lane == g, jnp.sum(dsk[g * B:(g + 1) * B, :]), 0.0)
        dsink_ref[0, 0:1, :] += row

    specs = _swa_specs(n_kv, group, 0, n_q, v_first)
    q_blk = pl.BlockSpec((B, group * HEAD_DIM), lambda kv, n: (n, kv))
    acc = pl.BlockSpec((1, S, HEAD_DIM), lambda kv, n: (kv, 0, 0))
    return pl.pallas_call(
        body, name="swa_bwd", grid=(n_kv, S // B),
        in_specs=specs + [q_blk, pl.BlockSpec((B, group * HEAD_DIM), lambda kv, n: (n, do_first + kv)),
                          pl.BlockSpec((1, B, group * LANES), lambda kv, n: (kv, n, 0)),
                          pl.BlockSpec(memory_space=pltpu.SMEM)],
        out_specs=[q_blk, acc, acc, pl.BlockSpec((1, 8, LANES), lambda kv, n: (kv, 0, 0))],
        out_shape=[jax.ShapeDtypeStruct((S, n_q * HEAD_DIM), BF16), jax.ShapeDtypeStruct((n_kv, S, HEAD_DIM), F32),
                   jax.ShapeDtypeStruct((n_kv, S, HEAD_DIM), F32), jax.ShapeDtypeStruct((n_kv, 8, LANES), F32)],
        compiler_params=pltpu.CompilerParams(dimension_semantics=("parallel", "arbitrary")),
    )(rq, rq, rq, proj, proj, o, do, lse_b, sinks)


def _adamw(w, g, m, v):
    m = ADAM_B1 * m + (1.0 - ADAM_B1) * g
    v = ADAM_B2 * v + (1.0 - ADAM_B2) * (g * g)
    m_hat = m / (1.0 - ADAM_B1 ** ADAM_STEP)
    v_hat = v / (1.0 - ADAM_B2 ** ADAM_STEP)
    delta = -ADAM_LR * (m_hat / (jnp.sqrt(v_hat) + ADAM_EPS) + ADAM_WD * w)
    return delta, m, v


def _mod_fwd(cond_in, w_mod, b_shard):
    R, D = cond_in.shape
    cols = w_mod.shape[1]
    tn = _fit(512, cols)

    def body(c_ref, w_ref, b_ref, o_ref):
        cv = c_ref[...]
        cond = (cv / (1.0 + jnp.exp(-cv))).astype(BF16)
        o_ref[...] = jnp.dot(cond, w_ref[...].astype(BF16), preferred_element_type=F32) + b_ref[...]

    return pl.pallas_call(
        body, name="mod_fwd", grid=(cols // tn,),
        in_specs=[pl.BlockSpec((R, D), lambda j: (0, 0)), pl.BlockSpec((D, tn), lambda j: (0, j)),
                  pl.BlockSpec((1, tn), lambda j: (0, j))],
        out_specs=pl.BlockSpec((R, tn), lambda j: (0, j)),
        out_shape=jax.ShapeDtypeStruct((R, cols), F32),
        compiler_params=pltpu.CompilerParams(dimension_semantics=("parallel",), vmem_limit_bytes=_vmem(3 * D * tn * 4)),
    )(cond_in, w_mod, b_shard)


def _mod_update(c_t, dmod, w, m, v):
    D, nb = c_t.shape
    cols = w.shape[1]
    tn = _fit(256, cols)

    def body(c_ref, d_ref, w_ref, m_ref, v_ref, g_ref, dl_ref, nm_ref, nv_ref):
        cv = c_ref[...]
        cond = cv / (1.0 + jnp.exp(-cv))
        g = jnp.zeros((D, tn), F32)
        for b in range(nb):
            g = g + cond[:, b:b + 1] * d_ref[b:b + 1, :]
        g_ref[...] = g
        dl_ref[...], nm_ref[...], nv_ref[...] = _adamw(w_ref[...], g, m_ref[...], v_ref[...])

    blk = pl.BlockSpec((D, tn), lambda j: (0, j))
    out = jax.ShapeDtypeStruct((D, cols), F32)
    return pl.pallas_call(
        body, name="mod_update", grid=(cols // tn,),
        in_specs=[pl.BlockSpec((D, nb), lambda j: (0, 0)), pl.BlockSpec((nb, tn), lambda j: (0, j)), blk, blk, blk],
        out_specs=[blk] * 4, out_shape=[out] * 4,
        compiler_params=pltpu.CompilerParams(dimension_semantics=("parallel",), vmem_limit_bytes=_vmem(18 * D * tn * 4)),
    )(c_t, dmod, w, m, v)


def _small_update(stacked, w, m, v):
    R, C = w.shape

    def body(s_ref, w_ref, m_ref, v_ref, g_ref, dl_ref, nm_ref, nv_ref):
        g = s_ref[0:R, :]
        for d in range(1, N_DEV):
            g = g + s_ref[d * R:(d + 1) * R, :]
        g_ref[...] = g
        dl_ref[...], nm_ref[...], nv_ref[...] = _adamw(w_ref[...], g, m_ref[...], v_ref[...])

    return pl.pallas_call(body, name="small_update", out_shape=[jax.ShapeDtypeStruct((R, C), F32)] * 4)(stacked, w, m, v)


def _place():
    return lax.axis_index("x"), lax.axis_index("y"), lax.axis_index("c")


def _allgather8(name, block):
    m_per, n = block.shape

    def body(x_ref, out_ref, token_ref, send_sems, recv_sems, local_sem):
        token_ref[...] = jnp.zeros_like(token_ref)
        x, y, c = _place()
        me, sibling = (x, y, c), (x, y, 1 - c)
        chips = [(1 - x, y), (x, 1 - y), (1 - x, 1 - y)]

        def rows(px, py, pc):
            return out_ref.at[pl.ds((4 * px + 2 * py + pc) * m_per, m_per), :]

        def copy(k, blk, to, src=None):
            return pltpu.make_async_remote_copy(
                src_ref=rows(*blk) if src is None else src, dst_ref=rows(*blk),
                send_sem=send_sems.at[k], recv_sem=recv_sems.at[k], device_id=to, device_id_type=MESH)

        mine = pltpu.make_async_copy(x_ref, rows(*me), local_sem)
        mine.start()
        first = [copy(0, me, sibling, src=x_ref)]
        first += [copy(1 + j, me, (*chip, c), src=x_ref) for j, chip in enumerate(chips)]
        for cp in first:
            cp.start()
        passed = [copy(4 + j, (*chip, c), sibling) for j, chip in enumerate(chips)]
        for j, chip in enumerate(chips):
            copy(1 + j, (*chip, c), me).wait_recv()
            passed[j].start()
        copy(0, sibling, me).wait_recv()
        for j, chip in enumerate(chips):
            copy(4 + j, (*chip, 1 - c), me).wait_recv()
        for cp in first + passed:
            cp.wait_send()
        mine.wait()

    vmem = pl.BlockSpec(memory_space=pltpu.VMEM)
    return pl.pallas_call(
        body, name=name,
        out_shape=[jax.ShapeDtypeStruct((N_DEV * m_per, n), block.dtype), jax.ShapeDtypeStruct((8, LANES), F32)],
        in_specs=[vmem], out_specs=[vmem, vmem],
        scratch_shapes=[pltpu.SemaphoreType.DMA((7,)), pltpu.SemaphoreType.DMA((7,)), pltpu.SemaphoreType.DMA],
    )(block)


_ANY = pl.BlockSpec(memory_space=pl.ANY)


def _half(ref, c, rows):
    return ref.at[pl.ds(c * (rows // 2), rows // 2), :]


_HBM = pl.BlockSpec(memory_space=pltpu.HBM)
_SEM = pl.BlockSpec(memory_space=pltpu.SEMAPHORE)
_EFFECT = pltpu.SideEffectType.DATAFLOW_SIDE_EFFECTING


def _ici_start(name, srcs, land_shapes, plan, per_source=3):
    ns, nl = len(srcs), len(land_shapes)
    n_copies = per_source * ns

    def body(*refs):
        src_refs, land_refs = refs[:ns], refs[ns:ns + nl]
        send_sems, recv_sems = refs[ns + nl], refs[ns + nl + 1]
        token = refs[-1]
        for n, (src, dst, peer, _) in enumerate(plan(src_refs, land_refs)):
            pltpu.make_async_remote_copy(src_ref=src, dst_ref=dst, send_sem=send_sems.at[n], recv_sem=recv_sems.at[n],
                                         device_id=peer, device_id_type=MESH).start()
        token[...] = jnp.zeros_like(token)

    lands = [lax.empty(s.shape, s.dtype) for s in land_shapes]
    out = pl.pallas_call(
        body, name=name,
        out_shape=(pltpu.SemaphoreType.DMA((n_copies,)), pltpu.SemaphoreType.DMA((n_copies,)),
                   *[pltpu.HBM(a.shape, a.dtype) for a in list(srcs) + lands], jax.ShapeDtypeStruct((8, LANES), F32)),
        in_specs=[_HBM] * (ns + nl),
        out_specs=(_SEM, _SEM, *[_HBM] * (ns + nl), pl.BlockSpec(memory_space=pltpu.VMEM)),
        input_output_aliases={n: 2 + n for n in range(ns + nl)},
        compiler_params=pltpu.CompilerParams(has_side_effects=_EFFECT),
    )(*[pltpu.with_memory_space_constraint(a, pltpu.HBM) for a in list(srcs) + lands])
    return out[0], out[1], list(out[2:2 + ns]), list(out[2 + ns:2 + ns + nl]), out[-1]


def _ici_wait(name, send_sems, recv_sems, srcs, lands, plan, after):
    ns, nl = len(srcs), len(lands)

    def body(*refs):
        src_refs, land_refs = refs[:ns], refs[ns:ns + nl]
        send_sems, recv_sems = refs[ns + nl], refs[ns + nl + 1]
        for n, (src, _, peer, mine) in enumerate(plan(src_refs, land_refs)):
            cp = pltpu.make_async_remote_copy(src_ref=src, dst_ref=mine, send_sem=send_sems.at[n],
                                              recv_sem=recv_sems.at[n], device_id=peer, device_id_type=MESH)
            cp.wait_send()
            cp.wait_recv()

    out = pl.pallas_call(
        body, name=name, out_shape=[pltpu.HBM(a.shape, a.dtype) for a in list(srcs) + list(lands)],
        in_specs=[_HBM] * (ns + nl) + [_SEM, _SEM, _ANY], out_specs=[_HBM] * (ns + nl),
        input_output_aliases={n: n for n in range(ns + nl)},
        compiler_params=pltpu.CompilerParams(has_side_effects=_EFFECT),
    )(*srcs, *lands, send_sems, recv_sems, after)
    return list(out[:ns]), list(out[ns:])


def _gather_plan(src_refs, land_refs):
    x, y, c = _place()
    copies = []
    for w, land in zip(src_refs, land_refs):
        R = w.shape[0]
        for cx, cy in [(1 - x, y), (x, 1 - y), (1 - x, 1 - y)]:
            copies.append((_half(w, c, R), _half(land.at[2 * x + y], c, R), (cx, cy, c),
                           _half(land.at[2 * cx + cy], c, R)))
    return copies


def _pass_plan(src_refs, land_refs):
    x, y, c = _place()
    copies = []
    for land in src_refs:
        R = land.shape[1]
        for cx, cy in [(1 - x, y), (x, 1 - y), (1 - x, 1 - y)]:
            slot = land.at[2 * cx + cy]
            copies.append((_half(slot, c, R), _half(slot, c, R), (x, y, 1 - c), _half(slot, 1 - c, R)))
    return copies


def _pair_plan(src_refs, land_refs):
    x, y, c = _place()
    copies = []
    for g, land in zip(src_refs, land_refs):
        half = g.shape[1] // 2
        copies.append((g.at[:, pl.ds((1 - c) * half, half), :], land, (x, y, 1 - c), land))
    return copies


def _share_plan(src_refs, land_refs):
    x, y, c = _place()
    return [(h, land, (x, y, 1 - c), land) for h, land in zip(src_refs, land_refs)]


def _pass_to_sibling(name, lands):
    nw = len(lands)

    def body(*refs):
        ins, outs = refs[:nw], refs[nw:2 * nw]
        send_sems, recv_sems = refs[2 * nw:]
        x, y, c = _place()
        chips = [(1 - x, y), (x, 1 - y), (1 - x, 1 - y)]
        copies = []
        for k in range(nw):
            R = ins[k].shape[1]
            for j, (cx, cy) in enumerate(chips):
                cp = pltpu.make_async_remote_copy(
                    src_ref=_half(ins[k].at[2 * cx + cy], c, R), dst_ref=_half(outs[k].at[2 * cx + cy], c, R),
                    send_sem=send_sems.at[3 * k + j], recv_sem=recv_sems.at[3 * k + j],
                    device_id=(x, y, 1 - c), device_id_type=MESH)
                cp.start()
                copies.append(cp)
        for k in range(nw):
            R = ins[k].shape[1]
            for j, (cx, cy) in enumerate(chips):
                pltpu.make_async_remote_copy(
                    src_ref=_half(ins[k].at[2 * cx + cy], c, R), dst_ref=_half(outs[k].at[2 * cx + cy], 1 - c, R),
                    send_sem=send_sems.at[3 * k + j], recv_sem=recv_sems.at[3 * k + j],
                    device_id=(x, y, 1 - c), device_id_type=MESH).wait_recv()
        for cp in copies:
            cp.wait_send()

    return pl.pallas_call(
        body, name=name, out_shape=[jax.ShapeDtypeStruct(a.shape, a.dtype) for a in lands],
        in_specs=[_ANY] * nw, out_specs=[_ANY] * nw, input_output_aliases={k: k for k in range(nw)},
        scratch_shapes=[pltpu.SemaphoreType.DMA((3 * nw,)), pltpu.SemaphoreType.DMA((3 * nw,))],
    )(*lands)


def _tie(vec, token):
    return vec + token[0:1, 0:1]


ROW_ALIGN = 16
TILE_ELEMS = 512 * 1024


def _tiles(rows, cols):
    fits = [t for t in range(ROW_ALIGN, min(rows, 256) + 1, ROW_ALIGN) if rows % t == 0]
    tr = fits[-1] if fits and fits[-1] >= 64 else rows
    tc = cols
    while tr * tc > TILE_ELEMS and tc % (2 * LANES) == 0:
        tc //= 2
    return tr, tc


def _pair_add(name, core, grad, recv):
    n, R, C = grad.shape
    half = R // 2
    tr, tc = _tiles(half, C)
    nr = half // tr

    def body(core_ref, g_ref, r_ref, o_ref):
        o_ref[...] = (g_ref[...].astype(F32) + r_ref[...].astype(F32)).astype(BF16)

    grid_spec = pltpu.PrefetchScalarGridSpec(
        num_scalar_prefetch=1, grid=(n, nr, C // tc),
        in_specs=[pl.BlockSpec((1, tr, tc), lambda s, r, q, core_ref: (s, core_ref[0] * nr + r, q)),
                  pl.BlockSpec((1, tr, tc), lambda s, r, q, core_ref: (s, r, q))],
        out_specs=pl.BlockSpec((1, tr, tc), lambda s, r, q, core_ref: (s, r, q)))
    return pl.pallas_call(
        body, name=name, grid_spec=grid_spec, out_shape=jax.ShapeDtypeStruct((n, half, C), BF16),
        compiler_params=pltpu.CompilerParams(dimension_semantics=("parallel", "parallel", "parallel")),
    )(core, grad, recv)


def _scatter_plan(src_refs, land_refs):
    x, y, c = _place()
    copies = []
    for p, land in zip(src_refs, land_refs):
        for j, (cx, cy) in enumerate([(1 - x, y), (x, 1 - y), (1 - x, 1 - y)]):
            copies.append((p.at[2 * cx + cy], land.at[j], (cx, cy, c), land.at[j]))
    return copies


def _chip_add(name, chip, sums, recv):
    _, H, C = sums.shape
    tr, tc = _tiles(H, C)

    def body(chip_ref, p_ref, r_ref, o_ref):
        total = p_ref[0].astype(F32)
        for j in range(3):
            total = total + r_ref[j].astype(F32)
        o_ref[...] = total

    grid_spec = pltpu.PrefetchScalarGridSpec(
        num_scalar_prefetch=1, grid=(H // tr, C // tc),
        in_specs=[pl.BlockSpec((1, tr, tc), lambda r, q, chip_ref: (chip_ref[0], r, q)),
                  pl.BlockSpec((3, tr, tc), lambda r, q, chip_ref: (0, r, q))],
        out_specs=pl.BlockSpec((tr, tc), lambda r, q, chip_ref: (r, q)))
    return pl.pallas_call(
        body, name=name, grid_spec=grid_spec, out_shape=jax.ShapeDtypeStruct((H, C), F32),
        compiler_params=pltpu.CompilerParams(dimension_semantics=("parallel", "parallel")),
    )(chip, sums, recv)


def _pair_share(name, halves):
    nw = len(halves)

    def body(*refs):
        hs, outs = refs[:nw], refs[nw:2 * nw]
        send_sems, recv_sems = refs[2 * nw:]
        x, y, c = _place()
        copies = []
        for k in range(nw):
            cp = pltpu.make_async_remote_copy(
                src_ref=hs[k], dst_ref=outs[k], send_sem=send_sems.at[k], recv_sem=recv_sems.at[k],
                device_id=(x, y, 1 - c), device_id_type=MESH)
            cp.start()
            copies.append(cp)
        for cp in copies:
            cp.wait()

    return pl.pallas_call(
        body, name=name,
        out_shape=[jax.ShapeDtypeStruct(h.shape, h.dtype) for h in halves],
        in_specs=[_ANY] * nw, out_specs=[_ANY] * nw,
        scratch_shapes=[pltpu.SemaphoreType.DMA((nw,)), pltpu.SemaphoreType.DMA((nw,))],
    )(*halves)


def _adam_halves(name, core, w, g_own, g_other, m, v):
    R, C = w.shape
    H = R // 2
    tr, tc = _tiles(H, C)
    nr, nc = H // tr, C // tc

    def body(core_ref, w_ref, go_ref, gr_ref, m_ref, v_ref, g_ref, dl_ref, nm_ref, nv_ref):
        own = (pl.program_id(0) // nr) == core_ref[0]
        g = jnp.where(own, go_ref[...], gr_ref[...])
        g_ref[...] = g
        dl_ref[...], nm_ref[...], nv_ref[...] = _adamw(w_ref[...], g, m_ref[...], v_ref[...])

    blk = pl.BlockSpec((tr, tc), lambda r, q, core_ref: (r, q))

    def half_spec(is_own):
        def index(r, q, core_ref):
            mine = ((r // nr) == core_ref[0]) == is_own
            done = is_own == (core_ref[0] == 0)
            return (jnp.where(mine, r % nr, jnp.where(done, nr - 1, 0)), jnp.where(mine, q, jnp.where(done, nc - 1, 0)))
        return pl.BlockSpec((tr, tc), index)
    out = jax.ShapeDtypeStruct((R, C), F32)
    grid_spec = pltpu.PrefetchScalarGridSpec(
        num_scalar_prefetch=1, grid=(R // tr, nc), in_specs=[blk, half_spec(True), half_spec(False), blk, blk],
        out_specs=[blk] * 4)
    return pl.pallas_call(
        body, name=name, grid_spec=grid_spec, out_shape=[out] * 4,
        compiler_params=pltpu.CompilerParams(dimension_semantics=("parallel", "parallel"),
                                             vmem_limit_bytes=_vmem(20 * tr * tc * 4)),
    )(core, w, g_own, g_other, m, v)


def kernel(x, c, w_mod, b_mod, g_pre_mix, g_post_mix, w_in, b_forget, swa_sinks, w_out, g_pre_mlp, g_post_mlp, w_up, w_down, loss_target, m_w_mod, m_b_mod, m_g_pre_mix, m_g_post_mix, m_w_in, m_b_forget, m_swa_sinks, m_w_out, m_g_pre_mlp, m_g_post_mlp, m_w_up, m_w_down, v_w_mod, v_b_mod, v_g_pre_mix, v_g_post_mix, v_w_in, v_b_forget, v_swa_sinks, v_w_out, v_g_pre_mlp, v_g_post_mlp, v_w_up, v_w_down):
    S, D = x.shape[1], x.shape[2]
    n_heads = D // HEAD_DIM
    n_fox = n_heads // 2
    n_swa = n_heads - n_fox
    n_kv = max(1, n_swa // 4)
    fox_w, swa_w, kv_w = n_fox * HEAD_DIM, n_swa * HEAD_DIM, n_kv * HEAD_DIM
    main_w = 3 * fox_w + swa_w + 2 * kv_w
    in_w = main_w + n_fox
    mod_cols = w_mod.shape[2]

    ax, ay, ac = _place()
    chip = 2 * ax + ay
    dev = 2 * chip + ac
    chip_arr = jnp.reshape(chip, (1,)).astype(jnp.int32)
    core_arr = jnp.reshape(ac, (1,)).astype(jnp.int32)

    x2, tgt = x[0], loss_target[0]

    c_all, _ = _allgather8("gather_c", c.reshape(8, D // 8))
    c_all = c_all.reshape(N_DEV, D)
    b_shard = lax.dynamic_slice_in_dim(b_mod, chip * mod_cols, mod_cols, axis=1)
    mod_shard = _mod_fwd(jnp.pad(c_all, ((0, 16 - N_DEV), (0, 0))), w_mod[0], b_shard)[:N_DEV]
    mod_all, token = _allgather8("gather_mod", mod_shard)
    mod_all = mod_all.reshape(N_CHIPS, 2, N_DEV, mod_cols)[:, 0]
    mod = lax.dynamic_index_in_dim(mod_all, dev, axis=1, keepdims=False).reshape(N_MOD, 1, D)
    sh_a, sc_a, gt_a, sh_m, sc_m, gt_m = [mod[n] for n in range(N_MOD)]

    in_rows = in_w // N_CHIPS
    in_rows_pad = -(-in_rows // LANES) * LANES
    slab_w = N_CHIPS * in_rows_pad

    def rows_of(a):
        return jnp.pad(a[0].T, ((0, in_rows_pad - in_rows), (0, 0)))

    def slab_cols(lo, hi):
        spans = []
        while lo < hi:
            s, r = divmod(lo, in_rows)
            n = min(hi - lo, in_rows - r)
            spans.append((s * in_rows_pad + r, s * in_rows_pad + r + n))
            lo += n
        return spans

    gate_lo = 3 * fox_w
    main_spans = slab_cols(0, gate_lo) + slab_cols(gate_lo + n_fox, in_w)
    (gate_first, gate_last), = slab_cols(gate_lo, gate_lo + n_fox)

    names = ["w_in", "w_out", "w_up", "w_down"]
    flights = {}
    for n, w in zip(names, [rows_of(w_in), w_out[0], w_up[0], w_down[0]]):
        shard = _tie(w, token).astype(BF16)
        flights[n] = _ici_start("gather_start_" + n, [shard], [jax.ShapeDtypeStruct((N_CHIPS,) + shard.shape, BF16)],
                                _gather_plan)
        token = flights[n][4]
    sc_a = _tie(sc_a, token)

    def arrived(n, after):
        send, recv, srcs, lands, _ = flights[n]
        srcs, lands = _ici_wait("gather_wait_" + n, send, recv, srcs, lands, _gather_plan, after)
        return srcs[0], _ici_start("gather_pass_start_" + n, lands, [], _pass_plan)

    def gathered(n, after, in_flight=None):
        if in_flight is None:
            send, recv, srcs, lands, _ = flights[n]
            srcs, lands = _ici_wait("gather_wait_" + n, send, recv, srcs, lands, _gather_plan, after)
            own, stack = srcs[0], _pass_to_sibling("gather_pass_" + n, lands)[0]
        else:
            own, (send, recv, lands, _, _) = in_flight
            stack = _ici_wait("gather_pass_wait_" + n, send, recv, lands, [], _pass_plan, after)[0][0]
        return lax.dynamic_update_index_in_dim(stack, own, chip, 0)

    d_ff = N_CHIPS * w_up.shape[2]

    h = _pre_norm(x2, g_pre_mix, sc_a, sh_a)
    in_state = [rows_of(a) for a in (w_in, m_w_in, v_w_in)]
    cos, sin_signed = _rope_tables(S)
    ready = h[:8, :LANES].astype(F32) + cos[:8] + sum(a[:8, :LANES] for a in in_state)
    w_slab_t = gathered("w_in", ready).reshape(slab_w, D)
    tm_p, tn_p = _fit(MM_TM, S), _fit(512, slab_w)
    win0 = gate_first // LANES * LANES
    win_j, win_off = divmod(win0, tn_p)
    assert win_off + 2 * LANES <= tn_p and gate_last - win0 <= 2 * LANES

    def proj_epilogue(acc, ex, outs):
        outs[0][...] = acc.astype(BF16)

        @pl.when(pl.program_id(1) == win_j)
        def _():
            outs[1][...] = acc[:, win_off:win_off + 2 * LANES]

    proj_slab, gate_win = _matmul(
        "in_proj", h, w_slab_t, "nt",
        [((S, slab_w), BF16, (tm_p, tn_p), lambda i, j: (i, j)), ((S, 2 * LANES), F32, (tm_p, 2 * LANES), lambda i, j: (i, 0))],
        proj_epilogue, tn=tn_p, revisits=True)
    proj = jnp.concatenate([proj_slab[:, lo:hi] for lo, hi in main_spans], axis=1)
    out_flight = arrived("w_out", proj_slab)
    fg = _tie(jnp.pad(gate_win[:, gate_first - win0:gate_last - win0], ((0, 0), (0, LANES - n_fox))), out_flight[1][4])
    b_pad = jnp.pad(b_forget, ((0, 0), (0, LANES - n_fox)))
    cum_row = _fox_gate_fwd(fg, b_pad)[:n_fox].reshape(n_fox, 1, S)
    fox_o, fox_lse = _fox_fwd(proj, cum_row, n_fox)

    rq = _rope("rope_fwd", proj, 3 * n_fox, n_swa + n_kv, cos, sin_signed)
    v_first = 3 * n_fox + n_swa + n_kv
    sinks = swa_sinks[0]
    swa_o, swa_lse = _swa_fwd(rq, proj, v_first, sinks, n_swa, n_kv)

    mixcat = jnp.concatenate([fox_o, swa_o], axis=1).astype(BF16)
    up_flight = arrived("w_up", mixcat)
    w_out_f = gathered("w_out", mixcat, out_flight).reshape(D, D)
    mix = _mm_plain("out_proj", mixcat, w_out_f, "nn", F32, after=up_flight[1][4])
    x1, h2 = _post_mix(x2, mix, g_post_mix, gt_a, g_pre_mlp, sc_m, sh_m)
    w_up_f = jnp.transpose(gathered("w_up", h2, up_flight), (1, 0, 2)).reshape(D, d_ff)

    tm_u, tn_u = _fit(MM_TM, S), _fit(MM_TN, d_ff)

    def up_epilogue(acc, ex, outs):
        outs[0][...] = acc.astype(BF16)
        r = jnp.maximum(acc, 0.0)
        outs[1][...] = (r * r).astype(BF16)

    ublk = ((S, d_ff), BF16, (tm_u, tn_u), lambda i, j: (i, j))
    u, a = _matmul("mlp_up", h2, w_up_f, "nn", [ublk, ublk], up_epilogue)
    w_down_f = gathered("w_down", a).reshape(d_ff, D)
    y = _mm_plain("mlp_down", a, w_down_f, "nn", F32)

    dy, dout, loss_part, acc_mlp_post = _loss_and_post_mlp_bwd(x1, y, tgt, g_post_mlp, gt_m)

    def du_epilogue(acc, ex, outs):
        outs[0][...] = (acc * (2.0 * jnp.maximum(ex[0][...].astype(F32), 0.0))).astype(BF16)

    du = _matmul("mlp_down_bwd", dy, w_down_f, "nt", [ublk], du_epilogue,
                 extras=[(u, (tm_u, tn_u), lambda i, j: (i, j))])[0]
    def pair_start(tag, fulls):
        return _ici_start("grad_pair_start_" + tag, fulls,
                          [jax.ShapeDtypeStruct((N_CHIPS, g.shape[1] // 2, g.shape[2]), BF16) for g in fulls],
                          _pair_plan, per_source=1)

    def scatter_start(tag, pair_flights, after):
        sums = []
        for k, (send, recv, fulls, lands, _) in enumerate(pair_flights):
            fulls, from_sibling = _ici_wait("grad_pair_wait_%s_%d" % (tag, k), send, recv, fulls, lands, _pair_plan, after)
            sums += [_pair_add("pair_add_%s_%d_%d" % (tag, k, n), core_arr, g, r)
                     for n, (g, r) in enumerate(zip(fulls, from_sibling))]
        return _ici_start("grad_scatter_start_" + tag, sums,
                          [jax.ShapeDtypeStruct((3,) + p.shape[1:], BF16) for p in sums], _scatter_plan)

    def scatter_finish(tag, flight, after):
        send, recv, srcs, lands, _ = flight
        sums, received = _ici_wait("grad_scatter_wait_" + tag, send, recv, srcs, lands, _scatter_plan, after)
        return [_chip_add("chip_add_%s_%d" % (tag, k), chip_arr, p, r) for k, (p, r) in enumerate(zip(sums, received))]

    g_down = _mm_plain("grad_w_down", a, dy, "tn", BF16)
    pair_down = pair_start("down", [g_down.reshape(N_CHIPS, d_ff // N_CHIPS, D)])
    tn_s = _fit(MM_TN, w_up.shape[2])
    per = w_up.shape[2] // tn_s

    def shard_epilogue(acc, ex, outs):
        outs[0][0] = acc.astype(BF16)

    g_up = _matmul("grad_w_up", h2, du, "tn",
                   [((N_CHIPS, D, w_up.shape[2]), BF16, (1, _fit(MM_TM, D), tn_s), lambda i, j: (j // per, i, j % per))],
                   shard_epilogue, extras=[_behind(pair_down[4])], tn=tn_s)[0]
    pair_up = pair_start("up", [g_up])
    dh2 = _mm_plain("mlp_up_bwd", du, w_up_f, "nt", F32, after=pair_up[4])
    flight_mlp = scatter_start("mlp", [pair_up, pair_down], dh2)
    dx1, dmix, acc_mid = _pre_mlp_and_post_mix_bwd(dh2, x1, dout, mix, _tie(g_pre_mlp, flight_mlp[4]), sc_m,
                                                   g_post_mix, gt_a)

    dmixcat = _mm_plain("out_proj_bwd", dmix, w_out_f, "nt", F32)
    g_out = _mm_plain("grad_w_out", mixcat, dmix, "tn", BF16)

    fdq, fdk, fdv, dcum_row, dcum_q = _fox_bwd(proj, fox_o, dmixcat, fox_lse, cum_row, n_fox)
    dcum_k = jnp.pad(dcum_row.reshape(n_fox, S), ((0, LANES - n_fox), (0, 0)))
    dfg, db_forget = _fox_gate_bwd(dcum_k, dcum_q, fg, b_pad)

    group_w = (n_swa // n_kv) * HEAD_DIM
    sdq, sdk, sdv, dsink = _swa_bwd(rq, proj, v_first, sinks, swa_o, dmixcat, fox_w // group_w, swa_lse, n_swa, n_kv)
    drq = jnp.concatenate([sdq, jnp.transpose(sdk, (1, 0, 2)).reshape(S, kv_w).astype(BF16)], axis=1)
    d_sq_sk = _rope("rope_bwd", drq, 0, n_swa + n_kv, cos, -sin_signed)
    dsv = jnp.transpose(sdv, (1, 0, 2)).reshape(S, kv_w).astype(BF16)
    dproj = jnp.concatenate([fdq, fdk, fdv, d_sq_sk, dsv], axis=1)

    pieces = []
    for s in range(N_CHIPS):
        lo, hi = s * in_rows, (s + 1) * in_rows
        for src, first, last, shift in [(dproj, 0, gate_lo, 0), (dfg, gate_lo, gate_lo + n_fox, gate_lo),
                                        (dproj, gate_lo + n_fox, in_w, n_fox)]:
            if max(lo, first) < min(hi, last):
                pieces.append(src[:, max(lo, first) - shift:min(hi, last) - shift])
        pieces.append(jnp.zeros((S, in_rows_pad - in_rows), BF16))
    dproj_slab = jnp.concatenate(pieces, axis=1)

    g_in_t = _mm_plain("grad_w_in", dproj_slab, h, "tn", BF16, tm=_fit(512, slab_w))
    pair_mix = pair_start("mix", [g_in_t.reshape(N_CHIPS, in_rows_pad, D), g_out.reshape(N_CHIPS, D // N_CHIPS, D)])
    dh = _mm_plain("in_proj_bwd", dproj_slab, w_slab_t, "nn", F32, after=pair_mix[4], tk=_fit(2560, slab_w))
    grad_x, acc_pre = _pre_mix_bwd(dh, x2, dx1, g_pre_mix, sc_a)

    zero_row = jnp.zeros((1, D), F32)
    tail = jnp.concatenate([db_forget[0:1, :n_fox], dsink[:, 0, :n_swa // n_kv].reshape(1, n_swa),
                            loss_part[0:1, 0:1], jnp.zeros((1, D - n_fox - n_swa - 1), F32)], axis=1)
    partial = jnp.concatenate([
        acc_pre[0:1], acc_pre[1:2], acc_mid[3:4], acc_mid[0:1], acc_mid[1:2], acc_mlp_post[0:1],
        acc_pre[2:3], acc_mid[4:5], acc_mid[2:3], acc_mlp_post[1:2], tail] + [zero_row] * 5, axis=0)
    gathered_small, token = _allgather8("gather_small_grads", partial)

    flight_mix = scatter_start("mix", [pair_mix], token)
    halves_mlp = scatter_finish("mlp", flight_mlp, flight_mix[4])
    share_mlp = _ici_start("grad_share_start_mlp", halves_mlp,
                           [jax.ShapeDtypeStruct(hv.shape, F32) for hv in halves_mlp], _share_plan, per_source=1)

    def pack(bm, gpm, gqm, gpl, gql, bf, sk):
        last = jnp.concatenate([bf, sk, jnp.zeros((1, D - n_fox - n_swa), F32)], axis=1)
        return jnp.concatenate([bm.reshape(N_MOD, D), gpm, gqm, gpl, gql, last, jnp.zeros((5, D), F32)], axis=0)

    def unpack(p):
        return {"b_mod": p[0:N_MOD].reshape(1, N_MOD * D), "g_pre_mix": p[6:7], "g_post_mix": p[7:8],
                "g_pre_mlp": p[8:9], "g_post_mlp": p[9:10], "b_forget": p[10:11, :n_fox],
                "swa_sinks": p[10:11, n_fox:n_fox + n_swa]}

    small_out = _small_update(
        gathered_small, _tie(pack(b_mod, g_pre_mix, g_post_mix, g_pre_mlp, g_post_mlp, b_forget, swa_sinks), share_mlp[4]),
        pack(m_b_mod, m_g_pre_mix, m_g_post_mix, m_g_pre_mlp, m_g_post_mlp, m_b_forget, m_swa_sinks),
        pack(v_b_mod, v_g_pre_mix, v_g_post_mix, v_g_pre_mlp, v_g_post_mlp, v_b_forget, v_swa_sinks))
    g_small, d_small, m_small, v_small = [unpack(p) for p in small_out]
    loss = small_out[0][N_MOD + 4, n_fox + n_swa]

    dmod_all = gathered_small.reshape(N_DEV, 16, D)[:, :N_MOD].reshape(N_DEV, N_MOD * D)
    dmod_shard = _tie(lax.dynamic_slice_in_dim(dmod_all, chip * mod_cols, mod_cols, axis=1), share_mlp[4])
    g_w_mod, d_w_mod, nm_w_mod, nv_w_mod = _mod_update(c_all.T, dmod_shard, w_mod[0], m_w_mod[0], v_w_mod[0])
    send, recv, halves_mlp, lands, _ = share_mlp
    halves_mlp, others_mlp = _ici_wait("grad_share_wait_mlp", send, recv, halves_mlp, lands, _share_plan,
                                       d_w_mod[:8, :LANES] + small_out[1][:8, :LANES])

    grads = dict(g_small, w_mod=g_w_mod[None])
    deltas = dict(d_small, w_mod=d_w_mod[None])
    new_m = dict(m_small, w_mod=nm_w_mod[None])
    new_v = dict(v_small, w_mod=nv_w_mod[None])
    weights = {"w_in": (w_in, m_w_in, v_w_in), "w_out": (w_out, m_w_out, v_w_out), "w_up": (w_up, m_w_up, v_w_up),
               "w_down": (w_down, m_w_down, v_w_down)}

    def big_update(n, own, other):
        transposed = n == "w_in"
        w, m, v = in_state if transposed else [a[0] for a in weights[n]]
        outs = _adam_halves("adam_" + n, core_arr, w, own, other, m, v)
        if transposed:
            outs = [o[:in_rows].T for o in outs]
        grads[n], deltas[n], new_m[n], new_v[n] = [o[None] for o in outs]

    big_update("w_up", halves_mlp[0], others_mlp[0])
    big_update("w_down", halves_mlp[1], others_mlp[1])
    ran = deltas["w_down"][0, :8, :LANES] + deltas["w_up"][0, :8, :LANES] + d_w_mod[:8, :LANES]
    halves_mix = scatter_finish("mix", flight_mix, ran)
    others_mix = _pair_share("grad_pair_share_mix", halves_mix)
    big_update("w_in", halves_mix[0], others_mix[0])
    big_update("w_out", halves_mix[1], others_mix[1])

    order = ["w_mod", "b_mod", "g_pre_mix", "g_post_mix", "w_in", "b_forget", "swa_sinks", "w_out", "g_pre_mlp",
             "g_post_mlp", "w_up", "w_down"]
    return (loss, grad_x[None], *[grads[n] for n in order], *[deltas[n] for n in order],
            *[new_m[n] for n in order], *[new_v[n] for n in order])
```

```python
import jax
import jax.numpy as jnp
from jax import lax
from jax.experimental import pallas as pl
from jax.experimental.pallas import tpu as pltpu

F32 = jnp.float32
BF16 = jnp.bfloat16
MESH = pl.DeviceIdType.MESH

HEAD_DIM = 128
SWA_BLOCK = 128
ROPE_THETA = 10000.0
NORM_EPS = 1e-6
NEG = -1e30
N_MOD = 6
ADAM_LR = 0.001
ADAM_B1 = 0.9
ADAM_B2 = 0.999
ADAM_EPS = 1e-08
ADAM_WD = 0.01
ADAM_STEP = 10
N_CHIPS = 4
N_DEV = 8
LANES = 128
VMEM_CAP = 60 * 1024 * 1024

_NN = (((1,), (0,)), ((), ()))
_NT = (((1,), (1,)), ((), ()))
_TN = (((0,), (0,)), ((), ()))


def _vmem(nbytes):
    assert nbytes * 5 // 4 <= VMEM_CAP, nbytes
    return VMEM_CAP


def _nbytes(shape, dtype):
    n = 1
    for s in shape:
        n *= s
    return n * jnp.dtype(dtype).itemsize


def _fit(t, n):
    t = min(t, n)
    assert n % t == 0, (t, n)
    return t


MM_TM, MM_TN, MM_TK = 512, 1024, 2048


def _matmul(name, a, b, mode, out_defs, epilogue, extras=(), tm=MM_TM, tn=MM_TN, tk=MM_TK, revisits=False):
    if mode == "nn":
        (M, K), (K2, N) = a.shape, b.shape
    elif mode == "nt":
        (M, K), (N, K2) = a.shape, b.shape
    else:
        (K, M), (K2, N) = a.shape, b.shape
    assert K == K2, (a.shape, b.shape, mode)
    tm, tn, tk = _fit(tm, M), _fit(tn, N), _fit(tk, K)
    nk = K // tk
    dims = {"nn": _NN, "nt": _NT, "tn": _TN}[mode]
    a_spec = (pl.BlockSpec((tk, tm), lambda i, j, k: (k, i)) if mode == "tn"
              else pl.BlockSpec((tm, tk), lambda i, j, k: (i, k)))
    b_spec = (pl.BlockSpec((tn, tk), lambda i, j, k: (j, k)) if mode == "nt"
              else pl.BlockSpec((tk, tn), lambda i, j, k: (k, j)))
    n_ex, n_out = len(extras), len(out_defs)

    def body(*refs):
        a_ref, b_ref = refs[0], refs[1]
        ex = refs[2:2 + n_ex]
        outs = refs[2 + n_ex:2 + n_ex + n_out]
        prod = lax.dot_general(a_ref[...], b_ref[...], dims, preferred_element_type=F32)
        if nk == 1:
            epilogue(prod, ex, outs)
        else:
            acc_ref = refs[-1]
            k = pl.program_id(2)

            @pl.when(k == 0)
            def _():
                acc_ref[...] = prod

            @pl.when(k > 0)
            def _():
                acc_ref[...] += prod

            @pl.when(k == nk - 1)
            def _():
                epilogue(acc_ref[...], ex, outs)

    def wrap(f):
        return lambda i, j, k: f(i, j)

    in_specs = [a_spec, b_spec] + [pl.BlockSpec(blk, wrap(f)) for _, blk, f in extras]
    out_specs = [pl.BlockSpec(blk, wrap(f)) for _, _, blk, f in out_defs]
    out_shape = [jax.ShapeDtypeStruct(s, d) for s, d, _, _ in out_defs]
    need = 2 * (tm * tk + tk * tn) * a.dtype.itemsize + 3 * tm * tn * 4
    need += sum(2 * _nbytes(blk, arr.dtype) for arr, blk, _ in extras)
    need += sum(2 * _nbytes(blk, d) for _, d, blk, _ in out_defs)
    res = pl.pallas_call(
        body, name=name, grid=(M // tm, N // tn, nk),
        in_specs=in_specs, out_specs=out_specs, out_shape=out_shape,
        scratch_shapes=[pltpu.VMEM((tm, tn), F32)] if nk > 1 else [],
        compiler_params=pltpu.CompilerParams(
            dimension_semantics=("parallel", "arbitrary" if revisits else "parallel", "arbitrary"),
            vmem_limit_bytes=_vmem(need)),
    )(a, b, *[arr for arr, _, _ in extras])
    return res


def _behind(token):
    return (token, (8, LANES), lambda i, j: (0, 0))


def _mm_plain(name, a, b, mode, out_dtype, after=None, **tiles):
    if mode == "nn":
        M, N = a.shape[0], b.shape[1]
    elif mode == "nt":
        M, N = a.shape[0], b.shape[0]
    else:
        M, N = a.shape[1], b.shape[1]
    tm, tn = _fit(tiles.get("tm", MM_TM), M), _fit(tiles.get("tn", MM_TN), N)

    def epi(acc, ex, outs):
        outs[0][...] = acc.astype(out_dtype)

    return _matmul(name, a, b, mode, [((M, N), out_dtype, (tm, tn), lambda i, j: (i, j))], epi,
                   extras=[] if after is None else [_behind(after)], **tiles)[0]


def _rstd(v):
    return lax.rsqrt(jnp.mean(v * v, axis=-1, keepdims=True) + NORM_EPS)


def _row_call(name, body, row_ins, vec_ins, row_outs, acc_outs, S, D, tr):
    tr = _fit(tr, S)
    row_spec = pl.BlockSpec((tr, D), lambda r: (r, 0))
    vec_spec = pl.BlockSpec((1, D), lambda r: (0, 0))
    in_specs = [row_spec] * len(row_ins) + [vec_spec] * len(vec_ins)
    out_specs = [row_spec] * len(row_outs) + [pl.BlockSpec(shp, lambda r: (0, 0)) for shp in acc_outs]
    out_shape = [jax.ShapeDtypeStruct((S, D), d) for d in row_outs] + [jax.ShapeDtypeStruct(shp, F32) for shp in acc_outs]
    need = sum(2 * tr * D * a.dtype.itemsize for a in row_ins) + sum(2 * tr * D * jnp.dtype(d).itemsize for d in row_outs)
    need += 8 * tr * D * 4
    return pl.pallas_call(
        body, name=name, grid=(S // tr,), in_specs=in_specs, out_specs=out_specs, out_shape=out_shape,
        compiler_params=pltpu.CompilerParams(dimension_semantics=("arbitrary",), vmem_limit_bytes=_vmem(need)),
    )(*row_ins, *vec_ins)


def _acc_rows(ref, rows):
    @pl.when(pl.program_id(0) == 0)
    def _():
        ref[...] = jnp.zeros_like(ref)
    for n, r in enumerate(rows):
        ref[n:n + 1, :] += r


def _pre_norm(x, g, sc, sh):
    S, D = x.shape

    def body(x_ref, g_ref, sc_ref, sh_ref, h_ref):
        xv = x_ref[...]
        xn = xv * _rstd(xv)
        h_ref[...] = (xn * g_ref[...] * (1.0 + sc_ref[...]) + sh_ref[...]).astype(BF16)

    return _row_call("pre_norm_mix", body, [x], [g, sc, sh], [BF16], [], S, D, 256)[0]


def _post_mix(x, mix, g_post, gt, g_pre, sc, sh):
    S, D = x.shape

    def body(x_ref, mix_ref, gp_ref, gt_ref, g2_ref, sc_ref, sh_ref, x1_ref, h2_ref):
        mv = mix_ref[...]
        x1 = x_ref[...] + gt_ref[...] * (mv * _rstd(mv) * gp_ref[...])
        x1_ref[...] = x1
        h2_ref[...] = (x1 * _rstd(x1) * g2_ref[...] * (1.0 + sc_ref[...]) + sh_ref[...]).astype(BF16)

    return _row_call("post_mix_pre_mlp", body, [x, mix], [g_post, gt, g_pre, sc, sh], [F32, BF16], [], S, D, 256)


def _loss_and_post_mlp_bwd(x1, y, target, g_post, gt):
    S, D = x1.shape

    def body(x1_ref, y_ref, t_ref, g_ref, gt_ref, dy_ref, dout_ref, loss_ref, acc_ref):
        yv = y_ref[...]
        r = _rstd(yv)
        yh = yv * r
        n = yh * g_ref[...]
        diff = x1_ref[...] + gt_ref[...] * n - t_ref[...]
        dout = diff * (1.0 / D)
        dout_ref[...] = dout
        dn = dout * gt_ref[...]
        dyh = dn * g_ref[...]
        dy_ref[...] = (r * (dyh - yh * jnp.mean(dyh * yh, axis=-1, keepdims=True))).astype(BF16)
        _acc_rows(acc_ref, [jnp.sum(dout * n, axis=0, keepdims=True), jnp.sum(dn * yh, axis=0, keepdims=True)])

        @pl.when(pl.program_id(0) == 0)
        def _():
            loss_ref[...] = jnp.zeros_like(loss_ref)
        loss_ref[...] += jnp.full(loss_ref.shape, (0.5 / D) * jnp.sum(diff * diff), F32)

    return _row_call("loss_post_mlp_bwd", body, [x1, y, target], [g_post, gt], [BF16, F32],
                     [(8, LANES), (8, D)], S, D, 128)


def _pre_mlp_and_post_mix_bwd(dh2, x1, dout, mix, g_pre, sc, g_post, gt):
    S, D = x1.shape

    def body(dh_ref, x1_ref, dout_ref, mix_ref, g_ref, sc_ref, gp_ref, gt_ref, dx1_ref, dmix_ref, acc_ref):
        dh = dh_ref[...]
        x1v = x1_ref[...]
        r3 = _rstd(x1v)
        xn = x1v * r3
        dxn = dh * (1.0 + sc_ref[...]) * g_ref[...]
        dx1 = dout_ref[...] + r3 * (dxn - xn * jnp.mean(dxn * xn, axis=-1, keepdims=True))
        dx1_ref[...] = dx1
        mv = mix_ref[...]
        r2 = _rstd(mv)
        mh = mv * r2
        dn = dx1 * gt_ref[...]
        dmh = dn * gp_ref[...]
        dmix_ref[...] = (r2 * (dmh - mh * jnp.mean(dmh * mh, axis=-1, keepdims=True))).astype(BF16)
        _acc_rows(acc_ref, [
            jnp.sum(dh, axis=0, keepdims=True),
            jnp.sum(dh * xn * g_ref[...], axis=0, keepdims=True),
            jnp.sum(dh * (1.0 + sc_ref[...]) * xn, axis=0, keepdims=True),
            jnp.sum(dx1 * mh * gp_ref[...], axis=0, keepdims=True),
            jnp.sum(dn * mh, axis=0, keepdims=True)])

    return _row_call("pre_mlp_post_mix_bwd", body, [dh2, x1, dout, mix], [g_pre, sc, g_post, gt], [F32, BF16],
                     [(8, D)], S, D, 128)


def _pre_mix_bwd(dh, x, dx1, g_pre, sc):
    S, D = x.shape

    def body(dh_ref, x_ref, dx1_ref, g_ref, sc_ref, gx_ref, acc_ref):
        dhv = dh_ref[...]
        xv = x_ref[...]
        r = _rstd(xv)
        xn = xv * r
        dxn = dhv * (1.0 + sc_ref[...]) * g_ref[...]
        gx_ref[...] = dx1_ref[...] + r * (dxn - xn * jnp.mean(dxn * xn, axis=-1, keepdims=True))
        _acc_rows(acc_ref, [
            jnp.sum(dhv, axis=0, keepdims=True),
            jnp.sum(dhv * xn * g_ref[...], axis=0, keepdims=True),
            jnp.sum(dhv * (1.0 + sc_ref[...]) * xn, axis=0, keepdims=True)])

    return _row_call("pre_mix_bwd", body, [dh, x, dx1], [g_pre, sc], [F32], [(8, D)], S, D, 128)


CUM_BLOCK = 256


def _tri(n, upper):
    r = lax.broadcasted_iota(jnp.int32, (n, n), 0)
    c = lax.broadcasted_iota(jnp.int32, (n, n), 1)
    return ((c >= r) if upper else (c <= r)).astype(F32)


def _fox_gate_fwd(fg, b_pad):
    S = fg.shape[0]
    cb = _fit(CUM_BLOCK, S)

    def body(fg_ref, b_ref, cumt_ref, cum_ref):
        low = _tri(cb, False)
        carry = jnp.zeros((1, LANES), F32)
        for n in range(S // cb):
            z = fg_ref[n * cb:(n + 1) * cb, :] + b_ref[...]
            logf = jnp.minimum(z, 0.0) - jnp.log(1.0 + jnp.exp(-jnp.abs(z)))
            blk = jnp.dot(low, logf, precision=lax.Precision.HIGHEST, preferred_element_type=F32) + carry
            cum_ref[n * cb:(n + 1) * cb, :] = blk
            carry = blk[cb - 1:cb, :]
        cumt_ref[...] = cum_ref[...].T

    return pl.pallas_call(
        body, name="fox_gate_fwd", out_shape=jax.ShapeDtypeStruct((LANES, S), F32),
        scratch_shapes=[pltpu.VMEM((S, LANES), F32)],
        compiler_params=pltpu.CompilerParams(vmem_limit_bytes=_vmem(6 * S * LANES * 4)),
    )(fg, b_pad)


def _fox_gate_bwd(dcum_k, dcum_q, fg, b_pad):
    S = fg.shape[0]
    n_fox = dcum_q.shape[0]
    cb = _fit(CUM_BLOCK, S)

    def body(dk_ref, dq_ref, fg_ref, b_ref, dfg_ref, db_ref, dc_ref):
        lane = lax.broadcasted_iota(jnp.int32, (S, LANES), 1)
        dc = dk_ref[...].T
        for h in range(n_fox):
            dc = dc + jnp.where(lane == h, dq_ref[h], 0.0)
        dc_ref[...] = dc
        up = _tri(cb, True)
        carry = jnp.zeros((1, LANES), F32)
        db = jnp.zeros((1, LANES), F32)
        for n in reversed(range(S // cb)):
            blk = jnp.dot(up, dc_ref[n * cb:(n + 1) * cb, :], precision=lax.Precision.HIGHEST,
                          preferred_element_type=F32) + carry
            carry = blk[0:1, :]
            z = fg_ref[n * cb:(n + 1) * cb, :] + b_ref[...]
            dfg = blk * (1.0 / (1.0 + jnp.exp(z)))
            dfg_ref[n * cb:(n + 1) * cb, :] = dfg.astype(BF16)
            db = db + jnp.sum(dfg, axis=0, keepdims=True)
        db_ref[...] = jnp.broadcast_to(db, db_ref.shape)

    return pl.pallas_call(
        body, name="fox_gate_bwd",
        out_shape=[jax.ShapeDtypeStruct((S, LANES), BF16), jax.ShapeDtypeStruct((8, LANES), F32)],
        scratch_shapes=[pltpu.VMEM((S, LANES), F32)],
        compiler_params=pltpu.CompilerParams(vmem_limit_bytes=_vmem((8 + 2 * n_fox) * S * LANES * 4)),
    )(dcum_k, dcum_q, fg, b_pad)


FOX_TILE = 512


LOG2E = 1.4426950408889634


def _fox_scores(q, k, ck2, masked, t):
    s = lax.dot_general(q, k, _NT, preferred_element_type=F32) * (HEAD_DIM ** -0.5 * LOG2E) - ck2
    if masked:
        row = lax.broadcasted_iota(jnp.int32, (t, t), 0)
        col = lax.broadcasted_iota(jnp.int32, (t, t), 1)
        s = jnp.where(col <= row, s, NEG)
    return s


def _fox_fwd(proj, cum_row, n_fox):
    S = proj.shape[0]
    t = _fit(FOX_TILE, S)
    nq = S // t

    def body(q_ref, k_ref, v_ref, ck_ref, o_ref, lse_ref):
        def q_block(qi, _):
            q0 = pl.multiple_of(qi * t, t)
            q = q_ref[pl.ds(q0, t), :]

            def kv_block(j, carry, masked):
                m, l, acc = carry
                k0 = pl.multiple_of(j * t, t)
                s = _fox_scores(q, k_ref[pl.ds(k0, t), :], ck_ref[0, :, pl.ds(k0, t)] * LOG2E, masked, t)
                m_new = jnp.maximum(m, jnp.max(s, axis=-1, keepdims=True))
                alpha = jnp.exp2(m - m_new)
                p = jnp.exp2(s - m_new)
                l = alpha * l + jnp.sum(p, axis=-1, keepdims=True)
                acc = alpha * acc + jnp.dot(p.astype(BF16), v_ref[pl.ds(k0, t), :], preferred_element_type=F32)
                return m_new, l, acc

            init = (jnp.full((t, 1), NEG, F32), jnp.zeros((t, 1), F32), jnp.zeros((t, HEAD_DIM), F32))
            carry = lax.fori_loop(0, qi, lambda j, cr: kv_block(j, cr, False), init)
            m, l, acc = kv_block(qi, carry, True)
            o_ref[pl.ds(q0, t), :] = acc / l
            lse_ref[0, pl.ds(q0, t), :] = jnp.broadcast_to(m + jnp.log(l) * LOG2E, (t, LANES))
            return 0

        lax.fori_loop(0, nq, q_block, 0)

    col = lambda off: pl.BlockSpec((S, HEAD_DIM), lambda h: (0, off + h))
    per_head = pl.BlockSpec((1, S, LANES), lambda h: (h, 0, 0))
    return pl.pallas_call(
        body, name="fox_fwd", grid=(n_fox,),
        in_specs=[col(0), col(n_fox), col(2 * n_fox), pl.BlockSpec((1, 1, S), lambda h: (h, 0, 0))],
        out_specs=[pl.BlockSpec((S, HEAD_DIM), lambda h: (0, h)), per_head],
        out_shape=[jax.ShapeDtypeStruct((S, n_fox * HEAD_DIM), F32), jax.ShapeDtypeStruct((n_fox, S, LANES), F32)],
        compiler_params=pltpu.CompilerParams(dimension_semantics=("parallel",),
                                             vmem_limit_bytes=_vmem(16 * S * HEAD_DIM * 4 + 12 * t * t * 4)),
    )(proj, proj, proj, cum_row)


def _fox_bwd(proj, o, do, lse_b, cum_row, n_fox):
    S = proj.shape[0]
    t = _fit(FOX_TILE, S)
    nq = S // t
    scale = HEAD_DIM ** -0.5

    def body(q_ref, k_ref, v_ref, o_ref, do_ref, lse_ref, ck_ref, dq_ref, dk_ref, dv_ref, dc_ref, dcq_ref,
             dq_acc, delta_ref):
        dq_acc[...] = jnp.zeros_like(dq_acc)
        dcq_ref[...] = jnp.zeros_like(dcq_ref)

        def delta_block(qi, _):
            q0 = pl.multiple_of(qi * t, t)
            d = jnp.sum(do_ref[pl.ds(q0, t), :] * o_ref[pl.ds(q0, t), :], axis=-1, keepdims=True)
            delta_ref[pl.ds(q0, t), :] = jnp.broadcast_to(d, (t, LANES))
            return 0

        lax.fori_loop(0, nq, delta_block, 0)

        def kv_block(j, _):
            k0 = pl.multiple_of(j * t, t)
            k = k_ref[pl.ds(k0, t), :]
            v = v_ref[pl.ds(k0, t), :]
            ck2 = ck_ref[0, :, pl.ds(k0, t)] * LOG2E

            def q_block(qi, carry, masked):
                dk, dv, dc = carry
                q0 = pl.multiple_of(qi * t, t)
                q = q_ref[pl.ds(q0, t), :]
                dov = do_ref[pl.ds(q0, t), :].astype(BF16)
                p = jnp.exp2(_fox_scores(q, k, ck2, masked, t) - lse_ref[0, pl.ds(q0, t), :][:, :1])
                dp = lax.dot_general(dov, v, _NT, preferred_element_type=F32)
                ds = p * (dp - delta_ref[pl.ds(q0, t), :][:, :1])
                dsb = ds.astype(BF16)
                dv = dv + lax.dot_general(p.astype(BF16), dov, _TN, preferred_element_type=F32)
                dk = dk + lax.dot_general(dsb, q, _TN, preferred_element_type=F32)
                dq_acc[pl.ds(q0, t), :] += jnp.dot(dsb, k, preferred_element_type=F32)
                dc = dc - jnp.sum(ds, axis=0, keepdims=True)
                dcq_ref[0, pl.ds(q0, t), :] += jnp.broadcast_to(jnp.sum(ds, axis=1, keepdims=True), (t, LANES))
                return dk, dv, dc

            init = (jnp.zeros((t, HEAD_DIM), F32), jnp.zeros((t, HEAD_DIM), F32), jnp.zeros((1, t), F32))
            carry = q_block(j, init, True)
            dk, dv, dc = lax.fori_loop(j + 1, nq, lambda qi, cr: q_block(qi, cr, False), carry)
            dk_ref[pl.ds(k0, t), :] = (dk * scale).astype(BF16)
            dv_ref[pl.ds(k0, t), :] = dv.astype(BF16)
            dc_ref[0, :, pl.ds(k0, t)] = dc
            return 0

        lax.fori_loop(0, nq, kv_block, 0)
        dq_ref[...] = (dq_acc[...] * scale).astype(BF16)

    col = lambda off: pl.BlockSpec((S, HEAD_DIM), lambda h: (0, off + h))
    per_head = pl.BlockSpec((1, S, LANES), lambda h: (h, 0, 0))
    row = pl.BlockSpec((1, 1, S), lambda h: (h, 0, 0))
    grad = jax.ShapeDtypeStruct((S, n_fox * HEAD_DIM), BF16)
    return pl.pallas_call(
        body, name="fox_bwd", grid=(n_fox,),
        in_specs=[col(0), col(n_fox), col(2 * n_fox), col(0), col(0), per_head, row],
        out_specs=[col(0), col(0), col(0), row, per_head],
        out_shape=[grad, grad, grad, jax.ShapeDtypeStruct((n_fox, 1, S), F32), jax.ShapeDtypeStruct((n_fox, S, LANES), F32)],
        scratch_shapes=[pltpu.VMEM((S, HEAD_DIM), F32), pltpu.VMEM((S, LANES), F32)],
        compiler_params=pltpu.CompilerParams(dimension_semantics=("parallel",),
                                             vmem_limit_bytes=_vmem(24 * S * HEAD_DIM * 4 + 16 * t * t * 4)),
    )(proj, proj, proj, o, do, lse_b, cum_row)


def _rope_tables(S):
    half = HEAD_DIM // 2
    inv_freq = 1.0 / (ROPE_THETA ** (jnp.arange(half, dtype=F32) * (2.0 / HEAD_DIM)))
    ang = jnp.arange(S).astype(F32)[:, None] * inv_freq[None, :]
    cos, sin = jnp.cos(ang), jnp.sin(ang)
    return jnp.concatenate([cos, cos], axis=-1), jnp.concatenate([-sin, sin], axis=-1)


def _rope(name, src, first_block, n_blocks, cos, sin_signed):
    S = src.shape[0]

    def body(x_ref, cos_ref, sin_ref, o_ref):
        xv = x_ref[...].astype(F32)
        o_ref[...] = (xv * cos_ref[...] + pltpu.roll(xv, HEAD_DIM // 2, 1) * sin_ref[...]).astype(BF16)

    table = pl.BlockSpec((S, HEAD_DIM), lambda n: (0, 0))
    return pl.pallas_call(
        body, name=name, grid=(n_blocks,),
        in_specs=[pl.BlockSpec((S, HEAD_DIM), lambda n: (0, first_block + n)), table, table],
        out_specs=pl.BlockSpec((S, HEAD_DIM), lambda n: (0, n)),
        out_shape=jax.ShapeDtypeStruct((S, n_blocks * HEAD_DIM), BF16),
        compiler_params=pltpu.CompilerParams(dimension_semantics=("parallel",),
                                             vmem_limit_bytes=_vmem(12 * S * HEAD_DIM * 4)),
    )(src, cos, sin_signed)


def _swa_tile(q_ref, kp_ref, kc_ref, n, group, scale):
    B = SWA_BLOCK
    qs = jnp.concatenate([q_ref[:, g * HEAD_DIM:(g + 1) * HEAD_DIM] for g in range(group)], axis=0)
    kcat = jnp.concatenate([kp_ref[...], kc_ref[...]], axis=0)
    s = lax.dot_general(qs, kcat, _NT, preferred_element_type=F32) * scale
    qi = lax.broadcasted_iota(jnp.int32, (group * B, 2 * B), 0) % B
    kj = lax.broadcasted_iota(jnp.int32, (group * B, 2 * B), 1)
    diff = qi + B - kj
    mask = (diff >= 0) & (diff < B) & ((n * B + kj - B) >= 0)
    return qs, kcat, jnp.where(mask, s, NEG)


def _swa_sink_col(sink_ref, kv, group):
    head = lax.broadcasted_iota(jnp.int32, (group * SWA_BLOCK, 1), 0) // SWA_BLOCK
    col = jnp.zeros((group * SWA_BLOCK, 1), F32)
    for g in range(group):
        col = jnp.where(head == g, sink_ref[kv * group + g], col)
    return col


def _swa_specs(n_kv, group, q_first, k_first, v_first):
    B = SWA_BLOCK
    prev = lambda n: jnp.maximum(n - 1, 0)
    return [
        pl.BlockSpec((B, group * HEAD_DIM), lambda kv, n: (n, q_first + kv)),
        pl.BlockSpec((B, HEAD_DIM), lambda kv, n: (prev(n), k_first + kv)),
        pl.BlockSpec((B, HEAD_DIM), lambda kv, n: (n, k_first + kv)),
        pl.BlockSpec((B, HEAD_DIM), lambda kv, n: (prev(n), v_first + kv)),
        pl.BlockSpec((B, HEAD_DIM), lambda kv, n: (n, v_first + kv)),
    ]


def _swa_fwd(rq, proj, v_first, sinks, n_q, n_kv):
    S = rq.shape[0]
    B = SWA_BLOCK
    group = n_q // n_kv
    scale = HEAD_DIM ** -0.5

    def body(q_ref, kp_ref, kc_ref, vp_ref, vc_ref, sink_ref, o_ref, lse_ref):
        kv, n = pl.program_id(0), pl.program_id(1)
        _, _, s = _swa_tile(q_ref, kp_ref, kc_ref, n, group, scale)
        sink = _swa_sink_col(sink_ref, kv, group)
        m = jnp.maximum(jnp.max(s, axis=-1, keepdims=True), sink)
        p = jnp.exp(s - m)
        denom = jnp.sum(p, axis=-1, keepdims=True) + jnp.exp(sink - m)
        vcat = jnp.concatenate([vp_ref[...], vc_ref[...]], axis=0)
        o = jnp.dot((p / denom).astype(BF16), vcat, preferred_element_type=F32)
        lse = m + jnp.log(denom)
        for g in range(group):
            o_ref[:, g * HEAD_DIM:(g + 1) * HEAD_DIM] = o[g * B:(g + 1) * B, :]
            lse_ref[0, :, g * LANES:(g + 1) * LANES] = jnp.broadcast_to(lse[g * B:(g + 1) * B, :], (B, LANES))

    specs = _swa_specs(n_kv, group, 0, n_q, v_first)
    q_blk = pl.BlockSpec((B, group * HEAD_DIM), lambda kv, n: (n, kv))
    return pl.pallas_call(
        body, name="swa_fwd", grid=(n_kv, S // B),
        in_specs=specs + [pl.BlockSpec(memory_space=pltpu.SMEM)],
        out_specs=[q_blk, pl.BlockSpec((1, B, group * LANES), lambda kv, n: (kv, n, 0))],
        out_shape=[jax.ShapeDtypeStruct((S, n_q * HEAD_DIM), F32), jax.ShapeDtypeStruct((n_kv, S, group * LANES), F32)],
        compiler_params=pltpu.CompilerParams(dimension_semantics=("parallel", "arbitrary")),
    )(rq, rq, rq, proj, proj, sinks)


def _swa_bwd(rq, proj, v_first, sinks, o, do, do_first, lse_b, n_q, n_kv):
    S = rq.shape[0]
    B = SWA_BLOCK
    group = n_q // n_kv
    scale = HEAD_DIM ** -0.5

    def body(q_ref, kp_ref, kc_ref, vp_ref, vc_ref, o_ref, do_ref, lse_ref, sink_ref,
             dq_ref, dk_ref, dv_ref, dsink_ref):
        kv, n = pl.program_id(0), pl.program_id(1)

        @pl.when(n == 0)
        def _():
            dk_ref[...] = jnp.zeros_like(dk_ref)
            dv_ref[...] = jnp.zeros_like(dv_ref)
            dsink_ref[...] = jnp.zeros_like(dsink_ref)

        qs, kcat, s = _swa_tile(q_ref, kp_ref, kc_ref, n, group, scale)
        sink = _swa_sink_col(sink_ref, kv, group)
        stack = lambda ref, w: jnp.concatenate([ref[:, g * w:(g + 1) * w] for g in range(group)], axis=0)
        lse = jnp.concatenate([lse_ref[0, :, g * LANES:g * LANES + 1] for g in range(group)], axis=0)
        do32 = stack(do_ref, HEAD_DIM)
        delta = jnp.sum(do32 * stack(o_ref, HEAD_DIM), axis=-1, keepdims=True)
        dov = do32.astype(BF16)
        p = jnp.exp(s - lse)
        vcat = jnp.concatenate([vp_ref[...], vc_ref[...]], axis=0)
        dp = lax.dot_general(dov, vcat, _NT, preferred_element_type=F32)
        ds = p * (dp - delta)
        dsb = ds.astype(BF16)
        dq = jnp.dot(dsb, kcat, preferred_element_type=F32) * scale
        for g in range(group):
            dq_ref[:, g * HEAD_DIM:(g + 1) * HEAD_DIM] = dq[g * B:(g + 1) * B, :].astype(BF16)
        dkcat = lax.dot_general(dsb, qs, _TN, preferred_element_type=F32) * scale
        dvcat = lax.dot_general(p.astype(BF16), dov, _TN, preferred_element_type=F32)
        prev0 = pl.multiple_of(jnp.maximum(n - 1, 0) * B, B)
        cur0 = pl.multiple_of(n * B, B)
        dk_ref[0, pl.ds(prev0, B), :] += dkcat[:B, :]
        dk_ref[0, pl.ds(cur0, B), :] += dkcat[B:, :]
        dv_ref[0, pl.ds(prev0, B), :] += dvcat[:B, :]
        dv_ref[0, pl.ds(cur0, B), :] += dvcat[B:, :]
        dsk = -jnp.exp(sink - lse) * delta
        lane = lax.broadcasted_iota(jnp.int32, (1, LANES), 1)
        row = jnp.zeros((1, LANES), F32)
        for g in range(group):
            row = row + jnp.where(lane == g, jnp.sum(dsk[g * B:(g + 1) * B, :]), 0.0)
        dsink_ref[0, 0:1, :] += row

    specs = _swa_specs(n_kv, group, 0, n_q, v_first)
    q_blk = pl.BlockSpec((B, group * HEAD_DIM), lambda kv, n: (n, kv))
    acc = pl.BlockSpec((1, S, HEAD_DIM), lambda kv, n: (kv, 0, 0))
    return pl.pallas_call(
        body, name="swa_bwd", grid=(n_kv, S // B),
        in_specs=specs + [q_blk, pl.BlockSpec((B, group * HEAD_DIM), lambda kv, n: (n, do_first + kv)),
                          pl.BlockSpec((1, B, group * LANES), lambda kv, n: (kv, n, 0)),
                          pl.BlockSpec(memory_space=pltpu.SMEM)],
        out_specs=[q_blk, acc, acc, pl.BlockSpec((1, 8, LANES), lambda kv, n: (kv, 0, 0))],
        out_shape=[jax.ShapeDtypeStruct((S, n_q * HEAD_DIM), BF16), jax.ShapeDtypeStruct((n_kv, S, HEAD_DIM), F32),
                   jax.ShapeDtypeStruct((n_kv, S, HEAD_DIM), F32), jax.ShapeDtypeStruct((n_kv, 8, LANES), F32)],
        compiler_params=pltpu.CompilerParams(dimension_semantics=("parallel", "arbitrary")),
    )(rq, rq, rq, proj, proj, o, do, lse_b, sinks)


def _adamw(w, g, m, v):
    m = ADAM_B1 * m + (1.0 - ADAM_B1) * g
    v = ADAM_B2 * v + (1.0 - ADAM_B2) * (g * g)
    m_hat = m / (1.0 - ADAM_B1 ** ADAM_STEP)
    v_hat = v / (1.0 - ADAM_B2 ** ADAM_STEP)
    delta = -ADAM_LR * (m_hat / (jnp.sqrt(v_hat) + ADAM_EPS) + ADAM_WD * w)
    return delta, m, v


def _mod_fwd(cond_in, w_mod, b_shard):
    R, D = cond_in.shape
    cols = w_mod.shape[1]
    tn = _fit(512, cols)

    def body(c_ref, w_ref, b_ref, o_ref):
        cv = c_ref[...]
        cond = (cv / (1.0 + jnp.exp(-cv))).astype(BF16)
        o_ref[...] = jnp.dot(cond, w_ref[...].astype(BF16), preferred_element_type=F32) + b_ref[...]

    return pl.pallas_call(
        body, name="mod_fwd", grid=(cols // tn,),
        in_specs=[pl.BlockSpec((R, D), lambda j: (0, 0)), pl.BlockSpec((D, tn), lambda j: (0, j)),
                  pl.BlockSpec((1, tn), lambda j: (0, j))],
        out_specs=pl.BlockSpec((R, tn), lambda j: (0, j)),
        out_shape=jax.ShapeDtypeStruct((R, cols), F32),
        compiler_params=pltpu.CompilerParams(dimension_semantics=("parallel",), vmem_limit_bytes=_vmem(3 * D * tn * 4)),
    )(cond_in, w_mod, b_shard)


def _mod_update(c_t, dmod, w, m, v):
    D, nb = c_t.shape
    cols = w.shape[1]
    tn = _fit(256, cols)

    def body(c_ref, d_ref, w_ref, m_ref, v_ref, g_ref, dl_ref, nm_ref, nv_ref):
        cv = c_ref[...]
        cond = cv / (1.0 + jnp.exp(-cv))
        g = jnp.zeros((D, tn), F32)
        for b in range(nb):
            g = g + cond[:, b:b + 1] * d_ref[b:b + 1, :]
        g_ref[...] = g
        dl_ref[...], nm_ref[...], nv_ref[...] = _adamw(w_ref[...], g, m_ref[...], v_ref[...])

    blk = pl.BlockSpec((D, tn), lambda j: (0, j))
    out = jax.ShapeDtypeStruct((D, cols), F32)
    return pl.pallas_call(
        body, name="mod_update", grid=(cols // tn,),
        in_specs=[pl.BlockSpec((D, nb), lambda j: (0, 0)), pl.BlockSpec((nb, tn), lambda j: (0, j)), blk, blk, blk],
        out_specs=[blk] * 4, out_shape=[out] * 4,
        compiler_params=pltpu.CompilerParams(dimension_semantics=("parallel",), vmem_limit_bytes=_vmem(18 * D * tn * 4)),
    )(c_t, dmod, w, m, v)


def _small_update(stacked, w, m, v):
    R, C = w.shape

    def body(s_ref, w_ref, m_ref, v_ref, g_ref, dl_ref, nm_ref, nv_ref):
        g = s_ref[0:R, :]
        for d in range(1, N_DEV):
            g = g + s_ref[d * R:(d + 1) * R, :]
        g_ref[...] = g
        dl_ref[...], nm_ref[...], nv_ref[...] = _adamw(w_ref[...], g, m_ref[...], v_ref[...])

    return pl.pallas_call(body, name="small_update", out_shape=[jax.ShapeDtypeStruct((R, C), F32)] * 4)(stacked, w, m, v)


def _place():
    return lax.axis_index("x"), lax.axis_index("y"), lax.axis_index("c")


def _allgather8(name, block):
    m_per, n = block.shape

    def body(x_ref, out_ref, token_ref, send_sems, recv_sems, local_sem):
        token_ref[...] = jnp.zeros_like(token_ref)
        x, y, c = _place()
        me, sibling = (x, y, c), (x, y, 1 - c)
        chips = [(1 - x, y), (x, 1 - y), (1 - x, 1 - y)]

        def rows(px, py, pc):
            return out_ref.at[pl.ds((4 * px + 2 * py + pc) * m_per, m_per), :]

        def copy(k, blk, to, src=None):
            return pltpu.make_async_remote_copy(
                src_ref=rows(*blk) if src is None else src, dst_ref=rows(*blk),
                send_sem=send_sems.at[k], recv_sem=recv_sems.at[k], device_id=to, device_id_type=MESH)

        mine = pltpu.make_async_copy(x_ref, rows(*me), local_sem)
        mine.start()
        first = [copy(0, me, sibling, src=x_ref)]
        first += [copy(1 + j, me, (*chip, c), src=x_ref) for j, chip in enumerate(chips)]
        for cp in first:
            cp.start()
        passed = [copy(4 + j, (*chip, c), sibling) for j, chip in enumerate(chips)]
        for j, chip in enumerate(chips):
            copy(1 + j, (*chip, c), me).wait_recv()
            passed[j].start()
        copy(0, sibling, me).wait_recv()
        for j, chip in enumerate(chips):
            copy(4 + j, (*chip, 1 - c), me).wait_recv()
        for cp in first + passed:
            cp.wait_send()
        mine.wait()

    vmem = pl.BlockSpec(memory_space=pltpu.VMEM)
    return pl.pallas_call(
        body, name=name,
        out_shape=[jax.ShapeDtypeStruct((N_DEV * m_per, n), block.dtype), jax.ShapeDtypeStruct((8, LANES), F32)],
        in_specs=[vmem], out_specs=[vmem, vmem],
        scratch_shapes=[pltpu.SemaphoreType.DMA((7,)), pltpu.SemaphoreType.DMA((7,)), pltpu.SemaphoreType.DMA],
    )(block)


_ANY = pl.BlockSpec(memory_space=pl.ANY)


def _half(ref, c, rows):
    return ref.at[pl.ds(c * (rows // 2), rows // 2), :]


_HBM = pl.BlockSpec(memory_space=pltpu.HBM)
_SEM = pl.BlockSpec(memory_space=pltpu.SEMAPHORE)
_EFFECT = pltpu.SideEffectType.DATAFLOW_SIDE_EFFECTING


def _ici_start(name, srcs, land_shapes, plan, per_source=3):
    ns, nl = len(srcs), len(land_shapes)
    n_copies = per_source * ns

    def body(*refs):
        src_refs, land_refs = refs[:ns], refs[ns:ns + nl]
        send_sems, recv_sems = refs[ns + nl], refs[ns + nl + 1]
        token = refs[-1]
        for n, (src, dst, peer, _) in enumerate(plan(src_refs, land_refs)):
            pltpu.make_async_remote_copy(src_ref=src, dst_ref=dst, send_sem=send_sems.at[n], recv_sem=recv_sems.at[n],
                                         device_id=peer, device_id_type=MESH).start()
        token[...] = jnp.zeros_like(token)

    lands = [lax.empty(s.shape, s.dtype) for s in land_shapes]
    out = pl.pallas_call(
        body, name=name,
        out_shape=(pltpu.SemaphoreType.DMA((n_copies,)), pltpu.SemaphoreType.DMA((n_copies,)),
                   *[pltpu.HBM(a.shape, a.dtype) for a in list(srcs) + lands], jax.ShapeDtypeStruct((8, LANES), F32)),
        in_specs=[_HBM] * (ns + nl),
        out_specs=(_SEM, _SEM, *[_HBM] * (ns + nl), pl.BlockSpec(memory_space=pltpu.VMEM)),
        input_output_aliases={n: 2 + n for n in range(ns + nl)},
        compiler_params=pltpu.CompilerParams(has_side_effects=_EFFECT),
    )(*[pltpu.with_memory_space_constraint(a, pltpu.HBM) for a in list(srcs) + lands])
    return out[0], out[1], list(out[2:2 + ns]), list(out[2 + ns:2 + ns + nl]), out[-1]


def _ici_wait(name, send_sems, recv_sems, srcs, lands, plan, after):
    ns, nl = len(srcs), len(lands)

    def body(*refs):
        src_refs, land_refs = refs[:ns], refs[ns:ns + nl]
        send_sems, recv_sems = refs[ns + nl], refs[ns + nl + 1]
        for n, (src, _, peer, mine) in enumerate(plan(src_refs, land_refs)):
            cp = pltpu.make_async_remote_copy(src_ref=src, dst_ref=mine, send_sem=send_sems.at[n],
                                              recv_sem=recv_sems.at[n], device_id=peer, device_id_type=MESH)
            cp.wait_send()
            cp.wait_recv()

    out = pl.pallas_call(
        body, name=name, out_shape=[pltpu.HBM(a.shape, a.dtype) for a in list(srcs) + list(lands)],
        in_specs=[_HBM] * (ns + nl) + [_SEM, _SEM, _ANY], out_specs=[_HBM] * (ns + nl),
        input_output_aliases={n: n for n in range(ns + nl)},
        compiler_params=pltpu.CompilerParams(has_side_effects=_EFFECT),
    )(*srcs, *lands, send_sems, recv_sems, after)
    return list(out[:ns]), list(out[ns:])


def _gather_plan(src_refs, land_refs):
    x, y, c = _place()
    copies = []
    for w, land in zip(src_refs, land_refs):
        R = w.shape[0]
        for cx, cy in [(1 - x, y), (x, 1 - y), (1 - x, 1 - y)]:
            copies.append((_half(w, c, R), _half(land.at[2 * x + y], c, R), (cx, cy, c),
                           _half(land.at[2 * cx + cy], c, R)))
    return copies


def _pass_plan(src_refs, land_refs):
    x, y, c = _place()
    copies = []
    for land in src_refs:
        R = land.shape[1]
        for cx, cy in [(1 - x, y), (x, 1 - y), (1 - x, 1 - y)]:
            slot = land.at[2 * cx + cy]
            copies.append((_half(slot, c, R), _half(slot, c, R), (x, y, 1 - c), _half(slot, 1 - c, R)))
    return copies


def _pair_plan(src_refs, land_refs):
    x, y, c = _place()
    copies = []
    for g, land in zip(src_refs, land_refs):
        half = g.shape[1] // 2
        copies.append((g.at[:, pl.ds((1 - c) * half, half), :], land, (x, y, 1 - c), land))
    return copies


def _share_plan(src_refs, land_refs):
    x, y, c = _place()
    return [(h, land, (x, y, 1 - c), land) for h, land in zip(src_refs, land_refs)]


def _pass_to_sibling(name, lands):
    nw = len(lands)

    def body(*refs):
        ins, outs = refs[:nw], refs[nw:2 * nw]
        send_sems, recv_sems = refs[2 * nw:]
        x, y, c = _place()
        chips = [(1 - x, y), (x, 1 - y), (1 - x, 1 - y)]
        copies = []
        for k in range(nw):
            R = ins[k].shape[1]
            for j, (cx, cy) in enumerate(chips):
                cp = pltpu.make_async_remote_copy(
                    src_ref=_half(ins[k].at[2 * cx + cy], c, R), dst_ref=_half(outs[k].at[2 * cx + cy], c, R),
                    send_sem=send_sems.at[3 * k + j], recv_sem=recv_sems.at[3 * k + j],
                    device_id=(x, y, 1 - c), device_id_type=MESH)
                cp.start()
                copies.append(cp)
        for k in range(nw):
            R = ins[k].shape[1]
            for j, (cx, cy) in enumerate(chips):
                pltpu.make_async_remote_copy(
                    src_ref=_half(ins[k].at[2 * cx + cy], c, R), dst_ref=_half(outs[k].at[2 * cx + cy], 1 - c, R),
                    send_sem=send_sems.at[3 * k + j], recv_sem=recv_sems.at[3 * k + j],
                    device_id=(x, y, 1 - c), device_id_type=MESH).wait_recv()
        for cp in copies:
            cp.wait_send()

    return pl.pallas_call(
        body, name=name, out_shape=[jax.ShapeDtypeStruct(a.shape, a.dtype) for a in lands],
        in_specs=[_ANY] * nw, out_specs=[_ANY] * nw, input_output_aliases={k: k for k in range(nw)},
        scratch_shapes=[pltpu.SemaphoreType.DMA((3 * nw,)), pltpu.SemaphoreType.DMA((3 * nw,))],
    )(*lands)


def _tie(vec, token):
    return vec + token[0:1, 0:1]


ROW_ALIGN = 16
TILE_ELEMS = 512 * 1024


def _tiles(rows, cols):
    fits = [t for t in range(ROW_ALIGN, min(rows, 256) + 1, ROW_ALIGN) if rows % t == 0]
    tr = fits[-1] if fits and fits[-1] >= 64 else rows
    tc = cols
    while tr * tc > TILE_ELEMS and tc % (2 * LANES) == 0:
        tc //= 2
    return tr, tc


def _pair_add(name, core, grad, recv):
    n, R, C = grad.shape
    half = R // 2
    tr, tc = _tiles(half, C)
    nr = half // tr

    def body(core_ref, g_ref, r_ref, o_ref):
        o_ref[...] = (g_ref[...].astype(F32) + r_ref[...].astype(F32)).astype(BF16)

    grid_spec = pltpu.PrefetchScalarGridSpec(
        num_scalar_prefetch=1, grid=(n, nr, C // tc),
        in_specs=[pl.BlockSpec((1, tr, tc), lambda s, r, q, core_ref: (s, core_ref[0] * nr + r, q)),
                  pl.BlockSpec((1, tr, tc), lambda s, r, q, core_ref: (s, r, q))],
        out_specs=pl.BlockSpec((1, tr, tc), lambda s, r, q, core_ref: (s, r, q)))
    return pl.pallas_call(
        body, name=name, grid_spec=grid_spec, out_shape=jax.ShapeDtypeStruct((n, half, C), BF16),
        compiler_params=pltpu.CompilerParams(dimension_semantics=("parallel", "parallel", "parallel")),
    )(core, grad, recv)


def _scatter_plan(src_refs, land_refs):
    x, y, c = _place()
    copies = []
    for p, land in zip(src_refs, land_refs):
        for j, (cx, cy) in enumerate([(1 - x, y), (x, 1 - y), (1 - x, 1 - y)]):
            copies.append((p.at[2 * cx + cy], land.at[j], (cx, cy, c), land.at[j]))
    return copies


def _chip_add(name, chip, sums, recv):
    _, H, C = sums.shape
    tr, tc = _tiles(H, C)

    def body(chip_ref, p_ref, r_ref, o_ref):
        total = p_ref[0].astype(F32)
        for j in range(3):
            total = total + r_ref[j].astype(F32)
        o_ref[...] = total

    grid_spec = pltpu.PrefetchScalarGridSpec(
        num_scalar_prefetch=1, grid=(H // tr, C // tc),
        in_specs=[pl.BlockSpec((1, tr, tc), lambda r, q, chip_ref: (chip_ref[0], r, q)),
                  pl.BlockSpec((3, tr, tc), lambda r, q, chip_ref: (0, r, q))],
        out_specs=pl.BlockSpec((tr, tc), lambda r, q, chip_ref: (r, q)))
    return pl.pallas_call(
        body, name=name, grid_spec=grid_spec, out_shape=jax.ShapeDtypeStruct((H, C), F32),
        compiler_params=pltpu.CompilerParams(dimension_semantics=("parallel", "parallel")),
    )(chip, sums, recv)


def _pair_share(name, halves):
    nw = len(halves)

    def body(*refs):
        hs, outs = refs[:nw], refs[nw:2 * nw]
        send_sems, recv_sems = refs[2 * nw:]
        x, y, c = _place()
        copies = []
        for k in range(nw):
            cp = pltpu.make_async_remote_copy(
                src_ref=hs[k], dst_ref=outs[k], send_sem=send_sems.at[k], recv_sem=recv_sems.at[k],
                device_id=(x, y, 1 - c), device_id_type=MESH)
            cp.start()
            copies.append(cp)
        for cp in copies:
            cp.wait()

    return pl.pallas_call(
        body, name=name,
        out_shape=[jax.ShapeDtypeStruct(h.shape, h.dtype) for h in halves],
        in_specs=[_ANY] * nw, out_specs=[_ANY] * nw,
        scratch_shapes=[pltpu.SemaphoreType.DMA((nw,)), pltpu.SemaphoreType.DMA((nw,))],
    )(*halves)


def _adam_halves(name, core, w, g_own, g_other, m, v):
    R, C = w.shape
    H = R // 2
    tr, tc = _tiles(H, C)
    nr, nc = H // tr, C // tc

    def body(core_ref, w_ref, go_ref, gr_ref, m_ref, v_ref, g_ref, dl_ref, nm_ref, nv_ref):
        own = (pl.program_id(0) // nr) == core_ref[0]
        g = jnp.where(own, go_ref[...], gr_ref[...])
        g_ref[...] = g
        dl_ref[...], nm_ref[...], nv_ref[...] = _adamw(w_ref[...], g, m_ref[...], v_ref[...])

    blk = pl.BlockSpec((tr, tc), lambda r, q, core_ref: (r, q))

    def half_spec(is_own):
        def index(r, q, core_ref):
            mine = ((r // nr) == core_ref[0]) == is_own
            done = is_own == (core_ref[0] == 0)
            return (jnp.where(mine, r % nr, jnp.where(done, nr - 1, 0)), jnp.where(mine, q, jnp.where(done, nc - 1, 0)))
        return pl.BlockSpec((tr, tc), index)
    out = jax.ShapeDtypeStruct((R, C), F32)
    grid_spec = pltpu.PrefetchScalarGridSpec(
        num_scalar_prefetch=1, grid=(R // tr, nc), in_specs=[blk, half_spec(True), half_spec(False), blk, blk],
        out_specs=[blk] * 4)
    return pl.pallas_call(
        body, name=name, grid_spec=grid_spec, out_shape=[out] * 4,
        compiler_params=pltpu.CompilerParams(dimension_semantics=("parallel", "parallel"),
                                             vmem_limit_bytes=_vmem(20 * tr * tc * 4)),
    )(core, w, g_own, g_other, m, v)


def kernel(x, c, w_mod, b_mod, g_pre_mix, g_post_mix, w_in, b_forget, swa_sinks, w_out, g_pre_mlp, g_post_mlp, w_up, w_down, loss_target, m_w_mod, m_b_mod, m_g_pre_mix, m_g_post_mix, m_w_in, m_b_forget, m_swa_sinks, m_w_out, m_g_pre_mlp, m_g_post_mlp, m_w_up, m_w_down, v_w_mod, v_b_mod, v_g_pre_mix, v_g_post_mix, v_w_in, v_b_forget, v_swa_sinks, v_w_out, v_g_pre_mlp, v_g_post_mlp, v_w_up, v_w_down):
    S, D = x.shape[1], x.shape[2]
    n_heads = D // HEAD_DIM
    n_fox = n_heads // 2
    n_swa = n_heads - n_fox
    n_kv = max(1, n_swa // 4)
    fox_w, swa_w, kv_w = n_fox * HEAD_DIM, n_swa * HEAD_DIM, n_kv * HEAD_DIM
    main_w = 3 * fox_w + swa_w + 2 * kv_w
    in_w = main_w + n_fox
    mod_cols = w_mod.shape[2]

    ax, ay, ac = _place()
    chip = 2 * ax + ay
    dev = 2 * chip + ac
    chip_arr = jnp.reshape(chip, (1,)).astype(jnp.int32)
    core_arr = jnp.reshape(ac, (1,)).astype(jnp.int32)

    x2, tgt = x[0], loss_target[0]

    c_all, _ = _allgather8("gather_c", c.reshape(8, D // 8))
    c_all = c_all.reshape(N_DEV, D)
    b_shard = lax.dynamic_slice_in_dim(b_mod, chip * mod_cols, mod_cols, axis=1)
    mod_shard = _mod_fwd(jnp.pad(c_all, ((0, 16 - N_DEV), (0, 0))), w_mod[0], b_shard)[:N_DEV]
    mod_all, token = _allgather8("gather_mod", mod_shard)
    mod_all = mod_all.reshape(N_CHIPS, 2, N_DEV, mod_cols)[:, 0]
    mod = lax.dynamic_index_in_dim(mod_all, dev, axis=1, keepdims=False).reshape(N_MOD, 1, D)
    sh_a, sc_a, gt_a, sh_m, sc_m, gt_m = [mod[n] for n in range(N_MOD)]

    in_rows = in_w // N_CHIPS
    in_rows_pad = -(-in_rows // LANES) * LANES
    slab_w = N_CHIPS * in_rows_pad

    def rows_of(a):
        return jnp.pad(a[0].T, ((0, in_rows_pad - in_rows), (0, 0)))

    def slab_cols(lo, hi):
        spans = []
        while lo < hi:
            s, r = divmod(lo, in_rows)
            n = min(hi - lo, in_rows - r)
            spans.append((s * in_rows_pad + r, s * in_rows_pad + r + n))
            lo += n
        return spans

    gate_lo = 3 * fox_w
    main_spans = slab_cols(0, gate_lo) + slab_cols(gate_lo + n_fox, in_w)
    (gate_first, gate_last), = slab_cols(gate_lo, gate_lo + n_fox)

    names = ["w_in", "w_out", "w_up", "w_down"]
    flights = {}
    for n, w in zip(names, [rows_of(w_in), w_out[0], w_up[0], w_down[0]]):
        shard = _tie(w, token).astype(BF16)
        flights[n] = _ici_start("gather_start_" + n, [shard], [jax.ShapeDtypeStruct((N_CHIPS,) + shard.shape, BF16)],
                                _gather_plan)
        token = flights[n][4]
    sc_a = _tie(sc_a, token)

    def arrived(n, after):
        send, recv, srcs, lands, _ = flights[n]
        srcs, lands = _ici_wait("gather_wait_" + n, send, recv, srcs, lands, _gather_plan, after)
        return srcs[0], _ici_start("gather_pass_start_" + n, lands, [], _pass_plan)

    def gathered(n, after, in_flight=None):
        if in_flight is None:
            send, recv, srcs, lands, _ = flights[n]
            srcs, lands = _ici_wait("gather_wait_" + n, send, recv, srcs, lands, _gather_plan, after)
            own, stack = srcs[0], _pass_to_sibling("gather_pass_" + n, lands)[0]
        else:
            own, (send, recv, lands, _, _) = in_flight
            stack = _ici_wait("gather_pass_wait_" + n, send, recv, lands, [], _pass_plan, after)[0][0]
        return lax.dynamic_update_index_in_dim(stack, own, chip, 0)

    d_ff = N_CHIPS * w_up.shape[2]

    h = _pre_norm(x2, g_pre_mix, sc_a, sh_a)
    in_state = [rows_of(a) for a in (w_in, m_w_in, v_w_in)]
    cos, sin_signed = _rope_tables(S)
    ready = h[:8, :LANES].astype(F32) + cos[:8] + sum(a[:8, :LANES] for a in in_state)
    w_slab_t = gathered("w_in", ready).reshape(slab_w, D)
    tm_p, tn_p = _fit(MM_TM, S), _fit(512, slab_w)
    win0 = gate_first // LANES * LANES
    win_j, win_off = divmod(win0, tn_p)
    assert win_off + 2 * LANES <= tn_p and gate_last - win0 <= 2 * LANES

    def proj_epilogue(acc, ex, outs):
        outs[0][...] = acc.astype(BF16)

        @pl.when(pl.program_id(1) == win_j)
        def _():
            outs[1][...] = acc[:, win_off:win_off + 2 * LANES]

    proj_slab, gate_win = _matmul(
        "in_proj", h, w_slab_t, "nt",
        [((S, slab_w), BF16, (tm_p, tn_p), lambda i, j: (i, j)), ((S, 2 * LANES), F32, (tm_p, 2 * LANES), lambda i, j: (i, 0))],
        proj_epilogue, tn=tn_p, revisits=True)
    proj = jnp.concatenate([proj_slab[:, lo:hi] for lo, hi in main_spans], axis=1)
    out_flight = arrived("w_out", proj_slab)
    fg = _tie(jnp.pad(gate_win[:, gate_first - win0:gate_last - win0], ((0, 0), (0, LANES - n_fox))), out_flight[1][4])
    b_pad = jnp.pad(b_forget, ((0, 0), (0, LANES - n_fox)))
    cum_row = _fox_gate_fwd(fg, b_pad)[:n_fox].reshape(n_fox, 1, S)
    fox_o, fox_lse = _fox_fwd(proj, cum_row, n_fox)

    rq = _rope("rope_fwd", proj, 3 * n_fox, n_swa + n_kv, cos, sin_signed)
    v_first = 3 * n_fox + n_swa + n_kv
    sinks = swa_sinks[0]
    swa_o, swa_lse = _swa_fwd(rq, proj, v_first, sinks, n_swa, n_kv)

    mixcat = jnp.concatenate([fox_o, swa_o], axis=1).astype(BF16)
    up_flight = arrived("w_up", mixcat)
    w_out_f = gathered("w_out", mixcat, out_flight).reshape(D, D)
    mix = _mm_plain("out_proj", mixcat, w_out_f, "nn", F32, after=up_flight[1][4])
    x1, h2 = _post_mix(x2, mix, g_post_mix, gt_a, g_pre_mlp, sc_m, sh_m)
    w_up_f = jnp.transpose(gathered("w_up", h2, up_flight), (1, 0, 2)).reshape(D, d_ff)

    tm_u, tn_u = _fit(MM_TM, S), _fit(MM_TN, d_ff)

    def up_epilogue(acc, ex, outs):
        outs[0][...] = acc.astype(BF16)
        r = jnp.maximum(acc, 0.0)
        outs[1][...] = (r * r).astype(BF16)

    ublk = ((S, d_ff), BF16, (tm_u, tn_u), lambda i, j: (i, j))
    u, a = _matmul("mlp_up", h2, w_up_f, "nn", [ublk, ublk], up_epilogue)
    w_down_f = gathered("w_down", a).reshape(d_ff, D)
    y = _mm_plain("mlp_down", a, w_down_f, "nn", F32)

    dy, dout, loss_part, acc_mlp_post = _loss_and_post_mlp_bwd(x1, y, tgt, g_post_mlp, gt_m)

    def du_epilogue(acc, ex, outs):
        outs[0][...] = (acc * (2.0 * jnp.maximum(ex[0][...].astype(F32), 0.0))).astype(BF16)

    du = _matmul("mlp_down_bwd", dy, w_down_f, "nt", [ublk], du_epilogue,
                 extras=[(u, (tm_u, tn_u), lambda i, j: (i, j))])[0]
    def pair_start(tag, fulls):
        return _ici_start("grad_pair_start_" + tag, fulls,
                          [jax.ShapeDtypeStruct((N_CHIPS, g.shape[1] // 2, g.shape[2]), BF16) for g in fulls],
                          _pair_plan, per_source=1)

    def scatter_start(tag, pair_flights, after):
        sums = []
        for k, (send, recv, fulls, lands, _) in enumerate(pair_flights):
            fulls, from_sibling = _ici_wait("grad_pair_wait_%s_%d" % (tag, k), send, recv, fulls, lands, _pair_plan, after)
            sums += [_pair_add("pair_add_%s_%d_%d" % (tag, k, n), core_arr, g, r)
                     for n, (g, r) in enumerate(zip(fulls, from_sibling))]
        return _ici_start("grad_scatter_start_" + tag, sums,
                          [jax.ShapeDtypeStruct((3,) + p.shape[1:], BF16) for p in sums], _scatter_plan)

    def scatter_finish(tag, flight, after):
        send, recv, srcs, lands, _ = flight
        sums, received = _ici_wait("grad_scatter_wait_" + tag, send, recv, srcs, lands, _scatter_plan, after)
        return [_chip_add("chip_add_%s_%d" % (tag, k), chip_arr, p, r) for k, (p, r) in enumerate(zip(sums, received))]

    g_down = _mm_plain("grad_w_down", a, dy, "tn", BF16)
    pair_down = pair_start("down", [g_down.reshape(N_CHIPS, d_ff // N_CHIPS, D)])
    tn_s = _fit(MM_TN, w_up.shape[2])
    per = w_up.shape[2] // tn_s

    def shard_epilogue(acc, ex, outs):
        outs[0][0] = acc.astype(BF16)

    g_up = _matmul("grad_w_up", h2, du, "tn",
                   [((N_CHIPS, D, w_up.shape[2]), BF16, (1, _fit(MM_TM, D), tn_s), lambda i, j: (j // per, i, j % per))],
                   shard_epilogue, extras=[_behind(pair_down[4])], tn=tn_s)[0]
    pair_up = pair_start("up", [g_up])
    dh2 = _mm_plain("mlp_up_bwd", du, w_up_f, "nt", F32, after=pair_up[4])
    flight_mlp = scatter_start("mlp", [pair_up, pair_down], dh2)
    dx1, dmix, acc_mid = _pre_mlp_and_post_mix_bwd(dh2, x1, dout, mix, _tie(g_pre_mlp, flight_mlp[4]), sc_m,
                                                   g_post_mix, gt_a)

    dmixcat = _mm_plain("out_proj_bwd", dmix, w_out_f, "nt", F32)
    g_out = _mm_plain("grad_w_out", mixcat, dmix, "tn", BF16)

    fdq, fdk, fdv, dcum_row, dcum_q = _fox_bwd(proj, fox_o, dmixcat, fox_lse, cum_row, n_fox)
    dcum_k = jnp.pad(dcum_row.reshape(n_fox, S), ((0, LANES - n_fox), (0, 0)))
    dfg, db_forget = _fox_gate_bwd(dcum_k, dcum_q, fg, b_pad)

    group_w = (n_swa // n_kv) * HEAD_DIM
    sdq, sdk, sdv, dsink = _swa_bwd(rq, proj, v_first, sinks, swa_o, dmixcat, fox_w // group_w, swa_lse, n_swa, n_kv)
    drq = jnp.concatenate([sdq, jnp.transpose(sdk, (1, 0, 2)).reshape(S, kv_w).astype(BF16)], axis=1)
    d_sq_sk = _rope("rope_bwd", drq, 0, n_swa + n_kv, cos, -sin_signed)
    dsv = jnp.transpose(sdv, (1, 0, 2)).reshape(S, kv_w).astype(BF16)
    dproj = jnp.concatenate([fdq, fdk, fdv, d_sq_sk, dsv], axis=1)

    pieces = []
    for s in range(N_CHIPS):
        lo, hi = s * in_rows, (s + 1) * in_rows
        for src, first, last, shift in [(dproj, 0, gate_lo, 0), (dfg, gate_lo, gate_lo + n_fox, gate_lo),
                                        (dproj, gate_lo + n_fox, in_w, n_fox)]:
            if max(lo, first) < min(hi, last):
                pieces.append(src[:, max(lo, first) - shift:min(hi, last) - shift])
        pieces.append(jnp.zeros((S, in_rows_pad - in_rows), BF16))
    dproj_slab = jnp.concatenate(pieces, axis=1)

    g_in_t = _mm_plain("grad_w_in", dproj_slab, h, "tn", BF16, tm=_fit(512, slab_w))
    pair_mix = pair_start("mix", [g_in_t.reshape(N_CHIPS, in_rows_pad, D), g_out.reshape(N_CHIPS, D // N_CHIPS, D)])
    dh = _mm_plain("in_proj_bwd", dproj_slab, w_slab_t, "nn", F32, after=pair_mix[4], tk=_fit(2560, slab_w))
    grad_x, acc_pre = _pre_mix_bwd(dh, x2, dx1, g_pre_mix, sc_a)

    zero_row = jnp.zeros((1, D), F32)
    tail = jnp.concatenate([db_forget[0:1, :n_fox], dsink[:, 0, :n_swa // n_kv].reshape(1, n_swa),
                            loss_part[0:1, 0:1], jnp.zeros((1, D - n_fox - n_swa - 1), F32)], axis=1)
    partial = jnp.concatenate([
        acc_pre[0:1], acc_pre[1:2], acc_mid[3:4], acc_mid[0:1], acc_mid[1:2], acc_mlp_post[0:1],
        acc_pre[2:3], acc_mid[4:5], acc_mid[2:3], acc_mlp_post[1:2], tail] + [zero_row] * 5, axis=0)
    gathered_small, token = _allgather8("gather_small_grads", partial)

    flight_mix = scatter_start("mix", [pair_mix], token)
    halves_mlp = scatter_finish("mlp", flight_mlp, flight_mix[4])
    share_mlp = _ici_start("grad_share_start_mlp", halves_mlp,
                           [jax.ShapeDtypeStruct(hv.shape, F32) for hv in halves_mlp], _share_plan, per_source=1)

    def pack(bm, gpm, gqm, gpl, gql, bf, sk):
        last = jnp.concatenate([bf, sk, jnp.zeros((1, D - n_fox - n_swa), F32)], axis=1)
        return jnp.concatenate([bm.reshape(N_MOD, D), gpm, gqm, gpl, gql, last, jnp.zeros((5, D), F32)], axis=0)

    def unpack(p):
        return {"b_mod": p[0:N_MOD].reshape(1, N_MOD * D), "g_pre_mix": p[6:7], "g_post_mix": p[7:8],
                "g_pre_mlp": p[8:9], "g_post_mlp": p[9:10], "b_forget": p[10:11, :n_fox],
                "swa_sinks": p[10:11, n_fox:n_fox + n_swa]}

    small_out = _small_update(
        gathered_small, _tie(pack(b_mod, g_pre_mix, g_post_mix, g_pre_mlp, g_post_mlp, b_forget, swa_sinks), share_mlp[4]),
        pack(m_b_mod, m_g_pre_mix, m_g_post_mix, m_g_pre_mlp, m_g_post_mlp, m_b_forget, m_swa_sinks),
        pack(v_b_mod, v_g_pre_mix, v_g_post_mix, v_g_pre_mlp, v_g_post_mlp, v_b_forget, v_swa_sinks))
    g_small, d_small, m_small, v_small = [unpack(p) for p in small_out]
    loss = small_out[0][N_MOD + 4, n_fox + n_swa]

    dmod_all = gathered_small.reshape(N_DEV, 16, D)[:, :N_MOD].reshape(N_DEV, N_MOD * D)
    dmod_shard = _tie(lax.dynamic_slice_in_dim(dmod_all, chip * mod_cols, mod_cols, axis=1), share_mlp[4])
    g_w_mod, d_w_mod, nm_w_mod, nv_w_mod = _mod_update(c_all.T, dmod_shard, w_mod[0], m_w_mod[0], v_w_mod[0])
    send, recv, halves_mlp, lands, _ = share_mlp
    halves_mlp, others_mlp = _ici_wait("grad_share_wait_mlp", send, recv, halves_mlp, lands, _share_plan,
                                       d_w_mod[:8, :LANES] + small_out[1][:8, :LANES])

    grads = dict(g_small, w_mod=g_w_mod[None])
    deltas = dict(d_small, w_mod=d_w_mod[None])
    new_m = dict(m_small, w_mod=nm_w_mod[None])
    new_v = dict(v_small, w_mod=nv_w_mod[None])
    weights = {"w_in": (w_in, m_w_in, v_w_in), "w_out": (w_out, m_w_out, v_w_out), "w_up": (w_up, m_w_up, v_w_up),
               "w_down": (w_down, m_w_down, v_w_down)}

    def big_update(n, own, other):
        transposed = n == "w_in"
        w, m, v = in_state if transposed else [a[0] for a in weights[n]]
        outs = _adam_halves("adam_" + n, core_arr, w, own, other, m, v)
        if transposed:
            outs = [o[:in_rows].T for o in outs]
        grads[n], deltas[n], new_m[n], new_v[n] = [o[None] for o in outs]

    big_update("w_up", halves_mlp[0], others_mlp[0])
    big_update("w_down", halves_mlp[1], others_mlp[1])
    ran = deltas["w_down"][0, :8, :LANES] + deltas["w_up"][0, :8, :LANES] + d_w_mod[:8, :LANES]
    halves_mix = scatter_finish("mix", flight_mix, ran)
    others_mix = _pair_share("grad_pair_share_mix", halves_mix)
    big_update("w_in", halves_mix[0], others_mix[0])
    big_update("w_out", halves_mix[1], others_mix[1])

    order = ["w_mod", "b_mod", "g_pre_mix", "g_post_mix", "w_in", "b_forget", "swa_sinks", "w_out", "g_pre_mlp",
             "g_post_mlp", "w_up", "w_down"]
    return (loss, grad_x[None], *[grads[n] for n in order], *[deltas[n] for n in order],
            *[new_m[n] for n in order], *[new_v[n] for n in order])
```

```python
import jax
import jax.numpy as jnp
from jax import lax
from jax.experimental import pallas as pl
from jax.experimental.pallas import tpu as pltpu

F32 = jnp.float32
BF16 = jnp.bfloat16
MESH = pl.DeviceIdType.MESH

HEAD_DIM = 128
SWA_BLOCK = 128
ROPE_THETA = 10000.0
NORM_EPS = 1e-6
NEG = -1e30
N_MOD = 6
ADAM_LR = 0.001
ADAM_B1 = 0.9
ADAM_B2 = 0.999
ADAM_EPS = 1e-08
ADAM_WD = 0.01
ADAM_STEP = 10
N_CHIPS = 4
N_DEV = 8
LANES = 128
VMEM_CAP = 60 * 1024 * 1024

_NN = (((1,), (0,)), ((), ()))
_NT = (((1,), (1,)), ((), ()))
_TN = (((0,), (0,)), ((), ()))


def _vmem(nbytes):
    return int(min(VMEM_CAP, nbytes * 5 // 4 + (4 << 20)))


def _nbytes(shape, dtype):
    n = 1
    for s in shape:
        n *= s
    return n * jnp.dtype(dtype).itemsize


def _fit(t, n):
    t = min(t, n)
    assert n % t == 0, (t, n)
    return t


MM_TM, MM_TN, MM_TK = 512, 1024, 2048


def _matmul(name, a, b, mode, out_defs, epilogue, extras=(), tm=MM_TM, tn=MM_TN, tk=MM_TK, revisits=False,
            row_sel=None):
    if mode == "nn":
        (M, K), (K2, N) = a.shape, b.shape
    elif mode == "nt":
        (M, K), (N, K2) = a.shape, b.shape
    else:
        (K, M), (K2, N) = a.shape, b.shape
    assert K == K2, (a.shape, b.shape, mode)
    tm, tn, tk = _fit(tm, M), _fit(tn, N), _fit(tk, K)
    nk = K // tk
    dims = {"nn": _NN, "nt": _NT, "tn": _TN}[mode]
    if row_sel is None:
        grid_m, a_row = M // tm, lambda i, *sel: i
    else:
        grid_m, a_row = row_sel[2], lambda i, *sel: row_sel[1](i, sel[0])
    a_spec = (pl.BlockSpec((tk, tm), lambda i, j, k, *sel: (k, a_row(i, *sel))) if mode == "tn"
              else pl.BlockSpec((tm, tk), lambda i, j, k, *sel: (a_row(i, *sel), k)))
    b_spec = (pl.BlockSpec((tn, tk), lambda i, j, k, *sel: (j, k)) if mode == "nt"
              else pl.BlockSpec((tk, tn), lambda i, j, k, *sel: (k, j)))
    n_ex, n_out = len(extras), len(out_defs)

    def body(*refs):
        if row_sel is not None:
            refs = refs[1:]
        a_ref, b_ref = refs[0], refs[1]
        ex = refs[2:2 + n_ex]
        outs = refs[2 + n_ex:2 + n_ex + n_out]
        prod = lax.dot_general(a_ref[...], b_ref[...], dims, preferred_element_type=F32)
        if nk == 1:
            epilogue(prod, ex, outs)
        else:
            acc_ref = refs[-1]
            k = pl.program_id(2)

            @pl.when(k == 0)
            def _():
                acc_ref[...] = prod

            @pl.when(k > 0)
            def _():
                acc_ref[...] += prod

            @pl.when(k == nk - 1)
            def _():
                epilogue(acc_ref[...], ex, outs)

    def wrap(f):
        return lambda i, j, k, *sel: f(i, j)

    in_specs = [a_spec, b_spec] + [pl.BlockSpec(blk, wrap(f)) for _, blk, f in extras]
    out_specs = [pl.BlockSpec(blk, wrap(f)) for _, _, blk, f in out_defs]
    out_shape = [jax.ShapeDtypeStruct(s, d) for s, d, _, _ in out_defs]
    need = 2 * (tm * tk + tk * tn) * a.dtype.itemsize + 3 * tm * tn * 4
    need += sum(2 * _nbytes(blk, arr.dtype) for arr, blk, _ in extras)
    need += sum(2 * _nbytes(blk, d) for _, d, blk, _ in out_defs)
    grid = (grid_m, N // tn, nk)
    scratch = [pltpu.VMEM((tm, tn), F32)] if nk > 1 else []
    params = pltpu.CompilerParams(
        dimension_semantics=("parallel", "arbitrary" if revisits else "parallel", "arbitrary"),
        vmem_limit_bytes=_vmem(need))
    operands = (a, b, *[arr for arr, _, _ in extras])
    if row_sel is None:
        return pl.pallas_call(body, name=name, grid=grid, in_specs=in_specs, out_specs=out_specs, out_shape=out_shape,
                              scratch_shapes=scratch, compiler_params=params)(*operands)
    grid_spec = pltpu.PrefetchScalarGridSpec(num_scalar_prefetch=1, grid=grid, in_specs=in_specs, out_specs=out_specs,
                                             scratch_shapes=scratch)
    return pl.pallas_call(body, name=name, grid_spec=grid_spec, out_shape=out_shape,
                          compiler_params=params)(row_sel[0], *operands)


def _grad_half(name, core, a, b, row_slabs, col_slabs, tm, other, recv=None, after=None):
    (_, M), (_, N) = a.shape, b.shape
    H = M // (2 * row_slabs)
    nh = H // tm
    tn = _fit(MM_TN, N // col_slabs)
    per = N // col_slabs // tn

    def a_block(i, core_ref):
        half = (1 - core_ref[0]) if other else core_ref[0]
        return (i // nh) * (2 * nh) + half * nh + i % nh

    def out_index(i, j):
        return (j // per, i, j % per) if col_slabs > 1 else (i // nh, i % nh, j)

    slabs = max(row_slabs, col_slabs)
    out_def = ((slabs, H, N // col_slabs), BF16, (1, tm, tn), out_index)

    def epilogue(acc, ex, outs):
        outs[0][0] = (acc if recv is None else acc + ex[0][0].astype(F32)).astype(BF16)

    extras = ([] if recv is None else [(recv, (1, tm, tn), out_index)]) + ([] if after is None else [_behind(after)])
    return _matmul(name, a, b, "tn", [out_def], epilogue, extras=extras, tm=tm, tn=tn,
                   row_sel=(core, a_block, row_slabs * nh))[0]


def _behind(token):
    return (token, (8, LANES), lambda i, j: (0, 0))


def _mm_plain(name, a, b, mode, out_dtype, after=None, **tiles):
    if mode == "nn":
        M, N = a.shape[0], b.shape[1]
    elif mode == "nt":
        M, N = a.shape[0], b.shape[0]
    else:
        M, N = a.shape[1], b.shape[1]
    tm, tn = _fit(tiles.get("tm", MM_TM), M), _fit(tiles.get("tn", MM_TN), N)

    def epi(acc, ex, outs):
        outs[0][...] = acc.astype(out_dtype)

    return _matmul(name, a, b, mode, [((M, N), out_dtype, (tm, tn), lambda i, j: (i, j))], epi,
                   extras=[] if after is None else [_behind(after)], **tiles)[0]


def _rstd(v):
    return lax.rsqrt(jnp.mean(v * v, axis=-1, keepdims=True) + NORM_EPS)


def _row_call(name, body, row_ins, vec_ins, row_outs, acc_outs, S, D, tr):
    tr = _fit(tr, S)
    row_spec = pl.BlockSpec((tr, D), lambda r: (r, 0))
    vec_spec = pl.BlockSpec((1, D), lambda r: (0, 0))
    in_specs = [row_spec] * len(row_ins) + [vec_spec] * len(vec_ins)
    out_specs = [row_spec] * len(row_outs) + [pl.BlockSpec(shp, lambda r: (0, 0)) for shp in acc_outs]
    out_shape = [jax.ShapeDtypeStruct((S, D), d) for d in row_outs] + [jax.ShapeDtypeStruct(shp, F32) for shp in acc_outs]
    need = sum(2 * tr * D * a.dtype.itemsize for a in row_ins) + sum(2 * tr * D * jnp.dtype(d).itemsize for d in row_outs)
    need += 8 * tr * D * 4
    return pl.pallas_call(
        body, name=name, grid=(S // tr,), in_specs=in_specs, out_specs=out_specs, out_shape=out_shape,
        compiler_params=pltpu.CompilerParams(dimension_semantics=("arbitrary",), vmem_limit_bytes=_vmem(need)),
    )(*row_ins, *vec_ins)


def _acc_rows(ref, rows):
    @pl.when(pl.program_id(0) == 0)
    def _():
        ref[...] = jnp.zeros_like(ref)
    for n, r in enumerate(rows):
        ref[n:n + 1, :] += r


def _pre_norm(x, g, sc, sh):
    S, D = x.shape

    def body(x_ref, g_ref, sc_ref, sh_ref, h_ref):
        xv = x_ref[...]
        xn = xv * _rstd(xv)
        h_ref[...] = (xn * g_ref[...] * (1.0 + sc_ref[...]) + sh_ref[...]).astype(BF16)

    return _row_call("pre_norm_mix", body, [x], [g, sc, sh], [BF16], [], S, D, 256)[0]


def _post_mix(x, mix, g_post, gt, g_pre, sc, sh):
    S, D = x.shape

    def body(x_ref, mix_ref, gp_ref, gt_ref, g2_ref, sc_ref, sh_ref, x1_ref, h2_ref):
        mv = mix_ref[...]
        x1 = x_ref[...] + gt_ref[...] * (mv * _rstd(mv) * gp_ref[...])
        x1_ref[...] = x1
        h2_ref[...] = (x1 * _rstd(x1) * g2_ref[...] * (1.0 + sc_ref[...]) + sh_ref[...]).astype(BF16)

    return _row_call("post_mix_pre_mlp", body, [x, mix], [g_post, gt, g_pre, sc, sh], [F32, BF16], [], S, D, 256)


def _loss_and_post_mlp_bwd(x1, y, target, g_post, gt):
    S, D = x1.shape

    def body(x1_ref, y_ref, t_ref, g_ref, gt_ref, dy_ref, dout_ref, loss_ref, acc_ref):
        yv = y_ref[...]
        r = _rstd(yv)
        yh = yv * r
        n = yh * g_ref[...]
        diff = x1_ref[...] + gt_ref[...] * n - t_ref[...]
        dout = diff * (1.0 / D)
        dout_ref[...] = dout
        dn = dout * gt_ref[...]
        dyh = dn * g_ref[...]
        dy_ref[...] = (r * (dyh - yh * jnp.mean(dyh * yh, axis=-1, keepdims=True))).astype(BF16)
        _acc_rows(acc_ref, [jnp.sum(dout * n, axis=0, keepdims=True), jnp.sum(dn * yh, axis=0, keepdims=True)])

        @pl.when(pl.program_id(0) == 0)
        def _():
            loss_ref[...] = jnp.zeros_like(loss_ref)
        loss_ref[...] += jnp.full(loss_ref.shape, (0.5 / D) * jnp.sum(diff * diff), F32)

    return _row_call("loss_post_mlp_bwd", body, [x1, y, target], [g_post, gt], [BF16, F32],
                     [(8, LANES), (8, D)], S, D, 128)


def _pre_mlp_and_post_mix_bwd(dh2, x1, dout, mix, g_pre, sc, g_post, gt):
    S, D = x1.shape

    def body(dh_ref, x1_ref, dout_ref, mix_ref, g_ref, sc_ref, gp_ref, gt_ref, dx1_ref, dmix_ref, acc_ref):
        dh = dh_ref[...]
        x1v = x1_ref[...]
        r3 = _rstd(x1v)
        xn = x1v * r3
        dxn = dh * (1.0 + sc_ref[...]) * g_ref[...]
        dx1 = dout_ref[...] + r3 * (dxn - xn * jnp.mean(dxn * xn, axis=-1, keepdims=True))
        dx1_ref[...] = dx1
        mv = mix_ref[...]
        r2 = _rstd(mv)
        mh = mv * r2
        dn = dx1 * gt_ref[...]
        dmh = dn * gp_ref[...]
        dmix_ref[...] = (r2 * (dmh - mh * jnp.mean(dmh * mh, axis=-1, keepdims=True))).astype(BF16)
        _acc_rows(acc_ref, [
            jnp.sum(dh, axis=0, keepdims=True),
            jnp.sum(dh * xn * g_ref[...], axis=0, keepdims=True),
            jnp.sum(dh * (1.0 + sc_ref[...]) * xn, axis=0, keepdims=True),
            jnp.sum(dx1 * mh * gp_ref[...], axis=0, keepdims=True),
            jnp.sum(dn * mh, axis=0, keepdims=True)])

    return _row_call("pre_mlp_post_mix_bwd", body, [dh2, x1, dout, mix], [g_pre, sc, g_post, gt], [F32, BF16],
                     [(8, D)], S, D, 128)


def _pre_mix_bwd(dh, x, dx1, g_pre, sc):
    S, D = x.shape

    def body(dh_ref, x_ref, dx1_ref, g_ref, sc_ref, gx_ref, acc_ref):
        dhv = dh_ref[...]
        xv = x_ref[...]
        r = _rstd(xv)
        xn = xv * r
        dxn = dhv * (1.0 + sc_ref[...]) * g_ref[...]
        gx_ref[...] = dx1_ref[...] + r * (dxn - xn * jnp.mean(dxn * xn, axis=-1, keepdims=True))
        _acc_rows(acc_ref, [
            jnp.sum(dhv, axis=0, keepdims=True),
            jnp.sum(dhv * xn * g_ref[...], axis=0, keepdims=True),
            jnp.sum(dhv * (1.0 + sc_ref[...]) * xn, axis=0, keepdims=True)])

    return _row_call("pre_mix_bwd", body, [dh, x, dx1], [g_pre, sc], [F32], [(8, D)], S, D, 128)


CUM_BLOCK = 256


def _tri(n, upper):
    r = lax.broadcasted_iota(jnp.int32, (n, n), 0)
    c = lax.broadcasted_iota(jnp.int32, (n, n), 1)
    return ((c >= r) if upper else (c <= r)).astype(F32)


def _fox_gate_fwd(fg, b_pad):
    S = fg.shape[0]
    cb = _fit(CUM_BLOCK, S)

    def body(fg_ref, b_ref, cumt_ref, cum_ref):
        low = _tri(cb, False)
        carry = jnp.zeros((1, LANES), F32)
        for n in range(S // cb):
            z = fg_ref[n * cb:(n + 1) * cb, :] + b_ref[...]
            logf = jnp.minimum(z, 0.0) - jnp.log(1.0 + jnp.exp(-jnp.abs(z)))
            blk = jnp.dot(low, logf, precision=lax.Precision.HIGHEST, preferred_element_type=F32) + carry
            cum_ref[n * cb:(n + 1) * cb, :] = blk
            carry = blk[cb - 1:cb, :]
        cumt_ref[...] = cum_ref[...].T

    return pl.pallas_call(
        body, name="fox_gate_fwd", out_shape=jax.ShapeDtypeStruct((LANES, S), F32),
        scratch_shapes=[pltpu.VMEM((S, LANES), F32)],
        compiler_params=pltpu.CompilerParams(vmem_limit_bytes=_vmem(6 * S * LANES * 4)),
    )(fg, b_pad)


def _fox_gate_bwd(dcum_k, dcum_q, fg, b_pad):
    S = fg.shape[0]
    n_fox = dcum_q.shape[0]
    cb = _fit(CUM_BLOCK, S)

    def body(dk_ref, dq_ref, fg_ref, b_ref, dfg_ref, db_ref, dc_ref):
        lane = lax.broadcasted_iota(jnp.int32, (S, LANES), 1)
        dc = dk_ref[...].T
        for h in range(n_fox):
            dc = dc + jnp.where(lane == h, dq_ref[h], 0.0)
        dc_ref[...] = dc
        up = _tri(cb, True)
        carry = jnp.zeros((1, LANES), F32)
        db = jnp.zeros((1, LANES), F32)
        for n in reversed(range(S // cb)):
            blk = jnp.dot(up, dc_ref[n * cb:(n + 1) * cb, :], precision=lax.Precision.HIGHEST,
                          preferred_element_type=F32) + carry
            carry = blk[0:1, :]
            z = fg_ref[n * cb:(n + 1) * cb, :] + b_ref[...]
            dfg = blk * (1.0 / (1.0 + jnp.exp(z)))
            dfg_ref[n * cb:(n + 1) * cb, :] = dfg.astype(BF16)
            db = db + jnp.sum(dfg, axis=0, keepdims=True)
        db_ref[...] = jnp.broadcast_to(db, db_ref.shape)

    return pl.pallas_call(
        body, name="fox_gate_bwd",
        out_shape=[jax.ShapeDtypeStruct((S, LANES), BF16), jax.ShapeDtypeStruct((8, LANES), F32)],
        scratch_shapes=[pltpu.VMEM((S, LANES), F32)],
        compiler_params=pltpu.CompilerParams(vmem_limit_bytes=_vmem((8 + 2 * n_fox) * S * LANES * 4)),
    )(dcum_k, dcum_q, fg, b_pad)


FOX_TILE = 512


LOG2E = 1.4426950408889634


def _fox_scores(q, k, ck2, masked, t):
    s = lax.dot_general(q, k, _NT, preferred_element_type=F32) * (HEAD_DIM ** -0.5 * LOG2E) - ck2
    if masked:
        row = lax.broadcasted_iota(jnp.int32, (t, t), 0)
        col = lax.broadcasted_iota(jnp.int32, (t, t), 1)
        s = jnp.where(col <= row, s, NEG)
    return s


def _fox_fwd(proj, cum_row, n_fox):
    S = proj.shape[0]
    t = _fit(FOX_TILE, S)
    nq = S // t

    def body(q_ref, k_ref, v_ref, ck_ref, o_ref, lse_ref):
        def q_block(qi, _):
            q0 = pl.multiple_of(qi * t, t)
            q = q_ref[pl.ds(q0, t), :]

            def kv_block(j, carry, masked):
                m, l, acc = carry
                k0 = pl.multiple_of(j * t, t)
                s = _fox_scores(q, k_ref[pl.ds(k0, t), :], ck_ref[0, :, pl.ds(k0, t)] * LOG2E, masked, t)
                m_new = jnp.maximum(m, jnp.max(s, axis=-1, keepdims=True))
                alpha = jnp.exp2(m - m_new)
                p = jnp.exp2(s - m_new)
                l = alpha * l + jnp.sum(p, axis=-1, keepdims=True)
                acc = alpha * acc + jnp.dot(p.astype(BF16), v_ref[pl.ds(k0, t), :], preferred_element_type=F32)
                return m_new, l, acc

            init = (jnp.full((t, 1), NEG, F32), jnp.zeros((t, 1), F32), jnp.zeros((t, HEAD_DIM), F32))
            carry = lax.fori_loop(0, qi, lambda j, cr: kv_block(j, cr, False), init)
            m, l, acc = kv_block(qi, carry, True)
            o_ref[pl.ds(q0, t), :] = acc / l
            lse_ref[0, pl.ds(q0, t), :] = jnp.broadcast_to(m + jnp.log(l) * LOG2E, (t, LANES))
            return 0

        lax.fori_loop(0, nq, q_block, 0)

    col = lambda off: pl.BlockSpec((S, HEAD_DIM), lambda h: (0, off + h))
    per_head = pl.BlockSpec((1, S, LANES), lambda h: (h, 0, 0))
    return pl.pallas_call(
        body, name="fox_fwd", grid=(n_fox,),
        in_specs=[col(0), col(n_fox), col(2 * n_fox), pl.BlockSpec((1, 1, S), lambda h: (h, 0, 0))],
        out_specs=[pl.BlockSpec((S, HEAD_DIM), lambda h: (0, h)), per_head],
        out_shape=[jax.ShapeDtypeStruct((S, n_fox * HEAD_DIM), F32), jax.ShapeDtypeStruct((n_fox, S, LANES), F32)],
        compiler_params=pltpu.CompilerParams(dimension_semantics=("parallel",),
                                             vmem_limit_bytes=_vmem(16 * S * HEAD_DIM * 4 + 12 * t * t * 4)),
    )(proj, proj, proj, cum_row)


def _fox_bwd(proj, o, do, lse_b, cum_row, n_fox):
    S = proj.shape[0]
    t = _fit(FOX_TILE, S)
    nq = S // t
    scale = HEAD_DIM ** -0.5

    def body(q_ref, k_ref, v_ref, o_ref, do_ref, lse_ref, ck_ref, dq_ref, dk_ref, dv_ref, dc_ref, dcq_ref,
             dq_acc, delta_ref):
        dq_acc[...] = jnp.zeros_like(dq_acc)
        dcq_ref[...] = jnp.zeros_like(dcq_ref)

        def delta_block(qi, _):
            q0 = pl.multiple_of(qi * t, t)
            d = jnp.sum(do_ref[pl.ds(q0, t), :] * o_ref[pl.ds(q0, t), :], axis=-1, keepdims=True)
            delta_ref[pl.ds(q0, t), :] = jnp.broadcast_to(d, (t, LANES))
            return 0

        lax.fori_loop(0, nq, delta_block, 0)

        def kv_block(j, _):
            k0 = pl.multiple_of(j * t, t)
            k = k_ref[pl.ds(k0, t), :]
            v = v_ref[pl.ds(k0, t), :]
            ck2 = ck_ref[0, :, pl.ds(k0, t)] * LOG2E

            def q_block(qi, carry, masked):
                dk, dv, dc = carry
                q0 = pl.multiple_of(qi * t, t)
                q = q_ref[pl.ds(q0, t), :]
                dov = do_ref[pl.ds(q0, t), :].astype(BF16)
                p = jnp.exp2(_fox_scores(q, k, ck2, masked, t) - lse_ref[0, pl.ds(q0, t), :][:, :1])
                dp = lax.dot_general(dov, v, _NT, preferred_element_type=F32)
                ds = p * (dp - delta_ref[pl.ds(q0, t), :][:, :1])
                dsb = ds.astype(BF16)
                dv = dv + lax.dot_general(p.astype(BF16), dov, _TN, preferred_element_type=F32)
                dk = dk + lax.dot_general(dsb, q, _TN, preferred_element_type=F32)
                dq_acc[pl.ds(q0, t), :] += jnp.dot(dsb, k, preferred_element_type=F32)
                dc = dc - jnp.sum(ds, axis=0, keepdims=True)
                dcq_ref[0, pl.ds(q0, t), :] += jnp.broadcast_to(jnp.sum(ds, axis=1, keepdims=True), (t, LANES))
                return dk, dv, dc

            init = (jnp.zeros((t, HEAD_DIM), F32), jnp.zeros((t, HEAD_DIM), F32), jnp.zeros((1, t), F32))
            carry = q_block(j, init, True)
            dk, dv, dc = lax.fori_loop(j + 1, nq, lambda qi, cr: q_block(qi, cr, False), carry)
            dk_ref[pl.ds(k0, t), :] = (dk * scale).astype(BF16)
            dv_ref[pl.ds(k0, t), :] = dv.astype(BF16)
            dc_ref[0, :, pl.ds(k0, t)] = dc
            return 0

        lax.fori_loop(0, nq, kv_block, 0)
        dq_ref[...] = (dq_acc[...] * scale).astype(BF16)

    col = lambda off: pl.BlockSpec((S, HEAD_DIM), lambda h: (0, off + h))
    per_head = pl.BlockSpec((1, S, LANES), lambda h: (h, 0, 0))
    row = pl.BlockSpec((1, 1, S), lambda h: (h, 0, 0))
    grad = jax.ShapeDtypeStruct((S, n_fox * HEAD_DIM), BF16)
    return pl.pallas_call(
        body, name="fox_bwd", grid=(n_fox,),
        in_specs=[col(0), col(n_fox), col(2 * n_fox), col(0), col(0), per_head, row],
        out_specs=[col(0), col(0), col(0), row, per_head],
        out_shape=[grad, grad, grad, jax.ShapeDtypeStruct((n_fox, 1, S), F32), jax.ShapeDtypeStruct((n_fox, S, LANES), F32)],
        scratch_shapes=[pltpu.VMEM((S, HEAD_DIM), F32), pltpu.VMEM((S, LANES), F32)],
        compiler_params=pltpu.CompilerParams(dimension_semantics=("parallel",),
                                             vmem_limit_bytes=_vmem(24 * S * HEAD_DIM * 4 + 16 * t * t * 4)),
    )(proj, proj, proj, o, do, lse_b, cum_row)


def _rope_tables(S):
    half = HEAD_DIM // 2
    inv_freq = 1.0 / (ROPE_THETA ** (jnp.arange(half, dtype=F32) * (2.0 / HEAD_DIM)))
    ang = jnp.arange(S).astype(F32)[:, None] * inv_freq[None, :]
    cos, sin = jnp.cos(ang), jnp.sin(ang)
    return jnp.concatenate([cos, cos], axis=-1), jnp.concatenate([-sin, sin], axis=-1)


def _rope(name, src, first_block, n_blocks, cos, sin_signed):
    S = src.shape[0]

    def body(x_ref, cos_ref, sin_ref, o_ref):
        xv = x_ref[...].astype(F32)
        o_ref[...] = (xv * cos_ref[...] + pltpu.roll(xv, HEAD_DIM // 2, 1) * sin_ref[...]).astype(BF16)

    table = pl.BlockSpec((S, HEAD_DIM), lambda n: (0, 0))
    return pl.pallas_call(
        body, name=name, grid=(n_blocks,),
        in_specs=[pl.BlockSpec((S, HEAD_DIM), lambda n: (0, first_block + n)), table, table],
        out_specs=pl.BlockSpec((S, HEAD_DIM), lambda n: (0, n)),
        out_shape=jax.ShapeDtypeStruct((S, n_blocks * HEAD_DIM), BF16),
        compiler_params=pltpu.CompilerParams(dimension_semantics=("parallel",),
                                             vmem_limit_bytes=_vmem(12 * S * HEAD_DIM * 4)),
    )(src, cos, sin_signed)


def _swa_tile(q_ref, kp_ref, kc_ref, n, group, scale):
    B = SWA_BLOCK
    qs = jnp.concatenate([q_ref[:, g * HEAD_DIM:(g + 1) * HEAD_DIM] for g in range(group)], axis=0)
    kcat = jnp.concatenate([kp_ref[...], kc_ref[...]], axis=0)
    s = lax.dot_general(qs, kcat, _NT, preferred_element_type=F32) * scale
    qi = lax.broadcasted_iota(jnp.int32, (group * B, 2 * B), 0) % B
    kj = lax.broadcasted_iota(jnp.int32, (group * B, 2 * B), 1)
    diff = qi + B - kj
    mask = (diff >= 0) & (diff < B) & ((n * B + kj - B) >= 0)
    return qs, kcat, jnp.where(mask, s, NEG)


def _swa_sink_col(sink_ref, kv, group):
    head = lax.broadcasted_iota(jnp.int32, (group * SWA_BLOCK, 1), 0) // SWA_BLOCK
    col = jnp.zeros((group * SWA_BLOCK, 1), F32)
    for g in range(group):
        col = jnp.where(head == g, sink_ref[kv * group + g], col)
    return col


def _swa_specs(n_kv, group, q_first, k_first, v_first):
    B = SWA_BLOCK
    prev = lambda n: jnp.maximum(n - 1, 0)
    return [
        pl.BlockSpec((B, group * HEAD_DIM), lambda kv, n: (n, q_first + kv)),
        pl.BlockSpec((B, HEAD_DIM), lambda kv, n: (prev(n), k_first + kv)),
        pl.BlockSpec((B, HEAD_DIM), lambda kv, n: (n, k_first + kv)),
        pl.BlockSpec((B, HEAD_DIM), lambda kv, n: (prev(n), v_first + kv)),
        pl.BlockSpec((B, HEAD_DIM), lambda kv, n: (n, v_first + kv)),
    ]


def _swa_fwd(rq, proj, v_first, sinks, n_q, n_kv):
    S = rq.shape[0]
    B = SWA_BLOCK
    group = n_q // n_kv
    scale = HEAD_DIM ** -0.5

    def body(q_ref, kp_ref, kc_ref, vp_ref, vc_ref, sink_ref, o_ref, lse_ref):
        kv, n = pl.program_id(0), pl.program_id(1)
        _, _, s = _swa_tile(q_ref, kp_ref, kc_ref, n, group, scale)
        sink = _swa_sink_col(sink_ref, kv, group)
        m = jnp.maximum(jnp.max(s, axis=-1, keepdims=True), sink)
        p = jnp.exp(s - m)
        denom = jnp.sum(p, axis=-1, keepdims=True) + jnp.exp(sink - m)
        vcat = jnp.concatenate([vp_ref[...], vc_ref[...]], axis=0)
        o = jnp.dot((p / denom).astype(BF16), vcat, preferred_element_type=F32)
        lse = m + jnp.log(denom)
        for g in range(group):
            o_ref[:, g * HEAD_DIM:(g + 1) * HEAD_DIM] = o[g * B:(g + 1) * B, :]
            lse_ref[0, :, g * LANES:(g + 1) * LANES] = jnp.broadcast_to(lse[g * B:(g + 1) * B, :], (B, LANES))

    specs = _swa_specs(n_kv, group, 0, n_q, v_first)
    q_blk = pl.BlockSpec((B, group * HEAD_DIM), lambda kv, n: (n, kv))
    return pl.pallas_call(
        body, name="swa_fwd", grid=(n_kv, S // B),
        in_specs=specs + [pl.BlockSpec(memory_space=pltpu.SMEM)],
        out_specs=[q_blk, pl.BlockSpec((1, B, group * LANES), lambda kv, n: (kv, n, 0))],
        out_shape=[jax.ShapeDtypeStruct((S, n_q * HEAD_DIM), F32), jax.ShapeDtypeStruct((n_kv, S, group * LANES), F32)],
        compiler_params=pltpu.CompilerParams(dimension_semantics=("parallel", "arbitrary")),
    )(rq, rq, rq, proj, proj, sinks)


def _swa_bwd(rq, proj, v_first, sinks, o, do, do_first, lse_b, n_q, n_kv):
    S = rq.shape[0]
    B = SWA_BLOCK
    group = n_q // n_kv
    scale = HEAD_DIM ** -0.5

    def body(q_ref, kp_ref, kc_ref, vp_ref, vc_ref, o_ref, do_ref, lse_ref, sink_ref,
             dq_ref, dk_ref, dv_ref, dsink_ref):
        kv, n = pl.program_id(0), pl.program_id(1)

        @pl.when(n == 0)
        def _():
            dk_ref[...] = jnp.zeros_like(dk_ref)
            dv_ref[...] = jnp.zeros_like(dv_ref)
            dsink_ref[...] = jnp.zeros_like(dsink_ref)

        qs, kcat, s = _swa_tile(q_ref, kp_ref, kc_ref, n, group, scale)
        sink = _swa_sink_col(sink_ref, kv, group)
        stack = lambda ref, w: jnp.concatenate([ref[:, g * w:(g + 1) * w] for g in range(group)], axis=0)
        lse = jnp.concatenate([lse_ref[0, :, g * LANES:g * LANES + 1] for g in range(group)], axis=0)
        do32 = stack(do_ref, HEAD_DIM)
        delta = jnp.sum(do32 * stack(o_ref, HEAD_DIM), axis=-1, keepdims=True)
        dov = do32.astype(BF16)
        p = jnp.exp(s - lse)
        vcat = jnp.concatenate([vp_ref[...], vc_ref[...]], axis=0)
        dp = lax.dot_general(dov, vcat, _NT, preferred_element_type=F32)
        ds = p * (dp - delta)
        dsb = ds.astype(BF16)
        dq = jnp.dot(dsb, kcat, preferred_element_type=F32) * scale
        for g in range(group):
            dq_ref[:, g * HEAD_DIM:(g + 1) * HEAD_DIM] = dq[g * B:(g + 1) * B, :].astype(BF16)
        dkcat = lax.dot_general(dsb, qs, _TN, preferred_element_type=F32) * scale
        dvcat = lax.dot_general(p.astype(BF16), dov, _TN, preferred_element_type=F32)
        prev0 = pl.multiple_of(jnp.maximum(n - 1, 0) * B, B)
        cur0 = pl.multiple_of(n * B, B)
        dk_ref[0, pl.ds(prev0, B), :] += dkcat[:B, :]
        dk_ref[0, pl.ds(cur0, B), :] += dkcat[B:, :]
        dv_ref[0, pl.ds(prev0, B), :] += dvcat[:B, :]
        dv_ref[0, pl.ds(cur0, B), :] += dvcat[B:, :]
        dsk = -jnp.exp(sink - lse) * delta
        lane = lax.broadcasted_iota(jnp.int32, (1, LANES), 1)
        row = jnp.zeros((1, LANES), F32)
        for g in range(group):
            row = row + jnp.where(lane == g, jnp.sum(dsk[g * B:(g + 1) * B, :]), 0.0)
        dsink_ref[0, 0:1, :] += row

    specs = _swa_specs(n_kv, group, 0, n_q, v_first)
    q_blk = pl.BlockSpec((B, group * HEAD_DIM), lambda kv, n: (n, kv))
    acc = pl.BlockSpec((1, S, HEAD_DIM), lambda kv, n: (kv, 0, 0))
    return pl.pallas_call(
        body, name="swa_bwd", grid=(n_kv, S // B),
        in_specs=specs + [q_blk, pl.BlockSpec((B, group * HEAD_DIM), lambda kv, n: (n, do_first + kv)),
                          pl.BlockSpec((1, B, group * LANES), lambda kv, n: (kv, n, 0)),
                          pl.BlockSpec(memory_space=pltpu.SMEM)],
        out_specs=[q_blk, acc, acc, pl.BlockSpec((1, 8, LANES), lambda kv, n: (kv, 0, 0))],
        out_shape=[jax.ShapeDtypeStruct((S, n_q * HEAD_DIM), BF16), jax.ShapeDtypeStruct((n_kv, S, HEAD_DIM), F32),
                   jax.ShapeDtypeStruct((n_kv, S, HEAD_DIM), F32), jax.ShapeDtypeStruct((n_kv, 8, LANES), F32)],
        compiler_params=pltpu.CompilerParams(dimension_semantics=("parallel", "arbitrary")),
    )(rq, rq, rq, proj, proj, o, do, lse_b, sinks)


def _adamw(w, g, m, v):
    m = ADAM_B1 * m + (1.0 - ADAM_B1) * g
    v = ADAM_B2 * v + (1.0 - ADAM_B2) * (g * g)
    m_hat = m / (1.0 - ADAM_B1 ** ADAM_STEP)
    v_hat = v / (1.0 - ADAM_B2 ** ADAM_STEP)
    delta = -ADAM_LR * (m_hat / (jnp.sqrt(v_hat) + ADAM_EPS) + ADAM_WD * w)
    return delta, m, v


def _mod_fwd(cond_in, w_mod, b_shard):
    R, D = cond_in.shape
    cols = w_mod.shape[1]
    tn = _fit(512, cols)

    def body(c_ref, w_ref, b_ref, o_ref):
        cv = c_ref[...]
        cond = (cv / (1.0 + jnp.exp(-cv))).astype(BF16)
        o_ref[...] = jnp.dot(cond, w_ref[...].astype(BF16), preferred_element_type=F32) + b_ref[...]

    return pl.pallas_call(
        body, name="mod_fwd", grid=(cols // tn,),
        in_specs=[pl.BlockSpec((R, D), lambda j: (0, 0)), pl.BlockSpec((D, tn), lambda j: (0, j)),
                  pl.BlockSpec((1, tn), lambda j: (0, j))],
        out_specs=pl.BlockSpec((R, tn), lambda j: (0, j)),
        out_shape=jax.ShapeDtypeStruct((R, cols), F32),
        compiler_params=pltpu.CompilerParams(dimension_semantics=("parallel",), vmem_limit_bytes=_vmem(3 * D * tn * 4)),
    )(cond_in, w_mod, b_shard)


def _mod_update(c_t, dmod, w, m, v):
    D, nb = c_t.shape
    cols = w.shape[1]
    tn = _fit(256, cols)

    def body(c_ref, d_ref, w_ref, m_ref, v_ref, g_ref, dl_ref, nm_ref, nv_ref):
        cv = c_ref[...]
        cond = cv / (1.0 + jnp.exp(-cv))
        g = jnp.zeros((D, tn), F32)
        for b in range(nb):
            g = g + cond[:, b:b + 1] * d_ref[b:b + 1, :]
        g_ref[...] = g
        dl_ref[...], nm_ref[...], nv_ref[...] = _adamw(w_ref[...], g, m_ref[...], v_ref[...])

    blk = pl.BlockSpec((D, tn), lambda j: (0, j))
    out = jax.ShapeDtypeStruct((D, cols), F32)
    return pl.pallas_call(
        body, name="mod_update", grid=(cols // tn,),
        in_specs=[pl.BlockSpec((D, nb), lambda j: (0, 0)), pl.BlockSpec((nb, tn), lambda j: (0, j)), blk, blk, blk],
        out_specs=[blk] * 4, out_shape=[out] * 4,
        compiler_params=pltpu.CompilerParams(dimension_semantics=("parallel",), vmem_limit_bytes=_vmem(18 * D * tn * 4)),
    )(c_t, dmod, w, m, v)


def _small_update(stacked, w, m, v):
    R, C = w.shape

    def body(s_ref, w_ref, m_ref, v_ref, g_ref, dl_ref, nm_ref, nv_ref):
        g = s_ref[0:R, :]
        for d in range(1, N_DEV):
            g = g + s_ref[d * R:(d + 1) * R, :]
        g_ref[...] = g
        dl_ref[...], nm_ref[...], nv_ref[...] = _adamw(w_ref[...], g, m_ref[...], v_ref[...])

    return pl.pallas_call(body, name="small_update", out_shape=[jax.ShapeDtypeStruct((R, C), F32)] * 4)(stacked, w, m, v)


def _place():
    return lax.axis_index("x"), lax.axis_index("y"), lax.axis_index("c")


def _allgather8(name, block):
    m_per, n = block.shape

    def body(x_ref, out_ref, token_ref, send_sems, recv_sems, local_sem):
        token_ref[...] = jnp.zeros_like(token_ref)
        x, y, c = _place()
        me, sibling = (x, y, c), (x, y, 1 - c)
        chips = [(1 - x, y), (x, 1 - y), (1 - x, 1 - y)]

        def rows(px, py, pc):
            return out_ref.at[pl.ds((4 * px + 2 * py + pc) * m_per, m_per), :]

        def copy(k, blk, to, src=None):
            return pltpu.make_async_remote_copy(
                src_ref=rows(*blk) if src is None else src, dst_ref=rows(*blk),
                send_sem=send_sems.at[k], recv_sem=recv_sems.at[k], device_id=to, device_id_type=MESH)

        mine = pltpu.make_async_copy(x_ref, rows(*me), local_sem)
        mine.start()
        first = [copy(0, me, sibling, src=x_ref)]
        first += [copy(1 + j, me, (*chip, c), src=x_ref) for j, chip in enumerate(chips)]
        for cp in first:
            cp.start()
        passed = [copy(4 + j, (*chip, c), sibling) for j, chip in enumerate(chips)]
        for j, chip in enumerate(chips):
            copy(1 + j, (*chip, c), me).wait_recv()
            passed[j].start()
        copy(0, sibling, me).wait_recv()
        for j, chip in enumerate(chips):
            copy(4 + j, (*chip, 1 - c), me).wait_recv()
        for cp in first + passed:
            cp.wait_send()
        mine.wait()

    vmem = pl.BlockSpec(memory_space=pltpu.VMEM)
    return pl.pallas_call(
        body, name=name,
        out_shape=[jax.ShapeDtypeStruct((N_DEV * m_per, n), block.dtype), jax.ShapeDtypeStruct((8, LANES), F32)],
        in_specs=[vmem], out_specs=[vmem, vmem],
        scratch_shapes=[pltpu.SemaphoreType.DMA((7,)), pltpu.SemaphoreType.DMA((7,)), pltpu.SemaphoreType.DMA],
    )(block)


_ANY = pl.BlockSpec(memory_space=pl.ANY)


def _half(ref, c, rows):
    return ref.at[pl.ds(c * (rows // 2), rows // 2), :]


_HBM = pl.BlockSpec(memory_space=pltpu.HBM)
_SEM = pl.BlockSpec(memory_space=pltpu.SEMAPHORE)
_EFFECT = pltpu.SideEffectType.DATAFLOW_SIDE_EFFECTING


def _ici_start(name, srcs, land_shapes, plan, per_source=3):
    ns, nl = len(srcs), len(land_shapes)
    n_copies = per_source * ns

    def body(*refs):
        src_refs, land_refs = refs[:ns], refs[ns:ns + nl]
        send_sems, recv_sems = refs[ns + nl], refs[ns + nl + 1]
        token = refs[-1]
        for n, (src, dst, peer, _) in enumerate(plan(src_refs, land_refs)):
            pltpu.make_async_remote_copy(src_ref=src, dst_ref=dst, send_sem=send_sems.at[n], recv_sem=recv_sems.at[n],
                                         device_id=peer, device_id_type=MESH).start()
        token[...] = jnp.zeros_like(token)

    lands = [lax.empty(s.shape, s.dtype) for s in land_shapes]
    out = pl.pallas_call(
        body, name=name,
        out_shape=(pltpu.SemaphoreType.DMA((n_copies,)), pltpu.SemaphoreType.DMA((n_copies,)),
                   *[pltpu.HBM(a.shape, a.dtype) for a in list(srcs) + lands], jax.ShapeDtypeStruct((8, LANES), F32)),
        in_specs=[_HBM] * (ns + nl),
        out_specs=(_SEM, _SEM, *[_HBM] * (ns + nl), pl.BlockSpec(memory_space=pltpu.VMEM)),
        input_output_aliases={n: 2 + n for n in range(ns + nl)},
        compiler_params=pltpu.CompilerParams(has_side_effects=_EFFECT),
    )(*[pltpu.with_memory_space_constraint(a, pltpu.HBM) for a in list(srcs) + lands])
    return out[0], out[1], list(out[2:2 + ns]), list(out[2 + ns:2 + ns + nl]), out[-1]


def _ici_wait(name, send_sems, recv_sems, srcs, lands, plan, after):
    ns, nl = len(srcs), len(lands)

    def body(*refs):
        src_refs, land_refs = refs[:ns], refs[ns:ns + nl]
        send_sems, recv_sems = refs[ns + nl], refs[ns + nl + 1]
        for n, (src, _, peer, mine) in enumerate(plan(src_refs, land_refs)):
            cp = pltpu.make_async_remote_copy(src_ref=src, dst_ref=mine, send_sem=send_sems.at[n],
                                              recv_sem=recv_sems.at[n], device_id=peer, device_id_type=MESH)
            cp.wait_send()
            cp.wait_recv()

    out = pl.pallas_call(
        body, name=name, out_shape=[pltpu.HBM(a.shape, a.dtype) for a in list(srcs) + list(lands)],
        in_specs=[_HBM] * (ns + nl) + [_SEM, _SEM, _ANY], out_specs=[_HBM] * (ns + nl),
        input_output_aliases={n: n for n in range(ns + nl)},
        compiler_params=pltpu.CompilerParams(has_side_effects=_EFFECT),
    )(*srcs, *lands, send_sems, recv_sems, after)
    return list(out[:ns]), list(out[ns:])


def _gather_plan(src_refs, land_refs):
    x, y, c = _place()
    copies = []
    for w, land in zip(src_refs, land_refs):
        R = w.shape[0]
        for cx, cy in [(1 - x, y), (x, 1 - y), (1 - x, 1 - y)]:
            copies.append((_half(w, c, R), _half(land.at[2 * x + y], c, R), (cx, cy, c),
                           _half(land.at[2 * cx + cy], c, R)))
    return copies


def _pass_plan(src_refs, land_refs):
    x, y, c = _place()
    copies = []
    for land in src_refs:
        R = land.shape[1]
        for cx, cy in [(1 - x, y), (x, 1 - y), (1 - x, 1 - y)]:
            slot = land.at[2 * cx + cy]
            copies.append((_half(slot, c, R), _half(slot, c, R), (x, y, 1 - c), _half(slot, 1 - c, R)))
    return copies


def _share_plan(src_refs, land_refs):
    x, y, c = _place()
    return [(h, land, (x, y, 1 - c), land) for h, land in zip(src_refs, land_refs)]


def _pass_to_sibling(name, lands):
    nw = len(lands)

    def body(*refs):
        ins, outs = refs[:nw], refs[nw:2 * nw]
        send_sems, recv_sems = refs[2 * nw:]
        x, y, c = _place()
        chips = [(1 - x, y), (x, 1 - y), (1 - x, 1 - y)]
        copies = []
        for k in range(nw):
            R = ins[k].shape[1]
            for j, (cx, cy) in enumerate(chips):
                cp = pltpu.make_async_remote_copy(
                    src_ref=_half(ins[k].at[2 * cx + cy], c, R), dst_ref=_half(outs[k].at[2 * cx + cy], c, R),
                    send_sem=send_sems.at[3 * k + j], recv_sem=recv_sems.at[3 * k + j],
                    device_id=(x, y, 1 - c), device_id_type=MESH)
                cp.start()
                copies.append(cp)
        for k in range(nw):
            R = ins[k].shape[1]
            for j, (cx, cy) in enumerate(chips):
                pltpu.make_async_remote_copy(
                    src_ref=_half(ins[k].at[2 * cx + cy], c, R), dst_ref=_half(outs[k].at[2 * cx + cy], 1 - c, R),
                    send_sem=send_sems.at[3 * k + j], recv_sem=recv_sems.at[3 * k + j],
                    device_id=(x, y, 1 - c), device_id_type=MESH).wait_recv()
        for cp in copies:
            cp.wait_send()

    return pl.pallas_call(
        body, name=name, out_shape=[jax.ShapeDtypeStruct(a.shape, a.dtype) for a in lands],
        in_specs=[_ANY] * nw, out_specs=[_ANY] * nw, input_output_aliases={k: k for k in range(nw)},
        scratch_shapes=[pltpu.SemaphoreType.DMA((3 * nw,)), pltpu.SemaphoreType.DMA((3 * nw,))],
    )(*lands)


def _tie(vec, token):
    return vec + token[0:1, 0:1]


ROW_ALIGN = 16
TILE_ELEMS = 512 * 1024


def _tiles(rows, cols):
    fits = [t for t in range(ROW_ALIGN, min(rows, 256) + 1, ROW_ALIGN) if rows % t == 0]
    tr = fits[-1] if fits and fits[-1] >= 64 else rows
    tc = cols
    while tr * tc > TILE_ELEMS and tc % (2 * LANES) == 0:
        tc //= 2
    return tr, tc


def _scatter_plan(src_refs, land_refs):
    x, y, c = _place()
    copies = []
    for p, land in zip(src_refs, land_refs):
        for j, (cx, cy) in enumerate([(1 - x, y), (x, 1 - y), (1 - x, 1 - y)]):
            copies.append((p.at[2 * cx + cy], land.at[j], (cx, cy, c), land.at[j]))
    return copies


def _chip_add(name, chip, sums, recv):
    _, H, C = sums.shape
    tr, tc = _tiles(H, C)

    def body(chip_ref, p_ref, r_ref, o_ref):
        total = p_ref[0].astype(F32)
        for j in range(3):
            total = total + r_ref[j].astype(F32)
        o_ref[...] = total

    grid_spec = pltpu.PrefetchScalarGridSpec(
        num_scalar_prefetch=1, grid=(H // tr, C // tc),
        in_specs=[pl.BlockSpec((1, tr, tc), lambda r, q, chip_ref: (chip_ref[0], r, q)),
                  pl.BlockSpec((3, tr, tc), lambda r, q, chip_ref: (0, r, q))],
        out_specs=pl.BlockSpec((tr, tc), lambda r, q, chip_ref: (r, q)))
    return pl.pallas_call(
        body, name=name, grid_spec=grid_spec, out_shape=jax.ShapeDtypeStruct((H, C), F32),
        compiler_params=pltpu.CompilerParams(dimension_semantics=("parallel", "parallel")),
    )(chip, sums, recv)


def _pair_share(name, halves):
    nw = len(halves)

    def body(*refs):
        hs, outs = refs[:nw], refs[nw:2 * nw]
        send_sems, recv_sems = refs[2 * nw:]
        x, y, c = _place()
        copies = []
        for k in range(nw):
            cp = pltpu.make_async_remote_copy(
                src_ref=hs[k], dst_ref=outs[k], send_sem=send_sems.at[k], recv_sem=recv_sems.at[k],
                device_id=(x, y, 1 - c), device_id_type=MESH)
            cp.start()
            copies.append(cp)
        for cp in copies:
            cp.wait()

    return pl.pallas_call(
        body, name=name,
        out_shape=[jax.ShapeDtypeStruct(h.shape, h.dtype) for h in halves],
        in_specs=[_ANY] * nw, out_specs=[_ANY] * nw,
        scratch_shapes=[pltpu.SemaphoreType.DMA((nw,)), pltpu.SemaphoreType.DMA((nw,))],
    )(*halves)


def _adam_halves(name, core, w, g_own, g_other, m, v):
    R, C = w.shape
    H = R // 2
    tr, tc = _tiles(H, C)
    nr, nc = H // tr, C // tc

    def body(core_ref, w_ref, go_ref, gr_ref, m_ref, v_ref, g_ref, dl_ref, nm_ref, nv_ref):
        own = (pl.program_id(0) // nr) == core_ref[0]
        g = jnp.where(own, go_ref[...], gr_ref[...])
        g_ref[...] = g
        dl_ref[...], nm_ref[...], nv_ref[...] = _adamw(w_ref[...], g, m_ref[...], v_ref[...])

    blk = pl.BlockSpec((tr, tc), lambda r, q, core_ref: (r, q))

    def half_spec(is_own):
        def index(r, q, core_ref):
            mine = ((r // nr) == core_ref[0]) == is_own
            done = is_own == (core_ref[0] == 0)
            return (jnp.where(mine, r % nr, jnp.where(done, nr - 1, 0)), jnp.where(mine, q, jnp.where(done, nc - 1, 0)))
        return pl.BlockSpec((tr, tc), index)
    out = jax.ShapeDtypeStruct((R, C), F32)
    grid_spec = pltpu.PrefetchScalarGridSpec(
        num_scalar_prefetch=1, grid=(R // tr, nc), in_specs=[blk, half_spec(True), half_spec(False), blk, blk],
        out_specs=[blk] * 4)
    return pl.pallas_call(
        body, name=name, grid_spec=grid_spec, out_shape=[out] * 4,
        compiler_params=pltpu.CompilerParams(dimension_semantics=("parallel", "parallel"),
                                             vmem_limit_bytes=_vmem(20 * tr * tc * 4)),
    )(core, w, g_own, g_other, m, v)


def kernel(x, c, w_mod, b_mod, g_pre_mix, g_post_mix, w_in, b_forget, swa_sinks, w_out, g_pre_mlp, g_post_mlp, w_up, w_down, loss_target, m_w_mod, m_b_mod, m_g_pre_mix, m_g_post_mix, m_w_in, m_b_forget, m_swa_sinks, m_w_out, m_g_pre_mlp, m_g_post_mlp, m_w_up, m_w_down, v_w_mod, v_b_mod, v_g_pre_mix, v_g_post_mix, v_w_in, v_b_forget, v_swa_sinks, v_w_out, v_g_pre_mlp, v_g_post_mlp, v_w_up, v_w_down):
    S, D = x.shape[1], x.shape[2]
    n_heads = D // HEAD_DIM
    n_fox = n_heads // 2
    n_swa = n_heads - n_fox
    n_kv = max(1, n_swa // 4)
    fox_w, swa_w, kv_w = n_fox * HEAD_DIM, n_swa * HEAD_DIM, n_kv * HEAD_DIM
    main_w = 3 * fox_w + swa_w + 2 * kv_w
    in_w = main_w + n_fox
    mod_cols = w_mod.shape[2]

    ax, ay, ac = _place()
    chip = 2 * ax + ay
    dev = 2 * chip + ac
    chip_arr = jnp.reshape(chip, (1,)).astype(jnp.int32)
    core_arr = jnp.reshape(ac, (1,)).astype(jnp.int32)

    x2, tgt = x[0], loss_target[0]

    c_all, _ = _allgather8("gather_c", c.reshape(8, D // 8))
    c_all = c_all.reshape(N_DEV, D)
    b_shard = lax.dynamic_slice_in_dim(b_mod, chip * mod_cols, mod_cols, axis=1)
    mod_shard = _mod_fwd(jnp.pad(c_all, ((0, 16 - N_DEV), (0, 0))), w_mod[0], b_shard)[:N_DEV]
    mod_all, token = _allgather8("gather_mod", mod_shard)
    mod_all = mod_all.reshape(N_CHIPS, 2, N_DEV, mod_cols)[:, 0]
    mod = lax.dynamic_index_in_dim(mod_all, dev, axis=1, keepdims=False).reshape(N_MOD, 1, D)
    sh_a, sc_a, gt_a, sh_m, sc_m, gt_m = [mod[n] for n in range(N_MOD)]

    in_rows = in_w // N_CHIPS
    in_rows_pad = -(-in_rows // (2 * LANES)) * (2 * LANES)
    slab_w = N_CHIPS * in_rows_pad

    def rows_of(a):
        return jnp.pad(a[0].T, ((0, in_rows_pad - in_rows), (0, 0)))

    def slab_cols(lo, hi):
        spans = []
        while lo < hi:
            s, r = divmod(lo, in_rows)
            n = min(hi - lo, in_rows - r)
            spans.append((s * in_rows_pad + r, s * in_rows_pad + r + n))
            lo += n
        return spans

    gate_lo = 3 * fox_w
    main_spans = slab_cols(0, gate_lo) + slab_cols(gate_lo + n_fox, in_w)
    (gate_first, gate_last), = slab_cols(gate_lo, gate_lo + n_fox)

    names = ["w_in", "w_out", "w_up", "w_down"]
    flights = {}
    for n, w in zip(names, [rows_of(w_in), w_out[0], w_up[0], w_down[0]]):
        shard = _tie(w, token).astype(BF16)
        flights[n] = _ici_start("gather_start_" + n, [shard], [jax.ShapeDtypeStruct((N_CHIPS,) + shard.shape, BF16)],
                                _gather_plan)
        token = flights[n][4]
    sc_a = _tie(sc_a, token)

    def arrived(n, after):
        send, recv, srcs, lands, _ = flights[n]
        srcs, lands = _ici_wait("gather_wait_" + n, send, recv, srcs, lands, _gather_plan, after)
        return srcs[0], _ici_start("gather_pass_start_" + n, lands, [], _pass_plan)

    def gathered(n, after, in_flight=None):
        if in_flight is None:
            send, recv, srcs, lands, _ = flights[n]
            srcs, lands = _ici_wait("gather_wait_" + n, send, recv, srcs, lands, _gather_plan, after)
            own, stack = srcs[0], _pass_to_sibling("gather_pass_" + n, lands)[0]
        else:
            own, (send, recv, lands, _, _) = in_flight
            stack = _ici_wait("gather_pass_wait_" + n, send, recv, lands, [], _pass_plan, after)[0][0]
        return lax.dynamic_update_index_in_dim(stack, own, chip, 0)

    d_ff = N_CHIPS * w_up.shape[2]

    h = _pre_norm(x2, g_pre_mix, sc_a, sh_a)
    in_state = [rows_of(w_in)] + [rows_of(_tie(a, token)) for a in (m_w_in, v_w_in)]
    cos, sin_signed = _rope_tables(S)
    ready = h[:8, :LANES].astype(F32) + cos[:8] + sum(a[:8, :LANES] for a in in_state)
    w_slab_t = gathered("w_in", ready).reshape(slab_w, D)
    tm_p, tn_p = _fit(MM_TM, S), _fit(512, slab_w)
    win0 = gate_first // LANES * LANES
    win_j, win_off = divmod(win0, tn_p)
    assert win_off + 2 * LANES <= tn_p and gate_last - win0 <= 2 * LANES

    def proj_epilogue(acc, ex, outs):
        outs[0][...] = acc.astype(BF16)

        @pl.when(pl.program_id(1) == win_j)
        def _():
            outs[1][...] = acc[:, win_off:win_off + 2 * LANES]

    proj_slab, gate_win = _matmul(
        "in_proj", h, w_slab_t, "nt",
        [((S, slab_w), BF16, (tm_p, tn_p), lambda i, j: (i, j)), ((S, 2 * LANES), F32, (tm_p, 2 * LANES), lambda i, j: (i, 0))],
        proj_epilogue, tn=tn_p, revisits=True)
    proj = jnp.concatenate([proj_slab[:, lo:hi] for lo, hi in main_spans], axis=1)
    out_flight = arrived("w_out", proj_slab)
    fg = _tie(jnp.pad(gate_win[:, gate_first - win0:gate_last - win0], ((0, 0), (0, LANES - n_fox))), out_flight[1][4])
    b_pad = jnp.pad(b_forget, ((0, 0), (0, LANES - n_fox)))
    cum_row = _fox_gate_fwd(fg, b_pad)[:n_fox].reshape(n_fox, 1, S)
    fox_o, fox_lse = _fox_fwd(proj, cum_row, n_fox)

    rq = _rope("rope_fwd", proj, 3 * n_fox, n_swa + n_kv, cos, sin_signed)
    v_first = 3 * n_fox + n_swa + n_kv
    sinks = swa_sinks[0]
    swa_o, swa_lse = _swa_fwd(rq, proj, v_first, sinks, n_swa, n_kv)

    mixcat = jnp.concatenate([fox_o, swa_o], axis=1).astype(BF16)
    up_flight = arrived("w_up", mixcat)
    w_out_f = gathered("w_out", mixcat, out_flight).reshape(D, D)
    mix = _mm_plain("out_proj", mixcat, w_out_f, "nn", F32, after=up_flight[1][4])
    x1, h2 = _post_mix(x2, mix, g_post_mix, gt_a, g_pre_mlp, sc_m, sh_m)
    w_up_f = jnp.transpose(gathered("w_up", h2, up_flight), (1, 0, 2)).reshape(D, d_ff)

    tm_u, tn_u = _fit(MM_TM, S), _fit(MM_TN, d_ff)

    def up_epilogue(acc, ex, outs):
        outs[0][...] = acc.astype(BF16)
        r = jnp.maximum(acc, 0.0)
        outs[1][...] = (r * r).astype(BF16)

    ublk = ((S, d_ff), BF16, (tm_u, tn_u), lambda i, j: (i, j))
    u, a = _matmul("mlp_up", h2, w_up_f, "nn", [ublk, ublk], up_epilogue)
    w_down_f = gathered("w_down", a).reshape(d_ff, D)
    y = _mm_plain("mlp_down", a, w_down_f, "nn", F32)

    dy, dout, loss_part, acc_mlp_post = _loss_and_post_mlp_bwd(x1, y, tgt, g_post_mlp, gt_m)

    def du_epilogue(acc, ex, outs):
        outs[0][...] = (acc * (2.0 * jnp.maximum(ex[0][...].astype(F32), 0.0))).astype(BF16)

    du = _matmul("mlp_down_bwd", dy, w_down_f, "nt", [ublk], du_epilogue,
                 extras=[(u, (tm_u, tn_u), lambda i, j: (i, j))])[0]
    def pair_send(tag, part):
        return _ici_start("grad_pair_start_" + tag, [part], [jax.ShapeDtypeStruct(part.shape, BF16)], _share_plan,
                          per_source=1)

    def pair_recv(tag, flight, after):
        send, recv, srcs, lands, _ = flight
        return _ici_wait("grad_pair_wait_" + tag, send, recv, srcs, lands, _share_plan, after)[1][0]

    def scatter_start(tag, sums):
        return _ici_start("grad_scatter_start_" + tag, sums,
                          [jax.ShapeDtypeStruct((3,) + p.shape[1:], BF16) for p in sums], _scatter_plan)

    def scatter_finish(tag, flight, after):
        send, recv, srcs, lands, _ = flight
        sums, received = _ici_wait("grad_scatter_wait_" + tag, send, recv, srcs, lands, _scatter_plan, after)
        return [_chip_add("chip_add_%s_%d" % (tag, k), chip_arr, p, r) for k, (p, r) in enumerate(zip(sums, received))]

    tm_g = _fit(MM_TM, D // 2)
    pair_down = pair_send("down", _grad_half("grad_w_down_a", core_arr, a, dy, N_CHIPS, 1, tm_g, True))
    pair_up = pair_send("up", _grad_half("grad_w_up_a", core_arr, h2, du, 1, N_CHIPS, tm_g, True, after=pair_down[4]))
    sum_down = _grad_half("grad_w_down_b", core_arr, a, dy, N_CHIPS, 1, tm_g, False,
                          recv=pair_recv("down", pair_down, pair_up[4]))
    sum_up = _grad_half("grad_w_up_b", core_arr, h2, du, 1, N_CHIPS, tm_g, False, recv=pair_recv("up", pair_up, sum_down))
    flight_mlp = scatter_start("mlp", [sum_up, sum_down])
    dh2 = _mm_plain("mlp_up_bwd", du, w_up_f, "nt", F32, after=flight_mlp[4])
    dx1, dmix, acc_mid = _pre_mlp_and_post_mix_bwd(dh2, x1, dout, mix, _tie(g_pre_mlp, flight_mlp[4]), sc_m,
                                                   g_post_mix, gt_a)

    dmixcat = _mm_plain("out_proj_bwd", dmix, w_out_f, "nt", F32)

    fdq, fdk, fdv, dcum_row, dcum_q = _fox_bwd(proj, fox_o, dmixcat, fox_lse, cum_row, n_fox)
    dcum_k = jnp.pad(dcum_row.reshape(n_fox, S), ((0, LANES - n_fox), (0, 0)))
    dfg, db_forget = _fox_gate_bwd(dcum_k, dcum_q, fg, b_pad)

    group_w = (n_swa // n_kv) * HEAD_DIM
    sdq, sdk, sdv, dsink = _swa_bwd(rq, proj, v_first, sinks, swa_o, dmixcat, fox_w // group_w, swa_lse, n_swa, n_kv)
    drq = jnp.concatenate([sdq, jnp.transpose(sdk, (1, 0, 2)).reshape(S, kv_w).astype(BF16)], axis=1)
    d_sq_sk = _rope("rope_bwd", drq, 0, n_swa + n_kv, cos, -sin_signed)
    dsv = jnp.transpose(sdv, (1, 0, 2)).reshape(S, kv_w).astype(BF16)
    dproj = jnp.concatenate([fdq, fdk, fdv, d_sq_sk, dsv], axis=1)

    pieces = []
    for s in range(N_CHIPS):
        lo, hi = s * in_rows, (s + 1) * in_rows
        for src, first, last, shift in [(dproj, 0, gate_lo, 0), (dfg, gate_lo, gate_lo + n_fox, gate_lo),
                                        (dproj, gate_lo + n_fox, in_w, n_fox)]:
            if max(lo, first) < min(hi, last):
                pieces.append(src[:, max(lo, first) - shift:min(hi, last) - shift])
        pieces.append(jnp.zeros((S, in_rows_pad - in_rows), BF16))
    dproj_slab = jnp.concatenate(pieces, axis=1)

    tm_in, tm_out = in_rows_pad // 2, D // (2 * N_CHIPS)
    pair_in = pair_send("in", _grad_half("grad_w_in_a", core_arr, dproj_slab, h, N_CHIPS, 1, tm_in, True))
    pair_out = pair_send("out", _grad_half("grad_w_out_a", core_arr, mixcat, dmix, N_CHIPS, 1, tm_out, True,
                                           after=pair_in[4]))
    sum_in = _grad_half("grad_w_in_b", core_arr, dproj_slab, h, N_CHIPS, 1, tm_in, False,
                        recv=pair_recv("in", pair_in, pair_out[4]))
    dh = _mm_plain("in_proj_bwd", dproj_slab, w_slab_t, "nn", F32, tk=slab_w // 2)
    grad_x, acc_pre = _pre_mix_bwd(dh, x2, dx1, g_pre_mix, sc_a)

    zero_row = jnp.zeros((1, D), F32)
    tail = jnp.concatenate([db_forget[0:1, :n_fox], dsink[:, 0, :n_swa // n_kv].reshape(1, n_swa),
                            loss_part[0:1, 0:1], jnp.zeros((1, D - n_fox - n_swa - 1), F32)], axis=1)
    partial = jnp.concatenate([
        acc_pre[0:1], acc_pre[1:2], acc_mid[3:4], acc_mid[0:1], acc_mid[1:2], acc_mlp_post[0:1],
        acc_pre[2:3], acc_mid[4:5], acc_mid[2:3], acc_mlp_post[1:2], tail] + [zero_row] * 5, axis=0)
    gathered_small, token = _allgather8("gather_small_grads", partial)

    sum_out = _grad_half("grad_w_out_b", core_arr, mixcat, dmix, N_CHIPS, 1, tm_out, False,
                         recv=pair_recv("out", pair_out, token))
    flight_mix = scatter_start("mix", [sum_in, sum_out])
    halves_mlp = scatter_finish("mlp", flight_mlp, flight_mix[4])
    share_mlp = _ici_start("grad_share_start_mlp", halves_mlp,
                           [jax.ShapeDtypeStruct(hv.shape, F32) for hv in halves_mlp], _share_plan, per_source=1)

    def pack(bm, gpm, gqm, gpl, gql, bf, sk):
        last = jnp.concatenate([bf, sk, jnp.zeros((1, D - n_fox - n_swa), F32)], axis=1)
        return jnp.concatenate([bm.reshape(N_MOD, D), gpm, gqm, gpl, gql, last, jnp.zeros((5, D), F32)], axis=0)

    def unpack(p):
        return {"b_mod": p[0:N_MOD].reshape(1, N_MOD * D), "g_pre_mix": p[6:7], "g_post_mix": p[7:8],
                "g_pre_mlp": p[8:9], "g_post_mlp": p[9:10], "b_forget": p[10:11, :n_fox],
                "swa_sinks": p[10:11, n_fox:n_fox + n_swa]}

    small_out = _small_update(
        gathered_small, _tie(pack(b_mod, g_pre_mix, g_post_mix, g_pre_mlp, g_post_mlp, b_forget, swa_sinks), share_mlp[4]),
        pack(m_b_mod, m_g_pre_mix, m_g_post_mix, m_g_pre_mlp, m_g_post_mlp, m_b_forget, m_swa_sinks),
        pack(v_b_mod, v_g_pre_mix, v_g_post_mix, v_g_pre_mlp, v_g_post_mlp, v_b_forget, v_swa_sinks))
    g_small, d_small, m_small, v_small = [unpack(p) for p in small_out]
    loss = small_out[0][N_MOD + 4, n_fox + n_swa]

    dmod_all = gathered_small.reshape(N_DEV, 16, D)[:, :N_MOD].reshape(N_DEV, N_MOD * D)
    dmod_shard = _tie(lax.dynamic_slice_in_dim(dmod_all, chip * mod_cols, mod_cols, axis=1), share_mlp[4])
    g_w_mod, d_w_mod, nm_w_mod, nv_w_mod = _mod_update(c_all.T, dmod_shard, w_mod[0], m_w_mod[0], v_w_mod[0])
    send, recv, halves_mlp, lands, _ = share_mlp
    halves_mlp, others_mlp = _ici_wait("grad_share_wait_mlp", send, recv, halves_mlp, lands, _share_plan,
                                       d_w_mod[:8, :LANES] + small_out[1][:8, :LANES])

    grads = dict(g_small, w_mod=g_w_mod[None])
    deltas = dict(d_small, w_mod=d_w_mod[None])
    new_m = dict(m_small, w_mod=nm_w_mod[None])
    new_v = dict(v_small, w_mod=nv_w_mod[None])
    weights = {"w_in": (w_in, m_w_in, v_w_in), "w_out": (w_out, m_w_out, v_w_out), "w_up": (w_up, m_w_up, v_w_up),
               "w_down": (w_down, m_w_down, v_w_down)}

    def big_update(n, own, other):
        transposed = n == "w_in"
        w, m, v = in_state if transposed else [a[0] for a in weights[n]]
        outs = _adam_halves("adam_" + n, core_arr, w, own, other, m, v)
        if transposed:
            outs = [o[:in_rows].T for o in outs]
        grads[n], deltas[n], new_m[n], new_v[n] = [o[None] for o in outs]

    big_update("w_up", halves_mlp[0], others_mlp[0])
    big_update("w_down", halves_mlp[1], others_mlp[1])
    ran = deltas["w_down"][0, :8, :LANES] + deltas["w_up"][0, :8, :LANES] + d_w_mod[:8, :LANES]
    halves_mix = scatter_finish("mix", flight_mix, ran)
    others_mix = _pair_share("grad_pair_share_mix", halves_mix)
    big_update("w_in", halves_mix[0], others_mix[0])
    big_update("w_out", halves_mix[1], others_mix[1])

    order = ["w_mod", "b_mod", "g_pre_mix", "g_post_mix", "w_in", "b_forget", "swa_sinks", "w_out", "g_pre_mlp",
             "g_post_mlp", "w_up", "w_down"]
    return (loss, grad_x[None], *[grads[n] for n in order], *[deltas[n] for n in order],
            *[new_m[n] for n in order], *[new_v[n] for n in order])
```

```python
import jax
import jax.numpy as jnp
from jax import lax
from jax.experimental import pallas as pl
from jax.experimental.pallas import tpu as pltpu

F32 = jnp.float32
BF16 = jnp.bfloat16
MESH = pl.DeviceIdType.MESH

HEAD_DIM = 128
SWA_BLOCK = 128
ROPE_THETA = 10000.0
NORM_EPS = 1e-6
NEG = -1e30
N_MOD = 6
ADAM_LR = 0.001
ADAM_B1 = 0.9
ADAM_B2 = 0.999
ADAM_EPS = 1e-08
ADAM_WD = 0.01
ADAM_STEP = 10
N_CHIPS = 4
N_DEV = 8
LANES = 128
VMEM_CAP = 60 * 1024 * 1024

_NN = (((1,), (0,)), ((), ()))
_NT = (((1,), (1,)), ((), ()))
_TN = (((0,), (0,)), ((), ()))


def _vmem(nbytes):
    return int(min(VMEM_CAP, nbytes * 5 // 4 + (4 << 20)))


def _nbytes(shape, dtype):
    n = 1
    for s in shape:
        n *= s
    return n * jnp.dtype(dtype).itemsize


def _fit(t, n):
    t = min(t, n)
    assert n % t == 0, (t, n)
    return t


MM_TM, MM_TN, MM_TK = 512, 1024, 2048


def _matmul(name, a, b, mode, out_defs, epilogue, extras=(), tm=MM_TM, tn=MM_TN, tk=MM_TK, revisits=False,
            row_sel=None):
    if mode == "nn":
        (M, K), (K2, N) = a.shape, b.shape
    elif mode == "nt":
        (M, K), (N, K2) = a.shape, b.shape
    else:
        (K, M), (K2, N) = a.shape, b.shape
    assert K == K2, (a.shape, b.shape, mode)
    tm, tn, tk = _fit(tm, M), _fit(tn, N), _fit(tk, K)
    nk = K // tk
    dims = {"nn": _NN, "nt": _NT, "tn": _TN}[mode]
    if row_sel is None:
        grid_m, a_row = M // tm, lambda i, *sel: i
    else:
        grid_m, a_row = row_sel[2], lambda i, *sel: row_sel[1](i, sel[0])
    a_spec = (pl.BlockSpec((tk, tm), lambda i, j, k, *sel: (k, a_row(i, *sel))) if mode == "tn"
              else pl.BlockSpec((tm, tk), lambda i, j, k, *sel: (a_row(i, *sel), k)))
    b_spec = (pl.BlockSpec((tn, tk), lambda i, j, k, *sel: (j, k)) if mode == "nt"
              else pl.BlockSpec((tk, tn), lambda i, j, k, *sel: (k, j)))
    n_ex, n_out = len(extras), len(out_defs)

    def body(*refs):
        if row_sel is not None:
            refs = refs[1:]
        a_ref, b_ref = refs[0], refs[1]
        ex = refs[2:2 + n_ex]
        outs = refs[2 + n_ex:2 + n_ex + n_out]
        prod = lax.dot_general(a_ref[...], b_ref[...], dims, preferred_element_type=F32)
        if nk == 1:
            epilogue(prod, ex, outs)
        else:
            acc_ref = refs[-1]
            k = pl.program_id(2)

            @pl.when(k == 0)
            def _():
                acc_ref[...] = prod

            @pl.when(k > 0)
            def _():
                acc_ref[...] += prod

            @pl.when(k == nk - 1)
            def _():
                epilogue(acc_ref[...], ex, outs)

    def wrap(f):
        return lambda i, j, k, *sel: f(i, j)

    in_specs = [a_spec, b_spec] + [pl.BlockSpec(blk, wrap(f)) for _, blk, f in extras]
    out_specs = [pl.BlockSpec(blk, wrap(f)) for _, _, blk, f in out_defs]
    out_shape = [jax.ShapeDtypeStruct(s, d) for s, d, _, _ in out_defs]
    need = 2 * (tm * tk + tk * tn) * a.dtype.itemsize + 3 * tm * tn * 4
    need += sum(2 * _nbytes(blk, arr.dtype) for arr, blk, _ in extras)
    need += sum(2 * _nbytes(blk, d) for _, d, blk, _ in out_defs)
    grid = (grid_m, N // tn, nk)
    scratch = [pltpu.VMEM((tm, tn), F32)] if nk > 1 else []
    params = pltpu.CompilerParams(
        dimension_semantics=("parallel", "arbitrary" if revisits else "parallel", "arbitrary"),
        vmem_limit_bytes=_vmem(need))
    operands = (a, b, *[arr for arr, _, _ in extras])
    if row_sel is None:
        return pl.pallas_call(body, name=name, grid=grid, in_specs=in_specs, out_specs=out_specs, out_shape=out_shape,
                              scratch_shapes=scratch, compiler_params=params)(*operands)
    grid_spec = pltpu.PrefetchScalarGridSpec(num_scalar_prefetch=1, grid=grid, in_specs=in_specs, out_specs=out_specs,
                                             scratch_shapes=scratch)
    return pl.pallas_call(body, name=name, grid_spec=grid_spec, out_shape=out_shape,
                          compiler_params=params)(row_sel[0], *operands)


def _grad_half(name, core, a, b, row_slabs, col_slabs, tm, other, recv=None, after=None):
    (_, M), (_, N) = a.shape, b.shape
    H = M // (2 * row_slabs)
    nh = H // tm
    tn = _fit(MM_TN, N // col_slabs)
    per = N // col_slabs // tn

    def a_block(i, core_ref):
        half = (1 - core_ref[0]) if other else core_ref[0]
        return (i // nh) * (2 * nh) + half * nh + i % nh

    def out_index(i, j):
        return (j // per, i, j % per) if col_slabs > 1 else (i // nh, i % nh, j)

    slabs = max(row_slabs, col_slabs)
    out_def = ((slabs, H, N // col_slabs), BF16, (1, tm, tn), out_index)

    def epilogue(acc, ex, outs):
        outs[0][0] = (acc if recv is None else acc + ex[0][0].astype(F32)).astype(BF16)

    extras = ([] if recv is None else [(recv, (1, tm, tn), out_index)]) + ([] if after is None else [_behind(after)])
    return _matmul(name, a, b, "tn", [out_def], epilogue, extras=extras, tm=tm, tn=tn,
                   row_sel=(core, a_block, row_slabs * nh))[0]


def _behind(token):
    return (token, (8, LANES), lambda i, j: (0, 0))


def _mm_plain(name, a, b, mode, out_dtype, after=None, **tiles):
    if mode == "nn":
        M, N = a.shape[0], b.shape[1]
    elif mode == "nt":
        M, N = a.shape[0], b.shape[0]
    else:
        M, N = a.shape[1], b.shape[1]
    tm, tn = _fit(tiles.get("tm", MM_TM), M), _fit(tiles.get("tn", MM_TN), N)

    def epi(acc, ex, outs):
        outs[0][...] = acc.astype(out_dtype)

    return _matmul(name, a, b, mode, [((M, N), out_dtype, (tm, tn), lambda i, j: (i, j))], epi,
                   extras=[] if after is None else [_behind(after)], **tiles)[0]


def _rstd(v):
    return lax.rsqrt(jnp.mean(v * v, axis=-1, keepdims=True) + NORM_EPS)


def _row_call(name, body, row_ins, vec_ins, row_outs, acc_outs, S, D, tr):
    tr = _fit(tr, S)
    row_spec = pl.BlockSpec((tr, D), lambda r: (r, 0))
    vec_spec = pl.BlockSpec((1, D), lambda r: (0, 0))
    in_specs = [row_spec] * len(row_ins) + [vec_spec] * len(vec_ins)
    out_specs = [row_spec] * len(row_outs) + [pl.BlockSpec(shp, lambda r: (0, 0)) for shp in acc_outs]
    out_shape = [jax.ShapeDtypeStruct((S, D), d) for d in row_outs] + [jax.ShapeDtypeStruct(shp, F32) for shp in acc_outs]
    need = sum(2 * tr * D * a.dtype.itemsize for a in row_ins) + sum(2 * tr * D * jnp.dtype(d).itemsize for d in row_outs)
    need += 8 * tr * D * 4
    return pl.pallas_call(
        body, name=name, grid=(S // tr,), in_specs=in_specs, out_specs=out_specs, out_shape=out_shape,
        compiler_params=pltpu.CompilerParams(dimension_semantics=("arbitrary",), vmem_limit_bytes=_vmem(need)),
    )(*row_ins, *vec_ins)


def _acc_rows(ref, rows):
    @pl.when(pl.program_id(0) == 0)
    def _():
        ref[...] = jnp.zeros_like(ref)
    for n, r in enumerate(rows):
        ref[n:n + 1, :] += r


def _pre_norm(x, g, sc, sh):
    S, D = x.shape

    def body(x_ref, g_ref, sc_ref, sh_ref, h_ref):
        xv = x_ref[...]
        xn = xv * _rstd(xv)
        h_ref[...] = (xn * g_ref[...] * (1.0 + sc_ref[...]) + sh_ref[...]).astype(BF16)

    return _row_call("pre_norm_mix", body, [x], [g, sc, sh], [BF16], [], S, D, 256)[0]


def _post_mix(x, mix, g_post, gt, g_pre, sc, sh):
    S, D = x.shape

    def body(x_ref, mix_ref, gp_ref, gt_ref, g2_ref, sc_ref, sh_ref, x1_ref, h2_ref):
        mv = mix_ref[...].astype(F32)
        x1 = x_ref[...] + gt_ref[...] * (mv * _rstd(mv) * gp_ref[...])
        x1_ref[...] = x1
        h2_ref[...] = (x1 * _rstd(x1) * g2_ref[...] * (1.0 + sc_ref[...]) + sh_ref[...]).astype(BF16)

    return _row_call("post_mix_pre_mlp", body, [x, mix], [g_post, gt, g_pre, sc, sh], [F32, BF16], [], S, D, 256)


def _loss_and_post_mlp_bwd(x1, y, target, g_post, gt):
    S, D = x1.shape

    def body(x1_ref, y_ref, t_ref, g_ref, gt_ref, dy_ref, dout_ref, loss_ref, acc_ref):
        yv = y_ref[...].astype(F32)
        r = _rstd(yv)
        yh = yv * r
        n = yh * g_ref[...]
        diff = x1_ref[...] + gt_ref[...] * n - t_ref[...]
        dout = diff * (1.0 / D)
        dout_ref[...] = dout
        dn = dout * gt_ref[...]
        dyh = dn * g_ref[...]
        dy_ref[...] = (r * (dyh - yh * jnp.mean(dyh * yh, axis=-1, keepdims=True))).astype(BF16)
        _acc_rows(acc_ref, [jnp.sum(dout * n, axis=0, keepdims=True), jnp.sum(dn * yh, axis=0, keepdims=True)])

        @pl.when(pl.program_id(0) == 0)
        def _():
            loss_ref[...] = jnp.zeros_like(loss_ref)
        loss_ref[...] += jnp.full(loss_ref.shape, (0.5 / D) * jnp.sum(diff * diff), F32)

    return _row_call("loss_post_mlp_bwd", body, [x1, y, target], [g_post, gt], [BF16, F32],
                     [(8, LANES), (8, D)], S, D, 128)


def _pre_mlp_and_post_mix_bwd(dh2, x1, dout, mix, g_pre, sc, g_post, gt):
    S, D = x1.shape

    def body(dh_ref, x1_ref, dout_ref, mix_ref, g_ref, sc_ref, gp_ref, gt_ref, dx1_ref, dmix_ref, acc_ref):
        dh = dh_ref[...].astype(F32)
        x1v = x1_ref[...]
        r3 = _rstd(x1v)
        xn = x1v * r3
        dxn = dh * (1.0 + sc_ref[...]) * g_ref[...]
        dx1 = dout_ref[...] + r3 * (dxn - xn * jnp.mean(dxn * xn, axis=-1, keepdims=True))
        dx1_ref[...] = dx1
        mv = mix_ref[...].astype(F32)
        r2 = _rstd(mv)
        mh = mv * r2
        dn = dx1 * gt_ref[...]
        dmh = dn * gp_ref[...]
        dmix_ref[...] = (r2 * (dmh - mh * jnp.mean(dmh * mh, axis=-1, keepdims=True))).astype(BF16)
        _acc_rows(acc_ref, [
            jnp.sum(dh, axis=0, keepdims=True),
            jnp.sum(dh * xn * g_ref[...], axis=0, keepdims=True),
            jnp.sum(dh * (1.0 + sc_ref[...]) * xn, axis=0, keepdims=True),
            jnp.sum(dx1 * mh * gp_ref[...], axis=0, keepdims=True),
            jnp.sum(dn * mh, axis=0, keepdims=True)])

    return _row_call("pre_mlp_post_mix_bwd", body, [dh2, x1, dout, mix], [g_pre, sc, g_post, gt], [F32, BF16],
                     [(8, D)], S, D, 128)


def _pre_mix_bwd(dh, x, dx1, g_pre, sc):
    S, D = x.shape

    def body(dh_ref, x_ref, dx1_ref, g_ref, sc_ref, gx_ref, acc_ref):
        dhv = dh_ref[...].astype(F32)
        xv = x_ref[...]
        r = _rstd(xv)
        xn = xv * r
        dxn = dhv * (1.0 + sc_ref[...]) * g_ref[...]
        gx_ref[...] = dx1_ref[...] + r * (dxn - xn * jnp.mean(dxn * xn, axis=-1, keepdims=True))
        _acc_rows(acc_ref, [
            jnp.sum(dhv, axis=0, keepdims=True),
            jnp.sum(dhv * xn * g_ref[...], axis=0, keepdims=True),
            jnp.sum(dhv * (1.0 + sc_ref[...]) * xn, axis=0, keepdims=True)])

    return _row_call("pre_mix_bwd", body, [dh, x, dx1], [g_pre, sc], [F32], [(8, D)], S, D, 128)


CUM_BLOCK = 256


def _tri(n, upper):
    r = lax.broadcasted_iota(jnp.int32, (n, n), 0)
    c = lax.broadcasted_iota(jnp.int32, (n, n), 1)
    return ((c >= r) if upper else (c <= r)).astype(F32)


def _fox_gate_fwd(fg, b_pad):
    S = fg.shape[0]
    cb = _fit(CUM_BLOCK, S)

    def body(fg_ref, b_ref, cumt_ref, cum_ref):
        low = _tri(cb, False)
        carry = jnp.zeros((1, LANES), F32)
        for n in range(S // cb):
            z = fg_ref[n * cb:(n + 1) * cb, :] + b_ref[...]
            logf = jnp.minimum(z, 0.0) - jnp.log(1.0 + jnp.exp(-jnp.abs(z)))
            blk = jnp.dot(low, logf, precision=lax.Precision.HIGHEST, preferred_element_type=F32) + carry
            cum_ref[n * cb:(n + 1) * cb, :] = blk
            carry = blk[cb - 1:cb, :]
        cumt_ref[...] = cum_ref[...].T

    return pl.pallas_call(
        body, name="fox_gate_fwd", out_shape=jax.ShapeDtypeStruct((LANES, S), F32),
        scratch_shapes=[pltpu.VMEM((S, LANES), F32)],
        compiler_params=pltpu.CompilerParams(vmem_limit_bytes=_vmem(6 * S * LANES * 4)),
    )(fg, b_pad)


def _fox_gate_bwd(dcum_k, dcum_q, fg, b_pad):
    S = fg.shape[0]
    n_fox = dcum_q.shape[0]
    cb = _fit(CUM_BLOCK, S)

    def body(dk_ref, dq_ref, fg_ref, b_ref, dfg_ref, db_ref, dc_ref):
        lane = lax.broadcasted_iota(jnp.int32, (S, LANES), 1)
        dc = dk_ref[...].T
        for h in range(n_fox):
            dc = dc + jnp.where(lane == h, dq_ref[h], 0.0)
        dc_ref[...] = dc
        up = _tri(cb, True)
        carry = jnp.zeros((1, LANES), F32)
        db = jnp.zeros((1, LANES), F32)
        for n in reversed(range(S // cb)):
            blk = jnp.dot(up, dc_ref[n * cb:(n + 1) * cb, :], precision=lax.Precision.HIGHEST,
                          preferred_element_type=F32) + carry
            carry = blk[0:1, :]
            z = fg_ref[n * cb:(n + 1) * cb, :] + b_ref[...]
            dfg = blk * (1.0 / (1.0 + jnp.exp(z)))
            dfg_ref[n * cb:(n + 1) * cb, :] = dfg.astype(BF16)
            db = db + jnp.sum(dfg, axis=0, keepdims=True)
        db_ref[...] = jnp.broadcast_to(db, db_ref.shape)

    return pl.pallas_call(
        body, name="fox_gate_bwd",
        out_shape=[jax.ShapeDtypeStruct((S, LANES), BF16), jax.ShapeDtypeStruct((8, LANES), F32)],
        scratch_shapes=[pltpu.VMEM((S, LANES), F32)],
        compiler_params=pltpu.CompilerParams(vmem_limit_bytes=_vmem((8 + 2 * n_fox) * S * LANES * 4)),
    )(dcum_k, dcum_q, fg, b_pad)


FOX_TILE = 512


LOG2E = 1.4426950408889634


def _fox_scores(q, k, ck2, masked, t):
    s = lax.dot_general(q, k, _NT, preferred_element_type=F32) * (HEAD_DIM ** -0.5 * LOG2E) - ck2
    if masked:
        row = lax.broadcasted_iota(jnp.int32, (t, t), 0)
        col = lax.broadcasted_iota(jnp.int32, (t, t), 1)
        s = jnp.where(col <= row, s, NEG)
    return s


def _fox_fwd(proj, cum_row, n_fox):
    S = proj.shape[0]
    t = _fit(FOX_TILE, S)
    nq = S // t

    def body(q_ref, k_ref, v_ref, ck_ref, o_ref, lse_ref):
        def q_block(qi, _):
            q0 = pl.multiple_of(qi * t, t)
            q = q_ref[pl.ds(q0, t), :]

            def kv_block(j, carry, masked):
                m, l, acc = carry
                k0 = pl.multiple_of(j * t, t)
                s = _fox_scores(q, k_ref[pl.ds(k0, t), :], ck_ref[0, :, pl.ds(k0, t)] * LOG2E, masked, t)
                m_new = jnp.maximum(m, jnp.max(s, axis=-1, keepdims=True))
                alpha = jnp.exp2(m - m_new)
                p = jnp.exp2(s - m_new)
                l = alpha * l + jnp.sum(p, axis=-1, keepdims=True)
                acc = alpha * acc + jnp.dot(p.astype(BF16), v_ref[pl.ds(k0, t), :], preferred_element_type=F32)
                return m_new, l, acc

            init = (jnp.full((t, 1), NEG, F32), jnp.zeros((t, 1), F32), jnp.zeros((t, HEAD_DIM), F32))
            carry = lax.fori_loop(0, qi, lambda j, cr: kv_block(j, cr, False), init)
            m, l, acc = kv_block(qi, carry, True)
            o_ref[pl.ds(q0, t), :] = acc / l
            lse_ref[0, pl.ds(q0, t), :] = jnp.broadcast_to(m + jnp.log(l) * LOG2E, (t, LANES))
            return 0

        lax.fori_loop(0, nq, q_block, 0)

    col = lambda off: pl.BlockSpec((S, HEAD_DIM), lambda h: (0, off + h))
    per_head = pl.BlockSpec((1, S, LANES), lambda h: (h, 0, 0))
    return pl.pallas_call(
        body, name="fox_fwd", grid=(n_fox,),
        in_specs=[col(0), col(n_fox), col(2 * n_fox), pl.BlockSpec((1, 1, S), lambda h: (h, 0, 0))],
        out_specs=[pl.BlockSpec((S, HEAD_DIM), lambda h: (0, h)), per_head],
        out_shape=[jax.ShapeDtypeStruct((S, n_fox * HEAD_DIM), F32), jax.ShapeDtypeStruct((n_fox, S, LANES), F32)],
        compiler_params=pltpu.CompilerParams(dimension_semantics=("parallel",),
                                             vmem_limit_bytes=_vmem(16 * S * HEAD_DIM * 4 + 12 * t * t * 4)),
    )(proj, proj, proj, cum_row)


def _fox_bwd(proj, o, do, lse_b, cum_row, n_fox):
    S = proj.shape[0]
    t = _fit(FOX_TILE, S)
    nq = S // t
    scale = HEAD_DIM ** -0.5

    def body(q_ref, k_ref, v_ref, o_ref, do_ref, lse_ref, ck_ref, dq_ref, dk_ref, dv_ref, dc_ref, dcq_ref,
             dq_acc, delta_ref):
        dq_acc[...] = jnp.zeros_like(dq_acc)
        dcq_ref[...] = jnp.zeros_like(dcq_ref)

        def delta_block(qi, _):
            q0 = pl.multiple_of(qi * t, t)
            d = jnp.sum(do_ref[pl.ds(q0, t), :] * o_ref[pl.ds(q0, t), :], axis=-1, keepdims=True)
            delta_ref[pl.ds(q0, t), :] = jnp.broadcast_to(d, (t, LANES))
            return 0

        lax.fori_loop(0, nq, delta_block, 0)

        def kv_block(j, _):
            k0 = pl.multiple_of(j * t, t)
            k = k_ref[pl.ds(k0, t), :]
            v = v_ref[pl.ds(k0, t), :]
            ck2 = ck_ref[0, :, pl.ds(k0, t)] * LOG2E

            def q_block(qi, carry, masked):
                dk, dv, dc = carry
                q0 = pl.multiple_of(qi * t, t)
                q = q_ref[pl.ds(q0, t), :]
                dov = do_ref[pl.ds(q0, t), :].astype(BF16)
                p = jnp.exp2(_fox_scores(q, k, ck2, masked, t) - lse_ref[0, pl.ds(q0, t), :][:, :1])
                dp = lax.dot_general(dov, v, _NT, preferred_element_type=F32)
                ds = p * (dp - delta_ref[pl.ds(q0, t), :][:, :1])
                dsb = ds.astype(BF16)
                dv = dv + lax.dot_general(p.astype(BF16), dov, _TN, preferred_element_type=F32)
                dk = dk + lax.dot_general(dsb, q, _TN, preferred_element_type=F32)
                dq_acc[pl.ds(q0, t), :] += jnp.dot(dsb, k, preferred_element_type=F32)
                dc = dc - jnp.sum(ds, axis=0, keepdims=True)
                dcq_ref[0, pl.ds(q0, t), :] += jnp.broadcast_to(jnp.sum(ds, axis=1, keepdims=True), (t, LANES))
                return dk, dv, dc

            init = (jnp.zeros((t, HEAD_DIM), F32), jnp.zeros((t, HEAD_DIM), F32), jnp.zeros((1, t), F32))
            carry = q_block(j, init, True)
            dk, dv, dc = lax.fori_loop(j + 1, nq, lambda qi, cr: q_block(qi, cr, False), carry)
            dk_ref[pl.ds(k0, t), :] = (dk * scale).astype(BF16)
            dv_ref[pl.ds(k0, t), :] = dv.astype(BF16)
            dc_ref[0, :, pl.ds(k0, t)] = dc
            return 0

        lax.fori_loop(0, nq, kv_block, 0)
        dq_ref[...] = (dq_acc[...] * scale).astype(BF16)

    col = lambda off: pl.BlockSpec((S, HEAD_DIM), lambda h: (0, off + h))
    per_head = pl.BlockSpec((1, S, LANES), lambda h: (h, 0, 0))
    row = pl.BlockSpec((1, 1, S), lambda h: (h, 0, 0))
    grad = jax.ShapeDtypeStruct((S, n_fox * HEAD_DIM), BF16)
    return pl.pallas_call(
        body, name="fox_bwd", grid=(n_fox,),
        in_specs=[col(0), col(n_fox), col(2 * n_fox), col(0), col(0), per_head, row],
        out_specs=[col(0), col(0), col(0), row, per_head],
        out_shape=[grad, grad, grad, jax.ShapeDtypeStruct((n_fox, 1, S), F32), jax.ShapeDtypeStruct((n_fox, S, LANES), F32)],
        scratch_shapes=[pltpu.VMEM((S, HEAD_DIM), F32), pltpu.VMEM((S, LANES), F32)],
        compiler_params=pltpu.CompilerParams(dimension_semantics=("parallel",),
                                             vmem_limit_bytes=_vmem(24 * S * HEAD_DIM * 4 + 16 * t * t * 4)),
    )(proj, proj, proj, o, do, lse_b, cum_row)


def _rope_tables(S):
    half = HEAD_DIM // 2
    inv_freq = 1.0 / (ROPE_THETA ** (jnp.arange(half, dtype=F32) * (2.0 / HEAD_DIM)))
    ang = jnp.arange(S).astype(F32)[:, None] * inv_freq[None, :]
    cos, sin = jnp.cos(ang), jnp.sin(ang)
    return jnp.concatenate([cos, cos], axis=-1), jnp.concatenate([-sin, sin], axis=-1)


def _rope(name, src, first_block, n_blocks, cos, sin_signed):
    S = src.shape[0]

    def body(x_ref, cos_ref, sin_ref, o_ref):
        xv = x_ref[...].astype(F32)
        o_ref[...] = (xv * cos_ref[...] + pltpu.roll(xv, HEAD_DIM // 2, 1) * sin_ref[...]).astype(BF16)

    table = pl.BlockSpec((S, HEAD_DIM), lambda n: (0, 0))
    return pl.pallas_call(
        body, name=name, grid=(n_blocks,),
        in_specs=[pl.BlockSpec((S, HEAD_DIM), lambda n: (0, first_block + n)), table, table],
        out_specs=pl.BlockSpec((S, HEAD_DIM), lambda n: (0, n)),
        out_shape=jax.ShapeDtypeStruct((S, n_blocks * HEAD_DIM), BF16),
        compiler_params=pltpu.CompilerParams(dimension_semantics=("parallel",),
                                             vmem_limit_bytes=_vmem(12 * S * HEAD_DIM * 4)),
    )(src, cos, sin_signed)


def _swa_tile(q_ref, kp_ref, kc_ref, n, group, scale):
    B = SWA_BLOCK
    qs = jnp.concatenate([q_ref[:, g * HEAD_DIM:(g + 1) * HEAD_DIM] for g in range(group)], axis=0)
    kcat = jnp.concatenate([kp_ref[...], kc_ref[...]], axis=0)
    s = lax.dot_general(qs, kcat, _NT, preferred_element_type=F32) * scale
    qi = lax.broadcasted_iota(jnp.int32, (group * B, 2 * B), 0) % B
    kj = lax.broadcasted_iota(jnp.int32, (group * B, 2 * B), 1)
    diff = qi + B - kj
    mask = (diff >= 0) & (diff < B) & ((n * B + kj - B) >= 0)
    return qs, kcat, jnp.where(mask, s, NEG)


def _swa_sink_col(sink_ref, kv, group):
    head = lax.broadcasted_iota(jnp.int32, (group * SWA_BLOCK, 1), 0) // SWA_BLOCK
    col = jnp.zeros((group * SWA_BLOCK, 1), F32)
    for g in range(group):
        col = jnp.where(head == g, sink_ref[kv * group + g], col)
    return col


def _swa_specs(n_kv, group, q_first, k_first, v_first):
    B = SWA_BLOCK
    prev = lambda n: jnp.maximum(n - 1, 0)
    return [
        pl.BlockSpec((B, group * HEAD_DIM), lambda kv, n: (n, q_first + kv)),
        pl.BlockSpec((B, HEAD_DIM), lambda kv, n: (prev(n), k_first + kv)),
        pl.BlockSpec((B, HEAD_DIM), lambda kv, n: (n, k_first + kv)),
        pl.BlockSpec((B, HEAD_DIM), lambda kv, n: (prev(n), v_first + kv)),
        pl.BlockSpec((B, HEAD_DIM), lambda kv, n: (n, v_first + kv)),
    ]


def _swa_fwd(rq, proj, v_first, sinks, n_q, n_kv):
    S = rq.shape[0]
    B = SWA_BLOCK
    group = n_q // n_kv
    scale = HEAD_DIM ** -0.5

    def body(q_ref, kp_ref, kc_ref, vp_ref, vc_ref, sink_ref, o_ref, lse_ref):
        kv, n = pl.program_id(0), pl.program_id(1)
        _, _, s = _swa_tile(q_ref, kp_ref, kc_ref, n, group, scale)
        sink = _swa_sink_col(sink_ref, kv, group)
        m = jnp.maximum(jnp.max(s, axis=-1, keepdims=True), sink)
        p = jnp.exp(s - m)
        denom = jnp.sum(p, axis=-1, keepdims=True) + jnp.exp(sink - m)
        vcat = jnp.concatenate([vp_ref[...], vc_ref[...]], axis=0)
        o = jnp.dot((p / denom).astype(BF16), vcat, preferred_element_type=F32)
        lse = m + jnp.log(denom)
        for g in range(group):
            o_ref[:, g * HEAD_DIM:(g + 1) * HEAD_DIM] = o[g * B:(g + 1) * B, :]
            lse_ref[0, :, g * LANES:(g + 1) * LANES] = jnp.broadcast_to(lse[g * B:(g + 1) * B, :], (B, LANES))

    specs = _swa_specs(n_kv, group, 0, n_q, v_first)
    q_blk = pl.BlockSpec((B, group * HEAD_DIM), lambda kv, n: (n, kv))
    return pl.pallas_call(
        body, name="swa_fwd", grid=(n_kv, S // B),
        in_specs=specs + [pl.BlockSpec(memory_space=pltpu.SMEM)],
        out_specs=[q_blk, pl.BlockSpec((1, B, group * LANES), lambda kv, n: (kv, n, 0))],
        out_shape=[jax.ShapeDtypeStruct((S, n_q * HEAD_DIM), F32), jax.ShapeDtypeStruct((n_kv, S, group * LANES), F32)],
        compiler_params=pltpu.CompilerParams(dimension_semantics=("parallel", "arbitrary")),
    )(rq, rq, rq, proj, proj, sinks)


def _swa_bwd(rq, proj, v_first, sinks, o, do, do_first, lse_b, n_q, n_kv):
    S = rq.shape[0]
    B = SWA_BLOCK
    group = n_q // n_kv
    scale = HEAD_DIM ** -0.5

    def body(q_ref, kp_ref, kc_ref, vp_ref, vc_ref, o_ref, do_ref, lse_ref, sink_ref,
             dq_ref, dk_ref, dv_ref, dsink_ref):
        kv, n = pl.program_id(0), pl.program_id(1)

        @pl.when(n == 0)
        def _():
            dk_ref[...] = jnp.zeros_like(dk_ref)
            dv_ref[...] = jnp.zeros_like(dv_ref)
            dsink_ref[...] = jnp.zeros_like(dsink_ref)

        qs, kcat, s = _swa_tile(q_ref, kp_ref, kc_ref, n, group, scale)
        sink = _swa_sink_col(sink_ref, kv, group)
        stack = lambda ref, w: jnp.concatenate([ref[:, g * w:(g + 1) * w] for g in range(group)], axis=0)
        lse = jnp.concatenate([lse_ref[0, :, g * LANES:g * LANES + 1] for g in range(group)], axis=0)
        do32 = stack(do_ref, HEAD_DIM)
        delta = jnp.sum(do32 * stack(o_ref, HEAD_DIM), axis=-1, keepdims=True)
        dov = do32.astype(BF16)
        p = jnp.exp(s - lse)
        vcat = jnp.concatenate([vp_ref[...], vc_ref[...]], axis=0)
        dp = lax.dot_general(dov, vcat, _NT, preferred_element_type=F32)
        ds = p * (dp - delta)
        dsb = ds.astype(BF16)
        dq = jnp.dot(dsb, kcat, preferred_element_type=F32) * scale
        for g in range(group):
            dq_ref[:, g * HEAD_DIM:(g + 1) * HEAD_DIM] = dq[g * B:(g + 1) * B, :].astype(BF16)
        dkcat = lax.dot_general(dsb, qs, _TN, preferred_element_type=F32) * scale
        dvcat = lax.dot_general(p.astype(BF16), dov, _TN, preferred_element_type=F32)
        prev0 = pl.multiple_of(jnp.maximum(n - 1, 0) * B, B)
        cur0 = pl.multiple_of(n * B, B)
        dk_ref[0, pl.ds(prev0, B), :] += dkcat[:B, :]
        dk_ref[0, pl.ds(cur0, B), :] += dkcat[B:, :]
        dv_ref[0, pl.ds(prev0, B), :] += dvcat[:B, :]
        dv_ref[0, pl.ds(cur0, B), :] += dvcat[B:, :]
        dsk = -jnp.exp(sink - lse) * delta
        lane = lax.broadcasted_iota(jnp.int32, (1, LANES), 1)
        row = jnp.zeros((1, LANES), F32)
        for g in range(group):
            row = row + jnp.where(lane == g, jnp.sum(dsk[g * B:(g + 1) * B, :]), 0.0)
        dsink_ref[0, 0:1, :] += row

    specs = _swa_specs(n_kv, group, 0, n_q, v_first)
    q_blk = pl.BlockSpec((B, group * HEAD_DIM), lambda kv, n: (n, kv))
    acc = pl.BlockSpec((1, S, HEAD_DIM), lambda kv, n: (kv, 0, 0))
    return pl.pallas_call(
        body, name="swa_bwd", grid=(n_kv, S // B),
        in_specs=specs + [q_blk, pl.BlockSpec((B, group * HEAD_DIM), lambda kv, n: (n, do_first + kv)),
                          pl.BlockSpec((1, B, group * LANES), lambda kv, n: (kv, n, 0)),
                          pl.BlockSpec(memory_space=pltpu.SMEM)],
        out_specs=[q_blk, acc, acc, pl.BlockSpec((1, 8, LANES), lambda kv, n: (kv, 0, 0))],
        out_shape=[jax.ShapeDtypeStruct((S, n_q * HEAD_DIM), BF16), jax.ShapeDtypeStruct((n_kv, S, HEAD_DIM), F32),
                   jax.ShapeDtypeStruct((n_kv, S, HEAD_DIM), F32), jax.ShapeDtypeStruct((n_kv, 8, LANES), F32)],
        compiler_params=pltpu.CompilerParams(dimension_semantics=("parallel", "arbitrary")),
    )(rq, rq, rq, proj, proj, o, do, lse_b, sinks)


def _adamw(w, g, m, v):
    m = ADAM_B1 * m + (1.0 - ADAM_B1) * g
    v = ADAM_B2 * v + (1.0 - ADAM_B2) * (g * g)
    m_hat = m / (1.0 - ADAM_B1 ** ADAM_STEP)
    v_hat = v / (1.0 - ADAM_B2 ** ADAM_STEP)
    delta = -ADAM_LR * (m_hat / (jnp.sqrt(v_hat) + ADAM_EPS) + ADAM_WD * w)
    return delta, m, v


def _mod_fwd(cond_in, w_mod, b_shard):
    R, D = cond_in.shape
    cols = w_mod.shape[1]
    tn = _fit(512, cols)

    def body(c_ref, w_ref, b_ref, o_ref):
        cv = c_ref[...]
        cond = (cv / (1.0 + jnp.exp(-cv))).astype(BF16)
        o_ref[...] = jnp.dot(cond, w_ref[...].astype(BF16), preferred_element_type=F32) + b_ref[...]

    return pl.pallas_call(
        body, name="mod_fwd", grid=(cols // tn,),
        in_specs=[pl.BlockSpec((R, D), lambda j: (0, 0)), pl.BlockSpec((D, tn), lambda j: (0, j)),
                  pl.BlockSpec((1, tn), lambda j: (0, j))],
        out_specs=pl.BlockSpec((R, tn), lambda j: (0, j)),
        out_shape=jax.ShapeDtypeStruct((R, cols), F32),
        compiler_params=pltpu.CompilerParams(dimension_semantics=("parallel",), vmem_limit_bytes=_vmem(3 * D * tn * 4)),
    )(cond_in, w_mod, b_shard)


def _mod_update(c_t, dmod, w, m, v):
    D, nb = c_t.shape
    cols = w.shape[1]
    tn = _fit(256, cols)

    def body(c_ref, d_ref, w_ref, m_ref, v_ref, g_ref, dl_ref, nm_ref, nv_ref):
        cv = c_ref[...]
        cond = cv / (1.0 + jnp.exp(-cv))
        g = jnp.zeros((D, tn), F32)
        for b in range(nb):
            g = g + cond[:, b:b + 1] * d_ref[b:b + 1, :]
        g_ref[...] = g
        dl_ref[...], nm_ref[...], nv_ref[...] = _adamw(w_ref[...], g, m_ref[...], v_ref[...])

    blk = pl.BlockSpec((D, tn), lambda j: (0, j))
    out = jax.ShapeDtypeStruct((D, cols), F32)
    return pl.pallas_call(
        body, name="mod_update", grid=(cols // tn,),
        in_specs=[pl.BlockSpec((D, nb), lambda j: (0, 0)), pl.BlockSpec((nb, tn), lambda j: (0, j)), blk, blk, blk],
        out_specs=[blk] * 4, out_shape=[out] * 4,
        compiler_params=pltpu.CompilerParams(dimension_semantics=("parallel",), vmem_limit_bytes=_vmem(18 * D * tn * 4)),
    )(c_t, dmod, w, m, v)


def _small_update(stacked, w, m, v):
    R, C = w.shape

    def body(s_ref, w_ref, m_ref, v_ref, g_ref, dl_ref, nm_ref, nv_ref):
        g = s_ref[0:R, :]
        for d in range(1, N_DEV):
            g = g + s_ref[d * R:(d + 1) * R, :]
        g_ref[...] = g
        dl_ref[...], nm_ref[...], nv_ref[...] = _adamw(w_ref[...], g, m_ref[...], v_ref[...])

    return pl.pallas_call(body, name="small_update", out_shape=[jax.ShapeDtypeStruct((R, C), F32)] * 4)(stacked, w, m, v)


def _place():
    return lax.axis_index("x"), lax.axis_index("y"), lax.axis_index("c")


def _allgather8(name, block):
    m_per, n = block.shape

    def body(x_ref, out_ref, token_ref, send_sems, recv_sems, local_sem):
        token_ref[...] = jnp.zeros_like(token_ref)
        x, y, c = _place()
        me, sibling = (x, y, c), (x, y, 1 - c)
        chips = [(1 - x, y), (x, 1 - y), (1 - x, 1 - y)]

        def rows(px, py, pc):
            return out_ref.at[pl.ds((4 * px + 2 * py + pc) * m_per, m_per), :]

        def copy(k, blk, to, src=None):
            return pltpu.make_async_remote_copy(
                src_ref=rows(*blk) if src is None else src, dst_ref=rows(*blk),
                send_sem=send_sems.at[k], recv_sem=recv_sems.at[k], device_id=to, device_id_type=MESH)

        mine = pltpu.make_async_copy(x_ref, rows(*me), local_sem)
        mine.start()
        first = [copy(0, me, sibling, src=x_ref)]
        first += [copy(1 + j, me, (*chip, c), src=x_ref) for j, chip in enumerate(chips)]
        for cp in first:
            cp.start()
        passed = [copy(4 + j, (*chip, c), sibling) for j, chip in enumerate(chips)]
        for j, chip in enumerate(chips):
            copy(1 + j, (*chip, c), me).wait_recv()
            passed[j].start()
        copy(0, sibling, me).wait_recv()
        for j, chip in enumerate(chips):
            copy(4 + j, (*chip, 1 - c), me).wait_recv()
        for cp in first + passed:
            cp.wait_send()
        mine.wait()

    vmem = pl.BlockSpec(memory_space=pltpu.VMEM)
    return pl.pallas_call(
        body, name=name,
        out_shape=[jax.ShapeDtypeStruct((N_DEV * m_per, n), block.dtype), jax.ShapeDtypeStruct((8, LANES), F32)],
        in_specs=[vmem], out_specs=[vmem, vmem],
        scratch_shapes=[pltpu.SemaphoreType.DMA((7,)), pltpu.SemaphoreType.DMA((7,)), pltpu.SemaphoreType.DMA],
    )(block)


_ANY = pl.BlockSpec(memory_space=pl.ANY)


def _half(ref, c, rows):
    return ref.at[pl.ds(c * (rows // 2), rows // 2), :]


_HBM = pl.BlockSpec(memory_space=pltpu.HBM)
_SEM = pl.BlockSpec(memory_space=pltpu.SEMAPHORE)
_EFFECT = pltpu.SideEffectType.DATAFLOW_SIDE_EFFECTING


def _ici_start(name, srcs, land_shapes, plan, per_source=3):
    ns, nl = len(srcs), len(land_shapes)
    n_copies = per_source * ns

    def body(*refs):
        src_refs, land_refs = refs[:ns], refs[ns:ns + nl]
        send_sems, recv_sems = refs[ns + nl], refs[ns + nl + 1]
        token = refs[-1]
        for n, (src, dst, peer, _) in enumerate(plan(src_refs, land_refs)):
            pltpu.make_async_remote_copy(src_ref=src, dst_ref=dst, send_sem=send_sems.at[n], recv_sem=recv_sems.at[n],
                                         device_id=peer, device_id_type=MESH).start()
        token[...] = jnp.zeros_like(token)

    lands = [lax.empty(s.shape, s.dtype) for s in land_shapes]
    out = pl.pallas_call(
        body, name=name,
        out_shape=(pltpu.SemaphoreType.DMA((n_copies,)), pltpu.SemaphoreType.DMA((n_copies,)),
                   *[pltpu.HBM(a.shape, a.dtype) for a in list(srcs) + lands], jax.ShapeDtypeStruct((8, LANES), F32)),
        in_specs=[_HBM] * (ns + nl),
        out_specs=(_SEM, _SEM, *[_HBM] * (ns + nl), pl.BlockSpec(memory_space=pltpu.VMEM)),
        input_output_aliases={n: 2 + n for n in range(ns + nl)},
        compiler_params=pltpu.CompilerParams(has_side_effects=_EFFECT),
    )(*[pltpu.with_memory_space_constraint(a, pltpu.HBM) for a in list(srcs) + lands])
    return out[0], out[1], list(out[2:2 + ns]), list(out[2 + ns:2 + ns + nl]), out[-1]


def _ici_wait(name, send_sems, recv_sems, srcs, lands, plan, after):
    ns, nl = len(srcs), len(lands)

    def body(*refs):
        src_refs, land_refs = refs[:ns], refs[ns:ns + nl]
        send_sems, recv_sems = refs[ns + nl], refs[ns + nl + 1]
        for n, (src, _, peer, mine) in enumerate(plan(src_refs, land_refs)):
            cp = pltpu.make_async_remote_copy(src_ref=src, dst_ref=mine, send_sem=send_sems.at[n],
                                              recv_sem=recv_sems.at[n], device_id=peer, device_id_type=MESH)
            cp.wait_send()
            cp.wait_recv()

    out = pl.pallas_call(
        body, name=name, out_shape=[pltpu.HBM(a.shape, a.dtype) for a in list(srcs) + list(lands)],
        in_specs=[_HBM] * (ns + nl) + [_SEM, _SEM, _ANY], out_specs=[_HBM] * (ns + nl),
        input_output_aliases={n: n for n in range(ns + nl)},
        compiler_params=pltpu.CompilerParams(has_side_effects=_EFFECT),
    )(*srcs, *lands, send_sems, recv_sems, after)
    return list(out[:ns]), list(out[ns:])


def _gather_plan(src_refs, land_refs):
    x, y, c = _place()
    copies = []
    for w, land in zip(src_refs, land_refs):
        R = w.shape[0]
        for cx, cy in [(1 - x, y), (x, 1 - y), (1 - x, 1 - y)]:
            copies.append((_half(w, c, R), _half(land.at[2 * x + y], c, R), (cx, cy, c),
                           _half(land.at[2 * cx + cy], c, R)))
    return copies


def _pass_plan(src_refs, land_refs):
    x, y, c = _place()
    copies = []
    for land in src_refs:
        R = land.shape[1]
        for cx, cy in [(1 - x, y), (x, 1 - y), (1 - x, 1 - y)]:
            slot = land.at[2 * cx + cy]
            copies.append((_half(slot, c, R), _half(slot, c, R), (x, y, 1 - c), _half(slot, 1 - c, R)))
    return copies


def _share_plan(src_refs, land_refs):
    x, y, c = _place()
    return [(h, land, (x, y, 1 - c), land) for h, land in zip(src_refs, land_refs)]


def _pass_to_sibling(name, lands):
    nw = len(lands)

    def body(*refs):
        ins, outs = refs[:nw], refs[nw:2 * nw]
        send_sems, recv_sems = refs[2 * nw:]
        x, y, c = _place()
        chips = [(1 - x, y), (x, 1 - y), (1 - x, 1 - y)]
        copies = []
        for k in range(nw):
            R = ins[k].shape[1]
            for j, (cx, cy) in enumerate(chips):
                cp = pltpu.make_async_remote_copy(
                    src_ref=_half(ins[k].at[2 * cx + cy], c, R), dst_ref=_half(outs[k].at[2 * cx + cy], c, R),
                    send_sem=send_sems.at[3 * k + j], recv_sem=recv_sems.at[3 * k + j],
                    device_id=(x, y, 1 - c), device_id_type=MESH)
                cp.start()
                copies.append(cp)
        for k in range(nw):
            R = ins[k].shape[1]
            for j, (cx, cy) in enumerate(chips):
                pltpu.make_async_remote_copy(
                    src_ref=_half(ins[k].at[2 * cx + cy], c, R), dst_ref=_half(outs[k].at[2 * cx + cy], 1 - c, R),
                    send_sem=send_sems.at[3 * k + j], recv_sem=recv_sems.at[3 * k + j],
                    device_id=(x, y, 1 - c), device_id_type=MESH).wait_recv()
        for cp in copies:
            cp.wait_send()

    return pl.pallas_call(
        body, name=name, out_shape=[jax.ShapeDtypeStruct(a.shape, a.dtype) for a in lands],
        in_specs=[_ANY] * nw, out_specs=[_ANY] * nw, input_output_aliases={k: k for k in range(nw)},
        scratch_shapes=[pltpu.SemaphoreType.DMA((3 * nw,)), pltpu.SemaphoreType.DMA((3 * nw,))],
    )(*lands)


def _tie(vec, token):
    return vec + token[0:1, 0:1]


ROW_ALIGN = 16
TILE_ELEMS = 512 * 1024


def _tiles(rows, cols):
    fits = [t for t in range(ROW_ALIGN, min(rows, 256) + 1, ROW_ALIGN) if rows % t == 0]
    tr = fits[-1] if fits and fits[-1] >= 64 else rows
    tc = cols
    while tr * tc > TILE_ELEMS and tc % (2 * LANES) == 0:
        tc //= 2
    return tr, tc


def _scatter_plan(src_refs, land_refs):
    x, y, c = _place()
    copies = []
    for p, land in zip(src_refs, land_refs):
        for j, (cx, cy) in enumerate([(1 - x, y), (x, 1 - y), (1 - x, 1 - y)]):
            copies.append((p.at[2 * cx + cy], land.at[j], (cx, cy, c), land.at[j]))
    return copies


def _chip_add(name, chip, sums, recv):
    _, H, C = sums.shape
    tr, tc = _tiles(H, C)

    def body(chip_ref, p_ref, r_ref, o_ref):
        total = p_ref[0].astype(F32)
        for j in range(3):
            total = total + r_ref[j].astype(F32)
        o_ref[...] = total

    grid_spec = pltpu.PrefetchScalarGridSpec(
        num_scalar_prefetch=1, grid=(H // tr, C // tc),
        in_specs=[pl.BlockSpec((1, tr, tc), lambda r, q, chip_ref: (chip_ref[0], r, q)),
                  pl.BlockSpec((3, tr, tc), lambda r, q, chip_ref: (0, r, q))],
        out_specs=pl.BlockSpec((tr, tc), lambda r, q, chip_ref: (r, q)))
    return pl.pallas_call(
        body, name=name, grid_spec=grid_spec, out_shape=jax.ShapeDtypeStruct((H, C), F32),
        compiler_params=pltpu.CompilerParams(dimension_semantics=("parallel", "parallel")),
    )(chip, sums, recv)


def _pair_share(name, halves):
    nw = len(halves)

    def body(*refs):
        hs, outs = refs[:nw], refs[nw:2 * nw]
        send_sems, recv_sems = refs[2 * nw:]
        x, y, c = _place()
        copies = []
        for k in range(nw):
            cp = pltpu.make_async_remote_copy(
                src_ref=hs[k], dst_ref=outs[k], send_sem=send_sems.at[k], recv_sem=recv_sems.at[k],
                device_id=(x, y, 1 - c), device_id_type=MESH)
            cp.start()
            copies.append(cp)
        for cp in copies:
            cp.wait()

    return pl.pallas_call(
        body, name=name,
        out_shape=[jax.ShapeDtypeStruct(h.shape, h.dtype) for h in halves],
        in_specs=[_ANY] * nw, out_specs=[_ANY] * nw,
        scratch_shapes=[pltpu.SemaphoreType.DMA((nw,)), pltpu.SemaphoreType.DMA((nw,))],
    )(*halves)


def _adam_halves(name, core, w, g_own, g_other, m, v):
    R, C = w.shape
    H = R // 2
    tr, tc = _tiles(H, C)
    nr, nc = H // tr, C // tc

    def body(core_ref, w_ref, go_ref, gr_ref, m_ref, v_ref, g_ref, dl_ref, nm_ref, nv_ref):
        own = (pl.program_id(0) // nr) == core_ref[0]
        g = jnp.where(own, go_ref[...], gr_ref[...])
        g_ref[...] = g
        dl_ref[...], nm_ref[...], nv_ref[...] = _adamw(w_ref[...], g, m_ref[...], v_ref[...])

    blk = pl.BlockSpec((tr, tc), lambda r, q, core_ref: (r, q))

    def half_spec(is_own):
        def index(r, q, core_ref):
            mine = ((r // nr) == core_ref[0]) == is_own
            done = is_own == (core_ref[0] == 0)
            return (jnp.where(mine, r % nr, jnp.where(done, nr - 1, 0)), jnp.where(mine, q, jnp.where(done, nc - 1, 0)))
        return pl.BlockSpec((tr, tc), index)
    out = jax.ShapeDtypeStruct((R, C), F32)
    grid_spec = pltpu.PrefetchScalarGridSpec(
        num_scalar_prefetch=1, grid=(R // tr, nc), in_specs=[blk, half_spec(True), half_spec(False), blk, blk],
        out_specs=[blk] * 4)
    return pl.pallas_call(
        body, name=name, grid_spec=grid_spec, out_shape=[out] * 4,
        compiler_params=pltpu.CompilerParams(dimension_semantics=("parallel", "parallel"),
                                             vmem_limit_bytes=_vmem(20 * tr * tc * 4)),
    )(core, w, g_own, g_other, m, v)


def kernel(x, c, w_mod, b_mod, g_pre_mix, g_post_mix, w_in, b_forget, swa_sinks, w_out, g_pre_mlp, g_post_mlp, w_up, w_down, loss_target, m_w_mod, m_b_mod, m_g_pre_mix, m_g_post_mix, m_w_in, m_b_forget, m_swa_sinks, m_w_out, m_g_pre_mlp, m_g_post_mlp, m_w_up, m_w_down, v_w_mod, v_b_mod, v_g_pre_mix, v_g_post_mix, v_w_in, v_b_forget, v_swa_sinks, v_w_out, v_g_pre_mlp, v_g_post_mlp, v_w_up, v_w_down):
    S, D = x.shape[1], x.shape[2]
    n_heads = D // HEAD_DIM
    n_fox = n_heads // 2
    n_swa = n_heads - n_fox
    n_kv = max(1, n_swa // 4)
    fox_w, swa_w, kv_w = n_fox * HEAD_DIM, n_swa * HEAD_DIM, n_kv * HEAD_DIM
    main_w = 3 * fox_w + swa_w + 2 * kv_w
    in_w = main_w + n_fox
    mod_cols = w_mod.shape[2]

    ax, ay, ac = _place()
    chip = 2 * ax + ay
    dev = 2 * chip + ac
    chip_arr = jnp.reshape(chip, (1,)).astype(jnp.int32)
    core_arr = jnp.reshape(ac, (1,)).astype(jnp.int32)

    x2, tgt = x[0], loss_target[0]

    c_all, _ = _allgather8("gather_c", c.reshape(8, D // 8))
    c_all = c_all.reshape(N_DEV, D)
    b_shard = lax.dynamic_slice_in_dim(b_mod, chip * mod_cols, mod_cols, axis=1)
    mod_shard = _mod_fwd(jnp.pad(c_all, ((0, 16 - N_DEV), (0, 0))), w_mod[0], b_shard)[:N_DEV]
    mod_all, token = _allgather8("gather_mod", mod_shard)
    mod_all = mod_all.reshape(N_CHIPS, 2, N_DEV, mod_cols)[:, 0]
    mod = lax.dynamic_index_in_dim(mod_all, dev, axis=1, keepdims=False).reshape(N_MOD, 1, D)
    sh_a, sc_a, gt_a, sh_m, sc_m, gt_m = [mod[n] for n in range(N_MOD)]

    in_rows = in_w // N_CHIPS
    in_rows_pad = -(-in_rows // (2 * LANES)) * (2 * LANES)
    slab_w = N_CHIPS * in_rows_pad

    def rows_of(a):
        return jnp.pad(a[0].T, ((0, in_rows_pad - in_rows), (0, 0)))

    def slab_cols(lo, hi):
        spans = []
        while lo < hi:
            s, r = divmod(lo, in_rows)
            n = min(hi - lo, in_rows - r)
            spans.append((s * in_rows_pad + r, s * in_rows_pad + r + n))
            lo += n
        return spans

    gate_lo = 3 * fox_w
    main_spans = slab_cols(0, gate_lo) + slab_cols(gate_lo + n_fox, in_w)
    (gate_first, gate_last), = slab_cols(gate_lo, gate_lo + n_fox)

    names = ["w_in", "w_out", "w_up", "w_down"]
    flights = {}
    for n, w in zip(names, [rows_of(w_in), w_out[0], w_up[0], w_down[0]]):
        shard = _tie(w, token).astype(BF16)
        flights[n] = _ici_start("gather_start_" + n, [shard], [jax.ShapeDtypeStruct((N_CHIPS,) + shard.shape, BF16)],
                                _gather_plan)
        token = flights[n][4]
    sc_a = _tie(sc_a, token)

    def arrived(n, after):
        send, recv, srcs, lands, _ = flights[n]
        srcs, lands = _ici_wait("gather_wait_" + n, send, recv, srcs, lands, _gather_plan, after)
        return srcs[0], _ici_start("gather_pass_start_" + n, lands, [], _pass_plan)

    def gathered(n, after, in_flight=None):
        if in_flight is None:
            send, recv, srcs, lands, _ = flights[n]
            srcs, lands = _ici_wait("gather_wait_" + n, send, recv, srcs, lands, _gather_plan, after)
            own, stack = srcs[0], _pass_to_sibling("gather_pass_" + n, lands)[0]
        else:
            own, (send, recv, lands, _, _) = in_flight
            stack = _ici_wait("gather_pass_wait_" + n, send, recv, lands, [], _pass_plan, after)[0][0]
        return lax.dynamic_update_index_in_dim(stack, own, chip, 0)

    d_ff = N_CHIPS * w_up.shape[2]

    h = _pre_norm(x2, g_pre_mix, sc_a, sh_a)
    in_state = [rows_of(w_in)] + [rows_of(_tie(a, token)) for a in (m_w_in, v_w_in)]
    cos, sin_signed = _rope_tables(S)
    ready = h[:8, :LANES].astype(F32) + cos[:8] + sum(a[:8, :LANES] for a in in_state)
    w_slab_t = gathered("w_in", ready).reshape(slab_w, D)
    tm_p, tn_p = _fit(MM_TM, S), _fit(MM_TN if slab_w % MM_TN == 0 else MM_TN // 2, slab_w)
    win0 = gate_first // LANES * LANES
    win_j, win_off = divmod(win0, tn_p)
    assert win_off + 2 * LANES <= tn_p and gate_last - win0 <= 2 * LANES

    def proj_epilogue(acc, ex, outs):
        outs[0][...] = acc.astype(BF16)

        @pl.when(pl.program_id(1) == win_j)
        def _():
            outs[1][...] = acc[:, win_off:win_off + 2 * LANES]

    proj_slab, gate_win = _matmul(
        "in_proj", h, w_slab_t, "nt",
        [((S, slab_w), BF16, (tm_p, tn_p), lambda i, j: (i, j)), ((S, 2 * LANES), F32, (tm_p, 2 * LANES), lambda i, j: (i, 0))],
        proj_epilogue, tn=tn_p, revisits=True)
    proj = jnp.concatenate([proj_slab[:, lo:hi] for lo, hi in main_spans], axis=1)
    out_flight = arrived("w_out", proj_slab)
    fg = _tie(jnp.pad(gate_win[:, gate_first - win0:gate_last - win0], ((0, 0), (0, LANES - n_fox))), out_flight[1][4])
    b_pad = jnp.pad(b_forget, ((0, 0), (0, LANES - n_fox)))
    cum_row = _fox_gate_fwd(fg, b_pad)[:n_fox].reshape(n_fox, 1, S)
    fox_o, fox_lse = _fox_fwd(proj, cum_row, n_fox)

    rq = _rope("rope_fwd", proj, 3 * n_fox, n_swa + n_kv, cos, sin_signed)
    v_first = 3 * n_fox + n_swa + n_kv
    sinks = swa_sinks[0]
    swa_o, swa_lse = _swa_fwd(rq, proj, v_first, sinks, n_swa, n_kv)

    mixcat = jnp.concatenate([fox_o, swa_o], axis=1).astype(BF16)
    up_flight = arrived("w_up", mixcat)
    w_out_f = gathered("w_out", mixcat, out_flight).reshape(D, D)
    mix = _mm_plain("out_proj", mixcat, w_out_f, "nn", BF16, after=up_flight[1][4])
    x1, h2 = _post_mix(x2, mix, g_post_mix, gt_a, g_pre_mlp, sc_m, sh_m)
    w_up_f = jnp.transpose(gathered("w_up", h2, up_flight), (1, 0, 2)).reshape(D, d_ff)

    tm_u, tn_u = _fit(MM_TM, S), _fit(MM_TN, d_ff)

    def up_epilogue(acc, ex, outs):
        outs[0][...] = acc.astype(BF16)
        r = jnp.maximum(acc, 0.0)
        outs[1][...] = (r * r).astype(BF16)

    ublk = ((S, d_ff), BF16, (tm_u, tn_u), lambda i, j: (i, j))
    u, a = _matmul("mlp_up", h2, w_up_f, "nn", [ublk, ublk], up_epilogue)
    w_down_f = gathered("w_down", a).reshape(d_ff, D)
    y = _mm_plain("mlp_down", a, w_down_f, "nn", BF16)

    dy, dout, loss_part, acc_mlp_post = _loss_and_post_mlp_bwd(x1, y, tgt, g_post_mlp, gt_m)

    def du_epilogue(acc, ex, outs):
        outs[0][...] = (acc * (2.0 * jnp.maximum(ex[0][...].astype(F32), 0.0))).astype(BF16)

    du = _matmul("mlp_down_bwd", dy, w_down_f, "nt", [ublk], du_epilogue,
                 extras=[(u, (tm_u, tn_u), lambda i, j: (i, j))])[0]
    def pair_send(tag, part):
        return _ici_start("grad_pair_start_" + tag, [part], [jax.ShapeDtypeStruct(part.shape, BF16)], _share_plan,
                          per_source=1)

    def pair_recv(tag, flight, after):
        send, recv, srcs, lands, _ = flight
        return _ici_wait("grad_pair_wait_" + tag, send, recv, srcs, lands, _share_plan, after)[1][0]

    def scatter_start(tag, sums):
        return _ici_start("grad_scatter_start_" + tag, sums,
                          [jax.ShapeDtypeStruct((3,) + p.shape[1:], BF16) for p in sums], _scatter_plan)

    def scatter_finish(tag, flight, after):
        send, recv, srcs, lands, _ = flight
        sums, received = _ici_wait("grad_scatter_wait_" + tag, send, recv, srcs, lands, _scatter_plan, after)
        return [_chip_add("chip_add_%s_%d" % (tag, k), chip_arr, p, r) for k, (p, r) in enumerate(zip(sums, received))]

    tm_g = _fit(MM_TM, D // 2)
    pair_down = pair_send("down", _grad_half("grad_w_down_a", core_arr, a, dy, N_CHIPS, 1, tm_g, True))
    pair_up = pair_send("up", _grad_half("grad_w_up_a", core_arr, h2, du, 1, N_CHIPS, tm_g, True, after=pair_down[4]))
    sum_down = _grad_half("grad_w_down_b", core_arr, a, dy, N_CHIPS, 1, tm_g, False,
                          recv=pair_recv("down", pair_down, pair_up[4]))
    sum_up = _grad_half("grad_w_up_b", core_arr, h2, du, 1, N_CHIPS, tm_g, False, recv=pair_recv("up", pair_up, sum_down))
    flight_mlp = scatter_start("mlp", [sum_up, sum_down])
    dh2 = _mm_plain("mlp_up_bwd", du, w_up_f, "nt", BF16, after=flight_mlp[4])
    dx1, dmix, acc_mid = _pre_mlp_and_post_mix_bwd(dh2, x1, dout, mix, _tie(g_pre_mlp, flight_mlp[4]), sc_m,
                                                   g_post_mix, gt_a)

    dmixcat = _mm_plain("out_proj_bwd", dmix, w_out_f, "nt", F32)

    fdq, fdk, fdv, dcum_row, dcum_q = _fox_bwd(proj, fox_o, dmixcat, fox_lse, cum_row, n_fox)
    dcum_k = jnp.pad(dcum_row.reshape(n_fox, S), ((0, LANES - n_fox), (0, 0)))
    dfg, db_forget = _fox_gate_bwd(dcum_k, dcum_q, fg, b_pad)

    group_w = (n_swa // n_kv) * HEAD_DIM
    sdq, sdk, sdv, dsink = _swa_bwd(rq, proj, v_first, sinks, swa_o, dmixcat, fox_w // group_w, swa_lse, n_swa, n_kv)
    drq = jnp.concatenate([sdq, jnp.transpose(sdk, (1, 0, 2)).reshape(S, kv_w).astype(BF16)], axis=1)
    d_sq_sk = _rope("rope_bwd", drq, 0, n_swa + n_kv, cos, -sin_signed)
    dsv = jnp.transpose(sdv, (1, 0, 2)).reshape(S, kv_w).astype(BF16)
    dproj = jnp.concatenate([fdq, fdk, fdv, d_sq_sk, dsv], axis=1)

    pieces = []
    for s in range(N_CHIPS):
        lo, hi = s * in_rows, (s + 1) * in_rows
        for src, first, last, shift in [(dproj, 0, gate_lo, 0), (dfg, gate_lo, gate_lo + n_fox, gate_lo),
                                        (dproj, gate_lo + n_fox, in_w, n_fox)]:
            if max(lo, first) < min(hi, last):
                pieces.append(src[:, max(lo, first) - shift:min(hi, last) - shift])
        pieces.append(jnp.zeros((S, in_rows_pad - in_rows), BF16))
    dproj_slab = jnp.concatenate(pieces, axis=1)

    tm_in, tm_out = in_rows_pad // 2, D // (2 * N_CHIPS)
    pair_in = pair_send("in", _grad_half("grad_w_in_a", core_arr, dproj_slab, h, N_CHIPS, 1, tm_in, True))
    pair_out = pair_send("out", _grad_half("grad_w_out_a", core_arr, mixcat, dmix, N_CHIPS, 1, tm_out, True,
                                           after=pair_in[4]))
    sum_in = _grad_half("grad_w_in_b", core_arr, dproj_slab, h, N_CHIPS, 1, tm_in, False,
                        recv=pair_recv("in", pair_in, pair_out[4]))
    dh = _mm_plain("in_proj_bwd", dproj_slab, w_slab_t, "nn", BF16, tk=slab_w // 2)
    grad_x, acc_pre = _pre_mix_bwd(dh, x2, dx1, g_pre_mix, sc_a)

    zero_row = jnp.zeros((1, D), F32)
    tail = jnp.concatenate([db_forget[0:1, :n_fox], dsink[:, 0, :n_swa // n_kv].reshape(1, n_swa),
                            loss_part[0:1, 0:1], jnp.zeros((1, D - n_fox - n_swa - 1), F32)], axis=1)
    partial = jnp.concatenate([
        acc_pre[0:1], acc_pre[1:2], acc_mid[3:4], acc_mid[0:1], acc_mid[1:2], acc_mlp_post[0:1],
        acc_pre[2:3], acc_mid[4:5], acc_mid[2:3], acc_mlp_post[1:2], tail] + [zero_row] * 5, axis=0)
    gathered_small, token = _allgather8("gather_small_grads", partial)

    sum_out = _grad_half("grad_w_out_b", core_arr, mixcat, dmix, N_CHIPS, 1, tm_out, False,
                         recv=pair_recv("out", pair_out, token))
    flight_mix = scatter_start("mix", [sum_in, sum_out])
    halves_mlp = scatter_finish("mlp", flight_mlp, flight_mix[4])
    share_mlp = _ici_start("grad_share_start_mlp", halves_mlp,
                           [jax.ShapeDtypeStruct(hv.shape, F32) for hv in halves_mlp], _share_plan, per_source=1)

    def pack(bm, gpm, gqm, gpl, gql, bf, sk):
        last = jnp.concatenate([bf, sk, jnp.zeros((1, D - n_fox - n_swa), F32)], axis=1)
        return jnp.concatenate([bm.reshape(N_MOD, D), gpm, gqm, gpl, gql, last, jnp.zeros((5, D), F32)], axis=0)

    def unpack(p):
        return {"b_mod": p[0:N_MOD].reshape(1, N_MOD * D), "g_pre_mix": p[6:7], "g_post_mix": p[7:8],
                "g_pre_mlp": p[8:9], "g_post_mlp": p[9:10], "b_forget": p[10:11, :n_fox],
                "swa_sinks": p[10:11, n_fox:n_fox + n_swa]}

    small_out = _small_update(
        gathered_small, _tie(pack(b_mod, g_pre_mix, g_post_mix, g_pre_mlp, g_post_mlp, b_forget, swa_sinks), share_mlp[4]),
        pack(m_b_mod, m_g_pre_mix, m_g_post_mix, m_g_pre_mlp, m_g_post_mlp, m_b_forget, m_swa_sinks),
        pack(v_b_mod, v_g_pre_mix, v_g_post_mix, v_g_pre_mlp, v_g_post_mlp, v_b_forget, v_swa_sinks))
    g_small, d_small, m_small, v_small = [unpack(p) for p in small_out]
    loss = small_out[0][N_MOD + 4, n_fox + n_swa]

    dmod_all = gathered_small.reshape(N_DEV, 16, D)[:, :N_MOD].reshape(N_DEV, N_MOD * D)
    dmod_shard = _tie(lax.dynamic_slice_in_dim(dmod_all, chip * mod_cols, mod_cols, axis=1), share_mlp[4])
    g_w_mod, d_w_mod, nm_w_mod, nv_w_mod = _mod_update(c_all.T, dmod_shard, w_mod[0], m_w_mod[0], v_w_mod[0])
    send, recv, halves_mlp, lands, _ = share_mlp
    halves_mlp, others_mlp = _ici_wait("grad_share_wait_mlp", send, recv, halves_mlp, lands, _share_plan,
                                       d_w_mod[:8, :LANES] + small_out[1][:8, :LANES])

    grads = dict(g_small, w_mod=g_w_mod[None])
    deltas = dict(d_small, w_mod=d_w_mod[None])
    new_m = dict(m_small, w_mod=nm_w_mod[None])
    new_v = dict(v_small, w_mod=nv_w_mod[None])
    weights = {"w_in": (w_in, m_w_in, v_w_in), "w_out": (w_out, m_w_out, v_w_out), "w_up": (w_up, m_w_up, v_w_up),
               "w_down": (w_down, m_w_down, v_w_down)}

    def big_update(n, own, other):
        transposed = n == "w_in"
        w, m, v = in_state if transposed else [a[0] for a in weights[n]]
        outs = _adam_halves("adam_" + n, core_arr, w, own, other, m, v)
        if transposed:
            outs = [o[:in_rows].T for o in outs]
        grads[n], deltas[n], new_m[n], new_v[n] = [o[None] for o in outs]

    big_update("w_up", halves_mlp[0], others_mlp[0])
    big_update("w_down", halves_mlp[1], others_mlp[1])
    ran = deltas["w_down"][0, :8, :LANES] + deltas["w_up"][0, :8, :LANES] + d_w_mod[:8, :LANES]
    halves_mix = scatter_finish("mix", flight_mix, ran)
    others_mix = _pair_share("grad_pair_share_mix", halves_mix)
    big_update("w_in", halves_mix[0], others_mix[0])
    big_update("w_out", halves_mix[1], others_mix[1])

    order = ["w_mod", "b_mod", "g_pre_mix", "g_post_mix", "w_in", "b_forget", "swa_sinks", "w_out", "g_pre_mlp",
             "g_post_mlp", "w_up", "w_down"]
    return (loss, grad_x[None], *[grads[n] for n in order], *[deltas[n] for n in order],
            *[new_m[n] for n in order], *[new_v[n] for n in order])
```

```python
import jax
import jax.numpy as jnp
from jax import lax
from jax.experimental import pallas as pl
from jax.experimental.pallas import tpu as pltpu

F32 = jnp.float32
BF16 = jnp.bfloat16
MESH = pl.DeviceIdType.MESH

HEAD_DIM = 128
SWA_BLOCK = 128
ROPE_THETA = 10000.0
NORM_EPS = 1e-6
NEG = -1e30
N_MOD = 6
ADAM_LR = 0.001
ADAM_B1 = 0.9
ADAM_B2 = 0.999
ADAM_EPS = 1e-08
ADAM_WD = 0.01
ADAM_STEP = 10
N_CHIPS = 4
N_DEV = 8
LANES = 128
VMEM_CAP = 60 * 1024 * 1024

_NN = (((1,), (0,)), ((), ()))
_NT = (((1,), (1,)), ((), ()))
_TN = (((0,), (0,)), ((), ()))


def _vmem(nbytes):
    return int(min(VMEM_CAP, nbytes * 5 // 4 + (4 << 20)))


def _nbytes(shape, dtype):
    n = 1
    for s in shape:
        n *= s
    return n * jnp.dtype(dtype).itemsize


def _fit(t, n):
    t = min(t, n)
    assert n % t == 0, (t, n)
    return t


MM_TM, MM_TN, MM_TK = 512, 1024, 2048


def _matmul(name, a, b, mode, out_defs, epilogue, extras=(), tm=MM_TM, tn=MM_TN, tk=MM_TK, revisits=False,
            row_sel=None):
    if mode == "nn":
        (M, K), (K2, N) = a.shape, b.shape
    elif mode == "nt":
        (M, K), (N, K2) = a.shape, b.shape
    else:
        (K, M), (K2, N) = a.shape, b.shape
    assert K == K2, (a.shape, b.shape, mode)
    tm, tn, tk = _fit(tm, M), _fit(tn, N), _fit(tk, K)
    nk = K // tk
    dims = {"nn": _NN, "nt": _NT, "tn": _TN}[mode]
    if row_sel is None:
        grid_m, a_row = M // tm, lambda i, *sel: i
    else:
        grid_m, a_row = row_sel[2], lambda i, *sel: row_sel[1](i, sel[0])
    a_spec = (pl.BlockSpec((tk, tm), lambda i, j, k, *sel: (k, a_row(i, *sel))) if mode == "tn"
              else pl.BlockSpec((tm, tk), lambda i, j, k, *sel: (a_row(i, *sel), k)))
    b_spec = (pl.BlockSpec((tn, tk), lambda i, j, k, *sel: (j, k)) if mode == "nt"
              else pl.BlockSpec((tk, tn), lambda i, j, k, *sel: (k, j)))
    n_ex, n_out = len(extras), len(out_defs)

    def body(*refs):
        if row_sel is not None:
            refs = refs[1:]
        a_ref, b_ref = refs[0], refs[1]
        ex = refs[2:2 + n_ex]
        outs = refs[2 + n_ex:2 + n_ex + n_out]
        prod = lax.dot_general(a_ref[...], b_ref[...], dims, preferred_element_type=F32)
        if nk == 1:
            epilogue(prod, ex, outs)
        else:
            acc_ref = refs[-1]
            k = pl.program_id(2)

            @pl.when(k == 0)
            def _():
                acc_ref[...] = prod

            @pl.when(k > 0)
            def _():
                acc_ref[...] += prod

            @pl.when(k == nk - 1)
            def _():
                epilogue(acc_ref[...], ex, outs)

    def wrap(f):
        return lambda i, j, k, *sel: f(i, j)

    in_specs = [a_spec, b_spec] + [pl.BlockSpec(blk, wrap(f)) for _, blk, f in extras]
    out_specs = [pl.BlockSpec(blk, wrap(f)) for _, _, blk, f in out_defs]
    out_shape = [jax.ShapeDtypeStruct(s, d) for s, d, _, _ in out_defs]
    need = 2 * (tm * tk + tk * tn) * a.dtype.itemsize + 3 * tm * tn * 4
    need += sum(2 * _nbytes(blk, arr.dtype) for arr, blk, _ in extras)
    need += sum(2 * _nbytes(blk, d) for _, d, blk, _ in out_defs)
    grid = (grid_m, N // tn, nk)
    scratch = [pltpu.VMEM((tm, tn), F32)] if nk > 1 else []
    params = pltpu.CompilerParams(
        dimension_semantics=("parallel", "arbitrary" if revisits else "parallel", "arbitrary"),
        vmem_limit_bytes=_vmem(need))
    operands = (a, b, *[arr for arr, _, _ in extras])
    if row_sel is None:
        return pl.pallas_call(body, name=name, grid=grid, in_specs=in_specs, out_specs=out_specs, out_shape=out_shape,
                              scratch_shapes=scratch, compiler_params=params)(*operands)
    grid_spec = pltpu.PrefetchScalarGridSpec(num_scalar_prefetch=1, grid=grid, in_specs=in_specs, out_specs=out_specs,
                                             scratch_shapes=scratch)
    return pl.pallas_call(body, name=name, grid_spec=grid_spec, out_shape=out_shape,
                          compiler_params=params)(row_sel[0], *operands)


def _grad_half(name, core, a, b, row_slabs, col_slabs, tm, other, recv=None, after=None):
    (_, M), (_, N) = a.shape, b.shape
    H = M // (2 * row_slabs)
    nh = H // tm
    tn = _fit(MM_TN, N // col_slabs)
    per = N // col_slabs // tn

    def a_block(i, core_ref):
        half = (1 - core_ref[0]) if other else core_ref[0]
        return (i // nh) * (2 * nh) + half * nh + i % nh

    def out_index(i, j):
        return (j // per, i, j % per) if col_slabs > 1 else (i // nh, i % nh, j)

    slabs = max(row_slabs, col_slabs)
    out_def = ((slabs, H, N // col_slabs), BF16, (1, tm, tn), out_index)

    def epilogue(acc, ex, outs):
        outs[0][0] = (acc if recv is None else acc + ex[0][0].astype(F32)).astype(BF16)

    extras = ([] if recv is None else [(recv, (1, tm, tn), out_index)]) + ([] if after is None else [_behind(after)])
    return _matmul(name, a, b, "tn", [out_def], epilogue, extras=extras, tm=tm, tn=tn,
                   row_sel=(core, a_block, row_slabs * nh))[0]


def _behind(token):
    return (token, (8, LANES), lambda i, j: (0, 0))


def _mm_plain(name, a, b, mode, out_dtype, after=None, **tiles):
    if mode == "nn":
        M, N = a.shape[0], b.shape[1]
    elif mode == "nt":
        M, N = a.shape[0], b.shape[0]
    else:
        M, N = a.shape[1], b.shape[1]
    tm, tn = _fit(tiles.get("tm", MM_TM), M), _fit(tiles.get("tn", MM_TN), N)

    def epi(acc, ex, outs):
        outs[0][...] = acc.astype(out_dtype)

    return _matmul(name, a, b, mode, [((M, N), out_dtype, (tm, tn), lambda i, j: (i, j))], epi,
                   extras=[] if after is None else [_behind(after)], **tiles)[0]


def _rstd(v):
    return lax.rsqrt(jnp.mean(v * v, axis=-1, keepdims=True) + NORM_EPS)


def _row_call(name, body, row_ins, vec_ins, row_outs, acc_outs, S, D, tr):
    tr = _fit(tr, S)
    row_spec = pl.BlockSpec((tr, D), lambda r: (r, 0))
    vec_spec = pl.BlockSpec((1, D), lambda r: (0, 0))
    in_specs = [row_spec] * len(row_ins) + [vec_spec] * len(vec_ins)
    out_specs = [row_spec] * len(row_outs) + [pl.BlockSpec(shp, lambda r: (0, 0)) for shp in acc_outs]
    out_shape = [jax.ShapeDtypeStruct((S, D), d) for d in row_outs] + [jax.ShapeDtypeStruct(shp, F32) for shp in acc_outs]
    need = sum(2 * tr * D * a.dtype.itemsize for a in row_ins) + sum(2 * tr * D * jnp.dtype(d).itemsize for d in row_outs)
    need += 8 * tr * D * 4
    return pl.pallas_call(
        body, name=name, grid=(S // tr,), in_specs=in_specs, out_specs=out_specs, out_shape=out_shape,
        compiler_params=pltpu.CompilerParams(dimension_semantics=("arbitrary",), vmem_limit_bytes=_vmem(need)),
    )(*row_ins, *vec_ins)


def _acc_rows(ref, rows):
    @pl.when(pl.program_id(0) == 0)
    def _():
        ref[...] = jnp.zeros_like(ref)
    for n, r in enumerate(rows):
        ref[n:n + 1, :] += r


def _pre_norm(x, g, sc, sh):
    S, D = x.shape

    def body(x_ref, g_ref, sc_ref, sh_ref, h_ref):
        xv = x_ref[...]
        xn = xv * _rstd(xv)
        h_ref[...] = (xn * g_ref[...] * (1.0 + sc_ref[...]) + sh_ref[...]).astype(BF16)

    return _row_call("pre_norm_mix", body, [x], [g, sc, sh], [BF16], [], S, D, 256)[0]


def _post_mix(x, mix, g_post, gt, g_pre, sc, sh):
    S, D = x.shape

    def body(x_ref, mix_ref, gp_ref, gt_ref, g2_ref, sc_ref, sh_ref, x1_ref, h2_ref):
        mv = mix_ref[...].astype(F32)
        x1 = x_ref[...] + gt_ref[...] * (mv * _rstd(mv) * gp_ref[...])
        x1_ref[...] = x1
        h2_ref[...] = (x1 * _rstd(x1) * g2_ref[...] * (1.0 + sc_ref[...]) + sh_ref[...]).astype(BF16)

    return _row_call("post_mix_pre_mlp", body, [x, mix], [g_post, gt, g_pre, sc, sh], [F32, BF16], [], S, D, 256)


def _loss_and_post_mlp_bwd(x1, y, target, g_post, gt):
    S, D = x1.shape

    def body(x1_ref, y_ref, t_ref, g_ref, gt_ref, dy_ref, dout_ref, loss_ref, acc_ref):
        yv = y_ref[...].astype(F32)
        r = _rstd(yv)
        yh = yv * r
        n = yh * g_ref[...]
        diff = x1_ref[...] + gt_ref[...] * n - t_ref[...]
        dout = diff * (1.0 / D)
        dout_ref[...] = dout
        dn = dout * gt_ref[...]
        dyh = dn * g_ref[...]
        dy_ref[...] = (r * (dyh - yh * jnp.mean(dyh * yh, axis=-1, keepdims=True))).astype(BF16)
        _acc_rows(acc_ref, [jnp.sum(dout * n, axis=0, keepdims=True), jnp.sum(dn * yh, axis=0, keepdims=True)])

        @pl.when(pl.program_id(0) == 0)
        def _():
            loss_ref[...] = jnp.zeros_like(loss_ref)
        loss_ref[...] += jnp.full(loss_ref.shape, (0.5 / D) * jnp.sum(diff * diff), F32)

    return _row_call("loss_post_mlp_bwd", body, [x1, y, target], [g_post, gt], [BF16, F32],
                     [(8, LANES), (8, D)], S, D, 128)


def _pre_mlp_and_post_mix_bwd(dh2, x1, dout, mix, g_pre, sc, g_post, gt):
    S, D = x1.shape

    def body(dh_ref, x1_ref, dout_ref, mix_ref, g_ref, sc_ref, gp_ref, gt_ref, dx1_ref, dmix_ref, acc_ref):
        dh = dh_ref[...].astype(F32)
        x1v = x1_ref[...]
        r3 = _rstd(x1v)
        xn = x1v * r3
        dxn = dh * (1.0 + sc_ref[...]) * g_ref[...]
        dx1 = dout_ref[...] + r3 * (dxn - xn * jnp.mean(dxn * xn, axis=-1, keepdims=True))
        dx1_ref[...] = dx1
        mv = mix_ref[...].astype(F32)
        r2 = _rstd(mv)
        mh = mv * r2
        dn = dx1 * gt_ref[...]
        dmh = dn * gp_ref[...]
        dmix_ref[...] = (r2 * (dmh - mh * jnp.mean(dmh * mh, axis=-1, keepdims=True))).astype(BF16)
        _acc_rows(acc_ref, [
            jnp.sum(dh, axis=0, keepdims=True),
            jnp.sum(dh * xn * g_ref[...], axis=0, keepdims=True),
            jnp.sum(dh * (1.0 + sc_ref[...]) * xn, axis=0, keepdims=True),
            jnp.sum(dx1 * mh * gp_ref[...], axis=0, keepdims=True),
            jnp.sum(dn * mh, axis=0, keepdims=True)])

    return _row_call("pre_mlp_post_mix_bwd", body, [dh2, x1, dout, mix], [g_pre, sc, g_post, gt], [F32, BF16],
                     [(8, D)], S, D, 128)


def _pre_mix_bwd(dh, x, dx1, g_pre, sc):
    S, D = x.shape

    def body(dh_ref, x_ref, dx1_ref, g_ref, sc_ref, gx_ref, acc_ref):
        dhv = dh_ref[...].astype(F32)
        xv = x_ref[...]
        r = _rstd(xv)
        xn = xv * r
        dxn = dhv * (1.0 + sc_ref[...]) * g_ref[...]
        gx_ref[...] = dx1_ref[...] + r * (dxn - xn * jnp.mean(dxn * xn, axis=-1, keepdims=True))
        _acc_rows(acc_ref, [
            jnp.sum(dhv, axis=0, keepdims=True),
            jnp.sum(dhv * xn * g_ref[...], axis=0, keepdims=True),
            jnp.sum(dhv * (1.0 + sc_ref[...]) * xn, axis=0, keepdims=True)])

    return _row_call("pre_mix_bwd", body, [dh, x, dx1], [g_pre, sc], [F32], [(8, D)], S, D, 128)


CUM_BLOCK = 256


def _tri(n, upper):
    r = lax.broadcasted_iota(jnp.int32, (n, n), 0)
    c = lax.broadcasted_iota(jnp.int32, (n, n), 1)
    return ((c >= r) if upper else (c <= r)).astype(F32)


def _fox_gate_fwd(fg, b_pad):
    S = fg.shape[0]
    cb = _fit(CUM_BLOCK, S)

    def body(fg_ref, b_ref, cumt_ref, cum_ref):
        low = _tri(cb, False)
        carry = jnp.zeros((1, LANES), F32)
        for n in range(S // cb):
            z = fg_ref[n * cb:(n + 1) * cb, :] + b_ref[...]
            logf = jnp.minimum(z, 0.0) - jnp.log(1.0 + jnp.exp(-jnp.abs(z)))
            blk = jnp.dot(low, logf, precision=lax.Precision.HIGHEST, preferred_element_type=F32) + carry
            cum_ref[n * cb:(n + 1) * cb, :] = blk
            carry = blk[cb - 1:cb, :]
        cumt_ref[...] = cum_ref[...].T

    return pl.pallas_call(
        body, name="fox_gate_fwd", out_shape=jax.ShapeDtypeStruct((LANES, S), F32),
        scratch_shapes=[pltpu.VMEM((S, LANES), F32)],
        compiler_params=pltpu.CompilerParams(vmem_limit_bytes=_vmem(6 * S * LANES * 4)),
    )(fg, b_pad)


def _fox_gate_bwd(dcum_k, dcum_q, fg, b_pad):
    S = fg.shape[0]
    n_fox = dcum_q.shape[0]
    cb = _fit(CUM_BLOCK, S)

    def body(dk_ref, dq_ref, fg_ref, b_ref, dfg_ref, db_ref, dc_ref):
        lane = lax.broadcasted_iota(jnp.int32, (S, LANES), 1)
        dc = dk_ref[...].T
        for h in range(n_fox):
            dc = dc + jnp.where(lane == h, dq_ref[h], 0.0)
        dc_ref[...] = dc
        up = _tri(cb, True)
        carry = jnp.zeros((1, LANES), F32)
        db = jnp.zeros((1, LANES), F32)
        for n in reversed(range(S // cb)):
            blk = jnp.dot(up, dc_ref[n * cb:(n + 1) * cb, :], precision=lax.Precision.HIGHEST,
                          preferred_element_type=F32) + carry
            carry = blk[0:1, :]
            z = fg_ref[n * cb:(n + 1) * cb, :] + b_ref[...]
            dfg = blk * (1.0 / (1.0 + jnp.exp(z)))
            dfg_ref[n * cb:(n + 1) * cb, :] = dfg.astype(BF16)
            db = db + jnp.sum(dfg, axis=0, keepdims=True)
        db_ref[...] = jnp.broadcast_to(db, db_ref.shape)

    return pl.pallas_call(
        body, name="fox_gate_bwd",
        out_shape=[jax.ShapeDtypeStruct((S, LANES), BF16), jax.ShapeDtypeStruct((8, LANES), F32)],
        scratch_shapes=[pltpu.VMEM((S, LANES), F32)],
        compiler_params=pltpu.CompilerParams(vmem_limit_bytes=_vmem((8 + 2 * n_fox) * S * LANES * 4)),
    )(dcum_k, dcum_q, fg, b_pad)


FOX_TILE = 512


LOG2E = 1.4426950408889634


def _fox_scores(q, k, ck2, masked, t):
    s = lax.dot_general(q, k, _NT, preferred_element_type=F32) * (HEAD_DIM ** -0.5 * LOG2E) - ck2
    if masked:
        row = lax.broadcasted_iota(jnp.int32, (t, t), 0)
        col = lax.broadcasted_iota(jnp.int32, (t, t), 1)
        s = jnp.where(col <= row, s, NEG)
    return s


def _fox_fwd(proj, cum_row, n_fox):
    S = proj.shape[0]
    t = _fit(FOX_TILE, S)
    nq = S // t

    def body(q_ref, k_ref, v_ref, ck_ref, o_ref, lse_ref):
        def q_block(qi, _):
            q0 = pl.multiple_of(qi * t, t)
            q = q_ref[pl.ds(q0, t), :]

            def kv_block(j, carry, masked):
                m, l, acc = carry
                k0 = pl.multiple_of(j * t, t)
                s = _fox_scores(q, k_ref[pl.ds(k0, t), :], ck_ref[0, :, pl.ds(k0, t)] * LOG2E, masked, t)
                m_new = jnp.maximum(m, jnp.max(s, axis=-1, keepdims=True))
                alpha = jnp.exp2(m - m_new)
                p = jnp.exp2(s - m_new)
                l = alpha * l + jnp.sum(p, axis=-1, keepdims=True)
                acc = alpha * acc + jnp.dot(p.astype(BF16), v_ref[pl.ds(k0, t), :], preferred_element_type=F32)
                return m_new, l, acc

            init = (jnp.full((t, 1), NEG, F32), jnp.zeros((t, 1), F32), jnp.zeros((t, HEAD_DIM), F32))
            carry = lax.fori_loop(0, qi, lambda j, cr: kv_block(j, cr, False), init)
            m, l, acc = kv_block(qi, carry, True)
            o_ref[pl.ds(q0, t), :] = acc / l
            lse_ref[0, pl.ds(q0, t), :] = jnp.broadcast_to(m + jnp.log(l) * LOG2E, (t, LANES))
            return 0

        lax.fori_loop(0, nq, q_block, 0)

    col = lambda off: pl.BlockSpec((S, HEAD_DIM), lambda h: (0, off + h))
    per_head = pl.BlockSpec((1, S, LANES), lambda h: (h, 0, 0))
    return pl.pallas_call(
        body, name="fox_fwd", grid=(n_fox,),
        in_specs=[col(0), col(n_fox), col(2 * n_fox), pl.BlockSpec((1, 1, S), lambda h: (h, 0, 0))],
        out_specs=[pl.BlockSpec((S, HEAD_DIM), lambda h: (0, h)), per_head],
        out_shape=[jax.ShapeDtypeStruct((S, n_fox * HEAD_DIM), F32), jax.ShapeDtypeStruct((n_fox, S, LANES), F32)],
        compiler_params=pltpu.CompilerParams(dimension_semantics=("parallel",),
                                             vmem_limit_bytes=_vmem(16 * S * HEAD_DIM * 4 + 12 * t * t * 4)),
    )(proj, proj, proj, cum_row)


def _fox_bwd(proj, o, do, lse_b, cum_row, n_fox):
    S = proj.shape[0]
    t = _fit(FOX_TILE, S)
    nq = S // t
    scale = HEAD_DIM ** -0.5

    def body(q_ref, k_ref, v_ref, o_ref, do_ref, lse_ref, ck_ref, dq_ref, dk_ref, dv_ref, dc_ref, dcq_ref,
             dq_acc, delta_ref):
        dq_acc[...] = jnp.zeros_like(dq_acc)
        dcq_ref[...] = jnp.zeros_like(dcq_ref)

        def delta_block(qi, _):
            q0 = pl.multiple_of(qi * t, t)
            d = jnp.sum(do_ref[pl.ds(q0, t), :] * o_ref[pl.ds(q0, t), :], axis=-1, keepdims=True)
            delta_ref[pl.ds(q0, t), :] = jnp.broadcast_to(d, (t, LANES))
            return 0

        lax.fori_loop(0, nq, delta_block, 0)

        def kv_block(j, _):
            k0 = pl.multiple_of(j * t, t)
            k = k_ref[pl.ds(k0, t), :]
            v = v_ref[pl.ds(k0, t), :]
            ck2 = ck_ref[0, :, pl.ds(k0, t)] * LOG2E

            def q_block(qi, carry, masked):
                dk, dv, dc = carry
                q0 = pl.multiple_of(qi * t, t)
                q = q_ref[pl.ds(q0, t), :]
                dov = do_ref[pl.ds(q0, t), :].astype(BF16)
                p = jnp.exp2(_fox_scores(q, k, ck2, masked, t) - lse_ref[0, pl.ds(q0, t), :][:, :1])
                dp = lax.dot_general(dov, v, _NT, preferred_element_type=F32)
                ds = p * (dp - delta_ref[pl.ds(q0, t), :][:, :1])
                dsb = ds.astype(BF16)
                dv = dv + lax.dot_general(p.astype(BF16), dov, _TN, preferred_element_type=F32)
                dk = dk + lax.dot_general(dsb, q, _TN, preferred_element_type=F32)
                dq_acc[pl.ds(q0, t), :] += jnp.dot(dsb, k, preferred_element_type=F32)
                dc = dc - jnp.sum(ds, axis=0, keepdims=True)
                dcq_ref[0, pl.ds(q0, t), :] += jnp.broadcast_to(jnp.sum(ds, axis=1, keepdims=True), (t, LANES))
                return dk, dv, dc

            init = (jnp.zeros((t, HEAD_DIM), F32), jnp.zeros((t, HEAD_DIM), F32), jnp.zeros((1, t), F32))
            carry = q_block(j, init, True)
            dk, dv, dc = lax.fori_loop(j + 1, nq, lambda qi, cr: q_block(qi, cr, False), carry)
            dk_ref[pl.ds(k0, t), :] = (dk * scale).astype(BF16)
            dv_ref[pl.ds(k0, t), :] = dv.astype(BF16)
            dc_ref[0, :, pl.ds(k0, t)] = dc
            return 0

        lax.fori_loop(0, nq, kv_block, 0)
        dq_ref[...] = (dq_acc[...] * scale).astype(BF16)

    col = lambda off: pl.BlockSpec((S, HEAD_DIM), lambda h: (0, off + h))
    per_head = pl.BlockSpec((1, S, LANES), lambda h: (h, 0, 0))
    row = pl.BlockSpec((1, 1, S), lambda h: (h, 0, 0))
    grad = jax.ShapeDtypeStruct((S, n_fox * HEAD_DIM), BF16)
    return pl.pallas_call(
        body, name="fox_bwd", grid=(n_fox,),
        in_specs=[col(0), col(n_fox), col(2 * n_fox), col(0), col(0), per_head, row],
        out_specs=[col(0), col(0), col(0), row, per_head],
        out_shape=[grad, grad, grad, jax.ShapeDtypeStruct((n_fox, 1, S), F32), jax.ShapeDtypeStruct((n_fox, S, LANES), F32)],
        scratch_shapes=[pltpu.VMEM((S, HEAD_DIM), F32), pltpu.VMEM((S, LANES), F32)],
        compiler_params=pltpu.CompilerParams(dimension_semantics=("parallel",),
                                             vmem_limit_bytes=_vmem(24 * S * HEAD_DIM * 4 + 16 * t * t * 4)),
    )(proj, proj, proj, o, do, lse_b, cum_row)


def _rope_tables(S):
    half = HEAD_DIM // 2
    inv_freq = 1.0 / (ROPE_THETA ** (jnp.arange(half, dtype=F32) * (2.0 / HEAD_DIM)))
    ang = jnp.arange(S).astype(F32)[:, None] * inv_freq[None, :]
    cos, sin = jnp.cos(ang), jnp.sin(ang)
    return jnp.concatenate([cos, cos], axis=-1), jnp.concatenate([-sin, sin], axis=-1)


def _rope(name, src, first_block, n_blocks, cos, sin_signed):
    S = src.shape[0]

    def body(x_ref, cos_ref, sin_ref, o_ref):
        xv = x_ref[...].astype(F32)
        o_ref[...] = (xv * cos_ref[...] + pltpu.roll(xv, HEAD_DIM // 2, 1) * sin_ref[...]).astype(BF16)

    table = pl.BlockSpec((S, HEAD_DIM), lambda n: (0, 0))
    return pl.pallas_call(
        body, name=name, grid=(n_blocks,),
        in_specs=[pl.BlockSpec((S, HEAD_DIM), lambda n: (0, first_block + n)), table, table],
        out_specs=pl.BlockSpec((S, HEAD_DIM), lambda n: (0, n)),
        out_shape=jax.ShapeDtypeStruct((S, n_blocks * HEAD_DIM), BF16),
        compiler_params=pltpu.CompilerParams(dimension_semantics=("parallel",),
                                             vmem_limit_bytes=_vmem(12 * S * HEAD_DIM * 4)),
    )(src, cos, sin_signed)


def _swa_tile(q_ref, kp_ref, kc_ref, n, group, scale):
    B = SWA_BLOCK
    qs = jnp.concatenate([q_ref[:, g * HEAD_DIM:(g + 1) * HEAD_DIM] for g in range(group)], axis=0)
    kcat = jnp.concatenate([kp_ref[...], kc_ref[...]], axis=0)
    s = lax.dot_general(qs, kcat, _NT, preferred_element_type=F32) * scale
    qi = lax.broadcasted_iota(jnp.int32, (group * B, 2 * B), 0) % B
    kj = lax.broadcasted_iota(jnp.int32, (group * B, 2 * B), 1)
    diff = qi + B - kj
    mask = (diff >= 0) & (diff < B) & ((n * B + kj - B) >= 0)
    return qs, kcat, jnp.where(mask, s, NEG)


def _swa_sink_col(sink_ref, kv, group):
    head = lax.broadcasted_iota(jnp.int32, (group * SWA_BLOCK, 1), 0) // SWA_BLOCK
    col = jnp.zeros((group * SWA_BLOCK, 1), F32)
    for g in range(group):
        col = jnp.where(head == g, sink_ref[kv * group + g], col)
    return col


def _swa_specs(n_kv, group, q_first, k_first, v_first):
    B = SWA_BLOCK
    prev = lambda n: jnp.maximum(n - 1, 0)
    return [
        pl.BlockSpec((B, group * HEAD_DIM), lambda kv, n: (n, q_first + kv)),
        pl.BlockSpec((B, HEAD_DIM), lambda kv, n: (prev(n), k_first + kv)),
        pl.BlockSpec((B, HEAD_DIM), lambda kv, n: (n, k_first + kv)),
        pl.BlockSpec((B, HEAD_DIM), lambda kv, n: (prev(n), v_first + kv)),
        pl.BlockSpec((B, HEAD_DIM), lambda kv, n: (n, v_first + kv)),
    ]


def _swa_fwd(rq, proj, v_first, sinks, n_q, n_kv):
    S = rq.shape[0]
    B = SWA_BLOCK
    group = n_q // n_kv
    scale = HEAD_DIM ** -0.5

    def body(q_ref, kp_ref, kc_ref, vp_ref, vc_ref, sink_ref, o_ref, lse_ref):
        kv, n = pl.program_id(0), pl.program_id(1)
        _, _, s = _swa_tile(q_ref, kp_ref, kc_ref, n, group, scale)
        sink = _swa_sink_col(sink_ref, kv, group)
        m = jnp.maximum(jnp.max(s, axis=-1, keepdims=True), sink)
        p = jnp.exp(s - m)
        denom = jnp.sum(p, axis=-1, keepdims=True) + jnp.exp(sink - m)
        vcat = jnp.concatenate([vp_ref[...], vc_ref[...]], axis=0)
        o = jnp.dot((p / denom).astype(BF16), vcat, preferred_element_type=F32)
        lse = m + jnp.log(denom)
        for g in range(group):
            o_ref[:, g * HEAD_DIM:(g + 1) * HEAD_DIM] = o[g * B:(g + 1) * B, :]
            lse_ref[0, :, g * LANES:(g + 1) * LANES] = jnp.broadcast_to(lse[g * B:(g + 1) * B, :], (B, LANES))

    specs = _swa_specs(n_kv, group, 0, n_q, v_first)
    q_blk = pl.BlockSpec((B, group * HEAD_DIM), lambda kv, n: (n, kv))
    return pl.pallas_call(
        body, name="swa_fwd", grid=(n_kv, S // B),
        in_specs=specs + [pl.BlockSpec(memory_space=pltpu.SMEM)],
        out_specs=[q_blk, pl.BlockSpec((1, B, group * LANES), lambda kv, n: (kv, n, 0))],
        out_shape=[jax.ShapeDtypeStruct((S, n_q * HEAD_DIM), F32), jax.ShapeDtypeStruct((n_kv, S, group * LANES), F32)],
        compiler_params=pltpu.CompilerParams(dimension_semantics=("parallel", "arbitrary")),
    )(rq, rq, rq, proj, proj, sinks)


def _swa_bwd(rq, proj, v_first, sinks, o, do, do_first, lse_b, n_q, n_kv):
    S = rq.shape[0]
    B = SWA_BLOCK
    group = n_q // n_kv
    scale = HEAD_DIM ** -0.5

    def body(q_ref, kp_ref, kc_ref, vp_ref, vc_ref, o_ref, do_ref, lse_ref, sink_ref,
             dq_ref, dk_ref, dv_ref, dsink_ref):
        kv, n = pl.program_id(0), pl.program_id(1)

        @pl.when(n == 0)
        def _():
            dk_ref[...] = jnp.zeros_like(dk_ref)
            dv_ref[...] = jnp.zeros_like(dv_ref)
            dsink_ref[...] = jnp.zeros_like(dsink_ref)

        qs, kcat, s = _swa_tile(q_ref, kp_ref, kc_ref, n, group, scale)
        sink = _swa_sink_col(sink_ref, kv, group)
        stack = lambda ref, w: jnp.concatenate([ref[:, g * w:(g + 1) * w] for g in range(group)], axis=0)
        lse = jnp.concatenate([lse_ref[0, :, g * LANES:g * LANES + 1] for g in range(group)], axis=0)
        do32 = stack(do_ref, HEAD_DIM)
        delta = jnp.sum(do32 * stack(o_ref, HEAD_DIM), axis=-1, keepdims=True)
        dov = do32.astype(BF16)
        p = jnp.exp(s - lse)
        vcat = jnp.concatenate([vp_ref[...], vc_ref[...]], axis=0)
        dp = lax.dot_general(dov, vcat, _NT, preferred_element_type=F32)
        ds = p * (dp - delta)
        dsb = ds.astype(BF16)
        dq = jnp.dot(dsb, kcat, preferred_element_type=F32) * scale
        for g in range(group):
            dq_ref[:, g * HEAD_DIM:(g + 1) * HEAD_DIM] = dq[g * B:(g + 1) * B, :].astype(BF16)
        dkcat = lax.dot_general(dsb, qs, _TN, preferred_element_type=F32) * scale
        dvcat = lax.dot_general(p.astype(BF16), dov, _TN, preferred_element_type=F32)
        prev0 = pl.multiple_of(jnp.maximum(n - 1, 0) * B, B)
        cur0 = pl.multiple_of(n * B, B)
        dk_ref[0, pl.ds(prev0, B), :] += dkcat[:B, :]
        dk_ref[0, pl.ds(cur0, B), :] += dkcat[B:, :]
        dv_ref[0, pl.ds(prev0, B), :] += dvcat[:B, :]
        dv_ref[0, pl.ds(cur0, B), :] += dvcat[B:, :]
        dsk = -jnp.exp(sink - lse) * delta
        lane = lax.broadcasted_iota(jnp.int32, (1, LANES), 1)
        row = jnp.zeros((1, LANES), F32)
        for g in range(group):
            row = row + jnp.where(lane == g, jnp.sum(dsk[g * B:(g + 1) * B, :]), 0.0)
        dsink_ref[0, 0:1, :] += row

    specs = _swa_specs(n_kv, group, 0, n_q, v_first)
    q_blk = pl.BlockSpec((B, group * HEAD_DIM), lambda kv, n: (n, kv))
    acc = pl.BlockSpec((1, S, HEAD_DIM), lambda kv, n: (kv, 0, 0))
    return pl.pallas_call(
        body, name="swa_bwd", grid=(n_kv, S // B),
        in_specs=specs + [q_blk, pl.BlockSpec((B, group * HEAD_DIM), lambda kv, n: (n, do_first + kv)),
                          pl.BlockSpec((1, B, group * LANES), lambda kv, n: (kv, n, 0)),
                          pl.BlockSpec(memory_space=pltpu.SMEM)],
        out_specs=[q_blk, acc, acc, pl.BlockSpec((1, 8, LANES), lambda kv, n: (kv, 0, 0))],
        out_shape=[jax.ShapeDtypeStruct((S, n_q * HEAD_DIM), BF16), jax.ShapeDtypeStruct((n_kv, S, HEAD_DIM), F32),
                   jax.ShapeDtypeStruct((n_kv, S, HEAD_DIM), F32), jax.ShapeDtypeStruct((n_kv, 8, LANES), F32)],
        compiler_params=pltpu.CompilerParams(dimension_semantics=("parallel", "arbitrary")),
    )(rq, rq, rq, proj, proj, o, do, lse_b, sinks)


def _adamw(w, g, m, v):
    m = ADAM_B1 * m + (1.0 - ADAM_B1) * g
    v = ADAM_B2 * v + (1.0 - ADAM_B2) * (g * g)
    m_hat = m / (1.0 - ADAM_B1 ** ADAM_STEP)
    v_hat = v / (1.0 - ADAM_B2 ** ADAM_STEP)
    delta = -ADAM_LR * (m_hat / (jnp.sqrt(v_hat) + ADAM_EPS) + ADAM_WD * w)
    return delta, m, v


def _mod_fwd(cond_in, w_mod, b_shard):
    R, D = cond_in.shape
    cols = w_mod.shape[1]
    tn = _fit(512, cols)

    def body(c_ref, w_ref, b_ref, o_ref):
        cv = c_ref[...]
        cond = (cv / (1.0 + jnp.exp(-cv))).astype(BF16)
        o_ref[...] = jnp.dot(cond, w_ref[...].astype(BF16), preferred_element_type=F32) + b_ref[...]

    return pl.pallas_call(
        body, name="mod_fwd", grid=(cols // tn,),
        in_specs=[pl.BlockSpec((R, D), lambda j: (0, 0)), pl.BlockSpec((D, tn), lambda j: (0, j)),
                  pl.BlockSpec((1, tn), lambda j: (0, j))],
        out_specs=pl.BlockSpec((R, tn), lambda j: (0, j)),
        out_shape=jax.ShapeDtypeStruct((R, cols), F32),
        compiler_params=pltpu.CompilerParams(dimension_semantics=("parallel",), vmem_limit_bytes=_vmem(3 * D * tn * 4)),
    )(cond_in, w_mod, b_shard)


def _mod_update(c_t, dmod, w, m, v):
    D, nb = c_t.shape
    cols = w.shape[1]
    tn = _fit(256, cols)

    def body(c_ref, d_ref, w_ref, m_ref, v_ref, g_ref, dl_ref, nm_ref, nv_ref):
        cv = c_ref[...]
        cond = cv / (1.0 + jnp.exp(-cv))
        g = jnp.zeros((D, tn), F32)
        for b in range(nb):
            g = g + cond[:, b:b + 1] * d_ref[b:b + 1, :]
        g_ref[...] = g
        dl_ref[...], nm_ref[...], nv_ref[...] = _adamw(w_ref[...], g, m_ref[...], v_ref[...])

    blk = pl.BlockSpec((D, tn), lambda j: (0, j))
    out = jax.ShapeDtypeStruct((D, cols), F32)
    return pl.pallas_call(
        body, name="mod_update", grid=(cols // tn,),
        in_specs=[pl.BlockSpec((D, nb), lambda j: (0, 0)), pl.BlockSpec((nb, tn), lambda j: (0, j)), blk, blk, blk],
        out_specs=[blk] * 4, out_shape=[out] * 4,
        compiler_params=pltpu.CompilerParams(dimension_semantics=("parallel",), vmem_limit_bytes=_vmem(18 * D * tn * 4)),
    )(c_t, dmod, w, m, v)


def _small_update(stacked, w, m, v):
    R, C = w.shape

    def body(s_ref, w_ref, m_ref, v_ref, g_ref, dl_ref, nm_ref, nv_ref):
        g = s_ref[0:R, :]
        for d in range(1, N_DEV):
            g = g + s_ref[d * R:(d + 1) * R, :]
        g_ref[...] = g
        dl_ref[...], nm_ref[...], nv_ref[...] = _adamw(w_ref[...], g, m_ref[...], v_ref[...])

    return pl.pallas_call(body, name="small_update", out_shape=[jax.ShapeDtypeStruct((R, C), F32)] * 4)(stacked, w, m, v)


def _place():
    return lax.axis_index("x"), lax.axis_index("y"), lax.axis_index("c")


def _allgather8(name, block):
    m_per, n = block.shape

    def body(x_ref, out_ref, token_ref, send_sems, recv_sems, local_sem):
        token_ref[...] = jnp.zeros_like(token_ref)
        x, y, c = _place()
        me, sibling = (x, y, c), (x, y, 1 - c)
        chips = [(1 - x, y), (x, 1 - y), (1 - x, 1 - y)]

        def rows(px, py, pc):
            return out_ref.at[pl.ds((4 * px + 2 * py + pc) * m_per, m_per), :]

        def copy(k, blk, to, src=None):
            return pltpu.make_async_remote_copy(
                src_ref=rows(*blk) if src is None else src, dst_ref=rows(*blk),
                send_sem=send_sems.at[k], recv_sem=recv_sems.at[k], device_id=to, device_id_type=MESH)

        mine = pltpu.make_async_copy(x_ref, rows(*me), local_sem)
        mine.start()
        first = [copy(0, me, sibling, src=x_ref)]
        first += [copy(1 + j, me, (*chip, c), src=x_ref) for j, chip in enumerate(chips)]
        for cp in first:
            cp.start()
        passed = [copy(4 + j, (*chip, c), sibling) for j, chip in enumerate(chips)]
        for j, chip in enumerate(chips):
            copy(1 + j, (*chip, c), me).wait_recv()
            passed[j].start()
        copy(0, sibling, me).wait_recv()
        for j, chip in enumerate(chips):
            copy(4 + j, (*chip, 1 - c), me).wait_recv()
        for cp in first + passed:
            cp.wait_send()
        mine.wait()

    vmem = pl.BlockSpec(memory_space=pltpu.VMEM)
    return pl.pallas_call(
        body, name=name,
        out_shape=[jax.ShapeDtypeStruct((N_DEV * m_per, n), block.dtype), jax.ShapeDtypeStruct((8, LANES), F32)],
        in_specs=[vmem], out_specs=[vmem, vmem],
        scratch_shapes=[pltpu.SemaphoreType.DMA((7,)), pltpu.SemaphoreType.DMA((7,)), pltpu.SemaphoreType.DMA],
    )(block)


_ANY = pl.BlockSpec(memory_space=pl.ANY)


def _half(ref, c, rows):
    return ref.at[pl.ds(c * (rows // 2), rows // 2), :]


_HBM = pl.BlockSpec(memory_space=pltpu.HBM)
_SEM = pl.BlockSpec(memory_space=pltpu.SEMAPHORE)
_EFFECT = pltpu.SideEffectType.DATAFLOW_SIDE_EFFECTING


def _ici_start(name, srcs, land_shapes, plan, per_source=3):
    ns, nl = len(srcs), len(land_shapes)
    n_copies = per_source * ns

    def body(*refs):
        src_refs, land_refs = refs[:ns], refs[ns:ns + nl]
        send_sems, recv_sems = refs[ns + nl], refs[ns + nl + 1]
        token = refs[-1]
        for n, (src, dst, peer, _) in enumerate(plan(src_refs, land_refs)):
            pltpu.make_async_remote_copy(src_ref=src, dst_ref=dst, send_sem=send_sems.at[n], recv_sem=recv_sems.at[n],
                                         device_id=peer, device_id_type=MESH).start()
        token[...] = jnp.zeros_like(token)

    lands = [lax.empty(s.shape, s.dtype) for s in land_shapes]
    out = pl.pallas_call(
        body, name=name,
        out_shape=(pltpu.SemaphoreType.DMA((n_copies,)), pltpu.SemaphoreType.DMA((n_copies,)),
                   *[pltpu.HBM(a.shape, a.dtype) for a in list(srcs) + lands], jax.ShapeDtypeStruct((8, LANES), F32)),
        in_specs=[_HBM] * (ns + nl),
        out_specs=(_SEM, _SEM, *[_HBM] * (ns + nl), pl.BlockSpec(memory_space=pltpu.VMEM)),
        input_output_aliases={n: 2 + n for n in range(ns + nl)},
        compiler_params=pltpu.CompilerParams(has_side_effects=_EFFECT),
    )(*[pltpu.with_memory_space_constraint(a, pltpu.HBM) for a in list(srcs) + lands])
    return out[0], out[1], list(out[2:2 + ns]), list(out[2 + ns:2 + ns + nl]), out[-1]


def _ici_wait(name, send_sems, recv_sems, srcs, lands, plan, after):
    ns, nl = len(srcs), len(lands)

    def body(*refs):
        src_refs, land_refs = refs[:ns], refs[ns:ns + nl]
        send_sems, recv_sems = refs[ns + nl], refs[ns + nl + 1]
        for n, (src, _, peer, mine) in enumerate(plan(src_refs, land_refs)):
            cp = pltpu.make_async_remote_copy(src_ref=src, dst_ref=mine, send_sem=send_sems.at[n],
                                              recv_sem=recv_sems.at[n], device_id=peer, device_id_type=MESH)
            cp.wait_send()
            cp.wait_recv()

    out = pl.pallas_call(
        body, name=name, out_shape=[pltpu.HBM(a.shape, a.dtype) for a in list(srcs) + list(lands)],
        in_specs=[_HBM] * (ns + nl) + [_SEM, _SEM, _ANY], out_specs=[_HBM] * (ns + nl),
        input_output_aliases={n: n for n in range(ns + nl)},
        compiler_params=pltpu.CompilerParams(has_side_effects=_EFFECT),
    )(*srcs, *lands, send_sems, recv_sems, after)
    return list(out[:ns]), list(out[ns:])


def _own_slab(name, chip, w, after):
    R, C = w.shape
    tr, tc = _tiles(R, C)

    def body(chip_ref, w_ref, after_ref, o_ref):
        o_ref[0] = w_ref[...].astype(BF16)

    grid_spec = pltpu.PrefetchScalarGridSpec(
        num_scalar_prefetch=1, grid=(R // tr, C // tc),
        in_specs=[pl.BlockSpec((tr, tc), lambda r, q, chip_ref: (r, q)),
                  pl.BlockSpec((8, LANES), lambda r, q, chip_ref: (0, 0))],
        out_specs=pl.BlockSpec((1, tr, tc), lambda r, q, chip_ref: (chip_ref[0], r, q)))
    return pl.pallas_call(
        body, name=name, grid_spec=grid_spec, out_shape=jax.ShapeDtypeStruct((N_CHIPS, R, C), BF16),
        compiler_params=pltpu.CompilerParams(dimension_semantics=("parallel", "parallel")),
    )(chip, w, after)


def _gather_plan(src_refs, land_refs):
    x, y, c = _place()
    copies = []
    for stack in src_refs:
        R = stack.shape[1]
        own = _half(stack.at[2 * x + y], c, R)
        for cx, cy in [(1 - x, y), (x, 1 - y), (1 - x, 1 - y)]:
            copies.append((own, own, (cx, cy, c), _half(stack.at[2 * cx + cy], c, R)))
    return copies


def _pass_plan(src_refs, land_refs):
    x, y, c = _place()
    copies = []
    for land in src_refs:
        R = land.shape[1]
        for cx, cy in [(1 - x, y), (x, 1 - y), (1 - x, 1 - y)]:
            slot = land.at[2 * cx + cy]
            copies.append((_half(slot, c, R), _half(slot, c, R), (x, y, 1 - c), _half(slot, 1 - c, R)))
    return copies


def _share_plan(src_refs, land_refs):
    x, y, c = _place()
    return [(h, land, (x, y, 1 - c), land) for h, land in zip(src_refs, land_refs)]


def _pass_to_sibling(name, lands):
    nw = len(lands)

    def body(*refs):
        ins, outs = refs[:nw], refs[nw:2 * nw]
        send_sems, recv_sems = refs[2 * nw:]
        x, y, c = _place()
        chips = [(1 - x, y), (x, 1 - y), (1 - x, 1 - y)]
        copies = []
        for k in range(nw):
            R = ins[k].shape[1]
            for j, (cx, cy) in enumerate(chips):
                cp = pltpu.make_async_remote_copy(
                    src_ref=_half(ins[k].at[2 * cx + cy], c, R), dst_ref=_half(outs[k].at[2 * cx + cy], c, R),
                    send_sem=send_sems.at[3 * k + j], recv_sem=recv_sems.at[3 * k + j],
                    device_id=(x, y, 1 - c), device_id_type=MESH)
                cp.start()
                copies.append(cp)
        for k in range(nw):
            R = ins[k].shape[1]
            for j, (cx, cy) in enumerate(chips):
                pltpu.make_async_remote_copy(
                    src_ref=_half(ins[k].at[2 * cx + cy], c, R), dst_ref=_half(outs[k].at[2 * cx + cy], 1 - c, R),
                    send_sem=send_sems.at[3 * k + j], recv_sem=recv_sems.at[3 * k + j],
                    device_id=(x, y, 1 - c), device_id_type=MESH).wait_recv()
        for cp in copies:
            cp.wait_send()

    return pl.pallas_call(
        body, name=name, out_shape=[jax.ShapeDtypeStruct(a.shape, a.dtype) for a in lands],
        in_specs=[_ANY] * nw, out_specs=[_ANY] * nw, input_output_aliases={k: k for k in range(nw)},
        scratch_shapes=[pltpu.SemaphoreType.DMA((3 * nw,)), pltpu.SemaphoreType.DMA((3 * nw,))],
    )(*lands)


def _tie(vec, token):
    return vec + token[0:1, 0:1]


ROW_ALIGN = 16
TILE_ELEMS = 512 * 1024


def _tiles(rows, cols):
    fits = [t for t in range(ROW_ALIGN, min(rows, 256) + 1, ROW_ALIGN) if rows % t == 0]
    tr = fits[-1] if fits and fits[-1] >= 64 else rows
    tc = cols
    while tr * tc > TILE_ELEMS and tc % (2 * LANES) == 0:
        tc //= 2
    return tr, tc


def _scatter_plan(src_refs, land_refs):
    x, y, c = _place()
    copies = []
    for p, land in zip(src_refs, land_refs):
        for j, (cx, cy) in enumerate([(1 - x, y), (x, 1 - y), (1 - x, 1 - y)]):
            copies.append((p.at[2 * cx + cy], land.at[j], (cx, cy, c), land.at[j]))
    return copies


def _chip_add(name, chip, sums, recv):
    _, H, C = sums.shape
    tr, tc = _tiles(H, C)

    def body(chip_ref, p_ref, r_ref, o_ref):
        total = p_ref[0].astype(F32)
        for j in range(3):
            total = total + r_ref[j].astype(F32)
        o_ref[...] = total

    grid_spec = pltpu.PrefetchScalarGridSpec(
        num_scalar_prefetch=1, grid=(H // tr, C // tc),
        in_specs=[pl.BlockSpec((1, tr, tc), lambda r, q, chip_ref: (chip_ref[0], r, q)),
                  pl.BlockSpec((3, tr, tc), lambda r, q, chip_ref: (0, r, q))],
        out_specs=pl.BlockSpec((tr, tc), lambda r, q, chip_ref: (r, q)))
    return pl.pallas_call(
        body, name=name, grid_spec=grid_spec, out_shape=jax.ShapeDtypeStruct((H, C), F32),
        compiler_params=pltpu.CompilerParams(dimension_semantics=("parallel", "parallel")),
    )(chip, sums, recv)


def _pair_share(name, halves):
    nw = len(halves)

    def body(*refs):
        hs, outs = refs[:nw], refs[nw:2 * nw]
        send_sems, recv_sems = refs[2 * nw:]
        x, y, c = _place()
        copies = []
        for k in range(nw):
            cp = pltpu.make_async_remote_copy(
                src_ref=hs[k], dst_ref=outs[k], send_sem=send_sems.at[k], recv_sem=recv_sems.at[k],
                device_id=(x, y, 1 - c), device_id_type=MESH)
            cp.start()
            copies.append(cp)
        for cp in copies:
            cp.wait()

    return pl.pallas_call(
        body, name=name,
        out_shape=[jax.ShapeDtypeStruct(h.shape, h.dtype) for h in halves],
        in_specs=[_ANY] * nw, out_specs=[_ANY] * nw,
        scratch_shapes=[pltpu.SemaphoreType.DMA((nw,)), pltpu.SemaphoreType.DMA((nw,))],
    )(*halves)


def _adam_halves(name, core, w, g_own, g_other, m, v):
    R, C = w.shape
    H = R // 2
    tr, tc = _tiles(H, C)
    nr, nc = H // tr, C // tc

    def body(core_ref, w_ref, go_ref, gr_ref, m_ref, v_ref, g_ref, dl_ref, nm_ref, nv_ref):
        own = (pl.program_id(0) // nr) == core_ref[0]
        g = jnp.where(own, go_ref[...], gr_ref[...])
        g_ref[...] = g
        dl_ref[...], nm_ref[...], nv_ref[...] = _adamw(w_ref[...], g, m_ref[...], v_ref[...])

    blk = pl.BlockSpec((tr, tc), lambda r, q, core_ref: (r, q))

    def half_spec(is_own):
        def index(r, q, core_ref):
            mine = ((r // nr) == core_ref[0]) == is_own
            done = is_own == (core_ref[0] == 0)
            return (jnp.where(mine, r % nr, jnp.where(done, nr - 1, 0)), jnp.where(mine, q, jnp.where(done, nc - 1, 0)))
        return pl.BlockSpec((tr, tc), index)
    out = jax.ShapeDtypeStruct((R, C), F32)
    grid_spec = pltpu.PrefetchScalarGridSpec(
        num_scalar_prefetch=1, grid=(R // tr, nc), in_specs=[blk, half_spec(True), half_spec(False), blk, blk],
        out_specs=[blk] * 4)
    return pl.pallas_call(
        body, name=name, grid_spec=grid_spec, out_shape=[out] * 4,
        compiler_params=pltpu.CompilerParams(dimension_semantics=("parallel", "parallel"),
                                             vmem_limit_bytes=_vmem(20 * tr * tc * 4)),
    )(core, w, g_own, g_other, m, v)


def kernel(x, c, w_mod, b_mod, g_pre_mix, g_post_mix, w_in, b_forget, swa_sinks, w_out, g_pre_mlp, g_post_mlp, w_up, w_down, loss_target, m_w_mod, m_b_mod, m_g_pre_mix, m_g_post_mix, m_w_in, m_b_forget, m_swa_sinks, m_w_out, m_g_pre_mlp, m_g_post_mlp, m_w_up, m_w_down, v_w_mod, v_b_mod, v_g_pre_mix, v_g_post_mix, v_w_in, v_b_forget, v_swa_sinks, v_w_out, v_g_pre_mlp, v_g_post_mlp, v_w_up, v_w_down):
    S, D = x.shape[1], x.shape[2]
    n_heads = D // HEAD_DIM
    n_fox = n_heads // 2
    n_swa = n_heads - n_fox
    n_kv = max(1, n_swa // 4)
    fox_w, swa_w, kv_w = n_fox * HEAD_DIM, n_swa * HEAD_DIM, n_kv * HEAD_DIM
    main_w = 3 * fox_w + swa_w + 2 * kv_w
    in_w = main_w + n_fox
    mod_cols = w_mod.shape[2]

    ax, ay, ac = _place()
    chip = 2 * ax + ay
    dev = 2 * chip + ac
    chip_arr = jnp.reshape(chip, (1,)).astype(jnp.int32)
    core_arr = jnp.reshape(ac, (1,)).astype(jnp.int32)

    x2, tgt = x[0], loss_target[0]

    c_all, _ = _allgather8("gather_c", c.reshape(8, D // 8))
    c_all = c_all.reshape(N_DEV, D)
    b_shard = lax.dynamic_slice_in_dim(b_mod, chip * mod_cols, mod_cols, axis=1)
    mod_shard = _mod_fwd(jnp.pad(c_all, ((0, 16 - N_DEV), (0, 0))), w_mod[0], b_shard)[:N_DEV]
    mod_all, token = _allgather8("gather_mod", mod_shard)
    mod_all = mod_all.reshape(N_CHIPS, 2, N_DEV, mod_cols)[:, 0]
    mod = lax.dynamic_index_in_dim(mod_all, dev, axis=1, keepdims=False).reshape(N_MOD, 1, D)
    sh_a, sc_a, gt_a, sh_m, sc_m, gt_m = [mod[n] for n in range(N_MOD)]

    in_rows = in_w // N_CHIPS
    in_rows_pad = -(-in_rows // (2 * LANES)) * (2 * LANES)
    slab_w = N_CHIPS * in_rows_pad

    def rows_of(a):
        return jnp.pad(a[0].T, ((0, in_rows_pad - in_rows), (0, 0)))

    def slab_cols(lo, hi):
        spans = []
        while lo < hi:
            s, r = divmod(lo, in_rows)
            n = min(hi - lo, in_rows - r)
            spans.append((s * in_rows_pad + r, s * in_rows_pad + r + n))
            lo += n
        return spans

    gate_lo = 3 * fox_w
    main_spans = slab_cols(0, gate_lo) + slab_cols(gate_lo + n_fox, in_w)
    (gate_first, gate_last), = slab_cols(gate_lo, gate_lo + n_fox)

    names = ["w_in", "w_out", "w_up", "w_down"]
    flights = {}
    for n, w in zip(names, [rows_of(w_in), w_out[0], w_up[0], w_down[0]]):
        flights[n] = _ici_start("gather_start_" + n, [_own_slab("own_slab_" + n, chip_arr, w, token)], [], _gather_plan)
        token = flights[n][4]
    sc_a = _tie(sc_a, token)

    def arrived(n, after):
        send, recv, stacks, _, _ = flights[n]
        stacks, _ = _ici_wait("gather_wait_" + n, send, recv, stacks, [], _gather_plan, after)
        return _ici_start("gather_pass_start_" + n, stacks, [], _pass_plan)

    def gathered(n, after, in_flight=None):
        if in_flight is None:
            send, recv, stacks, _, _ = flights[n]
            stacks, _ = _ici_wait("gather_wait_" + n, send, recv, stacks, [], _gather_plan, after)
            return _pass_to_sibling("gather_pass_" + n, stacks)[0]
        send, recv, stacks, _, _ = in_flight
        return _ici_wait("gather_pass_wait_" + n, send, recv, stacks, [], _pass_plan, after)[0][0]

    d_ff = N_CHIPS * w_up.shape[2]

    h = _pre_norm(x2, g_pre_mix, sc_a, sh_a)
    in_state = [rows_of(w_in)] + [rows_of(_tie(a, token)) for a in (m_w_in, v_w_in)]
    cos, sin_signed = _rope_tables(S)
    ready = h[:8, :LANES].astype(F32) + cos[:8] + sum(a[:8, :LANES] for a in in_state)
    w_slab_t = gathered("w_in", ready).reshape(slab_w, D)
    tm_p, tn_p = _fit(MM_TM, S), _fit(MM_TN if slab_w % MM_TN == 0 else MM_TN // 2, slab_w)
    win0 = gate_first // LANES * LANES
    win_j, win_off = divmod(win0, tn_p)
    assert win_off + 2 * LANES <= tn_p and gate_last - win0 <= 2 * LANES

    def proj_epilogue(acc, ex, outs):
        outs[0][...] = acc.astype(BF16)

        @pl.when(pl.program_id(1) == win_j)
        def _():
            outs[1][...] = acc[:, win_off:win_off + 2 * LANES]

    proj_slab, gate_win = _matmul(
        "in_proj", h, w_slab_t, "nt",
        [((S, slab_w), BF16, (tm_p, tn_p), lambda i, j: (i, j)), ((S, 2 * LANES), F32, (tm_p, 2 * LANES), lambda i, j: (i, 0))],
        proj_epilogue, tn=tn_p, revisits=True)
    proj = jnp.concatenate([proj_slab[:, lo:hi] for lo, hi in main_spans], axis=1)
    out_flight = arrived("w_out", proj_slab)
    fg = _tie(jnp.pad(gate_win[:, gate_first - win0:gate_last - win0], ((0, 0), (0, LANES - n_fox))), out_flight[4])
    b_pad = jnp.pad(b_forget, ((0, 0), (0, LANES - n_fox)))
    cum_row = _fox_gate_fwd(fg, b_pad)[:n_fox].reshape(n_fox, 1, S)
    fox_o, fox_lse = _fox_fwd(proj, cum_row, n_fox)

    rq = _rope("rope_fwd", proj, 3 * n_fox, n_swa + n_kv, cos, sin_signed)
    v_first = 3 * n_fox + n_swa + n_kv
    sinks = swa_sinks[0]
    swa_o, swa_lse = _swa_fwd(rq, proj, v_first, sinks, n_swa, n_kv)

    mixcat = jnp.concatenate([fox_o, swa_o], axis=1).astype(BF16)
    up_flight = arrived("w_up", mixcat)
    w_out_f = gathered("w_out", mixcat, out_flight).reshape(D, D)
    mix = _mm_plain("out_proj", mixcat, w_out_f, "nn", BF16, after=up_flight[4])
    x1, h2 = _post_mix(x2, mix, g_post_mix, gt_a, g_pre_mlp, sc_m, sh_m)
    w_up_f = jnp.transpose(gathered("w_up", h2, up_flight), (1, 0, 2)).reshape(D, d_ff)

    tm_u, tn_u = _fit(MM_TM, S), _fit(MM_TN, d_ff)

    def up_epilogue(acc, ex, outs):
        outs[0][...] = acc.astype(BF16)
        r = jnp.maximum(acc, 0.0)
        outs[1][...] = (r * r).astype(BF16)

    ublk = ((S, d_ff), BF16, (tm_u, tn_u), lambda i, j: (i, j))
    u, a = _matmul("mlp_up", h2, w_up_f, "nn", [ublk, ublk], up_epilogue)
    w_down_f = gathered("w_down", a).reshape(d_ff, D)
    y = _mm_plain("mlp_down", a, w_down_f, "nn", BF16)

    dy, dout, loss_part, acc_mlp_post = _loss_and_post_mlp_bwd(x1, y, tgt, g_post_mlp, gt_m)

    def du_epilogue(acc, ex, outs):
        outs[0][...] = (acc * (2.0 * jnp.maximum(ex[0][...].astype(F32), 0.0))).astype(BF16)

    du = _matmul("mlp_down_bwd", dy, w_down_f, "nt", [ublk], du_epilogue,
                 extras=[(u, (tm_u, tn_u), lambda i, j: (i, j))])[0]
    def pair_send(tag, part):
        return _ici_start("grad_pair_start_" + tag, [part], [jax.ShapeDtypeStruct(part.shape, BF16)], _share_plan,
                          per_source=1)

    def pair_recv(tag, flight, after):
        send, recv, srcs, lands, _ = flight
        return _ici_wait("grad_pair_wait_" + tag, send, recv, srcs, lands, _share_plan, after)[1][0]

    def scatter_start(tag, sums):
        return _ici_start("grad_scatter_start_" + tag, sums,
                          [jax.ShapeDtypeStruct((3,) + p.shape[1:], BF16) for p in sums], _scatter_plan)

    def scatter_finish(tag, flight, after):
        send, recv, srcs, lands, _ = flight
        sums, received = _ici_wait("grad_scatter_wait_" + tag, send, recv, srcs, lands, _scatter_plan, after)
        return [_chip_add("chip_add_%s_%d" % (tag, k), chip_arr, p, r) for k, (p, r) in enumerate(zip(sums, received))]

    tm_g = _fit(MM_TM, D // 2)
    pair_down = pair_send("down", _grad_half("grad_w_down_a", core_arr, a, dy, N_CHIPS, 1, tm_g, True))
    pair_up = pair_send("up", _grad_half("grad_w_up_a", core_arr, h2, du, 1, N_CHIPS, tm_g, True, after=pair_down[4]))
    sum_down = _grad_half("grad_w_down_b", core_arr, a, dy, N_CHIPS, 1, tm_g, False,
                          recv=pair_recv("down", pair_down, pair_up[4]))
    sum_up = _grad_half("grad_w_up_b", core_arr, h2, du, 1, N_CHIPS, tm_g, False, recv=pair_recv("up", pair_up, sum_down))
    flight_mlp = scatter_start("mlp", [sum_up, sum_down])
    dh2 = _mm_plain("mlp_up_bwd", du, w_up_f, "nt", BF16, after=flight_mlp[4])
    dx1, dmix, acc_mid = _pre_mlp_and_post_mix_bwd(dh2, x1, dout, mix, _tie(g_pre_mlp, flight_mlp[4]), sc_m,
                                                   g_post_mix, gt_a)

    dmixcat = _mm_plain("out_proj_bwd", dmix, w_out_f, "nt", F32)

    fdq, fdk, fdv, dcum_row, dcum_q = _fox_bwd(proj, fox_o, dmixcat, fox_lse, cum_row, n_fox)
    dcum_k = jnp.pad(dcum_row.reshape(n_fox, S), ((0, LANES - n_fox), (0, 0)))
    dfg, db_forget = _fox_gate_bwd(dcum_k, dcum_q, fg, b_pad)

    group_w = (n_swa // n_kv) * HEAD_DIM
    sdq, sdk, sdv, dsink = _swa_bwd(rq, proj, v_first, sinks, swa_o, dmixcat, fox_w // group_w, swa_lse, n_swa, n_kv)
    drq = jnp.concatenate([sdq, jnp.transpose(sdk, (1, 0, 2)).reshape(S, kv_w).astype(BF16)], axis=1)
    d_sq_sk = _rope("rope_bwd", drq, 0, n_swa + n_kv, cos, -sin_signed)
    dsv = jnp.transpose(sdv, (1, 0, 2)).reshape(S, kv_w).astype(BF16)
    dproj = jnp.concatenate([fdq, fdk, fdv, d_sq_sk, dsv], axis=1)

    pieces = []
    for s in range(N_CHIPS):
        lo, hi = s * in_rows, (s + 1) * in_rows
        for src, first, last, shift in [(dproj, 0, gate_lo, 0), (dfg, gate_lo, gate_lo + n_fox, gate_lo),
                                        (dproj, gate_lo + n_fox, in_w, n_fox)]:
            if max(lo, first) < min(hi, last):
                pieces.append(src[:, max(lo, first) - shift:min(hi, last) - shift])
        pieces.append(jnp.zeros((S, in_rows_pad - in_rows), BF16))
    dproj_slab = jnp.concatenate(pieces, axis=1)

    tm_in, tm_out = in_rows_pad // 2, D // (2 * N_CHIPS)
    pair_in = pair_send("in", _grad_half("grad_w_in_a", core_arr, dproj_slab, h, N_CHIPS, 1, tm_in, True))
    pair_out = pair_send("out", _grad_half("grad_w_out_a", core_arr, mixcat, dmix, N_CHIPS, 1, tm_out, True,
                                           after=pair_in[4]))
    sum_in = _grad_half("grad_w_in_b", core_arr, dproj_slab, h, N_CHIPS, 1, tm_in, False,
                        recv=pair_recv("in", pair_in, pair_out[4]))
    dh = _mm_plain("in_proj_bwd", dproj_slab, w_slab_t, "nn", BF16, tk=slab_w // 2)
    grad_x, acc_pre = _pre_mix_bwd(dh, x2, dx1, g_pre_mix, sc_a)

    zero_row = jnp.zeros((1, D), F32)
    tail = jnp.concatenate([db_forget[0:1, :n_fox], dsink[:, 0, :n_swa // n_kv].reshape(1, n_swa),
                            loss_part[0:1, 0:1], jnp.zeros((1, D - n_fox - n_swa - 1), F32)], axis=1)
    partial = jnp.concatenate([
        acc_pre[0:1], acc_pre[1:2], acc_mid[3:4], acc_mid[0:1], acc_mid[1:2], acc_mlp_post[0:1],
        acc_pre[2:3], acc_mid[4:5], acc_mid[2:3], acc_mlp_post[1:2], tail] + [zero_row] * 5, axis=0)
    gathered_small, token = _allgather8("gather_small_grads", partial)

    sum_out = _grad_half("grad_w_out_b", core_arr, mixcat, dmix, N_CHIPS, 1, tm_out, False,
                         recv=pair_recv("out", pair_out, token))
    flight_mix = scatter_start("mix", [sum_in, sum_out])
    halves_mlp = scatter_finish("mlp", flight_mlp, flight_mix[4])
    share_mlp = _ici_start("grad_share_start_mlp", halves_mlp,
                           [jax.ShapeDtypeStruct(hv.shape, F32) for hv in halves_mlp], _share_plan, per_source=1)

    def pack(bm, gpm, gqm, gpl, gql, bf, sk):
        last = jnp.concatenate([bf, sk, jnp.zeros((1, D - n_fox - n_swa), F32)], axis=1)
        return jnp.concatenate([bm.reshape(N_MOD, D), gpm, gqm, gpl, gql, last, jnp.zeros((5, D), F32)], axis=0)

    def unpack(p):
        return {"b_mod": p[0:N_MOD].reshape(1, N_MOD * D), "g_pre_mix": p[6:7], "g_post_mix": p[7:8],
                "g_pre_mlp": p[8:9], "g_post_mlp": p[9:10], "b_forget": p[10:11, :n_fox],
                "swa_sinks": p[10:11, n_fox:n_fox + n_swa]}

    small_out = _small_update(
        gathered_small, _tie(pack(b_mod, g_pre_mix, g_post_mix, g_pre_mlp, g_post_mlp, b_forget, swa_sinks), share_mlp[4]),
        pack(m_b_mod, m_g_pre_mix, m_g_post_mix, m_g_pre_mlp, m_g_post_mlp, m_b_forget, m_swa_sinks),
        pack(v_b_mod, v_g_pre_mix, v_g_post_mix, v_g_pre_mlp, v_g_post_mlp, v_b_forget, v_swa_sinks))
    g_small, d_small, m_small, v_small = [unpack(p) for p in small_out]
    loss = small_out[0][N_MOD + 4, n_fox + n_swa]

    dmod_all = gathered_small.reshape(N_DEV, 16, D)[:, :N_MOD].reshape(N_DEV, N_MOD * D)
    dmod_shard = _tie(lax.dynamic_slice_in_dim(dmod_all, chip * mod_cols, mod_cols, axis=1), share_mlp[4])
    g_w_mod, d_w_mod, nm_w_mod, nv_w_mod = _mod_update(c_all.T, dmod_shard, w_mod[0], m_w_mod[0], v_w_mod[0])
    send, recv, halves_mlp, lands, _ = share_mlp
    halves_mlp, others_mlp = _ici_wait("grad_share_wait_mlp", send, recv, halves_mlp, lands, _share_plan,
                                       d_w_mod[:8, :LANES] + small_out[1][:8, :LANES])

    grads = dict(g_small, w_mod=g_w_mod[None])
    deltas = dict(d_small, w_mod=d_w_mod[None])
    new_m = dict(m_small, w_mod=nm_w_mod[None])
    new_v = dict(v_small, w_mod=nv_w_mod[None])
    weights = {"w_in": (w_in, m_w_in, v_w_in), "w_out": (w_out, m_w_out, v_w_out), "w_up": (w_up, m_w_up, v_w_up),
               "w_down": (w_down, m_w_down, v_w_down)}

    def big_update(n, own, other):
        transposed = n == "w_in"
        w, m, v = in_state if transposed else [a[0] for a in weights[n]]
        outs = _adam_halves("adam_" + n, core_arr, w, own, other, m, v)
        if transposed:
            outs = [o[:in_rows].T for o in outs]
        grads[n], deltas[n], new_m[n], new_v[n] = [o[None] for o in outs]

    big_update("w_up", halves_mlp[0], others_mlp[0])
    big_update("w_down", halves_mlp[1], others_mlp[1])
    ran = deltas["w_down"][0, :8, :LANES] + deltas["w_up"][0, :8, :LANES] + d_w_mod[:8, :LANES]
    halves_mix = scatter_finish("mix", flight_mix, ran)
    others_mix = _pair_share("grad_pair_share_mix", halves_mix)
    big_update("w_in", halves_mix[0], others_mix[0])
    big_update("w_out", halves_mix[1], others_mix[1])

    order = ["w_mod", "b_mod", "g_pre_mix", "g_post_mix", "w_in", "b_forget", "swa_sinks", "w_out", "g_pre_mlp",
             "g_post_mlp", "w_up", "w_down"]
    return (loss, grad_x[None], *[grads[n] for n in order], *[deltas[n] for n in order],
            *[new_m[n] for n in order], *[new_v[n] for n in order])
```

```python
import jax
import jax.numpy as jnp
from jax import lax
from jax.experimental import pallas as pl
from jax.experimental.pallas import tpu as pltpu

F32 = jnp.float32
BF16 = jnp.bfloat16
MESH = pl.DeviceIdType.MESH

HEAD_DIM = 128
SWA_BLOCK = 128
ROPE_THETA = 10000.0
NORM_EPS = 1e-6
NEG = -1e30
N_MOD = 6
ADAM_LR = 0.001
ADAM_B1 = 0.9
ADAM_B2 = 0.999
ADAM_EPS = 1e-08
ADAM_WD = 0.01
ADAM_STEP = 10
N_CHIPS = 4
N_DEV = 8
LANES = 128
VMEM_CAP = 60 * 1024 * 1024

_NN = (((1,), (0,)), ((), ()))
_NT = (((1,), (1,)), ((), ()))
_TN = (((0,), (0,)), ((), ()))


def _vmem(nbytes):
    return int(min(VMEM_CAP, nbytes * 5 // 4 + (4 << 20)))


def _nbytes(shape, dtype):
    n = 1
    for s in shape:
        n *= s
    return n * jnp.dtype(dtype).itemsize


def _fit(t, n):
    t = min(t, n)
    assert n % t == 0, (t, n)
    return t


MM_TM, MM_TN, MM_TK = 512, 1024, 2048


def _matmul(name, a, b, mode, out_defs, epilogue, extras=(), tm=MM_TM, tn=MM_TN, tk=MM_TK, revisits=False,
            row_sel=None):
    if mode == "nn":
        (M, K), (K2, N) = a.shape, b.shape
    elif mode == "nt":
        (M, K), (N, K2) = a.shape, b.shape
    else:
        (K, M), (K2, N) = a.shape, b.shape
    assert K == K2, (a.shape, b.shape, mode)
    tm, tn, tk = _fit(tm, M), _fit(tn, N), _fit(tk, K)
    nk = K // tk
    dims = {"nn": _NN, "nt": _NT, "tn": _TN}[mode]
    if row_sel is None:
        grid_m, a_row = M // tm, lambda i, *sel: i
    else:
        grid_m, a_row = row_sel[2], lambda i, *sel: row_sel[1](i, sel[0])
    a_spec = (pl.BlockSpec((tk, tm), lambda i, j, k, *sel: (k, a_row(i, *sel))) if mode == "tn"
              else pl.BlockSpec((tm, tk), lambda i, j, k, *sel: (a_row(i, *sel), k)))
    b_spec = (pl.BlockSpec((tn, tk), lambda i, j, k, *sel: (j, k)) if mode == "nt"
              else pl.BlockSpec((tk, tn), lambda i, j, k, *sel: (k, j)))
    n_ex, n_out = len(extras), len(out_defs)

    def body(*refs):
        if row_sel is not None:
            refs = refs[1:]
        a_ref, b_ref = refs[0], refs[1]
        ex = refs[2:2 + n_ex]
        outs = refs[2 + n_ex:2 + n_ex + n_out]
        prod = lax.dot_general(a_ref[...], b_ref[...], dims, preferred_element_type=F32)
        if nk == 1:
            epilogue(prod, ex, outs)
        else:
            acc_ref = refs[-1]
            k = pl.program_id(2)

            @pl.when(k == 0)
            def _():
                acc_ref[...] = prod

            @pl.when(k > 0)
            def _():
                acc_ref[...] += prod

            @pl.when(k == nk - 1)
            def _():
                epilogue(acc_ref[...], ex, outs)

    def wrap(f):
        return lambda i, j, k, *sel: f(i, j)

    in_specs = [a_spec, b_spec] + [pl.BlockSpec(blk, wrap(f)) for _, blk, f in extras]
    out_specs = [pl.BlockSpec(blk, wrap(f)) for _, _, blk, f in out_defs]
    out_shape = [jax.ShapeDtypeStruct(s, d) for s, d, _, _ in out_defs]
    need = 2 * (tm * tk + tk * tn) * a.dtype.itemsize + 3 * tm * tn * 4
    need += sum(2 * _nbytes(blk, arr.dtype) for arr, blk, _ in extras)
    need += sum(2 * _nbytes(blk, d) for _, d, blk, _ in out_defs)
    grid = (grid_m, N // tn, nk)
    scratch = [pltpu.VMEM((tm, tn), F32)] if nk > 1 else []
    params = pltpu.CompilerParams(
        dimension_semantics=("parallel", "arbitrary" if revisits else "parallel", "arbitrary"),
        vmem_limit_bytes=_vmem(need))
    operands = (a, b, *[arr for arr, _, _ in extras])
    if row_sel is None:
        return pl.pallas_call(body, name=name, grid=grid, in_specs=in_specs, out_specs=out_specs, out_shape=out_shape,
                              scratch_shapes=scratch, compiler_params=params)(*operands)
    grid_spec = pltpu.PrefetchScalarGridSpec(num_scalar_prefetch=1, grid=grid, in_specs=in_specs, out_specs=out_specs,
                                             scratch_shapes=scratch)
    return pl.pallas_call(body, name=name, grid_spec=grid_spec, out_shape=out_shape,
                          compiler_params=params)(row_sel[0], *operands)


def _grad_half(name, core, a, b, row_slabs, col_slabs, tm, other, recv=None, after=None):
    (_, M), (_, N) = a.shape, b.shape
    H = M // (2 * row_slabs)
    nh = H // tm
    tn = _fit(MM_TN, N // col_slabs)
    per = N // col_slabs // tn

    def a_block(i, core_ref):
        half = (1 - core_ref[0]) if other else core_ref[0]
        return (i // nh) * (2 * nh) + half * nh + i % nh

    def out_index(i, j):
        return (j // per, i, j % per) if col_slabs > 1 else (i // nh, i % nh, j)

    slabs = max(row_slabs, col_slabs)
    out_def = ((slabs, H, N // col_slabs), BF16, (1, tm, tn), out_index)

    def epilogue(acc, ex, outs):
        outs[0][0] = (acc if recv is None else acc + ex[0][0].astype(F32)).astype(BF16)

    extras = ([] if recv is None else [(recv, (1, tm, tn), out_index)]) + ([] if after is None else [_behind(after)])
    return _matmul(name, a, b, "tn", [out_def], epilogue, extras=extras, tm=tm, tn=tn,
                   row_sel=(core, a_block, row_slabs * nh))[0]


def _behind(token):
    return (token, (8, LANES), lambda i, j: (0, 0))


def _mm_plain(name, a, b, mode, out_dtype, after=None, **tiles):
    if mode == "nn":
        M, N = a.shape[0], b.shape[1]
    elif mode == "nt":
        M, N = a.shape[0], b.shape[0]
    else:
        M, N = a.shape[1], b.shape[1]
    tm, tn = _fit(tiles.get("tm", MM_TM), M), _fit(tiles.get("tn", MM_TN), N)

    def epi(acc, ex, outs):
        outs[0][...] = acc.astype(out_dtype)

    return _matmul(name, a, b, mode, [((M, N), out_dtype, (tm, tn), lambda i, j: (i, j))], epi,
                   extras=[] if after is None else [_behind(after)], **tiles)[0]


def _rstd(v):
    return lax.rsqrt(jnp.mean(v * v, axis=-1, keepdims=True) + NORM_EPS)


def _row_call(name, body, row_ins, vec_ins, row_outs, acc_outs, S, D, tr):
    tr = _fit(tr, S)
    row_spec = pl.BlockSpec((tr, D), lambda r: (r, 0))
    vec_spec = pl.BlockSpec((1, D), lambda r: (0, 0))
    in_specs = [row_spec] * len(row_ins) + [vec_spec] * len(vec_ins)
    out_specs = [row_spec] * len(row_outs) + [pl.BlockSpec(shp, lambda r: (0, 0)) for shp in acc_outs]
    out_shape = [jax.ShapeDtypeStruct((S, D), d) for d in row_outs] + [jax.ShapeDtypeStruct(shp, F32) for shp in acc_outs]
    need = sum(2 * tr * D * a.dtype.itemsize for a in row_ins) + sum(2 * tr * D * jnp.dtype(d).itemsize for d in row_outs)
    need += 8 * tr * D * 4
    return pl.pallas_call(
        body, name=name, grid=(S // tr,), in_specs=in_specs, out_specs=out_specs, out_shape=out_shape,
        compiler_params=pltpu.CompilerParams(dimension_semantics=("arbitrary",), vmem_limit_bytes=_vmem(need)),
    )(*row_ins, *vec_ins)


def _acc_rows(ref, rows):
    @pl.when(pl.program_id(0) == 0)
    def _():
        ref[...] = jnp.zeros_like(ref)
    for n, r in enumerate(rows):
        ref[n:n + 1, :] += r


def _pre_norm(x, g, sc, sh):
    S, D = x.shape

    def body(x_ref, g_ref, sc_ref, sh_ref, h_ref):
        xv = x_ref[...]
        xn = xv * _rstd(xv)
        h_ref[...] = (xn * g_ref[...] * (1.0 + sc_ref[...]) + sh_ref[...]).astype(BF16)

    return _row_call("pre_norm_mix", body, [x], [g, sc, sh], [BF16], [], S, D, 256)[0]


def _post_mix(x, mix, g_post, gt, g_pre, sc, sh):
    S, D = x.shape

    def body(x_ref, mix_ref, gp_ref, gt_ref, g2_ref, sc_ref, sh_ref, x1_ref, h2_ref):
        mv = mix_ref[...].astype(F32)
        x1 = x_ref[...] + gt_ref[...] * (mv * _rstd(mv) * gp_ref[...])
        x1_ref[...] = x1
        h2_ref[...] = (x1 * _rstd(x1) * g2_ref[...] * (1.0 + sc_ref[...]) + sh_ref[...]).astype(BF16)

    return _row_call("post_mix_pre_mlp", body, [x, mix], [g_post, gt, g_pre, sc, sh], [F32, BF16], [], S, D, 256)


def _loss_and_post_mlp_bwd(x1, y, target, g_post, gt):
    S, D = x1.shape

    def body(x1_ref, y_ref, t_ref, g_ref, gt_ref, dy_ref, dout_ref, loss_ref, acc_ref):
        yv = y_ref[...].astype(F32)
        r = _rstd(yv)
        yh = yv * r
        n = yh * g_ref[...]
        diff = x1_ref[...] + gt_ref[...] * n - t_ref[...]
        dout = diff * (1.0 / D)
        dout_ref[...] = dout
        dn = dout * gt_ref[...]
        dyh = dn * g_ref[...]
        dy_ref[...] = (r * (dyh - yh * jnp.mean(dyh * yh, axis=-1, keepdims=True))).astype(BF16)
        _acc_rows(acc_ref, [jnp.sum(dout * n, axis=0, keepdims=True), jnp.sum(dn * yh, axis=0, keepdims=True)])

        @pl.when(pl.program_id(0) == 0)
        def _():
            loss_ref[...] = jnp.zeros_like(loss_ref)
        loss_ref[...] += jnp.full(loss_ref.shape, (0.5 / D) * jnp.sum(diff * diff), F32)

    return _row_call("loss_post_mlp_bwd", body, [x1, y, target], [g_post, gt], [BF16, F32],
                     [(8, LANES), (8, D)], S, D, 128)


def _pre_mlp_and_post_mix_bwd(dh2, x1, dout, mix, g_pre, sc, g_post, gt):
    S, D = x1.shape

    def body(dh_ref, x1_ref, dout_ref, mix_ref, g_ref, sc_ref, gp_ref, gt_ref, dx1_ref, dmix_ref, acc_ref):
        dh = dh_ref[...].astype(F32)
        x1v = x1_ref[...]
        r3 = _rstd(x1v)
        xn = x1v * r3
        dxn = dh * (1.0 + sc_ref[...]) * g_ref[...]
        dx1 = dout_ref[...] + r3 * (dxn - xn * jnp.mean(dxn * xn, axis=-1, keepdims=True))
        dx1_ref[...] = dx1
        mv = mix_ref[...].astype(F32)
        r2 = _rstd(mv)
        mh = mv * r2
        dn = dx1 * gt_ref[...]
        dmh = dn * gp_ref[...]
        dmix_ref[...] = (r2 * (dmh - mh * jnp.mean(dmh * mh, axis=-1, keepdims=True))).astype(BF16)
        _acc_rows(acc_ref, [
            jnp.sum(dh, axis=0, keepdims=True),
            jnp.sum(dh * xn * g_ref[...], axis=0, keepdims=True),
            jnp.sum(dh * (1.0 + sc_ref[...]) * xn, axis=0, keepdims=True),
            jnp.sum(dx1 * mh * gp_ref[...], axis=0, keepdims=True),
            jnp.sum(dn * mh, axis=0, keepdims=True)])

    return _row_call("pre_mlp_post_mix_bwd", body, [dh2, x1, dout, mix], [g_pre, sc, g_post, gt], [F32, BF16],
                     [(8, D)], S, D, 128)


def _pre_mix_bwd(dh, x, dx1, g_pre, sc):
    S, D = x.shape

    def body(dh_ref, x_ref, dx1_ref, g_ref, sc_ref, gx_ref, acc_ref):
        dhv = dh_ref[...].astype(F32)
        xv = x_ref[...]
        r = _rstd(xv)
        xn = xv * r
        dxn = dhv * (1.0 + sc_ref[...]) * g_ref[...]
        gx_ref[...] = dx1_ref[...] + r * (dxn - xn * jnp.mean(dxn * xn, axis=-1, keepdims=True))
        _acc_rows(acc_ref, [
            jnp.sum(dhv, axis=0, keepdims=True),
            jnp.sum(dhv * xn * g_ref[...], axis=0, keepdims=True),
            jnp.sum(dhv * (1.0 + sc_ref[...]) * xn, axis=0, keepdims=True)])

    return _row_call("pre_mix_bwd", body, [dh, x, dx1], [g_pre, sc], [F32], [(8, D)], S, D, 128)


CUM_BLOCK = 256


def _tri(n, upper):
    r = lax.broadcasted_iota(jnp.int32, (n, n), 0)
    c = lax.broadcasted_iota(jnp.int32, (n, n), 1)
    return ((c >= r) if upper else (c <= r)).astype(F32)


def _fox_gate_fwd(fg, b_pad):
    S = fg.shape[0]
    cb = _fit(CUM_BLOCK, S)

    def body(fg_ref, b_ref, cumt_ref, cum_ref):
        low = _tri(cb, False)
        carry = jnp.zeros((1, LANES), F32)
        for n in range(S // cb):
            z = fg_ref[n * cb:(n + 1) * cb, :] + b_ref[...]
            logf = jnp.minimum(z, 0.0) - jnp.log(1.0 + jnp.exp(-jnp.abs(z)))
            blk = jnp.dot(low, logf, precision=lax.Precision.HIGHEST, preferred_element_type=F32) + carry
            cum_ref[n * cb:(n + 1) * cb, :] = blk
            carry = blk[cb - 1:cb, :]
        cumt_ref[...] = cum_ref[...].T

    return pl.pallas_call(
        body, name="fox_gate_fwd", out_shape=jax.ShapeDtypeStruct((LANES, S), F32),
        scratch_shapes=[pltpu.VMEM((S, LANES), F32)],
        compiler_params=pltpu.CompilerParams(vmem_limit_bytes=_vmem(6 * S * LANES * 4)),
    )(fg, b_pad)


def _fox_gate_bwd(dcum_k, dcum_q, fg, b_pad):
    S = fg.shape[0]
    n_fox = dcum_q.shape[0]
    cb = _fit(CUM_BLOCK, S)

    def body(dk_ref, dq_ref, fg_ref, b_ref, dfg_ref, db_ref, dc_ref):
        lane = lax.broadcasted_iota(jnp.int32, (S, LANES), 1)
        dc = dk_ref[...].T
        for h in range(n_fox):
            dc = dc + jnp.where(lane == h, dq_ref[h], 0.0)
        dc_ref[...] = dc
        up = _tri(cb, True)
        carry = jnp.zeros((1, LANES), F32)
        db = jnp.zeros((1, LANES), F32)
        for n in reversed(range(S // cb)):
            blk = jnp.dot(up, dc_ref[n * cb:(n + 1) * cb, :], precision=lax.Precision.HIGHEST,
                          preferred_element_type=F32) + carry
            carry = blk[0:1, :]
            z = fg_ref[n * cb:(n + 1) * cb, :] + b_ref[...]
            dfg = blk * (1.0 / (1.0 + jnp.exp(z)))
            dfg_ref[n * cb:(n + 1) * cb, :] = dfg.astype(BF16)
            db = db + jnp.sum(dfg, axis=0, keepdims=True)
        db_ref[...] = jnp.broadcast_to(db, db_ref.shape)

    return pl.pallas_call(
        body, name="fox_gate_bwd",
        out_shape=[jax.ShapeDtypeStruct((S, LANES), BF16), jax.ShapeDtypeStruct((8, LANES), F32)],
        scratch_shapes=[pltpu.VMEM((S, LANES), F32)],
        compiler_params=pltpu.CompilerParams(vmem_limit_bytes=_vmem((8 + 2 * n_fox) * S * LANES * 4)),
    )(dcum_k, dcum_q, fg, b_pad)


FOX_TILE = 512


LOG2E = 1.4426950408889634


def _fox_scores(q, k, ck2, masked, t):
    s = lax.dot_general(q, k, _NT, preferred_element_type=F32) * (HEAD_DIM ** -0.5 * LOG2E) - ck2
    if masked:
        row = lax.broadcasted_iota(jnp.int32, (t, t), 0)
        col = lax.broadcasted_iota(jnp.int32, (t, t), 1)
        s = jnp.where(col <= row, s, NEG)
    return s


def _fox_fwd(proj, cum_row, n_fox):
    S = proj.shape[0]
    t = _fit(FOX_TILE, S)
    nq = S // t

    def body(q_ref, k_ref, v_ref, ck_ref, o_ref, lse_ref):
        def q_block(qi, _):
            q0 = pl.multiple_of(qi * t, t)
            q = q_ref[pl.ds(q0, t), :]

            def kv_block(j, carry, masked):
                m, l, acc = carry
                k0 = pl.multiple_of(j * t, t)
                s = _fox_scores(q, k_ref[pl.ds(k0, t), :], ck_ref[0, :, pl.ds(k0, t)] * LOG2E, masked, t)
                m_new = jnp.maximum(m, jnp.max(s, axis=-1, keepdims=True))
                alpha = jnp.exp2(m - m_new)
                p = jnp.exp2(s - m_new)
                l = alpha * l + jnp.sum(p, axis=-1, keepdims=True)
                acc = alpha * acc + jnp.dot(p.astype(BF16), v_ref[pl.ds(k0, t), :], preferred_element_type=F32)
                return m_new, l, acc

            init = (jnp.full((t, 1), NEG, F32), jnp.zeros((t, 1), F32), jnp.zeros((t, HEAD_DIM), F32))
            carry = lax.fori_loop(0, qi, lambda j, cr: kv_block(j, cr, False), init)
            m, l, acc = kv_block(qi, carry, True)
            o_ref[pl.ds(q0, t), :] = acc / l
            lse_ref[0, pl.ds(q0, t), :] = jnp.broadcast_to(m + jnp.log(l) * LOG2E, (t, LANES))
            return 0

        lax.fori_loop(0, nq, q_block, 0)

    col = lambda off: pl.BlockSpec((S, HEAD_DIM), lambda h: (0, off + h))
    per_head = pl.BlockSpec((1, S, LANES), lambda h: (h, 0, 0))
    return pl.pallas_call(
        body, name="fox_fwd", grid=(n_fox,),
        in_specs=[col(0), col(n_fox), col(2 * n_fox), pl.BlockSpec((1, 1, S), lambda h: (h, 0, 0))],
        out_specs=[pl.BlockSpec((S, HEAD_DIM), lambda h: (0, h)), per_head],
        out_shape=[jax.ShapeDtypeStruct((S, n_fox * HEAD_DIM), F32), jax.ShapeDtypeStruct((n_fox, S, LANES), F32)],
        compiler_params=pltpu.CompilerParams(dimension_semantics=("parallel",),
                                             vmem_limit_bytes=_vmem(16 * S * HEAD_DIM * 4 + 12 * t * t * 4)),
    )(proj, proj, proj, cum_row)


def _fox_bwd(proj, o, do, lse_b, cum_row, n_fox):
    S = proj.shape[0]
    t = _fit(FOX_TILE, S)
    nq = S // t
    scale = HEAD_DIM ** -0.5

    def body(q_ref, k_ref, v_ref, o_ref, do_ref, lse_ref, ck_ref, dq_ref, dk_ref, dv_ref, dc_ref, dcq_ref,
             dq_acc, delta_ref):
        dq_acc[...] = jnp.zeros_like(dq_acc)
        dcq_ref[...] = jnp.zeros_like(dcq_ref)

        def delta_block(qi, _):
            q0 = pl.multiple_of(qi * t, t)
            d = jnp.sum(do_ref[pl.ds(q0, t), :] * o_ref[pl.ds(q0, t), :], axis=-1, keepdims=True)
            delta_ref[pl.ds(q0, t), :] = jnp.broadcast_to(d, (t, LANES))
            return 0

        lax.fori_loop(0, nq, delta_block, 0)

        def kv_block(j, _):
            k0 = pl.multiple_of(j * t, t)
            k = k_ref[pl.ds(k0, t), :]
            v = v_ref[pl.ds(k0, t), :]
            ck2 = ck_ref[0, :, pl.ds(k0, t)] * LOG2E

            def q_block(qi, carry, masked):
                dk, dv, dc = carry
                q0 = pl.multiple_of(qi * t, t)
                q = q_ref[pl.ds(q0, t), :]
                dov = do_ref[pl.ds(q0, t), :].astype(BF16)
                p = jnp.exp2(_fox_scores(q, k, ck2, masked, t) - lse_ref[0, pl.ds(q0, t), :][:, :1])
                dp = lax.dot_general(dov, v, _NT, preferred_element_type=F32)
                ds = p * (dp - delta_ref[pl.ds(q0, t), :][:, :1])
                dsb = ds.astype(BF16)
                dv = dv + lax.dot_general(p.astype(BF16), dov, _TN, preferred_element_type=F32)
                dk = dk + lax.dot_general(dsb, q, _TN, preferred_element_type=F32)
                dq_acc[pl.ds(q0, t), :] += jnp.dot(dsb, k, preferred_element_type=F32)
                dc = dc - jnp.sum(ds, axis=0, keepdims=True)
                dcq_ref[0, pl.ds(q0, t), :] += jnp.broadcast_to(jnp.sum(ds, axis=1, keepdims=True), (t, LANES))
                return dk, dv, dc

            init = (jnp.zeros((t, HEAD_DIM), F32), jnp.zeros((t, HEAD_DIM), F32), jnp.zeros((1, t), F32))
            carry = q_block(j, init, True)
            dk, dv, dc = lax.fori_loop(j + 1, nq, lambda qi, cr: q_block(qi, cr, False), carry)
            dk_ref[pl.ds(k0, t), :] = (dk * scale).astype(BF16)
            dv_ref[pl.ds(k0, t), :] = dv.astype(BF16)
            dc_ref[0, :, pl.ds(k0, t)] = dc
            return 0

        lax.fori_loop(0, nq, kv_block, 0)
        dq_ref[...] = (dq_acc[...] * scale).astype(BF16)

    col = lambda off: pl.BlockSpec((S, HEAD_DIM), lambda h: (0, off + h))
    per_head = pl.BlockSpec((1, S, LANES), lambda h: (h, 0, 0))
    row = pl.BlockSpec((1, 1, S), lambda h: (h, 0, 0))
    grad = jax.ShapeDtypeStruct((S, n_fox * HEAD_DIM), BF16)
    return pl.pallas_call(
        body, name="fox_bwd", grid=(n_fox,),
        in_specs=[col(0), col(n_fox), col(2 * n_fox), col(0), col(0), per_head, row],
        out_specs=[col(0), col(0), col(0), row, per_head],
        out_shape=[grad, grad, grad, jax.ShapeDtypeStruct((n_fox, 1, S), F32), jax.ShapeDtypeStruct((n_fox, S, LANES), F32)],
        scratch_shapes=[pltpu.VMEM((S, HEAD_DIM), F32), pltpu.VMEM((S, LANES), F32)],
        compiler_params=pltpu.CompilerParams(dimension_semantics=("parallel",),
                                             vmem_limit_bytes=_vmem(24 * S * HEAD_DIM * 4 + 16 * t * t * 4)),
    )(proj, proj, proj, o, do, lse_b, cum_row)


def _rope_tables(S):
    half = HEAD_DIM // 2
    inv_freq = 1.0 / (ROPE_THETA ** (jnp.arange(half, dtype=F32) * (2.0 / HEAD_DIM)))
    ang = jnp.arange(S).astype(F32)[:, None] * inv_freq[None, :]
    cos, sin = jnp.cos(ang), jnp.sin(ang)
    return jnp.concatenate([cos, cos], axis=-1), jnp.concatenate([-sin, sin], axis=-1)


def _rope(name, src, first_block, n_blocks, cos, sin_signed):
    S = src.shape[0]

    def body(x_ref, cos_ref, sin_ref, o_ref):
        xv = x_ref[...].astype(F32)
        o_ref[...] = (xv * cos_ref[...] + pltpu.roll(xv, HEAD_DIM // 2, 1) * sin_ref[...]).astype(BF16)

    table = pl.BlockSpec((S, HEAD_DIM), lambda n: (0, 0))
    return pl.pallas_call(
        body, name=name, grid=(n_blocks,),
        in_specs=[pl.BlockSpec((S, HEAD_DIM), lambda n: (0, first_block + n)), table, table],
        out_specs=pl.BlockSpec((S, HEAD_DIM), lambda n: (0, n)),
        out_shape=jax.ShapeDtypeStruct((S, n_blocks * HEAD_DIM), BF16),
        compiler_params=pltpu.CompilerParams(dimension_semantics=("parallel",),
                                             vmem_limit_bytes=_vmem(12 * S * HEAD_DIM * 4)),
    )(src, cos, sin_signed)


def _swa_tile(q_ref, kp_ref, kc_ref, n, group, scale):
    B = SWA_BLOCK
    qs = jnp.concatenate([q_ref[:, g * HEAD_DIM:(g + 1) * HEAD_DIM] for g in range(group)], axis=0)
    kcat = jnp.concatenate([kp_ref[...], kc_ref[...]], axis=0)
    s = lax.dot_general(qs, kcat, _NT, preferred_element_type=F32) * scale
    qi = lax.broadcasted_iota(jnp.int32, (group * B, 2 * B), 0) % B
    kj = lax.broadcasted_iota(jnp.int32, (group * B, 2 * B), 1)
    diff = qi + B - kj
    mask = (diff >= 0) & (diff < B) & ((n * B + kj - B) >= 0)
    return qs, kcat, jnp.where(mask, s, NEG)


def _swa_sink_col(sink_ref, kv, group):
    head = lax.broadcasted_iota(jnp.int32, (group * SWA_BLOCK, 1), 0) // SWA_BLOCK
    col = jnp.zeros((group * SWA_BLOCK, 1), F32)
    for g in range(group):
        col = jnp.where(head == g, sink_ref[kv * group + g], col)
    return col


def _swa_specs(n_kv, group, q_first, k_first, v_first):
    B = SWA_BLOCK
    prev = lambda n: jnp.maximum(n - 1, 0)
    return [
        pl.BlockSpec((B, group * HEAD_DIM), lambda kv, n: (n, q_first + kv)),
        pl.BlockSpec((B, HEAD_DIM), lambda kv, n: (prev(n), k_first + kv)),
        pl.BlockSpec((B, HEAD_DIM), lambda kv, n: (n, k_first + kv)),
        pl.BlockSpec((B, HEAD_DIM), lambda kv, n: (prev(n), v_first + kv)),
        pl.BlockSpec((B, HEAD_DIM), lambda kv, n: (n, v_first + kv)),
    ]


def _swa_fwd(rq, proj, v_first, sinks, n_q, n_kv):
    S = rq.shape[0]
    B = SWA_BLOCK
    group = n_q // n_kv
    scale = HEAD_DIM ** -0.5

    def body(q_ref, kp_ref, kc_ref, vp_ref, vc_ref, sink_ref, o_ref, lse_ref):
        kv, n = pl.program_id(0), pl.program_id(1)
        _, _, s = _swa_tile(q_ref, kp_ref, kc_ref, n, group, scale)
        sink = _swa_sink_col(sink_ref, kv, group)
        m = jnp.maximum(jnp.max(s, axis=-1, keepdims=True), sink)
        p = jnp.exp(s - m)
        denom = jnp.sum(p, axis=-1, keepdims=True) + jnp.exp(sink - m)
        vcat = jnp.concatenate([vp_ref[...], vc_ref[...]], axis=0)
        o = jnp.dot((p / denom).astype(BF16), vcat, preferred_element_type=F32)
        lse = m + jnp.log(denom)
        for g in range(group):
            o_ref[:, g * HEAD_DIM:(g + 1) * HEAD_DIM] = o[g * B:(g + 1) * B, :]
            lse_ref[0, :, g * LANES:(g + 1) * LANES] = jnp.broadcast_to(lse[g * B:(g + 1) * B, :], (B, LANES))

    specs = _swa_specs(n_kv, group, 0, n_q, v_first)
    q_blk = pl.BlockSpec((B, group * HEAD_DIM), lambda kv, n: (n, kv))
    return pl.pallas_call(
        body, name="swa_fwd", grid=(n_kv, S // B),
        in_specs=specs + [pl.BlockSpec(memory_space=pltpu.SMEM)],
        out_specs=[q_blk, pl.BlockSpec((1, B, group * LANES), lambda kv, n: (kv, n, 0))],
        out_shape=[jax.ShapeDtypeStruct((S, n_q * HEAD_DIM), F32), jax.ShapeDtypeStruct((n_kv, S, group * LANES), F32)],
        compiler_params=pltpu.CompilerParams(dimension_semantics=("parallel", "arbitrary")),
    )(rq, rq, rq, proj, proj, sinks)


def _swa_bwd(rq, proj, v_first, sinks, o, do, do_first, lse_b, n_q, n_kv):
    S = rq.shape[0]
    B = SWA_BLOCK
    group = n_q // n_kv
    scale = HEAD_DIM ** -0.5

    def body(q_ref, kp_ref, kc_ref, vp_ref, vc_ref, o_ref, do_ref, lse_ref, sink_ref,
             dq_ref, dk_ref, dv_ref, dsink_ref):
        kv, n = pl.program_id(0), pl.program_id(1)

        @pl.when(n == 0)
        def _():
            dk_ref[...] = jnp.zeros_like(dk_ref)
            dv_ref[...] = jnp.zeros_like(dv_ref)
            dsink_ref[...] = jnp.zeros_like(dsink_ref)

        qs, kcat, s = _swa_tile(q_ref, kp_ref, kc_ref, n, group, scale)
        sink = _swa_sink_col(sink_ref, kv, group)
        stack = lambda ref, w: jnp.concatenate([ref[:, g * w:(g + 1) * w] for g in range(group)], axis=0)
        lse = jnp.concatenate([lse_ref[0, :, g * LANES:g * LANES + 1] for g in range(group)], axis=0)
        do32 = stack(do_ref, HEAD_DIM)
        delta = jnp.sum(do32 * stack(o_ref, HEAD_DIM), axis=-1, keepdims=True)
        dov = do32.astype(BF16)
        p = jnp.exp(s - lse)
        vcat = jnp.concatenate([vp_ref[...], vc_ref[...]], axis=0)
        dp = lax.dot_general(dov, vcat, _NT, preferred_element_type=F32)
        ds = p * (dp - delta)
        dsb = ds.astype(BF16)
        dq = jnp.dot(dsb, kcat, preferred_element_type=F32) * scale
        for g in range(group):
            dq_ref[:, g * HEAD_DIM:(g + 1) * HEAD_DIM] = dq[g * B:(g + 1) * B, :].astype(BF16)
        dkcat = lax.dot_general(dsb, qs, _TN, preferred_element_type=F32) * scale
        dvcat = lax.dot_general(p.astype(BF16), dov, _TN, preferred_element_type=F32)
        prev0 = pl.multiple_of(jnp.maximum(n - 1, 0) * B, B)
        cur0 = pl.multiple_of(n * B, B)
        dk_ref[0, pl.ds(prev0, B), :] += dkcat[:B, :]
        dk_ref[0, pl.ds(cur0, B), :] += dkcat[B:, :]
        dv_ref[0, pl.ds(prev0, B), :] += dvcat[:B, :]
        dv_ref[0, pl.ds(cur0, B), :] += dvcat[B:, :]
        dsk = -jnp.exp(sink - lse) * delta
        lane = lax.broadcasted_iota(jnp.int32, (1, LANES), 1)
        row = jnp.zeros((1, LANES), F32)
        for g in range(group):
            row = row + jnp.where(lane == g, jnp.sum(dsk[g * B:(g + 1) * B, :]), 0.0)
        dsink_ref[0, 0:1, :] += row

    specs = _swa_specs(n_kv, group, 0, n_q, v_first)
    q_blk = pl.BlockSpec((B, group * HEAD_DIM), lambda kv, n: (n, kv))
    acc = pl.BlockSpec((1, S, HEAD_DIM), lambda kv, n: (kv, 0, 0))
    return pl.pallas_call(
        body, name="swa_bwd", grid=(n_kv, S // B),
        in_specs=specs + [q_blk, pl.BlockSpec((B, group * HEAD_DIM), lambda kv, n: (n, do_first + kv)),
                          pl.BlockSpec((1, B, group * LANES), lambda kv, n: (kv, n, 0)),
                          pl.BlockSpec(memory_space=pltpu.SMEM)],
        out_specs=[q_blk, acc, acc, pl.BlockSpec((1, 8, LANES), lambda kv, n: (kv, 0, 0))],
        out_shape=[jax.ShapeDtypeStruct((S, n_q * HEAD_DIM), BF16), jax.ShapeDtypeStruct((n_kv, S, HEAD_DIM), F32),
                   jax.ShapeDtypeStruct((n_kv, S, HEAD_DIM), F32), jax.ShapeDtypeStruct((n_kv, 8, LANES), F32)],
        compiler_params=pltpu.CompilerParams(dimension_semantics=("parallel", "arbitrary")),
    )(rq, rq, rq, proj, proj, o, do, lse_b, sinks)


def _adamw(w, g, m, v):
    m = ADAM_B1 * m + (1.0 - ADAM_B1) * g
    v = ADAM_B2 * v + (1.0 - ADAM_B2) * (g * g)
    m_hat = m / (1.0 - ADAM_B1 ** ADAM_STEP)
    v_hat = v / (1.0 - ADAM_B2 ** ADAM_STEP)
    delta = -ADAM_LR * (m_hat / (jnp.sqrt(v_hat) + ADAM_EPS) + ADAM_WD * w)
    return delta, m, v


def _mod_fwd(cond_in, w_mod, b_shard):
    R, D = cond_in.shape
    cols = w_mod.shape[1]
    tn = _fit(512, cols)

    def body(c_ref, w_ref, b_ref, o_ref):
        cv = c_ref[...]
        cond = (cv / (1.0 + jnp.exp(-cv))).astype(BF16)
        o_ref[...] = jnp.dot(cond, w_ref[...].astype(BF16), preferred_element_type=F32) + b_ref[...]

    return pl.pallas_call(
        body, name="mod_fwd", grid=(cols // tn,),
        in_specs=[pl.BlockSpec((R, D), lambda j: (0, 0)), pl.BlockSpec((D, tn), lambda j: (0, j)),
                  pl.BlockSpec((1, tn), lambda j: (0, j))],
        out_specs=pl.BlockSpec((R, tn), lambda j: (0, j)),
        out_shape=jax.ShapeDtypeStruct((R, cols), F32),
        compiler_params=pltpu.CompilerParams(dimension_semantics=("parallel",), vmem_limit_bytes=_vmem(3 * D * tn * 4)),
    )(cond_in, w_mod, b_shard)


def _mod_update(c_t, dmod, w, m, v):
    D, nb = c_t.shape
    cols = w.shape[1]
    tn = _fit(256, cols)

    def body(c_ref, d_ref, w_ref, m_ref, v_ref, g_ref, dl_ref, nm_ref, nv_ref):
        cv = c_ref[...]
        cond = cv / (1.0 + jnp.exp(-cv))
        g = jnp.zeros((D, tn), F32)
        for b in range(nb):
            g = g + cond[:, b:b + 1] * d_ref[b:b + 1, :]
        g_ref[...] = g
        dl_ref[...], nm_ref[...], nv_ref[...] = _adamw(w_ref[...], g, m_ref[...], v_ref[...])

    blk = pl.BlockSpec((D, tn), lambda j: (0, j))
    out = jax.ShapeDtypeStruct((D, cols), F32)
    return pl.pallas_call(
        body, name="mod_update", grid=(cols // tn,),
        in_specs=[pl.BlockSpec((D, nb), lambda j: (0, 0)), pl.BlockSpec((nb, tn), lambda j: (0, j)), blk, blk, blk],
        out_specs=[blk] * 4, out_shape=[out] * 4,
        compiler_params=pltpu.CompilerParams(dimension_semantics=("parallel",), vmem_limit_bytes=_vmem(18 * D * tn * 4)),
    )(c_t, dmod, w, m, v)


def _small_update(stacked, w, m, v):
    R, C = w.shape

    def body(s_ref, w_ref, m_ref, v_ref, g_ref, dl_ref, nm_ref, nv_ref):
        g = s_ref[0:R, :]
        for d in range(1, N_DEV):
            g = g + s_ref[d * R:(d + 1) * R, :]
        g_ref[...] = g
        dl_ref[...], nm_ref[...], nv_ref[...] = _adamw(w_ref[...], g, m_ref[...], v_ref[...])

    return pl.pallas_call(body, name="small_update", out_shape=[jax.ShapeDtypeStruct((R, C), F32)] * 4)(stacked, w, m, v)


def _place():
    return lax.axis_index("x"), lax.axis_index("y"), lax.axis_index("c")


def _allgather8(name, block):
    m_per, n = block.shape

    def body(x_ref, out_ref, token_ref, send_sems, recv_sems, local_sem):
        token_ref[...] = jnp.zeros_like(token_ref)
        x, y, c = _place()
        me, sibling = (x, y, c), (x, y, 1 - c)
        chips = [(1 - x, y), (x, 1 - y), (1 - x, 1 - y)]

        def rows(px, py, pc):
            return out_ref.at[pl.ds((4 * px + 2 * py + pc) * m_per, m_per), :]

        def copy(k, blk, to, src=None):
            return pltpu.make_async_remote_copy(
                src_ref=rows(*blk) if src is None else src, dst_ref=rows(*blk),
                send_sem=send_sems.at[k], recv_sem=recv_sems.at[k], device_id=to, device_id_type=MESH)

        mine = pltpu.make_async_copy(x_ref, rows(*me), local_sem)
        mine.start()
        first = [copy(0, me, sibling, src=x_ref)]
        first += [copy(1 + j, me, (*chip, c), src=x_ref) for j, chip in enumerate(chips)]
        for cp in first:
            cp.start()
        passed = [copy(4 + j, (*chip, c), sibling) for j, chip in enumerate(chips)]
        for j, chip in enumerate(chips):
            copy(1 + j, (*chip, c), me).wait_recv()
            passed[j].start()
        copy(0, sibling, me).wait_recv()
        for j, chip in enumerate(chips):
            copy(4 + j, (*chip, 1 - c), me).wait_recv()
        for cp in first + passed:
            cp.wait_send()
        mine.wait()

    vmem = pl.BlockSpec(memory_space=pltpu.VMEM)
    return pl.pallas_call(
        body, name=name,
        out_shape=[jax.ShapeDtypeStruct((N_DEV * m_per, n), block.dtype), jax.ShapeDtypeStruct((8, LANES), F32)],
        in_specs=[vmem], out_specs=[vmem, vmem],
        scratch_shapes=[pltpu.SemaphoreType.DMA((7,)), pltpu.SemaphoreType.DMA((7,)), pltpu.SemaphoreType.DMA],
    )(block)


_ANY = pl.BlockSpec(memory_space=pl.ANY)


def _half(ref, c, rows):
    return ref.at[pl.ds(c * (rows // 2), rows // 2), :]


_HBM = pl.BlockSpec(memory_space=pltpu.HBM)
_SEM = pl.BlockSpec(memory_space=pltpu.SEMAPHORE)
_EFFECT = pltpu.SideEffectType.DATAFLOW_SIDE_EFFECTING


def _ici_start(name, srcs, land_shapes, plan, per_source=3):
    ns, nl = len(srcs), len(land_shapes)
    n_copies = per_source * ns

    def body(*refs):
        src_refs, land_refs = refs[:ns], refs[ns:ns + nl]
        send_sems, recv_sems = refs[ns + nl], refs[ns + nl + 1]
        token = refs[-1]
        for n, (src, dst, peer, _) in enumerate(plan(src_refs, land_refs)):
            pltpu.make_async_remote_copy(src_ref=src, dst_ref=dst, send_sem=send_sems.at[n], recv_sem=recv_sems.at[n],
                                         device_id=peer, device_id_type=MESH).start()
        token[...] = jnp.zeros_like(token)

    lands = [lax.empty(s.shape, s.dtype) for s in land_shapes]
    out = pl.pallas_call(
        body, name=name,
        out_shape=(pltpu.SemaphoreType.DMA((n_copies,)), pltpu.SemaphoreType.DMA((n_copies,)),
                   *[pltpu.HBM(a.shape, a.dtype) for a in list(srcs) + lands], jax.ShapeDtypeStruct((8, LANES), F32)),
        in_specs=[_HBM] * (ns + nl),
        out_specs=(_SEM, _SEM, *[_HBM] * (ns + nl), pl.BlockSpec(memory_space=pltpu.VMEM)),
        input_output_aliases={n: 2 + n for n in range(ns + nl)},
        compiler_params=pltpu.CompilerParams(has_side_effects=_EFFECT),
    )(*[pltpu.with_memory_space_constraint(a, pltpu.HBM) for a in list(srcs) + lands])
    return out[0], out[1], list(out[2:2 + ns]), list(out[2 + ns:2 + ns + nl]), out[-1]


def _ici_wait(name, send_sems, recv_sems, srcs, lands, plan, after):
    ns, nl = len(srcs), len(lands)
    after = list(after) if isinstance(after, (list, tuple)) else [after]

    def body(*refs):
        src_refs, land_refs = refs[:ns], refs[ns:ns + nl]
        send_sems, recv_sems = refs[ns + nl], refs[ns + nl + 1]
        for n, (src, _, peer, mine) in enumerate(plan(src_refs, land_refs)):
            cp = pltpu.make_async_remote_copy(src_ref=src, dst_ref=mine, send_sem=send_sems.at[n],
                                              recv_sem=recv_sems.at[n], device_id=peer, device_id_type=MESH)
            cp.wait_send()
            cp.wait_recv()

    out = pl.pallas_call(
        body, name=name, out_shape=[pltpu.HBM(a.shape, a.dtype) for a in list(srcs) + list(lands)],
        in_specs=[_HBM] * (ns + nl) + [_SEM, _SEM] + [_ANY] * len(after), out_specs=[_HBM] * (ns + nl),
        input_output_aliases={n: n for n in range(ns + nl)},
        compiler_params=pltpu.CompilerParams(has_side_effects=_EFFECT),
    )(*srcs, *lands, send_sems, recv_sems, *after)
    return list(out[:ns]), list(out[ns:])


def _own_slab(name, chip, w, after):
    R, C = w.shape
    tr, tc = _tiles(R, C)

    def body(chip_ref, w_ref, after_ref, o_ref):
        o_ref[0] = w_ref[...].astype(BF16)

    grid_spec = pltpu.PrefetchScalarGridSpec(
        num_scalar_prefetch=1, grid=(R // tr, C // tc),
        in_specs=[pl.BlockSpec((tr, tc), lambda r, q, chip_ref: (r, q)),
                  pl.BlockSpec((8, LANES), lambda r, q, chip_ref: (0, 0))],
        out_specs=pl.BlockSpec((1, tr, tc), lambda r, q, chip_ref: (chip_ref[0], r, q)))
    return pl.pallas_call(
        body, name=name, grid_spec=grid_spec, out_shape=jax.ShapeDtypeStruct((N_CHIPS, R, C), BF16),
        compiler_params=pltpu.CompilerParams(dimension_semantics=("parallel", "parallel")),
    )(chip, w, after)


def _gather_plan(src_refs, land_refs):
    x, y, c = _place()
    copies = []
    for stack in src_refs:
        R = stack.shape[1]
        own = _half(stack.at[2 * x + y], c, R)
        for cx, cy in [(1 - x, y), (x, 1 - y), (1 - x, 1 - y)]:
            copies.append((own, own, (cx, cy, c), _half(stack.at[2 * cx + cy], c, R)))
    return copies


def _pass_plan(src_refs, land_refs):
    x, y, c = _place()
    copies = []
    for land in src_refs:
        R = land.shape[1]
        for cx, cy in [(1 - x, y), (x, 1 - y), (1 - x, 1 - y)]:
            slot = land.at[2 * cx + cy]
            copies.append((_half(slot, c, R), _half(slot, c, R), (x, y, 1 - c), _half(slot, 1 - c, R)))
    return copies


def _share_plan(src_refs, land_refs):
    x, y, c = _place()
    return [(h, land, (x, y, 1 - c), land) for h, land in zip(src_refs, land_refs)]


def _pass_to_sibling(name, lands):
    nw = len(lands)

    def body(*refs):
        ins, outs = refs[:nw], refs[nw:2 * nw]
        send_sems, recv_sems = refs[2 * nw:]
        x, y, c = _place()
        chips = [(1 - x, y), (x, 1 - y), (1 - x, 1 - y)]
        copies = []
        for k in range(nw):
            R = ins[k].shape[1]
            for j, (cx, cy) in enumerate(chips):
                cp = pltpu.make_async_remote_copy(
                    src_ref=_half(ins[k].at[2 * cx + cy], c, R), dst_ref=_half(outs[k].at[2 * cx + cy], c, R),
                    send_sem=send_sems.at[3 * k + j], recv_sem=recv_sems.at[3 * k + j],
                    device_id=(x, y, 1 - c), device_id_type=MESH)
                cp.start()
                copies.append(cp)
        for k in range(nw):
            R = ins[k].shape[1]
            for j, (cx, cy) in enumerate(chips):
                pltpu.make_async_remote_copy(
                    src_ref=_half(ins[k].at[2 * cx + cy], c, R), dst_ref=_half(outs[k].at[2 * cx + cy], 1 - c, R),
                    send_sem=send_sems.at[3 * k + j], recv_sem=recv_sems.at[3 * k + j],
                    device_id=(x, y, 1 - c), device_id_type=MESH).wait_recv()
        for cp in copies:
            cp.wait_send()

    return pl.pallas_call(
        body, name=name, out_shape=[jax.ShapeDtypeStruct(a.shape, a.dtype) for a in lands],
        in_specs=[_ANY] * nw, out_specs=[_ANY] * nw, input_output_aliases={k: k for k in range(nw)},
        scratch_shapes=[pltpu.SemaphoreType.DMA((3 * nw,)), pltpu.SemaphoreType.DMA((3 * nw,))],
    )(*lands)


def _tie(vec, token):
    return vec + token[0:1, 0:1]


ROW_ALIGN = 16
TILE_ELEMS = 512 * 1024


def _tiles(rows, cols):
    fits = [t for t in range(ROW_ALIGN, min(rows, 256) + 1, ROW_ALIGN) if rows % t == 0]
    tr = fits[-1] if fits and fits[-1] >= 64 else rows
    tc = cols
    while tr * tc > TILE_ELEMS and tc % (2 * LANES) == 0:
        tc //= 2
    return tr, tc


def _scatter_plan(src_refs, land_refs):
    x, y, c = _place()
    copies = []
    for p, land in zip(src_refs, land_refs):
        for j, (cx, cy) in enumerate([(1 - x, y), (x, 1 - y), (1 - x, 1 - y)]):
            copies.append((p.at[2 * cx + cy], land.at[j], (cx, cy, c), land.at[j]))
    return copies


def _chip_add(name, chip, sums, recv):
    _, H, C = sums.shape
    tr, tc = _tiles(H, C)

    def body(chip_ref, p_ref, r_ref, o_ref):
        total = p_ref[0].astype(F32)
        for j in range(3):
            total = total + r_ref[j].astype(F32)
        o_ref[...] = total

    grid_spec = pltpu.PrefetchScalarGridSpec(
        num_scalar_prefetch=1, grid=(H // tr, C // tc),
        in_specs=[pl.BlockSpec((1, tr, tc), lambda r, q, chip_ref: (chip_ref[0], r, q)),
                  pl.BlockSpec((3, tr, tc), lambda r, q, chip_ref: (0, r, q))],
        out_specs=pl.BlockSpec((tr, tc), lambda r, q, chip_ref: (r, q)))
    return pl.pallas_call(
        body, name=name, grid_spec=grid_spec, out_shape=jax.ShapeDtypeStruct((H, C), F32),
        compiler_params=pltpu.CompilerParams(dimension_semantics=("parallel", "parallel")),
    )(chip, sums, recv)


def _pair_share(name, halves):
    nw = len(halves)

    def body(*refs):
        hs, outs = refs[:nw], refs[nw:2 * nw]
        send_sems, recv_sems = refs[2 * nw:]
        x, y, c = _place()
        copies = []
        for k in range(nw):
            cp = pltpu.make_async_remote_copy(
                src_ref=hs[k], dst_ref=outs[k], send_sem=send_sems.at[k], recv_sem=recv_sems.at[k],
                device_id=(x, y, 1 - c), device_id_type=MESH)
            cp.start()
            copies.append(cp)
        for cp in copies:
            cp.wait()

    return pl.pallas_call(
        body, name=name,
        out_shape=[jax.ShapeDtypeStruct(h.shape, h.dtype) for h in halves],
        in_specs=[_ANY] * nw, out_specs=[_ANY] * nw,
        scratch_shapes=[pltpu.SemaphoreType.DMA((nw,)), pltpu.SemaphoreType.DMA((nw,))],
    )(*halves)


def _adam_halves(name, core, w, g_own, g_other, m, v):
    R, C = w.shape
    H = R // 2
    tr, tc = _tiles(H, C)
    nr, nc = H // tr, C // tc

    def body(core_ref, w_ref, go_ref, gr_ref, m_ref, v_ref, g_ref, dl_ref, nm_ref, nv_ref):
        own = (pl.program_id(0) // nr) == core_ref[0]
        g = jnp.where(own, go_ref[...], gr_ref[...])
        g_ref[...] = g
        dl_ref[...], nm_ref[...], nv_ref[...] = _adamw(w_ref[...], g, m_ref[...], v_ref[...])

    blk = pl.BlockSpec((tr, tc), lambda r, q, core_ref: (r, q))

    def half_spec(is_own):
        def index(r, q, core_ref):
            mine = ((r // nr) == core_ref[0]) == is_own
            done = is_own == (core_ref[0] == 0)
            return (jnp.where(mine, r % nr, jnp.where(done, nr - 1, 0)), jnp.where(mine, q, jnp.where(done, nc - 1, 0)))
        return pl.BlockSpec((tr, tc), index)
    out = jax.ShapeDtypeStruct((R, C), F32)
    grid_spec = pltpu.PrefetchScalarGridSpec(
        num_scalar_prefetch=1, grid=(R // tr, nc), in_specs=[blk, half_spec(True), half_spec(False), blk, blk],
        out_specs=[blk] * 4)
    return pl.pallas_call(
        body, name=name, grid_spec=grid_spec, out_shape=[out] * 4,
        compiler_params=pltpu.CompilerParams(dimension_semantics=("parallel", "parallel"),
                                             vmem_limit_bytes=_vmem(20 * tr * tc * 4)),
    )(core, w, g_own, g_other, m, v)


def kernel(x, c, w_mod, b_mod, g_pre_mix, g_post_mix, w_in, b_forget, swa_sinks, w_out, g_pre_mlp, g_post_mlp, w_up, w_down, loss_target, m_w_mod, m_b_mod, m_g_pre_mix, m_g_post_mix, m_w_in, m_b_forget, m_swa_sinks, m_w_out, m_g_pre_mlp, m_g_post_mlp, m_w_up, m_w_down, v_w_mod, v_b_mod, v_g_pre_mix, v_g_post_mix, v_w_in, v_b_forget, v_swa_sinks, v_w_out, v_g_pre_mlp, v_g_post_mlp, v_w_up, v_w_down):
    S, D = x.shape[1], x.shape[2]
    n_heads = D // HEAD_DIM
    n_fox = n_heads // 2
    n_swa = n_heads - n_fox
    n_kv = max(1, n_swa // 4)
    fox_w, swa_w, kv_w = n_fox * HEAD_DIM, n_swa * HEAD_DIM, n_kv * HEAD_DIM
    main_w = 3 * fox_w + swa_w + 2 * kv_w
    in_w = main_w + n_fox
    mod_cols = w_mod.shape[2]

    ax, ay, ac = _place()
    chip = 2 * ax + ay
    dev = 2 * chip + ac
    chip_arr = jnp.reshape(chip, (1,)).astype(jnp.int32)
    core_arr = jnp.reshape(ac, (1,)).astype(jnp.int32)

    x2, tgt = x[0], loss_target[0]

    c_all, _ = _allgather8("gather_c", c.reshape(8, D // 8))
    c_all = c_all.reshape(N_DEV, D)
    b_shard = lax.dynamic_slice_in_dim(b_mod, chip * mod_cols, mod_cols, axis=1)
    mod_shard = _mod_fwd(jnp.pad(c_all, ((0, 16 - N_DEV), (0, 0))), w_mod[0], b_shard)[:N_DEV]
    mod_all, token = _allgather8("gather_mod", mod_shard)
    mod_all = mod_all.reshape(N_CHIPS, 2, N_DEV, mod_cols)[:, 0]
    mod = lax.dynamic_index_in_dim(mod_all, dev, axis=1, keepdims=False).reshape(N_MOD, 1, D)
    sh_a, sc_a, gt_a, sh_m, sc_m, gt_m = [mod[n] for n in range(N_MOD)]

    in_rows = in_w // N_CHIPS
    in_rows_pad = -(-in_rows // (2 * LANES)) * (2 * LANES)
    slab_w = N_CHIPS * in_rows_pad

    def rows_of(a):
        return jnp.pad(a[0].T, ((0, in_rows_pad - in_rows), (0, 0)))

    def slab_cols(lo, hi):
        spans = []
        while lo < hi:
            s, r = divmod(lo, in_rows)
            n = min(hi - lo, in_rows - r)
            spans.append((s * in_rows_pad + r, s * in_rows_pad + r + n))
            lo += n
        return spans

    gate_lo = 3 * fox_w
    main_spans = slab_cols(0, gate_lo) + slab_cols(gate_lo + n_fox, in_w)
    (gate_first, gate_last), = slab_cols(gate_lo, gate_lo + n_fox)

    names = ["w_in", "w_out", "w_up", "w_down"]
    flights = {}
    for n, w in zip(names, [rows_of(w_in), w_out[0], w_up[0], w_down[0]]):
        flights[n] = _ici_start("gather_start_" + n, [_own_slab("own_slab_" + n, chip_arr, w, token)], [], _gather_plan)
        token = flights[n][4]
    sc_a = _tie(sc_a, token)

    def arrived(n, after):
        send, recv, stacks, _, _ = flights[n]
        stacks, _ = _ici_wait("gather_wait_" + n, send, recv, stacks, [], _gather_plan, after)
        return _ici_start("gather_pass_start_" + n, stacks, [], _pass_plan)

    def gathered(n, after, in_flight=None):
        if in_flight is None:
            send, recv, stacks, _, _ = flights[n]
            stacks, _ = _ici_wait("gather_wait_" + n, send, recv, stacks, [], _gather_plan, after)
            return _pass_to_sibling("gather_pass_" + n, stacks)[0]
        send, recv, stacks, _, _ = in_flight
        return _ici_wait("gather_pass_wait_" + n, send, recv, stacks, [], _pass_plan, after)[0][0]

    d_ff = N_CHIPS * w_up.shape[2]

    h = _pre_norm(x2, g_pre_mix, sc_a, sh_a)
    in_state = [rows_of(w_in)] + [rows_of(_tie(a, token)) for a in (m_w_in, v_w_in)]
    cos, sin_signed = _rope_tables(S)

    def pack(bm, gpm, gqm, gpl, gql, bf, sk):
        last = jnp.concatenate([bf, sk, jnp.zeros((1, D - n_fox - n_swa), F32)], axis=1)
        return jnp.concatenate([bm.reshape(N_MOD, D), gpm, gqm, gpl, gql, last, jnp.zeros((5, D), F32)], axis=0)

    small_state = [pack(b_mod, g_pre_mix, g_post_mix, g_pre_mlp, g_post_mlp, b_forget, swa_sinks),
                   pack(m_b_mod, m_g_pre_mix, m_g_post_mix, m_g_pre_mlp, m_g_post_mlp, m_b_forget, m_swa_sinks),
                   pack(v_b_mod, v_g_pre_mix, v_g_post_mix, v_g_pre_mlp, v_g_post_mlp, v_b_forget, v_swa_sinks)]
    ready = h[:8, :LANES].astype(F32) + cos[:8]
    w_slab_t = gathered("w_in", [ready] + in_state[1:] + small_state).reshape(slab_w, D)
    tm_p, tn_p = _fit(MM_TM, S), _fit(MM_TN if slab_w % MM_TN == 0 else MM_TN // 2, slab_w)
    win0 = gate_first // LANES * LANES
    win_j, win_off = divmod(win0, tn_p)
    assert win_off + 2 * LANES <= tn_p and gate_last - win0 <= 2 * LANES

    def proj_epilogue(acc, ex, outs):
        outs[0][...] = acc.astype(BF16)

        @pl.when(pl.program_id(1) == win_j)
        def _():
            outs[1][...] = acc[:, win_off:win_off + 2 * LANES]

    proj_slab, gate_win = _matmul(
        "in_proj", h, w_slab_t, "nt",
        [((S, slab_w), BF16, (tm_p, tn_p), lambda i, j: (i, j)), ((S, 2 * LANES), F32, (tm_p, 2 * LANES), lambda i, j: (i, 0))],
        proj_epilogue, tn=tn_p, revisits=True)
    proj = jnp.concatenate([proj_slab[:, lo:hi] for lo, hi in main_spans], axis=1)
    out_flight = arrived("w_out", proj_slab)
    fg = _tie(jnp.pad(gate_win[:, gate_first - win0:gate_last - win0], ((0, 0), (0, LANES - n_fox))), out_flight[4])
    b_pad = jnp.pad(b_forget, ((0, 0), (0, LANES - n_fox)))
    cum_row = _fox_gate_fwd(fg, b_pad)[:n_fox].reshape(n_fox, 1, S)
    fox_o, fox_lse = _fox_fwd(proj, cum_row, n_fox)

    rq = _rope("rope_fwd", proj, 3 * n_fox, n_swa + n_kv, cos, sin_signed)
    v_first = 3 * n_fox + n_swa + n_kv
    sinks = swa_sinks[0]
    swa_o, swa_lse = _swa_fwd(rq, proj, v_first, sinks, n_swa, n_kv)

    mixcat = jnp.concatenate([fox_o, swa_o], axis=1).astype(BF16)
    up_flight = arrived("w_up", mixcat)
    w_out_f = gathered("w_out", mixcat, out_flight).reshape(D, D)
    mix = _mm_plain("out_proj", mixcat, w_out_f, "nn", BF16, after=up_flight[4])
    x1, h2 = _post_mix(x2, mix, g_post_mix, gt_a, g_pre_mlp, sc_m, sh_m)
    w_up_f = jnp.transpose(gathered("w_up", h2, up_flight), (1, 0, 2)).reshape(D, d_ff)

    tm_u, tn_u = _fit(MM_TM, S), _fit(MM_TN, d_ff)

    def up_epilogue(acc, ex, outs):
        outs[0][...] = acc.astype(BF16)
        r = jnp.maximum(acc, 0.0)
        outs[1][...] = (r * r).astype(BF16)

    ublk = ((S, d_ff), BF16, (tm_u, tn_u), lambda i, j: (i, j))
    u, a = _matmul("mlp_up", h2, w_up_f, "nn", [ublk, ublk], up_epilogue)
    w_down_f = gathered("w_down", a).reshape(d_ff, D)
    y = _mm_plain("mlp_down", a, w_down_f, "nn", BF16)

    dy, dout, loss_part, acc_mlp_post = _loss_and_post_mlp_bwd(x1, y, tgt, g_post_mlp, gt_m)

    def du_epilogue(acc, ex, outs):
        outs[0][...] = (acc * (2.0 * jnp.maximum(ex[0][...].astype(F32), 0.0))).astype(BF16)

    du = _matmul("mlp_down_bwd", dy, w_down_f, "nt", [ublk], du_epilogue,
                 extras=[(u, (tm_u, tn_u), lambda i, j: (i, j))])[0]
    def pair_send(tag, part):
        return _ici_start("grad_pair_start_" + tag, [part], [jax.ShapeDtypeStruct(part.shape, BF16)], _share_plan,
                          per_source=1)

    def pair_recv(tag, flight, after):
        send, recv, srcs, lands, _ = flight
        return _ici_wait("grad_pair_wait_" + tag, send, recv, srcs, lands, _share_plan, after)[1][0]

    def scatter_start(tag, sums):
        return _ici_start("grad_scatter_start_" + tag, sums,
                          [jax.ShapeDtypeStruct((3,) + p.shape[1:], BF16) for p in sums], _scatter_plan)

    def scatter_finish(tag, flight, after):
        send, recv, srcs, lands, _ = flight
        sums, received = _ici_wait("grad_scatter_wait_" + tag, send, recv, srcs, lands, _scatter_plan, after)
        return [_chip_add("chip_add_%s_%d" % (tag, k), chip_arr, p, r) for k, (p, r) in enumerate(zip(sums, received))]

    tm_g = _fit(MM_TM, D // 2)
    pair_down = pair_send("down", _grad_half("grad_w_down_a", core_arr, a, dy, N_CHIPS, 1, tm_g, True))
    pair_up = pair_send("up", _grad_half("grad_w_up_a", core_arr, h2, du, 1, N_CHIPS, tm_g, True, after=pair_down[4]))
    sum_down = _grad_half("grad_w_down_b", core_arr, a, dy, N_CHIPS, 1, tm_g, False,
                          recv=pair_recv("down", pair_down, pair_up[4]))
    sum_up = _grad_half("grad_w_up_b", core_arr, h2, du, 1, N_CHIPS, tm_g, False, recv=pair_recv("up", pair_up, sum_down))
    flight_mlp = scatter_start("mlp", [sum_up, sum_down])
    dh2 = _mm_plain("mlp_up_bwd", du, w_up_f, "nt", BF16, after=flight_mlp[4])
    dx1, dmix, acc_mid = _pre_mlp_and_post_mix_bwd(dh2, x1, dout, mix, _tie(g_pre_mlp, flight_mlp[4]), sc_m,
                                                   g_post_mix, gt_a)

    dmixcat = _mm_plain("out_proj_bwd", dmix, w_out_f, "nt", F32)

    fdq, fdk, fdv, dcum_row, dcum_q = _fox_bwd(proj, fox_o, dmixcat, fox_lse, cum_row, n_fox)
    dcum_k = jnp.pad(dcum_row.reshape(n_fox, S), ((0, LANES - n_fox), (0, 0)))
    dfg, db_forget = _fox_gate_bwd(dcum_k, dcum_q, fg, b_pad)

    group_w = (n_swa // n_kv) * HEAD_DIM
    sdq, sdk, sdv, dsink = _swa_bwd(rq, proj, v_first, sinks, swa_o, dmixcat, fox_w // group_w, swa_lse, n_swa, n_kv)
    drq = jnp.concatenate([sdq, jnp.transpose(sdk, (1, 0, 2)).reshape(S, kv_w).astype(BF16)], axis=1)
    d_sq_sk = _rope("rope_bwd", drq, 0, n_swa + n_kv, cos, -sin_signed)
    dsv = jnp.transpose(sdv, (1, 0, 2)).reshape(S, kv_w).astype(BF16)
    dproj = jnp.concatenate([fdq, fdk, fdv, d_sq_sk, dsv], axis=1)

    pieces = []
    for s in range(N_CHIPS):
        lo, hi = s * in_rows, (s + 1) * in_rows
        for src, first, last, shift in [(dproj, 0, gate_lo, 0), (dfg, gate_lo, gate_lo + n_fox, gate_lo),
                                        (dproj, gate_lo + n_fox, in_w, n_fox)]:
            if max(lo, first) < min(hi, last):
                pieces.append(src[:, max(lo, first) - shift:min(hi, last) - shift])
        pieces.append(jnp.zeros((S, in_rows_pad - in_rows), BF16))
    dproj_slab = jnp.concatenate(pieces, axis=1)

    tm_in, tm_out = in_rows_pad // 2, D // (2 * N_CHIPS)
    pair_in = pair_send("in", _grad_half("grad_w_in_a", core_arr, dproj_slab, h, N_CHIPS, 1, tm_in, True))
    pair_out = pair_send("out", _grad_half("grad_w_out_a", core_arr, mixcat, dmix, N_CHIPS, 1, tm_out, True,
                                           after=pair_in[4]))
    sum_in = _grad_half("grad_w_in_b", core_arr, dproj_slab, h, N_CHIPS, 1, tm_in, False,
                        recv=pair_recv("in", pair_in, pair_out[4]))
    dh = _mm_plain("in_proj_bwd", dproj_slab, w_slab_t, "nn", BF16, tk=slab_w // 2)
    grad_x, acc_pre = _pre_mix_bwd(dh, x2, dx1, g_pre_mix, sc_a)

    zero_row = jnp.zeros((1, D), F32)
    tail = jnp.concatenate([db_forget[0:1, :n_fox], dsink[:, 0, :n_swa // n_kv].reshape(1, n_swa),
                            loss_part[0:1, 0:1], jnp.zeros((1, D - n_fox - n_swa - 1), F32)], axis=1)
    partial = jnp.concatenate([
        acc_pre[0:1], acc_pre[1:2], acc_mid[3:4], acc_mid[0:1], acc_mid[1:2], acc_mlp_post[0:1],
        acc_pre[2:3], acc_mid[4:5], acc_mid[2:3], acc_mlp_post[1:2], tail] + [zero_row] * 5, axis=0)
    gathered_small, token = _allgather8("gather_small_grads", partial)

    sum_out = _grad_half("grad_w_out_b", core_arr, mixcat, dmix, N_CHIPS, 1, tm_out, False,
                         recv=pair_recv("out", pair_out, token))
    flight_mix = scatter_start("mix", [sum_in, sum_out])
    halves_mlp = scatter_finish("mlp", flight_mlp, flight_mix[4])
    share_mlp = _ici_start("grad_share_start_mlp", halves_mlp,
                           [jax.ShapeDtypeStruct(hv.shape, F32) for hv in halves_mlp], _share_plan, per_source=1)

    def unpack(p):
        return {"b_mod": p[0:N_MOD].reshape(1, N_MOD * D), "g_pre_mix": p[6:7], "g_post_mix": p[7:8],
                "g_pre_mlp": p[8:9], "g_post_mlp": p[9:10], "b_forget": p[10:11, :n_fox],
                "swa_sinks": p[10:11, n_fox:n_fox + n_swa]}

    small_out = _small_update(gathered_small, _tie(small_state[0], share_mlp[4]), small_state[1], small_state[2])
    g_small, d_small, m_small, v_small = [unpack(p) for p in small_out]
    loss = small_out[0][N_MOD + 4, n_fox + n_swa]

    dmod_all = gathered_small.reshape(N_DEV, 16, D)[:, :N_MOD].reshape(N_DEV, N_MOD * D)
    dmod_shard = _tie(lax.dynamic_slice_in_dim(dmod_all, chip * mod_cols, mod_cols, axis=1), share_mlp[4])
    g_w_mod, d_w_mod, nm_w_mod, nv_w_mod = _mod_update(c_all.T, dmod_shard, w_mod[0], m_w_mod[0], v_w_mod[0])
    send, recv, halves_mlp, lands, _ = share_mlp
    halves_mlp, others_mlp = _ici_wait("grad_share_wait_mlp", send, recv, halves_mlp, lands, _share_plan,
                                       d_w_mod[:8, :LANES] + small_out[1][:8, :LANES])

    grads = dict(g_small, w_mod=g_w_mod[None])
    deltas = dict(d_small, w_mod=d_w_mod[None])
    new_m = dict(m_small, w_mod=nm_w_mod[None])
    new_v = dict(v_small, w_mod=nv_w_mod[None])
    weights = {"w_in": (w_in, m_w_in, v_w_in), "w_out": (w_out, m_w_out, v_w_out), "w_up": (w_up, m_w_up, v_w_up),
               "w_down": (w_down, m_w_down, v_w_down)}

    def big_update(n, own, other):
        transposed = n == "w_in"
        w, m, v = in_state if transposed else [a[0] for a in weights[n]]
        outs = _adam_halves("adam_" + n, core_arr, w, own, other, m, v)
        if transposed:
            outs = [o[:in_rows].T for o in outs]
        grads[n], deltas[n], new_m[n], new_v[n] = [o[None] for o in outs]

    big_update("w_up", halves_mlp[0], others_mlp[0])
    big_update("w_down", halves_mlp[1], others_mlp[1])
    ran = deltas["w_down"][0, :8, :LANES] + deltas["w_up"][0, :8, :LANES] + d_w_mod[:8, :LANES]
    halves_mix = scatter_finish("mix", flight_mix, ran)
    others_mix = _pair_share("grad_pair_share_mix", halves_mix)
    big_update("w_in", halves_mix[0], others_mix[0])
    big_update("w_out", halves_mix[1], others_mix[1])

    order = ["w_mod", "b_mod", "g_pre_mix", "g_post_mix", "w_in", "b_forget", "swa_sinks", "w_out", "g_pre_mlp",
             "g_post_mlp", "w_up", "w_down"]
    return (loss, grad_x[None], *[grads[n] for n in order], *[deltas[n] for n in order],
            *[new_m[n] for n in order], *[new_v[n] for n in order])
```

```python
import jax
import jax.numpy as jnp
from jax import lax
from jax.experimental import pallas as pl
from jax.experimental.pallas import tpu as pltpu

F32 = jnp.float32
BF16 = jnp.bfloat16
MESH = pl.DeviceIdType.MESH

HEAD_DIM = 128
SWA_BLOCK = 128
ROPE_THETA = 10000.0
NORM_EPS = 1e-6
NEG = -1e30
N_MOD = 6
ADAM_LR = 0.001
ADAM_B1 = 0.9
ADAM_B2 = 0.999
ADAM_EPS = 1e-08
ADAM_WD = 0.01
ADAM_STEP = 10
N_CHIPS = 4
N_DEV = 8
LANES = 128
VMEM_CAP = 60 * 1024 * 1024

_NN = (((1,), (0,)), ((), ()))
_NT = (((1,), (1,)), ((), ()))
_TN = (((0,), (0,)), ((), ()))


def _vmem(nbytes):
    return int(min(VMEM_CAP, nbytes * 5 // 4 + (4 << 20)))


def _nbytes(shape, dtype):
    n = 1
    for s in shape:
        n *= s
    return n * jnp.dtype(dtype).itemsize


def _fit(t, n):
    t = min(t, n)
    assert n % t == 0, (t, n)
    return t


MM_TM, MM_TN, MM_TK = 512, 1024, 2048


def _matmul(name, a, b, mode, out_defs, epilogue, extras=(), tm=MM_TM, tn=MM_TN, tk=MM_TK, revisits=False,
            row_sel=None):
    if mode == "nn":
        (M, K), (K2, N) = a.shape, b.shape
    elif mode == "nt":
        (M, K), (N, K2) = a.shape, b.shape
    else:
        (K, M), (K2, N) = a.shape, b.shape
    assert K == K2, (a.shape, b.shape, mode)
    tm, tn, tk = _fit(tm, M), _fit(tn, N), _fit(tk, K)
    nk = K // tk
    dims = {"nn": _NN, "nt": _NT, "tn": _TN}[mode]
    if row_sel is None:
        grid_m, a_row = M // tm, lambda i, *sel: i
    else:
        grid_m, a_row = row_sel[2], lambda i, *sel: row_sel[1](i, sel[0])
    a_spec = (pl.BlockSpec((tk, tm), lambda i, j, k, *sel: (k, a_row(i, *sel))) if mode == "tn"
              else pl.BlockSpec((tm, tk), lambda i, j, k, *sel: (a_row(i, *sel), k)))
    b_spec = (pl.BlockSpec((tn, tk), lambda i, j, k, *sel: (j, k)) if mode == "nt"
              else pl.BlockSpec((tk, tn), lambda i, j, k, *sel: (k, j)))
    n_ex, n_out = len(extras), len(out_defs)

    def body(*refs):
        if row_sel is not None:
            refs = refs[1:]
        a_ref, b_ref = refs[0], refs[1]
        ex = refs[2:2 + n_ex]
        outs = refs[2 + n_ex:2 + n_ex + n_out]
        prod = lax.dot_general(a_ref[...], b_ref[...], dims, preferred_element_type=F32)
        if nk == 1:
            epilogue(prod, ex, outs)
        else:
            acc_ref = refs[-1]
            k = pl.program_id(2)

            @pl.when(k == 0)
            def _():
                acc_ref[...] = prod

            @pl.when(k > 0)
            def _():
                acc_ref[...] += prod

            @pl.when(k == nk - 1)
            def _():
                epilogue(acc_ref[...], ex, outs)

    def wrap(f):
        return lambda i, j, k, *sel: f(i, j)

    in_specs = [a_spec, b_spec] + [pl.BlockSpec(blk, wrap(f)) for _, blk, f in extras]
    out_specs = [pl.BlockSpec(blk, wrap(f)) for _, _, blk, f in out_defs]
    out_shape = [jax.ShapeDtypeStruct(s, d) for s, d, _, _ in out_defs]
    need = 2 * (tm * tk + tk * tn) * a.dtype.itemsize + 3 * tm * tn * 4
    need += sum(2 * _nbytes(blk, arr.dtype) for arr, blk, _ in extras)
    need += sum(2 * _nbytes(blk, d) for _, d, blk, _ in out_defs)
    grid = (grid_m, N // tn, nk)
    scratch = [pltpu.VMEM((tm, tn), F32)] if nk > 1 else []
    params = pltpu.CompilerParams(
        dimension_semantics=("parallel", "arbitrary" if revisits else "parallel", "arbitrary"),
        vmem_limit_bytes=_vmem(need))
    operands = (a, b, *[arr for arr, _, _ in extras])
    if row_sel is None:
        return pl.pallas_call(body, name=name, grid=grid, in_specs=in_specs, out_specs=out_specs, out_shape=out_shape,
                              scratch_shapes=scratch, compiler_params=params)(*operands)
    grid_spec = pltpu.PrefetchScalarGridSpec(num_scalar_prefetch=1, grid=grid, in_specs=in_specs, out_specs=out_specs,
                                             scratch_shapes=scratch)
    return pl.pallas_call(body, name=name, grid_spec=grid_spec, out_shape=out_shape,
                          compiler_params=params)(row_sel[0], *operands)


def _grad_half(name, core, a, b, row_slabs, col_slabs, tm, other, recv=None, after=None):
    (_, M), (_, N) = a.shape, b.shape
    H = M // (2 * row_slabs)
    nh = H // tm
    tn = _fit(MM_TN, N // col_slabs)
    per = N // col_slabs // tn

    def a_block(i, core_ref):
        half = (1 - core_ref[0]) if other else core_ref[0]
        return (i // nh) * (2 * nh) + half * nh + i % nh

    def out_index(i, j):
        return (j // per, i, j % per) if col_slabs > 1 else (i // nh, i % nh, j)

    slabs = max(row_slabs, col_slabs)
    out_def = ((slabs, H, N // col_slabs), BF16, (1, tm, tn), out_index)

    def epilogue(acc, ex, outs):
        outs[0][0] = (acc if recv is None else acc + ex[0][0].astype(F32)).astype(BF16)

    extras = ([] if recv is None else [(recv, (1, tm, tn), out_index)]) + ([] if after is None else [_behind(after)])
    return _matmul(name, a, b, "tn", [out_def], epilogue, extras=extras, tm=tm, tn=tn,
                   row_sel=(core, a_block, row_slabs * nh))[0]


def _behind(token):
    return (token, (8, LANES), lambda i, j: (0, 0))


def _mm_plain(name, a, b, mode, out_dtype, after=None, **tiles):
    if mode == "nn":
        M, N = a.shape[0], b.shape[1]
    elif mode == "nt":
        M, N = a.shape[0], b.shape[0]
    else:
        M, N = a.shape[1], b.shape[1]
    tm, tn = _fit(tiles.get("tm", MM_TM), M), _fit(tiles.get("tn", MM_TN), N)

    def epi(acc, ex, outs):
        outs[0][...] = acc.astype(out_dtype)

    return _matmul(name, a, b, mode, [((M, N), out_dtype, (tm, tn), lambda i, j: (i, j))], epi,
                   extras=[] if after is None else [_behind(after)], **tiles)[0]


def _rstd(v):
    return lax.rsqrt(jnp.mean(v * v, axis=-1, keepdims=True) + NORM_EPS)


def _row_call(name, body, row_ins, vec_ins, row_outs, acc_outs, S, D, tr):
    tr = _fit(tr, S)
    row_spec = pl.BlockSpec((tr, D), lambda r: (r, 0))
    vec_spec = pl.BlockSpec((1, D), lambda r: (0, 0))
    in_specs = [row_spec] * len(row_ins) + [vec_spec] * len(vec_ins)
    out_specs = [row_spec] * len(row_outs) + [pl.BlockSpec(shp, lambda r: (0, 0)) for shp in acc_outs]
    out_shape = [jax.ShapeDtypeStruct((S, D), d) for d in row_outs] + [jax.ShapeDtypeStruct(shp, F32) for shp in acc_outs]
    need = sum(2 * tr * D * a.dtype.itemsize for a in row_ins) + sum(2 * tr * D * jnp.dtype(d).itemsize for d in row_outs)
    need += 8 * tr * D * 4
    return pl.pallas_call(
        body, name=name, grid=(S // tr,), in_specs=in_specs, out_specs=out_specs, out_shape=out_shape,
        compiler_params=pltpu.CompilerParams(dimension_semantics=("arbitrary",), vmem_limit_bytes=_vmem(need)),
    )(*row_ins, *vec_ins)


def _acc_rows(ref, rows):
    @pl.when(pl.program_id(0) == 0)
    def _():
        ref[...] = jnp.zeros_like(ref)
    for n, r in enumerate(rows):
        ref[n:n + 1, :] += r


def _pre_norm(x, g, sc, sh):
    S, D = x.shape

    def body(x_ref, g_ref, sc_ref, sh_ref, h_ref):
        xv = x_ref[...]
        xn = xv * _rstd(xv)
        h_ref[...] = (xn * g_ref[...] * (1.0 + sc_ref[...]) + sh_ref[...]).astype(BF16)

    return _row_call("pre_norm_mix", body, [x], [g, sc, sh], [BF16], [], S, D, 256)[0]


def _post_mix(x, mix, g_post, gt, g_pre, sc, sh):
    S, D = x.shape

    def body(x_ref, mix_ref, gp_ref, gt_ref, g2_ref, sc_ref, sh_ref, x1_ref, h2_ref):
        mv = mix_ref[...].astype(F32)
        x1 = x_ref[...] + gt_ref[...] * (mv * _rstd(mv) * gp_ref[...])
        x1_ref[...] = x1
        h2_ref[...] = (x1 * _rstd(x1) * g2_ref[...] * (1.0 + sc_ref[...]) + sh_ref[...]).astype(BF16)

    return _row_call("post_mix_pre_mlp", body, [x, mix], [g_post, gt, g_pre, sc, sh], [F32, BF16], [], S, D, 256)


def _loss_and_post_mlp_bwd(x1, y, target, g_post, gt):
    S, D = x1.shape

    def body(x1_ref, y_ref, t_ref, g_ref, gt_ref, dy_ref, dout_ref, loss_ref, acc_ref):
        yv = y_ref[...].astype(F32)
        r = _rstd(yv)
        yh = yv * r
        n = yh * g_ref[...]
        diff = x1_ref[...] + gt_ref[...] * n - t_ref[...]
        dout = diff * (1.0 / D)
        dout_ref[...] = dout
        dn = dout * gt_ref[...]
        dyh = dn * g_ref[...]
        dy_ref[...] = (r * (dyh - yh * jnp.mean(dyh * yh, axis=-1, keepdims=True))).astype(BF16)
        _acc_rows(acc_ref, [jnp.sum(dout * n, axis=0, keepdims=True), jnp.sum(dn * yh, axis=0, keepdims=True)])

        @pl.when(pl.program_id(0) == 0)
        def _():
            loss_ref[...] = jnp.zeros_like(loss_ref)
        loss_ref[...] += jnp.full(loss_ref.shape, (0.5 / D) * jnp.sum(diff * diff), F32)

    return _row_call("loss_post_mlp_bwd", body, [x1, y, target], [g_post, gt], [BF16, F32],
                     [(8, LANES), (8, D)], S, D, 128)


def _pre_mlp_and_post_mix_bwd(dh2, x1, dout, mix, g_pre, sc, g_post, gt):
    S, D = x1.shape

    def body(dh_ref, x1_ref, dout_ref, mix_ref, g_ref, sc_ref, gp_ref, gt_ref, dx1_ref, dmix_ref, acc_ref):
        dh = dh_ref[...].astype(F32)
        x1v = x1_ref[...]
        r3 = _rstd(x1v)
        xn = x1v * r3
        dxn = dh * (1.0 + sc_ref[...]) * g_ref[...]
        dx1 = dout_ref[...] + r3 * (dxn - xn * jnp.mean(dxn * xn, axis=-1, keepdims=True))
        dx1_ref[...] = dx1
        mv = mix_ref[...].astype(F32)
        r2 = _rstd(mv)
        mh = mv * r2
        dn = dx1 * gt_ref[...]
        dmh = dn * gp_ref[...]
        dmix_ref[...] = (r2 * (dmh - mh * jnp.mean(dmh * mh, axis=-1, keepdims=True))).astype(BF16)
        _acc_rows(acc_ref, [
            jnp.sum(dh, axis=0, keepdims=True),
            jnp.sum(dh * xn * g_ref[...], axis=0, keepdims=True),
            jnp.sum(dh * (1.0 + sc_ref[...]) * xn, axis=0, keepdims=True),
            jnp.sum(dx1 * mh * gp_ref[...], axis=0, keepdims=True),
            jnp.sum(dn * mh, axis=0, keepdims=True)])

    return _row_call("pre_mlp_post_mix_bwd", body, [dh2, x1, dout, mix], [g_pre, sc, g_post, gt], [F32, BF16],
                     [(8, D)], S, D, 128)


def _pre_mix_bwd(dh, x, dx1, g_pre, sc):
    S, D = x.shape

    def body(dh_ref, x_ref, dx1_ref, g_ref, sc_ref, gx_ref, acc_ref):
        dhv = dh_ref[...].astype(F32)
        xv = x_ref[...]
        r = _rstd(xv)
        xn = xv * r
        dxn = dhv * (1.0 + sc_ref[...]) * g_ref[...]
        gx_ref[...] = dx1_ref[...] + r * (dxn - xn * jnp.mean(dxn * xn, axis=-1, keepdims=True))
        _acc_rows(acc_ref, [
            jnp.sum(dhv, axis=0, keepdims=True),
            jnp.sum(dhv * xn * g_ref[...], axis=0, keepdims=True),
            jnp.sum(dhv * (1.0 + sc_ref[...]) * xn, axis=0, keepdims=True)])

    return _row_call("pre_mix_bwd", body, [dh, x, dx1], [g_pre, sc], [F32], [(8, D)], S, D, 128)


CUM_BLOCK = 256


def _tri(n, upper):
    r = lax.broadcasted_iota(jnp.int32, (n, n), 0)
    c = lax.broadcasted_iota(jnp.int32, (n, n), 1)
    return ((c >= r) if upper else (c <= r)).astype(F32)


def _fox_gate_fwd(fg, b_pad):
    S = fg.shape[0]
    cb = _fit(CUM_BLOCK, S)

    def body(fg_ref, b_ref, cumt_ref, cum_ref):
        low = _tri(cb, False)
        carry = jnp.zeros((1, LANES), F32)
        for n in range(S // cb):
            z = fg_ref[n * cb:(n + 1) * cb, :] + b_ref[...]
            logf = jnp.minimum(z, 0.0) - jnp.log(1.0 + jnp.exp(-jnp.abs(z)))
            blk = jnp.dot(low, logf, precision=lax.Precision.HIGHEST, preferred_element_type=F32) + carry
            cum_ref[n * cb:(n + 1) * cb, :] = blk
            carry = blk[cb - 1:cb, :]
        cumt_ref[...] = cum_ref[...].T

    return pl.pallas_call(
        body, name="fox_gate_fwd", out_shape=jax.ShapeDtypeStruct((LANES, S), F32),
        scratch_shapes=[pltpu.VMEM((S, LANES), F32)],
        compiler_params=pltpu.CompilerParams(vmem_limit_bytes=_vmem(6 * S * LANES * 4)),
    )(fg, b_pad)


def _fox_gate_bwd(dcum_k, dcum_q, fg, b_pad):
    S = fg.shape[0]
    n_fox = dcum_q.shape[0]
    cb = _fit(CUM_BLOCK, S)

    def body(dk_ref, dq_ref, fg_ref, b_ref, dfg_ref, db_ref, dc_ref):
        lane = lax.broadcasted_iota(jnp.int32, (S, LANES), 1)
        dc = dk_ref[...].T
        for h in range(n_fox):
            dc = dc + jnp.where(lane == h, dq_ref[h], 0.0)
        dc_ref[...] = dc
        up = _tri(cb, True)
        carry = jnp.zeros((1, LANES), F32)
        db = jnp.zeros((1, LANES), F32)
        for n in reversed(range(S // cb)):
            blk = jnp.dot(up, dc_ref[n * cb:(n + 1) * cb, :], precision=lax.Precision.HIGHEST,
                          preferred_element_type=F32) + carry
            carry = blk[0:1, :]
            z = fg_ref[n * cb:(n + 1) * cb, :] + b_ref[...]
            dfg = blk * (1.0 / (1.0 + jnp.exp(z)))
            dfg_ref[n * cb:(n + 1) * cb, :] = dfg.astype(BF16)
            db = db + jnp.sum(dfg, axis=0, keepdims=True)
        db_ref[...] = jnp.broadcast_to(db, db_ref.shape)

    return pl.pallas_call(
        body, name="fox_gate_bwd",
        out_shape=[jax.ShapeDtypeStruct((S, LANES), BF16), jax.ShapeDtypeStruct((8, LANES), F32)],
        scratch_shapes=[pltpu.VMEM((S, LANES), F32)],
        compiler_params=pltpu.CompilerParams(vmem_limit_bytes=_vmem((8 + 2 * n_fox) * S * LANES * 4)),
    )(dcum_k, dcum_q, fg, b_pad)


FOX_TILE = 512


LOG2E = 1.4426950408889634


def _fox_scores(q, k, ck2, masked, t):
    s = lax.dot_general(q, k, _NT, preferred_element_type=F32) * (HEAD_DIM ** -0.5 * LOG2E) - ck2
    if masked:
        row = lax.broadcasted_iota(jnp.int32, (t, t), 0)
        col = lax.broadcasted_iota(jnp.int32, (t, t), 1)
        s = jnp.where(col <= row, s, NEG)
    return s


def _fox_fwd(proj, cum_row, n_fox):
    S = proj.shape[0]
    t = _fit(FOX_TILE, S)
    nq = S // t

    def body(q_ref, k_ref, v_ref, ck_ref, o_ref, lse_ref):
        def q_block(qi, _):
            q0 = pl.multiple_of(qi * t, t)
            q = q_ref[pl.ds(q0, t), :]

            def kv_block(j, carry, masked):
                m, l, acc = carry
                k0 = pl.multiple_of(j * t, t)
                s = _fox_scores(q, k_ref[pl.ds(k0, t), :], ck_ref[0, :, pl.ds(k0, t)] * LOG2E, masked, t)
                m_new = jnp.maximum(m, jnp.max(s, axis=-1, keepdims=True))
                alpha = jnp.exp2(m - m_new)
                p = jnp.exp2(s - m_new)
                l = alpha * l + jnp.sum(p, axis=-1, keepdims=True)
                acc = alpha * acc + jnp.dot(p.astype(BF16), v_ref[pl.ds(k0, t), :], preferred_element_type=F32)
                return m_new, l, acc

            init = (jnp.full((t, 1), NEG, F32), jnp.zeros((t, 1), F32), jnp.zeros((t, HEAD_DIM), F32))
            carry = lax.fori_loop(0, qi, lambda j, cr: kv_block(j, cr, False), init)
            m, l, acc = kv_block(qi, carry, True)
            o_ref[pl.ds(q0, t), :] = acc / l
            lse_ref[0, pl.ds(q0, t), :] = jnp.broadcast_to(m + jnp.log(l) * LOG2E, (t, LANES))
            return 0

        lax.fori_loop(0, nq, q_block, 0)

    col = lambda off: pl.BlockSpec((S, HEAD_DIM), lambda h: (0, off + h))
    per_head = pl.BlockSpec((1, S, LANES), lambda h: (h, 0, 0))
    return pl.pallas_call(
        body, name="fox_fwd", grid=(n_fox,),
        in_specs=[col(0), col(n_fox), col(2 * n_fox), pl.BlockSpec((1, 1, S), lambda h: (h, 0, 0))],
        out_specs=[pl.BlockSpec((S, HEAD_DIM), lambda h: (0, h)), per_head],
        out_shape=[jax.ShapeDtypeStruct((S, n_fox * HEAD_DIM), F32), jax.ShapeDtypeStruct((n_fox, S, LANES), F32)],
        compiler_params=pltpu.CompilerParams(dimension_semantics=("parallel",),
                                             vmem_limit_bytes=_vmem(16 * S * HEAD_DIM * 4 + 12 * t * t * 4)),
    )(proj, proj, proj, cum_row)


def _fox_bwd(proj, o, do, lse_b, cum_row, n_fox):
    S = proj.shape[0]
    t = _fit(FOX_TILE, S)
    nq = S // t
    scale = HEAD_DIM ** -0.5

    def body(q_ref, k_ref, v_ref, o_ref, do_ref, lse_ref, ck_ref, dq_ref, dk_ref, dv_ref, dc_ref, dcq_ref,
             dq_acc, delta_ref):
        dq_acc[...] = jnp.zeros_like(dq_acc)
        dcq_ref[...] = jnp.zeros_like(dcq_ref)

        def delta_block(qi, _):
            q0 = pl.multiple_of(qi * t, t)
            d = jnp.sum(do_ref[pl.ds(q0, t), :] * o_ref[pl.ds(q0, t), :], axis=-1, keepdims=True)
            delta_ref[pl.ds(q0, t), :] = jnp.broadcast_to(d, (t, LANES))
            return 0

        lax.fori_loop(0, nq, delta_block, 0)

        def kv_block(j, _):
            k0 = pl.multiple_of(j * t, t)
            k = k_ref[pl.ds(k0, t), :]
            v = v_ref[pl.ds(k0, t), :]
            ck2 = ck_ref[0, :, pl.ds(k0, t)] * LOG2E

            def q_block(qi, carry, masked):
                dk, dv, dc = carry
                q0 = pl.multiple_of(qi * t, t)
                q = q_ref[pl.ds(q0, t), :]
                dov = do_ref[pl.ds(q0, t), :].astype(BF16)
                p = jnp.exp2(_fox_scores(q, k, ck2, masked, t) - lse_ref[0, pl.ds(q0, t), :][:, :1])
                dp = lax.dot_general(dov, v, _NT, preferred_element_type=F32)
                ds = p * (dp - delta_ref[pl.ds(q0, t), :][:, :1])
                dsb = ds.astype(BF16)
                dv = dv + lax.dot_general(p.astype(BF16), dov, _TN, preferred_element_type=F32)
                dk = dk + lax.dot_general(dsb, q, _TN, preferred_element_type=F32)
                dq_acc[pl.ds(q0, t), :] += jnp.dot(dsb, k, preferred_element_type=F32)
                dc = dc - jnp.sum(ds, axis=0, keepdims=True)
                dcq_ref[0, pl.ds(q0, t), :] += jnp.broadcast_to(jnp.sum(ds, axis=1, keepdims=True), (t, LANES))
                return dk, dv, dc

            init = (jnp.zeros((t, HEAD_DIM), F32), jnp.zeros((t, HEAD_DIM), F32), jnp.zeros((1, t), F32))
            carry = q_block(j, init, True)
            dk, dv, dc = lax.fori_loop(j + 1, nq, lambda qi, cr: q_block(qi, cr, False), carry)
            dk_ref[pl.ds(k0, t), :] = (dk * scale).astype(BF16)
            dv_ref[pl.ds(k0, t), :] = dv.astype(BF16)
            dc_ref[0, :, pl.ds(k0, t)] = dc
            return 0

        lax.fori_loop(0, nq, kv_block, 0)
        dq_ref[...] = (dq_acc[...] * scale).astype(BF16)

    col = lambda off: pl.BlockSpec((S, HEAD_DIM), lambda h: (0, off + h))
    per_head = pl.BlockSpec((1, S, LANES), lambda h: (h, 0, 0))
    row = pl.BlockSpec((1, 1, S), lambda h: (h, 0, 0))
    grad = jax.ShapeDtypeStruct((S, n_fox * HEAD_DIM), BF16)
    return pl.pallas_call(
        body, name="fox_bwd", grid=(n_fox,),
        in_specs=[col(0), col(n_fox), col(2 * n_fox), col(0), col(0), per_head, row],
        out_specs=[col(0), col(0), col(0), row, per_head],
        out_shape=[grad, grad, grad, jax.ShapeDtypeStruct((n_fox, 1, S), F32), jax.ShapeDtypeStruct((n_fox, S, LANES), F32)],
        scratch_shapes=[pltpu.VMEM((S, HEAD_DIM), F32), pltpu.VMEM((S, LANES), F32)],
        compiler_params=pltpu.CompilerParams(dimension_semantics=("parallel",),
                                             vmem_limit_bytes=_vmem(24 * S * HEAD_DIM * 4 + 16 * t * t * 4)),
    )(proj, proj, proj, o, do, lse_b, cum_row)


def _rope_tables(S):
    half = HEAD_DIM // 2
    inv_freq = 1.0 / (ROPE_THETA ** (jnp.arange(half, dtype=F32) * (2.0 / HEAD_DIM)))
    ang = jnp.arange(S).astype(F32)[:, None] * inv_freq[None, :]
    cos, sin = jnp.cos(ang), jnp.sin(ang)
    return jnp.concatenate([cos, cos], axis=-1), jnp.concatenate([-sin, sin], axis=-1)


def _rope(name, src, first_block, n_blocks, cos, sin_signed):
    S = src.shape[0]

    def body(x_ref, cos_ref, sin_ref, o_ref):
        xv = x_ref[...].astype(F32)
        o_ref[...] = (xv * cos_ref[...] + pltpu.roll(xv, HEAD_DIM // 2, 1) * sin_ref[...]).astype(BF16)

    table = pl.BlockSpec((S, HEAD_DIM), lambda n: (0, 0))
    return pl.pallas_call(
        body, name=name, grid=(n_blocks,),
        in_specs=[pl.BlockSpec((S, HEAD_DIM), lambda n: (0, first_block + n)), table, table],
        out_specs=pl.BlockSpec((S, HEAD_DIM), lambda n: (0, n)),
        out_shape=jax.ShapeDtypeStruct((S, n_blocks * HEAD_DIM), BF16),
        compiler_params=pltpu.CompilerParams(dimension_semantics=("parallel",),
                                             vmem_limit_bytes=_vmem(12 * S * HEAD_DIM * 4)),
    )(src, cos, sin_signed)


def _swa_tile(q_ref, kp_ref, kc_ref, n, group, scale):
    B = SWA_BLOCK
    qs = jnp.concatenate([q_ref[:, g * HEAD_DIM:(g + 1) * HEAD_DIM] for g in range(group)], axis=0)
    kcat = jnp.concatenate([kp_ref[...], kc_ref[...]], axis=0)
    s = lax.dot_general(qs, kcat, _NT, preferred_element_type=F32) * scale
    qi = lax.broadcasted_iota(jnp.int32, (group * B, 2 * B), 0) % B
    kj = lax.broadcasted_iota(jnp.int32, (group * B, 2 * B), 1)
    diff = qi + B - kj
    mask = (diff >= 0) & (diff < B) & ((n * B + kj - B) >= 0)
    return qs, kcat, jnp.where(mask, s, NEG)


def _swa_sink_col(sink_ref, kv, group):
    head = lax.broadcasted_iota(jnp.int32, (group * SWA_BLOCK, 1), 0) // SWA_BLOCK
    col = jnp.zeros((group * SWA_BLOCK, 1), F32)
    for g in range(group):
        col = jnp.where(head == g, sink_ref[kv * group + g], col)
    return col


def _swa_specs(n_kv, group, q_first, k_first, v_first):
    B = SWA_BLOCK
    prev = lambda n: jnp.maximum(n - 1, 0)
    return [
        pl.BlockSpec((B, group * HEAD_DIM), lambda kv, n: (n, q_first + kv)),
        pl.BlockSpec((B, HEAD_DIM), lambda kv, n: (prev(n), k_first + kv)),
        pl.BlockSpec((B, HEAD_DIM), lambda kv, n: (n, k_first + kv)),
        pl.BlockSpec((B, HEAD_DIM), lambda kv, n: (prev(n), v_first + kv)),
        pl.BlockSpec((B, HEAD_DIM), lambda kv, n: (n, v_first + kv)),
    ]


def _swa_fwd(rq, proj, v_first, sinks, n_q, n_kv):
    S = rq.shape[0]
    B = SWA_BLOCK
    group = n_q // n_kv
    scale = HEAD_DIM ** -0.5

    def body(q_ref, kp_ref, kc_ref, vp_ref, vc_ref, sink_ref, o_ref, lse_ref):
        kv, n = pl.program_id(0), pl.program_id(1)
        _, _, s = _swa_tile(q_ref, kp_ref, kc_ref, n, group, scale)
        sink = _swa_sink_col(sink_ref, kv, group)
        m = jnp.maximum(jnp.max(s, axis=-1, keepdims=True), sink)
        p = jnp.exp(s - m)
        denom = jnp.sum(p, axis=-1, keepdims=True) + jnp.exp(sink - m)
        vcat = jnp.concatenate([vp_ref[...], vc_ref[...]], axis=0)
        o = jnp.dot((p / denom).astype(BF16), vcat, preferred_element_type=F32)
        lse = m + jnp.log(denom)
        for g in range(group):
            o_ref[:, g * HEAD_DIM:(g + 1) * HEAD_DIM] = o[g * B:(g + 1) * B, :]
            lse_ref[0, :, g * LANES:(g + 1) * LANES] = jnp.broadcast_to(lse[g * B:(g + 1) * B, :], (B, LANES))

    specs = _swa_specs(n_kv, group, 0, n_q, v_first)
    q_blk = pl.BlockSpec((B, group * HEAD_DIM), lambda kv, n: (n, kv))
    return pl.pallas_call(
        body, name="swa_fwd", grid=(n_kv, S // B),
        in_specs=specs + [pl.BlockSpec(memory_space=pltpu.SMEM)],
        out_specs=[q_blk, pl.BlockSpec((1, B, group * LANES), lambda kv, n: (kv, n, 0))],
        out_shape=[jax.ShapeDtypeStruct((S, n_q * HEAD_DIM), F32), jax.ShapeDtypeStruct((n_kv, S, group * LANES), F32)],
        compiler_params=pltpu.CompilerParams(dimension_semantics=("parallel", "arbitrary")),
    )(rq, rq, rq, proj, proj, sinks)


def _swa_bwd(rq, proj, v_first, sinks, o, do, do_first, lse_b, n_q, n_kv):
    S = rq.shape[0]
    B = SWA_BLOCK
    group = n_q // n_kv
    scale = HEAD_DIM ** -0.5

    def body(q_ref, kp_ref, kc_ref, vp_ref, vc_ref, o_ref, do_ref, lse_ref, sink_ref,
             dq_ref, dk_ref, dv_ref, dsink_ref):
        kv, n = pl.program_id(0), pl.program_id(1)

        @pl.when(n == 0)
        def _():
            dk_ref[...] = jnp.zeros_like(dk_ref)
            dv_ref[...] = jnp.zeros_like(dv_ref)
            dsink_ref[...] = jnp.zeros_like(dsink_ref)

        qs, kcat, s = _swa_tile(q_ref, kp_ref, kc_ref, n, group, scale)
        sink = _swa_sink_col(sink_ref, kv, group)
        stack = lambda ref, w: jnp.concatenate([ref[:, g * w:(g + 1) * w] for g in range(group)], axis=0)
        lse = jnp.concatenate([lse_ref[0, :, g * LANES:g * LANES + 1] for g in range(group)], axis=0)
        do32 = stack(do_ref, HEAD_DIM)
        delta = jnp.sum(do32 * stack(o_ref, HEAD_DIM), axis=-1, keepdims=True)
        dov = do32.astype(BF16)
        p = jnp.exp(s - lse)
        vcat = jnp.concatenate([vp_ref[...], vc_ref[...]], axis=0)
        dp = lax.dot_general(dov, vcat, _NT, preferred_element_type=F32)
        ds = p * (dp - delta)
        dsb = ds.astype(BF16)
        dq = jnp.dot(dsb, kcat, preferred_element_type=F32) * scale
        for g in range(group):
            dq_ref[:, g * HEAD_DIM:(g + 1) * HEAD_DIM] = dq[g * B:(g + 1) * B, :].astype(BF16)
        dkcat = lax.dot_general(dsb, qs, _TN, preferred_element_type=F32) * scale
        dvcat = lax.dot_general(p.astype(BF16), dov, _TN, preferred_element_type=F32)
        prev0 = pl.multiple_of(jnp.maximum(n - 1, 0) * B, B)
        cur0 = pl.multiple_of(n * B, B)
        dk_ref[0, pl.ds(prev0, B), :] += dkcat[:B, :]
        dk_ref[0, pl.ds(cur0, B), :] += dkcat[B:, :]
        dv_ref[0, pl.ds(prev0, B), :] += dvcat[:B, :]
        dv_ref[0, pl.ds(cur0, B), :] += dvcat[B:, :]
        dsk = -jnp.exp(sink - lse) * delta
        lane = lax.broadcasted_iota(jnp.int32, (1, LANES), 1)
        row = jnp.zeros((1, LANES), F32)
        for g in range(group):
            row = row + jnp.where(lane == g, jnp.sum(dsk[g * B:(g + 1) * B, :]), 0.0)
        dsink_ref[0, 0:1, :] += row

    specs = _swa_specs(n_kv, group, 0, n_q, v_first)
    q_blk = pl.BlockSpec((B, group * HEAD_DIM), lambda kv, n: (n, kv))
    acc = pl.BlockSpec((1, S, HEAD_DIM), lambda kv, n: (kv, 0, 0))
    return pl.pallas_call(
        body, name="swa_bwd", grid=(n_kv, S // B),
        in_specs=specs + [q_blk, pl.BlockSpec((B, group * HEAD_DIM), lambda kv, n: (n, do_first + kv)),
                          pl.BlockSpec((1, B, group * LANES), lambda kv, n: (kv, n, 0)),
                          pl.BlockSpec(memory_space=pltpu.SMEM)],
        out_specs=[q_blk, acc, acc, pl.BlockSpec((1, 8, LANES), lambda kv, n: (kv, 0, 0))],
        out_shape=[jax.ShapeDtypeStruct((S, n_q * HEAD_DIM), BF16), jax.ShapeDtypeStruct((n_kv, S, HEAD_DIM), F32),
                   jax.ShapeDtypeStruct((n_kv, S, HEAD_DIM), F32), jax.ShapeDtypeStruct((n_kv, 8, LANES), F32)],
        compiler_params=pltpu.CompilerParams(dimension_semantics=("parallel", "arbitrary")),
    )(rq, rq, rq, proj, proj, o, do, lse_b, sinks)


def _adamw(w, g, m, v):
    m = ADAM_B1 * m + (1.0 - ADAM_B1) * g
    v = ADAM_B2 * v + (1.0 - ADAM_B2) * (g * g)
    m_hat = m / (1.0 - ADAM_B1 ** ADAM_STEP)
    v_hat = v / (1.0 - ADAM_B2 ** ADAM_STEP)
    delta = -ADAM_LR * (m_hat / (jnp.sqrt(v_hat) + ADAM_EPS) + ADAM_WD * w)
    return delta, m, v


def _mod_fwd(cond_in, w_mod, b_shard):
    R, D = cond_in.shape
    cols = w_mod.shape[1]
    tn = _fit(512, cols)

    def body(c_ref, w_ref, b_ref, o_ref):
        cv = c_ref[...]
        cond = (cv / (1.0 + jnp.exp(-cv))).astype(BF16)
        o_ref[...] = jnp.dot(cond, w_ref[...].astype(BF16), preferred_element_type=F32) + b_ref[...]

    return pl.pallas_call(
        body, name="mod_fwd", grid=(cols // tn,),
        in_specs=[pl.BlockSpec((R, D), lambda j: (0, 0)), pl.BlockSpec((D, tn), lambda j: (0, j)),
                  pl.BlockSpec((1, tn), lambda j: (0, j))],
        out_specs=pl.BlockSpec((R, tn), lambda j: (0, j)),
        out_shape=jax.ShapeDtypeStruct((R, cols), F32),
        compiler_params=pltpu.CompilerParams(dimension_semantics=("parallel",), vmem_limit_bytes=_vmem(3 * D * tn * 4)),
    )(cond_in, w_mod, b_shard)


def _mod_update(c_t, dmod, w, m, v):
    D, nb = c_t.shape
    cols = w.shape[1]
    tn = _fit(256, cols)

    def body(c_ref, d_ref, w_ref, m_ref, v_ref, g_ref, dl_ref, nm_ref, nv_ref):
        cv = c_ref[...]
        cond = cv / (1.0 + jnp.exp(-cv))
        g = jnp.zeros((D, tn), F32)
        for b in range(nb):
            g = g + cond[:, b:b + 1] * d_ref[b:b + 1, :]
        g_ref[...] = g
        dl_ref[...], nm_ref[...], nv_ref[...] = _adamw(w_ref[...], g, m_ref[...], v_ref[...])

    blk = pl.BlockSpec((D, tn), lambda j: (0, j))
    out = jax.ShapeDtypeStruct((D, cols), F32)
    return pl.pallas_call(
        body, name="mod_update", grid=(cols // tn,),
        in_specs=[pl.BlockSpec((D, nb), lambda j: (0, 0)), pl.BlockSpec((nb, tn), lambda j: (0, j)), blk, blk, blk],
        out_specs=[blk] * 4, out_shape=[out] * 4,
        compiler_params=pltpu.CompilerParams(dimension_semantics=("parallel",), vmem_limit_bytes=_vmem(18 * D * tn * 4)),
    )(c_t, dmod, w, m, v)


def _small_update(stacked, w, m, v):
    R, C = w.shape

    def body(s_ref, w_ref, m_ref, v_ref, g_ref, dl_ref, nm_ref, nv_ref):
        g = s_ref[0:R, :]
        for d in range(1, N_DEV):
            g = g + s_ref[d * R:(d + 1) * R, :]
        g_ref[...] = g
        dl_ref[...], nm_ref[...], nv_ref[...] = _adamw(w_ref[...], g, m_ref[...], v_ref[...])

    return pl.pallas_call(body, name="small_update", out_shape=[jax.ShapeDtypeStruct((R, C), F32)] * 4)(stacked, w, m, v)


def _place():
    return lax.axis_index("x"), lax.axis_index("y"), lax.axis_index("c")


def _allgather8(name, block):
    m_per, n = block.shape

    def body(x_ref, out_ref, token_ref, send_sems, recv_sems, local_sem):
        token_ref[...] = jnp.zeros_like(token_ref)
        x, y, c = _place()
        me, sibling = (x, y, c), (x, y, 1 - c)
        chips = [(1 - x, y), (x, 1 - y), (1 - x, 1 - y)]

        def rows(px, py, pc):
            return out_ref.at[pl.ds((4 * px + 2 * py + pc) * m_per, m_per), :]

        def copy(k, blk, to, src=None):
            return pltpu.make_async_remote_copy(
                src_ref=rows(*blk) if src is None else src, dst_ref=rows(*blk),
                send_sem=send_sems.at[k], recv_sem=recv_sems.at[k], device_id=to, device_id_type=MESH)

        mine = pltpu.make_async_copy(x_ref, rows(*me), local_sem)
        mine.start()
        first = [copy(0, me, sibling, src=x_ref)]
        first += [copy(1 + j, me, (*chip, c), src=x_ref) for j, chip in enumerate(chips)]
        for cp in first:
            cp.start()
        passed = [copy(4 + j, (*chip, c), sibling) for j, chip in enumerate(chips)]
        for j, chip in enumerate(chips):
            copy(1 + j, (*chip, c), me).wait_recv()
            passed[j].start()
        copy(0, sibling, me).wait_recv()
        for j, chip in enumerate(chips):
            copy(4 + j, (*chip, 1 - c), me).wait_recv()
        for cp in first + passed:
            cp.wait_send()
        mine.wait()

    vmem = pl.BlockSpec(memory_space=pltpu.VMEM)
    return pl.pallas_call(
        body, name=name,
        out_shape=[jax.ShapeDtypeStruct((N_DEV * m_per, n), block.dtype), jax.ShapeDtypeStruct((8, LANES), F32)],
        in_specs=[vmem], out_specs=[vmem, vmem],
        scratch_shapes=[pltpu.SemaphoreType.DMA((7,)), pltpu.SemaphoreType.DMA((7,)), pltpu.SemaphoreType.DMA],
    )(block)


_ANY = pl.BlockSpec(memory_space=pl.ANY)


def _half(ref, c, rows):
    return ref.at[pl.ds(c * (rows // 2), rows // 2), :]


_HBM = pl.BlockSpec(memory_space=pltpu.HBM)
_SEM = pl.BlockSpec(memory_space=pltpu.SEMAPHORE)
_EFFECT = pltpu.SideEffectType.DATAFLOW_SIDE_EFFECTING


def _ici_start(name, srcs, land_shapes, plan, per_source=3, after=None):
    ns, nl = len(srcs), len(land_shapes)
    n_copies = per_source * ns
    n_in = ns + nl + (after is not None)

    def body(*refs):
        src_refs, land_refs = refs[:ns], refs[ns:ns + nl]
        send_sems, recv_sems = refs[n_in], refs[n_in + 1]
        token = refs[-1]
        for n, (src, dst, peer, _) in enumerate(plan(src_refs, land_refs)):
            pltpu.make_async_remote_copy(src_ref=src, dst_ref=dst, send_sem=send_sems.at[n], recv_sem=recv_sems.at[n],
                                         device_id=peer, device_id_type=MESH).start()
        token[...] = jnp.zeros_like(token)

    lands = [lax.empty(s.shape, s.dtype) for s in land_shapes]
    out = pl.pallas_call(
        body, name=name,
        out_shape=(pltpu.SemaphoreType.DMA((n_copies,)), pltpu.SemaphoreType.DMA((n_copies,)),
                   *[pltpu.HBM(a.shape, a.dtype) for a in list(srcs) + lands], jax.ShapeDtypeStruct((8, LANES), F32)),
        in_specs=[_HBM] * (ns + nl) + [_ANY] * (after is not None),
        out_specs=(_SEM, _SEM, *[_HBM] * (ns + nl), pl.BlockSpec(memory_space=pltpu.VMEM)),
        input_output_aliases={n: 2 + n for n in range(ns + nl)},
        compiler_params=pltpu.CompilerParams(has_side_effects=_EFFECT),
    )(*[pltpu.with_memory_space_constraint(a, pltpu.HBM) for a in list(srcs) + lands],
      *([] if after is None else [after]))
    return out[0], out[1], list(out[2:2 + ns]), list(out[2 + ns:2 + ns + nl]), out[-1]


def _ici_wait(name, send_sems, recv_sems, srcs, lands, plan, after):
    ns, nl = len(srcs), len(lands)
    after = list(after) if isinstance(after, (list, tuple)) else [after]

    def body(*refs):
        src_refs, land_refs = refs[:ns], refs[ns:ns + nl]
        send_sems, recv_sems = refs[ns + nl], refs[ns + nl + 1]
        for n, (src, _, peer, mine) in enumerate(plan(src_refs, land_refs)):
            cp = pltpu.make_async_remote_copy(src_ref=src, dst_ref=mine, send_sem=send_sems.at[n],
                                              recv_sem=recv_sems.at[n], device_id=peer, device_id_type=MESH)
            cp.wait_send()
            cp.wait_recv()

    out = pl.pallas_call(
        body, name=name, out_shape=[pltpu.HBM(a.shape, a.dtype) for a in list(srcs) + list(lands)],
        in_specs=[_HBM] * (ns + nl) + [_SEM, _SEM] + [_ANY] * len(after), out_specs=[_HBM] * (ns + nl),
        input_output_aliases={n: n for n in range(ns + nl)},
        compiler_params=pltpu.CompilerParams(has_side_effects=_EFFECT),
    )(*srcs, *lands, send_sems, recv_sems, *after)
    return list(out[:ns]), list(out[ns:])


def _own_slab(name, chip, w, after):
    R, C = w.shape
    tr, tc = _tiles(R, C)
    tied = [] if after is None else [after]

    def body(chip_ref, w_ref, *rest):
        stack_ref, token_ref = rest[-2:]
        stack_ref[0] = w_ref[...].astype(BF16)
        token_ref[...] = jnp.zeros_like(token_ref)

    small = pl.BlockSpec((8, LANES), lambda r, q, chip_ref: (0, 0))
    grid_spec = pltpu.PrefetchScalarGridSpec(
        num_scalar_prefetch=1, grid=(R // tr, C // tc),
        in_specs=[pl.BlockSpec((tr, tc), lambda r, q, chip_ref: (r, q))] + [small] * len(tied),
        out_specs=[pl.BlockSpec((1, tr, tc), lambda r, q, chip_ref: (chip_ref[0], r, q)), small])
    return pl.pallas_call(
        body, name=name, grid_spec=grid_spec,
        out_shape=[jax.ShapeDtypeStruct((N_CHIPS, R, C), BF16), jax.ShapeDtypeStruct((8, LANES), F32)],
        compiler_params=pltpu.CompilerParams(dimension_semantics=("arbitrary", "arbitrary")),
    )(chip, w, *tied)


def _gather_plan(src_refs, land_refs):
    x, y, c = _place()
    copies = []
    for stack in src_refs:
        R = stack.shape[1]
        own = _half(stack.at[2 * x + y], c, R)
        for cx, cy in [(1 - x, y), (x, 1 - y), (1 - x, 1 - y)]:
            copies.append((own, own, (cx, cy, c), _half(stack.at[2 * cx + cy], c, R)))
    return copies


def _pass_plan(src_refs, land_refs):
    x, y, c = _place()
    copies = []
    for land in src_refs:
        R = land.shape[1]
        for cx, cy in [(1 - x, y), (x, 1 - y), (1 - x, 1 - y)]:
            slot = land.at[2 * cx + cy]
            copies.append((_half(slot, c, R), _half(slot, c, R), (x, y, 1 - c), _half(slot, 1 - c, R)))
    return copies


def _share_plan(src_refs, land_refs):
    x, y, c = _place()
    return [(h, land, (x, y, 1 - c), land) for h, land in zip(src_refs, land_refs)]


def _pass_to_sibling(name, lands):
    nw = len(lands)

    def body(*refs):
        ins, outs = refs[:nw], refs[nw:2 * nw]
        send_sems, recv_sems = refs[2 * nw:]
        x, y, c = _place()
        chips = [(1 - x, y), (x, 1 - y), (1 - x, 1 - y)]
        copies = []
        for k in range(nw):
            R = ins[k].shape[1]
            for j, (cx, cy) in enumerate(chips):
                cp = pltpu.make_async_remote_copy(
                    src_ref=_half(ins[k].at[2 * cx + cy], c, R), dst_ref=_half(outs[k].at[2 * cx + cy], c, R),
                    send_sem=send_sems.at[3 * k + j], recv_sem=recv_sems.at[3 * k + j],
                    device_id=(x, y, 1 - c), device_id_type=MESH)
                cp.start()
                copies.append(cp)
        for k in range(nw):
            R = ins[k].shape[1]
            for j, (cx, cy) in enumerate(chips):
                pltpu.make_async_remote_copy(
                    src_ref=_half(ins[k].at[2 * cx + cy], c, R), dst_ref=_half(outs[k].at[2 * cx + cy], 1 - c, R),
                    send_sem=send_sems.at[3 * k + j], recv_sem=recv_sems.at[3 * k + j],
                    device_id=(x, y, 1 - c), device_id_type=MESH).wait_recv()
        for cp in copies:
            cp.wait_send()

    return pl.pallas_call(
        body, name=name, out_shape=[jax.ShapeDtypeStruct(a.shape, a.dtype) for a in lands],
        in_specs=[_ANY] * nw, out_specs=[_ANY] * nw, input_output_aliases={k: k for k in range(nw)},
        scratch_shapes=[pltpu.SemaphoreType.DMA((3 * nw,)), pltpu.SemaphoreType.DMA((3 * nw,))],
    )(*lands)


def _tie(vec, token):
    return vec + token[0:1, 0:1]


ROW_ALIGN = 16
TILE_ELEMS = 512 * 1024


def _tiles(rows, cols):
    fits = [t for t in range(ROW_ALIGN, min(rows, 256) + 1, ROW_ALIGN) if rows % t == 0]
    tr = fits[-1] if fits and fits[-1] >= 64 else rows
    tc = cols
    while tr * tc > TILE_ELEMS and tc % (2 * LANES) == 0:
        tc //= 2
    return tr, tc


def _scatter_plan(src_refs, land_refs):
    x, y, c = _place()
    copies = []
    for p, land in zip(src_refs, land_refs):
        for j, (cx, cy) in enumerate([(1 - x, y), (x, 1 - y), (1 - x, 1 - y)]):
            copies.append((p.at[2 * cx + cy], land.at[j], (cx, cy, c), land.at[j]))
    return copies


def _chip_add(name, chip, sums, recv):
    _, H, C = sums.shape
    tr, tc = _tiles(H, C)

    def body(chip_ref, p_ref, r_ref, o_ref):
        total = p_ref[0].astype(F32)
        for j in range(3):
            total = total + r_ref[j].astype(F32)
        o_ref[...] = total

    grid_spec = pltpu.PrefetchScalarGridSpec(
        num_scalar_prefetch=1, grid=(H // tr, C // tc),
        in_specs=[pl.BlockSpec((1, tr, tc), lambda r, q, chip_ref: (chip_ref[0], r, q)),
                  pl.BlockSpec((3, tr, tc), lambda r, q, chip_ref: (0, r, q))],
        out_specs=pl.BlockSpec((tr, tc), lambda r, q, chip_ref: (r, q)))
    return pl.pallas_call(
        body, name=name, grid_spec=grid_spec, out_shape=jax.ShapeDtypeStruct((H, C), F32),
        compiler_params=pltpu.CompilerParams(dimension_semantics=("parallel", "parallel")),
    )(chip, sums, recv)


def _pair_share(name, halves):
    nw = len(halves)

    def body(*refs):
        hs, outs = refs[:nw], refs[nw:2 * nw]
        send_sems, recv_sems = refs[2 * nw:]
        x, y, c = _place()
        copies = []
        for k in range(nw):
            cp = pltpu.make_async_remote_copy(
                src_ref=hs[k], dst_ref=outs[k], send_sem=send_sems.at[k], recv_sem=recv_sems.at[k],
                device_id=(x, y, 1 - c), device_id_type=MESH)
            cp.start()
            copies.append(cp)
        for cp in copies:
            cp.wait()

    return pl.pallas_call(
        body, name=name,
        out_shape=[jax.ShapeDtypeStruct(h.shape, h.dtype) for h in halves],
        in_specs=[_ANY] * nw, out_specs=[_ANY] * nw,
        scratch_shapes=[pltpu.SemaphoreType.DMA((nw,)), pltpu.SemaphoreType.DMA((nw,))],
    )(*halves)


def _adam_halves(name, core, w, g_own, g_other, m, v):
    R, C = w.shape
    H = R // 2
    tr, tc = _tiles(H, C)
    nr, nc = H // tr, C // tc

    def body(core_ref, w_ref, go_ref, gr_ref, m_ref, v_ref, g_ref, dl_ref, nm_ref, nv_ref):
        own = (pl.program_id(0) // nr) == core_ref[0]
        g = jnp.where(own, go_ref[...], gr_ref[...])
        g_ref[...] = g
        dl_ref[...], nm_ref[...], nv_ref[...] = _adamw(w_ref[...], g, m_ref[...], v_ref[...])

    blk = pl.BlockSpec((tr, tc), lambda r, q, core_ref: (r, q))

    def half_spec(is_own):
        def index(r, q, core_ref):
            mine = ((r // nr) == core_ref[0]) == is_own
            done = is_own == (core_ref[0] == 0)
            return (jnp.where(mine, r % nr, jnp.where(done, nr - 1, 0)), jnp.where(mine, q, jnp.where(done, nc - 1, 0)))
        return pl.BlockSpec((tr, tc), index)
    out = jax.ShapeDtypeStruct((R, C), F32)
    grid_spec = pltpu.PrefetchScalarGridSpec(
        num_scalar_prefetch=1, grid=(R // tr, nc), in_specs=[blk, half_spec(True), half_spec(False), blk, blk],
        out_specs=[blk] * 4)
    return pl.pallas_call(
        body, name=name, grid_spec=grid_spec, out_shape=[out] * 4,
        compiler_params=pltpu.CompilerParams(dimension_semantics=("parallel", "parallel"),
                                             vmem_limit_bytes=_vmem(20 * tr * tc * 4)),
    )(core, w, g_own, g_other, m, v)


def kernel(x, c, w_mod, b_mod, g_pre_mix, g_post_mix, w_in, b_forget, swa_sinks, w_out, g_pre_mlp, g_post_mlp, w_up, w_down, loss_target, m_w_mod, m_b_mod, m_g_pre_mix, m_g_post_mix, m_w_in, m_b_forget, m_swa_sinks, m_w_out, m_g_pre_mlp, m_g_post_mlp, m_w_up, m_w_down, v_w_mod, v_b_mod, v_g_pre_mix, v_g_post_mix, v_w_in, v_b_forget, v_swa_sinks, v_w_out, v_g_pre_mlp, v_g_post_mlp, v_w_up, v_w_down):
    S, D = x.shape[1], x.shape[2]
    n_heads = D // HEAD_DIM
    n_fox = n_heads // 2
    n_swa = n_heads - n_fox
    n_kv = max(1, n_swa // 4)
    fox_w, swa_w, kv_w = n_fox * HEAD_DIM, n_swa * HEAD_DIM, n_kv * HEAD_DIM
    main_w = 3 * fox_w + swa_w + 2 * kv_w
    in_w = main_w + n_fox
    mod_cols = w_mod.shape[2]

    ax, ay, ac = _place()
    chip = 2 * ax + ay
    dev = 2 * chip + ac
    chip_arr = jnp.reshape(chip, (1,)).astype(jnp.int32)
    core_arr = jnp.reshape(ac, (1,)).astype(jnp.int32)

    x2, tgt = x[0], loss_target[0]

    in_rows = in_w // N_CHIPS
    in_rows_pad = -(-in_rows // (2 * LANES)) * (2 * LANES)
    slab_w = N_CHIPS * in_rows_pad

    def rows_of(a):
        return jnp.pad(a[0].T, ((0, in_rows_pad - in_rows), (0, 0)))

    w_in_stack, token = _own_slab("own_slab_w_in", chip_arr, rows_of(w_in), None)

    c_all, _ = _allgather8("gather_c", _tie(c, token).reshape(8, D // 8))
    c_all = c_all.reshape(N_DEV, D)
    b_shard = lax.dynamic_slice_in_dim(b_mod, chip * mod_cols, mod_cols, axis=1)
    mod_shard = _mod_fwd(jnp.pad(c_all, ((0, 16 - N_DEV), (0, 0))), w_mod[0], b_shard)[:N_DEV]
    mod_all, token = _allgather8("gather_mod", mod_shard)
    mod_all = mod_all.reshape(N_CHIPS, 2, N_DEV, mod_cols)[:, 0]
    mod = lax.dynamic_index_in_dim(mod_all, dev, axis=1, keepdims=False).reshape(N_MOD, 1, D)
    sh_a, sc_a, gt_a, sh_m, sc_m, gt_m = [mod[n] for n in range(N_MOD)]

    def slab_cols(lo, hi):
        spans = []
        while lo < hi:
            s, r = divmod(lo, in_rows)
            n = min(hi - lo, in_rows - r)
            spans.append((s * in_rows_pad + r, s * in_rows_pad + r + n))
            lo += n
        return spans

    gate_lo = 3 * fox_w
    main_spans = slab_cols(0, gate_lo) + slab_cols(gate_lo + n_fox, in_w)
    (gate_first, gate_last), = slab_cols(gate_lo, gate_lo + n_fox)

    names = ["w_in", "w_out", "w_up", "w_down"]
    flights = {}
    for n, w in zip(names, [None, w_out[0], w_up[0], w_down[0]]):
        stack = w_in_stack if n == "w_in" else _own_slab("own_slab_" + n, chip_arr, w, token)[0]
        flights[n] = _ici_start("gather_start_" + n, [stack], [], _gather_plan, after=token)
        token = flights[n][4]
    sc_a = _tie(sc_a, token)

    def arrived(n, after):
        send, recv, stacks, _, _ = flights[n]
        stacks, _ = _ici_wait("gather_wait_" + n, send, recv, stacks, [], _gather_plan, after)
        return _ici_start("gather_pass_start_" + n, stacks, [], _pass_plan)

    def gathered(n, after, in_flight=None):
        if in_flight is None:
            send, recv, stacks, _, _ = flights[n]
            stacks, _ = _ici_wait("gather_wait_" + n, send, recv, stacks, [], _gather_plan, after)
            return _pass_to_sibling("gather_pass_" + n, stacks)[0]
        send, recv, stacks, _, _ = in_flight
        return _ici_wait("gather_pass_wait_" + n, send, recv, stacks, [], _pass_plan, after)[0][0]

    d_ff = N_CHIPS * w_up.shape[2]

    h = _pre_norm(x2, g_pre_mix, sc_a, sh_a)
    in_state = [rows_of(w_in)] + [rows_of(_tie(a, token)) for a in (m_w_in, v_w_in)]
    cos, sin_signed = _rope_tables(S)

    def pack(bm, gpm, gqm, gpl, gql, bf, sk):
        last = jnp.concatenate([bf, sk, jnp.zeros((1, D - n_fox - n_swa), F32)], axis=1)
        return jnp.concatenate([bm.reshape(N_MOD, D), gpm, gqm, gpl, gql, last, jnp.zeros((5, D), F32)], axis=0)

    small_state = [pack(b_mod, g_pre_mix, g_post_mix, g_pre_mlp, g_post_mlp, b_forget, swa_sinks),
                   pack(m_b_mod, m_g_pre_mix, m_g_post_mix, m_g_pre_mlp, m_g_post_mlp, m_b_forget, m_swa_sinks),
                   pack(v_b_mod, v_g_pre_mix, v_g_post_mix, v_g_pre_mlp, v_g_post_mlp, v_b_forget, v_swa_sinks)]
    ready = h[:8, :LANES].astype(F32) + cos[:8]
    w_slab_t = gathered("w_in", [ready] + in_state[1:] + small_state).reshape(slab_w, D)
    tm_p, tn_p = _fit(MM_TM, S), _fit(MM_TN if slab_w % MM_TN == 0 else MM_TN // 2, slab_w)
    win0 = gate_first // LANES * LANES
    win_j, win_off = divmod(win0, tn_p)
    assert win_off + 2 * LANES <= tn_p and gate_last - win0 <= 2 * LANES

    def proj_epilogue(acc, ex, outs):
        outs[0][...] = acc.astype(BF16)

        @pl.when(pl.program_id(1) == win_j)
        def _():
            outs[1][...] = acc[:, win_off:win_off + 2 * LANES]

    proj_slab, gate_win = _matmul(
        "in_proj", h, w_slab_t, "nt",
        [((S, slab_w), BF16, (tm_p, tn_p), lambda i, j: (i, j)), ((S, 2 * LANES), F32, (tm_p, 2 * LANES), lambda i, j: (i, 0))],
        proj_epilogue, tn=tn_p, revisits=True)
    proj = jnp.concatenate([proj_slab[:, lo:hi] for lo, hi in main_spans], axis=1)
    out_flight = arrived("w_out", proj_slab)
    fg = _tie(jnp.pad(gate_win[:, gate_first - win0:gate_last - win0], ((0, 0), (0, LANES - n_fox))), out_flight[4])
    b_pad = jnp.pad(b_forget, ((0, 0), (0, LANES - n_fox)))
    cum_row = _fox_gate_fwd(fg, b_pad)[:n_fox].reshape(n_fox, 1, S)
    fox_o, fox_lse = _fox_fwd(proj, cum_row, n_fox)

    rq = _rope("rope_fwd", proj, 3 * n_fox, n_swa + n_kv, cos, sin_signed)
    v_first = 3 * n_fox + n_swa + n_kv
    sinks = swa_sinks[0]
    swa_o, swa_lse = _swa_fwd(rq, proj, v_first, sinks, n_swa, n_kv)

    mixcat = jnp.concatenate([fox_o, swa_o], axis=1).astype(BF16)
    up_flight = arrived("w_up", mixcat)
    w_out_f = gathered("w_out", mixcat, out_flight).reshape(D, D)
    mix = _mm_plain("out_proj", mixcat, w_out_f, "nn", BF16, after=up_flight[4])
    x1, h2 = _post_mix(x2, mix, g_post_mix, gt_a, g_pre_mlp, sc_m, sh_m)
    w_up_f = jnp.transpose(gathered("w_up", h2, up_flight), (1, 0, 2)).reshape(D, d_ff)

    tm_u, tn_u = _fit(MM_TM, S), _fit(MM_TN, d_ff)

    def up_epilogue(acc, ex, outs):
        outs[0][...] = acc.astype(BF16)
        r = jnp.maximum(acc, 0.0)
        outs[1][...] = (r * r).astype(BF16)

    ublk = ((S, d_ff), BF16, (tm_u, tn_u), lambda i, j: (i, j))
    u, a = _matmul("mlp_up", h2, w_up_f, "nn", [ublk, ublk], up_epilogue)
    w_down_f = gathered("w_down", a).reshape(d_ff, D)
    y = _mm_plain("mlp_down", a, w_down_f, "nn", BF16)

    dy, dout, loss_part, acc_mlp_post = _loss_and_post_mlp_bwd(x1, y, tgt, g_post_mlp, gt_m)

    def du_epilogue(acc, ex, outs):
        outs[0][...] = (acc * (2.0 * jnp.maximum(ex[0][...].astype(F32), 0.0))).astype(BF16)

    du = _matmul("mlp_down_bwd", dy, w_down_f, "nt", [ublk], du_epilogue,
                 extras=[(u, (tm_u, tn_u), lambda i, j: (i, j))])[0]
    def pair_send(tag, part):
        return _ici_start("grad_pair_start_" + tag, [part], [jax.ShapeDtypeStruct(part.shape, BF16)], _share_plan,
                          per_source=1)

    def pair_recv(tag, flight, after):
        send, recv, srcs, lands, _ = flight
        return _ici_wait("grad_pair_wait_" + tag, send, recv, srcs, lands, _share_plan, after)[1][0]

    def scatter_start(tag, sums):
        return _ici_start("grad_scatter_start_" + tag, sums,
                          [jax.ShapeDtypeStruct((3,) + p.shape[1:], BF16) for p in sums], _scatter_plan)

    def scatter_finish(tag, flight, after):
        send, recv, srcs, lands, _ = flight
        sums, received = _ici_wait("grad_scatter_wait_" + tag, send, recv, srcs, lands, _scatter_plan, after)
        return [_chip_add("chip_add_%s_%d" % (tag, k), chip_arr, p, r) for k, (p, r) in enumerate(zip(sums, received))]

    tm_g = _fit(MM_TM, D // 2)
    pair_down = pair_send("down", _grad_half("grad_w_down_a", core_arr, a, dy, N_CHIPS, 1, tm_g, True))
    pair_up = pair_send("up", _grad_half("grad_w_up_a", core_arr, h2, du, 1, N_CHIPS, tm_g, True, after=pair_down[4]))
    sum_down = _grad_half("grad_w_down_b", core_arr, a, dy, N_CHIPS, 1, tm_g, False,
                          recv=pair_recv("down", pair_down, pair_up[4]))
    sum_up = _grad_half("grad_w_up_b", core_arr, h2, du, 1, N_CHIPS, tm_g, False, recv=pair_recv("up", pair_up, sum_down))
    flight_mlp = scatter_start("mlp", [sum_up, sum_down])
    dh2 = _mm_plain("mlp_up_bwd", du, w_up_f, "nt", BF16, after=flight_mlp[4])
    dx1, dmix, acc_mid = _pre_mlp_and_post_mix_bwd(dh2, x1, dout, mix, _tie(g_pre_mlp, flight_mlp[4]), sc_m,
                                                   g_post_mix, gt_a)

    dmixcat = _mm_plain("out_proj_bwd", dmix, w_out_f, "nt", F32)

    fdq, fdk, fdv, dcum_row, dcum_q = _fox_bwd(proj, fox_o, dmixcat, fox_lse, cum_row, n_fox)
    dcum_k = jnp.pad(dcum_row.reshape(n_fox, S), ((0, LANES - n_fox), (0, 0)))
    dfg, db_forget = _fox_gate_bwd(dcum_k, dcum_q, fg, b_pad)

    group_w = (n_swa // n_kv) * HEAD_DIM
    sdq, sdk, sdv, dsink = _swa_bwd(rq, proj, v_first, sinks, swa_o, dmixcat, fox_w // group_w, swa_lse, n_swa, n_kv)
    drq = jnp.concatenate([sdq, jnp.transpose(sdk, (1, 0, 2)).reshape(S, kv_w).astype(BF16)], axis=1)
    d_sq_sk = _rope("rope_bwd", drq, 0, n_swa + n_kv, cos, -sin_signed)
    dsv = jnp.transpose(sdv, (1, 0, 2)).reshape(S, kv_w).astype(BF16)
    dproj = jnp.concatenate([fdq, fdk, fdv, d_sq_sk, dsv], axis=1)

    pieces = []
    for s in range(N_CHIPS):
        lo, hi = s * in_rows, (s + 1) * in_rows
        for src, first, last, shift in [(dproj, 0, gate_lo, 0), (dfg, gate_lo, gate_lo + n_fox, gate_lo),
                                        (dproj, gate_lo + n_fox, in_w, n_fox)]:
            if max(lo, first) < min(hi, last):
                pieces.append(src[:, max(lo, first) - shift:min(hi, last) - shift])
        pieces.append(jnp.zeros((S, in_rows_pad - in_rows), BF16))
    dproj_slab = jnp.concatenate(pieces, axis=1)

    tm_in, tm_out = in_rows_pad // 2, D // (2 * N_CHIPS)
    pair_in = pair_send("in", _grad_half("grad_w_in_a", core_arr, dproj_slab, h, N_CHIPS, 1, tm_in, True))
    pair_out = pair_send("out", _grad_half("grad_w_out_a", core_arr, mixcat, dmix, N_CHIPS, 1, tm_out, True,
                                           after=pair_in[4]))
    sum_in = _grad_half("grad_w_in_b", core_arr, dproj_slab, h, N_CHIPS, 1, tm_in, False,
                        recv=pair_recv("in", pair_in, pair_out[4]))
    dh = _mm_plain("in_proj_bwd", dproj_slab, w_slab_t, "nn", BF16, tk=slab_w // 2)
    grad_x, acc_pre = _pre_mix_bwd(dh, x2, dx1, g_pre_mix, sc_a)

    zero_row = jnp.zeros((1, D), F32)
    tail = jnp.concatenate([db_forget[0:1, :n_fox], dsink[:, 0, :n_swa // n_kv].reshape(1, n_swa),
                            loss_part[0:1, 0:1], jnp.zeros((1, D - n_fox - n_swa - 1), F32)], axis=1)
    partial = jnp.concatenate([
        acc_pre[0:1], acc_pre[1:2], acc_mid[3:4], acc_mid[0:1], acc_mid[1:2], acc_mlp_post[0:1],
        acc_pre[2:3], acc_mid[4:5], acc_mid[2:3], acc_mlp_post[1:2], tail] + [zero_row] * 5, axis=0)
    gathered_small, token = _allgather8("gather_small_grads", partial)

    sum_out = _grad_half("grad_w_out_b", core_arr, mixcat, dmix, N_CHIPS, 1, tm_out, False,
                         recv=pair_recv("out", pair_out, token))
    flight_mix = scatter_start("mix", [sum_in, sum_out])
    halves_mlp = scatter_finish("mlp", flight_mlp, flight_mix[4])
    share_mlp = _ici_start("grad_share_start_mlp", halves_mlp,
                           [jax.ShapeDtypeStruct(hv.shape, F32) for hv in halves_mlp], _share_plan, per_source=1)

    def unpack(p):
        return {"b_mod": p[0:N_MOD].reshape(1, N_MOD * D), "g_pre_mix": p[6:7], "g_post_mix": p[7:8],
                "g_pre_mlp": p[8:9], "g_post_mlp": p[9:10], "b_forget": p[10:11, :n_fox],
                "swa_sinks": p[10:11, n_fox:n_fox + n_swa]}

    small_out = _small_update(gathered_small, _tie(small_state[0], share_mlp[4]), small_state[1], small_state[2])
    g_small, d_small, m_small, v_small = [unpack(p) for p in small_out]
    loss = small_out[0][N_MOD + 4, n_fox + n_swa]

    dmod_all = gathered_small.reshape(N_DEV, 16, D)[:, :N_MOD].reshape(N_DEV, N_MOD * D)
    dmod_shard = _tie(lax.dynamic_slice_in_dim(dmod_all, chip * mod_cols, mod_cols, axis=1), share_mlp[4])
    g_w_mod, d_w_mod, nm_w_mod, nv_w_mod = _mod_update(c_all.T, dmod_shard, w_mod[0], m_w_mod[0], v_w_mod[0])
    send, recv, halves_mlp, lands, _ = share_mlp
    halves_mlp, others_mlp = _ici_wait("grad_share_wait_mlp", send, recv, halves_mlp, lands, _share_plan,
                                       d_w_mod[:8, :LANES] + small_out[1][:8, :LANES])

    grads = dict(g_small, w_mod=g_w_mod[None])
    deltas = dict(d_small, w_mod=d_w_mod[None])
    new_m = dict(m_small, w_mod=nm_w_mod[None])
    new_v = dict(v_small, w_mod=nv_w_mod[None])
    weights = {"w_in": (w_in, m_w_in, v_w_in), "w_out": (w_out, m_w_out, v_w_out), "w_up": (w_up, m_w_up, v_w_up),
               "w_down": (w_down, m_w_down, v_w_down)}

    def big_update(n, own, other):
        transposed = n == "w_in"
        w, m, v = in_state if transposed else [a[0] for a in weights[n]]
        outs = _adam_halves("adam_" + n, core_arr, w, own, other, m, v)
        if transposed:
            outs = [o[:in_rows].T for o in outs]
        grads[n], deltas[n], new_m[n], new_v[n] = [o[None] for o in outs]

    big_update("w_up", halves_mlp[0], others_mlp[0])
    big_update("w_down", halves_mlp[1], others_mlp[1])
    ran = deltas["w_down"][0, :8, :LANES] + deltas["w_up"][0, :8, :LANES] + d_w_mod[:8, :LANES]
    halves_mix = scatter_finish("mix", flight_mix, ran)
    others_mix = _pair_share("grad_pair_share_mix", halves_mix)
    big_update("w_in", halves_mix[0], others_mix[0])
    big_update("w_out", halves_mix[1], others_mix[1])

    order = ["w_mod", "b_mod", "g_pre_mix", "g_post_mix", "w_in", "b_forget", "swa_sinks", "w_out", "g_pre_mlp",
             "g_post_mlp", "w_up", "w_down"]
    return (loss, grad_x[None], *[grads[n] for n in order], *[deltas[n] for n in order],
            *[new_m[n] for n in order], *[new_v[n] for n in order])
```

```python
import jax
import jax.numpy as jnp
from jax import lax
from jax.experimental import pallas as pl
from jax.experimental.pallas import tpu as pltpu

F32 = jnp.float32
BF16 = jnp.bfloat16
MESH = pl.DeviceIdType.MESH

HEAD_DIM = 128
SWA_BLOCK = 128
ROPE_THETA = 10000.0
NORM_EPS = 1e-6
NEG = -1e30
N_MOD = 6
ADAM_LR = 0.001
ADAM_B1 = 0.9
ADAM_B2 = 0.999
ADAM_EPS = 1e-08
ADAM_WD = 0.01
ADAM_STEP = 10
N_CHIPS = 4
N_DEV = 8
LANES = 128
VMEM_CAP = 60 * 1024 * 1024

_NN = (((1,), (0,)), ((), ()))
_NT = (((1,), (1,)), ((), ()))
_TN = (((0,), (0,)), ((), ()))


def _vmem(nbytes):
    return int(min(VMEM_CAP, nbytes * 5 // 4 + (4 << 20)))


def _nbytes(shape, dtype):
    n = 1
    for s in shape:
        n *= s
    return n * jnp.dtype(dtype).itemsize


def _fit(t, n):
    t = min(t, n)
    assert n % t == 0, (t, n)
    return t


MM_TM, MM_TN, MM_TK = 512, 1024, 2048


def _matmul(name, a, b, mode, out_defs, epilogue, extras=(), tm=MM_TM, tn=MM_TN, tk=MM_TK, revisits=False,
            row_sel=None):
    stacked = b.ndim == 3
    b_rows, b_cols = b.shape[-2], b.shape[-1] * (b.shape[0] if stacked else 1)
    if mode == "nn":
        (M, K), (K2, N) = a.shape, (b_rows, b_cols)
    elif mode == "nt":
        (M, K), (N, K2) = a.shape, (b_rows, b_cols)
    else:
        (K, M), (K2, N) = a.shape, (b_rows, b_cols)
    assert K == K2 and not (stacked and mode == "tn"), (a.shape, b.shape, mode)
    tm = _fit(tm, M)
    tn = _fit(tn, b.shape[-1] if stacked and mode == "nn" else N)
    tk = _fit(tk, b.shape[-1] if stacked and mode == "nt" else K)
    nk = K // tk
    dims = {"nn": _NN, "nt": _NT, "tn": _TN}[mode]
    if row_sel is None:
        grid_m, a_row = M // tm, lambda i, *sel: i
    else:
        grid_m, a_row = row_sel[2], lambda i, *sel: row_sel[1](i, sel[0])
    a_spec = (pl.BlockSpec((tk, tm), lambda i, j, k, *sel: (k, a_row(i, *sel))) if mode == "tn"
              else pl.BlockSpec((tm, tk), lambda i, j, k, *sel: (a_row(i, *sel), k)))
    if stacked:
        per = b.shape[-1] // (tk if mode == "nt" else tn)
        b_spec = (pl.BlockSpec((1, tn, tk), lambda i, j, k, *sel: (k // per, j, k % per)) if mode == "nt"
                  else pl.BlockSpec((1, tk, tn), lambda i, j, k, *sel: (j // per, k, j % per)))
    else:
        b_spec = (pl.BlockSpec((tn, tk), lambda i, j, k, *sel: (j, k)) if mode == "nt"
                  else pl.BlockSpec((tk, tn), lambda i, j, k, *sel: (k, j)))
    n_ex, n_out = len(extras), len(out_defs)

    def body(*refs):
        if row_sel is not None:
            refs = refs[1:]
        a_ref, b_ref = refs[0], refs[1]
        ex = refs[2:2 + n_ex]
        outs = refs[2 + n_ex:2 + n_ex + n_out]
        b_blk = b_ref[0] if stacked else b_ref[...]
        prod = lax.dot_general(a_ref[...], b_blk, dims, preferred_element_type=F32)
        if nk == 1:
            epilogue(prod, ex, outs)
        else:
            acc_ref = refs[-1]
            k = pl.program_id(2)

            @pl.when(k == 0)
            def _():
                acc_ref[...] = prod

            @pl.when(k > 0)
            def _():
                acc_ref[...] += prod

            @pl.when(k == nk - 1)
            def _():
                epilogue(acc_ref[...], ex, outs)

    def wrap(f):
        return lambda i, j, k, *sel: f(i, j)

    in_specs = [a_spec, b_spec] + [pl.BlockSpec(blk, wrap(f)) for _, blk, f in extras]
    out_specs = [pl.BlockSpec(blk, wrap(f)) for _, _, blk, f in out_defs]
    out_shape = [jax.ShapeDtypeStruct(s, d) for s, d, _, _ in out_defs]
    need = 2 * (tm * tk + tk * tn) * a.dtype.itemsize + 3 * tm * tn * 4
    need += sum(2 * _nbytes(blk, arr.dtype) for arr, blk, _ in extras)
    need += sum(2 * _nbytes(blk, d) for _, d, blk, _ in out_defs)
    grid = (grid_m, N // tn, nk)
    scratch = [pltpu.VMEM((tm, tn), F32)] if nk > 1 else []
    params = pltpu.CompilerParams(
        dimension_semantics=("parallel", "arbitrary" if revisits else "parallel", "arbitrary"),
        vmem_limit_bytes=_vmem(need))
    operands = (a, b, *[arr for arr, _, _ in extras])
    if row_sel is None:
        return pl.pallas_call(body, name=name, grid=grid, in_specs=in_specs, out_specs=out_specs, out_shape=out_shape,
                              scratch_shapes=scratch, compiler_params=params)(*operands)
    grid_spec = pltpu.PrefetchScalarGridSpec(num_scalar_prefetch=1, grid=grid, in_specs=in_specs, out_specs=out_specs,
                                             scratch_shapes=scratch)
    return pl.pallas_call(body, name=name, grid_spec=grid_spec, out_shape=out_shape,
                          compiler_params=params)(row_sel[0], *operands)


def _grad_half(name, core, a, b, row_slabs, col_slabs, tm, other, recv=None, after=None):
    (_, M), (_, N) = a.shape, b.shape
    H = M // (2 * row_slabs)
    nh = H // tm
    tn = _fit(MM_TN, N // col_slabs)
    per = N // col_slabs // tn

    def a_block(i, core_ref):
        half = (1 - core_ref[0]) if other else core_ref[0]
        return (i // nh) * (2 * nh) + half * nh + i % nh

    def out_index(i, j):
        return (j // per, i, j % per) if col_slabs > 1 else (i // nh, i % nh, j)

    slabs = max(row_slabs, col_slabs)
    out_def = ((slabs, H, N // col_slabs), BF16, (1, tm, tn), out_index)

    def epilogue(acc, ex, outs):
        outs[0][0] = (acc if recv is None else acc + ex[0][0].astype(F32)).astype(BF16)

    extras = ([] if recv is None else [(recv, (1, tm, tn), out_index)]) + ([] if after is None else [_behind(after)])
    return _matmul(name, a, b, "tn", [out_def], epilogue, extras=extras, tm=tm, tn=tn,
                   row_sel=(core, a_block, row_slabs * nh))[0]


def _behind(token):
    return (token, (8, LANES), lambda i, j: (0, 0))


def _mm_plain(name, a, b, mode, out_dtype, after=None, **tiles):
    if mode == "nn":
        M, N = a.shape[0], b.shape[-1] * (b.shape[0] if b.ndim == 3 else 1)
    elif mode == "nt":
        M, N = a.shape[0], b.shape[-2]
    else:
        M, N = a.shape[1], b.shape[1]
    tm, tn = _fit(tiles.get("tm", MM_TM), M), _fit(tiles.get("tn", MM_TN), N)

    def epi(acc, ex, outs):
        outs[0][...] = acc.astype(out_dtype)

    return _matmul(name, a, b, mode, [((M, N), out_dtype, (tm, tn), lambda i, j: (i, j))], epi,
                   extras=[] if after is None else [_behind(after)], **tiles)[0]


def _rstd(v):
    return lax.rsqrt(jnp.mean(v * v, axis=-1, keepdims=True) + NORM_EPS)


def _row_call(name, body, row_ins, vec_ins, row_outs, acc_outs, S, D, tr):
    tr = _fit(tr, S)
    row_spec = pl.BlockSpec((tr, D), lambda r: (r, 0))
    vec_spec = pl.BlockSpec((1, D), lambda r: (0, 0))
    in_specs = [row_spec] * len(row_ins) + [vec_spec] * len(vec_ins)
    out_specs = [row_spec] * len(row_outs) + [pl.BlockSpec(shp, lambda r: (0, 0)) for shp in acc_outs]
    out_shape = [jax.ShapeDtypeStruct((S, D), d) for d in row_outs] + [jax.ShapeDtypeStruct(shp, F32) for shp in acc_outs]
    need = sum(2 * tr * D * a.dtype.itemsize for a in row_ins) + sum(2 * tr * D * jnp.dtype(d).itemsize for d in row_outs)
    need += 8 * tr * D * 4
    return pl.pallas_call(
        body, name=name, grid=(S // tr,), in_specs=in_specs, out_specs=out_specs, out_shape=out_shape,
        compiler_params=pltpu.CompilerParams(dimension_semantics=("arbitrary",), vmem_limit_bytes=_vmem(need)),
    )(*row_ins, *vec_ins)


def _acc_rows(ref, rows):
    @pl.when(pl.program_id(0) == 0)
    def _():
        ref[...] = jnp.zeros_like(ref)
    for n, r in enumerate(rows):
        ref[n:n + 1, :] += r


def _pre_norm(x, g, sc, sh):
    S, D = x.shape

    def body(x_ref, g_ref, sc_ref, sh_ref, h_ref):
        xv = x_ref[...]
        xn = xv * _rstd(xv)
        h_ref[...] = (xn * g_ref[...] * (1.0 + sc_ref[...]) + sh_ref[...]).astype(BF16)

    return _row_call("pre_norm_mix", body, [x], [g, sc, sh], [BF16], [], S, D, 256)[0]


def _post_mix(x, mix, g_post, gt, g_pre, sc, sh):
    S, D = x.shape

    def body(x_ref, mix_ref, gp_ref, gt_ref, g2_ref, sc_ref, sh_ref, x1_ref, h2_ref):
        mv = mix_ref[...].astype(F32)
        x1 = x_ref[...] + gt_ref[...] * (mv * _rstd(mv) * gp_ref[...])
        x1_ref[...] = x1
        h2_ref[...] = (x1 * _rstd(x1) * g2_ref[...] * (1.0 + sc_ref[...]) + sh_ref[...]).astype(BF16)

    return _row_call("post_mix_pre_mlp", body, [x, mix], [g_post, gt, g_pre, sc, sh], [F32, BF16], [], S, D, 256)


def _loss_and_post_mlp_bwd(x1, y, target, g_post, gt):
    S, D = x1.shape

    def body(x1_ref, y_ref, t_ref, g_ref, gt_ref, dy_ref, dout_ref, loss_ref, acc_ref):
        yv = y_ref[...].astype(F32)
        r = _rstd(yv)
        yh = yv * r
        n = yh * g_ref[...]
        diff = x1_ref[...] + gt_ref[...] * n - t_ref[...]
        dout = diff * (1.0 / D)
        dout_ref[...] = dout
        dn = dout * gt_ref[...]
        dyh = dn * g_ref[...]
        dy_ref[...] = (r * (dyh - yh * jnp.mean(dyh * yh, axis=-1, keepdims=True))).astype(BF16)
        _acc_rows(acc_ref, [jnp.sum(dout * n, axis=0, keepdims=True), jnp.sum(dn * yh, axis=0, keepdims=True)])

        @pl.when(pl.program_id(0) == 0)
        def _():
            loss_ref[...] = jnp.zeros_like(loss_ref)
        loss_ref[...] += jnp.full(loss_ref.shape, (0.5 / D) * jnp.sum(diff * diff), F32)

    return _row_call("loss_post_mlp_bwd", body, [x1, y, target], [g_post, gt], [BF16, F32],
                     [(8, LANES), (8, D)], S, D, 128)


def _pre_mlp_and_post_mix_bwd(dh2, x1, dout, mix, g_pre, sc, g_post, gt):
    S, D = x1.shape

    def body(dh_ref, x1_ref, dout_ref, mix_ref, g_ref, sc_ref, gp_ref, gt_ref, dx1_ref, dmix_ref, acc_ref):
        dh = dh_ref[...].astype(F32)
        x1v = x1_ref[...]
        r3 = _rstd(x1v)
        xn = x1v * r3
        dxn = dh * (1.0 + sc_ref[...]) * g_ref[...]
        dx1 = dout_ref[...] + r3 * (dxn - xn * jnp.mean(dxn * xn, axis=-1, keepdims=True))
        dx1_ref[...] = dx1
        mv = mix_ref[...].astype(F32)
        r2 = _rstd(mv)
        mh = mv * r2
        dn = dx1 * gt_ref[...]
        dmh = dn * gp_ref[...]
        dmix_ref[...] = (r2 * (dmh - mh * jnp.mean(dmh * mh, axis=-1, keepdims=True))).astype(BF16)
        _acc_rows(acc_ref, [
            jnp.sum(dh, axis=0, keepdims=True),
            jnp.sum(dh * xn * g_ref[...], axis=0, keepdims=True),
            jnp.sum(dh * (1.0 + sc_ref[...]) * xn, axis=0, keepdims=True),
            jnp.sum(dx1 * mh * gp_ref[...], axis=0, keepdims=True),
            jnp.sum(dn * mh, axis=0, keepdims=True)])

    return _row_call("pre_mlp_post_mix_bwd", body, [dh2, x1, dout, mix], [g_pre, sc, g_post, gt], [F32, BF16],
                     [(8, D)], S, D, 128)


def _pre_mix_bwd(dh, x, dx1, g_pre, sc):
    S, D = x.shape

    def body(dh_ref, x_ref, dx1_ref, g_ref, sc_ref, gx_ref, acc_ref):
        dhv = dh_ref[...].astype(F32)
        xv = x_ref[...]
        r = _rstd(xv)
        xn = xv * r
        dxn = dhv * (1.0 + sc_ref[...]) * g_ref[...]
        gx_ref[...] = dx1_ref[...] + r * (dxn - xn * jnp.mean(dxn * xn, axis=-1, keepdims=True))
        _acc_rows(acc_ref, [
            jnp.sum(dhv, axis=0, keepdims=True),
            jnp.sum(dhv * xn * g_ref[...], axis=0, keepdims=True),
            jnp.sum(dhv * (1.0 + sc_ref[...]) * xn, axis=0, keepdims=True)])

    return _row_call("pre_mix_bwd", body, [dh, x, dx1], [g_pre, sc], [F32], [(8, D)], S, D, 128)


CUM_BLOCK = 256


def _tri(n, upper):
    r = lax.broadcasted_iota(jnp.int32, (n, n), 0)
    c = lax.broadcasted_iota(jnp.int32, (n, n), 1)
    return ((c >= r) if upper else (c <= r)).astype(F32)


def _fox_gate_fwd(fg, b_pad):
    S = fg.shape[0]
    cb = _fit(CUM_BLOCK, S)

    def body(fg_ref, b_ref, cumt_ref, cum_ref):
        low = _tri(cb, False)
        carry = jnp.zeros((1, LANES), F32)
        for n in range(S // cb):
            z = fg_ref[n * cb:(n + 1) * cb, :] + b_ref[...]
            logf = jnp.minimum(z, 0.0) - jnp.log(1.0 + jnp.exp(-jnp.abs(z)))
            blk = jnp.dot(low, logf, precision=lax.Precision.HIGHEST, preferred_element_type=F32) + carry
            cum_ref[n * cb:(n + 1) * cb, :] = blk
            carry = blk[cb - 1:cb, :]
        cumt_ref[...] = cum_ref[...].T

    return pl.pallas_call(
        body, name="fox_gate_fwd", out_shape=jax.ShapeDtypeStruct((LANES, S), F32),
        scratch_shapes=[pltpu.VMEM((S, LANES), F32)],
        compiler_params=pltpu.CompilerParams(vmem_limit_bytes=_vmem(6 * S * LANES * 4)),
    )(fg, b_pad)


def _fox_gate_bwd(dcum_k, dcum_q, fg, b_pad):
    S = fg.shape[0]
    n_fox = dcum_q.shape[0]
    cb = _fit(CUM_BLOCK, S)

    def body(dk_ref, dq_ref, fg_ref, b_ref, dfg_ref, db_ref, dc_ref):
        lane = lax.broadcasted_iota(jnp.int32, (S, LANES), 1)
        dc = dk_ref[...].T
        for h in range(n_fox):
            dc = dc + jnp.where(lane == h, dq_ref[h], 0.0)
        dc_ref[...] = dc
        up = _tri(cb, True)
        carry = jnp.zeros((1, LANES), F32)
        db = jnp.zeros((1, LANES), F32)
        for n in reversed(range(S // cb)):
            blk = jnp.dot(up, dc_ref[n * cb:(n + 1) * cb, :], precision=lax.Precision.HIGHEST,
                          preferred_element_type=F32) + carry
            carry = blk[0:1, :]
            z = fg_ref[n * cb:(n + 1) * cb, :] + b_ref[...]
            dfg = blk * (1.0 / (1.0 + jnp.exp(z)))
            dfg_ref[n * cb:(n + 1) * cb, :] = dfg.astype(BF16)
            db = db + jnp.sum(dfg, axis=0, keepdims=True)
        db_ref[...] = jnp.broadcast_to(db, db_ref.shape)

    return pl.pallas_call(
        body, name="fox_gate_bwd",
        out_shape=[jax.ShapeDtypeStruct((S, LANES), BF16), jax.ShapeDtypeStruct((8, LANES), F32)],
        scratch_shapes=[pltpu.VMEM((S, LANES), F32)],
        compiler_params=pltpu.CompilerParams(vmem_limit_bytes=_vmem((8 + 2 * n_fox) * S * LANES * 4)),
    )(dcum_k, dcum_q, fg, b_pad)


FOX_TILE = 512


LOG2E = 1.4426950408889634


def _fox_scores(q, k, ck2, masked, t):
    s = lax.dot_general(q, k, _NT, preferred_element_type=F32) * (HEAD_DIM ** -0.5 * LOG2E) - ck2
    if masked:
        row = lax.broadcasted_iota(jnp.int32, (t, t), 0)
        col = lax.broadcasted_iota(jnp.int32, (t, t), 1)
        s = jnp.where(col <= row, s, NEG)
    return s


def _fox_fwd(proj, cum_row, n_fox):
    S = proj.shape[0]
    t = _fit(FOX_TILE, S)
    nq = S // t

    def body(q_ref, k_ref, v_ref, ck_ref, o_ref, lse_ref):
        def q_block(qi, _):
            q0 = pl.multiple_of(qi * t, t)
            q = q_ref[pl.ds(q0, t), :]

            def kv_block(j, carry, masked):
                m, l, acc = carry
                k0 = pl.multiple_of(j * t, t)
                s = _fox_scores(q, k_ref[pl.ds(k0, t), :], ck_ref[0, :, pl.ds(k0, t)] * LOG2E, masked, t)
                m_new = jnp.maximum(m, jnp.max(s, axis=-1, keepdims=True))
                alpha = jnp.exp2(m - m_new)
                p = jnp.exp2(s - m_new)
                l = alpha * l + jnp.sum(p, axis=-1, keepdims=True)
                acc = alpha * acc + jnp.dot(p.astype(BF16), v_ref[pl.ds(k0, t), :], preferred_element_type=F32)
                return m_new, l, acc

            init = (jnp.full((t, 1), NEG, F32), jnp.zeros((t, 1), F32), jnp.zeros((t, HEAD_DIM), F32))
            carry = lax.fori_loop(0, qi, lambda j, cr: kv_block(j, cr, False), init)
            m, l, acc = kv_block(qi, carry, True)
            o_ref[pl.ds(q0, t), :] = acc / l
            lse_ref[0, pl.ds(q0, t), :] = jnp.broadcast_to(m + jnp.log(l) * LOG2E, (t, LANES))
            return 0

        lax.fori_loop(0, nq, q_block, 0)

    col = lambda off: pl.BlockSpec((S, HEAD_DIM), lambda h: (0, off + h))
    per_head = pl.BlockSpec((1, S, LANES), lambda h: (h, 0, 0))
    return pl.pallas_call(
        body, name="fox_fwd", grid=(n_fox,),
        in_specs=[col(0), col(n_fox), col(2 * n_fox), pl.BlockSpec((1, 1, S), lambda h: (h, 0, 0))],
        out_specs=[pl.BlockSpec((S, HEAD_DIM), lambda h: (0, h)), per_head],
        out_shape=[jax.ShapeDtypeStruct((S, n_fox * HEAD_DIM), F32), jax.ShapeDtypeStruct((n_fox, S, LANES), F32)],
        compiler_params=pltpu.CompilerParams(dimension_semantics=("parallel",),
                                             vmem_limit_bytes=_vmem(16 * S * HEAD_DIM * 4 + 12 * t * t * 4)),
    )(proj, proj, proj, cum_row)


def _fox_bwd(proj, o, do, lse_b, cum_row, n_fox):
    S = proj.shape[0]
    t = _fit(FOX_TILE, S)
    nq = S // t
    scale = HEAD_DIM ** -0.5

    def body(q_ref, k_ref, v_ref, o_ref, do_ref, lse_ref, ck_ref, dq_ref, dk_ref, dv_ref, dc_ref, dcq_ref,
             dq_acc, delta_ref):
        dq_acc[...] = jnp.zeros_like(dq_acc)
        dcq_ref[...] = jnp.zeros_like(dcq_ref)

        def delta_block(qi, _):
            q0 = pl.multiple_of(qi * t, t)
            d = jnp.sum(do_ref[pl.ds(q0, t), :] * o_ref[pl.ds(q0, t), :], axis=-1, keepdims=True)
            delta_ref[pl.ds(q0, t), :] = jnp.broadcast_to(d, (t, LANES))
            return 0

        lax.fori_loop(0, nq, delta_block, 0)

        def kv_block(j, _):
            k0 = pl.multiple_of(j * t, t)
            k = k_ref[pl.ds(k0, t), :]
            v = v_ref[pl.ds(k0, t), :]
            ck2 = ck_ref[0, :, pl.ds(k0, t)] * LOG2E

            def q_block(qi, carry, masked):
                dk, dv, dc = carry
                q0 = pl.multiple_of(qi * t, t)
                q = q_ref[pl.ds(q0, t), :]
                dov = do_ref[pl.ds(q0, t), :].astype(BF16)
                p = jnp.exp2(_fox_scores(q, k, ck2, masked, t) - lse_ref[0, pl.ds(q0, t), :][:, :1])
                dp = lax.dot_general(dov, v, _NT, preferred_element_type=F32)
                ds = p * (dp - delta_ref[pl.ds(q0, t), :][:, :1])
                dsb = ds.astype(BF16)
                dv = dv + lax.dot_general(p.astype(BF16), dov, _TN, preferred_element_type=F32)
                dk = dk + lax.dot_general(dsb, q, _TN, preferred_element_type=F32)
                dq_acc[pl.ds(q0, t), :] += jnp.dot(dsb, k, preferred_element_type=F32)
                dc = dc - jnp.sum(ds, axis=0, keepdims=True)
                dcq_ref[0, pl.ds(q0, t), :] += jnp.broadcast_to(jnp.sum(ds, axis=1, keepdims=True), (t, LANES))
                return dk, dv, dc

            init = (jnp.zeros((t, HEAD_DIM), F32), jnp.zeros((t, HEAD_DIM), F32), jnp.zeros((1, t), F32))
            carry = q_block(j, init, True)
            dk, dv, dc = lax.fori_loop(j + 1, nq, lambda qi, cr: q_block(qi, cr, False), carry)
            dk_ref[pl.ds(k0, t), :] = (dk * scale).astype(BF16)
            dv_ref[pl.ds(k0, t), :] = dv.astype(BF16)
            dc_ref[0, :, pl.ds(k0, t)] = dc
            return 0

        lax.fori_loop(0, nq, kv_block, 0)
        dq_ref[...] = (dq_acc[...] * scale).astype(BF16)

    col = lambda off: pl.BlockSpec((S, HEAD_DIM), lambda h: (0, off + h))
    per_head = pl.BlockSpec((1, S, LANES), lambda h: (h, 0, 0))
    row = pl.BlockSpec((1, 1, S), lambda h: (h, 0, 0))
    grad = jax.ShapeDtypeStruct((S, n_fox * HEAD_DIM), BF16)
    return pl.pallas_call(
        body, name="fox_bwd", grid=(n_fox,),
        in_specs=[col(0), col(n_fox), col(2 * n_fox), col(0), col(0), per_head, row],
        out_specs=[col(0), col(0), col(0), row, per_head],
        out_shape=[grad, grad, grad, jax.ShapeDtypeStruct((n_fox, 1, S), F32), jax.ShapeDtypeStruct((n_fox, S, LANES), F32)],
        scratch_shapes=[pltpu.VMEM((S, HEAD_DIM), F32), pltpu.VMEM((S, LANES), F32)],
        compiler_params=pltpu.CompilerParams(dimension_semantics=("parallel",),
                                             vmem_limit_bytes=_vmem(24 * S * HEAD_DIM * 4 + 16 * t * t * 4)),
    )(proj, proj, proj, o, do, lse_b, cum_row)


def _rope_tables(S):
    half = HEAD_DIM // 2
    inv_freq = 1.0 / (ROPE_THETA ** (jnp.arange(half, dtype=F32) * (2.0 / HEAD_DIM)))
    ang = jnp.arange(S).astype(F32)[:, None] * inv_freq[None, :]
    cos, sin = jnp.cos(ang), jnp.sin(ang)
    return jnp.concatenate([cos, cos], axis=-1), jnp.concatenate([-sin, sin], axis=-1)


def _rope(name, src, first_block, n_blocks, cos, sin_signed):
    S = src.shape[0]

    def body(x_ref, cos_ref, sin_ref, o_ref):
        xv = x_ref[...].astype(F32)
        o_ref[...] = (xv * cos_ref[...] + pltpu.roll(xv, HEAD_DIM // 2, 1) * sin_ref[...]).astype(BF16)

    table = pl.BlockSpec((S, HEAD_DIM), lambda n: (0, 0))
    return pl.pallas_call(
        body, name=name, grid=(n_blocks,),
        in_specs=[pl.BlockSpec((S, HEAD_DIM), lambda n: (0, first_block + n)), table, table],
        out_specs=pl.BlockSpec((S, HEAD_DIM), lambda n: (0, n)),
        out_shape=jax.ShapeDtypeStruct((S, n_blocks * HEAD_DIM), BF16),
        compiler_params=pltpu.CompilerParams(dimension_semantics=("parallel",),
                                             vmem_limit_bytes=_vmem(12 * S * HEAD_DIM * 4)),
    )(src, cos, sin_signed)


def _swa_tile(q_ref, kp_ref, kc_ref, n, group, scale):
    B = SWA_BLOCK
    qs = jnp.concatenate([q_ref[:, g * HEAD_DIM:(g + 1) * HEAD_DIM] for g in range(group)], axis=0)
    kcat = jnp.concatenate([kp_ref[...], kc_ref[...]], axis=0)
    s = lax.dot_general(qs, kcat, _NT, preferred_element_type=F32) * scale
    qi = lax.broadcasted_iota(jnp.int32, (group * B, 2 * B), 0) % B
    kj = lax.broadcasted_iota(jnp.int32, (group * B, 2 * B), 1)
    diff = qi + B - kj
    mask = (diff >= 0) & (diff < B) & ((n * B + kj - B) >= 0)
    return qs, kcat, jnp.where(mask, s, NEG)


def _swa_sink_col(sink_ref, kv, group):
    head = lax.broadcasted_iota(jnp.int32, (group * SWA_BLOCK, 1), 0) // SWA_BLOCK
    col = jnp.zeros((group * SWA_BLOCK, 1), F32)
    for g in range(group):
        col = jnp.where(head == g, sink_ref[kv * group + g], col)
    return col


def _swa_specs(n_kv, group, q_first, k_first, v_first):
    B = SWA_BLOCK
    prev = lambda n: jnp.maximum(n - 1, 0)
    return [
        pl.BlockSpec((B, group * HEAD_DIM), lambda kv, n: (n, q_first + kv)),
        pl.BlockSpec((B, HEAD_DIM), lambda kv, n: (prev(n), k_first + kv)),
        pl.BlockSpec((B, HEAD_DIM), lambda kv, n: (n, k_first + kv)),
        pl.BlockSpec((B, HEAD_DIM), lambda kv, n: (prev(n), v_first + kv)),
        pl.BlockSpec((B, HEAD_DIM), lambda kv, n: (n, v_first + kv)),
    ]


def _swa_fwd(rq, proj, v_first, sinks, n_q, n_kv):
    S = rq.shape[0]
    B = SWA_BLOCK
    group = n_q // n_kv
    scale = HEAD_DIM ** -0.5

    def body(q_ref, kp_ref, kc_ref, vp_ref, vc_ref, sink_ref, o_ref, lse_ref):
        kv, n = pl.program_id(0), pl.program_id(1)
        _, _, s = _swa_tile(q_ref, kp_ref, kc_ref, n, group, scale)
        sink = _swa_sink_col(sink_ref, kv, group)
        m = jnp.maximum(jnp.max(s, axis=-1, keepdims=True), sink)
        p = jnp.exp(s - m)
        denom = jnp.sum(p, axis=-1, keepdims=True) + jnp.exp(sink - m)
        vcat = jnp.concatenate([vp_ref[...], vc_ref[...]], axis=0)
        o = jnp.dot((p / denom).astype(BF16), vcat, preferred_element_type=F32)
        lse = m + jnp.log(denom)
        for g in range(group):
            o_ref[:, g * HEAD_DIM:(g + 1) * HEAD_DIM] = o[g * B:(g + 1) * B, :]
            lse_ref[0, :, g * LANES:(g + 1) * LANES] = jnp.broadcast_to(lse[g * B:(g + 1) * B, :], (B, LANES))

    specs = _swa_specs(n_kv, group, 0, n_q, v_first)
    q_blk = pl.BlockSpec((B, group * HEAD_DIM), lambda kv, n: (n, kv))
    return pl.pallas_call(
        body, name="swa_fwd", grid=(n_kv, S // B),
        in_specs=specs + [pl.BlockSpec(memory_space=pltpu.SMEM)],
        out_specs=[q_blk, pl.BlockSpec((1, B, group * LANES), lambda kv, n: (kv, n, 0))],
        out_shape=[jax.ShapeDtypeStruct((S, n_q * HEAD_DIM), F32), jax.ShapeDtypeStruct((n_kv, S, group * LANES), F32)],
        compiler_params=pltpu.CompilerParams(dimension_semantics=("parallel", "arbitrary")),
    )(rq, rq, rq, proj, proj, sinks)


def _swa_bwd(rq, proj, v_first, sinks, o, do, do_first, lse_b, n_q, n_kv):
    S = rq.shape[0]
    B = SWA_BLOCK
    group = n_q // n_kv
    scale = HEAD_DIM ** -0.5

    def body(q_ref, kp_ref, kc_ref, vp_ref, vc_ref, o_ref, do_ref, lse_ref, sink_ref,
             dq_ref, dk_ref, dv_ref, dsink_ref):
        kv, n = pl.program_id(0), pl.program_id(1)

        @pl.when(n == 0)
        def _():
            dk_ref[...] = jnp.zeros_like(dk_ref)
            dv_ref[...] = jnp.zeros_like(dv_ref)
            dsink_ref[...] = jnp.zeros_like(dsink_ref)

        qs, kcat, s = _swa_tile(q_ref, kp_ref, kc_ref, n, group, scale)
        sink = _swa_sink_col(sink_ref, kv, group)
        stack = lambda ref, w: jnp.concatenate([ref[:, g * w:(g + 1) * w] for g in range(group)], axis=0)
        lse = jnp.concatenate([lse_ref[0, :, g * LANES:g * LANES + 1] for g in range(group)], axis=0)
        do32 = stack(do_ref, HEAD_DIM)
        delta = jnp.sum(do32 * stack(o_ref, HEAD_DIM), axis=-1, keepdims=True)
        dov = do32.astype(BF16)
        p = jnp.exp(s - lse)
        vcat = jnp.concatenate([vp_ref[...], vc_ref[...]], axis=0)
        dp = lax.dot_general(dov, vcat, _NT, preferred_element_type=F32)
        ds = p * (dp - delta)
        dsb = ds.astype(BF16)
        dq = jnp.dot(dsb, kcat, preferred_element_type=F32) * scale
        for g in range(group):
            dq_ref[:, g * HEAD_DIM:(g + 1) * HEAD_DIM] = dq[g * B:(g + 1) * B, :].astype(BF16)
        dkcat = lax.dot_general(dsb, qs, _TN, preferred_element_type=F32) * scale
        dvcat = lax.dot_general(p.astype(BF16), dov, _TN, preferred_element_type=F32)
        prev0 = pl.multiple_of(jnp.maximum(n - 1, 0) * B, B)
        cur0 = pl.multiple_of(n * B, B)
        dk_ref[0, pl.ds(prev0, B), :] += dkcat[:B, :]
        dk_ref[0, pl.ds(cur0, B), :] += dkcat[B:, :]
        dv_ref[0, pl.ds(prev0, B), :] += dvcat[:B, :]
        dv_ref[0, pl.ds(cur0, B), :] += dvcat[B:, :]
        dsk = -jnp.exp(sink - lse) * delta
        lane = lax.broadcasted_iota(jnp.int32, (1, LANES), 1)
        row = jnp.zeros((1, LANES), F32)
        for g in range(group):
            row = row + jnp.where(lane == g, jnp.sum(dsk[g * B:(g + 1) * B, :]), 0.0)
        dsink_ref[0, 0:1, :] += row

    specs = _swa_specs(n_kv, group, 0, n_q, v_first)
    q_blk = pl.BlockSpec((B, group * HEAD_DIM), lambda kv, n: (n, kv))
    acc = pl.BlockSpec((1, S, HEAD_DIM), lambda kv, n: (kv, 0, 0))
    return pl.pallas_call(
        body, name="swa_bwd", grid=(n_kv, S // B),
        in_specs=specs + [q_blk, pl.BlockSpec((B, group * HEAD_DIM), lambda kv, n: (n, do_first + kv)),
                          pl.BlockSpec((1, B, group * LANES), lambda kv, n: (kv, n, 0)),
                          pl.BlockSpec(memory_space=pltpu.SMEM)],
        out_specs=[q_blk, acc, acc, pl.BlockSpec((1, 8, LANES), lambda kv, n: (kv, 0, 0))],
        out_shape=[jax.ShapeDtypeStruct((S, n_q * HEAD_DIM), BF16), jax.ShapeDtypeStruct((n_kv, S, HEAD_DIM), F32),
                   jax.ShapeDtypeStruct((n_kv, S, HEAD_DIM), F32), jax.ShapeDtypeStruct((n_kv, 8, LANES), F32)],
        compiler_params=pltpu.CompilerParams(dimension_semantics=("parallel", "arbitrary")),
    )(rq, rq, rq, proj, proj, o, do, lse_b, sinks)


def _adamw(w, g, m, v):
    m = ADAM_B1 * m + (1.0 - ADAM_B1) * g
    v = ADAM_B2 * v + (1.0 - ADAM_B2) * (g * g)
    m_hat = m / (1.0 - ADAM_B1 ** ADAM_STEP)
    v_hat = v / (1.0 - ADAM_B2 ** ADAM_STEP)
    delta = -ADAM_LR * (m_hat / (jnp.sqrt(v_hat) + ADAM_EPS) + ADAM_WD * w)
    return delta, m, v


def _mod_fwd(cond_in, w_mod, b_shard):
    R, D = cond_in.shape
    cols = w_mod.shape[1]
    tn = _fit(512, cols)

    def body(c_ref, w_ref, b_ref, o_ref):
        cv = c_ref[...]
        cond = (cv / (1.0 + jnp.exp(-cv))).astype(BF16)
        o_ref[...] = jnp.dot(cond, w_ref[...].astype(BF16), preferred_element_type=F32) + b_ref[...]

    return pl.pallas_call(
        body, name="mod_fwd", grid=(cols // tn,),
        in_specs=[pl.BlockSpec((R, D), lambda j: (0, 0)), pl.BlockSpec((D, tn), lambda j: (0, j)),
                  pl.BlockSpec((1, tn), lambda j: (0, j))],
        out_specs=pl.BlockSpec((R, tn), lambda j: (0, j)),
        out_shape=jax.ShapeDtypeStruct((R, cols), F32),
        compiler_params=pltpu.CompilerParams(dimension_semantics=("parallel",), vmem_limit_bytes=_vmem(3 * D * tn * 4)),
    )(cond_in, w_mod, b_shard)


def _mod_update(c_t, dmod, w, m, v):
    D, nb = c_t.shape
    cols = w.shape[1]
    tn = _fit(256, cols)

    def body(c_ref, d_ref, w_ref, m_ref, v_ref, g_ref, dl_ref, nm_ref, nv_ref):
        cv = c_ref[...]
        cond = cv / (1.0 + jnp.exp(-cv))
        g = jnp.zeros((D, tn), F32)
        for b in range(nb):
            g = g + cond[:, b:b + 1] * d_ref[b:b + 1, :]
        g_ref[...] = g
        dl_ref[...], nm_ref[...], nv_ref[...] = _adamw(w_ref[...], g, m_ref[...], v_ref[...])

    blk = pl.BlockSpec((D, tn), lambda j: (0, j))
    out = jax.ShapeDtypeStruct((D, cols), F32)
    return pl.pallas_call(
        body, name="mod_update", grid=(cols // tn,),
        in_specs=[pl.BlockSpec((D, nb), lambda j: (0, 0)), pl.BlockSpec((nb, tn), lambda j: (0, j)), blk, blk, blk],
        out_specs=[blk] * 4, out_shape=[out] * 4,
        compiler_params=pltpu.CompilerParams(dimension_semantics=("parallel",), vmem_limit_bytes=_vmem(18 * D * tn * 4)),
    )(c_t, dmod, w, m, v)


def _small_update(stacked, w, m, v):
    R, C = w.shape

    def body(s_ref, w_ref, m_ref, v_ref, g_ref, dl_ref, nm_ref, nv_ref):
        g = s_ref[0:R, :]
        for d in range(1, N_DEV):
            g = g + s_ref[d * R:(d + 1) * R, :]
        g_ref[...] = g
        dl_ref[...], nm_ref[...], nv_ref[...] = _adamw(w_ref[...], g, m_ref[...], v_ref[...])

    return pl.pallas_call(body, name="small_update", out_shape=[jax.ShapeDtypeStruct((R, C), F32)] * 4)(stacked, w, m, v)


def _place():
    return lax.axis_index("x"), lax.axis_index("y"), lax.axis_index("c")


def _allgather8(name, block):
    m_per, n = block.shape

    def body(x_ref, out_ref, token_ref, send_sems, recv_sems, local_sem):
        token_ref[...] = jnp.zeros_like(token_ref)
        x, y, c = _place()
        me, sibling = (x, y, c), (x, y, 1 - c)
        chips = [(1 - x, y), (x, 1 - y), (1 - x, 1 - y)]

        def rows(px, py, pc):
            return out_ref.at[pl.ds((4 * px + 2 * py + pc) * m_per, m_per), :]

        def copy(k, blk, to, src=None):
            return pltpu.make_async_remote_copy(
                src_ref=rows(*blk) if src is None else src, dst_ref=rows(*blk),
                send_sem=send_sems.at[k], recv_sem=recv_sems.at[k], device_id=to, device_id_type=MESH)

        mine = pltpu.make_async_copy(x_ref, rows(*me), local_sem)
        mine.start()
        first = [copy(0, me, sibling, src=x_ref)]
        first += [copy(1 + j, me, (*chip, c), src=x_ref) for j, chip in enumerate(chips)]
        for cp in first:
            cp.start()
        passed = [copy(4 + j, (*chip, c), sibling) for j, chip in enumerate(chips)]
        for j, chip in enumerate(chips):
            copy(1 + j, (*chip, c), me).wait_recv()
            passed[j].start()
        copy(0, sibling, me).wait_recv()
        for j, chip in enumerate(chips):
            copy(4 + j, (*chip, 1 - c), me).wait_recv()
        for cp in first + passed:
            cp.wait_send()
        mine.wait()

    vmem = pl.BlockSpec(memory_space=pltpu.VMEM)
    return pl.pallas_call(
        body, name=name,
        out_shape=[jax.ShapeDtypeStruct((N_DEV * m_per, n), block.dtype), jax.ShapeDtypeStruct((8, LANES), F32)],
        in_specs=[vmem], out_specs=[vmem, vmem],
        scratch_shapes=[pltpu.SemaphoreType.DMA((7,)), pltpu.SemaphoreType.DMA((7,)), pltpu.SemaphoreType.DMA],
    )(block)


_ANY = pl.BlockSpec(memory_space=pl.ANY)


def _half(ref, c, rows):
    return ref.at[pl.ds(c * (rows // 2), rows // 2), :]


_HBM = pl.BlockSpec(memory_space=pltpu.HBM)
_SEM = pl.BlockSpec(memory_space=pltpu.SEMAPHORE)
_EFFECT = pltpu.SideEffectType.DATAFLOW_SIDE_EFFECTING


def _ici_start(name, srcs, land_shapes, plan, per_source=3, after=None):
    ns, nl = len(srcs), len(land_shapes)
    n_copies = per_source * ns
    n_in = ns + nl + (after is not None)

    def body(*refs):
        src_refs, land_refs = refs[:ns], refs[ns:ns + nl]
        send_sems, recv_sems = refs[n_in], refs[n_in + 1]
        token = refs[-1]
        for n, (src, dst, peer, _) in enumerate(plan(src_refs, land_refs)):
            pltpu.make_async_remote_copy(src_ref=src, dst_ref=dst, send_sem=send_sems.at[n], recv_sem=recv_sems.at[n],
                                         device_id=peer, device_id_type=MESH).start()
        token[...] = jnp.zeros_like(token)

    lands = [lax.empty(s.shape, s.dtype) for s in land_shapes]
    out = pl.pallas_call(
        body, name=name,
        out_shape=(pltpu.SemaphoreType.DMA((n_copies,)), pltpu.SemaphoreType.DMA((n_copies,)),
                   *[pltpu.HBM(a.shape, a.dtype) for a in list(srcs) + lands], jax.ShapeDtypeStruct((8, LANES), F32)),
        in_specs=[_HBM] * (ns + nl) + [_ANY] * (after is not None),
        out_specs=(_SEM, _SEM, *[_HBM] * (ns + nl), pl.BlockSpec(memory_space=pltpu.VMEM)),
        input_output_aliases={n: 2 + n for n in range(ns + nl)},
        compiler_params=pltpu.CompilerParams(has_side_effects=_EFFECT),
    )(*[pltpu.with_memory_space_constraint(a, pltpu.HBM) for a in list(srcs) + lands],
      *([] if after is None else [after]))
    return out[0], out[1], list(out[2:2 + ns]), list(out[2 + ns:2 + ns + nl]), out[-1]


def _ici_wait(name, send_sems, recv_sems, srcs, lands, plan, after):
    ns, nl = len(srcs), len(lands)
    after = list(after) if isinstance(after, (list, tuple)) else [after]

    def body(*refs):
        src_refs, land_refs = refs[:ns], refs[ns:ns + nl]
        send_sems, recv_sems = refs[ns + nl], refs[ns + nl + 1]
        for n, (src, _, peer, mine) in enumerate(plan(src_refs, land_refs)):
            cp = pltpu.make_async_remote_copy(src_ref=src, dst_ref=mine, send_sem=send_sems.at[n],
                                              recv_sem=recv_sems.at[n], device_id=peer, device_id_type=MESH)
            cp.wait_send()
            cp.wait_recv()

    out = pl.pallas_call(
        body, name=name, out_shape=[pltpu.HBM(a.shape, a.dtype) for a in list(srcs) + list(lands)],
        in_specs=[_HBM] * (ns + nl) + [_SEM, _SEM] + [_ANY] * len(after), out_specs=[_HBM] * (ns + nl),
        input_output_aliases={n: n for n in range(ns + nl)},
        compiler_params=pltpu.CompilerParams(has_side_effects=_EFFECT),
    )(*srcs, *lands, send_sems, recv_sems, *after)
    return list(out[:ns]), list(out[ns:])


def _own_slab(name, chip, w, after):
    R, C = w.shape
    tr, tc = _tiles(R, C)
    tied = [] if after is None else [after]

    def body(chip_ref, w_ref, *rest):
        stack_ref, token_ref = rest[-2:]
        stack_ref[0] = w_ref[...].astype(BF16)
        token_ref[...] = jnp.zeros_like(token_ref)

    small = pl.BlockSpec((8, LANES), lambda r, q, chip_ref: (0, 0))
    grid_spec = pltpu.PrefetchScalarGridSpec(
        num_scalar_prefetch=1, grid=(R // tr, C // tc),
        in_specs=[pl.BlockSpec((tr, tc), lambda r, q, chip_ref: (r, q))] + [small] * len(tied),
        out_specs=[pl.BlockSpec((1, tr, tc), lambda r, q, chip_ref: (chip_ref[0], r, q)), small])
    return pl.pallas_call(
        body, name=name, grid_spec=grid_spec,
        out_shape=[jax.ShapeDtypeStruct((N_CHIPS, R, C), BF16), jax.ShapeDtypeStruct((8, LANES), F32)],
        compiler_params=pltpu.CompilerParams(dimension_semantics=("arbitrary", "arbitrary")),
    )(chip, w, *tied)


def _gather_plan(src_refs, land_refs):
    x, y, c = _place()
    copies = []
    for stack in src_refs:
        R = stack.shape[1]
        own = _half(stack.at[2 * x + y], c, R)
        for cx, cy in [(1 - x, y), (x, 1 - y), (1 - x, 1 - y)]:
            copies.append((own, own, (cx, cy, c), _half(stack.at[2 * cx + cy], c, R)))
    return copies


def _pass_plan(src_refs, land_refs):
    x, y, c = _place()
    copies = []
    for land in src_refs:
        R = land.shape[1]
        for cx, cy in [(1 - x, y), (x, 1 - y), (1 - x, 1 - y)]:
            slot = land.at[2 * cx + cy]
            copies.append((_half(slot, c, R), _half(slot, c, R), (x, y, 1 - c), _half(slot, 1 - c, R)))
    return copies


def _share_plan(src_refs, land_refs):
    x, y, c = _place()
    return [(h, land, (x, y, 1 - c), land) for h, land in zip(src_refs, land_refs)]


def _pass_to_sibling(name, lands):
    nw = len(lands)

    def body(*refs):
        ins, outs = refs[:nw], refs[nw:2 * nw]
        send_sems, recv_sems = refs[2 * nw:]
        x, y, c = _place()
        chips = [(1 - x, y), (x, 1 - y), (1 - x, 1 - y)]
        copies = []
        for k in range(nw):
            R = ins[k].shape[1]
            for j, (cx, cy) in enumerate(chips):
                cp = pltpu.make_async_remote_copy(
                    src_ref=_half(ins[k].at[2 * cx + cy], c, R), dst_ref=_half(outs[k].at[2 * cx + cy], c, R),
                    send_sem=send_sems.at[3 * k + j], recv_sem=recv_sems.at[3 * k + j],
                    device_id=(x, y, 1 - c), device_id_type=MESH)
                cp.start()
                copies.append(cp)
        for k in range(nw):
            R = ins[k].shape[1]
            for j, (cx, cy) in enumerate(chips):
                pltpu.make_async_remote_copy(
                    src_ref=_half(ins[k].at[2 * cx + cy], c, R), dst_ref=_half(outs[k].at[2 * cx + cy], 1 - c, R),
                    send_sem=send_sems.at[3 * k + j], recv_sem=recv_sems.at[3 * k + j],
                    device_id=(x, y, 1 - c), device_id_type=MESH).wait_recv()
        for cp in copies:
            cp.wait_send()

    return pl.pallas_call(
        body, name=name, out_shape=[jax.ShapeDtypeStruct(a.shape, a.dtype) for a in lands],
        in_specs=[_ANY] * nw, out_specs=[_ANY] * nw, input_output_aliases={k: k for k in range(nw)},
        scratch_shapes=[pltpu.SemaphoreType.DMA((3 * nw,)), pltpu.SemaphoreType.DMA((3 * nw,))],
    )(*lands)


def _tie(vec, token):
    return vec + token[0:1, 0:1]


ROW_ALIGN = 16
TILE_ELEMS = 512 * 1024


def _tiles(rows, cols):
    fits = [t for t in range(ROW_ALIGN, min(rows, 256) + 1, ROW_ALIGN) if rows % t == 0]
    tr = fits[-1] if fits and fits[-1] >= 64 else rows
    tc = cols
    while tr * tc > TILE_ELEMS and tc % (2 * LANES) == 0:
        tc //= 2
    return tr, tc


def _scatter_plan(src_refs, land_refs):
    x, y, c = _place()
    copies = []
    for p, land in zip(src_refs, land_refs):
        for j, (cx, cy) in enumerate([(1 - x, y), (x, 1 - y), (1 - x, 1 - y)]):
            copies.append((p.at[2 * cx + cy], land.at[j], (cx, cy, c), land.at[j]))
    return copies


def _chip_add(name, chip, sums, recv):
    _, H, C = sums.shape
    tr, tc = _tiles(H, C)

    def body(chip_ref, p_ref, r_ref, o_ref):
        total = p_ref[0].astype(F32)
        for j in range(3):
            total = total + r_ref[j].astype(F32)
        o_ref[...] = total

    grid_spec = pltpu.PrefetchScalarGridSpec(
        num_scalar_prefetch=1, grid=(H // tr, C // tc),
        in_specs=[pl.BlockSpec((1, tr, tc), lambda r, q, chip_ref: (chip_ref[0], r, q)),
                  pl.BlockSpec((3, tr, tc), lambda r, q, chip_ref: (0, r, q))],
        out_specs=pl.BlockSpec((tr, tc), lambda r, q, chip_ref: (r, q)))
    return pl.pallas_call(
        body, name=name, grid_spec=grid_spec, out_shape=jax.ShapeDtypeStruct((H, C), F32),
        compiler_params=pltpu.CompilerParams(dimension_semantics=("parallel", "parallel")),
    )(chip, sums, recv)


def _pair_share(name, halves):
    nw = len(halves)

    def body(*refs):
        hs, outs = refs[:nw], refs[nw:2 * nw]
        send_sems, recv_sems = refs[2 * nw:]
        x, y, c = _place()
        copies = []
        for k in range(nw):
            cp = pltpu.make_async_remote_copy(
                src_ref=hs[k], dst_ref=outs[k], send_sem=send_sems.at[k], recv_sem=recv_sems.at[k],
                device_id=(x, y, 1 - c), device_id_type=MESH)
            cp.start()
            copies.append(cp)
        for cp in copies:
            cp.wait()

    return pl.pallas_call(
        body, name=name,
        out_shape=[jax.ShapeDtypeStruct(h.shape, h.dtype) for h in halves],
        in_specs=[_ANY] * nw, out_specs=[_ANY] * nw,
        scratch_shapes=[pltpu.SemaphoreType.DMA((nw,)), pltpu.SemaphoreType.DMA((nw,))],
    )(*halves)


def _adam_halves(name, core, w, g_own, g_other, m, v):
    R, C = w.shape
    H = R // 2
    tr, tc = _tiles(H, C)
    nr, nc = H // tr, C // tc

    def body(core_ref, w_ref, go_ref, gr_ref, m_ref, v_ref, g_ref, dl_ref, nm_ref, nv_ref):
        own = (pl.program_id(0) // nr) == core_ref[0]
        g = jnp.where(own, go_ref[...], gr_ref[...])
        g_ref[...] = g
        dl_ref[...], nm_ref[...], nv_ref[...] = _adamw(w_ref[...], g, m_ref[...], v_ref[...])

    blk = pl.BlockSpec((tr, tc), lambda r, q, core_ref: (r, q))

    def half_spec(is_own):
        def index(r, q, core_ref):
            mine = ((r // nr) == core_ref[0]) == is_own
            done = is_own == (core_ref[0] == 0)
            return (jnp.where(mine, r % nr, jnp.where(done, nr - 1, 0)), jnp.where(mine, q, jnp.where(done, nc - 1, 0)))
        return pl.BlockSpec((tr, tc), index)
    out = jax.ShapeDtypeStruct((R, C), F32)
    grid_spec = pltpu.PrefetchScalarGridSpec(
        num_scalar_prefetch=1, grid=(R // tr, nc), in_specs=[blk, half_spec(True), half_spec(False), blk, blk],
        out_specs=[blk] * 4)
    return pl.pallas_call(
        body, name=name, grid_spec=grid_spec, out_shape=[out] * 4,
        compiler_params=pltpu.CompilerParams(dimension_semantics=("parallel", "parallel"),
                                             vmem_limit_bytes=_vmem(20 * tr * tc * 4)),
    )(core, w, g_own, g_other, m, v)


def kernel(x, c, w_mod, b_mod, g_pre_mix, g_post_mix, w_in, b_forget, swa_sinks, w_out, g_pre_mlp, g_post_mlp, w_up, w_down, loss_target, m_w_mod, m_b_mod, m_g_pre_mix, m_g_post_mix, m_w_in, m_b_forget, m_swa_sinks, m_w_out, m_g_pre_mlp, m_g_post_mlp, m_w_up, m_w_down, v_w_mod, v_b_mod, v_g_pre_mix, v_g_post_mix, v_w_in, v_b_forget, v_swa_sinks, v_w_out, v_g_pre_mlp, v_g_post_mlp, v_w_up, v_w_down):
    S, D = x.shape[1], x.shape[2]
    n_heads = D // HEAD_DIM
    n_fox = n_heads // 2
    n_swa = n_heads - n_fox
    n_kv = max(1, n_swa // 4)
    fox_w, swa_w, kv_w = n_fox * HEAD_DIM, n_swa * HEAD_DIM, n_kv * HEAD_DIM
    main_w = 3 * fox_w + swa_w + 2 * kv_w
    in_w = main_w + n_fox
    mod_cols = w_mod.shape[2]

    ax, ay, ac = _place()
    chip = 2 * ax + ay
    dev = 2 * chip + ac
    chip_arr = jnp.reshape(chip, (1,)).astype(jnp.int32)
    core_arr = jnp.reshape(ac, (1,)).astype(jnp.int32)

    x2, tgt = x[0], loss_target[0]

    in_rows = in_w // N_CHIPS
    in_rows_pad = -(-in_rows // (2 * LANES)) * (2 * LANES)
    slab_w = N_CHIPS * in_rows_pad

    def rows_of(a):
        return jnp.pad(a[0].T, ((0, in_rows_pad - in_rows), (0, 0)))

    w_in_stack, token = _own_slab("own_slab_w_in", chip_arr, rows_of(w_in), None)

    c_all, _ = _allgather8("gather_c", _tie(c, token).reshape(8, D // 8))
    c_all = c_all.reshape(N_DEV, D)
    b_shard = lax.dynamic_slice_in_dim(b_mod, chip * mod_cols, mod_cols, axis=1)
    mod_shard = _mod_fwd(jnp.pad(c_all, ((0, 16 - N_DEV), (0, 0))), w_mod[0], b_shard)[:N_DEV]
    mod_all, token = _allgather8("gather_mod", mod_shard)
    mod_all = mod_all.reshape(N_CHIPS, 2, N_DEV, mod_cols)[:, 0]
    mod = lax.dynamic_index_in_dim(mod_all, dev, axis=1, keepdims=False).reshape(N_MOD, 1, D)
    sh_a, sc_a, gt_a, sh_m, sc_m, gt_m = [mod[n] for n in range(N_MOD)]

    def slab_cols(lo, hi):
        spans = []
        while lo < hi:
            s, r = divmod(lo, in_rows)
            n = min(hi - lo, in_rows - r)
            spans.append((s * in_rows_pad + r, s * in_rows_pad + r + n))
            lo += n
        return spans

    gate_lo = 3 * fox_w
    main_spans = slab_cols(0, gate_lo) + slab_cols(gate_lo + n_fox, in_w)
    (gate_first, gate_last), = slab_cols(gate_lo, gate_lo + n_fox)

    names = ["w_in", "w_out", "w_up", "w_down"]
    flights = {}
    for n, w in zip(names, [None, w_out[0], w_up[0], w_down[0]]):
        stack = w_in_stack if n == "w_in" else _own_slab("own_slab_" + n, chip_arr, w, token)[0]
        flights[n] = _ici_start("gather_start_" + n, [stack], [], _gather_plan, after=token)
        token = flights[n][4]
    sc_a = _tie(sc_a, token)

    def arrived(n, after):
        send, recv, stacks, _, _ = flights[n]
        stacks, _ = _ici_wait("gather_wait_" + n, send, recv, stacks, [], _gather_plan, after)
        return _ici_start("gather_pass_start_" + n, stacks, [], _pass_plan)

    def gathered(n, after, in_flight=None):
        if in_flight is None:
            send, recv, stacks, _, _ = flights[n]
            stacks, _ = _ici_wait("gather_wait_" + n, send, recv, stacks, [], _gather_plan, after)
            return _pass_to_sibling("gather_pass_" + n, stacks)[0]
        send, recv, stacks, _, _ = in_flight
        return _ici_wait("gather_pass_wait_" + n, send, recv, stacks, [], _pass_plan, after)[0][0]

    d_ff = N_CHIPS * w_up.shape[2]

    h = _pre_norm(x2, g_pre_mix, sc_a, sh_a)
    in_state = [rows_of(w_in)] + [rows_of(_tie(a, token)) for a in (m_w_in, v_w_in)]
    cos, sin_signed = _rope_tables(S)

    def pack(bm, gpm, gqm, gpl, gql, bf, sk):
        last = jnp.concatenate([bf, sk, jnp.zeros((1, D - n_fox - n_swa), F32)], axis=1)
        return jnp.concatenate([bm.reshape(N_MOD, D), gpm, gqm, gpl, gql, last, jnp.zeros((5, D), F32)], axis=0)

    small_state = [pack(b_mod, g_pre_mix, g_post_mix, g_pre_mlp, g_post_mlp, b_forget, swa_sinks),
                   pack(m_b_mod, m_g_pre_mix, m_g_post_mix, m_g_pre_mlp, m_g_post_mlp, m_b_forget, m_swa_sinks),
                   pack(v_b_mod, v_g_pre_mix, v_g_post_mix, v_g_pre_mlp, v_g_post_mlp, v_b_forget, v_swa_sinks)]
    ready = h[:8, :LANES].astype(F32) + cos[:8]
    w_slab_t = gathered("w_in", [ready] + in_state[1:] + small_state).reshape(slab_w, D)
    tm_p, tn_p = _fit(MM_TM, S), _fit(MM_TN if slab_w % MM_TN == 0 else MM_TN // 2, slab_w)
    win0 = gate_first // LANES * LANES
    win_j, win_off = divmod(win0, tn_p)
    assert win_off + 2 * LANES <= tn_p and gate_last - win0 <= 2 * LANES

    def proj_epilogue(acc, ex, outs):
        outs[0][...] = acc.astype(BF16)

        @pl.when(pl.program_id(1) == win_j)
        def _():
            outs[1][...] = acc[:, win_off:win_off + 2 * LANES]

    proj_slab, gate_win = _matmul(
        "in_proj", h, w_slab_t, "nt",
        [((S, slab_w), BF16, (tm_p, tn_p), lambda i, j: (i, j)), ((S, 2 * LANES), F32, (tm_p, 2 * LANES), lambda i, j: (i, 0))],
        proj_epilogue, tn=tn_p, revisits=True)
    proj = jnp.concatenate([proj_slab[:, lo:hi] for lo, hi in main_spans], axis=1)
    out_flight = arrived("w_out", proj_slab)
    fg = _tie(jnp.pad(gate_win[:, gate_first - win0:gate_last - win0], ((0, 0), (0, LANES - n_fox))), out_flight[4])
    b_pad = jnp.pad(b_forget, ((0, 0), (0, LANES - n_fox)))
    cum_row = _fox_gate_fwd(fg, b_pad)[:n_fox].reshape(n_fox, 1, S)
    fox_o, fox_lse = _fox_fwd(proj, cum_row, n_fox)

    rq = _rope("rope_fwd", proj, 3 * n_fox, n_swa + n_kv, cos, sin_signed)
    v_first = 3 * n_fox + n_swa + n_kv
    sinks = swa_sinks[0]
    swa_o, swa_lse = _swa_fwd(rq, proj, v_first, sinks, n_swa, n_kv)

    mixcat = jnp.concatenate([fox_o, swa_o], axis=1).astype(BF16)
    up_flight = arrived("w_up", mixcat)
    w_out_f = gathered("w_out", mixcat, out_flight).reshape(D, D)
    mix = _mm_plain("out_proj", mixcat, w_out_f, "nn", BF16, after=up_flight[4])
    x1, h2 = _post_mix(x2, mix, g_post_mix, gt_a, g_pre_mlp, sc_m, sh_m)
    w_up_f = gathered("w_up", h2, up_flight)

    tm_u, tn_u = _fit(MM_TM, S), _fit(MM_TN, d_ff)

    def up_epilogue(acc, ex, outs):
        outs[0][...] = acc.astype(BF16)
        r = jnp.maximum(acc, 0.0)
        outs[1][...] = (r * r).astype(BF16)

    ublk = ((S, d_ff), BF16, (tm_u, tn_u), lambda i, j: (i, j))
    u, a = _matmul("mlp_up", h2, w_up_f, "nn", [ublk, ublk], up_epilogue)
    w_down_f = gathered("w_down", a).reshape(d_ff, D)
    y = _mm_plain("mlp_down", a, w_down_f, "nn", BF16)

    dy, dout, loss_part, acc_mlp_post = _loss_and_post_mlp_bwd(x1, y, tgt, g_post_mlp, gt_m)

    def du_epilogue(acc, ex, outs):
        outs[0][...] = (acc * (2.0 * jnp.maximum(ex[0][...].astype(F32), 0.0))).astype(BF16)

    du = _matmul("mlp_down_bwd", dy, w_down_f, "nt", [ublk], du_epilogue,
                 extras=[(u, (tm_u, tn_u), lambda i, j: (i, j))])[0]
    def pair_send(tag, part):
        return _ici_start("grad_pair_start_" + tag, [part], [jax.ShapeDtypeStruct(part.shape, BF16)], _share_plan,
                          per_source=1)

    def pair_recv(tag, flight, after):
        send, recv, srcs, lands, _ = flight
        return _ici_wait("grad_pair_wait_" + tag, send, recv, srcs, lands, _share_plan, after)[1][0]

    def scatter_start(tag, sums):
        return _ici_start("grad_scatter_start_" + tag, sums,
                          [jax.ShapeDtypeStruct((3,) + p.shape[1:], BF16) for p in sums], _scatter_plan)

    def scatter_finish(tag, flight, after):
        send, recv, srcs, lands, _ = flight
        sums, received = _ici_wait("grad_scatter_wait_" + tag, send, recv, srcs, lands, _scatter_plan, after)
        return [_chip_add("chip_add_%s_%d" % (tag, k), chip_arr, p, r) for k, (p, r) in enumerate(zip(sums, received))]

    tm_g = _fit(MM_TM, D // 2)
    pair_down = pair_send("down", _grad_half("grad_w_down_a", core_arr, a, dy, N_CHIPS, 1, tm_g, True))
    pair_up = pair_send("up", _grad_half("grad_w_up_a", core_arr, h2, du, 1, N_CHIPS, tm_g, True, after=pair_down[4]))
    sum_down = _grad_half("grad_w_down_b", core_arr, a, dy, N_CHIPS, 1, tm_g, False,
                          recv=pair_recv("down", pair_down, pair_up[4]))
    sum_up = _grad_half("grad_w_up_b", core_arr, h2, du, 1, N_CHIPS, tm_g, False, recv=pair_recv("up", pair_up, sum_down))
    flight_mlp = scatter_start("mlp", [sum_up, sum_down])
    dh2 = _mm_plain("mlp_up_bwd", du, w_up_f, "nt", BF16, after=flight_mlp[4])
    dx1, dmix, acc_mid = _pre_mlp_and_post_mix_bwd(dh2, x1, dout, mix, _tie(g_pre_mlp, flight_mlp[4]), sc_m,
                                                   g_post_mix, gt_a)

    dmixcat = _mm_plain("out_proj_bwd", dmix, w_out_f, "nt", F32)

    fdq, fdk, fdv, dcum_row, dcum_q = _fox_bwd(proj, fox_o, dmixcat, fox_lse, cum_row, n_fox)
    dcum_k = jnp.pad(dcum_row.reshape(n_fox, S), ((0, LANES - n_fox), (0, 0)))
    dfg, db_forget = _fox_gate_bwd(dcum_k, dcum_q, fg, b_pad)

    group_w = (n_swa // n_kv) * HEAD_DIM
    sdq, sdk, sdv, dsink = _swa_bwd(rq, proj, v_first, sinks, swa_o, dmixcat, fox_w // group_w, swa_lse, n_swa, n_kv)
    drq = jnp.concatenate([sdq, jnp.transpose(sdk, (1, 0, 2)).reshape(S, kv_w).astype(BF16)], axis=1)
    d_sq_sk = _rope("rope_bwd", drq, 0, n_swa + n_kv, cos, -sin_signed)
    dsv = jnp.transpose(sdv, (1, 0, 2)).reshape(S, kv_w).astype(BF16)
    dproj = jnp.concatenate([fdq, fdk, fdv, d_sq_sk, dsv], axis=1)

    pieces = []
    for s in range(N_CHIPS):
        lo, hi = s * in_rows, (s + 1) * in_rows
        for src, first, last, shift in [(dproj, 0, gate_lo, 0), (dfg, gate_lo, gate_lo + n_fox, gate_lo),
                                        (dproj, gate_lo + n_fox, in_w, n_fox)]:
            if max(lo, first) < min(hi, last):
                pieces.append(src[:, max(lo, first) - shift:min(hi, last) - shift])
        pieces.append(jnp.zeros((S, in_rows_pad - in_rows), BF16))
    dproj_slab = jnp.concatenate(pieces, axis=1)

    tm_in, tm_out = in_rows_pad // 2, D // (2 * N_CHIPS)
    pair_in = pair_send("in", _grad_half("grad_w_in_a", core_arr, dproj_slab, h, N_CHIPS, 1, tm_in, True))
    pair_out = pair_send("out", _grad_half("grad_w_out_a", core_arr, mixcat, dmix, N_CHIPS, 1, tm_out, True,
                                           after=pair_in[4]))
    sum_in = _grad_half("grad_w_in_b", core_arr, dproj_slab, h, N_CHIPS, 1, tm_in, False,
                        recv=pair_recv("in", pair_in, pair_out[4]))
    dh = _mm_plain("in_proj_bwd", dproj_slab, w_slab_t, "nn", BF16, tk=slab_w // 2)
    grad_x, acc_pre = _pre_mix_bwd(dh, x2, dx1, g_pre_mix, sc_a)

    zero_row = jnp.zeros((1, D), F32)
    tail = jnp.concatenate([db_forget[0:1, :n_fox], dsink[:, 0, :n_swa // n_kv].reshape(1, n_swa),
                            loss_part[0:1, 0:1], jnp.zeros((1, D - n_fox - n_swa - 1), F32)], axis=1)
    partial = jnp.concatenate([
        acc_pre[0:1], acc_pre[1:2], acc_mid[3:4], acc_mid[0:1], acc_mid[1:2], acc_mlp_post[0:1],
        acc_pre[2:3], acc_mid[4:5], acc_mid[2:3], acc_mlp_post[1:2], tail] + [zero_row] * 5, axis=0)
    gathered_small, token = _allgather8("gather_small_grads", partial)

    sum_out = _grad_half("grad_w_out_b", core_arr, mixcat, dmix, N_CHIPS, 1, tm_out, False,
                         recv=pair_recv("out", pair_out, token))
    flight_mix = scatter_start("mix", [sum_in, sum_out])
    halves_mlp = scatter_finish("mlp", flight_mlp, flight_mix[4])
    share_mlp = _ici_start("grad_share_start_mlp", halves_mlp,
                           [jax.ShapeDtypeStruct(hv.shape, F32) for hv in halves_mlp], _share_plan, per_source=1)

    def unpack(p):
        return {"b_mod": p[0:N_MOD].reshape(1, N_MOD * D), "g_pre_mix": p[6:7], "g_post_mix": p[7:8],
                "g_pre_mlp": p[8:9], "g_post_mlp": p[9:10], "b_forget": p[10:11, :n_fox],
                "swa_sinks": p[10:11, n_fox:n_fox + n_swa]}

    small_out = _small_update(gathered_small, _tie(small_state[0], share_mlp[4]), small_state[1], small_state[2])
    g_small, d_small, m_small, v_small = [unpack(p) for p in small_out]
    loss = small_out[0][N_MOD + 4, n_fox + n_swa]

    dmod_all = gathered_small.reshape(N_DEV, 16, D)[:, :N_MOD].reshape(N_DEV, N_MOD * D)
    dmod_shard = _tie(lax.dynamic_slice_in_dim(dmod_all, chip * mod_cols, mod_cols, axis=1), share_mlp[4])
    g_w_mod, d_w_mod, nm_w_mod, nv_w_mod = _mod_update(c_all.T, dmod_shard, w_mod[0], m_w_mod[0], v_w_mod[0])
    send, recv, halves_mlp, lands, _ = share_mlp
    halves_mlp, others_mlp = _ici_wait("grad_share_wait_mlp", send, recv, halves_mlp, lands, _share_plan,
                                       d_w_mod[:8, :LANES] + small_out[1][:8, :LANES])

    grads = dict(g_small, w_mod=g_w_mod[None])
    deltas = dict(d_small, w_mod=d_w_mod[None])
    new_m = dict(m_small, w_mod=nm_w_mod[None])
    new_v = dict(v_small, w_mod=nv_w_mod[None])
    weights = {"w_in": (w_in, m_w_in, v_w_in), "w_out": (w_out, m_w_out, v_w_out), "w_up": (w_up, m_w_up, v_w_up),
               "w_down": (w_down, m_w_down, v_w_down)}

    def big_update(n, own, other):
        transposed = n == "w_in"
        w, m, v = in_state if transposed else [a[0] for a in weights[n]]
        outs = _adam_halves("adam_" + n, core_arr, w, own, other, m, v)
        if transposed:
            outs = [o[:in_rows].T for o in outs]
        grads[n], deltas[n], new_m[n], new_v[n] = [o[None] for o in outs]

    big_update("w_up", halves_mlp[0], others_mlp[0])
    big_update("w_down", halves_mlp[1], others_mlp[1])
    ran = deltas["w_down"][0, :8, :LANES] + deltas["w_up"][0, :8, :LANES] + d_w_mod[:8, :LANES]
    halves_mix = scatter_finish("mix", flight_mix, ran)
    others_mix = _pair_share("grad_pair_share_mix", halves_mix)
    big_update("w_in", halves_mix[0], others_mix[0])
    big_update("w_out", halves_mix[1], others_mix[1])

    order = ["w_mod", "b_mod", "g_pre_mix", "g_post_mix", "w_in", "b_forget", "swa_sinks", "w_out", "g_pre_mlp",
             "g_post_mlp", "w_up", "w_down"]
    return (loss, grad_x[None], *[grads[n] for n in order], *[deltas[n] for n in order],
            *[new_m[n] for n in order], *[new_v[n] for n in order])
```

```python
import jax
import jax.numpy as jnp
from jax import lax
from jax.experimental import pallas as pl
from jax.experimental.pallas import tpu as pltpu

F32 = jnp.float32
BF16 = jnp.bfloat16
MESH = pl.DeviceIdType.MESH

HEAD_DIM = 128
SWA_BLOCK = 128
ROPE_THETA = 10000.0
NORM_EPS = 1e-6
NEG = -1e30
N_MOD = 6
ADAM_LR = 0.001
ADAM_B1 = 0.9
ADAM_B2 = 0.999
ADAM_EPS = 1e-08
ADAM_WD = 0.01
ADAM_STEP = 10
N_CHIPS = 4
N_DEV = 8
LANES = 128
VMEM_CAP = 60 * 1024 * 1024

_NN = (((1,), (0,)), ((), ()))
_NT = (((1,), (1,)), ((), ()))
_TN = (((0,), (0,)), ((), ()))


def _vmem(nbytes):
    return int(min(VMEM_CAP, nbytes * 5 // 4 + (4 << 20)))


def _nbytes(shape, dtype):
    n = 1
    for s in shape:
        n *= s
    return n * jnp.dtype(dtype).itemsize


def _fit(t, n):
    t = min(t, n)
    assert n % t == 0, (t, n)
    return t


MM_TM, MM_TN, MM_TK = 1024, 1024, 2048


def _matmul(name, a, b, mode, out_defs, epilogue, extras=(), tm=MM_TM, tn=MM_TN, tk=MM_TK, revisits=False,
            row_sel=None):
    stacked = b.ndim == 3
    b_rows, b_cols = b.shape[-2], b.shape[-1] * (b.shape[0] if stacked else 1)
    if mode == "nn":
        (M, K), (K2, N) = a.shape, (b_rows, b_cols)
    elif mode == "nt":
        (M, K), (N, K2) = a.shape, (b_rows, b_cols)
    else:
        (K, M), (K2, N) = a.shape, (b_rows, b_cols)
    assert K == K2 and not (stacked and mode == "tn"), (a.shape, b.shape, mode)
    tm = _fit(tm, M)
    tn = _fit(tn, b.shape[-1] if stacked and mode == "nn" else N)
    tk = _fit(tk, b.shape[-1] if stacked and mode == "nt" else K)
    nk = K // tk
    dims = {"nn": _NN, "nt": _NT, "tn": _TN}[mode]
    if row_sel is None:
        grid_m, a_row = M // tm, lambda i, *sel: i
    else:
        grid_m, a_row = row_sel[2], lambda i, *sel: row_sel[1](i, sel[0])
    a_spec = (pl.BlockSpec((tk, tm), lambda i, j, k, *sel: (k, a_row(i, *sel))) if mode == "tn"
              else pl.BlockSpec((tm, tk), lambda i, j, k, *sel: (a_row(i, *sel), k)))
    if stacked:
        per = b.shape[-1] // (tk if mode == "nt" else tn)
        b_spec = (pl.BlockSpec((1, tn, tk), lambda i, j, k, *sel: (k // per, j, k % per)) if mode == "nt"
                  else pl.BlockSpec((1, tk, tn), lambda i, j, k, *sel: (j // per, k, j % per)))
    else:
        b_spec = (pl.BlockSpec((tn, tk), lambda i, j, k, *sel: (j, k)) if mode == "nt"
                  else pl.BlockSpec((tk, tn), lambda i, j, k, *sel: (k, j)))
    n_ex, n_out = len(extras), len(out_defs)

    def body(*refs):
        if row_sel is not None:
            refs = refs[1:]
        a_ref, b_ref = refs[0], refs[1]
        ex = refs[2:2 + n_ex]
        outs = refs[2 + n_ex:2 + n_ex + n_out]
        b_blk = b_ref[0] if stacked else b_ref[...]
        prod = lax.dot_general(a_ref[...], b_blk, dims, preferred_element_type=F32)
        if nk == 1:
            epilogue(prod, ex, outs)
        else:
            acc_ref = refs[-1]
            k = pl.program_id(2)

            @pl.when(k == 0)
            def _():
                acc_ref[...] = prod

            @pl.when(k > 0)
            def _():
                acc_ref[...] += prod

            @pl.when(k == nk - 1)
            def _():
                epilogue(acc_ref[...], ex, outs)

    def wrap(f):
        return lambda i, j, k, *sel: f(i, j)

    in_specs = [a_spec, b_spec] + [pl.BlockSpec(blk, wrap(f)) for _, blk, f in extras]
    out_specs = [pl.BlockSpec(blk, wrap(f)) for _, _, blk, f in out_defs]
    out_shape = [jax.ShapeDtypeStruct(s, d) for s, d, _, _ in out_defs]
    need = 2 * (tm * tk + tk * tn) * a.dtype.itemsize + 3 * tm * tn * 4
    need += sum(2 * _nbytes(blk, arr.dtype) for arr, blk, _ in extras)
    need += sum(2 * _nbytes(blk, d) for _, d, blk, _ in out_defs)
    grid = (grid_m, N // tn, nk)
    scratch = [pltpu.VMEM((tm, tn), F32)] if nk > 1 else []
    params = pltpu.CompilerParams(
        dimension_semantics=("parallel", "arbitrary" if revisits else "parallel", "arbitrary"),
        vmem_limit_bytes=_vmem(need))
    operands = (a, b, *[arr for arr, _, _ in extras])
    if row_sel is None:
        return pl.pallas_call(body, name=name, grid=grid, in_specs=in_specs, out_specs=out_specs, out_shape=out_shape,
                              scratch_shapes=scratch, compiler_params=params)(*operands)
    grid_spec = pltpu.PrefetchScalarGridSpec(num_scalar_prefetch=1, grid=grid, in_specs=in_specs, out_specs=out_specs,
                                             scratch_shapes=scratch)
    return pl.pallas_call(body, name=name, grid_spec=grid_spec, out_shape=out_shape,
                          compiler_params=params)(row_sel[0], *operands)


def _grad_half(name, core, a, b, row_slabs, col_slabs, tm, other, recv=None, after=None):
    (_, M), (_, N) = a.shape, b.shape
    H = M // (2 * row_slabs)
    nh = H // tm
    tn = _fit(MM_TN, N // col_slabs)
    per = N // col_slabs // tn

    def a_block(i, core_ref):
        half = (1 - core_ref[0]) if other else core_ref[0]
        return (i // nh) * (2 * nh) + half * nh + i % nh

    def out_index(i, j):
        return (j // per, i, j % per) if col_slabs > 1 else (i // nh, i % nh, j)

    slabs = max(row_slabs, col_slabs)
    out_def = ((slabs, H, N // col_slabs), BF16, (1, tm, tn), out_index)

    def epilogue(acc, ex, outs):
        outs[0][0] = (acc if recv is None else acc + ex[0][0].astype(F32)).astype(BF16)

    extras = ([] if recv is None else [(recv, (1, tm, tn), out_index)]) + ([] if after is None else [_behind(after)])
    return _matmul(name, a, b, "tn", [out_def], epilogue, extras=extras, tm=tm, tn=tn,
                   row_sel=(core, a_block, row_slabs * nh))[0]


def _behind(token):
    return (token, (8, LANES), lambda i, j: (0, 0))


def _mm_plain(name, a, b, mode, out_dtype, after=None, **tiles):
    if mode == "nn":
        M, N = a.shape[0], b.shape[-1] * (b.shape[0] if b.ndim == 3 else 1)
    elif mode == "nt":
        M, N = a.shape[0], b.shape[-2]
    else:
        M, N = a.shape[1], b.shape[1]
    tm, tn = _fit(tiles.get("tm", MM_TM), M), _fit(tiles.get("tn", MM_TN), N)

    def epi(acc, ex, outs):
        outs[0][...] = acc.astype(out_dtype)

    return _matmul(name, a, b, mode, [((M, N), out_dtype, (tm, tn), lambda i, j: (i, j))], epi,
                   extras=[] if after is None else [_behind(after)], **tiles)[0]


def _rstd(v):
    return lax.rsqrt(jnp.mean(v * v, axis=-1, keepdims=True) + NORM_EPS)


def _row_call(name, body, row_ins, vec_ins, row_outs, acc_outs, S, D, tr):
    tr = _fit(tr, S)
    row_spec = pl.BlockSpec((tr, D), lambda r: (r, 0))
    vec_spec = pl.BlockSpec((1, D), lambda r: (0, 0))
    in_specs = [row_spec] * len(row_ins) + [vec_spec] * len(vec_ins)
    out_specs = [row_spec] * len(row_outs) + [pl.BlockSpec(shp, lambda r: (0, 0)) for shp in acc_outs]
    out_shape = [jax.ShapeDtypeStruct((S, D), d) for d in row_outs] + [jax.ShapeDtypeStruct(shp, F32) for shp in acc_outs]
    need = sum(2 * tr * D * a.dtype.itemsize for a in row_ins) + sum(2 * tr * D * jnp.dtype(d).itemsize for d in row_outs)
    need += 8 * tr * D * 4
    return pl.pallas_call(
        body, name=name, grid=(S // tr,), in_specs=in_specs, out_specs=out_specs, out_shape=out_shape,
        compiler_params=pltpu.CompilerParams(dimension_semantics=("arbitrary",), vmem_limit_bytes=_vmem(need)),
    )(*row_ins, *vec_ins)


def _acc_rows(ref, rows):
    @pl.when(pl.program_id(0) == 0)
    def _():
        ref[...] = jnp.zeros_like(ref)
    for n, r in enumerate(rows):
        ref[n:n + 1, :] += r


def _pre_norm(x, g, sc, sh):
    S, D = x.shape

    def body(x_ref, g_ref, sc_ref, sh_ref, h_ref):
        xv = x_ref[...]
        xn = xv * _rstd(xv)
        h_ref[...] = (xn * g_ref[...] * (1.0 + sc_ref[...]) + sh_ref[...]).astype(BF16)

    return _row_call("pre_norm_mix", body, [x], [g, sc, sh], [BF16], [], S, D, 256)[0]


def _post_mix(x, mix, g_post, gt, g_pre, sc, sh):
    S, D = x.shape

    def body(x_ref, mix_ref, gp_ref, gt_ref, g2_ref, sc_ref, sh_ref, x1_ref, h2_ref):
        mv = mix_ref[...].astype(F32)
        x1 = x_ref[...] + gt_ref[...] * (mv * _rstd(mv) * gp_ref[...])
        x1_ref[...] = x1
        h2_ref[...] = (x1 * _rstd(x1) * g2_ref[...] * (1.0 + sc_ref[...]) + sh_ref[...]).astype(BF16)

    return _row_call("post_mix_pre_mlp", body, [x, mix], [g_post, gt, g_pre, sc, sh], [F32, BF16], [], S, D, 256)


def _loss_and_post_mlp_bwd(x1, y, target, g_post, gt):
    S, D = x1.shape

    def body(x1_ref, y_ref, t_ref, g_ref, gt_ref, dy_ref, dout_ref, loss_ref, acc_ref):
        yv = y_ref[...].astype(F32)
        r = _rstd(yv)
        yh = yv * r
        n = yh * g_ref[...]
        diff = x1_ref[...] + gt_ref[...] * n - t_ref[...]
        dout = diff * (1.0 / D)
        dout_ref[...] = dout
        dn = dout * gt_ref[...]
        dyh = dn * g_ref[...]
        dy_ref[...] = (r * (dyh - yh * jnp.mean(dyh * yh, axis=-1, keepdims=True))).astype(BF16)
        _acc_rows(acc_ref, [jnp.sum(dout * n, axis=0, keepdims=True), jnp.sum(dn * yh, axis=0, keepdims=True)])

        @pl.when(pl.program_id(0) == 0)
        def _():
            loss_ref[...] = jnp.zeros_like(loss_ref)
        loss_ref[...] += jnp.full(loss_ref.shape, (0.5 / D) * jnp.sum(diff * diff), F32)

    return _row_call("loss_post_mlp_bwd", body, [x1, y, target], [g_post, gt], [BF16, F32],
                     [(8, LANES), (8, D)], S, D, 128)


def _pre_mlp_and_post_mix_bwd(dh2, x1, dout, mix, g_pre, sc, g_post, gt):
    S, D = x1.shape

    def body(dh_ref, x1_ref, dout_ref, mix_ref, g_ref, sc_ref, gp_ref, gt_ref, dx1_ref, dmix_ref, acc_ref):
        dh = dh_ref[...].astype(F32)
        x1v = x1_ref[...]
        r3 = _rstd(x1v)
        xn = x1v * r3
        dxn = dh * (1.0 + sc_ref[...]) * g_ref[...]
        dx1 = dout_ref[...] + r3 * (dxn - xn * jnp.mean(dxn * xn, axis=-1, keepdims=True))
        dx1_ref[...] = dx1
        mv = mix_ref[...].astype(F32)
        r2 = _rstd(mv)
        mh = mv * r2
        dn = dx1 * gt_ref[...]
        dmh = dn * gp_ref[...]
        dmix_ref[...] = (r2 * (dmh - mh * jnp.mean(dmh * mh, axis=-1, keepdims=True))).astype(BF16)
        _acc_rows(acc_ref, [
            jnp.sum(dh, axis=0, keepdims=True),
            jnp.sum(dh * xn * g_ref[...], axis=0, keepdims=True),
            jnp.sum(dh * (1.0 + sc_ref[...]) * xn, axis=0, keepdims=True),
            jnp.sum(dx1 * mh * gp_ref[...], axis=0, keepdims=True),
            jnp.sum(dn * mh, axis=0, keepdims=True)])

    return _row_call("pre_mlp_post_mix_bwd", body, [dh2, x1, dout, mix], [g_pre, sc, g_post, gt], [F32, BF16],
                     [(8, D)], S, D, 128)


def _pre_mix_bwd(dh, x, dx1, g_pre, sc):
    S, D = x.shape

    def body(dh_ref, x_ref, dx1_ref, g_ref, sc_ref, gx_ref, acc_ref):
        dhv = dh_ref[...].astype(F32)
        xv = x_ref[...]
        r = _rstd(xv)
        xn = xv * r
        dxn = dhv * (1.0 + sc_ref[...]) * g_ref[...]
        gx_ref[...] = dx1_ref[...] + r * (dxn - xn * jnp.mean(dxn * xn, axis=-1, keepdims=True))
        _acc_rows(acc_ref, [
            jnp.sum(dhv, axis=0, keepdims=True),
            jnp.sum(dhv * xn * g_ref[...], axis=0, keepdims=True),
            jnp.sum(dhv * (1.0 + sc_ref[...]) * xn, axis=0, keepdims=True)])

    return _row_call("pre_mix_bwd", body, [dh, x, dx1], [g_pre, sc], [F32], [(8, D)], S, D, 128)


CUM_BLOCK = 256


def _tri(n, upper):
    r = lax.broadcasted_iota(jnp.int32, (n, n), 0)
    c = lax.broadcasted_iota(jnp.int32, (n, n), 1)
    return ((c >= r) if upper else (c <= r)).astype(F32)


def _fox_gate_fwd(fg, b_pad):
    S = fg.shape[0]
    cb = _fit(CUM_BLOCK, S)

    def body(fg_ref, b_ref, cumt_ref, cum_ref):
        low = _tri(cb, False)
        carry = jnp.zeros((1, LANES), F32)
        for n in range(S // cb):
            z = fg_ref[n * cb:(n + 1) * cb, :] + b_ref[...]
            logf = jnp.minimum(z, 0.0) - jnp.log(1.0 + jnp.exp(-jnp.abs(z)))
            blk = jnp.dot(low, logf, precision=lax.Precision.HIGHEST, preferred_element_type=F32) + carry
            cum_ref[n * cb:(n + 1) * cb, :] = blk
            carry = blk[cb - 1:cb, :]
        cumt_ref[...] = cum_ref[...].T

    return pl.pallas_call(
        body, name="fox_gate_fwd", out_shape=jax.ShapeDtypeStruct((LANES, S), F32),
        scratch_shapes=[pltpu.VMEM((S, LANES), F32)],
        compiler_params=pltpu.CompilerParams(vmem_limit_bytes=_vmem(6 * S * LANES * 4)),
    )(fg, b_pad)


def _fox_gate_bwd(dcum_k, dcum_q, fg, b_pad):
    S = fg.shape[0]
    n_fox = dcum_q.shape[0]
    cb = _fit(CUM_BLOCK, S)

    def body(dk_ref, dq_ref, fg_ref, b_ref, dfg_ref, db_ref, dc_ref):
        lane = lax.broadcasted_iota(jnp.int32, (S, LANES), 1)
        dc = dk_ref[...].T
        for h in range(n_fox):
            dc = dc + jnp.where(lane == h, dq_ref[h], 0.0)
        dc_ref[...] = dc
        up = _tri(cb, True)
        carry = jnp.zeros((1, LANES), F32)
        db = jnp.zeros((1, LANES), F32)
        for n in reversed(range(S // cb)):
            blk = jnp.dot(up, dc_ref[n * cb:(n + 1) * cb, :], precision=lax.Precision.HIGHEST,
                          preferred_element_type=F32) + carry
            carry = blk[0:1, :]
            z = fg_ref[n * cb:(n + 1) * cb, :] + b_ref[...]
            dfg = blk * (1.0 / (1.0 + jnp.exp(z)))
            dfg_ref[n * cb:(n + 1) * cb, :] = dfg.astype(BF16)
            db = db + jnp.sum(dfg, axis=0, keepdims=True)
        db_ref[...] = jnp.broadcast_to(db, db_ref.shape)

    return pl.pallas_call(
        body, name="fox_gate_bwd",
        out_shape=[jax.ShapeDtypeStruct((S, LANES), BF16), jax.ShapeDtypeStruct((8, LANES), F32)],
        scratch_shapes=[pltpu.VMEM((S, LANES), F32)],
        compiler_params=pltpu.CompilerParams(vmem_limit_bytes=_vmem((8 + 2 * n_fox) * S * LANES * 4)),
    )(dcum_k, dcum_q, fg, b_pad)


FOX_TILE = 512


LOG2E = 1.4426950408889634


def _fox_scores(q, k, ck2, masked, t):
    s = lax.dot_general(q, k, _NT, preferred_element_type=F32) * (HEAD_DIM ** -0.5 * LOG2E) - ck2
    if masked:
        row = lax.broadcasted_iota(jnp.int32, (t, t), 0)
        col = lax.broadcasted_iota(jnp.int32, (t, t), 1)
        s = jnp.where(col <= row, s, NEG)
    return s


def _fox_fwd(proj, cum_row, n_fox):
    S = proj.shape[0]
    t = _fit(FOX_TILE, S)
    nq = S // t

    def body(q_ref, k_ref, v_ref, ck_ref, o_ref, lse_ref):
        def q_block(qi, _):
            q0 = pl.multiple_of(qi * t, t)
            q = q_ref[pl.ds(q0, t), :]

            def kv_block(j, carry, masked):
                m, l, acc = carry
                k0 = pl.multiple_of(j * t, t)
                s = _fox_scores(q, k_ref[pl.ds(k0, t), :], ck_ref[0, :, pl.ds(k0, t)] * LOG2E, masked, t)
                m_new = jnp.maximum(m, jnp.max(s, axis=-1, keepdims=True))
                alpha = jnp.exp2(m - m_new)
                p = jnp.exp2(s - m_new)
                l = alpha * l + jnp.sum(p, axis=-1, keepdims=True)
                acc = alpha * acc + jnp.dot(p.astype(BF16), v_ref[pl.ds(k0, t), :], preferred_element_type=F32)
                return m_new, l, acc

            init = (jnp.full((t, 1), NEG, F32), jnp.zeros((t, 1), F32), jnp.zeros((t, HEAD_DIM), F32))
            carry = lax.fori_loop(0, qi, lambda j, cr: kv_block(j, cr, False), init)
            m, l, acc = kv_block(qi, carry, True)
            o_ref[pl.ds(q0, t), :] = acc / l
            lse_ref[0, pl.ds(q0, t), :] = jnp.broadcast_to(m + jnp.log(l) * LOG2E, (t, LANES))
            return 0

        lax.fori_loop(0, nq, q_block, 0)

    col = lambda off: pl.BlockSpec((S, HEAD_DIM), lambda h: (0, off + h))
    per_head = pl.BlockSpec((1, S, LANES), lambda h: (h, 0, 0))
    return pl.pallas_call(
        body, name="fox_fwd", grid=(n_fox,),
        in_specs=[col(0), col(n_fox), col(2 * n_fox), pl.BlockSpec((1, 1, S), lambda h: (h, 0, 0))],
        out_specs=[pl.BlockSpec((S, HEAD_DIM), lambda h: (0, h)), per_head],
        out_shape=[jax.ShapeDtypeStruct((S, n_fox * HEAD_DIM), F32), jax.ShapeDtypeStruct((n_fox, S, LANES), F32)],
        compiler_params=pltpu.CompilerParams(dimension_semantics=("parallel",),
                                             vmem_limit_bytes=_vmem(16 * S * HEAD_DIM * 4 + 12 * t * t * 4)),
    )(proj, proj, proj, cum_row)


def _fox_bwd(proj, o, do, lse_b, cum_row, n_fox):
    S = proj.shape[0]
    t = _fit(FOX_TILE, S)
    nq = S // t
    scale = HEAD_DIM ** -0.5

    def body(q_ref, k_ref, v_ref, o_ref, do_ref, lse_ref, ck_ref, dq_ref, dk_ref, dv_ref, dc_ref, dcq_ref,
             dq_acc, delta_ref):
        dq_acc[...] = jnp.zeros_like(dq_acc)
        dcq_ref[...] = jnp.zeros_like(dcq_ref)

        def delta_block(qi, _):
            q0 = pl.multiple_of(qi * t, t)
            d = jnp.sum(do_ref[pl.ds(q0, t), :] * o_ref[pl.ds(q0, t), :], axis=-1, keepdims=True)
            delta_ref[pl.ds(q0, t), :] = jnp.broadcast_to(d, (t, LANES))
            return 0

        lax.fori_loop(0, nq, delta_block, 0)

        def kv_block(j, _):
            k0 = pl.multiple_of(j * t, t)
            k = k_ref[pl.ds(k0, t), :]
            v = v_ref[pl.ds(k0, t), :]
            ck2 = ck_ref[0, :, pl.ds(k0, t)] * LOG2E

            def q_block(qi, carry, masked):
                dk, dv, dc = carry
                q0 = pl.multiple_of(qi * t, t)
                q = q_ref[pl.ds(q0, t), :]
                dov = do_ref[pl.ds(q0, t), :].astype(BF16)
                p = jnp.exp2(_fox_scores(q, k, ck2, masked, t) - lse_ref[0, pl.ds(q0, t), :][:, :1])
                dp = lax.dot_general(dov, v, _NT, preferred_element_type=F32)
                ds = p * (dp - delta_ref[pl.ds(q0, t), :][:, :1])
                dsb = ds.astype(BF16)
                dv = dv + lax.dot_general(p.astype(BF16), dov, _TN, preferred_element_type=F32)
                dk = dk + lax.dot_general(dsb, q, _TN, preferred_element_type=F32)
                dq_acc[pl.ds(q0, t), :] += jnp.dot(dsb, k, preferred_element_type=F32)
                dc = dc - jnp.sum(ds, axis=0, keepdims=True)
                dcq_ref[0, pl.ds(q0, t), :] += jnp.broadcast_to(jnp.sum(ds, axis=1, keepdims=True), (t, LANES))
                return dk, dv, dc

            init = (jnp.zeros((t, HEAD_DIM), F32), jnp.zeros((t, HEAD_DIM), F32), jnp.zeros((1, t), F32))
            carry = q_block(j, init, True)
            dk, dv, dc = lax.fori_loop(j + 1, nq, lambda qi, cr: q_block(qi, cr, False), carry)
            dk_ref[pl.ds(k0, t), :] = (dk * scale).astype(BF16)
            dv_ref[pl.ds(k0, t), :] = dv.astype(BF16)
            dc_ref[0, :, pl.ds(k0, t)] = dc
            return 0

        lax.fori_loop(0, nq, kv_block, 0)
        dq_ref[...] = (dq_acc[...] * scale).astype(BF16)

    col = lambda off: pl.BlockSpec((S, HEAD_DIM), lambda h: (0, off + h))
    per_head = pl.BlockSpec((1, S, LANES), lambda h: (h, 0, 0))
    row = pl.BlockSpec((1, 1, S), lambda h: (h, 0, 0))
    grad = jax.ShapeDtypeStruct((S, n_fox * HEAD_DIM), BF16)
    return pl.pallas_call(
        body, name="fox_bwd", grid=(n_fox,),
        in_specs=[col(0), col(n_fox), col(2 * n_fox), col(0), col(0), per_head, row],
        out_specs=[col(0), col(0), col(0), row, per_head],
        out_shape=[grad, grad, grad, jax.ShapeDtypeStruct((n_fox, 1, S), F32), jax.ShapeDtypeStruct((n_fox, S, LANES), F32)],
        scratch_shapes=[pltpu.VMEM((S, HEAD_DIM), F32), pltpu.VMEM((S, LANES), F32)],
        compiler_params=pltpu.CompilerParams(dimension_semantics=("parallel",),
                                             vmem_limit_bytes=_vmem(24 * S * HEAD_DIM * 4 + 16 * t * t * 4)),
    )(proj, proj, proj, o, do, lse_b, cum_row)


def _rope_tables(S):
    half = HEAD_DIM // 2
    inv_freq = 1.0 / (ROPE_THETA ** (jnp.arange(half, dtype=F32) * (2.0 / HEAD_DIM)))
    ang = jnp.arange(S).astype(F32)[:, None] * inv_freq[None, :]
    cos, sin = jnp.cos(ang), jnp.sin(ang)
    return jnp.concatenate([cos, cos], axis=-1), jnp.concatenate([-sin, sin], axis=-1)


def _rope(name, src, first_block, n_blocks, cos, sin_signed):
    S = src.shape[0]

    def body(x_ref, cos_ref, sin_ref, o_ref):
        xv = x_ref[...].astype(F32)
        o_ref[...] = (xv * cos_ref[...] + pltpu.roll(xv, HEAD_DIM // 2, 1) * sin_ref[...]).astype(BF16)

    table = pl.BlockSpec((S, HEAD_DIM), lambda n: (0, 0))
    return pl.pallas_call(
        body, name=name, grid=(n_blocks,),
        in_specs=[pl.BlockSpec((S, HEAD_DIM), lambda n: (0, first_block + n)), table, table],
        out_specs=pl.BlockSpec((S, HEAD_DIM), lambda n: (0, n)),
        out_shape=jax.ShapeDtypeStruct((S, n_blocks * HEAD_DIM), BF16),
        compiler_params=pltpu.CompilerParams(dimension_semantics=("parallel",),
                                             vmem_limit_bytes=_vmem(12 * S * HEAD_DIM * 4)),
    )(src, cos, sin_signed)


def _swa_tile(q_ref, kp_ref, kc_ref, n, group, scale):
    B = SWA_BLOCK
    qs = jnp.concatenate([q_ref[:, g * HEAD_DIM:(g + 1) * HEAD_DIM] for g in range(group)], axis=0)
    kcat = jnp.concatenate([kp_ref[...], kc_ref[...]], axis=0)
    s = lax.dot_general(qs, kcat, _NT, preferred_element_type=F32) * scale
    qi = lax.broadcasted_iota(jnp.int32, (group * B, 2 * B), 0) % B
    kj = lax.broadcasted_iota(jnp.int32, (group * B, 2 * B), 1)
    diff = qi + B - kj
    mask = (diff >= 0) & (diff < B) & ((n * B + kj - B) >= 0)
    return qs, kcat, jnp.where(mask, s, NEG)


def _swa_sink_col(sink_ref, kv, group):
    head = lax.broadcasted_iota(jnp.int32, (group * SWA_BLOCK, 1), 0) // SWA_BLOCK
    col = jnp.zeros((group * SWA_BLOCK, 1), F32)
    for g in range(group):
        col = jnp.where(head == g, sink_ref[kv * group + g], col)
    return col


def _swa_specs(n_kv, group, q_first, k_first, v_first):
    B = SWA_BLOCK
    prev = lambda n: jnp.maximum(n - 1, 0)
    return [
        pl.BlockSpec((B, group * HEAD_DIM), lambda kv, n: (n, q_first + kv)),
        pl.BlockSpec((B, HEAD_DIM), lambda kv, n: (prev(n), k_first + kv)),
        pl.BlockSpec((B, HEAD_DIM), lambda kv, n: (n, k_first + kv)),
        pl.BlockSpec((B, HEAD_DIM), lambda kv, n: (prev(n), v_first + kv)),
        pl.BlockSpec((B, HEAD_DIM), lambda kv, n: (n, v_first + kv)),
    ]


def _swa_fwd(rq, proj, v_first, sinks, n_q, n_kv):
    S = rq.shape[0]
    B = SWA_BLOCK
    group = n_q // n_kv
    scale = HEAD_DIM ** -0.5

    def body(q_ref, kp_ref, kc_ref, vp_ref, vc_ref, sink_ref, o_ref, lse_ref):
        kv, n = pl.program_id(0), pl.program_id(1)
        _, _, s = _swa_tile(q_ref, kp_ref, kc_ref, n, group, scale)
        sink = _swa_sink_col(sink_ref, kv, group)
        m = jnp.maximum(jnp.max(s, axis=-1, keepdims=True), sink)
        p = jnp.exp(s - m)
        denom = jnp.sum(p, axis=-1, keepdims=True) + jnp.exp(sink - m)
        vcat = jnp.concatenate([vp_ref[...], vc_ref[...]], axis=0)
        o = jnp.dot((p / denom).astype(BF16), vcat, preferred_element_type=F32)
        lse = m + jnp.log(denom)
        for g in range(group):
            o_ref[:, g * HEAD_DIM:(g + 1) * HEAD_DIM] = o[g * B:(g + 1) * B, :]
            lse_ref[0, :, g * LANES:(g + 1) * LANES] = jnp.broadcast_to(lse[g * B:(g + 1) * B, :], (B, LANES))

    specs = _swa_specs(n_kv, group, 0, n_q, v_first)
    q_blk = pl.BlockSpec((B, group * HEAD_DIM), lambda kv, n: (n, kv))
    return pl.pallas_call(
        body, name="swa_fwd", grid=(n_kv, S // B),
        in_specs=specs + [pl.BlockSpec(memory_space=pltpu.SMEM)],
        out_specs=[q_blk, pl.BlockSpec((1, B, group * LANES), lambda kv, n: (kv, n, 0))],
        out_shape=[jax.ShapeDtypeStruct((S, n_q * HEAD_DIM), F32), jax.ShapeDtypeStruct((n_kv, S, group * LANES), F32)],
        compiler_params=pltpu.CompilerParams(dimension_semantics=("parallel", "arbitrary")),
    )(rq, rq, rq, proj, proj, sinks)


def _swa_bwd(rq, proj, v_first, sinks, o, do, do_first, lse_b, n_q, n_kv):
    S = rq.shape[0]
    B = SWA_BLOCK
    group = n_q // n_kv
    scale = HEAD_DIM ** -0.5

    def body(q_ref, kp_ref, kc_ref, vp_ref, vc_ref, o_ref, do_ref, lse_ref, sink_ref,
             dq_ref, dk_ref, dv_ref, dsink_ref):
        kv, n = pl.program_id(0), pl.program_id(1)

        @pl.when(n == 0)
        def _():
            dk_ref[...] = jnp.zeros_like(dk_ref)
            dv_ref[...] = jnp.zeros_like(dv_ref)
            dsink_ref[...] = jnp.zeros_like(dsink_ref)

        qs, kcat, s = _swa_tile(q_ref, kp_ref, kc_ref, n, group, scale)
        sink = _swa_sink_col(sink_ref, kv, group)
        stack = lambda ref, w: jnp.concatenate([ref[:, g * w:(g + 1) * w] for g in range(group)], axis=0)
        lse = jnp.concatenate([lse_ref[0, :, g * LANES:g * LANES + 1] for g in range(group)], axis=0)
        do32 = stack(do_ref, HEAD_DIM)
        delta = jnp.sum(do32 * stack(o_ref, HEAD_DIM), axis=-1, keepdims=True)
        dov = do32.astype(BF16)
        p = jnp.exp(s - lse)
        vcat = jnp.concatenate([vp_ref[...], vc_ref[...]], axis=0)
        dp = lax.dot_general(dov, vcat, _NT, preferred_element_type=F32)
        ds = p * (dp - delta)
        dsb = ds.astype(BF16)
        dq = jnp.dot(dsb, kcat, preferred_element_type=F32) * scale
        for g in range(group):
            dq_ref[:, g * HEAD_DIM:(g + 1) * HEAD_DIM] = dq[g * B:(g + 1) * B, :].astype(BF16)
        dkcat = lax.dot_general(dsb, qs, _TN, preferred_element_type=F32) * scale
        dvcat = lax.dot_general(p.astype(BF16), dov, _TN, preferred_element_type=F32)
        prev0 = pl.multiple_of(jnp.maximum(n - 1, 0) * B, B)
        cur0 = pl.multiple_of(n * B, B)
        dk_ref[0, pl.ds(prev0, B), :] += dkcat[:B, :]
        dk_ref[0, pl.ds(cur0, B), :] += dkcat[B:, :]
        dv_ref[0, pl.ds(prev0, B), :] += dvcat[:B, :]
        dv_ref[0, pl.ds(cur0, B), :] += dvcat[B:, :]
        dsk = -jnp.exp(sink - lse) * delta
        lane = lax.broadcasted_iota(jnp.int32, (1, LANES), 1)
        row = jnp.zeros((1, LANES), F32)
        for g in range(group):
            row = row + jnp.where(lane == g, jnp.sum(dsk[g * B:(g + 1) * B, :]), 0.0)
        dsink_ref[0, 0:1, :] += row

    specs = _swa_specs(n_kv, group, 0, n_q, v_first)
    q_blk = pl.BlockSpec((B, group * HEAD_DIM), lambda kv, n: (n, kv))
    acc = pl.BlockSpec((1, S, HEAD_DIM), lambda kv, n: (kv, 0, 0))
    return pl.pallas_call(
        body, name="swa_bwd", grid=(n_kv, S // B),
        in_specs=specs + [q_blk, pl.BlockSpec((B, group * HEAD_DIM), lambda kv, n: (n, do_first + kv)),
                          pl.BlockSpec((1, B, group * LANES), lambda kv, n: (kv, n, 0)),
                          pl.BlockSpec(memory_space=pltpu.SMEM)],
        out_specs=[q_blk, acc, acc, pl.BlockSpec((1, 8, LANES), lambda kv, n: (kv, 0, 0))],
        out_shape=[jax.ShapeDtypeStruct((S, n_q * HEAD_DIM), BF16), jax.ShapeDtypeStruct((n_kv, S, HEAD_DIM), F32),
                   jax.ShapeDtypeStruct((n_kv, S, HEAD_DIM), F32), jax.ShapeDtypeStruct((n_kv, 8, LANES), F32)],
        compiler_params=pltpu.CompilerParams(dimension_semantics=("parallel", "arbitrary")),
    )(rq, rq, rq, proj, proj, o, do, lse_b, sinks)


def _adamw(w, g, m, v):
    m = ADAM_B1 * m + (1.0 - ADAM_B1) * g
    v = ADAM_B2 * v + (1.0 - ADAM_B2) * (g * g)
    m_hat = m / (1.0 - ADAM_B1 ** ADAM_STEP)
    v_hat = v / (1.0 - ADAM_B2 ** ADAM_STEP)
    delta = -ADAM_LR * (m_hat / (jnp.sqrt(v_hat) + ADAM_EPS) + ADAM_WD * w)
    return delta, m, v


def _mod_fwd(cond_in, w_mod, b_shard):
    R, D = cond_in.shape
    cols = w_mod.shape[1]
    tn = _fit(512, cols)

    def body(c_ref, w_ref, b_ref, o_ref):
        cv = c_ref[...]
        cond = (cv / (1.0 + jnp.exp(-cv))).astype(BF16)
        o_ref[...] = jnp.dot(cond, w_ref[...].astype(BF16), preferred_element_type=F32) + b_ref[...]

    return pl.pallas_call(
        body, name="mod_fwd", grid=(cols // tn,),
        in_specs=[pl.BlockSpec((R, D), lambda j: (0, 0)), pl.BlockSpec((D, tn), lambda j: (0, j)),
                  pl.BlockSpec((1, tn), lambda j: (0, j))],
        out_specs=pl.BlockSpec((R, tn), lambda j: (0, j)),
        out_shape=jax.ShapeDtypeStruct((R, cols), F32),
        compiler_params=pltpu.CompilerParams(dimension_semantics=("parallel",), vmem_limit_bytes=_vmem(3 * D * tn * 4)),
    )(cond_in, w_mod, b_shard)


def _mod_update(c_t, dmod, w, m, v):
    D, nb = c_t.shape
    cols = w.shape[1]
    tn = _fit(256, cols)

    def body(c_ref, d_ref, w_ref, m_ref, v_ref, g_ref, dl_ref, nm_ref, nv_ref):
        cv = c_ref[...]
        cond = cv / (1.0 + jnp.exp(-cv))
        g = jnp.zeros((D, tn), F32)
        for b in range(nb):
            g = g + cond[:, b:b + 1] * d_ref[b:b + 1, :]
        g_ref[...] = g
        dl_ref[...], nm_ref[...], nv_ref[...] = _adamw(w_ref[...], g, m_ref[...], v_ref[...])

    blk = pl.BlockSpec((D, tn), lambda j: (0, j))
    out = jax.ShapeDtypeStruct((D, cols), F32)
    return pl.pallas_call(
        body, name="mod_update", grid=(cols // tn,),
        in_specs=[pl.BlockSpec((D, nb), lambda j: (0, 0)), pl.BlockSpec((nb, tn), lambda j: (0, j)), blk, blk, blk],
        out_specs=[blk] * 4, out_shape=[out] * 4,
        compiler_params=pltpu.CompilerParams(dimension_semantics=("parallel",), vmem_limit_bytes=_vmem(18 * D * tn * 4)),
    )(c_t, dmod, w, m, v)


def _small_update(stacked, w, m, v):
    R, C = w.shape

    def body(s_ref, w_ref, m_ref, v_ref, g_ref, dl_ref, nm_ref, nv_ref):
        g = s_ref[0:R, :]
        for d in range(1, N_DEV):
            g = g + s_ref[d * R:(d + 1) * R, :]
        g_ref[...] = g
        dl_ref[...], nm_ref[...], nv_ref[...] = _adamw(w_ref[...], g, m_ref[...], v_ref[...])

    return pl.pallas_call(body, name="small_update", out_shape=[jax.ShapeDtypeStruct((R, C), F32)] * 4)(stacked, w, m, v)


def _place():
    return lax.axis_index("x"), lax.axis_index("y"), lax.axis_index("c")


def _allgather8(name, block):
    m_per, n = block.shape

    def body(x_ref, out_ref, token_ref, send_sems, recv_sems, local_sem):
        token_ref[...] = jnp.zeros_like(token_ref)
        x, y, c = _place()
        me, sibling = (x, y, c), (x, y, 1 - c)
        chips = [(1 - x, y), (x, 1 - y), (1 - x, 1 - y)]

        def rows(px, py, pc):
            return out_ref.at[pl.ds((4 * px + 2 * py + pc) * m_per, m_per), :]

        def copy(k, blk, to, src=None):
            return pltpu.make_async_remote_copy(
                src_ref=rows(*blk) if src is None else src, dst_ref=rows(*blk),
                send_sem=send_sems.at[k], recv_sem=recv_sems.at[k], device_id=to, device_id_type=MESH)

        mine = pltpu.make_async_copy(x_ref, rows(*me), local_sem)
        mine.start()
        first = [copy(0, me, sibling, src=x_ref)]
        first += [copy(1 + j, me, (*chip, c), src=x_ref) for j, chip in enumerate(chips)]
        for cp in first:
            cp.start()
        passed = [copy(4 + j, (*chip, c), sibling) for j, chip in enumerate(chips)]
        for j, chip in enumerate(chips):
            copy(1 + j, (*chip, c), me).wait_recv()
            passed[j].start()
        copy(0, sibling, me).wait_recv()
        for j, chip in enumerate(chips):
            copy(4 + j, (*chip, 1 - c), me).wait_recv()
        for cp in first + passed:
            cp.wait_send()
        mine.wait()

    vmem = pl.BlockSpec(memory_space=pltpu.VMEM)
    return pl.pallas_call(
        body, name=name,
        out_shape=[jax.ShapeDtypeStruct((N_DEV * m_per, n), block.dtype), jax.ShapeDtypeStruct((8, LANES), F32)],
        in_specs=[vmem], out_specs=[vmem, vmem],
        scratch_shapes=[pltpu.SemaphoreType.DMA((7,)), pltpu.SemaphoreType.DMA((7,)), pltpu.SemaphoreType.DMA],
    )(block)


_ANY = pl.BlockSpec(memory_space=pl.ANY)


def _half(ref, c, rows):
    return ref.at[pl.ds(c * (rows // 2), rows // 2), :]


_HBM = pl.BlockSpec(memory_space=pltpu.HBM)
_SEM = pl.BlockSpec(memory_space=pltpu.SEMAPHORE)
_EFFECT = pltpu.SideEffectType.DATAFLOW_SIDE_EFFECTING


def _ici_start(name, srcs, land_shapes, plan, per_source=3, after=None):
    ns, nl = len(srcs), len(land_shapes)
    n_copies = per_source * ns
    n_in = ns + nl + (after is not None)

    def body(*refs):
        src_refs, land_refs = refs[:ns], refs[ns:ns + nl]
        send_sems, recv_sems = refs[n_in], refs[n_in + 1]
        token = refs[-1]
        for n, (src, dst, peer, _) in enumerate(plan(src_refs, land_refs)):
            pltpu.make_async_remote_copy(src_ref=src, dst_ref=dst, send_sem=send_sems.at[n], recv_sem=recv_sems.at[n],
                                         device_id=peer, device_id_type=MESH).start()
        token[...] = jnp.zeros_like(token)

    lands = [lax.empty(s.shape, s.dtype) for s in land_shapes]
    out = pl.pallas_call(
        body, name=name,
        out_shape=(pltpu.SemaphoreType.DMA((n_copies,)), pltpu.SemaphoreType.DMA((n_copies,)),
                   *[pltpu.HBM(a.shape, a.dtype) for a in list(srcs) + lands], jax.ShapeDtypeStruct((8, LANES), F32)),
        in_specs=[_HBM] * (ns + nl) + [_ANY] * (after is not None),
        out_specs=(_SEM, _SEM, *[_HBM] * (ns + nl), pl.BlockSpec(memory_space=pltpu.VMEM)),
        input_output_aliases={n: 2 + n for n in range(ns + nl)},
        compiler_params=pltpu.CompilerParams(has_side_effects=_EFFECT),
    )(*[pltpu.with_memory_space_constraint(a, pltpu.HBM) for a in list(srcs) + lands],
      *([] if after is None else [after]))
    return out[0], out[1], list(out[2:2 + ns]), list(out[2 + ns:2 + ns + nl]), out[-1]


def _ici_wait(name, send_sems, recv_sems, srcs, lands, plan, after):
    ns, nl = len(srcs), len(lands)
    after = list(after) if isinstance(after, (list, tuple)) else [after]

    def body(*refs):
        src_refs, land_refs = refs[:ns], refs[ns:ns + nl]
        send_sems, recv_sems = refs[ns + nl], refs[ns + nl + 1]
        for n, (src, _, peer, mine) in enumerate(plan(src_refs, land_refs)):
            cp = pltpu.make_async_remote_copy(src_ref=src, dst_ref=mine, send_sem=send_sems.at[n],
                                              recv_sem=recv_sems.at[n], device_id=peer, device_id_type=MESH)
            cp.wait_send()
            cp.wait_recv()

    out = pl.pallas_call(
        body, name=name, out_shape=[pltpu.HBM(a.shape, a.dtype) for a in list(srcs) + list(lands)],
        in_specs=[_HBM] * (ns + nl) + [_SEM, _SEM] + [_ANY] * len(after), out_specs=[_HBM] * (ns + nl),
        input_output_aliases={n: n for n in range(ns + nl)},
        compiler_params=pltpu.CompilerParams(has_side_effects=_EFFECT),
    )(*srcs, *lands, send_sems, recv_sems, *after)
    return list(out[:ns]), list(out[ns:])


def _own_slab(name, chip, w, after):
    R, C = w.shape
    tr, tc = _tiles(R, C)
    tied = [] if after is None else [after]

    def body(chip_ref, w_ref, *rest):
        stack_ref, token_ref = rest[-2:]
        stack_ref[0] = w_ref[...].astype(BF16)
        token_ref[...] = jnp.zeros_like(token_ref)

    small = pl.BlockSpec((8, LANES), lambda r, q, chip_ref: (0, 0))
    grid_spec = pltpu.PrefetchScalarGridSpec(
        num_scalar_prefetch=1, grid=(R // tr, C // tc),
        in_specs=[pl.BlockSpec((tr, tc), lambda r, q, chip_ref: (r, q))] + [small] * len(tied),
        out_specs=[pl.BlockSpec((1, tr, tc), lambda r, q, chip_ref: (chip_ref[0], r, q)), small])
    return pl.pallas_call(
        body, name=name, grid_spec=grid_spec,
        out_shape=[jax.ShapeDtypeStruct((N_CHIPS, R, C), BF16), jax.ShapeDtypeStruct((8, LANES), F32)],
        compiler_params=pltpu.CompilerParams(dimension_semantics=("arbitrary", "arbitrary")),
    )(chip, w, *tied)


def _gather_plan(src_refs, land_refs):
    x, y, c = _place()
    copies = []
    for stack in src_refs:
        R = stack.shape[1]
        own = _half(stack.at[2 * x + y], c, R)
        for cx, cy in [(1 - x, y), (x, 1 - y), (1 - x, 1 - y)]:
            copies.append((own, own, (cx, cy, c), _half(stack.at[2 * cx + cy], c, R)))
    return copies


def _pass_plan(src_refs, land_refs):
    x, y, c = _place()
    copies = []
    for land in src_refs:
        R = land.shape[1]
        for cx, cy in [(1 - x, y), (x, 1 - y), (1 - x, 1 - y)]:
            slot = land.at[2 * cx + cy]
            copies.append((_half(slot, c, R), _half(slot, c, R), (x, y, 1 - c), _half(slot, 1 - c, R)))
    return copies


def _share_plan(src_refs, land_refs):
    x, y, c = _place()
    return [(h, land, (x, y, 1 - c), land) for h, land in zip(src_refs, land_refs)]


def _pass_to_sibling(name, lands):
    nw = len(lands)

    def body(*refs):
        ins, outs = refs[:nw], refs[nw:2 * nw]
        send_sems, recv_sems = refs[2 * nw:]
        x, y, c = _place()
        chips = [(1 - x, y), (x, 1 - y), (1 - x, 1 - y)]
        copies = []
        for k in range(nw):
            R = ins[k].shape[1]
            for j, (cx, cy) in enumerate(chips):
                cp = pltpu.make_async_remote_copy(
                    src_ref=_half(ins[k].at[2 * cx + cy], c, R), dst_ref=_half(outs[k].at[2 * cx + cy], c, R),
                    send_sem=send_sems.at[3 * k + j], recv_sem=recv_sems.at[3 * k + j],
                    device_id=(x, y, 1 - c), device_id_type=MESH)
                cp.start()
                copies.append(cp)
        for k in range(nw):
            R = ins[k].shape[1]
            for j, (cx, cy) in enumerate(chips):
                pltpu.make_async_remote_copy(
                    src_ref=_half(ins[k].at[2 * cx + cy], c, R), dst_ref=_half(outs[k].at[2 * cx + cy], 1 - c, R),
                    send_sem=send_sems.at[3 * k + j], recv_sem=recv_sems.at[3 * k + j],
                    device_id=(x, y, 1 - c), device_id_type=MESH).wait_recv()
        for cp in copies:
            cp.wait_send()

    return pl.pallas_call(
        body, name=name, out_shape=[jax.ShapeDtypeStruct(a.shape, a.dtype) for a in lands],
        in_specs=[_ANY] * nw, out_specs=[_ANY] * nw, input_output_aliases={k: k for k in range(nw)},
        scratch_shapes=[pltpu.SemaphoreType.DMA((3 * nw,)), pltpu.SemaphoreType.DMA((3 * nw,))],
    )(*lands)


def _tie(vec, token):
    return vec + token[0:1, 0:1]


ROW_ALIGN = 16
TILE_ELEMS = 512 * 1024


def _tiles(rows, cols):
    fits = [t for t in range(ROW_ALIGN, min(rows, 256) + 1, ROW_ALIGN) if rows % t == 0]
    tr = fits[-1] if fits and fits[-1] >= 64 else rows
    tc = cols
    while tr * tc > TILE_ELEMS and tc % (2 * LANES) == 0:
        tc //= 2
    return tr, tc


def _scatter_plan(src_refs, land_refs):
    x, y, c = _place()
    copies = []
    for p, land in zip(src_refs, land_refs):
        for j, (cx, cy) in enumerate([(1 - x, y), (x, 1 - y), (1 - x, 1 - y)]):
            copies.append((p.at[2 * cx + cy], land.at[j], (cx, cy, c), land.at[j]))
    return copies


def _chip_add(name, chip, sums, recv):
    _, H, C = sums.shape
    tr, tc = _tiles(H, C)

    def body(chip_ref, p_ref, r_ref, o_ref):
        total = p_ref[0].astype(F32)
        for j in range(3):
            total = total + r_ref[j].astype(F32)
        o_ref[...] = total

    grid_spec = pltpu.PrefetchScalarGridSpec(
        num_scalar_prefetch=1, grid=(H // tr, C // tc),
        in_specs=[pl.BlockSpec((1, tr, tc), lambda r, q, chip_ref: (chip_ref[0], r, q)),
                  pl.BlockSpec((3, tr, tc), lambda r, q, chip_ref: (0, r, q))],
        out_specs=pl.BlockSpec((tr, tc), lambda r, q, chip_ref: (r, q)))
    return pl.pallas_call(
        body, name=name, grid_spec=grid_spec, out_shape=jax.ShapeDtypeStruct((H, C), F32),
        compiler_params=pltpu.CompilerParams(dimension_semantics=("parallel", "parallel")),
    )(chip, sums, recv)


def _pair_share(name, halves):
    nw = len(halves)

    def body(*refs):
        hs, outs = refs[:nw], refs[nw:2 * nw]
        send_sems, recv_sems = refs[2 * nw:]
        x, y, c = _place()
        copies = []
        for k in range(nw):
            cp = pltpu.make_async_remote_copy(
                src_ref=hs[k], dst_ref=outs[k], send_sem=send_sems.at[k], recv_sem=recv_sems.at[k],
                device_id=(x, y, 1 - c), device_id_type=MESH)
            cp.start()
            copies.append(cp)
        for cp in copies:
            cp.wait()

    return pl.pallas_call(
        body, name=name,
        out_shape=[jax.ShapeDtypeStruct(h.shape, h.dtype) for h in halves],
        in_specs=[_ANY] * nw, out_specs=[_ANY] * nw,
        scratch_shapes=[pltpu.SemaphoreType.DMA((nw,)), pltpu.SemaphoreType.DMA((nw,))],
    )(*halves)


def _adam_halves(name, core, w, g_own, g_other, m, v):
    R, C = w.shape
    H = R // 2
    tr, tc = _tiles(H, C)
    nr, nc = H // tr, C // tc

    def body(core_ref, w_ref, go_ref, gr_ref, m_ref, v_ref, g_ref, dl_ref, nm_ref, nv_ref):
        own = (pl.program_id(0) // nr) == core_ref[0]
        g = jnp.where(own, go_ref[...], gr_ref[...])
        g_ref[...] = g
        dl_ref[...], nm_ref[...], nv_ref[...] = _adamw(w_ref[...], g, m_ref[...], v_ref[...])

    blk = pl.BlockSpec((tr, tc), lambda r, q, core_ref: (r, q))

    def half_spec(is_own):
        def index(r, q, core_ref):
            mine = ((r // nr) == core_ref[0]) == is_own
            done = is_own == (core_ref[0] == 0)
            return (jnp.where(mine, r % nr, jnp.where(done, nr - 1, 0)), jnp.where(mine, q, jnp.where(done, nc - 1, 0)))
        return pl.BlockSpec((tr, tc), index)
    out = jax.ShapeDtypeStruct((R, C), F32)
    grid_spec = pltpu.PrefetchScalarGridSpec(
        num_scalar_prefetch=1, grid=(R // tr, nc), in_specs=[blk, half_spec(True), half_spec(False), blk, blk],
        out_specs=[blk] * 4)
    return pl.pallas_call(
        body, name=name, grid_spec=grid_spec, out_shape=[out] * 4,
        compiler_params=pltpu.CompilerParams(dimension_semantics=("parallel", "parallel"),
                                             vmem_limit_bytes=_vmem(20 * tr * tc * 4)),
    )(core, w, g_own, g_other, m, v)


def kernel(x, c, w_mod, b_mod, g_pre_mix, g_post_mix, w_in, b_forget, swa_sinks, w_out, g_pre_mlp, g_post_mlp, w_up, w_down, loss_target, m_w_mod, m_b_mod, m_g_pre_mix, m_g_post_mix, m_w_in, m_b_forget, m_swa_sinks, m_w_out, m_g_pre_mlp, m_g_post_mlp, m_w_up, m_w_down, v_w_mod, v_b_mod, v_g_pre_mix, v_g_post_mix, v_w_in, v_b_forget, v_swa_sinks, v_w_out, v_g_pre_mlp, v_g_post_mlp, v_w_up, v_w_down):
    S, D = x.shape[1], x.shape[2]
    n_heads = D // HEAD_DIM
    n_fox = n_heads // 2
    n_swa = n_heads - n_fox
    n_kv = max(1, n_swa // 4)
    fox_w, swa_w, kv_w = n_fox * HEAD_DIM, n_swa * HEAD_DIM, n_kv * HEAD_DIM
    main_w = 3 * fox_w + swa_w + 2 * kv_w
    in_w = main_w + n_fox
    mod_cols = w_mod.shape[2]

    ax, ay, ac = _place()
    chip = 2 * ax + ay
    dev = 2 * chip + ac
    chip_arr = jnp.reshape(chip, (1,)).astype(jnp.int32)
    core_arr = jnp.reshape(ac, (1,)).astype(jnp.int32)

    x2, tgt = x[0], loss_target[0]

    in_rows = in_w // N_CHIPS
    in_rows_pad = -(-in_rows // (2 * LANES)) * (2 * LANES)
    slab_w = N_CHIPS * in_rows_pad

    def rows_of(a):
        return jnp.pad(a[0].T, ((0, in_rows_pad - in_rows), (0, 0)))

    w_in_stack, token = _own_slab("own_slab_w_in", chip_arr, rows_of(w_in), None)

    c_all, _ = _allgather8("gather_c", _tie(c, token).reshape(8, D // 8))
    c_all = c_all.reshape(N_DEV, D)
    b_shard = lax.dynamic_slice_in_dim(b_mod, chip * mod_cols, mod_cols, axis=1)
    mod_shard = _mod_fwd(jnp.pad(c_all, ((0, 16 - N_DEV), (0, 0))), w_mod[0], b_shard)[:N_DEV]
    mod_all, token = _allgather8("gather_mod", mod_shard)
    mod_all = mod_all.reshape(N_CHIPS, 2, N_DEV, mod_cols)[:, 0]
    mod = lax.dynamic_index_in_dim(mod_all, dev, axis=1, keepdims=False).reshape(N_MOD, 1, D)
    sh_a, sc_a, gt_a, sh_m, sc_m, gt_m = [mod[n] for n in range(N_MOD)]

    def slab_cols(lo, hi):
        spans = []
        while lo < hi:
            s, r = divmod(lo, in_rows)
            n = min(hi - lo, in_rows - r)
            spans.append((s * in_rows_pad + r, s * in_rows_pad + r + n))
            lo += n
        return spans

    gate_lo = 3 * fox_w
    main_spans = slab_cols(0, gate_lo) + slab_cols(gate_lo + n_fox, in_w)
    (gate_first, gate_last), = slab_cols(gate_lo, gate_lo + n_fox)

    names = ["w_in", "w_out", "w_up", "w_down"]
    flights = {}
    for n, w in zip(names, [None, w_out[0], w_up[0], w_down[0]]):
        stack = w_in_stack if n == "w_in" else _own_slab("own_slab_" + n, chip_arr, w, token)[0]
        flights[n] = _ici_start("gather_start_" + n, [stack], [], _gather_plan, after=token)
        token = flights[n][4]
    sc_a = _tie(sc_a, token)

    def arrived(n, after):
        send, recv, stacks, _, _ = flights[n]
        stacks, _ = _ici_wait("gather_wait_" + n, send, recv, stacks, [], _gather_plan, after)
        return _ici_start("gather_pass_start_" + n, stacks, [], _pass_plan)

    def gathered(n, after, in_flight=None):
        if in_flight is None:
            send, recv, stacks, _, _ = flights[n]
            stacks, _ = _ici_wait("gather_wait_" + n, send, recv, stacks, [], _gather_plan, after)
            return _pass_to_sibling("gather_pass_" + n, stacks)[0]
        send, recv, stacks, _, _ = in_flight
        return _ici_wait("gather_pass_wait_" + n, send, recv, stacks, [], _pass_plan, after)[0][0]

    d_ff = N_CHIPS * w_up.shape[2]

    h = _pre_norm(x2, g_pre_mix, sc_a, sh_a)
    in_state = [rows_of(w_in)] + [rows_of(_tie(a, token)) for a in (m_w_in, v_w_in)]
    cos, sin_signed = _rope_tables(S)

    def pack(bm, gpm, gqm, gpl, gql, bf, sk):
        last = jnp.concatenate([bf, sk, jnp.zeros((1, D - n_fox - n_swa), F32)], axis=1)
        return jnp.concatenate([bm.reshape(N_MOD, D), gpm, gqm, gpl, gql, last, jnp.zeros((5, D), F32)], axis=0)

    small_state = [pack(b_mod, g_pre_mix, g_post_mix, g_pre_mlp, g_post_mlp, b_forget, swa_sinks),
                   pack(m_b_mod, m_g_pre_mix, m_g_post_mix, m_g_pre_mlp, m_g_post_mlp, m_b_forget, m_swa_sinks),
                   pack(v_b_mod, v_g_pre_mix, v_g_post_mix, v_g_pre_mlp, v_g_post_mlp, v_b_forget, v_swa_sinks)]
    ready = h[:8, :LANES].astype(F32) + cos[:8]
    w_slab_t = gathered("w_in", [ready] + in_state[1:] + small_state).reshape(slab_w, D)
    tm_p, tn_p = _fit(MM_TM, S), _fit(MM_TN if slab_w % MM_TN == 0 else MM_TN // 2, slab_w)
    win0 = gate_first // LANES * LANES
    win_j, win_off = divmod(win0, tn_p)
    assert win_off + 2 * LANES <= tn_p and gate_last - win0 <= 2 * LANES

    def proj_epilogue(acc, ex, outs):
        outs[0][...] = acc.astype(BF16)

        @pl.when(pl.program_id(1) == win_j)
        def _():
            outs[1][...] = acc[:, win_off:win_off + 2 * LANES]

    proj_slab, gate_win = _matmul(
        "in_proj", h, w_slab_t, "nt",
        [((S, slab_w), BF16, (tm_p, tn_p), lambda i, j: (i, j)), ((S, 2 * LANES), F32, (tm_p, 2 * LANES), lambda i, j: (i, 0))],
        proj_epilogue, tn=tn_p, revisits=True)
    proj = jnp.concatenate([proj_slab[:, lo:hi] for lo, hi in main_spans], axis=1)
    out_flight = arrived("w_out", proj_slab)
    fg = _tie(jnp.pad(gate_win[:, gate_first - win0:gate_last - win0], ((0, 0), (0, LANES - n_fox))), out_flight[4])
    b_pad = jnp.pad(b_forget, ((0, 0), (0, LANES - n_fox)))
    cum_row = _fox_gate_fwd(fg, b_pad)[:n_fox].reshape(n_fox, 1, S)
    fox_o, fox_lse = _fox_fwd(proj, cum_row, n_fox)

    rq = _rope("rope_fwd", proj, 3 * n_fox, n_swa + n_kv, cos, sin_signed)
    v_first = 3 * n_fox + n_swa + n_kv
    sinks = swa_sinks[0]
    swa_o, swa_lse = _swa_fwd(rq, proj, v_first, sinks, n_swa, n_kv)

    mixcat = jnp.concatenate([fox_o, swa_o], axis=1).astype(BF16)
    up_flight = arrived("w_up", mixcat)
    w_out_f = gathered("w_out", mixcat, out_flight).reshape(D, D)
    mix = _mm_plain("out_proj", mixcat, w_out_f, "nn", BF16, after=up_flight[4])
    x1, h2 = _post_mix(x2, mix, g_post_mix, gt_a, g_pre_mlp, sc_m, sh_m)
    w_up_f = gathered("w_up", h2, up_flight)

    tm_u, tn_u = _fit(MM_TM, S), _fit(MM_TN, d_ff)

    def up_epilogue(acc, ex, outs):
        outs[0][...] = acc.astype(BF16)
        r = jnp.maximum(acc, 0.0)
        outs[1][...] = (r * r).astype(BF16)

    ublk = ((S, d_ff), BF16, (tm_u, tn_u), lambda i, j: (i, j))
    u, a = _matmul("mlp_up", h2, w_up_f, "nn", [ublk, ublk], up_epilogue)
    w_down_f = gathered("w_down", a).reshape(d_ff, D)
    y = _mm_plain("mlp_down", a, w_down_f, "nn", BF16)

    dy, dout, loss_part, acc_mlp_post = _loss_and_post_mlp_bwd(x1, y, tgt, g_post_mlp, gt_m)

    def du_epilogue(acc, ex, outs):
        outs[0][...] = (acc * (2.0 * jnp.maximum(ex[0][...].astype(F32), 0.0))).astype(BF16)

    du = _matmul("mlp_down_bwd", dy, w_down_f, "nt", [ublk], du_epilogue,
                 extras=[(u, (tm_u, tn_u), lambda i, j: (i, j))])[0]
    def pair_send(tag, part):
        return _ici_start("grad_pair_start_" + tag, [part], [jax.ShapeDtypeStruct(part.shape, BF16)], _share_plan,
                          per_source=1)

    def pair_recv(tag, flight, after):
        send, recv, srcs, lands, _ = flight
        return _ici_wait("grad_pair_wait_" + tag, send, recv, srcs, lands, _share_plan, after)[1][0]

    def scatter_start(tag, sums):
        return _ici_start("grad_scatter_start_" + tag, sums,
                          [jax.ShapeDtypeStruct((3,) + p.shape[1:], BF16) for p in sums], _scatter_plan)

    def scatter_finish(tag, flight, after):
        send, recv, srcs, lands, _ = flight
        sums, received = _ici_wait("grad_scatter_wait_" + tag, send, recv, srcs, lands, _scatter_plan, after)
        return [_chip_add("chip_add_%s_%d" % (tag, k), chip_arr, p, r) for k, (p, r) in enumerate(zip(sums, received))]

    tm_g = _fit(MM_TM, D // 2)
    pair_down = pair_send("down", _grad_half("grad_w_down_a", core_arr, a, dy, N_CHIPS, 1, tm_g, True))
    pair_up = pair_send("up", _grad_half("grad_w_up_a", core_arr, h2, du, 1, N_CHIPS, tm_g, True, after=pair_down[4]))
    sum_down = _grad_half("grad_w_down_b", core_arr, a, dy, N_CHIPS, 1, tm_g, False,
                          recv=pair_recv("down", pair_down, pair_up[4]))
    sum_up = _grad_half("grad_w_up_b", core_arr, h2, du, 1, N_CHIPS, tm_g, False, recv=pair_recv("up", pair_up, sum_down))
    flight_mlp = scatter_start("mlp", [sum_up, sum_down])
    dh2 = _mm_plain("mlp_up_bwd", du, w_up_f, "nt", BF16, after=flight_mlp[4])
    dx1, dmix, acc_mid = _pre_mlp_and_post_mix_bwd(dh2, x1, dout, mix, _tie(g_pre_mlp, flight_mlp[4]), sc_m,
                                                   g_post_mix, gt_a)

    dmixcat = _mm_plain("out_proj_bwd", dmix, w_out_f, "nt", F32)

    fdq, fdk, fdv, dcum_row, dcum_q = _fox_bwd(proj, fox_o, dmixcat, fox_lse, cum_row, n_fox)
    dcum_k = jnp.pad(dcum_row.reshape(n_fox, S), ((0, LANES - n_fox), (0, 0)))
    dfg, db_forget = _fox_gate_bwd(dcum_k, dcum_q, fg, b_pad)

    group_w = (n_swa // n_kv) * HEAD_DIM
    sdq, sdk, sdv, dsink = _swa_bwd(rq, proj, v_first, sinks, swa_o, dmixcat, fox_w // group_w, swa_lse, n_swa, n_kv)
    drq = jnp.concatenate([sdq, jnp.transpose(sdk, (1, 0, 2)).reshape(S, kv_w).astype(BF16)], axis=1)
    d_sq_sk = _rope("rope_bwd", drq, 0, n_swa + n_kv, cos, -sin_signed)
    dsv = jnp.transpose(sdv, (1, 0, 2)).reshape(S, kv_w).astype(BF16)
    dproj = jnp.concatenate([fdq, fdk, fdv, d_sq_sk, dsv], axis=1)

    pieces = []
    for s in range(N_CHIPS):
        lo, hi = s * in_rows, (s + 1) * in_rows
        for src, first, last, shift in [(dproj, 0, gate_lo, 0), (dfg, gate_lo, gate_lo + n_fox, gate_lo),
                                        (dproj, gate_lo + n_fox, in_w, n_fox)]:
            if max(lo, first) < min(hi, last):
                pieces.append(src[:, max(lo, first) - shift:min(hi, last) - shift])
        pieces.append(jnp.zeros((S, in_rows_pad - in_rows), BF16))
    dproj_slab = jnp.concatenate(pieces, axis=1)

    tm_in, tm_out = in_rows_pad // 2, D // (2 * N_CHIPS)
    pair_in = pair_send("in", _grad_half("grad_w_in_a", core_arr, dproj_slab, h, N_CHIPS, 1, tm_in, True))
    pair_out = pair_send("out", _grad_half("grad_w_out_a", core_arr, mixcat, dmix, N_CHIPS, 1, tm_out, True,
                                           after=pair_in[4]))
    sum_in = _grad_half("grad_w_in_b", core_arr, dproj_slab, h, N_CHIPS, 1, tm_in, False,
                        recv=pair_recv("in", pair_in, pair_out[4]))
    dh = _mm_plain("in_proj_bwd", dproj_slab, w_slab_t, "nn", BF16, tk=slab_w // 2,
                   after=sum_in[0, :8, :LANES].astype(F32))
    grad_x, acc_pre = _pre_mix_bwd(dh, x2, dx1, g_pre_mix, sc_a)

    zero_row = jnp.zeros((1, D), F32)
    tail = jnp.concatenate([db_forget[0:1, :n_fox], dsink[:, 0, :n_swa // n_kv].reshape(1, n_swa),
                            loss_part[0:1, 0:1], jnp.zeros((1, D - n_fox - n_swa - 1), F32)], axis=1)
    partial = jnp.concatenate([
        acc_pre[0:1], acc_pre[1:2], acc_mid[3:4], acc_mid[0:1], acc_mid[1:2], acc_mlp_post[0:1],
        acc_pre[2:3], acc_mid[4:5], acc_mid[2:3], acc_mlp_post[1:2], tail] + [zero_row] * 5, axis=0)
    gathered_small, token = _allgather8("gather_small_grads", partial)

    sum_out = _grad_half("grad_w_out_b", core_arr, mixcat, dmix, N_CHIPS, 1, tm_out, False,
                         recv=pair_recv("out", pair_out, token))
    flight_mix = scatter_start("mix", [sum_in, sum_out])
    halves_mlp = scatter_finish("mlp", flight_mlp, flight_mix[4])
    share_up, share_down = [
        _ici_start("grad_share_start_" + n, [hv], [jax.ShapeDtypeStruct(hv.shape, F32)], _share_plan, per_source=1)
        for n, hv in zip(["up", "down"], halves_mlp)]

    def shared(tag, flight, after):
        send, recv, own, lands, _ = flight
        own, other = _ici_wait("grad_share_wait_" + tag, send, recv, own, lands, _share_plan, after)
        return own[0], other[0]

    def unpack(p):
        return {"b_mod": p[0:N_MOD].reshape(1, N_MOD * D), "g_pre_mix": p[6:7], "g_post_mix": p[7:8],
                "g_pre_mlp": p[8:9], "g_post_mlp": p[9:10], "b_forget": p[10:11, :n_fox],
                "swa_sinks": p[10:11, n_fox:n_fox + n_swa]}

    small_out = _small_update(gathered_small, _tie(small_state[0], share_down[4]), small_state[1], small_state[2])
    g_small, d_small, m_small, v_small = [unpack(p) for p in small_out]
    loss = small_out[0][N_MOD + 4, n_fox + n_swa]

    dmod_all = gathered_small.reshape(N_DEV, 16, D)[:, :N_MOD].reshape(N_DEV, N_MOD * D)
    dmod_shard = _tie(lax.dynamic_slice_in_dim(dmod_all, chip * mod_cols, mod_cols, axis=1), share_down[4])
    g_w_mod, d_w_mod, nm_w_mod, nv_w_mod = _mod_update(c_all.T, dmod_shard, w_mod[0], m_w_mod[0], v_w_mod[0])

    grads = dict(g_small, w_mod=g_w_mod[None])
    deltas = dict(d_small, w_mod=d_w_mod[None])
    new_m = dict(m_small, w_mod=nm_w_mod[None])
    new_v = dict(v_small, w_mod=nv_w_mod[None])
    weights = {"w_in": (w_in, m_w_in, v_w_in), "w_out": (w_out, m_w_out, v_w_out), "w_up": (w_up, m_w_up, v_w_up),
               "w_down": (w_down, m_w_down, v_w_down)}

    def big_update(n, own, other):
        transposed = n == "w_in"
        w, m, v = in_state if transposed else [a[0] for a in weights[n]]
        outs = _adam_halves("adam_" + n, core_arr, w, own, other, m, v)
        if transposed:
            outs = [o[:in_rows].T for o in outs]
        grads[n], deltas[n], new_m[n], new_v[n] = [o[None] for o in outs]

    big_update("w_down", *shared("down", share_down, d_w_mod[:8, :LANES] + small_out[1][:8, :LANES]))
    halves_mix = scatter_finish("mix", flight_mix, deltas["w_down"][0, :8, :LANES] + d_w_mod[:8, :LANES])
    others_mix = _pair_share("grad_pair_share_mix", halves_mix)
    big_update("w_in", halves_mix[0], others_mix[0])
    big_update("w_out", halves_mix[1], others_mix[1])
    big_update("w_up", *shared("up", share_up, deltas["w_out"][0, :8, :LANES] + deltas["w_in"][0, :8, :LANES]))

    order = ["w_mod", "b_mod", "g_pre_mix", "g_post_mix", "w_in", "b_forget", "swa_sinks", "w_out", "g_pre_mlp",
             "g_post_mlp", "w_up", "w_down"]
    return (loss, grad_x[None], *[grads[n] for n in order], *[deltas[n] for n in order],
            *[new_m[n] for n in order], *[new_v[n] for n in order])
```

```python
import jax
import jax.numpy as jnp
from jax import lax
from jax.experimental import pallas as pl
from jax.experimental.pallas import tpu as pltpu

F32 = jnp.float32
BF16 = jnp.bfloat16
MESH = pl.DeviceIdType.MESH

HEAD_DIM = 128
SWA_BLOCK = 128
ROPE_THETA = 10000.0
NORM_EPS = 1e-6
NEG = -1e30
N_MOD = 6
ADAM_LR = 0.001
ADAM_B1 = 0.9
ADAM_B2 = 0.999
ADAM_EPS = 1e-08
ADAM_WD = 0.01
ADAM_STEP = 10
N_CHIPS = 4
N_DEV = 8
LANES = 128
VMEM_CAP = 60 * 1024 * 1024

_NN = (((1,), (0,)), ((), ()))
_NT = (((1,), (1,)), ((), ()))
_TN = (((0,), (0,)), ((), ()))


def _vmem(nbytes):
    return int(min(VMEM_CAP, nbytes * 5 // 4 + (4 << 20)))


def _nbytes(shape, dtype):
    n = 1
    for s in shape:
        n *= s
    return n * jnp.dtype(dtype).itemsize


def _fit(t, n):
    t = min(t, n)
    assert n % t == 0, (t, n)
    return t


MM_TM, MM_TN, MM_TK = 1024, 1024, 2048


def _matmul(name, a, b, mode, out_defs, epilogue, extras=(), tm=MM_TM, tn=MM_TN, tk=MM_TK, revisits=False,
            row_sel=None):
    stacked = b.ndim == 3
    b_rows, b_cols = b.shape[-2], b.shape[-1] * (b.shape[0] if stacked else 1)
    if mode == "nn":
        (M, K), (K2, N) = a.shape, (b_rows, b_cols)
    elif mode == "nt":
        (M, K), (N, K2) = a.shape, (b_rows, b_cols)
    else:
        (K, M), (K2, N) = a.shape, (b_rows, b_cols)
    assert K == K2 and not (stacked and mode == "tn"), (a.shape, b.shape, mode)
    tm = _fit(tm, M)
    tn = _fit(tn, b.shape[-1] if stacked and mode == "nn" else N)
    tk = _fit(tk, b.shape[-1] if stacked and mode == "nt" else K)
    nk = K // tk
    dims = {"nn": _NN, "nt": _NT, "tn": _TN}[mode]
    if row_sel is None:
        grid_m, a_row = M // tm, lambda i, *sel: i
    else:
        grid_m, a_row = row_sel[2], lambda i, *sel: row_sel[1](i, sel[0])
    a_spec = (pl.BlockSpec((tk, tm), lambda i, j, k, *sel: (k, a_row(i, *sel))) if mode == "tn"
              else pl.BlockSpec((tm, tk), lambda i, j, k, *sel: (a_row(i, *sel), k)))
    if stacked:
        per = b.shape[-1] // (tk if mode == "nt" else tn)
        b_spec = (pl.BlockSpec((1, tn, tk), lambda i, j, k, *sel: (k // per, j, k % per)) if mode == "nt"
                  else pl.BlockSpec((1, tk, tn), lambda i, j, k, *sel: (j // per, k, j % per)))
    else:
        b_spec = (pl.BlockSpec((tn, tk), lambda i, j, k, *sel: (j, k)) if mode == "nt"
                  else pl.BlockSpec((tk, tn), lambda i, j, k, *sel: (k, j)))
    n_ex, n_out = len(extras), len(out_defs)

    def body(*refs):
        if row_sel is not None:
            refs = refs[1:]
        a_ref, b_ref = refs[0], refs[1]
        ex = refs[2:2 + n_ex]
        outs = refs[2 + n_ex:2 + n_ex + n_out]
        b_blk = b_ref[0] if stacked else b_ref[...]
        prod = lax.dot_general(a_ref[...], b_blk, dims, preferred_element_type=F32)
        if nk == 1:
            epilogue(prod, ex, outs)
        else:
            acc_ref = refs[-1]
            k = pl.program_id(2)

            @pl.when(k == 0)
            def _():
                acc_ref[...] = prod

            @pl.when(k > 0)
            def _():
                acc_ref[...] += prod

            @pl.when(k == nk - 1)
            def _():
                epilogue(acc_ref[...], ex, outs)

    def wrap(f):
        return lambda i, j, k, *sel: f(i, j)

    in_specs = [a_spec, b_spec] + [pl.BlockSpec(blk, wrap(f)) for _, blk, f in extras]
    out_specs = [pl.BlockSpec(blk, wrap(f)) for _, _, blk, f in out_defs]
    out_shape = [jax.ShapeDtypeStruct(s, d) for s, d, _, _ in out_defs]
    need = 2 * (tm * tk + tk * tn) * a.dtype.itemsize + 3 * tm * tn * 4
    need += sum(2 * _nbytes(blk, arr.dtype) for arr, blk, _ in extras)
    need += sum(2 * _nbytes(blk, d) for _, d, blk, _ in out_defs)
    grid = (grid_m, N // tn, nk)
    scratch = [pltpu.VMEM((tm, tn), F32)] if nk > 1 else []
    params = pltpu.CompilerParams(
        dimension_semantics=("parallel", "arbitrary" if revisits else "parallel", "arbitrary"),
        vmem_limit_bytes=_vmem(need))
    operands = (a, b, *[arr for arr, _, _ in extras])
    if row_sel is None:
        return pl.pallas_call(body, name=name, grid=grid, in_specs=in_specs, out_specs=out_specs, out_shape=out_shape,
                              scratch_shapes=scratch, compiler_params=params)(*operands)
    grid_spec = pltpu.PrefetchScalarGridSpec(num_scalar_prefetch=1, grid=grid, in_specs=in_specs, out_specs=out_specs,
                                             scratch_shapes=scratch)
    return pl.pallas_call(body, name=name, grid_spec=grid_spec, out_shape=out_shape,
                          compiler_params=params)(row_sel[0], *operands)


def _grad_half(name, core, a, b, row_slabs, col_slabs, tm, other, recv=None, after=None):
    (_, M), (_, N) = a.shape, b.shape
    H = M // (2 * row_slabs)
    nh = H // tm
    tn = _fit(MM_TN, N // col_slabs)
    per = N // col_slabs // tn

    def a_block(i, core_ref):
        half = (1 - core_ref[0]) if other else core_ref[0]
        return (i // nh) * (2 * nh) + half * nh + i % nh

    def out_index(i, j):
        return (j // per, i, j % per) if col_slabs > 1 else (i // nh, i % nh, j)

    slabs = max(row_slabs, col_slabs)
    out_def = ((slabs, H, N // col_slabs), BF16, (1, tm, tn), out_index)

    def epilogue(acc, ex, outs):
        outs[0][0] = (acc if recv is None else acc + ex[0][0].astype(F32)).astype(BF16)

    extras = ([] if recv is None else [(recv, (1, tm, tn), out_index)]) + ([] if after is None else [_behind(after)])
    return _matmul(name, a, b, "tn", [out_def], epilogue, extras=extras, tm=tm, tn=tn,
                   row_sel=(core, a_block, row_slabs * nh))[0]


def _behind(token):
    return (token, (8, LANES), lambda i, j: (0, 0))


def _mm_plain(name, a, b, mode, out_dtype, after=None, **tiles):
    if mode == "nn":
        M, N = a.shape[0], b.shape[-1] * (b.shape[0] if b.ndim == 3 else 1)
    elif mode == "nt":
        M, N = a.shape[0], b.shape[-2]
    else:
        M, N = a.shape[1], b.shape[1]
    tm, tn = _fit(tiles.get("tm", MM_TM), M), _fit(tiles.get("tn", MM_TN), N)

    def epi(acc, ex, outs):
        outs[0][...] = acc.astype(out_dtype)

    return _matmul(name, a, b, mode, [((M, N), out_dtype, (tm, tn), lambda i, j: (i, j))], epi,
                   extras=[] if after is None else [_behind(after)], **tiles)[0]


def _rstd(v):
    return lax.rsqrt(jnp.mean(v * v, axis=-1, keepdims=True) + NORM_EPS)


def _row_call(name, body, row_ins, vec_ins, row_outs, acc_outs, S, D, tr):
    tr = _fit(tr, S)
    row_spec = pl.BlockSpec((tr, D), lambda r: (r, 0))
    vec_spec = pl.BlockSpec((1, D), lambda r: (0, 0))
    in_specs = [row_spec] * len(row_ins) + [vec_spec] * len(vec_ins)
    out_specs = [row_spec] * len(row_outs) + [pl.BlockSpec(shp, lambda r: (0, 0)) for shp in acc_outs]
    out_shape = [jax.ShapeDtypeStruct((S, D), d) for d in row_outs] + [jax.ShapeDtypeStruct(shp, F32) for shp in acc_outs]
    need = sum(2 * tr * D * a.dtype.itemsize for a in row_ins) + sum(2 * tr * D * jnp.dtype(d).itemsize for d in row_outs)
    need += 8 * tr * D * 4
    return pl.pallas_call(
        body, name=name, grid=(S // tr,), in_specs=in_specs, out_specs=out_specs, out_shape=out_shape,
        compiler_params=pltpu.CompilerParams(dimension_semantics=("arbitrary",), vmem_limit_bytes=_vmem(need)),
    )(*row_ins, *vec_ins)


def _acc_rows(ref, rows):
    @pl.when(pl.program_id(0) == 0)
    def _():
        ref[...] = jnp.zeros_like(ref)
    for n, r in enumerate(rows):
        ref[n:n + 1, :] += r


def _pre_norm(x, g, sc, sh):
    S, D = x.shape

    def body(x_ref, g_ref, sc_ref, sh_ref, h_ref):
        xv = x_ref[...]
        xn = xv * _rstd(xv)
        h_ref[...] = (xn * g_ref[...] * (1.0 + sc_ref[...]) + sh_ref[...]).astype(BF16)

    return _row_call("pre_norm_mix", body, [x], [g, sc, sh], [BF16], [], S, D, 256)[0]


def _post_mix(x, mix, g_post, gt, g_pre, sc, sh):
    S, D = x.shape

    def body(x_ref, mix_ref, gp_ref, gt_ref, g2_ref, sc_ref, sh_ref, x1_ref, h2_ref):
        mv = mix_ref[...].astype(F32)
        x1 = x_ref[...] + gt_ref[...] * (mv * _rstd(mv) * gp_ref[...])
        x1_ref[...] = x1
        h2_ref[...] = (x1 * _rstd(x1) * g2_ref[...] * (1.0 + sc_ref[...]) + sh_ref[...]).astype(BF16)

    return _row_call("post_mix_pre_mlp", body, [x, mix], [g_post, gt, g_pre, sc, sh], [F32, BF16], [], S, D, 256)


def _loss_and_post_mlp_bwd(x1, y, target, g_post, gt):
    S, D = x1.shape

    def body(x1_ref, y_ref, t_ref, g_ref, gt_ref, dy_ref, dout_ref, loss_ref, acc_ref):
        yv = y_ref[...].astype(F32)
        r = _rstd(yv)
        yh = yv * r
        n = yh * g_ref[...]
        diff = x1_ref[...] + gt_ref[...] * n - t_ref[...]
        dout = diff * (1.0 / D)
        dout_ref[...] = dout
        dn = dout * gt_ref[...]
        dyh = dn * g_ref[...]
        dy_ref[...] = (r * (dyh - yh * jnp.mean(dyh * yh, axis=-1, keepdims=True))).astype(BF16)
        _acc_rows(acc_ref, [jnp.sum(dout * n, axis=0, keepdims=True), jnp.sum(dn * yh, axis=0, keepdims=True)])

        @pl.when(pl.program_id(0) == 0)
        def _():
            loss_ref[...] = jnp.zeros_like(loss_ref)
        loss_ref[...] += jnp.full(loss_ref.shape, (0.5 / D) * jnp.sum(diff * diff), F32)

    return _row_call("loss_post_mlp_bwd", body, [x1, y, target], [g_post, gt], [BF16, F32],
                     [(8, LANES), (8, D)], S, D, 128)


def _pre_mlp_and_post_mix_bwd(dh2, x1, dout, mix, g_pre, sc, g_post, gt):
    S, D = x1.shape

    def body(dh_ref, x1_ref, dout_ref, mix_ref, g_ref, sc_ref, gp_ref, gt_ref, dx1_ref, dmix_ref, acc_ref):
        dh = dh_ref[...].astype(F32)
        x1v = x1_ref[...]
        r3 = _rstd(x1v)
        xn = x1v * r3
        dxn = dh * (1.0 + sc_ref[...]) * g_ref[...]
        dx1 = dout_ref[...] + r3 * (dxn - xn * jnp.mean(dxn * xn, axis=-1, keepdims=True))
        dx1_ref[...] = dx1
        mv = mix_ref[...].astype(F32)
        r2 = _rstd(mv)
        mh = mv * r2
        dn = dx1 * gt_ref[...]
        dmh = dn * gp_ref[...]
        dmix_ref[...] = (r2 * (dmh - mh * jnp.mean(dmh * mh, axis=-1, keepdims=True))).astype(BF16)
        _acc_rows(acc_ref, [
            jnp.sum(dh, axis=0, keepdims=True),
            jnp.sum(dh * xn * g_ref[...], axis=0, keepdims=True),
            jnp.sum(dh * (1.0 + sc_ref[...]) * xn, axis=0, keepdims=True),
            jnp.sum(dx1 * mh * gp_ref[...], axis=0, keepdims=True),
            jnp.sum(dn * mh, axis=0, keepdims=True)])

    return _row_call("pre_mlp_post_mix_bwd", body, [dh2, x1, dout, mix], [g_pre, sc, g_post, gt], [F32, BF16],
                     [(8, D)], S, D, 128)


def _pre_mix_bwd(dh, x, dx1, g_pre, sc):
    S, D = x.shape

    def body(dh_ref, x_ref, dx1_ref, g_ref, sc_ref, gx_ref, acc_ref):
        dhv = dh_ref[...].astype(F32)
        xv = x_ref[...]
        r = _rstd(xv)
        xn = xv * r
        dxn = dhv * (1.0 + sc_ref[...]) * g_ref[...]
        gx_ref[...] = dx1_ref[...] + r * (dxn - xn * jnp.mean(dxn * xn, axis=-1, keepdims=True))
        _acc_rows(acc_ref, [
            jnp.sum(dhv, axis=0, keepdims=True),
            jnp.sum(dhv * xn * g_ref[...], axis=0, keepdims=True),
            jnp.sum(dhv * (1.0 + sc_ref[...]) * xn, axis=0, keepdims=True)])

    return _row_call("pre_mix_bwd", body, [dh, x, dx1], [g_pre, sc], [F32], [(8, D)], S, D, 128)


CUM_BLOCK = 256


def _tri(n, upper):
    r = lax.broadcasted_iota(jnp.int32, (n, n), 0)
    c = lax.broadcasted_iota(jnp.int32, (n, n), 1)
    return ((c >= r) if upper else (c <= r)).astype(F32)


def _fox_gate_fwd(fg, b_pad):
    S = fg.shape[0]
    cb = _fit(CUM_BLOCK, S)

    def body(fg_ref, b_ref, cumt_ref, cum_ref):
        low = _tri(cb, False)
        carry = jnp.zeros((1, LANES), F32)
        for n in range(S // cb):
            z = fg_ref[n * cb:(n + 1) * cb, :] + b_ref[...]
            logf = jnp.minimum(z, 0.0) - jnp.log(1.0 + jnp.exp(-jnp.abs(z)))
            blk = jnp.dot(low, logf, precision=lax.Precision.HIGHEST, preferred_element_type=F32) + carry
            cum_ref[n * cb:(n + 1) * cb, :] = blk
            carry = blk[cb - 1:cb, :]
        cumt_ref[...] = cum_ref[...].T

    return pl.pallas_call(
        body, name="fox_gate_fwd", out_shape=jax.ShapeDtypeStruct((LANES, S), F32),
        scratch_shapes=[pltpu.VMEM((S, LANES), F32)],
        compiler_params=pltpu.CompilerParams(vmem_limit_bytes=_vmem(6 * S * LANES * 4)),
    )(fg, b_pad)


def _fox_gate_bwd(dcum_k, dcum_q, fg, b_pad):
    S = fg.shape[0]
    n_fox = dcum_q.shape[0]
    cb = _fit(CUM_BLOCK, S)

    def body(dk_ref, dq_ref, fg_ref, b_ref, dfg_ref, db_ref, dc_ref):
        lane = lax.broadcasted_iota(jnp.int32, (S, LANES), 1)
        dc = dk_ref[...].T
        for h in range(n_fox):
            dc = dc + jnp.where(lane == h, dq_ref[h], 0.0)
        dc_ref[...] = dc
        up = _tri(cb, True)
        carry = jnp.zeros((1, LANES), F32)
        db = jnp.zeros((1, LANES), F32)
        for n in reversed(range(S // cb)):
            blk = jnp.dot(up, dc_ref[n * cb:(n + 1) * cb, :], precision=lax.Precision.HIGHEST,
                          preferred_element_type=F32) + carry
            carry = blk[0:1, :]
            z = fg_ref[n * cb:(n + 1) * cb, :] + b_ref[...]
            dfg = blk * (1.0 / (1.0 + jnp.exp(z)))
            dfg_ref[n * cb:(n + 1) * cb, :] = dfg.astype(BF16)
            db = db + jnp.sum(dfg, axis=0, keepdims=True)
        db_ref[...] = jnp.broadcast_to(db, db_ref.shape)

    return pl.pallas_call(
        body, name="fox_gate_bwd",
        out_shape=[jax.ShapeDtypeStruct((S, LANES), BF16), jax.ShapeDtypeStruct((8, LANES), F32)],
        scratch_shapes=[pltpu.VMEM((S, LANES), F32)],
        compiler_params=pltpu.CompilerParams(vmem_limit_bytes=_vmem((8 + 2 * n_fox) * S * LANES * 4)),
    )(dcum_k, dcum_q, fg, b_pad)


FOX_TILE = 512


LOG2E = 1.4426950408889634


def _fox_scores(q, k, ck2, masked, t):
    s = lax.dot_general(q, k, _NT, preferred_element_type=F32) * (HEAD_DIM ** -0.5 * LOG2E) - ck2
    if masked:
        row = lax.broadcasted_iota(jnp.int32, (t, t), 0)
        col = lax.broadcasted_iota(jnp.int32, (t, t), 1)
        s = jnp.where(col <= row, s, NEG)
    return s


def _fox_fwd(proj, cum_row, n_fox):
    S = proj.shape[0]
    t = _fit(FOX_TILE, S)
    nq = S // t

    def body(q_ref, k_ref, v_ref, ck_ref, o_ref, lse_ref):
        def q_block(qi, _):
            q0 = pl.multiple_of(qi * t, t)
            q = q_ref[pl.ds(q0, t), :]

            def kv_block(j, carry, masked):
                m, l, acc = carry
                k0 = pl.multiple_of(j * t, t)
                s = _fox_scores(q, k_ref[pl.ds(k0, t), :], ck_ref[0, :, pl.ds(k0, t)] * LOG2E, masked, t)
                m_new = jnp.maximum(m, jnp.max(s, axis=-1, keepdims=True))
                alpha = jnp.exp2(m - m_new)
                p = jnp.exp2(s - m_new)
                l = alpha * l + jnp.sum(p, axis=-1, keepdims=True)
                acc = alpha * acc + jnp.dot(p.astype(BF16), v_ref[pl.ds(k0, t), :], preferred_element_type=F32)
                return m_new, l, acc

            init = (jnp.full((t, 1), NEG, F32), jnp.zeros((t, 1), F32), jnp.zeros((t, HEAD_DIM), F32))
            carry = lax.fori_loop(0, qi, lambda j, cr: kv_block(j, cr, False), init)
            m, l, acc = kv_block(qi, carry, True)
            o_ref[pl.ds(q0, t), :] = acc / l
            lse_ref[0, pl.ds(q0, t), :] = jnp.broadcast_to(m + jnp.log(l) * LOG2E, (t, LANES))
            return 0

        lax.fori_loop(0, nq, q_block, 0)

    col = lambda off: pl.BlockSpec((S, HEAD_DIM), lambda h: (0, off + h))
    per_head = pl.BlockSpec((1, S, LANES), lambda h: (h, 0, 0))
    return pl.pallas_call(
        body, name="fox_fwd", grid=(n_fox,),
        in_specs=[col(0), col(n_fox), col(2 * n_fox), pl.BlockSpec((1, 1, S), lambda h: (h, 0, 0))],
        out_specs=[pl.BlockSpec((S, HEAD_DIM), lambda h: (0, h)), per_head],
        out_shape=[jax.ShapeDtypeStruct((S, n_fox * HEAD_DIM), F32), jax.ShapeDtypeStruct((n_fox, S, LANES), F32)],
        compiler_params=pltpu.CompilerParams(dimension_semantics=("parallel",),
                                             vmem_limit_bytes=_vmem(16 * S * HEAD_DIM * 4 + 12 * t * t * 4)),
    )(proj, proj, proj, cum_row)


def _fox_bwd(proj, o, do, lse_b, cum_row, n_fox):
    S = proj.shape[0]
    t = _fit(FOX_TILE, S)
    nq = S // t
    scale = HEAD_DIM ** -0.5

    def body(q_ref, k_ref, v_ref, o_ref, do_ref, lse_ref, ck_ref, dq_ref, dk_ref, dv_ref, dc_ref, dcq_ref,
             dq_acc, delta_ref):
        dq_acc[...] = jnp.zeros_like(dq_acc)
        dcq_ref[...] = jnp.zeros_like(dcq_ref)

        def delta_block(qi, _):
            q0 = pl.multiple_of(qi * t, t)
            d = jnp.sum(do_ref[pl.ds(q0, t), :] * o_ref[pl.ds(q0, t), :], axis=-1, keepdims=True)
            delta_ref[pl.ds(q0, t), :] = jnp.broadcast_to(d, (t, LANES))
            return 0

        lax.fori_loop(0, nq, delta_block, 0)

        def kv_block(j, _):
            k0 = pl.multiple_of(j * t, t)
            k = k_ref[pl.ds(k0, t), :]
            v = v_ref[pl.ds(k0, t), :]
            ck2 = ck_ref[0, :, pl.ds(k0, t)] * LOG2E

            def q_block(qi, carry, masked):
                dk, dv, dc = carry
                q0 = pl.multiple_of(qi * t, t)
                q = q_ref[pl.ds(q0, t), :]
                dov = do_ref[pl.ds(q0, t), :].astype(BF16)
                p = jnp.exp2(_fox_scores(q, k, ck2, masked, t) - lse_ref[0, pl.ds(q0, t), :][:, :1])
                dp = lax.dot_general(dov, v, _NT, preferred_element_type=F32)
                ds = p * (dp - delta_ref[pl.ds(q0, t), :][:, :1])
                dsb = ds.astype(BF16)
                dv = dv + lax.dot_general(p.astype(BF16), dov, _TN, preferred_element_type=F32)
                dk = dk + lax.dot_general(dsb, q, _TN, preferred_element_type=F32)
                dq_acc[pl.ds(q0, t), :] += jnp.dot(dsb, k, preferred_element_type=F32)
                dc = dc - jnp.sum(ds, axis=0, keepdims=True)
                dcq_ref[0, pl.ds(q0, t), :] += jnp.broadcast_to(jnp.sum(ds, axis=1, keepdims=True), (t, LANES))
                return dk, dv, dc

            init = (jnp.zeros((t, HEAD_DIM), F32), jnp.zeros((t, HEAD_DIM), F32), jnp.zeros((1, t), F32))
            carry = q_block(j, init, True)
            dk, dv, dc = lax.fori_loop(j + 1, nq, lambda qi, cr: q_block(qi, cr, False), carry)
            dk_ref[pl.ds(k0, t), :] = (dk * scale).astype(BF16)
            dv_ref[pl.ds(k0, t), :] = dv.astype(BF16)
            dc_ref[0, :, pl.ds(k0, t)] = dc
            return 0

        lax.fori_loop(0, nq, kv_block, 0)
        dq_ref[...] = (dq_acc[...] * scale).astype(BF16)

    col = lambda off: pl.BlockSpec((S, HEAD_DIM), lambda h: (0, off + h))
    per_head = pl.BlockSpec((1, S, LANES), lambda h: (h, 0, 0))
    row = pl.BlockSpec((1, 1, S), lambda h: (h, 0, 0))
    grad = jax.ShapeDtypeStruct((S, n_fox * HEAD_DIM), BF16)
    return pl.pallas_call(
        body, name="fox_bwd", grid=(n_fox,),
        in_specs=[col(0), col(n_fox), col(2 * n_fox), col(0), col(0), per_head, row],
        out_specs=[col(0), col(0), col(0), row, per_head],
        out_shape=[grad, grad, grad, jax.ShapeDtypeStruct((n_fox, 1, S), F32), jax.ShapeDtypeStruct((n_fox, S, LANES), F32)],
        scratch_shapes=[pltpu.VMEM((S, HEAD_DIM), F32), pltpu.VMEM((S, LANES), F32)],
        compiler_params=pltpu.CompilerParams(dimension_semantics=("parallel",),
                                             vmem_limit_bytes=_vmem(24 * S * HEAD_DIM * 4 + 16 * t * t * 4)),
    )(proj, proj, proj, o, do, lse_b, cum_row)


def _rope_tables(S):
    half = HEAD_DIM // 2
    inv_freq = 1.0 / (ROPE_THETA ** (jnp.arange(half, dtype=F32) * (2.0 / HEAD_DIM)))
    ang = jnp.arange(S).astype(F32)[:, None] * inv_freq[None, :]
    cos, sin = jnp.cos(ang), jnp.sin(ang)
    return jnp.concatenate([cos, cos], axis=-1), jnp.concatenate([-sin, sin], axis=-1)


def _rope(name, src, first_block, n_blocks, cos, sin_signed):
    S = src.shape[0]

    def body(x_ref, cos_ref, sin_ref, o_ref):
        xv = x_ref[...].astype(F32)
        o_ref[...] = (xv * cos_ref[...] + pltpu.roll(xv, HEAD_DIM // 2, 1) * sin_ref[...]).astype(BF16)

    table = pl.BlockSpec((S, HEAD_DIM), lambda n: (0, 0))
    return pl.pallas_call(
        body, name=name, grid=(n_blocks,),
        in_specs=[pl.BlockSpec((S, HEAD_DIM), lambda n: (0, first_block + n)), table, table],
        out_specs=pl.BlockSpec((S, HEAD_DIM), lambda n: (0, n)),
        out_shape=jax.ShapeDtypeStruct((S, n_blocks * HEAD_DIM), BF16),
        compiler_params=pltpu.CompilerParams(dimension_semantics=("parallel",),
                                             vmem_limit_bytes=_vmem(12 * S * HEAD_DIM * 4)),
    )(src, cos, sin_signed)


def _swa_tile(q_ref, kp_ref, kc_ref, n, group, scale):
    B = SWA_BLOCK
    qs = jnp.concatenate([q_ref[:, g * HEAD_DIM:(g + 1) * HEAD_DIM] for g in range(group)], axis=0)
    kcat = jnp.concatenate([kp_ref[...], kc_ref[...]], axis=0)
    s = lax.dot_general(qs, kcat, _NT, preferred_element_type=F32) * scale
    qi = lax.broadcasted_iota(jnp.int32, (group * B, 2 * B), 0) % B
    kj = lax.broadcasted_iota(jnp.int32, (group * B, 2 * B), 1)
    diff = qi + B - kj
    mask = (diff >= 0) & (diff < B) & ((n * B + kj - B) >= 0)
    return qs, kcat, jnp.where(mask, s, NEG)


def _swa_sink_col(sink_ref, kv, group):
    head = lax.broadcasted_iota(jnp.int32, (group * SWA_BLOCK, 1), 0) // SWA_BLOCK
    col = jnp.zeros((group * SWA_BLOCK, 1), F32)
    for g in range(group):
        col = jnp.where(head == g, sink_ref[kv * group + g], col)
    return col


def _swa_specs(n_kv, group, q_first, k_first, v_first):
    B = SWA_BLOCK
    prev = lambda n: jnp.maximum(n - 1, 0)
    return [
        pl.BlockSpec((B, group * HEAD_DIM), lambda kv, n: (n, q_first + kv)),
        pl.BlockSpec((B, HEAD_DIM), lambda kv, n: (prev(n), k_first + kv)),
        pl.BlockSpec((B, HEAD_DIM), lambda kv, n: (n, k_first + kv)),
        pl.BlockSpec((B, HEAD_DIM), lambda kv, n: (prev(n), v_first + kv)),
        pl.BlockSpec((B, HEAD_DIM), lambda kv, n: (n, v_first + kv)),
    ]


def _swa_fwd(rq, proj, v_first, sinks, n_q, n_kv):
    S = rq.shape[0]
    B = SWA_BLOCK
    group = n_q // n_kv
    scale = HEAD_DIM ** -0.5

    def body(q_ref, kp_ref, kc_ref, vp_ref, vc_ref, sink_ref, o_ref, lse_ref):
        kv, n = pl.program_id(0), pl.program_id(1)
        _, _, s = _swa_tile(q_ref, kp_ref, kc_ref, n, group, scale)
        sink = _swa_sink_col(sink_ref, kv, group)
        m = jnp.maximum(jnp.max(s, axis=-1, keepdims=True), sink)
        p = jnp.exp(s - m)
        denom = jnp.sum(p, axis=-1, keepdims=True) + jnp.exp(sink - m)
        vcat = jnp.concatenate([vp_ref[...], vc_ref[...]], axis=0)
        o = jnp.dot((p / denom).astype(BF16), vcat, preferred_element_type=F32)
        lse = m + jnp.log(denom)
        for g in range(group):
            o_ref[:, g * HEAD_DIM:(g + 1) * HEAD_DIM] = o[g * B:(g + 1) * B, :]
            lse_ref[0, :, g * LANES:(g + 1) * LANES] = jnp.broadcast_to(lse[g * B:(g + 1) * B, :], (B, LANES))

    specs = _swa_specs(n_kv, group, 0, n_q, v_first)
    q_blk = pl.BlockSpec((B, group * HEAD_DIM), lambda kv, n: (n, kv))
    return pl.pallas_call(
        body, name="swa_fwd", grid=(n_kv, S // B),
        in_specs=specs + [pl.BlockSpec(memory_space=pltpu.SMEM)],
        out_specs=[q_blk, pl.BlockSpec((1, B, group * LANES), lambda kv, n: (kv, n, 0))],
        out_shape=[jax.ShapeDtypeStruct((S, n_q * HEAD_DIM), F32), jax.ShapeDtypeStruct((n_kv, S, group * LANES), F32)],
        compiler_params=pltpu.CompilerParams(dimension_semantics=("parallel", "arbitrary")),
    )(rq, rq, rq, proj, proj, sinks)


def _swa_bwd(rq, proj, v_first, sinks, o, do, do_first, lse_b, n_q, n_kv):
    S = rq.shape[0]
    B = SWA_BLOCK
    group = n_q // n_kv
    scale = HEAD_DIM ** -0.5

    def body(q_ref, kp_ref, kc_ref, vp_ref, vc_ref, o_ref, do_ref, lse_ref, sink_ref,
             dq_ref, dk_ref, dv_ref, dsink_ref):
        kv, n = pl.program_id(0), pl.program_id(1)

        @pl.when(n == 0)
        def _():
            dk_ref[...] = jnp.zeros_like(dk_ref)
            dv_ref[...] = jnp.zeros_like(dv_ref)
            dsink_ref[...] = jnp.zeros_like(dsink_ref)

        qs, kcat, s = _swa_tile(q_ref, kp_ref, kc_ref, n, group, scale)
        sink = _swa_sink_col(sink_ref, kv, group)
        stack = lambda ref, w: jnp.concatenate([ref[:, g * w:(g + 1) * w] for g in range(group)], axis=0)
        lse = jnp.concatenate([lse_ref[0, :, g * LANES:g * LANES + 1] for g in range(group)], axis=0)
        do32 = stack(do_ref, HEAD_DIM)
        delta = jnp.sum(do32 * stack(o_ref, HEAD_DIM), axis=-1, keepdims=True)
        dov = do32.astype(BF16)
        p = jnp.exp(s - lse)
        vcat = jnp.concatenate([vp_ref[...], vc_ref[...]], axis=0)
        dp = lax.dot_general(dov, vcat, _NT, preferred_element_type=F32)
        ds = p * (dp - delta)
        dsb = ds.astype(BF16)
        dq = jnp.dot(dsb, kcat, preferred_element_type=F32) * scale
        for g in range(group):
            dq_ref[:, g * HEAD_DIM:(g + 1) * HEAD_DIM] = dq[g * B:(g + 1) * B, :].astype(BF16)
        dkcat = lax.dot_general(dsb, qs, _TN, preferred_element_type=F32) * scale
        dvcat = lax.dot_general(p.astype(BF16), dov, _TN, preferred_element_type=F32)
        prev0 = pl.multiple_of(jnp.maximum(n - 1, 0) * B, B)
        cur0 = pl.multiple_of(n * B, B)
        dk_ref[0, pl.ds(prev0, B), :] += dkcat[:B, :]
        dk_ref[0, pl.ds(cur0, B), :] += dkcat[B:, :]
        dv_ref[0, pl.ds(prev0, B), :] += dvcat[:B, :]
        dv_ref[0, pl.ds(cur0, B), :] += dvcat[B:, :]
        dsk = -jnp.exp(sink - lse) * delta
        lane = lax.broadcasted_iota(jnp.int32, (1, LANES), 1)
        row = jnp.zeros((1, LANES), F32)
        for g in range(group):
            row = row + jnp.where(lane == g, jnp.sum(dsk[g * B:(g + 1) * B, :]), 0.0)
        dsink_ref[0, 0:1, :] += row

    specs = _swa_specs(n_kv, group, 0, n_q, v_first)
    q_blk = pl.BlockSpec((B, group * HEAD_DIM), lambda kv, n: (n, kv))
    acc = pl.BlockSpec((1, S, HEAD_DIM), lambda kv, n: (kv, 0, 0))
    return pl.pallas_call(
        body, name="swa_bwd", grid=(n_kv, S // B),
        in_specs=specs + [q_blk, pl.BlockSpec((B, group * HEAD_DIM), lambda kv, n: (n, do_first + kv)),
                          pl.BlockSpec((1, B, group * LANES), lambda kv, n: (kv, n, 0)),
                          pl.BlockSpec(memory_space=pltpu.SMEM)],
        out_specs=[q_blk, acc, acc, pl.BlockSpec((1, 8, LANES), lambda kv, n: (kv, 0, 0))],
        out_shape=[jax.ShapeDtypeStruct((S, n_q * HEAD_DIM), BF16), jax.ShapeDtypeStruct((n_kv, S, HEAD_DIM), F32),
                   jax.ShapeDtypeStruct((n_kv, S, HEAD_DIM), F32), jax.ShapeDtypeStruct((n_kv, 8, LANES), F32)],
        compiler_params=pltpu.CompilerParams(dimension_semantics=("parallel", "arbitrary")),
    )(rq, rq, rq, proj, proj, o, do, lse_b, sinks)


def _adamw(w, g, m, v):
    m = ADAM_B1 * m + (1.0 - ADAM_B1) * g
    v = ADAM_B2 * v + (1.0 - ADAM_B2) * (g * g)
    m_hat = m / (1.0 - ADAM_B1 ** ADAM_STEP)
    v_hat = v / (1.0 - ADAM_B2 ** ADAM_STEP)
    delta = -ADAM_LR * (m_hat / (jnp.sqrt(v_hat) + ADAM_EPS) + ADAM_WD * w)
    return delta, m, v


def _mod_fwd(cond_in, w_mod, b_shard):
    R, D = cond_in.shape
    cols = w_mod.shape[1]
    tn = _fit(512, cols)

    def body(c_ref, w_ref, b_ref, o_ref):
        cv = c_ref[...]
        cond = (cv / (1.0 + jnp.exp(-cv))).astype(BF16)
        o_ref[...] = jnp.dot(cond, w_ref[...].astype(BF16), preferred_element_type=F32) + b_ref[...]

    return pl.pallas_call(
        body, name="mod_fwd", grid=(cols // tn,),
        in_specs=[pl.BlockSpec((R, D), lambda j: (0, 0)), pl.BlockSpec((D, tn), lambda j: (0, j)),
                  pl.BlockSpec((1, tn), lambda j: (0, j))],
        out_specs=pl.BlockSpec((R, tn), lambda j: (0, j)),
        out_shape=jax.ShapeDtypeStruct((R, cols), F32),
        compiler_params=pltpu.CompilerParams(dimension_semantics=("parallel",), vmem_limit_bytes=_vmem(3 * D * tn * 4)),
    )(cond_in, w_mod, b_shard)


def _mod_update(c_t, dmod, w, m, v):
    D, nb = c_t.shape
    cols = w.shape[1]
    tn = _fit(256, cols)

    def body(c_ref, d_ref, w_ref, m_ref, v_ref, g_ref, dl_ref, nm_ref, nv_ref):
        cv = c_ref[...]
        cond = cv / (1.0 + jnp.exp(-cv))
        g = jnp.zeros((D, tn), F32)
        for b in range(nb):
            g = g + cond[:, b:b + 1] * d_ref[b:b + 1, :]
        g_ref[...] = g
        dl_ref[...], nm_ref[...], nv_ref[...] = _adamw(w_ref[...], g, m_ref[...], v_ref[...])

    blk = pl.BlockSpec((D, tn), lambda j: (0, j))
    out = jax.ShapeDtypeStruct((D, cols), F32)
    return pl.pallas_call(
        body, name="mod_update", grid=(cols // tn,),
        in_specs=[pl.BlockSpec((D, nb), lambda j: (0, 0)), pl.BlockSpec((nb, tn), lambda j: (0, j)), blk, blk, blk],
        out_specs=[blk] * 4, out_shape=[out] * 4,
        compiler_params=pltpu.CompilerParams(dimension_semantics=("parallel",), vmem_limit_bytes=_vmem(18 * D * tn * 4)),
    )(c_t, dmod, w, m, v)


def _small_update(stacked, w, m, v):
    R, C = w.shape

    def body(s_ref, w_ref, m_ref, v_ref, g_ref, dl_ref, nm_ref, nv_ref):
        g = s_ref[0:R, :]
        for d in range(1, N_DEV):
            g = g + s_ref[d * R:(d + 1) * R, :]
        g_ref[...] = g
        dl_ref[...], nm_ref[...], nv_ref[...] = _adamw(w_ref[...], g, m_ref[...], v_ref[...])

    return pl.pallas_call(body, name="small_update", out_shape=[jax.ShapeDtypeStruct((R, C), F32)] * 4)(stacked, w, m, v)


def _place():
    return lax.axis_index("x"), lax.axis_index("y"), lax.axis_index("c")


def _allgather8(name, block):
    m_per, n = block.shape

    def body(x_ref, out_ref, token_ref, send_sems, recv_sems, local_sem):
        token_ref[...] = jnp.zeros_like(token_ref)
        x, y, c = _place()
        me, sibling = (x, y, c), (x, y, 1 - c)
        chips = [(1 - x, y), (x, 1 - y), (1 - x, 1 - y)]

        def rows(px, py, pc):
            return out_ref.at[pl.ds((4 * px + 2 * py + pc) * m_per, m_per), :]

        def copy(k, blk, to, src=None):
            return pltpu.make_async_remote_copy(
                src_ref=rows(*blk) if src is None else src, dst_ref=rows(*blk),
                send_sem=send_sems.at[k], recv_sem=recv_sems.at[k], device_id=to, device_id_type=MESH)

        mine = pltpu.make_async_copy(x_ref, rows(*me), local_sem)
        mine.start()
        first = [copy(0, me, sibling, src=x_ref)]
        first += [copy(1 + j, me, (*chip, c), src=x_ref) for j, chip in enumerate(chips)]
        for cp in first:
            cp.start()
        passed = [copy(4 + j, (*chip, c), sibling) for j, chip in enumerate(chips)]
        for j, chip in enumerate(chips):
            copy(1 + j, (*chip, c), me).wait_recv()
            passed[j].start()
        copy(0, sibling, me).wait_recv()
        for j, chip in enumerate(chips):
            copy(4 + j, (*chip, 1 - c), me).wait_recv()
        for cp in first + passed:
            cp.wait_send()
        mine.wait()

    vmem = pl.BlockSpec(memory_space=pltpu.VMEM)
    return pl.pallas_call(
        body, name=name,
        out_shape=[jax.ShapeDtypeStruct((N_DEV * m_per, n), block.dtype), jax.ShapeDtypeStruct((8, LANES), F32)],
        in_specs=[vmem], out_specs=[vmem, vmem],
        scratch_shapes=[pltpu.SemaphoreType.DMA((7,)), pltpu.SemaphoreType.DMA((7,)), pltpu.SemaphoreType.DMA],
    )(block)


_ANY = pl.BlockSpec(memory_space=pl.ANY)


def _half(ref, c, rows):
    return ref.at[pl.ds(c * (rows // 2), rows // 2), :]


_HBM = pl.BlockSpec(memory_space=pltpu.HBM)
_SEM = pl.BlockSpec(memory_space=pltpu.SEMAPHORE)
_EFFECT = pltpu.SideEffectType.DATAFLOW_SIDE_EFFECTING


def _ici_start(name, srcs, land_shapes, plan, per_source=3, after=None):
    ns, nl = len(srcs), len(land_shapes)
    n_copies = per_source * ns
    n_in = ns + nl + (after is not None)

    def body(*refs):
        src_refs, land_refs = refs[:ns], refs[ns:ns + nl]
        send_sems, recv_sems = refs[n_in], refs[n_in + 1]
        token = refs[-1]
        for n, (src, dst, peer, _) in enumerate(plan(src_refs, land_refs)):
            pltpu.make_async_remote_copy(src_ref=src, dst_ref=dst, send_sem=send_sems.at[n], recv_sem=recv_sems.at[n],
                                         device_id=peer, device_id_type=MESH).start()
        token[...] = jnp.zeros_like(token)

    lands = [lax.empty(s.shape, s.dtype) for s in land_shapes]
    out = pl.pallas_call(
        body, name=name,
        out_shape=(pltpu.SemaphoreType.DMA((n_copies,)), pltpu.SemaphoreType.DMA((n_copies,)),
                   *[pltpu.HBM(a.shape, a.dtype) for a in list(srcs) + lands], jax.ShapeDtypeStruct((8, LANES), F32)),
        in_specs=[_HBM] * (ns + nl) + [_ANY] * (after is not None),
        out_specs=(_SEM, _SEM, *[_HBM] * (ns + nl), pl.BlockSpec(memory_space=pltpu.VMEM)),
        input_output_aliases={n: 2 + n for n in range(ns + nl)},
        compiler_params=pltpu.CompilerParams(has_side_effects=_EFFECT),
    )(*[pltpu.with_memory_space_constraint(a, pltpu.HBM) for a in list(srcs) + lands],
      *([] if after is None else [after]))
    return out[0], out[1], list(out[2:2 + ns]), list(out[2 + ns:2 + ns + nl]), out[-1]


def _ici_wait(name, send_sems, recv_sems, srcs, lands, plan, after):
    ns, nl = len(srcs), len(lands)
    after = list(after) if isinstance(after, (list, tuple)) else [after]

    def body(*refs):
        src_refs, land_refs = refs[:ns], refs[ns:ns + nl]
        send_sems, recv_sems = refs[ns + nl], refs[ns + nl + 1]
        for n, (src, _, peer, mine) in enumerate(plan(src_refs, land_refs)):
            cp = pltpu.make_async_remote_copy(src_ref=src, dst_ref=mine, send_sem=send_sems.at[n],
                                              recv_sem=recv_sems.at[n], device_id=peer, device_id_type=MESH)
            cp.wait_send()
            cp.wait_recv()

    out = pl.pallas_call(
        body, name=name, out_shape=[pltpu.HBM(a.shape, a.dtype) for a in list(srcs) + list(lands)],
        in_specs=[_HBM] * (ns + nl) + [_SEM, _SEM] + [_ANY] * len(after), out_specs=[_HBM] * (ns + nl),
        input_output_aliases={n: n for n in range(ns + nl)},
        compiler_params=pltpu.CompilerParams(has_side_effects=_EFFECT),
    )(*srcs, *lands, send_sems, recv_sems, *after)
    return list(out[:ns]), list(out[ns:])


def _own_slab(name, chip, w, after):
    R, C = w.shape
    tr, tc = _tiles(R, C)
    tied = [] if after is None else [after]

    def body(chip_ref, w_ref, *rest):
        stack_ref, token_ref = rest[-2:]
        stack_ref[0] = w_ref[...].astype(BF16)
        token_ref[...] = jnp.zeros_like(token_ref)

    small = pl.BlockSpec((8, LANES), lambda r, q, chip_ref: (0, 0))
    grid_spec = pltpu.PrefetchScalarGridSpec(
        num_scalar_prefetch=1, grid=(R // tr, C // tc),
        in_specs=[pl.BlockSpec((tr, tc), lambda r, q, chip_ref: (r, q))] + [small] * len(tied),
        out_specs=[pl.BlockSpec((1, tr, tc), lambda r, q, chip_ref: (chip_ref[0], r, q)), small])
    return pl.pallas_call(
        body, name=name, grid_spec=grid_spec,
        out_shape=[jax.ShapeDtypeStruct((N_CHIPS, R, C), BF16), jax.ShapeDtypeStruct((8, LANES), F32)],
        compiler_params=pltpu.CompilerParams(dimension_semantics=("arbitrary", "arbitrary")),
    )(chip, w, *tied)


def _gather_plan(src_refs, land_refs):
    x, y, c = _place()
    copies = []
    for stack in src_refs:
        R = stack.shape[1]
        own = _half(stack.at[2 * x + y], c, R)
        for cx, cy in [(1 - x, y), (x, 1 - y), (1 - x, 1 - y)]:
            copies.append((own, own, (cx, cy, c), _half(stack.at[2 * cx + cy], c, R)))
    return copies


def _pass_plan(src_refs, land_refs):
    x, y, c = _place()
    copies = []
    for land in src_refs:
        R = land.shape[1]
        for cx, cy in [(1 - x, y), (x, 1 - y), (1 - x, 1 - y)]:
            slot = land.at[2 * cx + cy]
            copies.append((_half(slot, c, R), _half(slot, c, R), (x, y, 1 - c), _half(slot, 1 - c, R)))
    return copies


def _share_plan(src_refs, land_refs):
    x, y, c = _place()
    return [(h, land, (x, y, 1 - c), land) for h, land in zip(src_refs, land_refs)]


def _pass_to_sibling(name, lands):
    nw = len(lands)

    def body(*refs):
        ins, outs = refs[:nw], refs[nw:2 * nw]
        send_sems, recv_sems = refs[2 * nw:]
        x, y, c = _place()
        chips = [(1 - x, y), (x, 1 - y), (1 - x, 1 - y)]
        copies = []
        for k in range(nw):
            R = ins[k].shape[1]
            for j, (cx, cy) in enumerate(chips):
                cp = pltpu.make_async_remote_copy(
                    src_ref=_half(ins[k].at[2 * cx + cy], c, R), dst_ref=_half(outs[k].at[2 * cx + cy], c, R),
                    send_sem=send_sems.at[3 * k + j], recv_sem=recv_sems.at[3 * k + j],
                    device_id=(x, y, 1 - c), device_id_type=MESH)
                cp.start()
                copies.append(cp)
        for k in range(nw):
            R = ins[k].shape[1]
            for j, (cx, cy) in enumerate(chips):
                pltpu.make_async_remote_copy(
                    src_ref=_half(ins[k].at[2 * cx + cy], c, R), dst_ref=_half(outs[k].at[2 * cx + cy], 1 - c, R),
                    send_sem=send_sems.at[3 * k + j], recv_sem=recv_sems.at[3 * k + j],
                    device_id=(x, y, 1 - c), device_id_type=MESH).wait_recv()
        for cp in copies:
            cp.wait_send()

    return pl.pallas_call(
        body, name=name, out_shape=[jax.ShapeDtypeStruct(a.shape, a.dtype) for a in lands],
        in_specs=[_ANY] * nw, out_specs=[_ANY] * nw, input_output_aliases={k: k for k in range(nw)},
        scratch_shapes=[pltpu.SemaphoreType.DMA((3 * nw,)), pltpu.SemaphoreType.DMA((3 * nw,))],
    )(*lands)


def _tie(vec, token):
    return vec + token[0:1, 0:1]


ROW_ALIGN = 16
TILE_ELEMS = 512 * 1024


def _tiles(rows, cols):
    fits = [t for t in range(ROW_ALIGN, min(rows, 256) + 1, ROW_ALIGN) if rows % t == 0]
    tr = fits[-1] if fits and fits[-1] >= 64 else rows
    tc = cols
    while tr * tc > TILE_ELEMS and tc % (2 * LANES) == 0:
        tc //= 2
    return tr, tc


def _scatter_plan(src_refs, land_refs):
    x, y, c = _place()
    copies = []
    for p, land in zip(src_refs, land_refs):
        for j, (cx, cy) in enumerate([(1 - x, y), (x, 1 - y), (1 - x, 1 - y)]):
            copies.append((p.at[2 * cx + cy], land.at[j], (cx, cy, c), land.at[j]))
    return copies


def _chip_add(name, chip, sums, recv):
    _, H, C = sums.shape
    tr, tc = _tiles(H, C)

    def body(chip_ref, p_ref, r_ref, o_ref):
        total = p_ref[0].astype(F32)
        for j in range(3):
            total = total + r_ref[j].astype(F32)
        o_ref[...] = total

    grid_spec = pltpu.PrefetchScalarGridSpec(
        num_scalar_prefetch=1, grid=(H // tr, C // tc),
        in_specs=[pl.BlockSpec((1, tr, tc), lambda r, q, chip_ref: (chip_ref[0], r, q)),
                  pl.BlockSpec((3, tr, tc), lambda r, q, chip_ref: (0, r, q))],
        out_specs=pl.BlockSpec((tr, tc), lambda r, q, chip_ref: (r, q)))
    return pl.pallas_call(
        body, name=name, grid_spec=grid_spec, out_shape=jax.ShapeDtypeStruct((H, C), F32),
        compiler_params=pltpu.CompilerParams(dimension_semantics=("parallel", "parallel")),
    )(chip, sums, recv)


def _pair_share(name, halves):
    nw = len(halves)

    def body(*refs):
        hs, outs = refs[:nw], refs[nw:2 * nw]
        send_sems, recv_sems = refs[2 * nw:]
        x, y, c = _place()
        copies = []
        for k in range(nw):
            cp = pltpu.make_async_remote_copy(
                src_ref=hs[k], dst_ref=outs[k], send_sem=send_sems.at[k], recv_sem=recv_sems.at[k],
                device_id=(x, y, 1 - c), device_id_type=MESH)
            cp.start()
            copies.append(cp)
        for cp in copies:
            cp.wait()

    return pl.pallas_call(
        body, name=name,
        out_shape=[jax.ShapeDtypeStruct(h.shape, h.dtype) for h in halves],
        in_specs=[_ANY] * nw, out_specs=[_ANY] * nw,
        scratch_shapes=[pltpu.SemaphoreType.DMA((nw,)), pltpu.SemaphoreType.DMA((nw,))],
    )(*halves)


def _adam_halves(name, core, w, g_own, g_other, m, v):
    R, C = w.shape
    H = R // 2
    tr, tc = _tiles(H, C)
    nr, nc = H // tr, C // tc

    def body(core_ref, w_ref, go_ref, gr_ref, m_ref, v_ref, g_ref, dl_ref, nm_ref, nv_ref):
        own = (pl.program_id(0) // nr) == core_ref[0]
        g = jnp.where(own, go_ref[...], gr_ref[...])
        g_ref[...] = g
        dl_ref[...], nm_ref[...], nv_ref[...] = _adamw(w_ref[...], g, m_ref[...], v_ref[...])

    blk = pl.BlockSpec((tr, tc), lambda r, q, core_ref: (r, q))

    def half_spec(is_own):
        def index(r, q, core_ref):
            mine = ((r // nr) == core_ref[0]) == is_own
            done = is_own == (core_ref[0] == 0)
            return (jnp.where(mine, r % nr, jnp.where(done, nr - 1, 0)), jnp.where(mine, q, jnp.where(done, nc - 1, 0)))
        return pl.BlockSpec((tr, tc), index)
    out = jax.ShapeDtypeStruct((R, C), F32)
    grid_spec = pltpu.PrefetchScalarGridSpec(
        num_scalar_prefetch=1, grid=(R // tr, nc), in_specs=[blk, half_spec(True), half_spec(False), blk, blk],
        out_specs=[blk] * 4)
    return pl.pallas_call(
        body, name=name, grid_spec=grid_spec, out_shape=[out] * 4,
        compiler_params=pltpu.CompilerParams(dimension_semantics=("parallel", "parallel"),
                                             vmem_limit_bytes=_vmem(20 * tr * tc * 4)),
    )(core, w, g_own, g_other, m, v)


def kernel(x, c, w_mod, b_mod, g_pre_mix, g_post_mix, w_in, b_forget, swa_sinks, w_out, g_pre_mlp, g_post_mlp, w_up, w_down, loss_target, m_w_mod, m_b_mod, m_g_pre_mix, m_g_post_mix, m_w_in, m_b_forget, m_swa_sinks, m_w_out, m_g_pre_mlp, m_g_post_mlp, m_w_up, m_w_down, v_w_mod, v_b_mod, v_g_pre_mix, v_g_post_mix, v_w_in, v_b_forget, v_swa_sinks, v_w_out, v_g_pre_mlp, v_g_post_mlp, v_w_up, v_w_down):
    S, D = x.shape[1], x.shape[2]
    n_heads = D // HEAD_DIM
    n_fox = n_heads // 2
    n_swa = n_heads - n_fox
    n_kv = max(1, n_swa // 4)
    fox_w, swa_w, kv_w = n_fox * HEAD_DIM, n_swa * HEAD_DIM, n_kv * HEAD_DIM
    main_w = 3 * fox_w + swa_w + 2 * kv_w
    in_w = main_w + n_fox
    mod_cols = w_mod.shape[2]

    ax, ay, ac = _place()
    chip = 2 * ax + ay
    dev = 2 * chip + ac
    chip_arr = jnp.reshape(chip, (1,)).astype(jnp.int32)
    core_arr = jnp.reshape(ac, (1,)).astype(jnp.int32)

    x2, tgt = x[0], loss_target[0]

    in_rows = in_w // N_CHIPS
    in_rows_pad = -(-in_rows // (2 * LANES)) * (2 * LANES)
    slab_w = N_CHIPS * in_rows_pad

    def rows_of(a):
        return jnp.pad(a[0].T, ((0, in_rows_pad - in_rows), (0, 0)))

    w_in_stack, token = _own_slab("own_slab_w_in", chip_arr, rows_of(w_in), None)

    c_all, _ = _allgather8("gather_c", _tie(c, token).reshape(8, D // 8))
    c_all = c_all.reshape(N_DEV, D)
    b_shard = lax.dynamic_slice_in_dim(b_mod, chip * mod_cols, mod_cols, axis=1)
    mod_shard = _mod_fwd(jnp.pad(c_all, ((0, 16 - N_DEV), (0, 0))), w_mod[0], b_shard)[:N_DEV]
    mod_all, token = _allgather8("gather_mod", mod_shard)
    mod_all = mod_all.reshape(N_CHIPS, 2, N_DEV, mod_cols)[:, 0]
    mod = lax.dynamic_index_in_dim(mod_all, dev, axis=1, keepdims=False).reshape(N_MOD, 1, D)
    sh_a, sc_a, gt_a, sh_m, sc_m, gt_m = [mod[n] for n in range(N_MOD)]

    def slab_cols(lo, hi):
        spans = []
        while lo < hi:
            s, r = divmod(lo, in_rows)
            n = min(hi - lo, in_rows - r)
            spans.append((s * in_rows_pad + r, s * in_rows_pad + r + n))
            lo += n
        return spans

    gate_lo = 3 * fox_w
    main_spans = slab_cols(0, gate_lo) + slab_cols(gate_lo + n_fox, in_w)
    (gate_first, gate_last), = slab_cols(gate_lo, gate_lo + n_fox)

    names = ["w_in", "w_out", "w_up", "w_down"]
    flights = {}
    for n, w in zip(names, [None, w_out[0], w_up[0], w_down[0]]):
        stack = w_in_stack if n == "w_in" else _own_slab("own_slab_" + n, chip_arr, w, token)[0]
        flights[n] = _ici_start("gather_start_" + n, [stack], [], _gather_plan, after=token)
        token = flights[n][4]
    sc_a = _tie(sc_a, token)

    def arrived(n, after):
        send, recv, stacks, _, _ = flights[n]
        stacks, _ = _ici_wait("gather_wait_" + n, send, recv, stacks, [], _gather_plan, after)
        return _ici_start("gather_pass_start_" + n, stacks, [], _pass_plan)

    def gathered(n, after, in_flight=None):
        if in_flight is None:
            send, recv, stacks, _, _ = flights[n]
            stacks, _ = _ici_wait("gather_wait_" + n, send, recv, stacks, [], _gather_plan, after)
            return _pass_to_sibling("gather_pass_" + n, stacks)[0]
        send, recv, stacks, _, _ = in_flight
        return _ici_wait("gather_pass_wait_" + n, send, recv, stacks, [], _pass_plan, after)[0][0]

    d_ff = N_CHIPS * w_up.shape[2]

    h = _pre_norm(x2, g_pre_mix, sc_a, sh_a)
    in_state = [rows_of(w_in)] + [rows_of(_tie(a, token)) for a in (m_w_in, v_w_in)]
    cos, sin_signed = _rope_tables(S)

    def pack(bm, gpm, gqm, gpl, gql, bf, sk):
        last = jnp.concatenate([bf, sk, jnp.zeros((1, D - n_fox - n_swa), F32)], axis=1)
        return jnp.concatenate([bm.reshape(N_MOD, D), gpm, gqm, gpl, gql, last, jnp.zeros((5, D), F32)], axis=0)

    small_state = [pack(b_mod, g_pre_mix, g_post_mix, g_pre_mlp, g_post_mlp, b_forget, swa_sinks),
                   pack(m_b_mod, m_g_pre_mix, m_g_post_mix, m_g_pre_mlp, m_g_post_mlp, m_b_forget, m_swa_sinks),
                   pack(v_b_mod, v_g_pre_mix, v_g_post_mix, v_g_pre_mlp, v_g_post_mlp, v_b_forget, v_swa_sinks)]
    ready = h[:8, :LANES].astype(F32) + cos[:8]
    w_slab_t = gathered("w_in", [ready] + in_state[1:] + small_state).reshape(slab_w, D)
    tm_p, tn_p = _fit(MM_TM, S), _fit(MM_TN if slab_w % MM_TN == 0 else MM_TN // 2, slab_w)
    win0 = gate_first // LANES * LANES
    win_j, win_off = divmod(win0, tn_p)
    assert win_off + 2 * LANES <= tn_p and gate_last - win0 <= 2 * LANES

    def proj_epilogue(acc, ex, outs):
        outs[0][...] = acc.astype(BF16)

        @pl.when(pl.program_id(1) == win_j)
        def _():
            outs[1][...] = acc[:, win_off:win_off + 2 * LANES]

    proj_slab, gate_win = _matmul(
        "in_proj", h, w_slab_t, "nt",
        [((S, slab_w), BF16, (tm_p, tn_p), lambda i, j: (i, j)), ((S, 2 * LANES), F32, (tm_p, 2 * LANES), lambda i, j: (i, 0))],
        proj_epilogue, tn=tn_p, revisits=True)
    proj = jnp.concatenate([proj_slab[:, lo:hi] for lo, hi in main_spans], axis=1)
    out_flight = arrived("w_out", proj_slab)
    fg = _tie(jnp.pad(gate_win[:, gate_first - win0:gate_last - win0], ((0, 0), (0, LANES - n_fox))), out_flight[4])
    b_pad = jnp.pad(b_forget, ((0, 0), (0, LANES - n_fox)))
    cum_row = _fox_gate_fwd(fg, b_pad)[:n_fox].reshape(n_fox, 1, S)
    fox_o, fox_lse = _fox_fwd(proj, cum_row, n_fox)

    rq = _rope("rope_fwd", proj, 3 * n_fox, n_swa + n_kv, cos, sin_signed)
    v_first = 3 * n_fox + n_swa + n_kv
    sinks = swa_sinks[0]
    swa_o, swa_lse = _swa_fwd(rq, proj, v_first, sinks, n_swa, n_kv)

    mixcat = jnp.concatenate([fox_o, swa_o], axis=1).astype(BF16)
    up_flight = arrived("w_up", mixcat)
    w_out_f = gathered("w_out", mixcat, out_flight).reshape(D, D)
    mix = _mm_plain("out_proj", mixcat, w_out_f, "nn", BF16, after=up_flight[4])
    x1, h2 = _post_mix(x2, mix, g_post_mix, gt_a, g_pre_mlp, sc_m, sh_m)
    w_up_f = gathered("w_up", h2, up_flight)

    tm_u, tn_u = _fit(MM_TM, S), _fit(MM_TN, d_ff)

    def up_epilogue(acc, ex, outs):
        outs[0][...] = acc.astype(BF16)
        r = jnp.maximum(acc, 0.0)
        outs[1][...] = (r * r).astype(BF16)

    ublk = ((S, d_ff), BF16, (tm_u, tn_u), lambda i, j: (i, j))
    u, a = _matmul("mlp_up", h2, w_up_f, "nn", [ublk, ublk], up_epilogue)
    w_down_f = gathered("w_down", a).reshape(d_ff, D)
    y = _mm_plain("mlp_down", a, w_down_f, "nn", BF16)

    dy, dout, loss_part, acc_mlp_post = _loss_and_post_mlp_bwd(x1, y, tgt, g_post_mlp, gt_m)

    def du_epilogue(acc, ex, outs):
        outs[0][...] = (acc * (2.0 * jnp.maximum(ex[0][...].astype(F32), 0.0))).astype(BF16)

    du = _matmul("mlp_down_bwd", dy, w_down_f, "nt", [ublk], du_epilogue,
                 extras=[(u, (tm_u, tn_u), lambda i, j: (i, j))])[0]
    def pair_send(tag, part):
        return _ici_start("grad_pair_start_" + tag, [part], [jax.ShapeDtypeStruct(part.shape, BF16)], _share_plan,
                          per_source=1)

    def pair_recv(tag, flight, after):
        send, recv, srcs, lands, _ = flight
        return _ici_wait("grad_pair_wait_" + tag, send, recv, srcs, lands, _share_plan, after)[1][0]

    def scatter_start(tag, sums, after=None):
        return _ici_start("grad_scatter_start_" + tag, sums,
                          [jax.ShapeDtypeStruct((3,) + p.shape[1:], BF16) for p in sums], _scatter_plan, after=after)

    def scatter_finish(tag, flight, after):
        send, recv, srcs, lands, _ = flight
        sums, received = _ici_wait("grad_scatter_wait_" + tag, send, recv, srcs, lands, _scatter_plan, after)
        return [_chip_add("chip_add_%s_%d" % (tag, k), chip_arr, p, r) for k, (p, r) in enumerate(zip(sums, received))]

    tm_g = _fit(MM_TM, D // 2)
    pair_down = pair_send("down", _grad_half("grad_w_down_a", core_arr, a, dy, N_CHIPS, 1, tm_g, True))
    pair_up = pair_send("up", _grad_half("grad_w_up_a", core_arr, h2, du, 1, N_CHIPS, tm_g, True, after=pair_down[4]))
    sum_down = _grad_half("grad_w_down_b", core_arr, a, dy, N_CHIPS, 1, tm_g, False,
                          recv=pair_recv("down", pair_down, pair_up[4]))
    sum_up = _grad_half("grad_w_up_b", core_arr, h2, du, 1, N_CHIPS, tm_g, False, recv=pair_recv("up", pair_up, sum_down))
    flight_mlp = scatter_start("mlp", [sum_up, sum_down])
    dh2 = _mm_plain("mlp_up_bwd", du, w_up_f, "nt", BF16, after=flight_mlp[4])
    dx1, dmix, acc_mid = _pre_mlp_and_post_mix_bwd(dh2, x1, dout, mix, _tie(g_pre_mlp, flight_mlp[4]), sc_m,
                                                   g_post_mix, gt_a)

    dmixcat = _mm_plain("out_proj_bwd", dmix, w_out_f, "nt", F32)

    fdq, fdk, fdv, dcum_row, dcum_q = _fox_bwd(proj, fox_o, dmixcat, fox_lse, cum_row, n_fox)
    dcum_k = jnp.pad(dcum_row.reshape(n_fox, S), ((0, LANES - n_fox), (0, 0)))
    dfg, db_forget = _fox_gate_bwd(dcum_k, dcum_q, fg, b_pad)

    group_w = (n_swa // n_kv) * HEAD_DIM
    sdq, sdk, sdv, dsink = _swa_bwd(rq, proj, v_first, sinks, swa_o, dmixcat, fox_w // group_w, swa_lse, n_swa, n_kv)
    drq = jnp.concatenate([sdq, jnp.transpose(sdk, (1, 0, 2)).reshape(S, kv_w).astype(BF16)], axis=1)
    d_sq_sk = _rope("rope_bwd", drq, 0, n_swa + n_kv, cos, -sin_signed)
    dsv = jnp.transpose(sdv, (1, 0, 2)).reshape(S, kv_w).astype(BF16)
    dproj = jnp.concatenate([fdq, fdk, fdv, d_sq_sk, dsv], axis=1)

    pieces = []
    for s in range(N_CHIPS):
        lo, hi = s * in_rows, (s + 1) * in_rows
        for src, first, last, shift in [(dproj, 0, gate_lo, 0), (dfg, gate_lo, gate_lo + n_fox, gate_lo),
                                        (dproj, gate_lo + n_fox, in_w, n_fox)]:
            if max(lo, first) < min(hi, last):
                pieces.append(src[:, max(lo, first) - shift:min(hi, last) - shift])
        pieces.append(jnp.zeros((S, in_rows_pad - in_rows), BF16))
    dproj_slab = jnp.concatenate(pieces, axis=1)

    tm_in, tm_out = in_rows_pad // 2, D // (2 * N_CHIPS)
    pair_in = pair_send("in", _grad_half("grad_w_in_a", core_arr, dproj_slab, h, N_CHIPS, 1, tm_in, True))
    pair_out = pair_send("out", _grad_half("grad_w_out_a", core_arr, mixcat, dmix, N_CHIPS, 1, tm_out, True,
                                           after=pair_in[4]))
    sum_in = _grad_half("grad_w_in_b", core_arr, dproj_slab, h, N_CHIPS, 1, tm_in, False,
                        recv=pair_recv("in", pair_in, pair_out[4]))
    sum_out = _grad_half("grad_w_out_b", core_arr, mixcat, dmix, N_CHIPS, 1, tm_out, False,
                         recv=pair_recv("out", pair_out, sum_in[0, :8, :LANES]))
    dh = _mm_plain("in_proj_bwd", dproj_slab, w_slab_t, "nn", BF16, tk=slab_w // 2,
                   after=sum_out[0, :8, :LANES].astype(F32))
    grad_x, acc_pre = _pre_mix_bwd(dh, x2, dx1, g_pre_mix, sc_a)

    zero_row = jnp.zeros((1, D), F32)
    tail = jnp.concatenate([db_forget[0:1, :n_fox], dsink[:, 0, :n_swa // n_kv].reshape(1, n_swa),
                            loss_part[0:1, 0:1], jnp.zeros((1, D - n_fox - n_swa - 1), F32)], axis=1)
    partial = jnp.concatenate([
        acc_pre[0:1], acc_pre[1:2], acc_mid[3:4], acc_mid[0:1], acc_mid[1:2], acc_mlp_post[0:1],
        acc_pre[2:3], acc_mid[4:5], acc_mid[2:3], acc_mlp_post[1:2], tail] + [zero_row] * 5, axis=0)
    gathered_small, token = _allgather8("gather_small_grads", partial)

    flight_mix = scatter_start("mix", [sum_in, sum_out], after=token)
    halves_mlp = scatter_finish("mlp", flight_mlp, flight_mix[4])
    share_up, share_down = [
        _ici_start("grad_share_start_" + n, [hv], [jax.ShapeDtypeStruct(hv.shape, F32)], _share_plan, per_source=1)
        for n, hv in zip(["up", "down"], halves_mlp)]

    def shared(tag, flight, after):
        send, recv, own, lands, _ = flight
        own, other = _ici_wait("grad_share_wait_" + tag, send, recv, own, lands, _share_plan, after)
        return own[0], other[0]

    def unpack(p):
        return {"b_mod": p[0:N_MOD].reshape(1, N_MOD * D), "g_pre_mix": p[6:7], "g_post_mix": p[7:8],
                "g_pre_mlp": p[8:9], "g_post_mlp": p[9:10], "b_forget": p[10:11, :n_fox],
                "swa_sinks": p[10:11, n_fox:n_fox + n_swa]}

    small_out = _small_update(gathered_small, _tie(small_state[0], share_down[4] + share_up[4]), small_state[1],
                              small_state[2])
    g_small, d_small, m_small, v_small = [unpack(p) for p in small_out]
    loss = small_out[0][N_MOD + 4, n_fox + n_swa]

    dmod_all = gathered_small.reshape(N_DEV, 16, D)[:, :N_MOD].reshape(N_DEV, N_MOD * D)
    dmod_shard = _tie(lax.dynamic_slice_in_dim(dmod_all, chip * mod_cols, mod_cols, axis=1), share_down[4])
    g_w_mod, d_w_mod, nm_w_mod, nv_w_mod = _mod_update(c_all.T, dmod_shard, w_mod[0], m_w_mod[0], v_w_mod[0])

    grads = dict(g_small, w_mod=g_w_mod[None])
    deltas = dict(d_small, w_mod=d_w_mod[None])
    new_m = dict(m_small, w_mod=nm_w_mod[None])
    new_v = dict(v_small, w_mod=nv_w_mod[None])
    weights = {"w_in": (w_in, m_w_in, v_w_in), "w_out": (w_out, m_w_out, v_w_out), "w_up": (w_up, m_w_up, v_w_up),
               "w_down": (w_down, m_w_down, v_w_down)}

    def big_update(n, own, other):
        transposed = n == "w_in"
        w, m, v = in_state if transposed else [a[0] for a in weights[n]]
        outs = _adam_halves("adam_" + n, core_arr, w, own, other, m, v)
        if transposed:
            outs = [o[:in_rows].T for o in outs]
        grads[n], deltas[n], new_m[n], new_v[n] = [o[None] for o in outs]

    big_update("w_down", *shared("down", share_down, d_w_mod[:8, :LANES] + small_out[1][:8, :LANES]))
    halves_mix = scatter_finish("mix", flight_mix, deltas["w_down"][0, :8, :LANES] + d_w_mod[:8, :LANES])
    others_mix = _pair_share("grad_pair_share_mix", halves_mix)
    big_update("w_in", halves_mix[0], others_mix[0])
    big_update("w_out", halves_mix[1], others_mix[1])
    big_update("w_up", *shared("up", share_up, deltas["w_out"][0, :8, :LANES] + deltas["w_in"][0, :8, :LANES]))

    order = ["w_mod", "b_mod", "g_pre_mix", "g_post_mix", "w_in", "b_forget", "swa_sinks", "w_out", "g_pre_mlp",
             "g_post_mlp", "w_up", "w_down"]
    return (loss, grad_x[None], *[grads[n] for n in order], *[deltas[n] for n in order],
            *[new_m[n] for n in order], *[new_v[n] for n in order])
```

```python
import jax
import jax.numpy as jnp
from jax import lax
from jax.experimental import pallas as pl
from jax.experimental.pallas import tpu as pltpu

F32 = jnp.float32
BF16 = jnp.bfloat16
MESH = pl.DeviceIdType.MESH

HEAD_DIM = 128
SWA_BLOCK = 128
ROPE_THETA = 10000.0
NORM_EPS = 1e-6
NEG = -1e30
N_MOD = 6
ADAM_LR = 0.001
ADAM_B1 = 0.9
ADAM_B2 = 0.999
ADAM_EPS = 1e-08
ADAM_WD = 0.01
ADAM_STEP = 10
N_CHIPS = 4
N_DEV = 8
LANES = 128
VMEM_CAP = 60 * 1024 * 1024

_NN = (((1,), (0,)), ((), ()))
_NT = (((1,), (1,)), ((), ()))
_TN = (((0,), (0,)), ((), ()))


def _vmem(nbytes):
    return int(min(VMEM_CAP, nbytes * 5 // 4 + (4 << 20)))


def _nbytes(shape, dtype):
    n = 1
    for s in shape:
        n *= s
    return n * jnp.dtype(dtype).itemsize


def _fit(t, n):
    t = min(t, n)
    assert n % t == 0, (t, n)
    return t


MM_TM, MM_TN, MM_TK = 1024, 1024, 2048


def _matmul(name, a, b, mode, out_defs, epilogue, extras=(), tm=MM_TM, tn=MM_TN, tk=MM_TK, revisits=False,
            row_sel=None):
    stacked = b.ndim == 3
    b_rows, b_cols = b.shape[-2], b.shape[-1] * (b.shape[0] if stacked else 1)
    if mode == "nn":
        (M, K), (K2, N) = a.shape, (b_rows, b_cols)
    elif mode == "nt":
        (M, K), (N, K2) = a.shape, (b_rows, b_cols)
    else:
        (K, M), (K2, N) = a.shape, (b_rows, b_cols)
    assert K == K2 and not (stacked and mode == "tn"), (a.shape, b.shape, mode)
    tm = _fit(tm, M)
    tn = _fit(tn, b.shape[-1] if stacked and mode == "nn" else N)
    tk = _fit(tk, b.shape[-1] if stacked and mode == "nt" else K)
    nk = K // tk
    dims = {"nn": _NN, "nt": _NT, "tn": _TN}[mode]
    if row_sel is None:
        grid_m, a_row = M // tm, lambda i, *sel: i
    else:
        grid_m, a_row = row_sel[2], lambda i, *sel: row_sel[1](i, sel[0])
    a_spec = (pl.BlockSpec((tk, tm), lambda i, j, k, *sel: (k, a_row(i, *sel))) if mode == "tn"
              else pl.BlockSpec((tm, tk), lambda i, j, k, *sel: (a_row(i, *sel), k)))
    if stacked:
        per = b.shape[-1] // (tk if mode == "nt" else tn)
        b_spec = (pl.BlockSpec((1, tn, tk), lambda i, j, k, *sel: (k // per, j, k % per)) if mode == "nt"
                  else pl.BlockSpec((1, tk, tn), lambda i, j, k, *sel: (j // per, k, j % per)))
    else:
        b_spec = (pl.BlockSpec((tn, tk), lambda i, j, k, *sel: (j, k)) if mode == "nt"
                  else pl.BlockSpec((tk, tn), lambda i, j, k, *sel: (k, j)))
    n_ex, n_out = len(extras), len(out_defs)

    def body(*refs):
        if row_sel is not None:
            refs = refs[1:]
        a_ref, b_ref = refs[0], refs[1]
        ex = refs[2:2 + n_ex]
        outs = refs[2 + n_ex:2 + n_ex + n_out]
        b_blk = b_ref[0] if stacked else b_ref[...]
        prod = lax.dot_general(a_ref[...], b_blk, dims, preferred_element_type=F32)
        if nk == 1:
            epilogue(prod, ex, outs)
        else:
            acc_ref = refs[-1]
            k = pl.program_id(2)

            @pl.when(k == 0)
            def _():
                acc_ref[...] = prod

            @pl.when(k > 0)
            def _():
                acc_ref[...] += prod

            @pl.when(k == nk - 1)
            def _():
                epilogue(acc_ref[...], ex, outs)

    def wrap(f):
        return lambda i, j, k, *sel: f(i, j)

    in_specs = [a_spec, b_spec] + [pl.BlockSpec(blk, wrap(f)) for _, blk, f in extras]
    out_specs = [pl.BlockSpec(blk, wrap(f)) for _, _, blk, f in out_defs]
    out_shape = [jax.ShapeDtypeStruct(s, d) for s, d, _, _ in out_defs]
    need = 2 * (tm * tk + tk * tn) * a.dtype.itemsize + 3 * tm * tn * 4
    need += sum(2 * _nbytes(blk, arr.dtype) for arr, blk, _ in extras)
    need += sum(2 * _nbytes(blk, d) for _, d, blk, _ in out_defs)
    grid = (grid_m, N // tn, nk)
    scratch = [pltpu.VMEM((tm, tn), F32)] if nk > 1 else []
    params = pltpu.CompilerParams(
        dimension_semantics=("parallel", "arbitrary" if revisits else "parallel", "arbitrary"),
        vmem_limit_bytes=_vmem(need))
    operands = (a, b, *[arr for arr, _, _ in extras])
    if row_sel is None:
        return pl.pallas_call(body, name=name, grid=grid, in_specs=in_specs, out_specs=out_specs, out_shape=out_shape,
                              scratch_shapes=scratch, compiler_params=params)(*operands)
    grid_spec = pltpu.PrefetchScalarGridSpec(num_scalar_prefetch=1, grid=grid, in_specs=in_specs, out_specs=out_specs,
                                             scratch_shapes=scratch)
    return pl.pallas_call(body, name=name, grid_spec=grid_spec, out_shape=out_shape,
                          compiler_params=params)(row_sel[0], *operands)


def _grad_half(name, core, a, b, row_slabs, col_slabs, tm, other, recv=None, after=None):
    (_, M), (_, N) = a.shape, b.shape
    H = M // (2 * row_slabs)
    nh = H // tm
    tn = _fit(MM_TN, N // col_slabs)
    per = N // col_slabs // tn

    def a_block(i, core_ref):
        half = (1 - core_ref[0]) if other else core_ref[0]
        return (i // nh) * (2 * nh) + half * nh + i % nh

    def out_index(i, j):
        return (j // per, i, j % per) if col_slabs > 1 else (i // nh, i % nh, j)

    slabs = max(row_slabs, col_slabs)
    out_def = ((slabs, H, N // col_slabs), BF16, (1, tm, tn), out_index)

    def epilogue(acc, ex, outs):
        outs[0][0] = (acc if recv is None else acc + ex[0][0].astype(F32)).astype(BF16)

    extras = ([] if recv is None else [(recv, (1, tm, tn), out_index)]) + ([] if after is None else [_behind(after)])
    return _matmul(name, a, b, "tn", [out_def], epilogue, extras=extras, tm=tm, tn=tn,
                   row_sel=(core, a_block, row_slabs * nh))[0]


def _behind(token):
    return (token, (8, LANES), lambda i, j: (0, 0))


def _mm_plain(name, a, b, mode, out_dtype, after=None, **tiles):
    if mode == "nn":
        M, N = a.shape[0], b.shape[-1] * (b.shape[0] if b.ndim == 3 else 1)
    elif mode == "nt":
        M, N = a.shape[0], b.shape[-2]
    else:
        M, N = a.shape[1], b.shape[1]
    tm, tn = _fit(tiles.get("tm", MM_TM), M), _fit(tiles.get("tn", MM_TN), N)

    def epi(acc, ex, outs):
        outs[0][...] = acc.astype(out_dtype)

    return _matmul(name, a, b, mode, [((M, N), out_dtype, (tm, tn), lambda i, j: (i, j))], epi,
                   extras=[] if after is None else [_behind(after)], **tiles)[0]


def _rstd(v):
    return lax.rsqrt(jnp.mean(v * v, axis=-1, keepdims=True) + NORM_EPS)


ROW_TILE = 256


def _row_call(name, body, row_ins, vec_ins, row_outs, acc_outs, S, D):
    tr = _fit(ROW_TILE, S)
    row_spec = pl.BlockSpec((tr, D), lambda r: (r, 0))
    vec_spec = pl.BlockSpec((1, D), lambda r: (0, 0))
    in_specs = [row_spec] * len(row_ins) + [vec_spec] * len(vec_ins)
    out_specs = [row_spec] * len(row_outs) + [pl.BlockSpec(shp, lambda r: (0, 0)) for shp in acc_outs]
    out_shape = [jax.ShapeDtypeStruct((S, D), d) for d in row_outs] + [jax.ShapeDtypeStruct(shp, F32) for shp in acc_outs]
    need = sum(2 * tr * D * a.dtype.itemsize for a in row_ins) + sum(2 * tr * D * jnp.dtype(d).itemsize for d in row_outs)
    need += 8 * tr * D * 4
    return pl.pallas_call(
        body, name=name, grid=(S // tr,), in_specs=in_specs, out_specs=out_specs, out_shape=out_shape,
        compiler_params=pltpu.CompilerParams(dimension_semantics=("arbitrary",), vmem_limit_bytes=_vmem(need)),
    )(*row_ins, *vec_ins)


def _acc_rows(ref, rows):
    @pl.when(pl.program_id(0) == 0)
    def _():
        ref[...] = jnp.zeros_like(ref)
    for n, r in enumerate(rows):
        ref[n:n + 1, :] += r


def _pre_norm(x, g, sc, sh):
    S, D = x.shape

    def body(x_ref, g_ref, sc_ref, sh_ref, h_ref):
        xv = x_ref[...]
        xn = xv * _rstd(xv)
        h_ref[...] = (xn * g_ref[...] * (1.0 + sc_ref[...]) + sh_ref[...]).astype(BF16)

    return _row_call("pre_norm_mix", body, [x], [g, sc, sh], [BF16], [], S, D)[0]


def _post_mix(x, mix, g_post, gt, g_pre, sc, sh):
    S, D = x.shape

    def body(x_ref, mix_ref, gp_ref, gt_ref, g2_ref, sc_ref, sh_ref, x1_ref, h2_ref):
        mv = mix_ref[...].astype(F32)
        x1 = x_ref[...] + gt_ref[...] * (mv * _rstd(mv) * gp_ref[...])
        x1_ref[...] = x1
        h2_ref[...] = (x1 * _rstd(x1) * g2_ref[...] * (1.0 + sc_ref[...]) + sh_ref[...]).astype(BF16)

    return _row_call("post_mix_pre_mlp", body, [x, mix], [g_post, gt, g_pre, sc, sh], [F32, BF16], [], S, D)


def _loss_and_post_mlp_bwd(x1, y, target, g_post, gt):
    S, D = x1.shape

    def body(x1_ref, y_ref, t_ref, g_ref, gt_ref, dy_ref, dout_ref, loss_ref, acc_ref):
        yv = y_ref[...].astype(F32)
        r = _rstd(yv)
        yh = yv * r
        n = yh * g_ref[...]
        diff = x1_ref[...] + gt_ref[...] * n - t_ref[...]
        dout = diff * (1.0 / D)
        dout_ref[...] = dout
        dn = dout * gt_ref[...]
        dyh = dn * g_ref[...]
        dy_ref[...] = (r * (dyh - yh * jnp.mean(dyh * yh, axis=-1, keepdims=True))).astype(BF16)
        _acc_rows(acc_ref, [jnp.sum(dout * n, axis=0, keepdims=True), jnp.sum(dn * yh, axis=0, keepdims=True)])

        @pl.when(pl.program_id(0) == 0)
        def _():
            loss_ref[...] = jnp.zeros_like(loss_ref)
        loss_ref[...] += jnp.full(loss_ref.shape, (0.5 / D) * jnp.sum(diff * diff), F32)

    return _row_call("loss_post_mlp_bwd", body, [x1, y, target], [g_post, gt], [BF16, F32],
                     [(8, LANES), (8, D)], S, D)


def _pre_mlp_and_post_mix_bwd(dh2, x1, dout, mix, g_pre, sc, g_post, gt):
    S, D = x1.shape

    def body(dh_ref, x1_ref, dout_ref, mix_ref, g_ref, sc_ref, gp_ref, gt_ref, dx1_ref, dmix_ref, acc_ref):
        dh = dh_ref[...].astype(F32)
        x1v = x1_ref[...]
        r3 = _rstd(x1v)
        xn = x1v * r3
        dxn = dh * (1.0 + sc_ref[...]) * g_ref[...]
        dx1 = dout_ref[...] + r3 * (dxn - xn * jnp.mean(dxn * xn, axis=-1, keepdims=True))
        dx1_ref[...] = dx1
        mv = mix_ref[...].astype(F32)
        r2 = _rstd(mv)
        mh = mv * r2
        dn = dx1 * gt_ref[...]
        dmh = dn * gp_ref[...]
        dmix_ref[...] = (r2 * (dmh - mh * jnp.mean(dmh * mh, axis=-1, keepdims=True))).astype(BF16)
        _acc_rows(acc_ref, [
            jnp.sum(dh, axis=0, keepdims=True),
            jnp.sum(dh * xn * g_ref[...], axis=0, keepdims=True),
            jnp.sum(dh * (1.0 + sc_ref[...]) * xn, axis=0, keepdims=True),
            jnp.sum(dx1 * mh * gp_ref[...], axis=0, keepdims=True),
            jnp.sum(dn * mh, axis=0, keepdims=True)])

    return _row_call("pre_mlp_post_mix_bwd", body, [dh2, x1, dout, mix], [g_pre, sc, g_post, gt], [F32, BF16],
                     [(8, D)], S, D)


def _pre_mix_bwd(dh, x, dx1, g_pre, sc):
    S, D = x.shape

    def body(dh_ref, x_ref, dx1_ref, g_ref, sc_ref, gx_ref, acc_ref):
        dhv = dh_ref[...].astype(F32)
        xv = x_ref[...]
        r = _rstd(xv)
        xn = xv * r
        dxn = dhv * (1.0 + sc_ref[...]) * g_ref[...]
        gx_ref[...] = dx1_ref[...] + r * (dxn - xn * jnp.mean(dxn * xn, axis=-1, keepdims=True))
        _acc_rows(acc_ref, [
            jnp.sum(dhv, axis=0, keepdims=True),
            jnp.sum(dhv * xn * g_ref[...], axis=0, keepdims=True),
            jnp.sum(dhv * (1.0 + sc_ref[...]) * xn, axis=0, keepdims=True)])

    return _row_call("pre_mix_bwd", body, [dh, x, dx1], [g_pre, sc], [F32], [(8, D)], S, D)


CUM_BLOCK = 256


def _tri(n, upper):
    r = lax.broadcasted_iota(jnp.int32, (n, n), 0)
    c = lax.broadcasted_iota(jnp.int32, (n, n), 1)
    return ((c >= r) if upper else (c <= r)).astype(F32)


def _fox_gate_fwd(fg, b_pad):
    S = fg.shape[0]
    cb = _fit(CUM_BLOCK, S)

    def body(fg_ref, b_ref, cumt_ref, cum_ref):
        low = _tri(cb, False)
        carry = jnp.zeros((1, LANES), F32)
        for n in range(S // cb):
            z = fg_ref[n * cb:(n + 1) * cb, :] + b_ref[...]
            logf = jnp.minimum(z, 0.0) - jnp.log(1.0 + jnp.exp(-jnp.abs(z)))
            blk = jnp.dot(low, logf, precision=lax.Precision.HIGHEST, preferred_element_type=F32) + carry
            cum_ref[n * cb:(n + 1) * cb, :] = blk
            carry = blk[cb - 1:cb, :]
        cumt_ref[...] = cum_ref[...].T

    return pl.pallas_call(
        body, name="fox_gate_fwd", out_shape=jax.ShapeDtypeStruct((LANES, S), F32),
        scratch_shapes=[pltpu.VMEM((S, LANES), F32)],
        compiler_params=pltpu.CompilerParams(vmem_limit_bytes=_vmem(6 * S * LANES * 4)),
    )(fg, b_pad)


def _fox_gate_bwd(dcum_k, dcum_q, fg, b_pad):
    S = fg.shape[0]
    n_fox = dcum_q.shape[0]
    cb = _fit(CUM_BLOCK, S)

    def body(dk_ref, dq_ref, fg_ref, b_ref, dfg_ref, db_ref, dc_ref):
        lane = lax.broadcasted_iota(jnp.int32, (S, LANES), 1)
        dc = dk_ref[...].T
        for h in range(n_fox):
            dc = dc + jnp.where(lane == h, dq_ref[h], 0.0)
        dc_ref[...] = dc
        up = _tri(cb, True)
        carry = jnp.zeros((1, LANES), F32)
        db = jnp.zeros((1, LANES), F32)
        for n in reversed(range(S // cb)):
            blk = jnp.dot(up, dc_ref[n * cb:(n + 1) * cb, :], precision=lax.Precision.HIGHEST,
                          preferred_element_type=F32) + carry
            carry = blk[0:1, :]
            z = fg_ref[n * cb:(n + 1) * cb, :] + b_ref[...]
            dfg = blk * (1.0 / (1.0 + jnp.exp(z)))
            dfg_ref[n * cb:(n + 1) * cb, :] = dfg.astype(BF16)
            db = db + jnp.sum(dfg, axis=0, keepdims=True)
        db_ref[...] = jnp.broadcast_to(db, db_ref.shape)

    return pl.pallas_call(
        body, name="fox_gate_bwd",
        out_shape=[jax.ShapeDtypeStruct((S, LANES), BF16), jax.ShapeDtypeStruct((8, LANES), F32)],
        scratch_shapes=[pltpu.VMEM((S, LANES), F32)],
        compiler_params=pltpu.CompilerParams(vmem_limit_bytes=_vmem((8 + 2 * n_fox) * S * LANES * 4)),
    )(dcum_k, dcum_q, fg, b_pad)


FOX_TILE = 512


LOG2E = 1.4426950408889634


def _fox_scores(q, k, ck2, masked, t):
    s = lax.dot_general(q, k, _NT, preferred_element_type=F32) * (HEAD_DIM ** -0.5 * LOG2E) - ck2
    if masked:
        row = lax.broadcasted_iota(jnp.int32, (t, t), 0)
        col = lax.broadcasted_iota(jnp.int32, (t, t), 1)
        s = jnp.where(col <= row, s, NEG)
    return s


def _fox_fwd(proj, cum_row, n_fox):
    S = proj.shape[0]
    t = _fit(FOX_TILE, S)
    nq = S // t

    def body(q_ref, k_ref, v_ref, ck_ref, o_ref, lse_ref):
        def q_block(qi, _):
            q0 = pl.multiple_of(qi * t, t)
            q = q_ref[pl.ds(q0, t), :]

            def kv_block(j, carry, masked):
                m, l, acc = carry
                k0 = pl.multiple_of(j * t, t)
                s = _fox_scores(q, k_ref[pl.ds(k0, t), :], ck_ref[0, :, pl.ds(k0, t)] * LOG2E, masked, t)
                m_new = jnp.maximum(m, jnp.max(s, axis=-1, keepdims=True))
                alpha = jnp.exp2(m - m_new)
                p = jnp.exp2(s - m_new)
                l = alpha * l + jnp.sum(p, axis=-1, keepdims=True)
                acc = alpha * acc + jnp.dot(p.astype(BF16), v_ref[pl.ds(k0, t), :], preferred_element_type=F32)
                return m_new, l, acc

            init = (jnp.full((t, 1), NEG, F32), jnp.zeros((t, 1), F32), jnp.zeros((t, HEAD_DIM), F32))
            carry = lax.fori_loop(0, qi, lambda j, cr: kv_block(j, cr, False), init)
            m, l, acc = kv_block(qi, carry, True)
            o_ref[pl.ds(q0, t), :] = acc / l
            lse_ref[0, pl.ds(q0, t), :] = jnp.broadcast_to(m + jnp.log(l) * LOG2E, (t, LANES))
            return 0

        lax.fori_loop(0, nq, q_block, 0)

    col = lambda off: pl.BlockSpec((S, HEAD_DIM), lambda h: (0, off + h))
    per_head = pl.BlockSpec((1, S, LANES), lambda h: (h, 0, 0))
    return pl.pallas_call(
        body, name="fox_fwd", grid=(n_fox,),
        in_specs=[col(0), col(n_fox), col(2 * n_fox), pl.BlockSpec((1, 1, S), lambda h: (h, 0, 0))],
        out_specs=[pl.BlockSpec((S, HEAD_DIM), lambda h: (0, h)), per_head],
        out_shape=[jax.ShapeDtypeStruct((S, n_fox * HEAD_DIM), F32), jax.ShapeDtypeStruct((n_fox, S, LANES), F32)],
        compiler_params=pltpu.CompilerParams(dimension_semantics=("parallel",),
                                             vmem_limit_bytes=_vmem(16 * S * HEAD_DIM * 4 + 12 * t * t * 4)),
    )(proj, proj, proj, cum_row)


def _fox_bwd(proj, o, do, lse_b, cum_row, n_fox):
    S = proj.shape[0]
    t = _fit(FOX_TILE, S)
    nq = S // t
    scale = HEAD_DIM ** -0.5

    def body(q_ref, k_ref, v_ref, o_ref, do_ref, lse_ref, ck_ref, dq_ref, dk_ref, dv_ref, dc_ref, dcq_ref,
             dq_acc, delta_ref):
        dq_acc[...] = jnp.zeros_like(dq_acc)
        dcq_ref[...] = jnp.zeros_like(dcq_ref)

        def delta_block(qi, _):
            q0 = pl.multiple_of(qi * t, t)
            d = jnp.sum(do_ref[pl.ds(q0, t), :] * o_ref[pl.ds(q0, t), :], axis=-1, keepdims=True)
            delta_ref[pl.ds(q0, t), :] = jnp.broadcast_to(d, (t, LANES))
            return 0

        lax.fori_loop(0, nq, delta_block, 0)

        def kv_block(j, _):
            k0 = pl.multiple_of(j * t, t)
            k = k_ref[pl.ds(k0, t), :]
            v = v_ref[pl.ds(k0, t), :]
            ck2 = ck_ref[0, :, pl.ds(k0, t)] * LOG2E

            def q_block(qi, carry, masked):
                dk, dv, dc = carry
                q0 = pl.multiple_of(qi * t, t)
                q = q_ref[pl.ds(q0, t), :]
                dov = do_ref[pl.ds(q0, t), :].astype(BF16)
                p = jnp.exp2(_fox_scores(q, k, ck2, masked, t) - lse_ref[0, pl.ds(q0, t), :][:, :1])
                dp = lax.dot_general(dov, v, _NT, preferred_element_type=F32)
                ds = p * (dp - delta_ref[pl.ds(q0, t), :][:, :1])
                dsb = ds.astype(BF16)
                dv = dv + lax.dot_general(p.astype(BF16), dov, _TN, preferred_element_type=F32)
                dk = dk + lax.dot_general(dsb, q, _TN, preferred_element_type=F32)
                dq_acc[pl.ds(q0, t), :] += jnp.dot(dsb, k, preferred_element_type=F32)
                dc = dc - jnp.sum(ds, axis=0, keepdims=True)
                dcq_ref[0, pl.ds(q0, t), :] += jnp.broadcast_to(jnp.sum(ds, axis=1, keepdims=True), (t, LANES))
                return dk, dv, dc

            init = (jnp.zeros((t, HEAD_DIM), F32), jnp.zeros((t, HEAD_DIM), F32), jnp.zeros((1, t), F32))
            carry = q_block(j, init, True)
            dk, dv, dc = lax.fori_loop(j + 1, nq, lambda qi, cr: q_block(qi, cr, False), carry)
            dk_ref[pl.ds(k0, t), :] = (dk * scale).astype(BF16)
            dv_ref[pl.ds(k0, t), :] = dv.astype(BF16)
            dc_ref[0, :, pl.ds(k0, t)] = dc
            return 0

        lax.fori_loop(0, nq, kv_block, 0)
        dq_ref[...] = (dq_acc[...] * scale).astype(BF16)

    col = lambda off: pl.BlockSpec((S, HEAD_DIM), lambda h: (0, off + h))
    per_head = pl.BlockSpec((1, S, LANES), lambda h: (h, 0, 0))
    row = pl.BlockSpec((1, 1, S), lambda h: (h, 0, 0))
    grad = jax.ShapeDtypeStruct((S, n_fox * HEAD_DIM), BF16)
    return pl.pallas_call(
        body, name="fox_bwd", grid=(n_fox,),
        in_specs=[col(0), col(n_fox), col(2 * n_fox), col(0), col(0), per_head, row],
        out_specs=[col(0), col(0), col(0), row, per_head],
        out_shape=[grad, grad, grad, jax.ShapeDtypeStruct((n_fox, 1, S), F32), jax.ShapeDtypeStruct((n_fox, S, LANES), F32)],
        scratch_shapes=[pltpu.VMEM((S, HEAD_DIM), F32), pltpu.VMEM((S, LANES), F32)],
        compiler_params=pltpu.CompilerParams(dimension_semantics=("parallel",),
                                             vmem_limit_bytes=_vmem(24 * S * HEAD_DIM * 4 + 16 * t * t * 4)),
    )(proj, proj, proj, o, do, lse_b, cum_row)


def _rope_tables(S):
    half = HEAD_DIM // 2
    inv_freq = 1.0 / (ROPE_THETA ** (jnp.arange(half, dtype=F32) * (2.0 / HEAD_DIM)))
    ang = jnp.arange(S).astype(F32)[:, None] * inv_freq[None, :]
    cos, sin = jnp.cos(ang), jnp.sin(ang)
    return jnp.concatenate([cos, cos], axis=-1), jnp.concatenate([-sin, sin], axis=-1)


def _rope(name, src, first_block, n_blocks, cos, sin_signed):
    S = src.shape[0]

    def body(x_ref, cos_ref, sin_ref, o_ref):
        xv = x_ref[...].astype(F32)
        o_ref[...] = (xv * cos_ref[...] + pltpu.roll(xv, HEAD_DIM // 2, 1) * sin_ref[...]).astype(BF16)

    table = pl.BlockSpec((S, HEAD_DIM), lambda n: (0, 0))
    return pl.pallas_call(
        body, name=name, grid=(n_blocks,),
        in_specs=[pl.BlockSpec((S, HEAD_DIM), lambda n: (0, first_block + n)), table, table],
        out_specs=pl.BlockSpec((S, HEAD_DIM), lambda n: (0, n)),
        out_shape=jax.ShapeDtypeStruct((S, n_blocks * HEAD_DIM), BF16),
        compiler_params=pltpu.CompilerParams(dimension_semantics=("parallel",),
                                             vmem_limit_bytes=_vmem(12 * S * HEAD_DIM * 4)),
    )(src, cos, sin_signed)


def _swa_tile(q_ref, kp_ref, kc_ref, n, group, scale):
    B = SWA_BLOCK
    qs = jnp.concatenate([q_ref[:, g * HEAD_DIM:(g + 1) * HEAD_DIM] for g in range(group)], axis=0)
    kcat = jnp.concatenate([kp_ref[...], kc_ref[...]], axis=0)
    s = lax.dot_general(qs, kcat, _NT, preferred_element_type=F32) * scale
    qi = lax.broadcasted_iota(jnp.int32, (group * B, 2 * B), 0) % B
    kj = lax.broadcasted_iota(jnp.int32, (group * B, 2 * B), 1)
    diff = qi + B - kj
    mask = (diff >= 0) & (diff < B) & ((n * B + kj - B) >= 0)
    return qs, kcat, jnp.where(mask, s, NEG)


def _swa_sink_col(sink_ref, kv, group):
    head = lax.broadcasted_iota(jnp.int32, (group * SWA_BLOCK, 1), 0) // SWA_BLOCK
    col = jnp.zeros((group * SWA_BLOCK, 1), F32)
    for g in range(group):
        col = jnp.where(head == g, sink_ref[kv * group + g], col)
    return col


def _swa_specs(n_kv, group, q_first, k_first, v_first):
    B = SWA_BLOCK
    prev = lambda n: jnp.maximum(n - 1, 0)
    return [
        pl.BlockSpec((B, group * HEAD_DIM), lambda kv, n: (n, q_first + kv)),
        pl.BlockSpec((B, HEAD_DIM), lambda kv, n: (prev(n), k_first + kv)),
        pl.BlockSpec((B, HEAD_DIM), lambda kv, n: (n, k_first + kv)),
        pl.BlockSpec((B, HEAD_DIM), lambda kv, n: (prev(n), v_first + kv)),
        pl.BlockSpec((B, HEAD_DIM), lambda kv, n: (n, v_first + kv)),
    ]


def _swa_fwd(rq, proj, v_first, sinks, n_q, n_kv):
    S = rq.shape[0]
    B = SWA_BLOCK
    group = n_q // n_kv
    scale = HEAD_DIM ** -0.5

    def body(q_ref, kp_ref, kc_ref, vp_ref, vc_ref, sink_ref, o_ref, lse_ref):
        kv, n = pl.program_id(0), pl.program_id(1)
        _, _, s = _swa_tile(q_ref, kp_ref, kc_ref, n, group, scale)
        sink = _swa_sink_col(sink_ref, kv, group)
        m = jnp.maximum(jnp.max(s, axis=-1, keepdims=True), sink)
        p = jnp.exp(s - m)
        denom = jnp.sum(p, axis=-1, keepdims=True) + jnp.exp(sink - m)
        vcat = jnp.concatenate([vp_ref[...], vc_ref[...]], axis=0)
        o = jnp.dot((p / denom).astype(BF16), vcat, preferred_element_type=F32)
        lse = m + jnp.log(denom)
        for g in range(group):
            o_ref[:, g * HEAD_DIM:(g + 1) * HEAD_DIM] = o[g * B:(g + 1) * B, :]
            lse_ref[0, :, g * LANES:(g + 1) * LANES] = jnp.broadcast_to(lse[g * B:(g + 1) * B, :], (B, LANES))

    specs = _swa_specs(n_kv, group, 0, n_q, v_first)
    q_blk = pl.BlockSpec((B, group * HEAD_DIM), lambda kv, n: (n, kv))
    return pl.pallas_call(
        body, name="swa_fwd", grid=(n_kv, S // B),
        in_specs=specs + [pl.BlockSpec(memory_space=pltpu.SMEM)],
        out_specs=[q_blk, pl.BlockSpec((1, B, group * LANES), lambda kv, n: (kv, n, 0))],
        out_shape=[jax.ShapeDtypeStruct((S, n_q * HEAD_DIM), F32), jax.ShapeDtypeStruct((n_kv, S, group * LANES), F32)],
        compiler_params=pltpu.CompilerParams(dimension_semantics=("parallel", "arbitrary")),
    )(rq, rq, rq, proj, proj, sinks)


def _swa_bwd(rq, proj, v_first, sinks, o, do, do_first, lse_b, n_q, n_kv):
    S = rq.shape[0]
    B = SWA_BLOCK
    group = n_q // n_kv
    scale = HEAD_DIM ** -0.5

    def body(q_ref, kp_ref, kc_ref, vp_ref, vc_ref, o_ref, do_ref, lse_ref, sink_ref,
             dq_ref, dk_ref, dv_ref, dsink_ref):
        kv, n = pl.program_id(0), pl.program_id(1)

        @pl.when(n == 0)
        def _():
            dk_ref[...] = jnp.zeros_like(dk_ref)
            dv_ref[...] = jnp.zeros_like(dv_ref)
            dsink_ref[...] = jnp.zeros_like(dsink_ref)

        qs, kcat, s = _swa_tile(q_ref, kp_ref, kc_ref, n, group, scale)
        sink = _swa_sink_col(sink_ref, kv, group)
        stack = lambda ref, w: jnp.concatenate([ref[:, g * w:(g + 1) * w] for g in range(group)], axis=0)
        lse = jnp.concatenate([lse_ref[0, :, g * LANES:g * LANES + 1] for g in range(group)], axis=0)
        do32 = stack(do_ref, HEAD_DIM)
        delta = jnp.sum(do32 * stack(o_ref, HEAD_DIM), axis=-1, keepdims=True)
        dov = do32.astype(BF16)
        p = jnp.exp(s - lse)
        vcat = jnp.concatenate([vp_ref[...], vc_ref[...]], axis=0)
        dp = lax.dot_general(dov, vcat, _NT, preferred_element_type=F32)
        ds = p * (dp - delta)
        dsb = ds.astype(BF16)
        dq = jnp.dot(dsb, kcat, preferred_element_type=F32) * scale
        for g in range(group):
            dq_ref[:, g * HEAD_DIM:(g + 1) * HEAD_DIM] = dq[g * B:(g + 1) * B, :].astype(BF16)
        dkcat = lax.dot_general(dsb, qs, _TN, preferred_element_type=F32) * scale
        dvcat = lax.dot_general(p.astype(BF16), dov, _TN, preferred_element_type=F32)
        prev0 = pl.multiple_of(jnp.maximum(n - 1, 0) * B, B)
        cur0 = pl.multiple_of(n * B, B)
        dk_ref[0, pl.ds(prev0, B), :] += dkcat[:B, :]
        dk_ref[0, pl.ds(cur0, B), :] += dkcat[B:, :]
        dv_ref[0, pl.ds(prev0, B), :] += dvcat[:B, :]
        dv_ref[0, pl.ds(cur0, B), :] += dvcat[B:, :]
        dsk = -jnp.exp(sink - lse) * delta
        lane = lax.broadcasted_iota(jnp.int32, (1, LANES), 1)
        row = jnp.zeros((1, LANES), F32)
        for g in range(group):
            row = row + jnp.where(lane == g, jnp.sum(dsk[g * B:(g + 1) * B, :]), 0.0)
        dsink_ref[0, 0:1, :] += row

    specs = _swa_specs(n_kv, group, 0, n_q, v_first)
    q_blk = pl.BlockSpec((B, group * HEAD_DIM), lambda kv, n: (n, kv))
    acc = pl.BlockSpec((1, S, HEAD_DIM), lambda kv, n: (kv, 0, 0))
    return pl.pallas_call(
        body, name="swa_bwd", grid=(n_kv, S // B),
        in_specs=specs + [q_blk, pl.BlockSpec((B, group * HEAD_DIM), lambda kv, n: (n, do_first + kv)),
                          pl.BlockSpec((1, B, group * LANES), lambda kv, n: (kv, n, 0)),
                          pl.BlockSpec(memory_space=pltpu.SMEM)],
        out_specs=[q_blk, acc, acc, pl.BlockSpec((1, 8, LANES), lambda kv, n: (kv, 0, 0))],
        out_shape=[jax.ShapeDtypeStruct((S, n_q * HEAD_DIM), BF16), jax.ShapeDtypeStruct((n_kv, S, HEAD_DIM), F32),
                   jax.ShapeDtypeStruct((n_kv, S, HEAD_DIM), F32), jax.ShapeDtypeStruct((n_kv, 8, LANES), F32)],
        compiler_params=pltpu.CompilerParams(dimension_semantics=("parallel", "arbitrary")),
    )(rq, rq, rq, proj, proj, o, do, lse_b, sinks)


def _adamw(w, g, m, v):
    m = ADAM_B1 * m + (1.0 - ADAM_B1) * g
    v = ADAM_B2 * v + (1.0 - ADAM_B2) * (g * g)
    m_hat = m / (1.0 - ADAM_B1 ** ADAM_STEP)
    v_hat = v / (1.0 - ADAM_B2 ** ADAM_STEP)
    delta = -ADAM_LR * (m_hat / (jnp.sqrt(v_hat) + ADAM_EPS) + ADAM_WD * w)
    return delta, m, v


def _mod_fwd(cond_in, w_mod, b_shard):
    R, D = cond_in.shape
    cols = w_mod.shape[1]
    tn = _fit(512, cols)

    def body(c_ref, w_ref, b_ref, o_ref):
        cv = c_ref[...]
        cond = (cv / (1.0 + jnp.exp(-cv))).astype(BF16)
        o_ref[...] = jnp.dot(cond, w_ref[...].astype(BF16), preferred_element_type=F32) + b_ref[...]

    return pl.pallas_call(
        body, name="mod_fwd", grid=(cols // tn,),
        in_specs=[pl.BlockSpec((R, D), lambda j: (0, 0)), pl.BlockSpec((D, tn), lambda j: (0, j)),
                  pl.BlockSpec((1, tn), lambda j: (0, j))],
        out_specs=pl.BlockSpec((R, tn), lambda j: (0, j)),
        out_shape=jax.ShapeDtypeStruct((R, cols), F32),
        compiler_params=pltpu.CompilerParams(dimension_semantics=("parallel",), vmem_limit_bytes=_vmem(3 * D * tn * 4)),
    )(cond_in, w_mod, b_shard)


def _mod_update(c_t, dmod, w, m, v):
    D, nb = c_t.shape
    cols = w.shape[1]
    tn = _fit(256, cols)

    def body(c_ref, d_ref, w_ref, m_ref, v_ref, g_ref, dl_ref, nm_ref, nv_ref):
        cv = c_ref[...]
        cond = cv / (1.0 + jnp.exp(-cv))
        g = jnp.zeros((D, tn), F32)
        for b in range(nb):
            g = g + cond[:, b:b + 1] * d_ref[b:b + 1, :]
        g_ref[...] = g
        dl_ref[...], nm_ref[...], nv_ref[...] = _adamw(w_ref[...], g, m_ref[...], v_ref[...])

    blk = pl.BlockSpec((D, tn), lambda j: (0, j))
    out = jax.ShapeDtypeStruct((D, cols), F32)
    return pl.pallas_call(
        body, name="mod_update", grid=(cols // tn,),
        in_specs=[pl.BlockSpec((D, nb), lambda j: (0, 0)), pl.BlockSpec((nb, tn), lambda j: (0, j)), blk, blk, blk],
        out_specs=[blk] * 4, out_shape=[out] * 4,
        compiler_params=pltpu.CompilerParams(dimension_semantics=("parallel",), vmem_limit_bytes=_vmem(18 * D * tn * 4)),
    )(c_t, dmod, w, m, v)


def _small_update(stacked, w, m, v):
    R, C = w.shape

    def body(s_ref, w_ref, m_ref, v_ref, g_ref, dl_ref, nm_ref, nv_ref):
        g = s_ref[0:R, :]
        for d in range(1, N_DEV):
            g = g + s_ref[d * R:(d + 1) * R, :]
        g_ref[...] = g
        dl_ref[...], nm_ref[...], nv_ref[...] = _adamw(w_ref[...], g, m_ref[...], v_ref[...])

    return pl.pallas_call(body, name="small_update", out_shape=[jax.ShapeDtypeStruct((R, C), F32)] * 4)(stacked, w, m, v)


def _place():
    return lax.axis_index("x"), lax.axis_index("y"), lax.axis_index("c")


def _allgather8(name, block):
    m_per, n = block.shape

    def body(x_ref, out_ref, token_ref, send_sems, recv_sems, local_sem):
        token_ref[...] = jnp.zeros_like(token_ref)
        x, y, c = _place()
        me, sibling = (x, y, c), (x, y, 1 - c)
        chips = [(1 - x, y), (x, 1 - y), (1 - x, 1 - y)]

        def rows(px, py, pc):
            return out_ref.at[pl.ds((4 * px + 2 * py + pc) * m_per, m_per), :]

        def copy(k, blk, to, src=None):
            return pltpu.make_async_remote_copy(
                src_ref=rows(*blk) if src is None else src, dst_ref=rows(*blk),
                send_sem=send_sems.at[k], recv_sem=recv_sems.at[k], device_id=to, device_id_type=MESH)

        mine = pltpu.make_async_copy(x_ref, rows(*me), local_sem)
        mine.start()
        first = [copy(0, me, sibling, src=x_ref)]
        first += [copy(1 + j, me, (*chip, c), src=x_ref) for j, chip in enumerate(chips)]
        for cp in first:
            cp.start()
        passed = [copy(4 + j, (*chip, c), sibling) for j, chip in enumerate(chips)]
        for j, chip in enumerate(chips):
            copy(1 + j, (*chip, c), me).wait_recv()
            passed[j].start()
        copy(0, sibling, me).wait_recv()
        for j, chip in enumerate(chips):
            copy(4 + j, (*chip, 1 - c), me).wait_recv()
        for cp in first + passed:
            cp.wait_send()
        mine.wait()

    vmem = pl.BlockSpec(memory_space=pltpu.VMEM)
    return pl.pallas_call(
        body, name=name,
        out_shape=[jax.ShapeDtypeStruct((N_DEV * m_per, n), block.dtype), jax.ShapeDtypeStruct((8, LANES), F32)],
        in_specs=[vmem], out_specs=[vmem, vmem],
        scratch_shapes=[pltpu.SemaphoreType.DMA((7,)), pltpu.SemaphoreType.DMA((7,)), pltpu.SemaphoreType.DMA],
    )(block)


_ANY = pl.BlockSpec(memory_space=pl.ANY)


def _half(ref, c, rows):
    return ref.at[pl.ds(c * (rows // 2), rows // 2), :]


_HBM = pl.BlockSpec(memory_space=pltpu.HBM)
_SEM = pl.BlockSpec(memory_space=pltpu.SEMAPHORE)
_EFFECT = pltpu.SideEffectType.DATAFLOW_SIDE_EFFECTING


def _ici_start(name, srcs, land_shapes, plan, per_source=3, after=None):
    ns, nl = len(srcs), len(land_shapes)
    n_copies = per_source * ns
    n_in = ns + nl + (after is not None)

    def body(*refs):
        src_refs, land_refs = refs[:ns], refs[ns:ns + nl]
        send_sems, recv_sems = refs[n_in], refs[n_in + 1]
        token = refs[-1]
        for n, (src, dst, peer, _) in enumerate(plan(src_refs, land_refs)):
            pltpu.make_async_remote_copy(src_ref=src, dst_ref=dst, send_sem=send_sems.at[n], recv_sem=recv_sems.at[n],
                                         device_id=peer, device_id_type=MESH).start()
        token[...] = jnp.zeros_like(token)

    lands = [lax.empty(s.shape, s.dtype) for s in land_shapes]
    out = pl.pallas_call(
        body, name=name,
        out_shape=(pltpu.SemaphoreType.DMA((n_copies,)), pltpu.SemaphoreType.DMA((n_copies,)),
                   *[pltpu.HBM(a.shape, a.dtype) for a in list(srcs) + lands], jax.ShapeDtypeStruct((8, LANES), F32)),
        in_specs=[_HBM] * (ns + nl) + [_ANY] * (after is not None),
        out_specs=(_SEM, _SEM, *[_HBM] * (ns + nl), pl.BlockSpec(memory_space=pltpu.VMEM)),
        input_output_aliases={n: 2 + n for n in range(ns + nl)},
        compiler_params=pltpu.CompilerParams(has_side_effects=_EFFECT),
    )(*[pltpu.with_memory_space_constraint(a, pltpu.HBM) for a in list(srcs) + lands],
      *([] if after is None else [after]))
    return out[0], out[1], list(out[2:2 + ns]), list(out[2 + ns:2 + ns + nl]), out[-1]


def _ici_wait(name, send_sems, recv_sems, srcs, lands, plan, after):
    ns, nl = len(srcs), len(lands)
    after = list(after) if isinstance(after, (list, tuple)) else [after]

    def body(*refs):
        src_refs, land_refs = refs[:ns], refs[ns:ns + nl]
        send_sems, recv_sems = refs[ns + nl], refs[ns + nl + 1]
        for n, (src, _, peer, mine) in enumerate(plan(src_refs, land_refs)):
            cp = pltpu.make_async_remote_copy(src_ref=src, dst_ref=mine, send_sem=send_sems.at[n],
                                              recv_sem=recv_sems.at[n], device_id=peer, device_id_type=MESH)
            cp.wait_send()
            cp.wait_recv()

    out = pl.pallas_call(
        body, name=name, out_shape=[pltpu.HBM(a.shape, a.dtype) for a in list(srcs) + list(lands)],
        in_specs=[_HBM] * (ns + nl) + [_SEM, _SEM] + [_ANY] * len(after), out_specs=[_HBM] * (ns + nl),
        input_output_aliases={n: n for n in range(ns + nl)},
        compiler_params=pltpu.CompilerParams(has_side_effects=_EFFECT),
    )(*srcs, *lands, send_sems, recv_sems, *after)
    return list(out[:ns]), list(out[ns:])


def _own_slab(name, chip, w, after):
    R, C = w.shape
    tr, tc = _tiles(R, C)
    tied = [] if after is None else [after]

    def body(chip_ref, w_ref, *rest):
        stack_ref, token_ref = rest[-2:]
        stack_ref[0] = w_ref[...].astype(BF16)
        token_ref[...] = jnp.zeros_like(token_ref)

    small = pl.BlockSpec((8, LANES), lambda r, q, chip_ref: (0, 0))
    grid_spec = pltpu.PrefetchScalarGridSpec(
        num_scalar_prefetch=1, grid=(R // tr, C // tc),
        in_specs=[pl.BlockSpec((tr, tc), lambda r, q, chip_ref: (r, q))] + [small] * len(tied),
        out_specs=[pl.BlockSpec((1, tr, tc), lambda r, q, chip_ref: (chip_ref[0], r, q)), small])
    return pl.pallas_call(
        body, name=name, grid_spec=grid_spec,
        out_shape=[jax.ShapeDtypeStruct((N_CHIPS, R, C), BF16), jax.ShapeDtypeStruct((8, LANES), F32)],
        compiler_params=pltpu.CompilerParams(dimension_semantics=("arbitrary", "arbitrary")),
    )(chip, w, *tied)


def _gather_plan(src_refs, land_refs):
    x, y, c = _place()
    copies = []
    for stack in src_refs:
        R = stack.shape[1]
        own = _half(stack.at[2 * x + y], c, R)
        for cx, cy in [(1 - x, y), (x, 1 - y), (1 - x, 1 - y)]:
            copies.append((own, own, (cx, cy, c), _half(stack.at[2 * cx + cy], c, R)))
    return copies


def _pass_plan(src_refs, land_refs):
    x, y, c = _place()
    copies = []
    for land in src_refs:
        R = land.shape[1]
        for cx, cy in [(1 - x, y), (x, 1 - y), (1 - x, 1 - y)]:
            slot = land.at[2 * cx + cy]
            copies.append((_half(slot, c, R), _half(slot, c, R), (x, y, 1 - c), _half(slot, 1 - c, R)))
    return copies


def _share_plan(src_refs, land_refs):
    x, y, c = _place()
    return [(h, land, (x, y, 1 - c), land) for h, land in zip(src_refs, land_refs)]


def _pass_to_sibling(name, lands):
    nw = len(lands)

    def body(*refs):
        ins, outs = refs[:nw], refs[nw:2 * nw]
        send_sems, recv_sems = refs[2 * nw:]
        x, y, c = _place()
        chips = [(1 - x, y), (x, 1 - y), (1 - x, 1 - y)]
        copies = []
        for k in range(nw):
            R = ins[k].shape[1]
            for j, (cx, cy) in enumerate(chips):
                cp = pltpu.make_async_remote_copy(
                    src_ref=_half(ins[k].at[2 * cx + cy], c, R), dst_ref=_half(outs[k].at[2 * cx + cy], c, R),
                    send_sem=send_sems.at[3 * k + j], recv_sem=recv_sems.at[3 * k + j],
                    device_id=(x, y, 1 - c), device_id_type=MESH)
                cp.start()
                copies.append(cp)
        for k in range(nw):
            R = ins[k].shape[1]
            for j, (cx, cy) in enumerate(chips):
                pltpu.make_async_remote_copy(
                    src_ref=_half(ins[k].at[2 * cx + cy], c, R), dst_ref=_half(outs[k].at[2 * cx + cy], 1 - c, R),
                    send_sem=send_sems.at[3 * k + j], recv_sem=recv_sems.at[3 * k + j],
                    device_id=(x, y, 1 - c), device_id_type=MESH).wait_recv()
        for cp in copies:
            cp.wait_send()

    return pl.pallas_call(
        body, name=name, out_shape=[jax.ShapeDtypeStruct(a.shape, a.dtype) for a in lands],
        in_specs=[_ANY] * nw, out_specs=[_ANY] * nw, input_output_aliases={k: k for k in range(nw)},
        scratch_shapes=[pltpu.SemaphoreType.DMA((3 * nw,)), pltpu.SemaphoreType.DMA((3 * nw,))],
    )(*lands)


def _tie(vec, token):
    return vec + token[0:1, 0:1]


ROW_ALIGN = 16
TILE_ELEMS = 512 * 1024


def _tiles(rows, cols):
    fits = [t for t in range(ROW_ALIGN, min(rows, 256) + 1, ROW_ALIGN) if rows % t == 0]
    tr = fits[-1] if fits and fits[-1] >= 64 else rows
    tc = cols
    while tr * tc > TILE_ELEMS and tc % (2 * LANES) == 0:
        tc //= 2
    return tr, tc


def _scatter_plan(src_refs, land_refs):
    x, y, c = _place()
    copies = []
    for p, land in zip(src_refs, land_refs):
        for j, (cx, cy) in enumerate([(1 - x, y), (x, 1 - y), (1 - x, 1 - y)]):
            copies.append((p.at[2 * cx + cy], land.at[j], (cx, cy, c), land.at[j]))
    return copies


def _chip_add(name, chip, sums, recv):
    _, H, C = sums.shape
    tr, tc = _tiles(H, C)

    def body(chip_ref, p_ref, r_ref, o_ref):
        total = p_ref[0].astype(F32)
        for j in range(3):
            total = total + r_ref[j].astype(F32)
        o_ref[...] = total

    grid_spec = pltpu.PrefetchScalarGridSpec(
        num_scalar_prefetch=1, grid=(H // tr, C // tc),
        in_specs=[pl.BlockSpec((1, tr, tc), lambda r, q, chip_ref: (chip_ref[0], r, q)),
                  pl.BlockSpec((3, tr, tc), lambda r, q, chip_ref: (0, r, q))],
        out_specs=pl.BlockSpec((tr, tc), lambda r, q, chip_ref: (r, q)))
    return pl.pallas_call(
        body, name=name, grid_spec=grid_spec, out_shape=jax.ShapeDtypeStruct((H, C), F32),
        compiler_params=pltpu.CompilerParams(dimension_semantics=("parallel", "parallel")),
    )(chip, sums, recv)


def _pair_share(name, halves):
    nw = len(halves)

    def body(*refs):
        hs, outs = refs[:nw], refs[nw:2 * nw]
        send_sems, recv_sems = refs[2 * nw:]
        x, y, c = _place()
        copies = []
        for k in range(nw):
            cp = pltpu.make_async_remote_copy(
                src_ref=hs[k], dst_ref=outs[k], send_sem=send_sems.at[k], recv_sem=recv_sems.at[k],
                device_id=(x, y, 1 - c), device_id_type=MESH)
            cp.start()
            copies.append(cp)
        for cp in copies:
            cp.wait()

    return pl.pallas_call(
        body, name=name,
        out_shape=[jax.ShapeDtypeStruct(h.shape, h.dtype) for h in halves],
        in_specs=[_ANY] * nw, out_specs=[_ANY] * nw,
        scratch_shapes=[pltpu.SemaphoreType.DMA((nw,)), pltpu.SemaphoreType.DMA((nw,))],
    )(*halves)


def _adam_halves(name, core, w, g_own, g_other, m, v):
    R, C = w.shape
    H = R // 2
    tr, tc = _tiles(H, C)
    nr, nc = H // tr, C // tc

    def body(core_ref, w_ref, go_ref, gr_ref, m_ref, v_ref, g_ref, dl_ref, nm_ref, nv_ref):
        own = (pl.program_id(0) // nr) == core_ref[0]
        g = jnp.where(own, go_ref[...], gr_ref[...])
        g_ref[...] = g
        dl_ref[...], nm_ref[...], nv_ref[...] = _adamw(w_ref[...], g, m_ref[...], v_ref[...])

    blk = pl.BlockSpec((tr, tc), lambda r, q, core_ref: (r, q))

    def half_spec(is_own):
        def index(r, q, core_ref):
            mine = ((r // nr) == core_ref[0]) == is_own
            done = is_own == (core_ref[0] == 0)
            return (jnp.where(mine, r % nr, jnp.where(done, nr - 1, 0)), jnp.where(mine, q, jnp.where(done, nc - 1, 0)))
        return pl.BlockSpec((tr, tc), index)
    out = jax.ShapeDtypeStruct((R, C), F32)
    grid_spec = pltpu.PrefetchScalarGridSpec(
        num_scalar_prefetch=1, grid=(R // tr, nc), in_specs=[blk, half_spec(True), half_spec(False), blk, blk],
        out_specs=[blk] * 4)
    return pl.pallas_call(
        body, name=name, grid_spec=grid_spec, out_shape=[out] * 4,
        compiler_params=pltpu.CompilerParams(dimension_semantics=("parallel", "parallel"),
                                             vmem_limit_bytes=_vmem(20 * tr * tc * 4)),
    )(core, w, g_own, g_other, m, v)


def kernel(x, c, w_mod, b_mod, g_pre_mix, g_post_mix, w_in, b_forget, swa_sinks, w_out, g_pre_mlp, g_post_mlp, w_up, w_down, loss_target, m_w_mod, m_b_mod, m_g_pre_mix, m_g_post_mix, m_w_in, m_b_forget, m_swa_sinks, m_w_out, m_g_pre_mlp, m_g_post_mlp, m_w_up, m_w_down, v_w_mod, v_b_mod, v_g_pre_mix, v_g_post_mix, v_w_in, v_b_forget, v_swa_sinks, v_w_out, v_g_pre_mlp, v_g_post_mlp, v_w_up, v_w_down):
    S, D = x.shape[1], x.shape[2]
    n_heads = D // HEAD_DIM
    n_fox = n_heads // 2
    n_swa = n_heads - n_fox
    n_kv = max(1, n_swa // 4)
    fox_w, swa_w, kv_w = n_fox * HEAD_DIM, n_swa * HEAD_DIM, n_kv * HEAD_DIM
    main_w = 3 * fox_w + swa_w + 2 * kv_w
    in_w = main_w + n_fox
    mod_cols = w_mod.shape[2]

    ax, ay, ac = _place()
    chip = 2 * ax + ay
    dev = 2 * chip + ac
    chip_arr = jnp.reshape(chip, (1,)).astype(jnp.int32)
    core_arr = jnp.reshape(ac, (1,)).astype(jnp.int32)

    x2, tgt = x[0], loss_target[0]

    in_rows = in_w // N_CHIPS
    in_rows_pad = -(-in_rows // (2 * LANES)) * (2 * LANES)
    slab_w = N_CHIPS * in_rows_pad

    def rows_of(a):
        return jnp.pad(a[0].T, ((0, in_rows_pad - in_rows), (0, 0)))

    w_in_stack, token = _own_slab("own_slab_w_in", chip_arr, rows_of(w_in), None)

    c_all, _ = _allgather8("gather_c", _tie(c, token).reshape(8, D // 8))
    c_all = c_all.reshape(N_DEV, D)
    b_shard = lax.dynamic_slice_in_dim(b_mod, chip * mod_cols, mod_cols, axis=1)
    mod_shard = _mod_fwd(jnp.pad(c_all, ((0, 16 - N_DEV), (0, 0))), w_mod[0], b_shard)[:N_DEV]
    mod_all, token = _allgather8("gather_mod", mod_shard)
    mod_all = mod_all.reshape(N_CHIPS, 2, N_DEV, mod_cols)[:, 0]
    mod = lax.dynamic_index_in_dim(mod_all, dev, axis=1, keepdims=False).reshape(N_MOD, 1, D)
    sh_a, sc_a, gt_a, sh_m, sc_m, gt_m = [mod[n] for n in range(N_MOD)]

    def slab_cols(lo, hi):
        spans = []
        while lo < hi:
            s, r = divmod(lo, in_rows)
            n = min(hi - lo, in_rows - r)
            spans.append((s * in_rows_pad + r, s * in_rows_pad + r + n))
            lo += n
        return spans

    gate_lo = 3 * fox_w
    main_spans = slab_cols(0, gate_lo) + slab_cols(gate_lo + n_fox, in_w)
    (gate_first, gate_last), = slab_cols(gate_lo, gate_lo + n_fox)

    names = ["w_in", "w_out", "w_up", "w_down"]
    flights = {}
    for n, w in zip(names, [None, w_out[0], w_up[0], w_down[0]]):
        stack = w_in_stack if n == "w_in" else _own_slab("own_slab_" + n, chip_arr, w, token)[0]
        flights[n] = _ici_start("gather_start_" + n, [stack], [], _gather_plan, after=token)
        token = flights[n][4]
    sc_a = _tie(sc_a, token)

    def arrived(n, after):
        send, recv, stacks, _, _ = flights[n]
        stacks, _ = _ici_wait("gather_wait_" + n, send, recv, stacks, [], _gather_plan, after)
        return _ici_start("gather_pass_start_" + n, stacks, [], _pass_plan)

    def gathered(n, after, in_flight=None):
        if in_flight is None:
            send, recv, stacks, _, _ = flights[n]
            stacks, _ = _ici_wait("gather_wait_" + n, send, recv, stacks, [], _gather_plan, after)
            return _pass_to_sibling("gather_pass_" + n, stacks)[0]
        send, recv, stacks, _, _ = in_flight
        return _ici_wait("gather_pass_wait_" + n, send, recv, stacks, [], _pass_plan, after)[0][0]

    d_ff = N_CHIPS * w_up.shape[2]

    h = _pre_norm(x2, g_pre_mix, sc_a, sh_a)
    in_state = [rows_of(w_in)] + [rows_of(_tie(a, token)) for a in (m_w_in, v_w_in)]
    cos, sin_signed = _rope_tables(S)

    def pack(bm, gpm, gqm, gpl, gql, bf, sk):
        last = jnp.concatenate([bf, sk, jnp.zeros((1, D - n_fox - n_swa), F32)], axis=1)
        return jnp.concatenate([bm.reshape(N_MOD, D), gpm, gqm, gpl, gql, last, jnp.zeros((5, D), F32)], axis=0)

    small_state = [pack(b_mod, g_pre_mix, g_post_mix, g_pre_mlp, g_post_mlp, b_forget, swa_sinks),
                   pack(m_b_mod, m_g_pre_mix, m_g_post_mix, m_g_pre_mlp, m_g_post_mlp, m_b_forget, m_swa_sinks),
                   pack(v_b_mod, v_g_pre_mix, v_g_post_mix, v_g_pre_mlp, v_g_post_mlp, v_b_forget, v_swa_sinks)]
    ready = h[:8, :LANES].astype(F32) + cos[:8]
    w_slab_t = gathered("w_in", [ready] + in_state[1:] + small_state).reshape(slab_w, D)
    tm_p, tn_p = _fit(MM_TM, S), _fit(MM_TN if slab_w % MM_TN == 0 else MM_TN // 2, slab_w)
    win0 = gate_first // LANES * LANES
    win_j, win_off = divmod(win0, tn_p)
    assert win_off + 2 * LANES <= tn_p and gate_last - win0 <= 2 * LANES

    def proj_epilogue(acc, ex, outs):
        outs[0][...] = acc.astype(BF16)

        @pl.when(pl.program_id(1) == win_j)
        def _():
            outs[1][...] = acc[:, win_off:win_off + 2 * LANES]

    proj_slab, gate_win = _matmul(
        "in_proj", h, w_slab_t, "nt",
        [((S, slab_w), BF16, (tm_p, tn_p), lambda i, j: (i, j)), ((S, 2 * LANES), F32, (tm_p, 2 * LANES), lambda i, j: (i, 0))],
        proj_epilogue, tn=tn_p, revisits=True)
    proj = jnp.concatenate([proj_slab[:, lo:hi] for lo, hi in main_spans], axis=1)
    out_flight = arrived("w_out", proj_slab)
    fg = _tie(jnp.pad(gate_win[:, gate_first - win0:gate_last - win0], ((0, 0), (0, LANES - n_fox))), out_flight[4])
    b_pad = jnp.pad(b_forget, ((0, 0), (0, LANES - n_fox)))
    cum_row = _fox_gate_fwd(fg, b_pad)[:n_fox].reshape(n_fox, 1, S)
    fox_o, fox_lse = _fox_fwd(proj, cum_row, n_fox)

    rq = _rope("rope_fwd", proj, 3 * n_fox, n_swa + n_kv, cos, sin_signed)
    v_first = 3 * n_fox + n_swa + n_kv
    sinks = swa_sinks[0]
    swa_o, swa_lse = _swa_fwd(rq, proj, v_first, sinks, n_swa, n_kv)

    mixcat = jnp.concatenate([fox_o, swa_o], axis=1).astype(BF16)
    up_flight = arrived("w_up", mixcat)
    w_out_f = gathered("w_out", mixcat, out_flight).reshape(D, D)
    mix = _mm_plain("out_proj", mixcat, w_out_f, "nn", BF16, after=up_flight[4])
    x1, h2 = _post_mix(x2, mix, g_post_mix, gt_a, g_pre_mlp, sc_m, sh_m)
    w_up_f = gathered("w_up", h2, up_flight)

    tm_u, tn_u = _fit(MM_TM, S), _fit(MM_TN, d_ff)

    def up_epilogue(acc, ex, outs):
        outs[0][...] = acc.astype(BF16)
        r = jnp.maximum(acc, 0.0)
        outs[1][...] = (r * r).astype(BF16)

    ublk = ((S, d_ff), BF16, (tm_u, tn_u), lambda i, j: (i, j))
    u, a = _matmul("mlp_up", h2, w_up_f, "nn", [ublk, ublk], up_epilogue)
    w_down_f = gathered("w_down", a).reshape(d_ff, D)
    y = _mm_plain("mlp_down", a, w_down_f, "nn", BF16)

    dy, dout, loss_part, acc_mlp_post = _loss_and_post_mlp_bwd(x1, y, tgt, g_post_mlp, gt_m)

    def du_epilogue(acc, ex, outs):
        outs[0][...] = (acc * (2.0 * jnp.maximum(ex[0][...].astype(F32), 0.0))).astype(BF16)

    du = _matmul("mlp_down_bwd", dy, w_down_f, "nt", [ublk], du_epilogue,
                 extras=[(u, (tm_u, tn_u), lambda i, j: (i, j))])[0]
    def pair_send(tag, part):
        return _ici_start("grad_pair_start_" + tag, [part], [jax.ShapeDtypeStruct(part.shape, BF16)], _share_plan,
                          per_source=1)

    def pair_recv(tag, flight, after):
        send, recv, srcs, lands, _ = flight
        return _ici_wait("grad_pair_wait_" + tag, send, recv, srcs, lands, _share_plan, after)[1][0]

    def scatter_start(tag, sums, after=None):
        return _ici_start("grad_scatter_start_" + tag, sums,
                          [jax.ShapeDtypeStruct((3,) + p.shape[1:], BF16) for p in sums], _scatter_plan, after=after)

    def scatter_finish(tag, flight, after):
        send, recv, srcs, lands, _ = flight
        sums, received = _ici_wait("grad_scatter_wait_" + tag, send, recv, srcs, lands, _scatter_plan, after)
        return [_chip_add("chip_add_%s_%d" % (tag, k), chip_arr, p, r) for k, (p, r) in enumerate(zip(sums, received))]

    tm_g = _fit(MM_TM, D // 2)
    pair_down = pair_send("down", _grad_half("grad_w_down_a", core_arr, a, dy, N_CHIPS, 1, tm_g, True))
    pair_up = pair_send("up", _grad_half("grad_w_up_a", core_arr, h2, du, 1, N_CHIPS, tm_g, True, after=pair_down[4]))
    sum_down = _grad_half("grad_w_down_b", core_arr, a, dy, N_CHIPS, 1, tm_g, False,
                          recv=pair_recv("down", pair_down, pair_up[4]))
    sum_up = _grad_half("grad_w_up_b", core_arr, h2, du, 1, N_CHIPS, tm_g, False, recv=pair_recv("up", pair_up, sum_down))
    flight_mlp = scatter_start("mlp", [sum_up, sum_down])
    dh2 = _mm_plain("mlp_up_bwd", du, w_up_f, "nt", BF16, after=flight_mlp[4])
    dx1, dmix, acc_mid = _pre_mlp_and_post_mix_bwd(dh2, x1, dout, mix, _tie(g_pre_mlp, flight_mlp[4]), sc_m,
                                                   g_post_mix, gt_a)

    dmixcat = _mm_plain("out_proj_bwd", dmix, w_out_f, "nt", F32)

    fdq, fdk, fdv, dcum_row, dcum_q = _fox_bwd(proj, fox_o, dmixcat, fox_lse, cum_row, n_fox)
    dcum_k = jnp.pad(dcum_row.reshape(n_fox, S), ((0, LANES - n_fox), (0, 0)))
    dfg, db_forget = _fox_gate_bwd(dcum_k, dcum_q, fg, b_pad)

    group_w = (n_swa // n_kv) * HEAD_DIM
    sdq, sdk, sdv, dsink = _swa_bwd(rq, proj, v_first, sinks, swa_o, dmixcat, fox_w // group_w, swa_lse, n_swa, n_kv)
    drq = jnp.concatenate([sdq, jnp.transpose(sdk, (1, 0, 2)).reshape(S, kv_w).astype(BF16)], axis=1)
    d_sq_sk = _rope("rope_bwd", drq, 0, n_swa + n_kv, cos, -sin_signed)
    dsv = jnp.transpose(sdv, (1, 0, 2)).reshape(S, kv_w).astype(BF16)
    dproj = jnp.concatenate([fdq, fdk, fdv, d_sq_sk, dsv], axis=1)

    pieces = []
    for s in range(N_CHIPS):
        lo, hi = s * in_rows, (s + 1) * in_rows
        for src, first, last, shift in [(dproj, 0, gate_lo, 0), (dfg, gate_lo, gate_lo + n_fox, gate_lo),
                                        (dproj, gate_lo + n_fox, in_w, n_fox)]:
            if max(lo, first) < min(hi, last):
                pieces.append(src[:, max(lo, first) - shift:min(hi, last) - shift])
        pieces.append(jnp.zeros((S, in_rows_pad - in_rows), BF16))
    dproj_slab = jnp.concatenate(pieces, axis=1)

    tm_in, tm_out = in_rows_pad // 2, D // (2 * N_CHIPS)
    pair_in = pair_send("in", _grad_half("grad_w_in_a", core_arr, dproj_slab, h, N_CHIPS, 1, tm_in, True))
    pair_out = pair_send("out", _grad_half("grad_w_out_a", core_arr, mixcat, dmix, N_CHIPS, 1, tm_out, True,
                                           after=pair_in[4]))
    sum_in = _grad_half("grad_w_in_b", core_arr, dproj_slab, h, N_CHIPS, 1, tm_in, False,
                        recv=pair_recv("in", pair_in, pair_out[4]))
    sum_out = _grad_half("grad_w_out_b", core_arr, mixcat, dmix, N_CHIPS, 1, tm_out, False,
                         recv=pair_recv("out", pair_out, sum_in[0, :8, :LANES]))
    dh = _mm_plain("in_proj_bwd", dproj_slab, w_slab_t, "nn", BF16, tk=slab_w // 2,
                   after=sum_out[0, :8, :LANES].astype(F32))
    grad_x, acc_pre = _pre_mix_bwd(dh, x2, dx1, g_pre_mix, sc_a)

    zero_row = jnp.zeros((1, D), F32)
    tail = jnp.concatenate([db_forget[0:1, :n_fox], dsink[:, 0, :n_swa // n_kv].reshape(1, n_swa),
                            loss_part[0:1, 0:1], jnp.zeros((1, D - n_fox - n_swa - 1), F32)], axis=1)
    partial = jnp.concatenate([
        acc_pre[0:1], acc_pre[1:2], acc_mid[3:4], acc_mid[0:1], acc_mid[1:2], acc_mlp_post[0:1],
        acc_pre[2:3], acc_mid[4:5], acc_mid[2:3], acc_mlp_post[1:2], tail] + [zero_row] * 5, axis=0)
    gathered_small, token = _allgather8("gather_small_grads", partial)

    flight_mix = scatter_start("mix", [sum_in, sum_out], after=token)
    halves_mlp = scatter_finish("mlp", flight_mlp, flight_mix[4])
    share_up, share_down = [
        _ici_start("grad_share_start_" + n, [hv], [jax.ShapeDtypeStruct(hv.shape, F32)], _share_plan, per_source=1)
        for n, hv in zip(["up", "down"], halves_mlp)]

    def shared(tag, flight, after):
        send, recv, own, lands, _ = flight
        own, other = _ici_wait("grad_share_wait_" + tag, send, recv, own, lands, _share_plan, after)
        return own[0], other[0]

    def unpack(p):
        return {"b_mod": p[0:N_MOD].reshape(1, N_MOD * D), "g_pre_mix": p[6:7], "g_post_mix": p[7:8],
                "g_pre_mlp": p[8:9], "g_post_mlp": p[9:10], "b_forget": p[10:11, :n_fox],
                "swa_sinks": p[10:11, n_fox:n_fox + n_swa]}

    small_out = _small_update(gathered_small, _tie(small_state[0], share_down[4] + share_up[4]), small_state[1],
                              small_state[2])
    g_small, d_small, m_small, v_small = [unpack(p) for p in small_out]
    loss = small_out[0][N_MOD + 4, n_fox + n_swa]

    dmod_all = gathered_small.reshape(N_DEV, 16, D)[:, :N_MOD].reshape(N_DEV, N_MOD * D)
    dmod_shard = _tie(lax.dynamic_slice_in_dim(dmod_all, chip * mod_cols, mod_cols, axis=1), share_down[4])
    g_w_mod, d_w_mod, nm_w_mod, nv_w_mod = _mod_update(c_all.T, dmod_shard, w_mod[0], m_w_mod[0], v_w_mod[0])

    grads = dict(g_small, w_mod=g_w_mod[None])
    deltas = dict(d_small, w_mod=d_w_mod[None])
    new_m = dict(m_small, w_mod=nm_w_mod[None])
    new_v = dict(v_small, w_mod=nv_w_mod[None])
    weights = {"w_in": (w_in, m_w_in, v_w_in), "w_out": (w_out, m_w_out, v_w_out), "w_up": (w_up, m_w_up, v_w_up),
               "w_down": (w_down, m_w_down, v_w_down)}

    def big_update(n, own, other):
        transposed = n == "w_in"
        w, m, v = in_state if transposed else [a[0] for a in weights[n]]
        outs = _adam_halves("adam_" + n, core_arr, w, own, other, m, v)
        if transposed:
            outs = [o[:in_rows].T for o in outs]
        grads[n], deltas[n], new_m[n], new_v[n] = [o[None] for o in outs]

    big_update("w_down", *shared("down", share_down, d_w_mod[:8, :LANES] + small_out[1][:8, :LANES]))
    halves_mix = scatter_finish("mix", flight_mix, deltas["w_down"][0, :8, :LANES] + d_w_mod[:8, :LANES])
    others_mix = _pair_share("grad_pair_share_mix", halves_mix)
    big_update("w_in", halves_mix[0], others_mix[0])
    big_update("w_out", halves_mix[1], others_mix[1])
    big_update("w_up", *shared("up", share_up, deltas["w_out"][0, :8, :LANES] + deltas["w_in"][0, :8, :LANES]))

    order = ["w_mod", "b_mod", "g_pre_mix", "g_post_mix", "w_in", "b_forget", "swa_sinks", "w_out", "g_pre_mlp",
             "g_post_mlp", "w_up", "w_down"]
    return (loss, grad_x[None], *[grads[n] for n in order], *[deltas[n] for n in order],
            *[new_m[n] for n in order], *[new_v[n] for n in order])
```

```python
import jax
import jax.numpy as jnp
from jax import lax
from jax.experimental import pallas as pl
from jax.experimental.pallas import tpu as pltpu

F32 = jnp.float32
BF16 = jnp.bfloat16
MESH = pl.DeviceIdType.MESH

HEAD_DIM = 128
SWA_BLOCK = 128
ROPE_THETA = 10000.0
NORM_EPS = 1e-6
NEG = -1e30
N_MOD = 6
ADAM_LR = 0.001
ADAM_B1 = 0.9
ADAM_B2 = 0.999
ADAM_EPS = 1e-08
ADAM_WD = 0.01
ADAM_STEP = 10
N_CHIPS = 4
N_DEV = 8
LANES = 128
VMEM_CAP = 60 * 1024 * 1024

_NN = (((1,), (0,)), ((), ()))
_NT = (((1,), (1,)), ((), ()))
_TN = (((0,), (0,)), ((), ()))


def _vmem(nbytes):
    return int(min(VMEM_CAP, nbytes * 5 // 4 + (4 << 20)))


def _nbytes(shape, dtype):
    n = 1
    for s in shape:
        n *= s
    return n * jnp.dtype(dtype).itemsize


def _fit(t, n):
    t = min(t, n)
    assert n % t == 0, (t, n)
    return t


MM_TM, MM_TN, MM_TK = 1024, 1024, 2048


def _matmul(name, a, b, mode, out_defs, epilogue, extras=(), tm=MM_TM, tn=MM_TN, tk=MM_TK, revisits=False,
            row_sel=None):
    stacked = b.ndim == 3
    b_rows, b_cols = b.shape[-2], b.shape[-1] * (b.shape[0] if stacked else 1)
    if mode == "nn":
        (M, K), (K2, N) = a.shape, (b_rows, b_cols)
    elif mode == "nt":
        (M, K), (N, K2) = a.shape, (b_rows, b_cols)
    else:
        (K, M), (K2, N) = a.shape, (b_rows, b_cols)
    assert K == K2 and not (stacked and mode == "tn"), (a.shape, b.shape, mode)
    tm = _fit(tm, M)
    tn = _fit(tn, b.shape[-1] if stacked and mode == "nn" else N)
    tk = _fit(tk, b.shape[-1] if stacked and mode == "nt" else K)
    nk = K // tk
    dims = {"nn": _NN, "nt": _NT, "tn": _TN}[mode]
    if row_sel is None:
        grid_m, a_row = M // tm, lambda i, *sel: i
    else:
        grid_m, a_row = row_sel[2], lambda i, *sel: row_sel[1](i, sel[0])
    a_spec = (pl.BlockSpec((tk, tm), lambda i, j, k, *sel: (k, a_row(i, *sel))) if mode == "tn"
              else pl.BlockSpec((tm, tk), lambda i, j, k, *sel: (a_row(i, *sel), k)))
    if stacked:
        per = b.shape[-1] // (tk if mode == "nt" else tn)
        b_spec = (pl.BlockSpec((1, tn, tk), lambda i, j, k, *sel: (k // per, j, k % per)) if mode == "nt"
                  else pl.BlockSpec((1, tk, tn), lambda i, j, k, *sel: (j // per, k, j % per)))
    else:
        b_spec = (pl.BlockSpec((tn, tk), lambda i, j, k, *sel: (j, k)) if mode == "nt"
                  else pl.BlockSpec((tk, tn), lambda i, j, k, *sel: (k, j)))
    n_ex, n_out = len(extras), len(out_defs)

    def body(*refs):
        if row_sel is not None:
            refs = refs[1:]
        a_ref, b_ref = refs[0], refs[1]
        ex = refs[2:2 + n_ex]
        outs = refs[2 + n_ex:2 + n_ex + n_out]
        b_blk = b_ref[0] if stacked else b_ref[...]
        prod = lax.dot_general(a_ref[...], b_blk, dims, preferred_element_type=F32)
        if nk == 1:
            epilogue(prod, ex, outs)
        else:
            acc_ref = refs[-1]
            k = pl.program_id(2)

            @pl.when(k == 0)
            def _():
                acc_ref[...] = prod

            @pl.when(k > 0)
            def _():
                acc_ref[...] += prod

            @pl.when(k == nk - 1)
            def _():
                epilogue(acc_ref[...], ex, outs)

    def wrap(f):
        return lambda i, j, k, *sel: f(i, j)

    in_specs = [a_spec, b_spec] + [pl.BlockSpec(blk, wrap(f)) for _, blk, f in extras]
    out_specs = [pl.BlockSpec(blk, wrap(f)) for _, _, blk, f in out_defs]
    out_shape = [jax.ShapeDtypeStruct(s, d) for s, d, _, _ in out_defs]
    need = 2 * (tm * tk + tk * tn) * a.dtype.itemsize + 3 * tm * tn * 4
    need += sum(2 * _nbytes(blk, arr.dtype) for arr, blk, _ in extras)
    need += sum(2 * _nbytes(blk, d) for _, d, blk, _ in out_defs)
    grid = (grid_m, N // tn, nk)
    scratch = [pltpu.VMEM((tm, tn), F32)] if nk > 1 else []
    params = pltpu.CompilerParams(
        dimension_semantics=("parallel", "arbitrary" if revisits else "parallel", "arbitrary"),
        vmem_limit_bytes=_vmem(need))
    operands = (a, b, *[arr for arr, _, _ in extras])
    if row_sel is None:
        return pl.pallas_call(body, name=name, grid=grid, in_specs=in_specs, out_specs=out_specs, out_shape=out_shape,
                              scratch_shapes=scratch, compiler_params=params)(*operands)
    grid_spec = pltpu.PrefetchScalarGridSpec(num_scalar_prefetch=1, grid=grid, in_specs=in_specs, out_specs=out_specs,
                                             scratch_shapes=scratch)
    return pl.pallas_call(body, name=name, grid_spec=grid_spec, out_shape=out_shape,
                          compiler_params=params)(row_sel[0], *operands)


def _grad_half(name, core, a, b, row_slabs, col_slabs, tm, other, recv=None, after=None):
    (_, M), (_, N) = a.shape, b.shape
    H = M // (2 * row_slabs)
    nh = H // tm
    tn = _fit(MM_TN, N // col_slabs)
    per = N // col_slabs // tn

    def a_block(i, core_ref):
        half = (1 - core_ref[0]) if other else core_ref[0]
        return (i // nh) * (2 * nh) + half * nh + i % nh

    def out_index(i, j):
        return (j // per, i, j % per) if col_slabs > 1 else (i // nh, i % nh, j)

    slabs = max(row_slabs, col_slabs)
    out_def = ((slabs, H, N // col_slabs), BF16, (1, tm, tn), out_index)

    def epilogue(acc, ex, outs):
        outs[0][0] = (acc if recv is None else acc + ex[0][0].astype(F32)).astype(BF16)

    extras = ([] if recv is None else [(recv, (1, tm, tn), out_index)]) + ([] if after is None else [_behind(after)])
    return _matmul(name, a, b, "tn", [out_def], epilogue, extras=extras, tm=tm, tn=tn,
                   row_sel=(core, a_block, row_slabs * nh))[0]


def _behind(token):
    return (token, (8, LANES), lambda i, j: (0, 0))


def _mm_plain(name, a, b, mode, out_dtype, after=None, **tiles):
    if mode == "nn":
        M, N = a.shape[0], b.shape[-1] * (b.shape[0] if b.ndim == 3 else 1)
    elif mode == "nt":
        M, N = a.shape[0], b.shape[-2]
    else:
        M, N = a.shape[1], b.shape[1]
    tm, tn = _fit(tiles.get("tm", MM_TM), M), _fit(tiles.get("tn", MM_TN), N)

    def epi(acc, ex, outs):
        outs[0][...] = acc.astype(out_dtype)

    return _matmul(name, a, b, mode, [((M, N), out_dtype, (tm, tn), lambda i, j: (i, j))], epi,
                   extras=[] if after is None else [_behind(after)], **tiles)[0]


def _rstd(v):
    return lax.rsqrt(jnp.mean(v * v, axis=-1, keepdims=True) + NORM_EPS)


ROW_TILE = 256


def _row_call(name, body, row_ins, vec_ins, row_outs, acc_outs, S, D):
    tr = _fit(ROW_TILE, S)
    row_spec = pl.BlockSpec((tr, D), lambda r: (r, 0))
    vec_spec = pl.BlockSpec((1, D), lambda r: (0, 0))
    in_specs = [row_spec] * len(row_ins) + [vec_spec] * len(vec_ins)
    out_specs = [row_spec] * len(row_outs) + [pl.BlockSpec(shp, lambda r: (0, 0)) for shp in acc_outs]
    out_shape = [jax.ShapeDtypeStruct((S, D), d) for d in row_outs] + [jax.ShapeDtypeStruct(shp, F32) for shp in acc_outs]
    need = sum(2 * tr * D * a.dtype.itemsize for a in row_ins) + sum(2 * tr * D * jnp.dtype(d).itemsize for d in row_outs)
    need += 8 * tr * D * 4
    return pl.pallas_call(
        body, name=name, grid=(S // tr,), in_specs=in_specs, out_specs=out_specs, out_shape=out_shape,
        compiler_params=pltpu.CompilerParams(dimension_semantics=("arbitrary",), vmem_limit_bytes=_vmem(need)),
    )(*row_ins, *vec_ins)


def _acc_rows(ref, rows):
    @pl.when(pl.program_id(0) == 0)
    def _():
        ref[...] = jnp.zeros_like(ref)
    for n, r in enumerate(rows):
        ref[n:n + 1, :] += r


def _pre_norm(x, g, sc, sh):
    S, D = x.shape

    def body(x_ref, g_ref, sc_ref, sh_ref, h_ref):
        xv = x_ref[...]
        xn = xv * _rstd(xv)
        h_ref[...] = (xn * g_ref[...] * (1.0 + sc_ref[...]) + sh_ref[...]).astype(BF16)

    return _row_call("pre_norm_mix", body, [x], [g, sc, sh], [BF16], [], S, D)[0]


def _post_mix(x, mix, g_post, gt, g_pre, sc, sh):
    S, D = x.shape

    def body(x_ref, mix_ref, gp_ref, gt_ref, g2_ref, sc_ref, sh_ref, x1_ref, h2_ref):
        mv = mix_ref[...].astype(F32)
        x1 = x_ref[...] + gt_ref[...] * (mv * _rstd(mv) * gp_ref[...])
        x1_ref[...] = x1
        h2_ref[...] = (x1 * _rstd(x1) * g2_ref[...] * (1.0 + sc_ref[...]) + sh_ref[...]).astype(BF16)

    return _row_call("post_mix_pre_mlp", body, [x, mix], [g_post, gt, g_pre, sc, sh], [F32, BF16], [], S, D)


def _loss_and_post_mlp_bwd(x1, y, target, g_post, gt):
    S, D = x1.shape

    def body(x1_ref, y_ref, t_ref, g_ref, gt_ref, dy_ref, dout_ref, loss_ref, acc_ref):
        yv = y_ref[...].astype(F32)
        r = _rstd(yv)
        yh = yv * r
        n = yh * g_ref[...]
        diff = x1_ref[...] + gt_ref[...] * n - t_ref[...]
        dout = diff * (1.0 / D)
        dout_ref[...] = dout
        dn = dout * gt_ref[...]
        dyh = dn * g_ref[...]
        dy_ref[...] = (r * (dyh - yh * jnp.mean(dyh * yh, axis=-1, keepdims=True))).astype(BF16)
        _acc_rows(acc_ref, [jnp.sum(dout * n, axis=0, keepdims=True), jnp.sum(dn * yh, axis=0, keepdims=True)])

        @pl.when(pl.program_id(0) == 0)
        def _():
            loss_ref[...] = jnp.zeros_like(loss_ref)
        loss_ref[...] += jnp.full(loss_ref.shape, (0.5 / D) * jnp.sum(diff * diff), F32)

    return _row_call("loss_post_mlp_bwd", body, [x1, y, target], [g_post, gt], [BF16, F32],
                     [(8, LANES), (8, D)], S, D)


def _pre_mlp_and_post_mix_bwd(dh2, x1, dout, mix, g_pre, sc, g_post, gt):
    S, D = x1.shape

    def body(dh_ref, x1_ref, dout_ref, mix_ref, g_ref, sc_ref, gp_ref, gt_ref, dx1_ref, dmix_ref, acc_ref):
        dh = dh_ref[...].astype(F32)
        x1v = x1_ref[...]
        r3 = _rstd(x1v)
        xn = x1v * r3
        dxn = dh * (1.0 + sc_ref[...]) * g_ref[...]
        dx1 = dout_ref[...] + r3 * (dxn - xn * jnp.mean(dxn * xn, axis=-1, keepdims=True))
        dx1_ref[...] = dx1
        mv = mix_ref[...].astype(F32)
        r2 = _rstd(mv)
        mh = mv * r2
        dn = dx1 * gt_ref[...]
        dmh = dn * gp_ref[...]
        dmix_ref[...] = (r2 * (dmh - mh * jnp.mean(dmh * mh, axis=-1, keepdims=True))).astype(BF16)
        _acc_rows(acc_ref, [
            jnp.sum(dh, axis=0, keepdims=True),
            jnp.sum(dh * xn * g_ref[...], axis=0, keepdims=True),
            jnp.sum(dh * (1.0 + sc_ref[...]) * xn, axis=0, keepdims=True),
            jnp.sum(dx1 * mh * gp_ref[...], axis=0, keepdims=True),
            jnp.sum(dn * mh, axis=0, keepdims=True)])

    return _row_call("pre_mlp_post_mix_bwd", body, [dh2, x1, dout, mix], [g_pre, sc, g_post, gt], [F32, BF16],
                     [(8, D)], S, D)


def _pre_mix_bwd(dh, x, dx1, g_pre, sc):
    S, D = x.shape

    def body(dh_ref, x_ref, dx1_ref, g_ref, sc_ref, gx_ref, acc_ref):
        dhv = dh_ref[...].astype(F32)
        xv = x_ref[...]
        r = _rstd(xv)
        xn = xv * r
        dxn = dhv * (1.0 + sc_ref[...]) * g_ref[...]
        gx_ref[...] = dx1_ref[...] + r * (dxn - xn * jnp.mean(dxn * xn, axis=-1, keepdims=True))
        _acc_rows(acc_ref, [
            jnp.sum(dhv, axis=0, keepdims=True),
            jnp.sum(dhv * xn * g_ref[...], axis=0, keepdims=True),
            jnp.sum(dhv * (1.0 + sc_ref[...]) * xn, axis=0, keepdims=True)])

    return _row_call("pre_mix_bwd", body, [dh, x, dx1], [g_pre, sc], [F32], [(8, D)], S, D)


CUM_BLOCK = 256


def _tri(n, upper):
    r = lax.broadcasted_iota(jnp.int32, (n, n), 0)
    c = lax.broadcasted_iota(jnp.int32, (n, n), 1)
    return ((c >= r) if upper else (c <= r)).astype(F32)


def _fox_gate_fwd(fg, b_pad):
    S = fg.shape[0]
    cb = _fit(CUM_BLOCK, S)

    def body(fg_ref, b_ref, cumt_ref, cum_ref):
        low = _tri(cb, False)
        carry = jnp.zeros((1, LANES), F32)
        for n in range(S // cb):
            z = fg_ref[n * cb:(n + 1) * cb, :] + b_ref[...]
            logf = jnp.minimum(z, 0.0) - jnp.log(1.0 + jnp.exp(-jnp.abs(z)))
            blk = jnp.dot(low, logf, precision=lax.Precision.HIGHEST, preferred_element_type=F32) + carry
            cum_ref[n * cb:(n + 1) * cb, :] = blk
            carry = blk[cb - 1:cb, :]
        cumt_ref[...] = cum_ref[...].T

    return pl.pallas_call(
        body, name="fox_gate_fwd", out_shape=jax.ShapeDtypeStruct((LANES, S), F32),
        scratch_shapes=[pltpu.VMEM((S, LANES), F32)],
        compiler_params=pltpu.CompilerParams(vmem_limit_bytes=_vmem(6 * S * LANES * 4)),
    )(fg, b_pad)


def _fox_gate_bwd(dcum_k, dcum_q, fg, b_pad):
    S = fg.shape[0]
    n_fox = dcum_q.shape[0]
    cb = _fit(CUM_BLOCK, S)

    def body(dk_ref, dq_ref, fg_ref, b_ref, dfg_ref, db_ref, dc_ref):
        lane = lax.broadcasted_iota(jnp.int32, (S, LANES), 1)
        dc = dk_ref[...].T
        for h in range(n_fox):
            dc = dc + jnp.where(lane == h, dq_ref[h], 0.0)
        dc_ref[...] = dc
        up = _tri(cb, True)
        carry = jnp.zeros((1, LANES), F32)
        db = jnp.zeros((1, LANES), F32)
        for n in reversed(range(S // cb)):
            blk = jnp.dot(up, dc_ref[n * cb:(n + 1) * cb, :], precision=lax.Precision.HIGHEST,
                          preferred_element_type=F32) + carry
            carry = blk[0:1, :]
            z = fg_ref[n * cb:(n + 1) * cb, :] + b_ref[...]
            dfg = blk * (1.0 / (1.0 + jnp.exp(z)))
            dfg_ref[n * cb:(n + 1) * cb, :] = dfg.astype(BF16)
            db = db + jnp.sum(dfg, axis=0, keepdims=True)
        db_ref[...] = jnp.broadcast_to(db, db_ref.shape)

    return pl.pallas_call(
        body, name="fox_gate_bwd",
        out_shape=[jax.ShapeDtypeStruct((S, LANES), BF16), jax.ShapeDtypeStruct((8, LANES), F32)],
        scratch_shapes=[pltpu.VMEM((S, LANES), F32)],
        compiler_params=pltpu.CompilerParams(vmem_limit_bytes=_vmem((8 + 2 * n_fox) * S * LANES * 4)),
    )(dcum_k, dcum_q, fg, b_pad)


FOX_TILE = 512


LOG2E = 1.4426950408889634


def _fox_scores(q, k, ck2, masked, t):
    s = lax.dot_general(q, k, _NT, preferred_element_type=F32) * (HEAD_DIM ** -0.5 * LOG2E) - ck2
    if masked:
        row = lax.broadcasted_iota(jnp.int32, (t, t), 0)
        col = lax.broadcasted_iota(jnp.int32, (t, t), 1)
        s = jnp.where(col <= row, s, NEG)
    return s


def _fox_fwd(proj, cum_row, n_fox):
    S = proj.shape[0]
    t = _fit(FOX_TILE, S)
    nq = S // t

    def body(q_ref, k_ref, v_ref, ck_ref, o_ref, lse_ref):
        def q_block(qi, _):
            q0 = pl.multiple_of(qi * t, t)
            q = q_ref[pl.ds(q0, t), :]

            def kv_block(j, carry, masked):
                m, l, acc = carry
                k0 = pl.multiple_of(j * t, t)
                s = _fox_scores(q, k_ref[pl.ds(k0, t), :], ck_ref[0, :, pl.ds(k0, t)] * LOG2E, masked, t)
                m_new = jnp.maximum(m, jnp.max(s, axis=-1, keepdims=True))
                alpha = jnp.exp2(m - m_new)
                p = jnp.exp2(s - m_new)
                l = alpha * l + jnp.sum(p, axis=-1, keepdims=True)
                acc = alpha * acc + jnp.dot(p.astype(BF16), v_ref[pl.ds(k0, t), :], preferred_element_type=F32)
                return m_new, l, acc

            init = (jnp.full((t, 1), NEG, F32), jnp.zeros((t, 1), F32), jnp.zeros((t, HEAD_DIM), F32))
            carry = lax.fori_loop(0, qi, lambda j, cr: kv_block(j, cr, False), init)
            m, l, acc = kv_block(qi, carry, True)
            o_ref[pl.ds(q0, t), :] = acc / l
            lse_ref[0, pl.ds(q0, t), :] = jnp.broadcast_to(m + jnp.log(l) * LOG2E, (t, LANES))
            return 0

        lax.fori_loop(0, nq, q_block, 0)

    col = lambda off: pl.BlockSpec((S, HEAD_DIM), lambda h: (0, off + h))
    per_head = pl.BlockSpec((1, S, LANES), lambda h: (h, 0, 0))
    return pl.pallas_call(
        body, name="fox_fwd", grid=(n_fox,),
        in_specs=[col(0), col(n_fox), col(2 * n_fox), pl.BlockSpec((1, 1, S), lambda h: (h, 0, 0))],
        out_specs=[pl.BlockSpec((S, HEAD_DIM), lambda h: (0, h)), per_head],
        out_shape=[jax.ShapeDtypeStruct((S, n_fox * HEAD_DIM), F32), jax.ShapeDtypeStruct((n_fox, S, LANES), F32)],
        compiler_params=pltpu.CompilerParams(dimension_semantics=("parallel",),
                                             vmem_limit_bytes=_vmem(16 * S * HEAD_DIM * 4 + 12 * t * t * 4)),
    )(proj, proj, proj, cum_row)


def _fox_bwd(proj, o, do, lse_b, cum_row, n_fox):
    S = proj.shape[0]
    t = _fit(FOX_TILE, S)
    nq = S // t
    scale = HEAD_DIM ** -0.5

    def body(q_ref, k_ref, v_ref, o_ref, do_ref, lse_ref, ck_ref, dq_ref, dk_ref, dv_ref, dc_ref, dcq_ref,
             dq_acc, delta_ref):
        dq_acc[...] = jnp.zeros_like(dq_acc)
        dcq_ref[...] = jnp.zeros_like(dcq_ref)

        def delta_block(qi, _):
            q0 = pl.multiple_of(qi * t, t)
            d = jnp.sum(do_ref[pl.ds(q0, t), :] * o_ref[pl.ds(q0, t), :], axis=-1, keepdims=True)
            delta_ref[pl.ds(q0, t), :] = jnp.broadcast_to(d, (t, LANES))
            return 0

        lax.fori_loop(0, nq, delta_block, 0)

        def kv_block(j, _):
            k0 = pl.multiple_of(j * t, t)
            k = k_ref[pl.ds(k0, t), :]
            v = v_ref[pl.ds(k0, t), :]
            ck2 = ck_ref[0, :, pl.ds(k0, t)] * LOG2E

            def q_block(qi, carry, masked):
                dk, dv, dc = carry
                q0 = pl.multiple_of(qi * t, t)
                q = q_ref[pl.ds(q0, t), :]
                dov = do_ref[pl.ds(q0, t), :].astype(BF16)
                p = jnp.exp2(_fox_scores(q, k, ck2, masked, t) - lse_ref[0, pl.ds(q0, t), :][:, :1])
                dp = lax.dot_general(dov, v, _NT, preferred_element_type=F32)
                ds = p * (dp - delta_ref[pl.ds(q0, t), :][:, :1])
                dsb = ds.astype(BF16)
                dv = dv + lax.dot_general(p.astype(BF16), dov, _TN, preferred_element_type=F32)
                dk = dk + lax.dot_general(dsb, q, _TN, preferred_element_type=F32)
                dq_acc[pl.ds(q0, t), :] += jnp.dot(dsb, k, preferred_element_type=F32)
                dc = dc - jnp.sum(ds, axis=0, keepdims=True)
                dcq_ref[0, pl.ds(q0, t), :] += jnp.broadcast_to(jnp.sum(ds, axis=1, keepdims=True), (t, LANES))
                return dk, dv, dc

            init = (jnp.zeros((t, HEAD_DIM), F32), jnp.zeros((t, HEAD_DIM), F32), jnp.zeros((1, t), F32))
            carry = q_block(j, init, True)
            dk, dv, dc = lax.fori_loop(j + 1, nq, lambda qi, cr: q_block(qi, cr, False), carry)
            dk_ref[pl.ds(k0, t), :] = (dk * scale).astype(BF16)
            dv_ref[pl.ds(k0, t), :] = dv.astype(BF16)
            dc_ref[0, :, pl.ds(k0, t)] = dc
            return 0

        lax.fori_loop(0, nq, kv_block, 0)
        dq_ref[...] = (dq_acc[...] * scale).astype(BF16)

    col = lambda off: pl.BlockSpec((S, HEAD_DIM), lambda h: (0, off + h))
    per_head = pl.BlockSpec((1, S, LANES), lambda h: (h, 0, 0))
    row = pl.BlockSpec((1, 1, S), lambda h: (h, 0, 0))
    grad = jax.ShapeDtypeStruct((S, n_fox * HEAD_DIM), BF16)
    return pl.pallas_call(
        body, name="fox_bwd", grid=(n_fox,),
        in_specs=[col(0), col(n_fox), col(2 * n_fox), col(0), col(0), per_head, row],
        out_specs=[col(0), col(0), col(0), row, per_head],
        out_shape=[grad, grad, grad, jax.ShapeDtypeStruct((n_fox, 1, S), F32), jax.ShapeDtypeStruct((n_fox, S, LANES), F32)],
        scratch_shapes=[pltpu.VMEM((S, HEAD_DIM), F32), pltpu.VMEM((S, LANES), F32)],
        compiler_params=pltpu.CompilerParams(dimension_semantics=("parallel",),
                                             vmem_limit_bytes=_vmem(24 * S * HEAD_DIM * 4 + 16 * t * t * 4)),
    )(proj, proj, proj, o, do, lse_b, cum_row)


def _rope_tables(S):
    half = HEAD_DIM // 2
    inv_freq = 1.0 / (ROPE_THETA ** (jnp.arange(half, dtype=F32) * (2.0 / HEAD_DIM)))
    ang = jnp.arange(S).astype(F32)[:, None] * inv_freq[None, :]
    cos, sin = jnp.cos(ang), jnp.sin(ang)
    return jnp.concatenate([cos, cos], axis=-1), jnp.concatenate([-sin, sin], axis=-1)


def _rope(name, src, first_block, n_blocks, cos, sin_signed):
    S = src.shape[0]

    def body(x_ref, cos_ref, sin_ref, o_ref):
        xv = x_ref[...].astype(F32)
        o_ref[...] = (xv * cos_ref[...] + pltpu.roll(xv, HEAD_DIM // 2, 1) * sin_ref[...]).astype(BF16)

    table = pl.BlockSpec((S, HEAD_DIM), lambda n: (0, 0))
    return pl.pallas_call(
        body, name=name, grid=(n_blocks,),
        in_specs=[pl.BlockSpec((S, HEAD_DIM), lambda n: (0, first_block + n)), table, table],
        out_specs=pl.BlockSpec((S, HEAD_DIM), lambda n: (0, n)),
        out_shape=jax.ShapeDtypeStruct((S, n_blocks * HEAD_DIM), BF16),
        compiler_params=pltpu.CompilerParams(dimension_semantics=("parallel",),
                                             vmem_limit_bytes=_vmem(12 * S * HEAD_DIM * 4)),
    )(src, cos, sin_signed)


def _swa_tile(q_ref, kp_ref, kc_ref, n, group, scale):
    B = SWA_BLOCK
    qs = jnp.concatenate([q_ref[:, g * HEAD_DIM:(g + 1) * HEAD_DIM] for g in range(group)], axis=0)
    kcat = jnp.concatenate([kp_ref[...], kc_ref[...]], axis=0)
    s = lax.dot_general(qs, kcat, _NT, preferred_element_type=F32) * scale
    qi = lax.broadcasted_iota(jnp.int32, (group * B, 2 * B), 0) % B
    kj = lax.broadcasted_iota(jnp.int32, (group * B, 2 * B), 1)
    diff = qi + B - kj
    mask = (diff >= 0) & (diff < B) & ((n * B + kj - B) >= 0)
    return qs, kcat, jnp.where(mask, s, NEG)


def _swa_sink_col(sink_ref, kv, group):
    head = lax.broadcasted_iota(jnp.int32, (group * SWA_BLOCK, 1), 0) // SWA_BLOCK
    col = jnp.zeros((group * SWA_BLOCK, 1), F32)
    for g in range(group):
        col = jnp.where(head == g, sink_ref[kv * group + g], col)
    return col


def _swa_specs(n_kv, group, q_first, k_first, v_first):
    B = SWA_BLOCK
    prev = lambda n: jnp.maximum(n - 1, 0)
    return [
        pl.BlockSpec((B, group * HEAD_DIM), lambda kv, n: (n, q_first + kv)),
        pl.BlockSpec((B, HEAD_DIM), lambda kv, n: (prev(n), k_first + kv)),
        pl.BlockSpec((B, HEAD_DIM), lambda kv, n: (n, k_first + kv)),
        pl.BlockSpec((B, HEAD_DIM), lambda kv, n: (prev(n), v_first + kv)),
        pl.BlockSpec((B, HEAD_DIM), lambda kv, n: (n, v_first + kv)),
    ]


def _swa_fwd(rq, proj, v_first, sinks, n_q, n_kv):
    S = rq.shape[0]
    B = SWA_BLOCK
    group = n_q // n_kv
    scale = HEAD_DIM ** -0.5

    def body(q_ref, kp_ref, kc_ref, vp_ref, vc_ref, sink_ref, o_ref, lse_ref):
        kv, n = pl.program_id(0), pl.program_id(1)
        _, _, s = _swa_tile(q_ref, kp_ref, kc_ref, n, group, scale)
        sink = _swa_sink_col(sink_ref, kv, group)
        m = jnp.maximum(jnp.max(s, axis=-1, keepdims=True), sink)
        p = jnp.exp(s - m)
        denom = jnp.sum(p, axis=-1, keepdims=True) + jnp.exp(sink - m)
        vcat = jnp.concatenate([vp_ref[...], vc_ref[...]], axis=0)
        o = jnp.dot((p / denom).astype(BF16), vcat, preferred_element_type=F32)
        lse = m + jnp.log(denom)
        for g in range(group):
            o_ref[:, g * HEAD_DIM:(g + 1) * HEAD_DIM] = o[g * B:(g + 1) * B, :]
            lse_ref[0, :, g * LANES:(g + 1) * LANES] = jnp.broadcast_to(lse[g * B:(g + 1) * B, :], (B, LANES))

    specs = _swa_specs(n_kv, group, 0, n_q, v_first)
    q_blk = pl.BlockSpec((B, group * HEAD_DIM), lambda kv, n: (n, kv))
    return pl.pallas_call(
        body, name="swa_fwd", grid=(n_kv, S // B),
        in_specs=specs + [pl.BlockSpec(memory_space=pltpu.SMEM)],
        out_specs=[q_blk, pl.BlockSpec((1, B, group * LANES), lambda kv, n: (kv, n, 0))],
        out_shape=[jax.ShapeDtypeStruct((S, n_q * HEAD_DIM), F32), jax.ShapeDtypeStruct((n_kv, S, group * LANES), F32)],
        compiler_params=pltpu.CompilerParams(dimension_semantics=("parallel", "arbitrary")),
    )(rq, rq, rq, proj, proj, sinks)


def _swa_bwd(rq, proj, v_first, sinks, o, do, do_first, lse_b, n_q, n_kv):
    S = rq.shape[0]
    B = SWA_BLOCK
    group = n_q // n_kv
    scale = HEAD_DIM ** -0.5

    def body(q_ref, kp_ref, kc_ref, vp_ref, vc_ref, o_ref, do_ref, lse_ref, sink_ref,
             dq_ref, dk_ref, dv_ref, dsink_ref):
        kv, n = pl.program_id(0), pl.program_id(1)

        @pl.when(n == 0)
        def _():
            dk_ref[...] = jnp.zeros_like(dk_ref)
            dv_ref[...] = jnp.zeros_like(dv_ref)
            dsink_ref[...] = jnp.zeros_like(dsink_ref)

        qs, kcat, s = _swa_tile(q_ref, kp_ref, kc_ref, n, group, scale)
        sink = _swa_sink_col(sink_ref, kv, group)
        stack = lambda ref, w: jnp.concatenate([ref[:, g * w:(g + 1) * w] for g in range(group)], axis=0)
        lse = jnp.concatenate([lse_ref[0, :, g * LANES:g * LANES + 1] for g in range(group)], axis=0)
        do32 = stack(do_ref, HEAD_DIM)
        delta = jnp.sum(do32 * stack(o_ref, HEAD_DIM), axis=-1, keepdims=True)
        dov = do32.astype(BF16)
        p = jnp.exp(s - lse)
        vcat = jnp.concatenate([vp_ref[...], vc_ref[...]], axis=0)
        dp = lax.dot_general(dov, vcat, _NT, preferred_element_type=F32)
        ds = p * (dp - delta)
        dsb = ds.astype(BF16)
        dq = jnp.dot(dsb, kcat, preferred_element_type=F32) * scale
        for g in range(group):
            dq_ref[:, g * HEAD_DIM:(g + 1) * HEAD_DIM] = dq[g * B:(g + 1) * B, :].astype(BF16)
        dkcat = lax.dot_general(dsb, qs, _TN, preferred_element_type=F32) * scale
        dvcat = lax.dot_general(p.astype(BF16), dov, _TN, preferred_element_type=F32)
        prev0 = pl.multiple_of(jnp.maximum(n - 1, 0) * B, B)
        cur0 = pl.multiple_of(n * B, B)
        dk_ref[0, pl.ds(prev0, B), :] += dkcat[:B, :]
        dk_ref[0, pl.ds(cur0, B), :] += dkcat[B:, :]
        dv_ref[0, pl.ds(prev0, B), :] += dvcat[:B, :]
        dv_ref[0, pl.ds(cur0, B), :] += dvcat[B:, :]
        dsk = -jnp.exp(sink - lse) * delta
        lane = lax.broadcasted_iota(jnp.int32, (1, LANES), 1)
        row = jnp.zeros((1, LANES), F32)
        for g in range(group):
            row = row + jnp.where(lane == g, jnp.sum(dsk[g * B:(g + 1) * B, :]), 0.0)
        dsink_ref[0, 0:1, :] += row

    specs = _swa_specs(n_kv, group, 0, n_q, v_first)
    q_blk = pl.BlockSpec((B, group * HEAD_DIM), lambda kv, n: (n, kv))
    acc = pl.BlockSpec((1, S, HEAD_DIM), lambda kv, n: (kv, 0, 0))
    return pl.pallas_call(
        body, name="swa_bwd", grid=(n_kv, S // B),
        in_specs=specs + [q_blk, pl.BlockSpec((B, group * HEAD_DIM), lambda kv, n: (n, do_first + kv)),
                          pl.BlockSpec((1, B, group * LANES), lambda kv, n: (kv, n, 0)),
                          pl.BlockSpec(memory_space=pltpu.SMEM)],
        out_specs=[q_blk, acc, acc, pl.BlockSpec((1, 8, LANES), lambda kv, n: (kv, 0, 0))],
        out_shape=[jax.ShapeDtypeStruct((S, n_q * HEAD_DIM), BF16), jax.ShapeDtypeStruct((n_kv, S, HEAD_DIM), F32),
                   jax.ShapeDtypeStruct((n_kv, S, HEAD_DIM), F32), jax.ShapeDtypeStruct((n_kv, 8, LANES), F32)],
        compiler_params=pltpu.CompilerParams(dimension_semantics=("parallel", "arbitrary")),
    )(rq, rq, rq, proj, proj, o, do, lse_b, sinks)


def _adamw(w, g, m, v):
    m = ADAM_B1 * m + (1.0 - ADAM_B1) * g
    v = ADAM_B2 * v + (1.0 - ADAM_B2) * (g * g)
    m_hat = m / (1.0 - ADAM_B1 ** ADAM_STEP)
    v_hat = v / (1.0 - ADAM_B2 ** ADAM_STEP)
    delta = -ADAM_LR * (m_hat / (jnp.sqrt(v_hat) + ADAM_EPS) + ADAM_WD * w)
    return delta, m, v


def _mod_fwd(cond_in, w_mod, b_shard):
    R, D = cond_in.shape
    cols = w_mod.shape[1]
    tn = _fit(512, cols)

    def body(c_ref, w_ref, b_ref, o_ref):
        cv = c_ref[...]
        cond = (cv / (1.0 + jnp.exp(-cv))).astype(BF16)
        o_ref[...] = jnp.dot(cond, w_ref[...].astype(BF16), preferred_element_type=F32) + b_ref[...]

    return pl.pallas_call(
        body, name="mod_fwd", grid=(cols // tn,),
        in_specs=[pl.BlockSpec((R, D), lambda j: (0, 0)), pl.BlockSpec((D, tn), lambda j: (0, j)),
                  pl.BlockSpec((1, tn), lambda j: (0, j))],
        out_specs=pl.BlockSpec((R, tn), lambda j: (0, j)),
        out_shape=jax.ShapeDtypeStruct((R, cols), F32),
        compiler_params=pltpu.CompilerParams(dimension_semantics=("parallel",), vmem_limit_bytes=_vmem(3 * D * tn * 4)),
    )(cond_in, w_mod, b_shard)


def _mod_update(c_t, dmod, w, m, v):
    D, nb = c_t.shape
    cols = w.shape[1]
    tn = _fit(256, cols)

    def body(c_ref, d_ref, w_ref, m_ref, v_ref, g_ref, dl_ref, nm_ref, nv_ref):
        cv = c_ref[...]
        cond = cv / (1.0 + jnp.exp(-cv))
        g = jnp.zeros((D, tn), F32)
        for b in range(nb):
            g = g + cond[:, b:b + 1] * d_ref[b:b + 1, :]
        g_ref[...] = g
        dl_ref[...], nm_ref[...], nv_ref[...] = _adamw(w_ref[...], g, m_ref[...], v_ref[...])

    blk = pl.BlockSpec((D, tn), lambda j: (0, j))
    out = jax.ShapeDtypeStruct((D, cols), F32)
    return pl.pallas_call(
        body, name="mod_update", grid=(cols // tn,),
        in_specs=[pl.BlockSpec((D, nb), lambda j: (0, 0)), pl.BlockSpec((nb, tn), lambda j: (0, j)), blk, blk, blk],
        out_specs=[blk] * 4, out_shape=[out] * 4,
        compiler_params=pltpu.CompilerParams(dimension_semantics=("parallel",), vmem_limit_bytes=_vmem(18 * D * tn * 4)),
    )(c_t, dmod, w, m, v)


def _small_update(stacked, w, m, v):
    R, C = w.shape

    def body(s_ref, w_ref, m_ref, v_ref, g_ref, dl_ref, nm_ref, nv_ref):
        g = s_ref[0:R, :]
        for d in range(1, N_DEV):
            g = g + s_ref[d * R:(d + 1) * R, :]
        g_ref[...] = g
        dl_ref[...], nm_ref[...], nv_ref[...] = _adamw(w_ref[...], g, m_ref[...], v_ref[...])

    return pl.pallas_call(body, name="small_update", out_shape=[jax.ShapeDtypeStruct((R, C), F32)] * 4)(stacked, w, m, v)


def _place():
    return lax.axis_index("x"), lax.axis_index("y"), lax.axis_index("c")


def _allgather8(name, block):
    m_per, n = block.shape

    def body(x_ref, out_ref, token_ref, send_sems, recv_sems, local_sem):
        token_ref[...] = jnp.zeros_like(token_ref)
        x, y, c = _place()
        me, sibling = (x, y, c), (x, y, 1 - c)
        chips = [(1 - x, y), (x, 1 - y), (1 - x, 1 - y)]

        def rows(px, py, pc):
            return out_ref.at[pl.ds((4 * px + 2 * py + pc) * m_per, m_per), :]

        def copy(k, blk, to, src=None):
            return pltpu.make_async_remote_copy(
                src_ref=rows(*blk) if src is None else src, dst_ref=rows(*blk),
                send_sem=send_sems.at[k], recv_sem=recv_sems.at[k], device_id=to, device_id_type=MESH)

        mine = pltpu.make_async_copy(x_ref, rows(*me), local_sem)
        mine.start()
        first = [copy(0, me, sibling, src=x_ref)]
        first += [copy(1 + j, me, (*chip, c), src=x_ref) for j, chip in enumerate(chips)]
        for cp in first:
            cp.start()
        passed = [copy(4 + j, (*chip, c), sibling) for j, chip in enumerate(chips)]
        for j, chip in enumerate(chips):
            copy(1 + j, (*chip, c), me).wait_recv()
            passed[j].start()
        copy(0, sibling, me).wait_recv()
        for j, chip in enumerate(chips):
            copy(4 + j, (*chip, 1 - c), me).wait_recv()
        for cp in first + passed:
            cp.wait_send()
        mine.wait()

    vmem = pl.BlockSpec(memory_space=pltpu.VMEM)
    return pl.pallas_call(
        body, name=name,
        out_shape=[jax.ShapeDtypeStruct((N_DEV * m_per, n), block.dtype), jax.ShapeDtypeStruct((8, LANES), F32)],
        in_specs=[vmem], out_specs=[vmem, vmem],
        scratch_shapes=[pltpu.SemaphoreType.DMA((7,)), pltpu.SemaphoreType.DMA((7,)), pltpu.SemaphoreType.DMA],
    )(block)


_ANY = pl.BlockSpec(memory_space=pl.ANY)


def _half(ref, c, rows):
    return ref.at[pl.ds(c * (rows // 2), rows // 2), :]


_HBM = pl.BlockSpec(memory_space=pltpu.HBM)
_SEM = pl.BlockSpec(memory_space=pltpu.SEMAPHORE)
_EFFECT = pltpu.SideEffectType.DATAFLOW_SIDE_EFFECTING


def _ici_start(name, srcs, land_shapes, plan, per_source=3, after=None):
    ns, nl = len(srcs), len(land_shapes)
    n_copies = per_source * ns
    n_in = ns + nl + (after is not None)

    def body(*refs):
        src_refs, land_refs = refs[:ns], refs[ns:ns + nl]
        send_sems, recv_sems = refs[n_in], refs[n_in + 1]
        token = refs[-1]
        for n, (src, dst, peer, _) in enumerate(plan(src_refs, land_refs)):
            pltpu.make_async_remote_copy(src_ref=src, dst_ref=dst, send_sem=send_sems.at[n], recv_sem=recv_sems.at[n],
                                         device_id=peer, device_id_type=MESH).start()
        token[...] = jnp.zeros_like(token)

    lands = [lax.empty(s.shape, s.dtype) for s in land_shapes]
    out = pl.pallas_call(
        body, name=name,
        out_shape=(pltpu.SemaphoreType.DMA((n_copies,)), pltpu.SemaphoreType.DMA((n_copies,)),
                   *[pltpu.HBM(a.shape, a.dtype) for a in list(srcs) + lands], jax.ShapeDtypeStruct((8, LANES), F32)),
        in_specs=[_HBM] * (ns + nl) + [_ANY] * (after is not None),
        out_specs=(_SEM, _SEM, *[_HBM] * (ns + nl), pl.BlockSpec(memory_space=pltpu.VMEM)),
        input_output_aliases={n: 2 + n for n in range(ns + nl)},
        compiler_params=pltpu.CompilerParams(has_side_effects=_EFFECT),
    )(*[pltpu.with_memory_space_constraint(a, pltpu.HBM) for a in list(srcs) + lands],
      *([] if after is None else [after]))
    return out[0], out[1], list(out[2:2 + ns]), list(out[2 + ns:2 + ns + nl]), out[-1]


def _ici_wait(name, send_sems, recv_sems, srcs, lands, plan, after):
    ns, nl = len(srcs), len(lands)
    after = list(after) if isinstance(after, (list, tuple)) else [after]

    def body(*refs):
        src_refs, land_refs = refs[:ns], refs[ns:ns + nl]
        send_sems, recv_sems = refs[ns + nl], refs[ns + nl + 1]
        for n, (src, _, peer, mine) in enumerate(plan(src_refs, land_refs)):
            cp = pltpu.make_async_remote_copy(src_ref=src, dst_ref=mine, send_sem=send_sems.at[n],
                                              recv_sem=recv_sems.at[n], device_id=peer, device_id_type=MESH)
            cp.wait_send()
            cp.wait_recv()

    out = pl.pallas_call(
        body, name=name, out_shape=[pltpu.HBM(a.shape, a.dtype) for a in list(srcs) + list(lands)],
        in_specs=[_HBM] * (ns + nl) + [_SEM, _SEM] + [_ANY] * len(after), out_specs=[_HBM] * (ns + nl),
        input_output_aliases={n: n for n in range(ns + nl)},
        compiler_params=pltpu.CompilerParams(has_side_effects=_EFFECT),
    )(*srcs, *lands, send_sems, recv_sems, *after)
    return list(out[:ns]), list(out[ns:])


def _own_slab(name, chip, w, after):
    R, C = w.shape
    tr, tc = _tiles(R, C)
    tied = [] if after is None else [after]

    def body(chip_ref, w_ref, *rest):
        stack_ref, token_ref = rest[-2:]
        stack_ref[0] = w_ref[...].astype(BF16)
        token_ref[...] = jnp.zeros_like(token_ref)

    small = pl.BlockSpec((8, LANES), lambda r, q, chip_ref: (0, 0))
    grid_spec = pltpu.PrefetchScalarGridSpec(
        num_scalar_prefetch=1, grid=(R // tr, C // tc),
        in_specs=[pl.BlockSpec((tr, tc), lambda r, q, chip_ref: (r, q))] + [small] * len(tied),
        out_specs=[pl.BlockSpec((1, tr, tc), lambda r, q, chip_ref: (chip_ref[0], r, q)), small])
    return pl.pallas_call(
        body, name=name, grid_spec=grid_spec,
        out_shape=[jax.ShapeDtypeStruct((N_CHIPS, R, C), BF16), jax.ShapeDtypeStruct((8, LANES), F32)],
        compiler_params=pltpu.CompilerParams(dimension_semantics=("arbitrary", "arbitrary")),
    )(chip, w, *tied)


def _gather_plan(src_refs, land_refs):
    x, y, c = _place()
    copies = []
    for stack in src_refs:
        R = stack.shape[1]
        own = _half(stack.at[2 * x + y], c, R)
        for cx, cy in [(1 - x, y), (x, 1 - y), (1 - x, 1 - y)]:
            copies.append((own, own, (cx, cy, c), _half(stack.at[2 * cx + cy], c, R)))
    return copies


def _pass_plan(src_refs, land_refs):
    x, y, c = _place()
    copies = []
    for land in src_refs:
        R = land.shape[1]
        for cx, cy in [(1 - x, y), (x, 1 - y), (1 - x, 1 - y)]:
            slot = land.at[2 * cx + cy]
            copies.append((_half(slot, c, R), _half(slot, c, R), (x, y, 1 - c), _half(slot, 1 - c, R)))
    return copies


def _share_plan(src_refs, land_refs):
    x, y, c = _place()
    return [(h, land, (x, y, 1 - c), land) for h, land in zip(src_refs, land_refs)]


def _pass_to_sibling(name, lands):
    nw = len(lands)

    def body(*refs):
        ins, outs = refs[:nw], refs[nw:2 * nw]
        send_sems, recv_sems = refs[2 * nw:]
        x, y, c = _place()
        chips = [(1 - x, y), (x, 1 - y), (1 - x, 1 - y)]
        copies = []
        for k in range(nw):
            R = ins[k].shape[1]
            for j, (cx, cy) in enumerate(chips):
                cp = pltpu.make_async_remote_copy(
                    src_ref=_half(ins[k].at[2 * cx + cy], c, R), dst_ref=_half(outs[k].at[2 * cx + cy], c, R),
                    send_sem=send_sems.at[3 * k + j], recv_sem=recv_sems.at[3 * k + j],
                    device_id=(x, y, 1 - c), device_id_type=MESH)
                cp.start()
                copies.append(cp)
        for k in range(nw):
            R = ins[k].shape[1]
            for j, (cx, cy) in enumerate(chips):
                pltpu.make_async_remote_copy(
                    src_ref=_half(ins[k].at[2 * cx + cy], c, R), dst_ref=_half(outs[k].at[2 * cx + cy], 1 - c, R),
                    send_sem=send_sems.at[3 * k + j], recv_sem=recv_sems.at[3 * k + j],
                    device_id=(x, y, 1 - c), device_id_type=MESH).wait_recv()
        for cp in copies:
            cp.wait_send()

    return pl.pallas_call(
        body, name=name, out_shape=[jax.ShapeDtypeStruct(a.shape, a.dtype) for a in lands],
        in_specs=[_ANY] * nw, out_specs=[_ANY] * nw, input_output_aliases={k: k for k in range(nw)},
        scratch_shapes=[pltpu.SemaphoreType.DMA((3 * nw,)), pltpu.SemaphoreType.DMA((3 * nw,))],
    )(*lands)


def _tie(vec, token):
    return vec + token[0:1, 0:1]


ROW_ALIGN = 16
TILE_ELEMS = 512 * 1024


def _tiles(rows, cols):
    fits = [t for t in range(ROW_ALIGN, min(rows, 256) + 1, ROW_ALIGN) if rows % t == 0]
    tr = fits[-1] if fits and fits[-1] >= 64 else rows
    tc = cols
    while tr * tc > TILE_ELEMS and tc % (2 * LANES) == 0:
        tc //= 2
    return tr, tc


def _scatter_plan(src_refs, land_refs):
    x, y, c = _place()
    copies = []
    for p, land in zip(src_refs, land_refs):
        for j, (cx, cy) in enumerate([(1 - x, y), (x, 1 - y), (1 - x, 1 - y)]):
            copies.append((p.at[2 * cx + cy], land.at[j], (cx, cy, c), land.at[j]))
    return copies


def _chip_add(name, chip, sums, recv):
    _, H, C = sums.shape
    tr, tc = _tiles(H, C)

    def body(chip_ref, p_ref, r_ref, o_ref):
        total = p_ref[0].astype(F32)
        for j in range(3):
            total = total + r_ref[j].astype(F32)
        o_ref[...] = total

    grid_spec = pltpu.PrefetchScalarGridSpec(
        num_scalar_prefetch=1, grid=(H // tr, C // tc),
        in_specs=[pl.BlockSpec((1, tr, tc), lambda r, q, chip_ref: (chip_ref[0], r, q)),
                  pl.BlockSpec((3, tr, tc), lambda r, q, chip_ref: (0, r, q))],
        out_specs=pl.BlockSpec((tr, tc), lambda r, q, chip_ref: (r, q)))
    return pl.pallas_call(
        body, name=name, grid_spec=grid_spec, out_shape=jax.ShapeDtypeStruct((H, C), F32),
        compiler_params=pltpu.CompilerParams(dimension_semantics=("parallel", "parallel")),
    )(chip, sums, recv)


def _adam_halves(name, core, w, g_own, g_other, m, v):
    R, C = w.shape
    H = R // 2
    tr, tc = _tiles(H, C)
    nr, nc = H // tr, C // tc

    def body(core_ref, w_ref, go_ref, gr_ref, m_ref, v_ref, g_ref, dl_ref, nm_ref, nv_ref):
        own = (pl.program_id(0) // nr) == core_ref[0]
        g = jnp.where(own, go_ref[...], gr_ref[...])
        g_ref[...] = g
        dl_ref[...], nm_ref[...], nv_ref[...] = _adamw(w_ref[...], g, m_ref[...], v_ref[...])

    blk = pl.BlockSpec((tr, tc), lambda r, q, core_ref: (r, q))

    def half_spec(is_own):
        def index(r, q, core_ref):
            mine = ((r // nr) == core_ref[0]) == is_own
            done = is_own == (core_ref[0] == 0)
            return (jnp.where(mine, r % nr, jnp.where(done, nr - 1, 0)), jnp.where(mine, q, jnp.where(done, nc - 1, 0)))
        return pl.BlockSpec((tr, tc), index)
    out = jax.ShapeDtypeStruct((R, C), F32)
    grid_spec = pltpu.PrefetchScalarGridSpec(
        num_scalar_prefetch=1, grid=(R // tr, nc), in_specs=[blk, half_spec(True), half_spec(False), blk, blk],
        out_specs=[blk] * 4)
    return pl.pallas_call(
        body, name=name, grid_spec=grid_spec, out_shape=[out] * 4,
        compiler_params=pltpu.CompilerParams(dimension_semantics=("parallel", "parallel"),
                                             vmem_limit_bytes=_vmem(20 * tr * tc * 4)),
    )(core, w, g_own, g_other, m, v)


def _adam_one_half(name, core, w, g_half, m, v, own, started=None):
    R, C = w.shape
    H = R // 2
    tr, tc = _tiles(H, C)
    nr = H // tr

    def body(core_ref, w_ref, g_ref, m_ref, v_ref, *rest):
        go_ref, dl_ref, nm_ref, nv_ref = rest[-4:]
        g = g_ref[...]
        go_ref[...] = g
        dl_ref[...], nm_ref[...], nv_ref[...] = _adamw(w_ref[...], g, m_ref[...], v_ref[...])

    def row(r, core_ref):
        return (core_ref[0] if own else 1 - core_ref[0]) * nr + r

    blk = pl.BlockSpec((tr, tc), lambda r, q, core_ref: (row(r, core_ref), q))
    half = pl.BlockSpec((tr, tc), lambda r, q, core_ref: (r, q))
    prior = [] if started is None else list(started)
    grid_spec = pltpu.PrefetchScalarGridSpec(
        num_scalar_prefetch=1, grid=(nr, C // tc),
        in_specs=[blk, half, blk, blk] + [_ANY] * len(prior), out_specs=[blk] * 4)
    return pl.pallas_call(
        body, name=name, grid_spec=grid_spec, out_shape=[jax.ShapeDtypeStruct((R, C), F32)] * 4,
        input_output_aliases={5 + n: n for n in range(len(prior))},
        compiler_params=pltpu.CompilerParams(dimension_semantics=("parallel", "parallel"),
                                             vmem_limit_bytes=_vmem(20 * tr * tc * 4)),
    )(core, w, g_half, m, v, *prior)


def kernel(x, c, w_mod, b_mod, g_pre_mix, g_post_mix, w_in, b_forget, swa_sinks, w_out, g_pre_mlp, g_post_mlp, w_up, w_down, loss_target, m_w_mod, m_b_mod, m_g_pre_mix, m_g_post_mix, m_w_in, m_b_forget, m_swa_sinks, m_w_out, m_g_pre_mlp, m_g_post_mlp, m_w_up, m_w_down, v_w_mod, v_b_mod, v_g_pre_mix, v_g_post_mix, v_w_in, v_b_forget, v_swa_sinks, v_w_out, v_g_pre_mlp, v_g_post_mlp, v_w_up, v_w_down):
    S, D = x.shape[1], x.shape[2]
    n_heads = D // HEAD_DIM
    n_fox = n_heads // 2
    n_swa = n_heads - n_fox
    n_kv = max(1, n_swa // 4)
    fox_w, swa_w, kv_w = n_fox * HEAD_DIM, n_swa * HEAD_DIM, n_kv * HEAD_DIM
    main_w = 3 * fox_w + swa_w + 2 * kv_w
    in_w = main_w + n_fox
    mod_cols = w_mod.shape[2]

    ax, ay, ac = _place()
    chip = 2 * ax + ay
    dev = 2 * chip + ac
    chip_arr = jnp.reshape(chip, (1,)).astype(jnp.int32)
    core_arr = jnp.reshape(ac, (1,)).astype(jnp.int32)

    x2, tgt = x[0], loss_target[0]

    in_rows = in_w // N_CHIPS
    in_rows_pad = -(-in_rows // (2 * LANES)) * (2 * LANES)
    slab_w = N_CHIPS * in_rows_pad

    def rows_of(a):
        return jnp.pad(a[0].T, ((0, in_rows_pad - in_rows), (0, 0)))

    w_in_stack, token = _own_slab("own_slab_w_in", chip_arr, rows_of(w_in), None)

    c_all, _ = _allgather8("gather_c", _tie(c, token).reshape(8, D // 8))
    c_all = c_all.reshape(N_DEV, D)
    b_shard = lax.dynamic_slice_in_dim(b_mod, chip * mod_cols, mod_cols, axis=1)
    mod_shard = _mod_fwd(jnp.pad(c_all, ((0, 16 - N_DEV), (0, 0))), w_mod[0], b_shard)[:N_DEV]
    mod_all, token = _allgather8("gather_mod", mod_shard)
    mod_all = mod_all.reshape(N_CHIPS, 2, N_DEV, mod_cols)[:, 0]
    mod = lax.dynamic_index_in_dim(mod_all, dev, axis=1, keepdims=False).reshape(N_MOD, 1, D)
    sh_a, sc_a, gt_a, sh_m, sc_m, gt_m = [mod[n] for n in range(N_MOD)]

    def slab_cols(lo, hi):
        spans = []
        while lo < hi:
            s, r = divmod(lo, in_rows)
            n = min(hi - lo, in_rows - r)
            spans.append((s * in_rows_pad + r, s * in_rows_pad + r + n))
            lo += n
        return spans

    gate_lo = 3 * fox_w
    main_spans = slab_cols(0, gate_lo) + slab_cols(gate_lo + n_fox, in_w)
    (gate_first, gate_last), = slab_cols(gate_lo, gate_lo + n_fox)

    names = ["w_in", "w_out", "w_up", "w_down"]
    flights = {}
    for n, w in zip(names, [None, w_out[0], w_up[0], w_down[0]]):
        stack = w_in_stack if n == "w_in" else _own_slab("own_slab_" + n, chip_arr, w, token)[0]
        flights[n] = _ici_start("gather_start_" + n, [stack], [], _gather_plan, after=token)
        token = flights[n][4]
    sc_a = _tie(sc_a, token)

    def arrived(n, after):
        send, recv, stacks, _, _ = flights[n]
        stacks, _ = _ici_wait("gather_wait_" + n, send, recv, stacks, [], _gather_plan, after)
        return _ici_start("gather_pass_start_" + n, stacks, [], _pass_plan)

    def gathered(n, after, in_flight=None):
        if in_flight is None:
            send, recv, stacks, _, _ = flights[n]
            stacks, _ = _ici_wait("gather_wait_" + n, send, recv, stacks, [], _gather_plan, after)
            return _pass_to_sibling("gather_pass_" + n, stacks)[0]
        send, recv, stacks, _, _ = in_flight
        return _ici_wait("gather_pass_wait_" + n, send, recv, stacks, [], _pass_plan, after)[0][0]

    d_ff = N_CHIPS * w_up.shape[2]

    h = _pre_norm(x2, g_pre_mix, sc_a, sh_a)
    in_state = [rows_of(w_in)] + [rows_of(_tie(a, token)) for a in (m_w_in, v_w_in)]
    cos, sin_signed = _rope_tables(S)

    def pack(bm, gpm, gqm, gpl, gql, bf, sk):
        last = jnp.concatenate([bf, sk, jnp.zeros((1, D - n_fox - n_swa), F32)], axis=1)
        return jnp.concatenate([bm.reshape(N_MOD, D), gpm, gqm, gpl, gql, last, jnp.zeros((5, D), F32)], axis=0)

    small_state = [pack(b_mod, g_pre_mix, g_post_mix, g_pre_mlp, g_post_mlp, b_forget, swa_sinks),
                   pack(m_b_mod, m_g_pre_mix, m_g_post_mix, m_g_pre_mlp, m_g_post_mlp, m_b_forget, m_swa_sinks),
                   pack(v_b_mod, v_g_pre_mix, v_g_post_mix, v_g_pre_mlp, v_g_post_mlp, v_b_forget, v_swa_sinks)]
    ready = h[:8, :LANES].astype(F32) + cos[:8]
    w_slab_t = gathered("w_in", [ready] + in_state[1:] + small_state).reshape(slab_w, D)
    tm_p, tn_p = _fit(MM_TM, S), _fit(MM_TN if slab_w % MM_TN == 0 else MM_TN // 2, slab_w)
    win0 = gate_first // LANES * LANES
    win_j, win_off = divmod(win0, tn_p)
    assert win_off + 2 * LANES <= tn_p and gate_last - win0 <= 2 * LANES

    def proj_epilogue(acc, ex, outs):
        outs[0][...] = acc.astype(BF16)

        @pl.when(pl.program_id(1) == win_j)
        def _():
            outs[1][...] = acc[:, win_off:win_off + 2 * LANES]

    proj_slab, gate_win = _matmul(
        "in_proj", h, w_slab_t, "nt",
        [((S, slab_w), BF16, (tm_p, tn_p), lambda i, j: (i, j)), ((S, 2 * LANES), F32, (tm_p, 2 * LANES), lambda i, j: (i, 0))],
        proj_epilogue, tn=tn_p, revisits=True)
    proj = jnp.concatenate([proj_slab[:, lo:hi] for lo, hi in main_spans], axis=1)
    out_flight = arrived("w_out", proj_slab)
    fg = _tie(jnp.pad(gate_win[:, gate_first - win0:gate_last - win0], ((0, 0), (0, LANES - n_fox))), out_flight[4])
    b_pad = jnp.pad(b_forget, ((0, 0), (0, LANES - n_fox)))
    cum_row = _fox_gate_fwd(fg, b_pad)[:n_fox].reshape(n_fox, 1, S)
    fox_o, fox_lse = _fox_fwd(proj, cum_row, n_fox)

    rq = _rope("rope_fwd", proj, 3 * n_fox, n_swa + n_kv, cos, sin_signed)
    v_first = 3 * n_fox + n_swa + n_kv
    sinks = swa_sinks[0]
    swa_o, swa_lse = _swa_fwd(rq, proj, v_first, sinks, n_swa, n_kv)

    mixcat = jnp.concatenate([fox_o, swa_o], axis=1).astype(BF16)
    up_flight = arrived("w_up", mixcat)
    w_out_f = gathered("w_out", mixcat, out_flight).reshape(D, D)
    mix = _mm_plain("out_proj", mixcat, w_out_f, "nn", BF16, after=up_flight[4])
    x1, h2 = _post_mix(x2, mix, g_post_mix, gt_a, g_pre_mlp, sc_m, sh_m)
    w_up_f = gathered("w_up", h2, up_flight)

    tm_u, tn_u = _fit(MM_TM, S), _fit(MM_TN, d_ff)

    def up_epilogue(acc, ex, outs):
        outs[0][...] = acc.astype(BF16)
        r = jnp.maximum(acc, 0.0)
        outs[1][...] = (r * r).astype(BF16)

    ublk = ((S, d_ff), BF16, (tm_u, tn_u), lambda i, j: (i, j))
    u, a = _matmul("mlp_up", h2, w_up_f, "nn", [ublk, ublk], up_epilogue)
    w_down_f = gathered("w_down", a).reshape(d_ff, D)
    y = _mm_plain("mlp_down", a, w_down_f, "nn", BF16)

    dy, dout, loss_part, acc_mlp_post = _loss_and_post_mlp_bwd(x1, y, tgt, g_post_mlp, gt_m)

    def du_epilogue(acc, ex, outs):
        outs[0][...] = (acc * (2.0 * jnp.maximum(ex[0][...].astype(F32), 0.0))).astype(BF16)

    du = _matmul("mlp_down_bwd", dy, w_down_f, "nt", [ublk], du_epilogue,
                 extras=[(u, (tm_u, tn_u), lambda i, j: (i, j))])[0]
    def pair_send(tag, part):
        return _ici_start("grad_pair_start_" + tag, [part], [jax.ShapeDtypeStruct(part.shape, BF16)], _share_plan,
                          per_source=1)

    def pair_recv(tag, flight, after):
        send, recv, srcs, lands, _ = flight
        return _ici_wait("grad_pair_wait_" + tag, send, recv, srcs, lands, _share_plan, after)[1][0]

    def scatter_start(tag, sums, after=None):
        return _ici_start("grad_scatter_start_" + tag, sums,
                          [jax.ShapeDtypeStruct((3,) + p.shape[1:], BF16) for p in sums], _scatter_plan, after=after)

    def scatter_finish(tag, flight, after):
        send, recv, srcs, lands, _ = flight
        sums, received = _ici_wait("grad_scatter_wait_" + tag, send, recv, srcs, lands, _scatter_plan, after)
        return [_chip_add("chip_add_%s_%d" % (tag, k), chip_arr, p, r) for k, (p, r) in enumerate(zip(sums, received))]

    tm_g = _fit(MM_TM, D // 2)
    pair_down = pair_send("down", _grad_half("grad_w_down_a", core_arr, a, dy, N_CHIPS, 1, tm_g, True))
    pair_up = pair_send("up", _grad_half("grad_w_up_a", core_arr, h2, du, 1, N_CHIPS, tm_g, True, after=pair_down[4]))
    sum_down = _grad_half("grad_w_down_b", core_arr, a, dy, N_CHIPS, 1, tm_g, False,
                          recv=pair_recv("down", pair_down, pair_up[4]))
    sum_up = _grad_half("grad_w_up_b", core_arr, h2, du, 1, N_CHIPS, tm_g, False, recv=pair_recv("up", pair_up, sum_down))
    flight_mlp = scatter_start("mlp", [sum_up, sum_down])
    dh2 = _mm_plain("mlp_up_bwd", du, w_up_f, "nt", BF16, after=flight_mlp[4])
    dx1, dmix, acc_mid = _pre_mlp_and_post_mix_bwd(dh2, x1, dout, mix, _tie(g_pre_mlp, flight_mlp[4]), sc_m,
                                                   g_post_mix, gt_a)

    dmixcat = _mm_plain("out_proj_bwd", dmix, w_out_f, "nt", F32)

    fdq, fdk, fdv, dcum_row, dcum_q = _fox_bwd(proj, fox_o, dmixcat, fox_lse, cum_row, n_fox)
    dcum_k = jnp.pad(dcum_row.reshape(n_fox, S), ((0, LANES - n_fox), (0, 0)))
    dfg, db_forget = _fox_gate_bwd(dcum_k, dcum_q, fg, b_pad)

    group_w = (n_swa // n_kv) * HEAD_DIM
    sdq, sdk, sdv, dsink = _swa_bwd(rq, proj, v_first, sinks, swa_o, dmixcat, fox_w // group_w, swa_lse, n_swa, n_kv)
    drq = jnp.concatenate([sdq, jnp.transpose(sdk, (1, 0, 2)).reshape(S, kv_w).astype(BF16)], axis=1)
    d_sq_sk = _rope("rope_bwd", drq, 0, n_swa + n_kv, cos, -sin_signed)
    dsv = jnp.transpose(sdv, (1, 0, 2)).reshape(S, kv_w).astype(BF16)
    dproj = jnp.concatenate([fdq, fdk, fdv, d_sq_sk, dsv], axis=1)

    pieces = []
    for s in range(N_CHIPS):
        lo, hi = s * in_rows, (s + 1) * in_rows
        for src, first, last, shift in [(dproj, 0, gate_lo, 0), (dfg, gate_lo, gate_lo + n_fox, gate_lo),
                                        (dproj, gate_lo + n_fox, in_w, n_fox)]:
            if max(lo, first) < min(hi, last):
                pieces.append(src[:, max(lo, first) - shift:min(hi, last) - shift])
        pieces.append(jnp.zeros((S, in_rows_pad - in_rows), BF16))
    dproj_slab = jnp.concatenate(pieces, axis=1)

    tm_in, tm_out = in_rows_pad // 2, D // (2 * N_CHIPS)
    pair_in = pair_send("in", _grad_half("grad_w_in_a", core_arr, dproj_slab, h, N_CHIPS, 1, tm_in, True))
    pair_out = pair_send("out", _grad_half("grad_w_out_a", core_arr, mixcat, dmix, N_CHIPS, 1, tm_out, True,
                                           after=pair_in[4]))
    sum_in = _grad_half("grad_w_in_b", core_arr, dproj_slab, h, N_CHIPS, 1, tm_in, False,
                        recv=pair_recv("in", pair_in, pair_out[4]))
    sum_out = _grad_half("grad_w_out_b", core_arr, mixcat, dmix, N_CHIPS, 1, tm_out, False,
                         recv=pair_recv("out", pair_out, sum_in[0, :8, :LANES]))
    dh = _mm_plain("in_proj_bwd", dproj_slab, w_slab_t, "nn", BF16, tk=slab_w // 2,
                   after=sum_out[0, :8, :LANES].astype(F32))
    grad_x, acc_pre = _pre_mix_bwd(dh, x2, dx1, g_pre_mix, sc_a)

    zero_row = jnp.zeros((1, D), F32)
    tail = jnp.concatenate([db_forget[0:1, :n_fox], dsink[:, 0, :n_swa // n_kv].reshape(1, n_swa),
                            loss_part[0:1, 0:1], jnp.zeros((1, D - n_fox - n_swa - 1), F32)], axis=1)
    partial = jnp.concatenate([
        acc_pre[0:1], acc_pre[1:2], acc_mid[3:4], acc_mid[0:1], acc_mid[1:2], acc_mlp_post[0:1],
        acc_pre[2:3], acc_mid[4:5], acc_mid[2:3], acc_mlp_post[1:2], tail] + [zero_row] * 5, axis=0)
    gathered_small, token = _allgather8("gather_small_grads", partial)

    flight_mix = scatter_start("mix", [sum_in, sum_out], after=token)
    halves_mlp = scatter_finish("mlp", flight_mlp, flight_mix[4])
    share_up, share_down = [
        _ici_start("grad_share_start_" + n, [hv], [jax.ShapeDtypeStruct(hv.shape, F32)], _share_plan, per_source=1)
        for n, hv in zip(["up", "down"], halves_mlp)]

    def shared(tag, flight, after):
        send, recv, own, lands, _ = flight
        own, other = _ici_wait("grad_share_wait_" + tag, send, recv, own, lands, _share_plan, after)
        return own[0], other[0]

    def unpack(p):
        return {"b_mod": p[0:N_MOD].reshape(1, N_MOD * D), "g_pre_mix": p[6:7], "g_post_mix": p[7:8],
                "g_pre_mlp": p[8:9], "g_post_mlp": p[9:10], "b_forget": p[10:11, :n_fox],
                "swa_sinks": p[10:11, n_fox:n_fox + n_swa]}

    small_out = _small_update(gathered_small, _tie(small_state[0], share_down[4] + share_up[4]), small_state[1],
                              small_state[2])
    g_small, d_small, m_small, v_small = [unpack(p) for p in small_out]
    loss = small_out[0][N_MOD + 4, n_fox + n_swa]

    dmod_all = gathered_small.reshape(N_DEV, 16, D)[:, :N_MOD].reshape(N_DEV, N_MOD * D)
    dmod_shard = _tie(lax.dynamic_slice_in_dim(dmod_all, chip * mod_cols, mod_cols, axis=1), share_down[4])
    g_w_mod, d_w_mod, nm_w_mod, nv_w_mod = _mod_update(c_all.T, dmod_shard, w_mod[0], m_w_mod[0], v_w_mod[0])

    grads = dict(g_small, w_mod=g_w_mod[None])
    deltas = dict(d_small, w_mod=d_w_mod[None])
    new_m = dict(m_small, w_mod=nm_w_mod[None])
    new_v = dict(v_small, w_mod=nv_w_mod[None])
    weights = {"w_in": (w_in, m_w_in, v_w_in), "w_out": (w_out, m_w_out, v_w_out), "w_up": (w_up, m_w_up, v_w_up),
               "w_down": (w_down, m_w_down, v_w_down)}

    def big_update(n, own, other):
        transposed = n == "w_in"
        w, m, v = in_state if transposed else [a[0] for a in weights[n]]
        outs = _adam_halves("adam_" + n, core_arr, w, own, other, m, v)
        if transposed:
            outs = [o[:in_rows].T for o in outs]
        grads[n], deltas[n], new_m[n], new_v[n] = [o[None] for o in outs]

    big_update("w_down", *shared("down", share_down, d_w_mod[:8, :LANES] + small_out[1][:8, :LANES]))
    halves_mix = scatter_finish("mix", flight_mix, deltas["w_down"][0, :8, :LANES] + d_w_mod[:8, :LANES])
    send, recv, halves_mix, lands, _ = _ici_start(
        "grad_share_start_mix", halves_mix, [jax.ShapeDtypeStruct(hv.shape, F32) for hv in halves_mix], _share_plan,
        per_source=1)
    mix_state = [in_state, [a[0] for a in weights["w_out"]]]
    started = [_adam_one_half("adam_own_" + n, core_arr, w, g, m, v, True)
               for n, (w, m, v), g in zip(["w_in", "w_out"], mix_state, halves_mix)]
    _, others_mix = _ici_wait("grad_share_wait_mix", send, recv, halves_mix, lands, _share_plan,
                              [s[1] for s in started])
    for n, (w, m, v), g, part in zip(["w_in", "w_out"], mix_state, others_mix, started):
        outs = _adam_one_half("adam_other_" + n, core_arr, w, g, m, v, False, started=part)
        if n == "w_in":
            outs = [o[:in_rows].T for o in outs]
        grads[n], deltas[n], new_m[n], new_v[n] = [o[None] for o in outs]
    big_update("w_up", *shared("up", share_up, deltas["w_out"][0, :8, :LANES] + deltas["w_in"][0, :8, :LANES]))

    order = ["w_mod", "b_mod", "g_pre_mix", "g_post_mix", "w_in", "b_forget", "swa_sinks", "w_out", "g_pre_mlp",
             "g_post_mlp", "w_up", "w_down"]
    return (loss, grad_x[None], *[grads[n] for n in order], *[deltas[n] for n in order],
            *[new_m[n] for n in order], *[new_v[n] for n in order])
```

```python
import jax
import jax.numpy as jnp
from jax import lax
from jax.experimental import pallas as pl
from jax.experimental.pallas import tpu as pltpu

F32 = jnp.float32
BF16 = jnp.bfloat16
MESH = pl.DeviceIdType.MESH

HEAD_DIM = 128
SWA_BLOCK = 128
ROPE_THETA = 10000.0
NORM_EPS = 1e-6
NEG = -1e30
N_MOD = 6
ADAM_LR = 0.001
ADAM_B1 = 0.9
ADAM_B2 = 0.999
ADAM_EPS = 1e-08
ADAM_WD = 0.01
ADAM_STEP = 10
N_CHIPS = 4
N_DEV = 8
LANES = 128
VMEM_CAP = 60 * 1024 * 1024

_NN = (((1,), (0,)), ((), ()))
_NT = (((1,), (1,)), ((), ()))
_TN = (((0,), (0,)), ((), ()))


def _vmem(nbytes):
    return int(min(VMEM_CAP, nbytes * 21 // 20 + (2 << 20)))


def _nbytes(shape, dtype):
    n = 1
    for s in shape:
        n *= s
    return n * jnp.dtype(dtype).itemsize


def _fit(t, n):
    t = min(t, n)
    assert n % t == 0, (t, n)
    return t


MM_TM, MM_TN, MM_TK = 1024, 1024, 2048


def _matmul(name, a, b, mode, out_defs, epilogue, extras=(), tm=MM_TM, tn=MM_TN, tk=MM_TK, revisits=False,
            row_sel=None):
    stacked = b.ndim == 3
    b_rows, b_cols = b.shape[-2], b.shape[-1] * (b.shape[0] if stacked else 1)
    if mode == "nn":
        (M, K), (K2, N) = a.shape, (b_rows, b_cols)
    elif mode == "nt":
        (M, K), (N, K2) = a.shape, (b_rows, b_cols)
    else:
        (K, M), (K2, N) = a.shape, (b_rows, b_cols)
    assert K == K2 and not (stacked and mode == "tn"), (a.shape, b.shape, mode)
    tm = _fit(tm, M)
    tn = _fit(tn, b.shape[-1] if stacked and mode == "nn" else N)
    tk = _fit(tk, b.shape[-1] if stacked and mode == "nt" else K)
    nk = K // tk
    dims = {"nn": _NN, "nt": _NT, "tn": _TN}[mode]
    if row_sel is None:
        grid_m, a_row = M // tm, lambda i, *sel: i
    else:
        grid_m, a_row = row_sel[2], lambda i, *sel: row_sel[1](i, sel[0])
    a_spec = (pl.BlockSpec((tk, tm), lambda i, j, k, *sel: (k, a_row(i, *sel))) if mode == "tn"
              else pl.BlockSpec((tm, tk), lambda i, j, k, *sel: (a_row(i, *sel), k)))
    if stacked:
        per = b.shape[-1] // (tk if mode == "nt" else tn)
        b_spec = (pl.BlockSpec((1, tn, tk), lambda i, j, k, *sel: (k // per, j, k % per)) if mode == "nt"
                  else pl.BlockSpec((1, tk, tn), lambda i, j, k, *sel: (j // per, k, j % per)))
    else:
        b_spec = (pl.BlockSpec((tn, tk), lambda i, j, k, *sel: (j, k)) if mode == "nt"
                  else pl.BlockSpec((tk, tn), lambda i, j, k, *sel: (k, j)))
    n_ex, n_out = len(extras), len(out_defs)

    def body(*refs):
        if row_sel is not None:
            refs = refs[1:]
        a_ref, b_ref = refs[0], refs[1]
        ex = refs[2:2 + n_ex]
        outs = refs[2 + n_ex:2 + n_ex + n_out]
        b_blk = b_ref[0] if stacked else b_ref[...]
        prod = lax.dot_general(a_ref[...], b_blk, dims, preferred_element_type=F32)
        if nk == 1:
            epilogue(prod, ex, outs)
        else:
            acc_ref = refs[-1]
            k = pl.program_id(2)

            @pl.when(k == 0)
            def _():
                acc_ref[...] = prod

            @pl.when(k > 0)
            def _():
                acc_ref[...] += prod

            @pl.when(k == nk - 1)
            def _():
                epilogue(acc_ref[...], ex, outs)

    def wrap(f):
        return lambda i, j, k, *sel: f(i, j)

    in_specs = [a_spec, b_spec] + [pl.BlockSpec(blk, wrap(f)) for _, blk, f in extras]
    out_specs = [pl.BlockSpec(blk, wrap(f)) for _, _, blk, f in out_defs]
    out_shape = [jax.ShapeDtypeStruct(s, d) for s, d, _, _ in out_defs]
    need = 2 * (tm * tk + tk * tn) * a.dtype.itemsize + (2 if nk > 1 else 1) * tm * tn * 4
    need += sum(2 * _nbytes(blk, arr.dtype) for arr, blk, _ in extras)
    need += sum(2 * _nbytes(blk, d) for _, d, blk, _ in out_defs)
    grid = (grid_m, N // tn, nk)
    scratch = [pltpu.VMEM((tm, tn), F32)] if nk > 1 else []
    params = pltpu.CompilerParams(
        dimension_semantics=("parallel", "arbitrary" if revisits else "parallel", "arbitrary"),
        vmem_limit_bytes=_vmem(need))
    operands = (a, b, *[arr for arr, _, _ in extras])
    if row_sel is None:
        return pl.pallas_call(body, name=name, grid=grid, in_specs=in_specs, out_specs=out_specs, out_shape=out_shape,
                              scratch_shapes=scratch, compiler_params=params)(*operands)
    grid_spec = pltpu.PrefetchScalarGridSpec(num_scalar_prefetch=1, grid=grid, in_specs=in_specs, out_specs=out_specs,
                                             scratch_shapes=scratch)
    return pl.pallas_call(body, name=name, grid_spec=grid_spec, out_shape=out_shape,
                          compiler_params=params)(row_sel[0], *operands)


def _grad_half(name, core, a, b, row_slabs, col_slabs, tm, other, recv=None, after=None):
    (_, M), (_, N) = a.shape, b.shape
    H = M // (2 * row_slabs)
    nh = H // tm
    tn = _fit(MM_TN, N // col_slabs)
    per = N // col_slabs // tn

    def a_block(i, core_ref):
        half = (1 - core_ref[0]) if other else core_ref[0]
        return (i // nh) * (2 * nh) + half * nh + i % nh

    def out_index(i, j):
        return (j // per, i, j % per) if col_slabs > 1 else (i // nh, i % nh, j)

    slabs = max(row_slabs, col_slabs)
    out_def = ((slabs, H, N // col_slabs), BF16, (1, tm, tn), out_index)

    def epilogue(acc, ex, outs):
        outs[0][0] = (acc if recv is None else acc + ex[0][0].astype(F32)).astype(BF16)

    extras = ([] if recv is None else [(recv, (1, tm, tn), out_index)]) + ([] if after is None else [_behind(after)])
    return _matmul(name, a, b, "tn", [out_def], epilogue, extras=extras, tm=tm, tn=tn,
                   row_sel=(core, a_block, row_slabs * nh))[0]


def _behind(token):
    return (token, (8, LANES), lambda i, j: (0, 0))


def _mm_plain(name, a, b, mode, out_dtype, after=None, **tiles):
    if mode == "nn":
        M, N = a.shape[0], b.shape[-1] * (b.shape[0] if b.ndim == 3 else 1)
    elif mode == "nt":
        M, N = a.shape[0], b.shape[-2]
    else:
        M, N = a.shape[1], b.shape[1]
    tm, tn = _fit(tiles.get("tm", MM_TM), M), _fit(tiles.get("tn", MM_TN), N)

    def epi(acc, ex, outs):
        outs[0][...] = acc.astype(out_dtype)

    return _matmul(name, a, b, mode, [((M, N), out_dtype, (tm, tn), lambda i, j: (i, j))], epi,
                   extras=[] if after is None else [_behind(after)], **tiles)[0]


def _rstd(v):
    return lax.rsqrt(jnp.mean(v * v, axis=-1, keepdims=True) + NORM_EPS)


ROW_TILE = 256


def _row_call(name, body, row_ins, vec_ins, row_outs, acc_outs, S, D):
    tr = _fit(ROW_TILE, S)
    row_spec = pl.BlockSpec((tr, D), lambda r: (r, 0))
    vec_spec = pl.BlockSpec((1, D), lambda r: (0, 0))
    in_specs = [row_spec] * len(row_ins) + [vec_spec] * len(vec_ins)
    out_specs = [row_spec] * len(row_outs) + [pl.BlockSpec(shp, lambda r: (0, 0)) for shp in acc_outs]
    out_shape = [jax.ShapeDtypeStruct((S, D), d) for d in row_outs] + [jax.ShapeDtypeStruct(shp, F32) for shp in acc_outs]
    need = sum(2 * tr * D * a.dtype.itemsize for a in row_ins) + sum(2 * tr * D * jnp.dtype(d).itemsize for d in row_outs)
    need += 6 * tr * D * 4
    return pl.pallas_call(
        body, name=name, grid=(S // tr,), in_specs=in_specs, out_specs=out_specs, out_shape=out_shape,
        compiler_params=pltpu.CompilerParams(dimension_semantics=("arbitrary",), vmem_limit_bytes=_vmem(need)),
    )(*row_ins, *vec_ins)


def _acc_rows(ref, rows):
    @pl.when(pl.program_id(0) == 0)
    def _():
        ref[...] = jnp.zeros_like(ref)
    for n, r in enumerate(rows):
        ref[n:n + 1, :] += r


def _pre_norm(x, g, sc, sh):
    S, D = x.shape

    def body(x_ref, g_ref, sc_ref, sh_ref, h_ref):
        xv = x_ref[...]
        xn = xv * _rstd(xv)
        h_ref[...] = (xn * g_ref[...] * (1.0 + sc_ref[...]) + sh_ref[...]).astype(BF16)

    return _row_call("pre_norm_mix", body, [x], [g, sc, sh], [BF16], [], S, D)[0]


def _post_mix(x, mix, g_post, gt, g_pre, sc, sh):
    S, D = x.shape

    def body(x_ref, mix_ref, gp_ref, gt_ref, g2_ref, sc_ref, sh_ref, x1_ref, h2_ref):
        mv = mix_ref[...].astype(F32)
        x1 = x_ref[...] + gt_ref[...] * (mv * _rstd(mv) * gp_ref[...])
        x1_ref[...] = x1
        h2_ref[...] = (x1 * _rstd(x1) * g2_ref[...] * (1.0 + sc_ref[...]) + sh_ref[...]).astype(BF16)

    return _row_call("post_mix_pre_mlp", body, [x, mix], [g_post, gt, g_pre, sc, sh], [F32, BF16], [], S, D)


def _loss_and_post_mlp_bwd(x1, y, target, g_post, gt):
    S, D = x1.shape

    def body(x1_ref, y_ref, t_ref, g_ref, gt_ref, dy_ref, dout_ref, loss_ref, acc_ref):
        yv = y_ref[...].astype(F32)
        r = _rstd(yv)
        yh = yv * r
        n = yh * g_ref[...]
        diff = x1_ref[...] + gt_ref[...] * n - t_ref[...]
        dout = diff * (1.0 / D)
        dout_ref[...] = dout
        dn = dout * gt_ref[...]
        dyh = dn * g_ref[...]
        dy_ref[...] = (r * (dyh - yh * jnp.mean(dyh * yh, axis=-1, keepdims=True))).astype(BF16)
        _acc_rows(acc_ref, [jnp.sum(dout * n, axis=0, keepdims=True), jnp.sum(dn * yh, axis=0, keepdims=True)])

        @pl.when(pl.program_id(0) == 0)
        def _():
            loss_ref[...] = jnp.zeros_like(loss_ref)
        loss_ref[...] += jnp.full(loss_ref.shape, (0.5 / D) * jnp.sum(diff * diff), F32)

    return _row_call("loss_post_mlp_bwd", body, [x1, y, target], [g_post, gt], [BF16, F32],
                     [(8, LANES), (8, D)], S, D)


def _pre_mlp_and_post_mix_bwd(dh2, x1, dout, mix, g_pre, sc, g_post, gt):
    S, D = x1.shape

    def body(dh_ref, x1_ref, dout_ref, mix_ref, g_ref, sc_ref, gp_ref, gt_ref, dx1_ref, dmix_ref, acc_ref):
        dh = dh_ref[...].astype(F32)
        x1v = x1_ref[...]
        r3 = _rstd(x1v)
        xn = x1v * r3
        dxn = dh * (1.0 + sc_ref[...]) * g_ref[...]
        dx1 = dout_ref[...] + r3 * (dxn - xn * jnp.mean(dxn * xn, axis=-1, keepdims=True))
        dx1_ref[...] = dx1
        mv = mix_ref[...].astype(F32)
        r2 = _rstd(mv)
        mh = mv * r2
        dn = dx1 * gt_ref[...]
        dmh = dn * gp_ref[...]
        dmix_ref[...] = (r2 * (dmh - mh * jnp.mean(dmh * mh, axis=-1, keepdims=True))).astype(BF16)
        _acc_rows(acc_ref, [
            jnp.sum(dh, axis=0, keepdims=True),
            jnp.sum(dh * xn * g_ref[...], axis=0, keepdims=True),
            jnp.sum(dh * (1.0 + sc_ref[...]) * xn, axis=0, keepdims=True),
            jnp.sum(dx1 * mh * gp_ref[...], axis=0, keepdims=True),
            jnp.sum(dn * mh, axis=0, keepdims=True)])

    return _row_call("pre_mlp_post_mix_bwd", body, [dh2, x1, dout, mix], [g_pre, sc, g_post, gt], [F32, BF16],
                     [(8, D)], S, D)


def _pre_mix_bwd(dh, x, dx1, g_pre, sc):
    S, D = x.shape

    def body(dh_ref, x_ref, dx1_ref, g_ref, sc_ref, gx_ref, acc_ref):
        dhv = dh_ref[...].astype(F32)
        xv = x_ref[...]
        r = _rstd(xv)
        xn = xv * r
        dxn = dhv * (1.0 + sc_ref[...]) * g_ref[...]
        gx_ref[...] = dx1_ref[...] + r * (dxn - xn * jnp.mean(dxn * xn, axis=-1, keepdims=True))
        _acc_rows(acc_ref, [
            jnp.sum(dhv, axis=0, keepdims=True),
            jnp.sum(dhv * xn * g_ref[...], axis=0, keepdims=True),
            jnp.sum(dhv * (1.0 + sc_ref[...]) * xn, axis=0, keepdims=True)])

    return _row_call("pre_mix_bwd", body, [dh, x, dx1], [g_pre, sc], [F32], [(8, D)], S, D)


CUM_BLOCK = 256


def _tri(n, upper):
    r = lax.broadcasted_iota(jnp.int32, (n, n), 0)
    c = lax.broadcasted_iota(jnp.int32, (n, n), 1)
    return ((c >= r) if upper else (c <= r)).astype(F32)


def _fox_gate_fwd(fg, b_pad):
    S = fg.shape[0]
    cb = _fit(CUM_BLOCK, S)

    def body(fg_ref, b_ref, cumt_ref, cum_ref):
        low = _tri(cb, False)
        carry = jnp.zeros((1, LANES), F32)
        for n in range(S // cb):
            z = fg_ref[n * cb:(n + 1) * cb, :] + b_ref[...]
            logf = jnp.minimum(z, 0.0) - jnp.log(1.0 + jnp.exp(-jnp.abs(z)))
            blk = jnp.dot(low, logf, precision=lax.Precision.HIGHEST, preferred_element_type=F32) + carry
            cum_ref[n * cb:(n + 1) * cb, :] = blk
            carry = blk[cb - 1:cb, :]
        cumt_ref[...] = cum_ref[...].T

    return pl.pallas_call(
        body, name="fox_gate_fwd", out_shape=jax.ShapeDtypeStruct((LANES, S), F32),
        scratch_shapes=[pltpu.VMEM((S, LANES), F32)],
        compiler_params=pltpu.CompilerParams(vmem_limit_bytes=_vmem(6 * S * LANES * 4)),
    )(fg, b_pad)


def _fox_gate_bwd(dcum_k, dcum_q, fg, b_pad):
    S = fg.shape[0]
    n_fox = dcum_q.shape[0]
    cb = _fit(CUM_BLOCK, S)

    def body(dk_ref, dq_ref, fg_ref, b_ref, dfg_ref, db_ref, dc_ref):
        lane = lax.broadcasted_iota(jnp.int32, (S, LANES), 1)
        dc = dk_ref[...].T
        for h in range(n_fox):
            dc = dc + jnp.where(lane == h, dq_ref[h], 0.0)
        dc_ref[...] = dc
        up = _tri(cb, True)
        carry = jnp.zeros((1, LANES), F32)
        db = jnp.zeros((1, LANES), F32)
        for n in reversed(range(S // cb)):
            blk = jnp.dot(up, dc_ref[n * cb:(n + 1) * cb, :], precision=lax.Precision.HIGHEST,
                          preferred_element_type=F32) + carry
            carry = blk[0:1, :]
            z = fg_ref[n * cb:(n + 1) * cb, :] + b_ref[...]
            dfg = blk * (1.0 / (1.0 + jnp.exp(z)))
            dfg_ref[n * cb:(n + 1) * cb, :] = dfg.astype(BF16)
            db = db + jnp.sum(dfg, axis=0, keepdims=True)
        db_ref[...] = jnp.broadcast_to(db, db_ref.shape)

    return pl.pallas_call(
        body, name="fox_gate_bwd",
        out_shape=[jax.ShapeDtypeStruct((S, LANES), BF16), jax.ShapeDtypeStruct((8, LANES), F32)],
        scratch_shapes=[pltpu.VMEM((S, LANES), F32)],
        compiler_params=pltpu.CompilerParams(vmem_limit_bytes=_vmem((8 + 2 * n_fox) * S * LANES * 4)),
    )(dcum_k, dcum_q, fg, b_pad)


FOX_TILE = 512


LOG2E = 1.4426950408889634


def _fox_scores(q, k, ck2, masked, t):
    s = lax.dot_general(q, k, _NT, preferred_element_type=F32) * (HEAD_DIM ** -0.5 * LOG2E) - ck2
    if masked:
        row = lax.broadcasted_iota(jnp.int32, (t, t), 0)
        col = lax.broadcasted_iota(jnp.int32, (t, t), 1)
        s = jnp.where(col <= row, s, NEG)
    return s


def _fox_fwd(proj, cum_row, n_fox):
    S = proj.shape[0]
    t = _fit(FOX_TILE, S)
    nq = S // t

    def body(q_ref, k_ref, v_ref, ck_ref, o_ref, lse_ref):
        def q_block(qi, _):
            q0 = pl.multiple_of(qi * t, t)
            q = q_ref[pl.ds(q0, t), :]

            def kv_block(j, carry, masked):
                m, l, acc = carry
                k0 = pl.multiple_of(j * t, t)
                s = _fox_scores(q, k_ref[pl.ds(k0, t), :], ck_ref[0, :, pl.ds(k0, t)] * LOG2E, masked, t)
                m_new = jnp.maximum(m, jnp.max(s, axis=-1, keepdims=True))
                alpha = jnp.exp2(m - m_new)
                p = jnp.exp2(s - m_new)
                l = alpha * l + jnp.sum(p, axis=-1, keepdims=True)
                acc = alpha * acc + jnp.dot(p.astype(BF16), v_ref[pl.ds(k0, t), :], preferred_element_type=F32)
                return m_new, l, acc

            init = (jnp.full((t, 1), NEG, F32), jnp.zeros((t, 1), F32), jnp.zeros((t, HEAD_DIM), F32))
            carry = lax.fori_loop(0, qi, lambda j, cr: kv_block(j, cr, False), init)
            m, l, acc = kv_block(qi, carry, True)
            o_ref[pl.ds(q0, t), :] = acc / l
            lse_ref[0, pl.ds(q0, t), :] = jnp.broadcast_to(m + jnp.log(l) * LOG2E, (t, LANES))
            return 0

        lax.fori_loop(0, nq, q_block, 0)

    col = lambda off: pl.BlockSpec((S, HEAD_DIM), lambda h: (0, off + h))
    per_head = pl.BlockSpec((1, S, LANES), lambda h: (h, 0, 0))
    return pl.pallas_call(
        body, name="fox_fwd", grid=(n_fox,),
        in_specs=[col(0), col(n_fox), col(2 * n_fox), pl.BlockSpec((1, 1, S), lambda h: (h, 0, 0))],
        out_specs=[pl.BlockSpec((S, HEAD_DIM), lambda h: (0, h)), per_head],
        out_shape=[jax.ShapeDtypeStruct((S, n_fox * HEAD_DIM), F32), jax.ShapeDtypeStruct((n_fox, S, LANES), F32)],
        compiler_params=pltpu.CompilerParams(dimension_semantics=("parallel",),
                                             vmem_limit_bytes=_vmem(8 * S * HEAD_DIM * 4 + 6 * t * t * 4)),
    )(proj, proj, proj, cum_row)


def _fox_bwd(proj, o, do, lse_b, cum_row, n_fox):
    S = proj.shape[0]
    t = _fit(FOX_TILE, S)
    nq = S // t
    scale = HEAD_DIM ** -0.5

    def body(q_ref, k_ref, v_ref, o_ref, do_ref, lse_ref, ck_ref, dq_ref, dk_ref, dv_ref, dc_ref, dcq_ref,
             dq_acc, delta_ref):
        dq_acc[...] = jnp.zeros_like(dq_acc)
        dcq_ref[...] = jnp.zeros_like(dcq_ref)

        def delta_block(qi, _):
            q0 = pl.multiple_of(qi * t, t)
            d = jnp.sum(do_ref[pl.ds(q0, t), :] * o_ref[pl.ds(q0, t), :], axis=-1, keepdims=True)
            delta_ref[pl.ds(q0, t), :] = jnp.broadcast_to(d, (t, LANES))
            return 0

        lax.fori_loop(0, nq, delta_block, 0)

        def kv_block(j, _):
            k0 = pl.multiple_of(j * t, t)
            k = k_ref[pl.ds(k0, t), :]
            v = v_ref[pl.ds(k0, t), :]
            ck2 = ck_ref[0, :, pl.ds(k0, t)] * LOG2E

            def q_block(qi, carry, masked):
                dk, dv, dc = carry
                q0 = pl.multiple_of(qi * t, t)
                q = q_ref[pl.ds(q0, t), :]
                dov = do_ref[pl.ds(q0, t), :].astype(BF16)
                p = jnp.exp2(_fox_scores(q, k, ck2, masked, t) - lse_ref[0, pl.ds(q0, t), :][:, :1])
                dp = lax.dot_general(dov, v, _NT, preferred_element_type=F32)
                ds = p * (dp - delta_ref[pl.ds(q0, t), :][:, :1])
                dsb = ds.astype(BF16)
                dv = dv + lax.dot_general(p.astype(BF16), dov, _TN, preferred_element_type=F32)
                dk = dk + lax.dot_general(dsb, q, _TN, preferred_element_type=F32)
                dq_acc[pl.ds(q0, t), :] += jnp.dot(dsb, k, preferred_element_type=F32)
                dc = dc - jnp.sum(ds, axis=0, keepdims=True)
                dcq_ref[0, pl.ds(q0, t), :] += jnp.broadcast_to(jnp.sum(ds, axis=1, keepdims=True), (t, LANES))
                return dk, dv, dc

            init = (jnp.zeros((t, HEAD_DIM), F32), jnp.zeros((t, HEAD_DIM), F32), jnp.zeros((1, t), F32))
            carry = q_block(j, init, True)
            dk, dv, dc = lax.fori_loop(j + 1, nq, lambda qi, cr: q_block(qi, cr, False), carry)
            dk_ref[pl.ds(k0, t), :] = (dk * scale).astype(BF16)
            dv_ref[pl.ds(k0, t), :] = dv.astype(BF16)
            dc_ref[0, :, pl.ds(k0, t)] = dc
            return 0

        lax.fori_loop(0, nq, kv_block, 0)
        dq_ref[...] = (dq_acc[...] * scale).astype(BF16)

    col = lambda off: pl.BlockSpec((S, HEAD_DIM), lambda h: (0, off + h))
    per_head = pl.BlockSpec((1, S, LANES), lambda h: (h, 0, 0))
    row = pl.BlockSpec((1, 1, S), lambda h: (h, 0, 0))
    grad = jax.ShapeDtypeStruct((S, n_fox * HEAD_DIM), BF16)
    return pl.pallas_call(
        body, name="fox_bwd", grid=(n_fox,),
        in_specs=[col(0), col(n_fox), col(2 * n_fox), col(0), col(0), per_head, row],
        out_specs=[col(0), col(0), col(0), row, per_head],
        out_shape=[grad, grad, grad, jax.ShapeDtypeStruct((n_fox, 1, S), F32), jax.ShapeDtypeStruct((n_fox, S, LANES), F32)],
        scratch_shapes=[pltpu.VMEM((S, HEAD_DIM), F32), pltpu.VMEM((S, LANES), F32)],
        compiler_params=pltpu.CompilerParams(dimension_semantics=("parallel",),
                                             vmem_limit_bytes=_vmem(14 * S * HEAD_DIM * 4 + 10 * t * t * 4)),
    )(proj, proj, proj, o, do, lse_b, cum_row)


def _rope_tables(S):
    half = HEAD_DIM // 2
    inv_freq = 1.0 / (ROPE_THETA ** (jnp.arange(half, dtype=F32) * (2.0 / HEAD_DIM)))
    ang = jnp.arange(S).astype(F32)[:, None] * inv_freq[None, :]
    cos, sin = jnp.cos(ang), jnp.sin(ang)
    return jnp.concatenate([cos, cos], axis=-1), jnp.concatenate([-sin, sin], axis=-1)


def _rope(name, src, first_block, n_blocks, cos, sin_signed):
    S = src.shape[0]

    def body(x_ref, cos_ref, sin_ref, o_ref):
        xv = x_ref[...].astype(F32)
        o_ref[...] = (xv * cos_ref[...] + pltpu.roll(xv, HEAD_DIM // 2, 1) * sin_ref[...]).astype(BF16)

    table = pl.BlockSpec((S, HEAD_DIM), lambda n: (0, 0))
    return pl.pallas_call(
        body, name=name, grid=(n_blocks,),
        in_specs=[pl.BlockSpec((S, HEAD_DIM), lambda n: (0, first_block + n)), table, table],
        out_specs=pl.BlockSpec((S, HEAD_DIM), lambda n: (0, n)),
        out_shape=jax.ShapeDtypeStruct((S, n_blocks * HEAD_DIM), BF16),
        compiler_params=pltpu.CompilerParams(dimension_semantics=("parallel",),
                                             vmem_limit_bytes=_vmem(12 * S * HEAD_DIM * 4)),
    )(src, cos, sin_signed)


def _swa_tile(q_ref, kp_ref, kc_ref, n, group, scale):
    B = SWA_BLOCK
    qs = jnp.concatenate([q_ref[:, g * HEAD_DIM:(g + 1) * HEAD_DIM] for g in range(group)], axis=0)
    kcat = jnp.concatenate([kp_ref[...], kc_ref[...]], axis=0)
    s = lax.dot_general(qs, kcat, _NT, preferred_element_type=F32) * scale
    qi = lax.broadcasted_iota(jnp.int32, (group * B, 2 * B), 0) % B
    kj = lax.broadcasted_iota(jnp.int32, (group * B, 2 * B), 1)
    diff = qi + B - kj
    mask = (diff >= 0) & (diff < B) & ((n * B + kj - B) >= 0)
    return qs, kcat, jnp.where(mask, s, NEG)


def _swa_sink_col(sink_ref, kv, group):
    head = lax.broadcasted_iota(jnp.int32, (group * SWA_BLOCK, 1), 0) // SWA_BLOCK
    col = jnp.zeros((group * SWA_BLOCK, 1), F32)
    for g in range(group):
        col = jnp.where(head == g, sink_ref[kv * group + g], col)
    return col


def _swa_specs(n_kv, group, q_first, k_first, v_first):
    B = SWA_BLOCK
    prev = lambda n: jnp.maximum(n - 1, 0)
    return [
        pl.BlockSpec((B, group * HEAD_DIM), lambda kv, n: (n, q_first + kv)),
        pl.BlockSpec((B, HEAD_DIM), lambda kv, n: (prev(n), k_first + kv)),
        pl.BlockSpec((B, HEAD_DIM), lambda kv, n: (n, k_first + kv)),
        pl.BlockSpec((B, HEAD_DIM), lambda kv, n: (prev(n), v_first + kv)),
        pl.BlockSpec((B, HEAD_DIM), lambda kv, n: (n, v_first + kv)),
    ]


def _swa_fwd(rq, proj, v_first, sinks, n_q, n_kv):
    S = rq.shape[0]
    B = SWA_BLOCK
    group = n_q // n_kv
    scale = HEAD_DIM ** -0.5

    def body(q_ref, kp_ref, kc_ref, vp_ref, vc_ref, sink_ref, o_ref, lse_ref):
        kv, n = pl.program_id(0), pl.program_id(1)
        _, _, s = _swa_tile(q_ref, kp_ref, kc_ref, n, group, scale)
        sink = _swa_sink_col(sink_ref, kv, group)
        m = jnp.maximum(jnp.max(s, axis=-1, keepdims=True), sink)
        p = jnp.exp(s - m)
        denom = jnp.sum(p, axis=-1, keepdims=True) + jnp.exp(sink - m)
        vcat = jnp.concatenate([vp_ref[...], vc_ref[...]], axis=0)
        o = jnp.dot((p / denom).astype(BF16), vcat, preferred_element_type=F32)
        lse = m + jnp.log(denom)
        for g in range(group):
            o_ref[:, g * HEAD_DIM:(g + 1) * HEAD_DIM] = o[g * B:(g + 1) * B, :]
            lse_ref[0, :, g * LANES:(g + 1) * LANES] = jnp.broadcast_to(lse[g * B:(g + 1) * B, :], (B, LANES))

    specs = _swa_specs(n_kv, group, 0, n_q, v_first)
    q_blk = pl.BlockSpec((B, group * HEAD_DIM), lambda kv, n: (n, kv))
    return pl.pallas_call(
        body, name="swa_fwd", grid=(n_kv, S // B),
        in_specs=specs + [pl.BlockSpec(memory_space=pltpu.SMEM)],
        out_specs=[q_blk, pl.BlockSpec((1, B, group * LANES), lambda kv, n: (kv, n, 0))],
        out_shape=[jax.ShapeDtypeStruct((S, n_q * HEAD_DIM), F32), jax.ShapeDtypeStruct((n_kv, S, group * LANES), F32)],
        compiler_params=pltpu.CompilerParams(dimension_semantics=("parallel", "arbitrary")),
    )(rq, rq, rq, proj, proj, sinks)


def _swa_bwd(rq, proj, v_first, sinks, o, do, do_first, lse_b, n_q, n_kv):
    S = rq.shape[0]
    B = SWA_BLOCK
    group = n_q // n_kv
    scale = HEAD_DIM ** -0.5

    def body(q_ref, kp_ref, kc_ref, vp_ref, vc_ref, o_ref, do_ref, lse_ref, sink_ref,
             dq_ref, dk_ref, dv_ref, dsink_ref):
        kv, n = pl.program_id(0), pl.program_id(1)

        @pl.when(n == 0)
        def _():
            dk_ref[...] = jnp.zeros_like(dk_ref)
            dv_ref[...] = jnp.zeros_like(dv_ref)
            dsink_ref[...] = jnp.zeros_like(dsink_ref)

        qs, kcat, s = _swa_tile(q_ref, kp_ref, kc_ref, n, group, scale)
        sink = _swa_sink_col(sink_ref, kv, group)
        stack = lambda ref, w: jnp.concatenate([ref[:, g * w:(g + 1) * w] for g in range(group)], axis=0)
        lse = jnp.concatenate([lse_ref[0, :, g * LANES:g * LANES + 1] for g in range(group)], axis=0)
        do32 = stack(do_ref, HEAD_DIM)
        delta = jnp.sum(do32 * stack(o_ref, HEAD_DIM), axis=-1, keepdims=True)
        dov = do32.astype(BF16)
        p = jnp.exp(s - lse)
        vcat = jnp.concatenate([vp_ref[...], vc_ref[...]], axis=0)
        dp = lax.dot_general(dov, vcat, _NT, preferred_element_type=F32)
        ds = p * (dp - delta)
        dsb = ds.astype(BF16)
        dq = jnp.dot(dsb, kcat, preferred_element_type=F32) * scale
        for g in range(group):
            dq_ref[:, g * HEAD_DIM:(g + 1) * HEAD_DIM] = dq[g * B:(g + 1) * B, :].astype(BF16)
        dkcat = lax.dot_general(dsb, qs, _TN, preferred_element_type=F32) * scale
        dvcat = lax.dot_general(p.astype(BF16), dov, _TN, preferred_element_type=F32)
        prev0 = pl.multiple_of(jnp.maximum(n - 1, 0) * B, B)
        cur0 = pl.multiple_of(n * B, B)
        dk_ref[0, pl.ds(prev0, B), :] += dkcat[:B, :]
        dk_ref[0, pl.ds(cur0, B), :] += dkcat[B:, :]
        dv_ref[0, pl.ds(prev0, B), :] += dvcat[:B, :]
        dv_ref[0, pl.ds(cur0, B), :] += dvcat[B:, :]
        dsk = -jnp.exp(sink - lse) * delta
        lane = lax.broadcasted_iota(jnp.int32, (1, LANES), 1)
        row = jnp.zeros((1, LANES), F32)
        for g in range(group):
            row = row + jnp.where(lane == g, jnp.sum(dsk[g * B:(g + 1) * B, :]), 0.0)
        dsink_ref[0, 0:1, :] += row

    specs = _swa_specs(n_kv, group, 0, n_q, v_first)
    q_blk = pl.BlockSpec((B, group * HEAD_DIM), lambda kv, n: (n, kv))
    acc = pl.BlockSpec((1, S, HEAD_DIM), lambda kv, n: (kv, 0, 0))
    return pl.pallas_call(
        body, name="swa_bwd", grid=(n_kv, S // B),
        in_specs=specs + [q_blk, pl.BlockSpec((B, group * HEAD_DIM), lambda kv, n: (n, do_first + kv)),
                          pl.BlockSpec((1, B, group * LANES), lambda kv, n: (kv, n, 0)),
                          pl.BlockSpec(memory_space=pltpu.SMEM)],
        out_specs=[q_blk, acc, acc, pl.BlockSpec((1, 8, LANES), lambda kv, n: (kv, 0, 0))],
        out_shape=[jax.ShapeDtypeStruct((S, n_q * HEAD_DIM), BF16), jax.ShapeDtypeStruct((n_kv, S, HEAD_DIM), F32),
                   jax.ShapeDtypeStruct((n_kv, S, HEAD_DIM), F32), jax.ShapeDtypeStruct((n_kv, 8, LANES), F32)],
        compiler_params=pltpu.CompilerParams(dimension_semantics=("parallel", "arbitrary")),
    )(rq, rq, rq, proj, proj, o, do, lse_b, sinks)


def _adamw(w, g, m, v):
    m = ADAM_B1 * m + (1.0 - ADAM_B1) * g
    v = ADAM_B2 * v + (1.0 - ADAM_B2) * (g * g)
    m_hat = m / (1.0 - ADAM_B1 ** ADAM_STEP)
    v_hat = v / (1.0 - ADAM_B2 ** ADAM_STEP)
    delta = -ADAM_LR * (m_hat / (jnp.sqrt(v_hat) + ADAM_EPS) + ADAM_WD * w)
    return delta, m, v


def _mod_fwd(cond_in, w_mod, b_shard):
    R, D = cond_in.shape
    cols = w_mod.shape[1]
    tn = _fit(512, cols)

    def body(c_ref, w_ref, b_ref, o_ref):
        cv = c_ref[...]
        cond = (cv / (1.0 + jnp.exp(-cv))).astype(BF16)
        o_ref[...] = jnp.dot(cond, w_ref[...].astype(BF16), preferred_element_type=F32) + b_ref[...]

    return pl.pallas_call(
        body, name="mod_fwd", grid=(cols // tn,),
        in_specs=[pl.BlockSpec((R, D), lambda j: (0, 0)), pl.BlockSpec((D, tn), lambda j: (0, j)),
                  pl.BlockSpec((1, tn), lambda j: (0, j))],
        out_specs=pl.BlockSpec((R, tn), lambda j: (0, j)),
        out_shape=jax.ShapeDtypeStruct((R, cols), F32),
        compiler_params=pltpu.CompilerParams(dimension_semantics=("parallel",), vmem_limit_bytes=_vmem(3 * D * tn * 4)),
    )(cond_in, w_mod, b_shard)


def _mod_update(c_t, dmod, w, m, v):
    D, nb = c_t.shape
    cols = w.shape[1]
    tn = _fit(256, cols)

    def body(c_ref, d_ref, w_ref, m_ref, v_ref, g_ref, dl_ref, nm_ref, nv_ref):
        cv = c_ref[...]
        cond = cv / (1.0 + jnp.exp(-cv))
        g = jnp.zeros((D, tn), F32)
        for b in range(nb):
            g = g + cond[:, b:b + 1] * d_ref[b:b + 1, :]
        g_ref[...] = g
        dl_ref[...], nm_ref[...], nv_ref[...] = _adamw(w_ref[...], g, m_ref[...], v_ref[...])

    blk = pl.BlockSpec((D, tn), lambda j: (0, j))
    out = jax.ShapeDtypeStruct((D, cols), F32)
    return pl.pallas_call(
        body, name="mod_update", grid=(cols // tn,),
        in_specs=[pl.BlockSpec((D, nb), lambda j: (0, 0)), pl.BlockSpec((nb, tn), lambda j: (0, j)), blk, blk, blk],
        out_specs=[blk] * 4, out_shape=[out] * 4,
        compiler_params=pltpu.CompilerParams(dimension_semantics=("parallel",), vmem_limit_bytes=_vmem(18 * D * tn * 4)),
    )(c_t, dmod, w, m, v)


def _small_update(stacked, w, m, v):
    R, C = w.shape

    def body(s_ref, w_ref, m_ref, v_ref, g_ref, dl_ref, nm_ref, nv_ref):
        g = s_ref[0:R, :]
        for d in range(1, N_DEV):
            g = g + s_ref[d * R:(d + 1) * R, :]
        g_ref[...] = g
        dl_ref[...], nm_ref[...], nv_ref[...] = _adamw(w_ref[...], g, m_ref[...], v_ref[...])

    return pl.pallas_call(body, name="small_update", out_shape=[jax.ShapeDtypeStruct((R, C), F32)] * 4)(stacked, w, m, v)


def _place():
    return lax.axis_index("x"), lax.axis_index("y"), lax.axis_index("c")


def _allgather8(name, block):
    m_per, n = block.shape

    def body(x_ref, out_ref, token_ref, send_sems, recv_sems, local_sem):
        token_ref[...] = jnp.zeros_like(token_ref)
        x, y, c = _place()
        me, sibling = (x, y, c), (x, y, 1 - c)
        chips = [(1 - x, y), (x, 1 - y), (1 - x, 1 - y)]

        def rows(px, py, pc):
            return out_ref.at[pl.ds((4 * px + 2 * py + pc) * m_per, m_per), :]

        def copy(k, blk, to, src=None):
            return pltpu.make_async_remote_copy(
                src_ref=rows(*blk) if src is None else src, dst_ref=rows(*blk),
                send_sem=send_sems.at[k], recv_sem=recv_sems.at[k], device_id=to, device_id_type=MESH)

        mine = pltpu.make_async_copy(x_ref, rows(*me), local_sem)
        mine.start()
        first = [copy(0, me, sibling, src=x_ref)]
        first += [copy(1 + j, me, (*chip, c), src=x_ref) for j, chip in enumerate(chips)]
        for cp in first:
            cp.start()
        passed = [copy(4 + j, (*chip, c), sibling) for j, chip in enumerate(chips)]
        for j, chip in enumerate(chips):
            copy(1 + j, (*chip, c), me).wait_recv()
            passed[j].start()
        copy(0, sibling, me).wait_recv()
        for j, chip in enumerate(chips):
            copy(4 + j, (*chip, 1 - c), me).wait_recv()
        for cp in first + passed:
            cp.wait_send()
        mine.wait()

    vmem = pl.BlockSpec(memory_space=pltpu.VMEM)
    return pl.pallas_call(
        body, name=name,
        out_shape=[jax.ShapeDtypeStruct((N_DEV * m_per, n), block.dtype), jax.ShapeDtypeStruct((8, LANES), F32)],
        in_specs=[vmem], out_specs=[vmem, vmem],
        scratch_shapes=[pltpu.SemaphoreType.DMA((7,)), pltpu.SemaphoreType.DMA((7,)), pltpu.SemaphoreType.DMA],
    )(block)


_ANY = pl.BlockSpec(memory_space=pl.ANY)


def _half(ref, c, rows):
    return ref.at[pl.ds(c * (rows // 2), rows // 2), :]


_HBM = pl.BlockSpec(memory_space=pltpu.HBM)
_SEM = pl.BlockSpec(memory_space=pltpu.SEMAPHORE)
_EFFECT = pltpu.SideEffectType.DATAFLOW_SIDE_EFFECTING


def _ici_start(name, srcs, land_shapes, plan, per_source=3, after=None):
    ns, nl = len(srcs), len(land_shapes)
    n_copies = per_source * ns
    n_in = ns + nl + (after is not None)

    def body(*refs):
        src_refs, land_refs = refs[:ns], refs[ns:ns + nl]
        send_sems, recv_sems = refs[n_in], refs[n_in + 1]
        token = refs[-1]
        for n, (src, dst, peer, _) in enumerate(plan(src_refs, land_refs)):
            pltpu.make_async_remote_copy(src_ref=src, dst_ref=dst, send_sem=send_sems.at[n], recv_sem=recv_sems.at[n],
                                         device_id=peer, device_id_type=MESH).start()
        token[...] = jnp.zeros_like(token)

    lands = [lax.empty(s.shape, s.dtype) for s in land_shapes]
    out = pl.pallas_call(
        body, name=name,
        out_shape=(pltpu.SemaphoreType.DMA((n_copies,)), pltpu.SemaphoreType.DMA((n_copies,)),
                   *[pltpu.HBM(a.shape, a.dtype) for a in list(srcs) + lands], jax.ShapeDtypeStruct((8, LANES), F32)),
        in_specs=[_HBM] * (ns + nl) + [_ANY] * (after is not None),
        out_specs=(_SEM, _SEM, *[_HBM] * (ns + nl), pl.BlockSpec(memory_space=pltpu.VMEM)),
        input_output_aliases={n: 2 + n for n in range(ns + nl)},
        compiler_params=pltpu.CompilerParams(has_side_effects=_EFFECT),
    )(*[pltpu.with_memory_space_constraint(a, pltpu.HBM) for a in list(srcs) + lands],
      *([] if after is None else [after]))
    return out[0], out[1], list(out[2:2 + ns]), list(out[2 + ns:2 + ns + nl]), out[-1]


def _ici_wait(name, send_sems, recv_sems, srcs, lands, plan, after):
    ns, nl = len(srcs), len(lands)
    after = list(after) if isinstance(after, (list, tuple)) else [after]

    def body(*refs):
        src_refs, land_refs = refs[:ns], refs[ns:ns + nl]
        send_sems, recv_sems = refs[ns + nl], refs[ns + nl + 1]
        for n, (src, _, peer, mine) in enumerate(plan(src_refs, land_refs)):
            cp = pltpu.make_async_remote_copy(src_ref=src, dst_ref=mine, send_sem=send_sems.at[n],
                                              recv_sem=recv_sems.at[n], device_id=peer, device_id_type=MESH)
            cp.wait_send()
            cp.wait_recv()

    out = pl.pallas_call(
        body, name=name, out_shape=[pltpu.HBM(a.shape, a.dtype) for a in list(srcs) + list(lands)],
        in_specs=[_HBM] * (ns + nl) + [_SEM, _SEM] + [_ANY] * len(after), out_specs=[_HBM] * (ns + nl),
        input_output_aliases={n: n for n in range(ns + nl)},
        compiler_params=pltpu.CompilerParams(has_side_effects=_EFFECT),
    )(*srcs, *lands, send_sems, recv_sems, *after)
    return list(out[:ns]), list(out[ns:])


def _own_slab(name, chip, w, after):
    R, C = w.shape
    tr, tc = _tiles(R, C)
    tied = [] if after is None else [after]

    def body(chip_ref, w_ref, *rest):
        stack_ref, token_ref = rest[-2:]
        stack_ref[0] = w_ref[...].astype(BF16)
        token_ref[...] = jnp.zeros_like(token_ref)

    small = pl.BlockSpec((8, LANES), lambda r, q, chip_ref: (0, 0))
    grid_spec = pltpu.PrefetchScalarGridSpec(
        num_scalar_prefetch=1, grid=(R // tr, C // tc),
        in_specs=[pl.BlockSpec((tr, tc), lambda r, q, chip_ref: (r, q))] + [small] * len(tied),
        out_specs=[pl.BlockSpec((1, tr, tc), lambda r, q, chip_ref: (chip_ref[0], r, q)), small])
    return pl.pallas_call(
        body, name=name, grid_spec=grid_spec,
        out_shape=[jax.ShapeDtypeStruct((N_CHIPS, R, C), BF16), jax.ShapeDtypeStruct((8, LANES), F32)],
        compiler_params=pltpu.CompilerParams(dimension_semantics=("arbitrary", "arbitrary")),
    )(chip, w, *tied)


def _gather_plan(src_refs, land_refs):
    x, y, c = _place()
    copies = []
    for stack in src_refs:
        R = stack.shape[1]
        own = _half(stack.at[2 * x + y], c, R)
        for cx, cy in [(1 - x, y), (x, 1 - y), (1 - x, 1 - y)]:
            copies.append((own, own, (cx, cy, c), _half(stack.at[2 * cx + cy], c, R)))
    return copies


def _pass_plan(src_refs, land_refs):
    x, y, c = _place()
    copies = []
    for land in src_refs:
        R = land.shape[1]
        for cx, cy in [(1 - x, y), (x, 1 - y), (1 - x, 1 - y)]:
            slot = land.at[2 * cx + cy]
            copies.append((_half(slot, c, R), _half(slot, c, R), (x, y, 1 - c), _half(slot, 1 - c, R)))
    return copies


def _share_plan(src_refs, land_refs):
    x, y, c = _place()
    return [(h, land, (x, y, 1 - c), land) for h, land in zip(src_refs, land_refs)]


def _pass_to_sibling(name, lands):
    nw = len(lands)

    def body(*refs):
        ins, outs = refs[:nw], refs[nw:2 * nw]
        send_sems, recv_sems = refs[2 * nw:]
        x, y, c = _place()
        chips = [(1 - x, y), (x, 1 - y), (1 - x, 1 - y)]
        copies = []
        for k in range(nw):
            R = ins[k].shape[1]
            for j, (cx, cy) in enumerate(chips):
                cp = pltpu.make_async_remote_copy(
                    src_ref=_half(ins[k].at[2 * cx + cy], c, R), dst_ref=_half(outs[k].at[2 * cx + cy], c, R),
                    send_sem=send_sems.at[3 * k + j], recv_sem=recv_sems.at[3 * k + j],
                    device_id=(x, y, 1 - c), device_id_type=MESH)
                cp.start()
                copies.append(cp)
        for k in range(nw):
            R = ins[k].shape[1]
            for j, (cx, cy) in enumerate(chips):
                pltpu.make_async_remote_copy(
                    src_ref=_half(ins[k].at[2 * cx + cy], c, R), dst_ref=_half(outs[k].at[2 * cx + cy], 1 - c, R),
                    send_sem=send_sems.at[3 * k + j], recv_sem=recv_sems.at[3 * k + j],
                    device_id=(x, y, 1 - c), device_id_type=MESH).wait_recv()
        for cp in copies:
            cp.wait_send()

    return pl.pallas_call(
        body, name=name, out_shape=[jax.ShapeDtypeStruct(a.shape, a.dtype) for a in lands],
        in_specs=[_ANY] * nw, out_specs=[_ANY] * nw, input_output_aliases={k: k for k in range(nw)},
        scratch_shapes=[pltpu.SemaphoreType.DMA((3 * nw,)), pltpu.SemaphoreType.DMA((3 * nw,))],
    )(*lands)


def _tie(vec, token):
    return vec + token[0:1, 0:1]


ROW_ALIGN = 16
TILE_ELEMS = 512 * 1024


def _tiles(rows, cols):
    fits = [t for t in range(ROW_ALIGN, min(rows, 256) + 1, ROW_ALIGN) if rows % t == 0]
    tr = fits[-1] if fits and fits[-1] >= 64 else rows
    tc = cols
    while tr * tc > TILE_ELEMS and tc % (2 * LANES) == 0:
        tc //= 2
    return tr, tc


def _scatter_plan(src_refs, land_refs):
    x, y, c = _place()
    copies = []
    for p, land in zip(src_refs, land_refs):
        for j, (cx, cy) in enumerate([(1 - x, y), (x, 1 - y), (1 - x, 1 - y)]):
            copies.append((p.at[2 * cx + cy], land.at[j], (cx, cy, c), land.at[j]))
    return copies


def _chip_add(name, chip, sums, recv):
    _, H, C = sums.shape
    tr, tc = _tiles(H, C)

    def body(chip_ref, p_ref, r_ref, o_ref):
        total = p_ref[0].astype(F32)
        for j in range(3):
            total = total + r_ref[j].astype(F32)
        o_ref[...] = total

    grid_spec = pltpu.PrefetchScalarGridSpec(
        num_scalar_prefetch=1, grid=(H // tr, C // tc),
        in_specs=[pl.BlockSpec((1, tr, tc), lambda r, q, chip_ref: (chip_ref[0], r, q)),
                  pl.BlockSpec((3, tr, tc), lambda r, q, chip_ref: (0, r, q))],
        out_specs=pl.BlockSpec((tr, tc), lambda r, q, chip_ref: (r, q)))
    return pl.pallas_call(
        body, name=name, grid_spec=grid_spec, out_shape=jax.ShapeDtypeStruct((H, C), F32),
        compiler_params=pltpu.CompilerParams(dimension_semantics=("parallel", "parallel")),
    )(chip, sums, recv)


def _pair_share(name, halves):
    nw = len(halves)

    def body(*refs):
        hs, outs = refs[:nw], refs[nw:2 * nw]
        send_sems, recv_sems = refs[2 * nw:]
        x, y, c = _place()
        copies = []
        for k in range(nw):
            cp = pltpu.make_async_remote_copy(
                src_ref=hs[k], dst_ref=outs[k], send_sem=send_sems.at[k], recv_sem=recv_sems.at[k],
                device_id=(x, y, 1 - c), device_id_type=MESH)
            cp.start()
            copies.append(cp)
        for cp in copies:
            cp.wait()

    return pl.pallas_call(
        body, name=name,
        out_shape=[jax.ShapeDtypeStruct(h.shape, h.dtype) for h in halves],
        in_specs=[_ANY] * nw, out_specs=[_ANY] * nw,
        scratch_shapes=[pltpu.SemaphoreType.DMA((nw,)), pltpu.SemaphoreType.DMA((nw,))],
    )(*halves)


def _adam_halves(name, core, w, g_own, g_other, m, v):
    R, C = w.shape
    H = R // 2
    tr, tc = _tiles(H, C)
    nr, nc = H // tr, C // tc

    def body(core_ref, w_ref, go_ref, gr_ref, m_ref, v_ref, g_ref, dl_ref, nm_ref, nv_ref):
        own = (pl.program_id(0) // nr) == core_ref[0]
        g = jnp.where(own, go_ref[...], gr_ref[...])
        g_ref[...] = g
        dl_ref[...], nm_ref[...], nv_ref[...] = _adamw(w_ref[...], g, m_ref[...], v_ref[...])

    blk = pl.BlockSpec((tr, tc), lambda r, q, core_ref: (r, q))

    def half_spec(is_own):
        def index(r, q, core_ref):
            mine = ((r // nr) == core_ref[0]) == is_own
            done = is_own == (core_ref[0] == 0)
            return (jnp.where(mine, r % nr, jnp.where(done, nr - 1, 0)), jnp.where(mine, q, jnp.where(done, nc - 1, 0)))
        return pl.BlockSpec((tr, tc), index)
    out = jax.ShapeDtypeStruct((R, C), F32)
    grid_spec = pltpu.PrefetchScalarGridSpec(
        num_scalar_prefetch=1, grid=(R // tr, nc), in_specs=[blk, half_spec(True), half_spec(False), blk, blk],
        out_specs=[blk] * 4)
    return pl.pallas_call(
        body, name=name, grid_spec=grid_spec, out_shape=[out] * 4,
        compiler_params=pltpu.CompilerParams(dimension_semantics=("parallel", "parallel"),
                                             vmem_limit_bytes=_vmem(20 * tr * tc * 4)),
    )(core, w, g_own, g_other, m, v)


def kernel(x, c, w_mod, b_mod, g_pre_mix, g_post_mix, w_in, b_forget, swa_sinks, w_out, g_pre_mlp, g_post_mlp, w_up, w_down, loss_target, m_w_mod, m_b_mod, m_g_pre_mix, m_g_post_mix, m_w_in, m_b_forget, m_swa_sinks, m_w_out, m_g_pre_mlp, m_g_post_mlp, m_w_up, m_w_down, v_w_mod, v_b_mod, v_g_pre_mix, v_g_post_mix, v_w_in, v_b_forget, v_swa_sinks, v_w_out, v_g_pre_mlp, v_g_post_mlp, v_w_up, v_w_down):
    S, D = x.shape[1], x.shape[2]
    n_heads = D // HEAD_DIM
    n_fox = n_heads // 2
    n_swa = n_heads - n_fox
    n_kv = max(1, n_swa // 4)
    fox_w, swa_w, kv_w = n_fox * HEAD_DIM, n_swa * HEAD_DIM, n_kv * HEAD_DIM
    main_w = 3 * fox_w + swa_w + 2 * kv_w
    in_w = main_w + n_fox
    mod_cols = w_mod.shape[2]

    ax, ay, ac = _place()
    chip = 2 * ax + ay
    dev = 2 * chip + ac
    chip_arr = jnp.reshape(chip, (1,)).astype(jnp.int32)
    core_arr = jnp.reshape(ac, (1,)).astype(jnp.int32)

    x2, tgt = x[0], loss_target[0]

    in_rows = in_w // N_CHIPS
    in_rows_pad = -(-in_rows // (2 * LANES)) * (2 * LANES)
    slab_w = N_CHIPS * in_rows_pad

    def rows_of(a):
        return jnp.pad(a[0].T, ((0, in_rows_pad - in_rows), (0, 0)))

    w_in_stack, token = _own_slab("own_slab_w_in", chip_arr, rows_of(w_in), None)

    c_all, _ = _allgather8("gather_c", _tie(c, token).reshape(8, D // 8))
    c_all = c_all.reshape(N_DEV, D)
    b_shard = lax.dynamic_slice_in_dim(b_mod, chip * mod_cols, mod_cols, axis=1)
    mod_shard = _mod_fwd(jnp.pad(c_all, ((0, 16 - N_DEV), (0, 0))), w_mod[0], b_shard)[:N_DEV]
    mod_all, token = _allgather8("gather_mod", mod_shard)
    mod_all = mod_all.reshape(N_CHIPS, 2, N_DEV, mod_cols)[:, 0]
    mod = lax.dynamic_index_in_dim(mod_all, dev, axis=1, keepdims=False).reshape(N_MOD, 1, D)
    sh_a, sc_a, gt_a, sh_m, sc_m, gt_m = [mod[n] for n in range(N_MOD)]

    def slab_cols(lo, hi):
        spans = []
        while lo < hi:
            s, r = divmod(lo, in_rows)
            n = min(hi - lo, in_rows - r)
            spans.append((s * in_rows_pad + r, s * in_rows_pad + r + n))
            lo += n
        return spans

    gate_lo = 3 * fox_w
    main_spans = slab_cols(0, gate_lo) + slab_cols(gate_lo + n_fox, in_w)
    (gate_first, gate_last), = slab_cols(gate_lo, gate_lo + n_fox)

    names = ["w_in", "w_out", "w_up", "w_down"]
    flights = {}
    for n, w in zip(names, [None, w_out[0], w_up[0], w_down[0]]):
        stack = w_in_stack if n == "w_in" else _own_slab("own_slab_" + n, chip_arr, w, token)[0]
        flights[n] = _ici_start("gather_start_" + n, [stack], [], _gather_plan, after=token)
        token = flights[n][4]
    sc_a = _tie(sc_a, token)

    def arrived(n, after):
        send, recv, stacks, _, _ = flights[n]
        stacks, _ = _ici_wait("gather_wait_" + n, send, recv, stacks, [], _gather_plan, after)
        return _ici_start("gather_pass_start_" + n, stacks, [], _pass_plan)

    def gathered(n, after, in_flight=None):
        if in_flight is None:
            send, recv, stacks, _, _ = flights[n]
            stacks, _ = _ici_wait("gather_wait_" + n, send, recv, stacks, [], _gather_plan, after)
            return _pass_to_sibling("gather_pass_" + n, stacks)[0]
        send, recv, stacks, _, _ = in_flight
        return _ici_wait("gather_pass_wait_" + n, send, recv, stacks, [], _pass_plan, after)[0][0]

    d_ff = N_CHIPS * w_up.shape[2]

    h = _pre_norm(x2, g_pre_mix, sc_a, sh_a)
    in_state = [rows_of(w_in)] + [rows_of(_tie(a, token)) for a in (m_w_in, v_w_in)]
    cos, sin_signed = _rope_tables(S)

    def pack(bm, gpm, gqm, gpl, gql, bf, sk):
        last = jnp.concatenate([bf, sk, jnp.zeros((1, D - n_fox - n_swa), F32)], axis=1)
        return jnp.concatenate([bm.reshape(N_MOD, D), gpm, gqm, gpl, gql, last, jnp.zeros((5, D), F32)], axis=0)

    small_state = [pack(b_mod, g_pre_mix, g_post_mix, g_pre_mlp, g_post_mlp, b_forget, swa_sinks),
                   pack(m_b_mod, m_g_pre_mix, m_g_post_mix, m_g_pre_mlp, m_g_post_mlp, m_b_forget, m_swa_sinks),
                   pack(v_b_mod, v_g_pre_mix, v_g_post_mix, v_g_pre_mlp, v_g_post_mlp, v_b_forget, v_swa_sinks)]
    ready = h[:8, :LANES].astype(F32) + cos[:8]
    w_slab_t = gathered("w_in", [ready] + in_state[1:] + small_state).reshape(slab_w, D)
    tm_p, tn_p = _fit(MM_TM, S), _fit(MM_TN if slab_w % MM_TN == 0 else MM_TN // 2, slab_w)
    win0 = gate_first // LANES * LANES
    win_j, win_off = divmod(win0, tn_p)
    assert win_off + 2 * LANES <= tn_p and gate_last - win0 <= 2 * LANES

    def proj_epilogue(acc, ex, outs):
        outs[0][...] = acc.astype(BF16)

        @pl.when(pl.program_id(1) == win_j)
        def _():
            outs[1][...] = acc[:, win_off:win_off + 2 * LANES]

    proj_slab, gate_win = _matmul(
        "in_proj", h, w_slab_t, "nt",
        [((S, slab_w), BF16, (tm_p, tn_p), lambda i, j: (i, j)), ((S, 2 * LANES), F32, (tm_p, 2 * LANES), lambda i, j: (i, 0))],
        proj_epilogue, tn=tn_p, revisits=True)
    proj = jnp.concatenate([proj_slab[:, lo:hi] for lo, hi in main_spans], axis=1)
    out_flight = arrived("w_out", proj_slab)
    fg = _tie(jnp.pad(gate_win[:, gate_first - win0:gate_last - win0], ((0, 0), (0, LANES - n_fox))), out_flight[4])
    b_pad = jnp.pad(b_forget, ((0, 0), (0, LANES - n_fox)))
    cum_row = _fox_gate_fwd(fg, b_pad)[:n_fox].reshape(n_fox, 1, S)
    fox_o, fox_lse = _fox_fwd(proj, cum_row, n_fox)

    rq = _rope("rope_fwd", proj, 3 * n_fox, n_swa + n_kv, cos, sin_signed)
    v_first = 3 * n_fox + n_swa + n_kv
    sinks = swa_sinks[0]
    swa_o, swa_lse = _swa_fwd(rq, proj, v_first, sinks, n_swa, n_kv)

    mixcat = jnp.concatenate([fox_o, swa_o], axis=1).astype(BF16)
    up_flight = arrived("w_up", mixcat)
    w_out_f = gathered("w_out", mixcat, out_flight).reshape(D, D)
    mix = _mm_plain("out_proj", mixcat, w_out_f, "nn", BF16, after=up_flight[4])
    x1, h2 = _post_mix(x2, mix, g_post_mix, gt_a, g_pre_mlp, sc_m, sh_m)
    w_up_f = gathered("w_up", h2, up_flight)

    tm_u, tn_u = _fit(MM_TM, S), _fit(MM_TN, d_ff)

    def up_epilogue(acc, ex, outs):
        outs[0][...] = acc.astype(BF16)
        r = jnp.maximum(acc, 0.0)
        outs[1][...] = (r * r).astype(BF16)

    ublk = ((S, d_ff), BF16, (tm_u, tn_u), lambda i, j: (i, j))
    u, a = _matmul("mlp_up", h2, w_up_f, "nn", [ublk, ublk], up_epilogue)
    w_down_f = gathered("w_down", a).reshape(d_ff, D)
    y = _mm_plain("mlp_down", a, w_down_f, "nn", BF16)

    dy, dout, loss_part, acc_mlp_post = _loss_and_post_mlp_bwd(x1, y, tgt, g_post_mlp, gt_m)

    def du_epilogue(acc, ex, outs):
        outs[0][...] = (acc * (2.0 * jnp.maximum(ex[0][...].astype(F32), 0.0))).astype(BF16)

    du = _matmul("mlp_down_bwd", dy, w_down_f, "nt", [ublk], du_epilogue,
                 extras=[(u, (tm_u, tn_u), lambda i, j: (i, j))])[0]
    def pair_send(tag, part):
        return _ici_start("grad_pair_start_" + tag, [part], [jax.ShapeDtypeStruct(part.shape, BF16)], _share_plan,
                          per_source=1)

    def pair_recv(tag, flight, after):
        send, recv, srcs, lands, _ = flight
        return _ici_wait("grad_pair_wait_" + tag, send, recv, srcs, lands, _share_plan, after)[1][0]

    def scatter_start(tag, sums, after=None):
        return _ici_start("grad_scatter_start_" + tag, sums,
                          [jax.ShapeDtypeStruct((3,) + p.shape[1:], BF16) for p in sums], _scatter_plan, after=after)

    def scatter_finish(tag, flight, after):
        send, recv, srcs, lands, _ = flight
        sums, received = _ici_wait("grad_scatter_wait_" + tag, send, recv, srcs, lands, _scatter_plan, after)
        return [_chip_add("chip_add_%s_%d" % (tag, k), chip_arr, p, r) for k, (p, r) in enumerate(zip(sums, received))]

    tm_g = _fit(MM_TM, D // 2)
    pair_down = pair_send("down", _grad_half("grad_w_down_a", core_arr, a, dy, N_CHIPS, 1, tm_g, True))
    pair_up = pair_send("up", _grad_half("grad_w_up_a", core_arr, h2, du, 1, N_CHIPS, tm_g, True, after=pair_down[4]))
    sum_down = _grad_half("grad_w_down_b", core_arr, a, dy, N_CHIPS, 1, tm_g, False,
                          recv=pair_recv("down", pair_down, pair_up[4]))
    sum_up = _grad_half("grad_w_up_b", core_arr, h2, du, 1, N_CHIPS, tm_g, False, recv=pair_recv("up", pair_up, sum_down))
    flight_mlp = scatter_start("mlp", [sum_up, sum_down])
    dh2 = _mm_plain("mlp_up_bwd", du, w_up_f, "nt", BF16, after=flight_mlp[4])
    dx1, dmix, acc_mid = _pre_mlp_and_post_mix_bwd(dh2, x1, dout, mix, _tie(g_pre_mlp, flight_mlp[4]), sc_m,
                                                   g_post_mix, gt_a)

    dmixcat = _mm_plain("out_proj_bwd", dmix, w_out_f, "nt", F32)

    fdq, fdk, fdv, dcum_row, dcum_q = _fox_bwd(proj, fox_o, dmixcat, fox_lse, cum_row, n_fox)
    dcum_k = jnp.pad(dcum_row.reshape(n_fox, S), ((0, LANES - n_fox), (0, 0)))
    dfg, db_forget = _fox_gate_bwd(dcum_k, dcum_q, fg, b_pad)

    group_w = (n_swa // n_kv) * HEAD_DIM
    sdq, sdk, sdv, dsink = _swa_bwd(rq, proj, v_first, sinks, swa_o, dmixcat, fox_w // group_w, swa_lse, n_swa, n_kv)
    drq = jnp.concatenate([sdq, jnp.transpose(sdk, (1, 0, 2)).reshape(S, kv_w).astype(BF16)], axis=1)
    d_sq_sk = _rope("rope_bwd", drq, 0, n_swa + n_kv, cos, -sin_signed)
    dsv = jnp.transpose(sdv, (1, 0, 2)).reshape(S, kv_w).astype(BF16)
    dproj = jnp.concatenate([fdq, fdk, fdv, d_sq_sk, dsv], axis=1)

    pieces = []
    for s in range(N_CHIPS):
        lo, hi = s * in_rows, (s + 1) * in_rows
        for src, first, last, shift in [(dproj, 0, gate_lo, 0), (dfg, gate_lo, gate_lo + n_fox, gate_lo),
                                        (dproj, gate_lo + n_fox, in_w, n_fox)]:
            if max(lo, first) < min(hi, last):
                pieces.append(src[:, max(lo, first) - shift:min(hi, last) - shift])
        pieces.append(jnp.zeros((S, in_rows_pad - in_rows), BF16))
    dproj_slab = jnp.concatenate(pieces, axis=1)

    tm_in, tm_out = in_rows_pad // 2, D // (2 * N_CHIPS)
    pair_in = pair_send("in", _grad_half("grad_w_in_a", core_arr, dproj_slab, h, N_CHIPS, 1, tm_in, True))
    pair_out = pair_send("out", _grad_half("grad_w_out_a", core_arr, mixcat, dmix, N_CHIPS, 1, tm_out, True,
                                           after=pair_in[4]))
    sum_in = _grad_half("grad_w_in_b", core_arr, dproj_slab, h, N_CHIPS, 1, tm_in, False,
                        recv=pair_recv("in", pair_in, pair_out[4]))
    sum_out = _grad_half("grad_w_out_b", core_arr, mixcat, dmix, N_CHIPS, 1, tm_out, False,
                         recv=pair_recv("out", pair_out, sum_in[0, :8, :LANES]))
    dh = _mm_plain("in_proj_bwd", dproj_slab, w_slab_t, "nn", BF16, tk=slab_w // 2,
                   after=sum_out[0, :8, :LANES].astype(F32))
    grad_x, acc_pre = _pre_mix_bwd(dh, x2, dx1, g_pre_mix, sc_a)

    zero_row = jnp.zeros((1, D), F32)
    tail = jnp.concatenate([db_forget[0:1, :n_fox], dsink[:, 0, :n_swa // n_kv].reshape(1, n_swa),
                            loss_part[0:1, 0:1], jnp.zeros((1, D - n_fox - n_swa - 1), F32)], axis=1)
    partial = jnp.concatenate([
        acc_pre[0:1], acc_pre[1:2], acc_mid[3:4], acc_mid[0:1], acc_mid[1:2], acc_mlp_post[0:1],
        acc_pre[2:3], acc_mid[4:5], acc_mid[2:3], acc_mlp_post[1:2], tail] + [zero_row] * 5, axis=0)
    gathered_small, token = _allgather8("gather_small_grads", partial)

    flight_mix = scatter_start("mix", [sum_in, sum_out], after=token)
    halves_mlp = scatter_finish("mlp", flight_mlp, flight_mix[4])
    share_up, share_down = [
        _ici_start("grad_share_start_" + n, [hv], [jax.ShapeDtypeStruct(hv.shape, F32)], _share_plan, per_source=1)
        for n, hv in zip(["up", "down"], halves_mlp)]

    def shared(tag, flight, after):
        send, recv, own, lands, _ = flight
        own, other = _ici_wait("grad_share_wait_" + tag, send, recv, own, lands, _share_plan, after)
        return own[0], other[0]

    def unpack(p):
        return {"b_mod": p[0:N_MOD].reshape(1, N_MOD * D), "g_pre_mix": p[6:7], "g_post_mix": p[7:8],
                "g_pre_mlp": p[8:9], "g_post_mlp": p[9:10], "b_forget": p[10:11, :n_fox],
                "swa_sinks": p[10:11, n_fox:n_fox + n_swa]}

    small_out = _small_update(gathered_small, _tie(small_state[0], share_down[4] + share_up[4]), small_state[1],
                              small_state[2])
    g_small, d_small, m_small, v_small = [unpack(p) for p in small_out]
    loss = small_out[0][N_MOD + 4, n_fox + n_swa]

    dmod_all = gathered_small.reshape(N_DEV, 16, D)[:, :N_MOD].reshape(N_DEV, N_MOD * D)
    dmod_shard = _tie(lax.dynamic_slice_in_dim(dmod_all, chip * mod_cols, mod_cols, axis=1), share_down[4])
    g_w_mod, d_w_mod, nm_w_mod, nv_w_mod = _mod_update(c_all.T, dmod_shard, w_mod[0], m_w_mod[0], v_w_mod[0])

    grads = dict(g_small, w_mod=g_w_mod[None])
    deltas = dict(d_small, w_mod=d_w_mod[None])
    new_m = dict(m_small, w_mod=nm_w_mod[None])
    new_v = dict(v_small, w_mod=nv_w_mod[None])
    weights = {"w_in": (w_in, m_w_in, v_w_in), "w_out": (w_out, m_w_out, v_w_out), "w_up": (w_up, m_w_up, v_w_up),
               "w_down": (w_down, m_w_down, v_w_down)}

    def big_update(n, own, other):
        transposed = n == "w_in"
        w, m, v = in_state if transposed else [a[0] for a in weights[n]]
        outs = _adam_halves("adam_" + n, core_arr, w, own, other, m, v)
        if transposed:
            outs = [o[:in_rows].T for o in outs]
        grads[n], deltas[n], new_m[n], new_v[n] = [o[None] for o in outs]

    big_update("w_down", *shared("down", share_down, d_w_mod[:8, :LANES] + small_out[1][:8, :LANES]))
    halves_mix = scatter_finish("mix", flight_mix, deltas["w_down"][0, :8, :LANES] + d_w_mod[:8, :LANES])
    others_mix = _pair_share("grad_pair_share_mix", halves_mix)
    big_update("w_in", halves_mix[0], others_mix[0])
    big_update("w_out", halves_mix[1], others_mix[1])
    big_update("w_up", *shared("up", share_up, deltas["w_out"][0, :8, :LANES] + deltas["w_in"][0, :8, :LANES]))

    order = ["w_mod", "b_mod", "g_pre_mix", "g_post_mix", "w_in", "b_forget", "swa_sinks", "w_out", "g_pre_mlp",
             "g_post_mlp", "w_up", "w_down"]
    return (loss, grad_x[None], *[grads[n] for n in order], *[deltas[n] for n in order],
            *[new_m[n] for n in order], *[new_v[n] for n in order])
```

```python
import jax
import jax.numpy as jnp
from jax import lax
from jax.experimental import pallas as pl
from jax.experimental.pallas import tpu as pltpu

F32 = jnp.float32
BF16 = jnp.bfloat16
MESH = pl.DeviceIdType.MESH

HEAD_DIM = 128
SWA_BLOCK = 128
ROPE_THETA = 10000.0
NORM_EPS = 1e-6
NEG = -1e30
N_MOD = 6
ADAM_LR = 0.001
ADAM_B1 = 0.9
ADAM_B2 = 0.999
ADAM_EPS = 1e-08
ADAM_WD = 0.01
ADAM_STEP = 10
N_CHIPS = 4
N_DEV = 8
LANES = 128
VMEM_CAP = 60 * 1024 * 1024

_NN = (((1,), (0,)), ((), ()))
_NT = (((1,), (1,)), ((), ()))
_TN = (((0,), (0,)), ((), ()))


def _vmem(nbytes):
    return int(min(VMEM_CAP, nbytes * 5 // 4 + (4 << 20)))


def _nbytes(shape, dtype):
    n = 1
    for s in shape:
        n *= s
    return n * jnp.dtype(dtype).itemsize


def _fit(t, n):
    t = min(t, n)
    assert n % t == 0, (t, n)
    return t


MM_TM, MM_TN, MM_TK = 1024, 1024, 2048


def _matmul(name, a, b, mode, out_defs, epilogue, extras=(), tm=MM_TM, tn=MM_TN, tk=MM_TK, revisits=False,
            row_sel=None):
    stacked = b.ndim == 3
    b_rows, b_cols = b.shape[-2], b.shape[-1] * (b.shape[0] if stacked else 1)
    if mode == "nn":
        (M, K), (K2, N) = a.shape, (b_rows, b_cols)
    elif mode == "nt":
        (M, K), (N, K2) = a.shape, (b_rows, b_cols)
    else:
        (K, M), (K2, N) = a.shape, (b_rows, b_cols)
    assert K == K2 and not (stacked and mode == "tn"), (a.shape, b.shape, mode)
    tm = _fit(tm, M)
    tn = _fit(tn, b.shape[-1] if stacked and mode == "nn" else N)
    tk = _fit(tk, b.shape[-1] if stacked and mode == "nt" else K)
    nk = K // tk
    dims = {"nn": _NN, "nt": _NT, "tn": _TN}[mode]
    if row_sel is None:
        grid_m, a_row = M // tm, lambda i, *sel: i
    else:
        grid_m, a_row = row_sel[2], lambda i, *sel: row_sel[1](i, sel[0])
    a_spec = (pl.BlockSpec((tk, tm), lambda i, j, k, *sel: (k, a_row(i, *sel))) if mode == "tn"
              else pl.BlockSpec((tm, tk), lambda i, j, k, *sel: (a_row(i, *sel), k)))
    if stacked:
        per = b.shape[-1] // (tk if mode == "nt" else tn)
        b_spec = (pl.BlockSpec((1, tn, tk), lambda i, j, k, *sel: (k // per, j, k % per)) if mode == "nt"
                  else pl.BlockSpec((1, tk, tn), lambda i, j, k, *sel: (j // per, k, j % per)))
    else:
        b_spec = (pl.BlockSpec((tn, tk), lambda i, j, k, *sel: (j, k)) if mode == "nt"
                  else pl.BlockSpec((tk, tn), lambda i, j, k, *sel: (k, j)))
    n_ex, n_out = len(extras), len(out_defs)

    def body(*refs):
        if row_sel is not None:
            refs = refs[1:]
        a_ref, b_ref = refs[0], refs[1]
        ex = refs[2:2 + n_ex]
        outs = refs[2 + n_ex:2 + n_ex + n_out]
        b_blk = b_ref[0] if stacked else b_ref[...]
        prod = lax.dot_general(a_ref[...], b_blk, dims, preferred_element_type=F32)
        if nk == 1:
            epilogue(prod, ex, outs)
        else:
            acc_ref = refs[-1]
            k = pl.program_id(2)

            @pl.when(k == 0)
            def _():
                acc_ref[...] = prod

            @pl.when(k > 0)
            def _():
                acc_ref[...] += prod

            @pl.when(k == nk - 1)
            def _():
                epilogue(acc_ref[...], ex, outs)

    def wrap(f):
        return lambda i, j, k, *sel: f(i, j)

    in_specs = [a_spec, b_spec] + [pl.BlockSpec(blk, wrap(f)) for _, blk, f in extras]
    out_specs = [pl.BlockSpec(blk, wrap(f)) for _, _, blk, f in out_defs]
    out_shape = [jax.ShapeDtypeStruct(s, d) for s, d, _, _ in out_defs]
    need = 2 * (tm * tk + tk * tn) * a.dtype.itemsize + 3 * tm * tn * 4
    need += sum(2 * _nbytes(blk, arr.dtype) for arr, blk, _ in extras)
    need += sum(2 * _nbytes(blk, d) for _, d, blk, _ in out_defs)
    grid = (grid_m, N // tn, nk)
    scratch = [pltpu.VMEM((tm, tn), F32)] if nk > 1 else []
    params = pltpu.CompilerParams(
        dimension_semantics=("parallel", "arbitrary" if revisits else "parallel", "arbitrary"),
        vmem_limit_bytes=_vmem(need))
    operands = (a, b, *[arr for arr, _, _ in extras])
    if row_sel is None:
        return pl.pallas_call(body, name=name, grid=grid, in_specs=in_specs, out_specs=out_specs, out_shape=out_shape,
                              scratch_shapes=scratch, compiler_params=params)(*operands)
    grid_spec = pltpu.PrefetchScalarGridSpec(num_scalar_prefetch=1, grid=grid, in_specs=in_specs, out_specs=out_specs,
                                             scratch_shapes=scratch)
    return pl.pallas_call(body, name=name, grid_spec=grid_spec, out_shape=out_shape,
                          compiler_params=params)(row_sel[0], *operands)


def _grad_half(name, core, a, b, row_slabs, col_slabs, tm, other, recv=None, after=None):
    (_, M), (_, N) = a.shape, b.shape
    H = M // (2 * row_slabs)
    nh = H // tm
    tn = _fit(MM_TN, N // col_slabs)
    per = N // col_slabs // tn

    def a_block(i, core_ref):
        half = (1 - core_ref[0]) if other else core_ref[0]
        return (i // nh) * (2 * nh) + half * nh + i % nh

    def out_index(i, j):
        return (j // per, i, j % per) if col_slabs > 1 else (i // nh, i % nh, j)

    slabs = max(row_slabs, col_slabs)
    out_def = ((slabs, H, N // col_slabs), BF16, (1, tm, tn), out_index)

    def epilogue(acc, ex, outs):
        outs[0][0] = (acc if recv is None else acc + ex[0][0].astype(F32)).astype(BF16)

    extras = ([] if recv is None else [(recv, (1, tm, tn), out_index)]) + ([] if after is None else [_behind(after)])
    return _matmul(name, a, b, "tn", [out_def], epilogue, extras=extras, tm=tm, tn=tn,
                   row_sel=(core, a_block, row_slabs * nh))[0]


def _behind(token):
    return (token, (8, LANES), lambda i, j: (0, 0))


def _mm_plain(name, a, b, mode, out_dtype, after=None, **tiles):
    if mode == "nn":
        M, N = a.shape[0], b.shape[-1] * (b.shape[0] if b.ndim == 3 else 1)
    elif mode == "nt":
        M, N = a.shape[0], b.shape[-2]
    else:
        M, N = a.shape[1], b.shape[1]
    tm, tn = _fit(tiles.get("tm", MM_TM), M), _fit(tiles.get("tn", MM_TN), N)

    def epi(acc, ex, outs):
        outs[0][...] = acc.astype(out_dtype)

    return _matmul(name, a, b, mode, [((M, N), out_dtype, (tm, tn), lambda i, j: (i, j))], epi,
                   extras=[] if after is None else [_behind(after)], **tiles)[0]


def _rstd(v):
    return lax.rsqrt(jnp.mean(v * v, axis=-1, keepdims=True) + NORM_EPS)


ROW_TILE = 256


def _row_call(name, body, row_ins, vec_ins, row_outs, acc_outs, S, D):
    tr = _fit(ROW_TILE, S)
    row_spec = pl.BlockSpec((tr, D), lambda r: (r, 0))
    vec_spec = pl.BlockSpec((1, D), lambda r: (0, 0))
    in_specs = [row_spec] * len(row_ins) + [vec_spec] * len(vec_ins)
    out_specs = [row_spec] * len(row_outs) + [pl.BlockSpec(shp, lambda r: (0, 0)) for shp in acc_outs]
    out_shape = [jax.ShapeDtypeStruct((S, D), d) for d in row_outs] + [jax.ShapeDtypeStruct(shp, F32) for shp in acc_outs]
    need = sum(2 * tr * D * a.dtype.itemsize for a in row_ins) + sum(2 * tr * D * jnp.dtype(d).itemsize for d in row_outs)
    need += 8 * tr * D * 4
    return pl.pallas_call(
        body, name=name, grid=(S // tr,), in_specs=in_specs, out_specs=out_specs, out_shape=out_shape,
        compiler_params=pltpu.CompilerParams(dimension_semantics=("arbitrary",), vmem_limit_bytes=_vmem(need)),
    )(*row_ins, *vec_ins)


def _acc_rows(ref, rows):
    @pl.when(pl.program_id(0) == 0)
    def _():
        ref[...] = jnp.zeros_like(ref)
    for n, r in enumerate(rows):
        ref[n:n + 1, :] += r


def _pre_norm(x, g, sc, sh):
    S, D = x.shape

    def body(x_ref, g_ref, sc_ref, sh_ref, h_ref):
        xv = x_ref[...]
        xn = xv * _rstd(xv)
        h_ref[...] = (xn * g_ref[...] * (1.0 + sc_ref[...]) + sh_ref[...]).astype(BF16)

    return _row_call("pre_norm_mix", body, [x], [g, sc, sh], [BF16], [], S, D)[0]


def _post_mix(x, mix, g_post, gt, g_pre, sc, sh):
    S, D = x.shape

    def body(x_ref, mix_ref, gp_ref, gt_ref, g2_ref, sc_ref, sh_ref, x1_ref, h2_ref):
        mv = mix_ref[...].astype(F32)
        x1 = x_ref[...] + gt_ref[...] * (mv * _rstd(mv) * gp_ref[...])
        x1_ref[...] = x1
        h2_ref[...] = (x1 * _rstd(x1) * g2_ref[...] * (1.0 + sc_ref[...]) + sh_ref[...]).astype(BF16)

    return _row_call("post_mix_pre_mlp", body, [x, mix], [g_post, gt, g_pre, sc, sh], [F32, BF16], [], S, D)


def _loss_and_post_mlp_bwd(x1, y, target, g_post, gt):
    S, D = x1.shape

    def body(x1_ref, y_ref, t_ref, g_ref, gt_ref, dy_ref, dout_ref, loss_ref, acc_ref):
        yv = y_ref[...].astype(F32)
        r = _rstd(yv)
        yh = yv * r
        n = yh * g_ref[...]
        diff = x1_ref[...] + gt_ref[...] * n - t_ref[...]
        dout = diff * (1.0 / D)
        dout_ref[...] = dout
        dn = dout * gt_ref[...]
        dyh = dn * g_ref[...]
        dy_ref[...] = (r * (dyh - yh * jnp.mean(dyh * yh, axis=-1, keepdims=True))).astype(BF16)
        _acc_rows(acc_ref, [jnp.sum(dout * n, axis=0, keepdims=True), jnp.sum(dn * yh, axis=0, keepdims=True)])

        @pl.when(pl.program_id(0) == 0)
        def _():
            loss_ref[...] = jnp.zeros_like(loss_ref)
        loss_ref[...] += jnp.full(loss_ref.shape, (0.5 / D) * jnp.sum(diff * diff), F32)

    return _row_call("loss_post_mlp_bwd", body, [x1, y, target], [g_post, gt], [BF16, F32],
                     [(8, LANES), (8, D)], S, D)


def _pre_mlp_and_post_mix_bwd(dh2, x1, dout, mix, g_pre, sc, g_post, gt):
    S, D = x1.shape

    def body(dh_ref, x1_ref, dout_ref, mix_ref, g_ref, sc_ref, gp_ref, gt_ref, dx1_ref, dmix_ref, acc_ref):
        dh = dh_ref[...].astype(F32)
        x1v = x1_ref[...]
        r3 = _rstd(x1v)
        xn = x1v * r3
        dxn = dh * (1.0 + sc_ref[...]) * g_ref[...]
        dx1 = dout_ref[...] + r3 * (dxn - xn * jnp.mean(dxn * xn, axis=-1, keepdims=True))
        dx1_ref[...] = dx1
        mv = mix_ref[...].astype(F32)
        r2 = _rstd(mv)
        mh = mv * r2
        dn = dx1 * gt_ref[...]
        dmh = dn * gp_ref[...]
        dmix_ref[...] = (r2 * (dmh - mh * jnp.mean(dmh * mh, axis=-1, keepdims=True))).astype(BF16)
        _acc_rows(acc_ref, [
            jnp.sum(dh, axis=0, keepdims=True),
            jnp.sum(dh * xn * g_ref[...], axis=0, keepdims=True),
            jnp.sum(dh * (1.0 + sc_ref[...]) * xn, axis=0, keepdims=True),
            jnp.sum(dx1 * mh * gp_ref[...], axis=0, keepdims=True),
            jnp.sum(dn * mh, axis=0, keepdims=True)])

    return _row_call("pre_mlp_post_mix_bwd", body, [dh2, x1, dout, mix], [g_pre, sc, g_post, gt], [F32, BF16],
                     [(8, D)], S, D)


def _pre_mix_bwd(dh, x, dx1, g_pre, sc):
    S, D = x.shape

    def body(dh_ref, x_ref, dx1_ref, g_ref, sc_ref, gx_ref, acc_ref):
        dhv = dh_ref[...].astype(F32)
        xv = x_ref[...]
        r = _rstd(xv)
        xn = xv * r
        dxn = dhv * (1.0 + sc_ref[...]) * g_ref[...]
        gx_ref[...] = dx1_ref[...] + r * (dxn - xn * jnp.mean(dxn * xn, axis=-1, keepdims=True))
        _acc_rows(acc_ref, [
            jnp.sum(dhv, axis=0, keepdims=True),
            jnp.sum(dhv * xn * g_ref[...], axis=0, keepdims=True),
            jnp.sum(dhv * (1.0 + sc_ref[...]) * xn, axis=0, keepdims=True)])

    return _row_call("pre_mix_bwd", body, [dh, x, dx1], [g_pre, sc], [F32], [(8, D)], S, D)


CUM_BLOCK = 256


def _tri(n, upper):
    r = lax.broadcasted_iota(jnp.int32, (n, n), 0)
    c = lax.broadcasted_iota(jnp.int32, (n, n), 1)
    return ((c >= r) if upper else (c <= r)).astype(F32)


def _fox_gate_fwd(fg, b_pad):
    S = fg.shape[0]
    cb = _fit(CUM_BLOCK, S)

    def body(fg_ref, b_ref, cumt_ref, cum_ref):
        low = _tri(cb, False)
        carry = jnp.zeros((1, LANES), F32)
        for n in range(S // cb):
            z = fg_ref[n * cb:(n + 1) * cb, :] + b_ref[...]
            logf = jnp.minimum(z, 0.0) - jnp.log(1.0 + jnp.exp(-jnp.abs(z)))
            blk = jnp.dot(low, logf, precision=lax.Precision.HIGHEST, preferred_element_type=F32) + carry
            cum_ref[n * cb:(n + 1) * cb, :] = blk
            carry = blk[cb - 1:cb, :]
        cumt_ref[...] = cum_ref[...].T

    return pl.pallas_call(
        body, name="fox_gate_fwd", out_shape=jax.ShapeDtypeStruct((LANES, S), F32),
        scratch_shapes=[pltpu.VMEM((S, LANES), F32)],
        compiler_params=pltpu.CompilerParams(vmem_limit_bytes=_vmem(6 * S * LANES * 4)),
    )(fg, b_pad)


def _fox_gate_bwd(dcum_k, dcum_q, fg, b_pad):
    S = fg.shape[0]
    n_fox = dcum_q.shape[0]
    cb = _fit(CUM_BLOCK, S)

    def body(dk_ref, dq_ref, fg_ref, b_ref, dfg_ref, db_ref, dc_ref):
        lane = lax.broadcasted_iota(jnp.int32, (S, LANES), 1)
        dc = dk_ref[...].T
        for h in range(n_fox):
            dc = dc + jnp.where(lane == h, dq_ref[h], 0.0)
        dc_ref[...] = dc
        up = _tri(cb, True)
        carry = jnp.zeros((1, LANES), F32)
        db = jnp.zeros((1, LANES), F32)
        for n in reversed(range(S // cb)):
            blk = jnp.dot(up, dc_ref[n * cb:(n + 1) * cb, :], precision=lax.Precision.HIGHEST,
                          preferred_element_type=F32) + carry
            carry = blk[0:1, :]
            z = fg_ref[n * cb:(n + 1) * cb, :] + b_ref[...]
            dfg = blk * (1.0 / (1.0 + jnp.exp(z)))
            dfg_ref[n * cb:(n + 1) * cb, :] = dfg.astype(BF16)
            db = db + jnp.sum(dfg, axis=0, keepdims=True)
        db_ref[...] = jnp.broadcast_to(db, db_ref.shape)

    return pl.pallas_call(
        body, name="fox_gate_bwd",
        out_shape=[jax.ShapeDtypeStruct((S, LANES), BF16), jax.ShapeDtypeStruct((8, LANES), F32)],
        scratch_shapes=[pltpu.VMEM((S, LANES), F32)],
        compiler_params=pltpu.CompilerParams(vmem_limit_bytes=_vmem((8 + 2 * n_fox) * S * LANES * 4)),
    )(dcum_k, dcum_q, fg, b_pad)


FOX_TILE = 256


LOG2E = 1.4426950408889634


def _fox_scores(q, k, ck2, masked, t):
    s = lax.dot_general(q, k, _NT, preferred_element_type=F32) * (HEAD_DIM ** -0.5 * LOG2E) - ck2
    if masked:
        row = lax.broadcasted_iota(jnp.int32, (t, t), 0)
        col = lax.broadcasted_iota(jnp.int32, (t, t), 1)
        s = jnp.where(col <= row, s, NEG)
    return s


def _fox_fwd(proj, cum_row, n_fox):
    S = proj.shape[0]
    t = _fit(FOX_TILE, S)
    nq = S // t

    def body(q_ref, k_ref, v_ref, ck_ref, o_ref, lse_ref):
        def q_block(qi, _):
            q0 = pl.multiple_of(qi * t, t)
            q = q_ref[pl.ds(q0, t), :]

            def kv_block(j, carry, masked):
                m, l, acc = carry
                k0 = pl.multiple_of(j * t, t)
                s = _fox_scores(q, k_ref[pl.ds(k0, t), :], ck_ref[0, :, pl.ds(k0, t)] * LOG2E, masked, t)
                m_new = jnp.maximum(m, jnp.max(s, axis=-1, keepdims=True))
                alpha = jnp.exp2(m - m_new)
                p = jnp.exp2(s - m_new)
                l = alpha * l + jnp.sum(p, axis=-1, keepdims=True)
                acc = alpha * acc + jnp.dot(p.astype(BF16), v_ref[pl.ds(k0, t), :], preferred_element_type=F32)
                return m_new, l, acc

            init = (jnp.full((t, 1), NEG, F32), jnp.zeros((t, 1), F32), jnp.zeros((t, HEAD_DIM), F32))
            carry = lax.fori_loop(0, qi, lambda j, cr: kv_block(j, cr, False), init)
            m, l, acc = kv_block(qi, carry, True)
            o_ref[pl.ds(q0, t), :] = acc / l
            lse_ref[0, pl.ds(q0, t), :] = jnp.broadcast_to(m + jnp.log(l) * LOG2E, (t, LANES))
            return 0

        lax.fori_loop(0, nq, q_block, 0)

    col = lambda off: pl.BlockSpec((S, HEAD_DIM), lambda h: (0, off + h))
    per_head = pl.BlockSpec((1, S, LANES), lambda h: (h, 0, 0))
    return pl.pallas_call(
        body, name="fox_fwd", grid=(n_fox,),
        in_specs=[col(0), col(n_fox), col(2 * n_fox), pl.BlockSpec((1, 1, S), lambda h: (h, 0, 0))],
        out_specs=[pl.BlockSpec((S, HEAD_DIM), lambda h: (0, h)), per_head],
        out_shape=[jax.ShapeDtypeStruct((S, n_fox * HEAD_DIM), F32), jax.ShapeDtypeStruct((n_fox, S, LANES), F32)],
        compiler_params=pltpu.CompilerParams(dimension_semantics=("parallel",),
                                             vmem_limit_bytes=_vmem(16 * S * HEAD_DIM * 4 + 12 * t * t * 4)),
    )(proj, proj, proj, cum_row)


def _fox_bwd(proj, o, do, lse_b, cum_row, n_fox):
    S = proj.shape[0]
    t = _fit(FOX_TILE, S)
    nq = S // t
    scale = HEAD_DIM ** -0.5

    def body(q_ref, k_ref, v_ref, o_ref, do_ref, lse_ref, ck_ref, dq_ref, dk_ref, dv_ref, dc_ref, dcq_ref,
             dq_acc, delta_ref):
        dq_acc[...] = jnp.zeros_like(dq_acc)
        dcq_ref[...] = jnp.zeros_like(dcq_ref)

        def delta_block(qi, _):
            q0 = pl.multiple_of(qi * t, t)
            d = jnp.sum(do_ref[pl.ds(q0, t), :] * o_ref[pl.ds(q0, t), :], axis=-1, keepdims=True)
            delta_ref[pl.ds(q0, t), :] = jnp.broadcast_to(d, (t, LANES))
            return 0

        lax.fori_loop(0, nq, delta_block, 0)

        def kv_block(j, _):
            k0 = pl.multiple_of(j * t, t)
            k = k_ref[pl.ds(k0, t), :]
            v = v_ref[pl.ds(k0, t), :]
            ck2 = ck_ref[0, :, pl.ds(k0, t)] * LOG2E

            def q_block(qi, carry, masked):
                dk, dv, dc = carry
                q0 = pl.multiple_of(qi * t, t)
                q = q_ref[pl.ds(q0, t), :]
                dov = do_ref[pl.ds(q0, t), :].astype(BF16)
                p = jnp.exp2(_fox_scores(q, k, ck2, masked, t) - lse_ref[0, pl.ds(q0, t), :][:, :1])
                dp = lax.dot_general(dov, v, _NT, preferred_element_type=F32)
                ds = p * (dp - delta_ref[pl.ds(q0, t), :][:, :1])
                dsb = ds.astype(BF16)
                dv = dv + lax.dot_general(p.astype(BF16), dov, _TN, preferred_element_type=F32)
                dk = dk + lax.dot_general(dsb, q, _TN, preferred_element_type=F32)
                dq_acc[pl.ds(q0, t), :] += jnp.dot(dsb, k, preferred_element_type=F32)
                dc = dc - jnp.sum(ds, axis=0, keepdims=True)
                dcq_ref[0, pl.ds(q0, t), :] += jnp.broadcast_to(jnp.sum(ds, axis=1, keepdims=True), (t, LANES))
                return dk, dv, dc

            init = (jnp.zeros((t, HEAD_DIM), F32), jnp.zeros((t, HEAD_DIM), F32), jnp.zeros((1, t), F32))
            carry = q_block(j, init, True)
            dk, dv, dc = lax.fori_loop(j + 1, nq, lambda qi, cr: q_block(qi, cr, False), carry)
            dk_ref[pl.ds(k0, t), :] = (dk * scale).astype(BF16)
            dv_ref[pl.ds(k0, t), :] = dv.astype(BF16)
            dc_ref[0, :, pl.ds(k0, t)] = dc
            return 0

        lax.fori_loop(0, nq, kv_block, 0)
        dq_ref[...] = (dq_acc[...] * scale).astype(BF16)

    col = lambda off: pl.BlockSpec((S, HEAD_DIM), lambda h: (0, off + h))
    per_head = pl.BlockSpec((1, S, LANES), lambda h: (h, 0, 0))
    row = pl.BlockSpec((1, 1, S), lambda h: (h, 0, 0))
    grad = jax.ShapeDtypeStruct((S, n_fox * HEAD_DIM), BF16)
    return pl.pallas_call(
        body, name="fox_bwd", grid=(n_fox,),
        in_specs=[col(0), col(n_fox), col(2 * n_fox), col(0), col(0), per_head, row],
        out_specs=[col(0), col(0), col(0), row, per_head],
        out_shape=[grad, grad, grad, jax.ShapeDtypeStruct((n_fox, 1, S), F32), jax.ShapeDtypeStruct((n_fox, S, LANES), F32)],
        scratch_shapes=[pltpu.VMEM((S, HEAD_DIM), F32), pltpu.VMEM((S, LANES), F32)],
        compiler_params=pltpu.CompilerParams(dimension_semantics=("parallel",),
                                             vmem_limit_bytes=_vmem(24 * S * HEAD_DIM * 4 + 16 * t * t * 4)),
    )(proj, proj, proj, o, do, lse_b, cum_row)


def _rope_tables(S):
    half = HEAD_DIM // 2
    inv_freq = 1.0 / (ROPE_THETA ** (jnp.arange(half, dtype=F32) * (2.0 / HEAD_DIM)))
    ang = jnp.arange(S).astype(F32)[:, None] * inv_freq[None, :]
    cos, sin = jnp.cos(ang), jnp.sin(ang)
    return jnp.concatenate([cos, cos], axis=-1), jnp.concatenate([-sin, sin], axis=-1)


def _rope(name, src, first_block, n_blocks, cos, sin_signed):
    S = src.shape[0]

    def body(x_ref, cos_ref, sin_ref, o_ref):
        xv = x_ref[...].astype(F32)
        o_ref[...] = (xv * cos_ref[...] + pltpu.roll(xv, HEAD_DIM // 2, 1) * sin_ref[...]).astype(BF16)

    table = pl.BlockSpec((S, HEAD_DIM), lambda n: (0, 0))
    return pl.pallas_call(
        body, name=name, grid=(n_blocks,),
        in_specs=[pl.BlockSpec((S, HEAD_DIM), lambda n: (0, first_block + n)), table, table],
        out_specs=pl.BlockSpec((S, HEAD_DIM), lambda n: (0, n)),
        out_shape=jax.ShapeDtypeStruct((S, n_blocks * HEAD_DIM), BF16),
        compiler_params=pltpu.CompilerParams(dimension_semantics=("parallel",),
                                             vmem_limit_bytes=_vmem(12 * S * HEAD_DIM * 4)),
    )(src, cos, sin_signed)


def _swa_tile(q_ref, kp_ref, kc_ref, n, group, scale):
    B = SWA_BLOCK
    qs = jnp.concatenate([q_ref[:, g * HEAD_DIM:(g + 1) * HEAD_DIM] for g in range(group)], axis=0)
    kcat = jnp.concatenate([kp_ref[...], kc_ref[...]], axis=0)
    s = lax.dot_general(qs, kcat, _NT, preferred_element_type=F32) * scale
    qi = lax.broadcasted_iota(jnp.int32, (group * B, 2 * B), 0) % B
    kj = lax.broadcasted_iota(jnp.int32, (group * B, 2 * B), 1)
    diff = qi + B - kj
    mask = (diff >= 0) & (diff < B) & ((n * B + kj - B) >= 0)
    return qs, kcat, jnp.where(mask, s, NEG)


def _swa_sink_col(sink_ref, kv, group):
    head = lax.broadcasted_iota(jnp.int32, (group * SWA_BLOCK, 1), 0) // SWA_BLOCK
    col = jnp.zeros((group * SWA_BLOCK, 1), F32)
    for g in range(group):
        col = jnp.where(head == g, sink_ref[kv * group + g], col)
    return col


def _swa_specs(n_kv, group, q_first, k_first, v_first):
    B = SWA_BLOCK
    prev = lambda n: jnp.maximum(n - 1, 0)
    return [
        pl.BlockSpec((B, group * HEAD_DIM), lambda kv, n: (n, q_first + kv)),
        pl.BlockSpec((B, HEAD_DIM), lambda kv, n: (prev(n), k_first + kv)),
        pl.BlockSpec((B, HEAD_DIM), lambda kv, n: (n, k_first + kv)),
        pl.BlockSpec((B, HEAD_DIM), lambda kv, n: (prev(n), v_first + kv)),
        pl.BlockSpec((B, HEAD_DIM), lambda kv, n: (n, v_first + kv)),
    ]


def _swa_fwd(rq, proj, v_first, sinks, n_q, n_kv):
    S = rq.shape[0]
    B = SWA_BLOCK
    group = n_q // n_kv
    scale = HEAD_DIM ** -0.5

    def body(q_ref, kp_ref, kc_ref, vp_ref, vc_ref, sink_ref, o_ref, lse_ref):
        kv, n = pl.program_id(0), pl.program_id(1)
        _, _, s = _swa_tile(q_ref, kp_ref, kc_ref, n, group, scale)
        sink = _swa_sink_col(sink_ref, kv, group)
        m = jnp.maximum(jnp.max(s, axis=-1, keepdims=True), sink)
        p = jnp.exp(s - m)
        denom = jnp.sum(p, axis=-1, keepdims=True) + jnp.exp(sink - m)
        vcat = jnp.concatenate([vp_ref[...], vc_ref[...]], axis=0)
        o = jnp.dot((p / denom).astype(BF16), vcat, preferred_element_type=F32)
        lse = m + jnp.log(denom)
        for g in range(group):
            o_ref[:, g * HEAD_DIM:(g + 1) * HEAD_DIM] = o[g * B:(g + 1) * B, :]
            lse_ref[0, :, g * LANES:(g + 1) * LANES] = jnp.broadcast_to(lse[g * B:(g + 1) * B, :], (B, LANES))

    specs = _swa_specs(n_kv, group, 0, n_q, v_first)
    q_blk = pl.BlockSpec((B, group * HEAD_DIM), lambda kv, n: (n, kv))
    return pl.pallas_call(
        body, name="swa_fwd", grid=(n_kv, S // B),
        in_specs=specs + [pl.BlockSpec(memory_space=pltpu.SMEM)],
        out_specs=[q_blk, pl.BlockSpec((1, B, group * LANES), lambda kv, n: (kv, n, 0))],
        out_shape=[jax.ShapeDtypeStruct((S, n_q * HEAD_DIM), F32), jax.ShapeDtypeStruct((n_kv, S, group * LANES), F32)],
        compiler_params=pltpu.CompilerParams(dimension_semantics=("parallel", "arbitrary")),
    )(rq, rq, rq, proj, proj, sinks)


def _swa_bwd(rq, proj, v_first, sinks, o, do, do_first, lse_b, n_q, n_kv):
    S = rq.shape[0]
    B = SWA_BLOCK
    group = n_q // n_kv
    scale = HEAD_DIM ** -0.5

    def body(q_ref, kp_ref, kc_ref, vp_ref, vc_ref, o_ref, do_ref, lse_ref, sink_ref,
             dq_ref, dk_ref, dv_ref, dsink_ref):
        kv, n = pl.program_id(0), pl.program_id(1)

        @pl.when(n == 0)
        def _():
            dk_ref[...] = jnp.zeros_like(dk_ref)
            dv_ref[...] = jnp.zeros_like(dv_ref)
            dsink_ref[...] = jnp.zeros_like(dsink_ref)

        qs, kcat, s = _swa_tile(q_ref, kp_ref, kc_ref, n, group, scale)
        sink = _swa_sink_col(sink_ref, kv, group)
        stack = lambda ref, w: jnp.concatenate([ref[:, g * w:(g + 1) * w] for g in range(group)], axis=0)
        lse = jnp.concatenate([lse_ref[0, :, g * LANES:g * LANES + 1] for g in range(group)], axis=0)
        do32 = stack(do_ref, HEAD_DIM)
        delta = jnp.sum(do32 * stack(o_ref, HEAD_DIM), axis=-1, keepdims=True)
        dov = do32.astype(BF16)
        p = jnp.exp(s - lse)
        vcat = jnp.concatenate([vp_ref[...], vc_ref[...]], axis=0)
        dp = lax.dot_general(dov, vcat, _NT, preferred_element_type=F32)
        ds = p * (dp - delta)
        dsb = ds.astype(BF16)
        dq = jnp.dot(dsb, kcat, preferred_element_type=F32) * scale
        for g in range(group):
            dq_ref[:, g * HEAD_DIM:(g + 1) * HEAD_DIM] = dq[g * B:(g + 1) * B, :].astype(BF16)
        dkcat = lax.dot_general(dsb, qs, _TN, preferred_element_type=F32) * scale
        dvcat = lax.dot_general(p.astype(BF16), dov, _TN, preferred_element_type=F32)
        prev0 = pl.multiple_of(jnp.maximum(n - 1, 0) * B, B)
        cur0 = pl.multiple_of(n * B, B)
        dk_ref[0, pl.ds(prev0, B), :] += dkcat[:B, :]
        dk_ref[0, pl.ds(cur0, B), :] += dkcat[B:, :]
        dv_ref[0, pl.ds(prev0, B), :] += dvcat[:B, :]
        dv_ref[0, pl.ds(cur0, B), :] += dvcat[B:, :]
        dsk = -jnp.exp(sink - lse) * delta
        lane = lax.broadcasted_iota(jnp.int32, (1, LANES), 1)
        row = jnp.zeros((1, LANES), F32)
        for g in range(group):
            row = row + jnp.where(lane == g, jnp.sum(dsk[g * B:(g + 1) * B, :]), 0.0)
        dsink_ref[0, 0:1, :] += row

    specs = _swa_specs(n_kv, group, 0, n_q, v_first)
    q_blk = pl.BlockSpec((B, group * HEAD_DIM), lambda kv, n: (n, kv))
    acc = pl.BlockSpec((1, S, HEAD_DIM), lambda kv, n: (kv, 0, 0))
    return pl.pallas_call(
        body, name="swa_bwd", grid=(n_kv, S // B),
        in_specs=specs + [q_blk, pl.BlockSpec((B, group * HEAD_DIM), lambda kv, n: (n, do_first + kv)),
                          pl.BlockSpec((1, B, group * LANES), lambda kv, n: (kv, n, 0)),
                          pl.BlockSpec(memory_space=pltpu.SMEM)],
        out_specs=[q_blk, acc, acc, pl.BlockSpec((1, 8, LANES), lambda kv, n: (kv, 0, 0))],
        out_shape=[jax.ShapeDtypeStruct((S, n_q * HEAD_DIM), BF16), jax.ShapeDtypeStruct((n_kv, S, HEAD_DIM), F32),
                   jax.ShapeDtypeStruct((n_kv, S, HEAD_DIM), F32), jax.ShapeDtypeStruct((n_kv, 8, LANES), F32)],
        compiler_params=pltpu.CompilerParams(dimension_semantics=("parallel", "arbitrary")),
    )(rq, rq, rq, proj, proj, o, do, lse_b, sinks)


def _adamw(w, g, m, v):
    m = ADAM_B1 * m + (1.0 - ADAM_B1) * g
    v = ADAM_B2 * v + (1.0 - ADAM_B2) * (g * g)
    m_hat = m / (1.0 - ADAM_B1 ** ADAM_STEP)
    v_hat = v / (1.0 - ADAM_B2 ** ADAM_STEP)
    delta = -ADAM_LR * (m_hat / (jnp.sqrt(v_hat) + ADAM_EPS) + ADAM_WD * w)
    return delta, m, v


def _mod_fwd(cond_in, w_mod, b_shard):
    R, D = cond_in.shape
    cols = w_mod.shape[1]
    tn = _fit(512, cols)

    def body(c_ref, w_ref, b_ref, o_ref):
        cv = c_ref[...]
        cond = (cv / (1.0 + jnp.exp(-cv))).astype(BF16)
        o_ref[...] = jnp.dot(cond, w_ref[...].astype(BF16), preferred_element_type=F32) + b_ref[...]

    return pl.pallas_call(
        body, name="mod_fwd", grid=(cols // tn,),
        in_specs=[pl.BlockSpec((R, D), lambda j: (0, 0)), pl.BlockSpec((D, tn), lambda j: (0, j)),
                  pl.BlockSpec((1, tn), lambda j: (0, j))],
        out_specs=pl.BlockSpec((R, tn), lambda j: (0, j)),
        out_shape=jax.ShapeDtypeStruct((R, cols), F32),
        compiler_params=pltpu.CompilerParams(dimension_semantics=("parallel",), vmem_limit_bytes=_vmem(3 * D * tn * 4)),
    )(cond_in, w_mod, b_shard)


def _mod_update(c_t, dmod, w, m, v):
    D, nb = c_t.shape
    cols = w.shape[1]
    tr = _fit(128, D)

    def body(c_ref, d_ref, w_ref, m_ref, v_ref, g_ref, dl_ref, nm_ref, nv_ref):
        cv = c_ref[...]
        cond = cv / (1.0 + jnp.exp(-cv))
        g = jnp.zeros((tr, cols), F32)
        for b in range(nb):
            g = g + cond[:, b:b + 1] * d_ref[b:b + 1, :]
        g_ref[...] = g
        dl_ref[...], nm_ref[...], nv_ref[...] = _adamw(w_ref[...], g, m_ref[...], v_ref[...])

    blk = pl.BlockSpec((tr, cols), lambda r: (r, 0))
    out = jax.ShapeDtypeStruct((D, cols), F32)
    return pl.pallas_call(
        body, name="mod_update", grid=(D // tr,),
        in_specs=[pl.BlockSpec((tr, nb), lambda r: (r, 0)), pl.BlockSpec((nb, cols), lambda r: (0, 0)), blk, blk, blk],
        out_specs=[blk] * 4, out_shape=[out] * 4,
        compiler_params=pltpu.CompilerParams(dimension_semantics=("parallel",), vmem_limit_bytes=_vmem(18 * tr * cols * 4)),
    )(c_t, dmod, w, m, v)


def _small_update(stacked, w, m, v):
    R, C = w.shape

    def body(s_ref, w_ref, m_ref, v_ref, g_ref, dl_ref, nm_ref, nv_ref):
        g = s_ref[0:R, :]
        for d in range(1, N_DEV):
            g = g + s_ref[d * R:(d + 1) * R, :]
        g_ref[...] = g
        dl_ref[...], nm_ref[...], nv_ref[...] = _adamw(w_ref[...], g, m_ref[...], v_ref[...])

    return pl.pallas_call(body, name="small_update", out_shape=[jax.ShapeDtypeStruct((R, C), F32)] * 4)(stacked, w, m, v)


def _place():
    return lax.axis_index("x"), lax.axis_index("y"), lax.axis_index("c")


def _allgather8(name, block):
    m_per, n = block.shape

    def body(x_ref, out_ref, token_ref, send_sems, recv_sems, local_sem):
        token_ref[...] = jnp.zeros_like(token_ref)
        x, y, c = _place()
        me, sibling = (x, y, c), (x, y, 1 - c)
        chips = [(1 - x, y), (x, 1 - y), (1 - x, 1 - y)]

        def rows(px, py, pc):
            return out_ref.at[pl.ds((4 * px + 2 * py + pc) * m_per, m_per), :]

        def copy(k, blk, to, src=None):
            return pltpu.make_async_remote_copy(
                src_ref=rows(*blk) if src is None else src, dst_ref=rows(*blk),
                send_sem=send_sems.at[k], recv_sem=recv_sems.at[k], device_id=to, device_id_type=MESH)

        mine = pltpu.make_async_copy(x_ref, rows(*me), local_sem)
        mine.start()
        first = [copy(0, me, sibling, src=x_ref)]
        first += [copy(1 + j, me, (*chip, c), src=x_ref) for j, chip in enumerate(chips)]
        for cp in first:
            cp.start()
        passed = [copy(4 + j, (*chip, c), sibling) for j, chip in enumerate(chips)]
        for j, chip in enumerate(chips):
            copy(1 + j, (*chip, c), me).wait_recv()
            passed[j].start()
        copy(0, sibling, me).wait_recv()
        for j, chip in enumerate(chips):
            copy(4 + j, (*chip, 1 - c), me).wait_recv()
        for cp in first + passed:
            cp.wait_send()
        mine.wait()

    vmem = pl.BlockSpec(memory_space=pltpu.VMEM)
    return pl.pallas_call(
        body, name=name,
        out_shape=[jax.ShapeDtypeStruct((N_DEV * m_per, n), block.dtype), jax.ShapeDtypeStruct((8, LANES), F32)],
        in_specs=[vmem], out_specs=[vmem, vmem],
        scratch_shapes=[pltpu.SemaphoreType.DMA((7,)), pltpu.SemaphoreType.DMA((7,)), pltpu.SemaphoreType.DMA],
    )(block)


_ANY = pl.BlockSpec(memory_space=pl.ANY)


def _half(ref, c, rows):
    return ref.at[pl.ds(c * (rows // 2), rows // 2), :]


_HBM = pl.BlockSpec(memory_space=pltpu.HBM)
_SEM = pl.BlockSpec(memory_space=pltpu.SEMAPHORE)
_EFFECT = pltpu.SideEffectType.DATAFLOW_SIDE_EFFECTING


def _ici_start(name, srcs, land_shapes, plan, per_source=3, after=None):
    ns, nl = len(srcs), len(land_shapes)
    n_copies = per_source * ns
    n_in = ns + nl + (after is not None)

    def body(*refs):
        src_refs, land_refs = refs[:ns], refs[ns:ns + nl]
        send_sems, recv_sems = refs[n_in], refs[n_in + 1]
        token = refs[-1]
        for n, (src, dst, peer, _) in enumerate(plan(src_refs, land_refs)):
            pltpu.make_async_remote_copy(src_ref=src, dst_ref=dst, send_sem=send_sems.at[n], recv_sem=recv_sems.at[n],
                                         device_id=peer, device_id_type=MESH).start()
        token[...] = jnp.zeros_like(token)

    lands = [lax.empty(s.shape, s.dtype) for s in land_shapes]
    out = pl.pallas_call(
        body, name=name,
        out_shape=(pltpu.SemaphoreType.DMA((n_copies,)), pltpu.SemaphoreType.DMA((n_copies,)),
                   *[pltpu.HBM(a.shape, a.dtype) for a in list(srcs) + lands], jax.ShapeDtypeStruct((8, LANES), F32)),
        in_specs=[_HBM] * (ns + nl) + [_ANY] * (after is not None),
        out_specs=(_SEM, _SEM, *[_HBM] * (ns + nl), pl.BlockSpec(memory_space=pltpu.VMEM)),
        input_output_aliases={n: 2 + n for n in range(ns + nl)},
        compiler_params=pltpu.CompilerParams(has_side_effects=_EFFECT),
    )(*[pltpu.with_memory_space_constraint(a, pltpu.HBM) for a in list(srcs) + lands],
      *([] if after is None else [after]))
    return out[0], out[1], list(out[2:2 + ns]), list(out[2 + ns:2 + ns + nl]), out[-1]


def _ici_wait(name, send_sems, recv_sems, srcs, lands, plan, after):
    ns, nl = len(srcs), len(lands)
    after = list(after) if isinstance(after, (list, tuple)) else [after]

    def body(*refs):
        src_refs, land_refs = refs[:ns], refs[ns:ns + nl]
        send_sems, recv_sems = refs[ns + nl], refs[ns + nl + 1]
        for n, (src, _, peer, mine) in enumerate(plan(src_refs, land_refs)):
            cp = pltpu.make_async_remote_copy(src_ref=src, dst_ref=mine, send_sem=send_sems.at[n],
                                              recv_sem=recv_sems.at[n], device_id=peer, device_id_type=MESH)
            cp.wait_send()
            cp.wait_recv()

    out = pl.pallas_call(
        body, name=name, out_shape=[pltpu.HBM(a.shape, a.dtype) for a in list(srcs) + list(lands)],
        in_specs=[_HBM] * (ns + nl) + [_SEM, _SEM] + [_ANY] * len(after), out_specs=[_HBM] * (ns + nl),
        input_output_aliases={n: n for n in range(ns + nl)},
        compiler_params=pltpu.CompilerParams(has_side_effects=_EFFECT),
    )(*srcs, *lands, send_sems, recv_sems, *after)
    return list(out[:ns]), list(out[ns:])


def _own_slab(name, chip, w, after):
    R, C = w.shape
    tr, tc = _tiles(R, C)
    tied = [] if after is None else [after]

    def body(chip_ref, w_ref, *rest):
        stack_ref, token_ref = rest[-2:]
        stack_ref[0] = w_ref[...].astype(BF16)
        token_ref[...] = jnp.zeros_like(token_ref)

    small = pl.BlockSpec((8, LANES), lambda r, q, chip_ref: (0, 0))
    grid_spec = pltpu.PrefetchScalarGridSpec(
        num_scalar_prefetch=1, grid=(R // tr, C // tc),
        in_specs=[pl.BlockSpec((tr, tc), lambda r, q, chip_ref: (r, q))] + [small] * len(tied),
        out_specs=[pl.BlockSpec((1, tr, tc), lambda r, q, chip_ref: (chip_ref[0], r, q)), small])
    return pl.pallas_call(
        body, name=name, grid_spec=grid_spec,
        out_shape=[jax.ShapeDtypeStruct((N_CHIPS, R, C), BF16), jax.ShapeDtypeStruct((8, LANES), F32)],
        compiler_params=pltpu.CompilerParams(dimension_semantics=("arbitrary", "arbitrary")),
    )(chip, w, *tied)


def _gather_plan(src_refs, land_refs):
    x, y, c = _place()
    copies = []
    for stack in src_refs:
        R = stack.shape[1]
        own = _half(stack.at[2 * x + y], c, R)
        for cx, cy in [(1 - x, y), (x, 1 - y), (1 - x, 1 - y)]:
            copies.append((own, own, (cx, cy, c), _half(stack.at[2 * cx + cy], c, R)))
    return copies


def _pass_plan(src_refs, land_refs):
    x, y, c = _place()
    copies = []
    for land in src_refs:
        R = land.shape[1]
        for cx, cy in [(1 - x, y), (x, 1 - y), (1 - x, 1 - y)]:
            slot = land.at[2 * cx + cy]
            copies.append((_half(slot, c, R), _half(slot, c, R), (x, y, 1 - c), _half(slot, 1 - c, R)))
    return copies


def _share_plan(src_refs, land_refs):
    x, y, c = _place()
    return [(h, land, (x, y, 1 - c), land) for h, land in zip(src_refs, land_refs)]


def _pass_to_sibling(name, lands):
    nw = len(lands)

    def body(*refs):
        ins, outs = refs[:nw], refs[nw:2 * nw]
        send_sems, recv_sems = refs[2 * nw:]
        x, y, c = _place()
        chips = [(1 - x, y), (x, 1 - y), (1 - x, 1 - y)]
        copies = []
        for k in range(nw):
            R = ins[k].shape[1]
            for j, (cx, cy) in enumerate(chips):
                cp = pltpu.make_async_remote_copy(
                    src_ref=_half(ins[k].at[2 * cx + cy], c, R), dst_ref=_half(outs[k].at[2 * cx + cy], c, R),
                    send_sem=send_sems.at[3 * k + j], recv_sem=recv_sems.at[3 * k + j],
                    device_id=(x, y, 1 - c), device_id_type=MESH)
                cp.start()
                copies.append(cp)
        for k in range(nw):
            R = ins[k].shape[1]
            for j, (cx, cy) in enumerate(chips):
                pltpu.make_async_remote_copy(
                    src_ref=_half(ins[k].at[2 * cx + cy], c, R), dst_ref=_half(outs[k].at[2 * cx + cy], 1 - c, R),
                    send_sem=send_sems.at[3 * k + j], recv_sem=recv_sems.at[3 * k + j],
                    device_id=(x, y, 1 - c), device_id_type=MESH).wait_recv()
        for cp in copies:
            cp.wait_send()

    return pl.pallas_call(
        body, name=name, out_shape=[jax.ShapeDtypeStruct(a.shape, a.dtype) for a in lands],
        in_specs=[_ANY] * nw, out_specs=[_ANY] * nw, input_output_aliases={k: k for k in range(nw)},
        scratch_shapes=[pltpu.SemaphoreType.DMA((3 * nw,)), pltpu.SemaphoreType.DMA((3 * nw,))],
    )(*lands)


def _tie(vec, token):
    return vec + token[0:1, 0:1]


ROW_ALIGN = 16
TILE_ELEMS = 512 * 1024


def _tiles(rows, cols):
    fits = [t for t in range(ROW_ALIGN, min(rows, 256) + 1, ROW_ALIGN) if rows % t == 0]
    tr = fits[-1] if fits and fits[-1] >= 64 else rows
    tc = cols
    while tr * tc > TILE_ELEMS and tc % (2 * LANES) == 0:
        tc //= 2
    return tr, tc


def _scatter_plan(src_refs, land_refs):
    x, y, c = _place()
    copies = []
    for p, land in zip(src_refs, land_refs):
        for j, (cx, cy) in enumerate([(1 - x, y), (x, 1 - y), (1 - x, 1 - y)]):
            copies.append((p.at[2 * cx + cy], land.at[j], (cx, cy, c), land.at[j]))
    return copies


def _chip_add(name, chip, sums, recv):
    _, H, C = sums.shape
    tr, tc = _tiles(H, C)

    def body(chip_ref, p_ref, r_ref, o_ref):
        total = p_ref[0].astype(F32)
        for j in range(3):
            total = total + r_ref[j].astype(F32)
        o_ref[...] = total

    grid_spec = pltpu.PrefetchScalarGridSpec(
        num_scalar_prefetch=1, grid=(H // tr, C // tc),
        in_specs=[pl.BlockSpec((1, tr, tc), lambda r, q, chip_ref: (chip_ref[0], r, q)),
                  pl.BlockSpec((3, tr, tc), lambda r, q, chip_ref: (0, r, q))],
        out_specs=pl.BlockSpec((tr, tc), lambda r, q, chip_ref: (r, q)))
    return pl.pallas_call(
        body, name=name, grid_spec=grid_spec, out_shape=jax.ShapeDtypeStruct((H, C), F32),
        compiler_params=pltpu.CompilerParams(dimension_semantics=("parallel", "parallel")),
    )(chip, sums, recv)


def _pair_share(name, halves):
    nw = len(halves)

    def body(*refs):
        hs, outs = refs[:nw], refs[nw:2 * nw]
        send_sems, recv_sems = refs[2 * nw:]
        x, y, c = _place()
        copies = []
        for k in range(nw):
            cp = pltpu.make_async_remote_copy(
                src_ref=hs[k], dst_ref=outs[k], send_sem=send_sems.at[k], recv_sem=recv_sems.at[k],
                device_id=(x, y, 1 - c), device_id_type=MESH)
            cp.start()
            copies.append(cp)
        for cp in copies:
            cp.wait()

    return pl.pallas_call(
        body, name=name,
        out_shape=[jax.ShapeDtypeStruct(h.shape, h.dtype) for h in halves],
        in_specs=[_ANY] * nw, out_specs=[_ANY] * nw,
        scratch_shapes=[pltpu.SemaphoreType.DMA((nw,)), pltpu.SemaphoreType.DMA((nw,))],
    )(*halves)


def _adam_halves(name, core, w, g_own, g_other, m, v):
    R, C = w.shape
    H = R // 2
    tr, tc = _tiles(H, C)
    nr, nc = H // tr, C // tc

    def body(core_ref, w_ref, go_ref, gr_ref, m_ref, v_ref, g_ref, dl_ref, nm_ref, nv_ref):
        own = (pl.program_id(0) // nr) == core_ref[0]
        g = jnp.where(own, go_ref[...], gr_ref[...])
        g_ref[...] = g
        dl_ref[...], nm_ref[...], nv_ref[...] = _adamw(w_ref[...], g, m_ref[...], v_ref[...])

    blk = pl.BlockSpec((tr, tc), lambda r, q, core_ref: (r, q))

    def half_spec(is_own):
        def index(r, q, core_ref):
            mine = ((r // nr) == core_ref[0]) == is_own
            done = is_own == (core_ref[0] == 0)
            return (jnp.where(mine, r % nr, jnp.where(done, nr - 1, 0)), jnp.where(mine, q, jnp.where(done, nc - 1, 0)))
        return pl.BlockSpec((tr, tc), index)
    out = jax.ShapeDtypeStruct((R, C), F32)
    grid_spec = pltpu.PrefetchScalarGridSpec(
        num_scalar_prefetch=1, grid=(R // tr, nc), in_specs=[blk, half_spec(True), half_spec(False), blk, blk],
        out_specs=[blk] * 4)
    return pl.pallas_call(
        body, name=name, grid_spec=grid_spec, out_shape=[out] * 4,
        compiler_params=pltpu.CompilerParams(dimension_semantics=("parallel", "parallel"),
                                             vmem_limit_bytes=_vmem(20 * tr * tc * 4)),
    )(core, w, g_own, g_other, m, v)


def kernel(x, c, w_mod, b_mod, g_pre_mix, g_post_mix, w_in, b_forget, swa_sinks, w_out, g_pre_mlp, g_post_mlp, w_up, w_down, loss_target, m_w_mod, m_b_mod, m_g_pre_mix, m_g_post_mix, m_w_in, m_b_forget, m_swa_sinks, m_w_out, m_g_pre_mlp, m_g_post_mlp, m_w_up, m_w_down, v_w_mod, v_b_mod, v_g_pre_mix, v_g_post_mix, v_w_in, v_b_forget, v_swa_sinks, v_w_out, v_g_pre_mlp, v_g_post_mlp, v_w_up, v_w_down):
    S, D = x.shape[1], x.shape[2]
    n_heads = D // HEAD_DIM
    n_fox = n_heads // 2
    n_swa = n_heads - n_fox
    n_kv = max(1, n_swa // 4)
    fox_w, swa_w, kv_w = n_fox * HEAD_DIM, n_swa * HEAD_DIM, n_kv * HEAD_DIM
    main_w = 3 * fox_w + swa_w + 2 * kv_w
    in_w = main_w + n_fox
    mod_cols = w_mod.shape[2]

    ax, ay, ac = _place()
    chip = 2 * ax + ay
    dev = 2 * chip + ac
    chip_arr = jnp.reshape(chip, (1,)).astype(jnp.int32)
    core_arr = jnp.reshape(ac, (1,)).astype(jnp.int32)

    x2, tgt = x[0], loss_target[0]

    in_rows = in_w // N_CHIPS
    in_rows_pad = -(-in_rows // (2 * LANES)) * (2 * LANES)
    slab_w = N_CHIPS * in_rows_pad

    def rows_of(a):
        return jnp.pad(a[0].T, ((0, in_rows_pad - in_rows), (0, 0)))

    w_in_stack, token = _own_slab("own_slab_w_in", chip_arr, rows_of(w_in), None)

    c_all, _ = _allgather8("gather_c", _tie(c, token).reshape(8, D // 8))
    c_all = c_all.reshape(N_DEV, D)
    b_shard = lax.dynamic_slice_in_dim(b_mod, chip * mod_cols, mod_cols, axis=1)
    mod_shard = _mod_fwd(jnp.pad(c_all, ((0, 16 - N_DEV), (0, 0))), w_mod[0], b_shard)[:N_DEV]
    mod_all, token = _allgather8("gather_mod", mod_shard)
    mod_all = mod_all.reshape(N_CHIPS, 2, N_DEV, mod_cols)[:, 0]
    mod = lax.dynamic_index_in_dim(mod_all, dev, axis=1, keepdims=False).reshape(N_MOD, 1, D)
    sh_a, sc_a, gt_a, sh_m, sc_m, gt_m = [mod[n] for n in range(N_MOD)]

    def slab_cols(lo, hi):
        spans = []
        while lo < hi:
            s, r = divmod(lo, in_rows)
            n = min(hi - lo, in_rows - r)
            spans.append((s * in_rows_pad + r, s * in_rows_pad + r + n))
            lo += n
        return spans

    gate_lo = 3 * fox_w
    main_spans = slab_cols(0, gate_lo) + slab_cols(gate_lo + n_fox, in_w)
    (gate_first, gate_last), = slab_cols(gate_lo, gate_lo + n_fox)

    names = ["w_in", "w_out", "w_up", "w_down"]
    flights = {}
    for n, w in zip(names, [None, w_out[0], w_up[0], w_down[0]]):
        stack = w_in_stack if n == "w_in" else _own_slab("own_slab_" + n, chip_arr, w, token)[0]
        flights[n] = _ici_start("gather_start_" + n, [stack], [], _gather_plan, after=token)
        token = flights[n][4]
    sc_a = _tie(sc_a, token)

    def arrived(n, after):
        send, recv, stacks, _, _ = flights[n]
        stacks, _ = _ici_wait("gather_wait_" + n, send, recv, stacks, [], _gather_plan, after)
        return _ici_start("gather_pass_start_" + n, stacks, [], _pass_plan)

    def gathered(n, after, in_flight=None):
        if in_flight is None:
            send, recv, stacks, _, _ = flights[n]
            stacks, _ = _ici_wait("gather_wait_" + n, send, recv, stacks, [], _gather_plan, after)
            return _pass_to_sibling("gather_pass_" + n, stacks)[0]
        send, recv, stacks, _, _ = in_flight
        return _ici_wait("gather_pass_wait_" + n, send, recv, stacks, [], _pass_plan, after)[0][0]

    d_ff = N_CHIPS * w_up.shape[2]

    h = _pre_norm(x2, g_pre_mix, sc_a, sh_a)
    in_state = [rows_of(w_in)] + [rows_of(_tie(a, token)) for a in (m_w_in, v_w_in)]
    cos, sin_signed = _rope_tables(S)

    def pack(bm, gpm, gqm, gpl, gql, bf, sk):
        last = jnp.concatenate([bf, sk, jnp.zeros((1, D - n_fox - n_swa), F32)], axis=1)
        return jnp.concatenate([bm.reshape(N_MOD, D), gpm, gqm, gpl, gql, last, jnp.zeros((5, D), F32)], axis=0)

    small_state = [pack(b_mod, g_pre_mix, g_post_mix, g_pre_mlp, g_post_mlp, b_forget, swa_sinks),
                   pack(m_b_mod, m_g_pre_mix, m_g_post_mix, m_g_pre_mlp, m_g_post_mlp, m_b_forget, m_swa_sinks),
                   pack(v_b_mod, v_g_pre_mix, v_g_post_mix, v_g_pre_mlp, v_g_post_mlp, v_b_forget, v_swa_sinks)]
    ready = h[:8, :LANES].astype(F32) + cos[:8]
    w_slab_t = gathered("w_in", [ready] + in_state[1:] + small_state).reshape(slab_w, D)
    tm_p, tn_p = _fit(MM_TM, S), _fit(MM_TN if slab_w % MM_TN == 0 else MM_TN // 2, slab_w)
    win0 = gate_first // LANES * LANES
    win_j, win_off = divmod(win0, tn_p)
    assert win_off + 2 * LANES <= tn_p and gate_last - win0 <= 2 * LANES

    def proj_epilogue(acc, ex, outs):
        outs[0][...] = acc.astype(BF16)

        @pl.when(pl.program_id(1) == win_j)
        def _():
            outs[1][...] = acc[:, win_off:win_off + 2 * LANES]

    proj_slab, gate_win = _matmul(
        "in_proj", h, w_slab_t, "nt",
        [((S, slab_w), BF16, (tm_p, tn_p), lambda i, j: (i, j)), ((S, 2 * LANES), F32, (tm_p, 2 * LANES), lambda i, j: (i, 0))],
        proj_epilogue, tn=tn_p, revisits=True)
    proj = jnp.concatenate([proj_slab[:, lo:hi] for lo, hi in main_spans], axis=1)
    out_flight = arrived("w_out", proj_slab)
    fg = _tie(jnp.pad(gate_win[:, gate_first - win0:gate_last - win0], ((0, 0), (0, LANES - n_fox))), out_flight[4])
    b_pad = jnp.pad(b_forget, ((0, 0), (0, LANES - n_fox)))
    cum_row = _fox_gate_fwd(fg, b_pad)[:n_fox].reshape(n_fox, 1, S)
    fox_o, fox_lse = _fox_fwd(proj, cum_row, n_fox)

    rq = _rope("rope_fwd", proj, 3 * n_fox, n_swa + n_kv, cos, sin_signed)
    v_first = 3 * n_fox + n_swa + n_kv
    sinks = swa_sinks[0]
    swa_o, swa_lse = _swa_fwd(rq, proj, v_first, sinks, n_swa, n_kv)

    mixcat = jnp.concatenate([fox_o, swa_o], axis=1).astype(BF16)
    up_flight = arrived("w_up", mixcat)
    w_out_f = gathered("w_out", mixcat, out_flight).reshape(D, D)
    mix = _mm_plain("out_proj", mixcat, w_out_f, "nn", BF16, after=up_flight[4])
    x1, h2 = _post_mix(x2, mix, g_post_mix, gt_a, g_pre_mlp, sc_m, sh_m)
    w_up_f = gathered("w_up", h2, up_flight)

    tm_u, tn_u = _fit(MM_TM, S), _fit(MM_TN, d_ff)

    def up_epilogue(acc, ex, outs):
        outs[0][...] = acc.astype(BF16)
        r = jnp.maximum(acc, 0.0)
        outs[1][...] = (r * r).astype(BF16)

    ublk = ((S, d_ff), BF16, (tm_u, tn_u), lambda i, j: (i, j))
    u, a = _matmul("mlp_up", h2, w_up_f, "nn", [ublk, ublk], up_epilogue)
    w_down_f = gathered("w_down", a).reshape(d_ff, D)
    y = _mm_plain("mlp_down", a, w_down_f, "nn", BF16)

    dy, dout, loss_part, acc_mlp_post = _loss_and_post_mlp_bwd(x1, y, tgt, g_post_mlp, gt_m)

    def du_epilogue(acc, ex, outs):
        outs[0][...] = (acc * (2.0 * jnp.maximum(ex[0][...].astype(F32), 0.0))).astype(BF16)

    du = _matmul("mlp_down_bwd", dy, w_down_f, "nt", [ublk], du_epilogue,
                 extras=[(u, (tm_u, tn_u), lambda i, j: (i, j))])[0]
    def pair_send(tag, part):
        return _ici_start("grad_pair_start_" + tag, [part], [jax.ShapeDtypeStruct(part.shape, BF16)], _share_plan,
                          per_source=1)

    def pair_recv(tag, flight, after):
        send, recv, srcs, lands, _ = flight
        return _ici_wait("grad_pair_wait_" + tag, send, recv, srcs, lands, _share_plan, after)[1][0]

    def scatter_start(tag, sums, after=None):
        return _ici_start("grad_scatter_start_" + tag, sums,
                          [jax.ShapeDtypeStruct((3,) + p.shape[1:], BF16) for p in sums], _scatter_plan, after=after)

    def scatter_finish(tag, flight, after):
        send, recv, srcs, lands, _ = flight
        sums, received = _ici_wait("grad_scatter_wait_" + tag, send, recv, srcs, lands, _scatter_plan, after)
        return [_chip_add("chip_add_%s_%d" % (tag, k), chip_arr, p, r) for k, (p, r) in enumerate(zip(sums, received))]

    tm_g = _fit(MM_TM, D // 2)
    pair_down = pair_send("down", _grad_half("grad_w_down_a", core_arr, a, dy, N_CHIPS, 1, tm_g, True))
    pair_up = pair_send("up", _grad_half("grad_w_up_a", core_arr, h2, du, 1, N_CHIPS, tm_g, True, after=pair_down[4]))
    sum_down = _grad_half("grad_w_down_b", core_arr, a, dy, N_CHIPS, 1, tm_g, False,
                          recv=pair_recv("down", pair_down, pair_up[4]))
    sum_up = _grad_half("grad_w_up_b", core_arr, h2, du, 1, N_CHIPS, tm_g, False, recv=pair_recv("up", pair_up, sum_down))
    flight_mlp = scatter_start("mlp", [sum_up, sum_down])
    dh2 = _mm_plain("mlp_up_bwd", du, w_up_f, "nt", BF16, after=flight_mlp[4])
    dx1, dmix, acc_mid = _pre_mlp_and_post_mix_bwd(dh2, x1, dout, mix, _tie(g_pre_mlp, flight_mlp[4]), sc_m,
                                                   g_post_mix, gt_a)

    dmixcat = _mm_plain("out_proj_bwd", dmix, w_out_f, "nt", F32)

    fdq, fdk, fdv, dcum_row, dcum_q = _fox_bwd(proj, fox_o, dmixcat, fox_lse, cum_row, n_fox)
    dcum_k = jnp.pad(dcum_row.reshape(n_fox, S), ((0, LANES - n_fox), (0, 0)))
    dfg, db_forget = _fox_gate_bwd(dcum_k, dcum_q, fg, b_pad)

    group_w = (n_swa // n_kv) * HEAD_DIM
    sdq, sdk, sdv, dsink = _swa_bwd(rq, proj, v_first, sinks, swa_o, dmixcat, fox_w // group_w, swa_lse, n_swa, n_kv)
    drq = jnp.concatenate([sdq, jnp.transpose(sdk, (1, 0, 2)).reshape(S, kv_w).astype(BF16)], axis=1)
    d_sq_sk = _rope("rope_bwd", drq, 0, n_swa + n_kv, cos, -sin_signed)
    dsv = jnp.transpose(sdv, (1, 0, 2)).reshape(S, kv_w).astype(BF16)
    dproj = jnp.concatenate([fdq, fdk, fdv, d_sq_sk, dsv], axis=1)

    pieces = []
    for s in range(N_CHIPS):
        lo, hi = s * in_rows, (s + 1) * in_rows
        for src, first, last, shift in [(dproj, 0, gate_lo, 0), (dfg, gate_lo, gate_lo + n_fox, gate_lo),
                                        (dproj, gate_lo + n_fox, in_w, n_fox)]:
            if max(lo, first) < min(hi, last):
                pieces.append(src[:, max(lo, first) - shift:min(hi, last) - shift])
        pieces.append(jnp.zeros((S, in_rows_pad - in_rows), BF16))
    dproj_slab = jnp.concatenate(pieces, axis=1)

    tm_in, tm_out = in_rows_pad // 2, D // (2 * N_CHIPS)
    pair_in = pair_send("in", _grad_half("grad_w_in_a", core_arr, dproj_slab, h, N_CHIPS, 1, tm_in, True))
    pair_out = pair_send("out", _grad_half("grad_w_out_a", core_arr, mixcat, dmix, N_CHIPS, 1, tm_out, True,
                                           after=pair_in[4]))
    sum_in = _grad_half("grad_w_in_b", core_arr, dproj_slab, h, N_CHIPS, 1, tm_in, False,
                        recv=pair_recv("in", pair_in, pair_out[4]))
    sum_out = _grad_half("grad_w_out_b", core_arr, mixcat, dmix, N_CHIPS, 1, tm_out, False,
                         recv=pair_recv("out", pair_out, sum_in[0, :8, :LANES]))
    dh = _mm_plain("in_proj_bwd", dproj_slab, w_slab_t, "nn", BF16, tk=slab_w // 2,
                   after=sum_out[0, :8, :LANES].astype(F32))
    grad_x, acc_pre = _pre_mix_bwd(dh, x2, dx1, g_pre_mix, sc_a)

    zero_row = jnp.zeros((1, D), F32)
    tail = jnp.concatenate([db_forget[0:1, :n_fox], dsink[:, 0, :n_swa // n_kv].reshape(1, n_swa),
                            loss_part[0:1, 0:1], jnp.zeros((1, D - n_fox - n_swa - 1), F32)], axis=1)
    partial = jnp.concatenate([
        acc_pre[0:1], acc_pre[1:2], acc_mid[3:4], acc_mid[0:1], acc_mid[1:2], acc_mlp_post[0:1],
        acc_pre[2:3], acc_mid[4:5], acc_mid[2:3], acc_mlp_post[1:2], tail] + [zero_row] * 5, axis=0)
    gathered_small, token = _allgather8("gather_small_grads", partial)

    flight_mix = scatter_start("mix", [sum_in, sum_out], after=token)
    halves_mlp = scatter_finish("mlp", flight_mlp, flight_mix[4])
    share_up, share_down = [
        _ici_start("grad_share_start_" + n, [hv], [jax.ShapeDtypeStruct(hv.shape, F32)], _share_plan, per_source=1)
        for n, hv in zip(["up", "down"], halves_mlp)]

    def shared(tag, flight, after):
        send, recv, own, lands, _ = flight
        own, other = _ici_wait("grad_share_wait_" + tag, send, recv, own, lands, _share_plan, after)
        return own[0], other[0]

    def unpack(p):
        return {"b_mod": p[0:N_MOD].reshape(1, N_MOD * D), "g_pre_mix": p[6:7], "g_post_mix": p[7:8],
                "g_pre_mlp": p[8:9], "g_post_mlp": p[9:10], "b_forget": p[10:11, :n_fox],
                "swa_sinks": p[10:11, n_fox:n_fox + n_swa]}

    small_out = _small_update(gathered_small, _tie(small_state[0], share_down[4] + share_up[4]), small_state[1],
                              small_state[2])
    g_small, d_small, m_small, v_small = [unpack(p) for p in small_out]
    loss = small_out[0][N_MOD + 4, n_fox + n_swa]

    dmod_all = gathered_small.reshape(N_DEV, 16, D)[:, :N_MOD].reshape(N_DEV, N_MOD * D)
    dmod_shard = _tie(lax.dynamic_slice_in_dim(dmod_all, chip * mod_cols, mod_cols, axis=1), share_down[4])
    g_w_mod, d_w_mod, nm_w_mod, nv_w_mod = _mod_update(c_all.T, dmod_shard, w_mod[0], m_w_mod[0], v_w_mod[0])

    grads = dict(g_small, w_mod=g_w_mod[None])
    deltas = dict(d_small, w_mod=d_w_mod[None])
    new_m = dict(m_small, w_mod=nm_w_mod[None])
    new_v = dict(v_small, w_mod=nv_w_mod[None])
    weights = {"w_in": (w_in, m_w_in, v_w_in), "w_out": (w_out, m_w_out, v_w_out), "w_up": (w_up, m_w_up, v_w_up),
               "w_down": (w_down, m_w_down, v_w_down)}

    def big_update(n, own, other):
        transposed = n == "w_in"
        w, m, v = in_state if transposed else [a[0] for a in weights[n]]
        outs = _adam_halves("adam_" + n, core_arr, w, own, other, m, v)
        if transposed:
            outs = [o[:in_rows].T for o in outs]
        grads[n], deltas[n], new_m[n], new_v[n] = [o[None] for o in outs]

    big_update("w_down", *shared("down", share_down, d_w_mod[:8, :LANES] + small_out[1][:8, :LANES]))
    halves_mix = scatter_finish("mix", flight_mix, deltas["w_down"][0, :8, :LANES] + d_w_mod[:8, :LANES])
    others_mix = _pair_share("grad_pair_share_mix", halves_mix)
    big_update("w_in", halves_mix[0], others_mix[0])
    big_update("w_out", halves_mix[1], others_mix[1])
    big_update("w_up", *shared("up", share_up, deltas["w_out"][0, :8, :LANES] + deltas["w_in"][0, :8, :LANES]))

    order = ["w_mod", "b_mod", "g_pre_mix", "g_post_mix", "w_in", "b_forget", "swa_sinks", "w_out", "g_pre_mlp",
             "g_post_mlp", "w_up", "w_down"]
    return (loss, grad_x[None], *[grads[n] for n in order], *[deltas[n] for n in order],
            *[new_m[n] for n in order], *[new_v[n] for n in order])
```

```python
import jax
import jax.numpy as jnp
from jax import lax
from jax.experimental import pallas as pl
from jax.experimental.pallas import tpu as pltpu

F32 = jnp.float32
BF16 = jnp.bfloat16
MESH = pl.DeviceIdType.MESH

HEAD_DIM = 128
SWA_BLOCK = 128
ROPE_THETA = 10000.0
NORM_EPS = 1e-6
NEG = -1e30
N_MOD = 6
ADAM_LR = 0.001
ADAM_B1 = 0.9
ADAM_B2 = 0.999
ADAM_EPS = 1e-08
ADAM_WD = 0.01
ADAM_STEP = 10
N_CHIPS = 4
N_DEV = 8
LANES = 128
VMEM_CAP = 60 * 1024 * 1024

_NN = (((1,), (0,)), ((), ()))
_NT = (((1,), (1,)), ((), ()))
_TN = (((0,), (0,)), ((), ()))


def _vmem(nbytes):
    return int(min(VMEM_CAP, nbytes * 5 // 4 + (4 << 20)))


def _nbytes(shape, dtype):
    n = 1
    for s in shape:
        n *= s
    return n * jnp.dtype(dtype).itemsize


def _fit(t, n):
    t = min(t, n)
    assert n % t == 0, (t, n)
    return t


MM_TM, MM_TN, MM_TK = 1024, 1024, 2048


def _matmul(name, a, b, mode, out_defs, epilogue, extras=(), tm=MM_TM, tn=MM_TN, tk=MM_TK, revisits=False,
            row_sel=None):
    stacked = b.ndim == 3
    b_rows, b_cols = b.shape[-2], b.shape[-1] * (b.shape[0] if stacked else 1)
    if mode == "nn":
        (M, K), (K2, N) = a.shape, (b_rows, b_cols)
    elif mode == "nt":
        (M, K), (N, K2) = a.shape, (b_rows, b_cols)
    else:
        (K, M), (K2, N) = a.shape, (b_rows, b_cols)
    assert K == K2 and not (stacked and mode == "tn"), (a.shape, b.shape, mode)
    tm = _fit(tm, M)
    tn = _fit(tn, b.shape[-1] if stacked and mode == "nn" else N)
    tk = _fit(tk, b.shape[-1] if stacked and mode == "nt" else K)
    nk = K // tk
    dims = {"nn": _NN, "nt": _NT, "tn": _TN}[mode]
    if row_sel is None:
        grid_m, a_row = M // tm, lambda i, *sel: i
    else:
        grid_m, a_row = row_sel[2], lambda i, *sel: row_sel[1](i, sel[0])
    a_spec = (pl.BlockSpec((tk, tm), lambda i, j, k, *sel: (k, a_row(i, *sel))) if mode == "tn"
              else pl.BlockSpec((tm, tk), lambda i, j, k, *sel: (a_row(i, *sel), k)))
    if stacked:
        per = b.shape[-1] // (tk if mode == "nt" else tn)
        b_spec = (pl.BlockSpec((1, tn, tk), lambda i, j, k, *sel: (k // per, j, k % per)) if mode == "nt"
                  else pl.BlockSpec((1, tk, tn), lambda i, j, k, *sel: (j // per, k, j % per)))
    else:
        b_spec = (pl.BlockSpec((tn, tk), lambda i, j, k, *sel: (j, k)) if mode == "nt"
                  else pl.BlockSpec((tk, tn), lambda i, j, k, *sel: (k, j)))
    n_ex, n_out = len(extras), len(out_defs)

    def body(*refs):
        if row_sel is not None:
            refs = refs[1:]
        a_ref, b_ref = refs[0], refs[1]
        ex = refs[2:2 + n_ex]
        outs = refs[2 + n_ex:2 + n_ex + n_out]
        b_blk = b_ref[0] if stacked else b_ref[...]
        prod = lax.dot_general(a_ref[...], b_blk, dims, preferred_element_type=F32)
        if nk == 1:
            epilogue(prod, ex, outs)
        else:
            acc_ref = refs[-1]
            k = pl.program_id(2)

            @pl.when(k == 0)
            def _():
                acc_ref[...] = prod

            @pl.when(k > 0)
            def _():
                acc_ref[...] += prod

            @pl.when(k == nk - 1)
            def _():
                epilogue(acc_ref[...], ex, outs)

    def wrap(f):
        return lambda i, j, k, *sel: f(i, j)

    in_specs = [a_spec, b_spec] + [pl.BlockSpec(blk, wrap(f)) for _, blk, f in extras]
    out_specs = [pl.BlockSpec(blk, wrap(f)) for _, _, blk, f in out_defs]
    out_shape = [jax.ShapeDtypeStruct(s, d) for s, d, _, _ in out_defs]
    need = 2 * (tm * tk + tk * tn) * a.dtype.itemsize + 3 * tm * tn * 4
    need += sum(2 * _nbytes(blk, arr.dtype) for arr, blk, _ in extras)
    need += sum(2 * _nbytes(blk, d) for _, d, blk, _ in out_defs)
    grid = (grid_m, N // tn, nk)
    scratch = [pltpu.VMEM((tm, tn), F32)] if nk > 1 else []
    params = pltpu.CompilerParams(
        dimension_semantics=("parallel", "arbitrary" if revisits else "parallel", "arbitrary"),
        vmem_limit_bytes=_vmem(need))
    operands = (a, b, *[arr for arr, _, _ in extras])
    if row_sel is None:
        return pl.pallas_call(body, name=name, grid=grid, in_specs=in_specs, out_specs=out_specs, out_shape=out_shape,
                              scratch_shapes=scratch, compiler_params=params)(*operands)
    grid_spec = pltpu.PrefetchScalarGridSpec(num_scalar_prefetch=1, grid=grid, in_specs=in_specs, out_specs=out_specs,
                                             scratch_shapes=scratch)
    return pl.pallas_call(body, name=name, grid_spec=grid_spec, out_shape=out_shape,
                          compiler_params=params)(row_sel[0], *operands)


def _grad_half(name, core, a, b, row_slabs, col_slabs, tm, other, recv=None, after=None):
    (_, M), (_, N) = a.shape, b.shape
    H = M // (2 * row_slabs)
    nh = H // tm
    tn = _fit(MM_TN, N // col_slabs)
    per = N // col_slabs // tn

    def a_block(i, core_ref):
        half = (1 - core_ref[0]) if other else core_ref[0]
        return (i // nh) * (2 * nh) + half * nh + i % nh

    def out_index(i, j):
        return (j // per, i, j % per) if col_slabs > 1 else (i // nh, i % nh, j)

    slabs = max(row_slabs, col_slabs)
    out_def = ((slabs, H, N // col_slabs), BF16, (1, tm, tn), out_index)

    def epilogue(acc, ex, outs):
        outs[0][0] = (acc if recv is None else acc + ex[0][0].astype(F32)).astype(BF16)

    extras = ([] if recv is None else [(recv, (1, tm, tn), out_index)]) + ([] if after is None else [_behind(after)])
    return _matmul(name, a, b, "tn", [out_def], epilogue, extras=extras, tm=tm, tn=tn,
                   row_sel=(core, a_block, row_slabs * nh))[0]


def _behind(token):
    return (token, (8, LANES), lambda i, j: (0, 0))


def _mm_plain(name, a, b, mode, out_dtype, after=None, **tiles):
    if mode == "nn":
        M, N = a.shape[0], b.shape[-1] * (b.shape[0] if b.ndim == 3 else 1)
    elif mode == "nt":
        M, N = a.shape[0], b.shape[-2]
    else:
        M, N = a.shape[1], b.shape[1]
    tm, tn = _fit(tiles.get("tm", MM_TM), M), _fit(tiles.get("tn", MM_TN), N)

    def epi(acc, ex, outs):
        outs[0][...] = acc.astype(out_dtype)

    return _matmul(name, a, b, mode, [((M, N), out_dtype, (tm, tn), lambda i, j: (i, j))], epi,
                   extras=[] if after is None else [_behind(after)], **tiles)[0]


def _rstd(v):
    return lax.rsqrt(jnp.mean(v * v, axis=-1, keepdims=True) + NORM_EPS)


ROW_TILE = 256


def _row_call(name, body, row_ins, vec_ins, row_outs, acc_outs, S, D):
    tr = _fit(ROW_TILE, S)
    row_spec = pl.BlockSpec((tr, D), lambda r: (r, 0))
    vec_spec = pl.BlockSpec((1, D), lambda r: (0, 0))
    in_specs = [row_spec] * len(row_ins) + [vec_spec] * len(vec_ins)
    out_specs = [row_spec] * len(row_outs) + [pl.BlockSpec(shp, lambda r: (0, 0)) for shp in acc_outs]
    out_shape = [jax.ShapeDtypeStruct((S, D), d) for d in row_outs] + [jax.ShapeDtypeStruct(shp, F32) for shp in acc_outs]
    need = sum(2 * tr * D * a.dtype.itemsize for a in row_ins) + sum(2 * tr * D * jnp.dtype(d).itemsize for d in row_outs)
    need += 8 * tr * D * 4
    return pl.pallas_call(
        body, name=name, grid=(S // tr,), in_specs=in_specs, out_specs=out_specs, out_shape=out_shape,
        compiler_params=pltpu.CompilerParams(dimension_semantics=("arbitrary",), vmem_limit_bytes=_vmem(need)),
    )(*row_ins, *vec_ins)


def _acc_rows(ref, rows):
    @pl.when(pl.program_id(0) == 0)
    def _():
        ref[...] = jnp.zeros_like(ref)
    for n, r in enumerate(rows):
        ref[n:n + 1, :] += r


def _pre_norm(x, g, sc, sh):
    S, D = x.shape

    def body(x_ref, g_ref, sc_ref, sh_ref, h_ref):
        xv = x_ref[...]
        xn = xv * _rstd(xv)
        h_ref[...] = (xn * g_ref[...] * (1.0 + sc_ref[...]) + sh_ref[...]).astype(BF16)

    return _row_call("pre_norm_mix", body, [x], [g, sc, sh], [BF16], [], S, D)[0]


def _post_mix(x, mix, g_post, gt, g_pre, sc, sh):
    S, D = x.shape

    def body(x_ref, mix_ref, gp_ref, gt_ref, g2_ref, sc_ref, sh_ref, x1_ref, h2_ref):
        mv = mix_ref[...].astype(F32)
        x1 = x_ref[...] + gt_ref[...] * (mv * _rstd(mv) * gp_ref[...])
        x1_ref[...] = x1
        h2_ref[...] = (x1 * _rstd(x1) * g2_ref[...] * (1.0 + sc_ref[...]) + sh_ref[...]).astype(BF16)

    return _row_call("post_mix_pre_mlp", body, [x, mix], [g_post, gt, g_pre, sc, sh], [F32, BF16], [], S, D)


def _loss_and_post_mlp_bwd(x1, y, target, g_post, gt):
    S, D = x1.shape

    def body(x1_ref, y_ref, t_ref, g_ref, gt_ref, dy_ref, dout_ref, loss_ref, acc_ref):
        yv = y_ref[...].astype(F32)
        r = _rstd(yv)
        yh = yv * r
        n = yh * g_ref[...]
        diff = x1_ref[...] + gt_ref[...] * n - t_ref[...]
        dout = diff * (1.0 / D)
        dout_ref[...] = dout
        dn = dout * gt_ref[...]
        dyh = dn * g_ref[...]
        dy_ref[...] = (r * (dyh - yh * jnp.mean(dyh * yh, axis=-1, keepdims=True))).astype(BF16)
        _acc_rows(acc_ref, [jnp.sum(dout * n, axis=0, keepdims=True), jnp.sum(dn * yh, axis=0, keepdims=True)])

        @pl.when(pl.program_id(0) == 0)
        def _():
            loss_ref[...] = jnp.zeros_like(loss_ref)
        loss_ref[...] += jnp.full(loss_ref.shape, (0.5 / D) * jnp.sum(diff * diff), F32)

    return _row_call("loss_post_mlp_bwd", body, [x1, y, target], [g_post, gt], [BF16, F32],
                     [(8, LANES), (8, D)], S, D)


def _pre_mlp_and_post_mix_bwd(dh2, x1, dout, mix, g_pre, sc, g_post, gt):
    S, D = x1.shape

    def body(dh_ref, x1_ref, dout_ref, mix_ref, g_ref, sc_ref, gp_ref, gt_ref, dx1_ref, dmix_ref, acc_ref):
        dh = dh_ref[...].astype(F32)
        x1v = x1_ref[...]
        r3 = _rstd(x1v)
        xn = x1v * r3
        dxn = dh * (1.0 + sc_ref[...]) * g_ref[...]
        dx1 = dout_ref[...] + r3 * (dxn - xn * jnp.mean(dxn * xn, axis=-1, keepdims=True))
        dx1_ref[...] = dx1
        mv = mix_ref[...].astype(F32)
        r2 = _rstd(mv)
        mh = mv * r2
        dn = dx1 * gt_ref[...]
        dmh = dn * gp_ref[...]
        dmix_ref[...] = (r2 * (dmh - mh * jnp.mean(dmh * mh, axis=-1, keepdims=True))).astype(BF16)
        _acc_rows(acc_ref, [
            jnp.sum(dh, axis=0, keepdims=True),
            jnp.sum(dh * xn * g_ref[...], axis=0, keepdims=True),
            jnp.sum(dh * (1.0 + sc_ref[...]) * xn, axis=0, keepdims=True),
            jnp.sum(dx1 * mh * gp_ref[...], axis=0, keepdims=True),
            jnp.sum(dn * mh, axis=0, keepdims=True)])

    return _row_call("pre_mlp_post_mix_bwd", body, [dh2, x1, dout, mix], [g_pre, sc, g_post, gt], [F32, BF16],
                     [(8, D)], S, D)


def _pre_mix_bwd(dh, x, dx1, g_pre, sc):
    S, D = x.shape

    def body(dh_ref, x_ref, dx1_ref, g_ref, sc_ref, gx_ref, acc_ref):
        dhv = dh_ref[...].astype(F32)
        xv = x_ref[...]
        r = _rstd(xv)
        xn = xv * r
        dxn = dhv * (1.0 + sc_ref[...]) * g_ref[...]
        gx_ref[...] = dx1_ref[...] + r * (dxn - xn * jnp.mean(dxn * xn, axis=-1, keepdims=True))
        _acc_rows(acc_ref, [
            jnp.sum(dhv, axis=0, keepdims=True),
            jnp.sum(dhv * xn * g_ref[...], axis=0, keepdims=True),
            jnp.sum(dhv * (1.0 + sc_ref[...]) * xn, axis=0, keepdims=True)])

    return _row_call("pre_mix_bwd", body, [dh, x, dx1], [g_pre, sc], [F32], [(8, D)], S, D)


CUM_BLOCK = 256


def _tri(n, upper):
    r = lax.broadcasted_iota(jnp.int32, (n, n), 0)
    c = lax.broadcasted_iota(jnp.int32, (n, n), 1)
    return ((c >= r) if upper else (c <= r)).astype(F32)


def _fox_gate_fwd(fg, b_pad):
    S = fg.shape[0]
    cb = _fit(CUM_BLOCK, S)

    def body(fg_ref, b_ref, cumt_ref, cum_ref):
        low = _tri(cb, False)
        carry = jnp.zeros((1, LANES), F32)
        for n in range(S // cb):
            z = fg_ref[n * cb:(n + 1) * cb, :] + b_ref[...]
            logf = jnp.minimum(z, 0.0) - jnp.log(1.0 + jnp.exp(-jnp.abs(z)))
            blk = jnp.dot(low, logf, precision=lax.Precision.HIGHEST, preferred_element_type=F32) + carry
            cum_ref[n * cb:(n + 1) * cb, :] = blk
            carry = blk[cb - 1:cb, :]
        cumt_ref[...] = cum_ref[...].T

    return pl.pallas_call(
        body, name="fox_gate_fwd", out_shape=jax.ShapeDtypeStruct((LANES, S), F32),
        scratch_shapes=[pltpu.VMEM((S, LANES), F32)],
        compiler_params=pltpu.CompilerParams(vmem_limit_bytes=_vmem(6 * S * LANES * 4)),
    )(fg, b_pad)


def _fox_gate_bwd(dcum_k, dcum_q, fg, b_pad):
    S = fg.shape[0]
    n_fox = dcum_q.shape[0]
    cb = _fit(CUM_BLOCK, S)

    def body(dk_ref, dq_ref, fg_ref, b_ref, dfg_ref, db_ref, dc_ref):
        lane = lax.broadcasted_iota(jnp.int32, (S, LANES), 1)
        dc = dk_ref[...].T
        for h in range(n_fox):
            dc = dc + jnp.where(lane == h, dq_ref[h], 0.0)
        dc_ref[...] = dc
        up = _tri(cb, True)
        carry = jnp.zeros((1, LANES), F32)
        db = jnp.zeros((1, LANES), F32)
        for n in reversed(range(S // cb)):
            blk = jnp.dot(up, dc_ref[n * cb:(n + 1) * cb, :], precision=lax.Precision.HIGHEST,
                          preferred_element_type=F32) + carry
            carry = blk[0:1, :]
            z = fg_ref[n * cb:(n + 1) * cb, :] + b_ref[...]
            dfg = blk * (1.0 / (1.0 + jnp.exp(z)))
            dfg_ref[n * cb:(n + 1) * cb, :] = dfg.astype(BF16)
            db = db + jnp.sum(dfg, axis=0, keepdims=True)
        db_ref[...] = jnp.broadcast_to(db, db_ref.shape)

    return pl.pallas_call(
        body, name="fox_gate_bwd",
        out_shape=[jax.ShapeDtypeStruct((S, LANES), BF16), jax.ShapeDtypeStruct((8, LANES), F32)],
        scratch_shapes=[pltpu.VMEM((S, LANES), F32)],
        compiler_params=pltpu.CompilerParams(vmem_limit_bytes=_vmem((8 + 2 * n_fox) * S * LANES * 4)),
    )(dcum_k, dcum_q, fg, b_pad)


FOX_TILE = 512


LOG2E = 1.4426950408889634


def _fox_scores(q, k, ck2, masked, t):
    s = lax.dot_general(q, k, _NT, preferred_element_type=F32) * (HEAD_DIM ** -0.5 * LOG2E) - ck2
    if masked:
        row = lax.broadcasted_iota(jnp.int32, (t, t), 0)
        col = lax.broadcasted_iota(jnp.int32, (t, t), 1)
        s = jnp.where(col <= row, s, NEG)
    return s


def _fox_fwd(proj, cum_row, n_fox):
    S = proj.shape[0]
    t = _fit(FOX_TILE, S)
    nq = S // t

    def body(q_ref, k_ref, v_ref, ck_ref, o_ref, lse_ref):
        def q_block(qi, _):
            q0 = pl.multiple_of(qi * t, t)
            q = q_ref[pl.ds(q0, t), :]

            def kv_block(j, carry, masked):
                m, l, acc = carry
                k0 = pl.multiple_of(j * t, t)
                s = _fox_scores(q, k_ref[pl.ds(k0, t), :], ck_ref[0, :, pl.ds(k0, t)] * LOG2E, masked, t)
                m_new = jnp.maximum(m, jnp.max(s, axis=-1, keepdims=True))
                alpha = jnp.exp2(m - m_new)
                p = jnp.exp2(s - m_new)
                l = alpha * l + jnp.sum(p, axis=-1, keepdims=True)
                acc = alpha * acc + jnp.dot(p.astype(BF16), v_ref[pl.ds(k0, t), :], preferred_element_type=F32)
                return m_new, l, acc

            init = (jnp.full((t, 1), NEG, F32), jnp.zeros((t, 1), F32), jnp.zeros((t, HEAD_DIM), F32))
            carry = lax.fori_loop(0, qi, lambda j, cr: kv_block(j, cr, False), init)
            m, l, acc = kv_block(qi, carry, True)
            o_ref[pl.ds(q0, t), :] = acc / l
            lse_ref[0, pl.ds(q0, t), :] = jnp.broadcast_to(m + jnp.log(l) * LOG2E, (t, LANES))
            return 0

        lax.fori_loop(0, nq, q_block, 0)

    col = lambda off: pl.BlockSpec((S, HEAD_DIM), lambda h: (0, off + h))
    per_head = pl.BlockSpec((1, S, LANES), lambda h: (h, 0, 0))
    return pl.pallas_call(
        body, name="fox_fwd", grid=(n_fox,),
        in_specs=[col(0), col(n_fox), col(2 * n_fox), pl.BlockSpec((1, 1, S), lambda h: (h, 0, 0))],
        out_specs=[pl.BlockSpec((S, HEAD_DIM), lambda h: (0, h)), per_head],
        out_shape=[jax.ShapeDtypeStruct((S, n_fox * HEAD_DIM), F32), jax.ShapeDtypeStruct((n_fox, S, LANES), F32)],
        compiler_params=pltpu.CompilerParams(dimension_semantics=("parallel",),
                                             vmem_limit_bytes=_vmem(16 * S * HEAD_DIM * 4 + 12 * t * t * 4)),
    )(proj, proj, proj, cum_row)


def _fox_bwd(proj, o, do, lse_b, cum_row, n_fox):
    S = proj.shape[0]
    t = _fit(FOX_TILE, S)
    nq = S // t
    scale = HEAD_DIM ** -0.5

    def body(q_ref, k_ref, v_ref, o_ref, do_ref, lse_ref, ck_ref, dq_ref, dk_ref, dv_ref, dc_ref, dcq_ref,
             dq_acc, delta_ref):
        dq_acc[...] = jnp.zeros_like(dq_acc)
        dcq_ref[...] = jnp.zeros_like(dcq_ref)

        def delta_block(qi, _):
            q0 = pl.multiple_of(qi * t, t)
            d = jnp.sum(do_ref[pl.ds(q0, t), :] * o_ref[pl.ds(q0, t), :], axis=-1, keepdims=True)
            delta_ref[pl.ds(q0, t), :] = jnp.broadcast_to(d, (t, LANES))
            return 0

        lax.fori_loop(0, nq, delta_block, 0)

        def kv_block(j, _):
            k0 = pl.multiple_of(j * t, t)
            k = k_ref[pl.ds(k0, t), :]
            v = v_ref[pl.ds(k0, t), :]
            ck2 = ck_ref[0, :, pl.ds(k0, t)] * LOG2E

            def q_block(qi, carry, masked):
                dk, dv, dc = carry
                q0 = pl.multiple_of(qi * t, t)
                q = q_ref[pl.ds(q0, t), :]
                dov = do_ref[pl.ds(q0, t), :].astype(BF16)
                p = jnp.exp2(_fox_scores(q, k, ck2, masked, t) - lse_ref[0, pl.ds(q0, t), :][:, :1])
                dp = lax.dot_general(dov, v, _NT, preferred_element_type=F32)
                ds = p * (dp - delta_ref[pl.ds(q0, t), :][:, :1])
                dsb = ds.astype(BF16)
                dv = dv + lax.dot_general(p.astype(BF16), dov, _TN, preferred_element_type=F32)
                dk = dk + lax.dot_general(dsb, q, _TN, preferred_element_type=F32)
                dq_acc[pl.ds(q0, t), :] += jnp.dot(dsb, k, preferred_element_type=F32)
                dc = dc - jnp.sum(ds, axis=0, keepdims=True)
                dcq_ref[0, pl.ds(q0, t), :] += jnp.broadcast_to(jnp.sum(ds, axis=1, keepdims=True), (t, LANES))
                return dk, dv, dc

            init = (jnp.zeros((t, HEAD_DIM), F32), jnp.zeros((t, HEAD_DIM), F32), jnp.zeros((1, t), F32))
            carry = q_block(j, init, True)
            dk, dv, dc = lax.fori_loop(j + 1, nq, lambda qi, cr: q_block(qi, cr, False), carry)
            dk_ref[pl.ds(k0, t), :] = (dk * scale).astype(BF16)
            dv_ref[pl.ds(k0, t), :] = dv.astype(BF16)
            dc_ref[0, :, pl.ds(k0, t)] = dc
            return 0

        lax.fori_loop(0, nq, kv_block, 0)
        dq_ref[...] = (dq_acc[...] * scale).astype(BF16)

    col = lambda off: pl.BlockSpec((S, HEAD_DIM), lambda h: (0, off + h))
    per_head = pl.BlockSpec((1, S, LANES), lambda h: (h, 0, 0))
    row = pl.BlockSpec((1, 1, S), lambda h: (h, 0, 0))
    grad = jax.ShapeDtypeStruct((S, n_fox * HEAD_DIM), BF16)
    return pl.pallas_call(
        body, name="fox_bwd", grid=(n_fox,),
        in_specs=[col(0), col(n_fox), col(2 * n_fox), col(0), col(0), per_head, row],
        out_specs=[col(0), col(0), col(0), row, per_head],
        out_shape=[grad, grad, grad, jax.ShapeDtypeStruct((n_fox, 1, S), F32), jax.ShapeDtypeStruct((n_fox, S, LANES), F32)],
        scratch_shapes=[pltpu.VMEM((S, HEAD_DIM), F32), pltpu.VMEM((S, LANES), F32)],
        compiler_params=pltpu.CompilerParams(dimension_semantics=("parallel",),
                                             vmem_limit_bytes=_vmem(24 * S * HEAD_DIM * 4 + 16 * t * t * 4)),
    )(proj, proj, proj, o, do, lse_b, cum_row)


def _rope_tables(S):
    half = HEAD_DIM // 2
    inv_freq = 1.0 / (ROPE_THETA ** (jnp.arange(half, dtype=F32) * (2.0 / HEAD_DIM)))
    ang = jnp.arange(S).astype(F32)[:, None] * inv_freq[None, :]
    cos, sin = jnp.cos(ang), jnp.sin(ang)
    return jnp.concatenate([cos, cos], axis=-1), jnp.concatenate([-sin, sin], axis=-1)


def _rope(name, src, first_block, n_blocks, cos, sin_signed):
    S = src.shape[0]

    def body(x_ref, cos_ref, sin_ref, o_ref):
        xv = x_ref[...].astype(F32)
        o_ref[...] = (xv * cos_ref[...] + pltpu.roll(xv, HEAD_DIM // 2, 1) * sin_ref[...]).astype(BF16)

    table = pl.BlockSpec((S, HEAD_DIM), lambda n: (0, 0))
    return pl.pallas_call(
        body, name=name, grid=(n_blocks,),
        in_specs=[pl.BlockSpec((S, HEAD_DIM), lambda n: (0, first_block + n)), table, table],
        out_specs=pl.BlockSpec((S, HEAD_DIM), lambda n: (0, n)),
        out_shape=jax.ShapeDtypeStruct((S, n_blocks * HEAD_DIM), BF16),
        compiler_params=pltpu.CompilerParams(dimension_semantics=("parallel",),
                                             vmem_limit_bytes=_vmem(12 * S * HEAD_DIM * 4)),
    )(src, cos, sin_signed)


def _swa_tile(q_ref, kp_ref, kc_ref, n, group, scale):
    B = SWA_BLOCK
    qs = jnp.concatenate([q_ref[:, g * HEAD_DIM:(g + 1) * HEAD_DIM] for g in range(group)], axis=0)
    kcat = jnp.concatenate([kp_ref[...], kc_ref[...]], axis=0)
    s = lax.dot_general(qs, kcat, _NT, preferred_element_type=F32) * scale
    qi = lax.broadcasted_iota(jnp.int32, (group * B, 2 * B), 0) % B
    kj = lax.broadcasted_iota(jnp.int32, (group * B, 2 * B), 1)
    diff = qi + B - kj
    mask = (diff >= 0) & (diff < B) & ((n * B + kj - B) >= 0)
    return qs, kcat, jnp.where(mask, s, NEG)


def _swa_sink_col(sink_ref, kv, group):
    head = lax.broadcasted_iota(jnp.int32, (group * SWA_BLOCK, 1), 0) // SWA_BLOCK
    col = jnp.zeros((group * SWA_BLOCK, 1), F32)
    for g in range(group):
        col = jnp.where(head == g, sink_ref[kv * group + g], col)
    return col


def _swa_specs(n_kv, group, q_first, k_first, v_first):
    B = SWA_BLOCK
    prev = lambda n: jnp.maximum(n - 1, 0)
    return [
        pl.BlockSpec((B, group * HEAD_DIM), lambda kv, n: (n, q_first + kv)),
        pl.BlockSpec((B, HEAD_DIM), lambda kv, n: (prev(n), k_first + kv)),
        pl.BlockSpec((B, HEAD_DIM), lambda kv, n: (n, k_first + kv)),
        pl.BlockSpec((B, HEAD_DIM), lambda kv, n: (prev(n), v_first + kv)),
        pl.BlockSpec((B, HEAD_DIM), lambda kv, n: (n, v_first + kv)),
    ]


def _swa_fwd(rq, proj, v_first, sinks, n_q, n_kv):
    S = rq.shape[0]
    B = SWA_BLOCK
    group = n_q // n_kv
    scale = HEAD_DIM ** -0.5

    def body(q_ref, kp_ref, kc_ref, vp_ref, vc_ref, sink_ref, o_ref, lse_ref):
        kv, n = pl.program_id(0), pl.program_id(1)
        _, _, s = _swa_tile(q_ref, kp_ref, kc_ref, n, group, scale)
        sink = _swa_sink_col(sink_ref, kv, group)
        m = jnp.maximum(jnp.max(s, axis=-1, keepdims=True), sink)
        p = jnp.exp(s - m)
        denom = jnp.sum(p, axis=-1, keepdims=True) + jnp.exp(sink - m)
        vcat = jnp.concatenate([vp_ref[...], vc_ref[...]], axis=0)
        o = jnp.dot((p / denom).astype(BF16), vcat, preferred_element_type=F32)
        lse = m + jnp.log(denom)
        for g in range(group):
            o_ref[:, g * HEAD_DIM:(g + 1) * HEAD_DIM] = o[g * B:(g + 1) * B, :]
            lse_ref[0, :, g * LANES:(g + 1) * LANES] = jnp.broadcast_to(lse[g * B:(g + 1) * B, :], (B, LANES))

    specs = _swa_specs(n_kv, group, 0, n_q, v_first)
    q_blk = pl.BlockSpec((B, group * HEAD_DIM), lambda kv, n: (n, kv))
    return pl.pallas_call(
        body, name="swa_fwd", grid=(n_kv, S // B),
        in_specs=specs + [pl.BlockSpec(memory_space=pltpu.SMEM)],
        out_specs=[q_blk, pl.BlockSpec((1, B, group * LANES), lambda kv, n: (kv, n, 0))],
        out_shape=[jax.ShapeDtypeStruct((S, n_q * HEAD_DIM), F32), jax.ShapeDtypeStruct((n_kv, S, group * LANES), F32)],
        compiler_params=pltpu.CompilerParams(dimension_semantics=("parallel", "arbitrary")),
    )(rq, rq, rq, proj, proj, sinks)


def _swa_bwd(rq, proj, v_first, sinks, o, do, do_first, lse_b, n_q, n_kv):
    S = rq.shape[0]
    B = SWA_BLOCK
    group = n_q // n_kv
    scale = HEAD_DIM ** -0.5

    def body(q_ref, kp_ref, kc_ref, vp_ref, vc_ref, o_ref, do_ref, lse_ref, sink_ref,
             dq_ref, dk_ref, dv_ref, dsink_ref):
        kv, n = pl.program_id(0), pl.program_id(1)

        @pl.when(n == 0)
        def _():
            dk_ref[...] = jnp.zeros_like(dk_ref)
            dv_ref[...] = jnp.zeros_like(dv_ref)
            dsink_ref[...] = jnp.zeros_like(dsink_ref)

        qs, kcat, s = _swa_tile(q_ref, kp_ref, kc_ref, n, group, scale)
        sink = _swa_sink_col(sink_ref, kv, group)
        stack = lambda ref, w: jnp.concatenate([ref[:, g * w:(g + 1) * w] for g in range(group)], axis=0)
        lse = jnp.concatenate([lse_ref[0, :, g * LANES:g * LANES + 1] for g in range(group)], axis=0)
        do32 = stack(do_ref, HEAD_DIM)
        delta = jnp.sum(do32 * stack(o_ref, HEAD_DIM), axis=-1, keepdims=True)
        dov = do32.astype(BF16)
        p = jnp.exp(s - lse)
        vcat = jnp.concatenate([vp_ref[...], vc_ref[...]], axis=0)
        dp = lax.dot_general(dov, vcat, _NT, preferred_element_type=F32)
        ds = p * (dp - delta)
        dsb = ds.astype(BF16)
        dq = jnp.dot(dsb, kcat, preferred_element_type=F32) * scale
        for g in range(group):
            dq_ref[:, g * HEAD_DIM:(g + 1) * HEAD_DIM] = dq[g * B:(g + 1) * B, :].astype(BF16)
        dkcat = lax.dot_general(dsb, qs, _TN, preferred_element_type=F32) * scale
        dvcat = lax.dot_general(p.astype(BF16), dov, _TN, preferred_element_type=F32)
        prev0 = pl.multiple_of(jnp.maximum(n - 1, 0) * B, B)
        cur0 = pl.multiple_of(n * B, B)
        dk_ref[0, pl.ds(prev0, B), :] += dkcat[:B, :]
        dk_ref[0, pl.ds(cur0, B), :] += dkcat[B:, :]
        dv_ref[0, pl.ds(prev0, B), :] += dvcat[:B, :]
        dv_ref[0, pl.ds(cur0, B), :] += dvcat[B:, :]
        dsk = -jnp.exp(sink - lse) * delta
        lane = lax.broadcasted_iota(jnp.int32, (1, LANES), 1)
        row = jnp.zeros((1, LANES), F32)
        for g in range(group):
            row = row + jnp.where(lane == g, jnp.sum(dsk[g * B:(g + 1) * B, :]), 0.0)
        dsink_ref[0, 0:1, :] += row

    specs = _swa_specs(n_kv, group, 0, n_q, v_first)
    q_blk = pl.BlockSpec((B, group * HEAD_DIM), lambda kv, n: (n, kv))
    acc = pl.BlockSpec((1, S, HEAD_DIM), lambda kv, n: (kv, 0, 0))
    return pl.pallas_call(
        body, name="swa_bwd", grid=(n_kv, S // B),
        in_specs=specs + [q_blk, pl.BlockSpec((B, group * HEAD_DIM), lambda kv, n: (n, do_first + kv)),
                          pl.BlockSpec((1, B, group * LANES), lambda kv, n: (kv, n, 0)),
                          pl.BlockSpec(memory_space=pltpu.SMEM)],
        out_specs=[q_blk, acc, acc, pl.BlockSpec((1, 8, LANES), lambda kv, n: (kv, 0, 0))],
        out_shape=[jax.ShapeDtypeStruct((S, n_q * HEAD_DIM), BF16), jax.ShapeDtypeStruct((n_kv, S, HEAD_DIM), F32),
                   jax.ShapeDtypeStruct((n_kv, S, HEAD_DIM), F32), jax.ShapeDtypeStruct((n_kv, 8, LANES), F32)],
        compiler_params=pltpu.CompilerParams(dimension_semantics=("parallel", "arbitrary")),
    )(rq, rq, rq, proj, proj, o, do, lse_b, sinks)


def _adamw(w, g, m, v):
    m = ADAM_B1 * m + (1.0 - ADAM_B1) * g
    v = ADAM_B2 * v + (1.0 - ADAM_B2) * (g * g)
    m_hat = m / (1.0 - ADAM_B1 ** ADAM_STEP)
    v_hat = v / (1.0 - ADAM_B2 ** ADAM_STEP)
    delta = -ADAM_LR * (m_hat / (jnp.sqrt(v_hat) + ADAM_EPS) + ADAM_WD * w)
    return delta, m, v


def _mod_fwd(cond_in, w_mod, b_shard):
    R, D = cond_in.shape
    cols = w_mod.shape[1]
    tn = _fit(512, cols)

    def body(c_ref, w_ref, b_ref, o_ref):
        cv = c_ref[...]
        cond = (cv / (1.0 + jnp.exp(-cv))).astype(BF16)
        o_ref[...] = jnp.dot(cond, w_ref[...].astype(BF16), preferred_element_type=F32) + b_ref[...]

    return pl.pallas_call(
        body, name="mod_fwd", grid=(cols // tn,),
        in_specs=[pl.BlockSpec((R, D), lambda j: (0, 0)), pl.BlockSpec((D, tn), lambda j: (0, j)),
                  pl.BlockSpec((1, tn), lambda j: (0, j))],
        out_specs=pl.BlockSpec((R, tn), lambda j: (0, j)),
        out_shape=jax.ShapeDtypeStruct((R, cols), F32),
        compiler_params=pltpu.CompilerParams(dimension_semantics=("parallel",), vmem_limit_bytes=_vmem(3 * D * tn * 4)),
    )(cond_in, w_mod, b_shard)


def _mod_update(c_t, dmod, w, m, v):
    D, nb = c_t.shape
    cols = w.shape[1]
    tr = _fit(128, D)

    def body(c_ref, d_ref, w_ref, m_ref, v_ref, g_ref, dl_ref, nm_ref, nv_ref):
        cv = c_ref[...]
        cond = cv / (1.0 + jnp.exp(-cv))
        g = jnp.zeros((tr, cols), F32)
        for b in range(nb):
            g = g + cond[:, b:b + 1] * d_ref[b:b + 1, :]
        g_ref[...] = g
        dl_ref[...], nm_ref[...], nv_ref[...] = _adamw(w_ref[...], g, m_ref[...], v_ref[...])

    blk = pl.BlockSpec((tr, cols), lambda r: (r, 0))
    out = jax.ShapeDtypeStruct((D, cols), F32)
    return pl.pallas_call(
        body, name="mod_update", grid=(D // tr,),
        in_specs=[pl.BlockSpec((tr, nb), lambda r: (r, 0)), pl.BlockSpec((nb, cols), lambda r: (0, 0)), blk, blk, blk],
        out_specs=[blk] * 4, out_shape=[out] * 4,
        compiler_params=pltpu.CompilerParams(dimension_semantics=("parallel",), vmem_limit_bytes=_vmem(18 * tr * cols * 4)),
    )(c_t, dmod, w, m, v)


def _small_update(stacked, w, m, v):
    R, C = w.shape

    def body(s_ref, w_ref, m_ref, v_ref, g_ref, dl_ref, nm_ref, nv_ref):
        g = s_ref[0:R, :]
        for d in range(1, N_DEV):
            g = g + s_ref[d * R:(d + 1) * R, :]
        g_ref[...] = g
        dl_ref[...], nm_ref[...], nv_ref[...] = _adamw(w_ref[...], g, m_ref[...], v_ref[...])

    return pl.pallas_call(body, name="small_update", out_shape=[jax.ShapeDtypeStruct((R, C), F32)] * 4)(stacked, w, m, v)


def _place():
    return lax.axis_index("x"), lax.axis_index("y"), lax.axis_index("c")


def _allgather8(name, block):
    m_per, n = block.shape

    def body(x_ref, out_ref, token_ref, send_sems, recv_sems, local_sem):
        token_ref[...] = jnp.zeros_like(token_ref)
        x, y, c = _place()
        me, sibling = (x, y, c), (x, y, 1 - c)
        chips = [(1 - x, y), (x, 1 - y), (1 - x, 1 - y)]

        def rows(px, py, pc):
            return out_ref.at[pl.ds((4 * px + 2 * py + pc) * m_per, m_per), :]

        def copy(k, blk, to, src=None):
            return pltpu.make_async_remote_copy(
                src_ref=rows(*blk) if src is None else src, dst_ref=rows(*blk),
                send_sem=send_sems.at[k], recv_sem=recv_sems.at[k], device_id=to, device_id_type=MESH)

        mine = pltpu.make_async_copy(x_ref, rows(*me), local_sem)
        mine.start()
        first = [copy(0, me, sibling, src=x_ref)]
        first += [copy(1 + j, me, (*chip, c), src=x_ref) for j, chip in enumerate(chips)]
        for cp in first:
            cp.start()
        passed = [copy(4 + j, (*chip, c), sibling) for j, chip in enumerate(chips)]
        for j, chip in enumerate(chips):
            copy(1 + j, (*chip, c), me).wait_recv()
            passed[j].start()
        copy(0, sibling, me).wait_recv()
        for j, chip in enumerate(chips):
            copy(4 + j, (*chip, 1 - c), me).wait_recv()
        for cp in first + passed:
            cp.wait_send()
        mine.wait()

    vmem = pl.BlockSpec(memory_space=pltpu.VMEM)
    return pl.pallas_call(
        body, name=name,
        out_shape=[jax.ShapeDtypeStruct((N_DEV * m_per, n), block.dtype), jax.ShapeDtypeStruct((8, LANES), F32)],
        in_specs=[vmem], out_specs=[vmem, vmem],
        scratch_shapes=[pltpu.SemaphoreType.DMA((7,)), pltpu.SemaphoreType.DMA((7,)), pltpu.SemaphoreType.DMA],
    )(block)


_ANY = pl.BlockSpec(memory_space=pl.ANY)


def _half(ref, c, rows):
    return ref.at[pl.ds(c * (rows // 2), rows // 2), :]


_HBM = pl.BlockSpec(memory_space=pltpu.HBM)
_SEM = pl.BlockSpec(memory_space=pltpu.SEMAPHORE)
_EFFECT = pltpu.SideEffectType.DATAFLOW_SIDE_EFFECTING


def _ici_start(name, srcs, land_shapes, plan, per_source=3, after=None):
    ns, nl = len(srcs), len(land_shapes)
    n_copies = per_source * ns
    n_in = ns + nl + (after is not None)

    def body(*refs):
        src_refs, land_refs = refs[:ns], refs[ns:ns + nl]
        send_sems, recv_sems = refs[n_in], refs[n_in + 1]
        token = refs[-1]
        for n, (src, dst, peer, _) in enumerate(plan(src_refs, land_refs)):
            pltpu.make_async_remote_copy(src_ref=src, dst_ref=dst, send_sem=send_sems.at[n], recv_sem=recv_sems.at[n],
                                         device_id=peer, device_id_type=MESH).start()
        token[...] = jnp.zeros_like(token)

    lands = [lax.empty(s.shape, s.dtype) for s in land_shapes]
    out = pl.pallas_call(
        body, name=name,
        out_shape=(pltpu.SemaphoreType.DMA((n_copies,)), pltpu.SemaphoreType.DMA((n_copies,)),
                   *[pltpu.HBM(a.shape, a.dtype) for a in list(srcs) + lands], jax.ShapeDtypeStruct((8, LANES), F32)),
        in_specs=[_HBM] * (ns + nl) + [_ANY] * (after is not None),
        out_specs=(_SEM, _SEM, *[_HBM] * (ns + nl), pl.BlockSpec(memory_space=pltpu.VMEM)),
        input_output_aliases={n: 2 + n for n in range(ns + nl)},
        compiler_params=pltpu.CompilerParams(has_side_effects=_EFFECT),
    )(*[pltpu.with_memory_space_constraint(a, pltpu.HBM) for a in list(srcs) + lands],
      *([] if after is None else [after]))
    return out[0], out[1], list(out[2:2 + ns]), list(out[2 + ns:2 + ns + nl]), out[-1]


def _ici_wait(name, send_sems, recv_sems, srcs, lands, plan, after):
    ns, nl = len(srcs), len(lands)
    after = list(after) if isinstance(after, (list, tuple)) else [after]

    def body(*refs):
        src_refs, land_refs = refs[:ns], refs[ns:ns + nl]
        send_sems, recv_sems = refs[ns + nl], refs[ns + nl + 1]
        for n, (src, _, peer, mine) in enumerate(plan(src_refs, land_refs)):
            cp = pltpu.make_async_remote_copy(src_ref=src, dst_ref=mine, send_sem=send_sems.at[n],
                                              recv_sem=recv_sems.at[n], device_id=peer, device_id_type=MESH)
            cp.wait_send()
            cp.wait_recv()

    out = pl.pallas_call(
        body, name=name, out_shape=[pltpu.HBM(a.shape, a.dtype) for a in list(srcs) + list(lands)],
        in_specs=[_HBM] * (ns + nl) + [_SEM, _SEM] + [_ANY] * len(after), out_specs=[_HBM] * (ns + nl),
        input_output_aliases={n: n for n in range(ns + nl)},
        compiler_params=pltpu.CompilerParams(has_side_effects=_EFFECT),
    )(*srcs, *lands, send_sems, recv_sems, *after)
    return list(out[:ns]), list(out[ns:])


def _own_slab(name, chip, w, after):
    R, C = w.shape
    tr, tc = _tiles(R, C)
    tied = [] if after is None else [after]

    def body(chip_ref, w_ref, *rest):
        stack_ref, token_ref = rest[-2:]
        stack_ref[0] = w_ref[...].astype(BF16)
        token_ref[...] = jnp.zeros_like(token_ref)

    small = pl.BlockSpec((8, LANES), lambda r, q, chip_ref: (0, 0))
    grid_spec = pltpu.PrefetchScalarGridSpec(
        num_scalar_prefetch=1, grid=(R // tr, C // tc),
        in_specs=[pl.BlockSpec((tr, tc), lambda r, q, chip_ref: (r, q))] + [small] * len(tied),
        out_specs=[pl.BlockSpec((1, tr, tc), lambda r, q, chip_ref: (chip_ref[0], r, q)), small])
    return pl.pallas_call(
        body, name=name, grid_spec=grid_spec,
        out_shape=[jax.ShapeDtypeStruct((N_CHIPS, R, C), BF16), jax.ShapeDtypeStruct((8, LANES), F32)],
        compiler_params=pltpu.CompilerParams(dimension_semantics=("arbitrary", "arbitrary")),
    )(chip, w, *tied)


def _gather_plan(src_refs, land_refs):
    x, y, c = _place()
    copies = []
    for stack in src_refs:
        R = stack.shape[1]
        own = _half(stack.at[2 * x + y], c, R)
        for cx, cy in [(1 - x, y), (x, 1 - y), (1 - x, 1 - y)]:
            copies.append((own, own, (cx, cy, c), _half(stack.at[2 * cx + cy], c, R)))
    return copies


def _pass_plan(src_refs, land_refs):
    x, y, c = _place()
    copies = []
    for land in src_refs:
        R = land.shape[1]
        for cx, cy in [(1 - x, y), (x, 1 - y), (1 - x, 1 - y)]:
            slot = land.at[2 * cx + cy]
            copies.append((_half(slot, c, R), _half(slot, c, R), (x, y, 1 - c), _half(slot, 1 - c, R)))
    return copies


def _share_plan(src_refs, land_refs):
    x, y, c = _place()
    return [(h, land, (x, y, 1 - c), land) for h, land in zip(src_refs, land_refs)]


def _pass_to_sibling(name, lands):
    nw = len(lands)

    def body(*refs):
        ins, outs = refs[:nw], refs[nw:2 * nw]
        send_sems, recv_sems = refs[2 * nw:]
        x, y, c = _place()
        chips = [(1 - x, y), (x, 1 - y), (1 - x, 1 - y)]
        copies = []
        for k in range(nw):
            R = ins[k].shape[1]
            for j, (cx, cy) in enumerate(chips):
                cp = pltpu.make_async_remote_copy(
                    src_ref=_half(ins[k].at[2 * cx + cy], c, R), dst_ref=_half(outs[k].at[2 * cx + cy], c, R),
                    send_sem=send_sems.at[3 * k + j], recv_sem=recv_sems.at[3 * k + j],
                    device_id=(x, y, 1 - c), device_id_type=MESH)
                cp.start()
                copies.append(cp)
        for k in range(nw):
            R = ins[k].shape[1]
            for j, (cx, cy) in enumerate(chips):
                pltpu.make_async_remote_copy(
                    src_ref=_half(ins[k].at[2 * cx + cy], c, R), dst_ref=_half(outs[k].at[2 * cx + cy], 1 - c, R),
                    send_sem=send_sems.at[3 * k + j], recv_sem=recv_sems.at[3 * k + j],
                    device_id=(x, y, 1 - c), device_id_type=MESH).wait_recv()
        for cp in copies:
            cp.wait_send()

    return pl.pallas_call(
        body, name=name, out_shape=[jax.ShapeDtypeStruct(a.shape, a.dtype) for a in lands],
        in_specs=[_ANY] * nw, out_specs=[_ANY] * nw, input_output_aliases={k: k for k in range(nw)},
        scratch_shapes=[pltpu.SemaphoreType.DMA((3 * nw,)), pltpu.SemaphoreType.DMA((3 * nw,))],
    )(*lands)


def _tie(vec, token):
    return vec + token[0:1, 0:1]


ROW_ALIGN = 16
TILE_ELEMS = 512 * 1024


def _tiles(rows, cols):
    fits = [t for t in range(ROW_ALIGN, min(rows, 256) + 1, ROW_ALIGN) if rows % t == 0]
    tr = fits[-1] if fits and fits[-1] >= 64 else rows
    tc = cols
    while tr * tc > TILE_ELEMS and tc % (2 * LANES) == 0:
        tc //= 2
    return tr, tc


def _scatter_plan(src_refs, land_refs):
    x, y, c = _place()
    copies = []
    for p, land in zip(src_refs, land_refs):
        for j, (cx, cy) in enumerate([(1 - x, y), (x, 1 - y), (1 - x, 1 - y)]):
            copies.append((p.at[2 * cx + cy], land.at[j], (cx, cy, c), land.at[j]))
    return copies


def _chip_add(name, chip, sums, recv):
    _, H, C = sums.shape
    tr, tc = _tiles(H, C)

    def body(chip_ref, p_ref, r_ref, o_ref):
        total = p_ref[0].astype(F32)
        for j in range(3):
            total = total + r_ref[j].astype(F32)
        o_ref[...] = total

    grid_spec = pltpu.PrefetchScalarGridSpec(
        num_scalar_prefetch=1, grid=(H // tr, C // tc),
        in_specs=[pl.BlockSpec((1, tr, tc), lambda r, q, chip_ref: (chip_ref[0], r, q)),
                  pl.BlockSpec((3, tr, tc), lambda r, q, chip_ref: (0, r, q))],
        out_specs=pl.BlockSpec((tr, tc), lambda r, q, chip_ref: (r, q)))
    return pl.pallas_call(
        body, name=name, grid_spec=grid_spec, out_shape=jax.ShapeDtypeStruct((H, C), F32),
        compiler_params=pltpu.CompilerParams(dimension_semantics=("parallel", "parallel")),
    )(chip, sums, recv)


def _pair_share(name, halves):
    nw = len(halves)

    def body(*refs):
        hs, outs = refs[:nw], refs[nw:2 * nw]
        send_sems, recv_sems = refs[2 * nw:]
        x, y, c = _place()
        copies = []
        for k in range(nw):
            cp = pltpu.make_async_remote_copy(
                src_ref=hs[k], dst_ref=outs[k], send_sem=send_sems.at[k], recv_sem=recv_sems.at[k],
                device_id=(x, y, 1 - c), device_id_type=MESH)
            cp.start()
            copies.append(cp)
        for cp in copies:
            cp.wait()

    return pl.pallas_call(
        body, name=name,
        out_shape=[jax.ShapeDtypeStruct(h.shape, h.dtype) for h in halves],
        in_specs=[_ANY] * nw, out_specs=[_ANY] * nw,
        scratch_shapes=[pltpu.SemaphoreType.DMA((nw,)), pltpu.SemaphoreType.DMA((nw,))],
    )(*halves)


def _adam_halves(name, core, w, g_own, g_other, m, v):
    R, C = w.shape
    H = R // 2
    tr, tc = _tiles(H, C)
    nr, nc = H // tr, C // tc

    def body(core_ref, w_ref, go_ref, gr_ref, m_ref, v_ref, g_ref, dl_ref, nm_ref, nv_ref):
        own = (pl.program_id(0) // nr) == core_ref[0]
        g = jnp.where(own, go_ref[...], gr_ref[...])
        g_ref[...] = g
        dl_ref[...], nm_ref[...], nv_ref[...] = _adamw(w_ref[...], g, m_ref[...], v_ref[...])

    blk = pl.BlockSpec((tr, tc), lambda r, q, core_ref: (r, q))

    def half_spec(is_own):
        def index(r, q, core_ref):
            mine = ((r // nr) == core_ref[0]) == is_own
            done = is_own == (core_ref[0] == 0)
            return (jnp.where(mine, r % nr, jnp.where(done, nr - 1, 0)), jnp.where(mine, q, jnp.where(done, nc - 1, 0)))
        return pl.BlockSpec((tr, tc), index)
    out = jax.ShapeDtypeStruct((R, C), F32)
    grid_spec = pltpu.PrefetchScalarGridSpec(
        num_scalar_prefetch=1, grid=(R // tr, nc), in_specs=[blk, half_spec(True), half_spec(False), blk, blk],
        out_specs=[blk] * 4)
    return pl.pallas_call(
        body, name=name, grid_spec=grid_spec, out_shape=[out] * 4,
        compiler_params=pltpu.CompilerParams(dimension_semantics=("parallel", "parallel"),
                                             vmem_limit_bytes=_vmem(20 * tr * tc * 4)),
    )(core, w, g_own, g_other, m, v)


def kernel(x, c, w_mod, b_mod, g_pre_mix, g_post_mix, w_in, b_forget, swa_sinks, w_out, g_pre_mlp, g_post_mlp, w_up, w_down, loss_target, m_w_mod, m_b_mod, m_g_pre_mix, m_g_post_mix, m_w_in, m_b_forget, m_swa_sinks, m_w_out, m_g_pre_mlp, m_g_post_mlp, m_w_up, m_w_down, v_w_mod, v_b_mod, v_g_pre_mix, v_g_post_mix, v_w_in, v_b_forget, v_swa_sinks, v_w_out, v_g_pre_mlp, v_g_post_mlp, v_w_up, v_w_down):
    S, D = x.shape[1], x.shape[2]
    n_heads = D // HEAD_DIM
    n_fox = n_heads // 2
    n_swa = n_heads - n_fox
    n_kv = max(1, n_swa // 4)
    fox_w, swa_w, kv_w = n_fox * HEAD_DIM, n_swa * HEAD_DIM, n_kv * HEAD_DIM
    main_w = 3 * fox_w + swa_w + 2 * kv_w
    in_w = main_w + n_fox
    mod_cols = w_mod.shape[2]

    ax, ay, ac = _place()
    chip = 2 * ax + ay
    dev = 2 * chip + ac
    chip_arr = jnp.reshape(chip, (1,)).astype(jnp.int32)
    core_arr = jnp.reshape(ac, (1,)).astype(jnp.int32)

    x2, tgt = x[0], loss_target[0]

    in_rows = in_w // N_CHIPS
    in_rows_pad = -(-in_rows // (2 * LANES)) * (2 * LANES)
    slab_w = N_CHIPS * in_rows_pad

    def rows_of(a):
        return jnp.pad(a[0].T, ((0, in_rows_pad - in_rows), (0, 0)))

    w_in_stack, token = _own_slab("own_slab_w_in", chip_arr, rows_of(w_in), None)

    c_all, _ = _allgather8("gather_c", _tie(c, token).reshape(8, D // 8))
    c_all = c_all.reshape(N_DEV, D)
    b_shard = lax.dynamic_slice_in_dim(b_mod, chip * mod_cols, mod_cols, axis=1)
    mod_shard = _mod_fwd(jnp.pad(c_all, ((0, 16 - N_DEV), (0, 0))), w_mod[0], b_shard)[:N_DEV]
    mod_all, token = _allgather8("gather_mod", mod_shard)
    mod_all = mod_all.reshape(N_CHIPS, 2, N_DEV, mod_cols)[:, 0]
    mod = lax.dynamic_index_in_dim(mod_all, dev, axis=1, keepdims=False).reshape(N_MOD, 1, D)
    sh_a, sc_a, gt_a, sh_m, sc_m, gt_m = [mod[n] for n in range(N_MOD)]

    def slab_cols(lo, hi):
        spans = []
        while lo < hi:
            s, r = divmod(lo, in_rows)
            n = min(hi - lo, in_rows - r)
            spans.append((s * in_rows_pad + r, s * in_rows_pad + r + n))
            lo += n
        return spans

    gate_lo = 3 * fox_w
    main_spans = slab_cols(0, gate_lo) + slab_cols(gate_lo + n_fox, in_w)
    (gate_first, gate_last), = slab_cols(gate_lo, gate_lo + n_fox)

    names = ["w_in", "w_out", "w_up", "w_down"]
    flights = {}
    for n, w in zip(names, [None, w_out[0], w_up[0], w_down[0]]):
        stack = w_in_stack if n == "w_in" else _own_slab("own_slab_" + n, chip_arr, w, token)[0]
        flights[n] = _ici_start("gather_start_" + n, [stack], [], _gather_plan, after=token)
        token = flights[n][4]
    sc_a = _tie(sc_a, token)

    def arrived(n, after):
        send, recv, stacks, _, _ = flights[n]
        stacks, _ = _ici_wait("gather_wait_" + n, send, recv, stacks, [], _gather_plan, after)
        return _ici_start("gather_pass_start_" + n, stacks, [], _pass_plan)

    def gathered(n, after, in_flight=None):
        if in_flight is None:
            send, recv, stacks, _, _ = flights[n]
            stacks, _ = _ici_wait("gather_wait_" + n, send, recv, stacks, [], _gather_plan, after)
            return _pass_to_sibling("gather_pass_" + n, stacks)[0]
        send, recv, stacks, _, _ = in_flight
        return _ici_wait("gather_pass_wait_" + n, send, recv, stacks, [], _pass_plan, after)[0][0]

    d_ff = N_CHIPS * w_up.shape[2]

    h = _pre_norm(x2, g_pre_mix, sc_a, sh_a)
    in_state = [rows_of(w_in)] + [rows_of(_tie(a, token)) for a in (m_w_in, v_w_in)]
    cos, sin_signed = _rope_tables(S)

    def pack(bm, gpm, gqm, gpl, gql, bf, sk):
        last = jnp.concatenate([bf, sk, jnp.zeros((1, D - n_fox - n_swa), F32)], axis=1)
        return jnp.concatenate([bm.reshape(N_MOD, D), gpm, gqm, gpl, gql, last, jnp.zeros((5, D), F32)], axis=0)

    small_state = [pack(b_mod, g_pre_mix, g_post_mix, g_pre_mlp, g_post_mlp, b_forget, swa_sinks),
                   pack(m_b_mod, m_g_pre_mix, m_g_post_mix, m_g_pre_mlp, m_g_post_mlp, m_b_forget, m_swa_sinks),
                   pack(v_b_mod, v_g_pre_mix, v_g_post_mix, v_g_pre_mlp, v_g_post_mlp, v_b_forget, v_swa_sinks)]
    ready = h[:8, :LANES].astype(F32) + cos[:8]
    w_slab_t = gathered("w_in", [ready] + in_state[1:] + small_state).reshape(slab_w, D)
    tm_p, tn_p = _fit(MM_TM, S), _fit(MM_TN if slab_w % MM_TN == 0 else MM_TN // 2, slab_w)
    win0 = gate_first // LANES * LANES
    win_j, win_off = divmod(win0, tn_p)
    assert win_off + 2 * LANES <= tn_p and gate_last - win0 <= 2 * LANES

    def proj_epilogue(acc, ex, outs):
        outs[0][...] = acc.astype(BF16)

        @pl.when(pl.program_id(1) == win_j)
        def _():
            outs[1][...] = acc[:, win_off:win_off + 2 * LANES]

    proj_slab, gate_win = _matmul(
        "in_proj", h, w_slab_t, "nt",
        [((S, slab_w), BF16, (tm_p, tn_p), lambda i, j: (i, j)), ((S, 2 * LANES), F32, (tm_p, 2 * LANES), lambda i, j: (i, 0))],
        proj_epilogue, tn=tn_p, revisits=True)
    proj = jnp.concatenate([proj_slab[:, lo:hi] for lo, hi in main_spans], axis=1)
    out_flight = arrived("w_out", proj_slab)
    fg = _tie(jnp.pad(gate_win[:, gate_first - win0:gate_last - win0], ((0, 0), (0, LANES - n_fox))), out_flight[4])
    b_pad = jnp.pad(b_forget, ((0, 0), (0, LANES - n_fox)))
    cum_row = _fox_gate_fwd(fg, b_pad)[:n_fox].reshape(n_fox, 1, S)
    fox_o, fox_lse = _fox_fwd(proj, cum_row, n_fox)

    rq = _rope("rope_fwd", proj, 3 * n_fox, n_swa + n_kv, cos, sin_signed)
    v_first = 3 * n_fox + n_swa + n_kv
    sinks = swa_sinks[0]
    swa_o, swa_lse = _swa_fwd(rq, proj, v_first, sinks, n_swa, n_kv)

    mixcat = jnp.concatenate([fox_o, swa_o], axis=1).astype(BF16)
    up_flight = arrived("w_up", mixcat)
    w_out_f = gathered("w_out", mixcat, out_flight).reshape(D, D)
    mix = _mm_plain("out_proj", mixcat, w_out_f, "nn", BF16, after=up_flight[4])
    x1, h2 = _post_mix(x2, mix, g_post_mix, gt_a, g_pre_mlp, sc_m, sh_m)
    w_up_f = gathered("w_up", h2, up_flight)

    tm_u, tn_u = _fit(MM_TM, S), _fit(MM_TN, d_ff)

    def up_epilogue(acc, ex, outs):
        outs[0][...] = acc.astype(BF16)
        r = jnp.maximum(acc, 0.0)
        outs[1][...] = (r * r).astype(BF16)

    ublk = ((S, d_ff), BF16, (tm_u, tn_u), lambda i, j: (i, j))
    u, a = _matmul("mlp_up", h2, w_up_f, "nn", [ublk, ublk], up_epilogue)
    w_down_f = gathered("w_down", a).reshape(d_ff, D)
    y = _mm_plain("mlp_down", a, w_down_f, "nn", BF16)

    dy, dout, loss_part, acc_mlp_post = _loss_and_post_mlp_bwd(x1, y, tgt, g_post_mlp, gt_m)

    def du_epilogue(acc, ex, outs):
        outs[0][...] = (acc * (2.0 * jnp.maximum(ex[0][...].astype(F32), 0.0))).astype(BF16)

    du = _matmul("mlp_down_bwd", dy, w_down_f, "nt", [ublk], du_epilogue,
                 extras=[(u, (tm_u, tn_u), lambda i, j: (i, j))])[0]
    def pair_send(tag, part):
        return _ici_start("grad_pair_start_" + tag, [part], [jax.ShapeDtypeStruct(part.shape, BF16)], _share_plan,
                          per_source=1)

    def pair_recv(tag, flight, after):
        send, recv, srcs, lands, _ = flight
        return _ici_wait("grad_pair_wait_" + tag, send, recv, srcs, lands, _share_plan, after)[1][0]

    def scatter_start(tag, sums, after=None):
        return _ici_start("grad_scatter_start_" + tag, sums,
                          [jax.ShapeDtypeStruct((3,) + p.shape[1:], BF16) for p in sums], _scatter_plan, after=after)

    def scatter_finish(tag, flight, after):
        send, recv, srcs, lands, _ = flight
        sums, received = _ici_wait("grad_scatter_wait_" + tag, send, recv, srcs, lands, _scatter_plan, after)
        return [_chip_add("chip_add_%s_%d" % (tag, k), chip_arr, p, r) for k, (p, r) in enumerate(zip(sums, received))]

    tm_g = _fit(MM_TM, D // 2)
    pair_down = pair_send("down", _grad_half("grad_w_down_a", core_arr, a, dy, N_CHIPS, 1, tm_g, True))
    pair_up = pair_send("up", _grad_half("grad_w_up_a", core_arr, h2, du, 1, N_CHIPS, tm_g, True, after=pair_down[4]))
    sum_down = _grad_half("grad_w_down_b", core_arr, a, dy, N_CHIPS, 1, tm_g, False,
                          recv=pair_recv("down", pair_down, pair_up[4]))
    sum_up = _grad_half("grad_w_up_b", core_arr, h2, du, 1, N_CHIPS, tm_g, False, recv=pair_recv("up", pair_up, sum_down))
    flight_mlp = scatter_start("mlp", [sum_up, sum_down])
    dh2 = _mm_plain("mlp_up_bwd", du, w_up_f, "nt", BF16, after=flight_mlp[4])
    dx1, dmix, acc_mid = _pre_mlp_and_post_mix_bwd(dh2, x1, dout, mix, _tie(g_pre_mlp, flight_mlp[4]), sc_m,
                                                   g_post_mix, gt_a)

    dmixcat = _mm_plain("out_proj_bwd", dmix, w_out_f, "nt", F32)

    fdq, fdk, fdv, dcum_row, dcum_q = _fox_bwd(proj, fox_o, dmixcat, fox_lse, cum_row, n_fox)
    dcum_k = jnp.pad(dcum_row.reshape(n_fox, S), ((0, LANES - n_fox), (0, 0)))
    dfg, db_forget = _fox_gate_bwd(dcum_k, dcum_q, fg, b_pad)

    group_w = (n_swa // n_kv) * HEAD_DIM
    sdq, sdk, sdv, dsink = _swa_bwd(rq, proj, v_first, sinks, swa_o, dmixcat, fox_w // group_w, swa_lse, n_swa, n_kv)
    drq = jnp.concatenate([sdq, jnp.transpose(sdk, (1, 0, 2)).reshape(S, kv_w).astype(BF16)], axis=1)
    d_sq_sk = _rope("rope_bwd", drq, 0, n_swa + n_kv, cos, -sin_signed)
    dsv = jnp.transpose(sdv, (1, 0, 2)).reshape(S, kv_w).astype(BF16)
    dproj = jnp.concatenate([fdq, fdk, fdv, d_sq_sk, dsv], axis=1)

    pieces = []
    for s in range(N_CHIPS):
        lo, hi = s * in_rows, (s + 1) * in_rows
        for src, first, last, shift in [(dproj, 0, gate_lo, 0), (dfg, gate_lo, gate_lo + n_fox, gate_lo),
                                        (dproj, gate_lo + n_fox, in_w, n_fox)]:
            if max(lo, first) < min(hi, last):
                pieces.append(src[:, max(lo, first) - shift:min(hi, last) - shift])
        pieces.append(jnp.zeros((S, in_rows_pad - in_rows), BF16))
    dproj_slab = jnp.concatenate(pieces, axis=1)

    tm_in, tm_out = in_rows_pad // 2, D // (2 * N_CHIPS)
    pair_in = pair_send("in", _grad_half("grad_w_in_a", core_arr, dproj_slab, h, N_CHIPS, 1, tm_in, True))
    pair_out = pair_send("out", _grad_half("grad_w_out_a", core_arr, mixcat, dmix, N_CHIPS, 1, tm_out, True,
                                           after=pair_in[4]))
    sum_in = _grad_half("grad_w_in_b", core_arr, dproj_slab, h, N_CHIPS, 1, tm_in, False,
                        recv=pair_recv("in", pair_in, pair_out[4]))
    sum_out = _grad_half("grad_w_out_b", core_arr, mixcat, dmix, N_CHIPS, 1, tm_out, False,
                         recv=pair_recv("out", pair_out, sum_in[0, :8, :LANES]))
    dh = _mm_plain("in_proj_bwd", dproj_slab, w_slab_t, "nn", BF16, tk=slab_w // 2,
                   after=sum_out[0, :8, :LANES].astype(F32))
    grad_x, acc_pre = _pre_mix_bwd(dh, x2, dx1, g_pre_mix, sc_a)

    zero_row = jnp.zeros((1, D), F32)
    tail = jnp.concatenate([db_forget[0:1, :n_fox], dsink[:, 0, :n_swa // n_kv].reshape(1, n_swa),
                            loss_part[0:1, 0:1], jnp.zeros((1, D - n_fox - n_swa - 1), F32)], axis=1)
    partial = jnp.concatenate([
        acc_pre[0:1], acc_pre[1:2], acc_mid[3:4], acc_mid[0:1], acc_mid[1:2], acc_mlp_post[0:1],
        acc_pre[2:3], acc_mid[4:5], acc_mid[2:3], acc_mlp_post[1:2], tail] + [zero_row] * 5, axis=0)
    gathered_small, token = _allgather8("gather_small_grads", partial)

    flight_mix = scatter_start("mix", [sum_in, sum_out], after=token)
    halves_mlp = scatter_finish("mlp", flight_mlp, flight_mix[4])
    share_up, share_down = [
        _ici_start("grad_share_start_" + n, [hv], [jax.ShapeDtypeStruct(hv.shape, F32)], _share_plan, per_source=1)
        for n, hv in zip(["up", "down"], halves_mlp)]

    def shared(tag, flight, after):
        send, recv, own, lands, _ = flight
        own, other = _ici_wait("grad_share_wait_" + tag, send, recv, own, lands, _share_plan, after)
        return own[0], other[0]

    def unpack(p):
        return {"b_mod": p[0:N_MOD].reshape(1, N_MOD * D), "g_pre_mix": p[6:7], "g_post_mix": p[7:8],
                "g_pre_mlp": p[8:9], "g_post_mlp": p[9:10], "b_forget": p[10:11, :n_fox],
                "swa_sinks": p[10:11, n_fox:n_fox + n_swa]}

    small_out = _small_update(gathered_small, _tie(small_state[0], share_down[4] + share_up[4]), small_state[1],
                              small_state[2])
    g_small, d_small, m_small, v_small = [unpack(p) for p in small_out]
    loss = small_out[0][N_MOD + 4, n_fox + n_swa]

    dmod_all = gathered_small.reshape(N_DEV, 16, D)[:, :N_MOD].reshape(N_DEV, N_MOD * D)
    dmod_shard = _tie(lax.dynamic_slice_in_dim(dmod_all, chip * mod_cols, mod_cols, axis=1), share_down[4])
    g_w_mod, d_w_mod, nm_w_mod, nv_w_mod = _mod_update(c_all.T, dmod_shard, w_mod[0], m_w_mod[0], v_w_mod[0])

    grads = dict(g_small, w_mod=g_w_mod[None])
    deltas = dict(d_small, w_mod=d_w_mod[None])
    new_m = dict(m_small, w_mod=nm_w_mod[None])
    new_v = dict(v_small, w_mod=nv_w_mod[None])
    weights = {"w_in": (w_in, m_w_in, v_w_in), "w_out": (w_out, m_w_out, v_w_out), "w_up": (w_up, m_w_up, v_w_up),
               "w_down": (w_down, m_w_down, v_w_down)}

    def big_update(n, own, other):
        transposed = n == "w_in"
        w, m, v = in_state if transposed else [a[0] for a in weights[n]]
        outs = _adam_halves("adam_" + n, core_arr, w, own, other, m, v)
        if transposed:
            outs = [o[:in_rows].T for o in outs]
        grads[n], deltas[n], new_m[n], new_v[n] = [o[None] for o in outs]

    big_update("w_down", *shared("down", share_down, d_w_mod[:8, :LANES] + small_out[1][:8, :LANES]))
    halves_mix = scatter_finish("mix", flight_mix, deltas["w_down"][0, :8, :LANES] + d_w_mod[:8, :LANES])
    others_mix = _pair_share("grad_pair_share_mix", halves_mix)
    big_update("w_in", halves_mix[0], others_mix[0])
    big_update("w_out", halves_mix[1], others_mix[1])
    big_update("w_up", *shared("up", share_up, deltas["w_out"][0, :8, :LANES] + deltas["w_in"][0, :8, :LANES]))

    order = ["w_mod", "b_mod", "g_pre_mix", "g_post_mix", "w_in", "b_forget", "swa_sinks", "w_out", "g_pre_mlp",
             "g_post_mlp", "w_up", "w_down"]
    return (loss, grad_x[None], *[grads[n] for n in order], *[deltas[n] for n in order],
            *[new_m[n] for n in order], *[new_v[n] for n in order])
```

```python
import jax
import jax.numpy as jnp
from jax import lax
from jax.experimental import pallas as pl
from jax.experimental.pallas import tpu as pltpu

F32 = jnp.float32
BF16 = jnp.bfloat16
MESH = pl.DeviceIdType.MESH

HEAD_DIM = 128
SWA_BLOCK = 128
ROPE_THETA = 10000.0
NORM_EPS = 1e-6
NEG = -1e30
N_MOD = 6
ADAM_LR = 0.001
ADAM_B1 = 0.9
ADAM_B2 = 0.999
ADAM_EPS = 1e-08
ADAM_WD = 0.01
ADAM_STEP = 10
N_CHIPS = 4
N_DEV = 8
LANES = 128
VMEM_CAP = 60 * 1024 * 1024

_NN = (((1,), (0,)), ((), ()))
_NT = (((1,), (1,)), ((), ()))
_TN = (((0,), (0,)), ((), ()))


def _vmem(nbytes):
    return int(min(VMEM_CAP, nbytes * 5 // 4 + (4 << 20)))


def _nbytes(shape, dtype):
    n = 1
    for s in shape:
        n *= s
    return n * jnp.dtype(dtype).itemsize


def _fit(t, n):
    t = min(t, n)
    assert n % t == 0, (t, n)
    return t


MM_TM, MM_TN, MM_TK = 1024, 1024, 2048


def _matmul(name, a, b, mode, out_defs, epilogue, extras=(), tm=MM_TM, tn=MM_TN, tk=MM_TK, revisits=False,
            row_sel=None, col_sel=None, carry=()):
    stacked = b.ndim == 3
    b_rows, b_cols = b.shape[-2], b.shape[-1] * (b.shape[0] if stacked else 1)
    if mode == "nn":
        (M, K), (K2, N) = a.shape, (b_rows, b_cols)
    elif mode == "nt":
        (M, K), (N, K2) = a.shape, (b_rows, b_cols)
    else:
        (K, M), (K2, N) = a.shape, (b_rows, b_cols)
    assert K == K2 and not (stacked and mode == "tn"), (a.shape, b.shape, mode)
    tm = _fit(tm, M)
    tn = _fit(tn, b.shape[-1] if stacked and mode == "nn" else N)
    tk = _fit(tk, b.shape[-1] if stacked and mode == "nt" else K)
    nk = K // tk
    dims = {"nn": _NN, "nt": _NT, "tn": _TN}[mode]
    assert row_sel is None or col_sel is None
    picked = row_sel or col_sel
    if row_sel is None:
        grid_m, a_row = M // tm, lambda i, *sel: i
    else:
        grid_m, a_row = row_sel[2], lambda i, *sel: row_sel[1](i, sel[0])
    if col_sel is None:
        grid_n, b_col = N // tn, lambda j, *sel: j
    else:
        grid_n, b_col = col_sel[2], lambda j, *sel: col_sel[1](j, sel[0])
    a_spec = (pl.BlockSpec((tk, tm), lambda i, j, k, *sel: (k, a_row(i, *sel))) if mode == "tn"
              else pl.BlockSpec((tm, tk), lambda i, j, k, *sel: (a_row(i, *sel), k)))
    if stacked:
        assert col_sel is None
        per = b.shape[-1] // (tk if mode == "nt" else tn)
        b_spec = (pl.BlockSpec((1, tn, tk), lambda i, j, k, *sel: (k // per, j, k % per)) if mode == "nt"
                  else pl.BlockSpec((1, tk, tn), lambda i, j, k, *sel: (j // per, k, j % per)))
    else:
        b_spec = (pl.BlockSpec((tn, tk), lambda i, j, k, *sel: (b_col(j, *sel), k)) if mode == "nt"
                  else pl.BlockSpec((tk, tn), lambda i, j, k, *sel: (k, b_col(j, *sel))))
    n_ex, n_out, n_carry = len(extras), len(out_defs), len(carry)

    def body(*refs):
        if picked is not None:
            sel_ref, refs = refs[0], refs[1:]
        a_ref, b_ref = refs[0], refs[1]
        ex = refs[2:2 + n_ex]
        outs = refs[2 + n_ex + n_carry:2 + n_ex + n_carry + n_out]
        b_blk = b_ref[0] if stacked else b_ref[...]
        prod = lax.dot_general(a_ref[...], b_blk, dims, preferred_element_type=F32)
        if col_sel is not None:
            assert nk == 1
            epilogue(prod, ex, outs, col_sel[1](pl.program_id(1), sel_ref))
        elif nk == 1:
            epilogue(prod, ex, outs)
        else:
            acc_ref = refs[-1]
            k = pl.program_id(2)

            @pl.when(k == 0)
            def _():
                acc_ref[...] = prod

            @pl.when(k > 0)
            def _():
                acc_ref[...] += prod

            @pl.when(k == nk - 1)
            def _():
                epilogue(acc_ref[...], ex, outs)

    def wrap(f):
        return lambda i, j, k, *sel: f(i, b_col(j, *sel))

    in_specs = ([a_spec, b_spec] + [pl.BlockSpec(blk, wrap(f)) for _, blk, f in extras]
                + [pl.BlockSpec(memory_space=pl.ANY)] * n_carry)
    out_specs = [pl.BlockSpec(blk, wrap(f)) for _, _, blk, f in out_defs]
    out_shape = [jax.ShapeDtypeStruct(s, d) for s, d, _, _ in out_defs]
    need = 2 * (tm * tk + tk * tn) * a.dtype.itemsize + 3 * tm * tn * 4
    need += sum(2 * _nbytes(blk, arr.dtype) for arr, blk, _ in extras)
    need += sum(2 * _nbytes(blk, d) for _, d, blk, _ in out_defs)
    grid = (grid_m, grid_n, nk)
    scratch = [pltpu.VMEM((tm, tn), F32)] if nk > 1 else []
    params = pltpu.CompilerParams(
        dimension_semantics=("parallel", "arbitrary" if revisits else "parallel", "arbitrary"),
        vmem_limit_bytes=_vmem(need))
    operands = (a, b, *[arr for arr, _, _ in extras], *carry)
    first_carry = (picked is not None) + 2 + n_ex
    aliases = {first_carry + n: n for n in range(n_carry)}
    if picked is None:
        return pl.pallas_call(body, name=name, grid=grid, in_specs=in_specs, out_specs=out_specs, out_shape=out_shape,
                              scratch_shapes=scratch, input_output_aliases=aliases, compiler_params=params)(*operands)
    grid_spec = pltpu.PrefetchScalarGridSpec(num_scalar_prefetch=1, grid=grid, in_specs=in_specs, out_specs=out_specs,
                                             scratch_shapes=scratch)
    return pl.pallas_call(body, name=name, grid_spec=grid_spec, out_shape=out_shape, input_output_aliases=aliases,
                          compiler_params=params)(picked[0], *operands)


def _grad_half(name, core, a, b, row_slabs, col_slabs, tm, other, recv=None, after=None):
    (_, M), (_, N) = a.shape, b.shape
    H = M // (2 * row_slabs)
    nh = H // tm
    tn = _fit(MM_TN, N // col_slabs)
    per = N // col_slabs // tn

    def a_block(i, core_ref):
        half = (1 - core_ref[0]) if other else core_ref[0]
        return (i // nh) * (2 * nh) + half * nh + i % nh

    def out_index(i, j):
        return (j // per, i, j % per) if col_slabs > 1 else (i // nh, i % nh, j)

    slabs = max(row_slabs, col_slabs)
    out_def = ((slabs, H, N // col_slabs), BF16, (1, tm, tn), out_index)

    def epilogue(acc, ex, outs):
        outs[0][0] = (acc if recv is None else acc + ex[0][0].astype(F32)).astype(BF16)

    extras = ([] if recv is None else [(recv, (1, tm, tn), out_index)]) + ([] if after is None else [_behind(after)])
    return _matmul(name, a, b, "tn", [out_def], epilogue, extras=extras, tm=tm, tn=tn,
                   row_sel=(core, a_block, row_slabs * nh))[0]


def _behind(token):
    return (token, (8, LANES), lambda i, j: (0, 0))


def _mm_plain(name, a, b, mode, out_dtype, after=None, **tiles):
    if mode == "nn":
        M, N = a.shape[0], b.shape[-1] * (b.shape[0] if b.ndim == 3 else 1)
    elif mode == "nt":
        M, N = a.shape[0], b.shape[-2]
    else:
        M, N = a.shape[1], b.shape[1]
    tm, tn = _fit(tiles.get("tm", MM_TM), M), _fit(tiles.get("tn", MM_TN), N)

    def epi(acc, ex, outs):
        outs[0][...] = acc.astype(out_dtype)

    return _matmul(name, a, b, mode, [((M, N), out_dtype, (tm, tn), lambda i, j: (i, j))], epi,
                   extras=[] if after is None else [_behind(after)], **tiles)[0]


def _rstd(v):
    return lax.rsqrt(jnp.mean(v * v, axis=-1, keepdims=True) + NORM_EPS)


ROW_TILE = 256


def _row_call(name, body, row_ins, vec_ins, row_outs, acc_outs, S, D):
    tr = _fit(ROW_TILE, S)
    row_spec = pl.BlockSpec((tr, D), lambda r: (r, 0))
    vec_spec = pl.BlockSpec((1, D), lambda r: (0, 0))
    in_specs = [row_spec] * len(row_ins) + [vec_spec] * len(vec_ins)
    out_specs = [row_spec] * len(row_outs) + [pl.BlockSpec(shp, lambda r: (0, 0)) for shp in acc_outs]
    out_shape = [jax.ShapeDtypeStruct((S, D), d) for d in row_outs] + [jax.ShapeDtypeStruct(shp, F32) for shp in acc_outs]
    need = sum(2 * tr * D * a.dtype.itemsize for a in row_ins) + sum(2 * tr * D * jnp.dtype(d).itemsize for d in row_outs)
    need += 8 * tr * D * 4
    return pl.pallas_call(
        body, name=name, grid=(S // tr,), in_specs=in_specs, out_specs=out_specs, out_shape=out_shape,
        compiler_params=pltpu.CompilerParams(dimension_semantics=("arbitrary",), vmem_limit_bytes=_vmem(need)),
    )(*row_ins, *vec_ins)


def _acc_rows(ref, rows):
    @pl.when(pl.program_id(0) == 0)
    def _():
        ref[...] = jnp.zeros_like(ref)
    for n, r in enumerate(rows):
        ref[n:n + 1, :] += r


def _pre_norm(x, g, sc, sh):
    S, D = x.shape

    def body(x_ref, g_ref, sc_ref, sh_ref, h_ref):
        xv = x_ref[...]
        xn = xv * _rstd(xv)
        h_ref[...] = (xn * g_ref[...] * (1.0 + sc_ref[...]) + sh_ref[...]).astype(BF16)

    return _row_call("pre_norm_mix", body, [x], [g, sc, sh], [BF16], [], S, D)[0]


def _post_mix(x, mix, g_post, gt, g_pre, sc, sh):
    S, D = x.shape

    def body(x_ref, mix_ref, gp_ref, gt_ref, g2_ref, sc_ref, sh_ref, x1_ref, h2_ref):
        mv = mix_ref[...].astype(F32)
        x1 = x_ref[...] + gt_ref[...] * (mv * _rstd(mv) * gp_ref[...])
        x1_ref[...] = x1
        h2_ref[...] = (x1 * _rstd(x1) * g2_ref[...] * (1.0 + sc_ref[...]) + sh_ref[...]).astype(BF16)

    return _row_call("post_mix_pre_mlp", body, [x, mix], [g_post, gt, g_pre, sc, sh], [F32, BF16], [], S, D)


def _loss_and_post_mlp_bwd(x1, y, target, g_post, gt):
    S, D = x1.shape

    def body(x1_ref, y_ref, t_ref, g_ref, gt_ref, dy_ref, dout_ref, loss_ref, acc_ref):
        yv = y_ref[...].astype(F32)
        r = _rstd(yv)
        yh = yv * r
        n = yh * g_ref[...]
        diff = x1_ref[...] + gt_ref[...] * n - t_ref[...]
        dout = diff * (1.0 / D)
        dout_ref[...] = dout
        dn = dout * gt_ref[...]
        dyh = dn * g_ref[...]
        dy_ref[...] = (r * (dyh - yh * jnp.mean(dyh * yh, axis=-1, keepdims=True))).astype(BF16)
        _acc_rows(acc_ref, [jnp.sum(dout * n, axis=0, keepdims=True), jnp.sum(dn * yh, axis=0, keepdims=True)])

        @pl.when(pl.program_id(0) == 0)
        def _():
            loss_ref[...] = jnp.zeros_like(loss_ref)
        loss_ref[...] += jnp.full(loss_ref.shape, (0.5 / D) * jnp.sum(diff * diff), F32)

    return _row_call("loss_post_mlp_bwd", body, [x1, y, target], [g_post, gt], [BF16, F32],
                     [(8, LANES), (8, D)], S, D)


def _pre_mlp_and_post_mix_bwd(dh2, x1, dout, mix, g_pre, sc, g_post, gt):
    S, D = x1.shape

    def body(dh_ref, x1_ref, dout_ref, mix_ref, g_ref, sc_ref, gp_ref, gt_ref, dx1_ref, dmix_ref, acc_ref):
        dh = dh_ref[...].astype(F32)
        x1v = x1_ref[...]
        r3 = _rstd(x1v)
        xn = x1v * r3
        dxn = dh * (1.0 + sc_ref[...]) * g_ref[...]
        dx1 = dout_ref[...] + r3 * (dxn - xn * jnp.mean(dxn * xn, axis=-1, keepdims=True))
        dx1_ref[...] = dx1
        mv = mix_ref[...].astype(F32)
        r2 = _rstd(mv)
        mh = mv * r2
        dn = dx1 * gt_ref[...]
        dmh = dn * gp_ref[...]
        dmix_ref[...] = (r2 * (dmh - mh * jnp.mean(dmh * mh, axis=-1, keepdims=True))).astype(BF16)
        _acc_rows(acc_ref, [
            jnp.sum(dh, axis=0, keepdims=True),
            jnp.sum(dh * xn * g_ref[...], axis=0, keepdims=True),
            jnp.sum(dh * (1.0 + sc_ref[...]) * xn, axis=0, keepdims=True),
            jnp.sum(dx1 * mh * gp_ref[...], axis=0, keepdims=True),
            jnp.sum(dn * mh, axis=0, keepdims=True)])

    return _row_call("pre_mlp_post_mix_bwd", body, [dh2, x1, dout, mix], [g_pre, sc, g_post, gt], [F32, BF16],
                     [(8, D)], S, D)


def _pre_mix_bwd(dh, x, dx1, g_pre, sc):
    S, D = x.shape

    def body(dh_ref, x_ref, dx1_ref, g_ref, sc_ref, gx_ref, acc_ref):
        dhv = dh_ref[...].astype(F32)
        xv = x_ref[...]
        r = _rstd(xv)
        xn = xv * r
        dxn = dhv * (1.0 + sc_ref[...]) * g_ref[...]
        gx_ref[...] = dx1_ref[...] + r * (dxn - xn * jnp.mean(dxn * xn, axis=-1, keepdims=True))
        _acc_rows(acc_ref, [
            jnp.sum(dhv, axis=0, keepdims=True),
            jnp.sum(dhv * xn * g_ref[...], axis=0, keepdims=True),
            jnp.sum(dhv * (1.0 + sc_ref[...]) * xn, axis=0, keepdims=True)])

    return _row_call("pre_mix_bwd", body, [dh, x, dx1], [g_pre, sc], [F32], [(8, D)], S, D)


CUM_BLOCK = 256


def _tri(n, upper):
    r = lax.broadcasted_iota(jnp.int32, (n, n), 0)
    c = lax.broadcasted_iota(jnp.int32, (n, n), 1)
    return ((c >= r) if upper else (c <= r)).astype(F32)


def _fox_gate_fwd(fg, b_pad):
    S = fg.shape[0]
    cb = _fit(CUM_BLOCK, S)

    def body(fg_ref, b_ref, cumt_ref, cum_ref):
        low = _tri(cb, False)
        carry = jnp.zeros((1, LANES), F32)
        for n in range(S // cb):
            z = fg_ref[n * cb:(n + 1) * cb, :] + b_ref[...]
            logf = jnp.minimum(z, 0.0) - jnp.log(1.0 + jnp.exp(-jnp.abs(z)))
            blk = jnp.dot(low, logf, precision=lax.Precision.HIGHEST, preferred_element_type=F32) + carry
            cum_ref[n * cb:(n + 1) * cb, :] = blk
            carry = blk[cb - 1:cb, :]
        cumt_ref[...] = cum_ref[...].T

    return pl.pallas_call(
        body, name="fox_gate_fwd", out_shape=jax.ShapeDtypeStruct((LANES, S), F32),
        scratch_shapes=[pltpu.VMEM((S, LANES), F32)],
        compiler_params=pltpu.CompilerParams(vmem_limit_bytes=_vmem(6 * S * LANES * 4)),
    )(fg, b_pad)


def _fox_gate_bwd(dcum_k, dcum_q, fg, b_pad):
    S = fg.shape[0]
    n_fox = dcum_q.shape[0]
    cb = _fit(CUM_BLOCK, S)

    def body(dk_ref, dq_ref, fg_ref, b_ref, dfg_ref, db_ref, dc_ref):
        lane = lax.broadcasted_iota(jnp.int32, (S, LANES), 1)
        dc = dk_ref[...].T
        for h in range(n_fox):
            dc = dc + jnp.where(lane == h, dq_ref[h], 0.0)
        dc_ref[...] = dc
        up = _tri(cb, True)
        carry = jnp.zeros((1, LANES), F32)
        db = jnp.zeros((1, LANES), F32)
        for n in reversed(range(S // cb)):
            blk = jnp.dot(up, dc_ref[n * cb:(n + 1) * cb, :], precision=lax.Precision.HIGHEST,
                          preferred_element_type=F32) + carry
            carry = blk[0:1, :]
            z = fg_ref[n * cb:(n + 1) * cb, :] + b_ref[...]
            dfg = blk * (1.0 / (1.0 + jnp.exp(z)))
            dfg_ref[n * cb:(n + 1) * cb, :] = dfg.astype(BF16)
            db = db + jnp.sum(dfg, axis=0, keepdims=True)
        db_ref[...] = jnp.broadcast_to(db, db_ref.shape)

    return pl.pallas_call(
        body, name="fox_gate_bwd",
        out_shape=[jax.ShapeDtypeStruct((S, LANES), BF16), jax.ShapeDtypeStruct((8, LANES), F32)],
        scratch_shapes=[pltpu.VMEM((S, LANES), F32)],
        compiler_params=pltpu.CompilerParams(vmem_limit_bytes=_vmem((8 + 2 * n_fox) * S * LANES * 4)),
    )(dcum_k, dcum_q, fg, b_pad)


FOX_TILE = 512


LOG2E = 1.4426950408889634


def _fox_scores(q, k, ck2, masked, t):
    s = lax.dot_general(q, k, _NT, preferred_element_type=F32) * (HEAD_DIM ** -0.5 * LOG2E) - ck2
    if masked:
        row = lax.broadcasted_iota(jnp.int32, (t, t), 0)
        col = lax.broadcasted_iota(jnp.int32, (t, t), 1)
        s = jnp.where(col <= row, s, NEG)
    return s


def _fox_fwd(proj, cum_row, n_fox):
    S = proj.shape[0]
    t = _fit(FOX_TILE, S)
    nq = S // t

    def body(q_ref, k_ref, v_ref, ck_ref, o_ref, lse_ref):
        def q_block(qi, _):
            q0 = pl.multiple_of(qi * t, t)
            q = q_ref[pl.ds(q0, t), :]

            def kv_block(j, carry, masked):
                m, l, acc = carry
                k0 = pl.multiple_of(j * t, t)
                s = _fox_scores(q, k_ref[pl.ds(k0, t), :], ck_ref[0, :, pl.ds(k0, t)] * LOG2E, masked, t)
                m_new = jnp.maximum(m, jnp.max(s, axis=-1, keepdims=True))
                alpha = jnp.exp2(m - m_new)
                p = jnp.exp2(s - m_new)
                l = alpha * l + jnp.sum(p, axis=-1, keepdims=True)
                acc = alpha * acc + jnp.dot(p.astype(BF16), v_ref[pl.ds(k0, t), :], preferred_element_type=F32)
                return m_new, l, acc

            init = (jnp.full((t, 1), NEG, F32), jnp.zeros((t, 1), F32), jnp.zeros((t, HEAD_DIM), F32))
            carry = lax.fori_loop(0, qi, lambda j, cr: kv_block(j, cr, False), init)
            m, l, acc = kv_block(qi, carry, True)
            o_ref[pl.ds(q0, t), :] = acc / l
            lse_ref[0, pl.ds(q0, t), :] = jnp.broadcast_to(m + jnp.log(l) * LOG2E, (t, LANES))
            return 0

        lax.fori_loop(0, nq, q_block, 0)

    col = lambda off: pl.BlockSpec((S, HEAD_DIM), lambda h: (0, off + h))
    per_head = pl.BlockSpec((1, S, LANES), lambda h: (h, 0, 0))
    return pl.pallas_call(
        body, name="fox_fwd", grid=(n_fox,),
        in_specs=[col(0), col(n_fox), col(2 * n_fox), pl.BlockSpec((1, 1, S), lambda h: (h, 0, 0))],
        out_specs=[pl.BlockSpec((S, HEAD_DIM), lambda h: (0, h)), per_head],
        out_shape=[jax.ShapeDtypeStruct((S, n_fox * HEAD_DIM), F32), jax.ShapeDtypeStruct((n_fox, S, LANES), F32)],
        compiler_params=pltpu.CompilerParams(dimension_semantics=("parallel",),
                                             vmem_limit_bytes=_vmem(16 * S * HEAD_DIM * 4 + 12 * t * t * 4)),
    )(proj, proj, proj, cum_row)


def _fox_bwd(proj, o, do, lse_b, cum_row, n_fox):
    S = proj.shape[0]
    t = _fit(FOX_TILE, S)
    nq = S // t
    scale = HEAD_DIM ** -0.5

    def body(q_ref, k_ref, v_ref, o_ref, do_ref, lse_ref, ck_ref, dq_ref, dk_ref, dv_ref, dc_ref, dcq_ref,
             dq_acc, delta_ref):
        dq_acc[...] = jnp.zeros_like(dq_acc)
        dcq_ref[...] = jnp.zeros_like(dcq_ref)

        def delta_block(qi, _):
            q0 = pl.multiple_of(qi * t, t)
            d = jnp.sum(do_ref[pl.ds(q0, t), :] * o_ref[pl.ds(q0, t), :], axis=-1, keepdims=True)
            delta_ref[pl.ds(q0, t), :] = jnp.broadcast_to(d, (t, LANES))
            return 0

        lax.fori_loop(0, nq, delta_block, 0)

        def kv_block(j, _):
            k0 = pl.multiple_of(j * t, t)
            k = k_ref[pl.ds(k0, t), :]
            v = v_ref[pl.ds(k0, t), :]
            ck2 = ck_ref[0, :, pl.ds(k0, t)] * LOG2E

            def q_block(qi, carry, masked):
                dk, dv, dc = carry
                q0 = pl.multiple_of(qi * t, t)
                q = q_ref[pl.ds(q0, t), :]
                dov = do_ref[pl.ds(q0, t), :].astype(BF16)
                p = jnp.exp2(_fox_scores(q, k, ck2, masked, t) - lse_ref[0, pl.ds(q0, t), :][:, :1])
                dp = lax.dot_general(dov, v, _NT, preferred_element_type=F32)
                ds = p * (dp - delta_ref[pl.ds(q0, t), :][:, :1])
                dsb = ds.astype(BF16)
                dv = dv + lax.dot_general(p.astype(BF16), dov, _TN, preferred_element_type=F32)
                dk = dk + lax.dot_general(dsb, q, _TN, preferred_element_type=F32)
                dq_acc[pl.ds(q0, t), :] += jnp.dot(dsb, k, preferred_element_type=F32)
                dc = dc - jnp.sum(ds, axis=0, keepdims=True)
                dcq_ref[0, pl.ds(q0, t), :] += jnp.broadcast_to(jnp.sum(ds, axis=1, keepdims=True), (t, LANES))
                return dk, dv, dc

            init = (jnp.zeros((t, HEAD_DIM), F32), jnp.zeros((t, HEAD_DIM), F32), jnp.zeros((1, t), F32))
            carry = q_block(j, init, True)
            dk, dv, dc = lax.fori_loop(j + 1, nq, lambda qi, cr: q_block(qi, cr, False), carry)
            dk_ref[pl.ds(k0, t), :] = (dk * scale).astype(BF16)
            dv_ref[pl.ds(k0, t), :] = dv.astype(BF16)
            dc_ref[0, :, pl.ds(k0, t)] = dc
            return 0

        lax.fori_loop(0, nq, kv_block, 0)
        dq_ref[...] = (dq_acc[...] * scale).astype(BF16)

    col = lambda off: pl.BlockSpec((S, HEAD_DIM), lambda h: (0, off + h))
    per_head = pl.BlockSpec((1, S, LANES), lambda h: (h, 0, 0))
    row = pl.BlockSpec((1, 1, S), lambda h: (h, 0, 0))
    grad = jax.ShapeDtypeStruct((S, n_fox * HEAD_DIM), BF16)
    return pl.pallas_call(
        body, name="fox_bwd", grid=(n_fox,),
        in_specs=[col(0), col(n_fox), col(2 * n_fox), col(0), col(0), per_head, row],
        out_specs=[col(0), col(0), col(0), row, per_head],
        out_shape=[grad, grad, grad, jax.ShapeDtypeStruct((n_fox, 1, S), F32), jax.ShapeDtypeStruct((n_fox, S, LANES), F32)],
        scratch_shapes=[pltpu.VMEM((S, HEAD_DIM), F32), pltpu.VMEM((S, LANES), F32)],
        compiler_params=pltpu.CompilerParams(dimension_semantics=("parallel",),
                                             vmem_limit_bytes=_vmem(24 * S * HEAD_DIM * 4 + 16 * t * t * 4)),
    )(proj, proj, proj, o, do, lse_b, cum_row)


def _rope_tables(S):
    half = HEAD_DIM // 2
    inv_freq = 1.0 / (ROPE_THETA ** (jnp.arange(half, dtype=F32) * (2.0 / HEAD_DIM)))
    ang = jnp.arange(S).astype(F32)[:, None] * inv_freq[None, :]
    cos, sin = jnp.cos(ang), jnp.sin(ang)
    return jnp.concatenate([cos, cos], axis=-1), jnp.concatenate([-sin, sin], axis=-1)


def _rope(name, src, first_block, n_blocks, cos, sin_signed):
    S = src.shape[0]

    def body(x_ref, cos_ref, sin_ref, o_ref):
        xv = x_ref[...].astype(F32)
        o_ref[...] = (xv * cos_ref[...] + pltpu.roll(xv, HEAD_DIM // 2, 1) * sin_ref[...]).astype(BF16)

    table = pl.BlockSpec((S, HEAD_DIM), lambda n: (0, 0))
    return pl.pallas_call(
        body, name=name, grid=(n_blocks,),
        in_specs=[pl.BlockSpec((S, HEAD_DIM), lambda n: (0, first_block + n)), table, table],
        out_specs=pl.BlockSpec((S, HEAD_DIM), lambda n: (0, n)),
        out_shape=jax.ShapeDtypeStruct((S, n_blocks * HEAD_DIM), BF16),
        compiler_params=pltpu.CompilerParams(dimension_semantics=("parallel",),
                                             vmem_limit_bytes=_vmem(12 * S * HEAD_DIM * 4)),
    )(src, cos, sin_signed)


def _swa_tile(q_ref, kp_ref, kc_ref, n, group, scale):
    B = SWA_BLOCK
    qs = jnp.concatenate([q_ref[:, g * HEAD_DIM:(g + 1) * HEAD_DIM] for g in range(group)], axis=0)
    kcat = jnp.concatenate([kp_ref[...], kc_ref[...]], axis=0)
    s = lax.dot_general(qs, kcat, _NT, preferred_element_type=F32) * scale
    qi = lax.broadcasted_iota(jnp.int32, (group * B, 2 * B), 0) % B
    kj = lax.broadcasted_iota(jnp.int32, (group * B, 2 * B), 1)
    diff = qi + B - kj
    mask = (diff >= 0) & (diff < B) & ((n * B + kj - B) >= 0)
    return qs, kcat, jnp.where(mask, s, NEG)


def _swa_sink_col(sink_ref, kv, group):
    head = lax.broadcasted_iota(jnp.int32, (group * SWA_BLOCK, 1), 0) // SWA_BLOCK
    col = jnp.zeros((group * SWA_BLOCK, 1), F32)
    for g in range(group):
        col = jnp.where(head == g, sink_ref[kv * group + g], col)
    return col


def _swa_specs(n_kv, group, q_first, k_first, v_first):
    B = SWA_BLOCK
    prev = lambda n: jnp.maximum(n - 1, 0)
    return [
        pl.BlockSpec((B, group * HEAD_DIM), lambda kv, n: (n, q_first + kv)),
        pl.BlockSpec((B, HEAD_DIM), lambda kv, n: (prev(n), k_first + kv)),
        pl.BlockSpec((B, HEAD_DIM), lambda kv, n: (n, k_first + kv)),
        pl.BlockSpec((B, HEAD_DIM), lambda kv, n: (prev(n), v_first + kv)),
        pl.BlockSpec((B, HEAD_DIM), lambda kv, n: (n, v_first + kv)),
    ]


def _swa_fwd(rq, proj, v_first, sinks, n_q, n_kv):
    S = rq.shape[0]
    B = SWA_BLOCK
    group = n_q // n_kv
    scale = HEAD_DIM ** -0.5

    def body(q_ref, kp_ref, kc_ref, vp_ref, vc_ref, sink_ref, o_ref, lse_ref):
        kv, n = pl.program_id(0), pl.program_id(1)
        _, _, s = _swa_tile(q_ref, kp_ref, kc_ref, n, group, scale)
        sink = _swa_sink_col(sink_ref, kv, group)
        m = jnp.maximum(jnp.max(s, axis=-1, keepdims=True), sink)
        p = jnp.exp(s - m)
        denom = jnp.sum(p, axis=-1, keepdims=True) + jnp.exp(sink - m)
        vcat = jnp.concatenate([vp_ref[...], vc_ref[...]], axis=0)
        o = jnp.dot((p / denom).astype(BF16), vcat, preferred_element_type=F32)
        lse = m + jnp.log(denom)
        for g in range(group):
            o_ref[:, g * HEAD_DIM:(g + 1) * HEAD_DIM] = o[g * B:(g + 1) * B, :]
            lse_ref[0, :, g * LANES:(g + 1) * LANES] = jnp.broadcast_to(lse[g * B:(g + 1) * B, :], (B, LANES))

    specs = _swa_specs(n_kv, group, 0, n_q, v_first)
    q_blk = pl.BlockSpec((B, group * HEAD_DIM), lambda kv, n: (n, kv))
    return pl.pallas_call(
        body, name="swa_fwd", grid=(n_kv, S // B),
        in_specs=specs + [pl.BlockSpec(memory_space=pltpu.SMEM)],
        out_specs=[q_blk, pl.BlockSpec((1, B, group * LANES), lambda kv, n: (kv, n, 0))],
        out_shape=[jax.ShapeDtypeStruct((S, n_q * HEAD_DIM), F32), jax.ShapeDtypeStruct((n_kv, S, group * LANES), F32)],
        compiler_params=pltpu.CompilerParams(dimension_semantics=("parallel", "arbitrary")),
    )(rq, rq, rq, proj, proj, sinks)


def _swa_bwd(rq, proj, v_first, sinks, o, do, do_first, lse_b, n_q, n_kv):
    S = rq.shape[0]
    B = SWA_BLOCK
    group = n_q // n_kv
    scale = HEAD_DIM ** -0.5

    def body(q_ref, kp_ref, kc_ref, vp_ref, vc_ref, o_ref, do_ref, lse_ref, sink_ref,
             dq_ref, dk_ref, dv_ref, dsink_ref):
        kv, n = pl.program_id(0), pl.program_id(1)

        @pl.when(n == 0)
        def _():
            dk_ref[...] = jnp.zeros_like(dk_ref)
            dv_ref[...] = jnp.zeros_like(dv_ref)
            dsink_ref[...] = jnp.zeros_like(dsink_ref)

        qs, kcat, s = _swa_tile(q_ref, kp_ref, kc_ref, n, group, scale)
        sink = _swa_sink_col(sink_ref, kv, group)
        stack = lambda ref, w: jnp.concatenate([ref[:, g * w:(g + 1) * w] for g in range(group)], axis=0)
        lse = jnp.concatenate([lse_ref[0, :, g * LANES:g * LANES + 1] for g in range(group)], axis=0)
        do32 = stack(do_ref, HEAD_DIM)
        delta = jnp.sum(do32 * stack(o_ref, HEAD_DIM), axis=-1, keepdims=True)
        dov = do32.astype(BF16)
        p = jnp.exp(s - lse)
        vcat = jnp.concatenate([vp_ref[...], vc_ref[...]], axis=0)
        dp = lax.dot_general(dov, vcat, _NT, preferred_element_type=F32)
        ds = p * (dp - delta)
        dsb = ds.astype(BF16)
        dq = jnp.dot(dsb, kcat, preferred_element_type=F32) * scale
        for g in range(group):
            dq_ref[:, g * HEAD_DIM:(g + 1) * HEAD_DIM] = dq[g * B:(g + 1) * B, :].astype(BF16)
        dkcat = lax.dot_general(dsb, qs, _TN, preferred_element_type=F32) * scale
        dvcat = lax.dot_general(p.astype(BF16), dov, _TN, preferred_element_type=F32)
        prev0 = pl.multiple_of(jnp.maximum(n - 1, 0) * B, B)
        cur0 = pl.multiple_of(n * B, B)
        dk_ref[0, pl.ds(prev0, B), :] += dkcat[:B, :]
        dk_ref[0, pl.ds(cur0, B), :] += dkcat[B:, :]
        dv_ref[0, pl.ds(prev0, B), :] += dvcat[:B, :]
        dv_ref[0, pl.ds(cur0, B), :] += dvcat[B:, :]
        dsk = -jnp.exp(sink - lse) * delta
        lane = lax.broadcasted_iota(jnp.int32, (1, LANES), 1)
        row = jnp.zeros((1, LANES), F32)
        for g in range(group):
            row = row + jnp.where(lane == g, jnp.sum(dsk[g * B:(g + 1) * B, :]), 0.0)
        dsink_ref[0, 0:1, :] += row

    specs = _swa_specs(n_kv, group, 0, n_q, v_first)
    q_blk = pl.BlockSpec((B, group * HEAD_DIM), lambda kv, n: (n, kv))
    acc = pl.BlockSpec((1, S, HEAD_DIM), lambda kv, n: (kv, 0, 0))
    return pl.pallas_call(
        body, name="swa_bwd", grid=(n_kv, S // B),
        in_specs=specs + [q_blk, pl.BlockSpec((B, group * HEAD_DIM), lambda kv, n: (n, do_first + kv)),
                          pl.BlockSpec((1, B, group * LANES), lambda kv, n: (kv, n, 0)),
                          pl.BlockSpec(memory_space=pltpu.SMEM)],
        out_specs=[q_blk, acc, acc, pl.BlockSpec((1, 8, LANES), lambda kv, n: (kv, 0, 0))],
        out_shape=[jax.ShapeDtypeStruct((S, n_q * HEAD_DIM), BF16), jax.ShapeDtypeStruct((n_kv, S, HEAD_DIM), F32),
                   jax.ShapeDtypeStruct((n_kv, S, HEAD_DIM), F32), jax.ShapeDtypeStruct((n_kv, 8, LANES), F32)],
        compiler_params=pltpu.CompilerParams(dimension_semantics=("parallel", "arbitrary")),
    )(rq, rq, rq, proj, proj, o, do, lse_b, sinks)


def _adamw(w, g, m, v):
    m = ADAM_B1 * m + (1.0 - ADAM_B1) * g
    v = ADAM_B2 * v + (1.0 - ADAM_B2) * (g * g)
    m_hat = m / (1.0 - ADAM_B1 ** ADAM_STEP)
    v_hat = v / (1.0 - ADAM_B2 ** ADAM_STEP)
    delta = -ADAM_LR * (m_hat / (jnp.sqrt(v_hat) + ADAM_EPS) + ADAM_WD * w)
    return delta, m, v


def _mod_fwd(cond_in, w_mod, b_shard):
    R, D = cond_in.shape
    cols = w_mod.shape[1]
    tn = _fit(512, cols)

    def body(c_ref, w_ref, b_ref, o_ref):
        cv = c_ref[...]
        cond = (cv / (1.0 + jnp.exp(-cv))).astype(BF16)
        o_ref[...] = jnp.dot(cond, w_ref[...].astype(BF16), preferred_element_type=F32) + b_ref[...]

    return pl.pallas_call(
        body, name="mod_fwd", grid=(cols // tn,),
        in_specs=[pl.BlockSpec((R, D), lambda j: (0, 0)), pl.BlockSpec((D, tn), lambda j: (0, j)),
                  pl.BlockSpec((1, tn), lambda j: (0, j))],
        out_specs=pl.BlockSpec((R, tn), lambda j: (0, j)),
        out_shape=jax.ShapeDtypeStruct((R, cols), F32),
        compiler_params=pltpu.CompilerParams(dimension_semantics=("parallel",), vmem_limit_bytes=_vmem(3 * D * tn * 4)),
    )(cond_in, w_mod, b_shard)


def _mod_update(c_t, dmod, w, m, v):
    D, nb = c_t.shape
    cols = w.shape[1]
    tr = _fit(128, D)

    def body(c_ref, d_ref, w_ref, m_ref, v_ref, g_ref, dl_ref, nm_ref, nv_ref):
        cv = c_ref[...]
        cond = cv / (1.0 + jnp.exp(-cv))
        g = jnp.zeros((tr, cols), F32)
        for b in range(nb):
            g = g + cond[:, b:b + 1] * d_ref[b:b + 1, :]
        g_ref[...] = g
        dl_ref[...], nm_ref[...], nv_ref[...] = _adamw(w_ref[...], g, m_ref[...], v_ref[...])

    blk = pl.BlockSpec((tr, cols), lambda r: (r, 0))
    out = jax.ShapeDtypeStruct((D, cols), F32)
    return pl.pallas_call(
        body, name="mod_update", grid=(D // tr,),
        in_specs=[pl.BlockSpec((tr, nb), lambda r: (r, 0)), pl.BlockSpec((nb, cols), lambda r: (0, 0)), blk, blk, blk],
        out_specs=[blk] * 4, out_shape=[out] * 4,
        compiler_params=pltpu.CompilerParams(dimension_semantics=("parallel",), vmem_limit_bytes=_vmem(18 * tr * cols * 4)),
    )(c_t, dmod, w, m, v)


def _small_update(stacked, w, m, v):
    R, C = w.shape

    def body(s_ref, w_ref, m_ref, v_ref, g_ref, dl_ref, nm_ref, nv_ref):
        g = s_ref[0:R, :]
        for d in range(1, N_DEV):
            g = g + s_ref[d * R:(d + 1) * R, :]
        g_ref[...] = g
        dl_ref[...], nm_ref[...], nv_ref[...] = _adamw(w_ref[...], g, m_ref[...], v_ref[...])

    return pl.pallas_call(body, name="small_update", out_shape=[jax.ShapeDtypeStruct((R, C), F32)] * 4)(stacked, w, m, v)


def _place():
    return lax.axis_index("x"), lax.axis_index("y"), lax.axis_index("c")


def _allgather8(name, block):
    m_per, n = block.shape

    def body(x_ref, out_ref, token_ref, send_sems, recv_sems, local_sem):
        token_ref[...] = jnp.zeros_like(token_ref)
        x, y, c = _place()
        me, sibling = (x, y, c), (x, y, 1 - c)
        chips = [(1 - x, y), (x, 1 - y), (1 - x, 1 - y)]

        def rows(px, py, pc):
            return out_ref.at[pl.ds((4 * px + 2 * py + pc) * m_per, m_per), :]

        def copy(k, blk, to, src=None):
            return pltpu.make_async_remote_copy(
                src_ref=rows(*blk) if src is None else src, dst_ref=rows(*blk),
                send_sem=send_sems.at[k], recv_sem=recv_sems.at[k], device_id=to, device_id_type=MESH)

        mine = pltpu.make_async_copy(x_ref, rows(*me), local_sem)
        mine.start()
        first = [copy(0, me, sibling, src=x_ref)]
        first += [copy(1 + j, me, (*chip, c), src=x_ref) for j, chip in enumerate(chips)]
        for cp in first:
            cp.start()
        passed = [copy(4 + j, (*chip, c), sibling) for j, chip in enumerate(chips)]
        for j, chip in enumerate(chips):
            copy(1 + j, (*chip, c), me).wait_recv()
            passed[j].start()
        copy(0, sibling, me).wait_recv()
        for j, chip in enumerate(chips):
            copy(4 + j, (*chip, 1 - c), me).wait_recv()
        for cp in first + passed:
            cp.wait_send()
        mine.wait()

    vmem = pl.BlockSpec(memory_space=pltpu.VMEM)
    return pl.pallas_call(
        body, name=name,
        out_shape=[jax.ShapeDtypeStruct((N_DEV * m_per, n), block.dtype), jax.ShapeDtypeStruct((8, LANES), F32)],
        in_specs=[vmem], out_specs=[vmem, vmem],
        scratch_shapes=[pltpu.SemaphoreType.DMA((7,)), pltpu.SemaphoreType.DMA((7,)), pltpu.SemaphoreType.DMA],
    )(block)


_ANY = pl.BlockSpec(memory_space=pl.ANY)


def _half(ref, c, rows):
    return ref.at[pl.ds(c * (rows // 2), rows // 2), :]


_HBM = pl.BlockSpec(memory_space=pltpu.HBM)
_SEM = pl.BlockSpec(memory_space=pltpu.SEMAPHORE)
_EFFECT = pltpu.SideEffectType.DATAFLOW_SIDE_EFFECTING


def _ici_start(name, srcs, land_shapes, plan, per_source=3, after=None):
    ns, nl = len(srcs), len(land_shapes)
    n_copies = per_source * ns
    n_in = ns + nl + (after is not None)

    def body(*refs):
        src_refs, land_refs = refs[:ns], refs[ns:ns + nl]
        send_sems, recv_sems = refs[n_in], refs[n_in + 1]
        token = refs[-1]
        for n, (src, dst, peer, _) in enumerate(plan(src_refs, land_refs)):
            pltpu.make_async_remote_copy(src_ref=src, dst_ref=dst, send_sem=send_sems.at[n], recv_sem=recv_sems.at[n],
                                         device_id=peer, device_id_type=MESH).start()
        token[...] = jnp.zeros_like(token)

    lands = [lax.empty(s.shape, s.dtype) for s in land_shapes]
    out = pl.pallas_call(
        body, name=name,
        out_shape=(pltpu.SemaphoreType.DMA((n_copies,)), pltpu.SemaphoreType.DMA((n_copies,)),
                   *[pltpu.HBM(a.shape, a.dtype) for a in list(srcs) + lands], jax.ShapeDtypeStruct((8, LANES), F32)),
        in_specs=[_HBM] * (ns + nl) + [_ANY] * (after is not None),
        out_specs=(_SEM, _SEM, *[_HBM] * (ns + nl), pl.BlockSpec(memory_space=pltpu.VMEM)),
        input_output_aliases={n: 2 + n for n in range(ns + nl)},
        compiler_params=pltpu.CompilerParams(has_side_effects=_EFFECT),
    )(*[pltpu.with_memory_space_constraint(a, pltpu.HBM) for a in list(srcs) + lands],
      *([] if after is None else [after]))
    return out[0], out[1], list(out[2:2 + ns]), list(out[2 + ns:2 + ns + nl]), out[-1]


def _ici_wait(name, send_sems, recv_sems, srcs, lands, plan, after, arrivals=None, sends=None):
    ns, nl = len(srcs), len(lands)
    after = list(after) if isinstance(after, (list, tuple)) else [after]

    def body(*refs):
        src_refs, land_refs = refs[:ns], refs[ns:ns + nl]
        send_sems, recv_sems = refs[ns + nl], refs[ns + nl + 1]
        for n, (src, _, peer, mine) in enumerate(plan(src_refs, land_refs)):
            cp = pltpu.make_async_remote_copy(src_ref=src, dst_ref=mine, send_sem=send_sems.at[n],
                                              recv_sem=recv_sems.at[n], device_id=peer, device_id_type=MESH)
            if sends is None or n in sends:
                cp.wait_send()
            if arrivals is None or n in arrivals:
                cp.wait_recv()

    out = pl.pallas_call(
        body, name=name, out_shape=[pltpu.HBM(a.shape, a.dtype) for a in list(srcs) + list(lands)],
        in_specs=[_HBM] * (ns + nl) + [_SEM, _SEM] + [_ANY] * len(after), out_specs=[_HBM] * (ns + nl),
        input_output_aliases={n: n for n in range(ns + nl)},
        compiler_params=pltpu.CompilerParams(has_side_effects=_EFFECT),
    )(*srcs, *lands, send_sems, recv_sems, *after)
    return list(out[:ns]), list(out[ns:])


def _own_slab(name, chip, w, after):
    R, C = w.shape
    tr, tc = _tiles(R, C)
    tied = [] if after is None else [after]

    def body(chip_ref, w_ref, *rest):
        stack_ref, token_ref = rest[-2:]
        stack_ref[0] = w_ref[...].astype(BF16)
        token_ref[...] = jnp.zeros_like(token_ref)

    small = pl.BlockSpec((8, LANES), lambda r, q, chip_ref: (0, 0))
    grid_spec = pltpu.PrefetchScalarGridSpec(
        num_scalar_prefetch=1, grid=(R // tr, C // tc),
        in_specs=[pl.BlockSpec((tr, tc), lambda r, q, chip_ref: (r, q))] + [small] * len(tied),
        out_specs=[pl.BlockSpec((1, tr, tc), lambda r, q, chip_ref: (chip_ref[0], r, q)), small])
    return pl.pallas_call(
        body, name=name, grid_spec=grid_spec,
        out_shape=[jax.ShapeDtypeStruct((N_CHIPS, R, C), BF16), jax.ShapeDtypeStruct((8, LANES), F32)],
        compiler_params=pltpu.CompilerParams(dimension_semantics=("arbitrary", "arbitrary")),
    )(chip, w, *tied)


def _gather_plan(src_refs, land_refs):
    x, y, c = _place()
    copies = []
    for stack in src_refs:
        R = stack.shape[1]
        own = _half(stack.at[2 * x + y], c, R)
        for cx, cy in [(1 - x, y), (x, 1 - y), (1 - x, 1 - y)]:
            copies.append((own, own, (cx, cy, c), _half(stack.at[2 * cx + cy], c, R)))
    return copies


def _pass_plan(src_refs, land_refs):
    x, y, c = _place()
    copies = []
    for land in src_refs:
        R = land.shape[1]
        for cx, cy in [(1 - x, y), (x, 1 - y), (1 - x, 1 - y)]:
            slot = land.at[2 * cx + cy]
            copies.append((_half(slot, c, R), _half(slot, c, R), (x, y, 1 - c), _half(slot, 1 - c, R)))
    return copies


def _share_plan(src_refs, land_refs):
    x, y, c = _place()
    return [(h, land, (x, y, 1 - c), land) for h, land in zip(src_refs, land_refs)]


def _pass_to_sibling(name, lands, which=(0, 1, 2)):
    nw = len(lands)

    def body(*refs):
        ins, outs = refs[:nw], refs[nw:2 * nw]
        send_sems, recv_sems = refs[2 * nw:]
        x, y, c = _place()
        chips = [(j, chip) for j, chip in enumerate([(1 - x, y), (x, 1 - y), (1 - x, 1 - y)]) if j in which]
        copies = []
        for k in range(nw):
            R = ins[k].shape[1]
            for j, (cx, cy) in chips:
                cp = pltpu.make_async_remote_copy(
                    src_ref=_half(ins[k].at[2 * cx + cy], c, R), dst_ref=_half(outs[k].at[2 * cx + cy], c, R),
                    send_sem=send_sems.at[3 * k + j], recv_sem=recv_sems.at[3 * k + j],
                    device_id=(x, y, 1 - c), device_id_type=MESH)
                cp.start()
                copies.append(cp)
        for k in range(nw):
            R = ins[k].shape[1]
            for j, (cx, cy) in chips:
                pltpu.make_async_remote_copy(
                    src_ref=_half(ins[k].at[2 * cx + cy], c, R), dst_ref=_half(outs[k].at[2 * cx + cy], 1 - c, R),
                    send_sem=send_sems.at[3 * k + j], recv_sem=recv_sems.at[3 * k + j],
                    device_id=(x, y, 1 - c), device_id_type=MESH).wait_recv()
        for cp in copies:
            cp.wait_send()

    return pl.pallas_call(
        body, name=name, out_shape=[jax.ShapeDtypeStruct(a.shape, a.dtype) for a in lands],
        in_specs=[_ANY] * nw, out_specs=[_ANY] * nw, input_output_aliases={k: k for k in range(nw)},
        scratch_shapes=[pltpu.SemaphoreType.DMA((3 * nw,)), pltpu.SemaphoreType.DMA((3 * nw,))],
    )(*lands)


def _tie(vec, token):
    return vec + token[0:1, 0:1]


ROW_ALIGN = 16
TILE_ELEMS = 512 * 1024


def _tiles(rows, cols):
    fits = [t for t in range(ROW_ALIGN, min(rows, 256) + 1, ROW_ALIGN) if rows % t == 0]
    tr = fits[-1] if fits and fits[-1] >= 64 else rows
    tc = cols
    while tr * tc > TILE_ELEMS and tc % (2 * LANES) == 0:
        tc //= 2
    return tr, tc


def _scatter_plan(src_refs, land_refs):
    x, y, c = _place()
    copies = []
    for p, land in zip(src_refs, land_refs):
        for j, (cx, cy) in enumerate([(1 - x, y), (x, 1 - y), (1 - x, 1 - y)]):
            copies.append((p.at[2 * cx + cy], land.at[j], (cx, cy, c), land.at[j]))
    return copies


def _chip_add(name, chip, sums, recv):
    _, H, C = sums.shape
    tr, tc = _tiles(H, C)

    def body(chip_ref, p_ref, r_ref, o_ref):
        total = p_ref[0].astype(F32)
        for j in range(3):
            total = total + r_ref[j].astype(F32)
        o_ref[...] = total

    grid_spec = pltpu.PrefetchScalarGridSpec(
        num_scalar_prefetch=1, grid=(H // tr, C // tc),
        in_specs=[pl.BlockSpec((1, tr, tc), lambda r, q, chip_ref: (chip_ref[0], r, q)),
                  pl.BlockSpec((3, tr, tc), lambda r, q, chip_ref: (0, r, q))],
        out_specs=pl.BlockSpec((tr, tc), lambda r, q, chip_ref: (r, q)))
    return pl.pallas_call(
        body, name=name, grid_spec=grid_spec, out_shape=jax.ShapeDtypeStruct((H, C), F32),
        compiler_params=pltpu.CompilerParams(dimension_semantics=("parallel", "parallel")),
    )(chip, sums, recv)


def _pair_share(name, halves):
    nw = len(halves)

    def body(*refs):
        hs, outs = refs[:nw], refs[nw:2 * nw]
        send_sems, recv_sems = refs[2 * nw:]
        x, y, c = _place()
        copies = []
        for k in range(nw):
            cp = pltpu.make_async_remote_copy(
                src_ref=hs[k], dst_ref=outs[k], send_sem=send_sems.at[k], recv_sem=recv_sems.at[k],
                device_id=(x, y, 1 - c), device_id_type=MESH)
            cp.start()
            copies.append(cp)
        for cp in copies:
            cp.wait()

    return pl.pallas_call(
        body, name=name,
        out_shape=[jax.ShapeDtypeStruct(h.shape, h.dtype) for h in halves],
        in_specs=[_ANY] * nw, out_specs=[_ANY] * nw,
        scratch_shapes=[pltpu.SemaphoreType.DMA((nw,)), pltpu.SemaphoreType.DMA((nw,))],
    )(*halves)


def _adam_halves(name, core, w, g_own, g_other, m, v):
    R, C = w.shape
    H = R // 2
    tr, tc = _tiles(H, C)
    nr, nc = H // tr, C // tc

    def body(core_ref, w_ref, go_ref, gr_ref, m_ref, v_ref, g_ref, dl_ref, nm_ref, nv_ref):
        own = (pl.program_id(0) // nr) == core_ref[0]
        g = jnp.where(own, go_ref[...], gr_ref[...])
        g_ref[...] = g
        dl_ref[...], nm_ref[...], nv_ref[...] = _adamw(w_ref[...], g, m_ref[...], v_ref[...])

    blk = pl.BlockSpec((tr, tc), lambda r, q, core_ref: (r, q))

    def half_spec(is_own):
        def index(r, q, core_ref):
            mine = ((r // nr) == core_ref[0]) == is_own
            done = is_own == (core_ref[0] == 0)
            return (jnp.where(mine, r % nr, jnp.where(done, nr - 1, 0)), jnp.where(mine, q, jnp.where(done, nc - 1, 0)))
        return pl.BlockSpec((tr, tc), index)
    out = jax.ShapeDtypeStruct((R, C), F32)
    grid_spec = pltpu.PrefetchScalarGridSpec(
        num_scalar_prefetch=1, grid=(R // tr, nc), in_specs=[blk, half_spec(True), half_spec(False), blk, blk],
        out_specs=[blk] * 4)
    return pl.pallas_call(
        body, name=name, grid_spec=grid_spec, out_shape=[out] * 4,
        compiler_params=pltpu.CompilerParams(dimension_semantics=("parallel", "parallel"),
                                             vmem_limit_bytes=_vmem(20 * tr * tc * 4)),
    )(core, w, g_own, g_other, m, v)


def kernel(x, c, w_mod, b_mod, g_pre_mix, g_post_mix, w_in, b_forget, swa_sinks, w_out, g_pre_mlp, g_post_mlp, w_up, w_down, loss_target, m_w_mod, m_b_mod, m_g_pre_mix, m_g_post_mix, m_w_in, m_b_forget, m_swa_sinks, m_w_out, m_g_pre_mlp, m_g_post_mlp, m_w_up, m_w_down, v_w_mod, v_b_mod, v_g_pre_mix, v_g_post_mix, v_w_in, v_b_forget, v_swa_sinks, v_w_out, v_g_pre_mlp, v_g_post_mlp, v_w_up, v_w_down):
    S, D = x.shape[1], x.shape[2]
    n_heads = D // HEAD_DIM
    n_fox = n_heads // 2
    n_swa = n_heads - n_fox
    n_kv = max(1, n_swa // 4)
    fox_w, swa_w, kv_w = n_fox * HEAD_DIM, n_swa * HEAD_DIM, n_kv * HEAD_DIM
    main_w = 3 * fox_w + swa_w + 2 * kv_w
    in_w = main_w + n_fox
    mod_cols = w_mod.shape[2]

    ax, ay, ac = _place()
    chip = 2 * ax + ay
    dev = 2 * chip + ac
    chip_arr = jnp.reshape(chip, (1,)).astype(jnp.int32)
    core_arr = jnp.reshape(ac, (1,)).astype(jnp.int32)

    x2, tgt = x[0], loss_target[0]

    in_rows = in_w // N_CHIPS
    in_rows_pad = -(-in_rows // (2 * LANES)) * (2 * LANES)
    slab_w = N_CHIPS * in_rows_pad

    def rows_of(a):
        return jnp.pad(a[0].T, ((0, in_rows_pad - in_rows), (0, 0)))

    w_in_stack, token = _own_slab("own_slab_w_in", chip_arr, rows_of(w_in), None)

    c_all, _ = _allgather8("gather_c", _tie(c, token).reshape(8, D // 8))
    c_all = c_all.reshape(N_DEV, D)
    b_shard = lax.dynamic_slice_in_dim(b_mod, chip * mod_cols, mod_cols, axis=1)
    mod_shard = _mod_fwd(jnp.pad(c_all, ((0, 16 - N_DEV), (0, 0))), w_mod[0], b_shard)[:N_DEV]
    mod_all, token = _allgather8("gather_mod", mod_shard)
    mod_all = mod_all.reshape(N_CHIPS, 2, N_DEV, mod_cols)[:, 0]
    mod = lax.dynamic_index_in_dim(mod_all, dev, axis=1, keepdims=False).reshape(N_MOD, 1, D)
    sh_a, sc_a, gt_a, sh_m, sc_m, gt_m = [mod[n] for n in range(N_MOD)]

    def slab_cols(lo, hi):
        spans = []
        while lo < hi:
            s, r = divmod(lo, in_rows)
            n = min(hi - lo, in_rows - r)
            spans.append((s * in_rows_pad + r, s * in_rows_pad + r + n))
            lo += n
        return spans

    gate_lo = 3 * fox_w
    main_spans = slab_cols(0, gate_lo) + slab_cols(gate_lo + n_fox, in_w)
    (gate_first, gate_last), = slab_cols(gate_lo, gate_lo + n_fox)

    names = ["w_in", "w_out", "w_up", "w_down"]
    flights = {}
    for n, w in zip(names, [None, w_out[0], w_up[0], w_down[0]]):
        stack = w_in_stack if n == "w_in" else _own_slab("own_slab_" + n, chip_arr, w, token)[0]
        flights[n] = _ici_start("gather_start_" + n, [stack], [], _gather_plan, after=token)
        token = flights[n][4]
    sc_a = _tie(sc_a, token)

    def arrived(n, after):
        send, recv, stacks, _, _ = flights[n]
        stacks, _ = _ici_wait("gather_wait_" + n, send, recv, stacks, [], _gather_plan, after)
        return _ici_start("gather_pass_start_" + n, stacks, [], _pass_plan)

    def gathered(n, after, in_flight=None):
        if in_flight is None:
            send, recv, stacks, _, _ = flights[n]
            stacks, _ = _ici_wait("gather_wait_" + n, send, recv, stacks, [], _gather_plan, after)
            return _pass_to_sibling("gather_pass_" + n, stacks)[0]
        send, recv, stacks, _, _ = in_flight
        return _ici_wait("gather_pass_wait_" + n, send, recv, stacks, [], _pass_plan, after)[0][0]

    d_ff = N_CHIPS * w_up.shape[2]

    h = _pre_norm(x2, g_pre_mix, sc_a, sh_a)
    in_state = [rows_of(w_in)] + [rows_of(_tie(a, token)) for a in (m_w_in, v_w_in)]
    cos, sin_signed = _rope_tables(S)

    def pack(bm, gpm, gqm, gpl, gql, bf, sk):
        last = jnp.concatenate([bf, sk, jnp.zeros((1, D - n_fox - n_swa), F32)], axis=1)
        return jnp.concatenate([bm.reshape(N_MOD, D), gpm, gqm, gpl, gql, last, jnp.zeros((5, D), F32)], axis=0)

    small_state = [pack(b_mod, g_pre_mix, g_post_mix, g_pre_mlp, g_post_mlp, b_forget, swa_sinks),
                   pack(m_b_mod, m_g_pre_mix, m_g_post_mix, m_g_pre_mlp, m_g_post_mlp, m_b_forget, m_swa_sinks),
                   pack(v_b_mod, v_g_pre_mix, v_g_post_mix, v_g_pre_mlp, v_g_post_mlp, v_b_forget, v_swa_sinks)]
    ready = h[:8, :LANES].astype(F32) + cos[:8]
    tm_p, tn_p = _fit(MM_TM, S), in_rows_pad
    win0 = gate_first // LANES * LANES
    win_slab, win_off = divmod(win0, tn_p)
    assert win_off + 2 * LANES <= tn_p and gate_last - win0 <= 2 * LANES

    def proj_epilogue(acc, ex, outs, slab):
        outs[0][...] = acc.astype(BF16)

        @pl.when(slab == win_slab)
        def _():
            outs[1][...] = acc[:, win_off:win_off + 2 * LANES]

    proj_defs = [((S, slab_w), BF16, (tm_p, tn_p), lambda i, j: (i, j)),
                 ((S, 2 * LANES), F32, (tm_p, 2 * LANES), lambda i, j: (i, 0))]
    near = jnp.stack([chip, chip ^ 2, chip ^ 1]).astype(jnp.int32)
    far = jnp.reshape(chip ^ 3, (1,)).astype(jnp.int32)
    send, recv, stacks, _, _ = flights["w_in"]
    stacks, _ = _ici_wait("gather_wait_w_in_near", send, recv, stacks, [], _gather_plan,
                          [ready] + in_state[1:] + small_state, arrivals=(0, 1), sends=())
    stacks = _pass_to_sibling("gather_pass_w_in_near", stacks, which=(0, 1))
    proj_slab, gate_near = _matmul("in_proj_near", h, stacks[0].reshape(slab_w, D), "nt", proj_defs, proj_epilogue,
                                   tn=tn_p, revisits=True, col_sel=(near, lambda j, ref: ref[j], 3))
    stacks, _ = _ici_wait("gather_wait_w_in_far", send, recv, stacks, [], _gather_plan, proj_slab, arrivals=(2,))
    w_slab_t = _pass_to_sibling("gather_pass_w_in_far", stacks, which=(2,))[0].reshape(slab_w, D)
    proj_slab, gate_far = _matmul("in_proj_far", h, w_slab_t, "nt", proj_defs, proj_epilogue, tn=tn_p, revisits=True,
                                  col_sel=(far, lambda j, ref: ref[j], 1), carry=[proj_slab])
    gate_win = jnp.where((chip ^ 3) == win_slab, gate_far, gate_near)
    proj = jnp.concatenate([proj_slab[:, lo:hi] for lo, hi in main_spans], axis=1)
    out_flight = arrived("w_out", proj_slab)
    fg = _tie(jnp.pad(gate_win[:, gate_first - win0:gate_last - win0], ((0, 0), (0, LANES - n_fox))), out_flight[4])
    b_pad = jnp.pad(b_forget, ((0, 0), (0, LANES - n_fox)))
    cum_row = _fox_gate_fwd(fg, b_pad)[:n_fox].reshape(n_fox, 1, S)
    fox_o, fox_lse = _fox_fwd(proj, cum_row, n_fox)

    rq = _rope("rope_fwd", proj, 3 * n_fox, n_swa + n_kv, cos, sin_signed)
    v_first = 3 * n_fox + n_swa + n_kv
    sinks = swa_sinks[0]
    swa_o, swa_lse = _swa_fwd(rq, proj, v_first, sinks, n_swa, n_kv)

    mixcat = jnp.concatenate([fox_o, swa_o], axis=1).astype(BF16)
    up_flight = arrived("w_up", mixcat)
    w_out_f = gathered("w_out", mixcat, out_flight).reshape(D, D)
    mix = _mm_plain("out_proj", mixcat, w_out_f, "nn", BF16, after=up_flight[4])
    x1, h2 = _post_mix(x2, mix, g_post_mix, gt_a, g_pre_mlp, sc_m, sh_m)
    w_up_f = gathered("w_up", h2, up_flight)

    tm_u, tn_u = _fit(MM_TM, S), _fit(MM_TN, d_ff)

    def up_epilogue(acc, ex, outs):
        outs[0][...] = acc.astype(BF16)
        r = jnp.maximum(acc, 0.0)
        outs[1][...] = (r * r).astype(BF16)

    ublk = ((S, d_ff), BF16, (tm_u, tn_u), lambda i, j: (i, j))
    u, a = _matmul("mlp_up", h2, w_up_f, "nn", [ublk, ublk], up_epilogue)
    w_down_f = gathered("w_down", a).reshape(d_ff, D)
    y = _mm_plain("mlp_down", a, w_down_f, "nn", BF16)

    dy, dout, loss_part, acc_mlp_post = _loss_and_post_mlp_bwd(x1, y, tgt, g_post_mlp, gt_m)

    def du_epilogue(acc, ex, outs):
        outs[0][...] = (acc * (2.0 * jnp.maximum(ex[0][...].astype(F32), 0.0))).astype(BF16)

    du = _matmul("mlp_down_bwd", dy, w_down_f, "nt", [ublk], du_epilogue,
                 extras=[(u, (tm_u, tn_u), lambda i, j: (i, j))])[0]
    def pair_send(tag, part):
        return _ici_start("grad_pair_start_" + tag, [part], [jax.ShapeDtypeStruct(part.shape, BF16)], _share_plan,
                          per_source=1)

    def pair_recv(tag, flight, after):
        send, recv, srcs, lands, _ = flight
        return _ici_wait("grad_pair_wait_" + tag, send, recv, srcs, lands, _share_plan, after)[1][0]

    def scatter_start(tag, sums, after=None):
        return _ici_start("grad_scatter_start_" + tag, sums,
                          [jax.ShapeDtypeStruct((3,) + p.shape[1:], BF16) for p in sums], _scatter_plan, after=after)

    def scatter_finish(tag, flight, after):
        send, recv, srcs, lands, _ = flight
        sums, received = _ici_wait("grad_scatter_wait_" + tag, send, recv, srcs, lands, _scatter_plan, after)
        return [_chip_add("chip_add_%s_%d" % (tag, k), chip_arr, p, r) for k, (p, r) in enumerate(zip(sums, received))]

    tm_g = _fit(MM_TM, D // 2)
    pair_down = pair_send("down", _grad_half("grad_w_down_a", core_arr, a, dy, N_CHIPS, 1, tm_g, True))
    pair_up = pair_send("up", _grad_half("grad_w_up_a", core_arr, h2, du, 1, N_CHIPS, tm_g, True, after=pair_down[4]))
    sum_down = _grad_half("grad_w_down_b", core_arr, a, dy, N_CHIPS, 1, tm_g, False,
                          recv=pair_recv("down", pair_down, pair_up[4]))
    sum_up = _grad_half("grad_w_up_b", core_arr, h2, du, 1, N_CHIPS, tm_g, False, recv=pair_recv("up", pair_up, sum_down))
    flight_mlp = scatter_start("mlp", [sum_up, sum_down])
    dh2 = _mm_plain("mlp_up_bwd", du, w_up_f, "nt", BF16, after=flight_mlp[4])
    dx1, dmix, acc_mid = _pre_mlp_and_post_mix_bwd(dh2, x1, dout, mix, _tie(g_pre_mlp, flight_mlp[4]), sc_m,
                                                   g_post_mix, gt_a)

    dmixcat = _mm_plain("out_proj_bwd", dmix, w_out_f, "nt", F32)

    fdq, fdk, fdv, dcum_row, dcum_q = _fox_bwd(proj, fox_o, dmixcat, fox_lse, cum_row, n_fox)
    dcum_k = jnp.pad(dcum_row.reshape(n_fox, S), ((0, LANES - n_fox), (0, 0)))
    dfg, db_forget = _fox_gate_bwd(dcum_k, dcum_q, fg, b_pad)

    group_w = (n_swa // n_kv) * HEAD_DIM
    sdq, sdk, sdv, dsink = _swa_bwd(rq, proj, v_first, sinks, swa_o, dmixcat, fox_w // group_w, swa_lse, n_swa, n_kv)
    drq = jnp.concatenate([sdq, jnp.transpose(sdk, (1, 0, 2)).reshape(S, kv_w).astype(BF16)], axis=1)
    d_sq_sk = _rope("rope_bwd", drq, 0, n_swa + n_kv, cos, -sin_signed)
    dsv = jnp.transpose(sdv, (1, 0, 2)).reshape(S, kv_w).astype(BF16)
    dproj = jnp.concatenate([fdq, fdk, fdv, d_sq_sk, dsv], axis=1)

    pieces = []
    for s in range(N_CHIPS):
        lo, hi = s * in_rows, (s + 1) * in_rows
        for src, first, last, shift in [(dproj, 0, gate_lo, 0), (dfg, gate_lo, gate_lo + n_fox, gate_lo),
                                        (dproj, gate_lo + n_fox, in_w, n_fox)]:
            if max(lo, first) < min(hi, last):
                pieces.append(src[:, max(lo, first) - shift:min(hi, last) - shift])
        pieces.append(jnp.zeros((S, in_rows_pad - in_rows), BF16))
    dproj_slab = jnp.concatenate(pieces, axis=1)

    tm_in, tm_out = in_rows_pad // 2, D // (2 * N_CHIPS)
    pair_in = pair_send("in", _grad_half("grad_w_in_a", core_arr, dproj_slab, h, N_CHIPS, 1, tm_in, True))
    pair_out = pair_send("out", _grad_half("grad_w_out_a", core_arr, mixcat, dmix, N_CHIPS, 1, tm_out, True,
                                           after=pair_in[4]))
    sum_in = _grad_half("grad_w_in_b", core_arr, dproj_slab, h, N_CHIPS, 1, tm_in, False,
                        recv=pair_recv("in", pair_in, pair_out[4]))
    sum_out = _grad_half("grad_w_out_b", core_arr, mixcat, dmix, N_CHIPS, 1, tm_out, False,
                         recv=pair_recv("out", pair_out, sum_in[0, :8, :LANES]))
    dh = _mm_plain("in_proj_bwd", dproj_slab, w_slab_t, "nn", BF16, tk=slab_w // 2,
                   after=sum_out[0, :8, :LANES].astype(F32))
    grad_x, acc_pre = _pre_mix_bwd(dh, x2, dx1, g_pre_mix, sc_a)

    zero_row = jnp.zeros((1, D), F32)
    tail = jnp.concatenate([db_forget[0:1, :n_fox], dsink[:, 0, :n_swa // n_kv].reshape(1, n_swa),
                            loss_part[0:1, 0:1], jnp.zeros((1, D - n_fox - n_swa - 1), F32)], axis=1)
    partial = jnp.concatenate([
        acc_pre[0:1], acc_pre[1:2], acc_mid[3:4], acc_mid[0:1], acc_mid[1:2], acc_mlp_post[0:1],
        acc_pre[2:3], acc_mid[4:5], acc_mid[2:3], acc_mlp_post[1:2], tail] + [zero_row] * 5, axis=0)
    gathered_small, token = _allgather8("gather_small_grads", partial)

    flight_mix = scatter_start("mix", [sum_in, sum_out], after=token)
    halves_mlp = scatter_finish("mlp", flight_mlp, flight_mix[4])
    share_up, share_down = [
        _ici_start("grad_share_start_" + n, [hv], [jax.ShapeDtypeStruct(hv.shape, F32)], _share_plan, per_source=1)
        for n, hv in zip(["up", "down"], halves_mlp)]

    def shared(tag, flight, after):
        send, recv, own, lands, _ = flight
        own, other = _ici_wait("grad_share_wait_" + tag, send, recv, own, lands, _share_plan, after)
        return own[0], other[0]

    def unpack(p):
        return {"b_mod": p[0:N_MOD].reshape(1, N_MOD * D), "g_pre_mix": p[6:7], "g_post_mix": p[7:8],
                "g_pre_mlp": p[8:9], "g_post_mlp": p[9:10], "b_forget": p[10:11, :n_fox],
                "swa_sinks": p[10:11, n_fox:n_fox + n_swa]}

    small_out = _small_update(gathered_small, _tie(small_state[0], share_down[4] + share_up[4]), small_state[1],
                              small_state[2])
    g_small, d_small, m_small, v_small = [unpack(p) for p in small_out]
    loss = small_out[0][N_MOD + 4, n_fox + n_swa]

    dmod_all = gathered_small.reshape(N_DEV, 16, D)[:, :N_MOD].reshape(N_DEV, N_MOD * D)
    dmod_shard = _tie(lax.dynamic_slice_in_dim(dmod_all, chip * mod_cols, mod_cols, axis=1), share_down[4])
    g_w_mod, d_w_mod, nm_w_mod, nv_w_mod = _mod_update(c_all.T, dmod_shard, w_mod[0], m_w_mod[0], v_w_mod[0])

    grads = dict(g_small, w_mod=g_w_mod[None])
    deltas = dict(d_small, w_mod=d_w_mod[None])
    new_m = dict(m_small, w_mod=nm_w_mod[None])
    new_v = dict(v_small, w_mod=nv_w_mod[None])
    weights = {"w_in": (w_in, m_w_in, v_w_in), "w_out": (w_out, m_w_out, v_w_out), "w_up": (w_up, m_w_up, v_w_up),
               "w_down": (w_down, m_w_down, v_w_down)}

    def big_update(n, own, other):
        transposed = n == "w_in"
        w, m, v = in_state if transposed else [a[0] for a in weights[n]]
        outs = _adam_halves("adam_" + n, core_arr, w, own, other, m, v)
        if transposed:
            outs = [o[:in_rows].T for o in outs]
        grads[n], deltas[n], new_m[n], new_v[n] = [o[None] for o in outs]

    big_update("w_down", *shared("down", share_down, d_w_mod[:8, :LANES] + small_out[1][:8, :LANES]))
    halves_mix = scatter_finish("mix", flight_mix, deltas["w_down"][0, :8, :LANES] + d_w_mod[:8, :LANES])
    others_mix = _pair_share("grad_pair_share_mix", halves_mix)
    big_update("w_in", halves_mix[0], others_mix[0])
    big_update("w_out", halves_mix[1], others_mix[1])
    big_update("w_up", *shared("up", share_up, deltas["w_out"][0, :8, :LANES] + deltas["w_in"][0, :8, :LANES]))

    order = ["w_mod", "b_mod", "g_pre_mix", "g_post_mix", "w_in", "b_forget", "swa_sinks", "w_out", "g_pre_mlp",
             "g_post_mlp", "w_up", "w_down"]
    return (loss, grad_x[None], *[grads[n] for n in order], *[deltas[n] for n in order],
            *[new_m[n] for n in order], *[new_v[n] for n in order])
```

```python
import jax
import jax.numpy as jnp
from jax import lax
from jax.experimental import pallas as pl
from jax.experimental.pallas import tpu as pltpu

F32 = jnp.float32
BF16 = jnp.bfloat16
MESH = pl.DeviceIdType.MESH

HEAD_DIM = 128
SWA_BLOCK = 128
ROPE_THETA = 10000.0
NORM_EPS = 1e-6
NEG = -1e30
N_MOD = 6
ADAM_LR = 0.001
ADAM_B1 = 0.9
ADAM_B2 = 0.999
ADAM_EPS = 1e-08
ADAM_WD = 0.01
ADAM_STEP = 10
N_CHIPS = 4
N_DEV = 8
LANES = 128
VMEM_CAP = 60 * 1024 * 1024

_NN = (((1,), (0,)), ((), ()))
_NT = (((1,), (1,)), ((), ()))
_TN = (((0,), (0,)), ((), ()))


def _vmem(nbytes):
    return int(min(VMEM_CAP, nbytes * 5 // 4 + (4 << 20)))


def _nbytes(shape, dtype):
    n = 1
    for s in shape:
        n *= s
    return n * jnp.dtype(dtype).itemsize


def _fit(t, n):
    t = min(t, n)
    assert n % t == 0, (t, n)
    return t


MM_TM, MM_TN, MM_TK = 1024, 1024, 2048


def _matmul(name, a, b, mode, out_defs, epilogue, extras=(), tm=MM_TM, tn=MM_TN, tk=MM_TK, revisits=False,
            row_sel=None):
    stacked = b.ndim == 3
    b_rows, b_cols = b.shape[-2], b.shape[-1] * (b.shape[0] if stacked else 1)
    if mode == "nn":
        (M, K), (K2, N) = a.shape, (b_rows, b_cols)
    elif mode == "nt":
        (M, K), (N, K2) = a.shape, (b_rows, b_cols)
    else:
        (K, M), (K2, N) = a.shape, (b_rows, b_cols)
    assert K == K2 and not (stacked and mode == "tn"), (a.shape, b.shape, mode)
    tm = _fit(tm, M)
    tn = _fit(tn, b.shape[-1] if stacked and mode == "nn" else N)
    tk = _fit(tk, b.shape[-1] if stacked and mode == "nt" else K)
    nk = K // tk
    dims = {"nn": _NN, "nt": _NT, "tn": _TN}[mode]
    if row_sel is None:
        grid_m, a_row = M // tm, lambda i, *sel: i
    else:
        grid_m, a_row = row_sel[2], lambda i, *sel: row_sel[1](i, sel[0])
    a_spec = (pl.BlockSpec((tk, tm), lambda i, j, k, *sel: (k, a_row(i, *sel))) if mode == "tn"
              else pl.BlockSpec((tm, tk), lambda i, j, k, *sel: (a_row(i, *sel), k)))
    if stacked:
        per = b.shape[-1] // (tk if mode == "nt" else tn)
        b_spec = (pl.BlockSpec((1, tn, tk), lambda i, j, k, *sel: (k // per, j, k % per)) if mode == "nt"
                  else pl.BlockSpec((1, tk, tn), lambda i, j, k, *sel: (j // per, k, j % per)))
    else:
        b_spec = (pl.BlockSpec((tn, tk), lambda i, j, k, *sel: (j, k)) if mode == "nt"
                  else pl.BlockSpec((tk, tn), lambda i, j, k, *sel: (k, j)))
    n_ex, n_out = len(extras), len(out_defs)

    def body(*refs):
        if row_sel is not None:
            refs = refs[1:]
        a_ref, b_ref = refs[0], refs[1]
        ex = refs[2:2 + n_ex]
        outs = refs[2 + n_ex:2 + n_ex + n_out]
        b_blk = b_ref[0] if stacked else b_ref[...]
        prod = lax.dot_general(a_ref[...], b_blk, dims, preferred_element_type=F32)
        if nk == 1:
            epilogue(prod, ex, outs)
        else:
            acc_ref = refs[-1]
            k = pl.program_id(2)

            @pl.when(k == 0)
            def _():
                acc_ref[...] = prod

            @pl.when(k > 0)
            def _():
                acc_ref[...] += prod

            @pl.when(k == nk - 1)
            def _():
                epilogue(acc_ref[...], ex, outs)

    def wrap(f):
        return lambda i, j, k, *sel: f(i, j)

    in_specs = [a_spec, b_spec] + [pl.BlockSpec(blk, wrap(f)) for _, blk, f in extras]
    out_specs = [pl.BlockSpec(blk, wrap(f)) for _, _, blk, f in out_defs]
    out_shape = [jax.ShapeDtypeStruct(s, d) for s, d, _, _ in out_defs]
    need = 2 * (tm * tk + tk * tn) * a.dtype.itemsize + 3 * tm * tn * 4
    need += sum(2 * _nbytes(blk, arr.dtype) for arr, blk, _ in extras)
    need += sum(2 * _nbytes(blk, d) for _, d, blk, _ in out_defs)
    grid = (grid_m, N // tn, nk)
    scratch = [pltpu.VMEM((tm, tn), F32)] if nk > 1 else []
    params = pltpu.CompilerParams(
        dimension_semantics=("parallel", "arbitrary" if revisits else "parallel", "arbitrary"),
        vmem_limit_bytes=_vmem(need))
    operands = (a, b, *[arr for arr, _, _ in extras])
    if row_sel is None:
        return pl.pallas_call(body, name=name, grid=grid, in_specs=in_specs, out_specs=out_specs, out_shape=out_shape,
                              scratch_shapes=scratch, compiler_params=params)(*operands)
    grid_spec = pltpu.PrefetchScalarGridSpec(num_scalar_prefetch=1, grid=grid, in_specs=in_specs, out_specs=out_specs,
                                             scratch_shapes=scratch)
    return pl.pallas_call(body, name=name, grid_spec=grid_spec, out_shape=out_shape,
                          compiler_params=params)(row_sel[0], *operands)


def _grad_half(name, core, a, b, row_slabs, col_slabs, tm, other, recv=None, after=None):
    (_, M), (_, N) = a.shape, b.shape
    H = M // (2 * row_slabs)
    nh = H // tm
    tn = _fit(MM_TN, N // col_slabs)
    per = N // col_slabs // tn

    def a_block(i, core_ref):
        half = (1 - core_ref[0]) if other else core_ref[0]
        return (i // nh) * (2 * nh) + half * nh + i % nh

    def out_index(i, j):
        return (j // per, i, j % per) if col_slabs > 1 else (i // nh, i % nh, j)

    slabs = max(row_slabs, col_slabs)
    out_def = ((slabs, H, N // col_slabs), BF16, (1, tm, tn), out_index)

    def epilogue(acc, ex, outs):
        outs[0][0] = (acc if recv is None else acc + ex[0][0].astype(F32)).astype(BF16)

    extras = ([] if recv is None else [(recv, (1, tm, tn), out_index)]) + ([] if after is None else [_behind(after)])
    return _matmul(name, a, b, "tn", [out_def], epilogue, extras=extras, tm=tm, tn=tn,
                   row_sel=(core, a_block, row_slabs * nh))[0]


def _behind(token):
    return (token, (8, LANES), lambda i, j: (0, 0))


def _mm_plain(name, a, b, mode, out_dtype, after=None, **tiles):
    if mode == "nn":
        M, N = a.shape[0], b.shape[-1] * (b.shape[0] if b.ndim == 3 else 1)
    elif mode == "nt":
        M, N = a.shape[0], b.shape[-2]
    else:
        M, N = a.shape[1], b.shape[1]
    tm, tn = _fit(tiles.get("tm", MM_TM), M), _fit(tiles.get("tn", MM_TN), N)

    def epi(acc, ex, outs):
        outs[0][...] = acc.astype(out_dtype)

    return _matmul(name, a, b, mode, [((M, N), out_dtype, (tm, tn), lambda i, j: (i, j))], epi,
                   extras=[] if after is None else [_behind(after)], **tiles)[0]


def _rstd(v):
    return lax.rsqrt(jnp.mean(v * v, axis=-1, keepdims=True) + NORM_EPS)


ROW_TILE = 256


def _row_call(name, body, row_ins, vec_ins, row_outs, acc_outs, S, D):
    tr = _fit(ROW_TILE, S)
    row_spec = pl.BlockSpec((tr, D), lambda r: (r, 0))
    vec_spec = pl.BlockSpec((1, D), lambda r: (0, 0))
    in_specs = [row_spec] * len(row_ins) + [vec_spec] * len(vec_ins)
    out_specs = [row_spec] * len(row_outs) + [pl.BlockSpec(shp, lambda r: (0, 0)) for shp in acc_outs]
    out_shape = [jax.ShapeDtypeStruct((S, D), d) for d in row_outs] + [jax.ShapeDtypeStruct(shp, F32) for shp in acc_outs]
    need = sum(2 * tr * D * a.dtype.itemsize for a in row_ins) + sum(2 * tr * D * jnp.dtype(d).itemsize for d in row_outs)
    need += 8 * tr * D * 4
    return pl.pallas_call(
        body, name=name, grid=(S // tr,), in_specs=in_specs, out_specs=out_specs, out_shape=out_shape,
        compiler_params=pltpu.CompilerParams(dimension_semantics=("arbitrary",), vmem_limit_bytes=_vmem(need)),
    )(*row_ins, *vec_ins)


def _acc_rows(ref, rows):
    @pl.when(pl.program_id(0) == 0)
    def _():
        ref[...] = jnp.zeros_like(ref)
    for n, r in enumerate(rows):
        ref[n:n + 1, :] += r


def _pre_norm(x, g, sc, sh):
    S, D = x.shape

    def body(x_ref, g_ref, sc_ref, sh_ref, h_ref):
        xv = x_ref[...]
        xn = xv * _rstd(xv)
        h_ref[...] = (xn * g_ref[...] * (1.0 + sc_ref[...]) + sh_ref[...]).astype(BF16)

    return _row_call("pre_norm_mix", body, [x], [g, sc, sh], [BF16], [], S, D)[0]


def _post_mix(x, mix, g_post, gt, g_pre, sc, sh):
    S, D = x.shape

    def body(x_ref, mix_ref, gp_ref, gt_ref, g2_ref, sc_ref, sh_ref, x1_ref, h2_ref):
        mv = mix_ref[...].astype(F32)
        x1 = x_ref[...] + gt_ref[...] * (mv * _rstd(mv) * gp_ref[...])
        x1_ref[...] = x1
        h2_ref[...] = (x1 * _rstd(x1) * g2_ref[...] * (1.0 + sc_ref[...]) + sh_ref[...]).astype(BF16)

    return _row_call("post_mix_pre_mlp", body, [x, mix], [g_post, gt, g_pre, sc, sh], [F32, BF16], [], S, D)


def _loss_and_post_mlp_bwd(x1, y, target, g_post, gt):
    S, D = x1.shape

    def body(x1_ref, y_ref, t_ref, g_ref, gt_ref, dy_ref, dout_ref, loss_ref, acc_ref):
        yv = y_ref[...].astype(F32)
        r = _rstd(yv)
        yh = yv * r
        n = yh * g_ref[...]
        diff = x1_ref[...] + gt_ref[...] * n - t_ref[...]
        dout = diff * (1.0 / D)
        dout_ref[...] = dout
        dn = dout * gt_ref[...]
        dyh = dn * g_ref[...]
        dy_ref[...] = (r * (dyh - yh * jnp.mean(dyh * yh, axis=-1, keepdims=True))).astype(BF16)
        _acc_rows(acc_ref, [jnp.sum(dout * n, axis=0, keepdims=True), jnp.sum(dn * yh, axis=0, keepdims=True)])

        @pl.when(pl.program_id(0) == 0)
        def _():
            loss_ref[...] = jnp.zeros_like(loss_ref)
        loss_ref[...] += jnp.full(loss_ref.shape, (0.5 / D) * jnp.sum(diff * diff), F32)

    return _row_call("loss_post_mlp_bwd", body, [x1, y, target], [g_post, gt], [BF16, F32],
                     [(8, LANES), (8, D)], S, D)


def _pre_mlp_and_post_mix_bwd(dh2, x1, dout, mix, g_pre, sc, g_post, gt):
    S, D = x1.shape

    def body(dh_ref, x1_ref, dout_ref, mix_ref, g_ref, sc_ref, gp_ref, gt_ref, dx1_ref, dmix_ref, acc_ref):
        dh = dh_ref[...].astype(F32)
        x1v = x1_ref[...]
        r3 = _rstd(x1v)
        xn = x1v * r3
        dxn = dh * (1.0 + sc_ref[...]) * g_ref[...]
        dx1 = dout_ref[...] + r3 * (dxn - xn * jnp.mean(dxn * xn, axis=-1, keepdims=True))
        dx1_ref[...] = dx1
        mv = mix_ref[...].astype(F32)
        r2 = _rstd(mv)
        mh = mv * r2
        dn = dx1 * gt_ref[...]
        dmh = dn * gp_ref[...]
        dmix_ref[...] = (r2 * (dmh - mh * jnp.mean(dmh * mh, axis=-1, keepdims=True))).astype(BF16)
        _acc_rows(acc_ref, [
            jnp.sum(dh, axis=0, keepdims=True),
            jnp.sum(dh * xn * g_ref[...], axis=0, keepdims=True),
            jnp.sum(dh * (1.0 + sc_ref[...]) * xn, axis=0, keepdims=True),
            jnp.sum(dx1 * mh * gp_ref[...], axis=0, keepdims=True),
            jnp.sum(dn * mh, axis=0, keepdims=True)])

    return _row_call("pre_mlp_post_mix_bwd", body, [dh2, x1, dout, mix], [g_pre, sc, g_post, gt], [F32, BF16],
                     [(8, D)], S, D)


def _pre_mix_bwd(dh, x, dx1, g_pre, sc):
    S, D = x.shape

    def body(dh_ref, x_ref, dx1_ref, g_ref, sc_ref, gx_ref, acc_ref):
        dhv = dh_ref[...].astype(F32)
        xv = x_ref[...]
        r = _rstd(xv)
        xn = xv * r
        dxn = dhv * (1.0 + sc_ref[...]) * g_ref[...]
        gx_ref[...] = dx1_ref[...] + r * (dxn - xn * jnp.mean(dxn * xn, axis=-1, keepdims=True))
        _acc_rows(acc_ref, [
            jnp.sum(dhv, axis=0, keepdims=True),
            jnp.sum(dhv * xn * g_ref[...], axis=0, keepdims=True),
            jnp.sum(dhv * (1.0 + sc_ref[...]) * xn, axis=0, keepdims=True)])

    return _row_call("pre_mix_bwd", body, [dh, x, dx1], [g_pre, sc], [F32], [(8, D)], S, D)


CUM_BLOCK = 256


def _tri(n, upper):
    r = lax.broadcasted_iota(jnp.int32, (n, n), 0)
    c = lax.broadcasted_iota(jnp.int32, (n, n), 1)
    return ((c >= r) if upper else (c <= r)).astype(F32)


def _fox_gate_fwd(fg, b_pad):
    S = fg.shape[0]
    cb = _fit(CUM_BLOCK, S)

    def body(fg_ref, b_ref, cumt_ref, cum_ref):
        low = _tri(cb, False)
        carry = jnp.zeros((1, LANES), F32)
        for n in range(S // cb):
            z = fg_ref[n * cb:(n + 1) * cb, :] + b_ref[...]
            logf = jnp.minimum(z, 0.0) - jnp.log(1.0 + jnp.exp(-jnp.abs(z)))
            blk = jnp.dot(low, logf, precision=lax.Precision.HIGHEST, preferred_element_type=F32) + carry
            cum_ref[n * cb:(n + 1) * cb, :] = blk
            carry = blk[cb - 1:cb, :]
        cumt_ref[...] = cum_ref[...].T

    return pl.pallas_call(
        body, name="fox_gate_fwd", out_shape=jax.ShapeDtypeStruct((LANES, S), F32),
        scratch_shapes=[pltpu.VMEM((S, LANES), F32)],
        compiler_params=pltpu.CompilerParams(vmem_limit_bytes=_vmem(6 * S * LANES * 4)),
    )(fg, b_pad)


def _fox_gate_bwd(dcum_k, dcum_q, fg, b_pad):
    S = fg.shape[0]
    n_fox = dcum_q.shape[0]
    cb = _fit(CUM_BLOCK, S)

    def body(dk_ref, dq_ref, fg_ref, b_ref, dfg_ref, db_ref, dc_ref):
        lane = lax.broadcasted_iota(jnp.int32, (S, LANES), 1)
        dc = dk_ref[...].T
        for h in range(n_fox):
            dc = dc + jnp.where(lane == h, dq_ref[h], 0.0)
        dc_ref[...] = dc
        up = _tri(cb, True)
        carry = jnp.zeros((1, LANES), F32)
        db = jnp.zeros((1, LANES), F32)
        for n in reversed(range(S // cb)):
            blk = jnp.dot(up, dc_ref[n * cb:(n + 1) * cb, :], precision=lax.Precision.HIGHEST,
                          preferred_element_type=F32) + carry
            carry = blk[0:1, :]
            z = fg_ref[n * cb:(n + 1) * cb, :] + b_ref[...]
            dfg = blk * (1.0 / (1.0 + jnp.exp(z)))
            dfg_ref[n * cb:(n + 1) * cb, :] = dfg.astype(BF16)
            db = db + jnp.sum(dfg, axis=0, keepdims=True)
        db_ref[...] = jnp.broadcast_to(db, db_ref.shape)

    return pl.pallas_call(
        body, name="fox_gate_bwd",
        out_shape=[jax.ShapeDtypeStruct((S, LANES), BF16), jax.ShapeDtypeStruct((8, LANES), F32)],
        scratch_shapes=[pltpu.VMEM((S, LANES), F32)],
        compiler_params=pltpu.CompilerParams(vmem_limit_bytes=_vmem((8 + 2 * n_fox) * S * LANES * 4)),
    )(dcum_k, dcum_q, fg, b_pad)


FOX_TILE = 512


LOG2E = 1.4426950408889634


def _fox_scores(q, k, ck2, masked, t):
    s = lax.dot_general(q, k, _NT, preferred_element_type=F32) * (HEAD_DIM ** -0.5 * LOG2E) - ck2
    if masked:
        row = lax.broadcasted_iota(jnp.int32, (t, t), 0)
        col = lax.broadcasted_iota(jnp.int32, (t, t), 1)
        s = jnp.where(col <= row, s, NEG)
    return s


def _fox_fwd(proj, cum_row, n_fox):
    S = proj.shape[0]
    t = _fit(FOX_TILE, S)
    nq = S // t

    def body(q_ref, k_ref, v_ref, ck_ref, o_ref, lse_ref):
        def q_block(qi, _):
            q0 = pl.multiple_of(qi * t, t)
            q = q_ref[pl.ds(q0, t), :]

            def kv_block(j, carry, masked):
                m, l, acc = carry
                k0 = pl.multiple_of(j * t, t)
                s = _fox_scores(q, k_ref[pl.ds(k0, t), :], ck_ref[0, :, pl.ds(k0, t)] * LOG2E, masked, t)
                m_new = jnp.maximum(m, jnp.max(s, axis=-1, keepdims=True))
                alpha = jnp.exp2(m - m_new)
                p = jnp.exp2(s - m_new)
                l = alpha * l + jnp.sum(p, axis=-1, keepdims=True)
                acc = alpha * acc + jnp.dot(p.astype(BF16), v_ref[pl.ds(k0, t), :], preferred_element_type=F32)
                return m_new, l, acc

            init = (jnp.full((t, 1), NEG, F32), jnp.zeros((t, 1), F32), jnp.zeros((t, HEAD_DIM), F32))
            carry = lax.fori_loop(0, qi, lambda j, cr: kv_block(j, cr, False), init)
            m, l, acc = kv_block(qi, carry, True)
            o_ref[pl.ds(q0, t), :] = acc / l
            lse_ref[0, pl.ds(q0, t), :] = jnp.broadcast_to(m + jnp.log(l) * LOG2E, (t, LANES))
            return 0

        lax.fori_loop(0, nq, q_block, 0)

    col = lambda off: pl.BlockSpec((S, HEAD_DIM), lambda h: (0, off + h))
    per_head = pl.BlockSpec((1, S, LANES), lambda h: (h, 0, 0))
    return pl.pallas_call(
        body, name="fox_fwd", grid=(n_fox,),
        in_specs=[col(0), col(n_fox), col(2 * n_fox), pl.BlockSpec((1, 1, S), lambda h: (h, 0, 0))],
        out_specs=[pl.BlockSpec((S, HEAD_DIM), lambda h: (0, h)), per_head],
        out_shape=[jax.ShapeDtypeStruct((S, n_fox * HEAD_DIM), F32), jax.ShapeDtypeStruct((n_fox, S, LANES), F32)],
        compiler_params=pltpu.CompilerParams(dimension_semantics=("parallel",),
                                             vmem_limit_bytes=_vmem(16 * S * HEAD_DIM * 4 + 12 * t * t * 4)),
    )(proj, proj, proj, cum_row)


def _fox_bwd(proj, o, do, lse_b, cum_row, n_fox):
    S = proj.shape[0]
    t = _fit(FOX_TILE, S)
    nq = S // t
    scale = HEAD_DIM ** -0.5

    def body(q_ref, k_ref, v_ref, o_ref, do_ref, lse_ref, ck_ref, dq_ref, dk_ref, dv_ref, dc_ref, dcq_ref,
             dq_acc, delta_ref):
        dq_acc[...] = jnp.zeros_like(dq_acc)
        dcq_ref[...] = jnp.zeros_like(dcq_ref)

        def delta_block(qi, _):
            q0 = pl.multiple_of(qi * t, t)
            d = jnp.sum(do_ref[pl.ds(q0, t), :] * o_ref[pl.ds(q0, t), :], axis=-1, keepdims=True)
            delta_ref[pl.ds(q0, t), :] = jnp.broadcast_to(d, (t, LANES))
            return 0

        lax.fori_loop(0, nq, delta_block, 0)

        def kv_block(j, _):
            k0 = pl.multiple_of(j * t, t)
            k = k_ref[pl.ds(k0, t), :]
            v = v_ref[pl.ds(k0, t), :]
            ck2 = ck_ref[0, :, pl.ds(k0, t)] * LOG2E

            def q_block(qi, carry, masked):
                dk, dv, dc = carry
                q0 = pl.multiple_of(qi * t, t)
                q = q_ref[pl.ds(q0, t), :]
                dov = do_ref[pl.ds(q0, t), :].astype(BF16)
                p = jnp.exp2(_fox_scores(q, k, ck2, masked, t) - lse_ref[0, pl.ds(q0, t), :][:, :1])
                dp = lax.dot_general(dov, v, _NT, preferred_element_type=F32)
                ds = p * (dp - delta_ref[pl.ds(q0, t), :][:, :1])
                dsb = ds.astype(BF16)
                dv = dv + lax.dot_general(p.astype(BF16), dov, _TN, preferred_element_type=F32)
                dk = dk + lax.dot_general(dsb, q, _TN, preferred_element_type=F32)
                dq_acc[pl.ds(q0, t), :] += jnp.dot(dsb, k, preferred_element_type=F32)
                dc = dc - jnp.sum(ds, axis=0, keepdims=True)
                dcq_ref[0, pl.ds(q0, t), :] += jnp.broadcast_to(jnp.sum(ds, axis=1, keepdims=True), (t, LANES))
                return dk, dv, dc

            init = (jnp.zeros((t, HEAD_DIM), F32), jnp.zeros((t, HEAD_DIM), F32), jnp.zeros((1, t), F32))
            carry = q_block(j, init, True)
            dk, dv, dc = lax.fori_loop(j + 1, nq, lambda qi, cr: q_block(qi, cr, False), carry)
            dk_ref[pl.ds(k0, t), :] = (dk * scale).astype(BF16)
            dv_ref[pl.ds(k0, t), :] = dv.astype(BF16)
            dc_ref[0, :, pl.ds(k0, t)] = dc
            return 0

        lax.fori_loop(0, nq, kv_block, 0)
        dq_ref[...] = (dq_acc[...] * scale).astype(BF16)

    col = lambda off: pl.BlockSpec((S, HEAD_DIM), lambda h: (0, off + h))
    per_head = pl.BlockSpec((1, S, LANES), lambda h: (h, 0, 0))
    row = pl.BlockSpec((1, 1, S), lambda h: (h, 0, 0))
    grad = jax.ShapeDtypeStruct((S, n_fox * HEAD_DIM), BF16)
    return pl.pallas_call(
        body, name="fox_bwd", grid=(n_fox,),
        in_specs=[col(0), col(n_fox), col(2 * n_fox), col(0), col(0), per_head, row],
        out_specs=[col(0), col(0), col(0), row, per_head],
        out_shape=[grad, grad, grad, jax.ShapeDtypeStruct((n_fox, 1, S), F32), jax.ShapeDtypeStruct((n_fox, S, LANES), F32)],
        scratch_shapes=[pltpu.VMEM((S, HEAD_DIM), F32), pltpu.VMEM((S, LANES), F32)],
        compiler_params=pltpu.CompilerParams(dimension_semantics=("parallel",),
                                             vmem_limit_bytes=_vmem(24 * S * HEAD_DIM * 4 + 16 * t * t * 4)),
    )(proj, proj, proj, o, do, lse_b, cum_row)


def _rope_tables(S):
    half = HEAD_DIM // 2
    inv_freq = 1.0 / (ROPE_THETA ** (jnp.arange(half, dtype=F32) * (2.0 / HEAD_DIM)))
    ang = jnp.arange(S).astype(F32)[:, None] * inv_freq[None, :]
    cos, sin = jnp.cos(ang), jnp.sin(ang)
    return jnp.concatenate([cos, cos], axis=-1), jnp.concatenate([-sin, sin], axis=-1)


def _rope(name, src, first_block, n_blocks, cos, sin_signed):
    S = src.shape[0]

    def body(x_ref, cos_ref, sin_ref, o_ref):
        xv = x_ref[...].astype(F32)
        o_ref[...] = (xv * cos_ref[...] + pltpu.roll(xv, HEAD_DIM // 2, 1) * sin_ref[...]).astype(BF16)

    table = pl.BlockSpec((S, HEAD_DIM), lambda n: (0, 0))
    return pl.pallas_call(
        body, name=name, grid=(n_blocks,),
        in_specs=[pl.BlockSpec((S, HEAD_DIM), lambda n: (0, first_block + n)), table, table],
        out_specs=pl.BlockSpec((S, HEAD_DIM), lambda n: (0, n)),
        out_shape=jax.ShapeDtypeStruct((S, n_blocks * HEAD_DIM), BF16),
        compiler_params=pltpu.CompilerParams(dimension_semantics=("parallel",),
                                             vmem_limit_bytes=_vmem(12 * S * HEAD_DIM * 4)),
    )(src, cos, sin_signed)


def _swa_tile(q_ref, kp_ref, kc_ref, n, group, scale):
    B = SWA_BLOCK
    qs = jnp.concatenate([q_ref[:, g * HEAD_DIM:(g + 1) * HEAD_DIM] for g in range(group)], axis=0)
    kcat = jnp.concatenate([kp_ref[...], kc_ref[...]], axis=0)
    s = lax.dot_general(qs, kcat, _NT, preferred_element_type=F32) * scale
    qi = lax.broadcasted_iota(jnp.int32, (group * B, 2 * B), 0) % B
    kj = lax.broadcasted_iota(jnp.int32, (group * B, 2 * B), 1)
    diff = qi + B - kj
    mask = (diff >= 0) & (diff < B) & ((n * B + kj - B) >= 0)
    return qs, kcat, jnp.where(mask, s, NEG)


def _swa_sink_col(sink_ref, kv, group):
    head = lax.broadcasted_iota(jnp.int32, (group * SWA_BLOCK, 1), 0) // SWA_BLOCK
    col = jnp.zeros((group * SWA_BLOCK, 1), F32)
    for g in range(group):
        col = jnp.where(head == g, sink_ref[kv * group + g], col)
    return col


def _swa_specs(n_kv, group, q_first, k_first, v_first):
    B = SWA_BLOCK
    prev = lambda n: jnp.maximum(n - 1, 0)
    return [
        pl.BlockSpec((B, group * HEAD_DIM), lambda kv, n: (n, q_first + kv)),
        pl.BlockSpec((B, HEAD_DIM), lambda kv, n: (prev(n), k_first + kv)),
        pl.BlockSpec((B, HEAD_DIM), lambda kv, n: (n, k_first + kv)),
        pl.BlockSpec((B, HEAD_DIM), lambda kv, n: (prev(n), v_first + kv)),
        pl.BlockSpec((B, HEAD_DIM), lambda kv, n: (n, v_first + kv)),
    ]


def _swa_fwd(rq, proj, v_first, sinks, n_q, n_kv):
    S = rq.shape[0]
    B = SWA_BLOCK
    group = n_q // n_kv
    scale = HEAD_DIM ** -0.5

    def body(q_ref, kp_ref, kc_ref, vp_ref, vc_ref, sink_ref, o_ref, lse_ref):
        kv, n = pl.program_id(0), pl.program_id(1)
        _, _, s = _swa_tile(q_ref, kp_ref, kc_ref, n, group, scale)
        sink = _swa_sink_col(sink_ref, kv, group)
        m = jnp.maximum(jnp.max(s, axis=-1, keepdims=True), sink)
        p = jnp.exp(s - m)
        denom = jnp.sum(p, axis=-1, keepdims=True) + jnp.exp(sink - m)
        vcat = jnp.concatenate([vp_ref[...], vc_ref[...]], axis=0)
        o = jnp.dot((p / denom).astype(BF16), vcat, preferred_element_type=F32)
        lse = m + jnp.log(denom)
        for g in range(group):
            o_ref[:, g * HEAD_DIM:(g + 1) * HEAD_DIM] = o[g * B:(g + 1) * B, :]
            lse_ref[0, :, g * LANES:(g + 1) * LANES] = jnp.broadcast_to(lse[g * B:(g + 1) * B, :], (B, LANES))

    specs = _swa_specs(n_kv, group, 0, n_q, v_first)
    q_blk = pl.BlockSpec((B, group * HEAD_DIM), lambda kv, n: (n, kv))
    return pl.pallas_call(
        body, name="swa_fwd", grid=(n_kv, S // B),
        in_specs=specs + [pl.BlockSpec(memory_space=pltpu.SMEM)],
        out_specs=[q_blk, pl.BlockSpec((1, B, group * LANES), lambda kv, n: (kv, n, 0))],
        out_shape=[jax.ShapeDtypeStruct((S, n_q * HEAD_DIM), F32), jax.ShapeDtypeStruct((n_kv, S, group * LANES), F32)],
        compiler_params=pltpu.CompilerParams(dimension_semantics=("parallel", "arbitrary")),
    )(rq, rq, rq, proj, proj, sinks)


def _swa_bwd(rq, proj, v_first, sinks, o, do, do_first, lse_b, n_q, n_kv):
    S = rq.shape[0]
    B = SWA_BLOCK
    group = n_q // n_kv
    scale = HEAD_DIM ** -0.5

    def body(q_ref, kp_ref, kc_ref, vp_ref, vc_ref, o_ref, do_ref, lse_ref, sink_ref,
             dq_ref, dk_ref, dv_ref, dsink_ref):
        kv, n = pl.program_id(0), pl.program_id(1)

        @pl.when(n == 0)
        def _():
            dk_ref[...] = jnp.zeros_like(dk_ref)
            dv_ref[...] = jnp.zeros_like(dv_ref)
            dsink_ref[...] = jnp.zeros_like(dsink_ref)

        qs, kcat, s = _swa_tile(q_ref, kp_ref, kc_ref, n, group, scale)
        sink = _swa_sink_col(sink_ref, kv, group)
        stack = lambda ref, w: jnp.concatenate([ref[:, g * w:(g + 1) * w] for g in range(group)], axis=0)
        lse = jnp.concatenate([lse_ref[0, :, g * LANES:g * LANES + 1] for g in range(group)], axis=0)
        do32 = stack(do_ref, HEAD_DIM)
        delta = jnp.sum(do32 * stack(o_ref, HEAD_DIM), axis=-1, keepdims=True)
        dov = do32.astype(BF16)
        p = jnp.exp(s - lse)
        vcat = jnp.concatenate([vp_ref[...], vc_ref[...]], axis=0)
        dp = lax.dot_general(dov, vcat, _NT, preferred_element_type=F32)
        ds = p * (dp - delta)
        dsb = ds.astype(BF16)
        dq = jnp.dot(dsb, kcat, preferred_element_type=F32) * scale
        for g in range(group):
            dq_ref[:, g * HEAD_DIM:(g + 1) * HEAD_DIM] = dq[g * B:(g + 1) * B, :].astype(BF16)
        dkcat = lax.dot_general(dsb, qs, _TN, preferred_element_type=F32) * scale
        dvcat = lax.dot_general(p.astype(BF16), dov, _TN, preferred_element_type=F32)
        prev0 = pl.multiple_of(jnp.maximum(n - 1, 0) * B, B)
        cur0 = pl.multiple_of(n * B, B)
        dk_ref[0, pl.ds(prev0, B), :] += dkcat[:B, :]
        dk_ref[0, pl.ds(cur0, B), :] += dkcat[B:, :]
        dv_ref[0, pl.ds(prev0, B), :] += dvcat[:B, :]
        dv_ref[0, pl.ds(cur0, B), :] += dvcat[B:, :]
        dsk = -jnp.exp(sink - lse) * delta
        lane = lax.broadcasted_iota(jnp.int32, (1, LANES), 1)
        row = jnp.zeros((1, LANES), F32)
        for g in range(group):
            row = row + jnp.where(lane == g, jnp.sum(dsk[g * B:(g + 1) * B, :]), 0.0)
        dsink_ref[0, 0:1, :] += row

    specs = _swa_specs(n_kv, group, 0, n_q, v_first)
    q_blk = pl.BlockSpec((B, group * HEAD_DIM), lambda kv, n: (n, kv))
    acc = pl.BlockSpec((1, S, HEAD_DIM), lambda kv, n: (kv, 0, 0))
    return pl.pallas_call(
        body, name="swa_bwd", grid=(n_kv, S // B),
        in_specs=specs + [q_blk, pl.BlockSpec((B, group * HEAD_DIM), lambda kv, n: (n, do_first + kv)),
                          pl.BlockSpec((1, B, group * LANES), lambda kv, n: (kv, n, 0)),
                          pl.BlockSpec(memory_space=pltpu.SMEM)],
        out_specs=[q_blk, acc, acc, pl.BlockSpec((1, 8, LANES), lambda kv, n: (kv, 0, 0))],
        out_shape=[jax.ShapeDtypeStruct((S, n_q * HEAD_DIM), BF16), jax.ShapeDtypeStruct((n_kv, S, HEAD_DIM), F32),
                   jax.ShapeDtypeStruct((n_kv, S, HEAD_DIM), F32), jax.ShapeDtypeStruct((n_kv, 8, LANES), F32)],
        compiler_params=pltpu.CompilerParams(dimension_semantics=("parallel", "arbitrary")),
    )(rq, rq, rq, proj, proj, o, do, lse_b, sinks)


def _adamw(w, g, m, v):
    m = ADAM_B1 * m + (1.0 - ADAM_B1) * g
    v = ADAM_B2 * v + (1.0 - ADAM_B2) * (g * g)
    m_hat = m / (1.0 - ADAM_B1 ** ADAM_STEP)
    v_hat = v / (1.0 - ADAM_B2 ** ADAM_STEP)
    delta = -ADAM_LR * (m_hat / (jnp.sqrt(v_hat) + ADAM_EPS) + ADAM_WD * w)
    return delta, m, v


def _mod_fwd(cond_in, w_mod, b_shard):
    R, D = cond_in.shape
    cols = w_mod.shape[1]
    tn = _fit(512, cols)

    def body(c_ref, w_ref, b_ref, o_ref):
        cv = c_ref[...]
        cond = (cv / (1.0 + jnp.exp(-cv))).astype(BF16)
        o_ref[...] = jnp.dot(cond, w_ref[...].astype(BF16), preferred_element_type=F32) + b_ref[...]

    return pl.pallas_call(
        body, name="mod_fwd", grid=(cols // tn,),
        in_specs=[pl.BlockSpec((R, D), lambda j: (0, 0)), pl.BlockSpec((D, tn), lambda j: (0, j)),
                  pl.BlockSpec((1, tn), lambda j: (0, j))],
        out_specs=pl.BlockSpec((R, tn), lambda j: (0, j)),
        out_shape=jax.ShapeDtypeStruct((R, cols), F32),
        compiler_params=pltpu.CompilerParams(dimension_semantics=("parallel",), vmem_limit_bytes=_vmem(3 * D * tn * 4)),
    )(cond_in, w_mod, b_shard)


def _mod_update(c_t, dmod, w, m, v):
    D, nb = c_t.shape
    cols = w.shape[1]
    tr = _fit(128, D)

    def body(c_ref, d_ref, w_ref, m_ref, v_ref, g_ref, dl_ref, nm_ref, nv_ref):
        cv = c_ref[...]
        cond = cv / (1.0 + jnp.exp(-cv))
        g = jnp.zeros((tr, cols), F32)
        for b in range(nb):
            g = g + cond[:, b:b + 1] * d_ref[b:b + 1, :]
        g_ref[...] = g
        dl_ref[...], nm_ref[...], nv_ref[...] = _adamw(w_ref[...], g, m_ref[...], v_ref[...])

    blk = pl.BlockSpec((tr, cols), lambda r: (r, 0))
    out = jax.ShapeDtypeStruct((D, cols), F32)
    return pl.pallas_call(
        body, name="mod_update", grid=(D // tr,),
        in_specs=[pl.BlockSpec((tr, nb), lambda r: (r, 0)), pl.BlockSpec((nb, cols), lambda r: (0, 0)), blk, blk, blk],
        out_specs=[blk] * 4, out_shape=[out] * 4,
        compiler_params=pltpu.CompilerParams(dimension_semantics=("parallel",), vmem_limit_bytes=_vmem(18 * tr * cols * 4)),
    )(c_t, dmod, w, m, v)


def _small_update(stacked, w, m, v):
    R, C = w.shape

    def body(s_ref, w_ref, m_ref, v_ref, g_ref, dl_ref, nm_ref, nv_ref):
        g = s_ref[0:R, :]
        for d in range(1, N_DEV):
            g = g + s_ref[d * R:(d + 1) * R, :]
        g_ref[...] = g
        dl_ref[...], nm_ref[...], nv_ref[...] = _adamw(w_ref[...], g, m_ref[...], v_ref[...])

    return pl.pallas_call(body, name="small_update", out_shape=[jax.ShapeDtypeStruct((R, C), F32)] * 4)(stacked, w, m, v)


def _place():
    return lax.axis_index("x"), lax.axis_index("y"), lax.axis_index("c")


def _allgather8(name, block):
    m_per, n = block.shape

    def body(x_ref, out_ref, token_ref, send_sems, recv_sems, local_sem):
        token_ref[...] = jnp.zeros_like(token_ref)
        x, y, c = _place()
        me, sibling = (x, y, c), (x, y, 1 - c)
        chips = [(1 - x, y), (x, 1 - y), (1 - x, 1 - y)]

        def rows(px, py, pc):
            return out_ref.at[pl.ds((4 * px + 2 * py + pc) * m_per, m_per), :]

        def copy(k, blk, to, src=None):
            return pltpu.make_async_remote_copy(
                src_ref=rows(*blk) if src is None else src, dst_ref=rows(*blk),
                send_sem=send_sems.at[k], recv_sem=recv_sems.at[k], device_id=to, device_id_type=MESH)

        mine = pltpu.make_async_copy(x_ref, rows(*me), local_sem)
        mine.start()
        first = [copy(0, me, sibling, src=x_ref)]
        first += [copy(1 + j, me, (*chip, c), src=x_ref) for j, chip in enumerate(chips)]
        for cp in first:
            cp.start()
        passed = [copy(4 + j, (*chip, c), sibling) for j, chip in enumerate(chips)]
        for j, chip in enumerate(chips):
            copy(1 + j, (*chip, c), me).wait_recv()
            passed[j].start()
        copy(0, sibling, me).wait_recv()
        for j, chip in enumerate(chips):
            copy(4 + j, (*chip, 1 - c), me).wait_recv()
        for cp in first + passed:
            cp.wait_send()
        mine.wait()

    vmem = pl.BlockSpec(memory_space=pltpu.VMEM)
    return pl.pallas_call(
        body, name=name,
        out_shape=[jax.ShapeDtypeStruct((N_DEV * m_per, n), block.dtype), jax.ShapeDtypeStruct((8, LANES), F32)],
        in_specs=[vmem], out_specs=[vmem, vmem],
        scratch_shapes=[pltpu.SemaphoreType.DMA((7,)), pltpu.SemaphoreType.DMA((7,)), pltpu.SemaphoreType.DMA],
    )(block)


_ANY = pl.BlockSpec(memory_space=pl.ANY)


def _half(ref, c, rows):
    return ref.at[pl.ds(c * (rows // 2), rows // 2), :]


_HBM = pl.BlockSpec(memory_space=pltpu.HBM)
_SEM = pl.BlockSpec(memory_space=pltpu.SEMAPHORE)
_EFFECT = pltpu.SideEffectType.DATAFLOW_SIDE_EFFECTING


def _ici_start(name, srcs, land_shapes, plan, per_source=3, after=None):
    ns, nl = len(srcs), len(land_shapes)
    n_copies = per_source * ns
    n_in = ns + nl + (after is not None)

    def body(*refs):
        src_refs, land_refs = refs[:ns], refs[ns:ns + nl]
        send_sems, recv_sems = refs[n_in], refs[n_in + 1]
        token = refs[-1]
        for n, (src, dst, peer, _) in enumerate(plan(src_refs, land_refs)):
            pltpu.make_async_remote_copy(src_ref=src, dst_ref=dst, send_sem=send_sems.at[n], recv_sem=recv_sems.at[n],
                                         device_id=peer, device_id_type=MESH).start()
        token[...] = jnp.zeros_like(token)

    lands = [lax.empty(s.shape, s.dtype) for s in land_shapes]
    out = pl.pallas_call(
        body, name=name,
        out_shape=(pltpu.SemaphoreType.DMA((n_copies,)), pltpu.SemaphoreType.DMA((n_copies,)),
                   *[pltpu.HBM(a.shape, a.dtype) for a in list(srcs) + lands], jax.ShapeDtypeStruct((8, LANES), F32)),
        in_specs=[_HBM] * (ns + nl) + [_ANY] * (after is not None),
        out_specs=(_SEM, _SEM, *[_HBM] * (ns + nl), pl.BlockSpec(memory_space=pltpu.VMEM)),
        input_output_aliases={n: 2 + n for n in range(ns + nl)},
        compiler_params=pltpu.CompilerParams(has_side_effects=_EFFECT),
    )(*[pltpu.with_memory_space_constraint(a, pltpu.HBM) for a in list(srcs) + lands],
      *([] if after is None else [after]))
    return out[0], out[1], list(out[2:2 + ns]), list(out[2 + ns:2 + ns + nl]), out[-1]


def _ici_wait(name, send_sems, recv_sems, srcs, lands, plan, after):
    ns, nl = len(srcs), len(lands)
    after = list(after) if isinstance(after, (list, tuple)) else [after]

    def body(*refs):
        src_refs, land_refs = refs[:ns], refs[ns:ns + nl]
        send_sems, recv_sems = refs[ns + nl], refs[ns + nl + 1]
        for n, (src, _, peer, mine) in enumerate(plan(src_refs, land_refs)):
            cp = pltpu.make_async_remote_copy(src_ref=src, dst_ref=mine, send_sem=send_sems.at[n],
                                              recv_sem=recv_sems.at[n], device_id=peer, device_id_type=MESH)
            cp.wait_send()
            cp.wait_recv()

    out = pl.pallas_call(
        body, name=name, out_shape=[pltpu.HBM(a.shape, a.dtype) for a in list(srcs) + list(lands)],
        in_specs=[_HBM] * (ns + nl) + [_SEM, _SEM] + [_ANY] * len(after), out_specs=[_HBM] * (ns + nl),
        input_output_aliases={n: n for n in range(ns + nl)},
        compiler_params=pltpu.CompilerParams(has_side_effects=_EFFECT),
    )(*srcs, *lands, send_sems, recv_sems, *after)
    return list(out[:ns]), list(out[ns:])


def _own_slab(name, chip, w, after):
    R, C = w.shape
    tr, tc = _tiles(R, C)
    tied = [] if after is None else [after]

    def body(chip_ref, w_ref, *rest):
        stack_ref, token_ref = rest[-2:]
        stack_ref[0] = w_ref[...].astype(BF16)
        token_ref[...] = jnp.zeros_like(token_ref)

    small = pl.BlockSpec((8, LANES), lambda r, q, chip_ref: (0, 0))
    grid_spec = pltpu.PrefetchScalarGridSpec(
        num_scalar_prefetch=1, grid=(R // tr, C // tc),
        in_specs=[pl.BlockSpec((tr, tc), lambda r, q, chip_ref: (r, q))] + [small] * len(tied),
        out_specs=[pl.BlockSpec((1, tr, tc), lambda r, q, chip_ref: (chip_ref[0], r, q)), small])
    return pl.pallas_call(
        body, name=name, grid_spec=grid_spec,
        out_shape=[jax.ShapeDtypeStruct((N_CHIPS, R, C), BF16), jax.ShapeDtypeStruct((8, LANES), F32)],
        compiler_params=pltpu.CompilerParams(dimension_semantics=("arbitrary", "arbitrary")),
    )(chip, w, *tied)


def _gather_plan(src_refs, land_refs):
    x, y, c = _place()
    copies = []
    for stack in src_refs:
        R = stack.shape[1]
        own = _half(stack.at[2 * x + y], c, R)
        for cx, cy in [(1 - x, y), (x, 1 - y), (1 - x, 1 - y)]:
            copies.append((own, own, (cx, cy, c), _half(stack.at[2 * cx + cy], c, R)))
    return copies


def _pass_plan(src_refs, land_refs):
    x, y, c = _place()
    copies = []
    for land in src_refs:
        R = land.shape[1]
        for cx, cy in [(1 - x, y), (x, 1 - y), (1 - x, 1 - y)]:
            slot = land.at[2 * cx + cy]
            copies.append((_half(slot, c, R), _half(slot, c, R), (x, y, 1 - c), _half(slot, 1 - c, R)))
    return copies


def _share_plan(src_refs, land_refs):
    x, y, c = _place()
    return [(h, land, (x, y, 1 - c), land) for h, land in zip(src_refs, land_refs)]


def _pass_to_sibling(name, lands):
    nw = len(lands)

    def body(*refs):
        ins, outs = refs[:nw], refs[nw:2 * nw]
        send_sems, recv_sems = refs[2 * nw:]
        x, y, c = _place()
        chips = [(1 - x, y), (x, 1 - y), (1 - x, 1 - y)]
        copies = []
        for k in range(nw):
            R = ins[k].shape[1]
            for j, (cx, cy) in enumerate(chips):
                cp = pltpu.make_async_remote_copy(
                    src_ref=_half(ins[k].at[2 * cx + cy], c, R), dst_ref=_half(outs[k].at[2 * cx + cy], c, R),
                    send_sem=send_sems.at[3 * k + j], recv_sem=recv_sems.at[3 * k + j],
                    device_id=(x, y, 1 - c), device_id_type=MESH)
                cp.start()
                copies.append(cp)
        for k in range(nw):
            R = ins[k].shape[1]
            for j, (cx, cy) in enumerate(chips):
                pltpu.make_async_remote_copy(
                    src_ref=_half(ins[k].at[2 * cx + cy], c, R), dst_ref=_half(outs[k].at[2 * cx + cy], 1 - c, R),
                    send_sem=send_sems.at[3 * k + j], recv_sem=recv_sems.at[3 * k + j],
                    device_id=(x, y, 1 - c), device_id_type=MESH).wait_recv()
        for cp in copies:
            cp.wait_send()

    return pl.pallas_call(
        body, name=name, out_shape=[jax.ShapeDtypeStruct(a.shape, a.dtype) for a in lands],
        in_specs=[_ANY] * nw, out_specs=[_ANY] * nw, input_output_aliases={k: k for k in range(nw)},
        scratch_shapes=[pltpu.SemaphoreType.DMA((3 * nw,)), pltpu.SemaphoreType.DMA((3 * nw,))],
    )(*lands)


def _tie(vec, token):
    return vec + token[0:1, 0:1]


ROW_ALIGN = 16
TILE_ELEMS = 512 * 1024


def _tiles(rows, cols):
    fits = [t for t in range(ROW_ALIGN, min(rows, 256) + 1, ROW_ALIGN) if rows % t == 0]
    tr = fits[-1] if fits and fits[-1] >= 64 else rows
    tc = cols
    while tr * tc > TILE_ELEMS and tc % (2 * LANES) == 0:
        tc //= 2
    return tr, tc


def _scatter_plan(src_refs, land_refs):
    x, y, c = _place()
    copies = []
    for p, land in zip(src_refs, land_refs):
        for j, (cx, cy) in enumerate([(1 - x, y), (x, 1 - y), (1 - x, 1 - y)]):
            copies.append((p.at[2 * cx + cy], land.at[j], (cx, cy, c), land.at[j]))
    return copies


def _chip_add(name, chip, sums, recv):
    _, H, C = sums.shape
    tr, tc = _tiles(H, C)

    def body(chip_ref, p_ref, r_ref, o_ref):
        total = p_ref[0].astype(F32)
        for j in range(3):
            total = total + r_ref[j].astype(F32)
        o_ref[...] = total

    grid_spec = pltpu.PrefetchScalarGridSpec(
        num_scalar_prefetch=1, grid=(H // tr, C // tc),
        in_specs=[pl.BlockSpec((1, tr, tc), lambda r, q, chip_ref: (chip_ref[0], r, q)),
                  pl.BlockSpec((3, tr, tc), lambda r, q, chip_ref: (0, r, q))],
        out_specs=pl.BlockSpec((tr, tc), lambda r, q, chip_ref: (r, q)))
    return pl.pallas_call(
        body, name=name, grid_spec=grid_spec, out_shape=jax.ShapeDtypeStruct((H, C), F32),
        compiler_params=pltpu.CompilerParams(dimension_semantics=("parallel", "parallel")),
    )(chip, sums, recv)


def _pair_share(name, halves):
    nw = len(halves)

    def body(*refs):
        hs, outs = refs[:nw], refs[nw:2 * nw]
        send_sems, recv_sems = refs[2 * nw:]
        x, y, c = _place()
        copies = []
        for k in range(nw):
            cp = pltpu.make_async_remote_copy(
                src_ref=hs[k], dst_ref=outs[k], send_sem=send_sems.at[k], recv_sem=recv_sems.at[k],
                device_id=(x, y, 1 - c), device_id_type=MESH)
            cp.start()
            copies.append(cp)
        for cp in copies:
            cp.wait()

    return pl.pallas_call(
        body, name=name,
        out_shape=[jax.ShapeDtypeStruct(h.shape, h.dtype) for h in halves],
        in_specs=[_ANY] * nw, out_specs=[_ANY] * nw,
        scratch_shapes=[pltpu.SemaphoreType.DMA((nw,)), pltpu.SemaphoreType.DMA((nw,))],
    )(*halves)


def _adam_halves(name, core, w, g_own, g_other, m, v, out_rows=None):
    R, C = w.shape
    H = R // 2
    tr, tc = _tiles(H, C)
    nr, nc = H // tr, C // tc

    def body(core_ref, w_ref, go_ref, gr_ref, m_ref, v_ref, g_ref, dl_ref, nm_ref, nv_ref):
        own = (pl.program_id(0) // nr) == core_ref[0]
        g = jnp.where(own, go_ref[...], gr_ref[...])
        g_ref[...] = g
        dl_ref[...], nm_ref[...], nv_ref[...] = _adamw(w_ref[...], g, m_ref[...], v_ref[...])

    blk = pl.BlockSpec((tr, tc), lambda r, q, core_ref: (r, q))

    def half_spec(is_own):
        def index(r, q, core_ref):
            mine = ((r // nr) == core_ref[0]) == is_own
            done = is_own == (core_ref[0] == 0)
            return (jnp.where(mine, r % nr, jnp.where(done, nr - 1, 0)), jnp.where(mine, q, jnp.where(done, nc - 1, 0)))
        return pl.BlockSpec((tr, tc), index)
    out_rows = R if out_rows is None else out_rows
    assert R - tr < out_rows <= R, (R, tr, out_rows)
    out = jax.ShapeDtypeStruct((out_rows, C), F32)
    grid_spec = pltpu.PrefetchScalarGridSpec(
        num_scalar_prefetch=1, grid=(R // tr, nc), in_specs=[blk, half_spec(True), half_spec(False), blk, blk],
        out_specs=[blk] * 4)
    return pl.pallas_call(
        body, name=name, grid_spec=grid_spec, out_shape=[out] * 4,
        compiler_params=pltpu.CompilerParams(dimension_semantics=("parallel", "parallel"),
                                             vmem_limit_bytes=_vmem(20 * tr * tc * 4)),
    )(core, w, g_own, g_other, m, v)


def kernel(x, c, w_mod, b_mod, g_pre_mix, g_post_mix, w_in, b_forget, swa_sinks, w_out, g_pre_mlp, g_post_mlp, w_up, w_down, loss_target, m_w_mod, m_b_mod, m_g_pre_mix, m_g_post_mix, m_w_in, m_b_forget, m_swa_sinks, m_w_out, m_g_pre_mlp, m_g_post_mlp, m_w_up, m_w_down, v_w_mod, v_b_mod, v_g_pre_mix, v_g_post_mix, v_w_in, v_b_forget, v_swa_sinks, v_w_out, v_g_pre_mlp, v_g_post_mlp, v_w_up, v_w_down):
    S, D = x.shape[1], x.shape[2]
    n_heads = D // HEAD_DIM
    n_fox = n_heads // 2
    n_swa = n_heads - n_fox
    n_kv = max(1, n_swa // 4)
    fox_w, swa_w, kv_w = n_fox * HEAD_DIM, n_swa * HEAD_DIM, n_kv * HEAD_DIM
    main_w = 3 * fox_w + swa_w + 2 * kv_w
    in_w = main_w + n_fox
    mod_cols = w_mod.shape[2]

    ax, ay, ac = _place()
    chip = 2 * ax + ay
    dev = 2 * chip + ac
    chip_arr = jnp.reshape(chip, (1,)).astype(jnp.int32)
    core_arr = jnp.reshape(ac, (1,)).astype(jnp.int32)

    x2, tgt = x[0], loss_target[0]

    in_rows = in_w // N_CHIPS
    in_rows_pad = -(-in_rows // (2 * LANES)) * (2 * LANES)
    slab_w = N_CHIPS * in_rows_pad

    def rows_of(a):
        return jnp.pad(a[0].T, ((0, in_rows_pad - in_rows), (0, 0)))

    w_in_stack, token = _own_slab("own_slab_w_in", chip_arr, rows_of(w_in), None)

    c_all, _ = _allgather8("gather_c", _tie(c, token).reshape(8, D // 8))
    c_all = c_all.reshape(N_DEV, D)
    b_shard = lax.dynamic_slice_in_dim(b_mod, chip * mod_cols, mod_cols, axis=1)
    mod_shard = _mod_fwd(jnp.pad(c_all, ((0, 16 - N_DEV), (0, 0))), w_mod[0], b_shard)[:N_DEV]
    mod_all, token = _allgather8("gather_mod", mod_shard)
    mod_all = mod_all.reshape(N_CHIPS, 2, N_DEV, mod_cols)[:, 0]
    mod = lax.dynamic_index_in_dim(mod_all, dev, axis=1, keepdims=False).reshape(N_MOD, 1, D)
    sh_a, sc_a, gt_a, sh_m, sc_m, gt_m = [mod[n] for n in range(N_MOD)]

    def slab_cols(lo, hi):
        spans = []
        while lo < hi:
            s, r = divmod(lo, in_rows)
            n = min(hi - lo, in_rows - r)
            spans.append((s * in_rows_pad + r, s * in_rows_pad + r + n))
            lo += n
        return spans

    gate_lo = 3 * fox_w
    main_spans = slab_cols(0, gate_lo) + slab_cols(gate_lo + n_fox, in_w)
    (gate_first, gate_last), = slab_cols(gate_lo, gate_lo + n_fox)

    names = ["w_in", "w_out", "w_up", "w_down"]
    flights = {}
    for n, w in zip(names, [None, w_out[0], w_up[0], w_down[0]]):
        stack = w_in_stack if n == "w_in" else _own_slab("own_slab_" + n, chip_arr, w, token)[0]
        flights[n] = _ici_start("gather_start_" + n, [stack], [], _gather_plan, after=token)
        token = flights[n][4]
    sc_a = _tie(sc_a, token)

    def arrived(n, after):
        send, recv, stacks, _, _ = flights[n]
        stacks, _ = _ici_wait("gather_wait_" + n, send, recv, stacks, [], _gather_plan, after)
        return _ici_start("gather_pass_start_" + n, stacks, [], _pass_plan)

    def gathered(n, after, in_flight=None):
        if in_flight is None:
            send, recv, stacks, _, _ = flights[n]
            stacks, _ = _ici_wait("gather_wait_" + n, send, recv, stacks, [], _gather_plan, after)
            return _pass_to_sibling("gather_pass_" + n, stacks)[0]
        send, recv, stacks, _, _ = in_flight
        return _ici_wait("gather_pass_wait_" + n, send, recv, stacks, [], _pass_plan, after)[0][0]

    d_ff = N_CHIPS * w_up.shape[2]

    h = _pre_norm(x2, g_pre_mix, sc_a, sh_a)
    in_state = [rows_of(w_in)] + [rows_of(_tie(a, token)) for a in (m_w_in, v_w_in)]
    cos, sin_signed = _rope_tables(S)

    def pack(bm, gpm, gqm, gpl, gql, bf, sk):
        last = jnp.concatenate([bf, sk, jnp.zeros((1, D - n_fox - n_swa), F32)], axis=1)
        return jnp.concatenate([bm.reshape(N_MOD, D), gpm, gqm, gpl, gql, last, jnp.zeros((5, D), F32)], axis=0)

    small_state = [pack(b_mod, g_pre_mix, g_post_mix, g_pre_mlp, g_post_mlp, b_forget, swa_sinks),
                   pack(m_b_mod, m_g_pre_mix, m_g_post_mix, m_g_pre_mlp, m_g_post_mlp, m_b_forget, m_swa_sinks),
                   pack(v_b_mod, v_g_pre_mix, v_g_post_mix, v_g_pre_mlp, v_g_post_mlp, v_b_forget, v_swa_sinks)]
    ready = h[:8, :LANES].astype(F32) + cos[:8]
    w_slab_t = gathered("w_in", [ready] + in_state[1:] + small_state).reshape(slab_w, D)
    tm_p, tn_p = _fit(MM_TM, S), _fit(MM_TN if slab_w % MM_TN == 0 else MM_TN // 2, slab_w)
    win0 = gate_first // LANES * LANES
    win_j, win_off = divmod(win0, tn_p)
    assert win_off + 2 * LANES <= tn_p and gate_last - win0 <= 2 * LANES

    def proj_epilogue(acc, ex, outs):
        outs[0][...] = acc.astype(BF16)

        @pl.when(pl.program_id(1) == win_j)
        def _():
            outs[1][...] = acc[:, win_off:win_off + 2 * LANES]

    proj_slab, gate_win = _matmul(
        "in_proj", h, w_slab_t, "nt",
        [((S, slab_w), BF16, (tm_p, tn_p), lambda i, j: (i, j)), ((S, 2 * LANES), F32, (tm_p, 2 * LANES), lambda i, j: (i, 0))],
        proj_epilogue, tn=tn_p, revisits=True)
    proj = jnp.concatenate([proj_slab[:, lo:hi] for lo, hi in main_spans], axis=1)
    out_flight = arrived("w_out", proj_slab)
    fg = _tie(jnp.pad(gate_win[:, gate_first - win0:gate_last - win0], ((0, 0), (0, LANES - n_fox))), out_flight[4])
    b_pad = jnp.pad(b_forget, ((0, 0), (0, LANES - n_fox)))
    cum_row = _fox_gate_fwd(fg, b_pad)[:n_fox].reshape(n_fox, 1, S)
    fox_o, fox_lse = _fox_fwd(proj, cum_row, n_fox)

    rq = _rope("rope_fwd", proj, 3 * n_fox, n_swa + n_kv, cos, sin_signed)
    v_first = 3 * n_fox + n_swa + n_kv
    sinks = swa_sinks[0]
    swa_o, swa_lse = _swa_fwd(rq, proj, v_first, sinks, n_swa, n_kv)

    mixcat = jnp.concatenate([fox_o, swa_o], axis=1).astype(BF16)
    up_flight = arrived("w_up", mixcat)
    w_out_f = gathered("w_out", mixcat, out_flight).reshape(D, D)
    mix = _mm_plain("out_proj", mixcat, w_out_f, "nn", BF16, after=up_flight[4])
    x1, h2 = _post_mix(x2, mix, g_post_mix, gt_a, g_pre_mlp, sc_m, sh_m)
    w_up_f = gathered("w_up", h2, up_flight)

    tm_u, tn_u = _fit(MM_TM, S), _fit(MM_TN, d_ff)

    def up_epilogue(acc, ex, outs):
        outs[0][...] = acc.astype(BF16)
        r = jnp.maximum(acc, 0.0)
        outs[1][...] = (r * r).astype(BF16)

    ublk = ((S, d_ff), BF16, (tm_u, tn_u), lambda i, j: (i, j))
    u, a = _matmul("mlp_up", h2, w_up_f, "nn", [ublk, ublk], up_epilogue)
    w_down_f = gathered("w_down", a).reshape(d_ff, D)
    y = _mm_plain("mlp_down", a, w_down_f, "nn", BF16)

    dy, dout, loss_part, acc_mlp_post = _loss_and_post_mlp_bwd(x1, y, tgt, g_post_mlp, gt_m)

    def du_epilogue(acc, ex, outs):
        outs[0][...] = (acc * (2.0 * jnp.maximum(ex[0][...].astype(F32), 0.0))).astype(BF16)

    du = _matmul("mlp_down_bwd", dy, w_down_f, "nt", [ublk], du_epilogue,
                 extras=[(u, (tm_u, tn_u), lambda i, j: (i, j))])[0]
    def pair_send(tag, part):
        return _ici_start("grad_pair_start_" + tag, [part], [jax.ShapeDtypeStruct(part.shape, BF16)], _share_plan,
                          per_source=1)

    def pair_recv(tag, flight, after):
        send, recv, srcs, lands, _ = flight
        return _ici_wait("grad_pair_wait_" + tag, send, recv, srcs, lands, _share_plan, after)[1][0]

    def scatter_start(tag, sums, after=None):
        return _ici_start("grad_scatter_start_" + tag, sums,
                          [jax.ShapeDtypeStruct((3,) + p.shape[1:], BF16) for p in sums], _scatter_plan, after=after)

    def scatter_finish(tag, flight, after):
        send, recv, srcs, lands, _ = flight
        sums, received = _ici_wait("grad_scatter_wait_" + tag, send, recv, srcs, lands, _scatter_plan, after)
        return [_chip_add("chip_add_%s_%d" % (tag, k), chip_arr, p, r) for k, (p, r) in enumerate(zip(sums, received))]

    tm_g = _fit(MM_TM, D // 2)
    pair_down = pair_send("down", _grad_half("grad_w_down_a", core_arr, a, dy, N_CHIPS, 1, tm_g, True))
    pair_up = pair_send("up", _grad_half("grad_w_up_a", core_arr, h2, du, 1, N_CHIPS, tm_g, True, after=pair_down[4]))
    sum_down = _grad_half("grad_w_down_b", core_arr, a, dy, N_CHIPS, 1, tm_g, False,
                          recv=pair_recv("down", pair_down, pair_up[4]))
    sum_up = _grad_half("grad_w_up_b", core_arr, h2, du, 1, N_CHIPS, tm_g, False, recv=pair_recv("up", pair_up, sum_down))
    flight_mlp = scatter_start("mlp", [sum_up, sum_down])
    dh2 = _mm_plain("mlp_up_bwd", du, w_up_f, "nt", BF16, after=flight_mlp[4])
    dx1, dmix, acc_mid = _pre_mlp_and_post_mix_bwd(dh2, x1, dout, mix, _tie(g_pre_mlp, flight_mlp[4]), sc_m,
                                                   g_post_mix, gt_a)

    dmixcat = _mm_plain("out_proj_bwd", dmix, w_out_f, "nt", F32)

    fdq, fdk, fdv, dcum_row, dcum_q = _fox_bwd(proj, fox_o, dmixcat, fox_lse, cum_row, n_fox)
    dcum_k = jnp.pad(dcum_row.reshape(n_fox, S), ((0, LANES - n_fox), (0, 0)))
    dfg, db_forget = _fox_gate_bwd(dcum_k, dcum_q, fg, b_pad)

    group_w = (n_swa // n_kv) * HEAD_DIM
    sdq, sdk, sdv, dsink = _swa_bwd(rq, proj, v_first, sinks, swa_o, dmixcat, fox_w // group_w, swa_lse, n_swa, n_kv)
    drq = jnp.concatenate([sdq, jnp.transpose(sdk, (1, 0, 2)).reshape(S, kv_w).astype(BF16)], axis=1)
    d_sq_sk = _rope("rope_bwd", drq, 0, n_swa + n_kv, cos, -sin_signed)
    dsv = jnp.transpose(sdv, (1, 0, 2)).reshape(S, kv_w).astype(BF16)
    dproj = jnp.concatenate([fdq, fdk, fdv, d_sq_sk, dsv], axis=1)

    pieces = []
    for s in range(N_CHIPS):
        lo, hi = s * in_rows, (s + 1) * in_rows
        for src, first, last, shift in [(dproj, 0, gate_lo, 0), (dfg, gate_lo, gate_lo + n_fox, gate_lo),
                                        (dproj, gate_lo + n_fox, in_w, n_fox)]:
            if max(lo, first) < min(hi, last):
                pieces.append(src[:, max(lo, first) - shift:min(hi, last) - shift])
        pieces.append(jnp.zeros((S, in_rows_pad - in_rows), BF16))
    dproj_slab = jnp.concatenate(pieces, axis=1)

    tm_in, tm_out = in_rows_pad // 2, D // (2 * N_CHIPS)
    pair_in = pair_send("in", _grad_half("grad_w_in_a", core_arr, dproj_slab, h, N_CHIPS, 1, tm_in, True))
    pair_out = pair_send("out", _grad_half("grad_w_out_a", core_arr, mixcat, dmix, N_CHIPS, 1, tm_out, True,
                                           after=pair_in[4]))
    sum_in = _grad_half("grad_w_in_b", core_arr, dproj_slab, h, N_CHIPS, 1, tm_in, False,
                        recv=pair_recv("in", pair_in, pair_out[4]))
    sum_out = _grad_half("grad_w_out_b", core_arr, mixcat, dmix, N_CHIPS, 1, tm_out, False,
                         recv=pair_recv("out", pair_out, sum_in[0, :8, :LANES]))
    dh = _mm_plain("in_proj_bwd", dproj_slab, w_slab_t, "nn", BF16, tk=slab_w // 2,
                   after=sum_out[0, :8, :LANES].astype(F32))
    grad_x, acc_pre = _pre_mix_bwd(dh, x2, dx1, g_pre_mix, sc_a)

    zero_row = jnp.zeros((1, D), F32)
    tail = jnp.concatenate([db_forget[0:1, :n_fox], dsink[:, 0, :n_swa // n_kv].reshape(1, n_swa),
                            loss_part[0:1, 0:1], jnp.zeros((1, D - n_fox - n_swa - 1), F32)], axis=1)
    partial = jnp.concatenate([
        acc_pre[0:1], acc_pre[1:2], acc_mid[3:4], acc_mid[0:1], acc_mid[1:2], acc_mlp_post[0:1],
        acc_pre[2:3], acc_mid[4:5], acc_mid[2:3], acc_mlp_post[1:2], tail] + [zero_row] * 5, axis=0)
    gathered_small, token = _allgather8("gather_small_grads", partial)

    flight_mix = scatter_start("mix", [sum_in, sum_out], after=token)
    halves_mlp = scatter_finish("mlp", flight_mlp, flight_mix[4])
    share_up, share_down = [
        _ici_start("grad_share_start_" + n, [hv], [jax.ShapeDtypeStruct(hv.shape, F32)], _share_plan, per_source=1)
        for n, hv in zip(["up", "down"], halves_mlp)]

    def shared(tag, flight, after):
        send, recv, own, lands, _ = flight
        own, other = _ici_wait("grad_share_wait_" + tag, send, recv, own, lands, _share_plan, after)
        return own[0], other[0]

    def unpack(p):
        return {"b_mod": p[0:N_MOD].reshape(1, N_MOD * D), "g_pre_mix": p[6:7], "g_post_mix": p[7:8],
                "g_pre_mlp": p[8:9], "g_post_mlp": p[9:10], "b_forget": p[10:11, :n_fox],
                "swa_sinks": p[10:11, n_fox:n_fox + n_swa]}

    small_out = _small_update(gathered_small, _tie(small_state[0], share_down[4] + share_up[4]), small_state[1],
                              small_state[2])
    g_small, d_small, m_small, v_small = [unpack(p) for p in small_out]
    loss = small_out[0][N_MOD + 4, n_fox + n_swa]

    dmod_all = gathered_small.reshape(N_DEV, 16, D)[:, :N_MOD].reshape(N_DEV, N_MOD * D)
    dmod_shard = _tie(lax.dynamic_slice_in_dim(dmod_all, chip * mod_cols, mod_cols, axis=1), share_down[4])
    g_w_mod, d_w_mod, nm_w_mod, nv_w_mod = _mod_update(c_all.T, dmod_shard, w_mod[0], m_w_mod[0], v_w_mod[0])

    grads = dict(g_small, w_mod=g_w_mod[None])
    deltas = dict(d_small, w_mod=d_w_mod[None])
    new_m = dict(m_small, w_mod=nm_w_mod[None])
    new_v = dict(v_small, w_mod=nv_w_mod[None])
    weights = {"w_in": (w_in, m_w_in, v_w_in), "w_out": (w_out, m_w_out, v_w_out), "w_up": (w_up, m_w_up, v_w_up),
               "w_down": (w_down, m_w_down, v_w_down)}

    def big_update(n, own, other):
        transposed = n == "w_in"
        w, m, v = in_state if transposed else [a[0] for a in weights[n]]
        outs = _adam_halves("adam_" + n, core_arr, w, own, other, m, v, out_rows=in_rows if transposed else None)
        if transposed:
            outs = [o.T for o in outs]
        grads[n], deltas[n], new_m[n], new_v[n] = [o[None] for o in outs]

    big_update("w_down", *shared("down", share_down, d_w_mod[:8, :LANES] + small_out[1][:8, :LANES]))
    halves_mix = scatter_finish("mix", flight_mix, deltas["w_down"][0, :8, :LANES] + d_w_mod[:8, :LANES])
    others_mix = _pair_share("grad_pair_share_mix", halves_mix)
    big_update("w_in", halves_mix[0], others_mix[0])
    big_update("w_out", halves_mix[1], others_mix[1])
    big_update("w_up", *shared("up", share_up, deltas["w_out"][0, :8, :LANES] + deltas["w_in"][0, :8, :LANES]))

    order = ["w_mod", "b_mod", "g_pre_mix", "g_post_mix", "w_in", "b_forget", "swa_sinks", "w_out", "g_pre_mlp",
             "g_post_mlp", "w_up", "w_down"]
    return (loss, grad_x[None], *[grads[n] for n in order], *[deltas[n] for n in order],
            *[new_m[n] for n in order], *[new_v[n] for n in order])
```

```python
import jax
import jax.numpy as jnp
from jax import lax
from jax.experimental import pallas as pl
from jax.experimental.pallas import tpu as pltpu

F32 = jnp.float32
BF16 = jnp.bfloat16
MESH = pl.DeviceIdType.MESH

HEAD_DIM = 128
SWA_BLOCK = 128
ROPE_THETA = 10000.0
NORM_EPS = 1e-6
NEG = -1e30
N_MOD = 6
ADAM_LR = 0.001
ADAM_B1 = 0.9
ADAM_B2 = 0.999
ADAM_EPS = 1e-08
ADAM_WD = 0.01
ADAM_STEP = 10
N_CHIPS = 4
N_DEV = 8
LANES = 128
VMEM_CAP = 60 * 1024 * 1024

_NN = (((1,), (0,)), ((), ()))
_NT = (((1,), (1,)), ((), ()))
_TN = (((0,), (0,)), ((), ()))


def _vmem(nbytes):
    return int(min(VMEM_CAP, nbytes * 5 // 4 + (4 << 20)))


def _nbytes(shape, dtype):
    n = 1
    for s in shape:
        n *= s
    return n * jnp.dtype(dtype).itemsize


def _fit(t, n):
    t = min(t, n)
    assert n % t == 0, (t, n)
    return t


MM_TM, MM_TN, MM_TK = 1024, 1024, 2048


def _matmul(name, a, b, mode, out_defs, epilogue, extras=(), tm=MM_TM, tn=MM_TN, tk=MM_TK, revisits=False,
            row_sel=None):
    stacked = b.ndim == 3
    b_rows, b_cols = b.shape[-2], b.shape[-1] * (b.shape[0] if stacked else 1)
    if mode == "nn":
        (M, K), (K2, N) = a.shape, (b_rows, b_cols)
    elif mode == "nt":
        (M, K), (N, K2) = a.shape, (b_rows, b_cols)
    else:
        (K, M), (K2, N) = a.shape, (b_rows, b_cols)
    assert K == K2 and not (stacked and mode == "tn"), (a.shape, b.shape, mode)
    tm = _fit(tm, M)
    tn = _fit(tn, b.shape[-1] if stacked and mode == "nn" else N)
    tk = _fit(tk, b.shape[-1] if stacked and mode == "nt" else K)
    nk = K // tk
    dims = {"nn": _NN, "nt": _NT, "tn": _TN}[mode]
    if row_sel is None:
        grid_m, a_row = M // tm, lambda i, *sel: i
    else:
        grid_m, a_row = row_sel[2], lambda i, *sel: row_sel[1](i, sel[0])
    a_spec = (pl.BlockSpec((tk, tm), lambda i, j, k, *sel: (k, a_row(i, *sel))) if mode == "tn"
              else pl.BlockSpec((tm, tk), lambda i, j, k, *sel: (a_row(i, *sel), k)))
    if stacked:
        per = b.shape[-1] // (tk if mode == "nt" else tn)
        b_spec = (pl.BlockSpec((1, tn, tk), lambda i, j, k, *sel: (k // per, j, k % per)) if mode == "nt"
                  else pl.BlockSpec((1, tk, tn), lambda i, j, k, *sel: (j // per, k, j % per)))
    else:
        b_spec = (pl.BlockSpec((tn, tk), lambda i, j, k, *sel: (j, k)) if mode == "nt"
                  else pl.BlockSpec((tk, tn), lambda i, j, k, *sel: (k, j)))
    n_ex, n_out = len(extras), len(out_defs)

    def body(*refs):
        if row_sel is not None:
            refs = refs[1:]
        a_ref, b_ref = refs[0], refs[1]
        ex = refs[2:2 + n_ex]
        outs = refs[2 + n_ex:2 + n_ex + n_out]
        b_blk = b_ref[0] if stacked else b_ref[...]
        prod = lax.dot_general(a_ref[...], b_blk, dims, preferred_element_type=F32)
        if nk == 1:
            epilogue(prod, ex, outs)
        else:
            acc_ref = refs[-1]
            k = pl.program_id(2)

            @pl.when(k == 0)
            def _():
                acc_ref[...] = prod

            @pl.when(k > 0)
            def _():
                acc_ref[...] += prod

            @pl.when(k == nk - 1)
            def _():
                epilogue(acc_ref[...], ex, outs)

    def wrap(f):
        return lambda i, j, k, *sel: f(i, j)

    in_specs = [a_spec, b_spec] + [pl.BlockSpec(blk, wrap(f)) for _, blk, f in extras]
    out_specs = [pl.BlockSpec(blk, wrap(f)) for _, _, blk, f in out_defs]
    out_shape = [jax.ShapeDtypeStruct(s, d) for s, d, _, _ in out_defs]
    need = 2 * (tm * tk + tk * tn) * a.dtype.itemsize + 3 * tm * tn * 4
    need += sum(2 * _nbytes(blk, arr.dtype) for arr, blk, _ in extras)
    need += sum(2 * _nbytes(blk, d) for _, d, blk, _ in out_defs)
    grid = (grid_m, N // tn, nk)
    scratch = [pltpu.VMEM((tm, tn), F32)] if nk > 1 else []
    params = pltpu.CompilerParams(
        dimension_semantics=("parallel", "arbitrary" if revisits else "parallel", "arbitrary"),
        vmem_limit_bytes=_vmem(need))
    operands = (a, b, *[arr for arr, _, _ in extras])
    if row_sel is None:
        return pl.pallas_call(body, name=name, grid=grid, in_specs=in_specs, out_specs=out_specs, out_shape=out_shape,
                              scratch_shapes=scratch, compiler_params=params)(*operands)
    grid_spec = pltpu.PrefetchScalarGridSpec(num_scalar_prefetch=1, grid=grid, in_specs=in_specs, out_specs=out_specs,
                                             scratch_shapes=scratch)
    return pl.pallas_call(body, name=name, grid_spec=grid_spec, out_shape=out_shape,
                          compiler_params=params)(row_sel[0], *operands)


def _grad_half(name, core, a, b, row_slabs, col_slabs, tm, other, recv=None, after=None):
    (_, M), (_, N) = a.shape, b.shape
    H = M // (2 * row_slabs)
    nh = H // tm
    tn = _fit(MM_TN, N // col_slabs)
    per = N // col_slabs // tn

    def a_block(i, core_ref):
        half = (1 - core_ref[0]) if other else core_ref[0]
        return (i // nh) * (2 * nh) + half * nh + i % nh

    def out_index(i, j):
        return (j // per, i, j % per) if col_slabs > 1 else (i // nh, i % nh, j)

    slabs = max(row_slabs, col_slabs)
    out_def = ((slabs, H, N // col_slabs), BF16, (1, tm, tn), out_index)

    def epilogue(acc, ex, outs):
        outs[0][0] = (acc if recv is None else acc + ex[0][0].astype(F32)).astype(BF16)

    extras = ([] if recv is None else [(recv, (1, tm, tn), out_index)]) + ([] if after is None else [_behind(after)])
    return _matmul(name, a, b, "tn", [out_def], epilogue, extras=extras, tm=tm, tn=tn,
                   row_sel=(core, a_block, row_slabs * nh))[0]


def _behind(token):
    return (token, (8, LANES), lambda i, j: (0, 0))


def _mm_plain(name, a, b, mode, out_dtype, after=None, **tiles):
    if mode == "nn":
        M, N = a.shape[0], b.shape[-1] * (b.shape[0] if b.ndim == 3 else 1)
    elif mode == "nt":
        M, N = a.shape[0], b.shape[-2]
    else:
        M, N = a.shape[1], b.shape[1]
    tm, tn = _fit(tiles.get("tm", MM_TM), M), _fit(tiles.get("tn", MM_TN), N)

    def epi(acc, ex, outs):
        outs[0][...] = acc.astype(out_dtype)

    return _matmul(name, a, b, mode, [((M, N), out_dtype, (tm, tn), lambda i, j: (i, j))], epi,
                   extras=[] if after is None else [_behind(after)], **tiles)[0]


def _rstd(v):
    return lax.rsqrt(jnp.mean(v * v, axis=-1, keepdims=True) + NORM_EPS)


ROW_TILE = 256


def _row_call(name, body, row_ins, vec_ins, row_outs, acc_outs, S, D):
    tr = _fit(ROW_TILE, S)
    row_spec = pl.BlockSpec((tr, D), lambda r: (r, 0))
    vec_spec = pl.BlockSpec((1, D), lambda r: (0, 0))
    in_specs = [row_spec] * len(row_ins) + [vec_spec] * len(vec_ins)
    out_specs = [row_spec] * len(row_outs) + [pl.BlockSpec(shp, lambda r: (0, 0)) for shp in acc_outs]
    out_shape = [jax.ShapeDtypeStruct((S, D), d) for d in row_outs] + [jax.ShapeDtypeStruct(shp, F32) for shp in acc_outs]
    need = sum(2 * tr * D * a.dtype.itemsize for a in row_ins) + sum(2 * tr * D * jnp.dtype(d).itemsize for d in row_outs)
    need += 8 * tr * D * 4
    return pl.pallas_call(
        body, name=name, grid=(S // tr,), in_specs=in_specs, out_specs=out_specs, out_shape=out_shape,
        compiler_params=pltpu.CompilerParams(dimension_semantics=("arbitrary",), vmem_limit_bytes=_vmem(need)),
    )(*row_ins, *vec_ins)


def _acc_rows(ref, rows):
    @pl.when(pl.program_id(0) == 0)
    def _():
        ref[...] = jnp.zeros_like(ref)
    for n, r in enumerate(rows):
        ref[n:n + 1, :] += r


def _pre_norm(x, g, sc, sh):
    S, D = x.shape

    def body(x_ref, g_ref, sc_ref, sh_ref, h_ref):
        xv = x_ref[...]
        xn = xv * _rstd(xv)
        h_ref[...] = (xn * g_ref[...] * (1.0 + sc_ref[...]) + sh_ref[...]).astype(BF16)

    return _row_call("pre_norm_mix", body, [x], [g, sc, sh], [BF16], [], S, D)[0]


def _post_mix(x, mix, g_post, gt, g_pre, sc, sh):
    S, D = x.shape

    def body(x_ref, mix_ref, gp_ref, gt_ref, g2_ref, sc_ref, sh_ref, x1_ref, h2_ref):
        mv = mix_ref[...].astype(F32)
        x1 = x_ref[...] + gt_ref[...] * (mv * _rstd(mv) * gp_ref[...])
        x1_ref[...] = x1
        h2_ref[...] = (x1 * _rstd(x1) * g2_ref[...] * (1.0 + sc_ref[...]) + sh_ref[...]).astype(BF16)

    return _row_call("post_mix_pre_mlp", body, [x, mix], [g_post, gt, g_pre, sc, sh], [F32, BF16], [], S, D)


def _loss_and_post_mlp_bwd(x1, y, target, g_post, gt):
    S, D = x1.shape

    def body(x1_ref, y_ref, t_ref, g_ref, gt_ref, dy_ref, dout_ref, loss_ref, acc_ref):
        yv = y_ref[...].astype(F32)
        r = _rstd(yv)
        yh = yv * r
        n = yh * g_ref[...]
        diff = x1_ref[...] + gt_ref[...] * n - t_ref[...]
        dout = diff * (1.0 / D)
        dout_ref[...] = dout
        dn = dout * gt_ref[...]
        dyh = dn * g_ref[...]
        dy_ref[...] = (r * (dyh - yh * jnp.mean(dyh * yh, axis=-1, keepdims=True))).astype(BF16)
        _acc_rows(acc_ref, [jnp.sum(dout * n, axis=0, keepdims=True), jnp.sum(dn * yh, axis=0, keepdims=True)])

        @pl.when(pl.program_id(0) == 0)
        def _():
            loss_ref[...] = jnp.zeros_like(loss_ref)
        loss_ref[...] += jnp.full(loss_ref.shape, (0.5 / D) * jnp.sum(diff * diff), F32)

    return _row_call("loss_post_mlp_bwd", body, [x1, y, target], [g_post, gt], [BF16, F32],
                     [(8, LANES), (8, D)], S, D)


def _pre_mlp_and_post_mix_bwd(dh2, x1, dout, mix, g_pre, sc, g_post, gt):
    S, D = x1.shape

    def body(dh_ref, x1_ref, dout_ref, mix_ref, g_ref, sc_ref, gp_ref, gt_ref, dx1_ref, dmix_ref, acc_ref):
        dh = dh_ref[...].astype(F32)
        x1v = x1_ref[...]
        r3 = _rstd(x1v)
        xn = x1v * r3
        dxn = dh * (1.0 + sc_ref[...]) * g_ref[...]
        dx1 = dout_ref[...] + r3 * (dxn - xn * jnp.mean(dxn * xn, axis=-1, keepdims=True))
        dx1_ref[...] = dx1
        mv = mix_ref[...].astype(F32)
        r2 = _rstd(mv)
        mh = mv * r2
        dn = dx1 * gt_ref[...]
        dmh = dn * gp_ref[...]
        dmix_ref[...] = (r2 * (dmh - mh * jnp.mean(dmh * mh, axis=-1, keepdims=True))).astype(BF16)
        _acc_rows(acc_ref, [
            jnp.sum(dh, axis=0, keepdims=True),
            jnp.sum(dh * xn * g_ref[...], axis=0, keepdims=True),
            jnp.sum(dh * (1.0 + sc_ref[...]) * xn, axis=0, keepdims=True),
            jnp.sum(dx1 * mh * gp_ref[...], axis=0, keepdims=True),
            jnp.sum(dn * mh, axis=0, keepdims=True)])

    return _row_call("pre_mlp_post_mix_bwd", body, [dh2, x1, dout, mix], [g_pre, sc, g_post, gt], [F32, BF16],
                     [(8, D)], S, D)


def _pre_mix_bwd(dh, x, dx1, g_pre, sc):
    S, D = x.shape

    def body(dh_ref, x_ref, dx1_ref, g_ref, sc_ref, gx_ref, acc_ref):
        dhv = dh_ref[...].astype(F32)
        xv = x_ref[...]
        r = _rstd(xv)
        xn = xv * r
        dxn = dhv * (1.0 + sc_ref[...]) * g_ref[...]
        gx_ref[...] = dx1_ref[...] + r * (dxn - xn * jnp.mean(dxn * xn, axis=-1, keepdims=True))
        _acc_rows(acc_ref, [
            jnp.sum(dhv, axis=0, keepdims=True),
            jnp.sum(dhv * xn * g_ref[...], axis=0, keepdims=True),
            jnp.sum(dhv * (1.0 + sc_ref[...]) * xn, axis=0, keepdims=True)])

    return _row_call("pre_mix_bwd", body, [dh, x, dx1], [g_pre, sc], [F32], [(8, D)], S, D)


CUM_BLOCK = 256


def _tri(n, upper):
    r = lax.broadcasted_iota(jnp.int32, (n, n), 0)
    c = lax.broadcasted_iota(jnp.int32, (n, n), 1)
    return ((c >= r) if upper else (c <= r)).astype(F32)


def _fox_gate_fwd(fg, b_pad):
    S = fg.shape[0]
    cb = _fit(CUM_BLOCK, S)

    def body(fg_ref, b_ref, cumt_ref, cum_ref):
        low = _tri(cb, False)
        carry = jnp.zeros((1, LANES), F32)
        for n in range(S // cb):
            z = fg_ref[n * cb:(n + 1) * cb, :] + b_ref[...]
            logf = jnp.minimum(z, 0.0) - jnp.log(1.0 + jnp.exp(-jnp.abs(z)))
            blk = jnp.dot(low, logf, precision=lax.Precision.HIGHEST, preferred_element_type=F32) + carry
            cum_ref[n * cb:(n + 1) * cb, :] = blk
            carry = blk[cb - 1:cb, :]
        cumt_ref[...] = cum_ref[...].T

    return pl.pallas_call(
        body, name="fox_gate_fwd", out_shape=jax.ShapeDtypeStruct((LANES, S), F32),
        scratch_shapes=[pltpu.VMEM((S, LANES), F32)],
        compiler_params=pltpu.CompilerParams(vmem_limit_bytes=_vmem(6 * S * LANES * 4)),
    )(fg, b_pad)


def _fox_gate_bwd(dcum_k, dcum_q, fg, b_pad):
    S = fg.shape[0]
    n_fox = dcum_q.shape[0]
    cb = _fit(CUM_BLOCK, S)

    def body(dk_ref, dq_ref, fg_ref, b_ref, dfg_ref, db_ref, dc_ref):
        lane = lax.broadcasted_iota(jnp.int32, (S, LANES), 1)
        dc = dk_ref[...].T
        for h in range(n_fox):
            dc = dc + jnp.where(lane == h, dq_ref[h], 0.0)
        dc_ref[...] = dc
        up = _tri(cb, True)
        carry = jnp.zeros((1, LANES), F32)
        db = jnp.zeros((1, LANES), F32)
        for n in reversed(range(S // cb)):
            blk = jnp.dot(up, dc_ref[n * cb:(n + 1) * cb, :], precision=lax.Precision.HIGHEST,
                          preferred_element_type=F32) + carry
            carry = blk[0:1, :]
            z = fg_ref[n * cb:(n + 1) * cb, :] + b_ref[...]
            dfg = blk * (1.0 / (1.0 + jnp.exp(z)))
            dfg_ref[n * cb:(n + 1) * cb, :] = dfg.astype(BF16)
            db = db + jnp.sum(dfg, axis=0, keepdims=True)
        db_ref[...] = jnp.broadcast_to(db, db_ref.shape)

    return pl.pallas_call(
        body, name="fox_gate_bwd",
        out_shape=[jax.ShapeDtypeStruct((S, LANES), BF16), jax.ShapeDtypeStruct((8, LANES), F32)],
        scratch_shapes=[pltpu.VMEM((S, LANES), F32)],
        compiler_params=pltpu.CompilerParams(vmem_limit_bytes=_vmem((8 + 2 * n_fox) * S * LANES * 4)),
    )(dcum_k, dcum_q, fg, b_pad)


FOX_TILE = 512


LOG2E = 1.4426950408889634


def _fox_scores(q, k, ck2, masked, t):
    s = lax.dot_general(q, k, _NT, preferred_element_type=F32) * (HEAD_DIM ** -0.5 * LOG2E) - ck2
    if masked:
        row = lax.broadcasted_iota(jnp.int32, (t, t), 0)
        col = lax.broadcasted_iota(jnp.int32, (t, t), 1)
        s = jnp.where(col <= row, s, NEG)
    return s


def _fox_fwd(proj, cum_row, n_fox):
    S = proj.shape[0]
    t = _fit(FOX_TILE, S)
    nq = S // t

    def body(q_ref, k_ref, v_ref, ck_ref, o_ref, lse_ref):
        def q_block(qi, _):
            q0 = pl.multiple_of(qi * t, t)
            q = q_ref[pl.ds(q0, t), :]

            def kv_block(j, carry, masked):
                m, l, acc = carry
                k0 = pl.multiple_of(j * t, t)
                s = _fox_scores(q, k_ref[pl.ds(k0, t), :], ck_ref[0, :, pl.ds(k0, t)] * LOG2E, masked, t)
                m_new = jnp.maximum(m, jnp.max(s, axis=-1, keepdims=True))
                alpha = jnp.exp2(m - m_new)
                p = jnp.exp2(s - m_new)
                l = alpha * l + jnp.sum(p, axis=-1, keepdims=True)
                acc = alpha * acc + jnp.dot(p.astype(BF16), v_ref[pl.ds(k0, t), :], preferred_element_type=F32)
                return m_new, l, acc

            init = (jnp.full((t, 1), NEG, F32), jnp.zeros((t, 1), F32), jnp.zeros((t, HEAD_DIM), F32))
            carry = lax.fori_loop(0, qi, lambda j, cr: kv_block(j, cr, False), init)
            m, l, acc = kv_block(qi, carry, True)
            o_ref[pl.ds(q0, t), :] = acc / l
            lse_ref[0, pl.ds(q0, t), :] = jnp.broadcast_to(m + jnp.log(l) * LOG2E, (t, LANES))
            return 0

        lax.fori_loop(0, nq, q_block, 0)

    col = lambda off: pl.BlockSpec((S, HEAD_DIM), lambda h: (0, off + h))
    per_head = pl.BlockSpec((1, S, LANES), lambda h: (h, 0, 0))
    return pl.pallas_call(
        body, name="fox_fwd", grid=(n_fox,),
        in_specs=[col(0), col(n_fox), col(2 * n_fox), pl.BlockSpec((1, 1, S), lambda h: (h, 0, 0))],
        out_specs=[pl.BlockSpec((S, HEAD_DIM), lambda h: (0, h)), per_head],
        out_shape=[jax.ShapeDtypeStruct((S, n_fox * HEAD_DIM), F32), jax.ShapeDtypeStruct((n_fox, S, LANES), F32)],
        compiler_params=pltpu.CompilerParams(dimension_semantics=("parallel",),
                                             vmem_limit_bytes=_vmem(16 * S * HEAD_DIM * 4 + 12 * t * t * 4)),
    )(proj, proj, proj, cum_row)


def _fox_bwd(proj, o, do, lse_b, cum_row, n_fox):
    S = proj.shape[0]
    t = _fit(FOX_TILE, S)
    nq = S // t
    scale = HEAD_DIM ** -0.5

    def body(q_ref, k_ref, v_ref, o_ref, do_ref, lse_ref, ck_ref, dq_ref, dk_ref, dv_ref, dc_ref, dcq_ref,
             dq_acc, delta_ref):
        dq_acc[...] = jnp.zeros_like(dq_acc)
        dcq_ref[...] = jnp.zeros_like(dcq_ref)

        def delta_block(qi, _):
            q0 = pl.multiple_of(qi * t, t)
            d = jnp.sum(do_ref[pl.ds(q0, t), :] * o_ref[pl.ds(q0, t), :], axis=-1, keepdims=True)
            delta_ref[pl.ds(q0, t), :] = jnp.broadcast_to(d, (t, LANES))
            return 0

        lax.fori_loop(0, nq, delta_block, 0)

        def kv_block(j, _):
            k0 = pl.multiple_of(j * t, t)
            k = k_ref[pl.ds(k0, t), :]
            v = v_ref[pl.ds(k0, t), :]
            ck2 = ck_ref[0, :, pl.ds(k0, t)] * LOG2E

            def q_block(qi, carry, masked):
                dk, dv, dc = carry
                q0 = pl.multiple_of(qi * t, t)
                q = q_ref[pl.ds(q0, t), :]
                dov = do_ref[pl.ds(q0, t), :].astype(BF16)
                p = jnp.exp2(_fox_scores(q, k, ck2, masked, t) - lse_ref[0, pl.ds(q0, t), :][:, :1])
                dp = lax.dot_general(dov, v, _NT, preferred_element_type=F32)
                ds = p * (dp - delta_ref[pl.ds(q0, t), :][:, :1])
                dsb = ds.astype(BF16)
                dv = dv + lax.dot_general(p.astype(BF16), dov, _TN, preferred_element_type=F32)
                dk = dk + lax.dot_general(dsb, q, _TN, preferred_element_type=F32)
                dq_acc[pl.ds(q0, t), :] += jnp.dot(dsb, k, preferred_element_type=F32)
                dc = dc - jnp.sum(ds, axis=0, keepdims=True)
                dcq_ref[0, pl.ds(q0, t), :] += jnp.broadcast_to(jnp.sum(ds, axis=1, keepdims=True), (t, LANES))
                return dk, dv, dc

            init = (jnp.zeros((t, HEAD_DIM), F32), jnp.zeros((t, HEAD_DIM), F32), jnp.zeros((1, t), F32))
            carry = q_block(j, init, True)
            dk, dv, dc = lax.fori_loop(j + 1, nq, lambda qi, cr: q_block(qi, cr, False), carry)
            dk_ref[pl.ds(k0, t), :] = (dk * scale).astype(BF16)
            dv_ref[pl.ds(k0, t), :] = dv.astype(BF16)
            dc_ref[0, :, pl.ds(k0, t)] = dc
            return 0

        lax.fori_loop(0, nq, kv_block, 0)
        dq_ref[...] = (dq_acc[...] * scale).astype(BF16)

    col = lambda off: pl.BlockSpec((S, HEAD_DIM), lambda h: (0, off + h))
    per_head = pl.BlockSpec((1, S, LANES), lambda h: (h, 0, 0))
    row = pl.BlockSpec((1, 1, S), lambda h: (h, 0, 0))
    grad = jax.ShapeDtypeStruct((S, n_fox * HEAD_DIM), BF16)
    return pl.pallas_call(
        body, name="fox_bwd", grid=(n_fox,),
        in_specs=[col(0), col(n_fox), col(2 * n_fox), col(0), col(0), per_head, row],
        out_specs=[col(0), col(0), col(0), row, per_head],
        out_shape=[grad, grad, grad, jax.ShapeDtypeStruct((n_fox, 1, S), F32), jax.ShapeDtypeStruct((n_fox, S, LANES), F32)],
        scratch_shapes=[pltpu.VMEM((S, HEAD_DIM), F32), pltpu.VMEM((S, LANES), F32)],
        compiler_params=pltpu.CompilerParams(dimension_semantics=("parallel",),
                                             vmem_limit_bytes=_vmem(24 * S * HEAD_DIM * 4 + 16 * t * t * 4)),
    )(proj, proj, proj, o, do, lse_b, cum_row)


def _rope_tables(S):
    half = HEAD_DIM // 2
    inv_freq = 1.0 / (ROPE_THETA ** (jnp.arange(half, dtype=F32) * (2.0 / HEAD_DIM)))
    ang = jnp.arange(S).astype(F32)[:, None] * inv_freq[None, :]
    cos, sin = jnp.cos(ang), jnp.sin(ang)
    return jnp.concatenate([cos, cos], axis=-1), jnp.concatenate([-sin, sin], axis=-1)


def _rope(name, src, first_block, n_blocks, cos, sin_signed):
    S = src.shape[0]

    def body(x_ref, cos_ref, sin_ref, o_ref):
        xv = x_ref[...].astype(F32)
        o_ref[...] = (xv * cos_ref[...] + pltpu.roll(xv, HEAD_DIM // 2, 1) * sin_ref[...]).astype(BF16)

    table = pl.BlockSpec((S, HEAD_DIM), lambda n: (0, 0))
    return pl.pallas_call(
        body, name=name, grid=(n_blocks,),
        in_specs=[pl.BlockSpec((S, HEAD_DIM), lambda n: (0, first_block + n)), table, table],
        out_specs=pl.BlockSpec((S, HEAD_DIM), lambda n: (0, n)),
        out_shape=jax.ShapeDtypeStruct((S, n_blocks * HEAD_DIM), BF16),
        compiler_params=pltpu.CompilerParams(dimension_semantics=("parallel",),
                                             vmem_limit_bytes=_vmem(12 * S * HEAD_DIM * 4)),
    )(src, cos, sin_signed)


def _swa_tile(q_ref, kp_ref, kc_ref, n, group, scale):
    B = SWA_BLOCK
    qs = jnp.concatenate([q_ref[:, g * HEAD_DIM:(g + 1) * HEAD_DIM] for g in range(group)], axis=0)
    kcat = jnp.concatenate([kp_ref[...], kc_ref[...]], axis=0)
    s = lax.dot_general(qs, kcat, _NT, preferred_element_type=F32) * scale
    qi = lax.broadcasted_iota(jnp.int32, (group * B, 2 * B), 0) % B
    kj = lax.broadcasted_iota(jnp.int32, (group * B, 2 * B), 1)
    diff = qi + B - kj
    mask = (diff >= 0) & (diff < B) & ((n * B + kj - B) >= 0)
    return qs, kcat, jnp.where(mask, s, NEG)


def _swa_sink_col(sink_ref, kv, group):
    head = lax.broadcasted_iota(jnp.int32, (group * SWA_BLOCK, 1), 0) // SWA_BLOCK
    col = jnp.zeros((group * SWA_BLOCK, 1), F32)
    for g in range(group):
        col = jnp.where(head == g, sink_ref[kv * group + g], col)
    return col


def _swa_specs(n_kv, group, q_first, k_first, v_first):
    B = SWA_BLOCK
    prev = lambda n: jnp.maximum(n - 1, 0)
    return [
        pl.BlockSpec((B, group * HEAD_DIM), lambda kv, n: (n, q_first + kv)),
        pl.BlockSpec((B, HEAD_DIM), lambda kv, n: (prev(n), k_first + kv)),
        pl.BlockSpec((B, HEAD_DIM), lambda kv, n: (n, k_first + kv)),
        pl.BlockSpec((B, HEAD_DIM), lambda kv, n: (prev(n), v_first + kv)),
        pl.BlockSpec((B, HEAD_DIM), lambda kv, n: (n, v_first + kv)),
    ]


def _swa_fwd(rq, proj, v_first, sinks, n_q, n_kv):
    S = rq.shape[0]
    B = SWA_BLOCK
    group = n_q // n_kv
    scale = HEAD_DIM ** -0.5

    def body(q_ref, kp_ref, kc_ref, vp_ref, vc_ref, sink_ref, o_ref, lse_ref):
        kv, n = pl.program_id(0), pl.program_id(1)
        _, _, s = _swa_tile(q_ref, kp_ref, kc_ref, n, group, scale)
        sink = _swa_sink_col(sink_ref, kv, group)
        m = jnp.maximum(jnp.max(s, axis=-1, keepdims=True), sink)
        p = jnp.exp(s - m)
        denom = jnp.sum(p, axis=-1, keepdims=True) + jnp.exp(sink - m)
        vcat = jnp.concatenate([vp_ref[...], vc_ref[...]], axis=0)
        o = jnp.dot((p / denom).astype(BF16), vcat, preferred_element_type=F32)
        lse = m + jnp.log(denom)
        for g in range(group):
            o_ref[:, g * HEAD_DIM:(g + 1) * HEAD_DIM] = o[g * B:(g + 1) * B, :]
            lse_ref[0, :, g * LANES:(g + 1) * LANES] = jnp.broadcast_to(lse[g * B:(g + 1) * B, :], (B, LANES))

    specs = _swa_specs(n_kv, group, 0, n_q, v_first)
    q_blk = pl.BlockSpec((B, group * HEAD_DIM), lambda kv, n: (n, kv))
    return pl.pallas_call(
        body, name="swa_fwd", grid=(n_kv, S // B),
        in_specs=specs + [pl.BlockSpec(memory_space=pltpu.SMEM)],
        out_specs=[q_blk, pl.BlockSpec((1, B, group * LANES), lambda kv, n: (kv, n, 0))],
        out_shape=[jax.ShapeDtypeStruct((S, n_q * HEAD_DIM), F32), jax.ShapeDtypeStruct((n_kv, S, group * LANES), F32)],
        compiler_params=pltpu.CompilerParams(dimension_semantics=("parallel", "arbitrary")),
    )(rq, rq, rq, proj, proj, sinks)


def _swa_bwd(rq, proj, v_first, sinks, o, do, do_first, lse_b, n_q, n_kv):
    S = rq.shape[0]
    B = SWA_BLOCK
    group = n_q // n_kv
    scale = HEAD_DIM ** -0.5

    def body(q_ref, kp_ref, kc_ref, vp_ref, vc_ref, o_ref, do_ref, lse_ref, sink_ref,
             dq_ref, dk_ref, dv_ref, dsink_ref):
        kv, n = pl.program_id(0), pl.program_id(1)

        @pl.when(n == 0)
        def _():
            dk_ref[...] = jnp.zeros_like(dk_ref)
            dv_ref[...] = jnp.zeros_like(dv_ref)
            dsink_ref[...] = jnp.zeros_like(dsink_ref)

        qs, kcat, s = _swa_tile(q_ref, kp_ref, kc_ref, n, group, scale)
        sink = _swa_sink_col(sink_ref, kv, group)
        stack = lambda ref, w: jnp.concatenate([ref[:, g * w:(g + 1) * w] for g in range(group)], axis=0)
        lse = jnp.concatenate([lse_ref[0, :, g * LANES:g * LANES + 1] for g in range(group)], axis=0)
        do32 = stack(do_ref, HEAD_DIM)
        delta = jnp.sum(do32 * stack(o_ref, HEAD_DIM), axis=-1, keepdims=True)
        dov = do32.astype(BF16)
        p = jnp.exp(s - lse)
        vcat = jnp.concatenate([vp_ref[...], vc_ref[...]], axis=0)
        dp = lax.dot_general(dov, vcat, _NT, preferred_element_type=F32)
        ds = p * (dp - delta)
        dsb = ds.astype(BF16)
        dq = jnp.dot(dsb, kcat, preferred_element_type=F32) * scale
        for g in range(group):
            dq_ref[:, g * HEAD_DIM:(g + 1) * HEAD_DIM] = dq[g * B:(g + 1) * B, :].astype(BF16)
        dkcat = lax.dot_general(dsb, qs, _TN, preferred_element_type=F32) * scale
        dvcat = lax.dot_general(p.astype(BF16), dov, _TN, preferred_element_type=F32)
        prev0 = pl.multiple_of(jnp.maximum(n - 1, 0) * B, B)
        cur0 = pl.multiple_of(n * B, B)
        dk_ref[0, pl.ds(prev0, B), :] += dkcat[:B, :]
        dk_ref[0, pl.ds(cur0, B), :] += dkcat[B:, :]
        dv_ref[0, pl.ds(prev0, B), :] += dvcat[:B, :]
        dv_ref[0, pl.ds(cur0, B), :] += dvcat[B:, :]
        dsk = -jnp.exp(sink - lse) * delta
        lane = lax.broadcasted_iota(jnp.int32, (1, LANES), 1)
        row = jnp.zeros((1, LANES), F32)
        for g in range(group):
            row = row + jnp.where(lane == g, jnp.sum(dsk[g * B:(g + 1) * B, :]), 0.0)
        dsink_ref[0, 0:1, :] += row

    specs = _swa_specs(n_kv, group, 0, n_q, v_first)
    q_blk = pl.BlockSpec((B, group * HEAD_DIM), lambda kv, n: (n, kv))
    acc = pl.BlockSpec((1, S, HEAD_DIM), lambda kv, n: (kv, 0, 0))
    return pl.pallas_call(
        body, name="swa_bwd", grid=(n_kv, S // B),
        in_specs=specs + [q_blk, pl.BlockSpec((B, group * HEAD_DIM), lambda kv, n: (n, do_first + kv)),
                          pl.BlockSpec((1, B, group * LANES), lambda kv, n: (kv, n, 0)),
                          pl.BlockSpec(memory_space=pltpu.SMEM)],
        out_specs=[q_blk, acc, acc, pl.BlockSpec((1, 8, LANES), lambda kv, n: (kv, 0, 0))],
        out_shape=[jax.ShapeDtypeStruct((S, n_q * HEAD_DIM), BF16), jax.ShapeDtypeStruct((n_kv, S, HEAD_DIM), F32),
                   jax.ShapeDtypeStruct((n_kv, S, HEAD_DIM), F32), jax.ShapeDtypeStruct((n_kv, 8, LANES), F32)],
        compiler_params=pltpu.CompilerParams(dimension_semantics=("parallel", "arbitrary")),
    )(rq, rq, rq, proj, proj, o, do, lse_b, sinks)


def _adamw(w, g, m, v):
    m = ADAM_B1 * m + (1.0 - ADAM_B1) * g
    v = ADAM_B2 * v + (1.0 - ADAM_B2) * (g * g)
    m_hat = m / (1.0 - ADAM_B1 ** ADAM_STEP)
    v_hat = v / (1.0 - ADAM_B2 ** ADAM_STEP)
    delta = -ADAM_LR * (m_hat / (jnp.sqrt(v_hat) + ADAM_EPS) + ADAM_WD * w)
    return delta, m, v


def _mod_fwd(cond_in, w_mod, b_shard):
    R, D = cond_in.shape
    cols = w_mod.shape[1]
    tn = _fit(512, cols)

    def body(c_ref, w_ref, b_ref, o_ref):
        cv = c_ref[...]
        cond = (cv / (1.0 + jnp.exp(-cv))).astype(BF16)
        o_ref[...] = jnp.dot(cond, w_ref[...].astype(BF16), preferred_element_type=F32) + b_ref[...]

    return pl.pallas_call(
        body, name="mod_fwd", grid=(cols // tn,),
        in_specs=[pl.BlockSpec((R, D), lambda j: (0, 0)), pl.BlockSpec((D, tn), lambda j: (0, j)),
                  pl.BlockSpec((1, tn), lambda j: (0, j))],
        out_specs=pl.BlockSpec((R, tn), lambda j: (0, j)),
        out_shape=jax.ShapeDtypeStruct((R, cols), F32),
        compiler_params=pltpu.CompilerParams(dimension_semantics=("parallel",), vmem_limit_bytes=_vmem(3 * D * tn * 4)),
    )(cond_in, w_mod, b_shard)


def _mod_update(c_t, dmod, w, m, v):
    D, nb = c_t.shape
    cols = w.shape[1]
    tr = _fit(128, D)

    def body(c_ref, d_ref, w_ref, m_ref, v_ref, g_ref, dl_ref, nm_ref, nv_ref):
        cv = c_ref[...]
        cond = cv / (1.0 + jnp.exp(-cv))
        g = jnp.zeros((tr, cols), F32)
        for b in range(nb):
            g = g + cond[:, b:b + 1] * d_ref[b:b + 1, :]
        g_ref[...] = g
        dl_ref[...], nm_ref[...], nv_ref[...] = _adamw(w_ref[...], g, m_ref[...], v_ref[...])

    blk = pl.BlockSpec((tr, cols), lambda r: (r, 0))
    out = jax.ShapeDtypeStruct((D, cols), F32)
    return pl.pallas_call(
        body, name="mod_update", grid=(D // tr,),
        in_specs=[pl.BlockSpec((tr, nb), lambda r: (r, 0)), pl.BlockSpec((nb, cols), lambda r: (0, 0)), blk, blk, blk],
        out_specs=[blk] * 4, out_shape=[out] * 4,
        compiler_params=pltpu.CompilerParams(dimension_semantics=("parallel",), vmem_limit_bytes=_vmem(18 * tr * cols * 4)),
    )(c_t, dmod, w, m, v)


def _small_update(stacked, w, m, v):
    R, C = w.shape

    def body(s_ref, w_ref, m_ref, v_ref, g_ref, dl_ref, nm_ref, nv_ref):
        g = s_ref[0:R, :]
        for d in range(1, N_DEV):
            g = g + s_ref[d * R:(d + 1) * R, :]
        g_ref[...] = g
        dl_ref[...], nm_ref[...], nv_ref[...] = _adamw(w_ref[...], g, m_ref[...], v_ref[...])

    return pl.pallas_call(body, name="small_update", out_shape=[jax.ShapeDtypeStruct((R, C), F32)] * 4)(stacked, w, m, v)


def _place():
    return lax.axis_index("x"), lax.axis_index("y"), lax.axis_index("c")


def _allgather8(name, block):
    m_per, n = block.shape

    def body(x_ref, out_ref, token_ref, send_sems, recv_sems, local_sem):
        token_ref[...] = jnp.zeros_like(token_ref)
        x, y, c = _place()
        me, sibling = (x, y, c), (x, y, 1 - c)
        chips = [(1 - x, y), (x, 1 - y), (1 - x, 1 - y)]

        def rows(px, py, pc):
            return out_ref.at[pl.ds((4 * px + 2 * py + pc) * m_per, m_per), :]

        def copy(k, blk, to, src=None):
            return pltpu.make_async_remote_copy(
                src_ref=rows(*blk) if src is None else src, dst_ref=rows(*blk),
                send_sem=send_sems.at[k], recv_sem=recv_sems.at[k], device_id=to, device_id_type=MESH)

        mine = pltpu.make_async_copy(x_ref, rows(*me), local_sem)
        mine.start()
        first = [copy(0, me, sibling, src=x_ref)]
        first += [copy(1 + j, me, (*chip, c), src=x_ref) for j, chip in enumerate(chips)]
        for cp in first:
            cp.start()
        passed = [copy(4 + j, (*chip, c), sibling) for j, chip in enumerate(chips)]
        for j, chip in enumerate(chips):
            copy(1 + j, (*chip, c), me).wait_recv()
            passed[j].start()
        copy(0, sibling, me).wait_recv()
        for j, chip in enumerate(chips):
            copy(4 + j, (*chip, 1 - c), me).wait_recv()
        for cp in first + passed:
            cp.wait_send()
        mine.wait()

    vmem = pl.BlockSpec(memory_space=pltpu.VMEM)
    return pl.pallas_call(
        body, name=name,
        out_shape=[jax.ShapeDtypeStruct((N_DEV * m_per, n), block.dtype), jax.ShapeDtypeStruct((8, LANES), F32)],
        in_specs=[vmem], out_specs=[vmem, vmem],
        scratch_shapes=[pltpu.SemaphoreType.DMA((7,)), pltpu.SemaphoreType.DMA((7,)), pltpu.SemaphoreType.DMA],
    )(block)


_ANY = pl.BlockSpec(memory_space=pl.ANY)


def _half(ref, c, rows):
    return ref.at[pl.ds(c * (rows // 2), rows // 2), :]


_HBM = pl.BlockSpec(memory_space=pltpu.HBM)
_SEM = pl.BlockSpec(memory_space=pltpu.SEMAPHORE)
_EFFECT = pltpu.SideEffectType.DATAFLOW_SIDE_EFFECTING


def _ici_start(name, srcs, land_shapes, plan, per_source=3, after=None):
    ns, nl = len(srcs), len(land_shapes)
    n_copies = per_source * ns
    n_in = ns + nl + (after is not None)

    def body(*refs):
        src_refs, land_refs = refs[:ns], refs[ns:ns + nl]
        send_sems, recv_sems = refs[n_in], refs[n_in + 1]
        token = refs[-1]
        for n, (src, dst, peer, _) in enumerate(plan(src_refs, land_refs)):
            pltpu.make_async_remote_copy(src_ref=src, dst_ref=dst, send_sem=send_sems.at[n], recv_sem=recv_sems.at[n],
                                         device_id=peer, device_id_type=MESH).start()
        token[...] = jnp.zeros_like(token)

    lands = [lax.empty(s.shape, s.dtype) for s in land_shapes]
    out = pl.pallas_call(
        body, name=name,
        out_shape=(pltpu.SemaphoreType.DMA((n_copies,)), pltpu.SemaphoreType.DMA((n_copies,)),
                   *[pltpu.HBM(a.shape, a.dtype) for a in list(srcs) + lands], jax.ShapeDtypeStruct((8, LANES), F32)),
        in_specs=[_HBM] * (ns + nl) + [_ANY] * (after is not None),
        out_specs=(_SEM, _SEM, *[_HBM] * (ns + nl), pl.BlockSpec(memory_space=pltpu.VMEM)),
        input_output_aliases={n: 2 + n for n in range(ns + nl)},
        compiler_params=pltpu.CompilerParams(has_side_effects=_EFFECT),
    )(*[pltpu.with_memory_space_constraint(a, pltpu.HBM) for a in list(srcs) + lands],
      *([] if after is None else [after]))
    return out[0], out[1], list(out[2:2 + ns]), list(out[2 + ns:2 + ns + nl]), out[-1]


def _ici_wait(name, send_sems, recv_sems, srcs, lands, plan, after):
    ns, nl = len(srcs), len(lands)
    after = list(after) if isinstance(after, (list, tuple)) else [after]

    def body(*refs):
        src_refs, land_refs = refs[:ns], refs[ns:ns + nl]
        send_sems, recv_sems = refs[ns + nl], refs[ns + nl + 1]
        for n, (src, _, peer, mine) in enumerate(plan(src_refs, land_refs)):
            cp = pltpu.make_async_remote_copy(src_ref=src, dst_ref=mine, send_sem=send_sems.at[n],
                                              recv_sem=recv_sems.at[n], device_id=peer, device_id_type=MESH)
            cp.wait_send()
            cp.wait_recv()

    out = pl.pallas_call(
        body, name=name, out_shape=[pltpu.HBM(a.shape, a.dtype) for a in list(srcs) + list(lands)],
        in_specs=[_HBM] * (ns + nl) + [_SEM, _SEM] + [_ANY] * len(after), out_specs=[_HBM] * (ns + nl),
        input_output_aliases={n: n for n in range(ns + nl)},
        compiler_params=pltpu.CompilerParams(has_side_effects=_EFFECT),
    )(*srcs, *lands, send_sems, recv_sems, *after)
    return list(out[:ns]), list(out[ns:])


def _own_slab(name, chip, w, after):
    R, C = w.shape
    tr, tc = _tiles(R, C)
    tied = [] if after is None else [after]

    def body(chip_ref, w_ref, *rest):
        stack_ref, token_ref = rest[-2:]
        stack_ref[0] = w_ref[...].astype(BF16)
        token_ref[...] = jnp.zeros_like(token_ref)

    small = pl.BlockSpec((8, LANES), lambda r, q, chip_ref: (0, 0))
    grid_spec = pltpu.PrefetchScalarGridSpec(
        num_scalar_prefetch=1, grid=(R // tr, C // tc),
        in_specs=[pl.BlockSpec((tr, tc), lambda r, q, chip_ref: (r, q))] + [small] * len(tied),
        out_specs=[pl.BlockSpec((1, tr, tc), lambda r, q, chip_ref: (chip_ref[0], r, q)), small])
    return pl.pallas_call(
        body, name=name, grid_spec=grid_spec,
        out_shape=[jax.ShapeDtypeStruct((N_CHIPS, R, C), BF16), jax.ShapeDtypeStruct((8, LANES), F32)],
        compiler_params=pltpu.CompilerParams(dimension_semantics=("arbitrary", "arbitrary")),
    )(chip, w, *tied)


def _gather_plan(src_refs, land_refs):
    x, y, c = _place()
    copies = []
    for stack in src_refs:
        R = stack.shape[1]
        own = _half(stack.at[2 * x + y], c, R)
        for cx, cy in [(1 - x, y), (x, 1 - y), (1 - x, 1 - y)]:
            copies.append((own, own, (cx, cy, c), _half(stack.at[2 * cx + cy], c, R)))
    return copies


def _pass_plan(src_refs, land_refs):
    x, y, c = _place()
    copies = []
    for land in src_refs:
        R = land.shape[1]
        for cx, cy in [(1 - x, y), (x, 1 - y), (1 - x, 1 - y)]:
            slot = land.at[2 * cx + cy]
            copies.append((_half(slot, c, R), _half(slot, c, R), (x, y, 1 - c), _half(slot, 1 - c, R)))
    return copies


def _share_plan(src_refs, land_refs):
    x, y, c = _place()
    return [(h, land, (x, y, 1 - c), land) for h, land in zip(src_refs, land_refs)]


def _pass_to_sibling(name, lands):
    nw = len(lands)

    def body(*refs):
        ins, outs = refs[:nw], refs[nw:2 * nw]
        send_sems, recv_sems = refs[2 * nw:]
        x, y, c = _place()
        chips = [(1 - x, y), (x, 1 - y), (1 - x, 1 - y)]
        copies = []
        for k in range(nw):
            R = ins[k].shape[1]
            for j, (cx, cy) in enumerate(chips):
                cp = pltpu.make_async_remote_copy(
                    src_ref=_half(ins[k].at[2 * cx + cy], c, R), dst_ref=_half(outs[k].at[2 * cx + cy], c, R),
                    send_sem=send_sems.at[3 * k + j], recv_sem=recv_sems.at[3 * k + j],
                    device_id=(x, y, 1 - c), device_id_type=MESH)
                cp.start()
                copies.append(cp)
        for k in range(nw):
            R = ins[k].shape[1]
            for j, (cx, cy) in enumerate(chips):
                pltpu.make_async_remote_copy(
                    src_ref=_half(ins[k].at[2 * cx + cy], c, R), dst_ref=_half(outs[k].at[2 * cx + cy], 1 - c, R),
                    send_sem=send_sems.at[3 * k + j], recv_sem=recv_sems.at[3 * k + j],
                    device_id=(x, y, 1 - c), device_id_type=MESH).wait_recv()
        for cp in copies:
            cp.wait_send()

    return pl.pallas_call(
        body, name=name, out_shape=[jax.ShapeDtypeStruct(a.shape, a.dtype) for a in lands],
        in_specs=[_ANY] * nw, out_specs=[_ANY] * nw, input_output_aliases={k: k for k in range(nw)},
        scratch_shapes=[pltpu.SemaphoreType.DMA((3 * nw,)), pltpu.SemaphoreType.DMA((3 * nw,))],
    )(*lands)


def _tie(vec, token):
    return vec + token[0:1, 0:1]


ROW_ALIGN = 16
TILE_ELEMS = 512 * 1024


def _tiles(rows, cols):
    fits = [t for t in range(ROW_ALIGN, min(rows, 256) + 1, ROW_ALIGN) if rows % t == 0]
    tr = fits[-1] if fits and fits[-1] >= 64 else rows
    tc = cols
    while tr * tc > TILE_ELEMS and tc % (2 * LANES) == 0:
        tc //= 2
    return tr, tc


def _scatter_plan(src_refs, land_refs):
    x, y, c = _place()
    copies = []
    for p, land in zip(src_refs, land_refs):
        for j, (cx, cy) in enumerate([(1 - x, y), (x, 1 - y), (1 - x, 1 - y)]):
            copies.append((p.at[2 * cx + cy], land.at[j], (cx, cy, c), land.at[j]))
    return copies


def _chip_add(name, chip, sums, recv):
    _, H, C = sums.shape
    tr, tc = _tiles(H, C)

    def body(chip_ref, p_ref, r_ref, o_ref):
        total = p_ref[0].astype(F32)
        for j in range(3):
            total = total + r_ref[j].astype(F32)
        o_ref[...] = total

    grid_spec = pltpu.PrefetchScalarGridSpec(
        num_scalar_prefetch=1, grid=(H // tr, C // tc),
        in_specs=[pl.BlockSpec((1, tr, tc), lambda r, q, chip_ref: (chip_ref[0], r, q)),
                  pl.BlockSpec((3, tr, tc), lambda r, q, chip_ref: (0, r, q))],
        out_specs=pl.BlockSpec((tr, tc), lambda r, q, chip_ref: (r, q)))
    return pl.pallas_call(
        body, name=name, grid_spec=grid_spec, out_shape=jax.ShapeDtypeStruct((H, C), F32),
        compiler_params=pltpu.CompilerParams(dimension_semantics=("parallel", "parallel")),
    )(chip, sums, recv)


def _pair_share(name, halves):
    nw = len(halves)

    def body(*refs):
        hs, outs = refs[:nw], refs[nw:2 * nw]
        send_sems, recv_sems = refs[2 * nw:]
        x, y, c = _place()
        copies = []
        for k in range(nw):
            cp = pltpu.make_async_remote_copy(
                src_ref=hs[k], dst_ref=outs[k], send_sem=send_sems.at[k], recv_sem=recv_sems.at[k],
                device_id=(x, y, 1 - c), device_id_type=MESH)
            cp.start()
            copies.append(cp)
        for cp in copies:
            cp.wait()

    return pl.pallas_call(
        body, name=name,
        out_shape=[jax.ShapeDtypeStruct(h.shape, h.dtype) for h in halves],
        in_specs=[_ANY] * nw, out_specs=[_ANY] * nw,
        scratch_shapes=[pltpu.SemaphoreType.DMA((nw,)), pltpu.SemaphoreType.DMA((nw,))],
    )(*halves)


def _adam_halves(name, core, w, g_own, g_other, m, v, out_rows=None):
    R, C = w.shape
    H = R // 2
    tr, tc = _tiles(H, C)
    nr, nc = H // tr, C // tc

    def body(core_ref, w_ref, go_ref, gr_ref, m_ref, v_ref, g_ref, dl_ref, nm_ref, nv_ref):
        own = (pl.program_id(0) // nr) == core_ref[0]
        g = jnp.where(own, go_ref[...], gr_ref[...])
        g_ref[...] = g
        dl_ref[...], nm_ref[...], nv_ref[...] = _adamw(w_ref[...], g, m_ref[...], v_ref[...])

    blk = pl.BlockSpec((tr, tc), lambda r, q, core_ref: (r, q))

    def half_spec(is_own):
        def index(r, q, core_ref):
            mine = ((r // nr) == core_ref[0]) == is_own
            done = is_own == (core_ref[0] == 0)
            return (jnp.where(mine, r % nr, jnp.where(done, nr - 1, 0)), jnp.where(mine, q, jnp.where(done, nc - 1, 0)))
        return pl.BlockSpec((tr, tc), index)
    out_rows = R if out_rows is None else out_rows
    assert R - tr < out_rows <= R, (R, tr, out_rows)
    out = jax.ShapeDtypeStruct((out_rows, C), F32)
    grid_spec = pltpu.PrefetchScalarGridSpec(
        num_scalar_prefetch=1, grid=(R // tr, nc), in_specs=[blk, half_spec(True), half_spec(False), blk, blk],
        out_specs=[blk] * 4)
    return pl.pallas_call(
        body, name=name, grid_spec=grid_spec, out_shape=[out] * 4,
        compiler_params=pltpu.CompilerParams(dimension_semantics=("parallel", "parallel"),
                                             vmem_limit_bytes=_vmem(20 * tr * tc * 4)),
    )(core, w, g_own, g_other, m, v)


def kernel(x, c, w_mod, b_mod, g_pre_mix, g_post_mix, w_in, b_forget, swa_sinks, w_out, g_pre_mlp, g_post_mlp, w_up, w_down, loss_target, m_w_mod, m_b_mod, m_g_pre_mix, m_g_post_mix, m_w_in, m_b_forget, m_swa_sinks, m_w_out, m_g_pre_mlp, m_g_post_mlp, m_w_up, m_w_down, v_w_mod, v_b_mod, v_g_pre_mix, v_g_post_mix, v_w_in, v_b_forget, v_swa_sinks, v_w_out, v_g_pre_mlp, v_g_post_mlp, v_w_up, v_w_down):
    S, D = x.shape[1], x.shape[2]
    n_heads = D // HEAD_DIM
    n_fox = n_heads // 2
    n_swa = n_heads - n_fox
    n_kv = max(1, n_swa // 4)
    fox_w, swa_w, kv_w = n_fox * HEAD_DIM, n_swa * HEAD_DIM, n_kv * HEAD_DIM
    main_w = 3 * fox_w + swa_w + 2 * kv_w
    in_w = main_w + n_fox
    mod_cols = w_mod.shape[2]

    ax, ay, ac = _place()
    chip = 2 * ax + ay
    dev = 2 * chip + ac
    chip_arr = jnp.reshape(chip, (1,)).astype(jnp.int32)
    core_arr = jnp.reshape(ac, (1,)).astype(jnp.int32)

    x2, tgt = x[0], loss_target[0]

    in_rows = in_w // N_CHIPS
    in_rows_pad = -(-in_rows // (2 * LANES)) * (2 * LANES)
    slab_w = N_CHIPS * in_rows_pad

    def rows_of(a):
        return jnp.pad(a[0].T, ((0, in_rows_pad - in_rows), (0, 0)))

    w_in_stack, token = _own_slab("own_slab_w_in", chip_arr, rows_of(w_in), None)

    c_all, _ = _allgather8("gather_c", _tie(c, token).reshape(8, D // 8))
    c_all = c_all.reshape(N_DEV, D)
    b_shard = lax.dynamic_slice_in_dim(b_mod, chip * mod_cols, mod_cols, axis=1)
    mod_shard = _mod_fwd(jnp.pad(c_all, ((0, 16 - N_DEV), (0, 0))), w_mod[0], b_shard)[:N_DEV]
    mod_all, token = _allgather8("gather_mod", mod_shard)
    mod_all = mod_all.reshape(N_CHIPS, 2, N_DEV, mod_cols)[:, 0]
    mod = lax.dynamic_index_in_dim(mod_all, dev, axis=1, keepdims=False).reshape(N_MOD, 1, D)
    sh_a, sc_a, gt_a, sh_m, sc_m, gt_m = [mod[n] for n in range(N_MOD)]

    def slab_cols(lo, hi):
        spans = []
        while lo < hi:
            s, r = divmod(lo, in_rows)
            n = min(hi - lo, in_rows - r)
            spans.append((s * in_rows_pad + r, s * in_rows_pad + r + n))
            lo += n
        return spans

    gate_lo = 3 * fox_w
    main_spans = slab_cols(0, gate_lo) + slab_cols(gate_lo + n_fox, in_w)
    (gate_first, gate_last), = slab_cols(gate_lo, gate_lo + n_fox)

    names = ["w_in", "w_out", "w_up", "w_down"]
    flights = {}
    for n, w in zip(names, [None, w_out[0], w_up[0], w_down[0]]):
        stack = w_in_stack if n == "w_in" else _own_slab("own_slab_" + n, chip_arr, w, token)[0]
        flights[n] = _ici_start("gather_start_" + n, [stack], [], _gather_plan, after=token)
        token = flights[n][4]
    sc_a = _tie(sc_a, token)

    def arrived(n, after):
        send, recv, stacks, _, _ = flights[n]
        stacks, _ = _ici_wait("gather_wait_" + n, send, recv, stacks, [], _gather_plan, after)
        return _ici_start("gather_pass_start_" + n, stacks, [], _pass_plan)

    def gathered(n, after, in_flight=None):
        if in_flight is None:
            send, recv, stacks, _, _ = flights[n]
            stacks, _ = _ici_wait("gather_wait_" + n, send, recv, stacks, [], _gather_plan, after)
            return _pass_to_sibling("gather_pass_" + n, stacks)[0]
        send, recv, stacks, _, _ = in_flight
        return _ici_wait("gather_pass_wait_" + n, send, recv, stacks, [], _pass_plan, after)[0][0]

    d_ff = N_CHIPS * w_up.shape[2]

    h = _pre_norm(x2, g_pre_mix, sc_a, sh_a)
    in_state = [rows_of(w_in)] + [rows_of(_tie(a, token)) for a in (m_w_in, v_w_in)]
    cos, sin_signed = _rope_tables(S)

    def pack(bm, gpm, gqm, gpl, gql, bf, sk):
        last = jnp.concatenate([bf, sk, jnp.zeros((1, D - n_fox - n_swa), F32)], axis=1)
        return jnp.concatenate([bm.reshape(N_MOD, D), gpm, gqm, gpl, gql, last, jnp.zeros((5, D), F32)], axis=0)

    small_state = [pack(b_mod, g_pre_mix, g_post_mix, g_pre_mlp, g_post_mlp, b_forget, swa_sinks),
                   pack(m_b_mod, m_g_pre_mix, m_g_post_mix, m_g_pre_mlp, m_g_post_mlp, m_b_forget, m_swa_sinks),
                   pack(v_b_mod, v_g_pre_mix, v_g_post_mix, v_g_pre_mlp, v_g_post_mlp, v_b_forget, v_swa_sinks)]
    ready = h[:8, :LANES].astype(F32) + cos[:8]
    w_slab_t = gathered("w_in", [ready] + in_state[1:] + small_state).reshape(slab_w, D)
    tm_p, tn_p = _fit(MM_TM, S), _fit(MM_TN if slab_w % MM_TN == 0 else MM_TN // 2, slab_w)
    win0 = gate_first // LANES * LANES
    win_j, win_off = divmod(win0, tn_p)
    assert win_off + 2 * LANES <= tn_p and gate_last - win0 <= 2 * LANES

    def proj_epilogue(acc, ex, outs):
        outs[0][...] = acc.astype(BF16)

        @pl.when(pl.program_id(1) == win_j)
        def _():
            outs[1][...] = acc[:, win_off:win_off + 2 * LANES]

    proj_slab, gate_win = _matmul(
        "in_proj", h, w_slab_t, "nt",
        [((S, slab_w), BF16, (tm_p, tn_p), lambda i, j: (i, j)), ((S, 2 * LANES), F32, (tm_p, 2 * LANES), lambda i, j: (i, 0))],
        proj_epilogue, tn=tn_p, revisits=True)
    proj = jnp.concatenate([proj_slab[:, lo:hi] for lo, hi in main_spans], axis=1)
    out_flight = arrived("w_out", proj_slab)
    fg = _tie(jnp.pad(gate_win[:, gate_first - win0:gate_last - win0], ((0, 0), (0, LANES - n_fox))), out_flight[4])
    b_pad = jnp.pad(b_forget, ((0, 0), (0, LANES - n_fox)))
    cum_row = _fox_gate_fwd(fg, b_pad)[:n_fox].reshape(n_fox, 1, S)
    fox_o, fox_lse = _fox_fwd(proj, cum_row, n_fox)

    rq = _rope("rope_fwd", proj, 3 * n_fox, n_swa + n_kv, cos, sin_signed)
    v_first = 3 * n_fox + n_swa + n_kv
    sinks = swa_sinks[0]
    swa_o, swa_lse = _swa_fwd(rq, proj, v_first, sinks, n_swa, n_kv)

    mixcat = jnp.concatenate([fox_o, swa_o], axis=1).astype(BF16)
    up_flight = arrived("w_up", mixcat)
    w_out_f = gathered("w_out", mixcat, out_flight).reshape(D, D)
    mix = _mm_plain("out_proj", mixcat, w_out_f, "nn", BF16, after=up_flight[4])
    x1, h2 = _post_mix(x2, mix, g_post_mix, gt_a, g_pre_mlp, sc_m, sh_m)
    w_up_f = gathered("w_up", h2, up_flight)

    tm_u, tn_u = _fit(MM_TM, S), _fit(MM_TN, d_ff)

    def up_epilogue(acc, ex, outs):
        outs[0][...] = acc.astype(BF16)
        r = jnp.maximum(acc, 0.0)
        outs[1][...] = (r * r).astype(BF16)

    ublk = ((S, d_ff), BF16, (tm_u, tn_u), lambda i, j: (i, j))
    u, a = _matmul("mlp_up", h2, w_up_f, "nn", [ublk, ublk], up_epilogue)
    w_down_f = gathered("w_down", a).reshape(d_ff, D)
    y = _mm_plain("mlp_down", a, w_down_f, "nn", BF16)

    dy, dout, loss_part, acc_mlp_post = _loss_and_post_mlp_bwd(x1, y, tgt, g_post_mlp, gt_m)

    def du_epilogue(acc, ex, outs):
        outs[0][...] = (acc * (2.0 * jnp.maximum(ex[0][...].astype(F32), 0.0))).astype(BF16)

    du = _matmul("mlp_down_bwd", dy, w_down_f, "nt", [ublk], du_epilogue,
                 extras=[(u, (tm_u, tn_u), lambda i, j: (i, j))])[0]
    def pair_send(tag, part):
        return _ici_start("grad_pair_start_" + tag, [part], [jax.ShapeDtypeStruct(part.shape, BF16)], _share_plan,
                          per_source=1)

    def pair_recv(tag, flight, after):
        send, recv, srcs, lands, _ = flight
        return _ici_wait("grad_pair_wait_" + tag, send, recv, srcs, lands, _share_plan, after)[1][0]

    def scatter_start(tag, sums, after=None):
        return _ici_start("grad_scatter_start_" + tag, sums,
                          [jax.ShapeDtypeStruct((3,) + p.shape[1:], BF16) for p in sums], _scatter_plan, after=after)

    def scatter_finish(tag, flight, after):
        send, recv, srcs, lands, _ = flight
        sums, received = _ici_wait("grad_scatter_wait_" + tag, send, recv, srcs, lands, _scatter_plan, after)
        return [_chip_add("chip_add_%s_%d" % (tag, k), chip_arr, p, r) for k, (p, r) in enumerate(zip(sums, received))]

    tm_g = _fit(MM_TM, D // 2)
    pair_down = pair_send("down", _grad_half("grad_w_down_a", core_arr, a, dy, N_CHIPS, 1, tm_g, True))
    pair_up = pair_send("up", _grad_half("grad_w_up_a", core_arr, h2, du, 1, N_CHIPS, tm_g, True, after=pair_down[4]))
    sum_down = _grad_half("grad_w_down_b", core_arr, a, dy, N_CHIPS, 1, tm_g, False,
                          recv=pair_recv("down", pair_down, pair_up[4]))
    sum_up = _grad_half("grad_w_up_b", core_arr, h2, du, 1, N_CHIPS, tm_g, False, recv=pair_recv("up", pair_up, sum_down))
    flight_mlp = scatter_start("mlp", [sum_up, sum_down])
    dh2 = _mm_plain("mlp_up_bwd", du, w_up_f, "nt", BF16, after=flight_mlp[4])
    dx1, dmix, acc_mid = _pre_mlp_and_post_mix_bwd(dh2, x1, dout, mix, _tie(g_pre_mlp, flight_mlp[4]), sc_m,
                                                   g_post_mix, gt_a)

    dmixcat = _mm_plain("out_proj_bwd", dmix, w_out_f, "nt", F32)

    fdq, fdk, fdv, dcum_row, dcum_q = _fox_bwd(proj, fox_o, dmixcat, fox_lse, cum_row, n_fox)
    dcum_k = jnp.pad(dcum_row.reshape(n_fox, S), ((0, LANES - n_fox), (0, 0)))
    dfg, db_forget = _fox_gate_bwd(dcum_k, dcum_q, fg, b_pad)

    group_w = (n_swa // n_kv) * HEAD_DIM
    sdq, sdk, sdv, dsink = _swa_bwd(rq, proj, v_first, sinks, swa_o, dmixcat, fox_w // group_w, swa_lse, n_swa, n_kv)
    drq = jnp.concatenate([sdq, jnp.transpose(sdk, (1, 0, 2)).reshape(S, kv_w).astype(BF16)], axis=1)
    d_sq_sk = _rope("rope_bwd", drq, 0, n_swa + n_kv, cos, -sin_signed)
    dsv = jnp.transpose(sdv, (1, 0, 2)).reshape(S, kv_w).astype(BF16)
    dproj = jnp.concatenate([fdq, fdk, fdv, d_sq_sk, dsv], axis=1)

    pieces = []
    for s in range(N_CHIPS):
        lo, hi = s * in_rows, (s + 1) * in_rows
        for src, first, last, shift in [(dproj, 0, gate_lo, 0), (dfg, gate_lo, gate_lo + n_fox, gate_lo),
                                        (dproj, gate_lo + n_fox, in_w, n_fox)]:
            if max(lo, first) < min(hi, last):
                pieces.append(src[:, max(lo, first) - shift:min(hi, last) - shift])
        pieces.append(jnp.zeros((S, in_rows_pad - in_rows), BF16))
    dproj_slab = jnp.concatenate(pieces, axis=1)

    tm_in, tm_out = in_rows_pad // 2, D // (2 * N_CHIPS)
    pair_in = pair_send("in", _grad_half("grad_w_in_a", core_arr, dproj_slab, h, N_CHIPS, 1, tm_in, True))
    pair_out = pair_send("out", _grad_half("grad_w_out_a", core_arr, mixcat, dmix, N_CHIPS, 1, tm_out, True,
                                           after=pair_in[4]))
    sum_in = _grad_half("grad_w_in_b", core_arr, dproj_slab, h, N_CHIPS, 1, tm_in, False,
                        recv=pair_recv("in", pair_in, pair_out[4]))
    sum_out = _grad_half("grad_w_out_b", core_arr, mixcat, dmix, N_CHIPS, 1, tm_out, False,
                         recv=pair_recv("out", pair_out, sum_in[0, :8, :LANES]))
    dh = _mm_plain("in_proj_bwd", dproj_slab, w_slab_t, "nn", BF16, tk=slab_w // 2,
                   after=sum_out[0, :8, :LANES].astype(F32))
    grad_x, acc_pre = _pre_mix_bwd(dh, x2, dx1, g_pre_mix, sc_a)

    zero_row = jnp.zeros((1, D), F32)
    tail = jnp.concatenate([db_forget[0:1, :n_fox], dsink[:, 0, :n_swa // n_kv].reshape(1, n_swa),
                            loss_part[0:1, 0:1], jnp.zeros((1, D - n_fox - n_swa - 1), F32)], axis=1)
    partial = jnp.concatenate([
        acc_pre[0:1], acc_pre[1:2], acc_mid[3:4], acc_mid[0:1], acc_mid[1:2], acc_mlp_post[0:1],
        acc_pre[2:3], acc_mid[4:5], acc_mid[2:3], acc_mlp_post[1:2], tail] + [zero_row] * 5, axis=0)
    gathered_small, token = _allgather8("gather_small_grads", partial)

    flight_mix = scatter_start("mix", [sum_in, sum_out], after=token)
    halves_mlp = scatter_finish("mlp", flight_mlp, flight_mix[4])
    share_up, share_down = [
        _ici_start("grad_share_start_" + n, [hv], [jax.ShapeDtypeStruct(hv.shape, F32)], _share_plan, per_source=1)
        for n, hv in zip(["up", "down"], halves_mlp)]

    def shared(tag, flight, after):
        send, recv, own, lands, _ = flight
        own, other = _ici_wait("grad_share_wait_" + tag, send, recv, own, lands, _share_plan, after)
        return own[0], other[0]

    def unpack(p):
        return {"b_mod": p[0:N_MOD].reshape(1, N_MOD * D), "g_pre_mix": p[6:7], "g_post_mix": p[7:8],
                "g_pre_mlp": p[8:9], "g_post_mlp": p[9:10], "b_forget": p[10:11, :n_fox],
                "swa_sinks": p[10:11, n_fox:n_fox + n_swa]}

    small_out = _small_update(gathered_small, _tie(small_state[0], share_down[4] + share_up[4]), small_state[1],
                              small_state[2])
    g_small, d_small, m_small, v_small = [unpack(p) for p in small_out]
    loss = small_out[0][N_MOD + 4, n_fox + n_swa]

    dmod_all = gathered_small.reshape(N_DEV, 16, D)[:, :N_MOD].reshape(N_DEV, N_MOD * D)
    dmod_shard = _tie(lax.dynamic_slice_in_dim(dmod_all, chip * mod_cols, mod_cols, axis=1), share_down[4])
    g_w_mod, d_w_mod, nm_w_mod, nv_w_mod = _mod_update(c_all.T, dmod_shard, w_mod[0], m_w_mod[0], v_w_mod[0])

    grads = dict(g_small, w_mod=g_w_mod[None])
    deltas = dict(d_small, w_mod=d_w_mod[None])
    new_m = dict(m_small, w_mod=nm_w_mod[None])
    new_v = dict(v_small, w_mod=nv_w_mod[None])
    weights = {"w_in": (w_in, m_w_in, v_w_in), "w_out": (w_out, m_w_out, v_w_out), "w_up": (w_up, m_w_up, v_w_up),
               "w_down": (w_down, m_w_down, v_w_down)}

    updated = {}

    def big_update(n, own, other):
        transposed = n == "w_in"
        w, m, v = in_state if transposed else [a[0] for a in weights[n]]
        outs = _adam_halves("adam_" + n, core_arr, w, own, other, m, v, out_rows=in_rows if transposed else None)
        updated[n] = outs[1][:8, :LANES]
        if transposed:
            outs = [o.T for o in outs]
        grads[n], deltas[n], new_m[n], new_v[n] = [o[None] for o in outs]

    big_update("w_down", *shared("down", share_down, d_w_mod[:8, :LANES] + small_out[1][:8, :LANES]))
    halves_mix = scatter_finish("mix", flight_mix, updated["w_down"] + d_w_mod[:8, :LANES])
    others_mix = _pair_share("grad_pair_share_mix", halves_mix)
    big_update("w_in", halves_mix[0], others_mix[0])
    big_update("w_out", halves_mix[1], others_mix[1])
    big_update("w_up", *shared("up", share_up, updated["w_out"] + updated["w_in"]))

    order = ["w_mod", "b_mod", "g_pre_mix", "g_post_mix", "w_in", "b_forget", "swa_sinks", "w_out", "g_pre_mlp",
             "g_post_mlp", "w_up", "w_down"]
    return (loss, grad_x[None], *[grads[n] for n in order], *[deltas[n] for n in order],
            *[new_m[n] for n in order], *[new_v[n] for n in order])
```

```python
import jax
import jax.numpy as jnp
from jax import lax
from jax.experimental import pallas as pl
from jax.experimental.pallas import tpu as pltpu

F32 = jnp.float32
BF16 = jnp.bfloat16
MESH = pl.DeviceIdType.MESH

HEAD_DIM = 128
SWA_BLOCK = 128
ROPE_THETA = 10000.0
NORM_EPS = 1e-6
NEG = -1e30
N_MOD = 6
ADAM_LR = 0.001
ADAM_B1 = 0.9
ADAM_B2 = 0.999
ADAM_EPS = 1e-08
ADAM_WD = 0.01
ADAM_STEP = 10
N_CHIPS = 4
N_DEV = 8
LANES = 128
VMEM_CAP = 60 * 1024 * 1024

_NN = (((1,), (0,)), ((), ()))
_NT = (((1,), (1,)), ((), ()))
_TN = (((0,), (0,)), ((), ()))


def _vmem(nbytes):
    return int(min(VMEM_CAP, nbytes * 5 // 4 + (4 << 20)))


def _nbytes(shape, dtype):
    n = 1
    for s in shape:
        n *= s
    return n * jnp.dtype(dtype).itemsize


def _fit(t, n):
    t = min(t, n)
    assert n % t == 0, (t, n)
    return t


MM_TM, MM_TN, MM_TK = 1024, 1024, 2048


def _matmul(name, a, b, mode, out_defs, epilogue, extras=(), tm=MM_TM, tn=MM_TN, tk=MM_TK, revisits=False,
            row_sel=None):
    stacked = b.ndim == 3
    b_rows, b_cols = b.shape[-2], b.shape[-1] * (b.shape[0] if stacked else 1)
    if mode == "nn":
        (M, K), (K2, N) = a.shape, (b_rows, b_cols)
    elif mode == "nt":
        (M, K), (N, K2) = a.shape, (b_rows, b_cols)
    else:
        (K, M), (K2, N) = a.shape, (b_rows, b_cols)
    assert K == K2 and not (stacked and mode == "tn"), (a.shape, b.shape, mode)
    tm = _fit(tm, M)
    tn = _fit(tn, b.shape[-1] if stacked and mode == "nn" else N)
    tk = _fit(tk, b.shape[-1] if stacked and mode == "nt" else K)
    nk = K // tk
    dims = {"nn": _NN, "nt": _NT, "tn": _TN}[mode]
    if row_sel is None:
        grid_m, a_row = M // tm, lambda i, *sel: i
    else:
        grid_m, a_row = row_sel[2], lambda i, *sel: row_sel[1](i, sel[0])
    a_spec = (pl.BlockSpec((tk, tm), lambda i, j, k, *sel: (k, a_row(i, *sel))) if mode == "tn"
              else pl.BlockSpec((tm, tk), lambda i, j, k, *sel: (a_row(i, *sel), k)))
    if stacked:
        per = b.shape[-1] // (tk if mode == "nt" else tn)
        b_spec = (pl.BlockSpec((1, tn, tk), lambda i, j, k, *sel: (k // per, j, k % per)) if mode == "nt"
                  else pl.BlockSpec((1, tk, tn), lambda i, j, k, *sel: (j // per, k, j % per)))
    else:
        b_spec = (pl.BlockSpec((tn, tk), lambda i, j, k, *sel: (j, k)) if mode == "nt"
                  else pl.BlockSpec((tk, tn), lambda i, j, k, *sel: (k, j)))
    n_ex, n_out = len(extras), len(out_defs)

    def body(*refs):
        if row_sel is not None:
            refs = refs[1:]
        a_ref, b_ref = refs[0], refs[1]
        ex = refs[2:2 + n_ex]
        outs = refs[2 + n_ex:2 + n_ex + n_out]
        b_blk = b_ref[0] if stacked else b_ref[...]
        prod = lax.dot_general(a_ref[...], b_blk, dims, preferred_element_type=F32)
        if nk == 1:
            epilogue(prod, ex, outs)
        else:
            acc_ref = refs[-1]
            k = pl.program_id(2)

            @pl.when(k == 0)
            def _():
                acc_ref[...] = prod

            @pl.when(k > 0)
            def _():
                acc_ref[...] += prod

            @pl.when(k == nk - 1)
            def _():
                epilogue(acc_ref[...], ex, outs)

    def wrap(f):
        return lambda i, j, k, *sel: f(i, j)

    in_specs = [a_spec, b_spec] + [pl.BlockSpec(blk, wrap(f)) for _, blk, f in extras]
    out_specs = [pl.BlockSpec(blk, wrap(f)) for _, _, blk, f in out_defs]
    out_shape = [jax.ShapeDtypeStruct(s, d) for s, d, _, _ in out_defs]
    need = 2 * (tm * tk + tk * tn) * a.dtype.itemsize + 3 * tm * tn * 4
    need += sum(2 * _nbytes(blk, arr.dtype) for arr, blk, _ in extras)
    need += sum(2 * _nbytes(blk, d) for _, d, blk, _ in out_defs)
    grid = (grid_m, N // tn, nk)
    scratch = [pltpu.VMEM((tm, tn), F32)] if nk > 1 else []
    params = pltpu.CompilerParams(
        dimension_semantics=("parallel", "arbitrary" if revisits else "parallel", "arbitrary"),
        vmem_limit_bytes=_vmem(need))
    operands = (a, b, *[arr for arr, _, _ in extras])
    if row_sel is None:
        return pl.pallas_call(body, name=name, grid=grid, in_specs=in_specs, out_specs=out_specs, out_shape=out_shape,
                              scratch_shapes=scratch, compiler_params=params)(*operands)
    grid_spec = pltpu.PrefetchScalarGridSpec(num_scalar_prefetch=1, grid=grid, in_specs=in_specs, out_specs=out_specs,
                                             scratch_shapes=scratch)
    return pl.pallas_call(body, name=name, grid_spec=grid_spec, out_shape=out_shape,
                          compiler_params=params)(row_sel[0], *operands)


def _grad_half(name, core, a, b, row_slabs, col_slabs, tm, other, recv=None, after=None):
    (_, M), (_, N) = a.shape, b.shape
    H = M // (2 * row_slabs)
    nh = H // tm
    tn = _fit(MM_TN, N // col_slabs)
    per = N // col_slabs // tn

    def a_block(i, core_ref):
        half = (1 - core_ref[0]) if other else core_ref[0]
        return (i // nh) * (2 * nh) + half * nh + i % nh

    def out_index(i, j):
        return (j // per, i, j % per) if col_slabs > 1 else (i // nh, i % nh, j)

    slabs = max(row_slabs, col_slabs)
    out_def = ((slabs, H, N // col_slabs), BF16, (1, tm, tn), out_index)

    def epilogue(acc, ex, outs):
        outs[0][0] = (acc if recv is None else acc + ex[0][0].astype(F32)).astype(BF16)

    extras = ([] if recv is None else [(recv, (1, tm, tn), out_index)]) + ([] if after is None else [_behind(after)])
    return _matmul(name, a, b, "tn", [out_def], epilogue, extras=extras, tm=tm, tn=tn,
                   row_sel=(core, a_block, row_slabs * nh))[0]


def _behind(token):
    return (token, (8, LANES), lambda i, j: (0, 0))


def _mm_plain(name, a, b, mode, out_dtype, after=None, **tiles):
    if mode == "nn":
        M, N = a.shape[0], b.shape[-1] * (b.shape[0] if b.ndim == 3 else 1)
    elif mode == "nt":
        M, N = a.shape[0], b.shape[-2]
    else:
        M, N = a.shape[1], b.shape[1]
    tm, tn = _fit(tiles.get("tm", MM_TM), M), _fit(tiles.get("tn", MM_TN), N)

    def epi(acc, ex, outs):
        outs[0][...] = acc.astype(out_dtype)

    return _matmul(name, a, b, mode, [((M, N), out_dtype, (tm, tn), lambda i, j: (i, j))], epi,
                   extras=[] if after is None else [_behind(after)], **tiles)[0]


def _rstd(v):
    return lax.rsqrt(jnp.mean(v * v, axis=-1, keepdims=True) + NORM_EPS)


ROW_TILE = 256


def _row_call(name, body, row_ins, vec_ins, row_outs, acc_outs, S, D):
    tr = _fit(ROW_TILE, S)
    row_spec = pl.BlockSpec((tr, D), lambda r: (r, 0))
    vec_spec = pl.BlockSpec((1, D), lambda r: (0, 0))
    in_specs = [row_spec] * len(row_ins) + [vec_spec] * len(vec_ins)
    out_specs = [row_spec] * len(row_outs) + [pl.BlockSpec(shp, lambda r: (0, 0)) for shp in acc_outs]
    out_shape = [jax.ShapeDtypeStruct((S, D), d) for d in row_outs] + [jax.ShapeDtypeStruct(shp, F32) for shp in acc_outs]
    need = sum(2 * tr * D * a.dtype.itemsize for a in row_ins) + sum(2 * tr * D * jnp.dtype(d).itemsize for d in row_outs)
    need += 8 * tr * D * 4
    return pl.pallas_call(
        body, name=name, grid=(S // tr,), in_specs=in_specs, out_specs=out_specs, out_shape=out_shape,
        compiler_params=pltpu.CompilerParams(dimension_semantics=("arbitrary",), vmem_limit_bytes=_vmem(need)),
    )(*row_ins, *vec_ins)


def _acc_rows(ref, rows):
    @pl.when(pl.program_id(0) == 0)
    def _():
        ref[...] = jnp.zeros_like(ref)
    for n, r in enumerate(rows):
        ref[n:n + 1, :] += r


def _pre_norm(x, g, sc, sh):
    S, D = x.shape

    def body(x_ref, g_ref, sc_ref, sh_ref, h_ref):
        xv = x_ref[...]
        xn = xv * _rstd(xv)
        h_ref[...] = (xn * g_ref[...] * (1.0 + sc_ref[...]) + sh_ref[...]).astype(BF16)

    return _row_call("pre_norm_mix", body, [x], [g, sc, sh], [BF16], [], S, D)[0]


def _post_mix(x, mix, g_post, gt, g_pre, sc, sh):
    S, D = x.shape

    def body(x_ref, mix_ref, gp_ref, gt_ref, g2_ref, sc_ref, sh_ref, x1_ref, h2_ref):
        mv = mix_ref[...].astype(F32)
        x1 = x_ref[...] + gt_ref[...] * (mv * _rstd(mv) * gp_ref[...])
        x1_ref[...] = x1
        h2_ref[...] = (x1 * _rstd(x1) * g2_ref[...] * (1.0 + sc_ref[...]) + sh_ref[...]).astype(BF16)

    return _row_call("post_mix_pre_mlp", body, [x, mix], [g_post, gt, g_pre, sc, sh], [F32, BF16], [], S, D)


def _loss_and_post_mlp_bwd(x1, y, target, g_post, gt):
    S, D = x1.shape

    def body(x1_ref, y_ref, t_ref, g_ref, gt_ref, dy_ref, dout_ref, loss_ref, acc_ref):
        yv = y_ref[...].astype(F32)
        r = _rstd(yv)
        yh = yv * r
        n = yh * g_ref[...]
        diff = x1_ref[...] + gt_ref[...] * n - t_ref[...]
        dout = diff * (1.0 / D)
        dout_ref[...] = dout
        dn = dout * gt_ref[...]
        dyh = dn * g_ref[...]
        dy_ref[...] = (r * (dyh - yh * jnp.mean(dyh * yh, axis=-1, keepdims=True))).astype(BF16)
        _acc_rows(acc_ref, [jnp.sum(dout * n, axis=0, keepdims=True), jnp.sum(dn * yh, axis=0, keepdims=True)])

        @pl.when(pl.program_id(0) == 0)
        def _():
            loss_ref[...] = jnp.zeros_like(loss_ref)
        loss_ref[...] += jnp.full(loss_ref.shape, (0.5 / D) * jnp.sum(diff * diff), F32)

    return _row_call("loss_post_mlp_bwd", body, [x1, y, target], [g_post, gt], [BF16, F32],
                     [(8, LANES), (8, D)], S, D)


def _pre_mlp_and_post_mix_bwd(dh2, x1, dout, mix, g_pre, sc, g_post, gt):
    S, D = x1.shape

    def body(dh_ref, x1_ref, dout_ref, mix_ref, g_ref, sc_ref, gp_ref, gt_ref, dx1_ref, dmix_ref, acc_ref):
        dh = dh_ref[...].astype(F32)
        x1v = x1_ref[...]
        r3 = _rstd(x1v)
        xn = x1v * r3
        dxn = dh * (1.0 + sc_ref[...]) * g_ref[...]
        dx1 = dout_ref[...] + r3 * (dxn - xn * jnp.mean(dxn * xn, axis=-1, keepdims=True))
        dx1_ref[...] = dx1
        mv = mix_ref[...].astype(F32)
        r2 = _rstd(mv)
        mh = mv * r2
        dn = dx1 * gt_ref[...]
        dmh = dn * gp_ref[...]
        dmix_ref[...] = (r2 * (dmh - mh * jnp.mean(dmh * mh, axis=-1, keepdims=True))).astype(BF16)
        _acc_rows(acc_ref, [
            jnp.sum(dh, axis=0, keepdims=True),
            jnp.sum(dh * xn * g_ref[...], axis=0, keepdims=True),
            jnp.sum(dh * (1.0 + sc_ref[...]) * xn, axis=0, keepdims=True),
            jnp.sum(dx1 * mh * gp_ref[...], axis=0, keepdims=True),
            jnp.sum(dn * mh, axis=0, keepdims=True)])

    return _row_call("pre_mlp_post_mix_bwd", body, [dh2, x1, dout, mix], [g_pre, sc, g_post, gt], [F32, BF16],
                     [(8, D)], S, D)


def _pre_mix_bwd(dh, x, dx1, g_pre, sc):
    S, D = x.shape

    def body(dh_ref, x_ref, dx1_ref, g_ref, sc_ref, gx_ref, acc_ref):
        dhv = dh_ref[...].astype(F32)
        xv = x_ref[...]
        r = _rstd(xv)
        xn = xv * r
        dxn = dhv * (1.0 + sc_ref[...]) * g_ref[...]
        gx_ref[...] = dx1_ref[...] + r * (dxn - xn * jnp.mean(dxn * xn, axis=-1, keepdims=True))
        _acc_rows(acc_ref, [
            jnp.sum(dhv, axis=0, keepdims=True),
            jnp.sum(dhv * xn * g_ref[...], axis=0, keepdims=True),
            jnp.sum(dhv * (1.0 + sc_ref[...]) * xn, axis=0, keepdims=True)])

    return _row_call("pre_mix_bwd", body, [dh, x, dx1], [g_pre, sc], [F32], [(8, D)], S, D)


CUM_BLOCK = 256


def _tri(n, upper):
    r = lax.broadcasted_iota(jnp.int32, (n, n), 0)
    c = lax.broadcasted_iota(jnp.int32, (n, n), 1)
    return ((c >= r) if upper else (c <= r)).astype(F32)


def _fox_gate_fwd(fg, b_pad):
    S = fg.shape[0]
    cb = _fit(CUM_BLOCK, S)

    def body(fg_ref, b_ref, cumt_ref, cum_ref):
        low = _tri(cb, False)
        carry = jnp.zeros((1, LANES), F32)
        for n in range(S // cb):
            z = fg_ref[n * cb:(n + 1) * cb, :] + b_ref[...]
            logf = jnp.minimum(z, 0.0) - jnp.log(1.0 + jnp.exp(-jnp.abs(z)))
            blk = jnp.dot(low, logf, precision=lax.Precision.HIGHEST, preferred_element_type=F32) + carry
            cum_ref[n * cb:(n + 1) * cb, :] = blk
            carry = blk[cb - 1:cb, :]
        cumt_ref[...] = cum_ref[...].T

    return pl.pallas_call(
        body, name="fox_gate_fwd", out_shape=jax.ShapeDtypeStruct((LANES, S), F32),
        scratch_shapes=[pltpu.VMEM((S, LANES), F32)],
        compiler_params=pltpu.CompilerParams(vmem_limit_bytes=_vmem(6 * S * LANES * 4)),
    )(fg, b_pad)


def _fox_gate_bwd(dcum_k, dcum_q, fg, b_pad):
    S = fg.shape[0]
    n_fox = dcum_q.shape[0]
    cb = _fit(CUM_BLOCK, S)

    def body(dk_ref, dq_ref, fg_ref, b_ref, dfg_ref, db_ref, dc_ref):
        lane = lax.broadcasted_iota(jnp.int32, (S, LANES), 1)
        dc = dk_ref[...].T
        for h in range(n_fox):
            dc = dc + jnp.where(lane == h, dq_ref[h], 0.0)
        dc_ref[...] = dc
        up = _tri(cb, True)
        carry = jnp.zeros((1, LANES), F32)
        db = jnp.zeros((1, LANES), F32)
        for n in reversed(range(S // cb)):
            blk = jnp.dot(up, dc_ref[n * cb:(n + 1) * cb, :], precision=lax.Precision.HIGHEST,
                          preferred_element_type=F32) + carry
            carry = blk[0:1, :]
            z = fg_ref[n * cb:(n + 1) * cb, :] + b_ref[...]
            dfg = blk * (1.0 / (1.0 + jnp.exp(z)))
            dfg_ref[n * cb:(n + 1) * cb, :] = dfg.astype(BF16)
            db = db + jnp.sum(dfg, axis=0, keepdims=True)
        db_ref[...] = jnp.broadcast_to(db, db_ref.shape)

    return pl.pallas_call(
        body, name="fox_gate_bwd",
        out_shape=[jax.ShapeDtypeStruct((S, LANES), BF16), jax.ShapeDtypeStruct((8, LANES), F32)],
        scratch_shapes=[pltpu.VMEM((S, LANES), F32)],
        compiler_params=pltpu.CompilerParams(vmem_limit_bytes=_vmem((8 + 2 * n_fox) * S * LANES * 4)),
    )(dcum_k, dcum_q, fg, b_pad)


FOX_TILE = 512


LOG2E = 1.4426950408889634


def _fox_scores(q, k, ck2, masked, t):
    s = lax.dot_general(q, k, _NT, preferred_element_type=F32) * (HEAD_DIM ** -0.5 * LOG2E) - ck2
    if masked:
        row = lax.broadcasted_iota(jnp.int32, (t, t), 0)
        col = lax.broadcasted_iota(jnp.int32, (t, t), 1)
        s = jnp.where(col <= row, s, NEG)
    return s


def _fox_fwd(proj, cum_row, n_fox):
    S = proj.shape[0]
    t = _fit(FOX_TILE, S)
    nq = S // t

    def body(q_ref, k_ref, v_ref, ck_ref, o_ref, lse_ref):
        def q_block(qi, _):
            q0 = pl.multiple_of(qi * t, t)
            q = q_ref[pl.ds(q0, t), :]

            def kv_block(j, carry, masked):
                m, l, acc = carry
                k0 = pl.multiple_of(j * t, t)
                s = _fox_scores(q, k_ref[pl.ds(k0, t), :], ck_ref[0, :, pl.ds(k0, t)] * LOG2E, masked, t)
                m_new = jnp.maximum(m, jnp.max(s, axis=-1, keepdims=True))
                alpha = jnp.exp2(m - m_new)
                p = jnp.exp2(s - m_new)
                l = alpha * l + jnp.sum(p, axis=-1, keepdims=True)
                acc = alpha * acc + jnp.dot(p.astype(BF16), v_ref[pl.ds(k0, t), :], preferred_element_type=F32)
                return m_new, l, acc

            init = (jnp.full((t, 1), NEG, F32), jnp.zeros((t, 1), F32), jnp.zeros((t, HEAD_DIM), F32))
            carry = lax.fori_loop(0, qi, lambda j, cr: kv_block(j, cr, False), init)
            m, l, acc = kv_block(qi, carry, True)
            o_ref[pl.ds(q0, t), :] = acc / l
            lse_ref[0, pl.ds(q0, t), :] = jnp.broadcast_to(m + jnp.log(l) * LOG2E, (t, LANES))
            return 0

        lax.fori_loop(0, nq, q_block, 0)

    col = lambda off: pl.BlockSpec((S, HEAD_DIM), lambda h: (0, off + h))
    per_head = pl.BlockSpec((1, S, LANES), lambda h: (h, 0, 0))
    return pl.pallas_call(
        body, name="fox_fwd", grid=(n_fox,),
        in_specs=[col(0), col(n_fox), col(2 * n_fox), pl.BlockSpec((1, 1, S), lambda h: (h, 0, 0))],
        out_specs=[pl.BlockSpec((S, HEAD_DIM), lambda h: (0, h)), per_head],
        out_shape=[jax.ShapeDtypeStruct((S, n_fox * HEAD_DIM), F32), jax.ShapeDtypeStruct((n_fox, S, LANES), F32)],
        compiler_params=pltpu.CompilerParams(dimension_semantics=("parallel",),
                                             vmem_limit_bytes=_vmem(16 * S * HEAD_DIM * 4 + 12 * t * t * 4)),
    )(proj, proj, proj, cum_row)


def _fox_bwd(proj, o, do, lse_b, cum_row, n_fox):
    S = proj.shape[0]
    t = _fit(FOX_TILE, S)
    nq = S // t
    scale = HEAD_DIM ** -0.5

    def body(q_ref, k_ref, v_ref, o_ref, do_ref, lse_ref, ck_ref, dq_ref, dk_ref, dv_ref, dc_ref, dcq_ref,
             dq_acc, delta_ref):
        dq_acc[...] = jnp.zeros_like(dq_acc)
        dcq_ref[...] = jnp.zeros_like(dcq_ref)

        def delta_block(qi, _):
            q0 = pl.multiple_of(qi * t, t)
            d = jnp.sum(do_ref[pl.ds(q0, t), :] * o_ref[pl.ds(q0, t), :], axis=-1, keepdims=True)
            delta_ref[pl.ds(q0, t), :] = jnp.broadcast_to(d, (t, LANES))
            return 0

        lax.fori_loop(0, nq, delta_block, 0)

        def kv_block(j, _):
            k0 = pl.multiple_of(j * t, t)
            k = k_ref[pl.ds(k0, t), :]
            v = v_ref[pl.ds(k0, t), :]
            ck2 = ck_ref[0, :, pl.ds(k0, t)] * LOG2E

            def q_block(qi, carry, masked):
                dk, dv, dc = carry
                q0 = pl.multiple_of(qi * t, t)
                q = q_ref[pl.ds(q0, t), :]
                dov = do_ref[pl.ds(q0, t), :].astype(BF16)
                p = jnp.exp2(_fox_scores(q, k, ck2, masked, t) - lse_ref[0, pl.ds(q0, t), :][:, :1])
                dp = lax.dot_general(dov, v, _NT, preferred_element_type=F32)
                ds = p * (dp - delta_ref[pl.ds(q0, t), :][:, :1])
                dsb = ds.astype(BF16)
                dv = dv + lax.dot_general(p.astype(BF16), dov, _TN, preferred_element_type=F32)
                dk = dk + lax.dot_general(dsb, q, _TN, preferred_element_type=F32)
                dq_acc[pl.ds(q0, t), :] += jnp.dot(dsb, k, preferred_element_type=F32)
                dc = dc - jnp.sum(ds, axis=0, keepdims=True)
                dcq_ref[0, pl.ds(q0, t), :] += jnp.broadcast_to(jnp.sum(ds, axis=1, keepdims=True), (t, LANES))
                return dk, dv, dc

            init = (jnp.zeros((t, HEAD_DIM), F32), jnp.zeros((t, HEAD_DIM), F32), jnp.zeros((1, t), F32))
            carry = q_block(j, init, True)
            dk, dv, dc = lax.fori_loop(j + 1, nq, lambda qi, cr: q_block(qi, cr, False), carry)
            dk_ref[pl.ds(k0, t), :] = (dk * scale).astype(BF16)
            dv_ref[pl.ds(k0, t), :] = dv.astype(BF16)
            dc_ref[0, :, pl.ds(k0, t)] = dc
            return 0

        lax.fori_loop(0, nq, kv_block, 0)
        dq_ref[...] = (dq_acc[...] * scale).astype(BF16)

    col = lambda off: pl.BlockSpec((S, HEAD_DIM), lambda h: (0, off + h))
    per_head = pl.BlockSpec((1, S, LANES), lambda h: (h, 0, 0))
    row = pl.BlockSpec((1, 1, S), lambda h: (h, 0, 0))
    grad = jax.ShapeDtypeStruct((S, n_fox * HEAD_DIM), BF16)
    return pl.pallas_call(
        body, name="fox_bwd", grid=(n_fox,),
        in_specs=[col(0), col(n_fox), col(2 * n_fox), col(0), col(0), per_head, row],
        out_specs=[col(0), col(0), col(0), row, per_head],
        out_shape=[grad, grad, grad, jax.ShapeDtypeStruct((n_fox, 1, S), F32), jax.ShapeDtypeStruct((n_fox, S, LANES), F32)],
        scratch_shapes=[pltpu.VMEM((S, HEAD_DIM), F32), pltpu.VMEM((S, LANES), F32)],
        compiler_params=pltpu.CompilerParams(dimension_semantics=("parallel",),
                                             vmem_limit_bytes=_vmem(24 * S * HEAD_DIM * 4 + 16 * t * t * 4)),
    )(proj, proj, proj, o, do, lse_b, cum_row)


def _rope_tables(S):
    half = HEAD_DIM // 2
    inv_freq = 1.0 / (ROPE_THETA ** (jnp.arange(half, dtype=F32) * (2.0 / HEAD_DIM)))
    ang = jnp.arange(S).astype(F32)[:, None] * inv_freq[None, :]
    cos, sin = jnp.cos(ang), jnp.sin(ang)
    return jnp.concatenate([cos, cos], axis=-1), jnp.concatenate([-sin, sin], axis=-1)


def _rope(name, src, first_block, n_blocks, cos, sin_signed):
    S = src.shape[0]

    def body(x_ref, cos_ref, sin_ref, o_ref):
        xv = x_ref[...].astype(F32)
        o_ref[...] = (xv * cos_ref[...] + pltpu.roll(xv, HEAD_DIM // 2, 1) * sin_ref[...]).astype(BF16)

    table = pl.BlockSpec((S, HEAD_DIM), lambda n: (0, 0))
    return pl.pallas_call(
        body, name=name, grid=(n_blocks,),
        in_specs=[pl.BlockSpec((S, HEAD_DIM), lambda n: (0, first_block + n)), table, table],
        out_specs=pl.BlockSpec((S, HEAD_DIM), lambda n: (0, n)),
        out_shape=jax.ShapeDtypeStruct((S, n_blocks * HEAD_DIM), BF16),
        compiler_params=pltpu.CompilerParams(dimension_semantics=("parallel",),
                                             vmem_limit_bytes=_vmem(12 * S * HEAD_DIM * 4)),
    )(src, cos, sin_signed)


def _swa_tile(q_ref, kp_ref, kc_ref, n, group, scale):
    B = SWA_BLOCK
    qs = jnp.concatenate([q_ref[:, g * HEAD_DIM:(g + 1) * HEAD_DIM] for g in range(group)], axis=0)
    kcat = jnp.concatenate([kp_ref[...], kc_ref[...]], axis=0)
    s = lax.dot_general(qs, kcat, _NT, preferred_element_type=F32) * scale
    qi = lax.broadcasted_iota(jnp.int32, (group * B, 2 * B), 0) % B
    kj = lax.broadcasted_iota(jnp.int32, (group * B, 2 * B), 1)
    diff = qi + B - kj
    mask = (diff >= 0) & (diff < B) & ((n * B + kj - B) >= 0)
    return qs, kcat, jnp.where(mask, s, NEG)


def _swa_sink_col(sink_ref, kv, group):
    head = lax.broadcasted_iota(jnp.int32, (group * SWA_BLOCK, 1), 0) // SWA_BLOCK
    col = jnp.zeros((group * SWA_BLOCK, 1), F32)
    for g in range(group):
        col = jnp.where(head == g, sink_ref[kv * group + g], col)
    return col


def _swa_specs(n_kv, group, q_first, k_first, v_first):
    B = SWA_BLOCK
    prev = lambda n: jnp.maximum(n - 1, 0)
    return [
        pl.BlockSpec((B, group * HEAD_DIM), lambda kv, n: (n, q_first + kv)),
        pl.BlockSpec((B, HEAD_DIM), lambda kv, n: (prev(n), k_first + kv)),
        pl.BlockSpec((B, HEAD_DIM), lambda kv, n: (n, k_first + kv)),
        pl.BlockSpec((B, HEAD_DIM), lambda kv, n: (prev(n), v_first + kv)),
        pl.BlockSpec((B, HEAD_DIM), lambda kv, n: (n, v_first + kv)),
    ]


def _swa_fwd(rq, proj, v_first, sinks, n_q, n_kv):
    S = rq.shape[0]
    B = SWA_BLOCK
    group = n_q // n_kv
    scale = HEAD_DIM ** -0.5

    def body(q_ref, kp_ref, kc_ref, vp_ref, vc_ref, sink_ref, o_ref, lse_ref):
        kv, n = pl.program_id(0), pl.program_id(1)
        _, _, s = _swa_tile(q_ref, kp_ref, kc_ref, n, group, scale)
        sink = _swa_sink_col(sink_ref, kv, group)
        m = jnp.maximum(jnp.max(s, axis=-1, keepdims=True), sink)
        p = jnp.exp(s - m)
        denom = jnp.sum(p, axis=-1, keepdims=True) + jnp.exp(sink - m)
        vcat = jnp.concatenate([vp_ref[...], vc_ref[...]], axis=0)
        o = jnp.dot((p / denom).astype(BF16), vcat, preferred_element_type=F32)
        lse = m + jnp.log(denom)
        for g in range(group):
            o_ref[:, g * HEAD_DIM:(g + 1) * HEAD_DIM] = o[g * B:(g + 1) * B, :]
            lse_ref[0, :, g * LANES:(g + 1) * LANES] = jnp.broadcast_to(lse[g * B:(g + 1) * B, :], (B, LANES))

    specs = _swa_specs(n_kv, group, 0, n_q, v_first)
    q_blk = pl.BlockSpec((B, group * HEAD_DIM), lambda kv, n: (n, kv))
    return pl.pallas_call(
        body, name="swa_fwd", grid=(n_kv, S // B),
        in_specs=specs + [pl.BlockSpec(memory_space=pltpu.SMEM)],
        out_specs=[q_blk, pl.BlockSpec((1, B, group * LANES), lambda kv, n: (kv, n, 0))],
        out_shape=[jax.ShapeDtypeStruct((S, n_q * HEAD_DIM), F32), jax.ShapeDtypeStruct((n_kv, S, group * LANES), F32)],
        compiler_params=pltpu.CompilerParams(dimension_semantics=("parallel", "arbitrary")),
    )(rq, rq, rq, proj, proj, sinks)


def _swa_bwd(rq, proj, v_first, sinks, o, do, do_first, lse_b, n_q, n_kv):
    S = rq.shape[0]
    B = SWA_BLOCK
    group = n_q // n_kv
    scale = HEAD_DIM ** -0.5

    def body(q_ref, kp_ref, kc_ref, vp_ref, vc_ref, o_ref, do_ref, lse_ref, sink_ref,
             dq_ref, dk_ref, dv_ref, dsink_ref):
        kv, n = pl.program_id(0), pl.program_id(1)

        @pl.when(n == 0)
        def _():
            dk_ref[...] = jnp.zeros_like(dk_ref)
            dv_ref[...] = jnp.zeros_like(dv_ref)
            dsink_ref[...] = jnp.zeros_like(dsink_ref)

        qs, kcat, s = _swa_tile(q_ref, kp_ref, kc_ref, n, group, scale)
        sink = _swa_sink_col(sink_ref, kv, group)
        stack = lambda ref, w: jnp.concatenate([ref[:, g * w:(g + 1) * w] for g in range(group)], axis=0)
        lse = jnp.concatenate([lse_ref[0, :, g * LANES:g * LANES + 1] for g in range(group)], axis=0)
        do32 = stack(do_ref, HEAD_DIM)
        delta = jnp.sum(do32 * stack(o_ref, HEAD_DIM), axis=-1, keepdims=True)
        dov = do32.astype(BF16)
        p = jnp.exp(s - lse)
        vcat = jnp.concatenate([vp_ref[...], vc_ref[...]], axis=0)
        dp = lax.dot_general(dov, vcat, _NT, preferred_element_type=F32)
        ds = p * (dp - delta)
        dsb = ds.astype(BF16)
        dq = jnp.dot(dsb, kcat, preferred_element_type=F32) * scale
        for g in range(group):
            dq_ref[:, g * HEAD_DIM:(g + 1) * HEAD_DIM] = dq[g * B:(g + 1) * B, :].astype(BF16)
        dkcat = lax.dot_general(dsb, qs, _TN, preferred_element_type=F32) * scale
        dvcat = lax.dot_general(p.astype(BF16), dov, _TN, preferred_element_type=F32)
        prev0 = pl.multiple_of(jnp.maximum(n - 1, 0) * B, B)
        cur0 = pl.multiple_of(n * B, B)
        dk_ref[0, pl.ds(prev0, B), :] += dkcat[:B, :]
        dk_ref[0, pl.ds(cur0, B), :] += dkcat[B:, :]
        dv_ref[0, pl.ds(prev0, B), :] += dvcat[:B, :]
        dv_ref[0, pl.ds(cur0, B), :] += dvcat[B:, :]
        dsk = -jnp.exp(sink - lse) * delta
        lane = lax.broadcasted_iota(jnp.int32, (1, LANES), 1)
        row = jnp.zeros((1, LANES), F32)
        for g in range(group):
            row = row + jnp.where(lane == g, jnp.sum(dsk[g * B:(g + 1) * B, :]), 0.0)
        dsink_ref[0, 0:1, :] += row

    specs = _swa_specs(n_kv, group, 0, n_q, v_first)
    q_blk = pl.BlockSpec((B, group * HEAD_DIM), lambda kv, n: (n, kv))
    acc = pl.BlockSpec((1, S, HEAD_DIM), lambda kv, n: (kv, 0, 0))
    return pl.pallas_call(
        body, name="swa_bwd", grid=(n_kv, S // B),
        in_specs=specs + [q_blk, pl.BlockSpec((B, group * HEAD_DIM), lambda kv, n: (n, do_first + kv)),
                          pl.BlockSpec((1, B, group * LANES), lambda kv, n: (kv, n, 0)),
                          pl.BlockSpec(memory_space=pltpu.SMEM)],
        out_specs=[q_blk, acc, acc, pl.BlockSpec((1, 8, LANES), lambda kv, n: (kv, 0, 0))],
        out_shape=[jax.ShapeDtypeStruct((S, n_q * HEAD_DIM), BF16), jax.ShapeDtypeStruct((n_kv, S, HEAD_DIM), F32),
                   jax.ShapeDtypeStruct((n_kv, S, HEAD_DIM), F32), jax.ShapeDtypeStruct((n_kv, 8, LANES), F32)],
        compiler_params=pltpu.CompilerParams(dimension_semantics=("parallel", "arbitrary")),
    )(rq, rq, rq, proj, proj, o, do, lse_b, sinks)


def _adamw(w, g, m, v):
    m = ADAM_B1 * m + (1.0 - ADAM_B1) * g
    v = ADAM_B2 * v + (1.0 - ADAM_B2) * (g * g)
    m_hat = m / (1.0 - ADAM_B1 ** ADAM_STEP)
    v_hat = v / (1.0 - ADAM_B2 ** ADAM_STEP)
    delta = -ADAM_LR * (m_hat / (jnp.sqrt(v_hat) + ADAM_EPS) + ADAM_WD * w)
    return delta, m, v


def _mod_fwd(cond_in, w_mod, b_shard):
    R, D = cond_in.shape
    cols = w_mod.shape[1]
    tn = _fit(512, cols)

    def body(c_ref, w_ref, b_ref, o_ref):
        cv = c_ref[...]
        cond = (cv / (1.0 + jnp.exp(-cv))).astype(BF16)
        o_ref[...] = jnp.dot(cond, w_ref[...].astype(BF16), preferred_element_type=F32) + b_ref[...]

    return pl.pallas_call(
        body, name="mod_fwd", grid=(cols // tn,),
        in_specs=[pl.BlockSpec((R, D), lambda j: (0, 0)), pl.BlockSpec((D, tn), lambda j: (0, j)),
                  pl.BlockSpec((1, tn), lambda j: (0, j))],
        out_specs=pl.BlockSpec((R, tn), lambda j: (0, j)),
        out_shape=jax.ShapeDtypeStruct((R, cols), F32),
        compiler_params=pltpu.CompilerParams(dimension_semantics=("parallel",), vmem_limit_bytes=_vmem(3 * D * tn * 4)),
    )(cond_in, w_mod, b_shard)


def _mod_update(c_t, dmod, w, m, v):
    D, nb = c_t.shape
    cols = w.shape[1]
    tr = _fit(128, D)

    def body(c_ref, d_ref, w_ref, m_ref, v_ref, g_ref, dl_ref, nm_ref, nv_ref):
        cv = c_ref[...]
        cond = cv / (1.0 + jnp.exp(-cv))
        g = jnp.zeros((tr, cols), F32)
        for b in range(nb):
            g = g + cond[:, b:b + 1] * d_ref[b:b + 1, :]
        g_ref[...] = g
        dl_ref[...], nm_ref[...], nv_ref[...] = _adamw(w_ref[...], g, m_ref[...], v_ref[...])

    blk = pl.BlockSpec((tr, cols), lambda r: (r, 0))
    out = jax.ShapeDtypeStruct((D, cols), F32)
    return pl.pallas_call(
        body, name="mod_update", grid=(D // tr,),
        in_specs=[pl.BlockSpec((tr, nb), lambda r: (r, 0)), pl.BlockSpec((nb, cols), lambda r: (0, 0)), blk, blk, blk],
        out_specs=[blk] * 4, out_shape=[out] * 4,
        compiler_params=pltpu.CompilerParams(dimension_semantics=("parallel",), vmem_limit_bytes=_vmem(18 * tr * cols * 4)),
    )(c_t, dmod, w, m, v)


def _small_update(stacked, w, m, v):
    R, C = w.shape

    def body(s_ref, w_ref, m_ref, v_ref, g_ref, dl_ref, nm_ref, nv_ref):
        g = s_ref[0:R, :]
        for d in range(1, N_DEV):
            g = g + s_ref[d * R:(d + 1) * R, :]
        g_ref[...] = g
        dl_ref[...], nm_ref[...], nv_ref[...] = _adamw(w_ref[...], g, m_ref[...], v_ref[...])

    return pl.pallas_call(body, name="small_update", out_shape=[jax.ShapeDtypeStruct((R, C), F32)] * 4)(stacked, w, m, v)


def _place():
    return lax.axis_index("x"), lax.axis_index("y"), lax.axis_index("c")


def _allgather8(name, block):
    m_per, n = block.shape

    def body(x_ref, out_ref, token_ref, send_sems, recv_sems, local_sem):
        token_ref[...] = jnp.zeros_like(token_ref)
        x, y, c = _place()
        me, sibling = (x, y, c), (x, y, 1 - c)
        chips = [(1 - x, y), (x, 1 - y), (1 - x, 1 - y)]

        def rows(px, py, pc):
            return out_ref.at[pl.ds((4 * px + 2 * py + pc) * m_per, m_per), :]

        def copy(k, blk, to, src=None):
            return pltpu.make_async_remote_copy(
                src_ref=rows(*blk) if src is None else src, dst_ref=rows(*blk),
                send_sem=send_sems.at[k], recv_sem=recv_sems.at[k], device_id=to, device_id_type=MESH)

        mine = pltpu.make_async_copy(x_ref, rows(*me), local_sem)
        mine.start()
        first = [copy(0, me, sibling, src=x_ref)]
        first += [copy(1 + j, me, (*chip, c), src=x_ref) for j, chip in enumerate(chips)]
        for cp in first:
            cp.start()
        passed = [copy(4 + j, (*chip, c), sibling) for j, chip in enumerate(chips)]
        for j, chip in enumerate(chips):
            copy(1 + j, (*chip, c), me).wait_recv()
            passed[j].start()
        copy(0, sibling, me).wait_recv()
        for j, chip in enumerate(chips):
            copy(4 + j, (*chip, 1 - c), me).wait_recv()
        for cp in first + passed:
            cp.wait_send()
        mine.wait()

    vmem = pl.BlockSpec(memory_space=pltpu.VMEM)
    return pl.pallas_call(
        body, name=name,
        out_shape=[jax.ShapeDtypeStruct((N_DEV * m_per, n), block.dtype), jax.ShapeDtypeStruct((8, LANES), F32)],
        in_specs=[vmem], out_specs=[vmem, vmem],
        scratch_shapes=[pltpu.SemaphoreType.DMA((7,)), pltpu.SemaphoreType.DMA((7,)), pltpu.SemaphoreType.DMA],
    )(block)


_ANY = pl.BlockSpec(memory_space=pl.ANY)


def _half(ref, c, rows):
    return ref.at[pl.ds(c * (rows // 2), rows // 2), :]


_HBM = pl.BlockSpec(memory_space=pltpu.HBM)
_SEM = pl.BlockSpec(memory_space=pltpu.SEMAPHORE)
_EFFECT = pltpu.SideEffectType.DATAFLOW_SIDE_EFFECTING


def _ici_start(name, srcs, land_shapes, plan, per_source=3, after=None):
    ns, nl = len(srcs), len(land_shapes)
    n_copies = per_source * ns
    n_in = ns + nl + (after is not None)

    def body(*refs):
        src_refs, land_refs = refs[:ns], refs[ns:ns + nl]
        send_sems, recv_sems = refs[n_in], refs[n_in + 1]
        token = refs[-1]
        for n, (src, dst, peer, _) in enumerate(plan(src_refs, land_refs)):
            pltpu.make_async_remote_copy(src_ref=src, dst_ref=dst, send_sem=send_sems.at[n], recv_sem=recv_sems.at[n],
                                         device_id=peer, device_id_type=MESH).start()
        token[...] = jnp.zeros_like(token)

    lands = [lax.empty(s.shape, s.dtype) for s in land_shapes]
    out = pl.pallas_call(
        body, name=name,
        out_shape=(pltpu.SemaphoreType.DMA((n_copies,)), pltpu.SemaphoreType.DMA((n_copies,)),
                   *[pltpu.HBM(a.shape, a.dtype) for a in list(srcs) + lands], jax.ShapeDtypeStruct((8, LANES), F32)),
        in_specs=[_HBM] * (ns + nl) + [_ANY] * (after is not None),
        out_specs=(_SEM, _SEM, *[_HBM] * (ns + nl), pl.BlockSpec(memory_space=pltpu.VMEM)),
        input_output_aliases={n: 2 + n for n in range(ns + nl)},
        compiler_params=pltpu.CompilerParams(has_side_effects=_EFFECT),
    )(*[pltpu.with_memory_space_constraint(a, pltpu.HBM) for a in list(srcs) + lands],
      *([] if after is None else [after]))
    return out[0], out[1], list(out[2:2 + ns]), list(out[2 + ns:2 + ns + nl]), out[-1]


def _ici_wait(name, send_sems, recv_sems, srcs, lands, plan, after):
    ns, nl = len(srcs), len(lands)
    after = list(after) if isinstance(after, (list, tuple)) else [after]

    def body(*refs):
        src_refs, land_refs = refs[:ns], refs[ns:ns + nl]
        send_sems, recv_sems = refs[ns + nl], refs[ns + nl + 1]
        for n, (src, _, peer, mine) in enumerate(plan(src_refs, land_refs)):
            cp = pltpu.make_async_remote_copy(src_ref=src, dst_ref=mine, send_sem=send_sems.at[n],
                                              recv_sem=recv_sems.at[n], device_id=peer, device_id_type=MESH)
            cp.wait_send()
            cp.wait_recv()

    out = pl.pallas_call(
        body, name=name, out_shape=[pltpu.HBM(a.shape, a.dtype) for a in list(srcs) + list(lands)],
        in_specs=[_HBM] * (ns + nl) + [_SEM, _SEM] + [_ANY] * len(after), out_specs=[_HBM] * (ns + nl),
        input_output_aliases={n: n for n in range(ns + nl)},
        compiler_params=pltpu.CompilerParams(has_side_effects=_EFFECT),
    )(*srcs, *lands, send_sems, recv_sems, *after)
    return list(out[:ns]), list(out[ns:])


def _own_slab(name, chip, w, after):
    R, C = w.shape
    tr, tc = _tiles(R, C)
    tied = [] if after is None else [after]

    def body(chip_ref, w_ref, *rest):
        stack_ref, token_ref = rest[-2:]
        stack_ref[0] = w_ref[...].astype(BF16)
        token_ref[...] = jnp.zeros_like(token_ref)

    small = pl.BlockSpec((8, LANES), lambda r, q, chip_ref: (0, 0))
    grid_spec = pltpu.PrefetchScalarGridSpec(
        num_scalar_prefetch=1, grid=(R // tr, C // tc),
        in_specs=[pl.BlockSpec((tr, tc), lambda r, q, chip_ref: (r, q))] + [small] * len(tied),
        out_specs=[pl.BlockSpec((1, tr, tc), lambda r, q, chip_ref: (chip_ref[0], r, q)), small])
    return pl.pallas_call(
        body, name=name, grid_spec=grid_spec,
        out_shape=[jax.ShapeDtypeStruct((N_CHIPS, R, C), BF16), jax.ShapeDtypeStruct((8, LANES), F32)],
        compiler_params=pltpu.CompilerParams(dimension_semantics=("arbitrary", "arbitrary")),
    )(chip, w, *tied)


def _gather_plan(src_refs, land_refs):
    x, y, c = _place()
    copies = []
    for stack in src_refs:
        R = stack.shape[1]
        own = _half(stack.at[2 * x + y], c, R)
        for cx, cy in [(1 - x, y), (x, 1 - y), (1 - x, 1 - y)]:
            copies.append((own, own, (cx, cy, c), _half(stack.at[2 * cx + cy], c, R)))
    return copies


def _pass_plan(src_refs, land_refs):
    x, y, c = _place()
    copies = []
    for land in src_refs:
        R = land.shape[1]
        for cx, cy in [(1 - x, y), (x, 1 - y), (1 - x, 1 - y)]:
            slot = land.at[2 * cx + cy]
            copies.append((_half(slot, c, R), _half(slot, c, R), (x, y, 1 - c), _half(slot, 1 - c, R)))
    return copies


def _share_plan(src_refs, land_refs):
    x, y, c = _place()
    return [(h, land, (x, y, 1 - c), land) for h, land in zip(src_refs, land_refs)]


def _pass_to_sibling(name, lands):
    nw = len(lands)

    def body(*refs):
        ins, outs = refs[:nw], refs[nw:2 * nw]
        send_sems, recv_sems = refs[2 * nw:]
        x, y, c = _place()
        chips = [(1 - x, y), (x, 1 - y), (1 - x, 1 - y)]
        copies = []
        for k in range(nw):
            R = ins[k].shape[1]
            for j, (cx, cy) in enumerate(chips):
                cp = pltpu.make_async_remote_copy(
                    src_ref=_half(ins[k].at[2 * cx + cy], c, R), dst_ref=_half(outs[k].at[2 * cx + cy], c, R),
                    send_sem=send_sems.at[3 * k + j], recv_sem=recv_sems.at[3 * k + j],
                    device_id=(x, y, 1 - c), device_id_type=MESH)
                cp.start()
                copies.append(cp)
        for k in range(nw):
            R = ins[k].shape[1]
            for j, (cx, cy) in enumerate(chips):
                pltpu.make_async_remote_copy(
                    src_ref=_half(ins[k].at[2 * cx + cy], c, R), dst_ref=_half(outs[k].at[2 * cx + cy], 1 - c, R),
                    send_sem=send_sems.at[3 * k + j], recv_sem=recv_sems.at[3 * k + j],
                    device_id=(x, y, 1 - c), device_id_type=MESH).wait_recv()
        for cp in copies:
            cp.wait_send()

    return pl.pallas_call(
        body, name=name, out_shape=[jax.ShapeDtypeStruct(a.shape, a.dtype) for a in lands],
        in_specs=[_ANY] * nw, out_specs=[_ANY] * nw, input_output_aliases={k: k for k in range(nw)},
        scratch_shapes=[pltpu.SemaphoreType.DMA((3 * nw,)), pltpu.SemaphoreType.DMA((3 * nw,))],
    )(*lands)


def _tie(vec, token):
    return vec + token[0:1, 0:1]


def _take_columns(name, a, spans):
    S, W = a.shape
    out_w = sum(hi - lo for lo, hi in spans)
    tr = _fit(ROW_TILE, S)

    def body(a_ref, o_ref):
        rows = a_ref[...]
        o_ref[...] = jnp.concatenate([rows[:, lo:hi] for lo, hi in spans], axis=1)

    return pl.pallas_call(
        body, name=name, grid=(S // tr,), in_specs=[pl.BlockSpec((tr, W), lambda i: (i, 0))],
        out_specs=pl.BlockSpec((tr, out_w), lambda i: (i, 0)), out_shape=jax.ShapeDtypeStruct((S, out_w), a.dtype),
        compiler_params=pltpu.CompilerParams(dimension_semantics=("parallel",)),
    )(a)


ROW_ALIGN = 16
TILE_ELEMS = 512 * 1024


def _tiles(rows, cols):
    fits = [t for t in range(ROW_ALIGN, min(rows, 256) + 1, ROW_ALIGN) if rows % t == 0]
    tr = fits[-1] if fits and fits[-1] >= 64 else rows
    tc = cols
    while tr * tc > TILE_ELEMS and tc % (2 * LANES) == 0:
        tc //= 2
    return tr, tc


def _scatter_plan(src_refs, land_refs):
    x, y, c = _place()
    copies = []
    for p, land in zip(src_refs, land_refs):
        for j, (cx, cy) in enumerate([(1 - x, y), (x, 1 - y), (1 - x, 1 - y)]):
            copies.append((p.at[2 * cx + cy], land.at[j], (cx, cy, c), land.at[j]))
    return copies


def _chip_add(name, chip, sums, recv):
    _, H, C = sums.shape
    tr, tc = _tiles(H, C)

    def body(chip_ref, p_ref, r_ref, o_ref):
        total = p_ref[0].astype(F32)
        for j in range(3):
            total = total + r_ref[j].astype(F32)
        o_ref[...] = total

    grid_spec = pltpu.PrefetchScalarGridSpec(
        num_scalar_prefetch=1, grid=(H // tr, C // tc),
        in_specs=[pl.BlockSpec((1, tr, tc), lambda r, q, chip_ref: (chip_ref[0], r, q)),
                  pl.BlockSpec((3, tr, tc), lambda r, q, chip_ref: (0, r, q))],
        out_specs=pl.BlockSpec((tr, tc), lambda r, q, chip_ref: (r, q)))
    return pl.pallas_call(
        body, name=name, grid_spec=grid_spec, out_shape=jax.ShapeDtypeStruct((H, C), F32),
        compiler_params=pltpu.CompilerParams(dimension_semantics=("parallel", "parallel")),
    )(chip, sums, recv)


def _pair_share(name, halves):
    nw = len(halves)

    def body(*refs):
        hs, outs = refs[:nw], refs[nw:2 * nw]
        send_sems, recv_sems = refs[2 * nw:]
        x, y, c = _place()
        copies = []
        for k in range(nw):
            cp = pltpu.make_async_remote_copy(
                src_ref=hs[k], dst_ref=outs[k], send_sem=send_sems.at[k], recv_sem=recv_sems.at[k],
                device_id=(x, y, 1 - c), device_id_type=MESH)
            cp.start()
            copies.append(cp)
        for cp in copies:
            cp.wait()

    return pl.pallas_call(
        body, name=name,
        out_shape=[jax.ShapeDtypeStruct(h.shape, h.dtype) for h in halves],
        in_specs=[_ANY] * nw, out_specs=[_ANY] * nw,
        scratch_shapes=[pltpu.SemaphoreType.DMA((nw,)), pltpu.SemaphoreType.DMA((nw,))],
    )(*halves)


def _adam_halves(name, core, w, g_own, g_other, m, v, out_rows=None):
    R, C = w.shape
    H = R // 2
    tr, tc = _tiles(H, C)
    nr, nc = H // tr, C // tc

    def body(core_ref, w_ref, go_ref, gr_ref, m_ref, v_ref, g_ref, dl_ref, nm_ref, nv_ref):
        own = (pl.program_id(0) // nr) == core_ref[0]
        g = jnp.where(own, go_ref[...], gr_ref[...])
        g_ref[...] = g
        dl_ref[...], nm_ref[...], nv_ref[...] = _adamw(w_ref[...], g, m_ref[...], v_ref[...])

    blk = pl.BlockSpec((tr, tc), lambda r, q, core_ref: (r, q))

    def half_spec(is_own):
        def index(r, q, core_ref):
            mine = ((r // nr) == core_ref[0]) == is_own
            done = is_own == (core_ref[0] == 0)
            return (jnp.where(mine, r % nr, jnp.where(done, nr - 1, 0)), jnp.where(mine, q, jnp.where(done, nc - 1, 0)))
        return pl.BlockSpec((tr, tc), index)
    out_rows = R if out_rows is None else out_rows
    assert R - tr < out_rows <= R, (R, tr, out_rows)
    out = jax.ShapeDtypeStruct((out_rows, C), F32)
    grid_spec = pltpu.PrefetchScalarGridSpec(
        num_scalar_prefetch=1, grid=(R // tr, nc), in_specs=[blk, half_spec(True), half_spec(False), blk, blk],
        out_specs=[blk] * 4)
    return pl.pallas_call(
        body, name=name, grid_spec=grid_spec, out_shape=[out] * 4,
        compiler_params=pltpu.CompilerParams(dimension_semantics=("parallel", "parallel"),
                                             vmem_limit_bytes=_vmem(20 * tr * tc * 4)),
    )(core, w, g_own, g_other, m, v)


def kernel(x, c, w_mod, b_mod, g_pre_mix, g_post_mix, w_in, b_forget, swa_sinks, w_out, g_pre_mlp, g_post_mlp, w_up, w_down, loss_target, m_w_mod, m_b_mod, m_g_pre_mix, m_g_post_mix, m_w_in, m_b_forget, m_swa_sinks, m_w_out, m_g_pre_mlp, m_g_post_mlp, m_w_up, m_w_down, v_w_mod, v_b_mod, v_g_pre_mix, v_g_post_mix, v_w_in, v_b_forget, v_swa_sinks, v_w_out, v_g_pre_mlp, v_g_post_mlp, v_w_up, v_w_down):
    S, D = x.shape[1], x.shape[2]
    n_heads = D // HEAD_DIM
    n_fox = n_heads // 2
    n_swa = n_heads - n_fox
    n_kv = max(1, n_swa // 4)
    fox_w, swa_w, kv_w = n_fox * HEAD_DIM, n_swa * HEAD_DIM, n_kv * HEAD_DIM
    main_w = 3 * fox_w + swa_w + 2 * kv_w
    in_w = main_w + n_fox
    mod_cols = w_mod.shape[2]

    ax, ay, ac = _place()
    chip = 2 * ax + ay
    dev = 2 * chip + ac
    chip_arr = jnp.reshape(chip, (1,)).astype(jnp.int32)
    core_arr = jnp.reshape(ac, (1,)).astype(jnp.int32)

    x2, tgt = x[0], loss_target[0]

    in_rows = in_w // N_CHIPS
    in_rows_pad = -(-in_rows // (2 * LANES)) * (2 * LANES)
    slab_w = N_CHIPS * in_rows_pad

    def rows_of(a):
        return jnp.pad(a[0].T, ((0, in_rows_pad - in_rows), (0, 0)))

    w_in_stack, token = _own_slab("own_slab_w_in", chip_arr, rows_of(w_in), None)

    c_all, _ = _allgather8("gather_c", _tie(c, token).reshape(8, D // 8))
    c_all = c_all.reshape(N_DEV, D)
    b_shard = lax.dynamic_slice_in_dim(b_mod, chip * mod_cols, mod_cols, axis=1)
    mod_shard = _mod_fwd(jnp.pad(c_all, ((0, 16 - N_DEV), (0, 0))), w_mod[0], b_shard)[:N_DEV]
    mod_all, token = _allgather8("gather_mod", mod_shard)
    mod_all = mod_all.reshape(N_CHIPS, 2, N_DEV, mod_cols)[:, 0]
    mod = lax.dynamic_index_in_dim(mod_all, dev, axis=1, keepdims=False).reshape(N_MOD, 1, D)
    sh_a, sc_a, gt_a, sh_m, sc_m, gt_m = [mod[n] for n in range(N_MOD)]

    def slab_cols(lo, hi):
        spans = []
        while lo < hi:
            s, r = divmod(lo, in_rows)
            n = min(hi - lo, in_rows - r)
            spans.append((s * in_rows_pad + r, s * in_rows_pad + r + n))
            lo += n
        return spans

    gate_lo = 3 * fox_w
    main_spans = slab_cols(0, gate_lo) + slab_cols(gate_lo + n_fox, in_w)
    (gate_first, gate_last), = slab_cols(gate_lo, gate_lo + n_fox)

    names = ["w_in", "w_out", "w_up", "w_down"]
    flights = {}
    for n, w in zip(names, [None, w_out[0], w_up[0], w_down[0]]):
        stack = w_in_stack if n == "w_in" else _own_slab("own_slab_" + n, chip_arr, w, token)[0]
        flights[n] = _ici_start("gather_start_" + n, [stack], [], _gather_plan, after=token)
        token = flights[n][4]
    sc_a = _tie(sc_a, token)

    def arrived(n, after):
        send, recv, stacks, _, _ = flights[n]
        stacks, _ = _ici_wait("gather_wait_" + n, send, recv, stacks, [], _gather_plan, after)
        return _ici_start("gather_pass_start_" + n, stacks, [], _pass_plan)

    def gathered(n, after, in_flight=None):
        if in_flight is None:
            send, recv, stacks, _, _ = flights[n]
            stacks, _ = _ici_wait("gather_wait_" + n, send, recv, stacks, [], _gather_plan, after)
            return _pass_to_sibling("gather_pass_" + n, stacks)[0]
        send, recv, stacks, _, _ = in_flight
        return _ici_wait("gather_pass_wait_" + n, send, recv, stacks, [], _pass_plan, after)[0][0]

    d_ff = N_CHIPS * w_up.shape[2]

    h = _pre_norm(x2, g_pre_mix, sc_a, sh_a)
    in_state = [rows_of(w_in)] + [rows_of(_tie(a, token)) for a in (m_w_in, v_w_in)]
    cos, sin_signed = _rope_tables(S)

    def pack(bm, gpm, gqm, gpl, gql, bf, sk):
        last = jnp.concatenate([bf, sk, jnp.zeros((1, D - n_fox - n_swa), F32)], axis=1)
        return jnp.concatenate([bm.reshape(N_MOD, D), gpm, gqm, gpl, gql, last, jnp.zeros((5, D), F32)], axis=0)

    small_state = [pack(b_mod, g_pre_mix, g_post_mix, g_pre_mlp, g_post_mlp, b_forget, swa_sinks),
                   pack(m_b_mod, m_g_pre_mix, m_g_post_mix, m_g_pre_mlp, m_g_post_mlp, m_b_forget, m_swa_sinks),
                   pack(v_b_mod, v_g_pre_mix, v_g_post_mix, v_g_pre_mlp, v_g_post_mlp, v_b_forget, v_swa_sinks)]
    ready = h[:8, :LANES].astype(F32) + cos[:8]
    w_slab_t = gathered("w_in", [ready] + in_state[1:] + small_state).reshape(slab_w, D)
    tm_p, tn_p = _fit(MM_TM, S), _fit(MM_TN if slab_w % MM_TN == 0 else MM_TN // 2, slab_w)
    win0 = gate_first // LANES * LANES
    win_j, win_off = divmod(win0, tn_p)
    assert win_off + 2 * LANES <= tn_p and gate_last - win0 <= 2 * LANES

    def proj_epilogue(acc, ex, outs):
        outs[0][...] = acc.astype(BF16)

        @pl.when(pl.program_id(1) == win_j)
        def _():
            outs[1][...] = acc[:, win_off:win_off + 2 * LANES]

    proj_slab, gate_win = _matmul(
        "in_proj", h, w_slab_t, "nt",
        [((S, slab_w), BF16, (tm_p, tn_p), lambda i, j: (i, j)), ((S, 2 * LANES), F32, (tm_p, 2 * LANES), lambda i, j: (i, 0))],
        proj_epilogue, tn=tn_p, revisits=True)
    proj = _take_columns("proj_head_order", proj_slab, main_spans)
    out_flight = arrived("w_out", proj_slab)
    fg = _tie(jnp.pad(gate_win[:, gate_first - win0:gate_last - win0], ((0, 0), (0, LANES - n_fox))), out_flight[4])
    b_pad = jnp.pad(b_forget, ((0, 0), (0, LANES - n_fox)))
    cum_row = _fox_gate_fwd(fg, b_pad)[:n_fox].reshape(n_fox, 1, S)
    fox_o, fox_lse = _fox_fwd(proj, cum_row, n_fox)

    rq = _rope("rope_fwd", proj, 3 * n_fox, n_swa + n_kv, cos, sin_signed)
    v_first = 3 * n_fox + n_swa + n_kv
    sinks = swa_sinks[0]
    swa_o, swa_lse = _swa_fwd(rq, proj, v_first, sinks, n_swa, n_kv)

    mixcat = jnp.concatenate([fox_o, swa_o], axis=1).astype(BF16)
    up_flight = arrived("w_up", mixcat)
    w_out_f = gathered("w_out", mixcat, out_flight).reshape(D, D)
    mix = _mm_plain("out_proj", mixcat, w_out_f, "nn", BF16, after=up_flight[4])
    x1, h2 = _post_mix(x2, mix, g_post_mix, gt_a, g_pre_mlp, sc_m, sh_m)
    w_up_f = gathered("w_up", h2, up_flight)

    tm_u, tn_u = _fit(MM_TM, S), _fit(MM_TN, d_ff)

    def up_epilogue(acc, ex, outs):
        outs[0][...] = acc.astype(BF16)
        r = jnp.maximum(acc, 0.0)
        outs[1][...] = (r * r).astype(BF16)

    ublk = ((S, d_ff), BF16, (tm_u, tn_u), lambda i, j: (i, j))
    u, a = _matmul("mlp_up", h2, w_up_f, "nn", [ublk, ublk], up_epilogue)
    w_down_f = gathered("w_down", a).reshape(d_ff, D)
    y = _mm_plain("mlp_down", a, w_down_f, "nn", BF16)

    dy, dout, loss_part, acc_mlp_post = _loss_and_post_mlp_bwd(x1, y, tgt, g_post_mlp, gt_m)

    def du_epilogue(acc, ex, outs):
        outs[0][...] = (acc * (2.0 * jnp.maximum(ex[0][...].astype(F32), 0.0))).astype(BF16)

    du = _matmul("mlp_down_bwd", dy, w_down_f, "nt", [ublk], du_epilogue,
                 extras=[(u, (tm_u, tn_u), lambda i, j: (i, j))])[0]
    def pair_send(tag, part):
        return _ici_start("grad_pair_start_" + tag, [part], [jax.ShapeDtypeStruct(part.shape, BF16)], _share_plan,
                          per_source=1)

    def pair_recv(tag, flight, after):
        send, recv, srcs, lands, _ = flight
        return _ici_wait("grad_pair_wait_" + tag, send, recv, srcs, lands, _share_plan, after)[1][0]

    def scatter_start(tag, sums, after=None):
        return _ici_start("grad_scatter_start_" + tag, sums,
                          [jax.ShapeDtypeStruct((3,) + p.shape[1:], BF16) for p in sums], _scatter_plan, after=after)

    def scatter_finish(tag, flight, after):
        send, recv, srcs, lands, _ = flight
        sums, received = _ici_wait("grad_scatter_wait_" + tag, send, recv, srcs, lands, _scatter_plan, after)
        return [_chip_add("chip_add_%s_%d" % (tag, k), chip_arr, p, r) for k, (p, r) in enumerate(zip(sums, received))]

    tm_g = _fit(MM_TM, D // 2)
    pair_down = pair_send("down", _grad_half("grad_w_down_a", core_arr, a, dy, N_CHIPS, 1, tm_g, True))
    pair_up = pair_send("up", _grad_half("grad_w_up_a", core_arr, h2, du, 1, N_CHIPS, tm_g, True, after=pair_down[4]))
    sum_down = _grad_half("grad_w_down_b", core_arr, a, dy, N_CHIPS, 1, tm_g, False,
                          recv=pair_recv("down", pair_down, pair_up[4]))
    sum_up = _grad_half("grad_w_up_b", core_arr, h2, du, 1, N_CHIPS, tm_g, False, recv=pair_recv("up", pair_up, sum_down))
    flight_mlp = scatter_start("mlp", [sum_up, sum_down])
    dh2 = _mm_plain("mlp_up_bwd", du, w_up_f, "nt", BF16, after=flight_mlp[4])
    dx1, dmix, acc_mid = _pre_mlp_and_post_mix_bwd(dh2, x1, dout, mix, _tie(g_pre_mlp, flight_mlp[4]), sc_m,
                                                   g_post_mix, gt_a)

    dmixcat = _mm_plain("out_proj_bwd", dmix, w_out_f, "nt", F32)

    fdq, fdk, fdv, dcum_row, dcum_q = _fox_bwd(proj, fox_o, dmixcat, fox_lse, cum_row, n_fox)
    dcum_k = jnp.pad(dcum_row.reshape(n_fox, S), ((0, LANES - n_fox), (0, 0)))
    dfg, db_forget = _fox_gate_bwd(dcum_k, dcum_q, fg, b_pad)

    group_w = (n_swa // n_kv) * HEAD_DIM
    sdq, sdk, sdv, dsink = _swa_bwd(rq, proj, v_first, sinks, swa_o, dmixcat, fox_w // group_w, swa_lse, n_swa, n_kv)
    drq = jnp.concatenate([sdq, jnp.transpose(sdk, (1, 0, 2)).reshape(S, kv_w).astype(BF16)], axis=1)
    d_sq_sk = _rope("rope_bwd", drq, 0, n_swa + n_kv, cos, -sin_signed)
    dsv = jnp.transpose(sdv, (1, 0, 2)).reshape(S, kv_w).astype(BF16)
    dproj = jnp.concatenate([fdq, fdk, fdv, d_sq_sk, dsv], axis=1)

    pieces = []
    for s in range(N_CHIPS):
        lo, hi = s * in_rows, (s + 1) * in_rows
        for src, first, last, shift in [(dproj, 0, gate_lo, 0), (dfg, gate_lo, gate_lo + n_fox, gate_lo),
                                        (dproj, gate_lo + n_fox, in_w, n_fox)]:
            if max(lo, first) < min(hi, last):
                pieces.append(src[:, max(lo, first) - shift:min(hi, last) - shift])
        pieces.append(jnp.zeros((S, in_rows_pad - in_rows), BF16))
    dproj_slab = jnp.concatenate(pieces, axis=1)

    tm_in, tm_out = in_rows_pad // 2, D // (2 * N_CHIPS)
    pair_in = pair_send("in", _grad_half("grad_w_in_a", core_arr, dproj_slab, h, N_CHIPS, 1, tm_in, True))
    pair_out = pair_send("out", _grad_half("grad_w_out_a", core_arr, mixcat, dmix, N_CHIPS, 1, tm_out, True,
                                           after=pair_in[4]))
    sum_in = _grad_half("grad_w_in_b", core_arr, dproj_slab, h, N_CHIPS, 1, tm_in, False,
                        recv=pair_recv("in", pair_in, pair_out[4]))
    sum_out = _grad_half("grad_w_out_b", core_arr, mixcat, dmix, N_CHIPS, 1, tm_out, False,
                         recv=pair_recv("out", pair_out, sum_in[0, :8, :LANES]))
    dh = _mm_plain("in_proj_bwd", dproj_slab, w_slab_t, "nn", BF16, tk=slab_w // 2,
                   after=sum_out[0, :8, :LANES].astype(F32))
    grad_x, acc_pre = _pre_mix_bwd(dh, x2, dx1, g_pre_mix, sc_a)

    zero_row = jnp.zeros((1, D), F32)
    tail = jnp.concatenate([db_forget[0:1, :n_fox], dsink[:, 0, :n_swa // n_kv].reshape(1, n_swa),
                            loss_part[0:1, 0:1], jnp.zeros((1, D - n_fox - n_swa - 1), F32)], axis=1)
    partial = jnp.concatenate([
        acc_pre[0:1], acc_pre[1:2], acc_mid[3:4], acc_mid[0:1], acc_mid[1:2], acc_mlp_post[0:1],
        acc_pre[2:3], acc_mid[4:5], acc_mid[2:3], acc_mlp_post[1:2], tail] + [zero_row] * 5, axis=0)
    gathered_small, token = _allgather8("gather_small_grads", partial)

    flight_mix = scatter_start("mix", [sum_in, sum_out], after=token)
    halves_mlp = scatter_finish("mlp", flight_mlp, flight_mix[4])
    share_up, share_down = [
        _ici_start("grad_share_start_" + n, [hv], [jax.ShapeDtypeStruct(hv.shape, F32)], _share_plan, per_source=1)
        for n, hv in zip(["up", "down"], halves_mlp)]

    def shared(tag, flight, after):
        send, recv, own, lands, _ = flight
        own, other = _ici_wait("grad_share_wait_" + tag, send, recv, own, lands, _share_plan, after)
        return own[0], other[0]

    def unpack(p):
        return {"b_mod": p[0:N_MOD].reshape(1, N_MOD * D), "g_pre_mix": p[6:7], "g_post_mix": p[7:8],
                "g_pre_mlp": p[8:9], "g_post_mlp": p[9:10], "b_forget": p[10:11, :n_fox],
                "swa_sinks": p[10:11, n_fox:n_fox + n_swa]}

    small_out = _small_update(gathered_small, _tie(small_state[0], share_down[4] + share_up[4]), small_state[1],
                              small_state[2])
    g_small, d_small, m_small, v_small = [unpack(p) for p in small_out]
    loss = small_out[0][N_MOD + 4, n_fox + n_swa]

    dmod_all = gathered_small.reshape(N_DEV, 16, D)[:, :N_MOD].reshape(N_DEV, N_MOD * D)
    dmod_shard = _tie(lax.dynamic_slice_in_dim(dmod_all, chip * mod_cols, mod_cols, axis=1), share_down[4])
    g_w_mod, d_w_mod, nm_w_mod, nv_w_mod = _mod_update(c_all.T, dmod_shard, w_mod[0], m_w_mod[0], v_w_mod[0])

    grads = dict(g_small, w_mod=g_w_mod[None])
    deltas = dict(d_small, w_mod=d_w_mod[None])
    new_m = dict(m_small, w_mod=nm_w_mod[None])
    new_v = dict(v_small, w_mod=nv_w_mod[None])
    weights = {"w_in": (w_in, m_w_in, v_w_in), "w_out": (w_out, m_w_out, v_w_out), "w_up": (w_up, m_w_up, v_w_up),
               "w_down": (w_down, m_w_down, v_w_down)}

    updated = {}

    def big_update(n, own, other):
        transposed = n == "w_in"
        w, m, v = in_state if transposed else [a[0] for a in weights[n]]
        outs = _adam_halves("adam_" + n, core_arr, w, own, other, m, v, out_rows=in_rows if transposed else None)
        updated[n] = outs[1][:8, :LANES]
        if transposed:
            outs = [o.T for o in outs]
        grads[n], deltas[n], new_m[n], new_v[n] = [o[None] for o in outs]

    big_update("w_down", *shared("down", share_down, d_w_mod[:8, :LANES] + small_out[1][:8, :LANES]))
    halves_mix = scatter_finish("mix", flight_mix, updated["w_down"] + d_w_mod[:8, :LANES])
    others_mix = _pair_share("grad_pair_share_mix", halves_mix)
    big_update("w_in", halves_mix[0], others_mix[0])
    big_update("w_out", halves_mix[1], others_mix[1])
    big_update("w_up", *shared("up", share_up, updated["w_out"] + updated["w_in"]))

    order = ["w_mod", "b_mod", "g_pre_mix", "g_post_mix", "w_in", "b_forget", "swa_sinks", "w_out", "g_pre_mlp",
             "g_post_mlp", "w_up", "w_down"]
    return (loss, grad_x[None], *[grads[n] for n in order], *[deltas[n] for n in order],
            *[new_m[n] for n in order], *[new_v[n] for n in order])
```

```python
import jax
import jax.numpy as jnp
from jax import lax
from jax.experimental import pallas as pl
from jax.experimental.pallas import tpu as pltpu

F32 = jnp.float32
BF16 = jnp.bfloat16
MESH = pl.DeviceIdType.MESH

HEAD_DIM = 128
SWA_BLOCK = 128
ROPE_THETA = 10000.0
NORM_EPS = 1e-6
NEG = -1e30
N_MOD = 6
ADAM_LR = 0.001
ADAM_B1 = 0.9
ADAM_B2 = 0.999
ADAM_EPS = 1e-08
ADAM_WD = 0.01
ADAM_STEP = 10
N_CHIPS = 4
N_DEV = 8
LANES = 128
VMEM_CAP = 60 * 1024 * 1024

_NN = (((1,), (0,)), ((), ()))
_NT = (((1,), (1,)), ((), ()))
_TN = (((0,), (0,)), ((), ()))


def _vmem(nbytes):
    return int(min(VMEM_CAP, nbytes * 5 // 4 + (4 << 20)))


def _nbytes(shape, dtype):
    n = 1
    for s in shape:
        n *= s
    return n * jnp.dtype(dtype).itemsize


def _fit(t, n):
    t = min(t, n)
    assert n % t == 0, (t, n)
    return t


MM_TM, MM_TN, MM_TK = 1024, 1024, 2048


def _matmul(name, a, b, mode, out_defs, epilogue, extras=(), tm=MM_TM, tn=MM_TN, tk=MM_TK, revisits=False,
            row_sel=None):
    stacked = b.ndim == 3
    b_rows, b_cols = b.shape[-2], b.shape[-1] * (b.shape[0] if stacked else 1)
    if mode == "nn":
        (M, K), (K2, N) = a.shape, (b_rows, b_cols)
    elif mode == "nt":
        (M, K), (N, K2) = a.shape, (b_rows, b_cols)
    else:
        (K, M), (K2, N) = a.shape, (b_rows, b_cols)
    assert K == K2 and not (stacked and mode == "tn"), (a.shape, b.shape, mode)
    tm = _fit(tm, M)
    tn = _fit(tn, b.shape[-1] if stacked and mode == "nn" else N)
    tk = _fit(tk, b.shape[-1] if stacked and mode == "nt" else K)
    nk = K // tk
    dims = {"nn": _NN, "nt": _NT, "tn": _TN}[mode]
    if row_sel is None:
        grid_m, a_row = M // tm, lambda i, *sel: i
    else:
        grid_m, a_row = row_sel[2], lambda i, *sel: row_sel[1](i, sel[0])
    a_spec = (pl.BlockSpec((tk, tm), lambda i, j, k, *sel: (k, a_row(i, *sel))) if mode == "tn"
              else pl.BlockSpec((tm, tk), lambda i, j, k, *sel: (a_row(i, *sel), k)))
    if stacked:
        per = b.shape[-1] // (tk if mode == "nt" else tn)
        b_spec = (pl.BlockSpec((1, tn, tk), lambda i, j, k, *sel: (k // per, j, k % per)) if mode == "nt"
                  else pl.BlockSpec((1, tk, tn), lambda i, j, k, *sel: (j // per, k, j % per)))
    else:
        b_spec = (pl.BlockSpec((tn, tk), lambda i, j, k, *sel: (j, k)) if mode == "nt"
                  else pl.BlockSpec((tk, tn), lambda i, j, k, *sel: (k, j)))
    n_ex, n_out = len(extras), len(out_defs)

    def body(*refs):
        if row_sel is not None:
            refs = refs[1:]
        a_ref, b_ref = refs[0], refs[1]
        ex = refs[2:2 + n_ex]
        outs = refs[2 + n_ex:2 + n_ex + n_out]
        b_blk = b_ref[0] if stacked else b_ref[...]
        prod = lax.dot_general(a_ref[...], b_blk, dims, preferred_element_type=F32)
        if nk == 1:
            epilogue(prod, ex, outs)
        else:
            acc_ref = refs[-1]
            k = pl.program_id(2)

            @pl.when(k == 0)
            def _():
                acc_ref[...] = prod

            @pl.when(k > 0)
            def _():
                acc_ref[...] += prod

            @pl.when(k == nk - 1)
            def _():
                epilogue(acc_ref[...], ex, outs)

    def wrap(f):
        return lambda i, j, k, *sel: f(i, j)

    in_specs = [a_spec, b_spec] + [pl.BlockSpec(blk, wrap(f)) for _, blk, f in extras]
    out_specs = [pl.BlockSpec(blk, wrap(f)) for _, _, blk, f in out_defs]
    out_shape = [jax.ShapeDtypeStruct(s, d) for s, d, _, _ in out_defs]
    need = 2 * (tm * tk + tk * tn) * a.dtype.itemsize + 3 * tm * tn * 4
    need += sum(2 * _nbytes(blk, arr.dtype) for arr, blk, _ in extras)
    need += sum(2 * _nbytes(blk, d) for _, d, blk, _ in out_defs)
    grid = (grid_m, N // tn, nk)
    scratch = [pltpu.VMEM((tm, tn), F32)] if nk > 1 else []
    params = pltpu.CompilerParams(
        dimension_semantics=("parallel", "arbitrary" if revisits else "parallel", "arbitrary"),
        vmem_limit_bytes=_vmem(need))
    operands = (a, b, *[arr for arr, _, _ in extras])
    if row_sel is None:
        return pl.pallas_call(body, name=name, grid=grid, in_specs=in_specs, out_specs=out_specs, out_shape=out_shape,
                              scratch_shapes=scratch, compiler_params=params)(*operands)
    grid_spec = pltpu.PrefetchScalarGridSpec(num_scalar_prefetch=1, grid=grid, in_specs=in_specs, out_specs=out_specs,
                                             scratch_shapes=scratch)
    return pl.pallas_call(body, name=name, grid_spec=grid_spec, out_shape=out_shape,
                          compiler_params=params)(row_sel[0], *operands)


def _grad_half(name, core, a, b, row_slabs, col_slabs, tm, other, recv=None, after=None):
    (_, M), (_, N) = a.shape, b.shape
    H = M // (2 * row_slabs)
    nh = H // tm
    tn = _fit(MM_TN, N // col_slabs)
    per = N // col_slabs // tn

    def a_block(i, core_ref):
        half = (1 - core_ref[0]) if other else core_ref[0]
        return (i // nh) * (2 * nh) + half * nh + i % nh

    def out_index(i, j):
        return (j // per, i, j % per) if col_slabs > 1 else (i // nh, i % nh, j)

    slabs = max(row_slabs, col_slabs)
    out_def = ((slabs, H, N // col_slabs), BF16, (1, tm, tn), out_index)

    def epilogue(acc, ex, outs):
        outs[0][0] = (acc if recv is None else acc + ex[0][0].astype(F32)).astype(BF16)

    extras = ([] if recv is None else [(recv, (1, tm, tn), out_index)]) + ([] if after is None else [_behind(after)])
    return _matmul(name, a, b, "tn", [out_def], epilogue, extras=extras, tm=tm, tn=tn,
                   row_sel=(core, a_block, row_slabs * nh))[0]


def _behind(token):
    return (token, (8, LANES), lambda i, j: (0, 0))


def _mm_plain(name, a, b, mode, out_dtype, after=None, **tiles):
    if mode == "nn":
        M, N = a.shape[0], b.shape[-1] * (b.shape[0] if b.ndim == 3 else 1)
    elif mode == "nt":
        M, N = a.shape[0], b.shape[-2]
    else:
        M, N = a.shape[1], b.shape[1]
    tm, tn = _fit(tiles.get("tm", MM_TM), M), _fit(tiles.get("tn", MM_TN), N)

    def epi(acc, ex, outs):
        outs[0][...] = acc.astype(out_dtype)

    return _matmul(name, a, b, mode, [((M, N), out_dtype, (tm, tn), lambda i, j: (i, j))], epi,
                   extras=[] if after is None else [_behind(after)], **tiles)[0]


def _rstd(v):
    return lax.rsqrt(jnp.mean(v * v, axis=-1, keepdims=True) + NORM_EPS)


ROW_TILE = 256


def _row_call(name, body, row_ins, vec_ins, row_outs, acc_outs, S, D):
    tr = _fit(ROW_TILE, S)
    row_spec = pl.BlockSpec((tr, D), lambda r: (r, 0))
    vec_spec = pl.BlockSpec((1, D), lambda r: (0, 0))
    in_specs = [row_spec] * len(row_ins) + [vec_spec] * len(vec_ins)
    out_specs = [row_spec] * len(row_outs) + [pl.BlockSpec(shp, lambda r: (0, 0)) for shp in acc_outs]
    out_shape = [jax.ShapeDtypeStruct((S, D), d) for d in row_outs] + [jax.ShapeDtypeStruct(shp, F32) for shp in acc_outs]
    need = sum(2 * tr * D * a.dtype.itemsize for a in row_ins) + sum(2 * tr * D * jnp.dtype(d).itemsize for d in row_outs)
    need += 8 * tr * D * 4
    return pl.pallas_call(
        body, name=name, grid=(S // tr,), in_specs=in_specs, out_specs=out_specs, out_shape=out_shape,
        compiler_params=pltpu.CompilerParams(dimension_semantics=("arbitrary",), vmem_limit_bytes=_vmem(need)),
    )(*row_ins, *vec_ins)


def _acc_rows(ref, rows):
    @pl.when(pl.program_id(0) == 0)
    def _():
        ref[...] = jnp.zeros_like(ref)
    for n, r in enumerate(rows):
        ref[n:n + 1, :] += r


def _pre_norm(x, g, sc, sh):
    S, D = x.shape

    def body(x_ref, g_ref, sc_ref, sh_ref, h_ref):
        xv = x_ref[...]
        xn = xv * _rstd(xv)
        h_ref[...] = (xn * g_ref[...] * (1.0 + sc_ref[...]) + sh_ref[...]).astype(BF16)

    return _row_call("pre_norm_mix", body, [x], [g, sc, sh], [BF16], [], S, D)[0]


def _post_mix(x, mix, g_post, gt, g_pre, sc, sh):
    S, D = x.shape

    def body(x_ref, mix_ref, gp_ref, gt_ref, g2_ref, sc_ref, sh_ref, x1_ref, h2_ref):
        mv = mix_ref[...].astype(F32)
        x1 = x_ref[...] + gt_ref[...] * (mv * _rstd(mv) * gp_ref[...])
        x1_ref[...] = x1
        h2_ref[...] = (x1 * _rstd(x1) * g2_ref[...] * (1.0 + sc_ref[...]) + sh_ref[...]).astype(BF16)

    return _row_call("post_mix_pre_mlp", body, [x, mix], [g_post, gt, g_pre, sc, sh], [F32, BF16], [], S, D)


def _loss_and_post_mlp_bwd(x1, y, target, g_post, gt):
    S, D = x1.shape

    def body(x1_ref, y_ref, t_ref, g_ref, gt_ref, dy_ref, dout_ref, loss_ref, acc_ref):
        yv = y_ref[...].astype(F32)
        r = _rstd(yv)
        yh = yv * r
        n = yh * g_ref[...]
        diff = x1_ref[...] + gt_ref[...] * n - t_ref[...]
        dout = diff * (1.0 / D)
        dout_ref[...] = dout
        dn = dout * gt_ref[...]
        dyh = dn * g_ref[...]
        dy_ref[...] = (r * (dyh - yh * jnp.mean(dyh * yh, axis=-1, keepdims=True))).astype(BF16)
        _acc_rows(acc_ref, [jnp.sum(dout * n, axis=0, keepdims=True), jnp.sum(dn * yh, axis=0, keepdims=True)])

        @pl.when(pl.program_id(0) == 0)
        def _():
            loss_ref[...] = jnp.zeros_like(loss_ref)
        loss_ref[...] += jnp.full(loss_ref.shape, (0.5 / D) * jnp.sum(diff * diff), F32)

    return _row_call("loss_post_mlp_bwd", body, [x1, y, target], [g_post, gt], [BF16, F32],
                     [(8, LANES), (8, D)], S, D)


def _pre_mlp_and_post_mix_bwd(dh2, x1, dout, mix, g_pre, sc, g_post, gt):
    S, D = x1.shape

    def body(dh_ref, x1_ref, dout_ref, mix_ref, g_ref, sc_ref, gp_ref, gt_ref, dx1_ref, dmix_ref, acc_ref):
        dh = dh_ref[...].astype(F32)
        x1v = x1_ref[...]
        r3 = _rstd(x1v)
        xn = x1v * r3
        dxn = dh * (1.0 + sc_ref[...]) * g_ref[...]
        dx1 = dout_ref[...] + r3 * (dxn - xn * jnp.mean(dxn * xn, axis=-1, keepdims=True))
        dx1_ref[...] = dx1
        mv = mix_ref[...].astype(F32)
        r2 = _rstd(mv)
        mh = mv * r2
        dn = dx1 * gt_ref[...]
        dmh = dn * gp_ref[...]
        dmix_ref[...] = (r2 * (dmh - mh * jnp.mean(dmh * mh, axis=-1, keepdims=True))).astype(BF16)
        _acc_rows(acc_ref, [
            jnp.sum(dh, axis=0, keepdims=True),
            jnp.sum(dh * xn * g_ref[...], axis=0, keepdims=True),
            jnp.sum(dh * (1.0 + sc_ref[...]) * xn, axis=0, keepdims=True),
            jnp.sum(dx1 * mh * gp_ref[...], axis=0, keepdims=True),
            jnp.sum(dn * mh, axis=0, keepdims=True)])

    return _row_call("pre_mlp_post_mix_bwd", body, [dh2, x1, dout, mix], [g_pre, sc, g_post, gt], [F32, BF16],
                     [(8, D)], S, D)


def _pre_mix_bwd(dh, x, dx1, g_pre, sc):
    S, D = x.shape

    def body(dh_ref, x_ref, dx1_ref, g_ref, sc_ref, gx_ref, acc_ref):
        dhv = dh_ref[...].astype(F32)
        xv = x_ref[...]
        r = _rstd(xv)
        xn = xv * r
        dxn = dhv * (1.0 + sc_ref[...]) * g_ref[...]
        gx_ref[...] = dx1_ref[...] + r * (dxn - xn * jnp.mean(dxn * xn, axis=-1, keepdims=True))
        _acc_rows(acc_ref, [
            jnp.sum(dhv, axis=0, keepdims=True),
            jnp.sum(dhv * xn * g_ref[...], axis=0, keepdims=True),
            jnp.sum(dhv * (1.0 + sc_ref[...]) * xn, axis=0, keepdims=True)])

    return _row_call("pre_mix_bwd", body, [dh, x, dx1], [g_pre, sc], [F32], [(8, D)], S, D)


CUM_BLOCK = 256


def _tri(n, upper):
    r = lax.broadcasted_iota(jnp.int32, (n, n), 0)
    c = lax.broadcasted_iota(jnp.int32, (n, n), 1)
    return ((c >= r) if upper else (c <= r)).astype(F32)


def _fox_gate_fwd(fg, b_pad):
    S = fg.shape[0]
    cb = _fit(CUM_BLOCK, S)

    def body(fg_ref, b_ref, cumt_ref, cum_ref):
        low = _tri(cb, False)
        carry = jnp.zeros((1, LANES), F32)
        for n in range(S // cb):
            z = fg_ref[n * cb:(n + 1) * cb, :] + b_ref[...]
            logf = jnp.minimum(z, 0.0) - jnp.log(1.0 + jnp.exp(-jnp.abs(z)))
            blk = jnp.dot(low, logf, precision=lax.Precision.HIGHEST, preferred_element_type=F32) + carry
            cum_ref[n * cb:(n + 1) * cb, :] = blk
            carry = blk[cb - 1:cb, :]
        cumt_ref[...] = cum_ref[...].T

    return pl.pallas_call(
        body, name="fox_gate_fwd", out_shape=jax.ShapeDtypeStruct((LANES, S), F32),
        scratch_shapes=[pltpu.VMEM((S, LANES), F32)],
        compiler_params=pltpu.CompilerParams(vmem_limit_bytes=_vmem(6 * S * LANES * 4)),
    )(fg, b_pad)


def _fox_gate_bwd(dcum_k, dcum_q, fg, b_pad):
    S = fg.shape[0]
    n_fox = dcum_q.shape[0]
    cb = _fit(CUM_BLOCK, S)

    def body(dk_ref, dq_ref, fg_ref, b_ref, dfg_ref, db_ref, dc_ref):
        lane = lax.broadcasted_iota(jnp.int32, (S, LANES), 1)
        dc = dk_ref[...].T
        for h in range(n_fox):
            dc = dc + jnp.where(lane == h, dq_ref[h], 0.0)
        dc_ref[...] = dc
        up = _tri(cb, True)
        carry = jnp.zeros((1, LANES), F32)
        db = jnp.zeros((1, LANES), F32)
        for n in reversed(range(S // cb)):
            blk = jnp.dot(up, dc_ref[n * cb:(n + 1) * cb, :], precision=lax.Precision.HIGHEST,
                          preferred_element_type=F32) + carry
            carry = blk[0:1, :]
            z = fg_ref[n * cb:(n + 1) * cb, :] + b_ref[...]
            dfg = blk * (1.0 / (1.0 + jnp.exp(z)))
            dfg_ref[n * cb:(n + 1) * cb, :] = dfg.astype(BF16)
            db = db + jnp.sum(dfg, axis=0, keepdims=True)
        db_ref[...] = jnp.broadcast_to(db, db_ref.shape)

    return pl.pallas_call(
        body, name="fox_gate_bwd",
        out_shape=[jax.ShapeDtypeStruct((S, LANES), BF16), jax.ShapeDtypeStruct((8, LANES), F32)],
        scratch_shapes=[pltpu.VMEM((S, LANES), F32)],
        compiler_params=pltpu.CompilerParams(vmem_limit_bytes=_vmem((8 + 2 * n_fox) * S * LANES * 4)),
    )(dcum_k, dcum_q, fg, b_pad)


FOX_TILE = 512


LOG2E = 1.4426950408889634


def _fox_scores(q, k, ck2, masked, t):
    s = lax.dot_general(q, k, _NT, preferred_element_type=F32) * (HEAD_DIM ** -0.5 * LOG2E) - ck2
    if masked:
        row = lax.broadcasted_iota(jnp.int32, (t, t), 0)
        col = lax.broadcasted_iota(jnp.int32, (t, t), 1)
        s = jnp.where(col <= row, s, NEG)
    return s


def _fox_fwd(proj, cum_row, n_fox):
    S = proj.shape[0]
    t = _fit(FOX_TILE, S)
    nq = S // t

    def body(q_ref, k_ref, v_ref, ck_ref, o_ref, lse_ref):
        def q_block(qi, _):
            q0 = pl.multiple_of(qi * t, t)
            q = q_ref[pl.ds(q0, t), :]

            def kv_block(j, carry, masked):
                m, l, acc = carry
                k0 = pl.multiple_of(j * t, t)
                s = _fox_scores(q, k_ref[pl.ds(k0, t), :], ck_ref[0, :, pl.ds(k0, t)] * LOG2E, masked, t)
                m_new = jnp.maximum(m, jnp.max(s, axis=-1, keepdims=True))
                alpha = jnp.exp2(m - m_new)
                p = jnp.exp2(s - m_new)
                l = alpha * l + jnp.sum(p, axis=-1, keepdims=True)
                acc = alpha * acc + jnp.dot(p.astype(BF16), v_ref[pl.ds(k0, t), :], preferred_element_type=F32)
                return m_new, l, acc

            init = (jnp.full((t, 1), NEG, F32), jnp.zeros((t, 1), F32), jnp.zeros((t, HEAD_DIM), F32))
            carry = lax.fori_loop(0, qi, lambda j, cr: kv_block(j, cr, False), init)
            m, l, acc = kv_block(qi, carry, True)
            o_ref[pl.ds(q0, t), :] = acc / l
            lse_ref[0, pl.ds(q0, t), :] = jnp.broadcast_to(m + jnp.log(l) * LOG2E, (t, LANES))
            return 0

        lax.fori_loop(0, nq, q_block, 0)

    col = lambda off: pl.BlockSpec((S, HEAD_DIM), lambda h: (0, off + h))
    per_head = pl.BlockSpec((1, S, LANES), lambda h: (h, 0, 0))
    return pl.pallas_call(
        body, name="fox_fwd", grid=(n_fox,),
        in_specs=[col(0), col(n_fox), col(2 * n_fox), pl.BlockSpec((1, 1, S), lambda h: (h, 0, 0))],
        out_specs=[pl.BlockSpec((S, HEAD_DIM), lambda h: (0, h)), per_head],
        out_shape=[jax.ShapeDtypeStruct((S, n_fox * HEAD_DIM), F32), jax.ShapeDtypeStruct((n_fox, S, LANES), F32)],
        compiler_params=pltpu.CompilerParams(dimension_semantics=("parallel",),
                                             vmem_limit_bytes=_vmem(16 * S * HEAD_DIM * 4 + 12 * t * t * 4)),
    )(proj, proj, proj, cum_row)


def _fox_bwd(proj, o, do, lse_b, cum_row, n_fox):
    S = proj.shape[0]
    t = _fit(FOX_TILE, S)
    nq = S // t
    scale = HEAD_DIM ** -0.5

    def body(q_ref, k_ref, v_ref, o_ref, do_ref, lse_ref, ck_ref, dq_ref, dk_ref, dv_ref, dc_ref, dcq_ref,
             dq_acc, delta_ref):
        dq_acc[...] = jnp.zeros_like(dq_acc)
        dcq_ref[...] = jnp.zeros_like(dcq_ref)

        def delta_block(qi, _):
            q0 = pl.multiple_of(qi * t, t)
            d = jnp.sum(do_ref[pl.ds(q0, t), :] * o_ref[pl.ds(q0, t), :], axis=-1, keepdims=True)
            delta_ref[pl.ds(q0, t), :] = jnp.broadcast_to(d, (t, LANES))
            return 0

        lax.fori_loop(0, nq, delta_block, 0)

        def kv_block(j, _):
            k0 = pl.multiple_of(j * t, t)
            k = k_ref[pl.ds(k0, t), :]
            v = v_ref[pl.ds(k0, t), :]
            ck2 = ck_ref[0, :, pl.ds(k0, t)] * LOG2E

            def q_block(qi, carry, masked):
                dk, dv, dc = carry
                q0 = pl.multiple_of(qi * t, t)
                q = q_ref[pl.ds(q0, t), :]
                dov = do_ref[pl.ds(q0, t), :].astype(BF16)
                p = jnp.exp2(_fox_scores(q, k, ck2, masked, t) - lse_ref[0, pl.ds(q0, t), :][:, :1])
                dp = lax.dot_general(dov, v, _NT, preferred_element_type=F32)
                ds = p * (dp - delta_ref[pl.ds(q0, t), :][:, :1])
                dsb = ds.astype(BF16)
                dv = dv + lax.dot_general(p.astype(BF16), dov, _TN, preferred_element_type=F32)
                dk = dk + lax.dot_general(dsb, q, _TN, preferred_element_type=F32)
                dq_acc[pl.ds(q0, t), :] += jnp.dot(dsb, k, preferred_element_type=F32)
                dc = dc - jnp.sum(ds, axis=0, keepdims=True)
                dcq_ref[0, pl.ds(q0, t), :] += jnp.broadcast_to(jnp.sum(ds, axis=1, keepdims=True), (t, LANES))
                return dk, dv, dc

            init = (jnp.zeros((t, HEAD_DIM), F32), jnp.zeros((t, HEAD_DIM), F32), jnp.zeros((1, t), F32))
            carry = q_block(j, init, True)
            dk, dv, dc = lax.fori_loop(j + 1, nq, lambda qi, cr: q_block(qi, cr, False), carry)
            dk_ref[pl.ds(k0, t), :] = (dk * scale).astype(BF16)
            dv_ref[pl.ds(k0, t), :] = dv.astype(BF16)
            dc_ref[0, :, pl.ds(k0, t)] = dc
            return 0

        lax.fori_loop(0, nq, kv_block, 0)
        dq_ref[...] = (dq_acc[...] * scale).astype(BF16)

    col = lambda off: pl.BlockSpec((S, HEAD_DIM), lambda h: (0, off + h))
    per_head = pl.BlockSpec((1, S, LANES), lambda h: (h, 0, 0))
    row = pl.BlockSpec((1, 1, S), lambda h: (h, 0, 0))
    grad = jax.ShapeDtypeStruct((S, n_fox * HEAD_DIM), BF16)
    return pl.pallas_call(
        body, name="fox_bwd", grid=(n_fox,),
        in_specs=[col(0), col(n_fox), col(2 * n_fox), col(0), col(0), per_head, row],
        out_specs=[col(0), col(0), col(0), row, per_head],
        out_shape=[grad, grad, grad, jax.ShapeDtypeStruct((n_fox, 1, S), F32), jax.ShapeDtypeStruct((n_fox, S, LANES), F32)],
        scratch_shapes=[pltpu.VMEM((S, HEAD_DIM), F32), pltpu.VMEM((S, LANES), F32)],
        compiler_params=pltpu.CompilerParams(dimension_semantics=("parallel",),
                                             vmem_limit_bytes=_vmem(24 * S * HEAD_DIM * 4 + 16 * t * t * 4)),
    )(proj, proj, proj, o, do, lse_b, cum_row)


def _rope_tables(S):
    half = HEAD_DIM // 2
    inv_freq = 1.0 / (ROPE_THETA ** (jnp.arange(half, dtype=F32) * (2.0 / HEAD_DIM)))
    ang = jnp.arange(S).astype(F32)[:, None] * inv_freq[None, :]
    cos, sin = jnp.cos(ang), jnp.sin(ang)
    return jnp.concatenate([cos, cos], axis=-1), jnp.concatenate([-sin, sin], axis=-1)


def _rope(name, src, first_block, n_blocks, cos, sin_signed):
    S = src.shape[0]

    def body(x_ref, cos_ref, sin_ref, o_ref):
        xv = x_ref[...].astype(F32)
        o_ref[...] = (xv * cos_ref[...] + pltpu.roll(xv, HEAD_DIM // 2, 1) * sin_ref[...]).astype(BF16)

    table = pl.BlockSpec((S, HEAD_DIM), lambda n: (0, 0))
    return pl.pallas_call(
        body, name=name, grid=(n_blocks,),
        in_specs=[pl.BlockSpec((S, HEAD_DIM), lambda n: (0, first_block + n)), table, table],
        out_specs=pl.BlockSpec((S, HEAD_DIM), lambda n: (0, n)),
        out_shape=jax.ShapeDtypeStruct((S, n_blocks * HEAD_DIM), BF16),
        compiler_params=pltpu.CompilerParams(dimension_semantics=("parallel",),
                                             vmem_limit_bytes=_vmem(12 * S * HEAD_DIM * 4)),
    )(src, cos, sin_signed)


def _swa_tile(q_ref, kp_ref, kc_ref, n, group, scale):
    B = SWA_BLOCK
    qs = jnp.concatenate([q_ref[:, g * HEAD_DIM:(g + 1) * HEAD_DIM] for g in range(group)], axis=0)
    kcat = jnp.concatenate([kp_ref[...], kc_ref[...]], axis=0)
    s = lax.dot_general(qs, kcat, _NT, preferred_element_type=F32) * scale
    qi = lax.broadcasted_iota(jnp.int32, (group * B, 2 * B), 0) % B
    kj = lax.broadcasted_iota(jnp.int32, (group * B, 2 * B), 1)
    diff = qi + B - kj
    mask = (diff >= 0) & (diff < B) & ((n * B + kj - B) >= 0)
    return qs, kcat, jnp.where(mask, s, NEG)


def _swa_sink_col(sink_ref, kv, group):
    head = lax.broadcasted_iota(jnp.int32, (group * SWA_BLOCK, 1), 0) // SWA_BLOCK
    col = jnp.zeros((group * SWA_BLOCK, 1), F32)
    for g in range(group):
        col = jnp.where(head == g, sink_ref[kv * group + g], col)
    return col


def _swa_specs(n_kv, group, q_first, k_first, v_first):
    B = SWA_BLOCK
    prev = lambda n: jnp.maximum(n - 1, 0)
    return [
        pl.BlockSpec((B, group * HEAD_DIM), lambda kv, n: (n, q_first + kv)),
        pl.BlockSpec((B, HEAD_DIM), lambda kv, n: (prev(n), k_first + kv)),
        pl.BlockSpec((B, HEAD_DIM), lambda kv, n: (n, k_first + kv)),
        pl.BlockSpec((B, HEAD_DIM), lambda kv, n: (prev(n), v_first + kv)),
        pl.BlockSpec((B, HEAD_DIM), lambda kv, n: (n, v_first + kv)),
    ]


def _swa_fwd(rq, proj, v_first, sinks, n_q, n_kv):
    S = rq.shape[0]
    B = SWA_BLOCK
    group = n_q // n_kv
    scale = HEAD_DIM ** -0.5

    def body(q_ref, kp_ref, kc_ref, vp_ref, vc_ref, sink_ref, o_ref, lse_ref):
        kv, n = pl.program_id(0), pl.program_id(1)
        _, _, s = _swa_tile(q_ref, kp_ref, kc_ref, n, group, scale)
        sink = _swa_sink_col(sink_ref, kv, group)
        m = jnp.maximum(jnp.max(s, axis=-1, keepdims=True), sink)
        p = jnp.exp(s - m)
        denom = jnp.sum(p, axis=-1, keepdims=True) + jnp.exp(sink - m)
        vcat = jnp.concatenate([vp_ref[...], vc_ref[...]], axis=0)
        o = jnp.dot((p / denom).astype(BF16), vcat, preferred_element_type=F32)
        lse = m + jnp.log(denom)
        for g in range(group):
            o_ref[:, g * HEAD_DIM:(g + 1) * HEAD_DIM] = o[g * B:(g + 1) * B, :]
            lse_ref[0, :, g * LANES:(g + 1) * LANES] = jnp.broadcast_to(lse[g * B:(g + 1) * B, :], (B, LANES))

    specs = _swa_specs(n_kv, group, 0, n_q, v_first)
    q_blk = pl.BlockSpec((B, group * HEAD_DIM), lambda kv, n: (n, kv))
    return pl.pallas_call(
        body, name="swa_fwd", grid=(n_kv, S // B),
        in_specs=specs + [pl.BlockSpec(memory_space=pltpu.SMEM)],
        out_specs=[q_blk, pl.BlockSpec((1, B, group * LANES), lambda kv, n: (kv, n, 0))],
        out_shape=[jax.ShapeDtypeStruct((S, n_q * HEAD_DIM), F32), jax.ShapeDtypeStruct((n_kv, S, group * LANES), F32)],
        compiler_params=pltpu.CompilerParams(dimension_semantics=("parallel", "arbitrary")),
    )(rq, rq, rq, proj, proj, sinks)


def _swa_bwd(rq, proj, v_first, sinks, o, do, do_first, lse_b, n_q, n_kv):
    S = rq.shape[0]
    B = SWA_BLOCK
    group = n_q // n_kv
    scale = HEAD_DIM ** -0.5

    def body(q_ref, kp_ref, kc_ref, vp_ref, vc_ref, o_ref, do_ref, lse_ref, sink_ref,
             dq_ref, dk_ref, dv_ref, dsink_ref):
        kv, n = pl.program_id(0), pl.program_id(1)

        @pl.when(n == 0)
        def _():
            dk_ref[...] = jnp.zeros_like(dk_ref)
            dv_ref[...] = jnp.zeros_like(dv_ref)
            dsink_ref[...] = jnp.zeros_like(dsink_ref)

        qs, kcat, s = _swa_tile(q_ref, kp_ref, kc_ref, n, group, scale)
        sink = _swa_sink_col(sink_ref, kv, group)
        stack = lambda ref, w: jnp.concatenate([ref[:, g * w:(g + 1) * w] for g in range(group)], axis=0)
        lse = jnp.concatenate([lse_ref[0, :, g * LANES:g * LANES + 1] for g in range(group)], axis=0)
        do32 = stack(do_ref, HEAD_DIM)
        delta = jnp.sum(do32 * stack(o_ref, HEAD_DIM), axis=-1, keepdims=True)
        dov = do32.astype(BF16)
        p = jnp.exp(s - lse)
        vcat = jnp.concatenate([vp_ref[...], vc_ref[...]], axis=0)
        dp = lax.dot_general(dov, vcat, _NT, preferred_element_type=F32)
        ds = p * (dp - delta)
        dsb = ds.astype(BF16)
        dq = jnp.dot(dsb, kcat, preferred_element_type=F32) * scale
        for g in range(group):
            dq_ref[:, g * HEAD_DIM:(g + 1) * HEAD_DIM] = dq[g * B:(g + 1) * B, :].astype(BF16)
        dkcat = lax.dot_general(dsb, qs, _TN, preferred_element_type=F32) * scale
        dvcat = lax.dot_general(p.astype(BF16), dov, _TN, preferred_element_type=F32)
        prev0 = pl.multiple_of(jnp.maximum(n - 1, 0) * B, B)
        cur0 = pl.multiple_of(n * B, B)
        dk_ref[0, pl.ds(prev0, B), :] += dkcat[:B, :]
        dk_ref[0, pl.ds(cur0, B), :] += dkcat[B:, :]
        dv_ref[0, pl.ds(prev0, B), :] += dvcat[:B, :]
        dv_ref[0, pl.ds(cur0, B), :] += dvcat[B:, :]
        dsk = -jnp.exp(sink - lse) * delta
        lane = lax.broadcasted_iota(jnp.int32, (1, LANES), 1)
        row = jnp.zeros((1, LANES), F32)
        for g in range(group):
            row = row + jnp.where(lane == g, jnp.sum(dsk[g * B:(g + 1) * B, :]), 0.0)
        dsink_ref[0, 0:1, :] += row

    specs = _swa_specs(n_kv, group, 0, n_q, v_first)
    q_blk = pl.BlockSpec((B, group * HEAD_DIM), lambda kv, n: (n, kv))
    acc = pl.BlockSpec((1, S, HEAD_DIM), lambda kv, n: (kv, 0, 0))
    return pl.pallas_call(
        body, name="swa_bwd", grid=(n_kv, S // B),
        in_specs=specs + [q_blk, pl.BlockSpec((B, group * HEAD_DIM), lambda kv, n: (n, do_first + kv)),
                          pl.BlockSpec((1, B, group * LANES), lambda kv, n: (kv, n, 0)),
                          pl.BlockSpec(memory_space=pltpu.SMEM)],
        out_specs=[q_blk, acc, acc, pl.BlockSpec((1, 8, LANES), lambda kv, n: (kv, 0, 0))],
        out_shape=[jax.ShapeDtypeStruct((S, n_q * HEAD_DIM), BF16), jax.ShapeDtypeStruct((n_kv, S, HEAD_DIM), F32),
                   jax.ShapeDtypeStruct((n_kv, S, HEAD_DIM), F32), jax.ShapeDtypeStruct((n_kv, 8, LANES), F32)],
        compiler_params=pltpu.CompilerParams(dimension_semantics=("parallel", "arbitrary")),
    )(rq, rq, rq, proj, proj, o, do, lse_b, sinks)


def _adamw(w, g, m, v):
    m = ADAM_B1 * m + (1.0 - ADAM_B1) * g
    v = ADAM_B2 * v + (1.0 - ADAM_B2) * (g * g)
    m_hat = m / (1.0 - ADAM_B1 ** ADAM_STEP)
    v_hat = v / (1.0 - ADAM_B2 ** ADAM_STEP)
    delta = -ADAM_LR * (m_hat / (jnp.sqrt(v_hat) + ADAM_EPS) + ADAM_WD * w)
    return delta, m, v


def _mod_fwd(cond_in, w_mod, b_shard):
    R, D = cond_in.shape
    cols = w_mod.shape[1]
    tn = _fit(512, cols)

    def body(c_ref, w_ref, b_ref, o_ref):
        cv = c_ref[...]
        cond = (cv / (1.0 + jnp.exp(-cv))).astype(BF16)
        o_ref[...] = jnp.dot(cond, w_ref[...].astype(BF16), preferred_element_type=F32) + b_ref[...]

    return pl.pallas_call(
        body, name="mod_fwd", grid=(cols // tn,),
        in_specs=[pl.BlockSpec((R, D), lambda j: (0, 0)), pl.BlockSpec((D, tn), lambda j: (0, j)),
                  pl.BlockSpec((1, tn), lambda j: (0, j))],
        out_specs=pl.BlockSpec((R, tn), lambda j: (0, j)),
        out_shape=jax.ShapeDtypeStruct((R, cols), F32),
        compiler_params=pltpu.CompilerParams(dimension_semantics=("parallel",), vmem_limit_bytes=_vmem(3 * D * tn * 4)),
    )(cond_in, w_mod, b_shard)


def _mod_update(c_t, dmod, w, m, v):
    D, nb = c_t.shape
    cols = w.shape[1]
    tr = _fit(128, D)

    def body(c_ref, d_ref, w_ref, m_ref, v_ref, g_ref, dl_ref, nm_ref, nv_ref):
        cv = c_ref[...]
        cond = cv / (1.0 + jnp.exp(-cv))
        g = jnp.zeros((tr, cols), F32)
        for b in range(nb):
            g = g + cond[:, b:b + 1] * d_ref[b:b + 1, :]
        g_ref[...] = g
        dl_ref[...], nm_ref[...], nv_ref[...] = _adamw(w_ref[...], g, m_ref[...], v_ref[...])

    blk = pl.BlockSpec((tr, cols), lambda r: (r, 0))
    out = jax.ShapeDtypeStruct((D, cols), F32)
    return pl.pallas_call(
        body, name="mod_update", grid=(D // tr,),
        in_specs=[pl.BlockSpec((tr, nb), lambda r: (r, 0)), pl.BlockSpec((nb, cols), lambda r: (0, 0)), blk, blk, blk],
        out_specs=[blk] * 4, out_shape=[out] * 4,
        compiler_params=pltpu.CompilerParams(dimension_semantics=("parallel",), vmem_limit_bytes=_vmem(18 * tr * cols * 4)),
    )(c_t, dmod, w, m, v)


def _small_update(stacked, w, m, v):
    R, C = w.shape

    def body(s_ref, w_ref, m_ref, v_ref, g_ref, dl_ref, nm_ref, nv_ref):
        g = s_ref[0:R, :]
        for d in range(1, N_DEV):
            g = g + s_ref[d * R:(d + 1) * R, :]
        g_ref[...] = g
        dl_ref[...], nm_ref[...], nv_ref[...] = _adamw(w_ref[...], g, m_ref[...], v_ref[...])

    return pl.pallas_call(body, name="small_update", out_shape=[jax.ShapeDtypeStruct((R, C), F32)] * 4)(stacked, w, m, v)


def _place():
    return lax.axis_index("x"), lax.axis_index("y"), lax.axis_index("c")


def _allgather8(name, block):
    m_per, n = block.shape

    def body(x_ref, out_ref, token_ref, send_sems, recv_sems, local_sem):
        token_ref[...] = jnp.zeros_like(token_ref)
        x, y, c = _place()
        me, sibling = (x, y, c), (x, y, 1 - c)
        chips = [(1 - x, y), (x, 1 - y), (1 - x, 1 - y)]

        def rows(px, py, pc):
            return out_ref.at[pl.ds((4 * px + 2 * py + pc) * m_per, m_per), :]

        def copy(k, blk, to, src=None):
            return pltpu.make_async_remote_copy(
                src_ref=rows(*blk) if src is None else src, dst_ref=rows(*blk),
                send_sem=send_sems.at[k], recv_sem=recv_sems.at[k], device_id=to, device_id_type=MESH)

        mine = pltpu.make_async_copy(x_ref, rows(*me), local_sem)
        mine.start()
        first = [copy(0, me, sibling, src=x_ref)]
        first += [copy(1 + j, me, (*chip, c), src=x_ref) for j, chip in enumerate(chips)]
        for cp in first:
            cp.start()
        passed = [copy(4 + j, (*chip, c), sibling) for j, chip in enumerate(chips)]
        for j, chip in enumerate(chips):
            copy(1 + j, (*chip, c), me).wait_recv()
            passed[j].start()
        copy(0, sibling, me).wait_recv()
        for j, chip in enumerate(chips):
            copy(4 + j, (*chip, 1 - c), me).wait_recv()
        for cp in first + passed:
            cp.wait_send()
        mine.wait()

    vmem = pl.BlockSpec(memory_space=pltpu.VMEM)
    return pl.pallas_call(
        body, name=name,
        out_shape=[jax.ShapeDtypeStruct((N_DEV * m_per, n), block.dtype), jax.ShapeDtypeStruct((8, LANES), F32)],
        in_specs=[vmem], out_specs=[vmem, vmem],
        scratch_shapes=[pltpu.SemaphoreType.DMA((7,)), pltpu.SemaphoreType.DMA((7,)), pltpu.SemaphoreType.DMA],
    )(block)


_ANY = pl.BlockSpec(memory_space=pl.ANY)


def _half(ref, c, rows):
    return ref.at[pl.ds(c * (rows // 2), rows // 2), :]


_HBM = pl.BlockSpec(memory_space=pltpu.HBM)
_SEM = pl.BlockSpec(memory_space=pltpu.SEMAPHORE)
_EFFECT = pltpu.SideEffectType.DATAFLOW_SIDE_EFFECTING


def _ici_start(name, srcs, land_shapes, plan, per_source=3, after=None):
    ns, nl = len(srcs), len(land_shapes)
    n_copies = per_source * ns
    n_in = ns + nl + (after is not None)

    def body(*refs):
        src_refs, land_refs = refs[:ns], refs[ns:ns + nl]
        send_sems, recv_sems = refs[n_in], refs[n_in + 1]
        token = refs[-1]
        for n, (src, dst, peer, _) in enumerate(plan(src_refs, land_refs)):
            pltpu.make_async_remote_copy(src_ref=src, dst_ref=dst, send_sem=send_sems.at[n], recv_sem=recv_sems.at[n],
                                         device_id=peer, device_id_type=MESH).start()
        token[...] = jnp.zeros_like(token)

    lands = [lax.empty(s.shape, s.dtype) for s in land_shapes]
    out = pl.pallas_call(
        body, name=name,
        out_shape=(pltpu.SemaphoreType.DMA((n_copies,)), pltpu.SemaphoreType.DMA((n_copies,)),
                   *[pltpu.HBM(a.shape, a.dtype) for a in list(srcs) + lands], jax.ShapeDtypeStruct((8, LANES), F32)),
        in_specs=[_HBM] * (ns + nl) + [_ANY] * (after is not None),
        out_specs=(_SEM, _SEM, *[_HBM] * (ns + nl), pl.BlockSpec(memory_space=pltpu.VMEM)),
        input_output_aliases={n: 2 + n for n in range(ns + nl)},
        compiler_params=pltpu.CompilerParams(has_side_effects=_EFFECT),
    )(*[pltpu.with_memory_space_constraint(a, pltpu.HBM) for a in list(srcs) + lands],
      *([] if after is None else [after]))
    return out[0], out[1], list(out[2:2 + ns]), list(out[2 + ns:2 + ns + nl]), out[-1]


def _ici_wait(name, send_sems, recv_sems, srcs, lands, plan, after):
    ns, nl = len(srcs), len(lands)
    after = list(after) if isinstance(after, (list, tuple)) else [after]

    def body(*refs):
        src_refs, land_refs = refs[:ns], refs[ns:ns + nl]
        send_sems, recv_sems = refs[ns + nl], refs[ns + nl + 1]
        for n, (src, _, peer, mine) in enumerate(plan(src_refs, land_refs)):
            cp = pltpu.make_async_remote_copy(src_ref=src, dst_ref=mine, send_sem=send_sems.at[n],
                                              recv_sem=recv_sems.at[n], device_id=peer, device_id_type=MESH)
            cp.wait_send()
            cp.wait_recv()

    out = pl.pallas_call(
        body, name=name, out_shape=[pltpu.HBM(a.shape, a.dtype) for a in list(srcs) + list(lands)],
        in_specs=[_HBM] * (ns + nl) + [_SEM, _SEM] + [_ANY] * len(after), out_specs=[_HBM] * (ns + nl),
        input_output_aliases={n: n for n in range(ns + nl)},
        compiler_params=pltpu.CompilerParams(has_side_effects=_EFFECT),
    )(*srcs, *lands, send_sems, recv_sems, *after)
    return list(out[:ns]), list(out[ns:])


def _own_slab(name, chip, w, after):
    R, C = w.shape
    tr, tc = _tiles(R, C)
    tied = [] if after is None else [after]

    def body(chip_ref, w_ref, *rest):
        stack_ref, token_ref = rest[-2:]
        stack_ref[0] = w_ref[...].astype(BF16)
        token_ref[...] = jnp.zeros_like(token_ref)

    small = pl.BlockSpec((8, LANES), lambda r, q, chip_ref: (0, 0))
    grid_spec = pltpu.PrefetchScalarGridSpec(
        num_scalar_prefetch=1, grid=(R // tr, C // tc),
        in_specs=[pl.BlockSpec((tr, tc), lambda r, q, chip_ref: (r, q))] + [small] * len(tied),
        out_specs=[pl.BlockSpec((1, tr, tc), lambda r, q, chip_ref: (chip_ref[0], r, q)), small])
    return pl.pallas_call(
        body, name=name, grid_spec=grid_spec,
        out_shape=[jax.ShapeDtypeStruct((N_CHIPS, R, C), BF16), jax.ShapeDtypeStruct((8, LANES), F32)],
        compiler_params=pltpu.CompilerParams(dimension_semantics=("arbitrary", "arbitrary")),
    )(chip, w, *tied)


def _gather_plan(src_refs, land_refs):
    x, y, c = _place()
    copies = []
    for stack in src_refs:
        R = stack.shape[1]
        own = _half(stack.at[2 * x + y], c, R)
        for cx, cy in [(1 - x, y), (x, 1 - y), (1 - x, 1 - y)]:
            copies.append((own, own, (cx, cy, c), _half(stack.at[2 * cx + cy], c, R)))
    return copies


def _pass_plan(src_refs, land_refs):
    x, y, c = _place()
    copies = []
    for land in src_refs:
        R = land.shape[1]
        for cx, cy in [(1 - x, y), (x, 1 - y), (1 - x, 1 - y)]:
            slot = land.at[2 * cx + cy]
            copies.append((_half(slot, c, R), _half(slot, c, R), (x, y, 1 - c), _half(slot, 1 - c, R)))
    return copies


def _share_plan(src_refs, land_refs):
    x, y, c = _place()
    return [(h, land, (x, y, 1 - c), land) for h, land in zip(src_refs, land_refs)]


def _pass_to_sibling(name, lands):
    nw = len(lands)

    def body(*refs):
        ins, outs = refs[:nw], refs[nw:2 * nw]
        send_sems, recv_sems = refs[2 * nw:]
        x, y, c = _place()
        chips = [(1 - x, y), (x, 1 - y), (1 - x, 1 - y)]
        copies = []
        for k in range(nw):
            R = ins[k].shape[1]
            for j, (cx, cy) in enumerate(chips):
                cp = pltpu.make_async_remote_copy(
                    src_ref=_half(ins[k].at[2 * cx + cy], c, R), dst_ref=_half(outs[k].at[2 * cx + cy], c, R),
                    send_sem=send_sems.at[3 * k + j], recv_sem=recv_sems.at[3 * k + j],
                    device_id=(x, y, 1 - c), device_id_type=MESH)
                cp.start()
                copies.append(cp)
        for k in range(nw):
            R = ins[k].shape[1]
            for j, (cx, cy) in enumerate(chips):
                pltpu.make_async_remote_copy(
                    src_ref=_half(ins[k].at[2 * cx + cy], c, R), dst_ref=_half(outs[k].at[2 * cx + cy], 1 - c, R),
                    send_sem=send_sems.at[3 * k + j], recv_sem=recv_sems.at[3 * k + j],
                    device_id=(x, y, 1 - c), device_id_type=MESH).wait_recv()
        for cp in copies:
            cp.wait_send()

    return pl.pallas_call(
        body, name=name, out_shape=[jax.ShapeDtypeStruct(a.shape, a.dtype) for a in lands],
        in_specs=[_ANY] * nw, out_specs=[_ANY] * nw, input_output_aliases={k: k for k in range(nw)},
        scratch_shapes=[pltpu.SemaphoreType.DMA((3 * nw,)), pltpu.SemaphoreType.DMA((3 * nw,))],
    )(*lands)


def _tie(vec, token):
    return vec + token[0:1, 0:1]


def _lay_columns(name, arrays, pieces):
    S, dtype = arrays[0].shape[0], arrays[0].dtype
    out_w = sum(hi - lo for _, lo, hi in pieces)
    tr = _fit(ROW_TILE, S)

    def body(*refs):
        o_ref = refs[-1]
        rows = [r[...] for r in refs[:-1]]
        o_ref[...] = jnp.concatenate(
            [jnp.zeros((tr, hi - lo), dtype) if k is None else rows[k][:, lo:hi] for k, lo, hi in pieces], axis=1)

    need = 2 * tr * (out_w + sum(a.shape[1] for a in arrays)) * dtype.itemsize
    return pl.pallas_call(
        body, name=name, grid=(S // tr,), in_specs=[pl.BlockSpec((tr, a.shape[1]), lambda i: (i, 0)) for a in arrays],
        out_specs=pl.BlockSpec((tr, out_w), lambda i: (i, 0)), out_shape=jax.ShapeDtypeStruct((S, out_w), dtype),
        compiler_params=pltpu.CompilerParams(dimension_semantics=("parallel",), vmem_limit_bytes=_vmem(2 * need)),
    )(*arrays)


ROW_ALIGN = 16
TILE_ELEMS = 512 * 1024


def _tiles(rows, cols):
    fits = [t for t in range(ROW_ALIGN, min(rows, 256) + 1, ROW_ALIGN) if rows % t == 0]
    tr = fits[-1] if fits and fits[-1] >= 64 else rows
    tc = cols
    while tr * tc > TILE_ELEMS and tc % (2 * LANES) == 0:
        tc //= 2
    return tr, tc


def _scatter_plan(src_refs, land_refs):
    x, y, c = _place()
    copies = []
    for p, land in zip(src_refs, land_refs):
        for j, (cx, cy) in enumerate([(1 - x, y), (x, 1 - y), (1 - x, 1 - y)]):
            copies.append((p.at[2 * cx + cy], land.at[j], (cx, cy, c), land.at[j]))
    return copies


def _chip_add(name, chip, sums, recv):
    _, H, C = sums.shape
    tr, tc = _tiles(H, C)

    def body(chip_ref, p_ref, r_ref, o_ref):
        total = p_ref[0].astype(F32)
        for j in range(3):
            total = total + r_ref[j].astype(F32)
        o_ref[...] = total

    grid_spec = pltpu.PrefetchScalarGridSpec(
        num_scalar_prefetch=1, grid=(H // tr, C // tc),
        in_specs=[pl.BlockSpec((1, tr, tc), lambda r, q, chip_ref: (chip_ref[0], r, q)),
                  pl.BlockSpec((3, tr, tc), lambda r, q, chip_ref: (0, r, q))],
        out_specs=pl.BlockSpec((tr, tc), lambda r, q, chip_ref: (r, q)))
    return pl.pallas_call(
        body, name=name, grid_spec=grid_spec, out_shape=jax.ShapeDtypeStruct((H, C), F32),
        compiler_params=pltpu.CompilerParams(dimension_semantics=("parallel", "parallel")),
    )(chip, sums, recv)


def _pair_share(name, halves):
    nw = len(halves)

    def body(*refs):
        hs, outs = refs[:nw], refs[nw:2 * nw]
        send_sems, recv_sems = refs[2 * nw:]
        x, y, c = _place()
        copies = []
        for k in range(nw):
            cp = pltpu.make_async_remote_copy(
                src_ref=hs[k], dst_ref=outs[k], send_sem=send_sems.at[k], recv_sem=recv_sems.at[k],
                device_id=(x, y, 1 - c), device_id_type=MESH)
            cp.start()
            copies.append(cp)
        for cp in copies:
            cp.wait()

    return pl.pallas_call(
        body, name=name,
        out_shape=[jax.ShapeDtypeStruct(h.shape, h.dtype) for h in halves],
        in_specs=[_ANY] * nw, out_specs=[_ANY] * nw,
        scratch_shapes=[pltpu.SemaphoreType.DMA((nw,)), pltpu.SemaphoreType.DMA((nw,))],
    )(*halves)


def _adam_halves(name, core, w, g_own, g_other, m, v, out_rows=None):
    R, C = w.shape
    H = R // 2
    tr, tc = _tiles(H, C)
    nr, nc = H // tr, C // tc

    def body(core_ref, w_ref, go_ref, gr_ref, m_ref, v_ref, g_ref, dl_ref, nm_ref, nv_ref):
        own = (pl.program_id(0) // nr) == core_ref[0]
        g = jnp.where(own, go_ref[...], gr_ref[...])
        g_ref[...] = g
        dl_ref[...], nm_ref[...], nv_ref[...] = _adamw(w_ref[...], g, m_ref[...], v_ref[...])

    blk = pl.BlockSpec((tr, tc), lambda r, q, core_ref: (r, q))

    def half_spec(is_own):
        def index(r, q, core_ref):
            mine = ((r // nr) == core_ref[0]) == is_own
            done = is_own == (core_ref[0] == 0)
            return (jnp.where(mine, r % nr, jnp.where(done, nr - 1, 0)), jnp.where(mine, q, jnp.where(done, nc - 1, 0)))
        return pl.BlockSpec((tr, tc), index)
    out_rows = R if out_rows is None else out_rows
    assert R - tr < out_rows <= R, (R, tr, out_rows)
    out = jax.ShapeDtypeStruct((out_rows, C), F32)
    grid_spec = pltpu.PrefetchScalarGridSpec(
        num_scalar_prefetch=1, grid=(R // tr, nc), in_specs=[blk, half_spec(True), half_spec(False), blk, blk],
        out_specs=[blk] * 4)
    return pl.pallas_call(
        body, name=name, grid_spec=grid_spec, out_shape=[out] * 4,
        compiler_params=pltpu.CompilerParams(dimension_semantics=("parallel", "parallel"),
                                             vmem_limit_bytes=_vmem(20 * tr * tc * 4)),
    )(core, w, g_own, g_other, m, v)


def kernel(x, c, w_mod, b_mod, g_pre_mix, g_post_mix, w_in, b_forget, swa_sinks, w_out, g_pre_mlp, g_post_mlp, w_up, w_down, loss_target, m_w_mod, m_b_mod, m_g_pre_mix, m_g_post_mix, m_w_in, m_b_forget, m_swa_sinks, m_w_out, m_g_pre_mlp, m_g_post_mlp, m_w_up, m_w_down, v_w_mod, v_b_mod, v_g_pre_mix, v_g_post_mix, v_w_in, v_b_forget, v_swa_sinks, v_w_out, v_g_pre_mlp, v_g_post_mlp, v_w_up, v_w_down):
    S, D = x.shape[1], x.shape[2]
    n_heads = D // HEAD_DIM
    n_fox = n_heads // 2
    n_swa = n_heads - n_fox
    n_kv = max(1, n_swa // 4)
    fox_w, swa_w, kv_w = n_fox * HEAD_DIM, n_swa * HEAD_DIM, n_kv * HEAD_DIM
    main_w = 3 * fox_w + swa_w + 2 * kv_w
    in_w = main_w + n_fox
    mod_cols = w_mod.shape[2]

    ax, ay, ac = _place()
    chip = 2 * ax + ay
    dev = 2 * chip + ac
    chip_arr = jnp.reshape(chip, (1,)).astype(jnp.int32)
    core_arr = jnp.reshape(ac, (1,)).astype(jnp.int32)

    x2, tgt = x[0], loss_target[0]

    in_rows = in_w // N_CHIPS
    in_rows_pad = -(-in_rows // (2 * LANES)) * (2 * LANES)
    slab_w = N_CHIPS * in_rows_pad

    def rows_of(a):
        return jnp.pad(a[0].T, ((0, in_rows_pad - in_rows), (0, 0)))

    w_in_stack, token = _own_slab("own_slab_w_in", chip_arr, rows_of(w_in), None)

    c_all, _ = _allgather8("gather_c", _tie(c, token).reshape(8, D // 8))
    c_all = c_all.reshape(N_DEV, D)
    b_shard = lax.dynamic_slice_in_dim(b_mod, chip * mod_cols, mod_cols, axis=1)
    mod_shard = _mod_fwd(jnp.pad(c_all, ((0, 16 - N_DEV), (0, 0))), w_mod[0], b_shard)[:N_DEV]
    mod_all, token = _allgather8("gather_mod", mod_shard)
    mod_all = mod_all.reshape(N_CHIPS, 2, N_DEV, mod_cols)[:, 0]
    mod = lax.dynamic_index_in_dim(mod_all, dev, axis=1, keepdims=False).reshape(N_MOD, 1, D)
    sh_a, sc_a, gt_a, sh_m, sc_m, gt_m = [mod[n] for n in range(N_MOD)]

    def slab_cols(lo, hi):
        spans = []
        while lo < hi:
            s, r = divmod(lo, in_rows)
            n = min(hi - lo, in_rows - r)
            spans.append((s * in_rows_pad + r, s * in_rows_pad + r + n))
            lo += n
        return spans

    gate_lo = 3 * fox_w
    main_spans = slab_cols(0, gate_lo) + slab_cols(gate_lo + n_fox, in_w)
    (gate_first, gate_last), = slab_cols(gate_lo, gate_lo + n_fox)

    names = ["w_in", "w_out", "w_up", "w_down"]
    flights = {}
    for n, w in zip(names, [None, w_out[0], w_up[0], w_down[0]]):
        stack = w_in_stack if n == "w_in" else _own_slab("own_slab_" + n, chip_arr, w, token)[0]
        flights[n] = _ici_start("gather_start_" + n, [stack], [], _gather_plan, after=token)
        token = flights[n][4]
    sc_a = _tie(sc_a, token)

    def arrived(n, after):
        send, recv, stacks, _, _ = flights[n]
        stacks, _ = _ici_wait("gather_wait_" + n, send, recv, stacks, [], _gather_plan, after)
        return _ici_start("gather_pass_start_" + n, stacks, [], _pass_plan)

    def gathered(n, after, in_flight=None):
        if in_flight is None:
            send, recv, stacks, _, _ = flights[n]
            stacks, _ = _ici_wait("gather_wait_" + n, send, recv, stacks, [], _gather_plan, after)
            return _pass_to_sibling("gather_pass_" + n, stacks)[0]
        send, recv, stacks, _, _ = in_flight
        return _ici_wait("gather_pass_wait_" + n, send, recv, stacks, [], _pass_plan, after)[0][0]

    d_ff = N_CHIPS * w_up.shape[2]

    h = _pre_norm(x2, g_pre_mix, sc_a, sh_a)
    in_state = [rows_of(w_in)] + [rows_of(_tie(a, token)) for a in (m_w_in, v_w_in)]
    cos, sin_signed = _rope_tables(S)

    def pack(bm, gpm, gqm, gpl, gql, bf, sk):
        last = jnp.concatenate([bf, sk, jnp.zeros((1, D - n_fox - n_swa), F32)], axis=1)
        return jnp.concatenate([bm.reshape(N_MOD, D), gpm, gqm, gpl, gql, last, jnp.zeros((5, D), F32)], axis=0)

    small_state = [pack(b_mod, g_pre_mix, g_post_mix, g_pre_mlp, g_post_mlp, b_forget, swa_sinks),
                   pack(m_b_mod, m_g_pre_mix, m_g_post_mix, m_g_pre_mlp, m_g_post_mlp, m_b_forget, m_swa_sinks),
                   pack(v_b_mod, v_g_pre_mix, v_g_post_mix, v_g_pre_mlp, v_g_post_mlp, v_b_forget, v_swa_sinks)]
    ready = h[:8, :LANES].astype(F32) + cos[:8]
    w_slab_t = gathered("w_in", [ready] + in_state[1:] + small_state).reshape(slab_w, D)
    tm_p, tn_p = _fit(MM_TM, S), _fit(MM_TN if slab_w % MM_TN == 0 else MM_TN // 2, slab_w)
    win0 = gate_first // LANES * LANES
    win_j, win_off = divmod(win0, tn_p)
    assert win_off + 2 * LANES <= tn_p and gate_last - win0 <= 2 * LANES

    def proj_epilogue(acc, ex, outs):
        outs[0][...] = acc.astype(BF16)

        @pl.when(pl.program_id(1) == win_j)
        def _():
            outs[1][...] = acc[:, win_off:win_off + 2 * LANES]

    proj_slab, gate_win = _matmul(
        "in_proj", h, w_slab_t, "nt",
        [((S, slab_w), BF16, (tm_p, tn_p), lambda i, j: (i, j)), ((S, 2 * LANES), F32, (tm_p, 2 * LANES), lambda i, j: (i, 0))],
        proj_epilogue, tn=tn_p, revisits=True)
    proj = jnp.concatenate([proj_slab[:, lo:hi] for lo, hi in main_spans], axis=1)
    out_flight = arrived("w_out", proj_slab)
    fg = _tie(jnp.pad(gate_win[:, gate_first - win0:gate_last - win0], ((0, 0), (0, LANES - n_fox))), out_flight[4])
    b_pad = jnp.pad(b_forget, ((0, 0), (0, LANES - n_fox)))
    cum_row = _fox_gate_fwd(fg, b_pad)[:n_fox].reshape(n_fox, 1, S)
    fox_o, fox_lse = _fox_fwd(proj, cum_row, n_fox)

    rq = _rope("rope_fwd", proj, 3 * n_fox, n_swa + n_kv, cos, sin_signed)
    v_first = 3 * n_fox + n_swa + n_kv
    sinks = swa_sinks[0]
    swa_o, swa_lse = _swa_fwd(rq, proj, v_first, sinks, n_swa, n_kv)

    mixcat = jnp.concatenate([fox_o, swa_o], axis=1).astype(BF16)
    up_flight = arrived("w_up", mixcat)
    w_out_f = gathered("w_out", mixcat, out_flight).reshape(D, D)
    mix = _mm_plain("out_proj", mixcat, w_out_f, "nn", BF16, after=up_flight[4])
    x1, h2 = _post_mix(x2, mix, g_post_mix, gt_a, g_pre_mlp, sc_m, sh_m)
    w_up_f = gathered("w_up", h2, up_flight)

    tm_u, tn_u = _fit(MM_TM, S), _fit(MM_TN, d_ff)

    def up_epilogue(acc, ex, outs):
        outs[0][...] = acc.astype(BF16)
        r = jnp.maximum(acc, 0.0)
        outs[1][...] = (r * r).astype(BF16)

    ublk = ((S, d_ff), BF16, (tm_u, tn_u), lambda i, j: (i, j))
    u, a = _matmul("mlp_up", h2, w_up_f, "nn", [ublk, ublk], up_epilogue)
    w_down_f = gathered("w_down", a).reshape(d_ff, D)
    y = _mm_plain("mlp_down", a, w_down_f, "nn", BF16)

    dy, dout, loss_part, acc_mlp_post = _loss_and_post_mlp_bwd(x1, y, tgt, g_post_mlp, gt_m)

    def du_epilogue(acc, ex, outs):
        outs[0][...] = (acc * (2.0 * jnp.maximum(ex[0][...].astype(F32), 0.0))).astype(BF16)

    du = _matmul("mlp_down_bwd", dy, w_down_f, "nt", [ublk], du_epilogue,
                 extras=[(u, (tm_u, tn_u), lambda i, j: (i, j))])[0]
    def pair_send(tag, part):
        return _ici_start("grad_pair_start_" + tag, [part], [jax.ShapeDtypeStruct(part.shape, BF16)], _share_plan,
                          per_source=1)

    def pair_recv(tag, flight, after):
        send, recv, srcs, lands, _ = flight
        return _ici_wait("grad_pair_wait_" + tag, send, recv, srcs, lands, _share_plan, after)[1][0]

    def scatter_start(tag, sums, after=None):
        return _ici_start("grad_scatter_start_" + tag, sums,
                          [jax.ShapeDtypeStruct((3,) + p.shape[1:], BF16) for p in sums], _scatter_plan, after=after)

    def scatter_finish(tag, flight, after):
        send, recv, srcs, lands, _ = flight
        sums, received = _ici_wait("grad_scatter_wait_" + tag, send, recv, srcs, lands, _scatter_plan, after)
        return [_chip_add("chip_add_%s_%d" % (tag, k), chip_arr, p, r) for k, (p, r) in enumerate(zip(sums, received))]

    tm_g = _fit(MM_TM, D // 2)
    pair_down = pair_send("down", _grad_half("grad_w_down_a", core_arr, a, dy, N_CHIPS, 1, tm_g, True))
    pair_up = pair_send("up", _grad_half("grad_w_up_a", core_arr, h2, du, 1, N_CHIPS, tm_g, True, after=pair_down[4]))
    sum_down = _grad_half("grad_w_down_b", core_arr, a, dy, N_CHIPS, 1, tm_g, False,
                          recv=pair_recv("down", pair_down, pair_up[4]))
    sum_up = _grad_half("grad_w_up_b", core_arr, h2, du, 1, N_CHIPS, tm_g, False, recv=pair_recv("up", pair_up, sum_down))
    flight_mlp = scatter_start("mlp", [sum_up, sum_down])
    dh2 = _mm_plain("mlp_up_bwd", du, w_up_f, "nt", BF16, after=flight_mlp[4])
    dx1, dmix, acc_mid = _pre_mlp_and_post_mix_bwd(dh2, x1, dout, mix, _tie(g_pre_mlp, flight_mlp[4]), sc_m,
                                                   g_post_mix, gt_a)

    dmixcat = _mm_plain("out_proj_bwd", dmix, w_out_f, "nt", F32)

    fdq, fdk, fdv, dcum_row, dcum_q = _fox_bwd(proj, fox_o, dmixcat, fox_lse, cum_row, n_fox)
    dcum_k = jnp.pad(dcum_row.reshape(n_fox, S), ((0, LANES - n_fox), (0, 0)))
    dfg, db_forget = _fox_gate_bwd(dcum_k, dcum_q, fg, b_pad)

    group_w = (n_swa // n_kv) * HEAD_DIM
    sdq, sdk, sdv, dsink = _swa_bwd(rq, proj, v_first, sinks, swa_o, dmixcat, fox_w // group_w, swa_lse, n_swa, n_kv)
    drq = jnp.concatenate([sdq, jnp.transpose(sdk, (1, 0, 2)).reshape(S, kv_w).astype(BF16)], axis=1)
    d_sq_sk = _rope("rope_bwd", drq, 0, n_swa + n_kv, cos, -sin_signed)
    dsv = jnp.transpose(sdv, (1, 0, 2)).reshape(S, kv_w).astype(BF16)
    parts = [fdq, fdk, fdv, dfg, d_sq_sk, dsv]
    widths = [p.shape[1] for p in parts[:3]] + [n_fox] + [p.shape[1] for p in parts[4:]]
    starts = [sum(widths[:k]) for k in range(len(parts) + 1)]
    assert starts[3] == gate_lo and starts[-1] == in_w
    pieces = []
    for s in range(N_CHIPS):
        lo, hi = s * in_rows, (s + 1) * in_rows
        for k in range(len(parts)):
            first, last = max(lo, starts[k]), min(hi, starts[k + 1])
            if first < last:
                pieces.append((k, first - starts[k], last - starts[k]))
        pieces.append((None, 0, in_rows_pad - in_rows))
    dproj_slab = _lay_columns("dproj_slab_order", parts, pieces)

    tm_in, tm_out = in_rows_pad // 2, D // (2 * N_CHIPS)
    pair_in = pair_send("in", _grad_half("grad_w_in_a", core_arr, dproj_slab, h, N_CHIPS, 1, tm_in, True))
    pair_out = pair_send("out", _grad_half("grad_w_out_a", core_arr, mixcat, dmix, N_CHIPS, 1, tm_out, True,
                                           after=pair_in[4]))
    sum_in = _grad_half("grad_w_in_b", core_arr, dproj_slab, h, N_CHIPS, 1, tm_in, False,
                        recv=pair_recv("in", pair_in, pair_out[4]))
    sum_out = _grad_half("grad_w_out_b", core_arr, mixcat, dmix, N_CHIPS, 1, tm_out, False,
                         recv=pair_recv("out", pair_out, sum_in[0, :8, :LANES]))
    dh = _mm_plain("in_proj_bwd", dproj_slab, w_slab_t, "nn", BF16, tk=slab_w // 2,
                   after=sum_out[0, :8, :LANES].astype(F32))
    grad_x, acc_pre = _pre_mix_bwd(dh, x2, dx1, g_pre_mix, sc_a)

    zero_row = jnp.zeros((1, D), F32)
    tail = jnp.concatenate([db_forget[0:1, :n_fox], dsink[:, 0, :n_swa // n_kv].reshape(1, n_swa),
                            loss_part[0:1, 0:1], jnp.zeros((1, D - n_fox - n_swa - 1), F32)], axis=1)
    partial = jnp.concatenate([
        acc_pre[0:1], acc_pre[1:2], acc_mid[3:4], acc_mid[0:1], acc_mid[1:2], acc_mlp_post[0:1],
        acc_pre[2:3], acc_mid[4:5], acc_mid[2:3], acc_mlp_post[1:2], tail] + [zero_row] * 5, axis=0)
    gathered_small, token = _allgather8("gather_small_grads", partial)

    flight_mix = scatter_start("mix", [sum_in, sum_out], after=token)
    halves_mlp = scatter_finish("mlp", flight_mlp, flight_mix[4])
    share_up, share_down = [
        _ici_start("grad_share_start_" + n, [hv], [jax.ShapeDtypeStruct(hv.shape, F32)], _share_plan, per_source=1)
        for n, hv in zip(["up", "down"], halves_mlp)]

    def shared(tag, flight, after):
        send, recv, own, lands, _ = flight
        own, other = _ici_wait("grad_share_wait_" + tag, send, recv, own, lands, _share_plan, after)
        return own[0], other[0]

    def unpack(p):
        return {"b_mod": p[0:N_MOD].reshape(1, N_MOD * D), "g_pre_mix": p[6:7], "g_post_mix": p[7:8],
                "g_pre_mlp": p[8:9], "g_post_mlp": p[9:10], "b_forget": p[10:11, :n_fox],
                "swa_sinks": p[10:11, n_fox:n_fox + n_swa]}

    small_out = _small_update(gathered_small, _tie(small_state[0], share_down[4] + share_up[4]), small_state[1],
                              small_state[2])
    g_small, d_small, m_small, v_small = [unpack(p) for p in small_out]
    loss = small_out[0][N_MOD + 4, n_fox + n_swa]

    dmod_all = gathered_small.reshape(N_DEV, 16, D)[:, :N_MOD].reshape(N_DEV, N_MOD * D)
    dmod_shard = _tie(lax.dynamic_slice_in_dim(dmod_all, chip * mod_cols, mod_cols, axis=1), share_down[4])
    g_w_mod, d_w_mod, nm_w_mod, nv_w_mod = _mod_update(c_all.T, dmod_shard, w_mod[0], m_w_mod[0], v_w_mod[0])

    grads = dict(g_small, w_mod=g_w_mod[None])
    deltas = dict(d_small, w_mod=d_w_mod[None])
    new_m = dict(m_small, w_mod=nm_w_mod[None])
    new_v = dict(v_small, w_mod=nv_w_mod[None])
    weights = {"w_in": (w_in, m_w_in, v_w_in), "w_out": (w_out, m_w_out, v_w_out), "w_up": (w_up, m_w_up, v_w_up),
               "w_down": (w_down, m_w_down, v_w_down)}

    updated = {}

    def big_update(n, own, other):
        transposed = n == "w_in"
        w, m, v = in_state if transposed else [a[0] for a in weights[n]]
        outs = _adam_halves("adam_" + n, core_arr, w, own, other, m, v, out_rows=in_rows if transposed else None)
        updated[n] = outs[1][:8, :LANES]
        if transposed:
            outs = [o.T for o in outs]
        grads[n], deltas[n], new_m[n], new_v[n] = [o[None] for o in outs]

    big_update("w_down", *shared("down", share_down, d_w_mod[:8, :LANES] + small_out[1][:8, :LANES]))
    halves_mix = scatter_finish("mix", flight_mix, updated["w_down"] + d_w_mod[:8, :LANES])
    others_mix = _pair_share("grad_pair_share_mix", halves_mix)
    big_update("w_in", halves_mix[0], others_mix[0])
    big_update("w_out", halves_mix[1], others_mix[1])
    big_update("w_up", *shared("up", share_up, updated["w_out"] + updated["w_in"]))

    order = ["w_mod", "b_mod", "g_pre_mix", "g_post_mix", "w_in", "b_forget", "swa_sinks", "w_out", "g_pre_mlp",
             "g_post_mlp", "w_up", "w_down"]
    return (loss, grad_x[None], *[grads[n] for n in order], *[deltas[n] for n in order],
            *[new_m[n] for n in order], *[new_v[n] for n in order])
```

```python
import jax
import jax.numpy as jnp
from jax import lax
from jax.experimental import pallas as pl
from jax.experimental.pallas import tpu as pltpu

F32 = jnp.float32
BF16 = jnp.bfloat16
MESH = pl.DeviceIdType.MESH

HEAD_DIM = 128
SWA_BLOCK = 128
ROPE_THETA = 10000.0
NORM_EPS = 1e-6
NEG = -1e30
N_MOD = 6
ADAM_LR = 0.001
ADAM_B1 = 0.9
ADAM_B2 = 0.999
ADAM_EPS = 1e-08
ADAM_WD = 0.01
ADAM_STEP = 10
N_CHIPS = 4
N_DEV = 8
LANES = 128
VMEM_CAP = 60 * 1024 * 1024

_NN = (((1,), (0,)), ((), ()))
_NT = (((1,), (1,)), ((), ()))
_TN = (((0,), (0,)), ((), ()))


def _vmem(nbytes):
    return int(min(VMEM_CAP, nbytes * 5 // 4 + (4 << 20)))


def _nbytes(shape, dtype):
    n = 1
    for s in shape:
        n *= s
    return n * jnp.dtype(dtype).itemsize


def _fit(t, n):
    t = min(t, n)
    assert n % t == 0, (t, n)
    return t


MM_TM, MM_TN, MM_TK = 1024, 1024, 2048


def _matmul(name, a, b, mode, out_defs, epilogue, extras=(), tm=MM_TM, tn=MM_TN, tk=MM_TK, revisits=False,
            row_sel=None):
    stacked = b.ndim == 3
    b_rows, b_cols = b.shape[-2], b.shape[-1] * (b.shape[0] if stacked else 1)
    if mode == "nn":
        (M, K), (K2, N) = a.shape, (b_rows, b_cols)
    elif mode == "nt":
        (M, K), (N, K2) = a.shape, (b_rows, b_cols)
    else:
        (K, M), (K2, N) = a.shape, (b_rows, b_cols)
    assert K == K2 and not (stacked and mode == "tn"), (a.shape, b.shape, mode)
    tm = _fit(tm, M)
    tn = _fit(tn, b.shape[-1] if stacked and mode == "nn" else N)
    tk = _fit(tk, b.shape[-1] if stacked and mode == "nt" else K)
    nk = K // tk
    dims = {"nn": _NN, "nt": _NT, "tn": _TN}[mode]
    if row_sel is None:
        grid_m, a_row = M // tm, lambda i, *sel: i
    else:
        grid_m, a_row = row_sel[2], lambda i, *sel: row_sel[1](i, sel[0])
    a_spec = (pl.BlockSpec((tk, tm), lambda i, j, k, *sel: (k, a_row(i, *sel))) if mode == "tn"
              else pl.BlockSpec((tm, tk), lambda i, j, k, *sel: (a_row(i, *sel), k)))
    if stacked:
        per = b.shape[-1] // (tk if mode == "nt" else tn)
        b_spec = (pl.BlockSpec((1, tn, tk), lambda i, j, k, *sel: (k // per, j, k % per)) if mode == "nt"
                  else pl.BlockSpec((1, tk, tn), lambda i, j, k, *sel: (j // per, k, j % per)))
    else:
        b_spec = (pl.BlockSpec((tn, tk), lambda i, j, k, *sel: (j, k)) if mode == "nt"
                  else pl.BlockSpec((tk, tn), lambda i, j, k, *sel: (k, j)))
    n_ex, n_out = len(extras), len(out_defs)

    def body(*refs):
        if row_sel is not None:
            refs = refs[1:]
        a_ref, b_ref = refs[0], refs[1]
        ex = refs[2:2 + n_ex]
        outs = refs[2 + n_ex:2 + n_ex + n_out]
        b_blk = b_ref[0] if stacked else b_ref[...]
        prod = lax.dot_general(a_ref[...], b_blk, dims, preferred_element_type=F32)
        if nk == 1:
            epilogue(prod, ex, outs)
        else:
            acc_ref = refs[-1]
            k = pl.program_id(2)

            @pl.when(k == 0)
            def _():
                acc_ref[...] = prod

            @pl.when(k > 0)
            def _():
                acc_ref[...] += prod

            @pl.when(k == nk - 1)
            def _():
                epilogue(acc_ref[...], ex, outs)

    def wrap(f):
        return lambda i, j, k, *sel: f(i, j)

    in_specs = [a_spec, b_spec] + [pl.BlockSpec(blk, wrap(f)) for _, blk, f in extras]
    out_specs = [pl.BlockSpec(blk, wrap(f)) for _, _, blk, f in out_defs]
    out_shape = [jax.ShapeDtypeStruct(s, d) for s, d, _, _ in out_defs]
    need = 2 * (tm * tk + tk * tn) * a.dtype.itemsize + 3 * tm * tn * 4
    need += sum(2 * _nbytes(blk, arr.dtype) for arr, blk, _ in extras)
    need += sum(2 * _nbytes(blk, d) for _, d, blk, _ in out_defs)
    grid = (grid_m, N // tn, nk)
    scratch = [pltpu.VMEM((tm, tn), F32)] if nk > 1 else []
    params = pltpu.CompilerParams(
        dimension_semantics=("parallel", "arbitrary" if revisits else "parallel", "arbitrary"),
        vmem_limit_bytes=_vmem(need))
    operands = (a, b, *[arr for arr, _, _ in extras])
    if row_sel is None:
        return pl.pallas_call(body, name=name, grid=grid, in_specs=in_specs, out_specs=out_specs, out_shape=out_shape,
                              scratch_shapes=scratch, compiler_params=params)(*operands)
    grid_spec = pltpu.PrefetchScalarGridSpec(num_scalar_prefetch=1, grid=grid, in_specs=in_specs, out_specs=out_specs,
                                             scratch_shapes=scratch)
    return pl.pallas_call(body, name=name, grid_spec=grid_spec, out_shape=out_shape,
                          compiler_params=params)(row_sel[0], *operands)


def _grad_half(name, core, a, b, row_slabs, col_slabs, tm, other, recv=None, after=None):
    (_, M), (_, N) = a.shape, b.shape
    H = M // (2 * row_slabs)
    nh = H // tm
    tn = _fit(MM_TN, N // col_slabs)
    per = N // col_slabs // tn

    def a_block(i, core_ref):
        half = (1 - core_ref[0]) if other else core_ref[0]
        return (i // nh) * (2 * nh) + half * nh + i % nh

    def out_index(i, j):
        return (j // per, i, j % per) if col_slabs > 1 else (i // nh, i % nh, j)

    slabs = max(row_slabs, col_slabs)
    out_def = ((slabs, H, N // col_slabs), BF16, (1, tm, tn), out_index)

    def epilogue(acc, ex, outs):
        outs[0][0] = (acc if recv is None else acc + ex[0][0].astype(F32)).astype(BF16)

    extras = ([] if recv is None else [(recv, (1, tm, tn), out_index)]) + ([] if after is None else [_behind(after)])
    return _matmul(name, a, b, "tn", [out_def], epilogue, extras=extras, tm=tm, tn=tn,
                   row_sel=(core, a_block, row_slabs * nh))[0]


def _behind(token):
    return (token, (8, LANES), lambda i, j: (0, 0))


def _mm_plain(name, a, b, mode, out_dtype, after=None, **tiles):
    if mode == "nn":
        M, N = a.shape[0], b.shape[-1] * (b.shape[0] if b.ndim == 3 else 1)
    elif mode == "nt":
        M, N = a.shape[0], b.shape[-2]
    else:
        M, N = a.shape[1], b.shape[1]
    tm, tn = _fit(tiles.get("tm", MM_TM), M), _fit(tiles.get("tn", MM_TN), N)

    def epi(acc, ex, outs):
        outs[0][...] = acc.astype(out_dtype)

    return _matmul(name, a, b, mode, [((M, N), out_dtype, (tm, tn), lambda i, j: (i, j))], epi,
                   extras=[] if after is None else [_behind(after)], **tiles)[0]


def _rstd(v):
    return lax.rsqrt(jnp.mean(v * v, axis=-1, keepdims=True) + NORM_EPS)


ROW_TILE = 256


def _row_call(name, body, row_ins, vec_ins, row_outs, acc_outs, S, D):
    tr = _fit(ROW_TILE, S)
    row_spec = pl.BlockSpec((tr, D), lambda r: (r, 0))
    vec_spec = pl.BlockSpec((1, D), lambda r: (0, 0))
    in_specs = [row_spec] * len(row_ins) + [vec_spec] * len(vec_ins)
    out_specs = [row_spec] * len(row_outs) + [pl.BlockSpec(shp, lambda r: (0, 0)) for shp in acc_outs]
    out_shape = [jax.ShapeDtypeStruct((S, D), d) for d in row_outs] + [jax.ShapeDtypeStruct(shp, F32) for shp in acc_outs]
    need = sum(2 * tr * D * a.dtype.itemsize for a in row_ins) + sum(2 * tr * D * jnp.dtype(d).itemsize for d in row_outs)
    need += 8 * tr * D * 4
    return pl.pallas_call(
        body, name=name, grid=(S // tr,), in_specs=in_specs, out_specs=out_specs, out_shape=out_shape,
        compiler_params=pltpu.CompilerParams(dimension_semantics=("arbitrary",), vmem_limit_bytes=_vmem(need)),
    )(*row_ins, *vec_ins)


def _acc_rows(ref, rows):
    @pl.when(pl.program_id(0) == 0)
    def _():
        ref[...] = jnp.zeros_like(ref)
    for n, r in enumerate(rows):
        ref[n:n + 1, :] += r


def _pre_norm(x, g, sc, sh):
    S, D = x.shape

    def body(x_ref, g_ref, sc_ref, sh_ref, h_ref):
        xv = x_ref[...]
        xn = xv * _rstd(xv)
        h_ref[...] = (xn * g_ref[...] * (1.0 + sc_ref[...]) + sh_ref[...]).astype(BF16)

    return _row_call("pre_norm_mix", body, [x], [g, sc, sh], [BF16], [], S, D)[0]


def _post_mix(x, mix, g_post, gt, g_pre, sc, sh):
    S, D = x.shape

    def body(x_ref, mix_ref, gp_ref, gt_ref, g2_ref, sc_ref, sh_ref, x1_ref, h2_ref):
        mv = mix_ref[...].astype(F32)
        x1 = x_ref[...] + gt_ref[...] * (mv * _rstd(mv) * gp_ref[...])
        x1_ref[...] = x1
        h2_ref[...] = (x1 * _rstd(x1) * g2_ref[...] * (1.0 + sc_ref[...]) + sh_ref[...]).astype(BF16)

    return _row_call("post_mix_pre_mlp", body, [x, mix], [g_post, gt, g_pre, sc, sh], [F32, BF16], [], S, D)


def _loss_and_post_mlp_bwd(x1, y, target, g_post, gt):
    S, D = x1.shape

    def body(x1_ref, y_ref, t_ref, g_ref, gt_ref, dy_ref, dout_ref, loss_ref, acc_ref):
        yv = y_ref[...].astype(F32)
        r = _rstd(yv)
        yh = yv * r
        n = yh * g_ref[...]
        diff = x1_ref[...] + gt_ref[...] * n - t_ref[...]
        dout = diff * (1.0 / D)
        dout_ref[...] = dout
        dn = dout * gt_ref[...]
        dyh = dn * g_ref[...]
        dy_ref[...] = (r * (dyh - yh * jnp.mean(dyh * yh, axis=-1, keepdims=True))).astype(BF16)
        _acc_rows(acc_ref, [jnp.sum(dout * n, axis=0, keepdims=True), jnp.sum(dn * yh, axis=0, keepdims=True)])

        @pl.when(pl.program_id(0) == 0)
        def _():
            loss_ref[...] = jnp.zeros_like(loss_ref)
        loss_ref[...] += jnp.full(loss_ref.shape, (0.5 / D) * jnp.sum(diff * diff), F32)

    return _row_call("loss_post_mlp_bwd", body, [x1, y, target], [g_post, gt], [BF16, F32],
                     [(8, LANES), (8, D)], S, D)


def _pre_mlp_and_post_mix_bwd(dh2, x1, dout, mix, g_pre, sc, g_post, gt):
    S, D = x1.shape

    def body(dh_ref, x1_ref, dout_ref, mix_ref, g_ref, sc_ref, gp_ref, gt_ref, dx1_ref, dmix_ref, acc_ref):
        dh = dh_ref[...].astype(F32)
        x1v = x1_ref[...]
        r3 = _rstd(x1v)
        xn = x1v * r3
        dxn = dh * (1.0 + sc_ref[...]) * g_ref[...]
        dx1 = dout_ref[...] + r3 * (dxn - xn * jnp.mean(dxn * xn, axis=-1, keepdims=True))
        dx1_ref[...] = dx1
        mv = mix_ref[...].astype(F32)
        r2 = _rstd(mv)
        mh = mv * r2
        dn = dx1 * gt_ref[...]
        dmh = dn * gp_ref[...]
        dmix_ref[...] = (r2 * (dmh - mh * jnp.mean(dmh * mh, axis=-1, keepdims=True))).astype(BF16)
        _acc_rows(acc_ref, [
            jnp.sum(dh, axis=0, keepdims=True),
            jnp.sum(dh * xn * g_ref[...], axis=0, keepdims=True),
            jnp.sum(dh * (1.0 + sc_ref[...]) * xn, axis=0, keepdims=True),
            jnp.sum(dx1 * mh * gp_ref[...], axis=0, keepdims=True),
            jnp.sum(dn * mh, axis=0, keepdims=True)])

    return _row_call("pre_mlp_post_mix_bwd", body, [dh2, x1, dout, mix], [g_pre, sc, g_post, gt], [F32, BF16],
                     [(8, D)], S, D)


def _pre_mix_bwd(dh, x, dx1, g_pre, sc):
    S, D = x.shape

    def body(dh_ref, x_ref, dx1_ref, g_ref, sc_ref, gx_ref, acc_ref):
        dhv = dh_ref[...].astype(F32)
        xv = x_ref[...]
        r = _rstd(xv)
        xn = xv * r
        dxn = dhv * (1.0 + sc_ref[...]) * g_ref[...]
        gx_ref[...] = dx1_ref[...] + r * (dxn - xn * jnp.mean(dxn * xn, axis=-1, keepdims=True))
        _acc_rows(acc_ref, [
            jnp.sum(dhv, axis=0, keepdims=True),
            jnp.sum(dhv * xn * g_ref[...], axis=0, keepdims=True),
            jnp.sum(dhv * (1.0 + sc_ref[...]) * xn, axis=0, keepdims=True)])

    return _row_call("pre_mix_bwd", body, [dh, x, dx1], [g_pre, sc], [F32], [(8, D)], S, D)


CUM_BLOCK = 256


def _tri(n, upper):
    r = lax.broadcasted_iota(jnp.int32, (n, n), 0)
    c = lax.broadcasted_iota(jnp.int32, (n, n), 1)
    return ((c >= r) if upper else (c <= r)).astype(F32)


def _fox_gate_fwd(fg, b_pad):
    S = fg.shape[0]
    cb = _fit(CUM_BLOCK, S)

    def body(fg_ref, b_ref, cumt_ref, cum_ref):
        low = _tri(cb, False)
        carry = jnp.zeros((1, LANES), F32)
        for n in range(S // cb):
            z = fg_ref[n * cb:(n + 1) * cb, :] + b_ref[...]
            logf = jnp.minimum(z, 0.0) - jnp.log(1.0 + jnp.exp(-jnp.abs(z)))
            blk = jnp.dot(low, logf, precision=lax.Precision.HIGHEST, preferred_element_type=F32) + carry
            cum_ref[n * cb:(n + 1) * cb, :] = blk
            carry = blk[cb - 1:cb, :]
        cumt_ref[...] = cum_ref[...].T

    return pl.pallas_call(
        body, name="fox_gate_fwd", out_shape=jax.ShapeDtypeStruct((LANES, S), F32),
        scratch_shapes=[pltpu.VMEM((S, LANES), F32)],
        compiler_params=pltpu.CompilerParams(vmem_limit_bytes=_vmem(6 * S * LANES * 4)),
    )(fg, b_pad)


def _fox_gate_bwd(dcum_k, dcum_q, fg, b_pad):
    S = fg.shape[0]
    n_fox = dcum_q.shape[0]
    cb = _fit(CUM_BLOCK, S)

    def body(dk_ref, dq_ref, fg_ref, b_ref, dfg_ref, db_ref, dc_ref):
        lane = lax.broadcasted_iota(jnp.int32, (S, LANES), 1)
        dc = dk_ref[...].T
        for h in range(n_fox):
            dc = dc + jnp.where(lane == h, dq_ref[h], 0.0)
        dc_ref[...] = dc
        up = _tri(cb, True)
        carry = jnp.zeros((1, LANES), F32)
        db = jnp.zeros((1, LANES), F32)
        for n in reversed(range(S // cb)):
            blk = jnp.dot(up, dc_ref[n * cb:(n + 1) * cb, :], precision=lax.Precision.HIGHEST,
                          preferred_element_type=F32) + carry
            carry = blk[0:1, :]
            z = fg_ref[n * cb:(n + 1) * cb, :] + b_ref[...]
            dfg = blk * (1.0 / (1.0 + jnp.exp(z)))
            dfg_ref[n * cb:(n + 1) * cb, :] = dfg.astype(BF16)
            db = db + jnp.sum(dfg, axis=0, keepdims=True)
        db_ref[...] = jnp.broadcast_to(db, db_ref.shape)

    return pl.pallas_call(
        body, name="fox_gate_bwd",
        out_shape=[jax.ShapeDtypeStruct((S, LANES), BF16), jax.ShapeDtypeStruct((8, LANES), F32)],
        scratch_shapes=[pltpu.VMEM((S, LANES), F32)],
        compiler_params=pltpu.CompilerParams(vmem_limit_bytes=_vmem((8 + 2 * n_fox) * S * LANES * 4)),
    )(dcum_k, dcum_q, fg, b_pad)


FOX_TILE = 512


LOG2E = 1.4426950408889634


def _fox_scores(q, k, ck2, masked, t):
    s = lax.dot_general(q, k, _NT, preferred_element_type=F32) * (HEAD_DIM ** -0.5 * LOG2E) - ck2
    if masked:
        row = lax.broadcasted_iota(jnp.int32, (t, t), 0)
        col = lax.broadcasted_iota(jnp.int32, (t, t), 1)
        s = jnp.where(col <= row, s, NEG)
    return s


def _fox_fwd(proj, cum_row, n_fox):
    S = proj.shape[0]
    t = _fit(FOX_TILE, S)
    nq = S // t

    def body(q_ref, k_ref, v_ref, ck_ref, o_ref, lse_ref):
        def q_block(qi, _):
            q0 = pl.multiple_of(qi * t, t)
            q = q_ref[pl.ds(q0, t), :]

            def kv_block(j, carry, masked):
                m, l, acc = carry
                k0 = pl.multiple_of(j * t, t)
                s = _fox_scores(q, k_ref[pl.ds(k0, t), :], ck_ref[0, :, pl.ds(k0, t)] * LOG2E, masked, t)
                m_new = jnp.maximum(m, jnp.max(s, axis=-1, keepdims=True))
                alpha = jnp.exp2(m - m_new)
                p = jnp.exp2(s - m_new)
                l = alpha * l + jnp.sum(p, axis=-1, keepdims=True)
                acc = alpha * acc + jnp.dot(p.astype(BF16), v_ref[pl.ds(k0, t), :], preferred_element_type=F32)
                return m_new, l, acc

            init = (jnp.full((t, 1), NEG, F32), jnp.zeros((t, 1), F32), jnp.zeros((t, HEAD_DIM), F32))
            carry = lax.fori_loop(0, qi, lambda j, cr: kv_block(j, cr, False), init)
            m, l, acc = kv_block(qi, carry, True)
            o_ref[pl.ds(q0, t), :] = acc / l
            lse_ref[0, pl.ds(q0, t), :] = jnp.broadcast_to(m + jnp.log(l) * LOG2E, (t, LANES))
            return 0

        lax.fori_loop(0, nq, q_block, 0)

    col = lambda off: pl.BlockSpec((S, HEAD_DIM), lambda h: (0, off + h))
    per_head = pl.BlockSpec((1, S, LANES), lambda h: (h, 0, 0))
    return pl.pallas_call(
        body, name="fox_fwd", grid=(n_fox,),
        in_specs=[col(0), col(n_fox), col(2 * n_fox), pl.BlockSpec((1, 1, S), lambda h: (h, 0, 0))],
        out_specs=[pl.BlockSpec((S, HEAD_DIM), lambda h: (0, h)), per_head],
        out_shape=[jax.ShapeDtypeStruct((S, n_fox * HEAD_DIM), F32), jax.ShapeDtypeStruct((n_fox, S, LANES), F32)],
        compiler_params=pltpu.CompilerParams(dimension_semantics=("parallel",),
                                             vmem_limit_bytes=_vmem(16 * S * HEAD_DIM * 4 + 12 * t * t * 4)),
    )(proj, proj, proj, cum_row)


def _fox_bwd(proj, o, do, lse_b, cum_row, n_fox):
    S = proj.shape[0]
    t = _fit(FOX_TILE, S)
    nq = S // t
    scale = HEAD_DIM ** -0.5

    def body(q_ref, k_ref, v_ref, o_ref, do_ref, lse_ref, ck_ref, dq_ref, dk_ref, dv_ref, dc_ref, dcq_ref,
             dq_acc, delta_ref):
        dq_acc[...] = jnp.zeros_like(dq_acc)
        dcq_ref[...] = jnp.zeros_like(dcq_ref)

        def delta_block(qi, _):
            q0 = pl.multiple_of(qi * t, t)
            d = jnp.sum(do_ref[pl.ds(q0, t), :] * o_ref[pl.ds(q0, t), :], axis=-1, keepdims=True)
            delta_ref[pl.ds(q0, t), :] = jnp.broadcast_to(d, (t, LANES))
            return 0

        lax.fori_loop(0, nq, delta_block, 0)

        def kv_block(j, _):
            k0 = pl.multiple_of(j * t, t)
            k = k_ref[pl.ds(k0, t), :]
            v = v_ref[pl.ds(k0, t), :]
            ck2 = ck_ref[0, :, pl.ds(k0, t)] * LOG2E

            def q_block(qi, carry, masked):
                dk, dv, dc = carry
                q0 = pl.multiple_of(qi * t, t)
                q = q_ref[pl.ds(q0, t), :]
                dov = do_ref[pl.ds(q0, t), :].astype(BF16)
                p = jnp.exp2(_fox_scores(q, k, ck2, masked, t) - lse_ref[0, pl.ds(q0, t), :][:, :1])
                dp = lax.dot_general(dov, v, _NT, preferred_element_type=F32)
                ds = p * (dp - delta_ref[pl.ds(q0, t), :][:, :1])
                dsb = ds.astype(BF16)
                dv = dv + lax.dot_general(p.astype(BF16), dov, _TN, preferred_element_type=F32)
                dk = dk + lax.dot_general(dsb, q, _TN, preferred_element_type=F32)
                dq_acc[pl.ds(q0, t), :] += jnp.dot(dsb, k, preferred_element_type=F32)
                dc = dc - jnp.sum(ds, axis=0, keepdims=True)
                dcq_ref[0, pl.ds(q0, t), :] += jnp.broadcast_to(jnp.sum(ds, axis=1, keepdims=True), (t, LANES))
                return dk, dv, dc

            init = (jnp.zeros((t, HEAD_DIM), F32), jnp.zeros((t, HEAD_DIM), F32), jnp.zeros((1, t), F32))
            carry = q_block(j, init, True)
            dk, dv, dc = lax.fori_loop(j + 1, nq, lambda qi, cr: q_block(qi, cr, False), carry)
            dk_ref[pl.ds(k0, t), :] = (dk * scale).astype(BF16)
            dv_ref[pl.ds(k0, t), :] = dv.astype(BF16)
            dc_ref[0, :, pl.ds(k0, t)] = dc
            return 0

        lax.fori_loop(0, nq, kv_block, 0)
        dq_ref[...] = (dq_acc[...] * scale).astype(BF16)

    col = lambda off: pl.BlockSpec((S, HEAD_DIM), lambda h: (0, off + h))
    per_head = pl.BlockSpec((1, S, LANES), lambda h: (h, 0, 0))
    row = pl.BlockSpec((1, 1, S), lambda h: (h, 0, 0))
    grad = jax.ShapeDtypeStruct((S, n_fox * HEAD_DIM), BF16)
    return pl.pallas_call(
        body, name="fox_bwd", grid=(n_fox,),
        in_specs=[col(0), col(n_fox), col(2 * n_fox), col(0), col(0), per_head, row],
        out_specs=[col(0), col(0), col(0), row, per_head],
        out_shape=[grad, grad, grad, jax.ShapeDtypeStruct((n_fox, 1, S), F32), jax.ShapeDtypeStruct((n_fox, S, LANES), F32)],
        scratch_shapes=[pltpu.VMEM((S, HEAD_DIM), F32), pltpu.VMEM((S, LANES), F32)],
        compiler_params=pltpu.CompilerParams(dimension_semantics=("parallel",),
                                             vmem_limit_bytes=_vmem(24 * S * HEAD_DIM * 4 + 16 * t * t * 4)),
    )(proj, proj, proj, o, do, lse_b, cum_row)


def _rope_tables(S):
    half = HEAD_DIM // 2
    inv_freq = 1.0 / (ROPE_THETA ** (jnp.arange(half, dtype=F32) * (2.0 / HEAD_DIM)))
    ang = jnp.arange(S).astype(F32)[:, None] * inv_freq[None, :]
    cos, sin = jnp.cos(ang), jnp.sin(ang)
    return jnp.concatenate([cos, cos], axis=-1), jnp.concatenate([-sin, sin], axis=-1)


def _rope(name, src, first_block, n_blocks, cos, sin_signed):
    S = src.shape[0]

    def body(x_ref, cos_ref, sin_ref, o_ref):
        xv = x_ref[...].astype(F32)
        o_ref[...] = (xv * cos_ref[...] + pltpu.roll(xv, HEAD_DIM // 2, 1) * sin_ref[...]).astype(BF16)

    table = pl.BlockSpec((S, HEAD_DIM), lambda n: (0, 0))
    return pl.pallas_call(
        body, name=name, grid=(n_blocks,),
        in_specs=[pl.BlockSpec((S, HEAD_DIM), lambda n: (0, first_block + n)), table, table],
        out_specs=pl.BlockSpec((S, HEAD_DIM), lambda n: (0, n)),
        out_shape=jax.ShapeDtypeStruct((S, n_blocks * HEAD_DIM), BF16),
        compiler_params=pltpu.CompilerParams(dimension_semantics=("parallel",),
                                             vmem_limit_bytes=_vmem(12 * S * HEAD_DIM * 4)),
    )(src, cos, sin_signed)


def _swa_tile(q_ref, kp_ref, kc_ref, n, group, scale):
    B = SWA_BLOCK
    qs = jnp.concatenate([q_ref[:, g * HEAD_DIM:(g + 1) * HEAD_DIM] for g in range(group)], axis=0)
    kcat = jnp.concatenate([kp_ref[...], kc_ref[...]], axis=0)
    s = lax.dot_general(qs, kcat, _NT, preferred_element_type=F32) * scale
    qi = lax.broadcasted_iota(jnp.int32, (group * B, 2 * B), 0) % B
    kj = lax.broadcasted_iota(jnp.int32, (group * B, 2 * B), 1)
    diff = qi + B - kj
    mask = (diff >= 0) & (diff < B) & ((n * B + kj - B) >= 0)
    return qs, kcat, jnp.where(mask, s, NEG)


def _swa_sink_col(sink_ref, kv, group):
    head = lax.broadcasted_iota(jnp.int32, (group * SWA_BLOCK, 1), 0) // SWA_BLOCK
    col = jnp.zeros((group * SWA_BLOCK, 1), F32)
    for g in range(group):
        col = jnp.where(head == g, sink_ref[kv * group + g], col)
    return col


def _swa_specs(n_kv, group, q_first, k_first, v_first):
    B = SWA_BLOCK
    prev = lambda n: jnp.maximum(n - 1, 0)
    return [
        pl.BlockSpec((B, group * HEAD_DIM), lambda kv, n: (n, q_first + kv)),
        pl.BlockSpec((B, HEAD_DIM), lambda kv, n: (prev(n), k_first + kv)),
        pl.BlockSpec((B, HEAD_DIM), lambda kv, n: (n, k_first + kv)),
        pl.BlockSpec((B, HEAD_DIM), lambda kv, n: (prev(n), v_first + kv)),
        pl.BlockSpec((B, HEAD_DIM), lambda kv, n: (n, v_first + kv)),
    ]


def _swa_fwd(rq, proj, v_first, sinks, n_q, n_kv):
    S = rq.shape[0]
    B = SWA_BLOCK
    group = n_q // n_kv
    scale = HEAD_DIM ** -0.5

    def body(q_ref, kp_ref, kc_ref, vp_ref, vc_ref, sink_ref, o_ref, lse_ref):
        kv, n = pl.program_id(0), pl.program_id(1)
        _, _, s = _swa_tile(q_ref, kp_ref, kc_ref, n, group, scale)
        sink = _swa_sink_col(sink_ref, kv, group)
        m = jnp.maximum(jnp.max(s, axis=-1, keepdims=True), sink)
        p = jnp.exp(s - m)
        denom = jnp.sum(p, axis=-1, keepdims=True) + jnp.exp(sink - m)
        vcat = jnp.concatenate([vp_ref[...], vc_ref[...]], axis=0)
        o = jnp.dot((p / denom).astype(BF16), vcat, preferred_element_type=F32)
        lse = m + jnp.log(denom)
        for g in range(group):
            o_ref[:, g * HEAD_DIM:(g + 1) * HEAD_DIM] = o[g * B:(g + 1) * B, :]
            lse_ref[0, :, g * LANES:(g + 1) * LANES] = jnp.broadcast_to(lse[g * B:(g + 1) * B, :], (B, LANES))

    specs = _swa_specs(n_kv, group, 0, n_q, v_first)
    q_blk = pl.BlockSpec((B, group * HEAD_DIM), lambda kv, n: (n, kv))
    return pl.pallas_call(
        body, name="swa_fwd", grid=(n_kv, S // B),
        in_specs=specs + [pl.BlockSpec(memory_space=pltpu.SMEM)],
        out_specs=[q_blk, pl.BlockSpec((1, B, group * LANES), lambda kv, n: (kv, n, 0))],
        out_shape=[jax.ShapeDtypeStruct((S, n_q * HEAD_DIM), F32), jax.ShapeDtypeStruct((n_kv, S, group * LANES), F32)],
        compiler_params=pltpu.CompilerParams(dimension_semantics=("parallel", "arbitrary")),
    )(rq, rq, rq, proj, proj, sinks)


def _swa_bwd(rq, proj, v_first, sinks, o, do, do_first, lse_b, n_q, n_kv):
    S = rq.shape[0]
    B = SWA_BLOCK
    group = n_q // n_kv
    scale = HEAD_DIM ** -0.5

    def body(q_ref, kp_ref, kc_ref, vp_ref, vc_ref, o_ref, do_ref, lse_ref, sink_ref,
             dq_ref, dk_ref, dv_ref, dsink_ref):
        kv, n = pl.program_id(0), pl.program_id(1)

        @pl.when(n == 0)
        def _():
            dk_ref[...] = jnp.zeros_like(dk_ref)
            dv_ref[...] = jnp.zeros_like(dv_ref)
            dsink_ref[...] = jnp.zeros_like(dsink_ref)

        qs, kcat, s = _swa_tile(q_ref, kp_ref, kc_ref, n, group, scale)
        sink = _swa_sink_col(sink_ref, kv, group)
        stack = lambda ref, w: jnp.concatenate([ref[:, g * w:(g + 1) * w] for g in range(group)], axis=0)
        lse = jnp.concatenate([lse_ref[0, :, g * LANES:g * LANES + 1] for g in range(group)], axis=0)
        do32 = stack(do_ref, HEAD_DIM)
        delta = jnp.sum(do32 * stack(o_ref, HEAD_DIM), axis=-1, keepdims=True)
        dov = do32.astype(BF16)
        p = jnp.exp(s - lse)
        vcat = jnp.concatenate([vp_ref[...], vc_ref[...]], axis=0)
        dp = lax.dot_general(dov, vcat, _NT, preferred_element_type=F32)
        ds = p * (dp - delta)
        dsb = ds.astype(BF16)
        dq = jnp.dot(dsb, kcat, preferred_element_type=F32) * scale
        for g in range(group):
            dq_ref[:, g * HEAD_DIM:(g + 1) * HEAD_DIM] = dq[g * B:(g + 1) * B, :].astype(BF16)
        dkcat = lax.dot_general(dsb, qs, _TN, preferred_element_type=F32) * scale
        dvcat = lax.dot_general(p.astype(BF16), dov, _TN, preferred_element_type=F32)
        prev0 = pl.multiple_of(jnp.maximum(n - 1, 0) * B, B)
        cur0 = pl.multiple_of(n * B, B)
        dk_ref[0, pl.ds(prev0, B), :] += dkcat[:B, :]
        dk_ref[0, pl.ds(cur0, B), :] += dkcat[B:, :]
        dv_ref[0, pl.ds(prev0, B), :] += dvcat[:B, :]
        dv_ref[0, pl.ds(cur0, B), :] += dvcat[B:, :]
        dsk = -jnp.exp(sink - lse) * delta
        lane = lax.broadcasted_iota(jnp.int32, (1, LANES), 1)
        row = jnp.zeros((1, LANES), F32)
        for g in range(group):
            row = row + jnp.where(lane == g, jnp.sum(dsk[g * B:(g + 1) * B, :]), 0.0)
        dsink_ref[0, 0:1, :] += row

    specs = _swa_specs(n_kv, group, 0, n_q, v_first)
    q_blk = pl.BlockSpec((B, group * HEAD_DIM), lambda kv, n: (n, kv))
    acc = pl.BlockSpec((1, S, HEAD_DIM), lambda kv, n: (kv, 0, 0))
    return pl.pallas_call(
        body, name="swa_bwd", grid=(n_kv, S // B),
        in_specs=specs + [q_blk, pl.BlockSpec((B, group * HEAD_DIM), lambda kv, n: (n, do_first + kv)),
                          pl.BlockSpec((1, B, group * LANES), lambda kv, n: (kv, n, 0)),
                          pl.BlockSpec(memory_space=pltpu.SMEM)],
        out_specs=[q_blk, acc, acc, pl.BlockSpec((1, 8, LANES), lambda kv, n: (kv, 0, 0))],
        out_shape=[jax.ShapeDtypeStruct((S, n_q * HEAD_DIM), BF16), jax.ShapeDtypeStruct((n_kv, S, HEAD_DIM), F32),
                   jax.ShapeDtypeStruct((n_kv, S, HEAD_DIM), F32), jax.ShapeDtypeStruct((n_kv, 8, LANES), F32)],
        compiler_params=pltpu.CompilerParams(dimension_semantics=("parallel", "arbitrary")),
    )(rq, rq, rq, proj, proj, o, do, lse_b, sinks)


def _adamw(w, g, m, v):
    m = ADAM_B1 * m + (1.0 - ADAM_B1) * g
    v = ADAM_B2 * v + (1.0 - ADAM_B2) * (g * g)
    m_hat = m / (1.0 - ADAM_B1 ** ADAM_STEP)
    v_hat = v / (1.0 - ADAM_B2 ** ADAM_STEP)
    delta = -ADAM_LR * (m_hat / (jnp.sqrt(v_hat) + ADAM_EPS) + ADAM_WD * w)
    return delta, m, v


def _mod_fwd(cond_in, w_mod, b_shard):
    R, D = cond_in.shape
    cols = w_mod.shape[1]
    tn = _fit(512, cols)

    def body(c_ref, w_ref, b_ref, o_ref):
        cv = c_ref[...]
        cond = (cv / (1.0 + jnp.exp(-cv))).astype(BF16)
        o_ref[...] = jnp.dot(cond, w_ref[...].astype(BF16), preferred_element_type=F32) + b_ref[...]

    return pl.pallas_call(
        body, name="mod_fwd", grid=(cols // tn,),
        in_specs=[pl.BlockSpec((R, D), lambda j: (0, 0)), pl.BlockSpec((D, tn), lambda j: (0, j)),
                  pl.BlockSpec((1, tn), lambda j: (0, j))],
        out_specs=pl.BlockSpec((R, tn), lambda j: (0, j)),
        out_shape=jax.ShapeDtypeStruct((R, cols), F32),
        compiler_params=pltpu.CompilerParams(dimension_semantics=("parallel",), vmem_limit_bytes=_vmem(3 * D * tn * 4)),
    )(cond_in, w_mod, b_shard)


def _mod_update(c_t, dmod, w, m, v):
    D, nb = c_t.shape
    cols = w.shape[1]
    tr = _fit(128, D)

    def body(c_ref, d_ref, w_ref, m_ref, v_ref, g_ref, dl_ref, nm_ref, nv_ref):
        cv = c_ref[...]
        cond = cv / (1.0 + jnp.exp(-cv))
        g = jnp.zeros((tr, cols), F32)
        for b in range(nb):
            g = g + cond[:, b:b + 1] * d_ref[b:b + 1, :]
        g_ref[...] = g
        dl_ref[...], nm_ref[...], nv_ref[...] = _adamw(w_ref[...], g, m_ref[...], v_ref[...])

    blk = pl.BlockSpec((tr, cols), lambda r: (r, 0))
    out = jax.ShapeDtypeStruct((D, cols), F32)
    return pl.pallas_call(
        body, name="mod_update", grid=(D // tr,),
        in_specs=[pl.BlockSpec((tr, nb), lambda r: (r, 0)), pl.BlockSpec((nb, cols), lambda r: (0, 0)), blk, blk, blk],
        out_specs=[blk] * 4, out_shape=[out] * 4,
        compiler_params=pltpu.CompilerParams(dimension_semantics=("parallel",), vmem_limit_bytes=_vmem(18 * tr * cols * 4)),
    )(c_t, dmod, w, m, v)


def _small_update(stacked, w, m, v):
    R, C = w.shape

    def body(s_ref, w_ref, m_ref, v_ref, g_ref, dl_ref, nm_ref, nv_ref):
        g = s_ref[0:R, :]
        for d in range(1, N_DEV):
            g = g + s_ref[d * R:(d + 1) * R, :]
        g_ref[...] = g
        dl_ref[...], nm_ref[...], nv_ref[...] = _adamw(w_ref[...], g, m_ref[...], v_ref[...])

    return pl.pallas_call(body, name="small_update", out_shape=[jax.ShapeDtypeStruct((R, C), F32)] * 4)(stacked, w, m, v)


def _place():
    return lax.axis_index("x"), lax.axis_index("y"), lax.axis_index("c")


def _allgather8(name, block):
    m_per, n = block.shape

    def body(x_ref, out_ref, token_ref, send_sems, recv_sems, local_sem):
        token_ref[...] = jnp.zeros_like(token_ref)
        x, y, c = _place()
        me, sibling = (x, y, c), (x, y, 1 - c)
        chips = [(1 - x, y), (x, 1 - y), (1 - x, 1 - y)]

        def rows(px, py, pc):
            return out_ref.at[pl.ds((4 * px + 2 * py + pc) * m_per, m_per), :]

        def copy(k, blk, to, src=None):
            return pltpu.make_async_remote_copy(
                src_ref=rows(*blk) if src is None else src, dst_ref=rows(*blk),
                send_sem=send_sems.at[k], recv_sem=recv_sems.at[k], device_id=to, device_id_type=MESH)

        mine = pltpu.make_async_copy(x_ref, rows(*me), local_sem)
        mine.start()
        first = [copy(0, me, sibling, src=x_ref)]
        first += [copy(1 + j, me, (*chip, c), src=x_ref) for j, chip in enumerate(chips)]
        for cp in first:
            cp.start()
        passed = [copy(4 + j, (*chip, c), sibling) for j, chip in enumerate(chips)]
        for j, chip in enumerate(chips):
            copy(1 + j, (*chip, c), me).wait_recv()
            passed[j].start()
        copy(0, sibling, me).wait_recv()
        for j, chip in enumerate(chips):
            copy(4 + j, (*chip, 1 - c), me).wait_recv()
        for cp in first + passed:
            cp.wait_send()
        mine.wait()

    vmem = pl.BlockSpec(memory_space=pltpu.VMEM)
    return pl.pallas_call(
        body, name=name,
        out_shape=[jax.ShapeDtypeStruct((N_DEV * m_per, n), block.dtype), jax.ShapeDtypeStruct((8, LANES), F32)],
        in_specs=[vmem], out_specs=[vmem, vmem],
        scratch_shapes=[pltpu.SemaphoreType.DMA((7,)), pltpu.SemaphoreType.DMA((7,)), pltpu.SemaphoreType.DMA],
    )(block)


_ANY = pl.BlockSpec(memory_space=pl.ANY)


def _half(ref, c, rows):
    return ref.at[pl.ds(c * (rows // 2), rows // 2), :]


_HBM = pl.BlockSpec(memory_space=pltpu.HBM)
_SEM = pl.BlockSpec(memory_space=pltpu.SEMAPHORE)
_EFFECT = pltpu.SideEffectType.DATAFLOW_SIDE_EFFECTING


def _ici_start(name, srcs, land_shapes, plan, per_source=3, after=None):
    ns, nl = len(srcs), len(land_shapes)
    n_copies = per_source * ns
    n_in = ns + nl + (after is not None)

    def body(*refs):
        src_refs, land_refs = refs[:ns], refs[ns:ns + nl]
        send_sems, recv_sems = refs[n_in], refs[n_in + 1]
        token = refs[-1]
        for n, (src, dst, peer, _) in enumerate(plan(src_refs, land_refs)):
            pltpu.make_async_remote_copy(src_ref=src, dst_ref=dst, send_sem=send_sems.at[n], recv_sem=recv_sems.at[n],
                                         device_id=peer, device_id_type=MESH).start()
        token[...] = jnp.zeros_like(token)

    lands = [lax.empty(s.shape, s.dtype) for s in land_shapes]
    out = pl.pallas_call(
        body, name=name,
        out_shape=(pltpu.SemaphoreType.DMA((n_copies,)), pltpu.SemaphoreType.DMA((n_copies,)),
                   *[pltpu.HBM(a.shape, a.dtype) for a in list(srcs) + lands], jax.ShapeDtypeStruct((8, LANES), F32)),
        in_specs=[_HBM] * (ns + nl) + [_ANY] * (after is not None),
        out_specs=(_SEM, _SEM, *[_HBM] * (ns + nl), pl.BlockSpec(memory_space=pltpu.VMEM)),
        input_output_aliases={n: 2 + n for n in range(ns + nl)},
        compiler_params=pltpu.CompilerParams(has_side_effects=_EFFECT),
    )(*[pltpu.with_memory_space_constraint(a, pltpu.HBM) for a in list(srcs) + lands],
      *([] if after is None else [after]))
    return out[0], out[1], list(out[2:2 + ns]), list(out[2 + ns:2 + ns + nl]), out[-1]


def _ici_wait(name, send_sems, recv_sems, srcs, lands, plan, after):
    ns, nl = len(srcs), len(lands)
    after = list(after) if isinstance(after, (list, tuple)) else [after]

    def body(*refs):
        src_refs, land_refs = refs[:ns], refs[ns:ns + nl]
        send_sems, recv_sems = refs[ns + nl], refs[ns + nl + 1]
        for n, (src, _, peer, mine) in enumerate(plan(src_refs, land_refs)):
            cp = pltpu.make_async_remote_copy(src_ref=src, dst_ref=mine, send_sem=send_sems.at[n],
                                              recv_sem=recv_sems.at[n], device_id=peer, device_id_type=MESH)
            cp.wait_send()
            cp.wait_recv()

    out = pl.pallas_call(
        body, name=name, out_shape=[pltpu.HBM(a.shape, a.dtype) for a in list(srcs) + list(lands)],
        in_specs=[_HBM] * (ns + nl) + [_SEM, _SEM] + [_ANY] * len(after), out_specs=[_HBM] * (ns + nl),
        input_output_aliases={n: n for n in range(ns + nl)},
        compiler_params=pltpu.CompilerParams(has_side_effects=_EFFECT),
    )(*srcs, *lands, send_sems, recv_sems, *after)
    return list(out[:ns]), list(out[ns:])


def _own_slab(name, chip, w, after):
    R, C = w.shape
    tr, tc = _tiles(R, C)
    tied = [] if after is None else [after]

    def body(chip_ref, w_ref, *rest):
        stack_ref, token_ref = rest[-2:]
        stack_ref[0] = w_ref[...].astype(BF16)
        token_ref[...] = jnp.zeros_like(token_ref)

    small = pl.BlockSpec((8, LANES), lambda r, q, chip_ref: (0, 0))
    grid_spec = pltpu.PrefetchScalarGridSpec(
        num_scalar_prefetch=1, grid=(R // tr, C // tc),
        in_specs=[pl.BlockSpec((tr, tc), lambda r, q, chip_ref: (r, q))] + [small] * len(tied),
        out_specs=[pl.BlockSpec((1, tr, tc), lambda r, q, chip_ref: (chip_ref[0], r, q)), small])
    return pl.pallas_call(
        body, name=name, grid_spec=grid_spec,
        out_shape=[jax.ShapeDtypeStruct((N_CHIPS, R, C), BF16), jax.ShapeDtypeStruct((8, LANES), F32)],
        compiler_params=pltpu.CompilerParams(dimension_semantics=("arbitrary", "arbitrary")),
    )(chip, w, *tied)


def _gather_plan(src_refs, land_refs):
    x, y, c = _place()
    copies = []
    for stack in src_refs:
        R = stack.shape[1]
        own = _half(stack.at[2 * x + y], c, R)
        for cx, cy in [(1 - x, y), (x, 1 - y), (1 - x, 1 - y)]:
            copies.append((own, own, (cx, cy, c), _half(stack.at[2 * cx + cy], c, R)))
    return copies


def _pass_plan(src_refs, land_refs):
    x, y, c = _place()
    copies = []
    for land in src_refs:
        R = land.shape[1]
        for cx, cy in [(1 - x, y), (x, 1 - y), (1 - x, 1 - y)]:
            slot = land.at[2 * cx + cy]
            copies.append((_half(slot, c, R), _half(slot, c, R), (x, y, 1 - c), _half(slot, 1 - c, R)))
    return copies


def _share_plan(src_refs, land_refs):
    x, y, c = _place()
    return [(h, land, (x, y, 1 - c), land) for h, land in zip(src_refs, land_refs)]


def _pass_to_sibling(name, lands):
    nw = len(lands)

    def body(*refs):
        ins, outs = refs[:nw], refs[nw:2 * nw]
        send_sems, recv_sems = refs[2 * nw:]
        x, y, c = _place()
        chips = [(1 - x, y), (x, 1 - y), (1 - x, 1 - y)]
        copies = []
        for k in range(nw):
            R = ins[k].shape[1]
            for j, (cx, cy) in enumerate(chips):
                cp = pltpu.make_async_remote_copy(
                    src_ref=_half(ins[k].at[2 * cx + cy], c, R), dst_ref=_half(outs[k].at[2 * cx + cy], c, R),
                    send_sem=send_sems.at[3 * k + j], recv_sem=recv_sems.at[3 * k + j],
                    device_id=(x, y, 1 - c), device_id_type=MESH)
                cp.start()
                copies.append(cp)
        for k in range(nw):
            R = ins[k].shape[1]
            for j, (cx, cy) in enumerate(chips):
                pltpu.make_async_remote_copy(
                    src_ref=_half(ins[k].at[2 * cx + cy], c, R), dst_ref=_half(outs[k].at[2 * cx + cy], 1 - c, R),
                    send_sem=send_sems.at[3 * k + j], recv_sem=recv_sems.at[3 * k + j],
                    device_id=(x, y, 1 - c), device_id_type=MESH).wait_recv()
        for cp in copies:
            cp.wait_send()

    return pl.pallas_call(
        body, name=name, out_shape=[jax.ShapeDtypeStruct(a.shape, a.dtype) for a in lands],
        in_specs=[_ANY] * nw, out_specs=[_ANY] * nw, input_output_aliases={k: k for k in range(nw)},
        scratch_shapes=[pltpu.SemaphoreType.DMA((3 * nw,)), pltpu.SemaphoreType.DMA((3 * nw,))],
    )(*lands)


def _tie(vec, token):
    return vec + token[0:1, 0:1]


def _lay_columns(name, arrays, pieces):
    S, dtype = arrays[0].shape[0], arrays[0].dtype
    out_w = sum(hi - lo for _, lo, hi in pieces)
    tr = _fit(ROW_TILE, S)

    def body(*refs):
        o_ref = refs[-1]
        rows = [r[...] for r in refs[:-1]]
        o_ref[...] = jnp.concatenate(
            [jnp.zeros((tr, hi - lo), dtype) if k is None else rows[k][:, lo:hi] for k, lo, hi in pieces], axis=1)

    need = 2 * tr * (out_w + sum(a.shape[1] for a in arrays)) * dtype.itemsize
    return pl.pallas_call(
        body, name=name, grid=(S // tr,), in_specs=[pl.BlockSpec((tr, a.shape[1]), lambda i: (i, 0)) for a in arrays],
        out_specs=pl.BlockSpec((tr, out_w), lambda i: (i, 0)), out_shape=jax.ShapeDtypeStruct((S, out_w), dtype),
        compiler_params=pltpu.CompilerParams(dimension_semantics=("parallel",), vmem_limit_bytes=_vmem(2 * need)),
    )(*arrays)


ROW_ALIGN = 16
TILE_ELEMS = 512 * 1024


def _tiles(rows, cols):
    fits = [t for t in range(ROW_ALIGN, min(rows, 256) + 1, ROW_ALIGN) if rows % t == 0]
    tr = fits[-1] if fits and fits[-1] >= 64 else rows
    tc = cols
    while tr * tc > TILE_ELEMS and tc % (2 * LANES) == 0:
        tc //= 2
    return tr, tc


def _scatter_plan(src_refs, land_refs):
    x, y, c = _place()
    copies = []
    for p, land in zip(src_refs, land_refs):
        for j, (cx, cy) in enumerate([(1 - x, y), (x, 1 - y), (1 - x, 1 - y)]):
            copies.append((p.at[2 * cx + cy], land.at[j], (cx, cy, c), land.at[j]))
    return copies


def _chip_add(name, chip, sums, recv):
    _, H, C = sums.shape
    tr, tc = _tiles(H, C)

    def body(chip_ref, p_ref, r_ref, o_ref):
        total = p_ref[0].astype(F32)
        for j in range(3):
            total = total + r_ref[j].astype(F32)
        o_ref[...] = total

    grid_spec = pltpu.PrefetchScalarGridSpec(
        num_scalar_prefetch=1, grid=(H // tr, C // tc),
        in_specs=[pl.BlockSpec((1, tr, tc), lambda r, q, chip_ref: (chip_ref[0], r, q)),
                  pl.BlockSpec((3, tr, tc), lambda r, q, chip_ref: (0, r, q))],
        out_specs=pl.BlockSpec((tr, tc), lambda r, q, chip_ref: (r, q)))
    return pl.pallas_call(
        body, name=name, grid_spec=grid_spec, out_shape=jax.ShapeDtypeStruct((H, C), F32),
        compiler_params=pltpu.CompilerParams(dimension_semantics=("parallel", "parallel")),
    )(chip, sums, recv)


def _pair_share(name, halves):
    nw = len(halves)

    def body(*refs):
        hs, outs = refs[:nw], refs[nw:2 * nw]
        send_sems, recv_sems = refs[2 * nw:]
        x, y, c = _place()
        copies = []
        for k in range(nw):
            cp = pltpu.make_async_remote_copy(
                src_ref=hs[k], dst_ref=outs[k], send_sem=send_sems.at[k], recv_sem=recv_sems.at[k],
                device_id=(x, y, 1 - c), device_id_type=MESH)
            cp.start()
            copies.append(cp)
        for cp in copies:
            cp.wait()

    return pl.pallas_call(
        body, name=name,
        out_shape=[jax.ShapeDtypeStruct(h.shape, h.dtype) for h in halves],
        in_specs=[_ANY] * nw, out_specs=[_ANY] * nw,
        scratch_shapes=[pltpu.SemaphoreType.DMA((nw,)), pltpu.SemaphoreType.DMA((nw,))],
    )(*halves)


def _adam_halves(name, core, w, g_own, g_other, m, v, out_rows=None):
    R, C = w.shape
    H = R // 2
    tr, tc = _tiles(H, C)
    nr, nc = H // tr, C // tc

    def body(core_ref, w_ref, go_ref, gr_ref, m_ref, v_ref, g_ref, dl_ref, nm_ref, nv_ref):
        own = (pl.program_id(0) // nr) == core_ref[0]
        g = jnp.where(own, go_ref[...], gr_ref[...])
        g_ref[...] = g
        dl_ref[...], nm_ref[...], nv_ref[...] = _adamw(w_ref[...], g, m_ref[...], v_ref[...])

    blk = pl.BlockSpec((tr, tc), lambda r, q, core_ref: (r, q))

    def half_spec(is_own):
        def index(r, q, core_ref):
            mine = ((r // nr) == core_ref[0]) == is_own
            done = is_own == (core_ref[0] == 0)
            return (jnp.where(mine, r % nr, jnp.where(done, nr - 1, 0)), jnp.where(mine, q, jnp.where(done, nc - 1, 0)))
        return pl.BlockSpec((tr, tc), index)
    out_rows = R if out_rows is None else out_rows
    assert R - tr < out_rows <= R, (R, tr, out_rows)
    out = jax.ShapeDtypeStruct((out_rows, C), F32)
    grid_spec = pltpu.PrefetchScalarGridSpec(
        num_scalar_prefetch=1, grid=(R // tr, nc), in_specs=[blk, half_spec(True), half_spec(False), blk, blk],
        out_specs=[blk] * 4)
    return pl.pallas_call(
        body, name=name, grid_spec=grid_spec, out_shape=[out] * 4,
        compiler_params=pltpu.CompilerParams(dimension_semantics=("parallel", "parallel"),
                                             vmem_limit_bytes=_vmem(20 * tr * tc * 4)),
    )(core, w, g_own, g_other, m, v)


def kernel(x, c, w_mod, b_mod, g_pre_mix, g_post_mix, w_in, b_forget, swa_sinks, w_out, g_pre_mlp, g_post_mlp, w_up, w_down, loss_target, m_w_mod, m_b_mod, m_g_pre_mix, m_g_post_mix, m_w_in, m_b_forget, m_swa_sinks, m_w_out, m_g_pre_mlp, m_g_post_mlp, m_w_up, m_w_down, v_w_mod, v_b_mod, v_g_pre_mix, v_g_post_mix, v_w_in, v_b_forget, v_swa_sinks, v_w_out, v_g_pre_mlp, v_g_post_mlp, v_w_up, v_w_down):
    S, D = x.shape[1], x.shape[2]
    n_heads = D // HEAD_DIM
    n_fox = n_heads // 2
    n_swa = n_heads - n_fox
    n_kv = max(1, n_swa // 4)
    fox_w, swa_w, kv_w = n_fox * HEAD_DIM, n_swa * HEAD_DIM, n_kv * HEAD_DIM
    main_w = 3 * fox_w + swa_w + 2 * kv_w
    in_w = main_w + n_fox
    mod_cols = w_mod.shape[2]

    ax, ay, ac = _place()
    chip = 2 * ax + ay
    dev = 2 * chip + ac
    chip_arr = jnp.reshape(chip, (1,)).astype(jnp.int32)
    core_arr = jnp.reshape(ac, (1,)).astype(jnp.int32)

    x2, tgt = x[0], loss_target[0]

    in_rows = in_w // N_CHIPS
    in_rows_pad = -(-in_rows // (2 * LANES)) * (2 * LANES)
    slab_w = N_CHIPS * in_rows_pad

    def rows_of(a):
        return jnp.pad(a[0].T, ((0, in_rows_pad - in_rows), (0, 0)))

    flights = {}
    flights["w_out"] = _ici_start("gather_start_w_out", [_own_slab("own_slab_w_out", chip_arr, w_out[0], None)[0]], [],
                                  _gather_plan)
    w_in_stack, token = _own_slab("own_slab_w_in", chip_arr, rows_of(w_in), flights["w_out"][4])

    c_all, _ = _allgather8("gather_c", _tie(c, token).reshape(8, D // 8))
    c_all = c_all.reshape(N_DEV, D)
    b_shard = lax.dynamic_slice_in_dim(b_mod, chip * mod_cols, mod_cols, axis=1)
    mod_shard = _mod_fwd(jnp.pad(c_all, ((0, 16 - N_DEV), (0, 0))), w_mod[0], b_shard)[:N_DEV]
    mod_all, token = _allgather8("gather_mod", mod_shard)
    mod_all = mod_all.reshape(N_CHIPS, 2, N_DEV, mod_cols)[:, 0]
    mod = lax.dynamic_index_in_dim(mod_all, dev, axis=1, keepdims=False).reshape(N_MOD, 1, D)
    sh_a, sc_a, gt_a, sh_m, sc_m, gt_m = [mod[n] for n in range(N_MOD)]

    def slab_cols(lo, hi):
        spans = []
        while lo < hi:
            s, r = divmod(lo, in_rows)
            n = min(hi - lo, in_rows - r)
            spans.append((s * in_rows_pad + r, s * in_rows_pad + r + n))
            lo += n
        return spans

    gate_lo = 3 * fox_w
    main_spans = slab_cols(0, gate_lo) + slab_cols(gate_lo + n_fox, in_w)
    (gate_first, gate_last), = slab_cols(gate_lo, gate_lo + n_fox)

    for n, w in zip(["w_in", "w_up", "w_down"], [None, w_up[0], w_down[0]]):
        stack = w_in_stack if n == "w_in" else _own_slab("own_slab_" + n, chip_arr, w, token)[0]
        flights[n] = _ici_start("gather_start_" + n, [stack], [], _gather_plan, after=token)
        token = flights[n][4]
    sc_a = _tie(sc_a, token)

    def arrived(n, after):
        send, recv, stacks, _, _ = flights[n]
        stacks, _ = _ici_wait("gather_wait_" + n, send, recv, stacks, [], _gather_plan, after)
        return _ici_start("gather_pass_start_" + n, stacks, [], _pass_plan)

    def gathered(n, after, in_flight=None):
        if in_flight is None:
            send, recv, stacks, _, _ = flights[n]
            stacks, _ = _ici_wait("gather_wait_" + n, send, recv, stacks, [], _gather_plan, after)
            return _pass_to_sibling("gather_pass_" + n, stacks)[0]
        send, recv, stacks, _, _ = in_flight
        return _ici_wait("gather_pass_wait_" + n, send, recv, stacks, [], _pass_plan, after)[0][0]

    d_ff = N_CHIPS * w_up.shape[2]

    h = _pre_norm(x2, g_pre_mix, sc_a, sh_a)
    in_state = [rows_of(w_in)] + [rows_of(_tie(a, token)) for a in (m_w_in, v_w_in)]
    cos, sin_signed = _rope_tables(S)

    def pack(bm, gpm, gqm, gpl, gql, bf, sk):
        last = jnp.concatenate([bf, sk, jnp.zeros((1, D - n_fox - n_swa), F32)], axis=1)
        return jnp.concatenate([bm.reshape(N_MOD, D), gpm, gqm, gpl, gql, last, jnp.zeros((5, D), F32)], axis=0)

    small_state = [pack(b_mod, g_pre_mix, g_post_mix, g_pre_mlp, g_post_mlp, b_forget, swa_sinks),
                   pack(m_b_mod, m_g_pre_mix, m_g_post_mix, m_g_pre_mlp, m_g_post_mlp, m_b_forget, m_swa_sinks),
                   pack(v_b_mod, v_g_pre_mix, v_g_post_mix, v_g_pre_mlp, v_g_post_mlp, v_b_forget, v_swa_sinks)]
    ready = h[:8, :LANES].astype(F32) + cos[:8]
    w_slab_t = gathered("w_in", [ready] + in_state[1:] + small_state).reshape(slab_w, D)
    tm_p, tn_p = _fit(MM_TM, S), _fit(MM_TN if slab_w % MM_TN == 0 else MM_TN // 2, slab_w)
    win0 = gate_first // LANES * LANES
    win_j, win_off = divmod(win0, tn_p)
    assert win_off + 2 * LANES <= tn_p and gate_last - win0 <= 2 * LANES

    def proj_epilogue(acc, ex, outs):
        outs[0][...] = acc.astype(BF16)

        @pl.when(pl.program_id(1) == win_j)
        def _():
            outs[1][...] = acc[:, win_off:win_off + 2 * LANES]

    proj_slab, gate_win = _matmul(
        "in_proj", h, w_slab_t, "nt",
        [((S, slab_w), BF16, (tm_p, tn_p), lambda i, j: (i, j)), ((S, 2 * LANES), F32, (tm_p, 2 * LANES), lambda i, j: (i, 0))],
        proj_epilogue, tn=tn_p, revisits=True)
    proj = jnp.concatenate([proj_slab[:, lo:hi] for lo, hi in main_spans], axis=1)
    out_flight = arrived("w_out", proj_slab)
    fg = _tie(jnp.pad(gate_win[:, gate_first - win0:gate_last - win0], ((0, 0), (0, LANES - n_fox))), out_flight[4])
    b_pad = jnp.pad(b_forget, ((0, 0), (0, LANES - n_fox)))
    cum_row = _fox_gate_fwd(fg, b_pad)[:n_fox].reshape(n_fox, 1, S)
    fox_o, fox_lse = _fox_fwd(proj, cum_row, n_fox)

    rq = _rope("rope_fwd", proj, 3 * n_fox, n_swa + n_kv, cos, sin_signed)
    v_first = 3 * n_fox + n_swa + n_kv
    sinks = swa_sinks[0]
    swa_o, swa_lse = _swa_fwd(rq, proj, v_first, sinks, n_swa, n_kv)

    mixcat = jnp.concatenate([fox_o, swa_o], axis=1).astype(BF16)
    up_flight = arrived("w_up", mixcat)
    w_out_f = gathered("w_out", mixcat, out_flight).reshape(D, D)
    mix = _mm_plain("out_proj", mixcat, w_out_f, "nn", BF16, after=up_flight[4])
    x1, h2 = _post_mix(x2, mix, g_post_mix, gt_a, g_pre_mlp, sc_m, sh_m)
    w_up_f = gathered("w_up", h2, up_flight)

    tm_u, tn_u = _fit(MM_TM, S), _fit(MM_TN, d_ff)

    def up_epilogue(acc, ex, outs):
        outs[0][...] = acc.astype(BF16)
        r = jnp.maximum(acc, 0.0)
        outs[1][...] = (r * r).astype(BF16)

    ublk = ((S, d_ff), BF16, (tm_u, tn_u), lambda i, j: (i, j))
    u, a = _matmul("mlp_up", h2, w_up_f, "nn", [ublk, ublk], up_epilogue)
    w_down_f = gathered("w_down", a).reshape(d_ff, D)
    y = _mm_plain("mlp_down", a, w_down_f, "nn", BF16)

    dy, dout, loss_part, acc_mlp_post = _loss_and_post_mlp_bwd(x1, y, tgt, g_post_mlp, gt_m)

    def du_epilogue(acc, ex, outs):
        outs[0][...] = (acc * (2.0 * jnp.maximum(ex[0][...].astype(F32), 0.0))).astype(BF16)

    du = _matmul("mlp_down_bwd", dy, w_down_f, "nt", [ublk], du_epilogue,
                 extras=[(u, (tm_u, tn_u), lambda i, j: (i, j))])[0]
    def pair_send(tag, part):
        return _ici_start("grad_pair_start_" + tag, [part], [jax.ShapeDtypeStruct(part.shape, BF16)], _share_plan,
                          per_source=1)

    def pair_recv(tag, flight, after):
        send, recv, srcs, lands, _ = flight
        return _ici_wait("grad_pair_wait_" + tag, send, recv, srcs, lands, _share_plan, after)[1][0]

    def scatter_start(tag, sums, after=None):
        return _ici_start("grad_scatter_start_" + tag, sums,
                          [jax.ShapeDtypeStruct((3,) + p.shape[1:], BF16) for p in sums], _scatter_plan, after=after)

    def scatter_finish(tag, flight, after):
        send, recv, srcs, lands, _ = flight
        sums, received = _ici_wait("grad_scatter_wait_" + tag, send, recv, srcs, lands, _scatter_plan, after)
        return [_chip_add("chip_add_%s_%d" % (tag, k), chip_arr, p, r) for k, (p, r) in enumerate(zip(sums, received))]

    tm_g = _fit(MM_TM, D // 2)
    pair_down = pair_send("down", _grad_half("grad_w_down_a", core_arr, a, dy, N_CHIPS, 1, tm_g, True))
    pair_up = pair_send("up", _grad_half("grad_w_up_a", core_arr, h2, du, 1, N_CHIPS, tm_g, True, after=pair_down[4]))
    sum_down = _grad_half("grad_w_down_b", core_arr, a, dy, N_CHIPS, 1, tm_g, False,
                          recv=pair_recv("down", pair_down, pair_up[4]))
    sum_up = _grad_half("grad_w_up_b", core_arr, h2, du, 1, N_CHIPS, tm_g, False, recv=pair_recv("up", pair_up, sum_down))
    flight_mlp = scatter_start("mlp", [sum_up, sum_down])
    dh2 = _mm_plain("mlp_up_bwd", du, w_up_f, "nt", BF16, after=flight_mlp[4])
    dx1, dmix, acc_mid = _pre_mlp_and_post_mix_bwd(dh2, x1, dout, mix, _tie(g_pre_mlp, flight_mlp[4]), sc_m,
                                                   g_post_mix, gt_a)

    dmixcat = _mm_plain("out_proj_bwd", dmix, w_out_f, "nt", F32)

    fdq, fdk, fdv, dcum_row, dcum_q = _fox_bwd(proj, fox_o, dmixcat, fox_lse, cum_row, n_fox)
    dcum_k = jnp.pad(dcum_row.reshape(n_fox, S), ((0, LANES - n_fox), (0, 0)))
    dfg, db_forget = _fox_gate_bwd(dcum_k, dcum_q, fg, b_pad)

    group_w = (n_swa // n_kv) * HEAD_DIM
    sdq, sdk, sdv, dsink = _swa_bwd(rq, proj, v_first, sinks, swa_o, dmixcat, fox_w // group_w, swa_lse, n_swa, n_kv)
    drq = jnp.concatenate([sdq, jnp.transpose(sdk, (1, 0, 2)).reshape(S, kv_w).astype(BF16)], axis=1)
    d_sq_sk = _rope("rope_bwd", drq, 0, n_swa + n_kv, cos, -sin_signed)
    dsv = jnp.transpose(sdv, (1, 0, 2)).reshape(S, kv_w).astype(BF16)
    parts = [fdq, fdk, fdv, dfg, d_sq_sk, dsv]
    widths = [p.shape[1] for p in parts[:3]] + [n_fox] + [p.shape[1] for p in parts[4:]]
    starts = [sum(widths[:k]) for k in range(len(parts) + 1)]
    assert starts[3] == gate_lo and starts[-1] == in_w
    pieces = []
    for s in range(N_CHIPS):
        lo, hi = s * in_rows, (s + 1) * in_rows
        for k in range(len(parts)):
            first, last = max(lo, starts[k]), min(hi, starts[k + 1])
            if first < last:
                pieces.append((k, first - starts[k], last - starts[k]))
        pieces.append((None, 0, in_rows_pad - in_rows))
    dproj_slab = _lay_columns("dproj_slab_order", parts, pieces)

    tm_in, tm_out = in_rows_pad // 2, D // (2 * N_CHIPS)
    pair_in = pair_send("in", _grad_half("grad_w_in_a", core_arr, dproj_slab, h, N_CHIPS, 1, tm_in, True))
    pair_out = pair_send("out", _grad_half("grad_w_out_a", core_arr, mixcat, dmix, N_CHIPS, 1, tm_out, True,
                                           after=pair_in[4]))
    sum_in = _grad_half("grad_w_in_b", core_arr, dproj_slab, h, N_CHIPS, 1, tm_in, False,
                        recv=pair_recv("in", pair_in, pair_out[4]))
    sum_out = _grad_half("grad_w_out_b", core_arr, mixcat, dmix, N_CHIPS, 1, tm_out, False,
                         recv=pair_recv("out", pair_out, sum_in[0, :8, :LANES]))
    dh = _mm_plain("in_proj_bwd", dproj_slab, w_slab_t, "nn", BF16, tk=slab_w // 2,
                   after=sum_out[0, :8, :LANES].astype(F32))
    grad_x, acc_pre = _pre_mix_bwd(dh, x2, dx1, g_pre_mix, sc_a)

    zero_row = jnp.zeros((1, D), F32)
    tail = jnp.concatenate([db_forget[0:1, :n_fox], dsink[:, 0, :n_swa // n_kv].reshape(1, n_swa),
                            loss_part[0:1, 0:1], jnp.zeros((1, D - n_fox - n_swa - 1), F32)], axis=1)
    partial = jnp.concatenate([
        acc_pre[0:1], acc_pre[1:2], acc_mid[3:4], acc_mid[0:1], acc_mid[1:2], acc_mlp_post[0:1],
        acc_pre[2:3], acc_mid[4:5], acc_mid[2:3], acc_mlp_post[1:2], tail] + [zero_row] * 5, axis=0)
    gathered_small, token = _allgather8("gather_small_grads", partial)

    flight_mix = scatter_start("mix", [sum_in, sum_out], after=token)
    halves_mlp = scatter_finish("mlp", flight_mlp, flight_mix[4])
    share_up, share_down = [
        _ici_start("grad_share_start_" + n, [hv], [jax.ShapeDtypeStruct(hv.shape, F32)], _share_plan, per_source=1)
        for n, hv in zip(["up", "down"], halves_mlp)]

    def shared(tag, flight, after):
        send, recv, own, lands, _ = flight
        own, other = _ici_wait("grad_share_wait_" + tag, send, recv, own, lands, _share_plan, after)
        return own[0], other[0]

    def unpack(p):
        return {"b_mod": p[0:N_MOD].reshape(1, N_MOD * D), "g_pre_mix": p[6:7], "g_post_mix": p[7:8],
                "g_pre_mlp": p[8:9], "g_post_mlp": p[9:10], "b_forget": p[10:11, :n_fox],
                "swa_sinks": p[10:11, n_fox:n_fox + n_swa]}

    small_out = _small_update(gathered_small, _tie(small_state[0], share_down[4] + share_up[4]), small_state[1],
                              small_state[2])
    g_small, d_small, m_small, v_small = [unpack(p) for p in small_out]
    loss = small_out[0][N_MOD + 4, n_fox + n_swa]

    dmod_all = gathered_small.reshape(N_DEV, 16, D)[:, :N_MOD].reshape(N_DEV, N_MOD * D)
    dmod_shard = _tie(lax.dynamic_slice_in_dim(dmod_all, chip * mod_cols, mod_cols, axis=1), share_down[4])
    g_w_mod, d_w_mod, nm_w_mod, nv_w_mod = _mod_update(c_all.T, dmod_shard, w_mod[0], m_w_mod[0], v_w_mod[0])

    grads = dict(g_small, w_mod=g_w_mod[None])
    deltas = dict(d_small, w_mod=d_w_mod[None])
    new_m = dict(m_small, w_mod=nm_w_mod[None])
    new_v = dict(v_small, w_mod=nv_w_mod[None])
    weights = {"w_in": (w_in, m_w_in, v_w_in), "w_out": (w_out, m_w_out, v_w_out), "w_up": (w_up, m_w_up, v_w_up),
               "w_down": (w_down, m_w_down, v_w_down)}

    updated = {}

    def big_update(n, own, other):
        transposed = n == "w_in"
        w, m, v = in_state if transposed else [a[0] for a in weights[n]]
        outs = _adam_halves("adam_" + n, core_arr, w, own, other, m, v, out_rows=in_rows if transposed else None)
        updated[n] = outs[1][:8, :LANES]
        if transposed:
            outs = [o.T for o in outs]
        grads[n], deltas[n], new_m[n], new_v[n] = [o[None] for o in outs]

    big_update("w_down", *shared("down", share_down, d_w_mod[:8, :LANES] + small_out[1][:8, :LANES]))
    halves_mix = scatter_finish("mix", flight_mix, updated["w_down"] + d_w_mod[:8, :LANES])
    others_mix = _pair_share("grad_pair_share_mix", halves_mix)
    big_update("w_in", halves_mix[0], others_mix[0])
    big_update("w_out", halves_mix[1], others_mix[1])
    big_update("w_up", *shared("up", share_up, updated["w_out"] + updated["w_in"]))

    order = ["w_mod", "b_mod", "g_pre_mix", "g_post_mix", "w_in", "b_forget", "swa_sinks", "w_out", "g_pre_mlp",
             "g_post_mlp", "w_up", "w_down"]
    return (loss, grad_x[None], *[grads[n] for n in order], *[deltas[n] for n in order],
            *[new_m[n] for n in order], *[new_v[n] for n in order])
```

```python
import jax
import jax.numpy as jnp
from jax import lax
from jax.experimental import pallas as pl
from jax.experimental.pallas import tpu as pltpu

F32 = jnp.float32
BF16 = jnp.bfloat16
MESH = pl.DeviceIdType.MESH

HEAD_DIM = 128
SWA_BLOCK = 128
ROPE_THETA = 10000.0
NORM_EPS = 1e-6
NEG = -1e30
N_MOD = 6
ADAM_LR = 0.001
ADAM_B1 = 0.9
ADAM_B2 = 0.999
ADAM_EPS = 1e-08
ADAM_WD = 0.01
ADAM_STEP = 10
N_CHIPS = 4
N_DEV = 8
LANES = 128
VMEM_CAP = 60 * 1024 * 1024

_NN = (((1,), (0,)), ((), ()))
_NT = (((1,), (1,)), ((), ()))
_TN = (((0,), (0,)), ((), ()))


def _vmem(nbytes):
    return int(min(VMEM_CAP, nbytes * 5 // 4 + (4 << 20)))


def _nbytes(shape, dtype):
    n = 1
    for s in shape:
        n *= s
    return n * jnp.dtype(dtype).itemsize


def _fit(t, n):
    t = min(t, n)
    assert n % t == 0, (t, n)
    return t


MM_TM, MM_TN, MM_TK = 1024, 1024, 2048


def _matmul(name, a, b, mode, out_defs, epilogue, extras=(), tm=MM_TM, tn=MM_TN, tk=MM_TK, revisits=False,
            row_sel=None):
    stacked = b.ndim == 3
    b_rows, b_cols = b.shape[-2], b.shape[-1] * (b.shape[0] if stacked else 1)
    if mode == "nn":
        (M, K), (K2, N) = a.shape, (b_rows, b_cols)
    elif mode == "nt":
        (M, K), (N, K2) = a.shape, (b_rows, b_cols)
    else:
        (K, M), (K2, N) = a.shape, (b_rows, b_cols)
    assert K == K2 and not (stacked and mode == "tn"), (a.shape, b.shape, mode)
    tm = _fit(tm, M)
    tn = _fit(tn, b.shape[-1] if stacked and mode == "nn" else N)
    tk = _fit(tk, b.shape[-1] if stacked and mode == "nt" else K)
    nk = K // tk
    dims = {"nn": _NN, "nt": _NT, "tn": _TN}[mode]
    if row_sel is None:
        grid_m, a_row = M // tm, lambda i, *sel: i
    else:
        grid_m, a_row = row_sel[2], lambda i, *sel: row_sel[1](i, sel[0])
    a_spec = (pl.BlockSpec((tk, tm), lambda i, j, k, *sel: (k, a_row(i, *sel))) if mode == "tn"
              else pl.BlockSpec((tm, tk), lambda i, j, k, *sel: (a_row(i, *sel), k)))
    if stacked:
        per = b.shape[-1] // (tk if mode == "nt" else tn)
        b_spec = (pl.BlockSpec((1, tn, tk), lambda i, j, k, *sel: (k // per, j, k % per)) if mode == "nt"
                  else pl.BlockSpec((1, tk, tn), lambda i, j, k, *sel: (j // per, k, j % per)))
    else:
        b_spec = (pl.BlockSpec((tn, tk), lambda i, j, k, *sel: (j, k)) if mode == "nt"
                  else pl.BlockSpec((tk, tn), lambda i, j, k, *sel: (k, j)))
    n_ex, n_out = len(extras), len(out_defs)

    def body(*refs):
        if row_sel is not None:
            refs = refs[1:]
        a_ref, b_ref = refs[0], refs[1]
        ex = refs[2:2 + n_ex]
        outs = refs[2 + n_ex:2 + n_ex + n_out]
        b_blk = b_ref[0] if stacked else b_ref[...]
        prod = lax.dot_general(a_ref[...], b_blk, dims, preferred_element_type=F32)
        if nk == 1:
            epilogue(prod, ex, outs)
        else:
            acc_ref = refs[-1]
            k = pl.program_id(2)

            @pl.when(k == 0)
            def _():
                acc_ref[...] = prod

            @pl.when(k > 0)
            def _():
                acc_ref[...] += prod

            @pl.when(k == nk - 1)
            def _():
                epilogue(acc_ref[...], ex, outs)

    def wrap(f):
        return lambda i, j, k, *sel: f(i, j)

    in_specs = [a_spec, b_spec] + [pl.BlockSpec(blk, wrap(f)) for _, blk, f in extras]
    out_specs = [pl.BlockSpec(blk, wrap(f)) for _, _, blk, f in out_defs]
    out_shape = [jax.ShapeDtypeStruct(s, d) for s, d, _, _ in out_defs]
    need = 2 * (tm * tk + tk * tn) * a.dtype.itemsize + 3 * tm * tn * 4
    need += sum(2 * _nbytes(blk, arr.dtype) for arr, blk, _ in extras)
    need += sum(2 * _nbytes(blk, d) for _, d, blk, _ in out_defs)
    grid = (grid_m, N // tn, nk)
    scratch = [pltpu.VMEM((tm, tn), F32)] if nk > 1 else []
    params = pltpu.CompilerParams(
        dimension_semantics=("parallel", "arbitrary" if revisits else "parallel", "arbitrary"),
        vmem_limit_bytes=_vmem(need))
    operands = (a, b, *[arr for arr, _, _ in extras])
    if row_sel is None:
        return pl.pallas_call(body, name=name, grid=grid, in_specs=in_specs, out_specs=out_specs, out_shape=out_shape,
                              scratch_shapes=scratch, compiler_params=params)(*operands)
    grid_spec = pltpu.PrefetchScalarGridSpec(num_scalar_prefetch=1, grid=grid, in_specs=in_specs, out_specs=out_specs,
                                             scratch_shapes=scratch)
    return pl.pallas_call(body, name=name, grid_spec=grid_spec, out_shape=out_shape,
                          compiler_params=params)(row_sel[0], *operands)


def _grad_half(name, core, a, b, row_slabs, col_slabs, tm, other, recv=None, after=None):
    (_, M), (_, N) = a.shape, b.shape
    H = M // (2 * row_slabs)
    nh = H // tm
    tn = _fit(MM_TN, N // col_slabs)
    per = N // col_slabs // tn

    def a_block(i, core_ref):
        half = (1 - core_ref[0]) if other else core_ref[0]
        return (i // nh) * (2 * nh) + half * nh + i % nh

    def out_index(i, j):
        return (j // per, i, j % per) if col_slabs > 1 else (i // nh, i % nh, j)

    slabs = max(row_slabs, col_slabs)
    out_def = ((slabs, H, N // col_slabs), BF16, (1, tm, tn), out_index)

    def epilogue(acc, ex, outs):
        outs[0][0] = (acc if recv is None else acc + ex[0][0].astype(F32)).astype(BF16)

    extras = ([] if recv is None else [(recv, (1, tm, tn), out_index)]) + ([] if after is None else [_behind(after)])
    return _matmul(name, a, b, "tn", [out_def], epilogue, extras=extras, tm=tm, tn=tn,
                   row_sel=(core, a_block, row_slabs * nh))[0]


def _behind(token):
    return (token, (8, LANES), lambda i, j: (0, 0))


def _mm_plain(name, a, b, mode, out_dtype, after=None, **tiles):
    if mode == "nn":
        M, N = a.shape[0], b.shape[-1] * (b.shape[0] if b.ndim == 3 else 1)
    elif mode == "nt":
        M, N = a.shape[0], b.shape[-2]
    else:
        M, N = a.shape[1], b.shape[1]
    tm, tn = _fit(tiles.get("tm", MM_TM), M), _fit(tiles.get("tn", MM_TN), N)

    def epi(acc, ex, outs):
        outs[0][...] = acc.astype(out_dtype)

    return _matmul(name, a, b, mode, [((M, N), out_dtype, (tm, tn), lambda i, j: (i, j))], epi,
                   extras=[] if after is None else [_behind(after)], **tiles)[0]


def _rstd(v):
    return lax.rsqrt(jnp.mean(v * v, axis=-1, keepdims=True) + NORM_EPS)


ROW_TILE = 256


def _row_call(name, body, row_ins, vec_ins, row_outs, acc_outs, S, D):
    tr = _fit(ROW_TILE, S)
    row_spec = pl.BlockSpec((tr, D), lambda r: (r, 0))
    vec_spec = pl.BlockSpec((1, D), lambda r: (0, 0))
    in_specs = [row_spec] * len(row_ins) + [vec_spec] * len(vec_ins)
    out_specs = [row_spec] * len(row_outs) + [pl.BlockSpec(shp, lambda r: (0, 0)) for shp in acc_outs]
    out_shape = [jax.ShapeDtypeStruct((S, D), d) for d in row_outs] + [jax.ShapeDtypeStruct(shp, F32) for shp in acc_outs]
    need = sum(2 * tr * D * a.dtype.itemsize for a in row_ins) + sum(2 * tr * D * jnp.dtype(d).itemsize for d in row_outs)
    need += 8 * tr * D * 4
    return pl.pallas_call(
        body, name=name, grid=(S // tr,), in_specs=in_specs, out_specs=out_specs, out_shape=out_shape,
        compiler_params=pltpu.CompilerParams(dimension_semantics=("arbitrary",), vmem_limit_bytes=_vmem(need)),
    )(*row_ins, *vec_ins)


def _acc_rows(ref, rows):
    @pl.when(pl.program_id(0) == 0)
    def _():
        ref[...] = jnp.zeros_like(ref)
    for n, r in enumerate(rows):
        ref[n:n + 1, :] += r


def _pre_norm(x, g, sc, sh):
    S, D = x.shape

    def body(x_ref, g_ref, sc_ref, sh_ref, h_ref):
        xv = x_ref[...]
        xn = xv * _rstd(xv)
        h_ref[...] = (xn * g_ref[...] * (1.0 + sc_ref[...]) + sh_ref[...]).astype(BF16)

    return _row_call("pre_norm_mix", body, [x], [g, sc, sh], [BF16], [], S, D)[0]


def _post_mix(x, mix, g_post, gt, g_pre, sc, sh):
    S, D = x.shape

    def body(x_ref, mix_ref, gp_ref, gt_ref, g2_ref, sc_ref, sh_ref, x1_ref, h2_ref):
        mv = mix_ref[...].astype(F32)
        x1 = x_ref[...] + gt_ref[...] * (mv * _rstd(mv) * gp_ref[...])
        x1_ref[...] = x1
        h2_ref[...] = (x1 * _rstd(x1) * g2_ref[...] * (1.0 + sc_ref[...]) + sh_ref[...]).astype(BF16)

    return _row_call("post_mix_pre_mlp", body, [x, mix], [g_post, gt, g_pre, sc, sh], [F32, BF16], [], S, D)


def _loss_and_post_mlp_bwd(x1, y, target, g_post, gt):
    S, D = x1.shape

    def body(x1_ref, y_ref, t_ref, g_ref, gt_ref, dy_ref, dout_ref, loss_ref, acc_ref):
        yv = y_ref[...].astype(F32)
        r = _rstd(yv)
        yh = yv * r
        n = yh * g_ref[...]
        diff = x1_ref[...] + gt_ref[...] * n - t_ref[...]
        dout = diff * (1.0 / D)
        dout_ref[...] = dout
        dn = dout * gt_ref[...]
        dyh = dn * g_ref[...]
        dy_ref[...] = (r * (dyh - yh * jnp.mean(dyh * yh, axis=-1, keepdims=True))).astype(BF16)
        _acc_rows(acc_ref, [jnp.sum(dout * n, axis=0, keepdims=True), jnp.sum(dn * yh, axis=0, keepdims=True)])

        @pl.when(pl.program_id(0) == 0)
        def _():
            loss_ref[...] = jnp.zeros_like(loss_ref)
        loss_ref[...] += jnp.full(loss_ref.shape, (0.5 / D) * jnp.sum(diff * diff), F32)

    return _row_call("loss_post_mlp_bwd", body, [x1, y, target], [g_post, gt], [BF16, F32],
                     [(8, LANES), (8, D)], S, D)


def _pre_mlp_and_post_mix_bwd(dh2, x1, dout, mix, g_pre, sc, g_post, gt):
    S, D = x1.shape

    def body(dh_ref, x1_ref, dout_ref, mix_ref, g_ref, sc_ref, gp_ref, gt_ref, dx1_ref, dmix_ref, acc_ref):
        dh = dh_ref[...].astype(F32)
        x1v = x1_ref[...]
        r3 = _rstd(x1v)
        xn = x1v * r3
        dxn = dh * (1.0 + sc_ref[...]) * g_ref[...]
        dx1 = dout_ref[...] + r3 * (dxn - xn * jnp.mean(dxn * xn, axis=-1, keepdims=True))
        dx1_ref[...] = dx1
        mv = mix_ref[...].astype(F32)
        r2 = _rstd(mv)
        mh = mv * r2
        dn = dx1 * gt_ref[...]
        dmh = dn * gp_ref[...]
        dmix_ref[...] = (r2 * (dmh - mh * jnp.mean(dmh * mh, axis=-1, keepdims=True))).astype(BF16)
        _acc_rows(acc_ref, [
            jnp.sum(dh, axis=0, keepdims=True),
            jnp.sum(dh * xn * g_ref[...], axis=0, keepdims=True),
            jnp.sum(dh * (1.0 + sc_ref[...]) * xn, axis=0, keepdims=True),
            jnp.sum(dx1 * mh * gp_ref[...], axis=0, keepdims=True),
            jnp.sum(dn * mh, axis=0, keepdims=True)])

    return _row_call("pre_mlp_post_mix_bwd", body, [dh2, x1, dout, mix], [g_pre, sc, g_post, gt], [F32, BF16],
                     [(8, D)], S, D)


def _pre_mix_bwd(dh, x, dx1, g_pre, sc):
    S, D = x.shape

    def body(dh_ref, x_ref, dx1_ref, g_ref, sc_ref, gx_ref, acc_ref):
        dhv = dh_ref[...].astype(F32)
        xv = x_ref[...]
        r = _rstd(xv)
        xn = xv * r
        dxn = dhv * (1.0 + sc_ref[...]) * g_ref[...]
        gx_ref[...] = dx1_ref[...] + r * (dxn - xn * jnp.mean(dxn * xn, axis=-1, keepdims=True))
        _acc_rows(acc_ref, [
            jnp.sum(dhv, axis=0, keepdims=True),
            jnp.sum(dhv * xn * g_ref[...], axis=0, keepdims=True),
            jnp.sum(dhv * (1.0 + sc_ref[...]) * xn, axis=0, keepdims=True)])

    return _row_call("pre_mix_bwd", body, [dh, x, dx1], [g_pre, sc], [F32], [(8, D)], S, D)


CUM_BLOCK = 256


def _tri(n, upper):
    r = lax.broadcasted_iota(jnp.int32, (n, n), 0)
    c = lax.broadcasted_iota(jnp.int32, (n, n), 1)
    return ((c >= r) if upper else (c <= r)).astype(F32)


def _fox_gate_fwd(fg, b_pad):
    S = fg.shape[0]
    cb = _fit(CUM_BLOCK, S)

    def body(fg_ref, b_ref, cumt_ref, cum_ref):
        low = _tri(cb, False)
        carry = jnp.zeros((1, LANES), F32)
        for n in range(S // cb):
            z = fg_ref[n * cb:(n + 1) * cb, :] + b_ref[...]
            logf = jnp.minimum(z, 0.0) - jnp.log(1.0 + jnp.exp(-jnp.abs(z)))
            blk = jnp.dot(low, logf, precision=lax.Precision.HIGHEST, preferred_element_type=F32) + carry
            cum_ref[n * cb:(n + 1) * cb, :] = blk
            carry = blk[cb - 1:cb, :]
        cumt_ref[...] = cum_ref[...].T

    return pl.pallas_call(
        body, name="fox_gate_fwd", out_shape=jax.ShapeDtypeStruct((LANES, S), F32),
        scratch_shapes=[pltpu.VMEM((S, LANES), F32)],
        compiler_params=pltpu.CompilerParams(vmem_limit_bytes=_vmem(6 * S * LANES * 4)),
    )(fg, b_pad)


def _fox_gate_bwd(dcum_k, dcum_q, fg, b_pad):
    S = fg.shape[0]
    n_fox = dcum_q.shape[0]
    cb = _fit(CUM_BLOCK, S)

    def body(dk_ref, dq_ref, fg_ref, b_ref, dfg_ref, db_ref, dc_ref):
        lane = lax.broadcasted_iota(jnp.int32, (S, LANES), 1)
        dc = dk_ref[...].T
        for h in range(n_fox):
            dc = dc + jnp.where(lane == h, dq_ref[h], 0.0)
        dc_ref[...] = dc
        up = _tri(cb, True)
        carry = jnp.zeros((1, LANES), F32)
        db = jnp.zeros((1, LANES), F32)
        for n in reversed(range(S // cb)):
            blk = jnp.dot(up, dc_ref[n * cb:(n + 1) * cb, :], precision=lax.Precision.HIGHEST,
                          preferred_element_type=F32) + carry
            carry = blk[0:1, :]
            z = fg_ref[n * cb:(n + 1) * cb, :] + b_ref[...]
            dfg = blk * (1.0 / (1.0 + jnp.exp(z)))
            dfg_ref[n * cb:(n + 1) * cb, :] = dfg.astype(BF16)
            db = db + jnp.sum(dfg, axis=0, keepdims=True)
        db_ref[...] = jnp.broadcast_to(db, db_ref.shape)

    return pl.pallas_call(
        body, name="fox_gate_bwd",
        out_shape=[jax.ShapeDtypeStruct((S, LANES), BF16), jax.ShapeDtypeStruct((8, LANES), F32)],
        scratch_shapes=[pltpu.VMEM((S, LANES), F32)],
        compiler_params=pltpu.CompilerParams(vmem_limit_bytes=_vmem((8 + 2 * n_fox) * S * LANES * 4)),
    )(dcum_k, dcum_q, fg, b_pad)


FOX_TILE = 512


LOG2E = 1.4426950408889634


def _fox_scores(q, k, ck2, masked, t):
    s = lax.dot_general(q, k, _NT, preferred_element_type=F32) * (HEAD_DIM ** -0.5 * LOG2E) - ck2
    if masked:
        row = lax.broadcasted_iota(jnp.int32, (t, t), 0)
        col = lax.broadcasted_iota(jnp.int32, (t, t), 1)
        s = jnp.where(col <= row, s, NEG)
    return s


def _fox_fwd(proj, cum_row, n_fox):
    S = proj.shape[0]
    t = _fit(FOX_TILE, S)
    nq = S // t

    def body(q_ref, k_ref, v_ref, ck_ref, o_ref, lse_ref):
        def q_block(qi, _):
            q0 = pl.multiple_of(qi * t, t)
            q = q_ref[pl.ds(q0, t), :]

            def kv_block(j, carry, masked):
                m, l, acc = carry
                k0 = pl.multiple_of(j * t, t)
                s = _fox_scores(q, k_ref[pl.ds(k0, t), :], ck_ref[0, :, pl.ds(k0, t)] * LOG2E, masked, t)
                m_new = jnp.maximum(m, jnp.max(s, axis=-1, keepdims=True))
                alpha = jnp.exp2(m - m_new)
                p = jnp.exp2(s - m_new)
                l = alpha * l + jnp.sum(p, axis=-1, keepdims=True)
                acc = alpha * acc + jnp.dot(p.astype(BF16), v_ref[pl.ds(k0, t), :], preferred_element_type=F32)
                return m_new, l, acc

            init = (jnp.full((t, 1), NEG, F32), jnp.zeros((t, 1), F32), jnp.zeros((t, HEAD_DIM), F32))
            carry = lax.fori_loop(0, qi, lambda j, cr: kv_block(j, cr, False), init)
            m, l, acc = kv_block(qi, carry, True)
            o_ref[pl.ds(q0, t), :] = acc / l
            lse_ref[0, pl.ds(q0, t), :] = jnp.broadcast_to(m + jnp.log(l) * LOG2E, (t, LANES))
            return 0

        lax.fori_loop(0, nq, q_block, 0)

    col = lambda off: pl.BlockSpec((S, HEAD_DIM), lambda h: (0, off + h))
    per_head = pl.BlockSpec((1, S, LANES), lambda h: (h, 0, 0))
    return pl.pallas_call(
        body, name="fox_fwd", grid=(n_fox,),
        in_specs=[col(0), col(n_fox), col(2 * n_fox), pl.BlockSpec((1, 1, S), lambda h: (h, 0, 0))],
        out_specs=[pl.BlockSpec((S, HEAD_DIM), lambda h: (0, h)), per_head],
        out_shape=[jax.ShapeDtypeStruct((S, n_fox * HEAD_DIM), F32), jax.ShapeDtypeStruct((n_fox, S, LANES), F32)],
        compiler_params=pltpu.CompilerParams(dimension_semantics=("parallel",),
                                             vmem_limit_bytes=_vmem(16 * S * HEAD_DIM * 4 + 12 * t * t * 4)),
    )(proj, proj, proj, cum_row)


def _fox_bwd(proj, o, do, lse_b, cum_row, n_fox):
    S = proj.shape[0]
    t = _fit(FOX_TILE, S)
    nq = S // t
    scale = HEAD_DIM ** -0.5

    def body(q_ref, k_ref, v_ref, o_ref, do_ref, lse_ref, ck_ref, dq_ref, dk_ref, dv_ref, dc_ref, dcq_ref,
             dq_acc, delta_ref):
        dq_acc[...] = jnp.zeros_like(dq_acc)
        dcq_ref[...] = jnp.zeros_like(dcq_ref)

        def delta_block(qi, _):
            q0 = pl.multiple_of(qi * t, t)
            d = jnp.sum(do_ref[pl.ds(q0, t), :] * o_ref[pl.ds(q0, t), :], axis=-1, keepdims=True)
            delta_ref[pl.ds(q0, t), :] = jnp.broadcast_to(d, (t, LANES))
            return 0

        lax.fori_loop(0, nq, delta_block, 0)

        def kv_block(j, _):
            k0 = pl.multiple_of(j * t, t)
            k = k_ref[pl.ds(k0, t), :]
            v = v_ref[pl.ds(k0, t), :]
            ck2 = ck_ref[0, :, pl.ds(k0, t)] * LOG2E

            def q_block(qi, carry, masked):
                dk, dv, dc = carry
                q0 = pl.multiple_of(qi * t, t)
                q = q_ref[pl.ds(q0, t), :]
                dov = do_ref[pl.ds(q0, t), :].astype(BF16)
                p = jnp.exp2(_fox_scores(q, k, ck2, masked, t) - lse_ref[0, pl.ds(q0, t), :][:, :1])
                dp = lax.dot_general(dov, v, _NT, preferred_element_type=F32)
                ds = p * (dp - delta_ref[pl.ds(q0, t), :][:, :1])
                dsb = ds.astype(BF16)
                dv = dv + lax.dot_general(p.astype(BF16), dov, _TN, preferred_element_type=F32)
                dk = dk + lax.dot_general(dsb, q, _TN, preferred_element_type=F32)
                dq_acc[pl.ds(q0, t), :] += jnp.dot(dsb, k, preferred_element_type=F32)
                dc = dc - jnp.sum(ds, axis=0, keepdims=True)
                dcq_ref[0, pl.ds(q0, t), :] += jnp.broadcast_to(jnp.sum(ds, axis=1, keepdims=True), (t, LANES))
                return dk, dv, dc

            init = (jnp.zeros((t, HEAD_DIM), F32), jnp.zeros((t, HEAD_DIM), F32), jnp.zeros((1, t), F32))
            carry = q_block(j, init, True)
            dk, dv, dc = lax.fori_loop(j + 1, nq, lambda qi, cr: q_block(qi, cr, False), carry)
            dk_ref[pl.ds(k0, t), :] = (dk * scale).astype(BF16)
            dv_ref[pl.ds(k0, t), :] = dv.astype(BF16)
            dc_ref[0, :, pl.ds(k0, t)] = dc
            return 0

        lax.fori_loop(0, nq, kv_block, 0)
        dq_ref[...] = (dq_acc[...] * scale).astype(BF16)

    col = lambda off: pl.BlockSpec((S, HEAD_DIM), lambda h: (0, off + h))
    per_head = pl.BlockSpec((1, S, LANES), lambda h: (h, 0, 0))
    row = pl.BlockSpec((1, 1, S), lambda h: (h, 0, 0))
    grad = jax.ShapeDtypeStruct((S, n_fox * HEAD_DIM), BF16)
    return pl.pallas_call(
        body, name="fox_bwd", grid=(n_fox,),
        in_specs=[col(0), col(n_fox), col(2 * n_fox), col(0), col(0), per_head, row],
        out_specs=[col(0), col(0), col(0), row, per_head],
        out_shape=[grad, grad, grad, jax.ShapeDtypeStruct((n_fox, 1, S), F32), jax.ShapeDtypeStruct((n_fox, S, LANES), F32)],
        scratch_shapes=[pltpu.VMEM((S, HEAD_DIM), F32), pltpu.VMEM((S, LANES), F32)],
        compiler_params=pltpu.CompilerParams(dimension_semantics=("parallel",),
                                             vmem_limit_bytes=_vmem(24 * S * HEAD_DIM * 4 + 16 * t * t * 4)),
    )(proj, proj, proj, o, do, lse_b, cum_row)


def _rope_tables(S):
    half = HEAD_DIM // 2
    inv_freq = 1.0 / (ROPE_THETA ** (jnp.arange(half, dtype=F32) * (2.0 / HEAD_DIM)))
    ang = jnp.arange(S).astype(F32)[:, None] * inv_freq[None, :]
    cos, sin = jnp.cos(ang), jnp.sin(ang)
    return jnp.concatenate([cos, cos], axis=-1), jnp.concatenate([-sin, sin], axis=-1)


def _rope(name, src, first_block, n_blocks, cos, sin_signed):
    S = src.shape[0]

    def body(x_ref, cos_ref, sin_ref, o_ref):
        xv = x_ref[...].astype(F32)
        o_ref[...] = (xv * cos_ref[...] + pltpu.roll(xv, HEAD_DIM // 2, 1) * sin_ref[...]).astype(BF16)

    table = pl.BlockSpec((S, HEAD_DIM), lambda n: (0, 0))
    return pl.pallas_call(
        body, name=name, grid=(n_blocks,),
        in_specs=[pl.BlockSpec((S, HEAD_DIM), lambda n: (0, first_block + n)), table, table],
        out_specs=pl.BlockSpec((S, HEAD_DIM), lambda n: (0, n)),
        out_shape=jax.ShapeDtypeStruct((S, n_blocks * HEAD_DIM), BF16),
        compiler_params=pltpu.CompilerParams(dimension_semantics=("parallel",),
                                             vmem_limit_bytes=_vmem(12 * S * HEAD_DIM * 4)),
    )(src, cos, sin_signed)


def _swa_tile(q_ref, kp_ref, kc_ref, n, group, scale):
    B = SWA_BLOCK
    qs = jnp.concatenate([q_ref[:, g * HEAD_DIM:(g + 1) * HEAD_DIM] for g in range(group)], axis=0)
    kcat = jnp.concatenate([kp_ref[...], kc_ref[...]], axis=0)
    s = lax.dot_general(qs, kcat, _NT, preferred_element_type=F32) * scale
    qi = lax.broadcasted_iota(jnp.int32, (group * B, 2 * B), 0) % B
    kj = lax.broadcasted_iota(jnp.int32, (group * B, 2 * B), 1)
    diff = qi + B - kj
    mask = (diff >= 0) & (diff < B) & ((n * B + kj - B) >= 0)
    return qs, kcat, jnp.where(mask, s, NEG)


def _swa_sink_col(sink_ref, kv, group):
    head = lax.broadcasted_iota(jnp.int32, (group * SWA_BLOCK, 1), 0) // SWA_BLOCK
    col = jnp.zeros((group * SWA_BLOCK, 1), F32)
    for g in range(group):
        col = jnp.where(head == g, sink_ref[kv * group + g], col)
    return col


def _swa_specs(n_kv, group, q_first, k_first, v_first):
    B = SWA_BLOCK
    prev = lambda n: jnp.maximum(n - 1, 0)
    return [
        pl.BlockSpec((B, group * HEAD_DIM), lambda kv, n: (n, q_first + kv)),
        pl.BlockSpec((B, HEAD_DIM), lambda kv, n: (prev(n), k_first + kv)),
        pl.BlockSpec((B, HEAD_DIM), lambda kv, n: (n, k_first + kv)),
        pl.BlockSpec((B, HEAD_DIM), lambda kv, n: (prev(n), v_first + kv)),
        pl.BlockSpec((B, HEAD_DIM), lambda kv, n: (n, v_first + kv)),
    ]


def _swa_fwd(rq, proj, v_first, sinks, n_q, n_kv):
    S = rq.shape[0]
    B = SWA_BLOCK
    group = n_q // n_kv
    scale = HEAD_DIM ** -0.5

    def body(q_ref, kp_ref, kc_ref, vp_ref, vc_ref, sink_ref, o_ref, lse_ref):
        kv, n = pl.program_id(0), pl.program_id(1)
        _, _, s = _swa_tile(q_ref, kp_ref, kc_ref, n, group, scale)
        sink = _swa_sink_col(sink_ref, kv, group)
        m = jnp.maximum(jnp.max(s, axis=-1, keepdims=True), sink)
        p = jnp.exp(s - m)
        denom = jnp.sum(p, axis=-1, keepdims=True) + jnp.exp(sink - m)
        vcat = jnp.concatenate([vp_ref[...], vc_ref[...]], axis=0)
        o = jnp.dot((p / denom).astype(BF16), vcat, preferred_element_type=F32)
        lse = m + jnp.log(denom)
        for g in range(group):
            o_ref[:, g * HEAD_DIM:(g + 1) * HEAD_DIM] = o[g * B:(g + 1) * B, :]
            lse_ref[0, :, g * LANES:(g + 1) * LANES] = jnp.broadcast_to(lse[g * B:(g + 1) * B, :], (B, LANES))

    specs = _swa_specs(n_kv, group, 0, n_q, v_first)
    q_blk = pl.BlockSpec((B, group * HEAD_DIM), lambda kv, n: (n, kv))
    return pl.pallas_call(
        body, name="swa_fwd", grid=(n_kv, S // B),
        in_specs=specs + [pl.BlockSpec(memory_space=pltpu.SMEM)],
        out_specs=[q_blk, pl.BlockSpec((1, B, group * LANES), lambda kv, n: (kv, n, 0))],
        out_shape=[jax.ShapeDtypeStruct((S, n_q * HEAD_DIM), F32), jax.ShapeDtypeStruct((n_kv, S, group * LANES), F32)],
        compiler_params=pltpu.CompilerParams(dimension_semantics=("parallel", "arbitrary")),
    )(rq, rq, rq, proj, proj, sinks)


def _swa_bwd(rq, proj, v_first, sinks, o, do, do_first, lse_b, n_q, n_kv):
    S = rq.shape[0]
    B = SWA_BLOCK
    group = n_q // n_kv
    scale = HEAD_DIM ** -0.5

    def body(q_ref, kp_ref, kc_ref, vp_ref, vc_ref, o_ref, do_ref, lse_ref, sink_ref,
             dq_ref, dk_ref, dv_ref, dsink_ref):
        kv, n = pl.program_id(0), pl.program_id(1)

        @pl.when(n == 0)
        def _():
            dk_ref[...] = jnp.zeros_like(dk_ref)
            dv_ref[...] = jnp.zeros_like(dv_ref)
            dsink_ref[...] = jnp.zeros_like(dsink_ref)

        qs, kcat, s = _swa_tile(q_ref, kp_ref, kc_ref, n, group, scale)
        sink = _swa_sink_col(sink_ref, kv, group)
        stack = lambda ref, w: jnp.concatenate([ref[:, g * w:(g + 1) * w] for g in range(group)], axis=0)
        lse = jnp.concatenate([lse_ref[0, :, g * LANES:g * LANES + 1] for g in range(group)], axis=0)
        do32 = stack(do_ref, HEAD_DIM)
        delta = jnp.sum(do32 * stack(o_ref, HEAD_DIM), axis=-1, keepdims=True)
        dov = do32.astype(BF16)
        p = jnp.exp(s - lse)
        vcat = jnp.concatenate([vp_ref[...], vc_ref[...]], axis=0)
        dp = lax.dot_general(dov, vcat, _NT, preferred_element_type=F32)
        ds = p * (dp - delta)
        dsb = ds.astype(BF16)
        dq = jnp.dot(dsb, kcat, preferred_element_type=F32) * scale
        for g in range(group):
            dq_ref[:, g * HEAD_DIM:(g + 1) * HEAD_DIM] = dq[g * B:(g + 1) * B, :].astype(BF16)
        dkcat = lax.dot_general(dsb, qs, _TN, preferred_element_type=F32) * scale
        dvcat = lax.dot_general(p.astype(BF16), dov, _TN, preferred_element_type=F32)
        prev0 = pl.multiple_of(jnp.maximum(n - 1, 0) * B, B)
        cur0 = pl.multiple_of(n * B, B)
        dk_ref[0, pl.ds(prev0, B), :] += dkcat[:B, :]
        dk_ref[0, pl.ds(cur0, B), :] += dkcat[B:, :]
        dv_ref[0, pl.ds(prev0, B), :] += dvcat[:B, :]
        dv_ref[0, pl.ds(cur0, B), :] += dvcat[B:, :]
        dsk = -jnp.exp(sink - lse) * delta
        lane = lax.broadcasted_iota(jnp.int32, (1, LANES), 1)
        row = jnp.zeros((1, LANES), F32)
        for g in range(group):
            row = row + jnp.where(lane == g, jnp.sum(dsk[g * B:(g + 1) * B, :]), 0.0)
        dsink_ref[0, 0:1, :] += row

    specs = _swa_specs(n_kv, group, 0, n_q, v_first)
    q_blk = pl.BlockSpec((B, group * HEAD_DIM), lambda kv, n: (n, kv))
    acc = pl.BlockSpec((1, S, HEAD_DIM), lambda kv, n: (kv, 0, 0))
    return pl.pallas_call(
        body, name="swa_bwd", grid=(n_kv, S // B),
        in_specs=specs + [q_blk, pl.BlockSpec((B, group * HEAD_DIM), lambda kv, n: (n, do_first + kv)),
                          pl.BlockSpec((1, B, group * LANES), lambda kv, n: (kv, n, 0)),
                          pl.BlockSpec(memory_space=pltpu.SMEM)],
        out_specs=[q_blk, acc, acc, pl.BlockSpec((1, 8, LANES), lambda kv, n: (kv, 0, 0))],
        out_shape=[jax.ShapeDtypeStruct((S, n_q * HEAD_DIM), BF16), jax.ShapeDtypeStruct((n_kv, S, HEAD_DIM), F32),
                   jax.ShapeDtypeStruct((n_kv, S, HEAD_DIM), F32), jax.ShapeDtypeStruct((n_kv, 8, LANES), F32)],
        compiler_params=pltpu.CompilerParams(dimension_semantics=("parallel", "arbitrary")),
    )(rq, rq, rq, proj, proj, o, do, lse_b, sinks)


def _adamw(w, g, m, v):
    m = ADAM_B1 * m + (1.0 - ADAM_B1) * g
    v = ADAM_B2 * v + (1.0 - ADAM_B2) * (g * g)
    m_hat = m / (1.0 - ADAM_B1 ** ADAM_STEP)
    v_hat = v / (1.0 - ADAM_B2 ** ADAM_STEP)
    delta = -ADAM_LR * (m_hat / (jnp.sqrt(v_hat) + ADAM_EPS) + ADAM_WD * w)
    return delta, m, v


def _mod_fwd(cond_in, w_mod, b_shard):
    R, D = cond_in.shape
    cols = w_mod.shape[1]
    tn = _fit(512, cols)

    def body(c_ref, w_ref, b_ref, o_ref):
        cv = c_ref[...]
        cond = (cv / (1.0 + jnp.exp(-cv))).astype(BF16)
        o_ref[...] = jnp.dot(cond, w_ref[...].astype(BF16), preferred_element_type=F32) + b_ref[...]

    return pl.pallas_call(
        body, name="mod_fwd", grid=(cols // tn,),
        in_specs=[pl.BlockSpec((R, D), lambda j: (0, 0)), pl.BlockSpec((D, tn), lambda j: (0, j)),
                  pl.BlockSpec((1, tn), lambda j: (0, j))],
        out_specs=pl.BlockSpec((R, tn), lambda j: (0, j)),
        out_shape=jax.ShapeDtypeStruct((R, cols), F32),
        compiler_params=pltpu.CompilerParams(dimension_semantics=("parallel",), vmem_limit_bytes=_vmem(3 * D * tn * 4)),
    )(cond_in, w_mod, b_shard)


def _mod_update(c_t, dmod, w, m, v):
    D, nb = c_t.shape
    cols = w.shape[1]
    tr = _fit(128, D)

    def body(c_ref, d_ref, w_ref, m_ref, v_ref, g_ref, dl_ref, nm_ref, nv_ref):
        cv = c_ref[...]
        cond = cv / (1.0 + jnp.exp(-cv))
        g = jnp.zeros((tr, cols), F32)
        for b in range(nb):
            g = g + cond[:, b:b + 1] * d_ref[b:b + 1, :]
        g_ref[...] = g
        dl_ref[...], nm_ref[...], nv_ref[...] = _adamw(w_ref[...], g, m_ref[...], v_ref[...])

    blk = pl.BlockSpec((tr, cols), lambda r: (r, 0))
    out = jax.ShapeDtypeStruct((D, cols), F32)
    return pl.pallas_call(
        body, name="mod_update", grid=(D // tr,),
        in_specs=[pl.BlockSpec((tr, nb), lambda r: (r, 0)), pl.BlockSpec((nb, cols), lambda r: (0, 0)), blk, blk, blk],
        out_specs=[blk] * 4, out_shape=[out] * 4,
        compiler_params=pltpu.CompilerParams(dimension_semantics=("parallel",), vmem_limit_bytes=_vmem(18 * tr * cols * 4)),
    )(c_t, dmod, w, m, v)


def _small_update(stacked, w, m, v):
    R, C = w.shape

    def body(s_ref, w_ref, m_ref, v_ref, g_ref, dl_ref, nm_ref, nv_ref):
        g = s_ref[0:R, :]
        for d in range(1, N_DEV):
            g = g + s_ref[d * R:(d + 1) * R, :]
        g_ref[...] = g
        dl_ref[...], nm_ref[...], nv_ref[...] = _adamw(w_ref[...], g, m_ref[...], v_ref[...])

    return pl.pallas_call(body, name="small_update", out_shape=[jax.ShapeDtypeStruct((R, C), F32)] * 4)(stacked, w, m, v)


def _place():
    return lax.axis_index("x"), lax.axis_index("y"), lax.axis_index("c")


def _allgather8(name, block):
    m_per, n = block.shape

    def body(x_ref, out_ref, token_ref, send_sems, recv_sems, local_sem):
        token_ref[...] = jnp.zeros_like(token_ref)
        x, y, c = _place()
        me, sibling = (x, y, c), (x, y, 1 - c)
        chips = [(1 - x, y), (x, 1 - y), (1 - x, 1 - y)]

        def rows(px, py, pc):
            return out_ref.at[pl.ds((4 * px + 2 * py + pc) * m_per, m_per), :]

        def copy(k, blk, to, src=None):
            return pltpu.make_async_remote_copy(
                src_ref=rows(*blk) if src is None else src, dst_ref=rows(*blk),
                send_sem=send_sems.at[k], recv_sem=recv_sems.at[k], device_id=to, device_id_type=MESH)

        mine = pltpu.make_async_copy(x_ref, rows(*me), local_sem)
        mine.start()
        first = [copy(0, me, sibling, src=x_ref)]
        first += [copy(1 + j, me, (*chip, c), src=x_ref) for j, chip in enumerate(chips)]
        for cp in first:
            cp.start()
        passed = [copy(4 + j, (*chip, c), sibling) for j, chip in enumerate(chips)]
        for j, chip in enumerate(chips):
            copy(1 + j, (*chip, c), me).wait_recv()
            passed[j].start()
        copy(0, sibling, me).wait_recv()
        for j, chip in enumerate(chips):
            copy(4 + j, (*chip, 1 - c), me).wait_recv()
        for cp in first + passed:
            cp.wait_send()
        mine.wait()

    vmem = pl.BlockSpec(memory_space=pltpu.VMEM)
    return pl.pallas_call(
        body, name=name,
        out_shape=[jax.ShapeDtypeStruct((N_DEV * m_per, n), block.dtype), jax.ShapeDtypeStruct((8, LANES), F32)],
        in_specs=[vmem], out_specs=[vmem, vmem],
        scratch_shapes=[pltpu.SemaphoreType.DMA((7,)), pltpu.SemaphoreType.DMA((7,)), pltpu.SemaphoreType.DMA],
    )(block)


_ANY = pl.BlockSpec(memory_space=pl.ANY)


def _half(ref, c, rows):
    return ref.at[pl.ds(c * (rows // 2), rows // 2), :]


_HBM = pl.BlockSpec(memory_space=pltpu.HBM)
_SEM = pl.BlockSpec(memory_space=pltpu.SEMAPHORE)
_EFFECT = pltpu.SideEffectType.DATAFLOW_SIDE_EFFECTING


def _ici_start(name, srcs, land_shapes, plan, per_source=3, after=None):
    ns, nl = len(srcs), len(land_shapes)
    n_copies = per_source * ns
    n_in = ns + nl + (after is not None)

    def body(*refs):
        src_refs, land_refs = refs[:ns], refs[ns:ns + nl]
        send_sems, recv_sems = refs[n_in], refs[n_in + 1]
        token = refs[-1]
        for n, (src, dst, peer, _) in enumerate(plan(src_refs, land_refs)):
            pltpu.make_async_remote_copy(src_ref=src, dst_ref=dst, send_sem=send_sems.at[n], recv_sem=recv_sems.at[n],
                                         device_id=peer, device_id_type=MESH).start()
        token[...] = jnp.zeros_like(token)

    lands = [lax.empty(s.shape, s.dtype) for s in land_shapes]
    out = pl.pallas_call(
        body, name=name,
        out_shape=(pltpu.SemaphoreType.DMA((n_copies,)), pltpu.SemaphoreType.DMA((n_copies,)),
                   *[pltpu.HBM(a.shape, a.dtype) for a in list(srcs) + lands], jax.ShapeDtypeStruct((8, LANES), F32)),
        in_specs=[_HBM] * (ns + nl) + [_ANY] * (after is not None),
        out_specs=(_SEM, _SEM, *[_HBM] * (ns + nl), pl.BlockSpec(memory_space=pltpu.VMEM)),
        input_output_aliases={n: 2 + n for n in range(ns + nl)},
        compiler_params=pltpu.CompilerParams(has_side_effects=_EFFECT),
    )(*[pltpu.with_memory_space_constraint(a, pltpu.HBM) for a in list(srcs) + lands],
      *([] if after is None else [after]))
    return out[0], out[1], list(out[2:2 + ns]), list(out[2 + ns:2 + ns + nl]), out[-1]


def _ici_wait(name, send_sems, recv_sems, srcs, lands, plan, after):
    ns, nl = len(srcs), len(lands)
    after = list(after) if isinstance(after, (list, tuple)) else [after]

    def body(*refs):
        src_refs, land_refs = refs[:ns], refs[ns:ns + nl]
        send_sems, recv_sems = refs[ns + nl], refs[ns + nl + 1]
        for n, (src, _, peer, mine) in enumerate(plan(src_refs, land_refs)):
            cp = pltpu.make_async_remote_copy(src_ref=src, dst_ref=mine, send_sem=send_sems.at[n],
                                              recv_sem=recv_sems.at[n], device_id=peer, device_id_type=MESH)
            cp.wait_send()
            cp.wait_recv()

    out = pl.pallas_call(
        body, name=name, out_shape=[pltpu.HBM(a.shape, a.dtype) for a in list(srcs) + list(lands)],
        in_specs=[_HBM] * (ns + nl) + [_SEM, _SEM] + [_ANY] * len(after), out_specs=[_HBM] * (ns + nl),
        input_output_aliases={n: n for n in range(ns + nl)},
        compiler_params=pltpu.CompilerParams(has_side_effects=_EFFECT),
    )(*srcs, *lands, send_sems, recv_sems, *after)
    return list(out[:ns]), list(out[ns:])


def _own_slab(name, chip, w, after):
    R, C = w.shape
    tr, tc = _tiles(R, C)
    tied = [] if after is None else [after]

    def body(chip_ref, w_ref, *rest):
        stack_ref, token_ref = rest[-2:]
        stack_ref[0] = w_ref[...].astype(BF16)
        token_ref[...] = jnp.zeros_like(token_ref)

    small = pl.BlockSpec((8, LANES), lambda r, q, chip_ref: (0, 0))
    grid_spec = pltpu.PrefetchScalarGridSpec(
        num_scalar_prefetch=1, grid=(R // tr, C // tc),
        in_specs=[pl.BlockSpec((tr, tc), lambda r, q, chip_ref: (r, q))] + [small] * len(tied),
        out_specs=[pl.BlockSpec((1, tr, tc), lambda r, q, chip_ref: (chip_ref[0], r, q)), small])
    return pl.pallas_call(
        body, name=name, grid_spec=grid_spec,
        out_shape=[jax.ShapeDtypeStruct((N_CHIPS, R, C), BF16), jax.ShapeDtypeStruct((8, LANES), F32)],
        compiler_params=pltpu.CompilerParams(dimension_semantics=("arbitrary", "arbitrary")),
    )(chip, w, *tied)


def _gather_plan(src_refs, land_refs):
    x, y, c = _place()
    copies = []
    for stack in src_refs:
        R = stack.shape[1]
        own = _half(stack.at[2 * x + y], c, R)
        for cx, cy in [(1 - x, y), (x, 1 - y), (1 - x, 1 - y)]:
            copies.append((own, own, (cx, cy, c), _half(stack.at[2 * cx + cy], c, R)))
    return copies


def _pass_plan(src_refs, land_refs):
    x, y, c = _place()
    copies = []
    for land in src_refs:
        R = land.shape[1]
        for cx, cy in [(1 - x, y), (x, 1 - y), (1 - x, 1 - y)]:
            slot = land.at[2 * cx + cy]
            copies.append((_half(slot, c, R), _half(slot, c, R), (x, y, 1 - c), _half(slot, 1 - c, R)))
    return copies


def _share_plan(src_refs, land_refs):
    x, y, c = _place()
    return [(h, land, (x, y, 1 - c), land) for h, land in zip(src_refs, land_refs)]


def _pass_to_sibling(name, lands):
    nw = len(lands)

    def body(*refs):
        ins, outs = refs[:nw], refs[nw:2 * nw]
        send_sems, recv_sems = refs[2 * nw:]
        x, y, c = _place()
        chips = [(1 - x, y), (x, 1 - y), (1 - x, 1 - y)]
        copies = []
        for k in range(nw):
            R = ins[k].shape[1]
            for j, (cx, cy) in enumerate(chips):
                cp = pltpu.make_async_remote_copy(
                    src_ref=_half(ins[k].at[2 * cx + cy], c, R), dst_ref=_half(outs[k].at[2 * cx + cy], c, R),
                    send_sem=send_sems.at[3 * k + j], recv_sem=recv_sems.at[3 * k + j],
                    device_id=(x, y, 1 - c), device_id_type=MESH)
                cp.start()
                copies.append(cp)
        for k in range(nw):
            R = ins[k].shape[1]
            for j, (cx, cy) in enumerate(chips):
                pltpu.make_async_remote_copy(
                    src_ref=_half(ins[k].at[2 * cx + cy], c, R), dst_ref=_half(outs[k].at[2 * cx + cy], 1 - c, R),
                    send_sem=send_sems.at[3 * k + j], recv_sem=recv_sems.at[3 * k + j],
                    device_id=(x, y, 1 - c), device_id_type=MESH).wait_recv()
        for cp in copies:
            cp.wait_send()

    return pl.pallas_call(
        body, name=name, out_shape=[jax.ShapeDtypeStruct(a.shape, a.dtype) for a in lands],
        in_specs=[_ANY] * nw, out_specs=[_ANY] * nw, input_output_aliases={k: k for k in range(nw)},
        scratch_shapes=[pltpu.SemaphoreType.DMA((3 * nw,)), pltpu.SemaphoreType.DMA((3 * nw,))],
    )(*lands)


def _tie(vec, token):
    return vec + token[0:1, 0:1]


def _lay_columns(name, arrays, pieces):
    S, dtype = arrays[0].shape[0], arrays[0].dtype
    out_w = sum(hi - lo for _, lo, hi in pieces)
    tr = _fit(ROW_TILE, S)

    def body(*refs):
        o_ref = refs[-1]
        rows = [r[...] for r in refs[:-1]]
        o_ref[...] = jnp.concatenate(
            [jnp.zeros((tr, hi - lo), dtype) if k is None else rows[k][:, lo:hi] for k, lo, hi in pieces], axis=1)

    need = 2 * tr * (out_w + sum(a.shape[1] for a in arrays)) * dtype.itemsize
    return pl.pallas_call(
        body, name=name, grid=(S // tr,), in_specs=[pl.BlockSpec((tr, a.shape[1]), lambda i: (i, 0)) for a in arrays],
        out_specs=pl.BlockSpec((tr, out_w), lambda i: (i, 0)), out_shape=jax.ShapeDtypeStruct((S, out_w), dtype),
        compiler_params=pltpu.CompilerParams(dimension_semantics=("parallel",), vmem_limit_bytes=_vmem(2 * need)),
    )(*arrays)


ROW_ALIGN = 16
TILE_ELEMS = 512 * 1024


def _tiles(rows, cols):
    fits = [t for t in range(ROW_ALIGN, min(rows, 256) + 1, ROW_ALIGN) if rows % t == 0]
    tr = fits[-1] if fits and fits[-1] >= 64 else rows
    tc = cols
    while tr * tc > TILE_ELEMS and tc % (2 * LANES) == 0:
        tc //= 2
    return tr, tc


def _scatter_plan(src_refs, land_refs):
    x, y, c = _place()
    copies = []
    for p, land in zip(src_refs, land_refs):
        for j, (cx, cy) in enumerate([(1 - x, y), (x, 1 - y), (1 - x, 1 - y)]):
            copies.append((p.at[2 * cx + cy], land.at[j], (cx, cy, c), land.at[j]))
    return copies


def _chip_add(name, chip, sums, recv):
    _, H, C = sums.shape
    tr, tc = _tiles(H, C)

    def body(chip_ref, p_ref, r_ref, o_ref):
        total = p_ref[0].astype(F32)
        for j in range(3):
            total = total + r_ref[j].astype(F32)
        o_ref[...] = total

    grid_spec = pltpu.PrefetchScalarGridSpec(
        num_scalar_prefetch=1, grid=(H // tr, C // tc),
        in_specs=[pl.BlockSpec((1, tr, tc), lambda r, q, chip_ref: (chip_ref[0], r, q)),
                  pl.BlockSpec((3, tr, tc), lambda r, q, chip_ref: (0, r, q))],
        out_specs=pl.BlockSpec((tr, tc), lambda r, q, chip_ref: (r, q)))
    return pl.pallas_call(
        body, name=name, grid_spec=grid_spec, out_shape=jax.ShapeDtypeStruct((H, C), F32),
        compiler_params=pltpu.CompilerParams(dimension_semantics=("parallel", "parallel")),
    )(chip, sums, recv)


def _pair_share(name, halves):
    nw = len(halves)

    def body(*refs):
        hs, outs = refs[:nw], refs[nw:2 * nw]
        send_sems, recv_sems = refs[2 * nw:]
        x, y, c = _place()
        copies = []
        for k in range(nw):
            cp = pltpu.make_async_remote_copy(
                src_ref=hs[k], dst_ref=outs[k], send_sem=send_sems.at[k], recv_sem=recv_sems.at[k],
                device_id=(x, y, 1 - c), device_id_type=MESH)
            cp.start()
            copies.append(cp)
        for cp in copies:
            cp.wait()

    return pl.pallas_call(
        body, name=name,
        out_shape=[jax.ShapeDtypeStruct(h.shape, h.dtype) for h in halves],
        in_specs=[_ANY] * nw, out_specs=[_ANY] * nw,
        scratch_shapes=[pltpu.SemaphoreType.DMA((nw,)), pltpu.SemaphoreType.DMA((nw,))],
    )(*halves)


def _adam_halves(name, core, w, g_own, g_other, m, v, out_rows=None):
    R, C = w.shape
    H = R // 2
    tr, tc = _tiles(H, C)
    nr, nc = H // tr, C // tc

    def body(core_ref, w_ref, go_ref, gr_ref, m_ref, v_ref, g_ref, dl_ref, nm_ref, nv_ref):
        own = (pl.program_id(0) // nr) == core_ref[0]
        g = jnp.where(own, go_ref[...], gr_ref[...])
        g_ref[...] = g
        dl_ref[...], nm_ref[...], nv_ref[...] = _adamw(w_ref[...], g, m_ref[...], v_ref[...])

    blk = pl.BlockSpec((tr, tc), lambda r, q, core_ref: (r, q))

    def half_spec(is_own):
        def index(r, q, core_ref):
            mine = ((r // nr) == core_ref[0]) == is_own
            done = is_own == (core_ref[0] == 0)
            return (jnp.where(mine, r % nr, jnp.where(done, nr - 1, 0)), jnp.where(mine, q, jnp.where(done, nc - 1, 0)))
        return pl.BlockSpec((tr, tc), index)
    out_rows = R if out_rows is None else out_rows
    assert R - tr < out_rows <= R, (R, tr, out_rows)
    out = jax.ShapeDtypeStruct((out_rows, C), F32)
    grid_spec = pltpu.PrefetchScalarGridSpec(
        num_scalar_prefetch=1, grid=(R // tr, nc), in_specs=[blk, half_spec(True), half_spec(False), blk, blk],
        out_specs=[blk] * 4)
    return pl.pallas_call(
        body, name=name, grid_spec=grid_spec, out_shape=[out] * 4,
        compiler_params=pltpu.CompilerParams(dimension_semantics=("parallel", "parallel"),
                                             vmem_limit_bytes=_vmem(20 * tr * tc * 4)),
    )(core, w, g_own, g_other, m, v)


def kernel(x, c, w_mod, b_mod, g_pre_mix, g_post_mix, w_in, b_forget, swa_sinks, w_out, g_pre_mlp, g_post_mlp, w_up, w_down, loss_target, m_w_mod, m_b_mod, m_g_pre_mix, m_g_post_mix, m_w_in, m_b_forget, m_swa_sinks, m_w_out, m_g_pre_mlp, m_g_post_mlp, m_w_up, m_w_down, v_w_mod, v_b_mod, v_g_pre_mix, v_g_post_mix, v_w_in, v_b_forget, v_swa_sinks, v_w_out, v_g_pre_mlp, v_g_post_mlp, v_w_up, v_w_down):
    S, D = x.shape[1], x.shape[2]
    n_heads = D // HEAD_DIM
    n_fox = n_heads // 2
    n_swa = n_heads - n_fox
    n_kv = max(1, n_swa // 4)
    fox_w, swa_w, kv_w = n_fox * HEAD_DIM, n_swa * HEAD_DIM, n_kv * HEAD_DIM
    main_w = 3 * fox_w + swa_w + 2 * kv_w
    in_w = main_w + n_fox
    mod_cols = w_mod.shape[2]

    ax, ay, ac = _place()
    chip = 2 * ax + ay
    dev = 2 * chip + ac
    chip_arr = jnp.reshape(chip, (1,)).astype(jnp.int32)
    core_arr = jnp.reshape(ac, (1,)).astype(jnp.int32)

    x2, tgt = x[0], loss_target[0]

    in_rows = in_w // N_CHIPS
    in_rows_pad = -(-in_rows // (2 * LANES)) * (2 * LANES)
    slab_w = N_CHIPS * in_rows_pad

    def rows_of(a):
        return jnp.pad(a[0].T, ((0, in_rows_pad - in_rows), (0, 0)))

    w_in_stack, token = _own_slab("own_slab_w_in", chip_arr, rows_of(w_in), None)

    c_all, _ = _allgather8("gather_c", _tie(c, token).reshape(8, D // 8))
    c_all = c_all.reshape(N_DEV, D)
    b_shard = lax.dynamic_slice_in_dim(b_mod, chip * mod_cols, mod_cols, axis=1)
    mod_shard = _mod_fwd(jnp.pad(c_all, ((0, 16 - N_DEV), (0, 0))), w_mod[0], b_shard)[:N_DEV]
    mod_all, token = _allgather8("gather_mod", mod_shard)
    mod_all = mod_all.reshape(N_CHIPS, 2, N_DEV, mod_cols)[:, 0]
    mod = lax.dynamic_index_in_dim(mod_all, dev, axis=1, keepdims=False).reshape(N_MOD, 1, D)
    sh_a, sc_a, gt_a, sh_m, sc_m, gt_m = [mod[n] for n in range(N_MOD)]

    def slab_cols(lo, hi):
        spans = []
        while lo < hi:
            s, r = divmod(lo, in_rows)
            n = min(hi - lo, in_rows - r)
            spans.append((s * in_rows_pad + r, s * in_rows_pad + r + n))
            lo += n
        return spans

    gate_lo = 3 * fox_w
    main_spans = slab_cols(0, gate_lo) + slab_cols(gate_lo + n_fox, in_w)
    (gate_first, gate_last), = slab_cols(gate_lo, gate_lo + n_fox)

    names = ["w_in", "w_out", "w_up", "w_down"]
    flights = {}
    for n, w in zip(names, [None, w_out[0], w_up[0], w_down[0]]):
        stack = w_in_stack if n == "w_in" else _own_slab("own_slab_" + n, chip_arr, w, token)[0]
        flights[n] = _ici_start("gather_start_" + n, [stack], [], _gather_plan, after=token)
        token = flights[n][4]
    sc_a = _tie(sc_a, token)

    def arrived(n, after):
        send, recv, stacks, _, _ = flights[n]
        stacks, _ = _ici_wait("gather_wait_" + n, send, recv, stacks, [], _gather_plan, after)
        return _ici_start("gather_pass_start_" + n, stacks, [], _pass_plan)

    def gathered(n, after, in_flight=None):
        if in_flight is None:
            send, recv, stacks, _, _ = flights[n]
            stacks, _ = _ici_wait("gather_wait_" + n, send, recv, stacks, [], _gather_plan, after)
            return _pass_to_sibling("gather_pass_" + n, stacks)[0]
        send, recv, stacks, _, _ = in_flight
        return _ici_wait("gather_pass_wait_" + n, send, recv, stacks, [], _pass_plan, after)[0][0]

    d_ff = N_CHIPS * w_up.shape[2]

    h = _pre_norm(x2, g_pre_mix, sc_a, sh_a)
    in_state = [rows_of(w_in)] + [rows_of(_tie(a, token)) for a in (m_w_in, v_w_in)]
    cos, sin_signed = _rope_tables(S)

    def pack(bm, gpm, gqm, gpl, gql, bf, sk):
        last = jnp.concatenate([bf, sk, jnp.zeros((1, D - n_fox - n_swa), F32)], axis=1)
        return jnp.concatenate([bm.reshape(N_MOD, D), gpm, gqm, gpl, gql, last, jnp.zeros((5, D), F32)], axis=0)

    small_state = [pack(b_mod, g_pre_mix, g_post_mix, g_pre_mlp, g_post_mlp, b_forget, swa_sinks),
                   pack(m_b_mod, m_g_pre_mix, m_g_post_mix, m_g_pre_mlp, m_g_post_mlp, m_b_forget, m_swa_sinks),
                   pack(v_b_mod, v_g_pre_mix, v_g_post_mix, v_g_pre_mlp, v_g_post_mlp, v_b_forget, v_swa_sinks)]
    ready = h[:8, :LANES].astype(F32) + cos[:8]
    w_slab_t = gathered("w_in", [ready] + in_state[1:] + small_state).reshape(slab_w, D)
    tm_p, tn_p = _fit(MM_TM, S), _fit(MM_TN if slab_w % MM_TN == 0 else MM_TN // 2, slab_w)
    win0 = gate_first // LANES * LANES
    win_j, win_off = divmod(win0, tn_p)
    assert win_off + 2 * LANES <= tn_p and gate_last - win0 <= 2 * LANES

    def proj_epilogue(acc, ex, outs):
        outs[0][...] = acc.astype(BF16)

        @pl.when(pl.program_id(1) == win_j)
        def _():
            outs[1][...] = acc[:, win_off:win_off + 2 * LANES]

    proj_slab, gate_win = _matmul(
        "in_proj", h, w_slab_t, "nt",
        [((S, slab_w), BF16, (tm_p, tn_p), lambda i, j: (i, j)), ((S, 2 * LANES), F32, (tm_p, 2 * LANES), lambda i, j: (i, 0))],
        proj_epilogue, tn=tn_p, revisits=True)
    proj = jnp.concatenate([proj_slab[:, lo:hi] for lo, hi in main_spans], axis=1)
    out_flight = arrived("w_out", proj_slab)
    fg = _tie(jnp.pad(gate_win[:, gate_first - win0:gate_last - win0], ((0, 0), (0, LANES - n_fox))), out_flight[4])
    b_pad = jnp.pad(b_forget, ((0, 0), (0, LANES - n_fox)))
    cum_row = _fox_gate_fwd(fg, b_pad)[:n_fox].reshape(n_fox, 1, S)
    fox_o, fox_lse = _fox_fwd(proj, cum_row, n_fox)

    rq = _rope("rope_fwd", proj, 3 * n_fox, n_swa + n_kv, cos, sin_signed)
    v_first = 3 * n_fox + n_swa + n_kv
    sinks = swa_sinks[0]
    swa_o, swa_lse = _swa_fwd(rq, proj, v_first, sinks, n_swa, n_kv)

    mixcat = jnp.concatenate([fox_o, swa_o], axis=1).astype(BF16)
    up_flight = arrived("w_up", mixcat)
    w_out_f = gathered("w_out", mixcat, out_flight).reshape(D, D)
    mix = _mm_plain("out_proj", mixcat, w_out_f, "nn", BF16, after=up_flight[4])
    x1, h2 = _post_mix(x2, mix, g_post_mix, gt_a, g_pre_mlp, sc_m, sh_m)
    w_up_f = gathered("w_up", h2, up_flight)

    tm_u, tn_u = _fit(MM_TM, S), _fit(MM_TN, d_ff)

    def up_epilogue(acc, ex, outs):
        outs[0][...] = acc.astype(BF16)
        r = jnp.maximum(acc, 0.0)
        outs[1][...] = (r * r).astype(BF16)

    ublk = ((S, d_ff), BF16, (tm_u, tn_u), lambda i, j: (i, j))
    u, a = _matmul("mlp_up", h2, w_up_f, "nn", [ublk, ublk], up_epilogue)
    w_down_f = gathered("w_down", a).reshape(d_ff, D)
    y = _mm_plain("mlp_down", a, w_down_f, "nn", BF16)

    dy, dout, loss_part, acc_mlp_post = _loss_and_post_mlp_bwd(x1, y, tgt, g_post_mlp, gt_m)

    def du_epilogue(acc, ex, outs):
        outs[0][...] = (acc * (2.0 * jnp.maximum(ex[0][...].astype(F32), 0.0))).astype(BF16)

    du = _matmul("mlp_down_bwd", dy, w_down_f, "nt", [ublk], du_epilogue,
                 extras=[(u, (tm_u, tn_u), lambda i, j: (i, j))])[0]
    def pair_send(tag, part):
        return _ici_start("grad_pair_start_" + tag, [part], [jax.ShapeDtypeStruct(part.shape, BF16)], _share_plan,
                          per_source=1)

    def pair_recv(tag, flight, after):
        send, recv, srcs, lands, _ = flight
        return _ici_wait("grad_pair_wait_" + tag, send, recv, srcs, lands, _share_plan, after)[1][0]

    def scatter_start(tag, sums, after=None):
        return _ici_start("grad_scatter_start_" + tag, sums,
                          [jax.ShapeDtypeStruct((3,) + p.shape[1:], BF16) for p in sums], _scatter_plan, after=after)

    def scatter_finish(tag, flight, after):
        send, recv, srcs, lands, _ = flight
        sums, received = _ici_wait("grad_scatter_wait_" + tag, send, recv, srcs, lands, _scatter_plan, after)
        return [_chip_add("chip_add_%s_%d" % (tag, k), chip_arr, p, r) for k, (p, r) in enumerate(zip(sums, received))]

    tm_g = _fit(MM_TM, D // 2)
    pair_down = pair_send("down", _grad_half("grad_w_down_a", core_arr, a, dy, N_CHIPS, 1, tm_g, True))
    pair_up = pair_send("up", _grad_half("grad_w_up_a", core_arr, h2, du, 1, N_CHIPS, tm_g, True, after=pair_down[4]))
    sum_down = _grad_half("grad_w_down_b", core_arr, a, dy, N_CHIPS, 1, tm_g, False,
                          recv=pair_recv("down", pair_down, pair_up[4]))
    sum_up = _grad_half("grad_w_up_b", core_arr, h2, du, 1, N_CHIPS, tm_g, False, recv=pair_recv("up", pair_up, sum_down))
    flight_mlp = scatter_start("mlp", [sum_up, sum_down])
    dh2 = _mm_plain("mlp_up_bwd", du, w_up_f, "nt", BF16, after=flight_mlp[4])
    dx1, dmix, acc_mid = _pre_mlp_and_post_mix_bwd(dh2, x1, dout, mix, _tie(g_pre_mlp, flight_mlp[4]), sc_m,
                                                   g_post_mix, gt_a)

    dmixcat = _mm_plain("out_proj_bwd", dmix, w_out_f, "nt", F32)

    fdq, fdk, fdv, dcum_row, dcum_q = _fox_bwd(proj, fox_o, dmixcat, fox_lse, cum_row, n_fox)
    dcum_k = jnp.pad(dcum_row.reshape(n_fox, S), ((0, LANES - n_fox), (0, 0)))
    dfg, db_forget = _fox_gate_bwd(dcum_k, dcum_q, fg, b_pad)

    group_w = (n_swa // n_kv) * HEAD_DIM
    sdq, sdk, sdv, dsink = _swa_bwd(rq, proj, v_first, sinks, swa_o, dmixcat, fox_w // group_w, swa_lse, n_swa, n_kv)
    drq = jnp.concatenate([sdq, jnp.transpose(sdk, (1, 0, 2)).reshape(S, kv_w).astype(BF16)], axis=1)
    d_sq_sk = _rope("rope_bwd", drq, 0, n_swa + n_kv, cos, -sin_signed)
    dsv = jnp.transpose(sdv, (1, 0, 2)).reshape(S, kv_w).astype(BF16)
    parts = [fdq, fdk, fdv, dfg, d_sq_sk, dsv]
    widths = [p.shape[1] for p in parts[:3]] + [n_fox] + [p.shape[1] for p in parts[4:]]
    starts = [sum(widths[:k]) for k in range(len(parts) + 1)]
    assert starts[3] == gate_lo and starts[-1] == in_w
    pieces = []
    for s in range(N_CHIPS):
        lo, hi = s * in_rows, (s + 1) * in_rows
        for k in range(len(parts)):
            first, last = max(lo, starts[k]), min(hi, starts[k + 1])
            if first < last:
                pieces.append((k, first - starts[k], last - starts[k]))
        pieces.append((None, 0, in_rows_pad - in_rows))
    dproj_slab = _lay_columns("dproj_slab_order", parts, pieces)

    tm_in, tm_out = in_rows_pad // 2, D // (2 * N_CHIPS)
    pair_in = pair_send("in", _grad_half("grad_w_in_a", core_arr, dproj_slab, h, N_CHIPS, 1, tm_in, True))
    pair_out = pair_send("out", _grad_half("grad_w_out_a", core_arr, mixcat, dmix, N_CHIPS, 1, tm_out, True,
                                           after=pair_in[4]))
    sum_in = _grad_half("grad_w_in_b", core_arr, dproj_slab, h, N_CHIPS, 1, tm_in, False,
                        recv=pair_recv("in", pair_in, pair_out[4]))
    sum_out = _grad_half("grad_w_out_b", core_arr, mixcat, dmix, N_CHIPS, 1, tm_out, False,
                         recv=pair_recv("out", pair_out, sum_in[0, :8, :LANES]))
    dh = _mm_plain("in_proj_bwd", dproj_slab, w_slab_t, "nn", BF16, tk=slab_w // 2,
                   after=sum_out[0, :8, :LANES].astype(F32))
    grad_x, acc_pre = _pre_mix_bwd(dh, x2, dx1, g_pre_mix, sc_a)

    zero_row = jnp.zeros((1, D), F32)
    tail = jnp.concatenate([db_forget[0:1, :n_fox], dsink[:, 0, :n_swa // n_kv].reshape(1, n_swa),
                            loss_part[0:1, 0:1], jnp.zeros((1, D - n_fox - n_swa - 1), F32)], axis=1)
    partial = jnp.concatenate([
        acc_pre[0:1], acc_pre[1:2], acc_mid[3:4], acc_mid[0:1], acc_mid[1:2], acc_mlp_post[0:1],
        acc_pre[2:3], acc_mid[4:5], acc_mid[2:3], acc_mlp_post[1:2], tail] + [zero_row] * 5, axis=0)
    gathered_small, token = _allgather8("gather_small_grads", partial)

    flight_mix = scatter_start("mix", [sum_in, sum_out], after=token)
    halves_mlp = scatter_finish("mlp", flight_mlp, flight_mix[4])
    share_up, share_down = [
        _ici_start("grad_share_start_" + n, [hv], [jax.ShapeDtypeStruct(hv.shape, F32)], _share_plan, per_source=1)
        for n, hv in zip(["up", "down"], halves_mlp)]

    def shared(tag, flight, after):
        send, recv, own, lands, _ = flight
        own, other = _ici_wait("grad_share_wait_" + tag, send, recv, own, lands, _share_plan, after)
        return own[0], other[0]

    def unpack(p):
        return {"b_mod": p[0:N_MOD].reshape(1, N_MOD * D), "g_pre_mix": p[6:7], "g_post_mix": p[7:8],
                "g_pre_mlp": p[8:9], "g_post_mlp": p[9:10], "b_forget": p[10:11, :n_fox],
                "swa_sinks": p[10:11, n_fox:n_fox + n_swa]}

    small_out = _small_update(gathered_small, _tie(small_state[0], share_down[4] + share_up[4]), small_state[1],
                              small_state[2])
    g_small, d_small, m_small, v_small = [unpack(p) for p in small_out]
    loss = small_out[0][N_MOD + 4, n_fox + n_swa]

    dmod_all = gathered_small.reshape(N_DEV, 16, D)[:, :N_MOD].reshape(N_DEV, N_MOD * D)
    dmod_shard = _tie(lax.dynamic_slice_in_dim(dmod_all, chip * mod_cols, mod_cols, axis=1), share_down[4])
    g_w_mod, d_w_mod, nm_w_mod, nv_w_mod = _mod_update(c_all.T, dmod_shard, w_mod[0], m_w_mod[0], v_w_mod[0])

    grads = dict(g_small, w_mod=g_w_mod[None])
    deltas = dict(d_small, w_mod=d_w_mod[None])
    new_m = dict(m_small, w_mod=nm_w_mod[None])
    new_v = dict(v_small, w_mod=nv_w_mod[None])
    weights = {"w_in": (w_in, m_w_in, v_w_in), "w_out": (w_out, m_w_out, v_w_out), "w_up": (w_up, m_w_up, v_w_up),
               "w_down": (w_down, m_w_down, v_w_down)}

    updated = {}

    def big_update(n, own, other):
        transposed = n == "w_in"
        w, m, v = in_state if transposed else [a[0] for a in weights[n]]
        outs = _adam_halves("adam_" + n, core_arr, w, own, other, m, v, out_rows=in_rows if transposed else None)
        updated[n] = outs[1][:8, :LANES]
        if transposed:
            outs = [o.T for o in outs]
        grads[n], deltas[n], new_m[n], new_v[n] = [o[None] for o in outs]

    big_update("w_down", *shared("down", share_down, d_w_mod[:8, :LANES] + small_out[1][:8, :LANES]))
    halves_mix = scatter_finish("mix", flight_mix, updated["w_down"] + d_w_mod[:8, :LANES])
    share_in, share_out = [
        _ici_start("grad_share_start_" + n, [hv], [jax.ShapeDtypeStruct(hv.shape, F32)], _share_plan, per_source=1)
        for n, hv in zip(["in", "out"], halves_mix)]
    big_update("w_up", *shared("up", share_up, share_in[4] + share_out[4]))
    big_update("w_in", *shared("in", share_in, updated["w_up"]))
    big_update("w_out", *shared("out", share_out, updated["w_in"]))

    order = ["w_mod", "b_mod", "g_pre_mix", "g_post_mix", "w_in", "b_forget", "swa_sinks", "w_out", "g_pre_mlp",
             "g_post_mlp", "w_up", "w_down"]
    return (loss, grad_x[None], *[grads[n] for n in order], *[deltas[n] for n in order],
            *[new_m[n] for n in order], *[new_v[n] for n in order])
```

```python
import jax
import jax.numpy as jnp
from jax import lax
from jax.experimental import pallas as pl
from jax.experimental.pallas import tpu as pltpu

F32 = jnp.float32
BF16 = jnp.bfloat16
MESH = pl.DeviceIdType.MESH

HEAD_DIM = 128
SWA_BLOCK = 128
ROPE_THETA = 10000.0
NORM_EPS = 1e-6
NEG = -1e30
N_MOD = 6
ADAM_LR = 0.001
ADAM_B1 = 0.9
ADAM_B2 = 0.999
ADAM_EPS = 1e-08
ADAM_WD = 0.01
ADAM_STEP = 10
N_CHIPS = 4
N_DEV = 8
LANES = 128
VMEM_CAP = 60 * 1024 * 1024

_NN = (((1,), (0,)), ((), ()))
_NT = (((1,), (1,)), ((), ()))
_TN = (((0,), (0,)), ((), ()))


def _vmem(nbytes):
    return int(min(VMEM_CAP, nbytes * 5 // 4 + (4 << 20)))


def _nbytes(shape, dtype):
    n = 1
    for s in shape:
        n *= s
    return n * jnp.dtype(dtype).itemsize


def _fit(t, n):
    t = min(t, n)
    assert n % t == 0, (t, n)
    return t


MM_TM, MM_TN, MM_TK = 1024, 1024, 2048


def _matmul(name, a, b, mode, out_defs, epilogue, extras=(), tm=MM_TM, tn=MM_TN, tk=MM_TK, revisits=False,
            row_sel=None):
    stacked = b.ndim == 3
    b_rows, b_cols = b.shape[-2], b.shape[-1] * (b.shape[0] if stacked else 1)
    if mode == "nn":
        (M, K), (K2, N) = a.shape, (b_rows, b_cols)
    elif mode == "nt":
        (M, K), (N, K2) = a.shape, (b_rows, b_cols)
    else:
        (K, M), (K2, N) = a.shape, (b_rows, b_cols)
    assert K == K2 and not (stacked and mode == "tn"), (a.shape, b.shape, mode)
    tm = _fit(tm, M)
    tn = _fit(tn, b.shape[-1] if stacked and mode == "nn" else N)
    tk = _fit(tk, b.shape[-1] if stacked and mode == "nt" else K)
    nk = K // tk
    dims = {"nn": _NN, "nt": _NT, "tn": _TN}[mode]
    if row_sel is None:
        grid_m, a_row = M // tm, lambda i, *sel: i
    else:
        grid_m, a_row = row_sel[2], lambda i, *sel: row_sel[1](i, sel[0])
    a_spec = (pl.BlockSpec((tk, tm), lambda i, j, k, *sel: (k, a_row(i, *sel))) if mode == "tn"
              else pl.BlockSpec((tm, tk), lambda i, j, k, *sel: (a_row(i, *sel), k)))
    if stacked:
        per = b.shape[-1] // (tk if mode == "nt" else tn)
        b_spec = (pl.BlockSpec((1, tn, tk), lambda i, j, k, *sel: (k // per, j, k % per)) if mode == "nt"
                  else pl.BlockSpec((1, tk, tn), lambda i, j, k, *sel: (j // per, k, j % per)))
    else:
        b_spec = (pl.BlockSpec((tn, tk), lambda i, j, k, *sel: (j, k)) if mode == "nt"
                  else pl.BlockSpec((tk, tn), lambda i, j, k, *sel: (k, j)))
    n_ex, n_out = len(extras), len(out_defs)

    def body(*refs):
        if row_sel is not None:
            refs = refs[1:]
        a_ref, b_ref = refs[0], refs[1]
        ex = refs[2:2 + n_ex]
        outs = refs[2 + n_ex:2 + n_ex + n_out]
        b_blk = b_ref[0] if stacked else b_ref[...]
        prod = lax.dot_general(a_ref[...], b_blk, dims, preferred_element_type=F32)
        if nk == 1:
            epilogue(prod, ex, outs)
        else:
            acc_ref = refs[-1]
            k = pl.program_id(2)

            @pl.when(k == 0)
            def _():
                acc_ref[...] = prod

            @pl.when(k > 0)
            def _():
                acc_ref[...] += prod

            @pl.when(k == nk - 1)
            def _():
                epilogue(acc_ref[...], ex, outs)

    def wrap(f):
        return lambda i, j, k, *sel: f(i, j)

    in_specs = [a_spec, b_spec] + [pl.BlockSpec(blk, wrap(f)) for _, blk, f in extras]
    out_specs = [pl.BlockSpec(blk, wrap(f)) for _, _, blk, f in out_defs]
    out_shape = [jax.ShapeDtypeStruct(s, d) for s, d, _, _ in out_defs]
    need = 2 * (tm * tk + tk * tn) * a.dtype.itemsize + 3 * tm * tn * 4
    need += sum(2 * _nbytes(blk, arr.dtype) for arr, blk, _ in extras)
    need += sum(2 * _nbytes(blk, d) for _, d, blk, _ in out_defs)
    grid = (grid_m, N // tn, nk)
    scratch = [pltpu.VMEM((tm, tn), F32)] if nk > 1 else []
    params = pltpu.CompilerParams(
        dimension_semantics=("parallel", "arbitrary" if revisits else "parallel", "arbitrary"),
        vmem_limit_bytes=_vmem(need))
    operands = (a, b, *[arr for arr, _, _ in extras])
    if row_sel is None:
        return pl.pallas_call(body, name=name, grid=grid, in_specs=in_specs, out_specs=out_specs, out_shape=out_shape,
                              scratch_shapes=scratch, compiler_params=params)(*operands)
    grid_spec = pltpu.PrefetchScalarGridSpec(num_scalar_prefetch=1, grid=grid, in_specs=in_specs, out_specs=out_specs,
                                             scratch_shapes=scratch)
    return pl.pallas_call(body, name=name, grid_spec=grid_spec, out_shape=out_shape,
                          compiler_params=params)(row_sel[0], *operands)


def _grad_half(name, core, a, b, row_slabs, col_slabs, tm, other, recv=None, after=None):
    (_, M), (_, N) = a.shape, b.shape
    H = M // (2 * row_slabs)
    nh = H // tm
    tn = _fit(MM_TN, N // col_slabs)
    per = N // col_slabs // tn

    def a_block(i, core_ref):
        half = (1 - core_ref[0]) if other else core_ref[0]
        return (i // nh) * (2 * nh) + half * nh + i % nh

    def out_index(i, j):
        return (j // per, i, j % per) if col_slabs > 1 else (i // nh, i % nh, j)

    slabs = max(row_slabs, col_slabs)
    out_def = ((slabs, H, N // col_slabs), BF16, (1, tm, tn), out_index)

    def epilogue(acc, ex, outs):
        outs[0][0] = (acc if recv is None else acc + ex[0][0].astype(F32)).astype(BF16)

    extras = ([] if recv is None else [(recv, (1, tm, tn), out_index)]) + ([] if after is None else [_behind(after)])
    return _matmul(name, a, b, "tn", [out_def], epilogue, extras=extras, tm=tm, tn=tn,
                   row_sel=(core, a_block, row_slabs * nh))[0]


def _behind(token):
    return (token, (8, LANES), lambda i, j: (0, 0))


def _mm_plain(name, a, b, mode, out_dtype, after=None, **tiles):
    if mode == "nn":
        M, N = a.shape[0], b.shape[-1] * (b.shape[0] if b.ndim == 3 else 1)
    elif mode == "nt":
        M, N = a.shape[0], b.shape[-2]
    else:
        M, N = a.shape[1], b.shape[1]
    tm, tn = _fit(tiles.get("tm", MM_TM), M), _fit(tiles.get("tn", MM_TN), N)

    def epi(acc, ex, outs):
        outs[0][...] = acc.astype(out_dtype)

    return _matmul(name, a, b, mode, [((M, N), out_dtype, (tm, tn), lambda i, j: (i, j))], epi,
                   extras=[] if after is None else [_behind(after)], **tiles)[0]


def _rstd(v):
    return lax.rsqrt(jnp.mean(v * v, axis=-1, keepdims=True) + NORM_EPS)


ROW_TILE = 256


def _row_call(name, body, row_ins, vec_ins, row_outs, acc_outs, S, D):
    tr = _fit(ROW_TILE, S)
    row_spec = pl.BlockSpec((tr, D), lambda r: (r, 0))
    vec_spec = pl.BlockSpec((1, D), lambda r: (0, 0))
    in_specs = [row_spec] * len(row_ins) + [vec_spec] * len(vec_ins)
    out_specs = [row_spec] * len(row_outs) + [pl.BlockSpec(shp, lambda r: (0, 0)) for shp in acc_outs]
    out_shape = [jax.ShapeDtypeStruct((S, D), d) for d in row_outs] + [jax.ShapeDtypeStruct(shp, F32) for shp in acc_outs]
    need = sum(2 * tr * D * a.dtype.itemsize for a in row_ins) + sum(2 * tr * D * jnp.dtype(d).itemsize for d in row_outs)
    need += 8 * tr * D * 4
    return pl.pallas_call(
        body, name=name, grid=(S // tr,), in_specs=in_specs, out_specs=out_specs, out_shape=out_shape,
        compiler_params=pltpu.CompilerParams(dimension_semantics=("arbitrary",), vmem_limit_bytes=_vmem(need)),
    )(*row_ins, *vec_ins)


def _acc_rows(ref, rows):
    @pl.when(pl.program_id(0) == 0)
    def _():
        ref[...] = jnp.zeros_like(ref)
    for n, r in enumerate(rows):
        ref[n:n + 1, :] += r


def _pre_norm(x, g, sc, sh):
    S, D = x.shape

    def body(x_ref, g_ref, sc_ref, sh_ref, h_ref):
        xv = x_ref[...]
        xn = xv * _rstd(xv)
        h_ref[...] = (xn * g_ref[...] * (1.0 + sc_ref[...]) + sh_ref[...]).astype(BF16)

    return _row_call("pre_norm_mix", body, [x], [g, sc, sh], [BF16], [], S, D)[0]


def _post_mix(x, mix, g_post, gt, g_pre, sc, sh):
    S, D = x.shape

    def body(x_ref, mix_ref, gp_ref, gt_ref, g2_ref, sc_ref, sh_ref, x1_ref, h2_ref):
        mv = mix_ref[...].astype(F32)
        x1 = x_ref[...] + gt_ref[...] * (mv * _rstd(mv) * gp_ref[...])
        x1_ref[...] = x1
        h2_ref[...] = (x1 * _rstd(x1) * g2_ref[...] * (1.0 + sc_ref[...]) + sh_ref[...]).astype(BF16)

    return _row_call("post_mix_pre_mlp", body, [x, mix], [g_post, gt, g_pre, sc, sh], [F32, BF16], [], S, D)


def _loss_and_post_mlp_bwd(x1, y, target, g_post, gt):
    S, D = x1.shape

    def body(x1_ref, y_ref, t_ref, g_ref, gt_ref, dy_ref, dout_ref, loss_ref, acc_ref):
        yv = y_ref[...].astype(F32)
        r = _rstd(yv)
        yh = yv * r
        n = yh * g_ref[...]
        diff = x1_ref[...] + gt_ref[...] * n - t_ref[...]
        dout = diff * (1.0 / D)
        dout_ref[...] = dout
        dn = dout * gt_ref[...]
        dyh = dn * g_ref[...]
        dy_ref[...] = (r * (dyh - yh * jnp.mean(dyh * yh, axis=-1, keepdims=True))).astype(BF16)
        _acc_rows(acc_ref, [jnp.sum(dout * n, axis=0, keepdims=True), jnp.sum(dn * yh, axis=0, keepdims=True)])

        @pl.when(pl.program_id(0) == 0)
        def _():
            loss_ref[...] = jnp.zeros_like(loss_ref)
        loss_ref[...] += jnp.full(loss_ref.shape, (0.5 / D) * jnp.sum(diff * diff), F32)

    return _row_call("loss_post_mlp_bwd", body, [x1, y, target], [g_post, gt], [BF16, F32],
                     [(8, LANES), (8, D)], S, D)


def _pre_mlp_and_post_mix_bwd(dh2, x1, dout, mix, g_pre, sc, g_post, gt):
    S, D = x1.shape

    def body(dh_ref, x1_ref, dout_ref, mix_ref, g_ref, sc_ref, gp_ref, gt_ref, dx1_ref, dmix_ref, acc_ref):
        dh = dh_ref[...].astype(F32)
        x1v = x1_ref[...]
        r3 = _rstd(x1v)
        xn = x1v * r3
        dxn = dh * (1.0 + sc_ref[...]) * g_ref[...]
        dx1 = dout_ref[...] + r3 * (dxn - xn * jnp.mean(dxn * xn, axis=-1, keepdims=True))
        dx1_ref[...] = dx1
        mv = mix_ref[...].astype(F32)
        r2 = _rstd(mv)
        mh = mv * r2
        dn = dx1 * gt_ref[...]
        dmh = dn * gp_ref[...]
        dmix_ref[...] = (r2 * (dmh - mh * jnp.mean(dmh * mh, axis=-1, keepdims=True))).astype(BF16)
        _acc_rows(acc_ref, [
            jnp.sum(dh, axis=0, keepdims=True),
            jnp.sum(dh * xn * g_ref[...], axis=0, keepdims=True),
            jnp.sum(dh * (1.0 + sc_ref[...]) * xn, axis=0, keepdims=True),
            jnp.sum(dx1 * mh * gp_ref[...], axis=0, keepdims=True),
            jnp.sum(dn * mh, axis=0, keepdims=True)])

    return _row_call("pre_mlp_post_mix_bwd", body, [dh2, x1, dout, mix], [g_pre, sc, g_post, gt], [F32, BF16],
                     [(8, D)], S, D)


def _pre_mix_bwd(dh, x, dx1, g_pre, sc):
    S, D = x.shape

    def body(dh_ref, x_ref, dx1_ref, g_ref, sc_ref, gx_ref, acc_ref):
        dhv = dh_ref[...].astype(F32)
        xv = x_ref[...]
        r = _rstd(xv)
        xn = xv * r
        dxn = dhv * (1.0 + sc_ref[...]) * g_ref[...]
        gx_ref[...] = dx1_ref[...] + r * (dxn - xn * jnp.mean(dxn * xn, axis=-1, keepdims=True))
        _acc_rows(acc_ref, [
            jnp.sum(dhv, axis=0, keepdims=True),
            jnp.sum(dhv * xn * g_ref[...], axis=0, keepdims=True),
            jnp.sum(dhv * (1.0 + sc_ref[...]) * xn, axis=0, keepdims=True)])

    return _row_call("pre_mix_bwd", body, [dh, x, dx1], [g_pre, sc], [F32], [(8, D)], S, D)


CUM_BLOCK = 256


def _tri(n, upper):
    r = lax.broadcasted_iota(jnp.int32, (n, n), 0)
    c = lax.broadcasted_iota(jnp.int32, (n, n), 1)
    return ((c >= r) if upper else (c <= r)).astype(F32)


def _fox_gate_fwd(fg, b_pad):
    S = fg.shape[0]
    cb = _fit(CUM_BLOCK, S)

    def body(fg_ref, b_ref, cumt_ref, cum_ref):
        low = _tri(cb, False)
        carry = jnp.zeros((1, LANES), F32)
        for n in range(S // cb):
            z = fg_ref[n * cb:(n + 1) * cb, :] + b_ref[...]
            logf = jnp.minimum(z, 0.0) - jnp.log(1.0 + jnp.exp(-jnp.abs(z)))
            blk = jnp.dot(low, logf, precision=lax.Precision.HIGHEST, preferred_element_type=F32) + carry
            cum_ref[n * cb:(n + 1) * cb, :] = blk
            carry = blk[cb - 1:cb, :]
        cumt_ref[...] = cum_ref[...].T

    return pl.pallas_call(
        body, name="fox_gate_fwd", out_shape=jax.ShapeDtypeStruct((LANES, S), F32),
        scratch_shapes=[pltpu.VMEM((S, LANES), F32)],
        compiler_params=pltpu.CompilerParams(vmem_limit_bytes=_vmem(6 * S * LANES * 4)),
    )(fg, b_pad)


def _fox_gate_bwd(dcum_k, dcum_q, fg, b_pad):
    S = fg.shape[0]
    n_fox = dcum_q.shape[0]
    cb = _fit(CUM_BLOCK, S)

    def body(dk_ref, dq_ref, fg_ref, b_ref, dfg_ref, db_ref, dc_ref):
        lane = lax.broadcasted_iota(jnp.int32, (S, LANES), 1)
        dc = dk_ref[...].T
        for h in range(n_fox):
            dc = dc + jnp.where(lane == h, dq_ref[h], 0.0)
        dc_ref[...] = dc
        up = _tri(cb, True)
        carry = jnp.zeros((1, LANES), F32)
        db = jnp.zeros((1, LANES), F32)
        for n in reversed(range(S // cb)):
            blk = jnp.dot(up, dc_ref[n * cb:(n + 1) * cb, :], precision=lax.Precision.HIGHEST,
                          preferred_element_type=F32) + carry
            carry = blk[0:1, :]
            z = fg_ref[n * cb:(n + 1) * cb, :] + b_ref[...]
            dfg = blk * (1.0 / (1.0 + jnp.exp(z)))
            dfg_ref[n * cb:(n + 1) * cb, :] = dfg.astype(BF16)
            db = db + jnp.sum(dfg, axis=0, keepdims=True)
        db_ref[...] = jnp.broadcast_to(db, db_ref.shape)

    return pl.pallas_call(
        body, name="fox_gate_bwd",
        out_shape=[jax.ShapeDtypeStruct((S, LANES), BF16), jax.ShapeDtypeStruct((8, LANES), F32)],
        scratch_shapes=[pltpu.VMEM((S, LANES), F32)],
        compiler_params=pltpu.CompilerParams(vmem_limit_bytes=_vmem((8 + 2 * n_fox) * S * LANES * 4)),
    )(dcum_k, dcum_q, fg, b_pad)


FOX_TILE = 512


LOG2E = 1.4426950408889634


def _fox_scores(q, k, ck2, masked, t):
    s = lax.dot_general(q, k, _NT, preferred_element_type=F32) * (HEAD_DIM ** -0.5 * LOG2E) - ck2
    if masked:
        row = lax.broadcasted_iota(jnp.int32, (t, t), 0)
        col = lax.broadcasted_iota(jnp.int32, (t, t), 1)
        s = jnp.where(col <= row, s, NEG)
    return s


def _fox_fwd(proj, cum_row, n_fox):
    S = proj.shape[0]
    t = _fit(FOX_TILE, S)
    nq = S // t

    def body(q_ref, k_ref, v_ref, ck_ref, o_ref, lse_ref):
        def q_block(qi, _):
            q0 = pl.multiple_of(qi * t, t)
            q = q_ref[pl.ds(q0, t), :]

            def kv_block(j, carry, masked):
                m, l, acc = carry
                k0 = pl.multiple_of(j * t, t)
                s = _fox_scores(q, k_ref[pl.ds(k0, t), :], ck_ref[0, :, pl.ds(k0, t)] * LOG2E, masked, t)
                m_new = jnp.maximum(m, jnp.max(s, axis=-1, keepdims=True))
                alpha = jnp.exp2(m - m_new)
                p = jnp.exp2(s - m_new)
                l = alpha * l + jnp.sum(p, axis=-1, keepdims=True)
                acc = alpha * acc + jnp.dot(p.astype(BF16), v_ref[pl.ds(k0, t), :], preferred_element_type=F32)
                return m_new, l, acc

            init = (jnp.full((t, 1), NEG, F32), jnp.zeros((t, 1), F32), jnp.zeros((t, HEAD_DIM), F32))
            carry = lax.fori_loop(0, qi, lambda j, cr: kv_block(j, cr, False), init)
            m, l, acc = kv_block(qi, carry, True)
            o_ref[pl.ds(q0, t), :] = acc / l
            lse_ref[0, pl.ds(q0, t), :] = jnp.broadcast_to(m + jnp.log(l) * LOG2E, (t, LANES))
            return 0

        lax.fori_loop(0, nq, q_block, 0)

    col = lambda off: pl.BlockSpec((S, HEAD_DIM), lambda h: (0, off + h))
    per_head = pl.BlockSpec((1, S, LANES), lambda h: (h, 0, 0))
    return pl.pallas_call(
        body, name="fox_fwd", grid=(n_fox,),
        in_specs=[col(0), col(n_fox), col(2 * n_fox), pl.BlockSpec((1, 1, S), lambda h: (h, 0, 0))],
        out_specs=[pl.BlockSpec((S, HEAD_DIM), lambda h: (0, h)), per_head],
        out_shape=[jax.ShapeDtypeStruct((S, n_fox * HEAD_DIM), F32), jax.ShapeDtypeStruct((n_fox, S, LANES), F32)],
        compiler_params=pltpu.CompilerParams(dimension_semantics=("parallel",),
                                             vmem_limit_bytes=_vmem(16 * S * HEAD_DIM * 4 + 12 * t * t * 4)),
    )(proj, proj, proj, cum_row)


def _fox_bwd(proj, o, do, lse_b, cum_row, n_fox):
    S = proj.shape[0]
    t = _fit(FOX_TILE, S)
    nq = S // t
    scale = HEAD_DIM ** -0.5

    def body(q_ref, k_ref, v_ref, o_ref, do_ref, lse_ref, ck_ref, dq_ref, dk_ref, dv_ref, dc_ref, dcq_ref,
             dq_acc, delta_ref):
        dq_acc[...] = jnp.zeros_like(dq_acc)
        dcq_ref[...] = jnp.zeros_like(dcq_ref)

        def delta_block(qi, _):
            q0 = pl.multiple_of(qi * t, t)
            d = jnp.sum(do_ref[pl.ds(q0, t), :] * o_ref[pl.ds(q0, t), :], axis=-1, keepdims=True)
            delta_ref[pl.ds(q0, t), :] = jnp.broadcast_to(d, (t, LANES))
            return 0

        lax.fori_loop(0, nq, delta_block, 0)

        def kv_block(j, _):
            k0 = pl.multiple_of(j * t, t)
            k = k_ref[pl.ds(k0, t), :]
            v = v_ref[pl.ds(k0, t), :]
            ck2 = ck_ref[0, :, pl.ds(k0, t)] * LOG2E

            def q_block(qi, carry, masked):
                dk, dv, dc = carry
                q0 = pl.multiple_of(qi * t, t)
                q = q_ref[pl.ds(q0, t), :]
                dov = do_ref[pl.ds(q0, t), :].astype(BF16)
                p = jnp.exp2(_fox_scores(q, k, ck2, masked, t) - lse_ref[0, pl.ds(q0, t), :][:, :1])
                dp = lax.dot_general(dov, v, _NT, preferred_element_type=F32)
                ds = p * (dp - delta_ref[pl.ds(q0, t), :][:, :1])
                dsb = ds.astype(BF16)
                dv = dv + lax.dot_general(p.astype(BF16), dov, _TN, preferred_element_type=F32)
                dk = dk + lax.dot_general(dsb, q, _TN, preferred_element_type=F32)
                dq_acc[pl.ds(q0, t), :] += jnp.dot(dsb, k, preferred_element_type=F32)
                dc = dc - jnp.sum(ds, axis=0, keepdims=True)
                dcq_ref[0, pl.ds(q0, t), :] += jnp.broadcast_to(jnp.sum(ds, axis=1, keepdims=True), (t, LANES))
                return dk, dv, dc

            init = (jnp.zeros((t, HEAD_DIM), F32), jnp.zeros((t, HEAD_DIM), F32), jnp.zeros((1, t), F32))
            carry = q_block(j, init, True)
            dk, dv, dc = lax.fori_loop(j + 1, nq, lambda qi, cr: q_block(qi, cr, False), carry)
            dk_ref[pl.ds(k0, t), :] = (dk * scale).astype(BF16)
            dv_ref[pl.ds(k0, t), :] = dv.astype(BF16)
            dc_ref[0, :, pl.ds(k0, t)] = dc
            return 0

        lax.fori_loop(0, nq, kv_block, 0)
        dq_ref[...] = (dq_acc[...] * scale).astype(BF16)

    col = lambda off: pl.BlockSpec((S, HEAD_DIM), lambda h: (0, off + h))
    per_head = pl.BlockSpec((1, S, LANES), lambda h: (h, 0, 0))
    row = pl.BlockSpec((1, 1, S), lambda h: (h, 0, 0))
    grad = jax.ShapeDtypeStruct((S, n_fox * HEAD_DIM), BF16)
    return pl.pallas_call(
        body, name="fox_bwd", grid=(n_fox,),
        in_specs=[col(0), col(n_fox), col(2 * n_fox), col(0), col(0), per_head, row],
        out_specs=[col(0), col(0), col(0), row, per_head],
        out_shape=[grad, grad, grad, jax.ShapeDtypeStruct((n_fox, 1, S), F32), jax.ShapeDtypeStruct((n_fox, S, LANES), F32)],
        scratch_shapes=[pltpu.VMEM((S, HEAD_DIM), F32), pltpu.VMEM((S, LANES), F32)],
        compiler_params=pltpu.CompilerParams(dimension_semantics=("parallel",),
                                             vmem_limit_bytes=_vmem(24 * S * HEAD_DIM * 4 + 16 * t * t * 4)),
    )(proj, proj, proj, o, do, lse_b, cum_row)


def _rope_tables(S):
    half = HEAD_DIM // 2
    inv_freq = 1.0 / (ROPE_THETA ** (jnp.arange(half, dtype=F32) * (2.0 / HEAD_DIM)))
    ang = jnp.arange(S).astype(F32)[:, None] * inv_freq[None, :]
    cos, sin = jnp.cos(ang), jnp.sin(ang)
    return jnp.concatenate([cos, cos], axis=-1), jnp.concatenate([-sin, sin], axis=-1)


def _rope(name, src, first_block, n_blocks, cos, sin_signed):
    S = src.shape[0]

    def body(x_ref, cos_ref, sin_ref, o_ref):
        xv = x_ref[...].astype(F32)
        o_ref[...] = (xv * cos_ref[...] + pltpu.roll(xv, HEAD_DIM // 2, 1) * sin_ref[...]).astype(BF16)

    table = pl.BlockSpec((S, HEAD_DIM), lambda n: (0, 0))
    return pl.pallas_call(
        body, name=name, grid=(n_blocks,),
        in_specs=[pl.BlockSpec((S, HEAD_DIM), lambda n: (0, first_block + n)), table, table],
        out_specs=pl.BlockSpec((S, HEAD_DIM), lambda n: (0, n)),
        out_shape=jax.ShapeDtypeStruct((S, n_blocks * HEAD_DIM), BF16),
        compiler_params=pltpu.CompilerParams(dimension_semantics=("parallel",),
                                             vmem_limit_bytes=_vmem(12 * S * HEAD_DIM * 4)),
    )(src, cos, sin_signed)


def _swa_tile(q_ref, kp_ref, kc_ref, n, group, scale):
    B = SWA_BLOCK
    qs = jnp.concatenate([q_ref[:, g * HEAD_DIM:(g + 1) * HEAD_DIM] for g in range(group)], axis=0)
    kcat = jnp.concatenate([kp_ref[...], kc_ref[...]], axis=0)
    s = lax.dot_general(qs, kcat, _NT, preferred_element_type=F32) * scale
    qi = lax.broadcasted_iota(jnp.int32, (group * B, 2 * B), 0) % B
    kj = lax.broadcasted_iota(jnp.int32, (group * B, 2 * B), 1)
    diff = qi + B - kj
    mask = (diff >= 0) & (diff < B) & ((n * B + kj - B) >= 0)
    return qs, kcat, jnp.where(mask, s, NEG)


def _swa_sink_col(sink_ref, kv, group):
    head = lax.broadcasted_iota(jnp.int32, (group * SWA_BLOCK, 1), 0) // SWA_BLOCK
    col = jnp.zeros((group * SWA_BLOCK, 1), F32)
    for g in range(group):
        col = jnp.where(head == g, sink_ref[kv * group + g], col)
    return col


def _swa_specs(n_kv, group, q_first, k_first, v_first):
    B = SWA_BLOCK
    prev = lambda n: jnp.maximum(n - 1, 0)
    return [
        pl.BlockSpec((B, group * HEAD_DIM), lambda kv, n: (n, q_first + kv)),
        pl.BlockSpec((B, HEAD_DIM), lambda kv, n: (prev(n), k_first + kv)),
        pl.BlockSpec((B, HEAD_DIM), lambda kv, n: (n, k_first + kv)),
        pl.BlockSpec((B, HEAD_DIM), lambda kv, n: (prev(n), v_first + kv)),
        pl.BlockSpec((B, HEAD_DIM), lambda kv, n: (n, v_first + kv)),
    ]


def _swa_fwd(rq, proj, v_first, sinks, n_q, n_kv):
    S = rq.shape[0]
    B = SWA_BLOCK
    group = n_q // n_kv
    scale = HEAD_DIM ** -0.5

    def body(q_ref, kp_ref, kc_ref, vp_ref, vc_ref, sink_ref, o_ref, lse_ref):
        kv, n = pl.program_id(0), pl.program_id(1)
        _, _, s = _swa_tile(q_ref, kp_ref, kc_ref, n, group, scale)
        sink = _swa_sink_col(sink_ref, kv, group)
        m = jnp.maximum(jnp.max(s, axis=-1, keepdims=True), sink)
        p = jnp.exp(s - m)
        denom = jnp.sum(p, axis=-1, keepdims=True) + jnp.exp(sink - m)
        vcat = jnp.concatenate([vp_ref[...], vc_ref[...]], axis=0)
        o = jnp.dot((p / denom).astype(BF16), vcat, preferred_element_type=F32)
        lse = m + jnp.log(denom)
        for g in range(group):
            o_ref[:, g * HEAD_DIM:(g + 1) * HEAD_DIM] = o[g * B:(g + 1) * B, :]
            lse_ref[0, :, g * LANES:(g + 1) * LANES] = jnp.broadcast_to(lse[g * B:(g + 1) * B, :], (B, LANES))

    specs = _swa_specs(n_kv, group, 0, n_q, v_first)
    q_blk = pl.BlockSpec((B, group * HEAD_DIM), lambda kv, n: (n, kv))
    return pl.pallas_call(
        body, name="swa_fwd", grid=(n_kv, S // B),
        in_specs=specs + [pl.BlockSpec(memory_space=pltpu.SMEM)],
        out_specs=[q_blk, pl.BlockSpec((1, B, group * LANES), lambda kv, n: (kv, n, 0))],
        out_shape=[jax.ShapeDtypeStruct((S, n_q * HEAD_DIM), F32), jax.ShapeDtypeStruct((n_kv, S, group * LANES), F32)],
        compiler_params=pltpu.CompilerParams(dimension_semantics=("parallel", "arbitrary")),
    )(rq, rq, rq, proj, proj, sinks)


def _swa_bwd(rq, proj, v_first, sinks, o, do, do_first, lse_b, n_q, n_kv):
    S = rq.shape[0]
    B = SWA_BLOCK
    group = n_q // n_kv
    scale = HEAD_DIM ** -0.5

    def body(q_ref, kp_ref, kc_ref, vp_ref, vc_ref, o_ref, do_ref, lse_ref, sink_ref,
             dq_ref, dk_ref, dv_ref, dsink_ref):
        kv, n = pl.program_id(0), pl.program_id(1)

        @pl.when(n == 0)
        def _():
            dk_ref[...] = jnp.zeros_like(dk_ref)
            dv_ref[...] = jnp.zeros_like(dv_ref)
            dsink_ref[...] = jnp.zeros_like(dsink_ref)

        qs, kcat, s = _swa_tile(q_ref, kp_ref, kc_ref, n, group, scale)
        sink = _swa_sink_col(sink_ref, kv, group)
        stack = lambda ref, w: jnp.concatenate([ref[:, g * w:(g + 1) * w] for g in range(group)], axis=0)
        lse = jnp.concatenate([lse_ref[0, :, g * LANES:g * LANES + 1] for g in range(group)], axis=0)
        do32 = stack(do_ref, HEAD_DIM)
        delta = jnp.sum(do32 * stack(o_ref, HEAD_DIM), axis=-1, keepdims=True)
        dov = do32.astype(BF16)
        p = jnp.exp(s - lse)
        vcat = jnp.concatenate([vp_ref[...], vc_ref[...]], axis=0)
        dp = lax.dot_general(dov, vcat, _NT, preferred_element_type=F32)
        ds = p * (dp - delta)
        dsb = ds.astype(BF16)
        dq = jnp.dot(dsb, kcat, preferred_element_type=F32) * scale
        for g in range(group):
            dq_ref[:, g * HEAD_DIM:(g + 1) * HEAD_DIM] = dq[g * B:(g + 1) * B, :].astype(BF16)
        dkcat = lax.dot_general(dsb, qs, _TN, preferred_element_type=F32) * scale
        dvcat = lax.dot_general(p.astype(BF16), dov, _TN, preferred_element_type=F32)
        prev0 = pl.multiple_of(jnp.maximum(n - 1, 0) * B, B)
        cur0 = pl.multiple_of(n * B, B)
        dk_ref[0, pl.ds(prev0, B), :] += dkcat[:B, :]
        dk_ref[0, pl.ds(cur0, B), :] += dkcat[B:, :]
        dv_ref[0, pl.ds(prev0, B), :] += dvcat[:B, :]
        dv_ref[0, pl.ds(cur0, B), :] += dvcat[B:, :]
        dsk = -jnp.exp(sink - lse) * delta
        lane = lax.broadcasted_iota(jnp.int32, (1, LANES), 1)
        row = jnp.zeros((1, LANES), F32)
        for g in range(group):
            row = row + jnp.where(lane == g, jnp.sum(dsk[g * B:(g + 1) * B, :]), 0.0)
        dsink_ref[0, 0:1, :] += row

    specs = _swa_specs(n_kv, group, 0, n_q, v_first)
    q_blk = pl.BlockSpec((B, group * HEAD_DIM), lambda kv, n: (n, kv))
    acc = pl.BlockSpec((1, S, HEAD_DIM), lambda kv, n: (kv, 0, 0))
    return pl.pallas_call(
        body, name="swa_bwd", grid=(n_kv, S // B),
        in_specs=specs + [q_blk, pl.BlockSpec((B, group * HEAD_DIM), lambda kv, n: (n, do_first + kv)),
                          pl.BlockSpec((1, B, group * LANES), lambda kv, n: (kv, n, 0)),
                          pl.BlockSpec(memory_space=pltpu.SMEM)],
        out_specs=[q_blk, acc, acc, pl.BlockSpec((1, 8, LANES), lambda kv, n: (kv, 0, 0))],
        out_shape=[jax.ShapeDtypeStruct((S, n_q * HEAD_DIM), BF16), jax.ShapeDtypeStruct((n_kv, S, HEAD_DIM), F32),
                   jax.ShapeDtypeStruct((n_kv, S, HEAD_DIM), F32), jax.ShapeDtypeStruct((n_kv, 8, LANES), F32)],
        compiler_params=pltpu.CompilerParams(dimension_semantics=("parallel", "arbitrary")),
    )(rq, rq, rq, proj, proj, o, do, lse_b, sinks)


def _adamw(w, g, m, v):
    m = ADAM_B1 * m + (1.0 - ADAM_B1) * g
    v = ADAM_B2 * v + (1.0 - ADAM_B2) * (g * g)
    m_hat = m / (1.0 - ADAM_B1 ** ADAM_STEP)
    v_hat = v / (1.0 - ADAM_B2 ** ADAM_STEP)
    delta = -ADAM_LR * (m_hat / (jnp.sqrt(v_hat) + ADAM_EPS) + ADAM_WD * w)
    return delta, m, v


def _mod_fwd(cond_in, w_mod, b_shard):
    R, D = cond_in.shape
    cols = w_mod.shape[1]
    tn = _fit(512, cols)

    def body(c_ref, w_ref, b_ref, o_ref):
        cv = c_ref[...]
        cond = (cv / (1.0 + jnp.exp(-cv))).astype(BF16)
        o_ref[...] = jnp.dot(cond, w_ref[...].astype(BF16), preferred_element_type=F32) + b_ref[...]

    return pl.pallas_call(
        body, name="mod_fwd", grid=(cols // tn,),
        in_specs=[pl.BlockSpec((R, D), lambda j: (0, 0)), pl.BlockSpec((D, tn), lambda j: (0, j)),
                  pl.BlockSpec((1, tn), lambda j: (0, j))],
        out_specs=pl.BlockSpec((R, tn), lambda j: (0, j)),
        out_shape=jax.ShapeDtypeStruct((R, cols), F32),
        compiler_params=pltpu.CompilerParams(dimension_semantics=("parallel",), vmem_limit_bytes=_vmem(3 * D * tn * 4)),
    )(cond_in, w_mod, b_shard)


def _mod_update(c_t, dmod, w, m, v):
    D, nb = c_t.shape
    cols = w.shape[1]
    tr = _fit(128, D)

    def body(c_ref, d_ref, w_ref, m_ref, v_ref, g_ref, dl_ref, nm_ref, nv_ref):
        cv = c_ref[...]
        cond = cv / (1.0 + jnp.exp(-cv))
        g = jnp.zeros((tr, cols), F32)
        for b in range(nb):
            g = g + cond[:, b:b + 1] * d_ref[b:b + 1, :]
        g_ref[...] = g
        dl_ref[...], nm_ref[...], nv_ref[...] = _adamw(w_ref[...], g, m_ref[...], v_ref[...])

    blk = pl.BlockSpec((tr, cols), lambda r: (r, 0))
    out = jax.ShapeDtypeStruct((D, cols), F32)
    return pl.pallas_call(
        body, name="mod_update", grid=(D // tr,),
        in_specs=[pl.BlockSpec((tr, nb), lambda r: (r, 0)), pl.BlockSpec((nb, cols), lambda r: (0, 0)), blk, blk, blk],
        out_specs=[blk] * 4, out_shape=[out] * 4,
        compiler_params=pltpu.CompilerParams(dimension_semantics=("parallel",), vmem_limit_bytes=_vmem(18 * tr * cols * 4)),
    )(c_t, dmod, w, m, v)


def _small_update(stacked, w, m, v):
    R, C = w.shape

    def body(s_ref, w_ref, m_ref, v_ref, g_ref, dl_ref, nm_ref, nv_ref):
        g = s_ref[0:R, :]
        for d in range(1, N_DEV):
            g = g + s_ref[d * R:(d + 1) * R, :]
        g_ref[...] = g
        dl_ref[...], nm_ref[...], nv_ref[...] = _adamw(w_ref[...], g, m_ref[...], v_ref[...])

    return pl.pallas_call(body, name="small_update", out_shape=[jax.ShapeDtypeStruct((R, C), F32)] * 4)(stacked, w, m, v)


def _place():
    return lax.axis_index("x"), lax.axis_index("y"), lax.axis_index("c")


def _allgather8(name, block):
    m_per, n = block.shape

    def body(x_ref, out_ref, token_ref, send_sems, recv_sems, local_sem):
        token_ref[...] = jnp.zeros_like(token_ref)
        x, y, c = _place()
        me, sibling = (x, y, c), (x, y, 1 - c)
        chips = [(1 - x, y), (x, 1 - y), (1 - x, 1 - y)]

        def rows(px, py, pc):
            return out_ref.at[pl.ds((4 * px + 2 * py + pc) * m_per, m_per), :]

        def copy(k, blk, to, src=None):
            return pltpu.make_async_remote_copy(
                src_ref=rows(*blk) if src is None else src, dst_ref=rows(*blk),
                send_sem=send_sems.at[k], recv_sem=recv_sems.at[k], device_id=to, device_id_type=MESH)

        mine = pltpu.make_async_copy(x_ref, rows(*me), local_sem)
        mine.start()
        first = [copy(0, me, sibling, src=x_ref)]
        first += [copy(1 + j, me, (*chip, c), src=x_ref) for j, chip in enumerate(chips)]
        for cp in first:
            cp.start()
        passed = [copy(4 + j, (*chip, c), sibling) for j, chip in enumerate(chips)]
        for j, chip in enumerate(chips):
            copy(1 + j, (*chip, c), me).wait_recv()
            passed[j].start()
        copy(0, sibling, me).wait_recv()
        for j, chip in enumerate(chips):
            copy(4 + j, (*chip, 1 - c), me).wait_recv()
        for cp in first + passed:
            cp.wait_send()
        mine.wait()

    vmem = pl.BlockSpec(memory_space=pltpu.VMEM)
    return pl.pallas_call(
        body, name=name,
        out_shape=[jax.ShapeDtypeStruct((N_DEV * m_per, n), block.dtype), jax.ShapeDtypeStruct((8, LANES), F32)],
        in_specs=[vmem], out_specs=[vmem, vmem],
        scratch_shapes=[pltpu.SemaphoreType.DMA((7,)), pltpu.SemaphoreType.DMA((7,)), pltpu.SemaphoreType.DMA],
    )(block)


_ANY = pl.BlockSpec(memory_space=pl.ANY)


def _half(ref, c, rows):
    return ref.at[pl.ds(c * (rows // 2), rows // 2), :]


_HBM = pl.BlockSpec(memory_space=pltpu.HBM)
_SEM = pl.BlockSpec(memory_space=pltpu.SEMAPHORE)
_EFFECT = pltpu.SideEffectType.DATAFLOW_SIDE_EFFECTING


def _ici_start(name, srcs, land_shapes, plan, per_source=3, after=None):
    ns, nl = len(srcs), len(land_shapes)
    n_copies = per_source * ns
    n_in = ns + nl + (after is not None)

    def body(*refs):
        src_refs, land_refs = refs[:ns], refs[ns:ns + nl]
        send_sems, recv_sems = refs[n_in], refs[n_in + 1]
        token = refs[-1]
        for n, (src, dst, peer, _) in enumerate(plan(src_refs, land_refs)):
            pltpu.make_async_remote_copy(src_ref=src, dst_ref=dst, send_sem=send_sems.at[n], recv_sem=recv_sems.at[n],
                                         device_id=peer, device_id_type=MESH).start()
        token[...] = jnp.zeros_like(token)

    lands = [lax.empty(s.shape, s.dtype) for s in land_shapes]
    out = pl.pallas_call(
        body, name=name,
        out_shape=(pltpu.SemaphoreType.DMA((n_copies,)), pltpu.SemaphoreType.DMA((n_copies,)),
                   *[pltpu.HBM(a.shape, a.dtype) for a in list(srcs) + lands], jax.ShapeDtypeStruct((8, LANES), F32)),
        in_specs=[_HBM] * (ns + nl) + [_ANY] * (after is not None),
        out_specs=(_SEM, _SEM, *[_HBM] * (ns + nl), pl.BlockSpec(memory_space=pltpu.VMEM)),
        input_output_aliases={n: 2 + n for n in range(ns + nl)},
        compiler_params=pltpu.CompilerParams(has_side_effects=_EFFECT),
    )(*[pltpu.with_memory_space_constraint(a, pltpu.HBM) for a in list(srcs) + lands],
      *([] if after is None else [after]))
    return out[0], out[1], list(out[2:2 + ns]), list(out[2 + ns:2 + ns + nl]), out[-1]


def _ici_wait(name, send_sems, recv_sems, srcs, lands, plan, after):
    ns, nl = len(srcs), len(lands)
    after = list(after) if isinstance(after, (list, tuple)) else [after]

    def body(*refs):
        src_refs, land_refs = refs[:ns], refs[ns:ns + nl]
        send_sems, recv_sems = refs[ns + nl], refs[ns + nl + 1]
        for n, (src, _, peer, mine) in enumerate(plan(src_refs, land_refs)):
            cp = pltpu.make_async_remote_copy(src_ref=src, dst_ref=mine, send_sem=send_sems.at[n],
                                              recv_sem=recv_sems.at[n], device_id=peer, device_id_type=MESH)
            cp.wait_send()
            cp.wait_recv()

    out = pl.pallas_call(
        body, name=name, out_shape=[pltpu.HBM(a.shape, a.dtype) for a in list(srcs) + list(lands)],
        in_specs=[_HBM] * (ns + nl) + [_SEM, _SEM] + [_ANY] * len(after), out_specs=[_HBM] * (ns + nl),
        input_output_aliases={n: n for n in range(ns + nl)},
        compiler_params=pltpu.CompilerParams(has_side_effects=_EFFECT),
    )(*srcs, *lands, send_sems, recv_sems, *after)
    return list(out[:ns]), list(out[ns:])


def _own_slab(name, chip, w, after):
    R, C = w.shape
    tr, tc = _tiles(R, C)
    tied = [] if after is None else [after]

    def body(chip_ref, w_ref, *rest):
        stack_ref, token_ref = rest[-2:]
        stack_ref[0] = w_ref[...].astype(BF16)
        token_ref[...] = jnp.zeros_like(token_ref)

    small = pl.BlockSpec((8, LANES), lambda r, q, chip_ref: (0, 0))
    grid_spec = pltpu.PrefetchScalarGridSpec(
        num_scalar_prefetch=1, grid=(R // tr, C // tc),
        in_specs=[pl.BlockSpec((tr, tc), lambda r, q, chip_ref: (r, q))] + [small] * len(tied),
        out_specs=[pl.BlockSpec((1, tr, tc), lambda r, q, chip_ref: (chip_ref[0], r, q)), small])
    return pl.pallas_call(
        body, name=name, grid_spec=grid_spec,
        out_shape=[jax.ShapeDtypeStruct((N_CHIPS, R, C), BF16), jax.ShapeDtypeStruct((8, LANES), F32)],
        compiler_params=pltpu.CompilerParams(dimension_semantics=("arbitrary", "arbitrary")),
    )(chip, w, *tied)


def _gather_plan(src_refs, land_refs):
    x, y, c = _place()
    copies = []
    for stack in src_refs:
        R = stack.shape[1]
        own = _half(stack.at[2 * x + y], c, R)
        for cx, cy in [(1 - x, y), (x, 1 - y), (1 - x, 1 - y)]:
            copies.append((own, own, (cx, cy, c), _half(stack.at[2 * cx + cy], c, R)))
    return copies


def _pass_plan(src_refs, land_refs):
    x, y, c = _place()
    copies = []
    for land in src_refs:
        R = land.shape[1]
        for cx, cy in [(1 - x, y), (x, 1 - y), (1 - x, 1 - y)]:
            slot = land.at[2 * cx + cy]
            copies.append((_half(slot, c, R), _half(slot, c, R), (x, y, 1 - c), _half(slot, 1 - c, R)))
    return copies


def _share_plan(src_refs, land_refs):
    x, y, c = _place()
    return [(h, land, (x, y, 1 - c), land) for h, land in zip(src_refs, land_refs)]


def _pass_to_sibling(name, lands):
    nw = len(lands)

    def body(*refs):
        ins, outs = refs[:nw], refs[nw:2 * nw]
        send_sems, recv_sems = refs[2 * nw:]
        x, y, c = _place()
        chips = [(1 - x, y), (x, 1 - y), (1 - x, 1 - y)]
        copies = []
        for k in range(nw):
            R = ins[k].shape[1]
            for j, (cx, cy) in enumerate(chips):
                cp = pltpu.make_async_remote_copy(
                    src_ref=_half(ins[k].at[2 * cx + cy], c, R), dst_ref=_half(outs[k].at[2 * cx + cy], c, R),
                    send_sem=send_sems.at[3 * k + j], recv_sem=recv_sems.at[3 * k + j],
                    device_id=(x, y, 1 - c), device_id_type=MESH)
                cp.start()
                copies.append(cp)
        for k in range(nw):
            R = ins[k].shape[1]
            for j, (cx, cy) in enumerate(chips):
                pltpu.make_async_remote_copy(
                    src_ref=_half(ins[k].at[2 * cx + cy], c, R), dst_ref=_half(outs[k].at[2 * cx + cy], 1 - c, R),
                    send_sem=send_sems.at[3 * k + j], recv_sem=recv_sems.at[3 * k + j],
                    device_id=(x, y, 1 - c), device_id_type=MESH).wait_recv()
        for cp in copies:
            cp.wait_send()

    return pl.pallas_call(
        body, name=name, out_shape=[jax.ShapeDtypeStruct(a.shape, a.dtype) for a in lands],
        in_specs=[_ANY] * nw, out_specs=[_ANY] * nw, input_output_aliases={k: k for k in range(nw)},
        scratch_shapes=[pltpu.SemaphoreType.DMA((3 * nw,)), pltpu.SemaphoreType.DMA((3 * nw,))],
    )(*lands)


def _tie(vec, token):
    return vec + token[0:1, 0:1]


def _lay_columns(name, arrays, pieces):
    S, dtype = arrays[0].shape[0], arrays[0].dtype
    out_w = sum(hi - lo for _, lo, hi in pieces)
    tr = _fit(ROW_TILE, S)

    def body(*refs):
        o_ref = refs[-1]
        rows = [r[...] for r in refs[:-1]]
        o_ref[...] = jnp.concatenate(
            [jnp.zeros((tr, hi - lo), dtype) if k is None else rows[k][:, lo:hi] for k, lo, hi in pieces], axis=1)

    need = 2 * tr * (out_w + sum(a.shape[1] for a in arrays)) * dtype.itemsize
    return pl.pallas_call(
        body, name=name, grid=(S // tr,), in_specs=[pl.BlockSpec((tr, a.shape[1]), lambda i: (i, 0)) for a in arrays],
        out_specs=pl.BlockSpec((tr, out_w), lambda i: (i, 0)), out_shape=jax.ShapeDtypeStruct((S, out_w), dtype),
        compiler_params=pltpu.CompilerParams(dimension_semantics=("parallel",), vmem_limit_bytes=_vmem(2 * need)),
    )(*arrays)


ROW_ALIGN = 16
TILE_ELEMS = 512 * 1024


def _tiles(rows, cols):
    fits = [t for t in range(ROW_ALIGN, min(rows, 256) + 1, ROW_ALIGN) if rows % t == 0]
    tr = fits[-1] if fits and fits[-1] >= 64 else rows
    tc = cols
    while tr * tc > TILE_ELEMS and tc % (2 * LANES) == 0:
        tc //= 2
    return tr, tc


def _scatter_plan(src_refs, land_refs):
    x, y, c = _place()
    copies = []
    for p, land in zip(src_refs, land_refs):
        for j, (cx, cy) in enumerate([(1 - x, y), (x, 1 - y), (1 - x, 1 - y)]):
            copies.append((p.at[2 * cx + cy], land.at[j], (cx, cy, c), land.at[j]))
    return copies


def _chip_add(name, chip, sums, recv):
    _, H, C = sums.shape
    tr, tc = _tiles(H, C)

    def body(chip_ref, p_ref, r_ref, o_ref):
        total = p_ref[0].astype(F32)
        for j in range(3):
            total = total + r_ref[j].astype(F32)
        o_ref[...] = total

    grid_spec = pltpu.PrefetchScalarGridSpec(
        num_scalar_prefetch=1, grid=(H // tr, C // tc),
        in_specs=[pl.BlockSpec((1, tr, tc), lambda r, q, chip_ref: (chip_ref[0], r, q)),
                  pl.BlockSpec((3, tr, tc), lambda r, q, chip_ref: (0, r, q))],
        out_specs=pl.BlockSpec((tr, tc), lambda r, q, chip_ref: (r, q)))
    return pl.pallas_call(
        body, name=name, grid_spec=grid_spec, out_shape=jax.ShapeDtypeStruct((H, C), F32),
        compiler_params=pltpu.CompilerParams(dimension_semantics=("parallel", "parallel")),
    )(chip, sums, recv)


def _pair_share(name, halves):
    nw = len(halves)

    def body(*refs):
        hs, outs = refs[:nw], refs[nw:2 * nw]
        send_sems, recv_sems = refs[2 * nw:]
        x, y, c = _place()
        copies = []
        for k in range(nw):
            cp = pltpu.make_async_remote_copy(
                src_ref=hs[k], dst_ref=outs[k], send_sem=send_sems.at[k], recv_sem=recv_sems.at[k],
                device_id=(x, y, 1 - c), device_id_type=MESH)
            cp.start()
            copies.append(cp)
        for cp in copies:
            cp.wait()

    return pl.pallas_call(
        body, name=name,
        out_shape=[jax.ShapeDtypeStruct(h.shape, h.dtype) for h in halves],
        in_specs=[_ANY] * nw, out_specs=[_ANY] * nw,
        scratch_shapes=[pltpu.SemaphoreType.DMA((nw,)), pltpu.SemaphoreType.DMA((nw,))],
    )(*halves)


def _adam_halves(name, core, w, g_own, g_other, m, v, out_rows=None):
    R, C = w.shape
    H = R // 2
    tr, tc = _tiles(H, C)
    nr, nc = H // tr, C // tc

    def body(core_ref, w_ref, go_ref, gr_ref, m_ref, v_ref, g_ref, dl_ref, nm_ref, nv_ref):
        own = (pl.program_id(0) // nr) == core_ref[0]
        g = jnp.where(own, go_ref[...], gr_ref[...])
        g_ref[...] = g
        dl_ref[...], nm_ref[...], nv_ref[...] = _adamw(w_ref[...], g, m_ref[...], v_ref[...])

    blk = pl.BlockSpec((tr, tc), lambda r, q, core_ref: (r, q))

    def half_spec(is_own):
        def index(r, q, core_ref):
            mine = ((r // nr) == core_ref[0]) == is_own
            done = is_own == (core_ref[0] == 0)
            return (jnp.where(mine, r % nr, jnp.where(done, nr - 1, 0)), jnp.where(mine, q, jnp.where(done, nc - 1, 0)))
        return pl.BlockSpec((tr, tc), index)
    out_rows = R if out_rows is None else out_rows
    assert R - tr < out_rows <= R, (R, tr, out_rows)
    out = jax.ShapeDtypeStruct((out_rows, C), F32)
    grid_spec = pltpu.PrefetchScalarGridSpec(
        num_scalar_prefetch=1, grid=(R // tr, nc), in_specs=[blk, half_spec(True), half_spec(False), blk, blk],
        out_specs=[blk] * 4)
    return pl.pallas_call(
        body, name=name, grid_spec=grid_spec, out_shape=[out] * 4,
        compiler_params=pltpu.CompilerParams(dimension_semantics=("parallel", "parallel"),
                                             vmem_limit_bytes=_vmem(20 * tr * tc * 4)),
    )(core, w, g_own, g_other, m, v)


def kernel(x, c, w_mod, b_mod, g_pre_mix, g_post_mix, w_in, b_forget, swa_sinks, w_out, g_pre_mlp, g_post_mlp, w_up, w_down, loss_target, m_w_mod, m_b_mod, m_g_pre_mix, m_g_post_mix, m_w_in, m_b_forget, m_swa_sinks, m_w_out, m_g_pre_mlp, m_g_post_mlp, m_w_up, m_w_down, v_w_mod, v_b_mod, v_g_pre_mix, v_g_post_mix, v_w_in, v_b_forget, v_swa_sinks, v_w_out, v_g_pre_mlp, v_g_post_mlp, v_w_up, v_w_down):
    S, D = x.shape[1], x.shape[2]
    n_heads = D // HEAD_DIM
    n_fox = n_heads // 2
    n_swa = n_heads - n_fox
    n_kv = max(1, n_swa // 4)
    fox_w, swa_w, kv_w = n_fox * HEAD_DIM, n_swa * HEAD_DIM, n_kv * HEAD_DIM
    main_w = 3 * fox_w + swa_w + 2 * kv_w
    in_w = main_w + n_fox
    mod_cols = w_mod.shape[2]

    ax, ay, ac = _place()
    chip = 2 * ax + ay
    dev = 2 * chip + ac
    chip_arr = jnp.reshape(chip, (1,)).astype(jnp.int32)
    core_arr = jnp.reshape(ac, (1,)).astype(jnp.int32)

    x2, tgt = x[0], loss_target[0]

    in_rows = in_w // N_CHIPS
    in_rows_pad = -(-in_rows // (2 * LANES)) * (2 * LANES)
    slab_w = N_CHIPS * in_rows_pad

    def rows_of(a):
        return jnp.pad(a[0].T, ((0, in_rows_pad - in_rows), (0, 0)))

    w_in_stack, token = _own_slab("own_slab_w_in", chip_arr, rows_of(w_in), None)

    c_all, _ = _allgather8("gather_c", _tie(c, token).reshape(8, D // 8))
    c_all = c_all.reshape(N_DEV, D)
    b_shard = lax.dynamic_slice_in_dim(b_mod, chip * mod_cols, mod_cols, axis=1)
    mod_shard = _mod_fwd(jnp.pad(c_all, ((0, 16 - N_DEV), (0, 0))), w_mod[0], b_shard)[:N_DEV]
    mod_all, token = _allgather8("gather_mod", mod_shard)
    mod_all = mod_all.reshape(N_CHIPS, 2, N_DEV, mod_cols)[:, 0]
    mod = lax.dynamic_index_in_dim(mod_all, dev, axis=1, keepdims=False).reshape(N_MOD, 1, D)
    sh_a, sc_a, gt_a, sh_m, sc_m, gt_m = [mod[n] for n in range(N_MOD)]

    def slab_cols(lo, hi):
        spans = []
        while lo < hi:
            s, r = divmod(lo, in_rows)
            n = min(hi - lo, in_rows - r)
            spans.append((s * in_rows_pad + r, s * in_rows_pad + r + n))
            lo += n
        return spans

    gate_lo = 3 * fox_w
    main_spans = slab_cols(0, gate_lo) + slab_cols(gate_lo + n_fox, in_w)
    (gate_first, gate_last), = slab_cols(gate_lo, gate_lo + n_fox)

    names = ["w_in", "w_out", "w_up", "w_down"]
    flights = {}
    for n, w in zip(names, [None, w_out[0], w_up[0], w_down[0]]):
        stack = w_in_stack if n == "w_in" else _own_slab("own_slab_" + n, chip_arr, w, token)[0]
        flights[n] = _ici_start("gather_start_" + n, [stack], [], _gather_plan, after=token)
        token = flights[n][4]
    sc_a = _tie(sc_a, token)

    def arrived(n, after):
        send, recv, stacks, _, _ = flights[n]
        stacks, _ = _ici_wait("gather_wait_" + n, send, recv, stacks, [], _gather_plan, after)
        return _ici_start("gather_pass_start_" + n, stacks, [], _pass_plan)

    def gathered(n, after, in_flight=None):
        if in_flight is None:
            send, recv, stacks, _, _ = flights[n]
            stacks, _ = _ici_wait("gather_wait_" + n, send, recv, stacks, [], _gather_plan, after)
            return _pass_to_sibling("gather_pass_" + n, stacks)[0]
        send, recv, stacks, _, _ = in_flight
        return _ici_wait("gather_pass_wait_" + n, send, recv, stacks, [], _pass_plan, after)[0][0]

    d_ff = N_CHIPS * w_up.shape[2]

    h = _pre_norm(x2, g_pre_mix, sc_a, sh_a)
    in_state = [rows_of(w_in)] + [rows_of(_tie(a, token)) for a in (m_w_in, v_w_in)]
    cos, sin_signed = _rope_tables(S)

    def pack(bm, gpm, gqm, gpl, gql, bf, sk):
        last = jnp.concatenate([bf, sk, jnp.zeros((1, D - n_fox - n_swa), F32)], axis=1)
        return jnp.concatenate([bm.reshape(N_MOD, D), gpm, gqm, gpl, gql, last, jnp.zeros((5, D), F32)], axis=0)

    small_state = [pack(b_mod, g_pre_mix, g_post_mix, g_pre_mlp, g_post_mlp, b_forget, swa_sinks),
                   pack(m_b_mod, m_g_pre_mix, m_g_post_mix, m_g_pre_mlp, m_g_post_mlp, m_b_forget, m_swa_sinks),
                   pack(v_b_mod, v_g_pre_mix, v_g_post_mix, v_g_pre_mlp, v_g_post_mlp, v_b_forget, v_swa_sinks)]
    ready = h[:8, :LANES].astype(F32) + cos[:8]
    w_slab_t = gathered("w_in", [ready] + in_state[1:] + small_state).reshape(slab_w, D)
    tm_p, tn_p = _fit(MM_TM, S), _fit(MM_TN if slab_w % MM_TN == 0 else MM_TN // 2, slab_w)
    win0 = gate_first // LANES * LANES
    win_j, win_off = divmod(win0, tn_p)
    assert win_off + 2 * LANES <= tn_p and gate_last - win0 <= 2 * LANES

    def proj_epilogue(acc, ex, outs):
        outs[0][...] = acc.astype(BF16)

        @pl.when(pl.program_id(1) == win_j)
        def _():
            outs[1][...] = acc[:, win_off:win_off + 2 * LANES]

    proj_slab, gate_win = _matmul(
        "in_proj", h, w_slab_t, "nt",
        [((S, slab_w), BF16, (tm_p, tn_p), lambda i, j: (i, j)), ((S, 2 * LANES), F32, (tm_p, 2 * LANES), lambda i, j: (i, 0))],
        proj_epilogue, tn=tn_p, revisits=True)
    proj = jnp.concatenate([proj_slab[:, lo:hi] for lo, hi in main_spans], axis=1)
    out_flight = arrived("w_out", proj_slab)
    fg = _tie(jnp.pad(gate_win[:, gate_first - win0:gate_last - win0], ((0, 0), (0, LANES - n_fox))), out_flight[4])
    b_pad = jnp.pad(b_forget, ((0, 0), (0, LANES - n_fox)))
    cum_row = _fox_gate_fwd(fg, b_pad)[:n_fox].reshape(n_fox, 1, S)
    fox_o, fox_lse = _fox_fwd(proj, cum_row, n_fox)

    rq = _rope("rope_fwd", proj, 3 * n_fox, n_swa + n_kv, cos, sin_signed)
    v_first = 3 * n_fox + n_swa + n_kv
    sinks = swa_sinks[0]
    swa_o, swa_lse = _swa_fwd(rq, proj, v_first, sinks, n_swa, n_kv)

    mixcat = jnp.concatenate([fox_o, swa_o], axis=1).astype(BF16)
    up_flight = arrived("w_up", mixcat)
    w_out_f = gathered("w_out", mixcat, out_flight).reshape(D, D)
    mix = _mm_plain("out_proj", mixcat, w_out_f, "nn", BF16, after=up_flight[4])
    x1, h2 = _post_mix(x2, mix, g_post_mix, gt_a, g_pre_mlp, sc_m, sh_m)
    w_up_f = gathered("w_up", h2, up_flight)

    tm_u, tn_u = _fit(MM_TM, S), _fit(MM_TN, d_ff)

    def up_epilogue(acc, ex, outs):
        outs[0][...] = acc.astype(BF16)
        r = jnp.maximum(acc, 0.0)
        outs[1][...] = (r * r).astype(BF16)

    ublk = ((S, d_ff), BF16, (tm_u, tn_u), lambda i, j: (i, j))
    u, a = _matmul("mlp_up", h2, w_up_f, "nn", [ublk, ublk], up_epilogue)
    w_down_f = gathered("w_down", a).reshape(d_ff, D)
    y = _mm_plain("mlp_down", a, w_down_f, "nn", BF16)

    dy, dout, loss_part, acc_mlp_post = _loss_and_post_mlp_bwd(x1, y, tgt, g_post_mlp, gt_m)

    def du_epilogue(acc, ex, outs):
        outs[0][...] = (acc * (2.0 * jnp.maximum(ex[0][...].astype(F32), 0.0))).astype(BF16)

    du = _matmul("mlp_down_bwd", dy, w_down_f, "nt", [ublk], du_epilogue,
                 extras=[(u, (tm_u, tn_u), lambda i, j: (i, j))])[0]
    def pair_send(tag, part):
        return _ici_start("grad_pair_start_" + tag, [part], [jax.ShapeDtypeStruct(part.shape, BF16)], _share_plan,
                          per_source=1)

    def pair_recv(tag, flight, after):
        send, recv, srcs, lands, _ = flight
        return _ici_wait("grad_pair_wait_" + tag, send, recv, srcs, lands, _share_plan, after)[1][0]

    def scatter_start(tag, sums, after=None):
        return _ici_start("grad_scatter_start_" + tag, sums,
                          [jax.ShapeDtypeStruct((3,) + p.shape[1:], BF16) for p in sums], _scatter_plan, after=after)

    def scatter_finish(tag, flight, after):
        send, recv, srcs, lands, _ = flight
        sums, received = _ici_wait("grad_scatter_wait_" + tag, send, recv, srcs, lands, _scatter_plan, after)
        return [_chip_add("chip_add_%s_%d" % (tag, k), chip_arr, p, r) for k, (p, r) in enumerate(zip(sums, received))]

    tm_g = _fit(MM_TM, D // 2)
    pair_down = pair_send("down", _grad_half("grad_w_down_a", core_arr, a, dy, N_CHIPS, 1, tm_g, True))
    pair_up = pair_send("up", _grad_half("grad_w_up_a", core_arr, h2, du, 1, N_CHIPS, tm_g, True, after=pair_down[4]))
    sum_down = _grad_half("grad_w_down_b", core_arr, a, dy, N_CHIPS, 1, tm_g, False,
                          recv=pair_recv("down", pair_down, pair_up[4]))
    sum_up = _grad_half("grad_w_up_b", core_arr, h2, du, 1, N_CHIPS, tm_g, False, recv=pair_recv("up", pair_up, sum_down))
    flight_mlp = scatter_start("mlp", [sum_up, sum_down])
    dh2 = _mm_plain("mlp_up_bwd", du, w_up_f, "nt", BF16, after=flight_mlp[4])
    dx1, dmix, acc_mid = _pre_mlp_and_post_mix_bwd(dh2, x1, dout, mix, _tie(g_pre_mlp, flight_mlp[4]), sc_m,
                                                   g_post_mix, gt_a)

    dmixcat = _mm_plain("out_proj_bwd", dmix, w_out_f, "nt", F32)

    fdq, fdk, fdv, dcum_row, dcum_q = _fox_bwd(proj, fox_o, dmixcat, fox_lse, cum_row, n_fox)
    dcum_k = jnp.pad(dcum_row.reshape(n_fox, S), ((0, LANES - n_fox), (0, 0)))
    dfg, db_forget = _fox_gate_bwd(dcum_k, dcum_q, fg, b_pad)

    group_w = (n_swa // n_kv) * HEAD_DIM
    sdq, sdk, sdv, dsink = _swa_bwd(rq, proj, v_first, sinks, swa_o, dmixcat, fox_w // group_w, swa_lse, n_swa, n_kv)
    drq = jnp.concatenate([sdq, jnp.transpose(sdk, (1, 0, 2)).reshape(S, kv_w).astype(BF16)], axis=1)
    d_sq_sk = _rope("rope_bwd", drq, 0, n_swa + n_kv, cos, -sin_signed)
    dsv = jnp.transpose(sdv, (1, 0, 2)).reshape(S, kv_w).astype(BF16)
    parts = [fdq, fdk, fdv, dfg, d_sq_sk, dsv]
    widths = [p.shape[1] for p in parts[:3]] + [n_fox] + [p.shape[1] for p in parts[4:]]
    starts = [sum(widths[:k]) for k in range(len(parts) + 1)]
    assert starts[3] == gate_lo and starts[-1] == in_w
    pieces = []
    for s in range(N_CHIPS):
        lo, hi = s * in_rows, (s + 1) * in_rows
        for k in range(len(parts)):
            first, last = max(lo, starts[k]), min(hi, starts[k + 1])
            if first < last:
                pieces.append((k, first - starts[k], last - starts[k]))
        pieces.append((None, 0, in_rows_pad - in_rows))
    dproj_slab = _lay_columns("dproj_slab_order", parts, pieces)

    tm_in, tm_out = in_rows_pad // 2, D // (2 * N_CHIPS)
    pair_in = pair_send("in", _grad_half("grad_w_in_a", core_arr, dproj_slab, h, N_CHIPS, 1, tm_in, True))
    pair_out = pair_send("out", _grad_half("grad_w_out_a", core_arr, mixcat, dmix, N_CHIPS, 1, tm_out, True,
                                           after=pair_in[4]))
    sum_in = _grad_half("grad_w_in_b", core_arr, dproj_slab, h, N_CHIPS, 1, tm_in, False,
                        recv=pair_recv("in", pair_in, pair_out[4]))
    sum_out = _grad_half("grad_w_out_b", core_arr, mixcat, dmix, N_CHIPS, 1, tm_out, False,
                         recv=pair_recv("out", pair_out, sum_in[0, :8, :LANES]))
    dh = _mm_plain("in_proj_bwd", dproj_slab, w_slab_t, "nn", BF16, tk=slab_w // 2,
                   after=sum_out[0, :8, :LANES].astype(F32))
    grad_x, acc_pre = _pre_mix_bwd(dh, x2, dx1, g_pre_mix, sc_a)

    zero_row = jnp.zeros((1, D), F32)
    tail = jnp.concatenate([db_forget[0:1, :n_fox], dsink[:, 0, :n_swa // n_kv].reshape(1, n_swa),
                            loss_part[0:1, 0:1], jnp.zeros((1, D - n_fox - n_swa - 1), F32)], axis=1)
    partial = jnp.concatenate([
        acc_pre[0:1], acc_pre[1:2], acc_mid[3:4], acc_mid[0:1], acc_mid[1:2], acc_mlp_post[0:1],
        acc_pre[2:3], acc_mid[4:5], acc_mid[2:3], acc_mlp_post[1:2], tail] + [zero_row] * 5, axis=0)
    gathered_small, token = _allgather8("gather_small_grads", partial)

    flight_mix = scatter_start("mix", [sum_in, sum_out], after=token)
    halves_mlp = scatter_finish("mlp", flight_mlp, flight_mix[4])
    share_up, share_down = [
        _ici_start("grad_share_start_" + n, [hv], [jax.ShapeDtypeStruct(hv.shape, F32)], _share_plan, per_source=1)
        for n, hv in zip(["up", "down"], halves_mlp)]

    def shared(tag, flight, after):
        send, recv, own, lands, _ = flight
        own, other = _ici_wait("grad_share_wait_" + tag, send, recv, own, lands, _share_plan, after)
        return own[0], other[0]

    def unpack(p):
        return {"b_mod": p[0:N_MOD].reshape(1, N_MOD * D), "g_pre_mix": p[6:7], "g_post_mix": p[7:8],
                "g_pre_mlp": p[8:9], "g_post_mlp": p[9:10], "b_forget": p[10:11, :n_fox],
                "swa_sinks": p[10:11, n_fox:n_fox + n_swa]}

    small_out = _small_update(gathered_small, _tie(small_state[0], share_down[4] + share_up[4]), small_state[1],
                              small_state[2])
    g_small, d_small, m_small, v_small = [unpack(p) for p in small_out]
    loss = small_out[0][N_MOD + 4, n_fox + n_swa]

    dmod_all = gathered_small.reshape(N_DEV, 16, D)[:, :N_MOD].reshape(N_DEV, N_MOD * D)
    dmod_shard = _tie(lax.dynamic_slice_in_dim(dmod_all, chip * mod_cols, mod_cols, axis=1), share_down[4])
    g_w_mod, d_w_mod, nm_w_mod, nv_w_mod = _mod_update(c_all.T, dmod_shard, w_mod[0], m_w_mod[0], v_w_mod[0])

    grads = dict(g_small, w_mod=g_w_mod[None])
    deltas = dict(d_small, w_mod=d_w_mod[None])
    new_m = dict(m_small, w_mod=nm_w_mod[None])
    new_v = dict(v_small, w_mod=nv_w_mod[None])
    weights = {"w_in": (w_in, m_w_in, v_w_in), "w_out": (w_out, m_w_out, v_w_out), "w_up": (w_up, m_w_up, v_w_up),
               "w_down": (w_down, m_w_down, v_w_down)}

    updated = {}

    def big_update(n, own, other):
        transposed = n == "w_in"
        w, m, v = in_state if transposed else [a[0] for a in weights[n]]
        outs = _adam_halves("adam_" + n, core_arr, w, own, other, m, v, out_rows=in_rows if transposed else None)
        updated[n] = outs[1][:8, :LANES]
        if transposed:
            outs = [o.T for o in outs]
        grads[n], deltas[n], new_m[n], new_v[n] = [o[None] for o in outs]

    big_update("w_down", *shared("down", share_down, d_w_mod[:8, :LANES] + small_out[1][:8, :LANES]))
    halves_mix = scatter_finish("mix", flight_mix, updated["w_down"] + d_w_mod[:8, :LANES])
    share_out = _ici_start("grad_share_start_out", [halves_mix[1]], [jax.ShapeDtypeStruct(halves_mix[1].shape, F32)],
                           _share_plan, per_source=1)
    share_in = _ici_start("grad_share_start_in", [halves_mix[0]], [jax.ShapeDtypeStruct(halves_mix[0].shape, F32)],
                          _share_plan, per_source=1, after=share_out[4])
    big_update("w_out", *shared("out", share_out, share_in[4]))
    big_update("w_in", *shared("in", share_in, updated["w_out"]))
    big_update("w_up", *shared("up", share_up, updated["w_in"]))

    order = ["w_mod", "b_mod", "g_pre_mix", "g_post_mix", "w_in", "b_forget", "swa_sinks", "w_out", "g_pre_mlp",
             "g_post_mlp", "w_up", "w_down"]
    return (loss, grad_x[None], *[grads[n] for n in order], *[deltas[n] for n in order],
            *[new_m[n] for n in order], *[new_v[n] for n in order])
```
